```python
import jax, jax.numpy as jnp
from jax import lax
import numpy as np

D_MODEL = 1024
BATCH = 8
SEQ = 4096
DEPTH = 1

HG_HEADS = 4
HG_KEY_DIM = 128
HG_VAL_DIM = 128
HG_KEY_WIDTH = HG_HEADS * HG_KEY_DIM
HG_WIDTH = HG_HEADS * HG_VAL_DIM
HG_CHUNK = 64
MLA_HEADS = 4
MLA_NOPE_DIM = 128
MLA_ROPE_DIM = 64
MLA_V_DIM = 128
MLA_Q_RANK = 256
MLA_KV_RANK = 256
MLA_WIDTH = MLA_HEADS * MLA_V_DIM
ROPE_THETA = 10000.0
Q_BLOCK = 128
MIX_WIDTH = HG_WIDTH + MLA_WIDTH
D_FF = 4 * D_MODEL
N_MOD = 6
ADA_INIT = 0.5
RMS_EPS = 1e-6
LN_EPS = 1e-5
DN_ALPHA = (2.0 * DEPTH) ** 0.25
DN_BETA = (8.0 * DEPTH) ** -0.25
IN_SIZES = (HG_KEY_WIDTH, HG_KEY_WIDTH, HG_WIDTH, HG_WIDTH, MLA_Q_RANK, MLA_KV_RANK, MLA_ROPE_DIM)
IN_COLS = sum(IN_SIZES)
IN_SPLITS = tuple(int(s) for s in np.cumsum(IN_SIZES)[:-1])

kernel_name = 'hybrid_hgrn2_mla_deepnorm_adaln'


def rms_norm(x, w):
    xf = x.astype(jnp.float32)
    y = xf * lax.rsqrt(jnp.mean(xf * xf, axis=-1, keepdims=True) + RMS_EPS) * w.astype(jnp.float32)
    return y.astype(x.dtype)


def layer_norm(x, g, b):
    xf = x.astype(jnp.float32)
    mu = jnp.mean(xf, axis=-1, keepdims=True)
    xc = xf - mu
    var = jnp.mean(xc * xc, axis=-1, keepdims=True)
    y = xc * lax.rsqrt(var + LN_EPS) * g.astype(jnp.float32) + b.astype(jnp.float32)
    return y.astype(x.dtype)


def rope_tables(positions):
    inv_freq = 1.0 / (ROPE_THETA ** (jnp.arange(0, MLA_ROPE_DIM, 2, dtype=jnp.float32) / MLA_ROPE_DIM))
    ang = positions.astype(jnp.float32)[..., None] * inv_freq
    return jnp.cos(ang), jnp.sin(ang)


def apply_rope(x, cos, sin):
    xf = x.astype(jnp.float32)
    x1, x2 = jnp.split(xf, 2, axis=-1)
    return jnp.concatenate([x1 * cos - x2 * sin, x2 * cos + x1 * sin], axis=-1).astype(x.dtype)


def hgrn2_chunkwise(q, f_logit, v, lb):
    B, T, H, dk = q.shape
    dv = v.shape[-1]
    C = HG_CHUNK
    n = T // C
    f32 = jnp.float32
    forget = lb.astype(f32) + (1.0 - lb.astype(f32)) * jax.nn.sigmoid(f_logit.astype(f32))
    k = 1.0 - forget
    log_f = jnp.log(forget)

    def chunked(a):
        return a.astype(f32).reshape(B, n, C, H, a.shape[-1]).transpose(0, 3, 1, 2, 4)

    q, k, v, log_f = chunked(q), chunked(k), chunked(v), chunked(log_f)
    b = jnp.cumsum(log_f, axis=3)
    b_ref = b[:, :, :, C // 2 - 1:C // 2, :]
    b_last = b[:, :, :, C - 1:C, :]
    causal = jnp.tril(jnp.ones((C, C), dtype=bool))
    a = jnp.einsum('bhncd,bhnsd->bhncs', q * jnp.exp(b - b_ref), k * jnp.exp(b_ref - b))
    a = jnp.where(causal, a, 0.0)
    o_intra = jnp.einsum('bhncs,bhnse->bhnce', a, v)
    kv = jnp.einsum('bhnsd,bhnse->nbhde', k * jnp.exp(b_last - b), v)
    decay = jnp.exp(b_last[:, :, :, 0, :]).transpose(2, 0, 1, 3)

    def step(state, inp):
        d, kv_n = inp
        return d[..., None] * state + kv_n, state

    s0 = jnp.zeros((B, H, dk, dv), f32)
    _, s_prev = lax.scan(step, s0, (decay, kv))
    o_inter = jnp.einsum('bhncd,nbhde->bhnce', q * jnp.exp(b), s_prev)
    return (o_intra + o_inter).transpose(0, 2, 3, 1, 4).reshape(B, T, H, dv)


def mla_causal_attention(q_nope, q_pe, k_nope, k_pe, v):
    B, T, H, _ = q_nope.shape
    nb = T // Q_BLOCK
    scale = (MLA_NOPE_DIM + MLA_ROPE_DIM) ** -0.5
    qn = q_nope.reshape(B, nb, Q_BLOCK, H, MLA_NOPE_DIM).transpose(1, 0, 2, 3, 4)
    qp = q_pe.reshape(B, nb, Q_BLOCK, H, MLA_ROPE_DIM).transpose(1, 0, 2, 3, 4)
    starts = jnp.arange(nb, dtype=jnp.int32) * Q_BLOCK
    key_idx = jnp.arange(T, dtype=jnp.int32)
    neg = jnp.finfo(jnp.float32).min

    def block(args):
        qn_b, qp_b, start = args
        s = (jnp.einsum('bqhd,bkhd->bhqk', qn_b, k_nope).astype(jnp.float32)
             + jnp.einsum('bqhd,bkd->bhqk', qp_b, k_pe).astype(jnp.float32)) * scale
        q_idx = start + jnp.arange(Q_BLOCK, dtype=jnp.int32)
        s = jnp.where(key_idx[None, :] <= q_idx[:, None], s, neg)
        p = jax.nn.softmax(s, axis=-1)
        return jnp.einsum('bhqk,bkhd->bqhd', p.astype(v.dtype), v)

    out = lax.map(block, (qn, qp, starts))
    return out.transpose(1, 0, 2, 3, 4).reshape(B, T, H, MLA_V_DIM)


def hybrid_mixer(u, cos, sin, lb, w_in, hg_norm_w, q_norm_w, w_q_up, kv_norm_w, w_kv_up, w_out):
    B, T, _ = u.shape
    z = u @ w_in
    hq, hf, hi, hg, c_q, c_kv, k_pe = jnp.split(z, IN_SPLITS, axis=-1)
    o_hg = hgrn2_chunkwise(hq.reshape(B, T, HG_HEADS, HG_KEY_DIM),
                           hf.reshape(B, T, HG_HEADS, HG_KEY_DIM),
                           hi.reshape(B, T, HG_HEADS, HG_VAL_DIM),
                           lb.reshape(HG_HEADS, HG_KEY_DIM))
    o_hg = rms_norm(o_hg, hg_norm_w.reshape(HG_HEADS, HG_VAL_DIM))
    o_hg = (o_hg * jax.nn.silu(hg.astype(jnp.float32)).reshape(B, T, HG_HEADS, HG_VAL_DIM))
    o_hg = o_hg.astype(u.dtype).reshape(B, T, HG_WIDTH)
    q = (rms_norm(c_q, q_norm_w) @ w_q_up).reshape(B, T, MLA_HEADS, MLA_NOPE_DIM + MLA_ROPE_DIM)
    q_nope, q_pe = q[..., :MLA_NOPE_DIM], q[..., MLA_NOPE_DIM:]
    kvu = (rms_norm(c_kv, kv_norm_w) @ w_kv_up).reshape(B, T, MLA_HEADS, MLA_NOPE_DIM + MLA_V_DIM)
    k_nope, v = kvu[..., :MLA_NOPE_DIM], kvu[..., MLA_NOPE_DIM:]
    q_pe = apply_rope(q_pe, cos[:, :, None, :], sin[:, :, None, :])
    k_pe = apply_rope(k_pe, cos, sin)
    o_mla = mla_causal_attention(q_nope, q_pe, k_nope, k_pe, v).reshape(B, T, MLA_WIDTH)
    return jnp.concatenate([o_hg, o_mla.astype(u.dtype)], axis=-1) @ w_out


def _fwd_setup_inputs(seed: int = 0) -> dict:
    key = jax.random.key(seed)
    ks = jax.random.split(key, 20)

    def nrm(k, shape, scale):
        return jax.random.normal(k, shape, jnp.float32) * scale

    x = nrm(ks[0], (BATCH, SEQ, D_MODEL), 1.0)
    c = nrm(ks[1], (BATCH, D_MODEL), 1.0)
    offsets = jax.random.randint(ks[2], (BATCH, 1), 0, 1024, dtype=jnp.int32)
    positions = (jnp.arange(SEQ, dtype=jnp.int32)[None, :] + offsets).astype(jnp.int32)
    return {
        'x': x,
        'c': c,
        'positions': positions,
        'w_ada': nrm(ks[3], (DEPTH, D_MODEL, N_MOD * D_MODEL), ADA_INIT * D_MODEL ** -0.5),
        'b_ada': nrm(ks[4], (DEPTH, N_MOD * D_MODEL), 0.02),
        'w_in': nrm(ks[5], (DEPTH, D_MODEL, IN_COLS), D_MODEL ** -0.5),
        'hg_lower_bounds': nrm(ks[6], (DEPTH + 1, HG_KEY_WIDTH), 0.1),
        'hg_norm_w': 1.0 + nrm(ks[7], (DEPTH, HG_WIDTH), 0.02),
        'mla_q_norm_w': 1.0 + nrm(ks[8], (DEPTH, MLA_Q_RANK), 0.02),
        'w_q_up': nrm(ks[9], (DEPTH, MLA_Q_RANK, MLA_HEADS * (MLA_NOPE_DIM + MLA_ROPE_DIM)), MLA_Q_RANK ** -0.5),
        'mla_kv_norm_w': 1.0 + nrm(ks[10], (DEPTH, MLA_KV_RANK), 0.02),
        'w_kv_up': nrm(ks[11], (DEPTH, MLA_KV_RANK, MLA_HEADS * (MLA_NOPE_DIM + MLA_V_DIM)), MLA_KV_RANK ** -0.5),
        'w_out': nrm(ks[12], (DEPTH, MIX_WIDTH, D_MODEL), DN_BETA * MIX_WIDTH ** -0.5),
        'ln1_g': 1.0 + nrm(ks[13], (DEPTH, D_MODEL), 0.02),
        'ln1_b': nrm(ks[14], (DEPTH, D_MODEL), 0.02),
        'w_mlp_in': nrm(ks[15], (DEPTH, D_MODEL, D_FF), D_MODEL ** -0.5),
        'w_mlp_out': nrm(ks[16], (DEPTH, D_FF, D_MODEL), DN_BETA * D_FF ** -0.5),
        'ln2_g': 1.0 + nrm(ks[17], (DEPTH, D_MODEL), 0.02),
        'ln2_b': nrm(ks[18], (DEPTH, D_MODEL), 0.02),
    }


def _fwd_reference(x, c, positions, w_ada, b_ada, w_in, hg_lower_bounds, hg_norm_w, mla_q_norm_w, w_q_up,
              mla_kv_norm_w, w_kv_up, w_out, ln1_g, ln1_b, w_mlp_in, w_mlp_out, ln2_g, ln2_b):
    cos, sin = rope_tables(positions)
    lbs = jnp.cumsum(jax.nn.softmax(hg_lower_bounds.astype(jnp.float32), axis=0), axis=0)[:DEPTH]
    cond = jax.nn.silu(c)
    for l in range(DEPTH):
        mod = (cond @ w_ada[l] + b_ada[l])[:, None, :]
        sh_a, sc_a, g_a, sh_m, sc_m, g_m = jnp.split(mod, N_MOD, axis=-1)
        u = x * (1.0 + sc_a) + sh_a
        mix = hybrid_mixer(u, cos, sin, lbs[l], w_in[l], hg_norm_w[l], mla_q_norm_w[l], w_q_up[l],
                           mla_kv_norm_w[l], w_kv_up[l], w_out[l])
        x = layer_norm(DN_ALPHA * x + (1.0 + g_a) * mix, ln1_g[l], ln1_b[l])
        u = x * (1.0 + sc_m) + sh_m
        h = jnp.square(jax.nn.relu(u @ w_mlp_in[l])) @ w_mlp_out[l]
        x = layer_norm(DN_ALPHA * x + (1.0 + g_m) * h, ln2_g[l], ln2_b[l])
    return x


import jax as _jax
import jax.numpy as _jnp

TWIN_FORMAT = 'train_step'
FWD_PARAMS = ['x', 'c', 'positions', 'w_ada', 'b_ada', 'w_in', 'hg_lower_bounds', 'hg_norm_w', 'mla_q_norm_w', 'w_q_up', 'mla_kv_norm_w', 'w_kv_up', 'w_out', 'ln1_g', 'ln1_b', 'w_mlp_in', 'w_mlp_out', 'ln2_g', 'ln2_b']
TWIN_WEIGHTS = ['w_ada', 'b_ada', 'w_in', 'hg_lower_bounds', 'hg_norm_w', 'mla_q_norm_w', 'w_q_up', 'mla_kv_norm_w', 'w_kv_up', 'w_out', 'ln1_g', 'ln1_b', 'w_mlp_in', 'w_mlp_out', 'ln2_g', 'ln2_b']
TWIN_DIFF_INPUT = 'x'
TWIN_INPUTS = ['x', 'c', 'positions', 'w_ada', 'b_ada', 'w_in', 'hg_lower_bounds', 'hg_norm_w', 'mla_q_norm_w', 'w_q_up', 'mla_kv_norm_w', 'w_kv_up', 'w_out', 'ln1_g', 'ln1_b', 'w_mlp_in', 'w_mlp_out', 'ln2_g', 'ln2_b', 'loss_target', 'm_w_ada', 'm_b_ada', 'm_w_in', 'm_hg_lower_bounds', 'm_hg_norm_w', 'm_mla_q_norm_w', 'm_w_q_up', 'm_mla_kv_norm_w', 'm_w_kv_up', 'm_w_out', 'm_ln1_g', 'm_ln1_b', 'm_w_mlp_in', 'm_w_mlp_out', 'm_ln2_g', 'm_ln2_b', 'v_w_ada', 'v_b_ada', 'v_w_in', 'v_hg_lower_bounds', 'v_hg_norm_w', 'v_mla_q_norm_w', 'v_w_q_up', 'v_mla_kv_norm_w', 'v_w_kv_up', 'v_w_out', 'v_ln1_g', 'v_ln1_b', 'v_w_mlp_in', 'v_w_mlp_out', 'v_ln2_g', 'v_ln2_b']
TWIN_OUTPUTS = ['loss', 'grad_x', 'grad_w_ada', 'grad_b_ada', 'grad_w_in', 'grad_hg_lower_bounds', 'grad_hg_norm_w', 'grad_mla_q_norm_w', 'grad_w_q_up', 'grad_mla_kv_norm_w', 'grad_w_kv_up', 'grad_w_out', 'grad_ln1_g', 'grad_ln1_b', 'grad_w_mlp_in', 'grad_w_mlp_out', 'grad_ln2_g', 'grad_ln2_b', 'delta_w_ada', 'delta_b_ada', 'delta_w_in', 'delta_hg_lower_bounds', 'delta_hg_norm_w', 'delta_mla_q_norm_w', 'delta_w_q_up', 'delta_mla_kv_norm_w', 'delta_w_kv_up', 'delta_w_out', 'delta_ln1_g', 'delta_ln1_b', 'delta_w_mlp_in', 'delta_w_mlp_out', 'delta_ln2_g', 'delta_ln2_b', 'new_m_w_ada', 'new_m_b_ada', 'new_m_w_in', 'new_m_hg_lower_bounds', 'new_m_hg_norm_w', 'new_m_mla_q_norm_w', 'new_m_w_q_up', 'new_m_mla_kv_norm_w', 'new_m_w_kv_up', 'new_m_w_out', 'new_m_ln1_g', 'new_m_ln1_b', 'new_m_w_mlp_in', 'new_m_w_mlp_out', 'new_m_ln2_g', 'new_m_ln2_b', 'new_v_w_ada', 'new_v_b_ada', 'new_v_w_in', 'new_v_hg_lower_bounds', 'new_v_hg_norm_w', 'new_v_mla_q_norm_w', 'new_v_w_q_up', 'new_v_mla_kv_norm_w', 'new_v_w_kv_up', 'new_v_w_out', 'new_v_ln1_g', 'new_v_ln1_b', 'new_v_w_mlp_in', 'new_v_w_mlp_out', 'new_v_ln2_g', 'new_v_ln2_b']
TWIN_LEAF_KINDS = {'loss': 'loss', 'grad_x': 'grad_x', 'grad_w_ada': 'grad_w', 'grad_b_ada': 'grad_w', 'grad_w_in': 'grad_w', 'grad_hg_lower_bounds': 'grad_w', 'grad_hg_norm_w': 'grad_w', 'grad_mla_q_norm_w': 'grad_w', 'grad_w_q_up': 'grad_w', 'grad_mla_kv_norm_w': 'grad_w', 'grad_w_kv_up': 'grad_w', 'grad_w_out': 'grad_w', 'grad_ln1_g': 'grad_w', 'grad_ln1_b': 'grad_w', 'grad_w_mlp_in': 'grad_w', 'grad_w_mlp_out': 'grad_w', 'grad_ln2_g': 'grad_w', 'grad_ln2_b': 'grad_w', 'delta_w_ada': 'delta_w', 'delta_b_ada': 'delta_w', 'delta_w_in': 'delta_w', 'delta_hg_lower_bounds': 'delta_w', 'delta_hg_norm_w': 'delta_w', 'delta_mla_q_norm_w': 'delta_w', 'delta_w_q_up': 'delta_w', 'delta_mla_kv_norm_w': 'delta_w', 'delta_w_kv_up': 'delta_w', 'delta_w_out': 'delta_w', 'delta_ln1_g': 'delta_w', 'delta_ln1_b': 'delta_w', 'delta_w_mlp_in': 'delta_w', 'delta_w_mlp_out': 'delta_w', 'delta_ln2_g': 'delta_w', 'delta_ln2_b': 'delta_w', 'new_m_w_ada': 'new_m', 'new_m_b_ada': 'new_m', 'new_m_w_in': 'new_m', 'new_m_hg_lower_bounds': 'new_m', 'new_m_hg_norm_w': 'new_m', 'new_m_mla_q_norm_w': 'new_m', 'new_m_w_q_up': 'new_m', 'new_m_mla_kv_norm_w': 'new_m', 'new_m_w_kv_up': 'new_m', 'new_m_w_out': 'new_m', 'new_m_ln1_g': 'new_m', 'new_m_ln1_b': 'new_m', 'new_m_w_mlp_in': 'new_m', 'new_m_w_mlp_out': 'new_m', 'new_m_ln2_g': 'new_m', 'new_m_ln2_b': 'new_m', 'new_v_w_ada': 'new_v', 'new_v_b_ada': 'new_v', 'new_v_w_in': 'new_v', 'new_v_hg_lower_bounds': 'new_v', 'new_v_hg_norm_w': 'new_v', 'new_v_mla_q_norm_w': 'new_v', 'new_v_w_q_up': 'new_v', 'new_v_mla_kv_norm_w': 'new_v', 'new_v_w_kv_up': 'new_v', 'new_v_w_out': 'new_v', 'new_v_ln1_g': 'new_v', 'new_v_ln1_b': 'new_v', 'new_v_w_mlp_in': 'new_v', 'new_v_w_mlp_out': 'new_v', 'new_v_ln2_g': 'new_v', 'new_v_ln2_b': 'new_v'}


def _forward(args):
    return _fwd_reference(*[args[k] for k in FWD_PARAMS])


def _output_shape():
    out = _jax.eval_shape(lambda: _forward(_fwd_setup_inputs(0)))
    return out.shape, out.dtype

N_MICROBATCH = 1
ADAM_LR = 0.001
ADAM_B1 = 0.9
ADAM_B2 = 0.999
ADAM_EPS = 1e-08
ADAM_WD = 0.01
ADAM_STEP = 10
PER_EXAMPLE_BATCH_AXIS = {'x': 0, 'c': 0, 'positions': 0, 'loss_target': 0}
SHARED_INPUTS = []
_WEIGHT_DTYPES = {'w_ada': _jnp.float32, 'b_ada': _jnp.float32, 'w_in': _jnp.float32, 'hg_lower_bounds': _jnp.float32, 'hg_norm_w': _jnp.float32, 'mla_q_norm_w': _jnp.float32, 'w_q_up': _jnp.float32, 'mla_kv_norm_w': _jnp.float32, 'w_kv_up': _jnp.float32, 'w_out': _jnp.float32, 'ln1_g': _jnp.float32, 'ln1_b': _jnp.float32, 'w_mlp_in': _jnp.float32, 'w_mlp_out': _jnp.float32, 'ln2_g': _jnp.float32, 'ln2_b': _jnp.float32}
MOMENT_SCALE = {'w_ada': 8.111421e-02, 'b_ada': 1.753664e-01, 'w_in': 6.330002e-02, 'hg_lower_bounds': 3.818939e-02, 'hg_norm_w': 6.140487e-02, 'mla_q_norm_w': 2.174627e-02, 'w_q_up': 1.304237e-02, 'mla_kv_norm_w': 5.316013e-02, 'w_kv_up': 2.617969e-02, 'w_out': 8.414917e-02, 'ln1_g': 5.842729e-01, 'ln1_b': 4.449267e-01, 'w_mlp_in': 6.443688e-02, 'w_mlp_out': 2.551065e-01, 'ln2_g': 3.244095e+01, 'ln2_b': 8.132242e+00}


def _to_microbatches(a, axis):
    t = _jnp.moveaxis(a, axis, 0)
    t = t.reshape((N_MICROBATCH, t.shape[0] // N_MICROBATCH) + t.shape[1:])
    return _jnp.moveaxis(t, 1, axis + 1)


def setup_inputs(seed: int = 0) -> dict:
    inp = _fwd_setup_inputs(seed)
    key = _jax.random.fold_in(_jax.random.key(seed), 7919)
    shape, _ = _output_shape()
    out = dict(inp)
    out["loss_target"] = _jax.random.normal(_jax.random.fold_in(key, 0), shape, _jnp.float32)
    for i, name in enumerate(TWIN_WEIGHTS):
        w = inp[name].astype(_jnp.float32)
        if MOMENT_SCALE is None:
            s = _jnp.sqrt(_jnp.mean(_jnp.square(w)) + 1e-30)
        else:
            s = MOMENT_SCALE[name]
        km, kv = _jax.random.split(_jax.random.fold_in(key, i + 1))
        out[name] = w
        out["m_" + name] = s * _jax.random.normal(km, w.shape, _jnp.float32)
        out["v_" + name] = (s * s) * _jax.random.uniform(kv, w.shape, _jnp.float32, 0.5, 1.5)
    if N_MICROBATCH > 1:
        for name, axis in PER_EXAMPLE_BATCH_AXIS.items():
            out[name] = _to_microbatches(out[name], axis)
    return {'x': out['x'], 'c': out['c'], 'positions': out['positions'], 'w_ada': out['w_ada'], 'b_ada': out['b_ada'], 'w_in': out['w_in'], 'hg_lower_bounds': out['hg_lower_bounds'], 'hg_norm_w': out['hg_norm_w'], 'mla_q_norm_w': out['mla_q_norm_w'], 'w_q_up': out['w_q_up'], 'mla_kv_norm_w': out['mla_kv_norm_w'], 'w_kv_up': out['w_kv_up'], 'w_out': out['w_out'], 'ln1_g': out['ln1_g'], 'ln1_b': out['ln1_b'], 'w_mlp_in': out['w_mlp_in'], 'w_mlp_out': out['w_mlp_out'], 'ln2_g': out['ln2_g'], 'ln2_b': out['ln2_b'], 'loss_target': out['loss_target'], 'm_w_ada': out['m_w_ada'], 'm_b_ada': out['m_b_ada'], 'm_w_in': out['m_w_in'], 'm_hg_lower_bounds': out['m_hg_lower_bounds'], 'm_hg_norm_w': out['m_hg_norm_w'], 'm_mla_q_norm_w': out['m_mla_q_norm_w'], 'm_w_q_up': out['m_w_q_up'], 'm_mla_kv_norm_w': out['m_mla_kv_norm_w'], 'm_w_kv_up': out['m_w_kv_up'], 'm_w_out': out['m_w_out'], 'm_ln1_g': out['m_ln1_g'], 'm_ln1_b': out['m_ln1_b'], 'm_w_mlp_in': out['m_w_mlp_in'], 'm_w_mlp_out': out['m_w_mlp_out'], 'm_ln2_g': out['m_ln2_g'], 'm_ln2_b': out['m_ln2_b'], 'v_w_ada': out['v_w_ada'], 'v_b_ada': out['v_b_ada'], 'v_w_in': out['v_w_in'], 'v_hg_lower_bounds': out['v_hg_lower_bounds'], 'v_hg_norm_w': out['v_hg_norm_w'], 'v_mla_q_norm_w': out['v_mla_q_norm_w'], 'v_w_q_up': out['v_w_q_up'], 'v_mla_kv_norm_w': out['v_mla_kv_norm_w'], 'v_w_kv_up': out['v_w_kv_up'], 'v_w_out': out['v_w_out'], 'v_ln1_g': out['v_ln1_g'], 'v_ln1_b': out['v_ln1_b'], 'v_w_mlp_in': out['v_w_mlp_in'], 'v_w_mlp_out': out['v_w_mlp_out'], 'v_ln2_g': out['v_ln2_g'], 'v_ln2_b': out['v_ln2_b']}


def _loss(weights, diff, rest, loss_target):
    with _jax.named_scope("forward"):
        args = {**rest, TWIN_DIFF_INPUT: diff, **{k: w.astype(_WEIGHT_DTYPES[k]) for k, w in weights.items()}}
        y = _forward(args)
    with _jax.named_scope("loss_head"):
        err = _jnp.square(y.astype(_jnp.float32) - loss_target)
        return 0.5 * _jnp.sum(_jnp.mean(err, axis=-1)) if err.ndim else 0.5 * err


def _adamw(w, g, m, v):
    m = ADAM_B1 * m + (1.0 - ADAM_B1) * g
    v = ADAM_B2 * v + (1.0 - ADAM_B2) * _jnp.square(g)
    m_hat = m / (1.0 - ADAM_B1 ** ADAM_STEP)
    v_hat = v / (1.0 - ADAM_B2 ** ADAM_STEP)
    delta = -ADAM_LR * (m_hat / (_jnp.sqrt(v_hat) + ADAM_EPS) + ADAM_WD * w)
    return delta, m, v


def reference(x, c, positions, w_ada, b_ada, w_in, hg_lower_bounds, hg_norm_w, mla_q_norm_w, w_q_up, mla_kv_norm_w, w_kv_up, w_out, ln1_g, ln1_b, w_mlp_in, w_mlp_out, ln2_g, ln2_b, loss_target, m_w_ada, m_b_ada, m_w_in, m_hg_lower_bounds, m_hg_norm_w, m_mla_q_norm_w, m_w_q_up, m_mla_kv_norm_w, m_w_kv_up, m_w_out, m_ln1_g, m_ln1_b, m_w_mlp_in, m_w_mlp_out, m_ln2_g, m_ln2_b, v_w_ada, v_b_ada, v_w_in, v_hg_lower_bounds, v_hg_norm_w, v_mla_q_norm_w, v_w_q_up, v_mla_kv_norm_w, v_w_kv_up, v_w_out, v_ln1_g, v_ln1_b, v_w_mlp_in, v_w_mlp_out, v_ln2_g, v_ln2_b):
    given = dict(x=x, c=c, positions=positions, w_ada=w_ada, b_ada=b_ada, w_in=w_in, hg_lower_bounds=hg_lower_bounds, hg_norm_w=hg_norm_w, mla_q_norm_w=mla_q_norm_w, w_q_up=w_q_up, mla_kv_norm_w=mla_kv_norm_w, w_kv_up=w_kv_up, w_out=w_out, ln1_g=ln1_g, ln1_b=ln1_b, w_mlp_in=w_mlp_in, w_mlp_out=w_mlp_out, ln2_g=ln2_g, ln2_b=ln2_b, loss_target=loss_target, m_w_ada=m_w_ada, m_b_ada=m_b_ada, m_w_in=m_w_in, m_hg_lower_bounds=m_hg_lower_bounds, m_hg_norm_w=m_hg_norm_w, m_mla_q_norm_w=m_mla_q_norm_w, m_w_q_up=m_w_q_up, m_mla_kv_norm_w=m_mla_kv_norm_w, m_w_kv_up=m_w_kv_up, m_w_out=m_w_out, m_ln1_g=m_ln1_g, m_ln1_b=m_ln1_b, m_w_mlp_in=m_w_mlp_in, m_w_mlp_out=m_w_mlp_out, m_ln2_g=m_ln2_g, m_ln2_b=m_ln2_b, v_w_ada=v_w_ada, v_b_ada=v_b_ada, v_w_in=v_w_in, v_hg_lower_bounds=v_hg_lower_bounds, v_hg_norm_w=v_hg_norm_w, v_mla_q_norm_w=v_mla_q_norm_w, v_w_q_up=v_w_q_up, v_mla_kv_norm_w=v_mla_kv_norm_w, v_w_kv_up=v_w_kv_up, v_w_out=v_w_out, v_ln1_g=v_ln1_g, v_ln1_b=v_ln1_b, v_w_mlp_in=v_w_mlp_in, v_w_mlp_out=v_w_mlp_out, v_ln2_g=v_ln2_g, v_ln2_b=v_ln2_b)
    weights = {n: given[n] for n in TWIN_WEIGHTS}
    shared = {n: given[n] for n in SHARED_INPUTS}
    per_example = {n: given[n] for n in ['x', 'c', 'positions']}
    grad_fn = _jax.value_and_grad(_loss, argnums=(0, 1))

    def one_microbatch(ex, loss_target):
        ex = dict(ex)
        diff = ex.pop(TWIN_DIFF_INPUT)
        return grad_fn(weights, diff, {**shared, **ex}, loss_target)

    if N_MICROBATCH == 1:
        loss, (grad_w, grad_x) = one_microbatch(per_example, given["loss_target"])
    else:
        def body(carry, xs):
            loss_sum, grad_sum = carry
            l_k, (gw_k, gx_k) = one_microbatch(xs[0], xs[1])
            with _jax.named_scope("update"):
                return (loss_sum + l_k, _jax.tree.map(_jnp.add, grad_sum, gw_k)), gx_k

        init = (_jnp.zeros((), _jnp.float32), _jax.tree.map(_jnp.zeros_like, weights))
        (loss, grad_w), grad_x = _jax.lax.scan(body, init, (per_example, given["loss_target"]))
    with _jax.named_scope("update"):
        delta_w, new_m, new_v = {}, {}, {}
        for n in TWIN_WEIGHTS:
            delta_w[n], new_m[n], new_v[n] = _adamw(weights[n], grad_w[n], given["m_" + n], given["v_" + n])
    return (loss, grad_x, *[grad_w[n] for n in TWIN_WEIGHTS], *[delta_w[n] for n in TWIN_WEIGHTS],
            *[new_m[n] for n in TWIN_WEIGHTS], *[new_v[n] for n in TWIN_WEIGHTS])
```

```python
import functools

import jax
import jax.numpy as jnp
import numpy as np
from jax import lax
from jax.experimental import pallas as pl
from jax.experimental.pallas import tpu as pltpu

F32, BF16 = jnp.float32, jnp.bfloat16
N_DEV = 8
D_MODEL = 1024
HEADS = 4
HEAD_DIM = 128
ROPE_DIM = 64
QK_PAD = 256
CHUNK = 64
ROPE_THETA = 10000.0
RMS_EPS = 1e-6
LN_EPS = 1e-5
ALPHA = 2.0 ** 0.25
ATT_SCALE = (HEAD_DIM + ROPE_DIM) ** -0.5
ADAM_LR, ADAM_B1, ADAM_B2, ADAM_EPS, ADAM_WD, ADAM_STEP = 0.001, 0.9, 0.999, 1e-08, 0.01, 10
NEG_BIG = -1e30

ROW_TILE = 512
ROW_TILE_SMALL = 256
ATT_TILE = 256
HGRN_GROUP = 8
VMEM_LIMIT = 56 * 2 ** 20

NN = (((1,), (0,)), ((), ()))
NT = (((1,), (1,)), ((), ()))
TN = (((0,), (0,)), ((), ()))


def _dot(a, b, dims=NN):
    return lax.dot_general(a, b, dims, preferred_element_type=F32)


def _bdot(a, b, dims=NN):
    return lax.dot_general(a.astype(BF16), b.astype(BF16), dims, preferred_element_type=F32)


def _hdot(a, b, dims=NN):
    return lax.dot_general(a, b, dims, precision=lax.Precision.HIGHEST, preferred_element_type=F32)


def _params():
    return pltpu.CompilerParams(vmem_limit_bytes=VMEM_LIMIT)


def _sigmoid(x):
    return 1.0 / (1.0 + jnp.exp(-x))


def _rowsum(x):
    return jnp.sum(x, axis=0, keepdims=True)


def _lanemean(x):
    return jnp.mean(x, axis=-1, keepdims=True)


def _full(shape):
    nd = len(shape)
    return pl.BlockSpec(shape, lambda *_: (0,) * nd)


def _exchange(arrs, scatter, name):
    n = len(arrs)
    out_shape = [jax.ShapeDtypeStruct((N_DEV,) + (a.shape[1:] if scatter else a.shape), a.dtype) for a in arrs]

    def body(*refs):
        ins, outs = refs[:n], refs[n:2 * n]
        send_sems, recv_sems, loc_sems = refs[2 * n:]
        x, y, c = lax.axis_index("x"), lax.axis_index("y"), lax.axis_index("c")
        me = 4 * x + 2 * y + c
        started = []
        for k in range(n):
            loc = pltpu.make_async_copy(ins[k].at[me] if scatter else ins[k], outs[k].at[me], loc_sems.at[k])
            loc.start()
            started.append((loc, None))
            for p in range(1, N_DEV):
                px = (1 - x) if p & 4 else x
                py = (1 - y) if p & 2 else y
                pc = (1 - c) if p & 1 else c
                peer = 4 * px + 2 * py + pc
                send = pltpu.make_async_remote_copy(
                    src_ref=ins[k].at[peer] if scatter else ins[k], dst_ref=outs[k].at[me],
                    send_sem=send_sems.at[k, p - 1], recv_sem=recv_sems.at[k, p - 1],
                    device_id=(px, py, pc), device_id_type=pl.DeviceIdType.MESH)
                send.start()
                recv = pltpu.make_async_remote_copy(
                    src_ref=ins[k].at[peer] if scatter else ins[k], dst_ref=outs[k].at[peer],
                    send_sem=send_sems.at[k, p - 1], recv_sem=recv_sems.at[k, p - 1],
                    device_id=(px, py, pc), device_id_type=pl.DeviceIdType.MESH)
                started.append((send, recv))
        for first, recv in started:
            if recv is None:
                first.wait()
            else:
                recv.wait_recv()
                first.wait_send()

    hbm = pl.BlockSpec(memory_space=pltpu.HBM)
    return pl.pallas_call(
        body, name=name, out_shape=out_shape, in_specs=[hbm] * n, out_specs=[hbm] * n,
        scratch_shapes=[pltpu.SemaphoreType.DMA((n, N_DEV - 1)), pltpu.SemaphoreType.DMA((n, N_DEV - 1)),
                        pltpu.SemaphoreType.DMA((n,))],
    )(*arrs)


def _matmul(a, b, mode, name, out_dtype=F32, tm=512, tn=1024, tk=1024, a_fn=None, extras=(), out_slabs=None):
    if mode == "NN":
        (M, K), N = a.shape, b.shape[1]
    elif mode == "NT":
        (M, K), N = a.shape, b.shape[0]
    else:
        (K, M), N = a.shape, b.shape[1]
    if out_slabs:
        tn = N // out_slabs
    tm, tn, tk = min(tm, M), min(tn, N), min(tk, K)
    assert M % tm == 0 and N % tn == 0 and K % tk == 0, (name, M, N, K)
    nk = K // tk
    dims = {"NN": NN, "NT": NT, "TN": TN}[mode]
    ne = len(extras)

    def body(a_ref, b_ref, *rest):
        e_refs, o_ref, acc_ref = rest[:ne], rest[ne], rest[ne + 1]
        k = pl.program_id(2)

        @pl.when(k == 0)
        def _():
            acc_ref[...] = jnp.zeros_like(acc_ref)

        at = a_ref[...]
        if a_fn is not None:
            at = a_fn(at.astype(F32), *[e[...] for e in e_refs])
        acc_ref[...] += _bdot(at, b_ref[...], dims)

        @pl.when(k == nk - 1)
        def _():
            o_ref[...] = acc_ref[...].astype(out_dtype)

    if mode == "TN":
        a_spec = pl.BlockSpec((tk, tm), lambda i, j, k: (k, i))
        e_spec = pl.BlockSpec((1, tm), lambda i, j, k: (0, i))
    else:
        a_spec = pl.BlockSpec((tm, tk), lambda i, j, k: (i, k))
        e_spec = pl.BlockSpec((1, tk), lambda i, j, k: (0, k))
    if mode == "NT":
        b_spec = pl.BlockSpec((tn, tk), lambda i, j, k: (j, k))
    else:
        b_spec = pl.BlockSpec((tk, tn), lambda i, j, k: (k, j))
    if out_slabs:
        o_shape = jax.ShapeDtypeStruct((out_slabs, M, tn), out_dtype)
        o_spec = pl.BlockSpec((None, tm, tn), lambda i, j, k: (j, i, 0))
    else:
        o_shape = jax.ShapeDtypeStruct((M, N), out_dtype)
        o_spec = pl.BlockSpec((tm, tn), lambda i, j, k: (i, j))
    return pl.pallas_call(
        body, name=name, out_shape=o_shape, grid=(M // tm, N // tn, nk),
        in_specs=[a_spec, b_spec] + [e_spec] * ne, out_specs=o_spec,
        scratch_shapes=[pltpu.VMEM((tm, tn), F32)], compiler_params=_params(),
    )(a, b, *extras)


def _modulate(x, sc, sh):
    return x * (1.0 + sc) + sh


def _square(x):
    return x * x


def _mod_part(c_all, w_ada_s, b_s):
    def body(c_ref, w_ref, b_ref, mod_ref, cond_ref):
        cv = c_ref[...]
        cond = cv * _sigmoid(cv)
        cond_ref[...] = cond
        mod_ref[...] = _bdot(cond, w_ref[...]) + b_ref[...]

    return pl.pallas_call(
        body, name="mod_part",
        out_shape=[jax.ShapeDtypeStruct((N_DEV, w_ada_s.shape[1]), F32), jax.ShapeDtypeStruct(c_all.shape, F32)],
        compiler_params=_params(),
    )(c_all, w_ada_s, b_s)


def _rms_fwd(x, w):
    rs = lax.rsqrt(_lanemean(x * x) + RMS_EPS)
    return x * rs * w, rs


def _rms_bwd(x, rs, w, dy):
    xhat = x * rs
    dxh = dy * w
    return rs * (dxh - xhat * _lanemean(dxh * xhat)), dy * xhat


def _mla_pre(z, pos_col, invf, m_one, m_rot, wq_ext, wkv_ext, qnw, kvnw):
    T = z.shape[0]
    tm = min(ROW_TILE, T)

    def body(z_ref, pos_ref, invf_ref, mone_ref, mrot_ref, wq_ref, wkv_ref, qnw_ref, kvnw_ref,
             q_ref, k_ref, v_ref, c1_ref, s1_ref, cqn_ref, ckvn_ref):
        ang = pos_ref[...].astype(F32) * invf_ref[...]
        c1 = mone_ref[...] + mrot_ref[...] * jnp.cos(ang)
        s1 = mrot_ref[...] * jnp.sin(ang)
        c1_ref[...] = c1
        s1_ref[...] = s1
        cqn, _ = _rms_fwd(z_ref[:, 0:256], qnw_ref[...])
        ckvn, _ = _rms_fwd(z_ref[:, 256:512], kvnw_ref[...])
        cqn_ref[...] = cqn.astype(BF16)
        ckvn_ref[...] = ckvn.astype(BF16)
        qe = _bdot(cqn, wq_ref[...])
        kve = _bdot(ckvn, wkv_ref[...])
        k_rope = z_ref[:, 512:768] * c1 + z_ref[:, 768:1024] * s1
        for h in range(HEADS):
            q_ref[h] = (qe[:, 256 * h:256 * h + 256] * c1 + qe[:, 1024 + 256 * h:1280 + 256 * h] * s1).astype(BF16)
            k_ref[h] = (kve[:, 256 * h:256 * h + 256] + k_rope).astype(BF16)
            v_ref[h] = kve[:, 1024 + 128 * h:1152 + 128 * h].astype(BF16)

    row = lambda i: (i, 0)
    head = lambda i: (0, i, 0)
    return pl.pallas_call(
        body, name="mla_pre", grid=(T // tm,),
        in_specs=[pl.BlockSpec((tm, 1024), lambda i: (i, 2)), pl.BlockSpec((tm, 1), row),
                  _full((1, 256)), _full((1, 256)), _full((1, 256)), _full(wq_ext.shape), _full(wkv_ext.shape),
                  _full((1, 256)), _full((1, 256))],
        out_specs=[pl.BlockSpec((HEADS, tm, QK_PAD), head), pl.BlockSpec((HEADS, tm, QK_PAD), head),
                   pl.BlockSpec((HEADS, tm, HEAD_DIM), head), pl.BlockSpec((tm, 256), row), pl.BlockSpec((tm, 256), row),
                   pl.BlockSpec((tm, 256), row), pl.BlockSpec((tm, 256), row)],
        out_shape=[jax.ShapeDtypeStruct((HEADS, T, QK_PAD), BF16), jax.ShapeDtypeStruct((HEADS, T, QK_PAD), BF16),
                   jax.ShapeDtypeStruct((HEADS, T, HEAD_DIM), BF16), jax.ShapeDtypeStruct((T, 256), F32),
                   jax.ShapeDtypeStruct((T, 256), F32), jax.ShapeDtypeStruct((T, 256), BF16),
                   jax.ShapeDtypeStruct((T, 256), BF16)],
        compiler_params=_params(),
    )(z, pos_col, invf, m_one, m_rot, wq_ext, wkv_ext, qnw, kvnw)


def _mla_bwd(dq, dk, dv, z, c1, s1, wq_ext, wkv_ext, qnw, kvnw):
    T = z.shape[0]
    tm = min(ROW_TILE_SMALL, T)

    def body(dq_ref, dk_ref, dv_ref, z_ref, c1_ref, s1_ref, wq_ref, wkv_ref, qnw_ref, kvnw_ref,
             dz_ref, dqe_ref, dkve_ref, dqnw_ref, dkvnw_ref):
        @pl.when(pl.program_id(0) == 0)
        def _():
            dqnw_ref[...] = jnp.zeros_like(dqnw_ref)
            dkvnw_ref[...] = jnp.zeros_like(dkvnw_ref)

        c1, s1 = c1_ref[...], s1_ref[...]
        dkpe = jnp.zeros((tm, QK_PAD), F32)
        for h in range(HEADS):
            dqh, dkh = dq_ref[h], dk_ref[h]
            dqe_ref[:, 256 * h:256 * h + 256] = (dqh * c1).astype(BF16)
            dqe_ref[:, 1024 + 256 * h:1280 + 256 * h] = (dqh * s1).astype(BF16)
            dkve_ref[:, 256 * h:256 * h + 256] = dkh.astype(BF16)
            dkve_ref[:, 1024 + 128 * h:1152 + 128 * h] = dv_ref[h].astype(BF16)
            dkpe = dkpe + dkh
        dcqn = _dot(dqe_ref[...], wq_ref[...], NT)
        dckvn = _dot(dkve_ref[...], wkv_ref[...], NT)
        cq, ckv = z_ref[:, 0:256], z_ref[:, 256:512]
        _, rsq = _rms_fwd(cq, qnw_ref[...])
        _, rskv = _rms_fwd(ckv, kvnw_ref[...])
        dcq, wq_rows = _rms_bwd(cq, rsq, qnw_ref[...], dcqn)
        dckv, wkv_rows = _rms_bwd(ckv, rskv, kvnw_ref[...], dckvn)
        dqnw_ref[...] += _rowsum(wq_rows)
        dkvnw_ref[...] += _rowsum(wkv_rows)
        dz_ref[:, 0:256] = dcq
        dz_ref[:, 256:512] = dckv
        dz_ref[:, 512:768] = dkpe * c1
        dz_ref[:, 768:1024] = dkpe * s1

    row = lambda i: (i, 0)
    head = lambda i: (0, i, 0)
    return pl.pallas_call(
        body, name="mla_bwd", grid=(T // tm,),
        in_specs=[pl.BlockSpec((HEADS, tm, QK_PAD), head), pl.BlockSpec((HEADS, tm, QK_PAD), head),
                  pl.BlockSpec((HEADS, tm, HEAD_DIM), head), pl.BlockSpec((tm, 1024), lambda i: (i, 2)),
                  pl.BlockSpec((tm, 256), row), pl.BlockSpec((tm, 256), row), _full(wq_ext.shape), _full(wkv_ext.shape),
                  _full((1, 256)), _full((1, 256))],
        out_specs=[pl.BlockSpec((tm, 1024), row), pl.BlockSpec((tm, 2048), row), pl.BlockSpec((tm, 1536), row),
                   _full((1, 256)), _full((1, 256))],
        out_shape=[jax.ShapeDtypeStruct((T, 1024), F32), jax.ShapeDtypeStruct((T, 2048), BF16),
                   jax.ShapeDtypeStruct((T, 1536), BF16), jax.ShapeDtypeStruct((1, 256), F32),
                   jax.ShapeDtypeStruct((1, 256), F32)],
        compiler_params=_params(),
    )(dq, dk, dv, z, c1, s1, wq_ext, wkv_ext, qnw, kvnw)


def _lower_bound(lbraw_ref):
    a0, a1 = lbraw_ref[0:1, :], lbraw_ref[1:2, :]
    mx = jnp.maximum(a0, a1)
    e0, e1 = jnp.exp(a0 - mx), jnp.exp(a1 - mx)
    return e0 / (e0 + e1)


def _tri(lower):
    r = lax.broadcasted_iota(jnp.int32, (CHUNK, CHUNK), 0)
    c = lax.broadcasted_iota(jnp.int32, (CHUNK, CHUNK), 1)
    return (r >= c) if lower else (r <= c)


def _hgrn_gates(q, f, lb, tri_lo):
    sg = _sigmoid(f)
    forget = lb + (1.0 - lb) * sg
    k = 1.0 - forget
    b = _hdot(tri_lo.astype(F32), jnp.log(forget))
    b_ref, b_last = b[CHUNK // 2 - 1:CHUNK // 2, :], b[CHUNK - 1:CHUNK, :]
    e1, e2, e3, e4 = jnp.exp(b - b_ref), jnp.exp(b_ref - b), jnp.exp(b_last - b), jnp.exp(b)
    return dict(sg=sg, forget=forget, k=k, e1=e1, e2=e2, e3=e3, e4=e4, qa=q * e1, ka=k * e2, kl=k * e3, qb=q * e4,
                decay=jnp.exp(b_last))


def _hgrn_fwd(z, lbraw, nw):
    T = z.shape[0]
    G = min(HGRN_GROUP, T // CHUNK)
    rows = G * CHUNK
    n_chunks = T // CHUNK

    def body(q_ref, f_ref, i_ref, g_ref, lbraw_ref, nw_ref, oraw_ref, og_ref, sp_ref, st_ref):
        @pl.when(pl.program_id(0) == 0)
        def _():
            st_ref[...] = jnp.zeros_like(st_ref)

        lb_all = _lower_bound(lbraw_ref)
        tri_lo = _tri(True)

        def chunk(cc, carry):
            rs = pl.ds(pl.multiple_of(cc * CHUNK, CHUNK), CHUNK)
            for h in range(HEADS):
                ln = slice(HEAD_DIM * h, HEAD_DIM * (h + 1))
                v = i_ref[rs, ln]
                t = _hgrn_gates(q_ref[rs, ln], f_ref[rs, ln], lb_all[:, ln], tri_lo)
                st = st_ref[h]
                sp_ref[cc, h] = st
                a = jnp.where(tri_lo, _bdot(t["qa"], t["ka"], NT), 0.0)
                o = _bdot(a, v) + _bdot(t["qb"], st, NT)
                st_ref[h] = st * t["decay"] + _bdot(v, t["kl"], TN)
                oraw_ref[rs, ln] = o
                on, _ = _rms_fwd(o, nw_ref[:, ln])
                gate = g_ref[rs, ln]
                og_ref[rs, ln] = (on * (gate * _sigmoid(gate))).astype(BF16)
            return carry

        lax.fori_loop(0, G, chunk, 0)

    col = lambda j: pl.BlockSpec((rows, 512), lambda r, j=j: (r, j))
    return pl.pallas_call(
        body, name="hgrn_fwd", grid=(T // rows,),
        in_specs=[col(0), col(1), col(2), col(3), _full((2, 512)), _full((1, 512))],
        out_specs=[col(0), col(0), pl.BlockSpec((G, HEADS, HEAD_DIM, HEAD_DIM), lambda r: (r, 0, 0, 0))],
        out_shape=[jax.ShapeDtypeStruct((T, 512), F32), jax.ShapeDtypeStruct((T, 512), BF16),
                   jax.ShapeDtypeStruct((n_chunks, HEADS, HEAD_DIM, HEAD_DIM), F32)],
        scratch_shapes=[pltpu.VMEM((HEADS, HEAD_DIM, HEAD_DIM), F32)],
        compiler_params=_params(),
    )(z, z, z, z, lbraw, nw)


def _hgrn_bwd(dmixcat, z, oraw, sprev, lbraw, nw):
    T = z.shape[0]
    G = min(HGRN_GROUP, T // CHUNK)
    rows = G * CHUNK
    ng = T // rows

    def body(dog_ref, q_ref, f_ref, i_ref, g_ref, oraw_ref, sp_ref, lbraw_ref, nw_ref,
             dz_ref, dlb_ref, dnw_ref, dst_ref):
        @pl.when(pl.program_id(0) == 0)
        def _():
            dst_ref[...] = jnp.zeros_like(dst_ref)
            dlb_ref[...] = jnp.zeros_like(dlb_ref)
            dnw_ref[...] = jnp.zeros_like(dnw_ref)

        lb_all = _lower_bound(lbraw_ref)
        tri_lo, tri_up = _tri(True), _tri(False)
        rowid = lax.broadcasted_iota(jnp.int32, (CHUNK, HEAD_DIM), 0)

        def chunk(it, carry):
            cc = G - 1 - it
            rs = pl.ds(pl.multiple_of(cc * CHUNK, CHUNK), CHUNK)
            for h in range(HEADS):
                ln = slice(HEAD_DIM * h, HEAD_DIM * (h + 1))
                q, v, lb = q_ref[rs, ln], i_ref[rs, ln], lb_all[:, ln]
                t = _hgrn_gates(q, f_ref[rs, ln], lb, tri_lo)
                o, gate, nwh, dog = oraw_ref[rs, ln], g_ref[rs, ln], nw_ref[:, ln], dog_ref[rs, ln]
                _, rs_o = _rms_fwd(o, nwh)
                sgg = _sigmoid(gate)
                d_on = dog * (gate * sgg)
                dz_ref[rs, 1536 + HEAD_DIM * h:1536 + HEAD_DIM * (h + 1)] = (
                    dog * (o * rs_o * nwh) * (sgg * (1.0 + gate * (1.0 - sgg))))
                do, nw_rows = _rms_bwd(o, rs_o, nwh, d_on)
                dnw_ref[:, ln] += _rowsum(nw_rows)
                st, dst = sp_ref[cc, h], dst_ref[h]
                a = jnp.where(tri_lo, _bdot(t["qa"], t["ka"], NT), 0.0)
                dqb = _bdot(do, st)
                dkl = _bdot(v, dst)
                dv_ = _bdot(t["kl"], dst, NT) + _bdot(a, do, TN)
                ddecay = _rowsum(dst * st)
                da = jnp.where(tri_lo, _bdot(do, v, NT), 0.0)
                dqa = _bdot(da, t["ka"])
                dka = _bdot(da, t["qa"], TN)
                dst_ref[h] = dst * t["decay"] + _bdot(do, t["qb"], TN)
                pa, pk, pb, pl_ = dqa * t["qa"], dka * t["ka"], dqb * t["qb"], dkl * t["kl"]
                db = pa - pk + pb - pl_
                db = db + jnp.where(rowid == CHUNK // 2 - 1, _rowsum(pk - pa), 0.0)
                db = db + jnp.where(rowid == CHUNK - 1, _rowsum(pl_) + ddecay * t["decay"], 0.0)
                dlogf = _hdot(tri_up.astype(F32), db)
                dk_ = dka * t["e2"] + dkl * t["e3"]
                dforget = dlogf / t["forget"] - dk_
                sg = t["sg"]
                dz_ref[rs, ln] = dqa * t["e1"] + dqb * t["e4"]
                dz_ref[rs, 512 + HEAD_DIM * h:512 + HEAD_DIM * (h + 1)] = dforget * (1.0 - lb) * sg * (1.0 - sg)
                dz_ref[rs, 1024 + HEAD_DIM * h:1024 + HEAD_DIM * (h + 1)] = dv_
                dlb_ref[:, ln] += _rowsum(dforget * (1.0 - sg))
            return carry

        lax.fori_loop(0, G, chunk, 0)

    col = lambda j: pl.BlockSpec((rows, 512), lambda r, j=j: (ng - 1 - r, j))
    return pl.pallas_call(
        body, name="hgrn_bwd", grid=(ng,),
        in_specs=[col(0), col(0), col(1), col(2), col(3), col(0),
                  pl.BlockSpec((G, HEADS, HEAD_DIM, HEAD_DIM), lambda r: (ng - 1 - r, 0, 0, 0)),
                  _full((2, 512)), _full((1, 512))],
        out_specs=[pl.BlockSpec((rows, 2048), lambda r: (ng - 1 - r, 0)), _full((1, 512)), _full((1, 512))],
        out_shape=[jax.ShapeDtypeStruct((T, 2048), F32), jax.ShapeDtypeStruct((1, 512), F32),
                   jax.ShapeDtypeStruct((1, 512), F32)],
        scratch_shapes=[pltpu.VMEM((HEADS, HEAD_DIM, HEAD_DIM), F32)],
        compiler_params=_params(),
    )(dmixcat, z, z, z, z, oraw, sprev, lbraw, nw)


def _diag_mask(t):
    r = lax.broadcasted_iota(jnp.int32, (t, t), 0)
    c = lax.broadcasted_iota(jnp.int32, (t, t), 1)
    return r >= c


def _attn_fwd(q, k, v):
    _, T, _ = q.shape
    t = min(ATT_TILE, T)

    def body(q_ref, k_ref, v_ref, o_ref, lse_ref):
        i = pl.program_id(1)
        qb = q_ref[...]

        def step(j, carry, masked):
            m, l, acc = carry
            ks = pl.ds(pl.multiple_of(j * t, t), t)
            s = _dot(qb, k_ref[ks, :], NT) * ATT_SCALE
            if masked:
                s = jnp.where(_diag_mask(t), s, NEG_BIG)
            mn = jnp.maximum(m, jnp.max(s, axis=-1, keepdims=True))
            p = jnp.exp(s - mn)
            al = jnp.exp(m - mn)
            return mn, al * l + jnp.sum(p, axis=-1, keepdims=True), al * acc + _dot(p.astype(BF16), v_ref[ks, :])

        init = (jnp.full((t, 1), NEG_BIG, F32), jnp.zeros((t, 1), F32), jnp.zeros((t, HEAD_DIM), F32))
        carry = lax.fori_loop(0, i, lambda j, c: step(j, c, False), init)
        m, l, acc = step(i, carry, True)
        o_ref[...] = acc / l
        lse_ref[...] = jnp.broadcast_to(m + jnp.log(l), (t, HEAD_DIM))

    return pl.pallas_call(
        body, name="attn_fwd", grid=(HEADS, T // t),
        in_specs=[pl.BlockSpec((None, t, QK_PAD), lambda h, i: (h, i, 0)),
                  pl.BlockSpec((None, T, QK_PAD), lambda h, i: (h, 0, 0)),
                  pl.BlockSpec((None, T, HEAD_DIM), lambda h, i: (h, 0, 0))],
        out_specs=[pl.BlockSpec((t, HEAD_DIM), lambda h, i: (i, h)),
                   pl.BlockSpec((None, t, HEAD_DIM), lambda h, i: (h, i, 0))],
        out_shape=[jax.ShapeDtypeStruct((T, HEADS * HEAD_DIM), F32), jax.ShapeDtypeStruct((HEADS, T, HEAD_DIM), F32)],
        compiler_params=_params(),
    )(q, k, v)


def _attn_bwd(q, k, v, dmixcat, o, lse):
    _, T, _ = q.shape
    t = min(ATT_TILE, T)
    nq = T // t

    def body(q_ref, k_ref, v_ref, do_ref, o_ref, lse_ref, dq_ref, dk_ref, dv_ref, delta_ref):
        j = pl.program_id(1)

        @pl.when(j == 0)
        def _():
            dq_ref[...] = jnp.zeros_like(dq_ref)

            def fill(i, carry):
                rs = pl.ds(pl.multiple_of(i * t, t), t)
                delta_ref[rs, :] = jnp.broadcast_to(
                    jnp.sum(do_ref[rs, :] * o_ref[rs, :], axis=-1, keepdims=True), (t, HEAD_DIM))
                return carry

            lax.fori_loop(0, nq, fill, 0)

        kb, vb = k_ref[...], v_ref[...]

        def step(i, carry, masked):
            dk, dv = carry
            rs = pl.ds(pl.multiple_of(i * t, t), t)
            qb, dob = q_ref[rs, :], do_ref[rs, :].astype(BF16)
            s = _dot(qb, kb, NT) * ATT_SCALE
            p = jnp.exp(s - lse_ref[rs, 0:1])
            if masked:
                p = jnp.where(_diag_mask(t), p, 0.0)
            dp = _dot(dob, vb, NT)
            ds = (p * (dp - delta_ref[rs, 0:1]) * ATT_SCALE).astype(BF16)
            dq_ref[rs, :] += _dot(ds, kb)
            return dk + _dot(ds, qb, TN), dv + _dot(p.astype(BF16), dob, TN)

        carry = step(j, (jnp.zeros((t, QK_PAD), F32), jnp.zeros((t, HEAD_DIM), F32)), True)
        dk, dv = lax.fori_loop(j + 1, nq, lambda i, c: step(i, c, False), carry)
        dk_ref[...] = dk
        dv_ref[...] = dv

    return pl.pallas_call(
        body, name="attn_bwd", grid=(HEADS, nq),
        in_specs=[pl.BlockSpec((None, T, QK_PAD), lambda h, j: (h, 0, 0)),
                  pl.BlockSpec((None, t, QK_PAD), lambda h, j: (h, j, 0)),
                  pl.BlockSpec((None, t, HEAD_DIM), lambda h, j: (h, j, 0)),
                  pl.BlockSpec((T, HEAD_DIM), lambda h, j: (0, HEADS + h)),
                  pl.BlockSpec((T, HEAD_DIM), lambda h, j: (0, h)),
                  pl.BlockSpec((None, T, HEAD_DIM), lambda h, j: (h, 0, 0))],
        out_specs=[pl.BlockSpec((None, T, QK_PAD), lambda h, j: (h, 0, 0)),
                   pl.BlockSpec((None, t, QK_PAD), lambda h, j: (h, j, 0)),
                   pl.BlockSpec((None, t, HEAD_DIM), lambda h, j: (h, j, 0))],
        out_shape=[jax.ShapeDtypeStruct((HEADS, T, QK_PAD), F32), jax.ShapeDtypeStruct((HEADS, T, QK_PAD), F32),
                   jax.ShapeDtypeStruct((HEADS, T, HEAD_DIM), F32)],
        scratch_shapes=[pltpu.VMEM((T, HEAD_DIM), F32)],
        compiler_params=_params(),
    )(q, k, v, dmixcat, o, lse)


def _ln_fwd(r):
    mu = _lanemean(r)
    xc = r - mu
    rstd = lax.rsqrt(_lanemean(xc * xc) + LN_EPS)
    return xc * rstd, rstd


def _ln_bwd(dxh, xhat, rstd):
    return rstd * (dxh - _lanemean(dxh) - xhat * _lanemean(dxh * xhat))


def _mix_ln1(mixcat, w_out, x, g_a, ln1_g, ln1_b, sc_m, sh_m):
    T = x.shape[0]
    tm = min(ROW_TILE_SMALL, T)

    def body(mc_ref, w_ref, x_ref, ga_ref, g_ref, b_ref, sc_ref, sh_ref, mix_ref, xhat_ref, rstd_ref, u2_ref):
        mix = _dot(mc_ref[...], w_ref[...])
        mix_ref[...] = mix
        xhat, rstd = _ln_fwd(ALPHA * x_ref[...] + (1.0 + ga_ref[...]) * mix)
        xhat_ref[...] = xhat
        rstd_ref[...] = jnp.broadcast_to(rstd, (tm, 128))
        u2_ref[...] = _modulate(xhat * g_ref[...] + b_ref[...], sc_ref[...], sh_ref[...]).astype(BF16)

    row = pl.BlockSpec((tm, D_MODEL), lambda i: (i, 0))
    vec = _full((1, D_MODEL))
    return pl.pallas_call(
        body, name="mix_ln1", grid=(T // tm,),
        in_specs=[row, _full(w_out.shape), row, vec, vec, vec, vec, vec],
        out_specs=[row, row, pl.BlockSpec((tm, 128), lambda i: (i, 0)), row],
        out_shape=[jax.ShapeDtypeStruct((T, D_MODEL), F32), jax.ShapeDtypeStruct((T, D_MODEL), F32),
                   jax.ShapeDtypeStruct((T, 128), F32), jax.ShapeDtypeStruct((T, D_MODEL), BF16)],
        compiler_params=_params(),
    )(mixcat, w_out, x, g_a, ln1_g, ln1_b, sc_m, sh_m)


def _mlp_fwd(u2, w1, w2, xhat1, ln1_g, ln1_b, g_m, ln2_g, ln2_b, target):
    T = u2.shape[0]
    nf, _, tf = w1.shape
    tm = min(ROW_TILE, T)

    def body(u2_ref, w1_ref, w2_ref, xhat_ref, g1_ref, b1_ref, gm_ref, g2_ref, b2_ref, tgt_ref,
             r_ref, dr2_ref, dh_ref, dg2_ref, db2_ref, dgm_ref, loss_ref, acc_ref):
        i, f = pl.program_id(0), pl.program_id(1)

        @pl.when((i == 0) & (f == 0))
        def _():
            for ref in (dg2_ref, db2_ref, dgm_ref, loss_ref):
                ref[...] = jnp.zeros_like(ref)

        @pl.when(f == 0)
        def _():
            acc_ref[...] = jnp.zeros_like(acc_ref)

        r = jnp.maximum(_dot(u2_ref[...], w1_ref[...]), 0.0)
        r_ref[...] = r.astype(BF16)
        acc_ref[...] += _bdot(r * r, w2_ref[...])

        @pl.when(f == nf - 1)
        def _():
            h = acc_ref[...]
            x1 = xhat_ref[...] * g1_ref[...] + b1_ref[...]
            xhat2, rstd2 = _ln_fwd(ALPHA * x1 + (1.0 + gm_ref[...]) * h)
            err = xhat2 * g2_ref[...] + b2_ref[...] - tgt_ref[...]
            loss_ref[...] += jnp.sum(0.5 * _lanemean(err * err), axis=0, keepdims=True)
            dy = err * (1.0 / D_MODEL)
            dg2_ref[...] += _rowsum(dy * xhat2)
            db2_ref[...] += _rowsum(dy)
            dr2 = _ln_bwd(dy * g2_ref[...], xhat2, rstd2)
            dr2_ref[...] = dr2
            dgm_ref[...] += _rowsum(dr2 * h)
            dh_ref[...] = ((1.0 + gm_ref[...]) * dr2).astype(BF16)

    row = pl.BlockSpec((tm, D_MODEL), lambda i, f: (i, 0))
    vec = _full((1, D_MODEL))
    return pl.pallas_call(
        body, name="mlp_fwd", grid=(T // tm, nf),
        in_specs=[row, pl.BlockSpec((None, D_MODEL, tf), lambda i, f: (f, 0, 0)),
                  pl.BlockSpec((None, tf, D_MODEL), lambda i, f: (f, 0, 0)), row, vec, vec, vec, vec, vec, row],
        out_specs=[pl.BlockSpec((tm, tf), lambda i, f: (i, f)), row, row, vec, vec, vec, _full((1, 128))],
        out_shape=[jax.ShapeDtypeStruct((T, nf * tf), BF16), jax.ShapeDtypeStruct((T, D_MODEL), F32),
                   jax.ShapeDtypeStruct((T, D_MODEL), BF16), jax.ShapeDtypeStruct((1, D_MODEL), F32),
                   jax.ShapeDtypeStruct((1, D_MODEL), F32), jax.ShapeDtypeStruct((1, D_MODEL), F32),
                   jax.ShapeDtypeStruct((1, 128), F32)],
        scratch_shapes=[pltpu.VMEM((tm, D_MODEL), F32)],
        compiler_params=_params(),
    )(u2, w1, w2, xhat1, ln1_g, ln1_b, g_m, ln2_g, ln2_b, target)


def _mlp_bwd(dh, w1, w2, r, dr2, xhat1, rstd1, mix, ln1_g, ln1_b, sc_m, g_a):
    T = dh.shape[0]
    nf, _, tf = w1.shape
    tm = min(ROW_TILE, T)

    def body(dh_ref, w1_ref, w2_ref, r_ref, dr2_ref, xhat_ref, rstd_ref, mix_ref, g1_ref, b1_ref, sc_ref, ga_ref,
             dhpre_ref, dr1_ref, dmix_ref, dsc_ref, dsh_ref, dg1_ref, db1_ref, dga_ref, acc_ref):
        i, f = pl.program_id(0), pl.program_id(1)

        @pl.when((i == 0) & (f == 0))
        def _():
            for ref in (dsc_ref, dsh_ref, dg1_ref, db1_ref, dga_ref):
                ref[...] = jnp.zeros_like(ref)

        @pl.when(f == 0)
        def _():
            acc_ref[...] = jnp.zeros_like(acc_ref)

        dhpre = (_dot(dh_ref[...], w2_ref[...], NT) * (2.0 * r_ref[...].astype(F32))).astype(BF16)
        dhpre_ref[...] = dhpre
        acc_ref[...] += _dot(dhpre, w1_ref[...], NT)

        @pl.when(f == nf - 1)
        def _():
            du2 = acc_ref[...]
            xhat = xhat_ref[...]
            x1 = xhat * g1_ref[...] + b1_ref[...]
            dx1 = ALPHA * dr2_ref[...] + du2 * (1.0 + sc_ref[...])
            dsc_ref[...] += _rowsum(du2 * x1)
            dsh_ref[...] += _rowsum(du2)
            dg1_ref[...] += _rowsum(dx1 * xhat)
            db1_ref[...] += _rowsum(dx1)
            dr1 = _ln_bwd(dx1 * g1_ref[...], xhat, rstd_ref[:, 0:1])
            dr1_ref[...] = dr1
            dga_ref[...] += _rowsum(dr1 * mix_ref[...])
            dmix_ref[...] = ((1.0 + ga_ref[...]) * dr1).astype(BF16)

    row = pl.BlockSpec((tm, D_MODEL), lambda i, f: (i, 0))
    vec = _full((1, D_MODEL))
    return pl.pallas_call(
        body, name="mlp_bwd", grid=(T // tm, nf),
        in_specs=[row, pl.BlockSpec((None, D_MODEL, tf), lambda i, f: (f, 0, 0)),
                  pl.BlockSpec((None, tf, D_MODEL), lambda i, f: (f, 0, 0)),
                  pl.BlockSpec((tm, tf), lambda i, f: (i, f)), row, row, pl.BlockSpec((tm, 128), lambda i, f: (i, 0)),
                  row, vec, vec, vec, vec],
        out_specs=[pl.BlockSpec((tm, tf), lambda i, f: (i, f)), row, row, vec, vec, vec, vec, vec],
        out_shape=[jax.ShapeDtypeStruct((T, nf * tf), BF16), jax.ShapeDtypeStruct((T, D_MODEL), F32),
                   jax.ShapeDtypeStruct((T, D_MODEL), BF16)] + [jax.ShapeDtypeStruct((1, D_MODEL), F32)] * 5,
        scratch_shapes=[pltpu.VMEM((tm, D_MODEL), F32)],
        compiler_params=_params(),
    )(dh, w1, w2, r, dr2, xhat1, rstd1, mix, ln1_g, ln1_b, sc_m, g_a)


def _input_bwd(dz_h, dz_m, w_in_ext, x, dr1, sc_a):
    T = x.shape[0]
    tm = min(ROW_TILE_SMALL, T)

    def body(dzh_ref, dzm_ref, w_ref, x_ref, dr1_ref, sc_ref, gx_ref, dsc_ref, dsh_ref):
        @pl.when(pl.program_id(0) == 0)
        def _():
            dsc_ref[...] = jnp.zeros_like(dsc_ref)
            dsh_ref[...] = jnp.zeros_like(dsh_ref)

        du = _bdot(dzh_ref[...], w_ref[:, 0:2048], NT) + _bdot(dzm_ref[...], w_ref[:, 2048:3072], NT)
        gx_ref[...] = ALPHA * dr1_ref[...] + du * (1.0 + sc_ref[...])
        dsc_ref[...] += _rowsum(du * x_ref[...])
        dsh_ref[...] += _rowsum(du)

    row = pl.BlockSpec((tm, D_MODEL), lambda i: (i, 0))
    vec = _full((1, D_MODEL))
    return pl.pallas_call(
        body, name="input_bwd", grid=(T // tm,),
        in_specs=[pl.BlockSpec((tm, 2048), lambda i: (i, 0)), row, _full(w_in_ext.shape), row, row, vec],
        out_specs=[row, vec, vec],
        out_shape=[jax.ShapeDtypeStruct((T, D_MODEL), F32), jax.ShapeDtypeStruct((1, D_MODEL), F32),
                   jax.ShapeDtypeStruct((1, D_MODEL), F32)],
        compiler_params=_params(),
    )(dz_h, dz_m, w_in_ext, x, dr1, sc_a)


def _adam_math(w, g, m, v):
    m = ADAM_B1 * m + (1.0 - ADAM_B1) * g
    v = ADAM_B2 * v + (1.0 - ADAM_B2) * (g * g)
    m_hat = m / (1.0 - ADAM_B1 ** ADAM_STEP)
    v_hat = v / (1.0 - ADAM_B2 ** ADAM_STEP)
    return -ADAM_LR * (m_hat / (jnp.sqrt(v_hat) + ADAM_EPS) + ADAM_WD * w), m, v


def _adam(g_slabs, w, m, v, name, g_fn=None, g_extra=()):
    R, C = w.shape
    tr = R if R <= 256 else 256
    assert R % tr == 0
    ns = 0 if g_slabs is None else g_slabs.shape[0]
    ne = len(g_extra)

    def body(*refs):
        e_refs = refs[:ne]
        refs = refs[ne:]
        if ns:
            gs_ref, refs = refs[0], refs[1:]
        w_ref, m_ref, v_ref, g_ref, d_ref, nm_ref, nv_ref = refs
        if g_fn is not None:
            g = g_fn(*e_refs)
        else:
            g = gs_ref[0]
            for s in range(1, ns):
                g = g + gs_ref[s]
        d, nm, nv = _adam_math(w_ref[...], g, m_ref[...], v_ref[...])
        g_ref[...] = g
        d_ref[...] = d
        nm_ref[...] = nm
        nv_ref[...] = nv

    blk = pl.BlockSpec((tr, C), lambda i: (i, 0))
    in_specs = [pl.BlockSpec((tr, e.shape[1]), lambda i: (i, 0)) if e.shape[0] == R else _full(e.shape) for e in g_extra]
    args = list(g_extra)
    if ns:
        in_specs.append(pl.BlockSpec((ns, tr, C), lambda i: (0, i, 0)))
        args.append(g_slabs)
    return pl.pallas_call(
        body, name=name, grid=(R // tr,), in_specs=in_specs + [blk] * 3, out_specs=[blk] * 4,
        out_shape=[jax.ShapeDtypeStruct((R, C), F32)] * 4, compiler_params=_params(),
    )(*args, w, m, v)


def _small_reduce(small_all, lbraw):
    W = small_all.shape[-1]

    def body(s_ref, lbraw_ref, sum_ref, glb_ref):
        tot = s_ref[0]
        for i in range(1, N_DEV):
            tot = tot + s_ref[i]
        sum_ref[...] = tot
        lb = _lower_bound(lbraw_ref)
        g0 = tot[:, 6144:6656] * lb * (1.0 - lb)
        glb_ref[0:1, :] = g0
        glb_ref[1:2, :] = -g0

    return pl.pallas_call(
        body, name="small_reduce",
        out_shape=[jax.ShapeDtypeStruct((1, W), F32), jax.ShapeDtypeStruct((2, 512), F32)],
        compiler_params=_params(),
    )(small_all, lbraw)


def _rot_half_cols(w):
    return jnp.concatenate([-w[..., 32:], w[..., :32]], axis=-1)


def _unrot_half_cols(dw_rot):
    return jnp.concatenate([dw_rot[..., 32:], -dw_rot[..., :32]], axis=-1)


def _cols_from_slabs(g):
    s, r, c = g.shape
    return jnp.transpose(g, (1, 0, 2)).reshape(r, s * c)


def _slabs_from_cols(w):
    r, c = w.shape
    return jnp.transpose(w.reshape(r, N_DEV, c // N_DEV), (1, 0, 2))


def _build_ext(w_in, w_q_up, w_kv_up):
    k_in = w_in.shape[0]
    z64 = jnp.zeros((k_in, 64), BF16)
    z128 = jnp.zeros((k_in, 128), BF16)
    wk = w_in[:, 2560:2624]
    w_in_ext = jnp.concatenate([w_in[:, :2560], z128, wk, z64, z128, _rot_half_cols(wk), z64], axis=1)
    r = w_q_up.shape[0]
    z64, z128 = jnp.zeros((r, 64), BF16), jnp.zeros((r, 128), BF16)
    wq = w_q_up.reshape(r, HEADS, HEAD_DIM + ROPE_DIM)
    main = [jnp.concatenate([wq[:, h, :HEAD_DIM], wq[:, h, HEAD_DIM:], z64], axis=1) for h in range(HEADS)]
    rot = [jnp.concatenate([z128, _rot_half_cols(wq[:, h, HEAD_DIM:]), z64], axis=1) for h in range(HEADS)]
    wq_ext = jnp.concatenate(main + rot, axis=1)
    wkv = w_kv_up.reshape(r, HEADS, 2 * HEAD_DIM)
    kpad = [jnp.concatenate([wkv[:, h, :HEAD_DIM], z128], axis=1) for h in range(HEADS)]
    vals = [wkv[:, h, HEAD_DIM:] for h in range(HEADS)]
    wkv_ext = jnp.concatenate(kpad + vals, axis=1)
    return w_in_ext, wq_ext, wkv_ext


def _grads_from_ext(dw_in_h, dw_in_m, dwq_ext, dwkv_ext):
    dwk = dw_in_m[:, 512 + 128:512 + 192] + _unrot_half_cols(dw_in_m[:, 768 + 128:768 + 192])
    dw_in = jnp.concatenate([dw_in_h, dw_in_m[:, :512], dwk], axis=1)
    qcols = []
    for h in range(HEADS):
        main, rot = dwq_ext[:, 256 * h:256 * h + 256], dwq_ext[:, 1024 + 256 * h:1280 + 256 * h]
        qcols += [main[:, :128], main[:, 128:192] + _unrot_half_cols(rot[:, 128:192])]
    kvcols = []
    for h in range(HEADS):
        kvcols += [dwkv_ext[:, 256 * h:256 * h + 128], dwkv_ext[:, 1024 + 128 * h:1152 + 128 * h]]
    return dw_in, jnp.concatenate(qcols, axis=1), jnp.concatenate(kvcols, axis=1)


SMALL_W = 6144 + 512 + 512 + 256 + 256 + 4 * 1024 + 128


def kernel(x, c, positions, w_ada, b_ada, w_in, hg_lower_bounds, hg_norm_w, mla_q_norm_w, w_q_up, mla_kv_norm_w, w_kv_up, w_out, ln1_g, ln1_b, w_mlp_in, w_mlp_out, ln2_g, ln2_b, loss_target, m_w_ada, m_b_ada, m_w_in, m_hg_lower_bounds, m_hg_norm_w, m_mla_q_norm_w, m_w_q_up, m_mla_kv_norm_w, m_w_kv_up, m_w_out, m_ln1_g, m_ln1_b, m_w_mlp_in, m_w_mlp_out, m_ln2_g, m_ln2_b, v_w_ada, v_b_ada, v_w_in, v_hg_lower_bounds, v_hg_norm_w, v_mla_q_norm_w, v_w_q_up, v_mla_kv_norm_w, v_w_kv_up, v_w_out, v_ln1_g, v_ln1_b, v_w_mlp_in, v_w_mlp_out, v_ln2_g, v_ln2_b):
    T = x.shape[1]
    me = 4 * lax.axis_index("x") + 2 * lax.axis_index("y") + lax.axis_index("c")
    xs, tgt = x[0], loss_target[0]
    big = dict(w_in=w_in[0], w_q_up=w_q_up[0], w_kv_up=w_kv_up[0], w_out=w_out[0], w_mlp_in=w_mlp_in[0],
               w_mlp_out=w_mlp_out[0])
    names = list(big)

    gathered = _exchange([big[n].astype(BF16) for n in names] + [c], scatter=False, name="gather_weights")
    gw = dict(zip(names, gathered[:-1]))
    c_all = gathered[-1].reshape(N_DEV, D_MODEL)
    w_in_ext, wq_ext, wkv_ext = _build_ext(_cols_from_slabs(gw["w_in"]), _cols_from_slabs(gw["w_q_up"]),
                                           _cols_from_slabs(gw["w_kv_up"]))
    w_out_full = gw["w_out"].reshape(D_MODEL, D_MODEL)
    w1, w2 = gw["w_mlp_in"], gw["w_mlp_out"]

    ada_cols = w_ada.shape[2]
    mod_part, cond = _mod_part(c_all, w_ada[0], lax.dynamic_slice(b_ada, (0, me * ada_cols), (1, ada_cols)))
    (mod_all,) = _exchange([mod_part], scatter=False, name="gather_mod")
    mod_row = lax.dynamic_slice(mod_all, (0, me, 0), (N_DEV, 1, ada_cols)).reshape(1, N_DEV * ada_cols)
    sh_a, sc_a, g_a, sh_m, sc_m, g_m = [mod_row[:, D_MODEL * i:D_MODEL * (i + 1)] for i in range(6)]

    z = _matmul(xs, w_in_ext, "NN", "in_proj", a_fn=_modulate, extras=(sc_a, sh_a))
    inv_freq = 1.0 / (ROPE_THETA ** (jnp.arange(0, ROPE_DIM, 2, dtype=F32) / ROPE_DIM))
    zeros = lambda n: jnp.zeros((n,), F32)
    invf = jnp.concatenate([zeros(128), inv_freq, inv_freq, zeros(64)]).reshape(1, QK_PAD)
    m_one = jnp.concatenate([jnp.ones((128,), F32), zeros(128)]).reshape(1, QK_PAD)
    m_rot = jnp.concatenate([zeros(128), jnp.ones((64,), F32), zeros(64)]).reshape(1, QK_PAD)
    q, k, v, c1, s1, cqn, ckvn = _mla_pre(z, positions.reshape(T, 1), invf, m_one, m_rot, wq_ext, wkv_ext,
                                          mla_q_norm_w, mla_kv_norm_w)
    o_raw, o_gated, s_prev = _hgrn_fwd(z, hg_lower_bounds, hg_norm_w)
    o_mla, lse = _attn_fwd(q, k, v)
    mixcat = jnp.concatenate([o_gated, o_mla.astype(BF16)], axis=1)
    mix, xhat1, rstd1, u2 = _mix_ln1(mixcat, w_out_full, xs, g_a, ln1_g, ln1_b, sc_m, sh_m)
    r, dr2, dh, dln2_g, dln2_b, dg_m, loss_part = _mlp_fwd(u2, w1, w2, xhat1, ln1_g, ln1_b, g_m, ln2_g, ln2_b, tgt)

    dhpre, dr1, dmix, dsc_m, dsh_m, dln1_g, dln1_b, dg_a = _mlp_bwd(dh, w1, w2, r, dr2, xhat1, rstd1, mix, ln1_g,
                                                                    ln1_b, sc_m, g_a)
    dw2 = _matmul(r, dh, "TN", "wgrad_mlp_out", a_fn=_square, tk=512)
    dw1 = _matmul(u2, dhpre, "TN", "wgrad_mlp_in", tk=512, out_slabs=N_DEV)
    dmixcat = _matmul(dmix, w_out_full, "NT", "dgrad_out")
    dw_out = _matmul(mixcat, dmix, "TN", "wgrad_out", tk=512)
    dz_h, dlb, dnw = _hgrn_bwd(dmixcat, z, o_raw, s_prev, hg_lower_bounds, hg_norm_w)
    dq, dk, dv = _attn_bwd(q, k, v, dmixcat, o_mla, lse)
    dz_m, dq_ext, dkv_ext, dqnw, dkvnw = _mla_bwd(dq, dk, dv, z, c1, s1, wq_ext, wkv_ext, mla_q_norm_w,
                                                   mla_kv_norm_w)
    dwq_ext = _matmul(cqn, dq_ext, "TN", "wgrad_q_up", tk=512)
    dwkv_ext = _matmul(ckvn, dkv_ext, "TN", "wgrad_kv_up", tn=768, tk=512)
    grad_x, dsc_a, dsh_a = _input_bwd(dz_h, dz_m, w_in_ext, xs, dr1, sc_a)
    dw_in_h = _matmul(xs, dz_h, "TN", "wgrad_in_h", a_fn=_modulate, extras=(sc_a, sh_a), tk=512)
    dw_in_m = _matmul(xs, dz_m, "TN", "wgrad_in_m", a_fn=_modulate, extras=(sc_a, sh_a), tk=512)
    dw_in, dwq, dwkv = _grads_from_ext(dw_in_h, dw_in_m, dwq_ext, dwkv_ext)

    small = jnp.concatenate([dsh_a, dsc_a, dg_a, dsh_m, dsc_m, dg_m, dlb, dnw, dqnw, dkvnw, dln1_g, dln1_b, dln2_g,
                             dln2_b, loss_part], axis=1)
    slabs = dict(w_in=_slabs_from_cols(dw_in), w_q_up=_slabs_from_cols(dwq), w_kv_up=_slabs_from_cols(dwkv),
                 w_out=dw_out.reshape(N_DEV, D_MODEL // N_DEV, D_MODEL), w_mlp_in=dw1,
                 w_mlp_out=dw2.reshape(N_DEV, dw2.shape[0] // N_DEV, D_MODEL))
    received = dict(zip(names, _exchange([slabs[n] for n in names], scatter=True, name="scatter_grads")))
    (small_all,) = _exchange([small], scatter=False, name="gather_small")
    small_sum, glb = _small_reduce(small_all, hg_lower_bounds)

    moments = dict(w_in=(m_w_in, v_w_in), w_q_up=(m_w_q_up, v_w_q_up), w_kv_up=(m_w_kv_up, v_w_kv_up),
                   w_out=(m_w_out, v_w_out), w_mlp_in=(m_w_mlp_in, v_w_mlp_in), w_mlp_out=(m_w_mlp_out, v_w_mlp_out))
    res = {}
    for n in names:
        res[n] = _adam(received[n], big[n], moments[n][0][0], moments[n][1][0], name="adam_" + n)
    dmod_cols = lax.dynamic_slice(small_all.reshape(N_DEV, SMALL_W), (0, me * ada_cols), (N_DEV, ada_cols))
    cond_t = cond.T

    def ada_grad(ct_ref, dm_ref):
        g = ct_ref[:, 0:1] * dm_ref[0:1, :]
        for b in range(1, N_DEV):
            g = g + ct_ref[:, b:b + 1] * dm_ref[b:b + 1, :]
        return g

    res["w_ada"] = _adam(None, w_ada[0], m_w_ada[0], v_w_ada[0], name="adam_w_ada", g_fn=ada_grad,
                         g_extra=(cond_t, dmod_cols))

    seg = lambda a, b: small_sum[:, a:b]
    small_params = [("b_ada", b_ada, m_b_ada, v_b_ada, seg(0, 6144)),
                    ("hg_lower_bounds", hg_lower_bounds, m_hg_lower_bounds, v_hg_lower_bounds, glb),
                    ("hg_norm_w", hg_norm_w, m_hg_norm_w, v_hg_norm_w, seg(6656, 7168)),
                    ("mla_q_norm_w", mla_q_norm_w, m_mla_q_norm_w, v_mla_q_norm_w, seg(7168, 7424)),
                    ("mla_kv_norm_w", mla_kv_norm_w, m_mla_kv_norm_w, v_mla_kv_norm_w, seg(7424, 7680)),
                    ("ln1_g", ln1_g, m_ln1_g, v_ln1_g, seg(7680, 8704)), ("ln1_b", ln1_b, m_ln1_b, v_ln1_b, seg(8704, 9728)),
                    ("ln2_g", ln2_g, m_ln2_g, v_ln2_g, seg(9728, 10752)), ("ln2_b", ln2_b, m_ln2_b, v_ln2_b, seg(10752, 11776))]
    pack = lambda i: jnp.concatenate([p[i].reshape(1, -1) for p in small_params], axis=1)
    packed = _adam(pack(4)[None], pack(1), pack(2), pack(3), name="adam_small")
    off = 0
    for n, w, _, _, _ in small_params:
        res[n] = tuple(a[:, off:off + w.size].reshape(w.shape) for a in packed)
        off += w.size
    loss = small_sum[0, 11776]

    order = ["w_ada", "b_ada", "w_in", "hg_lower_bounds", "hg_norm_w", "mla_q_norm_w", "w_q_up", "mla_kv_norm_w",
             "w_kv_up", "w_out", "ln1_g", "ln1_b", "w_mlp_in", "w_mlp_out", "ln2_g", "ln2_b"]
    shaped = {n: tuple(a.reshape((1,) + a.shape) if n in big or n == "w_ada" else a for a in res[n]) for n in order}
    outs = [loss, grad_x.reshape(1, T, D_MODEL)]
    for i in range(4):
        outs += [shaped[n][i] for n in order]
    return tuple(outs)
```

```python
import functools

import jax
import jax.numpy as jnp
import numpy as np
from jax import lax
from jax.experimental import pallas as pl
from jax.experimental.pallas import tpu as pltpu

F32, BF16 = jnp.float32, jnp.bfloat16
N_DEV = 8
D_MODEL = 1024
HEADS = 4
HEAD_DIM = 128
ROPE_DIM = 64
QK_PAD = 256
CHUNK = 64
ROPE_THETA = 10000.0
RMS_EPS = 1e-6
LN_EPS = 1e-5
ALPHA = 2.0 ** 0.25
ATT_SCALE = (HEAD_DIM + ROPE_DIM) ** -0.5
ADAM_LR, ADAM_B1, ADAM_B2, ADAM_EPS, ADAM_WD, ADAM_STEP = 0.001, 0.9, 0.999, 1e-08, 0.01, 10
NEG_BIG = -1e30

ROW_TILE = 512
ROW_TILE_SMALL = 256
ATT_TILE = 256
HGRN_GROUP = 8
VMEM_LIMIT = 56 * 2 ** 20

NN = (((1,), (0,)), ((), ()))
NT = (((1,), (1,)), ((), ()))
TN = (((0,), (0,)), ((), ()))


def _dot(a, b, dims=NN):
    return lax.dot_general(a, b, dims, preferred_element_type=F32)


def _bdot(a, b, dims=NN):
    return lax.dot_general(a.astype(BF16), b.astype(BF16), dims, preferred_element_type=F32)


def _hdot(a, b, dims=NN):
    return lax.dot_general(a, b, dims, precision=lax.Precision.HIGHEST, preferred_element_type=F32)


def _params():
    return pltpu.CompilerParams(vmem_limit_bytes=VMEM_LIMIT)


def _sigmoid(x):
    return 1.0 / (1.0 + jnp.exp(-x))


def _rowsum(x):
    return jnp.sum(x, axis=0, keepdims=True)


def _lanemean(x):
    return jnp.mean(x, axis=-1, keepdims=True)


def _full(shape):
    nd = len(shape)
    return pl.BlockSpec(shape, lambda *_: (0,) * nd)


class _Exchange:
    def __init__(self, arrs, scatter):
        self.arrs, self.scatter, self.n = list(arrs), scatter, len(arrs)
        self.out_shape = [jax.ShapeDtypeStruct((N_DEV,) + (a.shape[1:] if scatter else a.shape), a.dtype)
                          for a in self.arrs]
        n = self.n
        self.scratch = [pltpu.SemaphoreType.DMA((n, N_DEV - 1)), pltpu.SemaphoreType.DMA((n, N_DEV - 1)),
                        pltpu.SemaphoreType.DMA((n,))]

    def _copies(self, ins, outs, sems):
        send_sems, recv_sems, loc_sems = sems
        x, y, c = lax.axis_index("x"), lax.axis_index("y"), lax.axis_index("c")
        me = 4 * x + 2 * y + c
        copies = []
        for k in range(self.n):
            src_of = (lambda i, k=k: ins[k].at[i]) if self.scatter else (lambda i, k=k: ins[k])
            copies.append((pltpu.make_async_copy(src_of(me), outs[k].at[me], loc_sems.at[k]), None))
            for p in range(1, N_DEV):
                px = (1 - x) if p & 4 else x
                py = (1 - y) if p & 2 else y
                pc = (1 - c) if p & 1 else c
                peer = 4 * px + 2 * py + pc
                both = dict(send_sem=send_sems.at[k, p - 1], recv_sem=recv_sems.at[k, p - 1],
                            device_id=(px, py, pc), device_id_type=pl.DeviceIdType.MESH)
                send = pltpu.make_async_remote_copy(src_ref=src_of(peer), dst_ref=outs[k].at[me], **both)
                recv = pltpu.make_async_remote_copy(src_ref=src_of(peer), dst_ref=outs[k].at[peer], **both)
                copies.append((send, recv))
        return copies

    def start(self, ins, outs, sems):
        for first, _ in self._copies(ins, outs, sems):
            first.start()

    def wait(self, ins, outs, sems):
        for first, recv in self._copies(ins, outs, sems):
            if recv is None:
                first.wait()
            else:
                recv.wait_recv()
                first.wait_send()


def _call(body, name, args, out_shape, grid=(), in_specs=(), out_specs=(), scratch_shapes=(), exchange=None):
    if exchange is None:
        return pl.pallas_call(body, name=name, grid=grid, in_specs=list(in_specs), out_specs=list(out_specs),
                              out_shape=list(out_shape), scratch_shapes=list(scratch_shapes),
                              compiler_params=_params())(*args), None
    ni, no, ns, nx = len(args), len(out_shape), len(scratch_shapes), exchange.n

    def wrapped(*refs):
        a, xi = refs[:ni], refs[ni:ni + nx]
        o, xo = refs[ni + nx:ni + nx + no], refs[ni + nx + no:ni + 2 * nx + no]
        s, xs = refs[ni + 2 * nx + no:ni + 2 * nx + no + ns], refs[ni + 2 * nx + no + ns:]
        first = last = None
        for d, g in enumerate(grid):
            f, l = pl.program_id(d) == 0, pl.program_id(d) == g - 1
            first, last = (f, l) if first is None else (first & f, last & l)

        @pl.when(first)
        def _():
            exchange.start(xi, xo, xs)

        body(*a, *o, *s)

        @pl.when(last)
        def _():
            exchange.wait(xi, xo, xs)

    hbm = pl.BlockSpec(memory_space=pltpu.HBM)
    res = pl.pallas_call(
        wrapped, name=name, grid=grid, in_specs=list(in_specs) + [hbm] * nx, out_specs=list(out_specs) + [hbm] * nx,
        out_shape=list(out_shape) + exchange.out_shape, scratch_shapes=list(scratch_shapes) + exchange.scratch,
        compiler_params=_params())(*args, *exchange.arrs)
    return res[:no], res[no:]


def _exchange(arrs, scatter, name):
    ex = _Exchange(arrs, scatter)

    def body(*refs):
        ins, outs, sems = refs[:ex.n], refs[ex.n:2 * ex.n], refs[2 * ex.n:]
        ex.start(ins, outs, sems)
        ex.wait(ins, outs, sems)

    hbm = pl.BlockSpec(memory_space=pltpu.HBM)
    return pl.pallas_call(body, name=name, out_shape=ex.out_shape, in_specs=[hbm] * ex.n, out_specs=[hbm] * ex.n,
                          scratch_shapes=ex.scratch)(*ex.arrs)


def _matmul(a, b, mode, name, out_dtype=F32, tm=512, tn=1024, tk=1024, a_fn=None, extras=(), out_slabs=None,
            exchange=None):
    if mode == "NN":
        (M, K), N = a.shape, b.shape[1]
    elif mode == "NT":
        (M, K), N = a.shape, b.shape[0]
    else:
        (K, M), N = a.shape, b.shape[1]
    if out_slabs:
        tn = N // out_slabs
    tm, tn, tk = min(tm, M), min(tn, N), min(tk, K)
    assert M % tm == 0 and N % tn == 0 and K % tk == 0, (name, M, N, K)
    nk = K // tk
    dims = {"NN": NN, "NT": NT, "TN": TN}[mode]
    ne = len(extras)

    def body(a_ref, b_ref, *rest):
        e_refs, o_ref, acc_ref = rest[:ne], rest[ne], rest[ne + 1]
        k = pl.program_id(2)

        @pl.when(k == 0)
        def _():
            acc_ref[...] = jnp.zeros_like(acc_ref)

        at = a_ref[...]
        if a_fn is not None:
            at = a_fn(at.astype(F32), *[e[...] for e in e_refs])
        acc_ref[...] += _bdot(at, b_ref[...], dims)

        @pl.when(k == nk - 1)
        def _():
            o_ref[...] = acc_ref[...].astype(out_dtype)

    if mode == "TN":
        a_spec = pl.BlockSpec((tk, tm), lambda i, j, k: (k, i))
        e_spec = pl.BlockSpec((1, tm), lambda i, j, k: (0, i))
    else:
        a_spec = pl.BlockSpec((tm, tk), lambda i, j, k: (i, k))
        e_spec = pl.BlockSpec((1, tk), lambda i, j, k: (0, k))
    if mode == "NT":
        b_spec = pl.BlockSpec((tn, tk), lambda i, j, k: (j, k))
    else:
        b_spec = pl.BlockSpec((tk, tn), lambda i, j, k: (k, j))
    if out_slabs:
        o_shape = jax.ShapeDtypeStruct((out_slabs, M, tn), out_dtype)
        o_spec = pl.BlockSpec((None, tm, tn), lambda i, j, k: (j, i, 0))
    else:
        o_shape = jax.ShapeDtypeStruct((M, N), out_dtype)
        o_spec = pl.BlockSpec((tm, tn), lambda i, j, k: (i, j))
    (out,), got = _call(body, name, (a, b, *extras), [o_shape], grid=(M // tm, N // tn, nk),
                        in_specs=[a_spec, b_spec] + [e_spec] * ne, out_specs=[o_spec],
                        scratch_shapes=[pltpu.VMEM((tm, tn), F32)], exchange=exchange)
    return out if exchange is None else (out, got)


def _modulate(x, sc, sh):
    return x * (1.0 + sc) + sh


def _square(x):
    return x * x


def _mod_part(c_all, w_ada_s, b_s):
    def body(c_ref, w_ref, b_ref, mod_ref, cond_ref):
        cv = c_ref[...]
        cond = cv * _sigmoid(cv)
        cond_ref[...] = cond
        mod_ref[...] = _bdot(cond, w_ref[...]) + b_ref[...]

    return pl.pallas_call(
        body, name="mod_part",
        out_shape=[jax.ShapeDtypeStruct((N_DEV, w_ada_s.shape[1]), F32), jax.ShapeDtypeStruct(c_all.shape, F32)],
        compiler_params=_params(),
    )(c_all, w_ada_s, b_s)


def _rms_fwd(x, w):
    rs = lax.rsqrt(_lanemean(x * x) + RMS_EPS)
    return x * rs * w, rs


def _rms_bwd(x, rs, w, dy):
    xhat = x * rs
    dxh = dy * w
    return rs * (dxh - xhat * _lanemean(dxh * xhat)), dy * xhat


def _mla_pre(z, pos_col, invf, m_one, m_rot, wq_ext, wkv_ext, qnw, kvnw):
    T = z.shape[0]
    tm = min(ROW_TILE, T)

    def body(z_ref, pos_ref, invf_ref, mone_ref, mrot_ref, wq_ref, wkv_ref, qnw_ref, kvnw_ref,
             q_ref, k_ref, v_ref, c1_ref, s1_ref, cqn_ref, ckvn_ref):
        ang = pos_ref[...].astype(F32) * invf_ref[...]
        c1 = mone_ref[...] + mrot_ref[...] * jnp.cos(ang)
        s1 = mrot_ref[...] * jnp.sin(ang)
        c1_ref[...] = c1
        s1_ref[...] = s1
        cqn, _ = _rms_fwd(z_ref[:, 0:256], qnw_ref[...])
        ckvn, _ = _rms_fwd(z_ref[:, 256:512], kvnw_ref[...])
        cqn_ref[...] = cqn.astype(BF16)
        ckvn_ref[...] = ckvn.astype(BF16)
        qe = _bdot(cqn, wq_ref[...])
        kve = _bdot(ckvn, wkv_ref[...])
        k_rope = z_ref[:, 512:768] * c1 + z_ref[:, 768:1024] * s1
        for h in range(HEADS):
            q_ref[h] = (qe[:, 256 * h:256 * h + 256] * c1 + qe[:, 1024 + 256 * h:1280 + 256 * h] * s1).astype(BF16)
            k_ref[h] = (kve[:, 256 * h:256 * h + 256] + k_rope).astype(BF16)
            v_ref[h] = kve[:, 1024 + 128 * h:1152 + 128 * h].astype(BF16)

    row = lambda i: (i, 0)
    head = lambda i: (0, i, 0)
    return pl.pallas_call(
        body, name="mla_pre", grid=(T // tm,),
        in_specs=[pl.BlockSpec((tm, 1024), lambda i: (i, 2)), pl.BlockSpec((tm, 1), row),
                  _full((1, 256)), _full((1, 256)), _full((1, 256)), _full(wq_ext.shape), _full(wkv_ext.shape),
                  _full((1, 256)), _full((1, 256))],
        out_specs=[pl.BlockSpec((HEADS, tm, QK_PAD), head), pl.BlockSpec((HEADS, tm, QK_PAD), head),
                   pl.BlockSpec((HEADS, tm, HEAD_DIM), head), pl.BlockSpec((tm, 256), row), pl.BlockSpec((tm, 256), row),
                   pl.BlockSpec((tm, 256), row), pl.BlockSpec((tm, 256), row)],
        out_shape=[jax.ShapeDtypeStruct((HEADS, T, QK_PAD), BF16), jax.ShapeDtypeStruct((HEADS, T, QK_PAD), BF16),
                   jax.ShapeDtypeStruct((HEADS, T, HEAD_DIM), BF16), jax.ShapeDtypeStruct((T, 256), F32),
                   jax.ShapeDtypeStruct((T, 256), F32), jax.ShapeDtypeStruct((T, 256), BF16),
                   jax.ShapeDtypeStruct((T, 256), BF16)],
        compiler_params=_params(),
    )(z, pos_col, invf, m_one, m_rot, wq_ext, wkv_ext, qnw, kvnw)


def _mla_bwd(dq, dk, dv, z, c1, s1, wq_ext, wkv_ext, qnw, kvnw):
    T = z.shape[0]
    tm = min(ROW_TILE_SMALL, T)

    def body(dq_ref, dk_ref, dv_ref, z_ref, c1_ref, s1_ref, wq_ref, wkv_ref, qnw_ref, kvnw_ref,
             dz_ref, dqe_ref, dkve_ref, dqnw_ref, dkvnw_ref):
        @pl.when(pl.program_id(0) == 0)
        def _():
            dqnw_ref[...] = jnp.zeros_like(dqnw_ref)
            dkvnw_ref[...] = jnp.zeros_like(dkvnw_ref)

        c1, s1 = c1_ref[...], s1_ref[...]
        dkpe = jnp.zeros((tm, QK_PAD), F32)
        for h in range(HEADS):
            dqh, dkh = dq_ref[h], dk_ref[h]
            dqe_ref[:, 256 * h:256 * h + 256] = (dqh * c1).astype(BF16)
            dqe_ref[:, 1024 + 256 * h:1280 + 256 * h] = (dqh * s1).astype(BF16)
            dkve_ref[:, 256 * h:256 * h + 256] = dkh.astype(BF16)
            dkve_ref[:, 1024 + 128 * h:1152 + 128 * h] = dv_ref[h].astype(BF16)
            dkpe = dkpe + dkh
        dcqn = _dot(dqe_ref[...], wq_ref[...], NT)
        dckvn = _dot(dkve_ref[...], wkv_ref[...], NT)
        cq, ckv = z_ref[:, 0:256], z_ref[:, 256:512]
        _, rsq = _rms_fwd(cq, qnw_ref[...])
        _, rskv = _rms_fwd(ckv, kvnw_ref[...])
        dcq, wq_rows = _rms_bwd(cq, rsq, qnw_ref[...], dcqn)
        dckv, wkv_rows = _rms_bwd(ckv, rskv, kvnw_ref[...], dckvn)
        dqnw_ref[...] += _rowsum(wq_rows)
        dkvnw_ref[...] += _rowsum(wkv_rows)
        dz_ref[:, 0:256] = dcq
        dz_ref[:, 256:512] = dckv
        dz_ref[:, 512:768] = dkpe * c1
        dz_ref[:, 768:1024] = dkpe * s1

    row = lambda i: (i, 0)
    head = lambda i: (0, i, 0)
    return pl.pallas_call(
        body, name="mla_bwd", grid=(T // tm,),
        in_specs=[pl.BlockSpec((HEADS, tm, QK_PAD), head), pl.BlockSpec((HEADS, tm, QK_PAD), head),
                  pl.BlockSpec((HEADS, tm, HEAD_DIM), head), pl.BlockSpec((tm, 1024), lambda i: (i, 2)),
                  pl.BlockSpec((tm, 256), row), pl.BlockSpec((tm, 256), row), _full(wq_ext.shape), _full(wkv_ext.shape),
                  _full((1, 256)), _full((1, 256))],
        out_specs=[pl.BlockSpec((tm, 1024), row), pl.BlockSpec((tm, 2048), row), pl.BlockSpec((tm, 1536), row),
                   _full((1, 256)), _full((1, 256))],
        out_shape=[jax.ShapeDtypeStruct((T, 1024), F32), jax.ShapeDtypeStruct((T, 2048), BF16),
                   jax.ShapeDtypeStruct((T, 1536), BF16), jax.ShapeDtypeStruct((1, 256), F32),
                   jax.ShapeDtypeStruct((1, 256), F32)],
        compiler_params=_params(),
    )(dq, dk, dv, z, c1, s1, wq_ext, wkv_ext, qnw, kvnw)


def _lower_bound(lbraw_ref):
    a0, a1 = lbraw_ref[0:1, :], lbraw_ref[1:2, :]
    mx = jnp.maximum(a0, a1)
    e0, e1 = jnp.exp(a0 - mx), jnp.exp(a1 - mx)
    return e0 / (e0 + e1)


def _tri(lower):
    r = lax.broadcasted_iota(jnp.int32, (CHUNK, CHUNK), 0)
    c = lax.broadcasted_iota(jnp.int32, (CHUNK, CHUNK), 1)
    return (r >= c) if lower else (r <= c)


def _hgrn_gates(q, f, lb, tri_lo):
    sg = _sigmoid(f)
    forget = lb + (1.0 - lb) * sg
    k = 1.0 - forget
    b = _hdot(tri_lo.astype(F32), jnp.log(forget))
    b_ref, b_last = b[CHUNK // 2 - 1:CHUNK // 2, :], b[CHUNK - 1:CHUNK, :]
    e1, e2, e3, e4 = jnp.exp(b - b_ref), jnp.exp(b_ref - b), jnp.exp(b_last - b), jnp.exp(b)
    return dict(sg=sg, forget=forget, k=k, e1=e1, e2=e2, e3=e3, e4=e4, qa=q * e1, ka=k * e2, kl=k * e3, qb=q * e4,
                decay=jnp.exp(b_last))


def _hgrn_fwd(z, lbraw, nw, exchange=None):
    T = z.shape[0]
    G = min(HGRN_GROUP, T // CHUNK)
    rows = G * CHUNK
    n_chunks = T // CHUNK

    def body(q_ref, f_ref, i_ref, g_ref, lbraw_ref, nw_ref, oraw_ref, og_ref, sp_ref, st_ref):
        @pl.when(pl.program_id(0) == 0)
        def _():
            st_ref[...] = jnp.zeros_like(st_ref)

        lb_all = _lower_bound(lbraw_ref)
        tri_lo = _tri(True)

        def chunk(cc, carry):
            rs = pl.ds(pl.multiple_of(cc * CHUNK, CHUNK), CHUNK)
            for h in range(HEADS):
                ln = slice(HEAD_DIM * h, HEAD_DIM * (h + 1))
                v = i_ref[rs, ln]
                t = _hgrn_gates(q_ref[rs, ln], f_ref[rs, ln], lb_all[:, ln], tri_lo)
                st = st_ref[h]
                sp_ref[cc, h] = st
                a = jnp.where(tri_lo, _bdot(t["qa"], t["ka"], NT), 0.0)
                o = _bdot(a, v) + _bdot(t["qb"], st, NT)
                st_ref[h] = st * t["decay"] + _bdot(v, t["kl"], TN)
                oraw_ref[rs, ln] = o
                on, _ = _rms_fwd(o, nw_ref[:, ln])
                gate = g_ref[rs, ln]
                og_ref[rs, ln] = (on * (gate * _sigmoid(gate))).astype(BF16)
            return carry

        lax.fori_loop(0, G, chunk, 0)

    col = lambda j: pl.BlockSpec((rows, 512), lambda r, j=j: (r, j))
    return _call(
        body, "hgrn_fwd", (z, z, z, z, lbraw, nw), grid=(T // rows,),
        in_specs=[col(0), col(1), col(2), col(3), _full((2, 512)), _full((1, 512))],
        out_specs=[col(0), col(0), pl.BlockSpec((G, HEADS, HEAD_DIM, HEAD_DIM), lambda r: (r, 0, 0, 0))],
        out_shape=[jax.ShapeDtypeStruct((T, 512), F32), jax.ShapeDtypeStruct((T, 512), BF16),
                   jax.ShapeDtypeStruct((n_chunks, HEADS, HEAD_DIM, HEAD_DIM), F32)],
        scratch_shapes=[pltpu.VMEM((HEADS, HEAD_DIM, HEAD_DIM), F32)], exchange=exchange)


def _hgrn_bwd(dmixcat, z, oraw, sprev, lbraw, nw, exchange=None):
    T = z.shape[0]
    G = min(HGRN_GROUP, T // CHUNK)
    rows = G * CHUNK
    ng = T // rows

    def body(dog_ref, q_ref, f_ref, i_ref, g_ref, oraw_ref, sp_ref, lbraw_ref, nw_ref,
             dz_ref, dlb_ref, dnw_ref, dst_ref):
        @pl.when(pl.program_id(0) == 0)
        def _():
            dst_ref[...] = jnp.zeros_like(dst_ref)
            dlb_ref[...] = jnp.zeros_like(dlb_ref)
            dnw_ref[...] = jnp.zeros_like(dnw_ref)

        lb_all = _lower_bound(lbraw_ref)
        tri_lo, tri_up = _tri(True), _tri(False)
        rowid = lax.broadcasted_iota(jnp.int32, (CHUNK, HEAD_DIM), 0)

        def chunk(it, carry):
            cc = G - 1 - it
            rs = pl.ds(pl.multiple_of(cc * CHUNK, CHUNK), CHUNK)
            for h in range(HEADS):
                ln = slice(HEAD_DIM * h, HEAD_DIM * (h + 1))
                q, v, lb = q_ref[rs, ln], i_ref[rs, ln], lb_all[:, ln]
                t = _hgrn_gates(q, f_ref[rs, ln], lb, tri_lo)
                o, gate, nwh, dog = oraw_ref[rs, ln], g_ref[rs, ln], nw_ref[:, ln], dog_ref[rs, ln]
                _, rs_o = _rms_fwd(o, nwh)
                sgg = _sigmoid(gate)
                d_on = dog * (gate * sgg)
                dz_ref[rs, 1536 + HEAD_DIM * h:1536 + HEAD_DIM * (h + 1)] = (
                    dog * (o * rs_o * nwh) * (sgg * (1.0 + gate * (1.0 - sgg))))
                do, nw_rows = _rms_bwd(o, rs_o, nwh, d_on)
                dnw_ref[:, ln] += _rowsum(nw_rows)
                st, dst = sp_ref[cc, h], dst_ref[h]
                a = jnp.where(tri_lo, _bdot(t["qa"], t["ka"], NT), 0.0)
                dqb = _bdot(do, st)
                dkl = _bdot(v, dst)
                dv_ = _bdot(t["kl"], dst, NT) + _bdot(a, do, TN)
                ddecay = _rowsum(dst * st)
                da = jnp.where(tri_lo, _bdot(do, v, NT), 0.0)
                dqa = _bdot(da, t["ka"])
                dka = _bdot(da, t["qa"], TN)
                dst_ref[h] = dst * t["decay"] + _bdot(do, t["qb"], TN)
                pa, pk, pb, pl_ = dqa * t["qa"], dka * t["ka"], dqb * t["qb"], dkl * t["kl"]
                db = pa - pk + pb - pl_
                db = db + jnp.where(rowid == CHUNK // 2 - 1, _rowsum(pk - pa), 0.0)
                db = db + jnp.where(rowid == CHUNK - 1, _rowsum(pl_) + ddecay * t["decay"], 0.0)
                dlogf = _hdot(tri_up.astype(F32), db)
                dk_ = dka * t["e2"] + dkl * t["e3"]
                dforget = dlogf / t["forget"] - dk_
                sg = t["sg"]
                dz_ref[rs, ln] = dqa * t["e1"] + dqb * t["e4"]
                dz_ref[rs, 512 + HEAD_DIM * h:512 + HEAD_DIM * (h + 1)] = dforget * (1.0 - lb) * sg * (1.0 - sg)
                dz_ref[rs, 1024 + HEAD_DIM * h:1024 + HEAD_DIM * (h + 1)] = dv_
                dlb_ref[:, ln] += _rowsum(dforget * (1.0 - sg))
            return carry

        lax.fori_loop(0, G, chunk, 0)

    col = lambda j: pl.BlockSpec((rows, 512), lambda r, j=j: (ng - 1 - r, j))
    return _call(
        body, "hgrn_bwd", (dmixcat, z, z, z, z, oraw, sprev, lbraw, nw), grid=(ng,),
        in_specs=[col(0), col(0), col(1), col(2), col(3), col(0),
                  pl.BlockSpec((G, HEADS, HEAD_DIM, HEAD_DIM), lambda r: (ng - 1 - r, 0, 0, 0)),
                  _full((2, 512)), _full((1, 512))],
        out_specs=[pl.BlockSpec((rows, 2048), lambda r: (ng - 1 - r, 0)), _full((1, 512)), _full((1, 512))],
        out_shape=[jax.ShapeDtypeStruct((T, 2048), F32), jax.ShapeDtypeStruct((1, 512), F32),
                   jax.ShapeDtypeStruct((1, 512), F32)],
        scratch_shapes=[pltpu.VMEM((HEADS, HEAD_DIM, HEAD_DIM), F32)], exchange=exchange)


def _diag_mask(t):
    r = lax.broadcasted_iota(jnp.int32, (t, t), 0)
    c = lax.broadcasted_iota(jnp.int32, (t, t), 1)
    return r >= c


def _attn_fwd(q, k, v, exchange=None):
    _, T, _ = q.shape
    t = min(ATT_TILE, T)

    def body(q_ref, k_ref, v_ref, o_ref, lse_ref):
        i = pl.program_id(1)
        qb = q_ref[...]

        def step(j, carry, masked):
            m, l, acc = carry
            ks = pl.ds(pl.multiple_of(j * t, t), t)
            s = _dot(qb, k_ref[ks, :], NT) * ATT_SCALE
            if masked:
                s = jnp.where(_diag_mask(t), s, NEG_BIG)
            mn = jnp.maximum(m, jnp.max(s, axis=-1, keepdims=True))
            p = jnp.exp(s - mn)
            al = jnp.exp(m - mn)
            return mn, al * l + jnp.sum(p, axis=-1, keepdims=True), al * acc + _dot(p.astype(BF16), v_ref[ks, :])

        init = (jnp.full((t, 1), NEG_BIG, F32), jnp.zeros((t, 1), F32), jnp.zeros((t, HEAD_DIM), F32))
        carry = lax.fori_loop(0, i, lambda j, c: step(j, c, False), init)
        m, l, acc = step(i, carry, True)
        o_ref[...] = acc / l
        lse_ref[...] = jnp.broadcast_to(m + jnp.log(l), (t, HEAD_DIM))

    return _call(
        body, "attn_fwd", (q, k, v), grid=(HEADS, T // t),
        in_specs=[pl.BlockSpec((None, t, QK_PAD), lambda h, i: (h, i, 0)),
                  pl.BlockSpec((None, T, QK_PAD), lambda h, i: (h, 0, 0)),
                  pl.BlockSpec((None, T, HEAD_DIM), lambda h, i: (h, 0, 0))],
        out_specs=[pl.BlockSpec((t, HEAD_DIM), lambda h, i: (i, h)),
                   pl.BlockSpec((None, t, HEAD_DIM), lambda h, i: (h, i, 0))],
        out_shape=[jax.ShapeDtypeStruct((T, HEADS * HEAD_DIM), F32), jax.ShapeDtypeStruct((HEADS, T, HEAD_DIM), F32)],
        exchange=exchange)


def _attn_bwd(q, k, v, dmixcat, o, lse, exchange=None):
    _, T, _ = q.shape
    t = min(ATT_TILE, T)
    nq = T // t

    def body(q_ref, k_ref, v_ref, do_ref, o_ref, lse_ref, dq_ref, dk_ref, dv_ref, delta_ref):
        j = pl.program_id(1)

        @pl.when(j == 0)
        def _():
            dq_ref[...] = jnp.zeros_like(dq_ref)

            def fill(i, carry):
                rs = pl.ds(pl.multiple_of(i * t, t), t)
                delta_ref[rs, :] = jnp.broadcast_to(
                    jnp.sum(do_ref[rs, :] * o_ref[rs, :], axis=-1, keepdims=True), (t, HEAD_DIM))
                return carry

            lax.fori_loop(0, nq, fill, 0)

        kb, vb = k_ref[...], v_ref[...]

        def step(i, carry, masked):
            dk, dv = carry
            rs = pl.ds(pl.multiple_of(i * t, t), t)
            qb, dob = q_ref[rs, :], do_ref[rs, :].astype(BF16)
            s = _dot(qb, kb, NT) * ATT_SCALE
            p = jnp.exp(s - lse_ref[rs, 0:1])
            if masked:
                p = jnp.where(_diag_mask(t), p, 0.0)
            dp = _dot(dob, vb, NT)
            ds = (p * (dp - delta_ref[rs, 0:1]) * ATT_SCALE).astype(BF16)
            dq_ref[rs, :] += _dot(ds, kb)
            return dk + _dot(ds, qb, TN), dv + _dot(p.astype(BF16), dob, TN)

        carry = step(j, (jnp.zeros((t, QK_PAD), F32), jnp.zeros((t, HEAD_DIM), F32)), True)
        dk, dv = lax.fori_loop(j + 1, nq, lambda i, c: step(i, c, False), carry)
        dk_ref[...] = dk
        dv_ref[...] = dv

    return _call(
        body, "attn_bwd", (q, k, v, dmixcat, o, lse), grid=(HEADS, nq),
        in_specs=[pl.BlockSpec((None, T, QK_PAD), lambda h, j: (h, 0, 0)),
                  pl.BlockSpec((None, t, QK_PAD), lambda h, j: (h, j, 0)),
                  pl.BlockSpec((None, t, HEAD_DIM), lambda h, j: (h, j, 0)),
                  pl.BlockSpec((T, HEAD_DIM), lambda h, j: (0, HEADS + h)),
                  pl.BlockSpec((T, HEAD_DIM), lambda h, j: (0, h)),
                  pl.BlockSpec((None, T, HEAD_DIM), lambda h, j: (h, 0, 0))],
        out_specs=[pl.BlockSpec((None, T, QK_PAD), lambda h, j: (h, 0, 0)),
                   pl.BlockSpec((None, t, QK_PAD), lambda h, j: (h, j, 0)),
                   pl.BlockSpec((None, t, HEAD_DIM), lambda h, j: (h, j, 0))],
        out_shape=[jax.ShapeDtypeStruct((HEADS, T, QK_PAD), F32), jax.ShapeDtypeStruct((HEADS, T, QK_PAD), F32),
                   jax.ShapeDtypeStruct((HEADS, T, HEAD_DIM), F32)],
        scratch_shapes=[pltpu.VMEM((T, HEAD_DIM), F32)], exchange=exchange)


def _ln_fwd(r):
    mu = _lanemean(r)
    xc = r - mu
    rstd = lax.rsqrt(_lanemean(xc * xc) + LN_EPS)
    return xc * rstd, rstd


def _ln_bwd(dxh, xhat, rstd):
    return rstd * (dxh - _lanemean(dxh) - xhat * _lanemean(dxh * xhat))


def _mix_ln1(mixcat, w_out, x, g_a, ln1_g, ln1_b, sc_m, sh_m):
    T = x.shape[0]
    tm = min(ROW_TILE_SMALL, T)

    def body(mc_ref, w_ref, x_ref, ga_ref, g_ref, b_ref, sc_ref, sh_ref, mix_ref, xhat_ref, rstd_ref, u2_ref):
        mix = _dot(mc_ref[...], w_ref[...])
        mix_ref[...] = mix
        xhat, rstd = _ln_fwd(ALPHA * x_ref[...] + (1.0 + ga_ref[...]) * mix)
        xhat_ref[...] = xhat
        rstd_ref[...] = jnp.broadcast_to(rstd, (tm, 128))
        u2_ref[...] = _modulate(xhat * g_ref[...] + b_ref[...], sc_ref[...], sh_ref[...]).astype(BF16)

    row = pl.BlockSpec((tm, D_MODEL), lambda i: (i, 0))
    vec = _full((1, D_MODEL))
    return pl.pallas_call(
        body, name="mix_ln1", grid=(T // tm,),
        in_specs=[row, _full(w_out.shape), row, vec, vec, vec, vec, vec],
        out_specs=[row, row, pl.BlockSpec((tm, 128), lambda i: (i, 0)), row],
        out_shape=[jax.ShapeDtypeStruct((T, D_MODEL), F32), jax.ShapeDtypeStruct((T, D_MODEL), F32),
                   jax.ShapeDtypeStruct((T, 128), F32), jax.ShapeDtypeStruct((T, D_MODEL), BF16)],
        compiler_params=_params(),
    )(mixcat, w_out, x, g_a, ln1_g, ln1_b, sc_m, sh_m)


def _mlp_fwd(u2, w1, w2, xhat1, ln1_g, ln1_b, g_m, ln2_g, ln2_b, target):
    T = u2.shape[0]
    nf, _, tf = w1.shape
    tm = min(ROW_TILE, T)

    def body(u2_ref, w1_ref, w2_ref, xhat_ref, g1_ref, b1_ref, gm_ref, g2_ref, b2_ref, tgt_ref,
             r_ref, dr2_ref, dh_ref, dg2_ref, db2_ref, dgm_ref, loss_ref, acc_ref):
        i, f = pl.program_id(0), pl.program_id(1)

        @pl.when((i == 0) & (f == 0))
        def _():
            for ref in (dg2_ref, db2_ref, dgm_ref, loss_ref):
                ref[...] = jnp.zeros_like(ref)

        @pl.when(f == 0)
        def _():
            acc_ref[...] = jnp.zeros_like(acc_ref)

        r = jnp.maximum(_dot(u2_ref[...], w1_ref[...]), 0.0)
        r_ref[...] = r.astype(BF16)
        acc_ref[...] += _bdot(r * r, w2_ref[...])

        @pl.when(f == nf - 1)
        def _():
            h = acc_ref[...]
            x1 = xhat_ref[...] * g1_ref[...] + b1_ref[...]
            xhat2, rstd2 = _ln_fwd(ALPHA * x1 + (1.0 + gm_ref[...]) * h)
            err = xhat2 * g2_ref[...] + b2_ref[...] - tgt_ref[...]
            loss_ref[...] += jnp.sum(0.5 * _lanemean(err * err), axis=0, keepdims=True)
            dy = err * (1.0 / D_MODEL)
            dg2_ref[...] += _rowsum(dy * xhat2)
            db2_ref[...] += _rowsum(dy)
            dr2 = _ln_bwd(dy * g2_ref[...], xhat2, rstd2)
            dr2_ref[...] = dr2
            dgm_ref[...] += _rowsum(dr2 * h)
            dh_ref[...] = ((1.0 + gm_ref[...]) * dr2).astype(BF16)

    row = pl.BlockSpec((tm, D_MODEL), lambda i, f: (i, 0))
    vec = _full((1, D_MODEL))
    return pl.pallas_call(
        body, name="mlp_fwd", grid=(T // tm, nf),
        in_specs=[row, pl.BlockSpec((None, D_MODEL, tf), lambda i, f: (f, 0, 0)),
                  pl.BlockSpec((None, tf, D_MODEL), lambda i, f: (f, 0, 0)), row, vec, vec, vec, vec, vec, row],
        out_specs=[pl.BlockSpec((tm, tf), lambda i, f: (i, f)), row, row, vec, vec, vec, _full((1, 128))],
        out_shape=[jax.ShapeDtypeStruct((T, nf * tf), BF16), jax.ShapeDtypeStruct((T, D_MODEL), F32),
                   jax.ShapeDtypeStruct((T, D_MODEL), BF16), jax.ShapeDtypeStruct((1, D_MODEL), F32),
                   jax.ShapeDtypeStruct((1, D_MODEL), F32), jax.ShapeDtypeStruct((1, D_MODEL), F32),
                   jax.ShapeDtypeStruct((1, 128), F32)],
        scratch_shapes=[pltpu.VMEM((tm, D_MODEL), F32)],
        compiler_params=_params(),
    )(u2, w1, w2, xhat1, ln1_g, ln1_b, g_m, ln2_g, ln2_b, target)


def _mlp_bwd(dh, w1, w2, r, dr2, xhat1, rstd1, mix, ln1_g, ln1_b, sc_m, g_a):
    T = dh.shape[0]
    nf, _, tf = w1.shape
    tm = min(ROW_TILE, T)

    def body(dh_ref, w1_ref, w2_ref, r_ref, dr2_ref, xhat_ref, rstd_ref, mix_ref, g1_ref, b1_ref, sc_ref, ga_ref,
             dhpre_ref, dr1_ref, dmix_ref, dsc_ref, dsh_ref, dg1_ref, db1_ref, dga_ref, acc_ref):
        i, f = pl.program_id(0), pl.program_id(1)

        @pl.when((i == 0) & (f == 0))
        def _():
            for ref in (dsc_ref, dsh_ref, dg1_ref, db1_ref, dga_ref):
                ref[...] = jnp.zeros_like(ref)

        @pl.when(f == 0)
        def _():
            acc_ref[...] = jnp.zeros_like(acc_ref)

        dhpre = (_dot(dh_ref[...], w2_ref[...], NT) * (2.0 * r_ref[...].astype(F32))).astype(BF16)
        dhpre_ref[...] = dhpre
        acc_ref[...] += _dot(dhpre, w1_ref[...], NT)

        @pl.when(f == nf - 1)
        def _():
            du2 = acc_ref[...]
            xhat = xhat_ref[...]
            x1 = xhat * g1_ref[...] + b1_ref[...]
            dx1 = ALPHA * dr2_ref[...] + du2 * (1.0 + sc_ref[...])
            dsc_ref[...] += _rowsum(du2 * x1)
            dsh_ref[...] += _rowsum(du2)
            dg1_ref[...] += _rowsum(dx1 * xhat)
            db1_ref[...] += _rowsum(dx1)
            dr1 = _ln_bwd(dx1 * g1_ref[...], xhat, rstd_ref[:, 0:1])
            dr1_ref[...] = dr1
            dga_ref[...] += _rowsum(dr1 * mix_ref[...])
            dmix_ref[...] = ((1.0 + ga_ref[...]) * dr1).astype(BF16)

    row = pl.BlockSpec((tm, D_MODEL), lambda i, f: (i, 0))
    vec = _full((1, D_MODEL))
    return pl.pallas_call(
        body, name="mlp_bwd", grid=(T // tm, nf),
        in_specs=[row, pl.BlockSpec((None, D_MODEL, tf), lambda i, f: (f, 0, 0)),
                  pl.BlockSpec((None, tf, D_MODEL), lambda i, f: (f, 0, 0)),
                  pl.BlockSpec((tm, tf), lambda i, f: (i, f)), row, row, pl.BlockSpec((tm, 128), lambda i, f: (i, 0)),
                  row, vec, vec, vec, vec],
        out_specs=[pl.BlockSpec((tm, tf), lambda i, f: (i, f)), row, row, vec, vec, vec, vec, vec],
        out_shape=[jax.ShapeDtypeStruct((T, nf * tf), BF16), jax.ShapeDtypeStruct((T, D_MODEL), F32),
                   jax.ShapeDtypeStruct((T, D_MODEL), BF16)] + [jax.ShapeDtypeStruct((1, D_MODEL), F32)] * 5,
        scratch_shapes=[pltpu.VMEM((tm, D_MODEL), F32)],
        compiler_params=_params(),
    )(dh, w1, w2, r, dr2, xhat1, rstd1, mix, ln1_g, ln1_b, sc_m, g_a)


def _input_bwd(dz_h, dz_m, w_in_ext, x, dr1, sc_a, exchange=None):
    T = x.shape[0]
    tm = min(ROW_TILE_SMALL, T)

    def body(dzh_ref, dzm_ref, w_ref, x_ref, dr1_ref, sc_ref, gx_ref, dsc_ref, dsh_ref):
        @pl.when(pl.program_id(0) == 0)
        def _():
            dsc_ref[...] = jnp.zeros_like(dsc_ref)
            dsh_ref[...] = jnp.zeros_like(dsh_ref)

        du = _bdot(dzh_ref[...], w_ref[:, 0:2048], NT) + _bdot(dzm_ref[...], w_ref[:, 2048:3072], NT)
        gx_ref[...] = ALPHA * dr1_ref[...] + du * (1.0 + sc_ref[...])
        dsc_ref[...] += _rowsum(du * x_ref[...])
        dsh_ref[...] += _rowsum(du)

    row = pl.BlockSpec((tm, D_MODEL), lambda i: (i, 0))
    vec = _full((1, D_MODEL))
    return _call(
        body, "input_bwd", (dz_h, dz_m, w_in_ext, x, dr1, sc_a), grid=(T // tm,),
        in_specs=[pl.BlockSpec((tm, 2048), lambda i: (i, 0)), row, _full(w_in_ext.shape), row, row, vec],
        out_specs=[row, vec, vec],
        out_shape=[jax.ShapeDtypeStruct((T, D_MODEL), F32), jax.ShapeDtypeStruct((1, D_MODEL), F32),
                   jax.ShapeDtypeStruct((1, D_MODEL), F32)], exchange=exchange)


def _adam_math(w, g, m, v):
    m = ADAM_B1 * m + (1.0 - ADAM_B1) * g
    v = ADAM_B2 * v + (1.0 - ADAM_B2) * (g * g)
    m_hat = m / (1.0 - ADAM_B1 ** ADAM_STEP)
    v_hat = v / (1.0 - ADAM_B2 ** ADAM_STEP)
    return -ADAM_LR * (m_hat / (jnp.sqrt(v_hat) + ADAM_EPS) + ADAM_WD * w), m, v


def _adam(g_slabs, w, m, v, name, g_fn=None, g_extra=()):
    R, C = w.shape
    tr = R if R <= 256 else 256
    assert R % tr == 0
    ns = 0 if g_slabs is None else g_slabs.shape[0]
    ne = len(g_extra)

    def body(*refs):
        e_refs = refs[:ne]
        refs = refs[ne:]
        if ns:
            gs_ref, refs = refs[0], refs[1:]
        w_ref, m_ref, v_ref, g_ref, d_ref, nm_ref, nv_ref = refs
        if g_fn is not None:
            g = g_fn(*e_refs)
        else:
            g = gs_ref[0].astype(F32)
            for s in range(1, ns):
                g = g + gs_ref[s].astype(F32)
        d, nm, nv = _adam_math(w_ref[...], g, m_ref[...], v_ref[...])
        g_ref[...] = g
        d_ref[...] = d
        nm_ref[...] = nm
        nv_ref[...] = nv

    blk = pl.BlockSpec((tr, C), lambda i: (i, 0))
    in_specs = [pl.BlockSpec((tr, e.shape[1]), lambda i: (i, 0)) if e.shape[0] == R else _full(e.shape) for e in g_extra]
    args = list(g_extra)
    if ns:
        in_specs.append(pl.BlockSpec((ns, tr, C), lambda i: (0, i, 0)))
        args.append(g_slabs)
    return pl.pallas_call(
        body, name=name, grid=(R // tr,), in_specs=in_specs + [blk] * 3, out_specs=[blk] * 4,
        out_shape=[jax.ShapeDtypeStruct((R, C), F32)] * 4, compiler_params=_params(),
    )(*args, w, m, v)


def _small_reduce(small_all, lbraw):
    W = small_all.shape[-1]

    def body(s_ref, lbraw_ref, sum_ref, glb_ref):
        tot = s_ref[0]
        for i in range(1, N_DEV):
            tot = tot + s_ref[i]
        sum_ref[...] = tot
        lb = _lower_bound(lbraw_ref)
        g0 = tot[:, 6144:6656] * lb * (1.0 - lb)
        glb_ref[0:1, :] = g0
        glb_ref[1:2, :] = -g0

    return pl.pallas_call(
        body, name="small_reduce",
        out_shape=[jax.ShapeDtypeStruct((1, W), F32), jax.ShapeDtypeStruct((2, 512), F32)],
        compiler_params=_params(),
    )(small_all, lbraw)


def _rot_half_cols(w):
    return jnp.concatenate([-w[..., 32:], w[..., :32]], axis=-1)


def _unrot_half_cols(dw_rot):
    return jnp.concatenate([dw_rot[..., 32:], -dw_rot[..., :32]], axis=-1)


def _cols_from_slabs(g):
    s, r, c = g.shape
    return jnp.transpose(g, (1, 0, 2)).reshape(r, s * c)


def _slabs_from_cols(w):
    r, c = w.shape
    return jnp.transpose(w.reshape(r, N_DEV, c // N_DEV), (1, 0, 2))


def _ext_in(w_in):
    k_in = w_in.shape[0]
    z64, z128 = jnp.zeros((k_in, 64), BF16), jnp.zeros((k_in, 128), BF16)
    wk = w_in[:, 2560:2624]
    return jnp.concatenate([w_in[:, :2560], z128, wk, z64, z128, _rot_half_cols(wk), z64], axis=1)


def _ext_q(w_q_up):
    r = w_q_up.shape[0]
    z64, z128 = jnp.zeros((r, 64), BF16), jnp.zeros((r, 128), BF16)
    wq = w_q_up.reshape(r, HEADS, HEAD_DIM + ROPE_DIM)
    main = [jnp.concatenate([wq[:, h, :HEAD_DIM], wq[:, h, HEAD_DIM:], z64], axis=1) for h in range(HEADS)]
    rot = [jnp.concatenate([z128, _rot_half_cols(wq[:, h, HEAD_DIM:]), z64], axis=1) for h in range(HEADS)]
    return jnp.concatenate(main + rot, axis=1)


def _ext_kv(w_kv_up):
    r = w_kv_up.shape[0]
    z128 = jnp.zeros((r, 128), BF16)
    wkv = w_kv_up.reshape(r, HEADS, 2 * HEAD_DIM)
    kpad = [jnp.concatenate([wkv[:, h, :HEAD_DIM], z128], axis=1) for h in range(HEADS)]
    vals = [wkv[:, h, HEAD_DIM:] for h in range(HEADS)]
    return jnp.concatenate(kpad + vals, axis=1)


def _grads_from_ext(dw_in_h, dw_in_m, dwq_ext, dwkv_ext):
    dwk = dw_in_m[:, 512 + 128:512 + 192] + _unrot_half_cols(dw_in_m[:, 768 + 128:768 + 192])
    dw_in = jnp.concatenate([dw_in_h, dw_in_m[:, :512], dwk], axis=1)
    qcols = []
    for h in range(HEADS):
        main, rot = dwq_ext[:, 256 * h:256 * h + 256], dwq_ext[:, 1024 + 256 * h:1280 + 256 * h]
        qcols += [main[:, :128], main[:, 128:192] + _unrot_half_cols(rot[:, 128:192])]
    kvcols = []
    for h in range(HEADS):
        kvcols += [dwkv_ext[:, 256 * h:256 * h + 128], dwkv_ext[:, 1024 + 128 * h:1152 + 128 * h]]
    return dw_in, jnp.concatenate(qcols, axis=1), jnp.concatenate(kvcols, axis=1)


SMALL_W = 6144 + 512 + 512 + 256 + 256 + 4 * 1024 + 128


def kernel(x, c, positions, w_ada, b_ada, w_in, hg_lower_bounds, hg_norm_w, mla_q_norm_w, w_q_up, mla_kv_norm_w, w_kv_up, w_out, ln1_g, ln1_b, w_mlp_in, w_mlp_out, ln2_g, ln2_b, loss_target, m_w_ada, m_b_ada, m_w_in, m_hg_lower_bounds, m_hg_norm_w, m_mla_q_norm_w, m_w_q_up, m_mla_kv_norm_w, m_w_kv_up, m_w_out, m_ln1_g, m_ln1_b, m_w_mlp_in, m_w_mlp_out, m_ln2_g, m_ln2_b, v_w_ada, v_b_ada, v_w_in, v_hg_lower_bounds, v_hg_norm_w, v_mla_q_norm_w, v_w_q_up, v_mla_kv_norm_w, v_w_kv_up, v_w_out, v_ln1_g, v_ln1_b, v_w_mlp_in, v_w_mlp_out, v_ln2_g, v_ln2_b):
    T = x.shape[1]
    me = 4 * lax.axis_index("x") + 2 * lax.axis_index("y") + lax.axis_index("c")
    xs, tgt = x[0], loss_target[0]
    big = dict(w_in=w_in[0], w_q_up=w_q_up[0], w_kv_up=w_kv_up[0], w_out=w_out[0], w_mlp_in=w_mlp_in[0],
               w_mlp_out=w_mlp_out[0])
    names = list(big)

    bf = {n: big[n].astype(BF16) for n in names}
    g_in, g_c = _exchange([bf["w_in"], c], scatter=False, name="gather_w_in")
    c_all = g_c.reshape(N_DEV, D_MODEL)

    ada_cols = w_ada.shape[2]
    mod_part, cond = _mod_part(c_all, w_ada[0], lax.dynamic_slice(b_ada, (0, me * ada_cols), (1, ada_cols)))
    (mod_all,) = _exchange([mod_part], scatter=False, name="gather_mod")
    mod_row = lax.dynamic_slice(mod_all, (0, me, 0), (N_DEV, 1, ada_cols)).reshape(1, N_DEV * ada_cols)
    sh_a, sc_a, g_a, sh_m, sc_m, g_m = [mod_row[:, D_MODEL * i:D_MODEL * (i + 1)] for i in range(6)]

    w_in_ext = _ext_in(_cols_from_slabs(g_in))
    z, (g_q, g_kv, g_out) = _matmul(xs, w_in_ext, "NN", "in_proj", a_fn=_modulate, extras=(sc_a, sh_a),
                                    exchange=_Exchange([bf["w_q_up"], bf["w_kv_up"], bf["w_out"]], False))
    wq_ext, wkv_ext = _ext_q(_cols_from_slabs(g_q)), _ext_kv(_cols_from_slabs(g_kv))
    w_out_full = g_out.reshape(D_MODEL, D_MODEL)
    inv_freq = 1.0 / (ROPE_THETA ** (jnp.arange(0, ROPE_DIM, 2, dtype=F32) / ROPE_DIM))
    zeros = lambda n: jnp.zeros((n,), F32)
    invf = jnp.concatenate([zeros(128), inv_freq, inv_freq, zeros(64)]).reshape(1, QK_PAD)
    m_one = jnp.concatenate([jnp.ones((128,), F32), zeros(128)]).reshape(1, QK_PAD)
    m_rot = jnp.concatenate([zeros(128), jnp.ones((64,), F32), zeros(64)]).reshape(1, QK_PAD)
    q, k, v, c1, s1, cqn, ckvn = _mla_pre(z, positions.reshape(T, 1), invf, m_one, m_rot, wq_ext, wkv_ext,
                                          mla_q_norm_w, mla_kv_norm_w)
    (o_raw, o_gated, s_prev), (w1,) = _hgrn_fwd(z, hg_lower_bounds, hg_norm_w,
                                                exchange=_Exchange([bf["w_mlp_in"]], False))
    (o_mla, lse), (w2,) = _attn_fwd(q, k, v, exchange=_Exchange([bf["w_mlp_out"]], False))
    mixcat = jnp.concatenate([o_gated, o_mla.astype(BF16)], axis=1)
    mix, xhat1, rstd1, u2 = _mix_ln1(mixcat, w_out_full, xs, g_a, ln1_g, ln1_b, sc_m, sh_m)
    r, dr2, dh, dln2_g, dln2_b, dg_m, loss_part = _mlp_fwd(u2, w1, w2, xhat1, ln1_g, ln1_b, g_m, ln2_g, ln2_b, tgt)

    dhpre, dr1, dmix, dsc_m, dsh_m, dln1_g, dln1_b, dg_a = _mlp_bwd(dh, w1, w2, r, dr2, xhat1, rstd1, mix, ln1_g,
                                                                    ln1_b, sc_m, g_a)
    received = {}
    dw2 = _matmul(r, dh, "TN", "wgrad_mlp_out", out_dtype=BF16, a_fn=_square, tk=512)
    dw1 = _matmul(u2, dhpre, "TN", "wgrad_mlp_in", out_dtype=BF16, tk=512, out_slabs=N_DEV)
    dmixcat = _matmul(dmix, w_out_full, "NT", "dgrad_out")
    dw_out = _matmul(mixcat, dmix, "TN", "wgrad_out", out_dtype=BF16, tk=512)
    (dz_h, dlb, dnw), (received["w_mlp_out"],) = _hgrn_bwd(
        dmixcat, z, o_raw, s_prev, hg_lower_bounds, hg_norm_w,
        exchange=_Exchange([dw2.reshape(N_DEV, dw2.shape[0] // N_DEV, D_MODEL)], True))
    (dq, dk, dv), (received["w_mlp_in"], received["w_out"]) = _attn_bwd(
        q, k, v, dmixcat, o_mla, lse,
        exchange=_Exchange([dw1, dw_out.reshape(N_DEV, D_MODEL // N_DEV, D_MODEL)], True))
    dz_m, dq_ext, dkv_ext, dqnw, dkvnw = _mla_bwd(dq, dk, dv, z, c1, s1, wq_ext, wkv_ext, mla_q_norm_w,
                                                   mla_kv_norm_w)
    dwq_ext = _matmul(cqn, dq_ext, "TN", "wgrad_q_up", tk=512)
    dwkv_ext = _matmul(ckvn, dkv_ext, "TN", "wgrad_kv_up", tn=768, tk=512)
    dw_in_h = _matmul(xs, dz_h, "TN", "wgrad_in_h", a_fn=_modulate, extras=(sc_a, sh_a), tk=512)
    dw_in_m = _matmul(xs, dz_m, "TN", "wgrad_in_m", a_fn=_modulate, extras=(sc_a, sh_a), tk=512)
    dw_in, dwq, dwkv = _grads_from_ext(dw_in_h, dw_in_m, dwq_ext, dwkv_ext)
    last = [_slabs_from_cols(g).astype(BF16) for g in (dw_in, dwq, dwkv)]
    (grad_x, dsc_a, dsh_a), (received["w_in"], received["w_q_up"], received["w_kv_up"]) = _input_bwd(
        dz_h, dz_m, w_in_ext, xs, dr1, sc_a, exchange=_Exchange(last, True))

    small = jnp.concatenate([dsh_a, dsc_a, dg_a, dsh_m, dsc_m, dg_m, dlb, dnw, dqnw, dkvnw, dln1_g, dln1_b, dln2_g,
                             dln2_b, loss_part], axis=1)
    (small_all,) = _exchange([small], scatter=False, name="gather_small")
    small_sum, glb = _small_reduce(small_all, hg_lower_bounds)

    moments = dict(w_in=(m_w_in, v_w_in), w_q_up=(m_w_q_up, v_w_q_up), w_kv_up=(m_w_kv_up, v_w_kv_up),
                   w_out=(m_w_out, v_w_out), w_mlp_in=(m_w_mlp_in, v_w_mlp_in), w_mlp_out=(m_w_mlp_out, v_w_mlp_out))
    res = {}
    for n in names:
        res[n] = _adam(received[n], big[n], moments[n][0][0], moments[n][1][0], name="adam_" + n)
    dmod_cols = lax.dynamic_slice(small_all.reshape(N_DEV, SMALL_W), (0, me * ada_cols), (N_DEV, ada_cols))
    cond_t = cond.T

    def ada_grad(ct_ref, dm_ref):
        g = ct_ref[:, 0:1] * dm_ref[0:1, :]
        for b in range(1, N_DEV):
            g = g + ct_ref[:, b:b + 1] * dm_ref[b:b + 1, :]
        return g

    res["w_ada"] = _adam(None, w_ada[0], m_w_ada[0], v_w_ada[0], name="adam_w_ada", g_fn=ada_grad,
                         g_extra=(cond_t, dmod_cols))

    seg = lambda a, b: small_sum[:, a:b]
    small_params = [("b_ada", b_ada, m_b_ada, v_b_ada, seg(0, 6144)),
                    ("hg_lower_bounds", hg_lower_bounds, m_hg_lower_bounds, v_hg_lower_bounds, glb),
                    ("hg_norm_w", hg_norm_w, m_hg_norm_w, v_hg_norm_w, seg(6656, 7168)),
                    ("mla_q_norm_w", mla_q_norm_w, m_mla_q_norm_w, v_mla_q_norm_w, seg(7168, 7424)),
                    ("mla_kv_norm_w", mla_kv_norm_w, m_mla_kv_norm_w, v_mla_kv_norm_w, seg(7424, 7680)),
                    ("ln1_g", ln1_g, m_ln1_g, v_ln1_g, seg(7680, 8704)), ("ln1_b", ln1_b, m_ln1_b, v_ln1_b, seg(8704, 9728)),
                    ("ln2_g", ln2_g, m_ln2_g, v_ln2_g, seg(9728, 10752)), ("ln2_b", ln2_b, m_ln2_b, v_ln2_b, seg(10752, 11776))]
    pack = lambda i: jnp.concatenate([p[i].reshape(1, -1) for p in small_params], axis=1)
    packed = _adam(pack(4)[None], pack(1), pack(2), pack(3), name="adam_small")
    off = 0
    for n, w, _, _, _ in small_params:
        res[n] = tuple(a[:, off:off + w.size].reshape(w.shape) for a in packed)
        off += w.size
    loss = small_sum[0, 11776]

    order = ["w_ada", "b_ada", "w_in", "hg_lower_bounds", "hg_norm_w", "mla_q_norm_w", "w_q_up", "mla_kv_norm_w",
             "w_kv_up", "w_out", "ln1_g", "ln1_b", "w_mlp_in", "w_mlp_out", "ln2_g", "ln2_b"]
    shaped = {n: tuple(a.reshape((1,) + a.shape) if n in big or n == "w_ada" else a for a in res[n]) for n in order}
    outs = [loss, grad_x.reshape(1, T, D_MODEL)]
    for i in range(4):
        outs += [shaped[n][i] for n in order]
    return tuple(outs)
```

```python
import functools

import jax
import jax.numpy as jnp
import numpy as np
from jax import lax
from jax.experimental import pallas as pl
from jax.experimental.pallas import tpu as pltpu

F32, BF16 = jnp.float32, jnp.bfloat16
N_DEV = 8
D_MODEL = 1024
HEADS = 4
HEAD_DIM = 128
ROPE_DIM = 64
QK_PAD = 256
CHUNK = 64
ROPE_THETA = 10000.0
RMS_EPS = 1e-6
LN_EPS = 1e-5
ALPHA = 2.0 ** 0.25
ATT_SCALE = (HEAD_DIM + ROPE_DIM) ** -0.5
ADAM_LR, ADAM_B1, ADAM_B2, ADAM_EPS, ADAM_WD, ADAM_STEP = 0.001, 0.9, 0.999, 1e-08, 0.01, 10
NEG_BIG = -1e30

ROW_TILE = 512
ROW_TILE_SMALL = 256
ATT_TILE = 512
HGRN_GROUP = 8
VMEM_LIMIT = 56 * 2 ** 20

NN = (((1,), (0,)), ((), ()))
NT = (((1,), (1,)), ((), ()))
TN = (((0,), (0,)), ((), ()))


def _dot(a, b, dims=NN):
    return lax.dot_general(a, b, dims, preferred_element_type=F32)


def _bdot(a, b, dims=NN):
    return lax.dot_general(a.astype(BF16), b.astype(BF16), dims, preferred_element_type=F32)


def _hdot(a, b, dims=NN):
    return lax.dot_general(a, b, dims, precision=lax.Precision.HIGHEST, preferred_element_type=F32)


def _params():
    return pltpu.CompilerParams(vmem_limit_bytes=VMEM_LIMIT)


def _sigmoid(x):
    return 1.0 / (1.0 + jnp.exp(-x))


def _rowsum(x):
    return jnp.sum(x, axis=0, keepdims=True)


def _lanemean(x):
    return jnp.mean(x, axis=-1, keepdims=True)


def _full(shape):
    nd = len(shape)
    return pl.BlockSpec(shape, lambda *_: (0,) * nd)


class _Exchange:
    def __init__(self, arrs, scatter):
        self.arrs, self.scatter, self.n = list(arrs), scatter, len(arrs)
        self.out_shape = [jax.ShapeDtypeStruct((N_DEV,) + (a.shape[1:] if scatter else a.shape), a.dtype)
                          for a in self.arrs]
        n = self.n
        self.scratch = [pltpu.SemaphoreType.DMA((n, N_DEV - 1)), pltpu.SemaphoreType.DMA((n, N_DEV - 1)),
                        pltpu.SemaphoreType.DMA((n,))]

    def _copies(self, ins, outs, sems):
        send_sems, recv_sems, loc_sems = sems
        x, y, c = lax.axis_index("x"), lax.axis_index("y"), lax.axis_index("c")
        me = 4 * x + 2 * y + c
        copies = []
        for k in range(self.n):
            src_of = (lambda i, k=k: ins[k].at[i]) if self.scatter else (lambda i, k=k: ins[k])
            copies.append((pltpu.make_async_copy(src_of(me), outs[k].at[me], loc_sems.at[k]), None))
            for p in range(1, N_DEV):
                px = (1 - x) if p & 4 else x
                py = (1 - y) if p & 2 else y
                pc = (1 - c) if p & 1 else c
                peer = 4 * px + 2 * py + pc
                both = dict(send_sem=send_sems.at[k, p - 1], recv_sem=recv_sems.at[k, p - 1],
                            device_id=(px, py, pc), device_id_type=pl.DeviceIdType.MESH)
                send = pltpu.make_async_remote_copy(src_ref=src_of(peer), dst_ref=outs[k].at[me], **both)
                recv = pltpu.make_async_remote_copy(src_ref=src_of(peer), dst_ref=outs[k].at[peer], **both)
                copies.append((send, recv))
        return copies

    def start(self, ins, outs, sems):
        for first, _ in self._copies(ins, outs, sems):
            first.start()

    def wait(self, ins, outs, sems):
        for first, recv in self._copies(ins, outs, sems):
            if recv is None:
                first.wait()
            else:
                recv.wait_recv()
                first.wait_send()


def _call(body, name, args, out_shape, grid=(), in_specs=(), out_specs=(), scratch_shapes=(), exchange=None):
    if exchange is None:
        return pl.pallas_call(body, name=name, grid=grid, in_specs=list(in_specs), out_specs=list(out_specs),
                              out_shape=list(out_shape), scratch_shapes=list(scratch_shapes),
                              compiler_params=_params())(*args), None
    ni, no, ns, nx = len(args), len(out_shape), len(scratch_shapes), exchange.n

    def wrapped(*refs):
        a, xi = refs[:ni], refs[ni:ni + nx]
        o, xo = refs[ni + nx:ni + nx + no], refs[ni + nx + no:ni + 2 * nx + no]
        s, xs = refs[ni + 2 * nx + no:ni + 2 * nx + no + ns], refs[ni + 2 * nx + no + ns:]
        first = last = None
        for d, g in enumerate(grid):
            f, l = pl.program_id(d) == 0, pl.program_id(d) == g - 1
            first, last = (f, l) if first is None else (first & f, last & l)

        @pl.when(first)
        def _():
            exchange.start(xi, xo, xs)

        body(*a, *o, *s)

        @pl.when(last)
        def _():
            exchange.wait(xi, xo, xs)

    hbm = pl.BlockSpec(memory_space=pltpu.HBM)
    res = pl.pallas_call(
        wrapped, name=name, grid=grid, in_specs=list(in_specs) + [hbm] * nx, out_specs=list(out_specs) + [hbm] * nx,
        out_shape=list(out_shape) + exchange.out_shape, scratch_shapes=list(scratch_shapes) + exchange.scratch,
        compiler_params=_params())(*args, *exchange.arrs)
    return res[:no], res[no:]


def _exchange(arrs, scatter, name):
    ex = _Exchange(arrs, scatter)

    def body(*refs):
        ins, outs, sems = refs[:ex.n], refs[ex.n:2 * ex.n], refs[2 * ex.n:]
        ex.start(ins, outs, sems)
        ex.wait(ins, outs, sems)

    hbm = pl.BlockSpec(memory_space=pltpu.HBM)
    return pl.pallas_call(body, name=name, out_shape=ex.out_shape, in_specs=[hbm] * ex.n, out_specs=[hbm] * ex.n,
                          scratch_shapes=ex.scratch)(*ex.arrs)


def _matmul(a, b, mode, name, out_dtype=F32, tm=512, tn=1024, tk=1024, a_fn=None, extras=(), out_slabs=None,
            exchange=None):
    if mode == "NN":
        (M, K), N = a.shape, b.shape[1]
    elif mode == "NT":
        (M, K), N = a.shape, b.shape[0]
    else:
        (K, M), N = a.shape, b.shape[1]
    if out_slabs:
        tn = N // out_slabs
    tm, tn, tk = min(tm, M), min(tn, N), min(tk, K)
    assert M % tm == 0 and N % tn == 0 and K % tk == 0, (name, M, N, K)
    nk = K // tk
    dims = {"NN": NN, "NT": NT, "TN": TN}[mode]
    ne = len(extras)

    def body(a_ref, b_ref, *rest):
        e_refs, o_ref, acc_ref = rest[:ne], rest[ne], rest[ne + 1]
        k = pl.program_id(2)

        @pl.when(k == 0)
        def _():
            acc_ref[...] = jnp.zeros_like(acc_ref)

        at = a_ref[...]
        if a_fn is not None:
            at = a_fn(at.astype(F32), *[e[...] for e in e_refs])
        acc_ref[...] += _bdot(at, b_ref[...], dims)

        @pl.when(k == nk - 1)
        def _():
            o_ref[...] = acc_ref[...].astype(out_dtype)

    if mode == "TN":
        a_spec = pl.BlockSpec((tk, tm), lambda i, j, k: (k, i))
        e_spec = pl.BlockSpec((1, tm), lambda i, j, k: (0, i))
    else:
        a_spec = pl.BlockSpec((tm, tk), lambda i, j, k: (i, k))
        e_spec = pl.BlockSpec((1, tk), lambda i, j, k: (0, k))
    if mode == "NT":
        b_spec = pl.BlockSpec((tn, tk), lambda i, j, k: (j, k))
    else:
        b_spec = pl.BlockSpec((tk, tn), lambda i, j, k: (k, j))
    if out_slabs:
        o_shape = jax.ShapeDtypeStruct((out_slabs, M, tn), out_dtype)
        o_spec = pl.BlockSpec((None, tm, tn), lambda i, j, k: (j, i, 0))
    else:
        o_shape = jax.ShapeDtypeStruct((M, N), out_dtype)
        o_spec = pl.BlockSpec((tm, tn), lambda i, j, k: (i, j))
    (out,), got = _call(body, name, (a, b, *extras), [o_shape], grid=(M // tm, N // tn, nk),
                        in_specs=[a_spec, b_spec] + [e_spec] * ne, out_specs=[o_spec],
                        scratch_shapes=[pltpu.VMEM((tm, tn), F32)], exchange=exchange)
    return out if exchange is None else (out, got)


def _modulate(x, sc, sh):
    return x * (1.0 + sc) + sh


def _square(x):
    return x * x


def _mod_part(c_all, w_ada_s, b_s):
    def body(c_ref, w_ref, b_ref, mod_ref, cond_ref):
        cv = c_ref[...]
        cond = cv * _sigmoid(cv)
        cond_ref[...] = cond
        mod_ref[...] = _bdot(cond, w_ref[...]) + b_ref[...]

    return pl.pallas_call(
        body, name="mod_part",
        out_shape=[jax.ShapeDtypeStruct((N_DEV, w_ada_s.shape[1]), F32), jax.ShapeDtypeStruct(c_all.shape, F32)],
        compiler_params=_params(),
    )(c_all, w_ada_s, b_s)


def _rms_fwd(x, w):
    rs = lax.rsqrt(_lanemean(x * x) + RMS_EPS)
    return x * rs * w, rs


def _rms_bwd(x, rs, w, dy):
    xhat = x * rs
    dxh = dy * w
    return rs * (dxh - xhat * _lanemean(dxh * xhat)), dy * xhat


def _mla_pre(z, pos_col, invf, m_one, m_rot, wq_ext, wkv_ext, qnw, kvnw):
    T = z.shape[0]
    tm = min(ROW_TILE, T)

    def body(z_ref, pos_ref, invf_ref, mone_ref, mrot_ref, wq_ref, wkv_ref, qnw_ref, kvnw_ref,
             q_ref, k_ref, v_ref, c1_ref, s1_ref, cqn_ref, ckvn_ref):
        ang = pos_ref[...].astype(F32) * invf_ref[...]
        c1 = mone_ref[...] + mrot_ref[...] * jnp.cos(ang)
        s1 = mrot_ref[...] * jnp.sin(ang)
        c1_ref[...] = c1
        s1_ref[...] = s1
        cqn, _ = _rms_fwd(z_ref[:, 0:256], qnw_ref[...])
        ckvn, _ = _rms_fwd(z_ref[:, 256:512], kvnw_ref[...])
        cqn_ref[...] = cqn.astype(BF16)
        ckvn_ref[...] = ckvn.astype(BF16)
        qe = _bdot(cqn, wq_ref[...])
        kve = _bdot(ckvn, wkv_ref[...])
        k_rope = z_ref[:, 512:768] * c1 + z_ref[:, 768:1024] * s1
        for h in range(HEADS):
            q_ref[h] = (qe[:, 256 * h:256 * h + 256] * c1 + qe[:, 1024 + 256 * h:1280 + 256 * h] * s1).astype(BF16)
            k_ref[h] = (kve[:, 256 * h:256 * h + 256] + k_rope).astype(BF16)
            v_ref[h] = kve[:, 1024 + 128 * h:1152 + 128 * h].astype(BF16)

    row = lambda i: (i, 0)
    head = lambda i: (0, i, 0)
    return pl.pallas_call(
        body, name="mla_pre", grid=(T // tm,),
        in_specs=[pl.BlockSpec((tm, 1024), lambda i: (i, 2)), pl.BlockSpec((tm, 1), row),
                  _full((1, 256)), _full((1, 256)), _full((1, 256)), _full(wq_ext.shape), _full(wkv_ext.shape),
                  _full((1, 256)), _full((1, 256))],
        out_specs=[pl.BlockSpec((HEADS, tm, QK_PAD), head), pl.BlockSpec((HEADS, tm, QK_PAD), head),
                   pl.BlockSpec((HEADS, tm, HEAD_DIM), head), pl.BlockSpec((tm, 256), row), pl.BlockSpec((tm, 256), row),
                   pl.BlockSpec((tm, 256), row), pl.BlockSpec((tm, 256), row)],
        out_shape=[jax.ShapeDtypeStruct((HEADS, T, QK_PAD), BF16), jax.ShapeDtypeStruct((HEADS, T, QK_PAD), BF16),
                   jax.ShapeDtypeStruct((HEADS, T, HEAD_DIM), BF16), jax.ShapeDtypeStruct((T, 256), F32),
                   jax.ShapeDtypeStruct((T, 256), F32), jax.ShapeDtypeStruct((T, 256), BF16),
                   jax.ShapeDtypeStruct((T, 256), BF16)],
        compiler_params=_params(),
    )(z, pos_col, invf, m_one, m_rot, wq_ext, wkv_ext, qnw, kvnw)


def _mla_bwd(dq, dk, dv, z, c1, s1, wq_ext, wkv_ext, qnw, kvnw):
    T = z.shape[0]
    tm = min(ROW_TILE_SMALL, T)

    def body(dq_ref, dk_ref, dv_ref, z_ref, c1_ref, s1_ref, wq_ref, wkv_ref, qnw_ref, kvnw_ref,
             dz_ref, dqe_ref, dkve_ref, dqnw_ref, dkvnw_ref):
        @pl.when(pl.program_id(0) == 0)
        def _():
            dqnw_ref[...] = jnp.zeros_like(dqnw_ref)
            dkvnw_ref[...] = jnp.zeros_like(dkvnw_ref)

        c1, s1 = c1_ref[...], s1_ref[...]
        dkpe = jnp.zeros((tm, QK_PAD), F32)
        for h in range(HEADS):
            dqh, dkh = dq_ref[h], dk_ref[h]
            dqe_ref[:, 256 * h:256 * h + 256] = (dqh * c1).astype(BF16)
            dqe_ref[:, 1024 + 256 * h:1280 + 256 * h] = (dqh * s1).astype(BF16)
            dkve_ref[:, 256 * h:256 * h + 256] = dkh.astype(BF16)
            dkve_ref[:, 1024 + 128 * h:1152 + 128 * h] = dv_ref[h].astype(BF16)
            dkpe = dkpe + dkh
        dcqn = _dot(dqe_ref[...], wq_ref[...], NT)
        dckvn = _dot(dkve_ref[...], wkv_ref[...], NT)
        cq, ckv = z_ref[:, 0:256], z_ref[:, 256:512]
        _, rsq = _rms_fwd(cq, qnw_ref[...])
        _, rskv = _rms_fwd(ckv, kvnw_ref[...])
        dcq, wq_rows = _rms_bwd(cq, rsq, qnw_ref[...], dcqn)
        dckv, wkv_rows = _rms_bwd(ckv, rskv, kvnw_ref[...], dckvn)
        dqnw_ref[...] += _rowsum(wq_rows)
        dkvnw_ref[...] += _rowsum(wkv_rows)
        dz_ref[:, 0:256] = dcq
        dz_ref[:, 256:512] = dckv
        dz_ref[:, 512:768] = dkpe * c1
        dz_ref[:, 768:1024] = dkpe * s1

    row = lambda i: (i, 0)
    head = lambda i: (0, i, 0)
    return pl.pallas_call(
        body, name="mla_bwd", grid=(T // tm,),
        in_specs=[pl.BlockSpec((HEADS, tm, QK_PAD), head), pl.BlockSpec((HEADS, tm, QK_PAD), head),
                  pl.BlockSpec((HEADS, tm, HEAD_DIM), head), pl.BlockSpec((tm, 1024), lambda i: (i, 2)),
                  pl.BlockSpec((tm, 256), row), pl.BlockSpec((tm, 256), row), _full(wq_ext.shape), _full(wkv_ext.shape),
                  _full((1, 256)), _full((1, 256))],
        out_specs=[pl.BlockSpec((tm, 1024), row), pl.BlockSpec((tm, 2048), row), pl.BlockSpec((tm, 1536), row),
                   _full((1, 256)), _full((1, 256))],
        out_shape=[jax.ShapeDtypeStruct((T, 1024), F32), jax.ShapeDtypeStruct((T, 2048), BF16),
                   jax.ShapeDtypeStruct((T, 1536), BF16), jax.ShapeDtypeStruct((1, 256), F32),
                   jax.ShapeDtypeStruct((1, 256), F32)],
        compiler_params=_params(),
    )(dq, dk, dv, z, c1, s1, wq_ext, wkv_ext, qnw, kvnw)


def _lower_bound(lbraw_ref):
    a0, a1 = lbraw_ref[0:1, :], lbraw_ref[1:2, :]
    mx = jnp.maximum(a0, a1)
    e0, e1 = jnp.exp(a0 - mx), jnp.exp(a1 - mx)
    return e0 / (e0 + e1)


def _tri(lower):
    r = lax.broadcasted_iota(jnp.int32, (CHUNK, CHUNK), 0)
    c = lax.broadcasted_iota(jnp.int32, (CHUNK, CHUNK), 1)
    return (r >= c) if lower else (r <= c)


def _hgrn_gates(q, f, lb, tri_lo):
    sg = _sigmoid(f)
    forget = lb + (1.0 - lb) * sg
    k = 1.0 - forget
    b = _hdot(tri_lo.astype(F32), jnp.log(forget))
    b_ref, b_last = b[CHUNK // 2 - 1:CHUNK // 2, :], b[CHUNK - 1:CHUNK, :]
    e1, e2, e3, e4 = jnp.exp(b - b_ref), jnp.exp(b_ref - b), jnp.exp(b_last - b), jnp.exp(b)
    return dict(sg=sg, forget=forget, k=k, e1=e1, e2=e2, e3=e3, e4=e4, qa=q * e1, ka=k * e2, kl=k * e3, qb=q * e4,
                decay=jnp.exp(b_last))


def _hgrn_fwd(z, lbraw, nw, exchange=None):
    T = z.shape[0]
    G = min(HGRN_GROUP, T // CHUNK)
    rows = G * CHUNK
    n_chunks = T // CHUNK

    def body(q_ref, f_ref, i_ref, g_ref, lbraw_ref, nw_ref, oraw_ref, og_ref, sp_ref, st_ref):
        @pl.when(pl.program_id(0) == 0)
        def _():
            st_ref[...] = jnp.zeros_like(st_ref)

        lb_all = _lower_bound(lbraw_ref)
        tri_lo = _tri(True)

        def chunk(cc, carry):
            rs = pl.ds(pl.multiple_of(cc * CHUNK, CHUNK), CHUNK)
            for h in range(HEADS):
                ln = slice(HEAD_DIM * h, HEAD_DIM * (h + 1))
                v = i_ref[rs, ln]
                t = _hgrn_gates(q_ref[rs, ln], f_ref[rs, ln], lb_all[:, ln], tri_lo)
                st = st_ref[h]
                sp_ref[cc, h] = st
                a = jnp.where(tri_lo, _bdot(t["qa"], t["ka"], NT), 0.0)
                o = _bdot(a, v) + _bdot(t["qb"], st, NT)
                st_ref[h] = st * t["decay"] + _bdot(v, t["kl"], TN)
                oraw_ref[rs, ln] = o
                on, _ = _rms_fwd(o, nw_ref[:, ln])
                gate = g_ref[rs, ln]
                og_ref[rs, ln] = (on * (gate * _sigmoid(gate))).astype(BF16)
            return carry

        lax.fori_loop(0, G, chunk, 0)

    col = lambda j: pl.BlockSpec((rows, 512), lambda r, j=j: (r, j))
    return _call(
        body, "hgrn_fwd", (z, z, z, z, lbraw, nw), grid=(T // rows,),
        in_specs=[col(0), col(1), col(2), col(3), _full((2, 512)), _full((1, 512))],
        out_specs=[col(0), col(0), pl.BlockSpec((G, HEADS, HEAD_DIM, HEAD_DIM), lambda r: (r, 0, 0, 0))],
        out_shape=[jax.ShapeDtypeStruct((T, 512), F32), jax.ShapeDtypeStruct((T, 512), BF16),
                   jax.ShapeDtypeStruct((n_chunks, HEADS, HEAD_DIM, HEAD_DIM), F32)],
        scratch_shapes=[pltpu.VMEM((HEADS, HEAD_DIM, HEAD_DIM), F32)], exchange=exchange)


def _hgrn_bwd(dmixcat, z, oraw, sprev, lbraw, nw, exchange=None):
    T = z.shape[0]
    G = min(HGRN_GROUP, T // CHUNK)
    rows = G * CHUNK
    ng = T // rows

    def body(dog_ref, q_ref, f_ref, i_ref, g_ref, oraw_ref, sp_ref, lbraw_ref, nw_ref,
             dz_ref, dlb_ref, dnw_ref, dst_ref):
        @pl.when(pl.program_id(0) == 0)
        def _():
            dst_ref[...] = jnp.zeros_like(dst_ref)
            dlb_ref[...] = jnp.zeros_like(dlb_ref)
            dnw_ref[...] = jnp.zeros_like(dnw_ref)

        lb_all = _lower_bound(lbraw_ref)
        tri_lo, tri_up = _tri(True), _tri(False)
        rowid = lax.broadcasted_iota(jnp.int32, (CHUNK, HEAD_DIM), 0)

        def chunk(it, carry):
            cc = G - 1 - it
            rs = pl.ds(pl.multiple_of(cc * CHUNK, CHUNK), CHUNK)
            for h in range(HEADS):
                ln = slice(HEAD_DIM * h, HEAD_DIM * (h + 1))
                q, v, lb = q_ref[rs, ln], i_ref[rs, ln], lb_all[:, ln]
                t = _hgrn_gates(q, f_ref[rs, ln], lb, tri_lo)
                o, gate, nwh, dog = oraw_ref[rs, ln], g_ref[rs, ln], nw_ref[:, ln], dog_ref[rs, ln]
                _, rs_o = _rms_fwd(o, nwh)
                sgg = _sigmoid(gate)
                d_on = dog * (gate * sgg)
                dz_ref[rs, 1536 + HEAD_DIM * h:1536 + HEAD_DIM * (h + 1)] = (
                    dog * (o * rs_o * nwh) * (sgg * (1.0 + gate * (1.0 - sgg))))
                do, nw_rows = _rms_bwd(o, rs_o, nwh, d_on)
                dnw_ref[:, ln] += _rowsum(nw_rows)
                st, dst = sp_ref[cc, h], dst_ref[h]
                a = jnp.where(tri_lo, _bdot(t["qa"], t["ka"], NT), 0.0)
                dqb = _bdot(do, st)
                dkl = _bdot(v, dst)
                dv_ = _bdot(t["kl"], dst, NT) + _bdot(a, do, TN)
                ddecay = _rowsum(dst * st)
                da = jnp.where(tri_lo, _bdot(do, v, NT), 0.0)
                dqa = _bdot(da, t["ka"])
                dka = _bdot(da, t["qa"], TN)
                dst_ref[h] = dst * t["decay"] + _bdot(do, t["qb"], TN)
                pa, pk, pb, pl_ = dqa * t["qa"], dka * t["ka"], dqb * t["qb"], dkl * t["kl"]
                db = pa - pk + pb - pl_
                db = db + jnp.where(rowid == CHUNK // 2 - 1, _rowsum(pk - pa), 0.0)
                db = db + jnp.where(rowid == CHUNK - 1, _rowsum(pl_) + ddecay * t["decay"], 0.0)
                dlogf = _hdot(tri_up.astype(F32), db)
                dk_ = dka * t["e2"] + dkl * t["e3"]
                dforget = dlogf / t["forget"] - dk_
                sg = t["sg"]
                dz_ref[rs, ln] = dqa * t["e1"] + dqb * t["e4"]
                dz_ref[rs, 512 + HEAD_DIM * h:512 + HEAD_DIM * (h + 1)] = dforget * (1.0 - lb) * sg * (1.0 - sg)
                dz_ref[rs, 1024 + HEAD_DIM * h:1024 + HEAD_DIM * (h + 1)] = dv_
                dlb_ref[:, ln] += _rowsum(dforget * (1.0 - sg))
            return carry

        lax.fori_loop(0, G, chunk, 0)

    col = lambda j: pl.BlockSpec((rows, 512), lambda r, j=j: (ng - 1 - r, j))
    return _call(
        body, "hgrn_bwd", (dmixcat, z, z, z, z, oraw, sprev, lbraw, nw), grid=(ng,),
        in_specs=[col(0), col(0), col(1), col(2), col(3), col(0),
                  pl.BlockSpec((G, HEADS, HEAD_DIM, HEAD_DIM), lambda r: (ng - 1 - r, 0, 0, 0)),
                  _full((2, 512)), _full((1, 512))],
        out_specs=[pl.BlockSpec((rows, 2048), lambda r: (ng - 1 - r, 0)), _full((1, 512)), _full((1, 512))],
        out_shape=[jax.ShapeDtypeStruct((T, 2048), F32), jax.ShapeDtypeStruct((1, 512), F32),
                   jax.ShapeDtypeStruct((1, 512), F32)],
        scratch_shapes=[pltpu.VMEM((HEADS, HEAD_DIM, HEAD_DIM), F32)], exchange=exchange)


def _diag_mask(t):
    r = lax.broadcasted_iota(jnp.int32, (t, t), 0)
    c = lax.broadcasted_iota(jnp.int32, (t, t), 1)
    return r >= c


def _attn_fwd(q, k, v, exchange=None):
    _, T, _ = q.shape
    t = min(ATT_TILE, T)

    def body(q_ref, k_ref, v_ref, o_ref, lse_ref):
        i = pl.program_id(1)
        qb = q_ref[...]

        def step(j, carry, masked):
            m, l, acc = carry
            ks = pl.ds(pl.multiple_of(j * t, t), t)
            s = _dot(qb, k_ref[ks, :], NT) * ATT_SCALE
            if masked:
                s = jnp.where(_diag_mask(t), s, NEG_BIG)
            mn = jnp.maximum(m, jnp.max(s, axis=-1, keepdims=True))
            p = jnp.exp(s - mn)
            al = jnp.exp(m - mn)
            return mn, al * l + jnp.sum(p, axis=-1, keepdims=True), al * acc + _dot(p.astype(BF16), v_ref[ks, :])

        init = (jnp.full((t, 1), NEG_BIG, F32), jnp.zeros((t, 1), F32), jnp.zeros((t, HEAD_DIM), F32))
        carry = lax.fori_loop(0, i, lambda j, c: step(j, c, False), init)
        m, l, acc = step(i, carry, True)
        o_ref[...] = acc / l
        lse_ref[...] = jnp.broadcast_to(m + jnp.log(l), (t, HEAD_DIM))

    return _call(
        body, "attn_fwd", (q, k, v), grid=(HEADS, T // t),
        in_specs=[pl.BlockSpec((None, t, QK_PAD), lambda h, i: (h, i, 0)),
                  pl.BlockSpec((None, T, QK_PAD), lambda h, i: (h, 0, 0)),
                  pl.BlockSpec((None, T, HEAD_DIM), lambda h, i: (h, 0, 0))],
        out_specs=[pl.BlockSpec((t, HEAD_DIM), lambda h, i: (i, h)),
                   pl.BlockSpec((None, t, HEAD_DIM), lambda h, i: (h, i, 0))],
        out_shape=[jax.ShapeDtypeStruct((T, HEADS * HEAD_DIM), F32), jax.ShapeDtypeStruct((HEADS, T, HEAD_DIM), F32)],
        exchange=exchange)


def _attn_bwd(q, k, v, dmixcat, o, lse, exchange=None):
    _, T, _ = q.shape
    t = min(ATT_TILE, T)
    nq = T // t

    def body(q_ref, k_ref, v_ref, do_ref, o_ref, lse_ref, dq_ref, dk_ref, dv_ref, delta_ref):
        j = pl.program_id(1)

        @pl.when(j == 0)
        def _():
            dq_ref[...] = jnp.zeros_like(dq_ref)

            def fill(i, carry):
                rs = pl.ds(pl.multiple_of(i * t, t), t)
                delta_ref[rs, :] = jnp.broadcast_to(
                    jnp.sum(do_ref[rs, :] * o_ref[rs, :], axis=-1, keepdims=True), (t, HEAD_DIM))
                return carry

            lax.fori_loop(0, nq, fill, 0)

        kb, vb = k_ref[...], v_ref[...]

        def step(i, carry, masked):
            dk, dv = carry
            rs = pl.ds(pl.multiple_of(i * t, t), t)
            qb, dob = q_ref[rs, :], do_ref[rs, :].astype(BF16)
            s = _dot(qb, kb, NT) * ATT_SCALE
            p = jnp.exp(s - lse_ref[rs, 0:1])
            if masked:
                p = jnp.where(_diag_mask(t), p, 0.0)
            dp = _dot(dob, vb, NT)
            ds = (p * (dp - delta_ref[rs, 0:1]) * ATT_SCALE).astype(BF16)
            dq_ref[rs, :] += _dot(ds, kb)
            return dk + _dot(ds, qb, TN), dv + _dot(p.astype(BF16), dob, TN)

        carry = step(j, (jnp.zeros((t, QK_PAD), F32), jnp.zeros((t, HEAD_DIM), F32)), True)
        dk, dv = lax.fori_loop(j + 1, nq, lambda i, c: step(i, c, False), carry)
        dk_ref[...] = dk
        dv_ref[...] = dv

    return _call(
        body, "attn_bwd", (q, k, v, dmixcat, o, lse), grid=(HEADS, nq),
        in_specs=[pl.BlockSpec((None, T, QK_PAD), lambda h, j: (h, 0, 0)),
                  pl.BlockSpec((None, t, QK_PAD), lambda h, j: (h, j, 0)),
                  pl.BlockSpec((None, t, HEAD_DIM), lambda h, j: (h, j, 0)),
                  pl.BlockSpec((T, HEAD_DIM), lambda h, j: (0, HEADS + h)),
                  pl.BlockSpec((T, HEAD_DIM), lambda h, j: (0, h)),
                  pl.BlockSpec((None, T, HEAD_DIM), lambda h, j: (h, 0, 0))],
        out_specs=[pl.BlockSpec((None, T, QK_PAD), lambda h, j: (h, 0, 0)),
                   pl.BlockSpec((None, t, QK_PAD), lambda h, j: (h, j, 0)),
                   pl.BlockSpec((None, t, HEAD_DIM), lambda h, j: (h, j, 0))],
        out_shape=[jax.ShapeDtypeStruct((HEADS, T, QK_PAD), F32), jax.ShapeDtypeStruct((HEADS, T, QK_PAD), F32),
                   jax.ShapeDtypeStruct((HEADS, T, HEAD_DIM), F32)],
        scratch_shapes=[pltpu.VMEM((T, HEAD_DIM), F32)], exchange=exchange)


def _ln_fwd(r):
    mu = _lanemean(r)
    xc = r - mu
    rstd = lax.rsqrt(_lanemean(xc * xc) + LN_EPS)
    return xc * rstd, rstd


def _ln_bwd(dxh, xhat, rstd):
    return rstd * (dxh - _lanemean(dxh) - xhat * _lanemean(dxh * xhat))


def _mix_ln1(mixcat, w_out, x, g_a, ln1_g, ln1_b, sc_m, sh_m):
    T = x.shape[0]
    tm = min(ROW_TILE_SMALL, T)

    def body(mc_ref, w_ref, x_ref, ga_ref, g_ref, b_ref, sc_ref, sh_ref, mix_ref, xhat_ref, rstd_ref, u2_ref):
        mix = _dot(mc_ref[...], w_ref[...])
        mix_ref[...] = mix
        xhat, rstd = _ln_fwd(ALPHA * x_ref[...] + (1.0 + ga_ref[...]) * mix)
        xhat_ref[...] = xhat
        rstd_ref[...] = jnp.broadcast_to(rstd, (tm, 128))
        u2_ref[...] = _modulate(xhat * g_ref[...] + b_ref[...], sc_ref[...], sh_ref[...]).astype(BF16)

    row = pl.BlockSpec((tm, D_MODEL), lambda i: (i, 0))
    vec = _full((1, D_MODEL))
    return pl.pallas_call(
        body, name="mix_ln1", grid=(T // tm,),
        in_specs=[row, _full(w_out.shape), row, vec, vec, vec, vec, vec],
        out_specs=[row, row, pl.BlockSpec((tm, 128), lambda i: (i, 0)), row],
        out_shape=[jax.ShapeDtypeStruct((T, D_MODEL), F32), jax.ShapeDtypeStruct((T, D_MODEL), F32),
                   jax.ShapeDtypeStruct((T, 128), F32), jax.ShapeDtypeStruct((T, D_MODEL), BF16)],
        compiler_params=_params(),
    )(mixcat, w_out, x, g_a, ln1_g, ln1_b, sc_m, sh_m)


def _mlp_fwd(u2, w1, w2, xhat1, ln1_g, ln1_b, g_m, ln2_g, ln2_b, target):
    T = u2.shape[0]
    nf, _, tf = w1.shape
    tm = min(ROW_TILE, T)

    def body(u2_ref, w1_ref, w2_ref, xhat_ref, g1_ref, b1_ref, gm_ref, g2_ref, b2_ref, tgt_ref,
             r_ref, dr2_ref, dh_ref, dg2_ref, db2_ref, dgm_ref, loss_ref, acc_ref):
        i, f = pl.program_id(0), pl.program_id(1)

        @pl.when((i == 0) & (f == 0))
        def _():
            for ref in (dg2_ref, db2_ref, dgm_ref, loss_ref):
                ref[...] = jnp.zeros_like(ref)

        @pl.when(f == 0)
        def _():
            acc_ref[...] = jnp.zeros_like(acc_ref)

        r = jnp.maximum(_dot(u2_ref[...], w1_ref[...]), 0.0)
        r_ref[...] = r.astype(BF16)
        acc_ref[...] += _bdot(r * r, w2_ref[...])

        @pl.when(f == nf - 1)
        def _():
            h = acc_ref[...]
            x1 = xhat_ref[...] * g1_ref[...] + b1_ref[...]
            xhat2, rstd2 = _ln_fwd(ALPHA * x1 + (1.0 + gm_ref[...]) * h)
            err = xhat2 * g2_ref[...] + b2_ref[...] - tgt_ref[...]
            loss_ref[...] += jnp.sum(0.5 * _lanemean(err * err), axis=0, keepdims=True)
            dy = err * (1.0 / D_MODEL)
            dg2_ref[...] += _rowsum(dy * xhat2)
            db2_ref[...] += _rowsum(dy)
            dr2 = _ln_bwd(dy * g2_ref[...], xhat2, rstd2)
            dr2_ref[...] = dr2
            dgm_ref[...] += _rowsum(dr2 * h)
            dh_ref[...] = ((1.0 + gm_ref[...]) * dr2).astype(BF16)

    row = pl.BlockSpec((tm, D_MODEL), lambda i, f: (i, 0))
    vec = _full((1, D_MODEL))
    return pl.pallas_call(
        body, name="mlp_fwd", grid=(T // tm, nf),
        in_specs=[row, pl.BlockSpec((None, D_MODEL, tf), lambda i, f: (f, 0, 0)),
                  pl.BlockSpec((None, tf, D_MODEL), lambda i, f: (f, 0, 0)), row, vec, vec, vec, vec, vec, row],
        out_specs=[pl.BlockSpec((tm, tf), lambda i, f: (i, f)), row, row, vec, vec, vec, _full((1, 128))],
        out_shape=[jax.ShapeDtypeStruct((T, nf * tf), BF16), jax.ShapeDtypeStruct((T, D_MODEL), F32),
                   jax.ShapeDtypeStruct((T, D_MODEL), BF16), jax.ShapeDtypeStruct((1, D_MODEL), F32),
                   jax.ShapeDtypeStruct((1, D_MODEL), F32), jax.ShapeDtypeStruct((1, D_MODEL), F32),
                   jax.ShapeDtypeStruct((1, 128), F32)],
        scratch_shapes=[pltpu.VMEM((tm, D_MODEL), F32)],
        compiler_params=_params(),
    )(u2, w1, w2, xhat1, ln1_g, ln1_b, g_m, ln2_g, ln2_b, target)


def _mlp_bwd(dh, w1, w2, r, dr2, xhat1, rstd1, mix, ln1_g, ln1_b, sc_m, g_a):
    T = dh.shape[0]
    nf, _, tf = w1.shape
    tm = min(ROW_TILE, T)

    def body(dh_ref, w1_ref, w2_ref, r_ref, dr2_ref, xhat_ref, rstd_ref, mix_ref, g1_ref, b1_ref, sc_ref, ga_ref,
             dhpre_ref, dr1_ref, dmix_ref, dsc_ref, dsh_ref, dg1_ref, db1_ref, dga_ref, acc_ref):
        i, f = pl.program_id(0), pl.program_id(1)

        @pl.when((i == 0) & (f == 0))
        def _():
            for ref in (dsc_ref, dsh_ref, dg1_ref, db1_ref, dga_ref):
                ref[...] = jnp.zeros_like(ref)

        @pl.when(f == 0)
        def _():
            acc_ref[...] = jnp.zeros_like(acc_ref)

        dhpre = (_dot(dh_ref[...], w2_ref[...], NT) * (2.0 * r_ref[...].astype(F32))).astype(BF16)
        dhpre_ref[...] = dhpre
        acc_ref[...] += _dot(dhpre, w1_ref[...], NT)

        @pl.when(f == nf - 1)
        def _():
            du2 = acc_ref[...]
            xhat = xhat_ref[...]
            x1 = xhat * g1_ref[...] + b1_ref[...]
            dx1 = ALPHA * dr2_ref[...] + du2 * (1.0 + sc_ref[...])
            dsc_ref[...] += _rowsum(du2 * x1)
            dsh_ref[...] += _rowsum(du2)
            dg1_ref[...] += _rowsum(dx1 * xhat)
            db1_ref[...] += _rowsum(dx1)
            dr1 = _ln_bwd(dx1 * g1_ref[...], xhat, rstd_ref[:, 0:1])
            dr1_ref[...] = dr1
            dga_ref[...] += _rowsum(dr1 * mix_ref[...])
            dmix_ref[...] = ((1.0 + ga_ref[...]) * dr1).astype(BF16)

    row = pl.BlockSpec((tm, D_MODEL), lambda i, f: (i, 0))
    vec = _full((1, D_MODEL))
    return pl.pallas_call(
        body, name="mlp_bwd", grid=(T // tm, nf),
        in_specs=[row, pl.BlockSpec((None, D_MODEL, tf), lambda i, f: (f, 0, 0)),
                  pl.BlockSpec((None, tf, D_MODEL), lambda i, f: (f, 0, 0)),
                  pl.BlockSpec((tm, tf), lambda i, f: (i, f)), row, row, pl.BlockSpec((tm, 128), lambda i, f: (i, 0)),
                  row, vec, vec, vec, vec],
        out_specs=[pl.BlockSpec((tm, tf), lambda i, f: (i, f)), row, row, vec, vec, vec, vec, vec],
        out_shape=[jax.ShapeDtypeStruct((T, nf * tf), BF16), jax.ShapeDtypeStruct((T, D_MODEL), F32),
                   jax.ShapeDtypeStruct((T, D_MODEL), BF16)] + [jax.ShapeDtypeStruct((1, D_MODEL), F32)] * 5,
        scratch_shapes=[pltpu.VMEM((tm, D_MODEL), F32)],
        compiler_params=_params(),
    )(dh, w1, w2, r, dr2, xhat1, rstd1, mix, ln1_g, ln1_b, sc_m, g_a)


def _input_bwd(dz_h, dz_m, w_in_ext, x, dr1, sc_a, exchange=None):
    T = x.shape[0]
    tm = min(ROW_TILE_SMALL, T)

    def body(dzh_ref, dzm_ref, w_ref, x_ref, dr1_ref, sc_ref, gx_ref, dsc_ref, dsh_ref):
        @pl.when(pl.program_id(0) == 0)
        def _():
            dsc_ref[...] = jnp.zeros_like(dsc_ref)
            dsh_ref[...] = jnp.zeros_like(dsh_ref)

        du = _bdot(dzh_ref[...], w_ref[:, 0:2048], NT) + _bdot(dzm_ref[...], w_ref[:, 2048:3072], NT)
        gx_ref[...] = ALPHA * dr1_ref[...] + du * (1.0 + sc_ref[...])
        dsc_ref[...] += _rowsum(du * x_ref[...])
        dsh_ref[...] += _rowsum(du)

    row = pl.BlockSpec((tm, D_MODEL), lambda i: (i, 0))
    vec = _full((1, D_MODEL))
    return _call(
        body, "input_bwd", (dz_h, dz_m, w_in_ext, x, dr1, sc_a), grid=(T // tm,),
        in_specs=[pl.BlockSpec((tm, 2048), lambda i: (i, 0)), row, _full(w_in_ext.shape), row, row, vec],
        out_specs=[row, vec, vec],
        out_shape=[jax.ShapeDtypeStruct((T, D_MODEL), F32), jax.ShapeDtypeStruct((1, D_MODEL), F32),
                   jax.ShapeDtypeStruct((1, D_MODEL), F32)], exchange=exchange)


def _adam_math(w, g, m, v):
    m = ADAM_B1 * m + (1.0 - ADAM_B1) * g
    v = ADAM_B2 * v + (1.0 - ADAM_B2) * (g * g)
    m_hat = m / (1.0 - ADAM_B1 ** ADAM_STEP)
    v_hat = v / (1.0 - ADAM_B2 ** ADAM_STEP)
    return -ADAM_LR * (m_hat / (jnp.sqrt(v_hat) + ADAM_EPS) + ADAM_WD * w), m, v


def _adam(g_slabs, w, m, v, name, g_fn=None, g_extra=()):
    R, C = w.shape
    tr = R if R <= 256 else 256
    assert R % tr == 0
    ns = 0 if g_slabs is None else g_slabs.shape[0]
    ne = len(g_extra)

    def body(*refs):
        e_refs = refs[:ne]
        refs = refs[ne:]
        if ns:
            gs_ref, refs = refs[0], refs[1:]
        w_ref, m_ref, v_ref, g_ref, d_ref, nm_ref, nv_ref = refs
        if g_fn is not None:
            g = g_fn(*e_refs)
        else:
            g = gs_ref[0].astype(F32)
            for s in range(1, ns):
                g = g + gs_ref[s].astype(F32)
        d, nm, nv = _adam_math(w_ref[...], g, m_ref[...], v_ref[...])
        g_ref[...] = g
        d_ref[...] = d
        nm_ref[...] = nm
        nv_ref[...] = nv

    blk = pl.BlockSpec((tr, C), lambda i: (i, 0))
    in_specs = [pl.BlockSpec((tr, e.shape[1]), lambda i: (i, 0)) if e.shape[0] == R else _full(e.shape) for e in g_extra]
    args = list(g_extra)
    if ns:
        in_specs.append(pl.BlockSpec((ns, tr, C), lambda i: (0, i, 0)))
        args.append(g_slabs)
    return pl.pallas_call(
        body, name=name, grid=(R // tr,), in_specs=in_specs + [blk] * 3, out_specs=[blk] * 4,
        out_shape=[jax.ShapeDtypeStruct((R, C), F32)] * 4, compiler_params=_params(),
    )(*args, w, m, v)


def _small_reduce(small_all, lbraw):
    W = small_all.shape[-1]

    def body(s_ref, lbraw_ref, sum_ref, glb_ref):
        tot = s_ref[0]
        for i in range(1, N_DEV):
            tot = tot + s_ref[i]
        sum_ref[...] = tot
        lb = _lower_bound(lbraw_ref)
        g0 = tot[:, 6144:6656] * lb * (1.0 - lb)
        glb_ref[0:1, :] = g0
        glb_ref[1:2, :] = -g0

    return pl.pallas_call(
        body, name="small_reduce",
        out_shape=[jax.ShapeDtypeStruct((1, W), F32), jax.ShapeDtypeStruct((2, 512), F32)],
        compiler_params=_params(),
    )(small_all, lbraw)


def _rot_half_cols(w):
    return jnp.concatenate([-w[..., 32:], w[..., :32]], axis=-1)


def _unrot_half_cols(dw_rot):
    return jnp.concatenate([dw_rot[..., 32:], -dw_rot[..., :32]], axis=-1)


def _cols_from_slabs(g):
    s, r, c = g.shape
    return jnp.transpose(g, (1, 0, 2)).reshape(r, s * c)


def _slabs_from_cols(w):
    r, c = w.shape
    return jnp.transpose(w.reshape(r, N_DEV, c // N_DEV), (1, 0, 2))


def _ext_in(w_in):
    k_in = w_in.shape[0]
    z64, z128 = jnp.zeros((k_in, 64), BF16), jnp.zeros((k_in, 128), BF16)
    wk = w_in[:, 2560:2624]
    return jnp.concatenate([w_in[:, :2560], z128, wk, z64, z128, _rot_half_cols(wk), z64], axis=1)


def _ext_q(w_q_up):
    r = w_q_up.shape[0]
    z64, z128 = jnp.zeros((r, 64), BF16), jnp.zeros((r, 128), BF16)
    wq = w_q_up.reshape(r, HEADS, HEAD_DIM + ROPE_DIM)
    main = [jnp.concatenate([wq[:, h, :HEAD_DIM], wq[:, h, HEAD_DIM:], z64], axis=1) for h in range(HEADS)]
    rot = [jnp.concatenate([z128, _rot_half_cols(wq[:, h, HEAD_DIM:]), z64], axis=1) for h in range(HEADS)]
    return jnp.concatenate(main + rot, axis=1)


def _ext_kv(w_kv_up):
    r = w_kv_up.shape[0]
    z128 = jnp.zeros((r, 128), BF16)
    wkv = w_kv_up.reshape(r, HEADS, 2 * HEAD_DIM)
    kpad = [jnp.concatenate([wkv[:, h, :HEAD_DIM], z128], axis=1) for h in range(HEADS)]
    vals = [wkv[:, h, HEAD_DIM:] for h in range(HEADS)]
    return jnp.concatenate(kpad + vals, axis=1)


def _grads_from_ext(dw_in_h, dw_in_m, dwq_ext, dwkv_ext):
    dwk = dw_in_m[:, 512 + 128:512 + 192] + _unrot_half_cols(dw_in_m[:, 768 + 128:768 + 192])
    dw_in = jnp.concatenate([dw_in_h, dw_in_m[:, :512], dwk], axis=1)
    qcols = []
    for h in range(HEADS):
        main, rot = dwq_ext[:, 256 * h:256 * h + 256], dwq_ext[:, 1024 + 256 * h:1280 + 256 * h]
        qcols += [main[:, :128], main[:, 128:192] + _unrot_half_cols(rot[:, 128:192])]
    kvcols = []
    for h in range(HEADS):
        kvcols += [dwkv_ext[:, 256 * h:256 * h + 128], dwkv_ext[:, 1024 + 128 * h:1152 + 128 * h]]
    return dw_in, jnp.concatenate(qcols, axis=1), jnp.concatenate(kvcols, axis=1)


SMALL_W = 6144 + 512 + 512 + 256 + 256 + 4 * 1024 + 128


def kernel(x, c, positions, w_ada, b_ada, w_in, hg_lower_bounds, hg_norm_w, mla_q_norm_w, w_q_up, mla_kv_norm_w, w_kv_up, w_out, ln1_g, ln1_b, w_mlp_in, w_mlp_out, ln2_g, ln2_b, loss_target, m_w_ada, m_b_ada, m_w_in, m_hg_lower_bounds, m_hg_norm_w, m_mla_q_norm_w, m_w_q_up, m_mla_kv_norm_w, m_w_kv_up, m_w_out, m_ln1_g, m_ln1_b, m_w_mlp_in, m_w_mlp_out, m_ln2_g, m_ln2_b, v_w_ada, v_b_ada, v_w_in, v_hg_lower_bounds, v_hg_norm_w, v_mla_q_norm_w, v_w_q_up, v_mla_kv_norm_w, v_w_kv_up, v_w_out, v_ln1_g, v_ln1_b, v_w_mlp_in, v_w_mlp_out, v_ln2_g, v_ln2_b):
    T = x.shape[1]
    me = 4 * lax.axis_index("x") + 2 * lax.axis_index("y") + lax.axis_index("c")
    xs, tgt = x[0], loss_target[0]
    big = dict(w_in=w_in[0], w_q_up=w_q_up[0], w_kv_up=w_kv_up[0], w_out=w_out[0], w_mlp_in=w_mlp_in[0],
               w_mlp_out=w_mlp_out[0])
    names = list(big)

    bf = {n: big[n].astype(BF16) for n in names}
    g_in, g_c = _exchange([bf["w_in"], c], scatter=False, name="gather_w_in")
    c_all = g_c.reshape(N_DEV, D_MODEL)

    ada_cols = w_ada.shape[2]
    mod_part, cond = _mod_part(c_all, w_ada[0], lax.dynamic_slice(b_ada, (0, me * ada_cols), (1, ada_cols)))
    (mod_all,) = _exchange([mod_part], scatter=False, name="gather_mod")
    mod_row = lax.dynamic_slice(mod_all, (0, me, 0), (N_DEV, 1, ada_cols)).reshape(1, N_DEV * ada_cols)
    sh_a, sc_a, g_a, sh_m, sc_m, g_m = [mod_row[:, D_MODEL * i:D_MODEL * (i + 1)] for i in range(6)]

    w_in_ext = _ext_in(_cols_from_slabs(g_in))
    z, (g_q, g_kv, g_out) = _matmul(xs, w_in_ext, "NN", "in_proj", a_fn=_modulate, extras=(sc_a, sh_a),
                                    exchange=_Exchange([bf["w_q_up"], bf["w_kv_up"], bf["w_out"]], False))
    wq_ext, wkv_ext = _ext_q(_cols_from_slabs(g_q)), _ext_kv(_cols_from_slabs(g_kv))
    w_out_full = g_out.reshape(D_MODEL, D_MODEL)
    inv_freq = 1.0 / (ROPE_THETA ** (jnp.arange(0, ROPE_DIM, 2, dtype=F32) / ROPE_DIM))
    zeros = lambda n: jnp.zeros((n,), F32)
    invf = jnp.concatenate([zeros(128), inv_freq, inv_freq, zeros(64)]).reshape(1, QK_PAD)
    m_one = jnp.concatenate([jnp.ones((128,), F32), zeros(128)]).reshape(1, QK_PAD)
    m_rot = jnp.concatenate([zeros(128), jnp.ones((64,), F32), zeros(64)]).reshape(1, QK_PAD)
    q, k, v, c1, s1, cqn, ckvn = _mla_pre(z, positions.reshape(T, 1), invf, m_one, m_rot, wq_ext, wkv_ext,
                                          mla_q_norm_w, mla_kv_norm_w)
    (o_raw, o_gated, s_prev), (w1,) = _hgrn_fwd(z, hg_lower_bounds, hg_norm_w,
                                                exchange=_Exchange([bf["w_mlp_in"]], False))
    (o_mla, lse), (w2,) = _attn_fwd(q, k, v, exchange=_Exchange([bf["w_mlp_out"]], False))
    mixcat = jnp.concatenate([o_gated, o_mla.astype(BF16)], axis=1)
    mix, xhat1, rstd1, u2 = _mix_ln1(mixcat, w_out_full, xs, g_a, ln1_g, ln1_b, sc_m, sh_m)
    r, dr2, dh, dln2_g, dln2_b, dg_m, loss_part = _mlp_fwd(u2, w1, w2, xhat1, ln1_g, ln1_b, g_m, ln2_g, ln2_b, tgt)

    dhpre, dr1, dmix, dsc_m, dsh_m, dln1_g, dln1_b, dg_a = _mlp_bwd(dh, w1, w2, r, dr2, xhat1, rstd1, mix, ln1_g,
                                                                    ln1_b, sc_m, g_a)
    received = {}
    dw2 = _matmul(r, dh, "TN", "wgrad_mlp_out", out_dtype=BF16, a_fn=_square, tk=512)
    dw1 = _matmul(u2, dhpre, "TN", "wgrad_mlp_in", out_dtype=BF16, tk=512, out_slabs=N_DEV)
    dmixcat = _matmul(dmix, w_out_full, "NT", "dgrad_out")
    dw_out = _matmul(mixcat, dmix, "TN", "wgrad_out", out_dtype=BF16, tk=512)
    (dz_h, dlb, dnw), (received["w_mlp_out"],) = _hgrn_bwd(
        dmixcat, z, o_raw, s_prev, hg_lower_bounds, hg_norm_w,
        exchange=_Exchange([dw2.reshape(N_DEV, dw2.shape[0] // N_DEV, D_MODEL)], True))
    (dq, dk, dv), (received["w_mlp_in"], received["w_out"]) = _attn_bwd(
        q, k, v, dmixcat, o_mla, lse,
        exchange=_Exchange([dw1, dw_out.reshape(N_DEV, D_MODEL // N_DEV, D_MODEL)], True))
    dz_m, dq_ext, dkv_ext, dqnw, dkvnw = _mla_bwd(dq, dk, dv, z, c1, s1, wq_ext, wkv_ext, mla_q_norm_w,
                                                   mla_kv_norm_w)
    dwq_ext = _matmul(cqn, dq_ext, "TN", "wgrad_q_up", tk=512)
    dwkv_ext = _matmul(ckvn, dkv_ext, "TN", "wgrad_kv_up", tn=768, tk=512)
    dw_in_h = _matmul(xs, dz_h, "TN", "wgrad_in_h", a_fn=_modulate, extras=(sc_a, sh_a), tk=512)
    dw_in_m = _matmul(xs, dz_m, "TN", "wgrad_in_m", a_fn=_modulate, extras=(sc_a, sh_a), tk=512)
    dw_in, dwq, dwkv = _grads_from_ext(dw_in_h, dw_in_m, dwq_ext, dwkv_ext)
    last = [_slabs_from_cols(g).astype(BF16) for g in (dw_in, dwq, dwkv)]
    (grad_x, dsc_a, dsh_a), (received["w_in"], received["w_q_up"], received["w_kv_up"]) = _input_bwd(
        dz_h, dz_m, w_in_ext, xs, dr1, sc_a, exchange=_Exchange(last, True))

    small = jnp.concatenate([dsh_a, dsc_a, dg_a, dsh_m, dsc_m, dg_m, dlb, dnw, dqnw, dkvnw, dln1_g, dln1_b, dln2_g,
                             dln2_b, loss_part], axis=1)
    (small_all,) = _exchange([small], scatter=False, name="gather_small")
    small_sum, glb = _small_reduce(small_all, hg_lower_bounds)

    moments = dict(w_in=(m_w_in, v_w_in), w_q_up=(m_w_q_up, v_w_q_up), w_kv_up=(m_w_kv_up, v_w_kv_up),
                   w_out=(m_w_out, v_w_out), w_mlp_in=(m_w_mlp_in, v_w_mlp_in), w_mlp_out=(m_w_mlp_out, v_w_mlp_out))
    res = {}
    for n in names:
        res[n] = _adam(received[n], big[n], moments[n][0][0], moments[n][1][0], name="adam_" + n)
    dmod_cols = lax.dynamic_slice(small_all.reshape(N_DEV, SMALL_W), (0, me * ada_cols), (N_DEV, ada_cols))
    cond_t = cond.T

    def ada_grad(ct_ref, dm_ref):
        g = ct_ref[:, 0:1] * dm_ref[0:1, :]
        for b in range(1, N_DEV):
            g = g + ct_ref[:, b:b + 1] * dm_ref[b:b + 1, :]
        return g

    res["w_ada"] = _adam(None, w_ada[0], m_w_ada[0], v_w_ada[0], name="adam_w_ada", g_fn=ada_grad,
                         g_extra=(cond_t, dmod_cols))

    seg = lambda a, b: small_sum[:, a:b]
    small_params = [("b_ada", b_ada, m_b_ada, v_b_ada, seg(0, 6144)),
                    ("hg_lower_bounds", hg_lower_bounds, m_hg_lower_bounds, v_hg_lower_bounds, glb),
                    ("hg_norm_w", hg_norm_w, m_hg_norm_w, v_hg_norm_w, seg(6656, 7168)),
                    ("mla_q_norm_w", mla_q_norm_w, m_mla_q_norm_w, v_mla_q_norm_w, seg(7168, 7424)),
                    ("mla_kv_norm_w", mla_kv_norm_w, m_mla_kv_norm_w, v_mla_kv_norm_w, seg(7424, 7680)),
                    ("ln1_g", ln1_g, m_ln1_g, v_ln1_g, seg(7680, 8704)), ("ln1_b", ln1_b, m_ln1_b, v_ln1_b, seg(8704, 9728)),
                    ("ln2_g", ln2_g, m_ln2_g, v_ln2_g, seg(9728, 10752)), ("ln2_b", ln2_b, m_ln2_b, v_ln2_b, seg(10752, 11776))]
    pack = lambda i: jnp.concatenate([p[i].reshape(1, -1) for p in small_params], axis=1)
    packed = _adam(pack(4)[None], pack(1), pack(2), pack(3), name="adam_small")
    off = 0
    for n, w, _, _, _ in small_params:
        res[n] = tuple(a[:, off:off + w.size].reshape(w.shape) for a in packed)
        off += w.size
    loss = small_sum[0, 11776]

    order = ["w_ada", "b_ada", "w_in", "hg_lower_bounds", "hg_norm_w", "mla_q_norm_w", "w_q_up", "mla_kv_norm_w",
             "w_kv_up", "w_out", "ln1_g", "ln1_b", "w_mlp_in", "w_mlp_out", "ln2_g", "ln2_b"]
    shaped = {n: tuple(a.reshape((1,) + a.shape) if n in big or n == "w_ada" else a for a in res[n]) for n in order}
    outs = [loss, grad_x.reshape(1, T, D_MODEL)]
    for i in range(4):
        outs += [shaped[n][i] for n in order]
    return tuple(outs)
```

```python
import functools

import jax
import jax.numpy as jnp
import numpy as np
from jax import lax
from jax.experimental import pallas as pl
from jax.experimental.pallas import tpu as pltpu

F32, BF16 = jnp.float32, jnp.bfloat16
N_DEV = 8
D_MODEL = 1024
HEADS = 4
HEAD_DIM = 128
ROPE_DIM = 64
QK_PAD = 256
CHUNK = 64
ROPE_THETA = 10000.0
RMS_EPS = 1e-6
LN_EPS = 1e-5
ALPHA = 2.0 ** 0.25
ATT_SCALE = (HEAD_DIM + ROPE_DIM) ** -0.5
ADAM_LR, ADAM_B1, ADAM_B2, ADAM_EPS, ADAM_WD, ADAM_STEP = 0.001, 0.9, 0.999, 1e-08, 0.01, 10
NEG_BIG = -1e30

ROW_TILE = 512
ROW_TILE_SMALL = 256
ATT_TILE = 512
HGRN_GROUP = 8
VMEM_LIMIT = 56 * 2 ** 20

NN = (((1,), (0,)), ((), ()))
NT = (((1,), (1,)), ((), ()))
TN = (((0,), (0,)), ((), ()))


def _dot(a, b, dims=NN):
    return lax.dot_general(a, b, dims, preferred_element_type=F32)


def _bdot(a, b, dims=NN):
    return lax.dot_general(a.astype(BF16), b.astype(BF16), dims, preferred_element_type=F32)


def _hdot(a, b, dims=NN):
    return lax.dot_general(a, b, dims, precision=lax.Precision.HIGHEST, preferred_element_type=F32)


def _params():
    return pltpu.CompilerParams(vmem_limit_bytes=VMEM_LIMIT)


def _sigmoid(x):
    return 1.0 / (1.0 + jnp.exp(-x))


def _rowsum(x):
    return jnp.sum(x, axis=0, keepdims=True)


def _lanemean(x):
    return jnp.mean(x, axis=-1, keepdims=True)


def _full(shape):
    nd = len(shape)
    return pl.BlockSpec(shape, lambda *_: (0,) * nd)


class _Exchange:
    def __init__(self, arrs, scatter):
        self.arrs, self.scatter, self.n = list(arrs), scatter, len(arrs)
        self.out_shape = [jax.ShapeDtypeStruct((N_DEV,) + (a.shape[1:] if scatter else a.shape), a.dtype)
                          for a in self.arrs]
        n = self.n
        self.scratch = [pltpu.SemaphoreType.DMA((n, N_DEV - 1)), pltpu.SemaphoreType.DMA((n, N_DEV - 1)),
                        pltpu.SemaphoreType.DMA((n,))]

    def _copies(self, ins, outs, sems):
        send_sems, recv_sems, loc_sems = sems
        x, y, c = lax.axis_index("x"), lax.axis_index("y"), lax.axis_index("c")
        me = 4 * x + 2 * y + c
        copies = []
        for k in range(self.n):
            src_of = (lambda i, k=k: ins[k].at[i]) if self.scatter else (lambda i, k=k: ins[k])
            copies.append((pltpu.make_async_copy(src_of(me), outs[k].at[me], loc_sems.at[k]), None))
            for p in range(1, N_DEV):
                px = (1 - x) if p & 4 else x
                py = (1 - y) if p & 2 else y
                pc = (1 - c) if p & 1 else c
                peer = 4 * px + 2 * py + pc
                both = dict(send_sem=send_sems.at[k, p - 1], recv_sem=recv_sems.at[k, p - 1],
                            device_id=(px, py, pc), device_id_type=pl.DeviceIdType.MESH)
                send = pltpu.make_async_remote_copy(src_ref=src_of(peer), dst_ref=outs[k].at[me], **both)
                recv = pltpu.make_async_remote_copy(src_ref=src_of(peer), dst_ref=outs[k].at[peer], **both)
                copies.append((send, recv))
        return copies

    def start(self, ins, outs, sems):
        for first, _ in self._copies(ins, outs, sems):
            first.start()

    def wait(self, ins, outs, sems):
        for first, recv in self._copies(ins, outs, sems):
            if recv is None:
                first.wait()
            else:
                recv.wait_recv()
                first.wait_send()


def _call(body, name, args, out_shape, grid=(), in_specs=(), out_specs=(), scratch_shapes=(), exchange=None):
    if exchange is None:
        return pl.pallas_call(body, name=name, grid=grid, in_specs=list(in_specs), out_specs=list(out_specs),
                              out_shape=list(out_shape), scratch_shapes=list(scratch_shapes),
                              compiler_params=_params())(*args), None
    ni, no, ns, nx = len(args), len(out_shape), len(scratch_shapes), exchange.n

    def wrapped(*refs):
        a, xi = refs[:ni], refs[ni:ni + nx]
        o, xo = refs[ni + nx:ni + nx + no], refs[ni + nx + no:ni + 2 * nx + no]
        s, xs = refs[ni + 2 * nx + no:ni + 2 * nx + no + ns], refs[ni + 2 * nx + no + ns:]
        first = last = None
        for d, g in enumerate(grid):
            f, l = pl.program_id(d) == 0, pl.program_id(d) == g - 1
            first, last = (f, l) if first is None else (first & f, last & l)

        @pl.when(first)
        def _():
            exchange.start(xi, xo, xs)

        body(*a, *o, *s)

        @pl.when(last)
        def _():
            exchange.wait(xi, xo, xs)

    hbm = pl.BlockSpec(memory_space=pltpu.HBM)
    res = pl.pallas_call(
        wrapped, name=name, grid=grid, in_specs=list(in_specs) + [hbm] * nx, out_specs=list(out_specs) + [hbm] * nx,
        out_shape=list(out_shape) + exchange.out_shape, scratch_shapes=list(scratch_shapes) + exchange.scratch,
        compiler_params=_params())(*args, *exchange.arrs)
    return res[:no], res[no:]


def _exchange(arrs, scatter, name):
    ex = _Exchange(arrs, scatter)

    def body(*refs):
        ins, outs, sems = refs[:ex.n], refs[ex.n:2 * ex.n], refs[2 * ex.n:]
        ex.start(ins, outs, sems)
        ex.wait(ins, outs, sems)

    hbm = pl.BlockSpec(memory_space=pltpu.HBM)
    return pl.pallas_call(body, name=name, out_shape=ex.out_shape, in_specs=[hbm] * ex.n, out_specs=[hbm] * ex.n,
                          scratch_shapes=ex.scratch)(*ex.arrs)


def _matmul(a, b, mode, name, out_dtype=F32, tm=512, tn=1024, tk=1024, a_fn=None, extras=(), out_slabs=None,
            exchange=None):
    if mode == "NN":
        (M, K), N = a.shape, b.shape[1]
    elif mode == "NT":
        (M, K), N = a.shape, b.shape[0]
    else:
        (K, M), N = a.shape, b.shape[1]
    if out_slabs:
        tn = N // out_slabs
    tm, tn, tk = min(tm, M), min(tn, N), min(tk, K)
    assert M % tm == 0 and N % tn == 0 and K % tk == 0, (name, M, N, K)
    nk = K // tk
    dims = {"NN": NN, "NT": NT, "TN": TN}[mode]
    ne = len(extras)

    def body(a_ref, b_ref, *rest):
        e_refs, o_ref, acc_ref = rest[:ne], rest[ne], rest[ne + 1]
        k = pl.program_id(2)

        @pl.when(k == 0)
        def _():
            acc_ref[...] = jnp.zeros_like(acc_ref)

        at = a_ref[...]
        if a_fn is not None:
            at = a_fn(at.astype(F32), *[e[...] for e in e_refs])
        acc_ref[...] += _bdot(at, b_ref[...], dims)

        @pl.when(k == nk - 1)
        def _():
            o_ref[...] = acc_ref[...].astype(out_dtype)

    if mode == "TN":
        a_spec = pl.BlockSpec((tk, tm), lambda i, j, k: (k, i))
        e_spec = pl.BlockSpec((1, tm), lambda i, j, k: (0, i))
    else:
        a_spec = pl.BlockSpec((tm, tk), lambda i, j, k: (i, k))
        e_spec = pl.BlockSpec((1, tk), lambda i, j, k: (0, k))
    if mode == "NT":
        b_spec = pl.BlockSpec((tn, tk), lambda i, j, k: (j, k))
    else:
        b_spec = pl.BlockSpec((tk, tn), lambda i, j, k: (k, j))
    if out_slabs:
        o_shape = jax.ShapeDtypeStruct((out_slabs, M, tn), out_dtype)
        o_spec = pl.BlockSpec((None, tm, tn), lambda i, j, k: (j, i, 0))
    else:
        o_shape = jax.ShapeDtypeStruct((M, N), out_dtype)
        o_spec = pl.BlockSpec((tm, tn), lambda i, j, k: (i, j))
    (out,), got = _call(body, name, (a, b, *extras), [o_shape], grid=(M // tm, N // tn, nk),
                        in_specs=[a_spec, b_spec] + [e_spec] * ne, out_specs=[o_spec],
                        scratch_shapes=[pltpu.VMEM((tm, tn), F32)], exchange=exchange)
    return out if exchange is None else (out, got)


def _modulate(x, sc, sh):
    return x * (1.0 + sc) + sh


def _square(x):
    return x * x


def _mod_part(c_all, w_ada_s, b_s):
    def body(c_ref, w_ref, b_ref, mod_ref, cond_ref):
        cv = c_ref[...]
        cond = cv * _sigmoid(cv)
        cond_ref[...] = cond
        mod_ref[...] = _bdot(cond, w_ref[...]) + b_ref[...]

    return pl.pallas_call(
        body, name="mod_part",
        out_shape=[jax.ShapeDtypeStruct((N_DEV, w_ada_s.shape[1]), F32), jax.ShapeDtypeStruct(c_all.shape, F32)],
        compiler_params=_params(),
    )(c_all, w_ada_s, b_s)


def _rms_fwd(x, w):
    rs = lax.rsqrt(_lanemean(x * x) + RMS_EPS)
    return x * rs * w, rs


def _rms_bwd(x, rs, w, dy):
    xhat = x * rs
    dxh = dy * w
    return rs * (dxh - xhat * _lanemean(dxh * xhat)), dy * xhat


def _mla_pre(z, pos_col, invf, m_one, m_rot, wq_ext, wkv_ext, qnw, kvnw):
    T = z.shape[0]
    tm = min(ROW_TILE, T)

    def body(z_ref, pos_ref, invf_ref, mone_ref, mrot_ref, wq_ref, wkv_ref, qnw_ref, kvnw_ref,
             q_ref, k_ref, v_ref, c1_ref, s1_ref, cqn_ref, ckvn_ref):
        ang = pos_ref[...].astype(F32) * invf_ref[...]
        c1 = mone_ref[...] + mrot_ref[...] * jnp.cos(ang)
        s1 = mrot_ref[...] * jnp.sin(ang)
        c1_ref[...] = c1
        s1_ref[...] = s1
        cqn, _ = _rms_fwd(z_ref[:, 0:256], qnw_ref[...])
        ckvn, _ = _rms_fwd(z_ref[:, 256:512], kvnw_ref[...])
        cqn_ref[...] = cqn.astype(BF16)
        ckvn_ref[...] = ckvn.astype(BF16)
        qe = _bdot(cqn, wq_ref[...])
        kve = _bdot(ckvn, wkv_ref[...])
        k_rope = z_ref[:, 512:768] * c1 + z_ref[:, 768:1024] * s1
        for h in range(HEADS):
            q_ref[h] = (qe[:, 256 * h:256 * h + 256] * c1 + qe[:, 1024 + 256 * h:1280 + 256 * h] * s1).astype(BF16)
            k_ref[h] = (kve[:, 256 * h:256 * h + 256] + k_rope).astype(BF16)
            v_ref[h] = kve[:, 1024 + 128 * h:1152 + 128 * h].astype(BF16)

    row = lambda i: (i, 0)
    head = lambda i: (0, i, 0)
    return pl.pallas_call(
        body, name="mla_pre", grid=(T // tm,),
        in_specs=[pl.BlockSpec((tm, 1024), lambda i: (i, 2)), pl.BlockSpec((tm, 1), row),
                  _full((1, 256)), _full((1, 256)), _full((1, 256)), _full(wq_ext.shape), _full(wkv_ext.shape),
                  _full((1, 256)), _full((1, 256))],
        out_specs=[pl.BlockSpec((HEADS, tm, QK_PAD), head), pl.BlockSpec((HEADS, tm, QK_PAD), head),
                   pl.BlockSpec((HEADS, tm, HEAD_DIM), head), pl.BlockSpec((tm, 256), row), pl.BlockSpec((tm, 256), row),
                   pl.BlockSpec((tm, 256), row), pl.BlockSpec((tm, 256), row)],
        out_shape=[jax.ShapeDtypeStruct((HEADS, T, QK_PAD), BF16), jax.ShapeDtypeStruct((HEADS, T, QK_PAD), BF16),
                   jax.ShapeDtypeStruct((HEADS, T, HEAD_DIM), BF16), jax.ShapeDtypeStruct((T, 256), F32),
                   jax.ShapeDtypeStruct((T, 256), F32), jax.ShapeDtypeStruct((T, 256), BF16),
                   jax.ShapeDtypeStruct((T, 256), BF16)],
        compiler_params=_params(),
    )(z, pos_col, invf, m_one, m_rot, wq_ext, wkv_ext, qnw, kvnw)


def _mla_bwd(dq, dk, dv, z, c1, s1, wq_ext, wkv_ext, qnw, kvnw):
    T = z.shape[0]
    tm = min(ROW_TILE_SMALL, T)

    def body(dq_ref, dk_ref, dv_ref, z_ref, c1_ref, s1_ref, wq_ref, wkv_ref, qnw_ref, kvnw_ref,
             dz_ref, dqe_ref, dkve_ref, dqnw_ref, dkvnw_ref):
        @pl.when(pl.program_id(0) == 0)
        def _():
            dqnw_ref[...] = jnp.zeros_like(dqnw_ref)
            dkvnw_ref[...] = jnp.zeros_like(dkvnw_ref)

        c1, s1 = c1_ref[...], s1_ref[...]
        dkpe = jnp.zeros((tm, QK_PAD), F32)
        for h in range(HEADS):
            dqh, dkh = dq_ref[h], dk_ref[h]
            dqe_ref[:, 256 * h:256 * h + 256] = (dqh * c1).astype(BF16)
            dqe_ref[:, 1024 + 256 * h:1280 + 256 * h] = (dqh * s1).astype(BF16)
            dkve_ref[:, 256 * h:256 * h + 256] = dkh.astype(BF16)
            dkve_ref[:, 1024 + 128 * h:1152 + 128 * h] = dv_ref[h].astype(BF16)
            dkpe = dkpe + dkh
        dcqn = _dot(dqe_ref[...], wq_ref[...], NT)
        dckvn = _dot(dkve_ref[...], wkv_ref[...], NT)
        cq, ckv = z_ref[:, 0:256], z_ref[:, 256:512]
        _, rsq = _rms_fwd(cq, qnw_ref[...])
        _, rskv = _rms_fwd(ckv, kvnw_ref[...])
        dcq, wq_rows = _rms_bwd(cq, rsq, qnw_ref[...], dcqn)
        dckv, wkv_rows = _rms_bwd(ckv, rskv, kvnw_ref[...], dckvn)
        dqnw_ref[...] += _rowsum(wq_rows)
        dkvnw_ref[...] += _rowsum(wkv_rows)
        dz_ref[:, 0:256] = dcq
        dz_ref[:, 256:512] = dckv
        dz_ref[:, 512:768] = dkpe * c1
        dz_ref[:, 768:1024] = dkpe * s1

    row = lambda i: (i, 0)
    head = lambda i: (0, i, 0)
    return pl.pallas_call(
        body, name="mla_bwd", grid=(T // tm,),
        in_specs=[pl.BlockSpec((HEADS, tm, QK_PAD), head), pl.BlockSpec((HEADS, tm, QK_PAD), head),
                  pl.BlockSpec((HEADS, tm, HEAD_DIM), head), pl.BlockSpec((tm, 1024), lambda i: (i, 2)),
                  pl.BlockSpec((tm, 256), row), pl.BlockSpec((tm, 256), row), _full(wq_ext.shape), _full(wkv_ext.shape),
                  _full((1, 256)), _full((1, 256))],
        out_specs=[pl.BlockSpec((tm, 1024), row), pl.BlockSpec((tm, 2048), row), pl.BlockSpec((tm, 1536), row),
                   _full((1, 256)), _full((1, 256))],
        out_shape=[jax.ShapeDtypeStruct((T, 1024), F32), jax.ShapeDtypeStruct((T, 2048), BF16),
                   jax.ShapeDtypeStruct((T, 1536), BF16), jax.ShapeDtypeStruct((1, 256), F32),
                   jax.ShapeDtypeStruct((1, 256), F32)],
        compiler_params=_params(),
    )(dq, dk, dv, z, c1, s1, wq_ext, wkv_ext, qnw, kvnw)


_HEAD_LANES = [slice(HEAD_DIM * h, HEAD_DIM * (h + 1)) for h in range(HEADS)]


def _lower_bound(lbraw_ref):
    a0, a1 = lbraw_ref[0:1, :], lbraw_ref[1:2, :]
    mx = jnp.maximum(a0, a1)
    e0, e1 = jnp.exp(a0 - mx), jnp.exp(a1 - mx)
    return e0 / (e0 + e1)


def _tri(lower):
    r = lax.broadcasted_iota(jnp.int32, (CHUNK, CHUNK), 0)
    c = lax.broadcasted_iota(jnp.int32, (CHUNK, CHUNK), 1)
    return (r >= c) if lower else (r <= c)


def _hgrn_gates(q, f, lb, tri_lo):
    sg = _sigmoid(f)
    forget = lb + (1.0 - lb) * sg
    k = 1.0 - forget
    b = _hdot(tri_lo.astype(F32), jnp.log(forget))
    b_ref, b_last = b[CHUNK // 2 - 1:CHUNK // 2, :], b[CHUNK - 1:CHUNK, :]
    e1, e2, e3, e4 = jnp.exp(b - b_ref), jnp.exp(b_ref - b), jnp.exp(b_last - b), jnp.exp(b)
    return dict(sg=sg, forget=forget, k=k, e1=e1, e2=e2, e3=e3, e4=e4, qa=q * e1, ka=k * e2, kl=k * e3, qb=q * e4,
                decay=jnp.exp(b_last))


def _hgrn_fwd(z, lbraw, nw, exchange=None):
    T = z.shape[0]
    G = min(HGRN_GROUP, T // CHUNK)
    rows = G * CHUNK
    n_chunks = T // CHUNK

    def body(q_ref, f_ref, i_ref, g_ref, lbraw_ref, nw_ref, oraw_ref, og_ref, sp_ref, st_ref):
        @pl.when(pl.program_id(0) == 0)
        def _():
            st_ref[...] = jnp.zeros_like(st_ref)

        lb_all = _lower_bound(lbraw_ref)
        tri_lo = _tri(True)

        def chunk(cc, carry):
            rs = pl.ds(pl.multiple_of(cc * CHUNK, CHUNK), CHUNK)
            t = _hgrn_gates(q_ref[rs, :], f_ref[rs, :], lb_all, tri_lo)
            v, gate = i_ref[rs, :], g_ref[rs, :]
            st = [st_ref[h] for h in range(HEADS)]
            a = [jnp.where(tri_lo, _bdot(t["qa"][:, s], t["ka"][:, s], NT), 0.0) for s in _HEAD_LANES]
            kv = [_bdot(v[:, s], t["kl"][:, s], TN) for s in _HEAD_LANES]
            o = [_bdot(a[h], v[:, s]) + _bdot(t["qb"][:, s], st[h], NT) for h, s in enumerate(_HEAD_LANES)]
            for h, s in enumerate(_HEAD_LANES):
                sp_ref[cc, h] = st[h]
                st_ref[h] = st[h] * t["decay"][:, s] + kv[h]
            oraw_ref[rs, :] = jnp.concatenate(o, axis=1)
            on = jnp.concatenate([_rms_fwd(o[h], nw_ref[:, s])[0] for h, s in enumerate(_HEAD_LANES)], axis=1)
            og_ref[rs, :] = (on * (gate * _sigmoid(gate))).astype(BF16)
            return carry

        lax.fori_loop(0, G, chunk, 0)

    col = lambda j: pl.BlockSpec((rows, 512), lambda r, j=j: (r, j))
    return _call(
        body, "hgrn_fwd", (z, z, z, z, lbraw, nw), grid=(T // rows,),
        in_specs=[col(0), col(1), col(2), col(3), _full((2, 512)), _full((1, 512))],
        out_specs=[col(0), col(0), pl.BlockSpec((G, HEADS, HEAD_DIM, HEAD_DIM), lambda r: (r, 0, 0, 0))],
        out_shape=[jax.ShapeDtypeStruct((T, 512), F32), jax.ShapeDtypeStruct((T, 512), BF16),
                   jax.ShapeDtypeStruct((n_chunks, HEADS, HEAD_DIM, HEAD_DIM), F32)],
        scratch_shapes=[pltpu.VMEM((HEADS, HEAD_DIM, HEAD_DIM), F32)], exchange=exchange)


def _hgrn_bwd(dmixcat, z, oraw, sprev, lbraw, nw, exchange=None):
    T = z.shape[0]
    G = min(HGRN_GROUP, T // CHUNK)
    rows = G * CHUNK
    ng = T // rows

    def body(dog_ref, q_ref, f_ref, i_ref, g_ref, oraw_ref, sp_ref, lbraw_ref, nw_ref,
             dz_ref, dlb_ref, dnw_ref, dst_ref):
        @pl.when(pl.program_id(0) == 0)
        def _():
            dst_ref[...] = jnp.zeros_like(dst_ref)
            dlb_ref[...] = jnp.zeros_like(dlb_ref)
            dnw_ref[...] = jnp.zeros_like(dnw_ref)

        lb_all = _lower_bound(lbraw_ref)
        tri_lo, tri_up = _tri(True), _tri(False)
        rowid = lax.broadcasted_iota(jnp.int32, (CHUNK, HEADS * HEAD_DIM), 0)

        def chunk(it, carry):
            cc = G - 1 - it
            rs = pl.ds(pl.multiple_of(cc * CHUNK, CHUNK), CHUNK)
            heads = list(enumerate(_HEAD_LANES))
            cat = lambda parts: jnp.concatenate(parts, axis=1)
            per_head_mean = lambda x: cat([jnp.broadcast_to(_lanemean(x[:, s]), (CHUNK, HEAD_DIM)) for s in _HEAD_LANES])
            t = _hgrn_gates(q_ref[rs, :], f_ref[rs, :], lb_all, tri_lo)
            v, gate, o, dog, nw_all = i_ref[rs, :], g_ref[rs, :], oraw_ref[rs, :], dog_ref[rs, :], nw_ref[...]
            rs_o = lax.rsqrt(per_head_mean(o * o) + RMS_EPS)
            xhat = o * rs_o
            sgg = _sigmoid(gate)
            d_on = dog * (gate * sgg)
            dz_ref[rs, 1536:2048] = dog * (xhat * nw_all) * (sgg * (1.0 + gate * (1.0 - sgg)))
            dxh = d_on * nw_all
            do = rs_o * (dxh - xhat * per_head_mean(dxh * xhat))
            dnw_ref[...] += _rowsum(d_on * xhat)
            st = [sp_ref[cc, h] for h in range(HEADS)]
            dst = [dst_ref[h] for h in range(HEADS)]
            a = [jnp.where(tri_lo, _bdot(t["qa"][:, s], t["ka"][:, s], NT), 0.0) for s in _HEAD_LANES]
            da = [jnp.where(tri_lo, _bdot(do[:, s], v[:, s], NT), 0.0) for s in _HEAD_LANES]
            dqb = cat([_bdot(do[:, s], st[h]) for h, s in heads])
            dkl = cat([_bdot(v[:, s], dst[h]) for h, s in heads])
            dv_ = cat([_bdot(t["kl"][:, s], dst[h], NT) + _bdot(a[h], do[:, s], TN) for h, s in heads])
            dqa = cat([_bdot(da[h], t["ka"][:, s]) for h, s in heads])
            dka = cat([_bdot(da[h], t["qa"][:, s], TN) for h, s in heads])
            ddecay = cat([_rowsum(dst[h] * st[h]) for h in range(HEADS)])
            for h, s in heads:
                dst_ref[h] = dst[h] * t["decay"][:, s] + _bdot(do[:, s], t["qb"][:, s], TN)
            pa, pk, pb, pl_ = dqa * t["qa"], dka * t["ka"], dqb * t["qb"], dkl * t["kl"]
            db = pa - pk + pb - pl_
            db = db + jnp.where(rowid == CHUNK // 2 - 1, _rowsum(pk - pa), 0.0)
            db = db + jnp.where(rowid == CHUNK - 1, _rowsum(pl_) + ddecay * t["decay"], 0.0)
            dlogf = _hdot(tri_up.astype(F32), db)
            dforget = dlogf / t["forget"] - (dka * t["e2"] + dkl * t["e3"])
            sg = t["sg"]
            dz_ref[rs, 0:512] = dqa * t["e1"] + dqb * t["e4"]
            dz_ref[rs, 512:1024] = dforget * (1.0 - lb_all) * sg * (1.0 - sg)
            dz_ref[rs, 1024:1536] = dv_
            dlb_ref[...] += _rowsum(dforget * (1.0 - sg))
            return carry

        lax.fori_loop(0, G, chunk, 0)

    col = lambda j: pl.BlockSpec((rows, 512), lambda r, j=j: (ng - 1 - r, j))
    return _call(
        body, "hgrn_bwd", (dmixcat, z, z, z, z, oraw, sprev, lbraw, nw), grid=(ng,),
        in_specs=[col(0), col(0), col(1), col(2), col(3), col(0),
                  pl.BlockSpec((G, HEADS, HEAD_DIM, HEAD_DIM), lambda r: (ng - 1 - r, 0, 0, 0)),
                  _full((2, 512)), _full((1, 512))],
        out_specs=[pl.BlockSpec((rows, 2048), lambda r: (ng - 1 - r, 0)), _full((1, 512)), _full((1, 512))],
        out_shape=[jax.ShapeDtypeStruct((T, 2048), F32), jax.ShapeDtypeStruct((1, 512), F32),
                   jax.ShapeDtypeStruct((1, 512), F32)],
        scratch_shapes=[pltpu.VMEM((HEADS, HEAD_DIM, HEAD_DIM), F32)], exchange=exchange)


def _diag_mask(t):
    r = lax.broadcasted_iota(jnp.int32, (t, t), 0)
    c = lax.broadcasted_iota(jnp.int32, (t, t), 1)
    return r >= c


def _attn_fwd(q, k, v, exchange=None):
    _, T, _ = q.shape
    t = min(ATT_TILE, T)

    def body(q_ref, k_ref, v_ref, o_ref, lse_ref):
        i = pl.program_id(1)
        qb = q_ref[...]

        def step(j, carry, masked):
            m, l, acc = carry
            ks = pl.ds(pl.multiple_of(j * t, t), t)
            s = _dot(qb, k_ref[ks, :], NT) * ATT_SCALE
            if masked:
                s = jnp.where(_diag_mask(t), s, NEG_BIG)
            mn = jnp.maximum(m, jnp.max(s, axis=-1, keepdims=True))
            p = jnp.exp(s - mn)
            al = jnp.exp(m - mn)
            return mn, al * l + jnp.sum(p, axis=-1, keepdims=True), al * acc + _dot(p.astype(BF16), v_ref[ks, :])

        init = (jnp.full((t, 1), NEG_BIG, F32), jnp.zeros((t, 1), F32), jnp.zeros((t, HEAD_DIM), F32))
        carry = lax.fori_loop(0, i, lambda j, c: step(j, c, False), init)
        m, l, acc = step(i, carry, True)
        o_ref[...] = acc / l
        lse_ref[...] = jnp.broadcast_to(m + jnp.log(l), (t, HEAD_DIM))

    return _call(
        body, "attn_fwd", (q, k, v), grid=(HEADS, T // t),
        in_specs=[pl.BlockSpec((None, t, QK_PAD), lambda h, i: (h, i, 0)),
                  pl.BlockSpec((None, T, QK_PAD), lambda h, i: (h, 0, 0)),
                  pl.BlockSpec((None, T, HEAD_DIM), lambda h, i: (h, 0, 0))],
        out_specs=[pl.BlockSpec((t, HEAD_DIM), lambda h, i: (i, h)),
                   pl.BlockSpec((None, t, HEAD_DIM), lambda h, i: (h, i, 0))],
        out_shape=[jax.ShapeDtypeStruct((T, HEADS * HEAD_DIM), F32), jax.ShapeDtypeStruct((HEADS, T, HEAD_DIM), F32)],
        exchange=exchange)


def _attn_bwd(q, k, v, dmixcat, o, lse, exchange=None):
    _, T, _ = q.shape
    t = min(ATT_TILE, T)
    nq = T // t

    def body(q_ref, k_ref, v_ref, do_ref, o_ref, lse_ref, dq_ref, dk_ref, dv_ref, delta_ref):
        j = pl.program_id(1)

        @pl.when(j == 0)
        def _():
            dq_ref[...] = jnp.zeros_like(dq_ref)

            def fill(i, carry):
                rs = pl.ds(pl.multiple_of(i * t, t), t)
                delta_ref[rs, :] = jnp.broadcast_to(
                    jnp.sum(do_ref[rs, :] * o_ref[rs, :], axis=-1, keepdims=True), (t, HEAD_DIM))
                return carry

            lax.fori_loop(0, nq, fill, 0)

        kb, vb = k_ref[...], v_ref[...]

        def step(i, carry, masked):
            dk, dv = carry
            rs = pl.ds(pl.multiple_of(i * t, t), t)
            qb, dob = q_ref[rs, :], do_ref[rs, :].astype(BF16)
            s = _dot(qb, kb, NT) * ATT_SCALE
            p = jnp.exp(s - lse_ref[rs, 0:1])
            if masked:
                p = jnp.where(_diag_mask(t), p, 0.0)
            dp = _dot(dob, vb, NT)
            ds = (p * (dp - delta_ref[rs, 0:1]) * ATT_SCALE).astype(BF16)
            dq_ref[rs, :] += _dot(ds, kb)
            return dk + _dot(ds, qb, TN), dv + _dot(p.astype(BF16), dob, TN)

        carry = step(j, (jnp.zeros((t, QK_PAD), F32), jnp.zeros((t, HEAD_DIM), F32)), True)
        dk, dv = lax.fori_loop(j + 1, nq, lambda i, c: step(i, c, False), carry)
        dk_ref[...] = dk
        dv_ref[...] = dv

    return _call(
        body, "attn_bwd", (q, k, v, dmixcat, o, lse), grid=(HEADS, nq),
        in_specs=[pl.BlockSpec((None, T, QK_PAD), lambda h, j: (h, 0, 0)),
                  pl.BlockSpec((None, t, QK_PAD), lambda h, j: (h, j, 0)),
                  pl.BlockSpec((None, t, HEAD_DIM), lambda h, j: (h, j, 0)),
                  pl.BlockSpec((T, HEAD_DIM), lambda h, j: (0, HEADS + h)),
                  pl.BlockSpec((T, HEAD_DIM), lambda h, j: (0, h)),
                  pl.BlockSpec((None, T, HEAD_DIM), lambda h, j: (h, 0, 0))],
        out_specs=[pl.BlockSpec((None, T, QK_PAD), lambda h, j: (h, 0, 0)),
                   pl.BlockSpec((None, t, QK_PAD), lambda h, j: (h, j, 0)),
                   pl.BlockSpec((None, t, HEAD_DIM), lambda h, j: (h, j, 0))],
        out_shape=[jax.ShapeDtypeStruct((HEADS, T, QK_PAD), F32), jax.ShapeDtypeStruct((HEADS, T, QK_PAD), F32),
                   jax.ShapeDtypeStruct((HEADS, T, HEAD_DIM), F32)],
        scratch_shapes=[pltpu.VMEM((T, HEAD_DIM), F32)], exchange=exchange)


def _ln_fwd(r):
    mu = _lanemean(r)
    xc = r - mu
    rstd = lax.rsqrt(_lanemean(xc * xc) + LN_EPS)
    return xc * rstd, rstd


def _ln_bwd(dxh, xhat, rstd):
    return rstd * (dxh - _lanemean(dxh) - xhat * _lanemean(dxh * xhat))


def _mix_ln1(mixcat, w_out, x, g_a, ln1_g, ln1_b, sc_m, sh_m):
    T = x.shape[0]
    tm = min(ROW_TILE_SMALL, T)

    def body(mc_ref, w_ref, x_ref, ga_ref, g_ref, b_ref, sc_ref, sh_ref, mix_ref, xhat_ref, rstd_ref, u2_ref):
        mix = _dot(mc_ref[...], w_ref[...])
        mix_ref[...] = mix
        xhat, rstd = _ln_fwd(ALPHA * x_ref[...] + (1.0 + ga_ref[...]) * mix)
        xhat_ref[...] = xhat
        rstd_ref[...] = jnp.broadcast_to(rstd, (tm, 128))
        u2_ref[...] = _modulate(xhat * g_ref[...] + b_ref[...], sc_ref[...], sh_ref[...]).astype(BF16)

    row = pl.BlockSpec((tm, D_MODEL), lambda i: (i, 0))
    vec = _full((1, D_MODEL))
    return pl.pallas_call(
        body, name="mix_ln1", grid=(T // tm,),
        in_specs=[row, _full(w_out.shape), row, vec, vec, vec, vec, vec],
        out_specs=[row, row, pl.BlockSpec((tm, 128), lambda i: (i, 0)), row],
        out_shape=[jax.ShapeDtypeStruct((T, D_MODEL), F32), jax.ShapeDtypeStruct((T, D_MODEL), F32),
                   jax.ShapeDtypeStruct((T, 128), F32), jax.ShapeDtypeStruct((T, D_MODEL), BF16)],
        compiler_params=_params(),
    )(mixcat, w_out, x, g_a, ln1_g, ln1_b, sc_m, sh_m)


def _mlp_fwd(u2, w1, w2, xhat1, ln1_g, ln1_b, g_m, ln2_g, ln2_b, target):
    T = u2.shape[0]
    nf, _, tf = w1.shape
    tm = min(ROW_TILE, T)

    def body(u2_ref, w1_ref, w2_ref, xhat_ref, g1_ref, b1_ref, gm_ref, g2_ref, b2_ref, tgt_ref,
             r_ref, dr2_ref, dh_ref, dg2_ref, db2_ref, dgm_ref, loss_ref, acc_ref):
        i, f = pl.program_id(0), pl.program_id(1)

        @pl.when((i == 0) & (f == 0))
        def _():
            for ref in (dg2_ref, db2_ref, dgm_ref, loss_ref):
                ref[...] = jnp.zeros_like(ref)

        @pl.when(f == 0)
        def _():
            acc_ref[...] = jnp.zeros_like(acc_ref)

        r = jnp.maximum(_dot(u2_ref[...], w1_ref[...]), 0.0)
        r_ref[...] = r.astype(BF16)
        acc_ref[...] += _bdot(r * r, w2_ref[...])

        @pl.when(f == nf - 1)
        def _():
            h = acc_ref[...]
            x1 = xhat_ref[...] * g1_ref[...] + b1_ref[...]
            xhat2, rstd2 = _ln_fwd(ALPHA * x1 + (1.0 + gm_ref[...]) * h)
            err = xhat2 * g2_ref[...] + b2_ref[...] - tgt_ref[...]
            loss_ref[...] += jnp.sum(0.5 * _lanemean(err * err), axis=0, keepdims=True)
            dy = err * (1.0 / D_MODEL)
            dg2_ref[...] += _rowsum(dy * xhat2)
            db2_ref[...] += _rowsum(dy)
            dr2 = _ln_bwd(dy * g2_ref[...], xhat2, rstd2)
            dr2_ref[...] = dr2
            dgm_ref[...] += _rowsum(dr2 * h)
            dh_ref[...] = ((1.0 + gm_ref[...]) * dr2).astype(BF16)

    row = pl.BlockSpec((tm, D_MODEL), lambda i, f: (i, 0))
    vec = _full((1, D_MODEL))
    return pl.pallas_call(
        body, name="mlp_fwd", grid=(T // tm, nf),
        in_specs=[row, pl.BlockSpec((None, D_MODEL, tf), lambda i, f: (f, 0, 0)),
                  pl.BlockSpec((None, tf, D_MODEL), lambda i, f: (f, 0, 0)), row, vec, vec, vec, vec, vec, row],
        out_specs=[pl.BlockSpec((tm, tf), lambda i, f: (i, f)), row, row, vec, vec, vec, _full((1, 128))],
        out_shape=[jax.ShapeDtypeStruct((T, nf * tf), BF16), jax.ShapeDtypeStruct((T, D_MODEL), F32),
                   jax.ShapeDtypeStruct((T, D_MODEL), BF16), jax.ShapeDtypeStruct((1, D_MODEL), F32),
                   jax.ShapeDtypeStruct((1, D_MODEL), F32), jax.ShapeDtypeStruct((1, D_MODEL), F32),
                   jax.ShapeDtypeStruct((1, 128), F32)],
        scratch_shapes=[pltpu.VMEM((tm, D_MODEL), F32)],
        compiler_params=_params(),
    )(u2, w1, w2, xhat1, ln1_g, ln1_b, g_m, ln2_g, ln2_b, target)


def _mlp_bwd(dh, w1, w2, r, dr2, xhat1, rstd1, mix, ln1_g, ln1_b, sc_m, g_a):
    T = dh.shape[0]
    nf, _, tf = w1.shape
    tm = min(ROW_TILE, T)

    def body(dh_ref, w1_ref, w2_ref, r_ref, dr2_ref, xhat_ref, rstd_ref, mix_ref, g1_ref, b1_ref, sc_ref, ga_ref,
             dhpre_ref, dr1_ref, dmix_ref, dsc_ref, dsh_ref, dg1_ref, db1_ref, dga_ref, acc_ref):
        i, f = pl.program_id(0), pl.program_id(1)

        @pl.when((i == 0) & (f == 0))
        def _():
            for ref in (dsc_ref, dsh_ref, dg1_ref, db1_ref, dga_ref):
                ref[...] = jnp.zeros_like(ref)

        @pl.when(f == 0)
        def _():
            acc_ref[...] = jnp.zeros_like(acc_ref)

        dhpre = (_dot(dh_ref[...], w2_ref[...], NT) * (2.0 * r_ref[...].astype(F32))).astype(BF16)
        dhpre_ref[...] = dhpre
        acc_ref[...] += _dot(dhpre, w1_ref[...], NT)

        @pl.when(f == nf - 1)
        def _():
            du2 = acc_ref[...]
            xhat = xhat_ref[...]
            x1 = xhat * g1_ref[...] + b1_ref[...]
            dx1 = ALPHA * dr2_ref[...] + du2 * (1.0 + sc_ref[...])
            dsc_ref[...] += _rowsum(du2 * x1)
            dsh_ref[...] += _rowsum(du2)
            dg1_ref[...] += _rowsum(dx1 * xhat)
            db1_ref[...] += _rowsum(dx1)
            dr1 = _ln_bwd(dx1 * g1_ref[...], xhat, rstd_ref[:, 0:1])
            dr1_ref[...] = dr1
            dga_ref[...] += _rowsum(dr1 * mix_ref[...])
            dmix_ref[...] = ((1.0 + ga_ref[...]) * dr1).astype(BF16)

    row = pl.BlockSpec((tm, D_MODEL), lambda i, f: (i, 0))
    vec = _full((1, D_MODEL))
    return pl.pallas_call(
        body, name="mlp_bwd", grid=(T // tm, nf),
        in_specs=[row, pl.BlockSpec((None, D_MODEL, tf), lambda i, f: (f, 0, 0)),
                  pl.BlockSpec((None, tf, D_MODEL), lambda i, f: (f, 0, 0)),
                  pl.BlockSpec((tm, tf), lambda i, f: (i, f)), row, row, pl.BlockSpec((tm, 128), lambda i, f: (i, 0)),
                  row, vec, vec, vec, vec],
        out_specs=[pl.BlockSpec((tm, tf), lambda i, f: (i, f)), row, row, vec, vec, vec, vec, vec],
        out_shape=[jax.ShapeDtypeStruct((T, nf * tf), BF16), jax.ShapeDtypeStruct((T, D_MODEL), F32),
                   jax.ShapeDtypeStruct((T, D_MODEL), BF16)] + [jax.ShapeDtypeStruct((1, D_MODEL), F32)] * 5,
        scratch_shapes=[pltpu.VMEM((tm, D_MODEL), F32)],
        compiler_params=_params(),
    )(dh, w1, w2, r, dr2, xhat1, rstd1, mix, ln1_g, ln1_b, sc_m, g_a)


def _input_bwd(dz_h, dz_m, w_in_ext, x, dr1, sc_a, exchange=None):
    T = x.shape[0]
    tm = min(ROW_TILE_SMALL, T)

    def body(dzh_ref, dzm_ref, w_ref, x_ref, dr1_ref, sc_ref, gx_ref, dsc_ref, dsh_ref):
        @pl.when(pl.program_id(0) == 0)
        def _():
            dsc_ref[...] = jnp.zeros_like(dsc_ref)
            dsh_ref[...] = jnp.zeros_like(dsh_ref)

        du = _bdot(dzh_ref[...], w_ref[:, 0:2048], NT) + _bdot(dzm_ref[...], w_ref[:, 2048:3072], NT)
        gx_ref[...] = ALPHA * dr1_ref[...] + du * (1.0 + sc_ref[...])
        dsc_ref[...] += _rowsum(du * x_ref[...])
        dsh_ref[...] += _rowsum(du)

    row = pl.BlockSpec((tm, D_MODEL), lambda i: (i, 0))
    vec = _full((1, D_MODEL))
    return _call(
        body, "input_bwd", (dz_h, dz_m, w_in_ext, x, dr1, sc_a), grid=(T // tm,),
        in_specs=[pl.BlockSpec((tm, 2048), lambda i: (i, 0)), row, _full(w_in_ext.shape), row, row, vec],
        out_specs=[row, vec, vec],
        out_shape=[jax.ShapeDtypeStruct((T, D_MODEL), F32), jax.ShapeDtypeStruct((1, D_MODEL), F32),
                   jax.ShapeDtypeStruct((1, D_MODEL), F32)], exchange=exchange)


def _adam_math(w, g, m, v):
    m = ADAM_B1 * m + (1.0 - ADAM_B1) * g
    v = ADAM_B2 * v + (1.0 - ADAM_B2) * (g * g)
    m_hat = m / (1.0 - ADAM_B1 ** ADAM_STEP)
    v_hat = v / (1.0 - ADAM_B2 ** ADAM_STEP)
    return -ADAM_LR * (m_hat / (jnp.sqrt(v_hat) + ADAM_EPS) + ADAM_WD * w), m, v


def _adam(g_slabs, w, m, v, name, g_fn=None, g_extra=()):
    R, C = w.shape
    tr = R if R <= 256 else 256
    assert R % tr == 0
    ns = 0 if g_slabs is None else g_slabs.shape[0]
    ne = len(g_extra)

    def body(*refs):
        e_refs = refs[:ne]
        refs = refs[ne:]
        if ns:
            gs_ref, refs = refs[0], refs[1:]
        w_ref, m_ref, v_ref, g_ref, d_ref, nm_ref, nv_ref = refs
        if g_fn is not None:
            g = g_fn(*e_refs)
        else:
            g = gs_ref[0].astype(F32)
            for s in range(1, ns):
                g = g + gs_ref[s].astype(F32)
        d, nm, nv = _adam_math(w_ref[...], g, m_ref[...], v_ref[...])
        g_ref[...] = g
        d_ref[...] = d
        nm_ref[...] = nm
        nv_ref[...] = nv

    blk = pl.BlockSpec((tr, C), lambda i: (i, 0))
    in_specs = [pl.BlockSpec((tr, e.shape[1]), lambda i: (i, 0)) if e.shape[0] == R else _full(e.shape) for e in g_extra]
    args = list(g_extra)
    if ns:
        in_specs.append(pl.BlockSpec((ns, tr, C), lambda i: (0, i, 0)))
        args.append(g_slabs)
    return pl.pallas_call(
        body, name=name, grid=(R // tr,), in_specs=in_specs + [blk] * 3, out_specs=[blk] * 4,
        out_shape=[jax.ShapeDtypeStruct((R, C), F32)] * 4, compiler_params=_params(),
    )(*args, w, m, v)


def _small_reduce(small_all, lbraw):
    W = small_all.shape[-1]

    def body(s_ref, lbraw_ref, sum_ref, glb_ref):
        tot = s_ref[0]
        for i in range(1, N_DEV):
            tot = tot + s_ref[i]
        sum_ref[...] = tot
        lb = _lower_bound(lbraw_ref)
        g0 = tot[:, 6144:6656] * lb * (1.0 - lb)
        glb_ref[0:1, :] = g0
        glb_ref[1:2, :] = -g0

    return pl.pallas_call(
        body, name="small_reduce",
        out_shape=[jax.ShapeDtypeStruct((1, W), F32), jax.ShapeDtypeStruct((2, 512), F32)],
        compiler_params=_params(),
    )(small_all, lbraw)


def _rot_half_cols(w):
    return jnp.concatenate([-w[..., 32:], w[..., :32]], axis=-1)


def _unrot_half_cols(dw_rot):
    return jnp.concatenate([dw_rot[..., 32:], -dw_rot[..., :32]], axis=-1)


def _cols_from_slabs(g):
    s, r, c = g.shape
    return jnp.transpose(g, (1, 0, 2)).reshape(r, s * c)


def _slabs_from_cols(w):
    r, c = w.shape
    return jnp.transpose(w.reshape(r, N_DEV, c // N_DEV), (1, 0, 2))


def _ext_in(w_in):
    k_in = w_in.shape[0]
    z64, z128 = jnp.zeros((k_in, 64), BF16), jnp.zeros((k_in, 128), BF16)
    wk = w_in[:, 2560:2624]
    return jnp.concatenate([w_in[:, :2560], z128, wk, z64, z128, _rot_half_cols(wk), z64], axis=1)


def _ext_q(w_q_up):
    r = w_q_up.shape[0]
    z64, z128 = jnp.zeros((r, 64), BF16), jnp.zeros((r, 128), BF16)
    wq = w_q_up.reshape(r, HEADS, HEAD_DIM + ROPE_DIM)
    main = [jnp.concatenate([wq[:, h, :HEAD_DIM], wq[:, h, HEAD_DIM:], z64], axis=1) for h in range(HEADS)]
    rot = [jnp.concatenate([z128, _rot_half_cols(wq[:, h, HEAD_DIM:]), z64], axis=1) for h in range(HEADS)]
    return jnp.concatenate(main + rot, axis=1)


def _ext_kv(w_kv_up):
    r = w_kv_up.shape[0]
    z128 = jnp.zeros((r, 128), BF16)
    wkv = w_kv_up.reshape(r, HEADS, 2 * HEAD_DIM)
    kpad = [jnp.concatenate([wkv[:, h, :HEAD_DIM], z128], axis=1) for h in range(HEADS)]
    vals = [wkv[:, h, HEAD_DIM:] for h in range(HEADS)]
    return jnp.concatenate(kpad + vals, axis=1)


def _grads_from_ext(dw_in_h, dw_in_m, dwq_ext, dwkv_ext):
    dwk = dw_in_m[:, 512 + 128:512 + 192] + _unrot_half_cols(dw_in_m[:, 768 + 128:768 + 192])
    dw_in = jnp.concatenate([dw_in_h, dw_in_m[:, :512], dwk], axis=1)
    qcols = []
    for h in range(HEADS):
        main, rot = dwq_ext[:, 256 * h:256 * h + 256], dwq_ext[:, 1024 + 256 * h:1280 + 256 * h]
        qcols += [main[:, :128], main[:, 128:192] + _unrot_half_cols(rot[:, 128:192])]
    kvcols = []
    for h in range(HEADS):
        kvcols += [dwkv_ext[:, 256 * h:256 * h + 128], dwkv_ext[:, 1024 + 128 * h:1152 + 128 * h]]
    return dw_in, jnp.concatenate(qcols, axis=1), jnp.concatenate(kvcols, axis=1)


SMALL_W = 6144 + 512 + 512 + 256 + 256 + 4 * 1024 + 128


def kernel(x, c, positions, w_ada, b_ada, w_in, hg_lower_bounds, hg_norm_w, mla_q_norm_w, w_q_up, mla_kv_norm_w, w_kv_up, w_out, ln1_g, ln1_b, w_mlp_in, w_mlp_out, ln2_g, ln2_b, loss_target, m_w_ada, m_b_ada, m_w_in, m_hg_lower_bounds, m_hg_norm_w, m_mla_q_norm_w, m_w_q_up, m_mla_kv_norm_w, m_w_kv_up, m_w_out, m_ln1_g, m_ln1_b, m_w_mlp_in, m_w_mlp_out, m_ln2_g, m_ln2_b, v_w_ada, v_b_ada, v_w_in, v_hg_lower_bounds, v_hg_norm_w, v_mla_q_norm_w, v_w_q_up, v_mla_kv_norm_w, v_w_kv_up, v_w_out, v_ln1_g, v_ln1_b, v_w_mlp_in, v_w_mlp_out, v_ln2_g, v_ln2_b):
    T = x.shape[1]
    me = 4 * lax.axis_index("x") + 2 * lax.axis_index("y") + lax.axis_index("c")
    xs, tgt = x[0], loss_target[0]
    big = dict(w_in=w_in[0], w_q_up=w_q_up[0], w_kv_up=w_kv_up[0], w_out=w_out[0], w_mlp_in=w_mlp_in[0],
               w_mlp_out=w_mlp_out[0])
    names = list(big)

    bf = {n: big[n].astype(BF16) for n in names}
    g_in, g_c = _exchange([bf["w_in"], c], scatter=False, name="gather_w_in")
    c_all = g_c.reshape(N_DEV, D_MODEL)

    ada_cols = w_ada.shape[2]
    mod_part, cond = _mod_part(c_all, w_ada[0], lax.dynamic_slice(b_ada, (0, me * ada_cols), (1, ada_cols)))
    (mod_all,) = _exchange([mod_part], scatter=False, name="gather_mod")
    mod_row = lax.dynamic_slice(mod_all, (0, me, 0), (N_DEV, 1, ada_cols)).reshape(1, N_DEV * ada_cols)
    sh_a, sc_a, g_a, sh_m, sc_m, g_m = [mod_row[:, D_MODEL * i:D_MODEL * (i + 1)] for i in range(6)]

    w_in_ext = _ext_in(_cols_from_slabs(g_in))
    z, (g_q, g_kv, g_out) = _matmul(xs, w_in_ext, "NN", "in_proj", a_fn=_modulate, extras=(sc_a, sh_a),
                                    exchange=_Exchange([bf["w_q_up"], bf["w_kv_up"], bf["w_out"]], False))
    wq_ext, wkv_ext = _ext_q(_cols_from_slabs(g_q)), _ext_kv(_cols_from_slabs(g_kv))
    w_out_full = g_out.reshape(D_MODEL, D_MODEL)
    inv_freq = 1.0 / (ROPE_THETA ** (jnp.arange(0, ROPE_DIM, 2, dtype=F32) / ROPE_DIM))
    zeros = lambda n: jnp.zeros((n,), F32)
    invf = jnp.concatenate([zeros(128), inv_freq, inv_freq, zeros(64)]).reshape(1, QK_PAD)
    m_one = jnp.concatenate([jnp.ones((128,), F32), zeros(128)]).reshape(1, QK_PAD)
    m_rot = jnp.concatenate([zeros(128), jnp.ones((64,), F32), zeros(64)]).reshape(1, QK_PAD)
    q, k, v, c1, s1, cqn, ckvn = _mla_pre(z, positions.reshape(T, 1), invf, m_one, m_rot, wq_ext, wkv_ext,
                                          mla_q_norm_w, mla_kv_norm_w)
    (o_raw, o_gated, s_prev), (w1,) = _hgrn_fwd(z, hg_lower_bounds, hg_norm_w,
                                                exchange=_Exchange([bf["w_mlp_in"]], False))
    (o_mla, lse), (w2,) = _attn_fwd(q, k, v, exchange=_Exchange([bf["w_mlp_out"]], False))
    mixcat = jnp.concatenate([o_gated, o_mla.astype(BF16)], axis=1)
    mix, xhat1, rstd1, u2 = _mix_ln1(mixcat, w_out_full, xs, g_a, ln1_g, ln1_b, sc_m, sh_m)
    r, dr2, dh, dln2_g, dln2_b, dg_m, loss_part = _mlp_fwd(u2, w1, w2, xhat1, ln1_g, ln1_b, g_m, ln2_g, ln2_b, tgt)

    dhpre, dr1, dmix, dsc_m, dsh_m, dln1_g, dln1_b, dg_a = _mlp_bwd(dh, w1, w2, r, dr2, xhat1, rstd1, mix, ln1_g,
                                                                    ln1_b, sc_m, g_a)
    received = {}
    dw2 = _matmul(r, dh, "TN", "wgrad_mlp_out", out_dtype=BF16, a_fn=_square, tm=1024)
    dw1 = _matmul(u2, dhpre, "TN", "wgrad_mlp_in", out_dtype=BF16, tm=1024, out_slabs=N_DEV)
    dmixcat = _matmul(dmix, w_out_full, "NT", "dgrad_out")
    dw_out = _matmul(mixcat, dmix, "TN", "wgrad_out", out_dtype=BF16, tm=1024)
    (dz_h, dlb, dnw), (received["w_mlp_out"],) = _hgrn_bwd(
        dmixcat, z, o_raw, s_prev, hg_lower_bounds, hg_norm_w,
        exchange=_Exchange([dw2.reshape(N_DEV, dw2.shape[0] // N_DEV, D_MODEL)], True))
    (dq, dk, dv), (received["w_mlp_in"], received["w_out"]) = _attn_bwd(
        q, k, v, dmixcat, o_mla, lse,
        exchange=_Exchange([dw1, dw_out.reshape(N_DEV, D_MODEL // N_DEV, D_MODEL)], True))
    dz_m, dq_ext, dkv_ext, dqnw, dkvnw = _mla_bwd(dq, dk, dv, z, c1, s1, wq_ext, wkv_ext, mla_q_norm_w,
                                                   mla_kv_norm_w)
    dwq_ext = _matmul(cqn, dq_ext, "TN", "wgrad_q_up", tn=2048)
    dwkv_ext = _matmul(ckvn, dkv_ext, "TN", "wgrad_kv_up", tn=1536)
    dw_in_h = _matmul(xs, dz_h, "TN", "wgrad_in_h", a_fn=_modulate, extras=(sc_a, sh_a), tm=1024)
    dw_in_m = _matmul(xs, dz_m, "TN", "wgrad_in_m", a_fn=_modulate, extras=(sc_a, sh_a), tm=1024)
    dw_in, dwq, dwkv = _grads_from_ext(dw_in_h, dw_in_m, dwq_ext, dwkv_ext)
    last = [_slabs_from_cols(g).astype(BF16) for g in (dw_in, dwq, dwkv)]
    (grad_x, dsc_a, dsh_a), (received["w_in"], received["w_q_up"], received["w_kv_up"]) = _input_bwd(
        dz_h, dz_m, w_in_ext, xs, dr1, sc_a, exchange=_Exchange(last, True))

    small = jnp.concatenate([dsh_a, dsc_a, dg_a, dsh_m, dsc_m, dg_m, dlb, dnw, dqnw, dkvnw, dln1_g, dln1_b, dln2_g,
                             dln2_b, loss_part], axis=1)
    (small_all,) = _exchange([small], scatter=False, name="gather_small")
    small_sum, glb = _small_reduce(small_all, hg_lower_bounds)

    moments = dict(w_in=(m_w_in, v_w_in), w_q_up=(m_w_q_up, v_w_q_up), w_kv_up=(m_w_kv_up, v_w_kv_up),
                   w_out=(m_w_out, v_w_out), w_mlp_in=(m_w_mlp_in, v_w_mlp_in), w_mlp_out=(m_w_mlp_out, v_w_mlp_out))
    res = {}
    for n in names:
        res[n] = _adam(received[n], big[n], moments[n][0][0], moments[n][1][0], name="adam_" + n)
    dmod_cols = lax.dynamic_slice(small_all.reshape(N_DEV, SMALL_W), (0, me * ada_cols), (N_DEV, ada_cols))
    cond_t = cond.T

    def ada_grad(ct_ref, dm_ref):
        g = ct_ref[:, 0:1] * dm_ref[0:1, :]
        for b in range(1, N_DEV):
            g = g + ct_ref[:, b:b + 1] * dm_ref[b:b + 1, :]
        return g

    res["w_ada"] = _adam(None, w_ada[0], m_w_ada[0], v_w_ada[0], name="adam_w_ada", g_fn=ada_grad,
                         g_extra=(cond_t, dmod_cols))

    seg = lambda a, b: small_sum[:, a:b]
    small_params = [("b_ada", b_ada, m_b_ada, v_b_ada, seg(0, 6144)),
                    ("hg_lower_bounds", hg_lower_bounds, m_hg_lower_bounds, v_hg_lower_bounds, glb),
                    ("hg_norm_w", hg_norm_w, m_hg_norm_w, v_hg_norm_w, seg(6656, 7168)),
                    ("mla_q_norm_w", mla_q_norm_w, m_mla_q_norm_w, v_mla_q_norm_w, seg(7168, 7424)),
                    ("mla_kv_norm_w", mla_kv_norm_w, m_mla_kv_norm_w, v_mla_kv_norm_w, seg(7424, 7680)),
                    ("ln1_g", ln1_g, m_ln1_g, v_ln1_g, seg(7680, 8704)), ("ln1_b", ln1_b, m_ln1_b, v_ln1_b, seg(8704, 9728)),
                    ("ln2_g", ln2_g, m_ln2_g, v_ln2_g, seg(9728, 10752)), ("ln2_b", ln2_b, m_ln2_b, v_ln2_b, seg(10752, 11776))]
    pack = lambda i: jnp.concatenate([p[i].reshape(1, -1) for p in small_params], axis=1)
    packed = _adam(pack(4)[None], pack(1), pack(2), pack(3), name="adam_small")
    off = 0
    for n, w, _, _, _ in small_params:
        res[n] = tuple(a[:, off:off + w.size].reshape(w.shape) for a in packed)
        off += w.size
    loss = small_sum[0, 11776]

    order = ["w_ada", "b_ada", "w_in", "hg_lower_bounds", "hg_norm_w", "mla_q_norm_w", "w_q_up", "mla_kv_norm_w",
             "w_kv_up", "w_out", "ln1_g", "ln1_b", "w_mlp_in", "w_mlp_out", "ln2_g", "ln2_b"]
    shaped = {n: tuple(a.reshape((1,) + a.shape) if n in big or n == "w_ada" else a for a in res[n]) for n in order}
    outs = [loss, grad_x.reshape(1, T, D_MODEL)]
    for i in range(4):
        outs += [shaped[n][i] for n in order]
    return tuple(outs)
```

```python
import functools

import jax
import jax.numpy as jnp
import numpy as np
from jax import lax
from jax.experimental import pallas as pl
from jax.experimental.pallas import tpu as pltpu

F32, BF16 = jnp.float32, jnp.bfloat16
N_DEV = 8
D_MODEL = 1024
HEADS = 4
HEAD_DIM = 128
ROPE_DIM = 64
QK_PAD = 256
CHUNK = 64
ROPE_THETA = 10000.0
RMS_EPS = 1e-6
LN_EPS = 1e-5
ALPHA = 2.0 ** 0.25
ATT_SCALE = (HEAD_DIM + ROPE_DIM) ** -0.5
ADAM_LR, ADAM_B1, ADAM_B2, ADAM_EPS, ADAM_WD, ADAM_STEP = 0.001, 0.9, 0.999, 1e-08, 0.01, 10
NEG_BIG = -1e30

ROW_TILE = 512
ROW_TILE_SMALL = 256
ATT_TILE = 512
HGRN_GROUP = 8
VMEM_LIMIT = 56 * 2 ** 20

NN = (((1,), (0,)), ((), ()))
NT = (((1,), (1,)), ((), ()))
TN = (((0,), (0,)), ((), ()))


def _dot(a, b, dims=NN):
    return lax.dot_general(a, b, dims, preferred_element_type=F32)


def _bdot(a, b, dims=NN):
    return lax.dot_general(a.astype(BF16), b.astype(BF16), dims, preferred_element_type=F32)


def _hdot(a, b, dims=NN):
    return lax.dot_general(a, b, dims, precision=lax.Precision.HIGHEST, preferred_element_type=F32)


def _params():
    return pltpu.CompilerParams(vmem_limit_bytes=VMEM_LIMIT)


def _sigmoid(x):
    return 1.0 / (1.0 + jnp.exp(-x))


def _rowsum(x):
    return jnp.sum(x, axis=0, keepdims=True)


def _lanemean(x):
    return jnp.mean(x, axis=-1, keepdims=True)


def _full(shape):
    nd = len(shape)
    return pl.BlockSpec(shape, lambda *_: (0,) * nd)


_GROUPS = {
    "all": ((1, 2, 3, 4, 5, 6, 7), lambda x, y, c: 4 * x + 2 * y + c),
    "chips": ((2, 4, 6), lambda x, y, c: 2 * x + y),
    "pair": ((1,), lambda x, y, c: c),
}


class _Exchange:
    def __init__(self, arrs, scatter, group="all"):
        self.arrs, self.scatter, self.n = list(arrs), scatter, len(arrs)
        self.flips, self.slot = _GROUPS[group]
        size = len(self.flips) + 1
        self.out_shape = [jax.ShapeDtypeStruct((size,) + (a.shape[1:] if scatter else a.shape), a.dtype)
                          for a in self.arrs]
        n = self.n
        self.scratch = [pltpu.SemaphoreType.DMA((n, size - 1)), pltpu.SemaphoreType.DMA((n, size - 1)),
                        pltpu.SemaphoreType.DMA((n,))]

    def _copies(self, ins, outs, sems):
        send_sems, recv_sems, loc_sems = sems
        x, y, c = lax.axis_index("x"), lax.axis_index("y"), lax.axis_index("c")
        me = self.slot(x, y, c)
        copies = []
        for k in range(self.n):
            src_of = (lambda i, k=k: ins[k].at[i]) if self.scatter else (lambda i, k=k: ins[k])
            copies.append((pltpu.make_async_copy(src_of(me), outs[k].at[me], loc_sems.at[k]), None))
            for j, p in enumerate(self.flips):
                px = (1 - x) if p & 4 else x
                py = (1 - y) if p & 2 else y
                pc = (1 - c) if p & 1 else c
                peer = self.slot(px, py, pc)
                both = dict(send_sem=send_sems.at[k, j], recv_sem=recv_sems.at[k, j],
                            device_id=(px, py, pc), device_id_type=pl.DeviceIdType.MESH)
                send = pltpu.make_async_remote_copy(src_ref=src_of(peer), dst_ref=outs[k].at[me], **both)
                recv = pltpu.make_async_remote_copy(src_ref=src_of(peer), dst_ref=outs[k].at[peer], **both)
                copies.append((send, recv))
        return copies

    def start(self, ins, outs, sems):
        for first, _ in self._copies(ins, outs, sems):
            first.start()

    def wait(self, ins, outs, sems):
        for first, recv in self._copies(ins, outs, sems):
            if recv is None:
                first.wait()
            else:
                recv.wait_recv()
                first.wait_send()


def _call(body, name, args, out_shape, grid=(), in_specs=(), out_specs=(), scratch_shapes=(), exchange=None):
    if exchange is None:
        return pl.pallas_call(body, name=name, grid=grid, in_specs=list(in_specs), out_specs=list(out_specs),
                              out_shape=list(out_shape), scratch_shapes=list(scratch_shapes),
                              compiler_params=_params())(*args), None
    exs = list(exchange) if isinstance(exchange, (list, tuple)) else [exchange]
    ni, no, ns, nx = len(args), len(out_shape), len(scratch_shapes), sum(e.n for e in exs)

    def wrapped(*refs):
        a, xi = refs[:ni], refs[ni:ni + nx]
        o, xo = refs[ni + nx:ni + nx + no], refs[ni + nx + no:ni + 2 * nx + no]
        s, xs = refs[ni + 2 * nx + no:ni + 2 * nx + no + ns], refs[ni + 2 * nx + no + ns:]
        parts, at = [], 0
        for j, e in enumerate(exs):
            parts.append((e, xi[at:at + e.n], xo[at:at + e.n], xs[3 * j:3 * j + 3]))
            at += e.n
        first = last = None
        for d, g in enumerate(grid):
            f, l = pl.program_id(d) == 0, pl.program_id(d) == g - 1
            first, last = (f, l) if first is None else (first & f, last & l)

        @pl.when(first)
        def _():
            for e, ins, outs, sems in parts:
                e.start(ins, outs, sems)

        body(*a, *o, *s)

        @pl.when(last)
        def _():
            for e, ins, outs, sems in parts:
                e.wait(ins, outs, sems)

    hbm = pl.BlockSpec(memory_space=pltpu.HBM)
    res = pl.pallas_call(
        wrapped, name=name, grid=grid, in_specs=list(in_specs) + [hbm] * nx, out_specs=list(out_specs) + [hbm] * nx,
        out_shape=list(out_shape) + [o_ for e in exs for o_ in e.out_shape],
        scratch_shapes=list(scratch_shapes) + [s_ for e in exs for s_ in e.scratch],
        compiler_params=_params())(*args, *[a_ for e in exs for a_ in e.arrs])
    return res[:no], res[no:]


def _exchange(arrs, scatter, name):
    ex = _Exchange(arrs, scatter)

    def body(*refs):
        ins, outs, sems = refs[:ex.n], refs[ex.n:2 * ex.n], refs[2 * ex.n:]
        ex.start(ins, outs, sems)
        ex.wait(ins, outs, sems)

    hbm = pl.BlockSpec(memory_space=pltpu.HBM)
    return pl.pallas_call(body, name=name, out_shape=ex.out_shape, in_specs=[hbm] * ex.n, out_specs=[hbm] * ex.n,
                          scratch_shapes=ex.scratch)(*ex.arrs)


def _gather_two_level(arrs, name):
    n = len(arrs)
    ex1 = _Exchange(arrs, False, "chips")
    ex2 = _Exchange(ex1.out_shape, False, "pair")

    def body(*refs):
        ins, mid, outs, sems = refs[:n], refs[n:2 * n], refs[2 * n:3 * n], refs[3 * n:]
        ex1.start(ins, mid, sems[:3])
        ex1.wait(ins, mid, sems[:3])
        ex2.start(mid, outs, sems[3:])
        ex2.wait(mid, outs, sems[3:])

    hbm = pl.BlockSpec(memory_space=pltpu.HBM)
    res = pl.pallas_call(body, name=name, out_shape=ex1.out_shape + ex2.out_shape, in_specs=[hbm] * n,
                         out_specs=[hbm] * (2 * n), scratch_shapes=ex1.scratch + ex2.scratch)(*arrs)
    return res[n:]


def _matmul(a, b, mode, name, out_dtype=F32, tm=512, tn=1024, tk=1024, a_fn=None, extras=(), out_slabs=None,
            exchange=None):
    if mode == "NN":
        (M, K), N = a.shape, b.shape[1]
    elif mode == "NT":
        (M, K), N = a.shape, b.shape[0]
    else:
        (K, M), N = a.shape, b.shape[1]
    if out_slabs:
        tn = N // out_slabs
    tm, tn, tk = min(tm, M), min(tn, N), min(tk, K)
    assert M % tm == 0 and N % tn == 0 and K % tk == 0, (name, M, N, K)
    nk = K // tk
    dims = {"NN": NN, "NT": NT, "TN": TN}[mode]
    ne = len(extras)

    def body(a_ref, b_ref, *rest):
        e_refs, o_ref, acc_ref = rest[:ne], rest[ne], rest[ne + 1]
        k = pl.program_id(2)

        @pl.when(k == 0)
        def _():
            acc_ref[...] = jnp.zeros_like(acc_ref)

        at = a_ref[...]
        if a_fn is not None:
            at = a_fn(at.astype(F32), *[e[...] for e in e_refs])
        acc_ref[...] += _bdot(at, b_ref[...], dims)

        @pl.when(k == nk - 1)
        def _():
            o_ref[...] = acc_ref[...].astype(out_dtype)

    if mode == "TN":
        a_spec = pl.BlockSpec((tk, tm), lambda i, j, k: (k, i))
        e_spec = pl.BlockSpec((1, tm), lambda i, j, k: (0, i))
    else:
        a_spec = pl.BlockSpec((tm, tk), lambda i, j, k: (i, k))
        e_spec = pl.BlockSpec((1, tk), lambda i, j, k: (0, k))
    if mode == "NT":
        b_spec = pl.BlockSpec((tn, tk), lambda i, j, k: (j, k))
    else:
        b_spec = pl.BlockSpec((tk, tn), lambda i, j, k: (k, j))
    if out_slabs:
        o_shape = jax.ShapeDtypeStruct((out_slabs, M, tn), out_dtype)
        o_spec = pl.BlockSpec((None, tm, tn), lambda i, j, k: (j, i, 0))
    else:
        o_shape = jax.ShapeDtypeStruct((M, N), out_dtype)
        o_spec = pl.BlockSpec((tm, tn), lambda i, j, k: (i, j))
    (out,), got = _call(body, name, (a, b, *extras), [o_shape], grid=(M // tm, N // tn, nk),
                        in_specs=[a_spec, b_spec] + [e_spec] * ne, out_specs=[o_spec],
                        scratch_shapes=[pltpu.VMEM((tm, tn), F32)], exchange=exchange)
    return out if exchange is None else (out, got)


def _modulate(x, sc, sh):
    return x * (1.0 + sc) + sh


def _square(x):
    return x * x


def _mod_part(c_all, w_ada_s, b_s):
    def body(c_ref, w_ref, b_ref, mod_ref, cond_ref):
        cv = c_ref[...]
        cond = cv * _sigmoid(cv)
        cond_ref[...] = cond
        mod_ref[...] = _bdot(cond, w_ref[...]) + b_ref[...]

    return pl.pallas_call(
        body, name="mod_part",
        out_shape=[jax.ShapeDtypeStruct((N_DEV, w_ada_s.shape[1]), F32), jax.ShapeDtypeStruct(c_all.shape, F32)],
        compiler_params=_params(),
    )(c_all, w_ada_s, b_s)


def _rms_fwd(x, w):
    rs = lax.rsqrt(_lanemean(x * x) + RMS_EPS)
    return x * rs * w, rs


def _rms_bwd(x, rs, w, dy):
    xhat = x * rs
    dxh = dy * w
    return rs * (dxh - xhat * _lanemean(dxh * xhat)), dy * xhat


def _mla_pre(z, pos_col, invf, m_one, m_rot, wq_ext, wkv_ext, qnw, kvnw):
    T = z.shape[0]
    tm = min(ROW_TILE, T)

    def body(z_ref, pos_ref, invf_ref, mone_ref, mrot_ref, wq_ref, wkv_ref, qnw_ref, kvnw_ref,
             q_ref, k_ref, v_ref, c1_ref, s1_ref, cqn_ref, ckvn_ref):
        ang = pos_ref[...].astype(F32) * invf_ref[...]
        c1 = mone_ref[...] + mrot_ref[...] * jnp.cos(ang)
        s1 = mrot_ref[...] * jnp.sin(ang)
        c1_ref[...] = c1
        s1_ref[...] = s1
        cqn, _ = _rms_fwd(z_ref[:, 0:256], qnw_ref[...])
        ckvn, _ = _rms_fwd(z_ref[:, 256:512], kvnw_ref[...])
        cqn_ref[...] = cqn.astype(BF16)
        ckvn_ref[...] = ckvn.astype(BF16)
        qe = _bdot(cqn, wq_ref[...])
        kve = _bdot(ckvn, wkv_ref[...])
        k_rope = z_ref[:, 512:768] * c1 + z_ref[:, 768:1024] * s1
        for h in range(HEADS):
            q_ref[h] = (qe[:, 256 * h:256 * h + 256] * c1 + qe[:, 1024 + 256 * h:1280 + 256 * h] * s1).astype(BF16)
            k_ref[h] = (kve[:, 256 * h:256 * h + 256] + k_rope).astype(BF16)
            v_ref[h] = kve[:, 1024 + 128 * h:1152 + 128 * h].astype(BF16)

    row = lambda i: (i, 0)
    head = lambda i: (0, i, 0)
    return pl.pallas_call(
        body, name="mla_pre", grid=(T // tm,),
        in_specs=[pl.BlockSpec((tm, 1024), lambda i: (i, 2)), pl.BlockSpec((tm, 1), row),
                  _full((1, 256)), _full((1, 256)), _full((1, 256)), _full(wq_ext.shape), _full(wkv_ext.shape),
                  _full((1, 256)), _full((1, 256))],
        out_specs=[pl.BlockSpec((HEADS, tm, QK_PAD), head), pl.BlockSpec((HEADS, tm, QK_PAD), head),
                   pl.BlockSpec((HEADS, tm, HEAD_DIM), head), pl.BlockSpec((tm, 256), row), pl.BlockSpec((tm, 256), row),
                   pl.BlockSpec((tm, 256), row), pl.BlockSpec((tm, 256), row)],
        out_shape=[jax.ShapeDtypeStruct((HEADS, T, QK_PAD), BF16), jax.ShapeDtypeStruct((HEADS, T, QK_PAD), BF16),
                   jax.ShapeDtypeStruct((HEADS, T, HEAD_DIM), BF16), jax.ShapeDtypeStruct((T, 256), F32),
                   jax.ShapeDtypeStruct((T, 256), F32), jax.ShapeDtypeStruct((T, 256), BF16),
                   jax.ShapeDtypeStruct((T, 256), BF16)],
        compiler_params=_params(),
    )(z, pos_col, invf, m_one, m_rot, wq_ext, wkv_ext, qnw, kvnw)


def _mla_bwd(dq, dk, dv, z, c1, s1, wq_ext, wkv_ext, qnw, kvnw):
    T = z.shape[0]
    tm = min(ROW_TILE_SMALL, T)

    def body(dq_ref, dk_ref, dv_ref, z_ref, c1_ref, s1_ref, wq_ref, wkv_ref, qnw_ref, kvnw_ref,
             dz_ref, dqe_ref, dkve_ref, dqnw_ref, dkvnw_ref):
        @pl.when(pl.program_id(0) == 0)
        def _():
            dqnw_ref[...] = jnp.zeros_like(dqnw_ref)
            dkvnw_ref[...] = jnp.zeros_like(dkvnw_ref)

        c1, s1 = c1_ref[...], s1_ref[...]
        dkpe = jnp.zeros((tm, QK_PAD), F32)
        for h in range(HEADS):
            dqh, dkh = dq_ref[h], dk_ref[h]
            dqe_ref[:, 256 * h:256 * h + 256] = (dqh * c1).astype(BF16)
            dqe_ref[:, 1024 + 256 * h:1280 + 256 * h] = (dqh * s1).astype(BF16)
            dkve_ref[:, 256 * h:256 * h + 256] = dkh.astype(BF16)
            dkve_ref[:, 1024 + 128 * h:1152 + 128 * h] = dv_ref[h].astype(BF16)
            dkpe = dkpe + dkh
        dcqn = _dot(dqe_ref[...], wq_ref[...], NT)
        dckvn = _dot(dkve_ref[...], wkv_ref[...], NT)
        cq, ckv = z_ref[:, 0:256], z_ref[:, 256:512]
        _, rsq = _rms_fwd(cq, qnw_ref[...])
        _, rskv = _rms_fwd(ckv, kvnw_ref[...])
        dcq, wq_rows = _rms_bwd(cq, rsq, qnw_ref[...], dcqn)
        dckv, wkv_rows = _rms_bwd(ckv, rskv, kvnw_ref[...], dckvn)
        dqnw_ref[...] += _rowsum(wq_rows)
        dkvnw_ref[...] += _rowsum(wkv_rows)
        dz_ref[:, 0:256] = dcq
        dz_ref[:, 256:512] = dckv
        dz_ref[:, 512:768] = dkpe * c1
        dz_ref[:, 768:1024] = dkpe * s1

    row = lambda i: (i, 0)
    head = lambda i: (0, i, 0)
    return pl.pallas_call(
        body, name="mla_bwd", grid=(T // tm,),
        in_specs=[pl.BlockSpec((HEADS, tm, QK_PAD), head), pl.BlockSpec((HEADS, tm, QK_PAD), head),
                  pl.BlockSpec((HEADS, tm, HEAD_DIM), head), pl.BlockSpec((tm, 1024), lambda i: (i, 2)),
                  pl.BlockSpec((tm, 256), row), pl.BlockSpec((tm, 256), row), _full(wq_ext.shape), _full(wkv_ext.shape),
                  _full((1, 256)), _full((1, 256))],
        out_specs=[pl.BlockSpec((tm, 1024), row), pl.BlockSpec((tm, 2048), row), pl.BlockSpec((tm, 1536), row),
                   _full((1, 256)), _full((1, 256))],
        out_shape=[jax.ShapeDtypeStruct((T, 1024), F32), jax.ShapeDtypeStruct((T, 2048), BF16),
                   jax.ShapeDtypeStruct((T, 1536), BF16), jax.ShapeDtypeStruct((1, 256), F32),
                   jax.ShapeDtypeStruct((1, 256), F32)],
        compiler_params=_params(),
    )(dq, dk, dv, z, c1, s1, wq_ext, wkv_ext, qnw, kvnw)


_HEAD_LANES = [slice(HEAD_DIM * h, HEAD_DIM * (h + 1)) for h in range(HEADS)]


def _lower_bound(lbraw_ref):
    a0, a1 = lbraw_ref[0:1, :], lbraw_ref[1:2, :]
    mx = jnp.maximum(a0, a1)
    e0, e1 = jnp.exp(a0 - mx), jnp.exp(a1 - mx)
    return e0 / (e0 + e1)


def _tri(lower):
    r = lax.broadcasted_iota(jnp.int32, (CHUNK, CHUNK), 0)
    c = lax.broadcasted_iota(jnp.int32, (CHUNK, CHUNK), 1)
    return (r >= c) if lower else (r <= c)


def _hgrn_gates(q, f, lb, tri_lo):
    sg = _sigmoid(f)
    forget = lb + (1.0 - lb) * sg
    k = 1.0 - forget
    b = _hdot(tri_lo.astype(F32), jnp.log(forget))
    b_ref, b_last = b[CHUNK // 2 - 1:CHUNK // 2, :], b[CHUNK - 1:CHUNK, :]
    e1, e2, e3, e4 = jnp.exp(b - b_ref), jnp.exp(b_ref - b), jnp.exp(b_last - b), jnp.exp(b)
    return dict(sg=sg, forget=forget, k=k, e1=e1, e2=e2, e3=e3, e4=e4, qa=q * e1, ka=k * e2, kl=k * e3, qb=q * e4,
                decay=jnp.exp(b_last))


def _hgrn_fwd(z, lbraw, nw, exchange=None):
    T = z.shape[0]
    G = min(HGRN_GROUP, T // CHUNK)
    rows = G * CHUNK
    n_chunks = T // CHUNK

    def body(q_ref, f_ref, i_ref, g_ref, lbraw_ref, nw_ref, oraw_ref, og_ref, sp_ref, st_ref):
        @pl.when(pl.program_id(0) == 0)
        def _():
            st_ref[...] = jnp.zeros_like(st_ref)

        lb_all = _lower_bound(lbraw_ref)
        tri_lo = _tri(True)

        def chunk(cc, carry):
            rs = pl.ds(pl.multiple_of(cc * CHUNK, CHUNK), CHUNK)
            t = _hgrn_gates(q_ref[rs, :], f_ref[rs, :], lb_all, tri_lo)
            v, gate = i_ref[rs, :], g_ref[rs, :]
            st = [st_ref[h] for h in range(HEADS)]
            a = [jnp.where(tri_lo, _bdot(t["qa"][:, s], t["ka"][:, s], NT), 0.0) for s in _HEAD_LANES]
            kv = [_bdot(v[:, s], t["kl"][:, s], TN) for s in _HEAD_LANES]
            o = [_bdot(a[h], v[:, s]) + _bdot(t["qb"][:, s], st[h], NT) for h, s in enumerate(_HEAD_LANES)]
            for h, s in enumerate(_HEAD_LANES):
                sp_ref[cc, h] = st[h]
                st_ref[h] = st[h] * t["decay"][:, s] + kv[h]
            oraw_ref[rs, :] = jnp.concatenate(o, axis=1)
            on = jnp.concatenate([_rms_fwd(o[h], nw_ref[:, s])[0] for h, s in enumerate(_HEAD_LANES)], axis=1)
            og_ref[rs, :] = (on * (gate * _sigmoid(gate))).astype(BF16)
            return carry

        lax.fori_loop(0, G, chunk, 0)

    col = lambda j: pl.BlockSpec((rows, 512), lambda r, j=j: (r, j))
    return _call(
        body, "hgrn_fwd", (z, z, z, z, lbraw, nw), grid=(T // rows,),
        in_specs=[col(0), col(1), col(2), col(3), _full((2, 512)), _full((1, 512))],
        out_specs=[col(0), col(0), pl.BlockSpec((G, HEADS, HEAD_DIM, HEAD_DIM), lambda r: (r, 0, 0, 0))],
        out_shape=[jax.ShapeDtypeStruct((T, 512), F32), jax.ShapeDtypeStruct((T, 512), BF16),
                   jax.ShapeDtypeStruct((n_chunks, HEADS, HEAD_DIM, HEAD_DIM), F32)],
        scratch_shapes=[pltpu.VMEM((HEADS, HEAD_DIM, HEAD_DIM), F32)], exchange=exchange)


def _hgrn_bwd(dmixcat, z, oraw, sprev, lbraw, nw, exchange=None):
    T = z.shape[0]
    G = min(HGRN_GROUP, T // CHUNK)
    rows = G * CHUNK
    ng = T // rows

    def body(dog_ref, q_ref, f_ref, i_ref, g_ref, oraw_ref, sp_ref, lbraw_ref, nw_ref,
             dz_ref, dlb_ref, dnw_ref, dst_ref):
        @pl.when(pl.program_id(0) == 0)
        def _():
            dst_ref[...] = jnp.zeros_like(dst_ref)
            dlb_ref[...] = jnp.zeros_like(dlb_ref)
            dnw_ref[...] = jnp.zeros_like(dnw_ref)

        lb_all = _lower_bound(lbraw_ref)
        tri_lo, tri_up = _tri(True), _tri(False)
        rowid = lax.broadcasted_iota(jnp.int32, (CHUNK, HEADS * HEAD_DIM), 0)

        def chunk(it, carry):
            cc = G - 1 - it
            rs = pl.ds(pl.multiple_of(cc * CHUNK, CHUNK), CHUNK)
            heads = list(enumerate(_HEAD_LANES))
            cat = lambda parts: jnp.concatenate(parts, axis=1)
            per_head_mean = lambda x: cat([jnp.broadcast_to(_lanemean(x[:, s]), (CHUNK, HEAD_DIM)) for s in _HEAD_LANES])
            t = _hgrn_gates(q_ref[rs, :], f_ref[rs, :], lb_all, tri_lo)
            v, gate, o, dog, nw_all = i_ref[rs, :], g_ref[rs, :], oraw_ref[rs, :], dog_ref[rs, :], nw_ref[...]
            rs_o = lax.rsqrt(per_head_mean(o * o) + RMS_EPS)
            xhat = o * rs_o
            sgg = _sigmoid(gate)
            d_on = dog * (gate * sgg)
            dz_ref[rs, 1536:2048] = dog * (xhat * nw_all) * (sgg * (1.0 + gate * (1.0 - sgg)))
            dxh = d_on * nw_all
            do = rs_o * (dxh - xhat * per_head_mean(dxh * xhat))
            dnw_ref[...] += _rowsum(d_on * xhat)
            st = [sp_ref[cc, h] for h in range(HEADS)]
            dst = [dst_ref[h] for h in range(HEADS)]
            a = [jnp.where(tri_lo, _bdot(t["qa"][:, s], t["ka"][:, s], NT), 0.0) for s in _HEAD_LANES]
            da = [jnp.where(tri_lo, _bdot(do[:, s], v[:, s], NT), 0.0) for s in _HEAD_LANES]
            dqb = cat([_bdot(do[:, s], st[h]) for h, s in heads])
            dkl = cat([_bdot(v[:, s], dst[h]) for h, s in heads])
            dv_ = cat([_bdot(t["kl"][:, s], dst[h], NT) + _bdot(a[h], do[:, s], TN) for h, s in heads])
            dqa = cat([_bdot(da[h], t["ka"][:, s]) for h, s in heads])
            dka = cat([_bdot(da[h], t["qa"][:, s], TN) for h, s in heads])
            ddecay = cat([_rowsum(dst[h] * st[h]) for h in range(HEADS)])
            for h, s in heads:
                dst_ref[h] = dst[h] * t["decay"][:, s] + _bdot(do[:, s], t["qb"][:, s], TN)
            pa, pk, pb, pl_ = dqa * t["qa"], dka * t["ka"], dqb * t["qb"], dkl * t["kl"]
            db = pa - pk + pb - pl_
            db = db + jnp.where(rowid == CHUNK // 2 - 1, _rowsum(pk - pa), 0.0)
            db = db + jnp.where(rowid == CHUNK - 1, _rowsum(pl_) + ddecay * t["decay"], 0.0)
            dlogf = _hdot(tri_up.astype(F32), db)
            dforget = dlogf / t["forget"] - (dka * t["e2"] + dkl * t["e3"])
            sg = t["sg"]
            dz_ref[rs, 0:512] = dqa * t["e1"] + dqb * t["e4"]
            dz_ref[rs, 512:1024] = dforget * (1.0 - lb_all) * sg * (1.0 - sg)
            dz_ref[rs, 1024:1536] = dv_
            dlb_ref[...] += _rowsum(dforget * (1.0 - sg))
            return carry

        lax.fori_loop(0, G, chunk, 0)

    col = lambda j: pl.BlockSpec((rows, 512), lambda r, j=j: (ng - 1 - r, j))
    return _call(
        body, "hgrn_bwd", (dmixcat, z, z, z, z, oraw, sprev, lbraw, nw), grid=(ng,),
        in_specs=[col(0), col(0), col(1), col(2), col(3), col(0),
                  pl.BlockSpec((G, HEADS, HEAD_DIM, HEAD_DIM), lambda r: (ng - 1 - r, 0, 0, 0)),
                  _full((2, 512)), _full((1, 512))],
        out_specs=[pl.BlockSpec((rows, 2048), lambda r: (ng - 1 - r, 0)), _full((1, 512)), _full((1, 512))],
        out_shape=[jax.ShapeDtypeStruct((T, 2048), F32), jax.ShapeDtypeStruct((1, 512), F32),
                   jax.ShapeDtypeStruct((1, 512), F32)],
        scratch_shapes=[pltpu.VMEM((HEADS, HEAD_DIM, HEAD_DIM), F32)], exchange=exchange)


def _diag_mask(t):
    r = lax.broadcasted_iota(jnp.int32, (t, t), 0)
    c = lax.broadcasted_iota(jnp.int32, (t, t), 1)
    return r >= c


def _attn_fwd(q, k, v, exchange=None):
    _, T, _ = q.shape
    t = min(ATT_TILE, T)

    def body(q_ref, k_ref, v_ref, o_ref, lse_ref):
        i = pl.program_id(1)
        qb = q_ref[...]

        def step(j, carry, masked):
            m, l, acc = carry
            ks = pl.ds(pl.multiple_of(j * t, t), t)
            s = _dot(qb, k_ref[ks, :], NT) * ATT_SCALE
            if masked:
                s = jnp.where(_diag_mask(t), s, NEG_BIG)
            mn = jnp.maximum(m, jnp.max(s, axis=-1, keepdims=True))
            p = jnp.exp(s - mn)
            al = jnp.exp(m - mn)
            return mn, al * l + jnp.sum(p, axis=-1, keepdims=True), al * acc + _dot(p.astype(BF16), v_ref[ks, :])

        init = (jnp.full((t, 1), NEG_BIG, F32), jnp.zeros((t, 1), F32), jnp.zeros((t, HEAD_DIM), F32))
        carry = lax.fori_loop(0, i, lambda j, c: step(j, c, False), init)
        m, l, acc = step(i, carry, True)
        o_ref[...] = acc / l
        lse_ref[...] = jnp.broadcast_to(m + jnp.log(l), (t, HEAD_DIM))

    return _call(
        body, "attn_fwd", (q, k, v), grid=(HEADS, T // t),
        in_specs=[pl.BlockSpec((None, t, QK_PAD), lambda h, i: (h, i, 0)),
                  pl.BlockSpec((None, T, QK_PAD), lambda h, i: (h, 0, 0)),
                  pl.BlockSpec((None, T, HEAD_DIM), lambda h, i: (h, 0, 0))],
        out_specs=[pl.BlockSpec((t, HEAD_DIM), lambda h, i: (i, h)),
                   pl.BlockSpec((None, t, HEAD_DIM), lambda h, i: (h, i, 0))],
        out_shape=[jax.ShapeDtypeStruct((T, HEADS * HEAD_DIM), F32), jax.ShapeDtypeStruct((HEADS, T, HEAD_DIM), F32)],
        exchange=exchange)


def _attn_bwd(q, k, v, dmixcat, o, lse, exchange=None):
    _, T, _ = q.shape
    t = min(ATT_TILE, T)
    nq = T // t

    def body(q_ref, k_ref, v_ref, do_ref, o_ref, lse_ref, dq_ref, dk_ref, dv_ref, delta_ref):
        j = pl.program_id(1)

        @pl.when(j == 0)
        def _():
            dq_ref[...] = jnp.zeros_like(dq_ref)

            def fill(i, carry):
                rs = pl.ds(pl.multiple_of(i * t, t), t)
                delta_ref[rs, :] = jnp.broadcast_to(
                    jnp.sum(do_ref[rs, :] * o_ref[rs, :], axis=-1, keepdims=True), (t, HEAD_DIM))
                return carry

            lax.fori_loop(0, nq, fill, 0)

        kb, vb = k_ref[...], v_ref[...]

        def step(i, carry, masked):
            dk, dv = carry
            rs = pl.ds(pl.multiple_of(i * t, t), t)
            qb, dob = q_ref[rs, :], do_ref[rs, :].astype(BF16)
            s = _dot(qb, kb, NT) * ATT_SCALE
            p = jnp.exp(s - lse_ref[rs, 0:1])
            if masked:
                p = jnp.where(_diag_mask(t), p, 0.0)
            dp = _dot(dob, vb, NT)
            ds = (p * (dp - delta_ref[rs, 0:1]) * ATT_SCALE).astype(BF16)
            dq_ref[rs, :] += _dot(ds, kb)
            return dk + _dot(ds, qb, TN), dv + _dot(p.astype(BF16), dob, TN)

        carry = step(j, (jnp.zeros((t, QK_PAD), F32), jnp.zeros((t, HEAD_DIM), F32)), True)
        dk, dv = lax.fori_loop(j + 1, nq, lambda i, c: step(i, c, False), carry)
        dk_ref[...] = dk
        dv_ref[...] = dv

    return _call(
        body, "attn_bwd", (q, k, v, dmixcat, o, lse), grid=(HEADS, nq),
        in_specs=[pl.BlockSpec((None, T, QK_PAD), lambda h, j: (h, 0, 0)),
                  pl.BlockSpec((None, t, QK_PAD), lambda h, j: (h, j, 0)),
                  pl.BlockSpec((None, t, HEAD_DIM), lambda h, j: (h, j, 0)),
                  pl.BlockSpec((T, HEAD_DIM), lambda h, j: (0, HEADS + h)),
                  pl.BlockSpec((T, HEAD_DIM), lambda h, j: (0, h)),
                  pl.BlockSpec((None, T, HEAD_DIM), lambda h, j: (h, 0, 0))],
        out_specs=[pl.BlockSpec((None, T, QK_PAD), lambda h, j: (h, 0, 0)),
                   pl.BlockSpec((None, t, QK_PAD), lambda h, j: (h, j, 0)),
                   pl.BlockSpec((None, t, HEAD_DIM), lambda h, j: (h, j, 0))],
        out_shape=[jax.ShapeDtypeStruct((HEADS, T, QK_PAD), F32), jax.ShapeDtypeStruct((HEADS, T, QK_PAD), F32),
                   jax.ShapeDtypeStruct((HEADS, T, HEAD_DIM), F32)],
        scratch_shapes=[pltpu.VMEM((T, HEAD_DIM), F32)], exchange=exchange)


def _ln_fwd(r):
    mu = _lanemean(r)
    xc = r - mu
    rstd = lax.rsqrt(_lanemean(xc * xc) + LN_EPS)
    return xc * rstd, rstd


def _ln_bwd(dxh, xhat, rstd):
    return rstd * (dxh - _lanemean(dxh) - xhat * _lanemean(dxh * xhat))


def _mix_ln1(mixcat, w_out, x, g_a, ln1_g, ln1_b, sc_m, sh_m, exchange=None):
    T = x.shape[0]
    tm = min(ROW_TILE_SMALL, T)

    def body(mc_ref, w_ref, x_ref, ga_ref, g_ref, b_ref, sc_ref, sh_ref, mix_ref, xhat_ref, rstd_ref, u2_ref):
        mix = _dot(mc_ref[...], w_ref[...])
        mix_ref[...] = mix
        xhat, rstd = _ln_fwd(ALPHA * x_ref[...] + (1.0 + ga_ref[...]) * mix)
        xhat_ref[...] = xhat
        rstd_ref[...] = jnp.broadcast_to(rstd, (tm, 128))
        u2_ref[...] = _modulate(xhat * g_ref[...] + b_ref[...], sc_ref[...], sh_ref[...]).astype(BF16)

    row = pl.BlockSpec((tm, D_MODEL), lambda i: (i, 0))
    vec = _full((1, D_MODEL))
    return _call(
        body, "mix_ln1", (mixcat, w_out, x, g_a, ln1_g, ln1_b, sc_m, sh_m), grid=(T // tm,),
        in_specs=[row, _full(w_out.shape), row, vec, vec, vec, vec, vec],
        out_specs=[row, row, pl.BlockSpec((tm, 128), lambda i: (i, 0)), row],
        out_shape=[jax.ShapeDtypeStruct((T, D_MODEL), F32), jax.ShapeDtypeStruct((T, D_MODEL), F32),
                   jax.ShapeDtypeStruct((T, 128), F32), jax.ShapeDtypeStruct((T, D_MODEL), BF16)],
        exchange=exchange)


def _mlp_fwd(u2, w1, w2, xhat1, ln1_g, ln1_b, g_m, ln2_g, ln2_b, target):
    T = u2.shape[0]
    nf, tf = N_DEV, w1.shape[-1]
    tm = min(ROW_TILE, T)

    def body(u2_ref, w1_ref, w2_ref, xhat_ref, g1_ref, b1_ref, gm_ref, g2_ref, b2_ref, tgt_ref,
             r_ref, dr2_ref, dh_ref, dg2_ref, db2_ref, dgm_ref, loss_ref, acc_ref):
        i, f = pl.program_id(0), pl.program_id(1)

        @pl.when((i == 0) & (f == 0))
        def _():
            for ref in (dg2_ref, db2_ref, dgm_ref, loss_ref):
                ref[...] = jnp.zeros_like(ref)

        @pl.when(f == 0)
        def _():
            acc_ref[...] = jnp.zeros_like(acc_ref)

        r = jnp.maximum(_dot(u2_ref[...], w1_ref[...]), 0.0)
        r_ref[...] = r.astype(BF16)
        acc_ref[...] += _bdot(r * r, w2_ref[...])

        @pl.when(f == nf - 1)
        def _():
            h = acc_ref[...]
            x1 = xhat_ref[...] * g1_ref[...] + b1_ref[...]
            xhat2, rstd2 = _ln_fwd(ALPHA * x1 + (1.0 + gm_ref[...]) * h)
            err = xhat2 * g2_ref[...] + b2_ref[...] - tgt_ref[...]
            loss_ref[...] += jnp.sum(0.5 * _lanemean(err * err), axis=0, keepdims=True)
            dy = err * (1.0 / D_MODEL)
            dg2_ref[...] += _rowsum(dy * xhat2)
            db2_ref[...] += _rowsum(dy)
            dr2 = _ln_bwd(dy * g2_ref[...], xhat2, rstd2)
            dr2_ref[...] = dr2
            dgm_ref[...] += _rowsum(dr2 * h)
            dh_ref[...] = ((1.0 + gm_ref[...]) * dr2).astype(BF16)

    row = pl.BlockSpec((tm, D_MODEL), lambda i, f: (i, 0))
    vec = _full((1, D_MODEL))
    return pl.pallas_call(
        body, name="mlp_fwd", grid=(T // tm, nf),
        in_specs=[row, pl.BlockSpec((None, None, D_MODEL, tf), lambda i, f: (f % 2, f // 2, 0, 0)),
                  pl.BlockSpec((None, None, tf, D_MODEL), lambda i, f: (f % 2, f // 2, 0, 0)),
                  row, vec, vec, vec, vec, vec, row],
        out_specs=[pl.BlockSpec((tm, tf), lambda i, f: (i, f)), row, row, vec, vec, vec, _full((1, 128))],
        out_shape=[jax.ShapeDtypeStruct((T, nf * tf), BF16), jax.ShapeDtypeStruct((T, D_MODEL), F32),
                   jax.ShapeDtypeStruct((T, D_MODEL), BF16), jax.ShapeDtypeStruct((1, D_MODEL), F32),
                   jax.ShapeDtypeStruct((1, D_MODEL), F32), jax.ShapeDtypeStruct((1, D_MODEL), F32),
                   jax.ShapeDtypeStruct((1, 128), F32)],
        scratch_shapes=[pltpu.VMEM((tm, D_MODEL), F32)],
        compiler_params=_params(),
    )(u2, w1, w2, xhat1, ln1_g, ln1_b, g_m, ln2_g, ln2_b, target)


def _mlp_bwd(dh, w1, w2, r, dr2, xhat1, rstd1, mix, ln1_g, ln1_b, sc_m, g_a):
    T = dh.shape[0]
    nf, tf = N_DEV, w1.shape[-1]
    tm = min(ROW_TILE, T)

    def body(dh_ref, w1_ref, w2_ref, r_ref, dr2_ref, xhat_ref, rstd_ref, mix_ref, g1_ref, b1_ref, sc_ref, ga_ref,
             dhpre_ref, dr1_ref, dmix_ref, dsc_ref, dsh_ref, dg1_ref, db1_ref, dga_ref, acc_ref):
        i, f = pl.program_id(0), pl.program_id(1)

        @pl.when((i == 0) & (f == 0))
        def _():
            for ref in (dsc_ref, dsh_ref, dg1_ref, db1_ref, dga_ref):
                ref[...] = jnp.zeros_like(ref)

        @pl.when(f == 0)
        def _():
            acc_ref[...] = jnp.zeros_like(acc_ref)

        dhpre = (_dot(dh_ref[...], w2_ref[...], NT) * (2.0 * r_ref[...].astype(F32))).astype(BF16)
        dhpre_ref[...] = dhpre
        acc_ref[...] += _dot(dhpre, w1_ref[...], NT)

        @pl.when(f == nf - 1)
        def _():
            du2 = acc_ref[...]
            xhat = xhat_ref[...]
            x1 = xhat * g1_ref[...] + b1_ref[...]
            dx1 = ALPHA * dr2_ref[...] + du2 * (1.0 + sc_ref[...])
            dsc_ref[...] += _rowsum(du2 * x1)
            dsh_ref[...] += _rowsum(du2)
            dg1_ref[...] += _rowsum(dx1 * xhat)
            db1_ref[...] += _rowsum(dx1)
            dr1 = _ln_bwd(dx1 * g1_ref[...], xhat, rstd_ref[:, 0:1])
            dr1_ref[...] = dr1
            dga_ref[...] += _rowsum(dr1 * mix_ref[...])
            dmix_ref[...] = ((1.0 + ga_ref[...]) * dr1).astype(BF16)

    row = pl.BlockSpec((tm, D_MODEL), lambda i, f: (i, 0))
    vec = _full((1, D_MODEL))
    return pl.pallas_call(
        body, name="mlp_bwd", grid=(T // tm, nf),
        in_specs=[row, pl.BlockSpec((None, None, D_MODEL, tf), lambda i, f: (f % 2, f // 2, 0, 0)),
                  pl.BlockSpec((None, None, tf, D_MODEL), lambda i, f: (f % 2, f // 2, 0, 0)),
                  pl.BlockSpec((tm, tf), lambda i, f: (i, f)), row, row, pl.BlockSpec((tm, 128), lambda i, f: (i, 0)),
                  row, vec, vec, vec, vec],
        out_specs=[pl.BlockSpec((tm, tf), lambda i, f: (i, f)), row, row, vec, vec, vec, vec, vec],
        out_shape=[jax.ShapeDtypeStruct((T, nf * tf), BF16), jax.ShapeDtypeStruct((T, D_MODEL), F32),
                   jax.ShapeDtypeStruct((T, D_MODEL), BF16)] + [jax.ShapeDtypeStruct((1, D_MODEL), F32)] * 5,
        scratch_shapes=[pltpu.VMEM((tm, D_MODEL), F32)],
        compiler_params=_params(),
    )(dh, w1, w2, r, dr2, xhat1, rstd1, mix, ln1_g, ln1_b, sc_m, g_a)


def _input_bwd(dz_h, dz_m, w_in_ext, x, dr1, sc_a, exchange=None):
    T = x.shape[0]
    tm = min(ROW_TILE_SMALL, T)

    def body(dzh_ref, dzm_ref, w_ref, x_ref, dr1_ref, sc_ref, gx_ref, dsc_ref, dsh_ref):
        @pl.when(pl.program_id(0) == 0)
        def _():
            dsc_ref[...] = jnp.zeros_like(dsc_ref)
            dsh_ref[...] = jnp.zeros_like(dsh_ref)

        du = _bdot(dzh_ref[...], w_ref[:, 0:2048], NT) + _bdot(dzm_ref[...], w_ref[:, 2048:3072], NT)
        gx_ref[...] = ALPHA * dr1_ref[...] + du * (1.0 + sc_ref[...])
        dsc_ref[...] += _rowsum(du * x_ref[...])
        dsh_ref[...] += _rowsum(du)

    row = pl.BlockSpec((tm, D_MODEL), lambda i: (i, 0))
    vec = _full((1, D_MODEL))
    return _call(
        body, "input_bwd", (dz_h, dz_m, w_in_ext, x, dr1, sc_a), grid=(T // tm,),
        in_specs=[pl.BlockSpec((tm, 2048), lambda i: (i, 0)), row, _full(w_in_ext.shape), row, row, vec],
        out_specs=[row, vec, vec],
        out_shape=[jax.ShapeDtypeStruct((T, D_MODEL), F32), jax.ShapeDtypeStruct((1, D_MODEL), F32),
                   jax.ShapeDtypeStruct((1, D_MODEL), F32)], exchange=exchange)


def _adam_math(w, g, m, v):
    m = ADAM_B1 * m + (1.0 - ADAM_B1) * g
    v = ADAM_B2 * v + (1.0 - ADAM_B2) * (g * g)
    m_hat = m / (1.0 - ADAM_B1 ** ADAM_STEP)
    v_hat = v / (1.0 - ADAM_B2 ** ADAM_STEP)
    return -ADAM_LR * (m_hat / (jnp.sqrt(v_hat) + ADAM_EPS) + ADAM_WD * w), m, v


def _adam(g_slabs, w, m, v, name, g_fn=None, g_extra=()):
    R, C = w.shape
    tr = R if R <= 256 else 256
    assert R % tr == 0
    ns = 0 if g_slabs is None else g_slabs.shape[0]
    ne = len(g_extra)

    def body(*refs):
        e_refs = refs[:ne]
        refs = refs[ne:]
        if ns:
            gs_ref, refs = refs[0], refs[1:]
        w_ref, m_ref, v_ref, g_ref, d_ref, nm_ref, nv_ref = refs
        if g_fn is not None:
            g = g_fn(*e_refs)
        else:
            g = gs_ref[0].astype(F32)
            for s in range(1, ns):
                g = g + gs_ref[s].astype(F32)
        d, nm, nv = _adam_math(w_ref[...], g, m_ref[...], v_ref[...])
        g_ref[...] = g
        d_ref[...] = d
        nm_ref[...] = nm
        nv_ref[...] = nv

    blk = pl.BlockSpec((tr, C), lambda i: (i, 0))
    in_specs = [pl.BlockSpec((tr, e.shape[1]), lambda i: (i, 0)) if e.shape[0] == R else _full(e.shape) for e in g_extra]
    args = list(g_extra)
    if ns:
        in_specs.append(pl.BlockSpec((ns, tr, C), lambda i: (0, i, 0)))
        args.append(g_slabs)
    return pl.pallas_call(
        body, name=name, grid=(R // tr,), in_specs=in_specs + [blk] * 3, out_specs=[blk] * 4,
        out_shape=[jax.ShapeDtypeStruct((R, C), F32)] * 4, compiler_params=_params(),
    )(*args, w, m, v)


def _adam_small(small_all, params):
    n = len(params)

    def body(*refs):
        s_ref, refs = refs[0], refs[1:]
        wmv, loss_ref, outs = refs[:3 * n], refs[3 * n], refs[3 * n + 1:]
        tot = s_ref[0]
        for i in range(1, N_DEV):
            tot = tot + s_ref[i]
        loss_ref[...] = tot[:, SMALL_W - 128:]
        for j, (w, _, _, off) in enumerate(params):
            w_ref, m_ref, v_ref = wmv[3 * j:3 * j + 3]
            g_ref, d_ref, nm_ref, nv_ref = outs[4 * j:4 * j + 4]
            if w.shape[0] == 2:
                lb = _lower_bound(w_ref)
                g0 = tot[:, off:off + w.shape[1]] * lb * (1.0 - lb)
                rows = [(slice(0, 1), g0), (slice(1, 2), -g0)]
            else:
                rows = [(slice(0, 1), tot[:, off:off + w.shape[1]])]
            for rs, g in rows:
                d, nm, nv = _adam_math(w_ref[rs, :], g, m_ref[rs, :], v_ref[rs, :])
                g_ref[rs, :], d_ref[rs, :], nm_ref[rs, :], nv_ref[rs, :] = g, d, nm, nv

    out_shape = [jax.ShapeDtypeStruct((1, 128), F32)]
    for w, _, _, _ in params:
        out_shape += [jax.ShapeDtypeStruct(w.shape, F32)] * 4
    res = pl.pallas_call(body, name="adam_small", out_shape=out_shape, compiler_params=_params())(
        small_all, *[a for w, m, v, _ in params for a in (w, m, v)])
    return res[0], [tuple(res[1 + 4 * j:5 + 4 * j]) for j in range(n)]


def _rot_half_cols(w):
    return jnp.concatenate([-w[..., 32:], w[..., :32]], axis=-1)


def _unrot_half_cols(dw_rot):
    return jnp.concatenate([dw_rot[..., 32:], -dw_rot[..., :32]], axis=-1)


def _cols_from_slabs(g):
    if g.ndim == 4:
        g = jnp.transpose(g, (1, 0, 2, 3)).reshape((N_DEV,) + g.shape[2:])
    s, r, c = g.shape
    return jnp.transpose(g, (1, 0, 2)).reshape(r, s * c)


def _slabs_from_cols(w):
    r, c = w.shape
    return jnp.transpose(w.reshape(r, N_DEV, c // N_DEV), (1, 0, 2))


def _ext_in(w_in):
    k_in = w_in.shape[0]
    z64, z128 = jnp.zeros((k_in, 64), BF16), jnp.zeros((k_in, 128), BF16)
    wk = w_in[:, 2560:2624]
    return jnp.concatenate([w_in[:, :2560], z128, wk, z64, z128, _rot_half_cols(wk), z64], axis=1)


def _ext_q(w_q_up):
    r = w_q_up.shape[0]
    z64, z128 = jnp.zeros((r, 64), BF16), jnp.zeros((r, 128), BF16)
    wq = w_q_up.reshape(r, HEADS, HEAD_DIM + ROPE_DIM)
    main = [jnp.concatenate([wq[:, h, :HEAD_DIM], wq[:, h, HEAD_DIM:], z64], axis=1) for h in range(HEADS)]
    rot = [jnp.concatenate([z128, _rot_half_cols(wq[:, h, HEAD_DIM:]), z64], axis=1) for h in range(HEADS)]
    return jnp.concatenate(main + rot, axis=1)


def _ext_kv(w_kv_up):
    r = w_kv_up.shape[0]
    z128 = jnp.zeros((r, 128), BF16)
    wkv = w_kv_up.reshape(r, HEADS, 2 * HEAD_DIM)
    kpad = [jnp.concatenate([wkv[:, h, :HEAD_DIM], z128], axis=1) for h in range(HEADS)]
    vals = [wkv[:, h, HEAD_DIM:] for h in range(HEADS)]
    return jnp.concatenate(kpad + vals, axis=1)


def _grad_in_from_ext(dw_in_h, dw_in_m):
    dwk = dw_in_m[:, 512 + 128:512 + 192] + _unrot_half_cols(dw_in_m[:, 768 + 128:768 + 192])
    return jnp.concatenate([dw_in_h, dw_in_m[:, :512], dwk], axis=1)


def _grads_qkv_from_ext(dwq_ext, dwkv_ext):
    qcols = []
    for h in range(HEADS):
        main, rot = dwq_ext[:, 256 * h:256 * h + 256], dwq_ext[:, 1024 + 256 * h:1280 + 256 * h]
        qcols += [main[:, :128], main[:, 128:192] + _unrot_half_cols(rot[:, 128:192])]
    kvcols = []
    for h in range(HEADS):
        kvcols += [dwkv_ext[:, 256 * h:256 * h + 128], dwkv_ext[:, 1024 + 128 * h:1152 + 128 * h]]
    return jnp.concatenate(qcols, axis=1), jnp.concatenate(kvcols, axis=1)


SMALL_W = 6144 + 512 + 512 + 256 + 256 + 4 * 1024 + 128


def kernel(x, c, positions, w_ada, b_ada, w_in, hg_lower_bounds, hg_norm_w, mla_q_norm_w, w_q_up, mla_kv_norm_w, w_kv_up, w_out, ln1_g, ln1_b, w_mlp_in, w_mlp_out, ln2_g, ln2_b, loss_target, m_w_ada, m_b_ada, m_w_in, m_hg_lower_bounds, m_hg_norm_w, m_mla_q_norm_w, m_w_q_up, m_mla_kv_norm_w, m_w_kv_up, m_w_out, m_ln1_g, m_ln1_b, m_w_mlp_in, m_w_mlp_out, m_ln2_g, m_ln2_b, v_w_ada, v_b_ada, v_w_in, v_hg_lower_bounds, v_hg_norm_w, v_mla_q_norm_w, v_w_q_up, v_mla_kv_norm_w, v_w_kv_up, v_w_out, v_ln1_g, v_ln1_b, v_w_mlp_in, v_w_mlp_out, v_ln2_g, v_ln2_b):
    T = x.shape[1]
    me = 4 * lax.axis_index("x") + 2 * lax.axis_index("y") + lax.axis_index("c")
    xs, tgt = x[0], loss_target[0]
    big = dict(w_in=w_in[0], w_q_up=w_q_up[0], w_kv_up=w_kv_up[0], w_out=w_out[0], w_mlp_in=w_mlp_in[0],
               w_mlp_out=w_mlp_out[0])
    names = list(big)

    bf = {n: big[n].astype(BF16) for n in names}
    g_in, g_c = _gather_two_level([bf["w_in"], c], name="gather_w_in")
    c_all = jnp.transpose(g_c.reshape(2, 4, D_MODEL), (1, 0, 2)).reshape(N_DEV, D_MODEL)

    ada_cols = w_ada.shape[2]
    mod_part, cond = _mod_part(c_all, w_ada[0], lax.dynamic_slice(b_ada, (0, me * ada_cols), (1, ada_cols)))
    (mod_all,) = _exchange([mod_part], scatter=False, name="gather_mod")
    mod_row = lax.dynamic_slice(mod_all, (0, me, 0), (N_DEV, 1, ada_cols)).reshape(1, N_DEV * ada_cols)
    sh_a, sc_a, g_a, sh_m, sc_m, g_m = [mod_row[:, D_MODEL * i:D_MODEL * (i + 1)] for i in range(6)]

    w_in_ext = _ext_in(_cols_from_slabs(g_in))
    z, (g_q, g_kv, g_out) = _matmul(xs, w_in_ext, "NN", "in_proj", a_fn=_modulate, extras=(sc_a, sh_a),
                                    exchange=_Exchange([bf["w_q_up"], bf["w_kv_up"], bf["w_out"]], False))
    wq_ext, wkv_ext = _ext_q(_cols_from_slabs(g_q)), _ext_kv(_cols_from_slabs(g_kv))
    w_out_full = g_out.reshape(D_MODEL, D_MODEL)
    inv_freq = 1.0 / (ROPE_THETA ** (jnp.arange(0, ROPE_DIM, 2, dtype=F32) / ROPE_DIM))
    zeros = lambda n: jnp.zeros((n,), F32)
    invf = jnp.concatenate([zeros(128), inv_freq, inv_freq, zeros(64)]).reshape(1, QK_PAD)
    m_one = jnp.concatenate([jnp.ones((128,), F32), zeros(128)]).reshape(1, QK_PAD)
    m_rot = jnp.concatenate([zeros(128), jnp.ones((64,), F32), zeros(64)]).reshape(1, QK_PAD)
    q, k, v, c1, s1, cqn, ckvn = _mla_pre(z, positions.reshape(T, 1), invf, m_one, m_rot, wq_ext, wkv_ext,
                                          mla_q_norm_w, mla_kv_norm_w)
    (o_raw, o_gated, s_prev), (w1_chips,) = _hgrn_fwd(
        z, hg_lower_bounds, hg_norm_w, exchange=_Exchange([bf["w_mlp_in"]], False, "chips"))
    (o_mla, lse), (w1, w2_chips) = _attn_fwd(
        q, k, v, exchange=[_Exchange([w1_chips], False, "pair"),
                           _Exchange([bf["w_mlp_out"]], False, "chips")])
    mixcat = jnp.concatenate([o_gated, o_mla.astype(BF16)], axis=1)
    (mix, xhat1, rstd1, u2), (w2,) = _mix_ln1(mixcat, w_out_full, xs, g_a, ln1_g, ln1_b, sc_m, sh_m,
                                              exchange=_Exchange([w2_chips], False, "pair"))
    r, dr2, dh, dln2_g, dln2_b, dg_m, loss_part = _mlp_fwd(u2, w1, w2, xhat1, ln1_g, ln1_b, g_m, ln2_g, ln2_b, tgt)

    dhpre, dr1, dmix, dsc_m, dsh_m, dln1_g, dln1_b, dg_a = _mlp_bwd(dh, w1, w2, r, dr2, xhat1, rstd1, mix, ln1_g,
                                                                    ln1_b, sc_m, g_a)
    received = {}
    dw2 = _matmul(r, dh, "TN", "wgrad_mlp_out", out_dtype=BF16, a_fn=_square, tm=1024)
    dw1 = _matmul(u2, dhpre, "TN", "wgrad_mlp_in", out_dtype=BF16, tm=1024, out_slabs=N_DEV)
    dmixcat = _matmul(dmix, w_out_full, "NT", "dgrad_out")
    dw_out = _matmul(mixcat, dmix, "TN", "wgrad_out", out_dtype=BF16, tm=1024)
    (dz_h, dlb, dnw), (received["w_out"],) = _hgrn_bwd(
        dmixcat, z, o_raw, s_prev, hg_lower_bounds, hg_norm_w,
        exchange=_Exchange([dw_out.reshape(N_DEV, D_MODEL // N_DEV, D_MODEL)], True))
    (dq, dk, dv), (received["w_mlp_in"], received["w_mlp_out"]) = _attn_bwd(
        q, k, v, dmixcat, o_mla, lse,
        exchange=_Exchange([dw1, dw2.reshape(N_DEV, dw2.shape[0] // N_DEV, D_MODEL)], True))
    dz_m, dq_ext, dkv_ext, dqnw, dkvnw = _mla_bwd(dq, dk, dv, z, c1, s1, wq_ext, wkv_ext, mla_q_norm_w,
                                                   mla_kv_norm_w)
    dwq_ext = _matmul(cqn, dq_ext, "TN", "wgrad_q_up", tn=2048)
    dwkv_ext = _matmul(ckvn, dkv_ext, "TN", "wgrad_kv_up", tn=1536)
    dwq, dwkv = _grads_qkv_from_ext(dwq_ext, dwkv_ext)
    dw_in_h, (received["w_q_up"], received["w_kv_up"]) = _matmul(
        xs, dz_h, "TN", "wgrad_in_h", a_fn=_modulate, extras=(sc_a, sh_a), tm=1024,
        exchange=_Exchange([_slabs_from_cols(g).astype(BF16) for g in (dwq, dwkv)], True))
    dw_in_m = _matmul(xs, dz_m, "TN", "wgrad_in_m", a_fn=_modulate, extras=(sc_a, sh_a), tm=1024)
    dw_in = _grad_in_from_ext(dw_in_h, dw_in_m)
    (grad_x, dsc_a, dsh_a), (received["w_in"],) = _input_bwd(
        dz_h, dz_m, w_in_ext, xs, dr1, sc_a, exchange=_Exchange([_slabs_from_cols(dw_in).astype(BF16)], True))

    small = jnp.concatenate([dsh_a, dsc_a, dg_a, dsh_m, dsc_m, dg_m, dlb, dnw, dqnw, dkvnw, dln1_g, dln1_b, dln2_g,
                             dln2_b, loss_part], axis=1)
    (small_all,) = _exchange([small], scatter=False, name="gather_small")

    moments = dict(w_in=(m_w_in, v_w_in), w_q_up=(m_w_q_up, v_w_q_up), w_kv_up=(m_w_kv_up, v_w_kv_up),
                   w_out=(m_w_out, v_w_out), w_mlp_in=(m_w_mlp_in, v_w_mlp_in), w_mlp_out=(m_w_mlp_out, v_w_mlp_out))
    res = {}
    for n in names:
        res[n] = _adam(received[n], big[n], moments[n][0][0], moments[n][1][0], name="adam_" + n)
    dmod_cols = lax.dynamic_slice(small_all.reshape(N_DEV, SMALL_W), (0, me * ada_cols), (N_DEV, ada_cols))
    cond_t = cond.T

    def ada_grad(ct_ref, dm_ref):
        g = ct_ref[:, 0:1] * dm_ref[0:1, :]
        for b in range(1, N_DEV):
            g = g + ct_ref[:, b:b + 1] * dm_ref[b:b + 1, :]
        return g

    res["w_ada"] = _adam(None, w_ada[0], m_w_ada[0], v_w_ada[0], name="adam_w_ada", g_fn=ada_grad,
                         g_extra=(cond_t, dmod_cols))

    small_params = [("b_ada", b_ada, m_b_ada, v_b_ada, 0),
                    ("hg_lower_bounds", hg_lower_bounds, m_hg_lower_bounds, v_hg_lower_bounds, 6144),
                    ("hg_norm_w", hg_norm_w, m_hg_norm_w, v_hg_norm_w, 6656),
                    ("mla_q_norm_w", mla_q_norm_w, m_mla_q_norm_w, v_mla_q_norm_w, 7168),
                    ("mla_kv_norm_w", mla_kv_norm_w, m_mla_kv_norm_w, v_mla_kv_norm_w, 7424),
                    ("ln1_g", ln1_g, m_ln1_g, v_ln1_g, 7680), ("ln1_b", ln1_b, m_ln1_b, v_ln1_b, 8704),
                    ("ln2_g", ln2_g, m_ln2_g, v_ln2_g, 9728), ("ln2_b", ln2_b, m_ln2_b, v_ln2_b, 10752)]
    loss_row, small_res = _adam_small(small_all, [p[1:] for p in small_params])
    for p, r4 in zip(small_params, small_res):
        res[p[0]] = r4
    loss = loss_row[0, 0]

    order = ["w_ada", "b_ada", "w_in", "hg_lower_bounds", "hg_norm_w", "mla_q_norm_w", "w_q_up", "mla_kv_norm_w",
             "w_kv_up", "w_out", "ln1_g", "ln1_b", "w_mlp_in", "w_mlp_out", "ln2_g", "ln2_b"]
    shaped = {n: tuple(a.reshape((1,) + a.shape) if n in big or n == "w_ada" else a for a in res[n]) for n in order}
    outs = [loss, grad_x.reshape(1, T, D_MODEL)]
    for i in range(4):
        outs += [shaped[n][i] for n in order]
    return tuple(outs)
```

```python
import functools

import jax
import jax.numpy as jnp
import numpy as np
from jax import lax
from jax.experimental import pallas as pl
from jax.experimental.pallas import tpu as pltpu

F32, BF16 = jnp.float32, jnp.bfloat16
N_DEV = 8
D_MODEL = 1024
HEADS = 4
HEAD_DIM = 128
ROPE_DIM = 64
QK_PAD = 256
CHUNK = 64
ROPE_THETA = 10000.0
RMS_EPS = 1e-6
LN_EPS = 1e-5
ALPHA = 2.0 ** 0.25
ATT_SCALE = (HEAD_DIM + ROPE_DIM) ** -0.5
ADAM_LR, ADAM_B1, ADAM_B2, ADAM_EPS, ADAM_WD, ADAM_STEP = 0.001, 0.9, 0.999, 1e-08, 0.01, 10
NEG_BIG = -1e30

ROW_TILE = 512
ROW_TILE_SMALL = 256
ATT_TILE = 512
HGRN_GROUP = 8
VMEM_LIMIT = 56 * 2 ** 20

NN = (((1,), (0,)), ((), ()))
NT = (((1,), (1,)), ((), ()))
TN = (((0,), (0,)), ((), ()))


def _dot(a, b, dims=NN):
    return lax.dot_general(a, b, dims, preferred_element_type=F32)


def _bdot(a, b, dims=NN):
    return lax.dot_general(a.astype(BF16), b.astype(BF16), dims, preferred_element_type=F32)


def _hdot(a, b, dims=NN):
    return lax.dot_general(a, b, dims, precision=lax.Precision.HIGHEST, preferred_element_type=F32)


def _params():
    return pltpu.CompilerParams(vmem_limit_bytes=VMEM_LIMIT)


def _sigmoid(x):
    return 1.0 / (1.0 + jnp.exp(-x))


def _rowsum(x):
    return jnp.sum(x, axis=0, keepdims=True)


def _lanemean(x):
    return jnp.mean(x, axis=-1, keepdims=True)


def _full(shape):
    nd = len(shape)
    return pl.BlockSpec(shape, lambda *_: (0,) * nd)


_GROUPS = {
    "all": ((1, 2, 3, 4, 5, 6, 7), lambda x, y, c: 4 * x + 2 * y + c),
    "chips": ((2, 4, 6), lambda x, y, c: 2 * x + y),
    "pair": ((1,), lambda x, y, c: c),
}


class _Exchange:
    def __init__(self, arrs, scatter, group="all", pieces=1):
        self.arrs, self.scatter, self.n, self.pieces = list(arrs), scatter, len(arrs), pieces
        self.flips, self.slot = _GROUPS[group]
        size = len(self.flips) + 1
        self.out_shape = [jax.ShapeDtypeStruct((size,) + (a.shape[1:] if scatter else a.shape), a.dtype)
                          for a in self.arrs]
        for s in self.out_shape:
            assert s.shape[1] % pieces == 0, (s.shape, pieces)
        n = self.n
        self.scratch = [pltpu.SemaphoreType.DMA((n, size - 1, pieces)), pltpu.SemaphoreType.DMA((n, size - 1, pieces)),
                        pltpu.SemaphoreType.DMA((n, pieces))]

    def _copies(self, ins, outs, sems):
        send_sems, recv_sems, loc_sems = sems
        x, y, c = lax.axis_index("x"), lax.axis_index("y"), lax.axis_index("c")
        me = self.slot(x, y, c)
        copies = []
        for k in range(self.n):
            rows = self.out_shape[k].shape[1] // self.pieces
            for q in range(self.pieces):
                cut = pl.ds(q * rows, rows)
                src_of = ((lambda i, k=k, cut=cut: ins[k].at[i, cut]) if self.scatter
                          else (lambda i, k=k, cut=cut: ins[k].at[cut]))
                copies.append((pltpu.make_async_copy(src_of(me), outs[k].at[me, cut], loc_sems.at[k, q]), None))
                for j, p in enumerate(self.flips):
                    px = (1 - x) if p & 4 else x
                    py = (1 - y) if p & 2 else y
                    pc = (1 - c) if p & 1 else c
                    peer = self.slot(px, py, pc)
                    both = dict(send_sem=send_sems.at[k, j, q], recv_sem=recv_sems.at[k, j, q],
                                device_id=(px, py, pc), device_id_type=pl.DeviceIdType.MESH)
                    send = pltpu.make_async_remote_copy(src_ref=src_of(peer), dst_ref=outs[k].at[me, cut], **both)
                    recv = pltpu.make_async_remote_copy(src_ref=src_of(peer), dst_ref=outs[k].at[peer, cut], **both)
                    copies.append((send, recv))
        return copies

    def start(self, ins, outs, sems):
        for first, _ in self._copies(ins, outs, sems):
            first.start()

    def wait(self, ins, outs, sems):
        for first, recv in self._copies(ins, outs, sems):
            if recv is None:
                first.wait()
            else:
                recv.wait_recv()
                first.wait_send()


def _call(body, name, args, out_shape, grid=(), in_specs=(), out_specs=(), scratch_shapes=(), exchange=None):
    if exchange is None:
        return pl.pallas_call(body, name=name, grid=grid, in_specs=list(in_specs), out_specs=list(out_specs),
                              out_shape=list(out_shape), scratch_shapes=list(scratch_shapes),
                              compiler_params=_params())(*args), None
    exs = list(exchange) if isinstance(exchange, (list, tuple)) else [exchange]
    ni, no, ns, nx = len(args), len(out_shape), len(scratch_shapes), sum(e.n for e in exs)

    def wrapped(*refs):
        a, xi = refs[:ni], refs[ni:ni + nx]
        o, xo = refs[ni + nx:ni + nx + no], refs[ni + nx + no:ni + 2 * nx + no]
        s, xs = refs[ni + 2 * nx + no:ni + 2 * nx + no + ns], refs[ni + 2 * nx + no + ns:]
        parts, at = [], 0
        for j, e in enumerate(exs):
            parts.append((e, xi[at:at + e.n], xo[at:at + e.n], xs[3 * j:3 * j + 3]))
            at += e.n
        first = last = None
        for d, g in enumerate(grid):
            f, l = pl.program_id(d) == 0, pl.program_id(d) == g - 1
            first, last = (f, l) if first is None else (first & f, last & l)

        @pl.when(first)
        def _():
            for e, ins, outs, sems in parts:
                e.start(ins, outs, sems)

        body(*a, *o, *s)

        @pl.when(last)
        def _():
            for e, ins, outs, sems in parts:
                e.wait(ins, outs, sems)

    hbm = pl.BlockSpec(memory_space=pltpu.HBM)
    res = pl.pallas_call(
        wrapped, name=name, grid=grid, in_specs=list(in_specs) + [hbm] * nx, out_specs=list(out_specs) + [hbm] * nx,
        out_shape=list(out_shape) + [o_ for e in exs for o_ in e.out_shape],
        scratch_shapes=list(scratch_shapes) + [s_ for e in exs for s_ in e.scratch],
        compiler_params=_params())(*args, *[a_ for e in exs for a_ in e.arrs])
    return res[:no], res[no:]


def _exchange(arrs, scatter, name):
    ex = _Exchange(arrs, scatter)

    def body(*refs):
        ins, outs, sems = refs[:ex.n], refs[ex.n:2 * ex.n], refs[2 * ex.n:]
        ex.start(ins, outs, sems)
        ex.wait(ins, outs, sems)

    hbm = pl.BlockSpec(memory_space=pltpu.HBM)
    return pl.pallas_call(body, name=name, out_shape=ex.out_shape, in_specs=[hbm] * ex.n, out_specs=[hbm] * ex.n,
                          scratch_shapes=ex.scratch)(*ex.arrs)


def _gather_two_level(arrs, name):
    n = len(arrs)
    ex1 = _Exchange(arrs, False, "chips")
    ex2 = _Exchange(ex1.out_shape, False, "pair", pieces=4)

    def body(*refs):
        ins, mid, outs, sems = refs[:n], refs[n:2 * n], refs[2 * n:3 * n], refs[3 * n:]
        ex1.start(ins, mid, sems[:3])
        ex1.wait(ins, mid, sems[:3])
        ex2.start(mid, outs, sems[3:])
        ex2.wait(mid, outs, sems[3:])

    hbm = pl.BlockSpec(memory_space=pltpu.HBM)
    res = pl.pallas_call(body, name=name, out_shape=ex1.out_shape + ex2.out_shape, in_specs=[hbm] * n,
                         out_specs=[hbm] * (2 * n), scratch_shapes=ex1.scratch + ex2.scratch)(*arrs)
    return res[n:]


def _matmul(a, b, mode, name, out_dtype=F32, tm=512, tn=1024, tk=1024, a_fn=None, extras=(), out_slabs=None,
            exchange=None):
    if mode == "NN":
        (M, K), N = a.shape, b.shape[1]
    elif mode == "NT":
        (M, K), N = a.shape, b.shape[0]
    else:
        (K, M), N = a.shape, b.shape[1]
    if out_slabs:
        tn = N // out_slabs
    tm, tn, tk = min(tm, M), min(tn, N), min(tk, K)
    assert M % tm == 0 and N % tn == 0 and K % tk == 0, (name, M, N, K)
    nk = K // tk
    dims = {"NN": NN, "NT": NT, "TN": TN}[mode]
    ne = len(extras)

    def body(a_ref, b_ref, *rest):
        e_refs, o_ref, acc_ref = rest[:ne], rest[ne], rest[ne + 1]
        k = pl.program_id(2)

        @pl.when(k == 0)
        def _():
            acc_ref[...] = jnp.zeros_like(acc_ref)

        at = a_ref[...]
        if a_fn is not None:
            at = a_fn(at.astype(F32), *[e[...] for e in e_refs])
        acc_ref[...] += _bdot(at, b_ref[...], dims)

        @pl.when(k == nk - 1)
        def _():
            o_ref[...] = acc_ref[...].astype(out_dtype)

    if mode == "TN":
        a_spec = pl.BlockSpec((tk, tm), lambda i, j, k: (k, i))
        e_spec = pl.BlockSpec((1, tm), lambda i, j, k: (0, i))
    else:
        a_spec = pl.BlockSpec((tm, tk), lambda i, j, k: (i, k))
        e_spec = pl.BlockSpec((1, tk), lambda i, j, k: (0, k))
    if mode == "NT":
        b_spec = pl.BlockSpec((tn, tk), lambda i, j, k: (j, k))
    else:
        b_spec = pl.BlockSpec((tk, tn), lambda i, j, k: (k, j))
    if out_slabs:
        o_shape = jax.ShapeDtypeStruct((out_slabs, M, tn), out_dtype)
        o_spec = pl.BlockSpec((None, tm, tn), lambda i, j, k: (j, i, 0))
    else:
        o_shape = jax.ShapeDtypeStruct((M, N), out_dtype)
        o_spec = pl.BlockSpec((tm, tn), lambda i, j, k: (i, j))
    (out,), got = _call(body, name, (a, b, *extras), [o_shape], grid=(M // tm, N // tn, nk),
                        in_specs=[a_spec, b_spec] + [e_spec] * ne, out_specs=[o_spec],
                        scratch_shapes=[pltpu.VMEM((tm, tn), F32)], exchange=exchange)
    return out if exchange is None else (out, got)


def _modulate(x, sc, sh):
    return x * (1.0 + sc) + sh


def _square(x):
    return x * x


def _mod_part(c_all, w_ada_s, b_s):
    def body(c_ref, w_ref, b_ref, mod_ref, cond_ref):
        cv = c_ref[...]
        cond = cv * _sigmoid(cv)
        cond_ref[...] = cond
        mod_ref[...] = _bdot(cond, w_ref[...]) + b_ref[...]

    return pl.pallas_call(
        body, name="mod_part",
        out_shape=[jax.ShapeDtypeStruct((N_DEV, w_ada_s.shape[1]), F32), jax.ShapeDtypeStruct(c_all.shape, F32)],
        compiler_params=_params(),
    )(c_all, w_ada_s, b_s)


def _rms_fwd(x, w):
    rs = lax.rsqrt(_lanemean(x * x) + RMS_EPS)
    return x * rs * w, rs


def _rms_bwd(x, rs, w, dy):
    xhat = x * rs
    dxh = dy * w
    return rs * (dxh - xhat * _lanemean(dxh * xhat)), dy * xhat


def _mla_pre(z, pos_col, invf, m_one, m_rot, wq_ext, wkv_ext, qnw, kvnw):
    T = z.shape[0]
    tm = min(ROW_TILE, T)

    def body(z_ref, pos_ref, invf_ref, mone_ref, mrot_ref, wq_ref, wkv_ref, qnw_ref, kvnw_ref,
             q_ref, k_ref, v_ref, c1_ref, s1_ref, cqn_ref, ckvn_ref):
        ang = pos_ref[...].astype(F32) * invf_ref[...]
        c1 = mone_ref[...] + mrot_ref[...] * jnp.cos(ang)
        s1 = mrot_ref[...] * jnp.sin(ang)
        c1_ref[...] = c1
        s1_ref[...] = s1
        cqn, _ = _rms_fwd(z_ref[:, 0:256], qnw_ref[...])
        ckvn, _ = _rms_fwd(z_ref[:, 256:512], kvnw_ref[...])
        cqn_ref[...] = cqn.astype(BF16)
        ckvn_ref[...] = ckvn.astype(BF16)
        qe = _bdot(cqn, wq_ref[...])
        kve = _bdot(ckvn, wkv_ref[...])
        k_rope = z_ref[:, 512:768] * c1 + z_ref[:, 768:1024] * s1
        for h in range(HEADS):
            q_ref[h] = (qe[:, 256 * h:256 * h + 256] * c1 + qe[:, 1024 + 256 * h:1280 + 256 * h] * s1).astype(BF16)
            k_ref[h] = (kve[:, 256 * h:256 * h + 256] + k_rope).astype(BF16)
            v_ref[h] = kve[:, 1024 + 128 * h:1152 + 128 * h].astype(BF16)

    row = lambda i: (i, 0)
    head = lambda i: (0, i, 0)
    return pl.pallas_call(
        body, name="mla_pre", grid=(T // tm,),
        in_specs=[pl.BlockSpec((tm, 1024), lambda i: (i, 2)), pl.BlockSpec((tm, 1), row),
                  _full((1, 256)), _full((1, 256)), _full((1, 256)), _full(wq_ext.shape), _full(wkv_ext.shape),
                  _full((1, 256)), _full((1, 256))],
        out_specs=[pl.BlockSpec((HEADS, tm, QK_PAD), head), pl.BlockSpec((HEADS, tm, QK_PAD), head),
                   pl.BlockSpec((HEADS, tm, HEAD_DIM), head), pl.BlockSpec((tm, 256), row), pl.BlockSpec((tm, 256), row),
                   pl.BlockSpec((tm, 256), row), pl.BlockSpec((tm, 256), row)],
        out_shape=[jax.ShapeDtypeStruct((HEADS, T, QK_PAD), BF16), jax.ShapeDtypeStruct((HEADS, T, QK_PAD), BF16),
                   jax.ShapeDtypeStruct((HEADS, T, HEAD_DIM), BF16), jax.ShapeDtypeStruct((T, 256), F32),
                   jax.ShapeDtypeStruct((T, 256), F32), jax.ShapeDtypeStruct((T, 256), BF16),
                   jax.ShapeDtypeStruct((T, 256), BF16)],
        compiler_params=_params(),
    )(z, pos_col, invf, m_one, m_rot, wq_ext, wkv_ext, qnw, kvnw)


def _mla_bwd(dq, dk, dv, z, c1, s1, wq_ext, wkv_ext, qnw, kvnw):
    T = z.shape[0]
    tm = min(ROW_TILE_SMALL, T)

    def body(dq_ref, dk_ref, dv_ref, z_ref, c1_ref, s1_ref, wq_ref, wkv_ref, qnw_ref, kvnw_ref,
             dz_ref, dqe_ref, dkve_ref, dqnw_ref, dkvnw_ref):
        @pl.when(pl.program_id(0) == 0)
        def _():
            dqnw_ref[...] = jnp.zeros_like(dqnw_ref)
            dkvnw_ref[...] = jnp.zeros_like(dkvnw_ref)

        c1, s1 = c1_ref[...], s1_ref[...]
        dkpe = jnp.zeros((tm, QK_PAD), F32)
        for h in range(HEADS):
            dqh, dkh = dq_ref[h], dk_ref[h]
            dqe_ref[:, 256 * h:256 * h + 256] = (dqh * c1).astype(BF16)
            dqe_ref[:, 1024 + 256 * h:1280 + 256 * h] = (dqh * s1).astype(BF16)
            dkve_ref[:, 256 * h:256 * h + 256] = dkh.astype(BF16)
            dkve_ref[:, 1024 + 128 * h:1152 + 128 * h] = dv_ref[h].astype(BF16)
            dkpe = dkpe + dkh
        dcqn = _dot(dqe_ref[...], wq_ref[...], NT)
        dckvn = _dot(dkve_ref[...], wkv_ref[...], NT)
        cq, ckv = z_ref[:, 0:256], z_ref[:, 256:512]
        _, rsq = _rms_fwd(cq, qnw_ref[...])
        _, rskv = _rms_fwd(ckv, kvnw_ref[...])
        dcq, wq_rows = _rms_bwd(cq, rsq, qnw_ref[...], dcqn)
        dckv, wkv_rows = _rms_bwd(ckv, rskv, kvnw_ref[...], dckvn)
        dqnw_ref[...] += _rowsum(wq_rows)
        dkvnw_ref[...] += _rowsum(wkv_rows)
        dz_ref[:, 0:256] = dcq
        dz_ref[:, 256:512] = dckv
        dz_ref[:, 512:768] = dkpe * c1
        dz_ref[:, 768:1024] = dkpe * s1

    row = lambda i: (i, 0)
    head = lambda i: (0, i, 0)
    return pl.pallas_call(
        body, name="mla_bwd", grid=(T // tm,),
        in_specs=[pl.BlockSpec((HEADS, tm, QK_PAD), head), pl.BlockSpec((HEADS, tm, QK_PAD), head),
                  pl.BlockSpec((HEADS, tm, HEAD_DIM), head), pl.BlockSpec((tm, 1024), lambda i: (i, 2)),
                  pl.BlockSpec((tm, 256), row), pl.BlockSpec((tm, 256), row), _full(wq_ext.shape), _full(wkv_ext.shape),
                  _full((1, 256)), _full((1, 256))],
        out_specs=[pl.BlockSpec((tm, 1024), row), pl.BlockSpec((tm, 2048), row), pl.BlockSpec((tm, 1536), row),
                   _full((1, 256)), _full((1, 256))],
        out_shape=[jax.ShapeDtypeStruct((T, 1024), F32), jax.ShapeDtypeStruct((T, 2048), BF16),
                   jax.ShapeDtypeStruct((T, 1536), BF16), jax.ShapeDtypeStruct((1, 256), F32),
                   jax.ShapeDtypeStruct((1, 256), F32)],
        compiler_params=_params(),
    )(dq, dk, dv, z, c1, s1, wq_ext, wkv_ext, qnw, kvnw)


_HEAD_LANES = [slice(HEAD_DIM * h, HEAD_DIM * (h + 1)) for h in range(HEADS)]


def _lower_bound(lbraw_ref):
    a0, a1 = lbraw_ref[0:1, :], lbraw_ref[1:2, :]
    mx = jnp.maximum(a0, a1)
    e0, e1 = jnp.exp(a0 - mx), jnp.exp(a1 - mx)
    return e0 / (e0 + e1)


def _tri(lower):
    r = lax.broadcasted_iota(jnp.int32, (CHUNK, CHUNK), 0)
    c = lax.broadcasted_iota(jnp.int32, (CHUNK, CHUNK), 1)
    return (r >= c) if lower else (r <= c)


def _hgrn_gates(q, f, lb, tri_lo):
    sg = _sigmoid(f)
    forget = lb + (1.0 - lb) * sg
    k = 1.0 - forget
    b = _hdot(tri_lo.astype(F32), jnp.log(forget))
    b_ref, b_last = b[CHUNK // 2 - 1:CHUNK // 2, :], b[CHUNK - 1:CHUNK, :]
    e1, e2, e3, e4 = jnp.exp(b - b_ref), jnp.exp(b_ref - b), jnp.exp(b_last - b), jnp.exp(b)
    return dict(sg=sg, forget=forget, k=k, e1=e1, e2=e2, e3=e3, e4=e4, qa=q * e1, ka=k * e2, kl=k * e3, qb=q * e4,
                decay=jnp.exp(b_last))


def _hgrn_fwd(z, lbraw, nw, exchange=None):
    T = z.shape[0]
    G = min(HGRN_GROUP, T // CHUNK)
    rows = G * CHUNK
    n_chunks = T // CHUNK

    def body(q_ref, f_ref, i_ref, g_ref, lbraw_ref, nw_ref, oraw_ref, og_ref, sp_ref, st_ref):
        @pl.when(pl.program_id(0) == 0)
        def _():
            st_ref[...] = jnp.zeros_like(st_ref)

        lb_all = _lower_bound(lbraw_ref)
        tri_lo = _tri(True)

        def chunk(cc, carry):
            rs = pl.ds(pl.multiple_of(cc * CHUNK, CHUNK), CHUNK)
            t = _hgrn_gates(q_ref[rs, :], f_ref[rs, :], lb_all, tri_lo)
            v, gate = i_ref[rs, :], g_ref[rs, :]
            st = [st_ref[h] for h in range(HEADS)]
            a = [jnp.where(tri_lo, _bdot(t["qa"][:, s], t["ka"][:, s], NT), 0.0) for s in _HEAD_LANES]
            kv = [_bdot(v[:, s], t["kl"][:, s], TN) for s in _HEAD_LANES]
            o = [_bdot(a[h], v[:, s]) + _bdot(t["qb"][:, s], st[h], NT) for h, s in enumerate(_HEAD_LANES)]
            for h, s in enumerate(_HEAD_LANES):
                sp_ref[cc, h] = st[h]
                st_ref[h] = st[h] * t["decay"][:, s] + kv[h]
            oraw_ref[rs, :] = jnp.concatenate(o, axis=1)
            on = jnp.concatenate([_rms_fwd(o[h], nw_ref[:, s])[0] for h, s in enumerate(_HEAD_LANES)], axis=1)
            og_ref[rs, :] = (on * (gate * _sigmoid(gate))).astype(BF16)
            return carry

        lax.fori_loop(0, G, chunk, 0)

    col = lambda j: pl.BlockSpec((rows, 512), lambda r, j=j: (r, j))
    return _call(
        body, "hgrn_fwd", (z, z, z, z, lbraw, nw), grid=(T // rows,),
        in_specs=[col(0), col(1), col(2), col(3), _full((2, 512)), _full((1, 512))],
        out_specs=[col(0), col(0), pl.BlockSpec((G, HEADS, HEAD_DIM, HEAD_DIM), lambda r: (r, 0, 0, 0))],
        out_shape=[jax.ShapeDtypeStruct((T, 512), F32), jax.ShapeDtypeStruct((T, 512), BF16),
                   jax.ShapeDtypeStruct((n_chunks, HEADS, HEAD_DIM, HEAD_DIM), F32)],
        scratch_shapes=[pltpu.VMEM((HEADS, HEAD_DIM, HEAD_DIM), F32)], exchange=exchange)


def _hgrn_bwd(dmixcat, z, oraw, sprev, lbraw, nw, exchange=None):
    T = z.shape[0]
    G = min(HGRN_GROUP, T // CHUNK)
    rows = G * CHUNK
    ng = T // rows

    def body(dog_ref, q_ref, f_ref, i_ref, g_ref, oraw_ref, sp_ref, lbraw_ref, nw_ref,
             dz_ref, dlb_ref, dnw_ref, dst_ref):
        @pl.when(pl.program_id(0) == 0)
        def _():
            dst_ref[...] = jnp.zeros_like(dst_ref)
            dlb_ref[...] = jnp.zeros_like(dlb_ref)
            dnw_ref[...] = jnp.zeros_like(dnw_ref)

        lb_all = _lower_bound(lbraw_ref)
        tri_lo, tri_up = _tri(True), _tri(False)
        rowid = lax.broadcasted_iota(jnp.int32, (CHUNK, HEADS * HEAD_DIM), 0)

        def chunk(it, carry):
            cc = G - 1 - it
            rs = pl.ds(pl.multiple_of(cc * CHUNK, CHUNK), CHUNK)
            heads = list(enumerate(_HEAD_LANES))
            cat = lambda parts: jnp.concatenate(parts, axis=1)
            per_head_mean = lambda x: cat([jnp.broadcast_to(_lanemean(x[:, s]), (CHUNK, HEAD_DIM)) for s in _HEAD_LANES])
            t = _hgrn_gates(q_ref[rs, :], f_ref[rs, :], lb_all, tri_lo)
            v, gate, o, dog, nw_all = i_ref[rs, :], g_ref[rs, :], oraw_ref[rs, :], dog_ref[rs, :], nw_ref[...]
            rs_o = lax.rsqrt(per_head_mean(o * o) + RMS_EPS)
            xhat = o * rs_o
            sgg = _sigmoid(gate)
            d_on = dog * (gate * sgg)
            dz_ref[rs, 1536:2048] = dog * (xhat * nw_all) * (sgg * (1.0 + gate * (1.0 - sgg)))
            dxh = d_on * nw_all
            do = rs_o * (dxh - xhat * per_head_mean(dxh * xhat))
            dnw_ref[...] += _rowsum(d_on * xhat)
            st = [sp_ref[cc, h] for h in range(HEADS)]
            dst = [dst_ref[h] for h in range(HEADS)]
            a = [jnp.where(tri_lo, _bdot(t["qa"][:, s], t["ka"][:, s], NT), 0.0) for s in _HEAD_LANES]
            da = [jnp.where(tri_lo, _bdot(do[:, s], v[:, s], NT), 0.0) for s in _HEAD_LANES]
            dqb = cat([_bdot(do[:, s], st[h]) for h, s in heads])
            dkl = cat([_bdot(v[:, s], dst[h]) for h, s in heads])
            dv_ = cat([_bdot(t["kl"][:, s], dst[h], NT) + _bdot(a[h], do[:, s], TN) for h, s in heads])
            dqa = cat([_bdot(da[h], t["ka"][:, s]) for h, s in heads])
            dka = cat([_bdot(da[h], t["qa"][:, s], TN) for h, s in heads])
            ddecay = cat([_rowsum(dst[h] * st[h]) for h in range(HEADS)])
            for h, s in heads:
                dst_ref[h] = dst[h] * t["decay"][:, s] + _bdot(do[:, s], t["qb"][:, s], TN)
            pa, pk, pb, pl_ = dqa * t["qa"], dka * t["ka"], dqb * t["qb"], dkl * t["kl"]
            db = pa - pk + pb - pl_
            db = db + jnp.where(rowid == CHUNK // 2 - 1, _rowsum(pk - pa), 0.0)
            db = db + jnp.where(rowid == CHUNK - 1, _rowsum(pl_) + ddecay * t["decay"], 0.0)
            dlogf = _hdot(tri_up.astype(F32), db)
            dforget = dlogf / t["forget"] - (dka * t["e2"] + dkl * t["e3"])
            sg = t["sg"]
            dz_ref[rs, 0:512] = dqa * t["e1"] + dqb * t["e4"]
            dz_ref[rs, 512:1024] = dforget * (1.0 - lb_all) * sg * (1.0 - sg)
            dz_ref[rs, 1024:1536] = dv_
            dlb_ref[...] += _rowsum(dforget * (1.0 - sg))
            return carry

        lax.fori_loop(0, G, chunk, 0)

    col = lambda j: pl.BlockSpec((rows, 512), lambda r, j=j: (ng - 1 - r, j))
    return _call(
        body, "hgrn_bwd", (dmixcat, z, z, z, z, oraw, sprev, lbraw, nw), grid=(ng,),
        in_specs=[col(0), col(0), col(1), col(2), col(3), col(0),
                  pl.BlockSpec((G, HEADS, HEAD_DIM, HEAD_DIM), lambda r: (ng - 1 - r, 0, 0, 0)),
                  _full((2, 512)), _full((1, 512))],
        out_specs=[pl.BlockSpec((rows, 2048), lambda r: (ng - 1 - r, 0)), _full((1, 512)), _full((1, 512))],
        out_shape=[jax.ShapeDtypeStruct((T, 2048), F32), jax.ShapeDtypeStruct((1, 512), F32),
                   jax.ShapeDtypeStruct((1, 512), F32)],
        scratch_shapes=[pltpu.VMEM((HEADS, HEAD_DIM, HEAD_DIM), F32)], exchange=exchange)


def _diag_mask(t):
    r = lax.broadcasted_iota(jnp.int32, (t, t), 0)
    c = lax.broadcasted_iota(jnp.int32, (t, t), 1)
    return r >= c


def _attn_fwd(q, k, v, exchange=None):
    _, T, _ = q.shape
    t = min(ATT_TILE, T)

    def body(q_ref, k_ref, v_ref, o_ref, lse_ref):
        i = pl.program_id(1)
        qb = q_ref[...]

        def step(j, carry, masked):
            m, l, acc = carry
            ks = pl.ds(pl.multiple_of(j * t, t), t)
            s = _dot(qb, k_ref[ks, :], NT) * ATT_SCALE
            if masked:
                s = jnp.where(_diag_mask(t), s, NEG_BIG)
            mn = jnp.maximum(m, jnp.max(s, axis=-1, keepdims=True))
            p = jnp.exp(s - mn)
            al = jnp.exp(m - mn)
            return mn, al * l + jnp.sum(p, axis=-1, keepdims=True), al * acc + _dot(p.astype(BF16), v_ref[ks, :])

        init = (jnp.full((t, 1), NEG_BIG, F32), jnp.zeros((t, 1), F32), jnp.zeros((t, HEAD_DIM), F32))
        carry = lax.fori_loop(0, i, lambda j, c: step(j, c, False), init)
        m, l, acc = step(i, carry, True)
        o_ref[...] = acc / l
        lse_ref[...] = jnp.broadcast_to(m + jnp.log(l), (t, HEAD_DIM))

    return _call(
        body, "attn_fwd", (q, k, v), grid=(HEADS, T // t),
        in_specs=[pl.BlockSpec((None, t, QK_PAD), lambda h, i: (h, i, 0)),
                  pl.BlockSpec((None, T, QK_PAD), lambda h, i: (h, 0, 0)),
                  pl.BlockSpec((None, T, HEAD_DIM), lambda h, i: (h, 0, 0))],
        out_specs=[pl.BlockSpec((t, HEAD_DIM), lambda h, i: (i, h)),
                   pl.BlockSpec((None, t, HEAD_DIM), lambda h, i: (h, i, 0))],
        out_shape=[jax.ShapeDtypeStruct((T, HEADS * HEAD_DIM), F32), jax.ShapeDtypeStruct((HEADS, T, HEAD_DIM), F32)],
        exchange=exchange)


def _attn_bwd(q, k, v, dmixcat, o, lse, exchange=None):
    _, T, _ = q.shape
    t = min(ATT_TILE, T)
    nq = T // t

    def body(q_ref, k_ref, v_ref, do_ref, o_ref, lse_ref, dq_ref, dk_ref, dv_ref, delta_ref):
        j = pl.program_id(1)

        @pl.when(j == 0)
        def _():
            dq_ref[...] = jnp.zeros_like(dq_ref)

            def fill(i, carry):
                rs = pl.ds(pl.multiple_of(i * t, t), t)
                delta_ref[rs, :] = jnp.broadcast_to(
                    jnp.sum(do_ref[rs, :] * o_ref[rs, :], axis=-1, keepdims=True), (t, HEAD_DIM))
                return carry

            lax.fori_loop(0, nq, fill, 0)

        kb, vb = k_ref[...], v_ref[...]

        def step(i, carry, masked):
            dk, dv = carry
            rs = pl.ds(pl.multiple_of(i * t, t), t)
            qb, dob = q_ref[rs, :], do_ref[rs, :].astype(BF16)
            s = _dot(qb, kb, NT) * ATT_SCALE
            p = jnp.exp(s - lse_ref[rs, 0:1])
            if masked:
                p = jnp.where(_diag_mask(t), p, 0.0)
            dp = _dot(dob, vb, NT)
            ds = (p * (dp - delta_ref[rs, 0:1]) * ATT_SCALE).astype(BF16)
            dq_ref[rs, :] += _dot(ds, kb)
            return dk + _dot(ds, qb, TN), dv + _dot(p.astype(BF16), dob, TN)

        carry = step(j, (jnp.zeros((t, QK_PAD), F32), jnp.zeros((t, HEAD_DIM), F32)), True)
        dk, dv = lax.fori_loop(j + 1, nq, lambda i, c: step(i, c, False), carry)
        dk_ref[...] = dk
        dv_ref[...] = dv

    return _call(
        body, "attn_bwd", (q, k, v, dmixcat, o, lse), grid=(HEADS, nq),
        in_specs=[pl.BlockSpec((None, T, QK_PAD), lambda h, j: (h, 0, 0)),
                  pl.BlockSpec((None, t, QK_PAD), lambda h, j: (h, j, 0)),
                  pl.BlockSpec((None, t, HEAD_DIM), lambda h, j: (h, j, 0)),
                  pl.BlockSpec((T, HEAD_DIM), lambda h, j: (0, HEADS + h)),
                  pl.BlockSpec((T, HEAD_DIM), lambda h, j: (0, h)),
                  pl.BlockSpec((None, T, HEAD_DIM), lambda h, j: (h, 0, 0))],
        out_specs=[pl.BlockSpec((None, T, QK_PAD), lambda h, j: (h, 0, 0)),
                   pl.BlockSpec((None, t, QK_PAD), lambda h, j: (h, j, 0)),
                   pl.BlockSpec((None, t, HEAD_DIM), lambda h, j: (h, j, 0))],
        out_shape=[jax.ShapeDtypeStruct((HEADS, T, QK_PAD), F32), jax.ShapeDtypeStruct((HEADS, T, QK_PAD), F32),
                   jax.ShapeDtypeStruct((HEADS, T, HEAD_DIM), F32)],
        scratch_shapes=[pltpu.VMEM((T, HEAD_DIM), F32)], exchange=exchange)


def _ln_fwd(r):
    mu = _lanemean(r)
    xc = r - mu
    rstd = lax.rsqrt(_lanemean(xc * xc) + LN_EPS)
    return xc * rstd, rstd


def _ln_bwd(dxh, xhat, rstd):
    return rstd * (dxh - _lanemean(dxh) - xhat * _lanemean(dxh * xhat))


def _mix_ln1(mixcat, w_out, x, g_a, ln1_g, ln1_b, sc_m, sh_m, exchange=None):
    T = x.shape[0]
    tm = min(ROW_TILE_SMALL, T)

    def body(mc_ref, w_ref, x_ref, ga_ref, g_ref, b_ref, sc_ref, sh_ref, mix_ref, xhat_ref, rstd_ref, u2_ref):
        mix = _dot(mc_ref[...], w_ref[...])
        mix_ref[...] = mix
        xhat, rstd = _ln_fwd(ALPHA * x_ref[...] + (1.0 + ga_ref[...]) * mix)
        xhat_ref[...] = xhat
        rstd_ref[...] = jnp.broadcast_to(rstd, (tm, 128))
        u2_ref[...] = _modulate(xhat * g_ref[...] + b_ref[...], sc_ref[...], sh_ref[...]).astype(BF16)

    row = pl.BlockSpec((tm, D_MODEL), lambda i: (i, 0))
    vec = _full((1, D_MODEL))
    return _call(
        body, "mix_ln1", (mixcat, w_out, x, g_a, ln1_g, ln1_b, sc_m, sh_m), grid=(T // tm,),
        in_specs=[row, _full(w_out.shape), row, vec, vec, vec, vec, vec],
        out_specs=[row, row, pl.BlockSpec((tm, 128), lambda i: (i, 0)), row],
        out_shape=[jax.ShapeDtypeStruct((T, D_MODEL), F32), jax.ShapeDtypeStruct((T, D_MODEL), F32),
                   jax.ShapeDtypeStruct((T, 128), F32), jax.ShapeDtypeStruct((T, D_MODEL), BF16)],
        exchange=exchange)


def _mlp_fwd(u2, w1, w2, xhat1, ln1_g, ln1_b, g_m, ln2_g, ln2_b, target):
    T = u2.shape[0]
    nf, tf = N_DEV, w1.shape[-1]
    tm = min(ROW_TILE, T)

    def body(u2_ref, w1_ref, w2_ref, xhat_ref, g1_ref, b1_ref, gm_ref, g2_ref, b2_ref, tgt_ref,
             r_ref, dr2_ref, dh_ref, dg2_ref, db2_ref, dgm_ref, loss_ref, acc_ref):
        i, f = pl.program_id(0), pl.program_id(1)

        @pl.when((i == 0) & (f == 0))
        def _():
            for ref in (dg2_ref, db2_ref, dgm_ref, loss_ref):
                ref[...] = jnp.zeros_like(ref)

        @pl.when(f == 0)
        def _():
            acc_ref[...] = jnp.zeros_like(acc_ref)

        r = jnp.maximum(_dot(u2_ref[...], w1_ref[...]), 0.0)
        r_ref[...] = r.astype(BF16)
        acc_ref[...] += _bdot(r * r, w2_ref[...])

        @pl.when(f == nf - 1)
        def _():
            h = acc_ref[...]
            x1 = xhat_ref[...] * g1_ref[...] + b1_ref[...]
            xhat2, rstd2 = _ln_fwd(ALPHA * x1 + (1.0 + gm_ref[...]) * h)
            err = xhat2 * g2_ref[...] + b2_ref[...] - tgt_ref[...]
            loss_ref[...] += jnp.sum(0.5 * _lanemean(err * err), axis=0, keepdims=True)
            dy = err * (1.0 / D_MODEL)
            dg2_ref[...] += _rowsum(dy * xhat2)
            db2_ref[...] += _rowsum(dy)
            dr2 = _ln_bwd(dy * g2_ref[...], xhat2, rstd2)
            dr2_ref[...] = dr2
            dgm_ref[...] += _rowsum(dr2 * h)
            dh_ref[...] = ((1.0 + gm_ref[...]) * dr2).astype(BF16)

    row = pl.BlockSpec((tm, D_MODEL), lambda i, f: (i, 0))
    vec = _full((1, D_MODEL))
    return pl.pallas_call(
        body, name="mlp_fwd", grid=(T // tm, nf),
        in_specs=[row, pl.BlockSpec((None, None, D_MODEL, tf), lambda i, f: (f % 2, f // 2, 0, 0)),
                  pl.BlockSpec((None, None, tf, D_MODEL), lambda i, f: (f % 2, f // 2, 0, 0)),
                  row, vec, vec, vec, vec, vec, row],
        out_specs=[pl.BlockSpec((tm, tf), lambda i, f: (i, f)), row, row, vec, vec, vec, _full((1, 128))],
        out_shape=[jax.ShapeDtypeStruct((T, nf * tf), BF16), jax.ShapeDtypeStruct((T, D_MODEL), F32),
                   jax.ShapeDtypeStruct((T, D_MODEL), BF16), jax.ShapeDtypeStruct((1, D_MODEL), F32),
                   jax.ShapeDtypeStruct((1, D_MODEL), F32), jax.ShapeDtypeStruct((1, D_MODEL), F32),
                   jax.ShapeDtypeStruct((1, 128), F32)],
        scratch_shapes=[pltpu.VMEM((tm, D_MODEL), F32)],
        compiler_params=_params(),
    )(u2, w1, w2, xhat1, ln1_g, ln1_b, g_m, ln2_g, ln2_b, target)


def _mlp_bwd(dh, w1, w2, r, dr2, xhat1, rstd1, mix, ln1_g, ln1_b, sc_m, g_a):
    T = dh.shape[0]
    nf, tf = N_DEV, w1.shape[-1]
    tm = min(ROW_TILE, T)

    def body(dh_ref, w1_ref, w2_ref, r_ref, dr2_ref, xhat_ref, rstd_ref, mix_ref, g1_ref, b1_ref, sc_ref, ga_ref,
             dhpre_ref, dr1_ref, dmix_ref, dsc_ref, dsh_ref, dg1_ref, db1_ref, dga_ref, acc_ref):
        i, f = pl.program_id(0), pl.program_id(1)

        @pl.when((i == 0) & (f == 0))
        def _():
            for ref in (dsc_ref, dsh_ref, dg1_ref, db1_ref, dga_ref):
                ref[...] = jnp.zeros_like(ref)

        @pl.when(f == 0)
        def _():
            acc_ref[...] = jnp.zeros_like(acc_ref)

        dhpre = (_dot(dh_ref[...], w2_ref[...], NT) * (2.0 * r_ref[...].astype(F32))).astype(BF16)
        dhpre_ref[...] = dhpre
        acc_ref[...] += _dot(dhpre, w1_ref[...], NT)

        @pl.when(f == nf - 1)
        def _():
            du2 = acc_ref[...]
            xhat = xhat_ref[...]
            x1 = xhat * g1_ref[...] + b1_ref[...]
            dx1 = ALPHA * dr2_ref[...] + du2 * (1.0 + sc_ref[...])
            dsc_ref[...] += _rowsum(du2 * x1)
            dsh_ref[...] += _rowsum(du2)
            dg1_ref[...] += _rowsum(dx1 * xhat)
            db1_ref[...] += _rowsum(dx1)
            dr1 = _ln_bwd(dx1 * g1_ref[...], xhat, rstd_ref[:, 0:1])
            dr1_ref[...] = dr1
            dga_ref[...] += _rowsum(dr1 * mix_ref[...])
            dmix_ref[...] = ((1.0 + ga_ref[...]) * dr1).astype(BF16)

    row = pl.BlockSpec((tm, D_MODEL), lambda i, f: (i, 0))
    vec = _full((1, D_MODEL))
    return pl.pallas_call(
        body, name="mlp_bwd", grid=(T // tm, nf),
        in_specs=[row, pl.BlockSpec((None, None, D_MODEL, tf), lambda i, f: (f % 2, f // 2, 0, 0)),
                  pl.BlockSpec((None, None, tf, D_MODEL), lambda i, f: (f % 2, f // 2, 0, 0)),
                  pl.BlockSpec((tm, tf), lambda i, f: (i, f)), row, row, pl.BlockSpec((tm, 128), lambda i, f: (i, 0)),
                  row, vec, vec, vec, vec],
        out_specs=[pl.BlockSpec((tm, tf), lambda i, f: (i, f)), row, row, vec, vec, vec, vec, vec],
        out_shape=[jax.ShapeDtypeStruct((T, nf * tf), BF16), jax.ShapeDtypeStruct((T, D_MODEL), F32),
                   jax.ShapeDtypeStruct((T, D_MODEL), BF16)] + [jax.ShapeDtypeStruct((1, D_MODEL), F32)] * 5,
        scratch_shapes=[pltpu.VMEM((tm, D_MODEL), F32)],
        compiler_params=_params(),
    )(dh, w1, w2, r, dr2, xhat1, rstd1, mix, ln1_g, ln1_b, sc_m, g_a)


def _input_bwd(dz_h, dz_m, w_in_ext, x, dr1, sc_a, exchange=None):
    T = x.shape[0]
    tm = min(ROW_TILE_SMALL, T)

    def body(dzh_ref, dzm_ref, w_ref, x_ref, dr1_ref, sc_ref, gx_ref, dsc_ref, dsh_ref):
        @pl.when(pl.program_id(0) == 0)
        def _():
            dsc_ref[...] = jnp.zeros_like(dsc_ref)
            dsh_ref[...] = jnp.zeros_like(dsh_ref)

        du = _bdot(dzh_ref[...], w_ref[:, 0:2048], NT) + _bdot(dzm_ref[...], w_ref[:, 2048:3072], NT)
        gx_ref[...] = ALPHA * dr1_ref[...] + du * (1.0 + sc_ref[...])
        dsc_ref[...] += _rowsum(du * x_ref[...])
        dsh_ref[...] += _rowsum(du)

    row = pl.BlockSpec((tm, D_MODEL), lambda i: (i, 0))
    vec = _full((1, D_MODEL))
    return _call(
        body, "input_bwd", (dz_h, dz_m, w_in_ext, x, dr1, sc_a), grid=(T // tm,),
        in_specs=[pl.BlockSpec((tm, 2048), lambda i: (i, 0)), row, _full(w_in_ext.shape), row, row, vec],
        out_specs=[row, vec, vec],
        out_shape=[jax.ShapeDtypeStruct((T, D_MODEL), F32), jax.ShapeDtypeStruct((1, D_MODEL), F32),
                   jax.ShapeDtypeStruct((1, D_MODEL), F32)], exchange=exchange)


def _adam_math(w, g, m, v):
    m = ADAM_B1 * m + (1.0 - ADAM_B1) * g
    v = ADAM_B2 * v + (1.0 - ADAM_B2) * (g * g)
    m_hat = m / (1.0 - ADAM_B1 ** ADAM_STEP)
    v_hat = v / (1.0 - ADAM_B2 ** ADAM_STEP)
    return -ADAM_LR * (m_hat / (jnp.sqrt(v_hat) + ADAM_EPS) + ADAM_WD * w), m, v


def _adam(g_slabs, w, m, v, name, g_fn=None, g_extra=()):
    R, C = w.shape
    tr = R if R <= 256 else 256
    assert R % tr == 0
    ns = 0 if g_slabs is None else g_slabs.shape[0]
    ne = len(g_extra)

    def body(*refs):
        e_refs = refs[:ne]
        refs = refs[ne:]
        if ns:
            gs_ref, refs = refs[0], refs[1:]
        w_ref, m_ref, v_ref, g_ref, d_ref, nm_ref, nv_ref = refs
        if g_fn is not None:
            g = g_fn(*e_refs)
        else:
            g = gs_ref[0].astype(F32)
            for s in range(1, ns):
                g = g + gs_ref[s].astype(F32)
        d, nm, nv = _adam_math(w_ref[...], g, m_ref[...], v_ref[...])
        g_ref[...] = g
        d_ref[...] = d
        nm_ref[...] = nm
        nv_ref[...] = nv

    blk = pl.BlockSpec((tr, C), lambda i: (i, 0))
    in_specs = [pl.BlockSpec((tr, e.shape[1]), lambda i: (i, 0)) if e.shape[0] == R else _full(e.shape) for e in g_extra]
    args = list(g_extra)
    if ns:
        in_specs.append(pl.BlockSpec((ns, tr, C), lambda i: (0, i, 0)))
        args.append(g_slabs)
    return pl.pallas_call(
        body, name=name, grid=(R // tr,), in_specs=in_specs + [blk] * 3, out_specs=[blk] * 4,
        out_shape=[jax.ShapeDtypeStruct((R, C), F32)] * 4, compiler_params=_params(),
    )(*args, w, m, v)


def _adam_small(small_all, params):
    n = len(params)

    def body(*refs):
        s_ref, refs = refs[0], refs[1:]
        wmv, loss_ref, outs = refs[:3 * n], refs[3 * n], refs[3 * n + 1:]
        tot = s_ref[0]
        for i in range(1, N_DEV):
            tot = tot + s_ref[i]
        loss_ref[...] = tot[:, SMALL_W - 128:]
        for j, (w, _, _, off) in enumerate(params):
            w_ref, m_ref, v_ref = wmv[3 * j:3 * j + 3]
            g_ref, d_ref, nm_ref, nv_ref = outs[4 * j:4 * j + 4]
            if w.shape[0] == 2:
                lb = _lower_bound(w_ref)
                g0 = tot[:, off:off + w.shape[1]] * lb * (1.0 - lb)
                rows = [(slice(0, 1), g0), (slice(1, 2), -g0)]
            else:
                rows = [(slice(0, 1), tot[:, off:off + w.shape[1]])]
            for rs, g in rows:
                d, nm, nv = _adam_math(w_ref[rs, :], g, m_ref[rs, :], v_ref[rs, :])
                g_ref[rs, :], d_ref[rs, :], nm_ref[rs, :], nv_ref[rs, :] = g, d, nm, nv

    out_shape = [jax.ShapeDtypeStruct((1, 128), F32)]
    for w, _, _, _ in params:
        out_shape += [jax.ShapeDtypeStruct(w.shape, F32)] * 4
    res = pl.pallas_call(body, name="adam_small", out_shape=out_shape, compiler_params=_params())(
        small_all, *[a for w, m, v, _ in params for a in (w, m, v)])
    return res[0], [tuple(res[1 + 4 * j:5 + 4 * j]) for j in range(n)]


def _rot_half_cols(w):
    return jnp.concatenate([-w[..., 32:], w[..., :32]], axis=-1)


def _unrot_half_cols(dw_rot):
    return jnp.concatenate([dw_rot[..., 32:], -dw_rot[..., :32]], axis=-1)


def _cols_from_slabs(g):
    if g.ndim == 4:
        g = jnp.transpose(g, (1, 0, 2, 3)).reshape((N_DEV,) + g.shape[2:])
    s, r, c = g.shape
    return jnp.transpose(g, (1, 0, 2)).reshape(r, s * c)


def _slabs_from_cols(w):
    r, c = w.shape
    return jnp.transpose(w.reshape(r, N_DEV, c // N_DEV), (1, 0, 2))


def _ext_in(w_in):
    k_in = w_in.shape[0]
    z64, z128 = jnp.zeros((k_in, 64), BF16), jnp.zeros((k_in, 128), BF16)
    wk = w_in[:, 2560:2624]
    return jnp.concatenate([w_in[:, :2560], z128, wk, z64, z128, _rot_half_cols(wk), z64], axis=1)


def _ext_q(w_q_up):
    r = w_q_up.shape[0]
    z64, z128 = jnp.zeros((r, 64), BF16), jnp.zeros((r, 128), BF16)
    wq = w_q_up.reshape(r, HEADS, HEAD_DIM + ROPE_DIM)
    main = [jnp.concatenate([wq[:, h, :HEAD_DIM], wq[:, h, HEAD_DIM:], z64], axis=1) for h in range(HEADS)]
    rot = [jnp.concatenate([z128, _rot_half_cols(wq[:, h, HEAD_DIM:]), z64], axis=1) for h in range(HEADS)]
    return jnp.concatenate(main + rot, axis=1)


def _ext_kv(w_kv_up):
    r = w_kv_up.shape[0]
    z128 = jnp.zeros((r, 128), BF16)
    wkv = w_kv_up.reshape(r, HEADS, 2 * HEAD_DIM)
    kpad = [jnp.concatenate([wkv[:, h, :HEAD_DIM], z128], axis=1) for h in range(HEADS)]
    vals = [wkv[:, h, HEAD_DIM:] for h in range(HEADS)]
    return jnp.concatenate(kpad + vals, axis=1)


def _grad_in_from_ext(dw_in_h, dw_in_m):
    dwk = dw_in_m[:, 512 + 128:512 + 192] + _unrot_half_cols(dw_in_m[:, 768 + 128:768 + 192])
    return jnp.concatenate([dw_in_h, dw_in_m[:, :512], dwk], axis=1)


def _grads_qkv_from_ext(dwq_ext, dwkv_ext):
    qcols = []
    for h in range(HEADS):
        main, rot = dwq_ext[:, 256 * h:256 * h + 256], dwq_ext[:, 1024 + 256 * h:1280 + 256 * h]
        qcols += [main[:, :128], main[:, 128:192] + _unrot_half_cols(rot[:, 128:192])]
    kvcols = []
    for h in range(HEADS):
        kvcols += [dwkv_ext[:, 256 * h:256 * h + 128], dwkv_ext[:, 1024 + 128 * h:1152 + 128 * h]]
    return jnp.concatenate(qcols, axis=1), jnp.concatenate(kvcols, axis=1)


SMALL_W = 6144 + 512 + 512 + 256 + 256 + 4 * 1024 + 128


def kernel(x, c, positions, w_ada, b_ada, w_in, hg_lower_bounds, hg_norm_w, mla_q_norm_w, w_q_up, mla_kv_norm_w, w_kv_up, w_out, ln1_g, ln1_b, w_mlp_in, w_mlp_out, ln2_g, ln2_b, loss_target, m_w_ada, m_b_ada, m_w_in, m_hg_lower_bounds, m_hg_norm_w, m_mla_q_norm_w, m_w_q_up, m_mla_kv_norm_w, m_w_kv_up, m_w_out, m_ln1_g, m_ln1_b, m_w_mlp_in, m_w_mlp_out, m_ln2_g, m_ln2_b, v_w_ada, v_b_ada, v_w_in, v_hg_lower_bounds, v_hg_norm_w, v_mla_q_norm_w, v_w_q_up, v_mla_kv_norm_w, v_w_kv_up, v_w_out, v_ln1_g, v_ln1_b, v_w_mlp_in, v_w_mlp_out, v_ln2_g, v_ln2_b):
    T = x.shape[1]
    me = 4 * lax.axis_index("x") + 2 * lax.axis_index("y") + lax.axis_index("c")
    xs, tgt = x[0], loss_target[0]
    big = dict(w_in=w_in[0], w_q_up=w_q_up[0], w_kv_up=w_kv_up[0], w_out=w_out[0], w_mlp_in=w_mlp_in[0],
               w_mlp_out=w_mlp_out[0])
    names = list(big)

    bf = {n: big[n].astype(BF16) for n in names}
    g_in, g_c = _gather_two_level([bf["w_in"], c], name="gather_w_in")
    c_all = jnp.transpose(g_c.reshape(2, 4, D_MODEL), (1, 0, 2)).reshape(N_DEV, D_MODEL)

    ada_cols = w_ada.shape[2]
    mod_part, cond = _mod_part(c_all, w_ada[0], lax.dynamic_slice(b_ada, (0, me * ada_cols), (1, ada_cols)))
    (mod_all,) = _exchange([mod_part], scatter=False, name="gather_mod")
    mod_row = lax.dynamic_slice(mod_all, (0, me, 0), (N_DEV, 1, ada_cols)).reshape(1, N_DEV * ada_cols)
    sh_a, sc_a, g_a, sh_m, sc_m, g_m = [mod_row[:, D_MODEL * i:D_MODEL * (i + 1)] for i in range(6)]

    w_in_ext = _ext_in(_cols_from_slabs(g_in))
    z, (g_q, g_kv, g_out) = _matmul(xs, w_in_ext, "NN", "in_proj", a_fn=_modulate, extras=(sc_a, sh_a),
                                    exchange=_Exchange([bf["w_q_up"], bf["w_kv_up"], bf["w_out"]], False))
    wq_ext, wkv_ext = _ext_q(_cols_from_slabs(g_q)), _ext_kv(_cols_from_slabs(g_kv))
    w_out_full = g_out.reshape(D_MODEL, D_MODEL)
    inv_freq = 1.0 / (ROPE_THETA ** (jnp.arange(0, ROPE_DIM, 2, dtype=F32) / ROPE_DIM))
    zeros = lambda n: jnp.zeros((n,), F32)
    invf = jnp.concatenate([zeros(128), inv_freq, inv_freq, zeros(64)]).reshape(1, QK_PAD)
    m_one = jnp.concatenate([jnp.ones((128,), F32), zeros(128)]).reshape(1, QK_PAD)
    m_rot = jnp.concatenate([zeros(128), jnp.ones((64,), F32), zeros(64)]).reshape(1, QK_PAD)
    q, k, v, c1, s1, cqn, ckvn = _mla_pre(z, positions.reshape(T, 1), invf, m_one, m_rot, wq_ext, wkv_ext,
                                          mla_q_norm_w, mla_kv_norm_w)
    (o_raw, o_gated, s_prev), (w1_chips,) = _hgrn_fwd(
        z, hg_lower_bounds, hg_norm_w, exchange=_Exchange([bf["w_mlp_in"]], False, "chips"))
    cut32 = lambda a: a.reshape((32, a.shape[1] // 8) + a.shape[2:])
    (o_mla, lse), (w1, w2_chips) = _attn_fwd(
        q, k, v, exchange=[_Exchange([cut32(w1_chips)], False, "pair", pieces=8),
                           _Exchange([bf["w_mlp_out"]], False, "chips")])
    w1 = w1.reshape((2, 4) + w1_chips.shape[1:])
    mixcat = jnp.concatenate([o_gated, o_mla.astype(BF16)], axis=1)
    (mix, xhat1, rstd1, u2), (w2,) = _mix_ln1(mixcat, w_out_full, xs, g_a, ln1_g, ln1_b, sc_m, sh_m,
                                              exchange=_Exchange([cut32(w2_chips)], False, "pair", pieces=8))
    w2 = w2.reshape((2, 4) + w2_chips.shape[1:])
    r, dr2, dh, dln2_g, dln2_b, dg_m, loss_part = _mlp_fwd(u2, w1, w2, xhat1, ln1_g, ln1_b, g_m, ln2_g, ln2_b, tgt)

    dhpre, dr1, dmix, dsc_m, dsh_m, dln1_g, dln1_b, dg_a = _mlp_bwd(dh, w1, w2, r, dr2, xhat1, rstd1, mix, ln1_g,
                                                                    ln1_b, sc_m, g_a)
    received = {}
    dw2 = _matmul(r, dh, "TN", "wgrad_mlp_out", out_dtype=BF16, a_fn=_square, tm=1024)
    dw1 = _matmul(u2, dhpre, "TN", "wgrad_mlp_in", out_dtype=BF16, tm=1024, out_slabs=N_DEV)
    dmixcat = _matmul(dmix, w_out_full, "NT", "dgrad_out")
    dw_out = _matmul(mixcat, dmix, "TN", "wgrad_out", out_dtype=BF16, tm=1024)
    (dz_h, dlb, dnw), (received["w_out"],) = _hgrn_bwd(
        dmixcat, z, o_raw, s_prev, hg_lower_bounds, hg_norm_w,
        exchange=_Exchange([dw_out.reshape(N_DEV, D_MODEL // N_DEV, D_MODEL)], True))
    (dq, dk, dv), (received["w_mlp_in"], received["w_mlp_out"]) = _attn_bwd(
        q, k, v, dmixcat, o_mla, lse,
        exchange=_Exchange([dw1, dw2.reshape(N_DEV, dw2.shape[0] // N_DEV, D_MODEL)], True))
    dz_m, dq_ext, dkv_ext, dqnw, dkvnw = _mla_bwd(dq, dk, dv, z, c1, s1, wq_ext, wkv_ext, mla_q_norm_w,
                                                   mla_kv_norm_w)
    dwq_ext = _matmul(cqn, dq_ext, "TN", "wgrad_q_up", tn=2048)
    dwkv_ext = _matmul(ckvn, dkv_ext, "TN", "wgrad_kv_up", tn=1536)
    dwq, dwkv = _grads_qkv_from_ext(dwq_ext, dwkv_ext)
    dw_in_h, (received["w_q_up"], received["w_kv_up"]) = _matmul(
        xs, dz_h, "TN", "wgrad_in_h", a_fn=_modulate, extras=(sc_a, sh_a), tm=1024,
        exchange=_Exchange([_slabs_from_cols(g).astype(BF16) for g in (dwq, dwkv)], True))
    dw_in_m = _matmul(xs, dz_m, "TN", "wgrad_in_m", a_fn=_modulate, extras=(sc_a, sh_a), tm=1024)
    dw_in = _grad_in_from_ext(dw_in_h, dw_in_m)
    (grad_x, dsc_a, dsh_a), (received["w_in"],) = _input_bwd(
        dz_h, dz_m, w_in_ext, xs, dr1, sc_a, exchange=_Exchange([_slabs_from_cols(dw_in).astype(BF16)], True))

    small = jnp.concatenate([dsh_a, dsc_a, dg_a, dsh_m, dsc_m, dg_m, dlb, dnw, dqnw, dkvnw, dln1_g, dln1_b, dln2_g,
                             dln2_b, loss_part], axis=1)
    (small_all,) = _exchange([small], scatter=False, name="gather_small")

    moments = dict(w_in=(m_w_in, v_w_in), w_q_up=(m_w_q_up, v_w_q_up), w_kv_up=(m_w_kv_up, v_w_kv_up),
                   w_out=(m_w_out, v_w_out), w_mlp_in=(m_w_mlp_in, v_w_mlp_in), w_mlp_out=(m_w_mlp_out, v_w_mlp_out))
    res = {}
    for n in names:
        res[n] = _adam(received[n], big[n], moments[n][0][0], moments[n][1][0], name="adam_" + n)
    dmod_cols = lax.dynamic_slice(small_all.reshape(N_DEV, SMALL_W), (0, me * ada_cols), (N_DEV, ada_cols))
    cond_t = cond.T

    def ada_grad(ct_ref, dm_ref):
        g = ct_ref[:, 0:1] * dm_ref[0:1, :]
        for b in range(1, N_DEV):
            g = g + ct_ref[:, b:b + 1] * dm_ref[b:b + 1, :]
        return g

    res["w_ada"] = _adam(None, w_ada[0], m_w_ada[0], v_w_ada[0], name="adam_w_ada", g_fn=ada_grad,
                         g_extra=(cond_t, dmod_cols))

    small_params = [("b_ada", b_ada, m_b_ada, v_b_ada, 0),
                    ("hg_lower_bounds", hg_lower_bounds, m_hg_lower_bounds, v_hg_lower_bounds, 6144),
                    ("hg_norm_w", hg_norm_w, m_hg_norm_w, v_hg_norm_w, 6656),
                    ("mla_q_norm_w", mla_q_norm_w, m_mla_q_norm_w, v_mla_q_norm_w, 7168),
                    ("mla_kv_norm_w", mla_kv_norm_w, m_mla_kv_norm_w, v_mla_kv_norm_w, 7424),
                    ("ln1_g", ln1_g, m_ln1_g, v_ln1_g, 7680), ("ln1_b", ln1_b, m_ln1_b, v_ln1_b, 8704),
                    ("ln2_g", ln2_g, m_ln2_g, v_ln2_g, 9728), ("ln2_b", ln2_b, m_ln2_b, v_ln2_b, 10752)]
    loss_row, small_res = _adam_small(small_all, [p[1:] for p in small_params])
    for p, r4 in zip(small_params, small_res):
        res[p[0]] = r4
    loss = loss_row[0, 0]

    order = ["w_ada", "b_ada", "w_in", "hg_lower_bounds", "hg_norm_w", "mla_q_norm_w", "w_q_up", "mla_kv_norm_w",
             "w_kv_up", "w_out", "ln1_g", "ln1_b", "w_mlp_in", "w_mlp_out", "ln2_g", "ln2_b"]
    shaped = {n: tuple(a.reshape((1,) + a.shape) if n in big or n == "w_ada" else a for a in res[n]) for n in order}
    outs = [loss, grad_x.reshape(1, T, D_MODEL)]
    for i in range(4):
        outs += [shaped[n][i] for n in order]
    return tuple(outs)
```

```python
import functools

import jax
import jax.numpy as jnp
import numpy as np
from jax import lax
from jax.experimental import pallas as pl
from jax.experimental.pallas import tpu as pltpu

F32, BF16 = jnp.float32, jnp.bfloat16
N_DEV = 8
D_MODEL = 1024
HEADS = 4
HEAD_DIM = 128
ROPE_DIM = 64
QK_PAD = 256
CHUNK = 64
ROPE_THETA = 10000.0
RMS_EPS = 1e-6
LN_EPS = 1e-5
ALPHA = 2.0 ** 0.25
ATT_SCALE = (HEAD_DIM + ROPE_DIM) ** -0.5
LN2 = float(np.log(2.0))
Q_PRESCALE = ATT_SCALE / LN2
ADAM_LR, ADAM_B1, ADAM_B2, ADAM_EPS, ADAM_WD, ADAM_STEP = 0.001, 0.9, 0.999, 1e-08, 0.01, 10
NEG_BIG = -1e30

ROW_TILE = 512
ROW_TILE_SMALL = 256
ATT_TILE = 512
HGRN_GROUP = 8
VMEM_LIMIT = 56 * 2 ** 20

NN = (((1,), (0,)), ((), ()))
NT = (((1,), (1,)), ((), ()))
TN = (((0,), (0,)), ((), ()))


def _dot(a, b, dims=NN):
    return lax.dot_general(a, b, dims, preferred_element_type=F32)


def _bdot(a, b, dims=NN):
    return lax.dot_general(a.astype(BF16), b.astype(BF16), dims, preferred_element_type=F32)


def _hdot(a, b, dims=NN):
    return lax.dot_general(a, b, dims, precision=lax.Precision.HIGHEST, preferred_element_type=F32)


def _params():
    return pltpu.CompilerParams(vmem_limit_bytes=VMEM_LIMIT)


def _sigmoid(x):
    return 1.0 / (1.0 + jnp.exp(-x))


def _rowsum(x):
    return jnp.sum(x, axis=0, keepdims=True)


def _lanemean(x):
    return jnp.mean(x, axis=-1, keepdims=True)


def _full(shape):
    nd = len(shape)
    return pl.BlockSpec(shape, lambda *_: (0,) * nd)


class _Exchange:
    def __init__(self, arrs, scatter):
        self.arrs, self.scatter, self.n = list(arrs), scatter, len(arrs)
        self.out_shape = [jax.ShapeDtypeStruct((N_DEV,) + (a.shape[1:] if scatter else a.shape), a.dtype)
                          for a in self.arrs]
        n = self.n
        self.scratch = [pltpu.SemaphoreType.DMA((n, N_DEV - 1)), pltpu.SemaphoreType.DMA((n, N_DEV - 1)),
                        pltpu.SemaphoreType.DMA((n,))]

    def _copies(self, ins, outs, sems):
        send_sems, recv_sems, loc_sems = sems
        x, y, c = lax.axis_index("x"), lax.axis_index("y"), lax.axis_index("c")
        me = 4 * x + 2 * y + c
        copies = []
        for k in range(self.n):
            src_of = (lambda i, k=k: ins[k].at[i]) if self.scatter else (lambda i, k=k: ins[k])
            copies.append((pltpu.make_async_copy(src_of(me), outs[k].at[me], loc_sems.at[k]), None))
            for p in range(1, N_DEV):
                px = (1 - x) if p & 4 else x
                py = (1 - y) if p & 2 else y
                pc = (1 - c) if p & 1 else c
                peer = 4 * px + 2 * py + pc
                both = dict(send_sem=send_sems.at[k, p - 1], recv_sem=recv_sems.at[k, p - 1],
                            device_id=(px, py, pc), device_id_type=pl.DeviceIdType.MESH)
                send = pltpu.make_async_remote_copy(src_ref=src_of(peer), dst_ref=outs[k].at[me], **both)
                recv = pltpu.make_async_remote_copy(src_ref=src_of(peer), dst_ref=outs[k].at[peer], **both)
                copies.append((send, recv))
        return copies

    def start(self, ins, outs, sems):
        for first, _ in self._copies(ins, outs, sems):
            first.start()

    def wait(self, ins, outs, sems):
        for first, recv in self._copies(ins, outs, sems):
            if recv is None:
                first.wait()
            else:
                recv.wait_recv()
                first.wait_send()


def _call(body, name, args, out_shape, grid=(), in_specs=(), out_specs=(), scratch_shapes=(), exchange=None):
    if exchange is None:
        return pl.pallas_call(body, name=name, grid=grid, in_specs=list(in_specs), out_specs=list(out_specs),
                              out_shape=list(out_shape), scratch_shapes=list(scratch_shapes),
                              compiler_params=_params())(*args), None
    exs = list(exchange) if isinstance(exchange, (list, tuple)) else [exchange]
    ni, no, ns, nx = len(args), len(out_shape), len(scratch_shapes), sum(e.n for e in exs)

    def wrapped(*refs):
        a, xi = refs[:ni], refs[ni:ni + nx]
        o, xo = refs[ni + nx:ni + nx + no], refs[ni + nx + no:ni + 2 * nx + no]
        s, xs = refs[ni + 2 * nx + no:ni + 2 * nx + no + ns], refs[ni + 2 * nx + no + ns:]
        parts, at = [], 0
        for j, e in enumerate(exs):
            parts.append((e, xi[at:at + e.n], xo[at:at + e.n], xs[3 * j:3 * j + 3]))
            at += e.n
        first = last = None
        for d, g in enumerate(grid):
            f, l = pl.program_id(d) == 0, pl.program_id(d) == g - 1
            first, last = (f, l) if first is None else (first & f, last & l)

        @pl.when(first)
        def _():
            for e, ins, outs, sems in parts:
                e.start(ins, outs, sems)

        body(*a, *o, *s)

        @pl.when(last)
        def _():
            for e, ins, outs, sems in parts:
                e.wait(ins, outs, sems)

    hbm = pl.BlockSpec(memory_space=pltpu.HBM)
    res = pl.pallas_call(
        wrapped, name=name, grid=grid, in_specs=list(in_specs) + [hbm] * nx, out_specs=list(out_specs) + [hbm] * nx,
        out_shape=list(out_shape) + [o_ for e in exs for o_ in e.out_shape],
        scratch_shapes=list(scratch_shapes) + [s_ for e in exs for s_ in e.scratch],
        compiler_params=_params())(*args, *[a_ for e in exs for a_ in e.arrs])
    return res[:no], res[no:]


def _exchange(arrs, scatter, name):
    ex = _Exchange(arrs, scatter)

    def body(*refs):
        ins, outs, sems = refs[:ex.n], refs[ex.n:2 * ex.n], refs[2 * ex.n:]
        ex.start(ins, outs, sems)
        ex.wait(ins, outs, sems)

    hbm = pl.BlockSpec(memory_space=pltpu.HBM)
    return pl.pallas_call(body, name=name, out_shape=ex.out_shape, in_specs=[hbm] * ex.n, out_specs=[hbm] * ex.n,
                          scratch_shapes=ex.scratch)(*ex.arrs)


def _matmul(a, b, mode, name, out_dtype=F32, tm=512, tn=1024, tk=1024, a_fn=None, extras=(), out_slabs=None,
            exchange=None):
    if mode == "NN":
        (M, K), N = a.shape, b.shape[1]
    elif mode == "NT":
        (M, K), N = a.shape, b.shape[0]
    else:
        (K, M), N = a.shape, b.shape[1]
    if out_slabs:
        tn = N // out_slabs
    tm, tn, tk = min(tm, M), min(tn, N), min(tk, K)
    assert M % tm == 0 and N % tn == 0 and K % tk == 0, (name, M, N, K)
    nk = K // tk
    dims = {"NN": NN, "NT": NT, "TN": TN}[mode]
    ne = len(extras)

    def body(a_ref, b_ref, *rest):
        e_refs, o_ref, acc_ref = rest[:ne], rest[ne], rest[ne + 1]
        k = pl.program_id(2)

        @pl.when(k == 0)
        def _():
            acc_ref[...] = jnp.zeros_like(acc_ref)

        at = a_ref[...]
        if a_fn is not None:
            at = a_fn(at.astype(F32), *[e[...] for e in e_refs])
        acc_ref[...] += _bdot(at, b_ref[...], dims)

        @pl.when(k == nk - 1)
        def _():
            o_ref[...] = acc_ref[...].astype(out_dtype)

    if mode == "TN":
        a_spec = pl.BlockSpec((tk, tm), lambda i, j, k: (k, i))
        e_spec = pl.BlockSpec((1, tm), lambda i, j, k: (0, i))
    else:
        a_spec = pl.BlockSpec((tm, tk), lambda i, j, k: (i, k))
        e_spec = pl.BlockSpec((1, tk), lambda i, j, k: (0, k))
    if mode == "NT":
        b_spec = pl.BlockSpec((tn, tk), lambda i, j, k: (j, k))
    else:
        b_spec = pl.BlockSpec((tk, tn), lambda i, j, k: (k, j))
    if out_slabs:
        o_shape = jax.ShapeDtypeStruct((out_slabs, M, tn), out_dtype)
        o_spec = pl.BlockSpec((None, tm, tn), lambda i, j, k: (j, i, 0))
    else:
        o_shape = jax.ShapeDtypeStruct((M, N), out_dtype)
        o_spec = pl.BlockSpec((tm, tn), lambda i, j, k: (i, j))
    (out,), got = _call(body, name, (a, b, *extras), [o_shape], grid=(M // tm, N // tn, nk),
                        in_specs=[a_spec, b_spec] + [e_spec] * ne, out_specs=[o_spec],
                        scratch_shapes=[pltpu.VMEM((tm, tn), F32)], exchange=exchange)
    return out if exchange is None else (out, got)


def _modulate(x, sc, sh):
    return x * (1.0 + sc) + sh


def _square(x):
    return x * x


def _mod_part(c_all, w_ada_s, b_s):
    def body(c_ref, w_ref, b_ref, mod_ref, cond_ref):
        cv = c_ref[...]
        cond = cv * _sigmoid(cv)
        cond_ref[...] = cond
        mod_ref[...] = _bdot(cond, w_ref[...]) + b_ref[...]

    return pl.pallas_call(
        body, name="mod_part",
        out_shape=[jax.ShapeDtypeStruct((N_DEV, w_ada_s.shape[1]), F32), jax.ShapeDtypeStruct(c_all.shape, F32)],
        compiler_params=_params(),
    )(c_all, w_ada_s, b_s)


def _rms_fwd(x, w):
    rs = lax.rsqrt(_lanemean(x * x) + RMS_EPS)
    return x * rs * w, rs


def _rms_bwd(x, rs, w, dy):
    xhat = x * rs
    dxh = dy * w
    return rs * (dxh - xhat * _lanemean(dxh * xhat)), dy * xhat


def _mla_pre(z, pos_col, invf, m_one, m_rot, wq_ext, wkv_ext, qnw, kvnw):
    T = z.shape[0]
    tm = min(ROW_TILE, T)

    def body(z_ref, pos_ref, invf_ref, mone_ref, mrot_ref, wq_ref, wkv_ref, qnw_ref, kvnw_ref,
             q_ref, k_ref, v_ref, c1_ref, s1_ref, cqn_ref, ckvn_ref):
        ang = pos_ref[...].astype(F32) * invf_ref[...]
        c1 = mone_ref[...] + mrot_ref[...] * jnp.cos(ang)
        s1 = mrot_ref[...] * jnp.sin(ang)
        c1_ref[...] = c1
        s1_ref[...] = s1
        cqn, _ = _rms_fwd(z_ref[:, 0:256], qnw_ref[...])
        ckvn, _ = _rms_fwd(z_ref[:, 256:512], kvnw_ref[...])
        cqn_ref[...] = cqn.astype(BF16)
        ckvn_ref[...] = ckvn.astype(BF16)
        qe = _bdot(cqn, wq_ref[...])
        kve = _bdot(ckvn, wkv_ref[...])
        k_rope = z_ref[:, 512:768] * c1 + z_ref[:, 768:1024] * s1
        for h in range(HEADS):
            q_ref[h] = ((qe[:, 256 * h:256 * h + 256] * c1 + qe[:, 1024 + 256 * h:1280 + 256 * h] * s1)
                        * Q_PRESCALE).astype(BF16)
            k_ref[h] = (kve[:, 256 * h:256 * h + 256] + k_rope).astype(BF16)
            v_ref[h] = kve[:, 1024 + 128 * h:1152 + 128 * h].astype(BF16)

    row = lambda i: (i, 0)
    head = lambda i: (0, i, 0)
    return pl.pallas_call(
        body, name="mla_pre", grid=(T // tm,),
        in_specs=[pl.BlockSpec((tm, 1024), lambda i: (i, 2)), pl.BlockSpec((tm, 1), row),
                  _full((1, 256)), _full((1, 256)), _full((1, 256)), _full(wq_ext.shape), _full(wkv_ext.shape),
                  _full((1, 256)), _full((1, 256))],
        out_specs=[pl.BlockSpec((HEADS, tm, QK_PAD), head), pl.BlockSpec((HEADS, tm, QK_PAD), head),
                   pl.BlockSpec((HEADS, tm, HEAD_DIM), head), pl.BlockSpec((tm, 256), row), pl.BlockSpec((tm, 256), row),
                   pl.BlockSpec((tm, 256), row), pl.BlockSpec((tm, 256), row)],
        out_shape=[jax.ShapeDtypeStruct((HEADS, T, QK_PAD), BF16), jax.ShapeDtypeStruct((HEADS, T, QK_PAD), BF16),
                   jax.ShapeDtypeStruct((HEADS, T, HEAD_DIM), BF16), jax.ShapeDtypeStruct((T, 256), F32),
                   jax.ShapeDtypeStruct((T, 256), F32), jax.ShapeDtypeStruct((T, 256), BF16),
                   jax.ShapeDtypeStruct((T, 256), BF16)],
        compiler_params=_params(),
    )(z, pos_col, invf, m_one, m_rot, wq_ext, wkv_ext, qnw, kvnw)


def _mla_bwd(dq, dk, dv, z, c1, s1, wq_ext, wkv_ext, qnw, kvnw):
    T = z.shape[0]
    tm = min(ROW_TILE_SMALL, T)

    def body(dq_ref, dk_ref, dv_ref, z_ref, c1_ref, s1_ref, wq_ref, wkv_ref, qnw_ref, kvnw_ref,
             dz_ref, dqe_ref, dkve_ref, dqnw_ref, dkvnw_ref):
        @pl.when(pl.program_id(0) == 0)
        def _():
            dqnw_ref[...] = jnp.zeros_like(dqnw_ref)
            dkvnw_ref[...] = jnp.zeros_like(dkvnw_ref)

        c1, s1 = c1_ref[...], s1_ref[...]
        dkpe = jnp.zeros((tm, QK_PAD), F32)
        for h in range(HEADS):
            dqh, dkh = dq_ref[h] * Q_PRESCALE, dk_ref[h]
            dqe_ref[:, 256 * h:256 * h + 256] = (dqh * c1).astype(BF16)
            dqe_ref[:, 1024 + 256 * h:1280 + 256 * h] = (dqh * s1).astype(BF16)
            dkve_ref[:, 256 * h:256 * h + 256] = dkh.astype(BF16)
            dkve_ref[:, 1024 + 128 * h:1152 + 128 * h] = dv_ref[h].astype(BF16)
            dkpe = dkpe + dkh
        dcqn = _dot(dqe_ref[...], wq_ref[...], NT)
        dckvn = _dot(dkve_ref[...], wkv_ref[...], NT)
        cq, ckv = z_ref[:, 0:256], z_ref[:, 256:512]
        _, rsq = _rms_fwd(cq, qnw_ref[...])
        _, rskv = _rms_fwd(ckv, kvnw_ref[...])
        dcq, wq_rows = _rms_bwd(cq, rsq, qnw_ref[...], dcqn)
        dckv, wkv_rows = _rms_bwd(ckv, rskv, kvnw_ref[...], dckvn)
        dqnw_ref[...] += _rowsum(wq_rows)
        dkvnw_ref[...] += _rowsum(wkv_rows)
        dz_ref[:, 0:256] = dcq
        dz_ref[:, 256:512] = dckv
        dz_ref[:, 512:768] = dkpe * c1
        dz_ref[:, 768:1024] = dkpe * s1

    row = lambda i: (i, 0)
    head = lambda i: (0, i, 0)
    return pl.pallas_call(
        body, name="mla_bwd", grid=(T // tm,),
        in_specs=[pl.BlockSpec((HEADS, tm, QK_PAD), head), pl.BlockSpec((HEADS, tm, QK_PAD), head),
                  pl.BlockSpec((HEADS, tm, HEAD_DIM), head), pl.BlockSpec((tm, 1024), lambda i: (i, 2)),
                  pl.BlockSpec((tm, 256), row), pl.BlockSpec((tm, 256), row), _full(wq_ext.shape), _full(wkv_ext.shape),
                  _full((1, 256)), _full((1, 256))],
        out_specs=[pl.BlockSpec((tm, 1024), row), pl.BlockSpec((tm, 2048), row), pl.BlockSpec((tm, 1536), row),
                   _full((1, 256)), _full((1, 256))],
        out_shape=[jax.ShapeDtypeStruct((T, 1024), F32), jax.ShapeDtypeStruct((T, 2048), BF16),
                   jax.ShapeDtypeStruct((T, 1536), BF16), jax.ShapeDtypeStruct((1, 256), F32),
                   jax.ShapeDtypeStruct((1, 256), F32)],
        compiler_params=_params(),
    )(dq, dk, dv, z, c1, s1, wq_ext, wkv_ext, qnw, kvnw)


_HEAD_LANES = [slice(HEAD_DIM * h, HEAD_DIM * (h + 1)) for h in range(HEADS)]


def _lower_bound(lbraw_ref):
    a0, a1 = lbraw_ref[0:1, :], lbraw_ref[1:2, :]
    mx = jnp.maximum(a0, a1)
    e0, e1 = jnp.exp(a0 - mx), jnp.exp(a1 - mx)
    return e0 / (e0 + e1)


def _tri(lower):
    r = lax.broadcasted_iota(jnp.int32, (CHUNK, CHUNK), 0)
    c = lax.broadcasted_iota(jnp.int32, (CHUNK, CHUNK), 1)
    return (r >= c) if lower else (r <= c)


def _hgrn_gates(q, f, lb, tri_lo):
    sg = _sigmoid(f)
    forget = lb + (1.0 - lb) * sg
    k = 1.0 - forget
    b = _hdot(tri_lo.astype(F32), jnp.log(forget))
    b_ref, b_last = b[CHUNK // 2 - 1:CHUNK // 2, :], b[CHUNK - 1:CHUNK, :]
    e1, e2, e3, e4 = jnp.exp(b - b_ref), jnp.exp(b_ref - b), jnp.exp(b_last - b), jnp.exp(b)
    return dict(sg=sg, forget=forget, k=k, e1=e1, e2=e2, e3=e3, e4=e4, qa=q * e1, ka=k * e2, kl=k * e3, qb=q * e4,
                decay=jnp.exp(b_last))


def _hgrn_fwd(z, lbraw, nw, exchange=None):
    T = z.shape[0]
    G = min(HGRN_GROUP, T // CHUNK)
    rows = G * CHUNK
    n_chunks = T // CHUNK

    def body(q_ref, f_ref, i_ref, g_ref, lbraw_ref, nw_ref, oraw_ref, og_ref, sp_ref, st_ref):
        @pl.when(pl.program_id(0) == 0)
        def _():
            st_ref[...] = jnp.zeros_like(st_ref)

        lb_all = _lower_bound(lbraw_ref)
        tri_lo = _tri(True)

        def chunk(cc, carry):
            rs = pl.ds(pl.multiple_of(cc * CHUNK, CHUNK), CHUNK)
            t = _hgrn_gates(q_ref[rs, :], f_ref[rs, :], lb_all, tri_lo)
            v, gate = i_ref[rs, :], g_ref[rs, :]
            st = [st_ref[h] for h in range(HEADS)]
            a = [jnp.where(tri_lo, _bdot(t["qa"][:, s], t["ka"][:, s], NT), 0.0) for s in _HEAD_LANES]
            kv = [_bdot(v[:, s], t["kl"][:, s], TN) for s in _HEAD_LANES]
            o = [_bdot(a[h], v[:, s]) + _bdot(t["qb"][:, s], st[h], NT) for h, s in enumerate(_HEAD_LANES)]
            for h, s in enumerate(_HEAD_LANES):
                sp_ref[cc, h] = st[h]
                st_ref[h] = st[h] * t["decay"][:, s] + kv[h]
            oraw_ref[rs, :] = jnp.concatenate(o, axis=1)
            on = jnp.concatenate([_rms_fwd(o[h], nw_ref[:, s])[0] for h, s in enumerate(_HEAD_LANES)], axis=1)
            og_ref[rs, :] = (on * (gate * _sigmoid(gate))).astype(BF16)
            return carry

        lax.fori_loop(0, G, chunk, 0, unroll=2)

    col = lambda j: pl.BlockSpec((rows, 512), lambda r, j=j: (r, j))
    return _call(
        body, "hgrn_fwd", (z, z, z, z, lbraw, nw), grid=(T // rows,),
        in_specs=[col(0), col(1), col(2), col(3), _full((2, 512)), _full((1, 512))],
        out_specs=[col(0), col(0), pl.BlockSpec((G, HEADS, HEAD_DIM, HEAD_DIM), lambda r: (r, 0, 0, 0))],
        out_shape=[jax.ShapeDtypeStruct((T, 512), F32), jax.ShapeDtypeStruct((T, 512), BF16),
                   jax.ShapeDtypeStruct((n_chunks, HEADS, HEAD_DIM, HEAD_DIM), F32)],
        scratch_shapes=[pltpu.VMEM((HEADS, HEAD_DIM, HEAD_DIM), F32)], exchange=exchange)


def _hgrn_bwd(dmixcat, z, oraw, sprev, lbraw, nw, exchange=None):
    T = z.shape[0]
    G = min(HGRN_GROUP, T // CHUNK)
    rows = G * CHUNK
    ng = T // rows

    def body(dog_ref, q_ref, f_ref, i_ref, g_ref, oraw_ref, sp_ref, lbraw_ref, nw_ref,
             dz_ref, dlb_ref, dnw_ref, dst_ref):
        @pl.when(pl.program_id(0) == 0)
        def _():
            dst_ref[...] = jnp.zeros_like(dst_ref)
            dlb_ref[...] = jnp.zeros_like(dlb_ref)
            dnw_ref[...] = jnp.zeros_like(dnw_ref)

        lb_all = _lower_bound(lbraw_ref)
        tri_lo, tri_up = _tri(True), _tri(False)
        rowid = lax.broadcasted_iota(jnp.int32, (CHUNK, HEADS * HEAD_DIM), 0)

        def chunk(it, carry):
            cc = G - 1 - it
            rs = pl.ds(pl.multiple_of(cc * CHUNK, CHUNK), CHUNK)
            heads = list(enumerate(_HEAD_LANES))
            cat = lambda parts: jnp.concatenate(parts, axis=1)
            per_head_mean = lambda x: cat([jnp.broadcast_to(_lanemean(x[:, s]), (CHUNK, HEAD_DIM)) for s in _HEAD_LANES])
            t = _hgrn_gates(q_ref[rs, :], f_ref[rs, :], lb_all, tri_lo)
            v, gate, o, dog, nw_all = i_ref[rs, :], g_ref[rs, :], oraw_ref[rs, :], dog_ref[rs, :], nw_ref[...]
            rs_o = lax.rsqrt(per_head_mean(o * o) + RMS_EPS)
            xhat = o * rs_o
            sgg = _sigmoid(gate)
            d_on = dog * (gate * sgg)
            dz_ref[rs, 1536:2048] = dog * (xhat * nw_all) * (sgg * (1.0 + gate * (1.0 - sgg)))
            dxh = d_on * nw_all
            do = rs_o * (dxh - xhat * per_head_mean(dxh * xhat))
            dnw_ref[...] += _rowsum(d_on * xhat)
            st = [sp_ref[cc, h] for h in range(HEADS)]
            dst = [dst_ref[h] for h in range(HEADS)]
            a = [jnp.where(tri_lo, _bdot(t["qa"][:, s], t["ka"][:, s], NT), 0.0) for s in _HEAD_LANES]
            da = [jnp.where(tri_lo, _bdot(do[:, s], v[:, s], NT), 0.0) for s in _HEAD_LANES]
            dqb = cat([_bdot(do[:, s], st[h]) for h, s in heads])
            dkl = cat([_bdot(v[:, s], dst[h]) for h, s in heads])
            dv_ = cat([_bdot(t["kl"][:, s], dst[h], NT) + _bdot(a[h], do[:, s], TN) for h, s in heads])
            dqa = cat([_bdot(da[h], t["ka"][:, s]) for h, s in heads])
            dka = cat([_bdot(da[h], t["qa"][:, s], TN) for h, s in heads])
            ddecay = cat([_rowsum(dst[h] * st[h]) for h in range(HEADS)])
            for h, s in heads:
                dst_ref[h] = dst[h] * t["decay"][:, s] + _bdot(do[:, s], t["qb"][:, s], TN)
            pa, pk, pb, pl_ = dqa * t["qa"], dka * t["ka"], dqb * t["qb"], dkl * t["kl"]
            db = pa - pk + pb - pl_
            db = db + jnp.where(rowid == CHUNK // 2 - 1, _rowsum(pk - pa), 0.0)
            db = db + jnp.where(rowid == CHUNK - 1, _rowsum(pl_) + ddecay * t["decay"], 0.0)
            dlogf = _hdot(tri_up.astype(F32), db)
            dforget = dlogf / t["forget"] - (dka * t["e2"] + dkl * t["e3"])
            sg = t["sg"]
            dz_ref[rs, 0:512] = dqa * t["e1"] + dqb * t["e4"]
            dz_ref[rs, 512:1024] = dforget * (1.0 - lb_all) * sg * (1.0 - sg)
            dz_ref[rs, 1024:1536] = dv_
            dlb_ref[...] += _rowsum(dforget * (1.0 - sg))
            return carry

        lax.fori_loop(0, G, chunk, 0, unroll=2)

    col = lambda j: pl.BlockSpec((rows, 512), lambda r, j=j: (ng - 1 - r, j))
    return _call(
        body, "hgrn_bwd", (dmixcat, z, z, z, z, oraw, sprev, lbraw, nw), grid=(ng,),
        in_specs=[col(0), col(0), col(1), col(2), col(3), col(0),
                  pl.BlockSpec((G, HEADS, HEAD_DIM, HEAD_DIM), lambda r: (ng - 1 - r, 0, 0, 0)),
                  _full((2, 512)), _full((1, 512))],
        out_specs=[pl.BlockSpec((rows, 2048), lambda r: (ng - 1 - r, 0)), _full((1, 512)), _full((1, 512))],
        out_shape=[jax.ShapeDtypeStruct((T, 2048), F32), jax.ShapeDtypeStruct((1, 512), F32),
                   jax.ShapeDtypeStruct((1, 512), F32)],
        scratch_shapes=[pltpu.VMEM((HEADS, HEAD_DIM, HEAD_DIM), F32)], exchange=exchange)


def _diag_mask(t):
    r = lax.broadcasted_iota(jnp.int32, (t, t), 0)
    c = lax.broadcasted_iota(jnp.int32, (t, t), 1)
    return r >= c


def _attn_fwd(q, k, v, exchange=None):
    _, T, _ = q.shape
    t = min(ATT_TILE, T)

    def body(q_ref, k_ref, v_ref, o_ref, lse_ref):
        i = pl.program_id(1)
        qb = q_ref[...]

        def step(j, carry, masked):
            m, l, acc = carry
            ks = pl.ds(pl.multiple_of(j * t, t), t)
            s = _dot(qb, k_ref[ks, :], NT)
            if masked:
                s = jnp.where(_diag_mask(t), s, NEG_BIG)
            mn = jnp.maximum(m, jnp.max(s, axis=-1, keepdims=True))
            p = jnp.exp2(s - mn)
            al = jnp.exp2(m - mn)
            return mn, al * l + jnp.sum(p, axis=-1, keepdims=True), al * acc + _dot(p.astype(BF16), v_ref[ks, :])

        init = (jnp.full((t, 1), NEG_BIG, F32), jnp.zeros((t, 1), F32), jnp.zeros((t, HEAD_DIM), F32))
        carry = lax.fori_loop(0, i, lambda j, c: step(j, c, False), init)
        m, l, acc = step(i, carry, True)
        o_ref[...] = acc / l
        lse_ref[...] = jnp.broadcast_to(m + jnp.log2(l), (t, HEAD_DIM))

    return _call(
        body, "attn_fwd", (q, k, v), grid=(HEADS, T // t),
        in_specs=[pl.BlockSpec((None, t, QK_PAD), lambda h, i: (h, i, 0)),
                  pl.BlockSpec((None, T, QK_PAD), lambda h, i: (h, 0, 0)),
                  pl.BlockSpec((None, T, HEAD_DIM), lambda h, i: (h, 0, 0))],
        out_specs=[pl.BlockSpec((t, HEAD_DIM), lambda h, i: (i, h)),
                   pl.BlockSpec((None, t, HEAD_DIM), lambda h, i: (h, i, 0))],
        out_shape=[jax.ShapeDtypeStruct((T, HEADS * HEAD_DIM), F32), jax.ShapeDtypeStruct((HEADS, T, HEAD_DIM), F32)],
        exchange=exchange)


def _attn_bwd(q, k, v, dmixcat, o, lse, exchange=None):
    _, T, _ = q.shape
    t = min(ATT_TILE, T)
    nq = T // t

    def body(q_ref, k_ref, v_ref, do_ref, o_ref, lse_ref, dq_ref, dk_ref, dv_ref, delta_ref):
        j = pl.program_id(1)

        @pl.when(j == 0)
        def _():
            dq_ref[...] = jnp.zeros_like(dq_ref)

            def fill(i, carry):
                rs = pl.ds(pl.multiple_of(i * t, t), t)
                delta_ref[rs, :] = jnp.broadcast_to(
                    jnp.sum(do_ref[rs, :] * o_ref[rs, :], axis=-1, keepdims=True), (t, HEAD_DIM))
                return carry

            lax.fori_loop(0, nq, fill, 0)

        kb, vb = k_ref[...], v_ref[...]

        def step(i, carry, masked):
            dk, dv = carry
            rs = pl.ds(pl.multiple_of(i * t, t), t)
            qb, dob = q_ref[rs, :], do_ref[rs, :].astype(BF16)
            p = jnp.exp2(_dot(qb, kb, NT) - lse_ref[rs, 0:1])
            if masked:
                p = jnp.where(_diag_mask(t), p, 0.0)
            dp = _dot(dob, vb, NT)
            ds = (p * (dp - delta_ref[rs, 0:1]) * LN2).astype(BF16)
            dq_ref[rs, :] += _dot(ds, kb)
            return dk + _dot(ds, qb, TN), dv + _dot(p.astype(BF16), dob, TN)

        carry = step(j, (jnp.zeros((t, QK_PAD), F32), jnp.zeros((t, HEAD_DIM), F32)), True)
        dk, dv = lax.fori_loop(j + 1, nq, lambda i, c: step(i, c, False), carry)
        dk_ref[...] = dk
        dv_ref[...] = dv

    return _call(
        body, "attn_bwd", (q, k, v, dmixcat, o, lse), grid=(HEADS, nq),
        in_specs=[pl.BlockSpec((None, T, QK_PAD), lambda h, j: (h, 0, 0)),
                  pl.BlockSpec((None, t, QK_PAD), lambda h, j: (h, j, 0)),
                  pl.BlockSpec((None, t, HEAD_DIM), lambda h, j: (h, j, 0)),
                  pl.BlockSpec((T, HEAD_DIM), lambda h, j: (0, HEADS + h)),
                  pl.BlockSpec((T, HEAD_DIM), lambda h, j: (0, h)),
                  pl.BlockSpec((None, T, HEAD_DIM), lambda h, j: (h, 0, 0))],
        out_specs=[pl.BlockSpec((None, T, QK_PAD), lambda h, j: (h, 0, 0)),
                   pl.BlockSpec((None, t, QK_PAD), lambda h, j: (h, j, 0)),
                   pl.BlockSpec((None, t, HEAD_DIM), lambda h, j: (h, j, 0))],
        out_shape=[jax.ShapeDtypeStruct((HEADS, T, QK_PAD), F32), jax.ShapeDtypeStruct((HEADS, T, QK_PAD), F32),
                   jax.ShapeDtypeStruct((HEADS, T, HEAD_DIM), F32)],
        scratch_shapes=[pltpu.VMEM((T, HEAD_DIM), F32)], exchange=exchange)


def _ln_fwd(r):
    mu = _lanemean(r)
    xc = r - mu
    rstd = lax.rsqrt(_lanemean(xc * xc) + LN_EPS)
    return xc * rstd, rstd


def _ln_bwd(dxh, xhat, rstd):
    return rstd * (dxh - _lanemean(dxh) - xhat * _lanemean(dxh * xhat))


def _mix_ln1(mixcat, w_out, x, g_a, ln1_g, ln1_b, sc_m, sh_m, exchange=None):
    T = x.shape[0]
    tm = min(ROW_TILE_SMALL, T)

    def body(mc_ref, w_ref, x_ref, ga_ref, g_ref, b_ref, sc_ref, sh_ref, mix_ref, xhat_ref, rstd_ref, u2_ref):
        mix = _dot(mc_ref[...], w_ref[...])
        mix_ref[...] = mix
        xhat, rstd = _ln_fwd(ALPHA * x_ref[...] + (1.0 + ga_ref[...]) * mix)
        xhat_ref[...] = xhat
        rstd_ref[...] = jnp.broadcast_to(rstd, (tm, 128))
        u2_ref[...] = _modulate(xhat * g_ref[...] + b_ref[...], sc_ref[...], sh_ref[...]).astype(BF16)

    row = pl.BlockSpec((tm, D_MODEL), lambda i: (i, 0))
    vec = _full((1, D_MODEL))
    return _call(
        body, "mix_ln1", (mixcat, w_out, x, g_a, ln1_g, ln1_b, sc_m, sh_m), grid=(T // tm,),
        in_specs=[row, _full(w_out.shape), row, vec, vec, vec, vec, vec],
        out_specs=[row, row, pl.BlockSpec((tm, 128), lambda i: (i, 0)), row],
        out_shape=[jax.ShapeDtypeStruct((T, D_MODEL), F32), jax.ShapeDtypeStruct((T, D_MODEL), F32),
                   jax.ShapeDtypeStruct((T, 128), F32), jax.ShapeDtypeStruct((T, D_MODEL), BF16)],
        exchange=exchange)


def _mlp_fwd(u2, w1, w2, xhat1, ln1_g, ln1_b, g_m, ln2_g, ln2_b, target):
    T = u2.shape[0]
    nf, tf = N_DEV, w1.shape[-1]
    tm = min(ROW_TILE, T)

    def body(u2_ref, w1_ref, w2_ref, xhat_ref, g1_ref, b1_ref, gm_ref, g2_ref, b2_ref, tgt_ref,
             r_ref, dr2_ref, dh_ref, dg2_ref, db2_ref, dgm_ref, loss_ref, acc_ref):
        i, f = pl.program_id(0), pl.program_id(1)

        @pl.when((i == 0) & (f == 0))
        def _():
            for ref in (dg2_ref, db2_ref, dgm_ref, loss_ref):
                ref[...] = jnp.zeros_like(ref)

        @pl.when(f == 0)
        def _():
            acc_ref[...] = jnp.zeros_like(acc_ref)

        r = jnp.maximum(_dot(u2_ref[...], w1_ref[...]), 0.0)
        r_ref[...] = r.astype(BF16)
        acc_ref[...] += _bdot(r * r, w2_ref[...])

        @pl.when(f == nf - 1)
        def _():
            h = acc_ref[...]
            x1 = xhat_ref[...] * g1_ref[...] + b1_ref[...]
            xhat2, rstd2 = _ln_fwd(ALPHA * x1 + (1.0 + gm_ref[...]) * h)
            err = xhat2 * g2_ref[...] + b2_ref[...] - tgt_ref[...]
            loss_ref[...] += jnp.sum(0.5 * _lanemean(err * err), axis=0, keepdims=True)
            dy = err * (1.0 / D_MODEL)
            dg2_ref[...] += _rowsum(dy * xhat2)
            db2_ref[...] += _rowsum(dy)
            dr2 = _ln_bwd(dy * g2_ref[...], xhat2, rstd2)
            dr2_ref[...] = dr2
            dgm_ref[...] += _rowsum(dr2 * h)
            dh_ref[...] = ((1.0 + gm_ref[...]) * dr2).astype(BF16)

    row = pl.BlockSpec((tm, D_MODEL), lambda i, f: (i, 0))
    vec = _full((1, D_MODEL))
    return pl.pallas_call(
        body, name="mlp_fwd", grid=(T // tm, nf),
        in_specs=[row, pl.BlockSpec((None, D_MODEL, tf), lambda i, f: (f, 0, 0)),
                  pl.BlockSpec((None, tf, D_MODEL), lambda i, f: (f, 0, 0)),
                  row, vec, vec, vec, vec, vec, row],
        out_specs=[pl.BlockSpec((tm, tf), lambda i, f: (i, f)), row, row, vec, vec, vec, _full((1, 128))],
        out_shape=[jax.ShapeDtypeStruct((T, nf * tf), BF16), jax.ShapeDtypeStruct((T, D_MODEL), F32),
                   jax.ShapeDtypeStruct((T, D_MODEL), BF16), jax.ShapeDtypeStruct((1, D_MODEL), F32),
                   jax.ShapeDtypeStruct((1, D_MODEL), F32), jax.ShapeDtypeStruct((1, D_MODEL), F32),
                   jax.ShapeDtypeStruct((1, 128), F32)],
        scratch_shapes=[pltpu.VMEM((tm, D_MODEL), F32)],
        compiler_params=_params(),
    )(u2, w1, w2, xhat1, ln1_g, ln1_b, g_m, ln2_g, ln2_b, target)


def _mlp_bwd(dh, w1, w2, r, dr2, xhat1, rstd1, mix, ln1_g, ln1_b, sc_m, g_a):
    T = dh.shape[0]
    nf, tf = N_DEV, w1.shape[-1]
    tm = min(ROW_TILE, T)

    def body(dh_ref, w1_ref, w2_ref, r_ref, dr2_ref, xhat_ref, rstd_ref, mix_ref, g1_ref, b1_ref, sc_ref, ga_ref,
             dhpre_ref, dr1_ref, dmix_ref, dsc_ref, dsh_ref, dg1_ref, db1_ref, dga_ref, acc_ref):
        i, f = pl.program_id(0), pl.program_id(1)

        @pl.when((i == 0) & (f == 0))
        def _():
            for ref in (dsc_ref, dsh_ref, dg1_ref, db1_ref, dga_ref):
                ref[...] = jnp.zeros_like(ref)

        @pl.when(f == 0)
        def _():
            acc_ref[...] = jnp.zeros_like(acc_ref)

        dhpre = (_dot(dh_ref[...], w2_ref[...], NT) * (2.0 * r_ref[...].astype(F32))).astype(BF16)
        dhpre_ref[...] = dhpre
        acc_ref[...] += _dot(dhpre, w1_ref[...], NT)

        @pl.when(f == nf - 1)
        def _():
            du2 = acc_ref[...]
            xhat = xhat_ref[...]
            x1 = xhat * g1_ref[...] + b1_ref[...]
            dx1 = ALPHA * dr2_ref[...] + du2 * (1.0 + sc_ref[...])
            dsc_ref[...] += _rowsum(du2 * x1)
            dsh_ref[...] += _rowsum(du2)
            dg1_ref[...] += _rowsum(dx1 * xhat)
            db1_ref[...] += _rowsum(dx1)
            dr1 = _ln_bwd(dx1 * g1_ref[...], xhat, rstd_ref[:, 0:1])
            dr1_ref[...] = dr1
            dga_ref[...] += _rowsum(dr1 * mix_ref[...])
            dmix_ref[...] = ((1.0 + ga_ref[...]) * dr1).astype(BF16)

    row = pl.BlockSpec((tm, D_MODEL), lambda i, f: (i, 0))
    vec = _full((1, D_MODEL))
    return pl.pallas_call(
        body, name="mlp_bwd", grid=(T // tm, nf),
        in_specs=[row, pl.BlockSpec((None, D_MODEL, tf), lambda i, f: (f, 0, 0)),
                  pl.BlockSpec((None, tf, D_MODEL), lambda i, f: (f, 0, 0)),
                  pl.BlockSpec((tm, tf), lambda i, f: (i, f)), row, row, pl.BlockSpec((tm, 128), lambda i, f: (i, 0)),
                  row, vec, vec, vec, vec],
        out_specs=[pl.BlockSpec((tm, tf), lambda i, f: (i, f)), row, row, vec, vec, vec, vec, vec],
        out_shape=[jax.ShapeDtypeStruct((T, nf * tf), BF16), jax.ShapeDtypeStruct((T, D_MODEL), F32),
                   jax.ShapeDtypeStruct((T, D_MODEL), BF16)] + [jax.ShapeDtypeStruct((1, D_MODEL), F32)] * 5,
        scratch_shapes=[pltpu.VMEM((tm, D_MODEL), F32)],
        compiler_params=_params(),
    )(dh, w1, w2, r, dr2, xhat1, rstd1, mix, ln1_g, ln1_b, sc_m, g_a)


def _input_bwd(dz_h, dz_m, w_in_ext, x, dr1, sc_a, exchange=None):
    T = x.shape[0]
    tm = min(ROW_TILE_SMALL, T)

    def body(dzh_ref, dzm_ref, w_ref, x_ref, dr1_ref, sc_ref, gx_ref, dsc_ref, dsh_ref):
        @pl.when(pl.program_id(0) == 0)
        def _():
            dsc_ref[...] = jnp.zeros_like(dsc_ref)
            dsh_ref[...] = jnp.zeros_like(dsh_ref)

        du = _bdot(dzh_ref[...], w_ref[:, 0:2048], NT) + _bdot(dzm_ref[...], w_ref[:, 2048:3072], NT)
        gx_ref[...] = ALPHA * dr1_ref[...] + du * (1.0 + sc_ref[...])
        dsc_ref[...] += _rowsum(du * x_ref[...])
        dsh_ref[...] += _rowsum(du)

    row = pl.BlockSpec((tm, D_MODEL), lambda i: (i, 0))
    vec = _full((1, D_MODEL))
    return _call(
        body, "input_bwd", (dz_h, dz_m, w_in_ext, x, dr1, sc_a), grid=(T // tm,),
        in_specs=[pl.BlockSpec((tm, 2048), lambda i: (i, 0)), row, _full(w_in_ext.shape), row, row, vec],
        out_specs=[row, vec, vec],
        out_shape=[jax.ShapeDtypeStruct((T, D_MODEL), F32), jax.ShapeDtypeStruct((1, D_MODEL), F32),
                   jax.ShapeDtypeStruct((1, D_MODEL), F32)], exchange=exchange)


def _adam_math(w, g, m, v):
    m = ADAM_B1 * m + (1.0 - ADAM_B1) * g
    v = ADAM_B2 * v + (1.0 - ADAM_B2) * (g * g)
    m_hat = m / (1.0 - ADAM_B1 ** ADAM_STEP)
    v_hat = v / (1.0 - ADAM_B2 ** ADAM_STEP)
    return -ADAM_LR * (m_hat / (jnp.sqrt(v_hat) + ADAM_EPS) + ADAM_WD * w), m, v


def _adam(g_slabs, w, m, v, name, g_fn=None, g_extra=()):
    R, C = w.shape
    tr = R if R <= 256 else 256
    assert R % tr == 0
    ns = 0 if g_slabs is None else g_slabs.shape[0]
    ne = len(g_extra)

    def body(*refs):
        e_refs = refs[:ne]
        refs = refs[ne:]
        if ns:
            gs_ref, refs = refs[0], refs[1:]
        w_ref, m_ref, v_ref, g_ref, d_ref, nm_ref, nv_ref = refs
        if g_fn is not None:
            g = g_fn(*e_refs)
        else:
            g = gs_ref[0].astype(F32)
            for s in range(1, ns):
                g = g + gs_ref[s].astype(F32)
        d, nm, nv = _adam_math(w_ref[...], g, m_ref[...], v_ref[...])
        g_ref[...] = g
        d_ref[...] = d
        nm_ref[...] = nm
        nv_ref[...] = nv

    blk = pl.BlockSpec((tr, C), lambda i: (i, 0))
    in_specs = [pl.BlockSpec((tr, e.shape[1]), lambda i: (i, 0)) if e.shape[0] == R else _full(e.shape) for e in g_extra]
    args = list(g_extra)
    if ns:
        in_specs.append(pl.BlockSpec((ns, tr, C), lambda i: (0, i, 0)))
        args.append(g_slabs)
    return pl.pallas_call(
        body, name=name, grid=(R // tr,), in_specs=in_specs + [blk] * 3, out_specs=[blk] * 4,
        out_shape=[jax.ShapeDtypeStruct((R, C), F32)] * 4, compiler_params=_params(),
    )(*args, w, m, v)


def _adam_small(small_all, params):
    n = len(params)

    def body(*refs):
        s_ref, refs = refs[0], refs[1:]
        wmv, loss_ref, outs = refs[:3 * n], refs[3 * n], refs[3 * n + 1:]
        tot = s_ref[0]
        for i in range(1, N_DEV):
            tot = tot + s_ref[i]
        loss_ref[...] = tot[:, SMALL_W - 128:]
        for j, (w, _, _, off) in enumerate(params):
            w_ref, m_ref, v_ref = wmv[3 * j:3 * j + 3]
            g_ref, d_ref, nm_ref, nv_ref = outs[4 * j:4 * j + 4]
            if w.shape[0] == 2:
                lb = _lower_bound(w_ref)
                g0 = tot[:, off:off + w.shape[1]] * lb * (1.0 - lb)
                rows = [(slice(0, 1), g0), (slice(1, 2), -g0)]
            else:
                rows = [(slice(0, 1), tot[:, off:off + w.shape[1]])]
            for rs, g in rows:
                d, nm, nv = _adam_math(w_ref[rs, :], g, m_ref[rs, :], v_ref[rs, :])
                g_ref[rs, :], d_ref[rs, :], nm_ref[rs, :], nv_ref[rs, :] = g, d, nm, nv

    out_shape = [jax.ShapeDtypeStruct((1, 128), F32)]
    for w, _, _, _ in params:
        out_shape += [jax.ShapeDtypeStruct(w.shape, F32)] * 4
    res = pl.pallas_call(body, name="adam_small", out_shape=out_shape, compiler_params=_params())(
        small_all, *[a for w, m, v, _ in params for a in (w, m, v)])
    return res[0], [tuple(res[1 + 4 * j:5 + 4 * j]) for j in range(n)]


def _rot_half_cols(w):
    return jnp.concatenate([-w[..., 32:], w[..., :32]], axis=-1)


def _unrot_half_cols(dw_rot):
    return jnp.concatenate([dw_rot[..., 32:], -dw_rot[..., :32]], axis=-1)


def _cols_from_slabs(g):
    s, r, c = g.shape
    return jnp.transpose(g, (1, 0, 2)).reshape(r, s * c)


def _slabs_from_cols(w):
    r, c = w.shape
    return jnp.transpose(w.reshape(r, N_DEV, c // N_DEV), (1, 0, 2))


def _ext_in(w_in):
    k_in = w_in.shape[0]
    z64, z128 = jnp.zeros((k_in, 64), BF16), jnp.zeros((k_in, 128), BF16)
    wk = w_in[:, 2560:2624]
    return jnp.concatenate([w_in[:, :2560], z128, wk, z64, z128, _rot_half_cols(wk), z64], axis=1)


def _ext_q(w_q_up):
    r = w_q_up.shape[0]
    z64, z128 = jnp.zeros((r, 64), BF16), jnp.zeros((r, 128), BF16)
    wq = w_q_up.reshape(r, HEADS, HEAD_DIM + ROPE_DIM)
    main = [jnp.concatenate([wq[:, h, :HEAD_DIM], wq[:, h, HEAD_DIM:], z64], axis=1) for h in range(HEADS)]
    rot = [jnp.concatenate([z128, _rot_half_cols(wq[:, h, HEAD_DIM:]), z64], axis=1) for h in range(HEADS)]
    return jnp.concatenate(main + rot, axis=1)


def _ext_kv(w_kv_up):
    r = w_kv_up.shape[0]
    z128 = jnp.zeros((r, 128), BF16)
    wkv = w_kv_up.reshape(r, HEADS, 2 * HEAD_DIM)
    kpad = [jnp.concatenate([wkv[:, h, :HEAD_DIM], z128], axis=1) for h in range(HEADS)]
    vals = [wkv[:, h, HEAD_DIM:] for h in range(HEADS)]
    return jnp.concatenate(kpad + vals, axis=1)


def _grad_in_from_ext(dw_in_h, dw_in_m):
    dwk = dw_in_m[:, 512 + 128:512 + 192] + _unrot_half_cols(dw_in_m[:, 768 + 128:768 + 192])
    return jnp.concatenate([dw_in_h, dw_in_m[:, :512], dwk], axis=1)


def _grads_qkv_from_ext(dwq_ext, dwkv_ext):
    qcols = []
    for h in range(HEADS):
        main, rot = dwq_ext[:, 256 * h:256 * h + 256], dwq_ext[:, 1024 + 256 * h:1280 + 256 * h]
        qcols += [main[:, :128], main[:, 128:192] + _unrot_half_cols(rot[:, 128:192])]
    kvcols = []
    for h in range(HEADS):
        kvcols += [dwkv_ext[:, 256 * h:256 * h + 128], dwkv_ext[:, 1024 + 128 * h:1152 + 128 * h]]
    return jnp.concatenate(qcols, axis=1), jnp.concatenate(kvcols, axis=1)


SMALL_W = 6144 + 512 + 512 + 256 + 256 + 4 * 1024 + 128


def kernel(x, c, positions, w_ada, b_ada, w_in, hg_lower_bounds, hg_norm_w, mla_q_norm_w, w_q_up, mla_kv_norm_w, w_kv_up, w_out, ln1_g, ln1_b, w_mlp_in, w_mlp_out, ln2_g, ln2_b, loss_target, m_w_ada, m_b_ada, m_w_in, m_hg_lower_bounds, m_hg_norm_w, m_mla_q_norm_w, m_w_q_up, m_mla_kv_norm_w, m_w_kv_up, m_w_out, m_ln1_g, m_ln1_b, m_w_mlp_in, m_w_mlp_out, m_ln2_g, m_ln2_b, v_w_ada, v_b_ada, v_w_in, v_hg_lower_bounds, v_hg_norm_w, v_mla_q_norm_w, v_w_q_up, v_mla_kv_norm_w, v_w_kv_up, v_w_out, v_ln1_g, v_ln1_b, v_w_mlp_in, v_w_mlp_out, v_ln2_g, v_ln2_b):
    T = x.shape[1]
    me = 4 * lax.axis_index("x") + 2 * lax.axis_index("y") + lax.axis_index("c")
    xs, tgt = x[0], loss_target[0]
    big = dict(w_in=w_in[0], w_q_up=w_q_up[0], w_kv_up=w_kv_up[0], w_out=w_out[0], w_mlp_in=w_mlp_in[0],
               w_mlp_out=w_mlp_out[0])
    names = list(big)

    bf = {n: big[n].astype(BF16) for n in names}
    g_in, g_c = _exchange([bf["w_in"], c], scatter=False, name="gather_w_in")
    c_all = g_c.reshape(N_DEV, D_MODEL)

    ada_cols = w_ada.shape[2]
    mod_part, cond = _mod_part(c_all, w_ada[0], lax.dynamic_slice(b_ada, (0, me * ada_cols), (1, ada_cols)))
    (mod_all,) = _exchange([mod_part], scatter=False, name="gather_mod")
    mod_row = lax.dynamic_slice(mod_all, (0, me, 0), (N_DEV, 1, ada_cols)).reshape(1, N_DEV * ada_cols)
    sh_a, sc_a, g_a, sh_m, sc_m, g_m = [mod_row[:, D_MODEL * i:D_MODEL * (i + 1)] for i in range(6)]

    w_in_ext = _ext_in(_cols_from_slabs(g_in))
    z, (g_q, g_kv, g_out) = _matmul(xs, w_in_ext, "NN", "in_proj", a_fn=_modulate, extras=(sc_a, sh_a),
                                    exchange=_Exchange([bf["w_q_up"], bf["w_kv_up"], bf["w_out"]], False))
    wq_ext, wkv_ext = _ext_q(_cols_from_slabs(g_q)), _ext_kv(_cols_from_slabs(g_kv))
    w_out_full = g_out.reshape(D_MODEL, D_MODEL)
    inv_freq = 1.0 / (ROPE_THETA ** (jnp.arange(0, ROPE_DIM, 2, dtype=F32) / ROPE_DIM))
    zeros = lambda n: jnp.zeros((n,), F32)
    invf = jnp.concatenate([zeros(128), inv_freq, inv_freq, zeros(64)]).reshape(1, QK_PAD)
    m_one = jnp.concatenate([jnp.ones((128,), F32), zeros(128)]).reshape(1, QK_PAD)
    m_rot = jnp.concatenate([zeros(128), jnp.ones((64,), F32), zeros(64)]).reshape(1, QK_PAD)
    q, k, v, c1, s1, cqn, ckvn = _mla_pre(z, positions.reshape(T, 1), invf, m_one, m_rot, wq_ext, wkv_ext,
                                          mla_q_norm_w, mla_kv_norm_w)
    (o_raw, o_gated, s_prev), (w1,) = _hgrn_fwd(z, hg_lower_bounds, hg_norm_w,
                                                exchange=_Exchange([bf["w_mlp_in"]], False))
    (o_mla, lse), (w2,) = _attn_fwd(q, k, v, exchange=_Exchange([bf["w_mlp_out"]], False))
    mixcat = jnp.concatenate([o_gated, o_mla.astype(BF16)], axis=1)
    mix, xhat1, rstd1, u2 = _mix_ln1(mixcat, w_out_full, xs, g_a, ln1_g, ln1_b, sc_m, sh_m)[0]
    r, dr2, dh, dln2_g, dln2_b, dg_m, loss_part = _mlp_fwd(u2, w1, w2, xhat1, ln1_g, ln1_b, g_m, ln2_g, ln2_b, tgt)

    dhpre, dr1, dmix, dsc_m, dsh_m, dln1_g, dln1_b, dg_a = _mlp_bwd(dh, w1, w2, r, dr2, xhat1, rstd1, mix, ln1_g,
                                                                    ln1_b, sc_m, g_a)
    received = {}
    dw2 = _matmul(r, dh, "TN", "wgrad_mlp_out", out_dtype=BF16, a_fn=_square, tm=1024)
    dw1 = _matmul(u2, dhpre, "TN", "wgrad_mlp_in", out_dtype=BF16, tm=1024, out_slabs=N_DEV)
    dmixcat = _matmul(dmix, w_out_full, "NT", "dgrad_out")
    dw_out = _matmul(mixcat, dmix, "TN", "wgrad_out", out_dtype=BF16, tm=1024)
    (dz_h, dlb, dnw), (received["w_out"],) = _hgrn_bwd(
        dmixcat, z, o_raw, s_prev, hg_lower_bounds, hg_norm_w,
        exchange=_Exchange([dw_out.reshape(N_DEV, D_MODEL // N_DEV, D_MODEL)], True))
    (dq, dk, dv), (received["w_mlp_in"], received["w_mlp_out"]) = _attn_bwd(
        q, k, v, dmixcat, o_mla, lse,
        exchange=_Exchange([dw1, dw2.reshape(N_DEV, dw2.shape[0] // N_DEV, D_MODEL)], True))
    dz_m, dq_ext, dkv_ext, dqnw, dkvnw = _mla_bwd(dq, dk, dv, z, c1, s1, wq_ext, wkv_ext, mla_q_norm_w,
                                                   mla_kv_norm_w)
    dwq_ext = _matmul(cqn, dq_ext, "TN", "wgrad_q_up", tn=2048)
    dwkv_ext = _matmul(ckvn, dkv_ext, "TN", "wgrad_kv_up", tn=1536)
    dwq, dwkv = _grads_qkv_from_ext(dwq_ext, dwkv_ext)
    dw_in_h, (received["w_q_up"], received["w_kv_up"]) = _matmul(
        xs, dz_h, "TN", "wgrad_in_h", a_fn=_modulate, extras=(sc_a, sh_a), tm=1024,
        exchange=_Exchange([_slabs_from_cols(g).astype(BF16) for g in (dwq, dwkv)], True))
    dw_in_m = _matmul(xs, dz_m, "TN", "wgrad_in_m", a_fn=_modulate, extras=(sc_a, sh_a), tm=1024)
    dw_in = _grad_in_from_ext(dw_in_h, dw_in_m)
    (grad_x, dsc_a, dsh_a), (received["w_in"],) = _input_bwd(
        dz_h, dz_m, w_in_ext, xs, dr1, sc_a, exchange=_Exchange([_slabs_from_cols(dw_in).astype(BF16)], True))

    small = jnp.concatenate([dsh_a, dsc_a, dg_a, dsh_m, dsc_m, dg_m, dlb, dnw, dqnw, dkvnw, dln1_g, dln1_b, dln2_g,
                             dln2_b, loss_part], axis=1)
    (small_all,) = _exchange([small], scatter=False, name="gather_small")

    moments = dict(w_in=(m_w_in, v_w_in), w_q_up=(m_w_q_up, v_w_q_up), w_kv_up=(m_w_kv_up, v_w_kv_up),
                   w_out=(m_w_out, v_w_out), w_mlp_in=(m_w_mlp_in, v_w_mlp_in), w_mlp_out=(m_w_mlp_out, v_w_mlp_out))
    res = {}
    for n in names:
        res[n] = _adam(received[n], big[n], moments[n][0][0], moments[n][1][0], name="adam_" + n)
    dmod_cols = lax.dynamic_slice(small_all.reshape(N_DEV, SMALL_W), (0, me * ada_cols), (N_DEV, ada_cols))
    cond_t = cond.T

    def ada_grad(ct_ref, dm_ref):
        g = ct_ref[:, 0:1] * dm_ref[0:1, :]
        for b in range(1, N_DEV):
            g = g + ct_ref[:, b:b + 1] * dm_ref[b:b + 1, :]
        return g

    res["w_ada"] = _adam(None, w_ada[0], m_w_ada[0], v_w_ada[0], name="adam_w_ada", g_fn=ada_grad,
                         g_extra=(cond_t, dmod_cols))

    small_params = [("b_ada", b_ada, m_b_ada, v_b_ada, 0),
                    ("hg_lower_bounds", hg_lower_bounds, m_hg_lower_bounds, v_hg_lower_bounds, 6144),
                    ("hg_norm_w", hg_norm_w, m_hg_norm_w, v_hg_norm_w, 6656),
                    ("mla_q_norm_w", mla_q_norm_w, m_mla_q_norm_w, v_mla_q_norm_w, 7168),
                    ("mla_kv_norm_w", mla_kv_norm_w, m_mla_kv_norm_w, v_mla_kv_norm_w, 7424),
                    ("ln1_g", ln1_g, m_ln1_g, v_ln1_g, 7680), ("ln1_b", ln1_b, m_ln1_b, v_ln1_b, 8704),
                    ("ln2_g", ln2_g, m_ln2_g, v_ln2_g, 9728), ("ln2_b", ln2_b, m_ln2_b, v_ln2_b, 10752)]
    loss_row, small_res = _adam_small(small_all, [p[1:] for p in small_params])
    for p, r4 in zip(small_params, small_res):
        res[p[0]] = r4
    loss = loss_row[0, 0]

    order = ["w_ada", "b_ada", "w_in", "hg_lower_bounds", "hg_norm_w", "mla_q_norm_w", "w_q_up", "mla_kv_norm_w",
             "w_kv_up", "w_out", "ln1_g", "ln1_b", "w_mlp_in", "w_mlp_out", "ln2_g", "ln2_b"]
    shaped = {n: tuple(a.reshape((1,) + a.shape) if n in big or n == "w_ada" else a for a in res[n]) for n in order}
    outs = [loss, grad_x.reshape(1, T, D_MODEL)]
    for i in range(4):
        outs += [shaped[n][i] for n in order]
    return tuple(outs)
```

```python
import functools

import jax
import jax.numpy as jnp
import numpy as np
from jax import lax
from jax.experimental import pallas as pl
from jax.experimental.pallas import tpu as pltpu

F32, BF16 = jnp.float32, jnp.bfloat16
N_DEV = 8
D_MODEL = 1024
HEADS = 4
HEAD_DIM = 128
ROPE_DIM = 64
QK_PAD = 256
CHUNK = 64
ROPE_THETA = 10000.0
RMS_EPS = 1e-6
LN_EPS = 1e-5
ALPHA = 2.0 ** 0.25
ATT_SCALE = (HEAD_DIM + ROPE_DIM) ** -0.5
LN2 = float(np.log(2.0))
Q_PRESCALE = ATT_SCALE / LN2
ADAM_LR, ADAM_B1, ADAM_B2, ADAM_EPS, ADAM_WD, ADAM_STEP = 0.001, 0.9, 0.999, 1e-08, 0.01, 10
NEG_BIG = -1e30

ROW_TILE = 512
ROW_TILE_SMALL = 256
ATT_TILE = 512
HGRN_GROUP = 8
VMEM_LIMIT = 56 * 2 ** 20

NN = (((1,), (0,)), ((), ()))
NT = (((1,), (1,)), ((), ()))
TN = (((0,), (0,)), ((), ()))


def _dot(a, b, dims=NN):
    return lax.dot_general(a, b, dims, preferred_element_type=F32)


def _bdot(a, b, dims=NN):
    return lax.dot_general(a.astype(BF16), b.astype(BF16), dims, preferred_element_type=F32)


def _hdot(a, b, dims=NN):
    return lax.dot_general(a, b, dims, precision=lax.Precision.HIGHEST, preferred_element_type=F32)


def _params():
    return pltpu.CompilerParams(vmem_limit_bytes=VMEM_LIMIT)


def _sigmoid(x):
    return 1.0 / (1.0 + jnp.exp(-x))


def _rowsum(x):
    return jnp.sum(x, axis=0, keepdims=True)


def _lanemean(x):
    return jnp.mean(x, axis=-1, keepdims=True)


def _full(shape):
    nd = len(shape)
    return pl.BlockSpec(shape, lambda *_: (0,) * nd)


class _Exchange:
    def __init__(self, arrs, scatter):
        self.arrs, self.scatter, self.n = list(arrs), scatter, len(arrs)
        self.out_shape = [jax.ShapeDtypeStruct((N_DEV,) + (a.shape[1:] if scatter else a.shape), a.dtype)
                          for a in self.arrs]
        n = self.n
        self.scratch = [pltpu.SemaphoreType.DMA((n, N_DEV - 1)), pltpu.SemaphoreType.DMA((n, N_DEV - 1)),
                        pltpu.SemaphoreType.DMA((n,))]

    def _copies(self, ins, outs, sems):
        send_sems, recv_sems, loc_sems = sems
        x, y, c = lax.axis_index("x"), lax.axis_index("y"), lax.axis_index("c")
        me = 4 * x + 2 * y + c
        copies = []
        for k in range(self.n):
            src_of = (lambda i, k=k: ins[k].at[i]) if self.scatter else (lambda i, k=k: ins[k])
            copies.append((pltpu.make_async_copy(src_of(me), outs[k].at[me], loc_sems.at[k]), None))
            for p in range(1, N_DEV):
                px = (1 - x) if p & 4 else x
                py = (1 - y) if p & 2 else y
                pc = (1 - c) if p & 1 else c
                peer = 4 * px + 2 * py + pc
                both = dict(send_sem=send_sems.at[k, p - 1], recv_sem=recv_sems.at[k, p - 1],
                            device_id=(px, py, pc), device_id_type=pl.DeviceIdType.MESH)
                send = pltpu.make_async_remote_copy(src_ref=src_of(peer), dst_ref=outs[k].at[me], **both)
                recv = pltpu.make_async_remote_copy(src_ref=src_of(peer), dst_ref=outs[k].at[peer], **both)
                copies.append((send, recv))
        return copies

    def start(self, ins, outs, sems):
        for first, _ in self._copies(ins, outs, sems):
            first.start()

    def wait(self, ins, outs, sems):
        for first, recv in self._copies(ins, outs, sems):
            if recv is None:
                first.wait()
            else:
                recv.wait_recv()
                first.wait_send()


def _call(body, name, args, out_shape, grid=(), in_specs=(), out_specs=(), scratch_shapes=(), exchange=None):
    if exchange is None:
        return pl.pallas_call(body, name=name, grid=grid, in_specs=list(in_specs), out_specs=list(out_specs),
                              out_shape=list(out_shape), scratch_shapes=list(scratch_shapes),
                              compiler_params=_params())(*args), None
    exs = list(exchange) if isinstance(exchange, (list, tuple)) else [exchange]
    ni, no, ns, nx = len(args), len(out_shape), len(scratch_shapes), sum(e.n for e in exs)

    def wrapped(*refs):
        a, xi = refs[:ni], refs[ni:ni + nx]
        o, xo = refs[ni + nx:ni + nx + no], refs[ni + nx + no:ni + 2 * nx + no]
        s, xs = refs[ni + 2 * nx + no:ni + 2 * nx + no + ns], refs[ni + 2 * nx + no + ns:]
        parts, at = [], 0
        for j, e in enumerate(exs):
            parts.append((e, xi[at:at + e.n], xo[at:at + e.n], xs[3 * j:3 * j + 3]))
            at += e.n
        first = last = None
        for d, g in enumerate(grid):
            f, l = pl.program_id(d) == 0, pl.program_id(d) == g - 1
            first, last = (f, l) if first is None else (first & f, last & l)

        @pl.when(first)
        def _():
            for e, ins, outs, sems in parts:
                e.start(ins, outs, sems)

        body(*a, *o, *s)

        @pl.when(last)
        def _():
            for e, ins, outs, sems in parts:
                e.wait(ins, outs, sems)

    hbm = pl.BlockSpec(memory_space=pltpu.HBM)
    res = pl.pallas_call(
        wrapped, name=name, grid=grid, in_specs=list(in_specs) + [hbm] * nx, out_specs=list(out_specs) + [hbm] * nx,
        out_shape=list(out_shape) + [o_ for e in exs for o_ in e.out_shape],
        scratch_shapes=list(scratch_shapes) + [s_ for e in exs for s_ in e.scratch],
        compiler_params=_params())(*args, *[a_ for e in exs for a_ in e.arrs])
    return res[:no], res[no:]


def _gather_two_level(arrs, name):
    n = len(arrs)
    out_shape = [jax.ShapeDtypeStruct((N_DEV,) + a.shape, a.dtype) for a in arrs]

    def body(*refs):
        ins, outs = refs[:n], refs[n:2 * n]
        send_sems, recv_sems, loc_sems = refs[2 * n:]
        x, y, c = lax.axis_index("x"), lax.axis_index("y"), lax.axis_index("c")
        me, sibling = (x, y, c), (x, y, 1 - c)
        chips = [(1 - x, y), (x, 1 - y), (1 - x, 1 - y)]

        def copy(k, j, block, to, src=None):
            px, py, pc = block
            dst = outs[k].at[4 * px + 2 * py + pc]
            return pltpu.make_async_remote_copy(src_ref=dst if src is None else src, dst_ref=dst,
                                                send_sem=send_sems.at[k, j], recv_sem=recv_sems.at[k, j],
                                                device_id=to, device_id_type=pl.DeviceIdType.MESH)

        mine = [pltpu.make_async_copy(ins[k], outs[k].at[4 * x + 2 * y + c], loc_sems.at[k]) for k in range(n)]
        first = []
        for k in range(n):
            mine[k].start()
            first.append(copy(k, 0, me, sibling, src=ins[k]))
            first += [copy(k, 1 + j, me, (*chip, c), src=ins[k]) for j, chip in enumerate(chips)]
        for cp in first:
            cp.start()
        passed = []
        for j, chip in enumerate(chips):
            for k in range(n):
                copy(k, 1 + j, (*chip, c), me).wait_recv()
                passed.append(copy(k, 4 + j, (*chip, c), sibling))
                passed[-1].start()
        for k in range(n):
            copy(k, 0, sibling, me).wait_recv()
            for j, chip in enumerate(chips):
                copy(k, 4 + j, (*chip, 1 - c), me).wait_recv()
        for cp in first + passed:
            cp.wait_send()
        for cp in mine:
            cp.wait()

    vmem = pl.BlockSpec(memory_space=pltpu.VMEM)
    return pl.pallas_call(body, name=name, out_shape=out_shape, in_specs=[vmem] * n, out_specs=[vmem] * n,
                          scratch_shapes=[pltpu.SemaphoreType.DMA((n, 7)), pltpu.SemaphoreType.DMA((n, 7)),
                                          pltpu.SemaphoreType.DMA((n,))], compiler_params=_params())(*arrs)


def _exchange(arrs, scatter, name):
    ex = _Exchange(arrs, scatter)

    def body(*refs):
        ins, outs, sems = refs[:ex.n], refs[ex.n:2 * ex.n], refs[2 * ex.n:]
        ex.start(ins, outs, sems)
        ex.wait(ins, outs, sems)

    hbm = pl.BlockSpec(memory_space=pltpu.HBM)
    return pl.pallas_call(body, name=name, out_shape=ex.out_shape, in_specs=[hbm] * ex.n, out_specs=[hbm] * ex.n,
                          scratch_shapes=ex.scratch)(*ex.arrs)


def _matmul(a, b, mode, name, out_dtype=F32, tm=512, tn=1024, tk=1024, a_fn=None, extras=(), out_slabs=None,
            exchange=None):
    if mode == "NN":
        (M, K), N = a.shape, b.shape[1]
    elif mode == "NT":
        (M, K), N = a.shape, b.shape[0]
    else:
        (K, M), N = a.shape, b.shape[1]
    if out_slabs:
        tn = N // out_slabs
    tm, tn, tk = min(tm, M), min(tn, N), min(tk, K)
    assert M % tm == 0 and N % tn == 0 and K % tk == 0, (name, M, N, K)
    nk = K // tk
    dims = {"NN": NN, "NT": NT, "TN": TN}[mode]
    ne = len(extras)

    def body(a_ref, b_ref, *rest):
        e_refs, o_ref, acc_ref = rest[:ne], rest[ne], rest[ne + 1]
        k = pl.program_id(2)

        @pl.when(k == 0)
        def _():
            acc_ref[...] = jnp.zeros_like(acc_ref)

        at = a_ref[...]
        if a_fn is not None:
            at = a_fn(at.astype(F32), *[e[...] for e in e_refs])
        acc_ref[...] += _bdot(at, b_ref[...], dims)

        @pl.when(k == nk - 1)
        def _():
            o_ref[...] = acc_ref[...].astype(out_dtype)

    if mode == "TN":
        a_spec = pl.BlockSpec((tk, tm), lambda i, j, k: (k, i))
        e_spec = pl.BlockSpec((1, tm), lambda i, j, k: (0, i))
    else:
        a_spec = pl.BlockSpec((tm, tk), lambda i, j, k: (i, k))
        e_spec = pl.BlockSpec((1, tk), lambda i, j, k: (0, k))
    if mode == "NT":
        b_spec = pl.BlockSpec((tn, tk), lambda i, j, k: (j, k))
    else:
        b_spec = pl.BlockSpec((tk, tn), lambda i, j, k: (k, j))
    if out_slabs:
        o_shape = jax.ShapeDtypeStruct((out_slabs, M, tn), out_dtype)
        o_spec = pl.BlockSpec((None, tm, tn), lambda i, j, k: (j, i, 0))
    else:
        o_shape = jax.ShapeDtypeStruct((M, N), out_dtype)
        o_spec = pl.BlockSpec((tm, tn), lambda i, j, k: (i, j))
    (out,), got = _call(body, name, (a, b, *extras), [o_shape], grid=(M // tm, N // tn, nk),
                        in_specs=[a_spec, b_spec] + [e_spec] * ne, out_specs=[o_spec],
                        scratch_shapes=[pltpu.VMEM((tm, tn), F32)], exchange=exchange)
    return out if exchange is None else (out, got)


def _modulate(x, sc, sh):
    return x * (1.0 + sc) + sh


def _square(x):
    return x * x


def _mod_part(c_all, w_ada_s, b_s):
    def body(c_ref, w_ref, b_ref, mod_ref, cond_ref):
        cv = c_ref[...]
        cond = cv * _sigmoid(cv)
        cond_ref[...] = cond
        mod_ref[...] = _bdot(cond, w_ref[...]) + b_ref[...]

    return pl.pallas_call(
        body, name="mod_part",
        out_shape=[jax.ShapeDtypeStruct((N_DEV, w_ada_s.shape[1]), F32), jax.ShapeDtypeStruct(c_all.shape, F32)],
        compiler_params=_params(),
    )(c_all, w_ada_s, b_s)


def _rms_fwd(x, w):
    rs = lax.rsqrt(_lanemean(x * x) + RMS_EPS)
    return x * rs * w, rs


def _rms_bwd(x, rs, w, dy):
    xhat = x * rs
    dxh = dy * w
    return rs * (dxh - xhat * _lanemean(dxh * xhat)), dy * xhat


def _mla_pre(z, pos_col, invf, m_one, m_rot, wq_ext, wkv_ext, qnw, kvnw):
    T = z.shape[0]
    tm = min(ROW_TILE, T)

    def body(z_ref, pos_ref, invf_ref, mone_ref, mrot_ref, wq_ref, wkv_ref, qnw_ref, kvnw_ref,
             q_ref, k_ref, v_ref, c1_ref, s1_ref, cqn_ref, ckvn_ref):
        ang = pos_ref[...].astype(F32) * invf_ref[...]
        c1 = mone_ref[...] + mrot_ref[...] * jnp.cos(ang)
        s1 = mrot_ref[...] * jnp.sin(ang)
        c1_ref[...] = c1
        s1_ref[...] = s1
        cqn, _ = _rms_fwd(z_ref[:, 0:256], qnw_ref[...])
        ckvn, _ = _rms_fwd(z_ref[:, 256:512], kvnw_ref[...])
        cqn_ref[...] = cqn.astype(BF16)
        ckvn_ref[...] = ckvn.astype(BF16)
        qe = _bdot(cqn, wq_ref[...])
        kve = _bdot(ckvn, wkv_ref[...])
        k_rope = z_ref[:, 512:768] * c1 + z_ref[:, 768:1024] * s1
        for h in range(HEADS):
            q_ref[h] = ((qe[:, 256 * h:256 * h + 256] * c1 + qe[:, 1024 + 256 * h:1280 + 256 * h] * s1)
                        * Q_PRESCALE).astype(BF16)
            k_ref[h] = (kve[:, 256 * h:256 * h + 256] + k_rope).astype(BF16)
            v_ref[h] = kve[:, 1024 + 128 * h:1152 + 128 * h].astype(BF16)

    row = lambda i: (i, 0)
    head = lambda i: (0, i, 0)
    return pl.pallas_call(
        body, name="mla_pre", grid=(T // tm,),
        in_specs=[pl.BlockSpec((tm, 1024), lambda i: (i, 2)), pl.BlockSpec((tm, 1), row),
                  _full((1, 256)), _full((1, 256)), _full((1, 256)), _full(wq_ext.shape), _full(wkv_ext.shape),
                  _full((1, 256)), _full((1, 256))],
        out_specs=[pl.BlockSpec((HEADS, tm, QK_PAD), head), pl.BlockSpec((HEADS, tm, QK_PAD), head),
                   pl.BlockSpec((HEADS, tm, HEAD_DIM), head), pl.BlockSpec((tm, 256), row), pl.BlockSpec((tm, 256), row),
                   pl.BlockSpec((tm, 256), row), pl.BlockSpec((tm, 256), row)],
        out_shape=[jax.ShapeDtypeStruct((HEADS, T, QK_PAD), BF16), jax.ShapeDtypeStruct((HEADS, T, QK_PAD), BF16),
                   jax.ShapeDtypeStruct((HEADS, T, HEAD_DIM), BF16), jax.ShapeDtypeStruct((T, 256), F32),
                   jax.ShapeDtypeStruct((T, 256), F32), jax.ShapeDtypeStruct((T, 256), BF16),
                   jax.ShapeDtypeStruct((T, 256), BF16)],
        compiler_params=_params(),
    )(z, pos_col, invf, m_one, m_rot, wq_ext, wkv_ext, qnw, kvnw)


def _mla_bwd(dq, dk, dv, z, c1, s1, wq_ext, wkv_ext, qnw, kvnw):
    T = z.shape[0]
    tm = min(ROW_TILE_SMALL, T)

    def body(dq_ref, dk_ref, dv_ref, z_ref, c1_ref, s1_ref, wq_ref, wkv_ref, qnw_ref, kvnw_ref,
             dz_ref, dqe_ref, dkve_ref, dqnw_ref, dkvnw_ref):
        @pl.when(pl.program_id(0) == 0)
        def _():
            dqnw_ref[...] = jnp.zeros_like(dqnw_ref)
            dkvnw_ref[...] = jnp.zeros_like(dkvnw_ref)

        c1, s1 = c1_ref[...], s1_ref[...]
        dkpe = jnp.zeros((tm, QK_PAD), F32)
        for h in range(HEADS):
            dqh, dkh = dq_ref[h] * Q_PRESCALE, dk_ref[h]
            dqe_ref[:, 256 * h:256 * h + 256] = (dqh * c1).astype(BF16)
            dqe_ref[:, 1024 + 256 * h:1280 + 256 * h] = (dqh * s1).astype(BF16)
            dkve_ref[:, 256 * h:256 * h + 256] = dkh.astype(BF16)
            dkve_ref[:, 1024 + 128 * h:1152 + 128 * h] = dv_ref[h].astype(BF16)
            dkpe = dkpe + dkh
        dcqn = _dot(dqe_ref[...], wq_ref[...], NT)
        dckvn = _dot(dkve_ref[...], wkv_ref[...], NT)
        cq, ckv = z_ref[:, 0:256], z_ref[:, 256:512]
        _, rsq = _rms_fwd(cq, qnw_ref[...])
        _, rskv = _rms_fwd(ckv, kvnw_ref[...])
        dcq, wq_rows = _rms_bwd(cq, rsq, qnw_ref[...], dcqn)
        dckv, wkv_rows = _rms_bwd(ckv, rskv, kvnw_ref[...], dckvn)
        dqnw_ref[...] += _rowsum(wq_rows)
        dkvnw_ref[...] += _rowsum(wkv_rows)
        dz_ref[:, 0:256] = dcq
        dz_ref[:, 256:512] = dckv
        dz_ref[:, 512:768] = dkpe * c1
        dz_ref[:, 768:1024] = dkpe * s1

    row = lambda i: (i, 0)
    head = lambda i: (0, i, 0)
    return pl.pallas_call(
        body, name="mla_bwd", grid=(T // tm,),
        in_specs=[pl.BlockSpec((HEADS, tm, QK_PAD), head), pl.BlockSpec((HEADS, tm, QK_PAD), head),
                  pl.BlockSpec((HEADS, tm, HEAD_DIM), head), pl.BlockSpec((tm, 1024), lambda i: (i, 2)),
                  pl.BlockSpec((tm, 256), row), pl.BlockSpec((tm, 256), row), _full(wq_ext.shape), _full(wkv_ext.shape),
                  _full((1, 256)), _full((1, 256))],
        out_specs=[pl.BlockSpec((tm, 1024), row), pl.BlockSpec((tm, 2048), row), pl.BlockSpec((tm, 1536), row),
                   _full((1, 256)), _full((1, 256))],
        out_shape=[jax.ShapeDtypeStruct((T, 1024), F32), jax.ShapeDtypeStruct((T, 2048), BF16),
                   jax.ShapeDtypeStruct((T, 1536), BF16), jax.ShapeDtypeStruct((1, 256), F32),
                   jax.ShapeDtypeStruct((1, 256), F32)],
        compiler_params=_params(),
    )(dq, dk, dv, z, c1, s1, wq_ext, wkv_ext, qnw, kvnw)


_HEAD_LANES = [slice(HEAD_DIM * h, HEAD_DIM * (h + 1)) for h in range(HEADS)]


def _lower_bound(lbraw_ref):
    a0, a1 = lbraw_ref[0:1, :], lbraw_ref[1:2, :]
    mx = jnp.maximum(a0, a1)
    e0, e1 = jnp.exp(a0 - mx), jnp.exp(a1 - mx)
    return e0 / (e0 + e1)


def _tri(lower):
    r = lax.broadcasted_iota(jnp.int32, (CHUNK, CHUNK), 0)
    c = lax.broadcasted_iota(jnp.int32, (CHUNK, CHUNK), 1)
    return (r >= c) if lower else (r <= c)


def _hgrn_gates(q, f, lb, tri_lo):
    sg = _sigmoid(f)
    forget = lb + (1.0 - lb) * sg
    k = 1.0 - forget
    b = _hdot(tri_lo.astype(F32), jnp.log(forget))
    b_ref, b_last = b[CHUNK // 2 - 1:CHUNK // 2, :], b[CHUNK - 1:CHUNK, :]
    e1, e2, e3, e4 = jnp.exp(b - b_ref), jnp.exp(b_ref - b), jnp.exp(b_last - b), jnp.exp(b)
    return dict(sg=sg, forget=forget, k=k, e1=e1, e2=e2, e3=e3, e4=e4, qa=q * e1, ka=k * e2, kl=k * e3, qb=q * e4,
                decay=jnp.exp(b_last))


def _hgrn_fwd(z, lbraw, nw, exchange=None):
    T = z.shape[0]
    G = min(HGRN_GROUP, T // CHUNK)
    rows = G * CHUNK
    n_chunks = T // CHUNK

    def body(q_ref, f_ref, i_ref, g_ref, lbraw_ref, nw_ref, oraw_ref, og_ref, sp_ref, st_ref):
        @pl.when(pl.program_id(0) == 0)
        def _():
            st_ref[...] = jnp.zeros_like(st_ref)

        lb_all = _lower_bound(lbraw_ref)
        tri_lo = _tri(True)

        def chunk(cc, carry):
            rs = pl.ds(pl.multiple_of(cc * CHUNK, CHUNK), CHUNK)
            t = _hgrn_gates(q_ref[rs, :], f_ref[rs, :], lb_all, tri_lo)
            v, gate = i_ref[rs, :], g_ref[rs, :]
            st = [st_ref[h] for h in range(HEADS)]
            a = [jnp.where(tri_lo, _bdot(t["qa"][:, s], t["ka"][:, s], NT), 0.0) for s in _HEAD_LANES]
            kv = [_bdot(v[:, s], t["kl"][:, s], TN) for s in _HEAD_LANES]
            o = [_bdot(a[h], v[:, s]) + _bdot(t["qb"][:, s], st[h], NT) for h, s in enumerate(_HEAD_LANES)]
            for h, s in enumerate(_HEAD_LANES):
                sp_ref[cc, h] = st[h]
                st_ref[h] = st[h] * t["decay"][:, s] + kv[h]
            oraw_ref[rs, :] = jnp.concatenate(o, axis=1)
            on = jnp.concatenate([_rms_fwd(o[h], nw_ref[:, s])[0] for h, s in enumerate(_HEAD_LANES)], axis=1)
            og_ref[rs, :] = (on * (gate * _sigmoid(gate))).astype(BF16)
            return carry

        lax.fori_loop(0, G, chunk, 0, unroll=2)

    col = lambda j: pl.BlockSpec((rows, 512), lambda r, j=j: (r, j))
    return _call(
        body, "hgrn_fwd", (z, z, z, z, lbraw, nw), grid=(T // rows,),
        in_specs=[col(0), col(1), col(2), col(3), _full((2, 512)), _full((1, 512))],
        out_specs=[col(0), col(0), pl.BlockSpec((G, HEADS, HEAD_DIM, HEAD_DIM), lambda r: (r, 0, 0, 0))],
        out_shape=[jax.ShapeDtypeStruct((T, 512), F32), jax.ShapeDtypeStruct((T, 512), BF16),
                   jax.ShapeDtypeStruct((n_chunks, HEADS, HEAD_DIM, HEAD_DIM), F32)],
        scratch_shapes=[pltpu.VMEM((HEADS, HEAD_DIM, HEAD_DIM), F32)], exchange=exchange)


def _hgrn_bwd(dmixcat, z, oraw, sprev, lbraw, nw, exchange=None):
    T = z.shape[0]
    G = min(HGRN_GROUP, T // CHUNK)
    rows = G * CHUNK
    ng = T // rows

    def body(dog_ref, q_ref, f_ref, i_ref, g_ref, oraw_ref, sp_ref, lbraw_ref, nw_ref,
             dz_ref, dlb_ref, dnw_ref, dst_ref):
        @pl.when(pl.program_id(0) == 0)
        def _():
            dst_ref[...] = jnp.zeros_like(dst_ref)
            dlb_ref[...] = jnp.zeros_like(dlb_ref)
            dnw_ref[...] = jnp.zeros_like(dnw_ref)

        lb_all = _lower_bound(lbraw_ref)
        tri_lo, tri_up = _tri(True), _tri(False)
        rowid = lax.broadcasted_iota(jnp.int32, (CHUNK, HEADS * HEAD_DIM), 0)

        def chunk(it, carry):
            cc = G - 1 - it
            rs = pl.ds(pl.multiple_of(cc * CHUNK, CHUNK), CHUNK)
            heads = list(enumerate(_HEAD_LANES))
            cat = lambda parts: jnp.concatenate(parts, axis=1)
            per_head_mean = lambda x: cat([jnp.broadcast_to(_lanemean(x[:, s]), (CHUNK, HEAD_DIM)) for s in _HEAD_LANES])
            t = _hgrn_gates(q_ref[rs, :], f_ref[rs, :], lb_all, tri_lo)
            v, gate, o, dog, nw_all = i_ref[rs, :], g_ref[rs, :], oraw_ref[rs, :], dog_ref[rs, :], nw_ref[...]
            rs_o = lax.rsqrt(per_head_mean(o * o) + RMS_EPS)
            xhat = o * rs_o
            sgg = _sigmoid(gate)
            d_on = dog * (gate * sgg)
            dz_ref[rs, 1536:2048] = dog * (xhat * nw_all) * (sgg * (1.0 + gate * (1.0 - sgg)))
            dxh = d_on * nw_all
            do = rs_o * (dxh - xhat * per_head_mean(dxh * xhat))
            dnw_ref[...] += _rowsum(d_on * xhat)
            st = [sp_ref[cc, h] for h in range(HEADS)]
            dst = [dst_ref[h] for h in range(HEADS)]
            a = [jnp.where(tri_lo, _bdot(t["qa"][:, s], t["ka"][:, s], NT), 0.0) for s in _HEAD_LANES]
            da = [jnp.where(tri_lo, _bdot(do[:, s], v[:, s], NT), 0.0) for s in _HEAD_LANES]
            dqb = cat([_bdot(do[:, s], st[h]) for h, s in heads])
            dkl = cat([_bdot(v[:, s], dst[h]) for h, s in heads])
            dv_ = cat([_bdot(t["kl"][:, s], dst[h], NT) + _bdot(a[h], do[:, s], TN) for h, s in heads])
            dqa = cat([_bdot(da[h], t["ka"][:, s]) for h, s in heads])
            dka = cat([_bdot(da[h], t["qa"][:, s], TN) for h, s in heads])
            ddecay = cat([_rowsum(dst[h] * st[h]) for h in range(HEADS)])
            for h, s in heads:
                dst_ref[h] = dst[h] * t["decay"][:, s] + _bdot(do[:, s], t["qb"][:, s], TN)
            pa, pk, pb, pl_ = dqa * t["qa"], dka * t["ka"], dqb * t["qb"], dkl * t["kl"]
            db = pa - pk + pb - pl_
            db = db + jnp.where(rowid == CHUNK // 2 - 1, _rowsum(pk - pa), 0.0)
            db = db + jnp.where(rowid == CHUNK - 1, _rowsum(pl_) + ddecay * t["decay"], 0.0)
            dlogf = _hdot(tri_up.astype(F32), db)
            dforget = dlogf / t["forget"] - (dka * t["e2"] + dkl * t["e3"])
            sg = t["sg"]
            dz_ref[rs, 0:512] = dqa * t["e1"] + dqb * t["e4"]
            dz_ref[rs, 512:1024] = dforget * (1.0 - lb_all) * sg * (1.0 - sg)
            dz_ref[rs, 1024:1536] = dv_
            dlb_ref[...] += _rowsum(dforget * (1.0 - sg))
            return carry

        lax.fori_loop(0, G, chunk, 0, unroll=2)

    col = lambda j: pl.BlockSpec((rows, 512), lambda r, j=j: (ng - 1 - r, j))
    return _call(
        body, "hgrn_bwd", (dmixcat, z, z, z, z, oraw, sprev, lbraw, nw), grid=(ng,),
        in_specs=[col(0), col(0), col(1), col(2), col(3), col(0),
                  pl.BlockSpec((G, HEADS, HEAD_DIM, HEAD_DIM), lambda r: (ng - 1 - r, 0, 0, 0)),
                  _full((2, 512)), _full((1, 512))],
        out_specs=[pl.BlockSpec((rows, 2048), lambda r: (ng - 1 - r, 0)), _full((1, 512)), _full((1, 512))],
        out_shape=[jax.ShapeDtypeStruct((T, 2048), F32), jax.ShapeDtypeStruct((1, 512), F32),
                   jax.ShapeDtypeStruct((1, 512), F32)],
        scratch_shapes=[pltpu.VMEM((HEADS, HEAD_DIM, HEAD_DIM), F32)], exchange=exchange)


def _diag_mask(t):
    r = lax.broadcasted_iota(jnp.int32, (t, t), 0)
    c = lax.broadcasted_iota(jnp.int32, (t, t), 1)
    return r >= c


def _attn_fwd(q, k, v, exchange=None):
    _, T, _ = q.shape
    t = min(ATT_TILE, T)

    def body(q_ref, k_ref, v_ref, o_ref, lse_ref):
        i = pl.program_id(1)
        qb = q_ref[...]

        def step(j, carry, masked):
            m, l, acc = carry
            ks = pl.ds(pl.multiple_of(j * t, t), t)
            s = _dot(qb, k_ref[ks, :], NT)
            if masked:
                s = jnp.where(_diag_mask(t), s, NEG_BIG)
            mn = jnp.maximum(m, jnp.max(s, axis=-1, keepdims=True))
            p = jnp.exp2(s - mn)
            al = jnp.exp2(m - mn)
            return mn, al * l + jnp.sum(p, axis=-1, keepdims=True), al * acc + _dot(p.astype(BF16), v_ref[ks, :])

        init = (jnp.full((t, 1), NEG_BIG, F32), jnp.zeros((t, 1), F32), jnp.zeros((t, HEAD_DIM), F32))
        carry = lax.fori_loop(0, i, lambda j, c: step(j, c, False), init)
        m, l, acc = step(i, carry, True)
        o_ref[...] = acc / l
        lse_ref[...] = jnp.broadcast_to(m + jnp.log2(l), (t, HEAD_DIM))

    return _call(
        body, "attn_fwd", (q, k, v), grid=(HEADS, T // t),
        in_specs=[pl.BlockSpec((None, t, QK_PAD), lambda h, i: (h, i, 0)),
                  pl.BlockSpec((None, T, QK_PAD), lambda h, i: (h, 0, 0)),
                  pl.BlockSpec((None, T, HEAD_DIM), lambda h, i: (h, 0, 0))],
        out_specs=[pl.BlockSpec((t, HEAD_DIM), lambda h, i: (i, h)),
                   pl.BlockSpec((None, t, HEAD_DIM), lambda h, i: (h, i, 0))],
        out_shape=[jax.ShapeDtypeStruct((T, HEADS * HEAD_DIM), F32), jax.ShapeDtypeStruct((HEADS, T, HEAD_DIM), F32)],
        exchange=exchange)


def _attn_bwd(q, k, v, dmixcat, o, lse, exchange=None):
    _, T, _ = q.shape
    t = min(ATT_TILE, T)
    nq = T // t

    def body(q_ref, k_ref, v_ref, do_ref, o_ref, lse_ref, dq_ref, dk_ref, dv_ref, delta_ref):
        j = pl.program_id(1)

        @pl.when(j == 0)
        def _():
            dq_ref[...] = jnp.zeros_like(dq_ref)

            def fill(i, carry):
                rs = pl.ds(pl.multiple_of(i * t, t), t)
                delta_ref[rs, :] = jnp.broadcast_to(
                    jnp.sum(do_ref[rs, :] * o_ref[rs, :], axis=-1, keepdims=True), (t, HEAD_DIM))
                return carry

            lax.fori_loop(0, nq, fill, 0)

        kb, vb = k_ref[...], v_ref[...]

        def step(i, carry, masked):
            dk, dv = carry
            rs = pl.ds(pl.multiple_of(i * t, t), t)
            qb, dob = q_ref[rs, :], do_ref[rs, :].astype(BF16)
            p = jnp.exp2(_dot(qb, kb, NT) - lse_ref[rs, 0:1])
            if masked:
                p = jnp.where(_diag_mask(t), p, 0.0)
            dp = _dot(dob, vb, NT)
            ds = (p * (dp - delta_ref[rs, 0:1]) * LN2).astype(BF16)
            dq_ref[rs, :] += _dot(ds, kb)
            return dk + _dot(ds, qb, TN), dv + _dot(p.astype(BF16), dob, TN)

        carry = step(j, (jnp.zeros((t, QK_PAD), F32), jnp.zeros((t, HEAD_DIM), F32)), True)
        dk, dv = lax.fori_loop(j + 1, nq, lambda i, c: step(i, c, False), carry)
        dk_ref[...] = dk
        dv_ref[...] = dv

    return _call(
        body, "attn_bwd", (q, k, v, dmixcat, o, lse), grid=(HEADS, nq),
        in_specs=[pl.BlockSpec((None, T, QK_PAD), lambda h, j: (h, 0, 0)),
                  pl.BlockSpec((None, t, QK_PAD), lambda h, j: (h, j, 0)),
                  pl.BlockSpec((None, t, HEAD_DIM), lambda h, j: (h, j, 0)),
                  pl.BlockSpec((T, HEAD_DIM), lambda h, j: (0, HEADS + h)),
                  pl.BlockSpec((T, HEAD_DIM), lambda h, j: (0, h)),
                  pl.BlockSpec((None, T, HEAD_DIM), lambda h, j: (h, 0, 0))],
        out_specs=[pl.BlockSpec((None, T, QK_PAD), lambda h, j: (h, 0, 0)),
                   pl.BlockSpec((None, t, QK_PAD), lambda h, j: (h, j, 0)),
                   pl.BlockSpec((None, t, HEAD_DIM), lambda h, j: (h, j, 0))],
        out_shape=[jax.ShapeDtypeStruct((HEADS, T, QK_PAD), F32), jax.ShapeDtypeStruct((HEADS, T, QK_PAD), F32),
                   jax.ShapeDtypeStruct((HEADS, T, HEAD_DIM), F32)],
        scratch_shapes=[pltpu.VMEM((T, HEAD_DIM), F32)], exchange=exchange)


def _ln_fwd(r):
    mu = _lanemean(r)
    xc = r - mu
    rstd = lax.rsqrt(_lanemean(xc * xc) + LN_EPS)
    return xc * rstd, rstd


def _ln_bwd(dxh, xhat, rstd):
    return rstd * (dxh - _lanemean(dxh) - xhat * _lanemean(dxh * xhat))


def _mix_ln1(mixcat, w_out, x, g_a, ln1_g, ln1_b, sc_m, sh_m, exchange=None):
    T = x.shape[0]
    tm = min(ROW_TILE_SMALL, T)

    def body(mc_ref, w_ref, x_ref, ga_ref, g_ref, b_ref, sc_ref, sh_ref, mix_ref, xhat_ref, rstd_ref, u2_ref):
        mix = _dot(mc_ref[...], w_ref[...])
        mix_ref[...] = mix
        xhat, rstd = _ln_fwd(ALPHA * x_ref[...] + (1.0 + ga_ref[...]) * mix)
        xhat_ref[...] = xhat
        rstd_ref[...] = jnp.broadcast_to(rstd, (tm, 128))
        u2_ref[...] = _modulate(xhat * g_ref[...] + b_ref[...], sc_ref[...], sh_ref[...]).astype(BF16)

    row = pl.BlockSpec((tm, D_MODEL), lambda i: (i, 0))
    vec = _full((1, D_MODEL))
    return _call(
        body, "mix_ln1", (mixcat, w_out, x, g_a, ln1_g, ln1_b, sc_m, sh_m), grid=(T // tm,),
        in_specs=[row, _full(w_out.shape), row, vec, vec, vec, vec, vec],
        out_specs=[row, row, pl.BlockSpec((tm, 128), lambda i: (i, 0)), row],
        out_shape=[jax.ShapeDtypeStruct((T, D_MODEL), F32), jax.ShapeDtypeStruct((T, D_MODEL), F32),
                   jax.ShapeDtypeStruct((T, 128), F32), jax.ShapeDtypeStruct((T, D_MODEL), BF16)],
        exchange=exchange)


def _mlp_fwd(u2, w1, w2, xhat1, ln1_g, ln1_b, g_m, ln2_g, ln2_b, target):
    T = u2.shape[0]
    nf, tf = N_DEV, w1.shape[-1]
    tm = min(ROW_TILE, T)

    def body(u2_ref, w1_ref, w2_ref, xhat_ref, g1_ref, b1_ref, gm_ref, g2_ref, b2_ref, tgt_ref,
             r_ref, dr2_ref, dh_ref, dg2_ref, db2_ref, dgm_ref, loss_ref, acc_ref):
        i, f = pl.program_id(0), pl.program_id(1)

        @pl.when((i == 0) & (f == 0))
        def _():
            for ref in (dg2_ref, db2_ref, dgm_ref, loss_ref):
                ref[...] = jnp.zeros_like(ref)

        @pl.when(f == 0)
        def _():
            acc_ref[...] = jnp.zeros_like(acc_ref)

        r = jnp.maximum(_dot(u2_ref[...], w1_ref[...]), 0.0)
        r_ref[...] = r.astype(BF16)
        acc_ref[...] += _bdot(r * r, w2_ref[...])

        @pl.when(f == nf - 1)
        def _():
            h = acc_ref[...]
            x1 = xhat_ref[...] * g1_ref[...] + b1_ref[...]
            xhat2, rstd2 = _ln_fwd(ALPHA * x1 + (1.0 + gm_ref[...]) * h)
            err = xhat2 * g2_ref[...] + b2_ref[...] - tgt_ref[...]
            loss_ref[...] += jnp.sum(0.5 * _lanemean(err * err), axis=0, keepdims=True)
            dy = err * (1.0 / D_MODEL)
            dg2_ref[...] += _rowsum(dy * xhat2)
            db2_ref[...] += _rowsum(dy)
            dr2 = _ln_bwd(dy * g2_ref[...], xhat2, rstd2)
            dr2_ref[...] = dr2
            dgm_ref[...] += _rowsum(dr2 * h)
            dh_ref[...] = ((1.0 + gm_ref[...]) * dr2).astype(BF16)

    row = pl.BlockSpec((tm, D_MODEL), lambda i, f: (i, 0))
    vec = _full((1, D_MODEL))
    return pl.pallas_call(
        body, name="mlp_fwd", grid=(T // tm, nf),
        in_specs=[row, pl.BlockSpec((None, D_MODEL, tf), lambda i, f: (f, 0, 0)),
                  pl.BlockSpec((None, tf, D_MODEL), lambda i, f: (f, 0, 0)),
                  row, vec, vec, vec, vec, vec, row],
        out_specs=[pl.BlockSpec((tm, tf), lambda i, f: (i, f)), row, row, vec, vec, vec, _full((1, 128))],
        out_shape=[jax.ShapeDtypeStruct((T, nf * tf), BF16), jax.ShapeDtypeStruct((T, D_MODEL), F32),
                   jax.ShapeDtypeStruct((T, D_MODEL), BF16), jax.ShapeDtypeStruct((1, D_MODEL), F32),
                   jax.ShapeDtypeStruct((1, D_MODEL), F32), jax.ShapeDtypeStruct((1, D_MODEL), F32),
                   jax.ShapeDtypeStruct((1, 128), F32)],
        scratch_shapes=[pltpu.VMEM((tm, D_MODEL), F32)],
        compiler_params=_params(),
    )(u2, w1, w2, xhat1, ln1_g, ln1_b, g_m, ln2_g, ln2_b, target)


def _mlp_bwd(dh, w1, w2, r, dr2, xhat1, rstd1, mix, ln1_g, ln1_b, sc_m, g_a):
    T = dh.shape[0]
    nf, tf = N_DEV, w1.shape[-1]
    tm = min(ROW_TILE, T)

    def body(dh_ref, w1_ref, w2_ref, r_ref, dr2_ref, xhat_ref, rstd_ref, mix_ref, g1_ref, b1_ref, sc_ref, ga_ref,
             dhpre_ref, dr1_ref, dmix_ref, dsc_ref, dsh_ref, dg1_ref, db1_ref, dga_ref, acc_ref):
        i, f = pl.program_id(0), pl.program_id(1)

        @pl.when((i == 0) & (f == 0))
        def _():
            for ref in (dsc_ref, dsh_ref, dg1_ref, db1_ref, dga_ref):
                ref[...] = jnp.zeros_like(ref)

        @pl.when(f == 0)
        def _():
            acc_ref[...] = jnp.zeros_like(acc_ref)

        dhpre = (_dot(dh_ref[...], w2_ref[...], NT) * (2.0 * r_ref[...].astype(F32))).astype(BF16)
        dhpre_ref[...] = dhpre
        acc_ref[...] += _dot(dhpre, w1_ref[...], NT)

        @pl.when(f == nf - 1)
        def _():
            du2 = acc_ref[...]
            xhat = xhat_ref[...]
            x1 = xhat * g1_ref[...] + b1_ref[...]
            dx1 = ALPHA * dr2_ref[...] + du2 * (1.0 + sc_ref[...])
            dsc_ref[...] += _rowsum(du2 * x1)
            dsh_ref[...] += _rowsum(du2)
            dg1_ref[...] += _rowsum(dx1 * xhat)
            db1_ref[...] += _rowsum(dx1)
            dr1 = _ln_bwd(dx1 * g1_ref[...], xhat, rstd_ref[:, 0:1])
            dr1_ref[...] = dr1
            dga_ref[...] += _rowsum(dr1 * mix_ref[...])
            dmix_ref[...] = ((1.0 + ga_ref[...]) * dr1).astype(BF16)

    row = pl.BlockSpec((tm, D_MODEL), lambda i, f: (i, 0))
    vec = _full((1, D_MODEL))
    return pl.pallas_call(
        body, name="mlp_bwd", grid=(T // tm, nf),
        in_specs=[row, pl.BlockSpec((None, D_MODEL, tf), lambda i, f: (f, 0, 0)),
                  pl.BlockSpec((None, tf, D_MODEL), lambda i, f: (f, 0, 0)),
                  pl.BlockSpec((tm, tf), lambda i, f: (i, f)), row, row, pl.BlockSpec((tm, 128), lambda i, f: (i, 0)),
                  row, vec, vec, vec, vec],
        out_specs=[pl.BlockSpec((tm, tf), lambda i, f: (i, f)), row, row, vec, vec, vec, vec, vec],
        out_shape=[jax.ShapeDtypeStruct((T, nf * tf), BF16), jax.ShapeDtypeStruct((T, D_MODEL), F32),
                   jax.ShapeDtypeStruct((T, D_MODEL), BF16)] + [jax.ShapeDtypeStruct((1, D_MODEL), F32)] * 5,
        scratch_shapes=[pltpu.VMEM((tm, D_MODEL), F32)],
        compiler_params=_params(),
    )(dh, w1, w2, r, dr2, xhat1, rstd1, mix, ln1_g, ln1_b, sc_m, g_a)


def _input_bwd(dz_h, dz_m, w_in_ext, x, dr1, sc_a, exchange=None):
    T = x.shape[0]
    tm = min(ROW_TILE_SMALL, T)

    def body(dzh_ref, dzm_ref, w_ref, x_ref, dr1_ref, sc_ref, gx_ref, dsc_ref, dsh_ref):
        @pl.when(pl.program_id(0) == 0)
        def _():
            dsc_ref[...] = jnp.zeros_like(dsc_ref)
            dsh_ref[...] = jnp.zeros_like(dsh_ref)

        du = _bdot(dzh_ref[...], w_ref[:, 0:2048], NT) + _bdot(dzm_ref[...], w_ref[:, 2048:3072], NT)
        gx_ref[...] = ALPHA * dr1_ref[...] + du * (1.0 + sc_ref[...])
        dsc_ref[...] += _rowsum(du * x_ref[...])
        dsh_ref[...] += _rowsum(du)

    row = pl.BlockSpec((tm, D_MODEL), lambda i: (i, 0))
    vec = _full((1, D_MODEL))
    return _call(
        body, "input_bwd", (dz_h, dz_m, w_in_ext, x, dr1, sc_a), grid=(T // tm,),
        in_specs=[pl.BlockSpec((tm, 2048), lambda i: (i, 0)), row, _full(w_in_ext.shape), row, row, vec],
        out_specs=[row, vec, vec],
        out_shape=[jax.ShapeDtypeStruct((T, D_MODEL), F32), jax.ShapeDtypeStruct((1, D_MODEL), F32),
                   jax.ShapeDtypeStruct((1, D_MODEL), F32)], exchange=exchange)


def _adam_math(w, g, m, v):
    m = ADAM_B1 * m + (1.0 - ADAM_B1) * g
    v = ADAM_B2 * v + (1.0 - ADAM_B2) * (g * g)
    m_hat = m / (1.0 - ADAM_B1 ** ADAM_STEP)
    v_hat = v / (1.0 - ADAM_B2 ** ADAM_STEP)
    return -ADAM_LR * (m_hat / (jnp.sqrt(v_hat) + ADAM_EPS) + ADAM_WD * w), m, v


def _adam(g_slabs, w, m, v, name, g_fn=None, g_extra=()):
    R, C = w.shape
    tr = R if R <= 256 else 256
    assert R % tr == 0
    ns = 0 if g_slabs is None else g_slabs.shape[0]
    ne = len(g_extra)

    def body(*refs):
        e_refs = refs[:ne]
        refs = refs[ne:]
        if ns:
            gs_ref, refs = refs[0], refs[1:]
        w_ref, m_ref, v_ref, g_ref, d_ref, nm_ref, nv_ref = refs
        if g_fn is not None:
            g = g_fn(*e_refs)
        else:
            g = gs_ref[0].astype(F32)
            for s in range(1, ns):
                g = g + gs_ref[s].astype(F32)
        d, nm, nv = _adam_math(w_ref[...], g, m_ref[...], v_ref[...])
        g_ref[...] = g
        d_ref[...] = d
        nm_ref[...] = nm
        nv_ref[...] = nv

    blk = pl.BlockSpec((tr, C), lambda i: (i, 0))
    in_specs = [pl.BlockSpec((tr, e.shape[1]), lambda i: (i, 0)) if e.shape[0] == R else _full(e.shape) for e in g_extra]
    args = list(g_extra)
    if ns:
        in_specs.append(pl.BlockSpec((ns, tr, C), lambda i: (0, i, 0)))
        args.append(g_slabs)
    return pl.pallas_call(
        body, name=name, grid=(R // tr,), in_specs=in_specs + [blk] * 3, out_specs=[blk] * 4,
        out_shape=[jax.ShapeDtypeStruct((R, C), F32)] * 4, compiler_params=_params(),
    )(*args, w, m, v)


def _adam_small(small_all, params):
    n = len(params)

    def body(*refs):
        s_ref, refs = refs[0], refs[1:]
        wmv, loss_ref, outs = refs[:3 * n], refs[3 * n], refs[3 * n + 1:]
        tot = s_ref[0]
        for i in range(1, N_DEV):
            tot = tot + s_ref[i]
        loss_ref[...] = tot[:, SMALL_W - 128:]
        for j, (w, _, _, off) in enumerate(params):
            w_ref, m_ref, v_ref = wmv[3 * j:3 * j + 3]
            g_ref, d_ref, nm_ref, nv_ref = outs[4 * j:4 * j + 4]
            if w.shape[0] == 2:
                lb = _lower_bound(w_ref)
                g0 = tot[:, off:off + w.shape[1]] * lb * (1.0 - lb)
                rows = [(slice(0, 1), g0), (slice(1, 2), -g0)]
            else:
                rows = [(slice(0, 1), tot[:, off:off + w.shape[1]])]
            for rs, g in rows:
                d, nm, nv = _adam_math(w_ref[rs, :], g, m_ref[rs, :], v_ref[rs, :])
                g_ref[rs, :], d_ref[rs, :], nm_ref[rs, :], nv_ref[rs, :] = g, d, nm, nv

    out_shape = [jax.ShapeDtypeStruct((1, 128), F32)]
    for w, _, _, _ in params:
        out_shape += [jax.ShapeDtypeStruct(w.shape, F32)] * 4
    res = pl.pallas_call(body, name="adam_small", out_shape=out_shape, compiler_params=_params())(
        small_all, *[a for w, m, v, _ in params for a in (w, m, v)])
    return res[0], [tuple(res[1 + 4 * j:5 + 4 * j]) for j in range(n)]


def _rot_half_cols(w):
    return jnp.concatenate([-w[..., 32:], w[..., :32]], axis=-1)


def _unrot_half_cols(dw_rot):
    return jnp.concatenate([dw_rot[..., 32:], -dw_rot[..., :32]], axis=-1)


def _cols_from_slabs(g):
    s, r, c = g.shape
    return jnp.transpose(g, (1, 0, 2)).reshape(r, s * c)


def _slabs_from_cols(w):
    r, c = w.shape
    return jnp.transpose(w.reshape(r, N_DEV, c // N_DEV), (1, 0, 2))


def _ext_in(w_in):
    k_in = w_in.shape[0]
    z64, z128 = jnp.zeros((k_in, 64), BF16), jnp.zeros((k_in, 128), BF16)
    wk = w_in[:, 2560:2624]
    return jnp.concatenate([w_in[:, :2560], z128, wk, z64, z128, _rot_half_cols(wk), z64], axis=1)


def _ext_q(w_q_up):
    r = w_q_up.shape[0]
    z64, z128 = jnp.zeros((r, 64), BF16), jnp.zeros((r, 128), BF16)
    wq = w_q_up.reshape(r, HEADS, HEAD_DIM + ROPE_DIM)
    main = [jnp.concatenate([wq[:, h, :HEAD_DIM], wq[:, h, HEAD_DIM:], z64], axis=1) for h in range(HEADS)]
    rot = [jnp.concatenate([z128, _rot_half_cols(wq[:, h, HEAD_DIM:]), z64], axis=1) for h in range(HEADS)]
    return jnp.concatenate(main + rot, axis=1)


def _ext_kv(w_kv_up):
    r = w_kv_up.shape[0]
    z128 = jnp.zeros((r, 128), BF16)
    wkv = w_kv_up.reshape(r, HEADS, 2 * HEAD_DIM)
    kpad = [jnp.concatenate([wkv[:, h, :HEAD_DIM], z128], axis=1) for h in range(HEADS)]
    vals = [wkv[:, h, HEAD_DIM:] for h in range(HEADS)]
    return jnp.concatenate(kpad + vals, axis=1)


def _grad_in_from_ext(dw_in_h, dw_in_m):
    dwk = dw_in_m[:, 512 + 128:512 + 192] + _unrot_half_cols(dw_in_m[:, 768 + 128:768 + 192])
    return jnp.concatenate([dw_in_h, dw_in_m[:, :512], dwk], axis=1)


def _grads_qkv_from_ext(dwq_ext, dwkv_ext):
    qcols = []
    for h in range(HEADS):
        main, rot = dwq_ext[:, 256 * h:256 * h + 256], dwq_ext[:, 1024 + 256 * h:1280 + 256 * h]
        qcols += [main[:, :128], main[:, 128:192] + _unrot_half_cols(rot[:, 128:192])]
    kvcols = []
    for h in range(HEADS):
        kvcols += [dwkv_ext[:, 256 * h:256 * h + 128], dwkv_ext[:, 1024 + 128 * h:1152 + 128 * h]]
    return jnp.concatenate(qcols, axis=1), jnp.concatenate(kvcols, axis=1)


SMALL_W = 6144 + 512 + 512 + 256 + 256 + 4 * 1024 + 128


def kernel(x, c, positions, w_ada, b_ada, w_in, hg_lower_bounds, hg_norm_w, mla_q_norm_w, w_q_up, mla_kv_norm_w, w_kv_up, w_out, ln1_g, ln1_b, w_mlp_in, w_mlp_out, ln2_g, ln2_b, loss_target, m_w_ada, m_b_ada, m_w_in, m_hg_lower_bounds, m_hg_norm_w, m_mla_q_norm_w, m_w_q_up, m_mla_kv_norm_w, m_w_kv_up, m_w_out, m_ln1_g, m_ln1_b, m_w_mlp_in, m_w_mlp_out, m_ln2_g, m_ln2_b, v_w_ada, v_b_ada, v_w_in, v_hg_lower_bounds, v_hg_norm_w, v_mla_q_norm_w, v_w_q_up, v_mla_kv_norm_w, v_w_kv_up, v_w_out, v_ln1_g, v_ln1_b, v_w_mlp_in, v_w_mlp_out, v_ln2_g, v_ln2_b):
    T = x.shape[1]
    me = 4 * lax.axis_index("x") + 2 * lax.axis_index("y") + lax.axis_index("c")
    xs, tgt = x[0], loss_target[0]
    big = dict(w_in=w_in[0], w_q_up=w_q_up[0], w_kv_up=w_kv_up[0], w_out=w_out[0], w_mlp_in=w_mlp_in[0],
               w_mlp_out=w_mlp_out[0])
    names = list(big)

    bf = {n: big[n].astype(BF16) for n in names}
    g_in, g_c = _gather_two_level([bf["w_in"], c], name="gather_w_in")
    c_all = g_c.reshape(N_DEV, D_MODEL)

    ada_cols = w_ada.shape[2]
    mod_part, cond = _mod_part(c_all, w_ada[0], lax.dynamic_slice(b_ada, (0, me * ada_cols), (1, ada_cols)))
    (mod_all,) = _exchange([mod_part], scatter=False, name="gather_mod")
    mod_row = lax.dynamic_slice(mod_all, (0, me, 0), (N_DEV, 1, ada_cols)).reshape(1, N_DEV * ada_cols)
    sh_a, sc_a, g_a, sh_m, sc_m, g_m = [mod_row[:, D_MODEL * i:D_MODEL * (i + 1)] for i in range(6)]

    w_in_ext = _ext_in(_cols_from_slabs(g_in))
    z, (g_q, g_kv, g_out) = _matmul(xs, w_in_ext, "NN", "in_proj", a_fn=_modulate, extras=(sc_a, sh_a),
                                    exchange=_Exchange([bf["w_q_up"], bf["w_kv_up"], bf["w_out"]], False))
    wq_ext, wkv_ext = _ext_q(_cols_from_slabs(g_q)), _ext_kv(_cols_from_slabs(g_kv))
    w_out_full = g_out.reshape(D_MODEL, D_MODEL)
    inv_freq = 1.0 / (ROPE_THETA ** (jnp.arange(0, ROPE_DIM, 2, dtype=F32) / ROPE_DIM))
    zeros = lambda n: jnp.zeros((n,), F32)
    invf = jnp.concatenate([zeros(128), inv_freq, inv_freq, zeros(64)]).reshape(1, QK_PAD)
    m_one = jnp.concatenate([jnp.ones((128,), F32), zeros(128)]).reshape(1, QK_PAD)
    m_rot = jnp.concatenate([zeros(128), jnp.ones((64,), F32), zeros(64)]).reshape(1, QK_PAD)
    q, k, v, c1, s1, cqn, ckvn = _mla_pre(z, positions.reshape(T, 1), invf, m_one, m_rot, wq_ext, wkv_ext,
                                          mla_q_norm_w, mla_kv_norm_w)
    (o_raw, o_gated, s_prev), (w1,) = _hgrn_fwd(z, hg_lower_bounds, hg_norm_w,
                                                exchange=_Exchange([bf["w_mlp_in"]], False))
    (o_mla, lse), (w2,) = _attn_fwd(q, k, v, exchange=_Exchange([bf["w_mlp_out"]], False))
    mixcat = jnp.concatenate([o_gated, o_mla.astype(BF16)], axis=1)
    mix, xhat1, rstd1, u2 = _mix_ln1(mixcat, w_out_full, xs, g_a, ln1_g, ln1_b, sc_m, sh_m)[0]
    r, dr2, dh, dln2_g, dln2_b, dg_m, loss_part = _mlp_fwd(u2, w1, w2, xhat1, ln1_g, ln1_b, g_m, ln2_g, ln2_b, tgt)

    dhpre, dr1, dmix, dsc_m, dsh_m, dln1_g, dln1_b, dg_a = _mlp_bwd(dh, w1, w2, r, dr2, xhat1, rstd1, mix, ln1_g,
                                                                    ln1_b, sc_m, g_a)
    received = {}
    dw2 = _matmul(r, dh, "TN", "wgrad_mlp_out", out_dtype=BF16, a_fn=_square, tm=1024)
    dw1 = _matmul(u2, dhpre, "TN", "wgrad_mlp_in", out_dtype=BF16, tm=1024, out_slabs=N_DEV)
    dmixcat = _matmul(dmix, w_out_full, "NT", "dgrad_out")
    dw_out = _matmul(mixcat, dmix, "TN", "wgrad_out", out_dtype=BF16, tm=1024)
    (dz_h, dlb, dnw), (received["w_out"],) = _hgrn_bwd(
        dmixcat, z, o_raw, s_prev, hg_lower_bounds, hg_norm_w,
        exchange=_Exchange([dw_out.reshape(N_DEV, D_MODEL // N_DEV, D_MODEL)], True))
    (dq, dk, dv), (received["w_mlp_in"], received["w_mlp_out"]) = _attn_bwd(
        q, k, v, dmixcat, o_mla, lse,
        exchange=_Exchange([dw1, dw2.reshape(N_DEV, dw2.shape[0] // N_DEV, D_MODEL)], True))
    dz_m, dq_ext, dkv_ext, dqnw, dkvnw = _mla_bwd(dq, dk, dv, z, c1, s1, wq_ext, wkv_ext, mla_q_norm_w,
                                                   mla_kv_norm_w)
    dwq_ext = _matmul(cqn, dq_ext, "TN", "wgrad_q_up", tn=2048)
    dwkv_ext = _matmul(ckvn, dkv_ext, "TN", "wgrad_kv_up", tn=1536)
    dwq, dwkv = _grads_qkv_from_ext(dwq_ext, dwkv_ext)
    dw_in_h, (received["w_q_up"], received["w_kv_up"]) = _matmul(
        xs, dz_h, "TN", "wgrad_in_h", a_fn=_modulate, extras=(sc_a, sh_a), tm=1024,
        exchange=_Exchange([_slabs_from_cols(g).astype(BF16) for g in (dwq, dwkv)], True))
    dw_in_m = _matmul(xs, dz_m, "TN", "wgrad_in_m", a_fn=_modulate, extras=(sc_a, sh_a), tm=1024)
    dw_in = _grad_in_from_ext(dw_in_h, dw_in_m)
    (grad_x, dsc_a, dsh_a), (received["w_in"],) = _input_bwd(
        dz_h, dz_m, w_in_ext, xs, dr1, sc_a, exchange=_Exchange([_slabs_from_cols(dw_in).astype(BF16)], True))

    small = jnp.concatenate([dsh_a, dsc_a, dg_a, dsh_m, dsc_m, dg_m, dlb, dnw, dqnw, dkvnw, dln1_g, dln1_b, dln2_g,
                             dln2_b, loss_part], axis=1)
    (small_all,) = _exchange([small], scatter=False, name="gather_small")

    moments = dict(w_in=(m_w_in, v_w_in), w_q_up=(m_w_q_up, v_w_q_up), w_kv_up=(m_w_kv_up, v_w_kv_up),
                   w_out=(m_w_out, v_w_out), w_mlp_in=(m_w_mlp_in, v_w_mlp_in), w_mlp_out=(m_w_mlp_out, v_w_mlp_out))
    res = {}
    for n in names:
        res[n] = _adam(received[n], big[n], moments[n][0][0], moments[n][1][0], name="adam_" + n)
    dmod_cols = lax.dynamic_slice(small_all.reshape(N_DEV, SMALL_W), (0, me * ada_cols), (N_DEV, ada_cols))
    cond_t = cond.T

    def ada_grad(ct_ref, dm_ref):
        g = ct_ref[:, 0:1] * dm_ref[0:1, :]
        for b in range(1, N_DEV):
            g = g + ct_ref[:, b:b + 1] * dm_ref[b:b + 1, :]
        return g

    res["w_ada"] = _adam(None, w_ada[0], m_w_ada[0], v_w_ada[0], name="adam_w_ada", g_fn=ada_grad,
                         g_extra=(cond_t, dmod_cols))

    small_params = [("b_ada", b_ada, m_b_ada, v_b_ada, 0),
                    ("hg_lower_bounds", hg_lower_bounds, m_hg_lower_bounds, v_hg_lower_bounds, 6144),
                    ("hg_norm_w", hg_norm_w, m_hg_norm_w, v_hg_norm_w, 6656),
                    ("mla_q_norm_w", mla_q_norm_w, m_mla_q_norm_w, v_mla_q_norm_w, 7168),
                    ("mla_kv_norm_w", mla_kv_norm_w, m_mla_kv_norm_w, v_mla_kv_norm_w, 7424),
                    ("ln1_g", ln1_g, m_ln1_g, v_ln1_g, 7680), ("ln1_b", ln1_b, m_ln1_b, v_ln1_b, 8704),
                    ("ln2_g", ln2_g, m_ln2_g, v_ln2_g, 9728), ("ln2_b", ln2_b, m_ln2_b, v_ln2_b, 10752)]
    loss_row, small_res = _adam_small(small_all, [p[1:] for p in small_params])
    for p, r4 in zip(small_params, small_res):
        res[p[0]] = r4
    loss = loss_row[0, 0]

    order = ["w_ada", "b_ada", "w_in", "hg_lower_bounds", "hg_norm_w", "mla_q_norm_w", "w_q_up", "mla_kv_norm_w",
             "w_kv_up", "w_out", "ln1_g", "ln1_b", "w_mlp_in", "w_mlp_out", "ln2_g", "ln2_b"]
    shaped = {n: tuple(a.reshape((1,) + a.shape) if n in big or n == "w_ada" else a for a in res[n]) for n in order}
    outs = [loss, grad_x.reshape(1, T, D_MODEL)]
    for i in range(4):
        outs += [shaped[n][i] for n in order]
    return tuple(outs)
```

```python
import functools

import jax
import jax.numpy as jnp
import numpy as np
from jax import lax
from jax.experimental import pallas as pl
from jax.experimental.pallas import tpu as pltpu

F32, BF16 = jnp.float32, jnp.bfloat16
N_DEV = 8
D_MODEL = 1024
HEADS = 4
HEAD_DIM = 128
ROPE_DIM = 64
QK_PAD = 256
CHUNK = 64
ROPE_THETA = 10000.0
RMS_EPS = 1e-6
LN_EPS = 1e-5
ALPHA = 2.0 ** 0.25
ATT_SCALE = (HEAD_DIM + ROPE_DIM) ** -0.5
LN2 = float(np.log(2.0))
Q_PRESCALE = ATT_SCALE / LN2
ADAM_LR, ADAM_B1, ADAM_B2, ADAM_EPS, ADAM_WD, ADAM_STEP = 0.001, 0.9, 0.999, 1e-08, 0.01, 10
NEG_BIG = -1e30

ROW_TILE = 512
ROW_TILE_SMALL = 256
ATT_TILE = 512
HGRN_GROUP = 8
VMEM_LIMIT = 56 * 2 ** 20

NN = (((1,), (0,)), ((), ()))
NT = (((1,), (1,)), ((), ()))
TN = (((0,), (0,)), ((), ()))


def _dot(a, b, dims=NN):
    return lax.dot_general(a, b, dims, preferred_element_type=F32)


def _bdot(a, b, dims=NN):
    return lax.dot_general(a.astype(BF16), b.astype(BF16), dims, preferred_element_type=F32)


def _hdot(a, b, dims=NN):
    return lax.dot_general(a, b, dims, precision=lax.Precision.HIGHEST, preferred_element_type=F32)


def _params():
    return pltpu.CompilerParams(vmem_limit_bytes=VMEM_LIMIT)


def _sigmoid(x):
    return 1.0 / (1.0 + jnp.exp(-x))


def _rowsum(x):
    return jnp.sum(x, axis=0, keepdims=True)


def _lanemean(x):
    return jnp.mean(x, axis=-1, keepdims=True)


def _full(shape):
    nd = len(shape)
    return pl.BlockSpec(shape, lambda *_: (0,) * nd)


class _Exchange:
    def __init__(self, arrs, scatter):
        self.arrs, self.scatter, self.n = list(arrs), scatter, len(arrs)
        self.out_shape = [jax.ShapeDtypeStruct((N_DEV,) + (a.shape[1:] if scatter else a.shape), a.dtype)
                          for a in self.arrs]
        n = self.n
        self.scratch = [pltpu.SemaphoreType.DMA((n, N_DEV - 1)), pltpu.SemaphoreType.DMA((n, N_DEV - 1)),
                        pltpu.SemaphoreType.DMA((n,))]

    def _copies(self, ins, outs, sems):
        send_sems, recv_sems, loc_sems = sems
        x, y, c = lax.axis_index("x"), lax.axis_index("y"), lax.axis_index("c")
        me = 4 * x + 2 * y + c
        copies = []
        for k in range(self.n):
            src_of = (lambda i, k=k: ins[k].at[i]) if self.scatter else (lambda i, k=k: ins[k])
            copies.append((pltpu.make_async_copy(src_of(me), outs[k].at[me], loc_sems.at[k]), None))
            for p in range(1, N_DEV):
                px = (1 - x) if p & 4 else x
                py = (1 - y) if p & 2 else y
                pc = (1 - c) if p & 1 else c
                peer = 4 * px + 2 * py + pc
                both = dict(send_sem=send_sems.at[k, p - 1], recv_sem=recv_sems.at[k, p - 1],
                            device_id=(px, py, pc), device_id_type=pl.DeviceIdType.MESH)
                send = pltpu.make_async_remote_copy(src_ref=src_of(peer), dst_ref=outs[k].at[me], **both)
                recv = pltpu.make_async_remote_copy(src_ref=src_of(peer), dst_ref=outs[k].at[peer], **both)
                copies.append((send, recv))
        return copies

    def start(self, ins, outs, sems):
        for first, _ in self._copies(ins, outs, sems):
            first.start()

    def middle(self, ins, outs, sems):
        pass

    def wait(self, ins, outs, sems):
        for first, recv in self._copies(ins, outs, sems):
            if recv is None:
                first.wait()
            else:
                recv.wait_recv()
                first.wait_send()


class _StagedGather:
    def __init__(self, arr):
        self.arrs, self.n = [arr], 1
        self.out_shape = [jax.ShapeDtypeStruct((N_DEV,) + arr.shape, arr.dtype)]
        self.scratch = [pltpu.VMEM((N_DEV,) + arr.shape, arr.dtype), pltpu.SemaphoreType.DMA((7,)),
                        pltpu.SemaphoreType.DMA((7,)), pltpu.SemaphoreType.DMA((2,))]

    def _parts(self, scr):
        stage, send_sems, recv_sems, loc_sems = scr
        x, y, c = lax.axis_index("x"), lax.axis_index("y"), lax.axis_index("c")
        me, sibling = (x, y, c), (x, y, 1 - c)
        chips = [(1 - x, y), (x, 1 - y), (1 - x, 1 - y)]

        def copy(j, block, to):
            px, py, pc = block
            slot = stage.at[4 * px + 2 * py + pc]
            return pltpu.make_async_remote_copy(src_ref=slot, dst_ref=slot, send_sem=send_sems.at[j],
                                                recv_sem=recv_sems.at[j], device_id=to,
                                                device_id_type=pl.DeviceIdType.MESH)

        return stage, loc_sems, me, sibling, chips, c, copy

    def start(self, ins, outs, scr):
        stage, loc_sems, me, sibling, chips, c, copy = self._parts(scr)
        x, y, _ = me
        own = pltpu.make_async_copy(ins[0], stage.at[4 * x + 2 * y + c], loc_sems.at[0])
        own.start()
        own.wait()
        copy(0, me, sibling).start()
        for j, chip in enumerate(chips):
            copy(1 + j, me, (*chip, c)).start()

    def middle(self, ins, outs, scr):
        stage, loc_sems, me, sibling, chips, c, copy = self._parts(scr)
        for j, chip in enumerate(chips):
            copy(1 + j, (*chip, c), me).wait_recv()
            copy(4 + j, (*chip, c), sibling).start()

    def wait(self, ins, outs, scr):
        stage, loc_sems, me, sibling, chips, c, copy = self._parts(scr)
        copy(0, sibling, me).wait_recv()
        for j, chip in enumerate(chips):
            copy(4 + j, (*chip, 1 - c), me).wait_recv()
        copy(0, me, sibling).wait_send()
        for j, chip in enumerate(chips):
            copy(1 + j, me, (*chip, c)).wait_send()
            copy(4 + j, (*chip, c), sibling).wait_send()
        whole = pltpu.make_async_copy(stage, outs[0], loc_sems.at[1])
        whole.start()
        whole.wait()


def _call(body, name, args, out_shape, grid=(), in_specs=(), out_specs=(), scratch_shapes=(), exchange=None,
          middle_at=0.8):
    if exchange is None:
        return pl.pallas_call(body, name=name, grid=grid, in_specs=list(in_specs), out_specs=list(out_specs),
                              out_shape=list(out_shape), scratch_shapes=list(scratch_shapes),
                              compiler_params=_params())(*args), None
    exs = list(exchange) if isinstance(exchange, (list, tuple)) else [exchange]
    ni, no, ns, nx = len(args), len(out_shape), len(scratch_shapes), sum(e.n for e in exs)
    steps = int(np.prod(grid))
    mid_step = min(max(int(steps * middle_at), 1), steps - 1)

    def wrapped(*refs):
        a, xi = refs[:ni], refs[ni:ni + nx]
        o, xo = refs[ni + nx:ni + nx + no], refs[ni + nx + no:ni + 2 * nx + no]
        s, xs = refs[ni + 2 * nx + no:ni + 2 * nx + no + ns], refs[ni + 2 * nx + no + ns:]
        parts, at, sat = [], 0, 0
        for e in exs:
            parts.append((e, xi[at:at + e.n], xo[at:at + e.n], xs[sat:sat + len(e.scratch)]))
            at, sat = at + e.n, sat + len(e.scratch)
        step = 0
        for d, g in enumerate(grid):
            step = step * g + pl.program_id(d)

        @pl.when(step == 0)
        def _():
            for e, ins, outs, sems in parts:
                e.start(ins, outs, sems)

        @pl.when(step == mid_step)
        def _():
            for e, ins, outs, sems in parts:
                e.middle(ins, outs, sems)

        body(*a, *o, *s)

        @pl.when(step == steps - 1)
        def _():
            for e, ins, outs, sems in parts:
                e.wait(ins, outs, sems)

    hbm = pl.BlockSpec(memory_space=pltpu.HBM)
    res = pl.pallas_call(
        wrapped, name=name, grid=grid, in_specs=list(in_specs) + [hbm] * nx, out_specs=list(out_specs) + [hbm] * nx,
        out_shape=list(out_shape) + [o_ for e in exs for o_ in e.out_shape],
        scratch_shapes=list(scratch_shapes) + [s_ for e in exs for s_ in e.scratch],
        compiler_params=_params())(*args, *[a_ for e in exs for a_ in e.arrs])
    return res[:no], res[no:]


def _gather_two_level(arrs, name):
    n = len(arrs)
    out_shape = [jax.ShapeDtypeStruct((N_DEV,) + a.shape, a.dtype) for a in arrs]

    def body(*refs):
        ins, outs = refs[:n], refs[n:2 * n]
        send_sems, recv_sems, loc_sems = refs[2 * n:]
        x, y, c = lax.axis_index("x"), lax.axis_index("y"), lax.axis_index("c")
        me, sibling = (x, y, c), (x, y, 1 - c)
        chips = [(1 - x, y), (x, 1 - y), (1 - x, 1 - y)]

        def copy(k, j, block, to, src=None):
            px, py, pc = block
            dst = outs[k].at[4 * px + 2 * py + pc]
            return pltpu.make_async_remote_copy(src_ref=dst if src is None else src, dst_ref=dst,
                                                send_sem=send_sems.at[k, j], recv_sem=recv_sems.at[k, j],
                                                device_id=to, device_id_type=pl.DeviceIdType.MESH)

        mine = [pltpu.make_async_copy(ins[k], outs[k].at[4 * x + 2 * y + c], loc_sems.at[k]) for k in range(n)]
        first = []
        for k in range(n):
            mine[k].start()
            first.append(copy(k, 0, me, sibling, src=ins[k]))
            first += [copy(k, 1 + j, me, (*chip, c), src=ins[k]) for j, chip in enumerate(chips)]
        for cp in first:
            cp.start()
        passed = []
        for j, chip in enumerate(chips):
            for k in range(n):
                copy(k, 1 + j, (*chip, c), me).wait_recv()
                passed.append(copy(k, 4 + j, (*chip, c), sibling))
                passed[-1].start()
        for k in range(n):
            copy(k, 0, sibling, me).wait_recv()
            for j, chip in enumerate(chips):
                copy(k, 4 + j, (*chip, 1 - c), me).wait_recv()
        for cp in first + passed:
            cp.wait_send()
        for cp in mine:
            cp.wait()

    vmem = pl.BlockSpec(memory_space=pltpu.VMEM)
    return pl.pallas_call(body, name=name, out_shape=out_shape, in_specs=[vmem] * n, out_specs=[vmem] * n,
                          scratch_shapes=[pltpu.SemaphoreType.DMA((n, 7)), pltpu.SemaphoreType.DMA((n, 7)),
                                          pltpu.SemaphoreType.DMA((n,))], compiler_params=_params())(*arrs)


def _exchange(arrs, scatter, name):
    ex = _Exchange(arrs, scatter)

    def body(*refs):
        ins, outs, sems = refs[:ex.n], refs[ex.n:2 * ex.n], refs[2 * ex.n:]
        ex.start(ins, outs, sems)
        ex.wait(ins, outs, sems)

    hbm = pl.BlockSpec(memory_space=pltpu.HBM)
    return pl.pallas_call(body, name=name, out_shape=ex.out_shape, in_specs=[hbm] * ex.n, out_specs=[hbm] * ex.n,
                          scratch_shapes=ex.scratch)(*ex.arrs)


def _matmul(a, b, mode, name, out_dtype=F32, tm=512, tn=1024, tk=1024, a_fn=None, extras=(), out_slabs=None,
            exchange=None):
    if mode == "NN":
        (M, K), N = a.shape, b.shape[1]
    elif mode == "NT":
        (M, K), N = a.shape, b.shape[0]
    else:
        (K, M), N = a.shape, b.shape[1]
    if out_slabs:
        tn = N // out_slabs
    tm, tn, tk = min(tm, M), min(tn, N), min(tk, K)
    assert M % tm == 0 and N % tn == 0 and K % tk == 0, (name, M, N, K)
    nk = K // tk
    dims = {"NN": NN, "NT": NT, "TN": TN}[mode]
    ne = len(extras)

    def body(a_ref, b_ref, *rest):
        e_refs, o_ref, acc_ref = rest[:ne], rest[ne], rest[ne + 1]
        k = pl.program_id(2)

        @pl.when(k == 0)
        def _():
            acc_ref[...] = jnp.zeros_like(acc_ref)

        at = a_ref[...]
        if a_fn is not None:
            at = a_fn(at.astype(F32), *[e[...] for e in e_refs])
        acc_ref[...] += _bdot(at, b_ref[...], dims)

        @pl.when(k == nk - 1)
        def _():
            o_ref[...] = acc_ref[...].astype(out_dtype)

    if mode == "TN":
        a_spec = pl.BlockSpec((tk, tm), lambda i, j, k: (k, i))
        e_spec = pl.BlockSpec((1, tm), lambda i, j, k: (0, i))
    else:
        a_spec = pl.BlockSpec((tm, tk), lambda i, j, k: (i, k))
        e_spec = pl.BlockSpec((1, tk), lambda i, j, k: (0, k))
    if mode == "NT":
        b_spec = pl.BlockSpec((tn, tk), lambda i, j, k: (j, k))
    else:
        b_spec = pl.BlockSpec((tk, tn), lambda i, j, k: (k, j))
    if out_slabs:
        o_shape = jax.ShapeDtypeStruct((out_slabs, M, tn), out_dtype)
        o_spec = pl.BlockSpec((None, tm, tn), lambda i, j, k: (j, i, 0))
    else:
        o_shape = jax.ShapeDtypeStruct((M, N), out_dtype)
        o_spec = pl.BlockSpec((tm, tn), lambda i, j, k: (i, j))
    (out,), got = _call(body, name, (a, b, *extras), [o_shape], grid=(M // tm, N // tn, nk),
                        in_specs=[a_spec, b_spec] + [e_spec] * ne, out_specs=[o_spec],
                        scratch_shapes=[pltpu.VMEM((tm, tn), F32)], exchange=exchange)
    return out if exchange is None else (out, got)


def _modulate(x, sc, sh):
    return x * (1.0 + sc) + sh


def _square(x):
    return x * x


def _mod_part(c_all, w_ada_s, b_s):
    def body(c_ref, w_ref, b_ref, mod_ref, cond_ref):
        cv = c_ref[...]
        cond = cv * _sigmoid(cv)
        cond_ref[...] = cond
        mod_ref[...] = _bdot(cond, w_ref[...]) + b_ref[...]

    return pl.pallas_call(
        body, name="mod_part",
        out_shape=[jax.ShapeDtypeStruct((N_DEV, w_ada_s.shape[1]), F32), jax.ShapeDtypeStruct(c_all.shape, F32)],
        compiler_params=_params(),
    )(c_all, w_ada_s, b_s)


def _rms_fwd(x, w):
    rs = lax.rsqrt(_lanemean(x * x) + RMS_EPS)
    return x * rs * w, rs


def _rms_bwd(x, rs, w, dy):
    xhat = x * rs
    dxh = dy * w
    return rs * (dxh - xhat * _lanemean(dxh * xhat)), dy * xhat


def _mla_pre(z, pos_col, invf, m_one, m_rot, wq_ext, wkv_ext, qnw, kvnw):
    T = z.shape[0]
    tm = min(ROW_TILE, T)

    def body(z_ref, pos_ref, invf_ref, mone_ref, mrot_ref, wq_ref, wkv_ref, qnw_ref, kvnw_ref,
             q_ref, k_ref, v_ref, c1_ref, s1_ref, cqn_ref, ckvn_ref):
        ang = pos_ref[...].astype(F32) * invf_ref[...]
        c1 = mone_ref[...] + mrot_ref[...] * jnp.cos(ang)
        s1 = mrot_ref[...] * jnp.sin(ang)
        c1_ref[...] = c1
        s1_ref[...] = s1
        cqn, _ = _rms_fwd(z_ref[:, 0:256], qnw_ref[...])
        ckvn, _ = _rms_fwd(z_ref[:, 256:512], kvnw_ref[...])
        cqn_ref[...] = cqn.astype(BF16)
        ckvn_ref[...] = ckvn.astype(BF16)
        qe = _bdot(cqn, wq_ref[...])
        kve = _bdot(ckvn, wkv_ref[...])
        k_rope = z_ref[:, 512:768] * c1 + z_ref[:, 768:1024] * s1
        for h in range(HEADS):
            q_ref[h] = ((qe[:, 256 * h:256 * h + 256] * c1 + qe[:, 1024 + 256 * h:1280 + 256 * h] * s1)
                        * Q_PRESCALE).astype(BF16)
            k_ref[h] = (kve[:, 256 * h:256 * h + 256] + k_rope).astype(BF16)
            v_ref[h] = kve[:, 1024 + 128 * h:1152 + 128 * h].astype(BF16)

    row = lambda i: (i, 0)
    head = lambda i: (0, i, 0)
    return pl.pallas_call(
        body, name="mla_pre", grid=(T // tm,),
        in_specs=[pl.BlockSpec((tm, 1024), lambda i: (i, 2)), pl.BlockSpec((tm, 1), row),
                  _full((1, 256)), _full((1, 256)), _full((1, 256)), _full(wq_ext.shape), _full(wkv_ext.shape),
                  _full((1, 256)), _full((1, 256))],
        out_specs=[pl.BlockSpec((HEADS, tm, QK_PAD), head), pl.BlockSpec((HEADS, tm, QK_PAD), head),
                   pl.BlockSpec((HEADS, tm, HEAD_DIM), head), pl.BlockSpec((tm, 256), row), pl.BlockSpec((tm, 256), row),
                   pl.BlockSpec((tm, 256), row), pl.BlockSpec((tm, 256), row)],
        out_shape=[jax.ShapeDtypeStruct((HEADS, T, QK_PAD), BF16), jax.ShapeDtypeStruct((HEADS, T, QK_PAD), BF16),
                   jax.ShapeDtypeStruct((HEADS, T, HEAD_DIM), BF16), jax.ShapeDtypeStruct((T, 256), F32),
                   jax.ShapeDtypeStruct((T, 256), F32), jax.ShapeDtypeStruct((T, 256), BF16),
                   jax.ShapeDtypeStruct((T, 256), BF16)],
        compiler_params=_params(),
    )(z, pos_col, invf, m_one, m_rot, wq_ext, wkv_ext, qnw, kvnw)


def _mla_bwd(dq, dk, dv, z, c1, s1, wq_ext, wkv_ext, qnw, kvnw):
    T = z.shape[0]
    tm = min(ROW_TILE_SMALL, T)

    def body(dq_ref, dk_ref, dv_ref, z_ref, c1_ref, s1_ref, wq_ref, wkv_ref, qnw_ref, kvnw_ref,
             dz_ref, dqe_ref, dkve_ref, dqnw_ref, dkvnw_ref):
        @pl.when(pl.program_id(0) == 0)
        def _():
            dqnw_ref[...] = jnp.zeros_like(dqnw_ref)
            dkvnw_ref[...] = jnp.zeros_like(dkvnw_ref)

        c1, s1 = c1_ref[...], s1_ref[...]
        dkpe = jnp.zeros((tm, QK_PAD), F32)
        for h in range(HEADS):
            dqh, dkh = dq_ref[h] * Q_PRESCALE, dk_ref[h]
            dqe_ref[:, 256 * h:256 * h + 256] = (dqh * c1).astype(BF16)
            dqe_ref[:, 1024 + 256 * h:1280 + 256 * h] = (dqh * s1).astype(BF16)
            dkve_ref[:, 256 * h:256 * h + 256] = dkh.astype(BF16)
            dkve_ref[:, 1024 + 128 * h:1152 + 128 * h] = dv_ref[h].astype(BF16)
            dkpe = dkpe + dkh
        dcqn = _dot(dqe_ref[...], wq_ref[...], NT)
        dckvn = _dot(dkve_ref[...], wkv_ref[...], NT)
        cq, ckv = z_ref[:, 0:256], z_ref[:, 256:512]
        _, rsq = _rms_fwd(cq, qnw_ref[...])
        _, rskv = _rms_fwd(ckv, kvnw_ref[...])
        dcq, wq_rows = _rms_bwd(cq, rsq, qnw_ref[...], dcqn)
        dckv, wkv_rows = _rms_bwd(ckv, rskv, kvnw_ref[...], dckvn)
        dqnw_ref[...] += _rowsum(wq_rows)
        dkvnw_ref[...] += _rowsum(wkv_rows)
        dz_ref[:, 0:256] = dcq
        dz_ref[:, 256:512] = dckv
        dz_ref[:, 512:768] = dkpe * c1
        dz_ref[:, 768:1024] = dkpe * s1

    row = lambda i: (i, 0)
    head = lambda i: (0, i, 0)
    return pl.pallas_call(
        body, name="mla_bwd", grid=(T // tm,),
        in_specs=[pl.BlockSpec((HEADS, tm, QK_PAD), head), pl.BlockSpec((HEADS, tm, QK_PAD), head),
                  pl.BlockSpec((HEADS, tm, HEAD_DIM), head), pl.BlockSpec((tm, 1024), lambda i: (i, 2)),
                  pl.BlockSpec((tm, 256), row), pl.BlockSpec((tm, 256), row), _full(wq_ext.shape), _full(wkv_ext.shape),
                  _full((1, 256)), _full((1, 256))],
        out_specs=[pl.BlockSpec((tm, 1024), row), pl.BlockSpec((tm, 2048), row), pl.BlockSpec((tm, 1536), row),
                   _full((1, 256)), _full((1, 256))],
        out_shape=[jax.ShapeDtypeStruct((T, 1024), F32), jax.ShapeDtypeStruct((T, 2048), BF16),
                   jax.ShapeDtypeStruct((T, 1536), BF16), jax.ShapeDtypeStruct((1, 256), F32),
                   jax.ShapeDtypeStruct((1, 256), F32)],
        compiler_params=_params(),
    )(dq, dk, dv, z, c1, s1, wq_ext, wkv_ext, qnw, kvnw)


_HEAD_LANES = [slice(HEAD_DIM * h, HEAD_DIM * (h + 1)) for h in range(HEADS)]


def _lower_bound(lbraw_ref):
    a0, a1 = lbraw_ref[0:1, :], lbraw_ref[1:2, :]
    mx = jnp.maximum(a0, a1)
    e0, e1 = jnp.exp(a0 - mx), jnp.exp(a1 - mx)
    return e0 / (e0 + e1)


def _tri(lower):
    r = lax.broadcasted_iota(jnp.int32, (CHUNK, CHUNK), 0)
    c = lax.broadcasted_iota(jnp.int32, (CHUNK, CHUNK), 1)
    return (r >= c) if lower else (r <= c)


def _hgrn_gates(q, f, lb, tri_lo):
    sg = _sigmoid(f)
    forget = lb + (1.0 - lb) * sg
    k = 1.0 - forget
    b = _hdot(tri_lo.astype(F32), jnp.log(forget))
    b_ref, b_last = b[CHUNK // 2 - 1:CHUNK // 2, :], b[CHUNK - 1:CHUNK, :]
    e1, e2, e3, e4 = jnp.exp(b - b_ref), jnp.exp(b_ref - b), jnp.exp(b_last - b), jnp.exp(b)
    return dict(sg=sg, forget=forget, k=k, e1=e1, e2=e2, e3=e3, e4=e4, qa=q * e1, ka=k * e2, kl=k * e3, qb=q * e4,
                decay=jnp.exp(b_last))


def _hgrn_fwd(z, lbraw, nw, exchange=None):
    T = z.shape[0]
    G = min(HGRN_GROUP, T // CHUNK)
    rows = G * CHUNK
    n_chunks = T // CHUNK

    def body(q_ref, f_ref, i_ref, g_ref, lbraw_ref, nw_ref, oraw_ref, og_ref, sp_ref, st_ref):
        @pl.when(pl.program_id(0) == 0)
        def _():
            st_ref[...] = jnp.zeros_like(st_ref)

        lb_all = _lower_bound(lbraw_ref)
        tri_lo = _tri(True)

        def chunk(cc, carry):
            rs = pl.ds(pl.multiple_of(cc * CHUNK, CHUNK), CHUNK)
            t = _hgrn_gates(q_ref[rs, :], f_ref[rs, :], lb_all, tri_lo)
            v, gate = i_ref[rs, :], g_ref[rs, :]
            st = [st_ref[h] for h in range(HEADS)]
            a = [jnp.where(tri_lo, _bdot(t["qa"][:, s], t["ka"][:, s], NT), 0.0) for s in _HEAD_LANES]
            kv = [_bdot(v[:, s], t["kl"][:, s], TN) for s in _HEAD_LANES]
            o = [_bdot(a[h], v[:, s]) + _bdot(t["qb"][:, s], st[h], NT) for h, s in enumerate(_HEAD_LANES)]
            for h, s in enumerate(_HEAD_LANES):
                sp_ref[cc, h] = st[h]
                st_ref[h] = st[h] * t["decay"][:, s] + kv[h]
            oraw_ref[rs, :] = jnp.concatenate(o, axis=1)
            on = jnp.concatenate([_rms_fwd(o[h], nw_ref[:, s])[0] for h, s in enumerate(_HEAD_LANES)], axis=1)
            og_ref[rs, :] = (on * (gate * _sigmoid(gate))).astype(BF16)
            return carry

        lax.fori_loop(0, G, chunk, 0, unroll=2)

    col = lambda j: pl.BlockSpec((rows, 512), lambda r, j=j: (r, j))
    return _call(
        body, "hgrn_fwd", (z, z, z, z, lbraw, nw), grid=(T // rows,),
        in_specs=[col(0), col(1), col(2), col(3), _full((2, 512)), _full((1, 512))],
        out_specs=[col(0), col(0), pl.BlockSpec((G, HEADS, HEAD_DIM, HEAD_DIM), lambda r: (r, 0, 0, 0))],
        out_shape=[jax.ShapeDtypeStruct((T, 512), F32), jax.ShapeDtypeStruct((T, 512), BF16),
                   jax.ShapeDtypeStruct((n_chunks, HEADS, HEAD_DIM, HEAD_DIM), F32)],
        scratch_shapes=[pltpu.VMEM((HEADS, HEAD_DIM, HEAD_DIM), F32)], exchange=exchange)


def _hgrn_bwd(dmixcat, z, oraw, sprev, lbraw, nw, exchange=None):
    T = z.shape[0]
    G = min(HGRN_GROUP, T // CHUNK)
    rows = G * CHUNK
    ng = T // rows

    def body(dog_ref, q_ref, f_ref, i_ref, g_ref, oraw_ref, sp_ref, lbraw_ref, nw_ref,
             dz_ref, dlb_ref, dnw_ref, dst_ref):
        @pl.when(pl.program_id(0) == 0)
        def _():
            dst_ref[...] = jnp.zeros_like(dst_ref)
            dlb_ref[...] = jnp.zeros_like(dlb_ref)
            dnw_ref[...] = jnp.zeros_like(dnw_ref)

        lb_all = _lower_bound(lbraw_ref)
        tri_lo, tri_up = _tri(True), _tri(False)
        rowid = lax.broadcasted_iota(jnp.int32, (CHUNK, HEADS * HEAD_DIM), 0)

        def chunk(it, carry):
            cc = G - 1 - it
            rs = pl.ds(pl.multiple_of(cc * CHUNK, CHUNK), CHUNK)
            heads = list(enumerate(_HEAD_LANES))
            cat = lambda parts: jnp.concatenate(parts, axis=1)
            per_head_mean = lambda x: cat([jnp.broadcast_to(_lanemean(x[:, s]), (CHUNK, HEAD_DIM)) for s in _HEAD_LANES])
            t = _hgrn_gates(q_ref[rs, :], f_ref[rs, :], lb_all, tri_lo)
            v, gate, o, dog, nw_all = i_ref[rs, :], g_ref[rs, :], oraw_ref[rs, :], dog_ref[rs, :], nw_ref[...]
            rs_o = lax.rsqrt(per_head_mean(o * o) + RMS_EPS)
            xhat = o * rs_o
            sgg = _sigmoid(gate)
            d_on = dog * (gate * sgg)
            dz_ref[rs, 1536:2048] = dog * (xhat * nw_all) * (sgg * (1.0 + gate * (1.0 - sgg)))
            dxh = d_on * nw_all
            do = rs_o * (dxh - xhat * per_head_mean(dxh * xhat))
            dnw_ref[...] += _rowsum(d_on * xhat)
            st = [sp_ref[cc, h] for h in range(HEADS)]
            dst = [dst_ref[h] for h in range(HEADS)]
            a = [jnp.where(tri_lo, _bdot(t["qa"][:, s], t["ka"][:, s], NT), 0.0) for s in _HEAD_LANES]
            da = [jnp.where(tri_lo, _bdot(do[:, s], v[:, s], NT), 0.0) for s in _HEAD_LANES]
            dqb = cat([_bdot(do[:, s], st[h]) for h, s in heads])
            dkl = cat([_bdot(v[:, s], dst[h]) for h, s in heads])
            dv_ = cat([_bdot(t["kl"][:, s], dst[h], NT) + _bdot(a[h], do[:, s], TN) for h, s in heads])
            dqa = cat([_bdot(da[h], t["ka"][:, s]) for h, s in heads])
            dka = cat([_bdot(da[h], t["qa"][:, s], TN) for h, s in heads])
            ddecay = cat([_rowsum(dst[h] * st[h]) for h in range(HEADS)])
            for h, s in heads:
                dst_ref[h] = dst[h] * t["decay"][:, s] + _bdot(do[:, s], t["qb"][:, s], TN)
            pa, pk, pb, pl_ = dqa * t["qa"], dka * t["ka"], dqb * t["qb"], dkl * t["kl"]
            db = pa - pk + pb - pl_
            db = db + jnp.where(rowid == CHUNK // 2 - 1, _rowsum(pk - pa), 0.0)
            db = db + jnp.where(rowid == CHUNK - 1, _rowsum(pl_) + ddecay * t["decay"], 0.0)
            dlogf = _hdot(tri_up.astype(F32), db)
            dforget = dlogf / t["forget"] - (dka * t["e2"] + dkl * t["e3"])
            sg = t["sg"]
            dz_ref[rs, 0:512] = dqa * t["e1"] + dqb * t["e4"]
            dz_ref[rs, 512:1024] = dforget * (1.0 - lb_all) * sg * (1.0 - sg)
            dz_ref[rs, 1024:1536] = dv_
            dlb_ref[...] += _rowsum(dforget * (1.0 - sg))
            return carry

        lax.fori_loop(0, G, chunk, 0, unroll=2)

    col = lambda j: pl.BlockSpec((rows, 512), lambda r, j=j: (ng - 1 - r, j))
    return _call(
        body, "hgrn_bwd", (dmixcat, z, z, z, z, oraw, sprev, lbraw, nw), grid=(ng,),
        in_specs=[col(0), col(0), col(1), col(2), col(3), col(0),
                  pl.BlockSpec((G, HEADS, HEAD_DIM, HEAD_DIM), lambda r: (ng - 1 - r, 0, 0, 0)),
                  _full((2, 512)), _full((1, 512))],
        out_specs=[pl.BlockSpec((rows, 2048), lambda r: (ng - 1 - r, 0)), _full((1, 512)), _full((1, 512))],
        out_shape=[jax.ShapeDtypeStruct((T, 2048), F32), jax.ShapeDtypeStruct((1, 512), F32),
                   jax.ShapeDtypeStruct((1, 512), F32)],
        scratch_shapes=[pltpu.VMEM((HEADS, HEAD_DIM, HEAD_DIM), F32)], exchange=exchange)


def _diag_mask(t):
    r = lax.broadcasted_iota(jnp.int32, (t, t), 0)
    c = lax.broadcasted_iota(jnp.int32, (t, t), 1)
    return r >= c


def _attn_fwd(q, k, v, exchange=None):
    _, T, _ = q.shape
    t = min(ATT_TILE, T)

    def body(q_ref, k_ref, v_ref, o_ref, lse_ref):
        i = pl.program_id(1)
        qb = q_ref[...]

        def step(j, carry, masked):
            m, l, acc = carry
            ks = pl.ds(pl.multiple_of(j * t, t), t)
            s = _dot(qb, k_ref[ks, :], NT)
            if masked:
                s = jnp.where(_diag_mask(t), s, NEG_BIG)
            mn = jnp.maximum(m, jnp.max(s, axis=-1, keepdims=True))
            p = jnp.exp2(s - mn)
            al = jnp.exp2(m - mn)
            return mn, al * l + jnp.sum(p, axis=-1, keepdims=True), al * acc + _dot(p.astype(BF16), v_ref[ks, :])

        init = (jnp.full((t, 1), NEG_BIG, F32), jnp.zeros((t, 1), F32), jnp.zeros((t, HEAD_DIM), F32))
        carry = lax.fori_loop(0, i, lambda j, c: step(j, c, False), init)
        m, l, acc = step(i, carry, True)
        o_ref[...] = acc / l
        lse_ref[...] = jnp.broadcast_to(m + jnp.log2(l), (t, HEAD_DIM))

    return _call(
        body, "attn_fwd", (q, k, v), grid=(HEADS, T // t),
        in_specs=[pl.BlockSpec((None, t, QK_PAD), lambda h, i: (h, i, 0)),
                  pl.BlockSpec((None, T, QK_PAD), lambda h, i: (h, 0, 0)),
                  pl.BlockSpec((None, T, HEAD_DIM), lambda h, i: (h, 0, 0))],
        out_specs=[pl.BlockSpec((t, HEAD_DIM), lambda h, i: (i, h)),
                   pl.BlockSpec((None, t, HEAD_DIM), lambda h, i: (h, i, 0))],
        out_shape=[jax.ShapeDtypeStruct((T, HEADS * HEAD_DIM), F32), jax.ShapeDtypeStruct((HEADS, T, HEAD_DIM), F32)],
        exchange=exchange)


def _attn_bwd(q, k, v, dmixcat, o, lse, exchange=None):
    _, T, _ = q.shape
    t = min(ATT_TILE, T)
    nq = T // t

    def body(q_ref, k_ref, v_ref, do_ref, o_ref, lse_ref, dq_ref, dk_ref, dv_ref, delta_ref):
        j = pl.program_id(1)

        @pl.when(j == 0)
        def _():
            dq_ref[...] = jnp.zeros_like(dq_ref)

            def fill(i, carry):
                rs = pl.ds(pl.multiple_of(i * t, t), t)
                delta_ref[rs, :] = jnp.broadcast_to(
                    jnp.sum(do_ref[rs, :] * o_ref[rs, :], axis=-1, keepdims=True), (t, HEAD_DIM))
                return carry

            lax.fori_loop(0, nq, fill, 0)

        kb, vb = k_ref[...], v_ref[...]

        def step(i, carry, masked):
            dk, dv = carry
            rs = pl.ds(pl.multiple_of(i * t, t), t)
            qb, dob = q_ref[rs, :], do_ref[rs, :].astype(BF16)
            p = jnp.exp2(_dot(qb, kb, NT) - lse_ref[rs, 0:1])
            if masked:
                p = jnp.where(_diag_mask(t), p, 0.0)
            dp = _dot(dob, vb, NT)
            ds = (p * (dp - delta_ref[rs, 0:1]) * LN2).astype(BF16)
            dq_ref[rs, :] += _dot(ds, kb)
            return dk + _dot(ds, qb, TN), dv + _dot(p.astype(BF16), dob, TN)

        carry = step(j, (jnp.zeros((t, QK_PAD), F32), jnp.zeros((t, HEAD_DIM), F32)), True)
        dk, dv = lax.fori_loop(j + 1, nq, lambda i, c: step(i, c, False), carry)
        dk_ref[...] = dk
        dv_ref[...] = dv

    return _call(
        body, "attn_bwd", (q, k, v, dmixcat, o, lse), grid=(HEADS, nq),
        in_specs=[pl.BlockSpec((None, T, QK_PAD), lambda h, j: (h, 0, 0)),
                  pl.BlockSpec((None, t, QK_PAD), lambda h, j: (h, j, 0)),
                  pl.BlockSpec((None, t, HEAD_DIM), lambda h, j: (h, j, 0)),
                  pl.BlockSpec((T, HEAD_DIM), lambda h, j: (0, HEADS + h)),
                  pl.BlockSpec((T, HEAD_DIM), lambda h, j: (0, h)),
                  pl.BlockSpec((None, T, HEAD_DIM), lambda h, j: (h, 0, 0))],
        out_specs=[pl.BlockSpec((None, T, QK_PAD), lambda h, j: (h, 0, 0)),
                   pl.BlockSpec((None, t, QK_PAD), lambda h, j: (h, j, 0)),
                   pl.BlockSpec((None, t, HEAD_DIM), lambda h, j: (h, j, 0))],
        out_shape=[jax.ShapeDtypeStruct((HEADS, T, QK_PAD), F32), jax.ShapeDtypeStruct((HEADS, T, QK_PAD), F32),
                   jax.ShapeDtypeStruct((HEADS, T, HEAD_DIM), F32)],
        scratch_shapes=[pltpu.VMEM((T, HEAD_DIM), F32)], exchange=exchange)


def _ln_fwd(r):
    mu = _lanemean(r)
    xc = r - mu
    rstd = lax.rsqrt(_lanemean(xc * xc) + LN_EPS)
    return xc * rstd, rstd


def _ln_bwd(dxh, xhat, rstd):
    return rstd * (dxh - _lanemean(dxh) - xhat * _lanemean(dxh * xhat))


def _mix_ln1(mixcat, w_out, x, g_a, ln1_g, ln1_b, sc_m, sh_m, exchange=None):
    T = x.shape[0]
    tm = min(ROW_TILE_SMALL, T)

    def body(mc_ref, w_ref, x_ref, ga_ref, g_ref, b_ref, sc_ref, sh_ref, mix_ref, xhat_ref, rstd_ref, u2_ref):
        mix = _dot(mc_ref[...], w_ref[...])
        mix_ref[...] = mix
        xhat, rstd = _ln_fwd(ALPHA * x_ref[...] + (1.0 + ga_ref[...]) * mix)
        xhat_ref[...] = xhat
        rstd_ref[...] = jnp.broadcast_to(rstd, (tm, 128))
        u2_ref[...] = _modulate(xhat * g_ref[...] + b_ref[...], sc_ref[...], sh_ref[...]).astype(BF16)

    row = pl.BlockSpec((tm, D_MODEL), lambda i: (i, 0))
    vec = _full((1, D_MODEL))
    return _call(
        body, "mix_ln1", (mixcat, w_out, x, g_a, ln1_g, ln1_b, sc_m, sh_m), grid=(T // tm,),
        in_specs=[row, _full(w_out.shape), row, vec, vec, vec, vec, vec],
        out_specs=[row, row, pl.BlockSpec((tm, 128), lambda i: (i, 0)), row],
        out_shape=[jax.ShapeDtypeStruct((T, D_MODEL), F32), jax.ShapeDtypeStruct((T, D_MODEL), F32),
                   jax.ShapeDtypeStruct((T, 128), F32), jax.ShapeDtypeStruct((T, D_MODEL), BF16)],
        exchange=exchange)


def _mlp_fwd(u2, w1, w2, xhat1, ln1_g, ln1_b, g_m, ln2_g, ln2_b, target):
    T = u2.shape[0]
    nf, tf = N_DEV, w1.shape[-1]
    tm = min(ROW_TILE, T)

    def body(u2_ref, w1_ref, w2_ref, xhat_ref, g1_ref, b1_ref, gm_ref, g2_ref, b2_ref, tgt_ref,
             r_ref, dr2_ref, dh_ref, dg2_ref, db2_ref, dgm_ref, loss_ref, acc_ref):
        i, f = pl.program_id(0), pl.program_id(1)

        @pl.when((i == 0) & (f == 0))
        def _():
            for ref in (dg2_ref, db2_ref, dgm_ref, loss_ref):
                ref[...] = jnp.zeros_like(ref)

        @pl.when(f == 0)
        def _():
            acc_ref[...] = jnp.zeros_like(acc_ref)

        r = jnp.maximum(_dot(u2_ref[...], w1_ref[...]), 0.0)
        r_ref[...] = r.astype(BF16)
        acc_ref[...] += _bdot(r * r, w2_ref[...])

        @pl.when(f == nf - 1)
        def _():
            h = acc_ref[...]
            x1 = xhat_ref[...] * g1_ref[...] + b1_ref[...]
            xhat2, rstd2 = _ln_fwd(ALPHA * x1 + (1.0 + gm_ref[...]) * h)
            err = xhat2 * g2_ref[...] + b2_ref[...] - tgt_ref[...]
            loss_ref[...] += jnp.sum(0.5 * _lanemean(err * err), axis=0, keepdims=True)
            dy = err * (1.0 / D_MODEL)
            dg2_ref[...] += _rowsum(dy * xhat2)
            db2_ref[...] += _rowsum(dy)
            dr2 = _ln_bwd(dy * g2_ref[...], xhat2, rstd2)
            dr2_ref[...] = dr2
            dgm_ref[...] += _rowsum(dr2 * h)
            dh_ref[...] = ((1.0 + gm_ref[...]) * dr2).astype(BF16)

    row = pl.BlockSpec((tm, D_MODEL), lambda i, f: (i, 0))
    vec = _full((1, D_MODEL))
    return pl.pallas_call(
        body, name="mlp_fwd", grid=(T // tm, nf),
        in_specs=[row, pl.BlockSpec((None, D_MODEL, tf), lambda i, f: (f, 0, 0)),
                  pl.BlockSpec((None, tf, D_MODEL), lambda i, f: (f, 0, 0)),
                  row, vec, vec, vec, vec, vec, row],
        out_specs=[pl.BlockSpec((tm, tf), lambda i, f: (i, f)), row, row, vec, vec, vec, _full((1, 128))],
        out_shape=[jax.ShapeDtypeStruct((T, nf * tf), BF16), jax.ShapeDtypeStruct((T, D_MODEL), F32),
                   jax.ShapeDtypeStruct((T, D_MODEL), BF16), jax.ShapeDtypeStruct((1, D_MODEL), F32),
                   jax.ShapeDtypeStruct((1, D_MODEL), F32), jax.ShapeDtypeStruct((1, D_MODEL), F32),
                   jax.ShapeDtypeStruct((1, 128), F32)],
        scratch_shapes=[pltpu.VMEM((tm, D_MODEL), F32)],
        compiler_params=_params(),
    )(u2, w1, w2, xhat1, ln1_g, ln1_b, g_m, ln2_g, ln2_b, target)


def _mlp_bwd(dh, w1, w2, r, dr2, xhat1, rstd1, mix, ln1_g, ln1_b, sc_m, g_a):
    T = dh.shape[0]
    nf, tf = N_DEV, w1.shape[-1]
    tm = min(ROW_TILE, T)

    def body(dh_ref, w1_ref, w2_ref, r_ref, dr2_ref, xhat_ref, rstd_ref, mix_ref, g1_ref, b1_ref, sc_ref, ga_ref,
             dhpre_ref, dr1_ref, dmix_ref, dsc_ref, dsh_ref, dg1_ref, db1_ref, dga_ref, acc_ref):
        i, f = pl.program_id(0), pl.program_id(1)

        @pl.when((i == 0) & (f == 0))
        def _():
            for ref in (dsc_ref, dsh_ref, dg1_ref, db1_ref, dga_ref):
                ref[...] = jnp.zeros_like(ref)

        @pl.when(f == 0)
        def _():
            acc_ref[...] = jnp.zeros_like(acc_ref)

        dhpre = (_dot(dh_ref[...], w2_ref[...], NT) * (2.0 * r_ref[...].astype(F32))).astype(BF16)
        dhpre_ref[...] = dhpre
        acc_ref[...] += _dot(dhpre, w1_ref[...], NT)

        @pl.when(f == nf - 1)
        def _():
            du2 = acc_ref[...]
            xhat = xhat_ref[...]
            x1 = xhat * g1_ref[...] + b1_ref[...]
            dx1 = ALPHA * dr2_ref[...] + du2 * (1.0 + sc_ref[...])
            dsc_ref[...] += _rowsum(du2 * x1)
            dsh_ref[...] += _rowsum(du2)
            dg1_ref[...] += _rowsum(dx1 * xhat)
            db1_ref[...] += _rowsum(dx1)
            dr1 = _ln_bwd(dx1 * g1_ref[...], xhat, rstd_ref[:, 0:1])
            dr1_ref[...] = dr1
            dga_ref[...] += _rowsum(dr1 * mix_ref[...])
            dmix_ref[...] = ((1.0 + ga_ref[...]) * dr1).astype(BF16)

    row = pl.BlockSpec((tm, D_MODEL), lambda i, f: (i, 0))
    vec = _full((1, D_MODEL))
    return pl.pallas_call(
        body, name="mlp_bwd", grid=(T // tm, nf),
        in_specs=[row, pl.BlockSpec((None, D_MODEL, tf), lambda i, f: (f, 0, 0)),
                  pl.BlockSpec((None, tf, D_MODEL), lambda i, f: (f, 0, 0)),
                  pl.BlockSpec((tm, tf), lambda i, f: (i, f)), row, row, pl.BlockSpec((tm, 128), lambda i, f: (i, 0)),
                  row, vec, vec, vec, vec],
        out_specs=[pl.BlockSpec((tm, tf), lambda i, f: (i, f)), row, row, vec, vec, vec, vec, vec],
        out_shape=[jax.ShapeDtypeStruct((T, nf * tf), BF16), jax.ShapeDtypeStruct((T, D_MODEL), F32),
                   jax.ShapeDtypeStruct((T, D_MODEL), BF16)] + [jax.ShapeDtypeStruct((1, D_MODEL), F32)] * 5,
        scratch_shapes=[pltpu.VMEM((tm, D_MODEL), F32)],
        compiler_params=_params(),
    )(dh, w1, w2, r, dr2, xhat1, rstd1, mix, ln1_g, ln1_b, sc_m, g_a)


def _input_bwd(dz_h, dz_m, w_in_ext, x, dr1, sc_a, exchange=None):
    T = x.shape[0]
    tm = min(ROW_TILE_SMALL, T)

    def body(dzh_ref, dzm_ref, w_ref, x_ref, dr1_ref, sc_ref, gx_ref, dsc_ref, dsh_ref):
        @pl.when(pl.program_id(0) == 0)
        def _():
            dsc_ref[...] = jnp.zeros_like(dsc_ref)
            dsh_ref[...] = jnp.zeros_like(dsh_ref)

        du = _bdot(dzh_ref[...], w_ref[:, 0:2048], NT) + _bdot(dzm_ref[...], w_ref[:, 2048:3072], NT)
        gx_ref[...] = ALPHA * dr1_ref[...] + du * (1.0 + sc_ref[...])
        dsc_ref[...] += _rowsum(du * x_ref[...])
        dsh_ref[...] += _rowsum(du)

    row = pl.BlockSpec((tm, D_MODEL), lambda i: (i, 0))
    vec = _full((1, D_MODEL))
    return _call(
        body, "input_bwd", (dz_h, dz_m, w_in_ext, x, dr1, sc_a), grid=(T // tm,),
        in_specs=[pl.BlockSpec((tm, 2048), lambda i: (i, 0)), row, _full(w_in_ext.shape), row, row, vec],
        out_specs=[row, vec, vec],
        out_shape=[jax.ShapeDtypeStruct((T, D_MODEL), F32), jax.ShapeDtypeStruct((1, D_MODEL), F32),
                   jax.ShapeDtypeStruct((1, D_MODEL), F32)], exchange=exchange)


def _adam_math(w, g, m, v):
    m = ADAM_B1 * m + (1.0 - ADAM_B1) * g
    v = ADAM_B2 * v + (1.0 - ADAM_B2) * (g * g)
    m_hat = m / (1.0 - ADAM_B1 ** ADAM_STEP)
    v_hat = v / (1.0 - ADAM_B2 ** ADAM_STEP)
    return -ADAM_LR * (m_hat / (jnp.sqrt(v_hat) + ADAM_EPS) + ADAM_WD * w), m, v


def _adam(g_slabs, w, m, v, name, g_fn=None, g_extra=()):
    R, C = w.shape
    tr = R if R <= 256 else 256
    assert R % tr == 0
    ns = 0 if g_slabs is None else g_slabs.shape[0]
    ne = len(g_extra)

    def body(*refs):
        e_refs = refs[:ne]
        refs = refs[ne:]
        if ns:
            gs_ref, refs = refs[0], refs[1:]
        w_ref, m_ref, v_ref, g_ref, d_ref, nm_ref, nv_ref = refs
        if g_fn is not None:
            g = g_fn(*e_refs)
        else:
            g = gs_ref[0].astype(F32)
            for s in range(1, ns):
                g = g + gs_ref[s].astype(F32)
        d, nm, nv = _adam_math(w_ref[...], g, m_ref[...], v_ref[...])
        g_ref[...] = g
        d_ref[...] = d
        nm_ref[...] = nm
        nv_ref[...] = nv

    blk = pl.BlockSpec((tr, C), lambda i: (i, 0))
    in_specs = [pl.BlockSpec((tr, e.shape[1]), lambda i: (i, 0)) if e.shape[0] == R else _full(e.shape) for e in g_extra]
    args = list(g_extra)
    if ns:
        in_specs.append(pl.BlockSpec((ns, tr, C), lambda i: (0, i, 0)))
        args.append(g_slabs)
    return pl.pallas_call(
        body, name=name, grid=(R // tr,), in_specs=in_specs + [blk] * 3, out_specs=[blk] * 4,
        out_shape=[jax.ShapeDtypeStruct((R, C), F32)] * 4, compiler_params=_params(),
    )(*args, w, m, v)


def _adam_small(small_all, params):
    n = len(params)

    def body(*refs):
        s_ref, refs = refs[0], refs[1:]
        wmv, loss_ref, outs = refs[:3 * n], refs[3 * n], refs[3 * n + 1:]
        tot = s_ref[0]
        for i in range(1, N_DEV):
            tot = tot + s_ref[i]
        loss_ref[...] = tot[:, SMALL_W - 128:]
        for j, (w, _, _, off) in enumerate(params):
            w_ref, m_ref, v_ref = wmv[3 * j:3 * j + 3]
            g_ref, d_ref, nm_ref, nv_ref = outs[4 * j:4 * j + 4]
            if w.shape[0] == 2:
                lb = _lower_bound(w_ref)
                g0 = tot[:, off:off + w.shape[1]] * lb * (1.0 - lb)
                rows = [(slice(0, 1), g0), (slice(1, 2), -g0)]
            else:
                rows = [(slice(0, 1), tot[:, off:off + w.shape[1]])]
            for rs, g in rows:
                d, nm, nv = _adam_math(w_ref[rs, :], g, m_ref[rs, :], v_ref[rs, :])
                g_ref[rs, :], d_ref[rs, :], nm_ref[rs, :], nv_ref[rs, :] = g, d, nm, nv

    out_shape = [jax.ShapeDtypeStruct((1, 128), F32)]
    for w, _, _, _ in params:
        out_shape += [jax.ShapeDtypeStruct(w.shape, F32)] * 4
    res = pl.pallas_call(body, name="adam_small", out_shape=out_shape, compiler_params=_params())(
        small_all, *[a for w, m, v, _ in params for a in (w, m, v)])
    return res[0], [tuple(res[1 + 4 * j:5 + 4 * j]) for j in range(n)]


def _rot_half_cols(w):
    return jnp.concatenate([-w[..., 32:], w[..., :32]], axis=-1)


def _unrot_half_cols(dw_rot):
    return jnp.concatenate([dw_rot[..., 32:], -dw_rot[..., :32]], axis=-1)


def _cols_from_slabs(g):
    s, r, c = g.shape
    return jnp.transpose(g, (1, 0, 2)).reshape(r, s * c)


def _slabs_from_cols(w):
    r, c = w.shape
    return jnp.transpose(w.reshape(r, N_DEV, c // N_DEV), (1, 0, 2))


def _ext_in(w_in):
    k_in = w_in.shape[0]
    z64, z128 = jnp.zeros((k_in, 64), BF16), jnp.zeros((k_in, 128), BF16)
    wk = w_in[:, 2560:2624]
    return jnp.concatenate([w_in[:, :2560], z128, wk, z64, z128, _rot_half_cols(wk), z64], axis=1)


def _ext_q(w_q_up):
    r = w_q_up.shape[0]
    z64, z128 = jnp.zeros((r, 64), BF16), jnp.zeros((r, 128), BF16)
    wq = w_q_up.reshape(r, HEADS, HEAD_DIM + ROPE_DIM)
    main = [jnp.concatenate([wq[:, h, :HEAD_DIM], wq[:, h, HEAD_DIM:], z64], axis=1) for h in range(HEADS)]
    rot = [jnp.concatenate([z128, _rot_half_cols(wq[:, h, HEAD_DIM:]), z64], axis=1) for h in range(HEADS)]
    return jnp.concatenate(main + rot, axis=1)


def _ext_kv(w_kv_up):
    r = w_kv_up.shape[0]
    z128 = jnp.zeros((r, 128), BF16)
    wkv = w_kv_up.reshape(r, HEADS, 2 * HEAD_DIM)
    kpad = [jnp.concatenate([wkv[:, h, :HEAD_DIM], z128], axis=1) for h in range(HEADS)]
    vals = [wkv[:, h, HEAD_DIM:] for h in range(HEADS)]
    return jnp.concatenate(kpad + vals, axis=1)


def _grad_in_from_ext(dw_in_h, dw_in_m):
    dwk = dw_in_m[:, 512 + 128:512 + 192] + _unrot_half_cols(dw_in_m[:, 768 + 128:768 + 192])
    return jnp.concatenate([dw_in_h, dw_in_m[:, :512], dwk], axis=1)


def _grads_qkv_from_ext(dwq_ext, dwkv_ext):
    qcols = []
    for h in range(HEADS):
        main, rot = dwq_ext[:, 256 * h:256 * h + 256], dwq_ext[:, 1024 + 256 * h:1280 + 256 * h]
        qcols += [main[:, :128], main[:, 128:192] + _unrot_half_cols(rot[:, 128:192])]
    kvcols = []
    for h in range(HEADS):
        kvcols += [dwkv_ext[:, 256 * h:256 * h + 128], dwkv_ext[:, 1024 + 128 * h:1152 + 128 * h]]
    return jnp.concatenate(qcols, axis=1), jnp.concatenate(kvcols, axis=1)


SMALL_W = 6144 + 512 + 512 + 256 + 256 + 4 * 1024 + 128


def kernel(x, c, positions, w_ada, b_ada, w_in, hg_lower_bounds, hg_norm_w, mla_q_norm_w, w_q_up, mla_kv_norm_w, w_kv_up, w_out, ln1_g, ln1_b, w_mlp_in, w_mlp_out, ln2_g, ln2_b, loss_target, m_w_ada, m_b_ada, m_w_in, m_hg_lower_bounds, m_hg_norm_w, m_mla_q_norm_w, m_w_q_up, m_mla_kv_norm_w, m_w_kv_up, m_w_out, m_ln1_g, m_ln1_b, m_w_mlp_in, m_w_mlp_out, m_ln2_g, m_ln2_b, v_w_ada, v_b_ada, v_w_in, v_hg_lower_bounds, v_hg_norm_w, v_mla_q_norm_w, v_w_q_up, v_mla_kv_norm_w, v_w_kv_up, v_w_out, v_ln1_g, v_ln1_b, v_w_mlp_in, v_w_mlp_out, v_ln2_g, v_ln2_b):
    T = x.shape[1]
    me = 4 * lax.axis_index("x") + 2 * lax.axis_index("y") + lax.axis_index("c")
    xs, tgt = x[0], loss_target[0]
    big = dict(w_in=w_in[0], w_q_up=w_q_up[0], w_kv_up=w_kv_up[0], w_out=w_out[0], w_mlp_in=w_mlp_in[0],
               w_mlp_out=w_mlp_out[0])
    names = list(big)

    bf = {n: big[n].astype(BF16) for n in names}
    g_in, g_c = _gather_two_level([bf["w_in"], c], name="gather_w_in")
    c_all = g_c.reshape(N_DEV, D_MODEL)

    ada_cols = w_ada.shape[2]
    mod_part, cond = _mod_part(c_all, w_ada[0], lax.dynamic_slice(b_ada, (0, me * ada_cols), (1, ada_cols)))
    (mod_all,) = _exchange([mod_part], scatter=False, name="gather_mod")
    mod_row = lax.dynamic_slice(mod_all, (0, me, 0), (N_DEV, 1, ada_cols)).reshape(1, N_DEV * ada_cols)
    sh_a, sc_a, g_a, sh_m, sc_m, g_m = [mod_row[:, D_MODEL * i:D_MODEL * (i + 1)] for i in range(6)]

    w_in_ext = _ext_in(_cols_from_slabs(g_in))
    z, (g_q, g_kv, g_out) = _matmul(xs, w_in_ext, "NN", "in_proj", a_fn=_modulate, extras=(sc_a, sh_a),
                                    exchange=_Exchange([bf["w_q_up"], bf["w_kv_up"], bf["w_out"]], False))
    wq_ext, wkv_ext = _ext_q(_cols_from_slabs(g_q)), _ext_kv(_cols_from_slabs(g_kv))
    w_out_full = g_out.reshape(D_MODEL, D_MODEL)
    inv_freq = 1.0 / (ROPE_THETA ** (jnp.arange(0, ROPE_DIM, 2, dtype=F32) / ROPE_DIM))
    zeros = lambda n: jnp.zeros((n,), F32)
    invf = jnp.concatenate([zeros(128), inv_freq, inv_freq, zeros(64)]).reshape(1, QK_PAD)
    m_one = jnp.concatenate([jnp.ones((128,), F32), zeros(128)]).reshape(1, QK_PAD)
    m_rot = jnp.concatenate([zeros(128), jnp.ones((64,), F32), zeros(64)]).reshape(1, QK_PAD)
    q, k, v, c1, s1, cqn, ckvn = _mla_pre(z, positions.reshape(T, 1), invf, m_one, m_rot, wq_ext, wkv_ext,
                                          mla_q_norm_w, mla_kv_norm_w)
    (o_raw, o_gated, s_prev), (w1,) = _hgrn_fwd(z, hg_lower_bounds, hg_norm_w,
                                                exchange=_StagedGather(bf["w_mlp_in"]))
    (o_mla, lse), (w2,) = _attn_fwd(q, k, v, exchange=_StagedGather(bf["w_mlp_out"]))
    mixcat = jnp.concatenate([o_gated, o_mla.astype(BF16)], axis=1)
    mix, xhat1, rstd1, u2 = _mix_ln1(mixcat, w_out_full, xs, g_a, ln1_g, ln1_b, sc_m, sh_m)[0]
    r, dr2, dh, dln2_g, dln2_b, dg_m, loss_part = _mlp_fwd(u2, w1, w2, xhat1, ln1_g, ln1_b, g_m, ln2_g, ln2_b, tgt)

    dhpre, dr1, dmix, dsc_m, dsh_m, dln1_g, dln1_b, dg_a = _mlp_bwd(dh, w1, w2, r, dr2, xhat1, rstd1, mix, ln1_g,
                                                                    ln1_b, sc_m, g_a)
    received = {}
    dw2 = _matmul(r, dh, "TN", "wgrad_mlp_out", out_dtype=BF16, a_fn=_square, tm=1024)
    dw1 = _matmul(u2, dhpre, "TN", "wgrad_mlp_in", out_dtype=BF16, tm=1024, out_slabs=N_DEV)
    dmixcat = _matmul(dmix, w_out_full, "NT", "dgrad_out")
    dw_out = _matmul(mixcat, dmix, "TN", "wgrad_out", out_dtype=BF16, tm=1024)
    (dz_h, dlb, dnw), (received["w_out"],) = _hgrn_bwd(
        dmixcat, z, o_raw, s_prev, hg_lower_bounds, hg_norm_w,
        exchange=_Exchange([dw_out.reshape(N_DEV, D_MODEL // N_DEV, D_MODEL)], True))
    (dq, dk, dv), (received["w_mlp_in"], received["w_mlp_out"]) = _attn_bwd(
        q, k, v, dmixcat, o_mla, lse,
        exchange=_Exchange([dw1, dw2.reshape(N_DEV, dw2.shape[0] // N_DEV, D_MODEL)], True))
    dz_m, dq_ext, dkv_ext, dqnw, dkvnw = _mla_bwd(dq, dk, dv, z, c1, s1, wq_ext, wkv_ext, mla_q_norm_w,
                                                   mla_kv_norm_w)
    dwq_ext = _matmul(cqn, dq_ext, "TN", "wgrad_q_up", tn=2048)
    dwkv_ext = _matmul(ckvn, dkv_ext, "TN", "wgrad_kv_up", tn=1536)
    dwq, dwkv = _grads_qkv_from_ext(dwq_ext, dwkv_ext)
    dw_in_h, (received["w_q_up"], received["w_kv_up"]) = _matmul(
        xs, dz_h, "TN", "wgrad_in_h", a_fn=_modulate, extras=(sc_a, sh_a), tm=1024,
        exchange=_Exchange([_slabs_from_cols(g).astype(BF16) for g in (dwq, dwkv)], True))
    dw_in_m = _matmul(xs, dz_m, "TN", "wgrad_in_m", a_fn=_modulate, extras=(sc_a, sh_a), tm=1024)
    dw_in = _grad_in_from_ext(dw_in_h, dw_in_m)
    (grad_x, dsc_a, dsh_a), (received["w_in"],) = _input_bwd(
        dz_h, dz_m, w_in_ext, xs, dr1, sc_a, exchange=_Exchange([_slabs_from_cols(dw_in).astype(BF16)], True))

    small = jnp.concatenate([dsh_a, dsc_a, dg_a, dsh_m, dsc_m, dg_m, dlb, dnw, dqnw, dkvnw, dln1_g, dln1_b, dln2_g,
                             dln2_b, loss_part], axis=1)
    (small_all,) = _exchange([small], scatter=False, name="gather_small")

    moments = dict(w_in=(m_w_in, v_w_in), w_q_up=(m_w_q_up, v_w_q_up), w_kv_up=(m_w_kv_up, v_w_kv_up),
                   w_out=(m_w_out, v_w_out), w_mlp_in=(m_w_mlp_in, v_w_mlp_in), w_mlp_out=(m_w_mlp_out, v_w_mlp_out))
    res = {}
    for n in names:
        res[n] = _adam(received[n], big[n], moments[n][0][0], moments[n][1][0], name="adam_" + n)
    dmod_cols = lax.dynamic_slice(small_all.reshape(N_DEV, SMALL_W), (0, me * ada_cols), (N_DEV, ada_cols))
    cond_t = cond.T

    def ada_grad(ct_ref, dm_ref):
        g = ct_ref[:, 0:1] * dm_ref[0:1, :]
        for b in range(1, N_DEV):
            g = g + ct_ref[:, b:b + 1] * dm_ref[b:b + 1, :]
        return g

    res["w_ada"] = _adam(None, w_ada[0], m_w_ada[0], v_w_ada[0], name="adam_w_ada", g_fn=ada_grad,
                         g_extra=(cond_t, dmod_cols))

    small_params = [("b_ada", b_ada, m_b_ada, v_b_ada, 0),
                    ("hg_lower_bounds", hg_lower_bounds, m_hg_lower_bounds, v_hg_lower_bounds, 6144),
                    ("hg_norm_w", hg_norm_w, m_hg_norm_w, v_hg_norm_w, 6656),
                    ("mla_q_norm_w", mla_q_norm_w, m_mla_q_norm_w, v_mla_q_norm_w, 7168),
                    ("mla_kv_norm_w", mla_kv_norm_w, m_mla_kv_norm_w, v_mla_kv_norm_w, 7424),
                    ("ln1_g", ln1_g, m_ln1_g, v_ln1_g, 7680), ("ln1_b", ln1_b, m_ln1_b, v_ln1_b, 8704),
                    ("ln2_g", ln2_g, m_ln2_g, v_ln2_g, 9728), ("ln2_b", ln2_b, m_ln2_b, v_ln2_b, 10752)]
    loss_row, small_res = _adam_small(small_all, [p[1:] for p in small_params])
    for p, r4 in zip(small_params, small_res):
        res[p[0]] = r4
    loss = loss_row[0, 0]

    order = ["w_ada", "b_ada", "w_in", "hg_lower_bounds", "hg_norm_w", "mla_q_norm_w", "w_q_up", "mla_kv_norm_w",
             "w_kv_up", "w_out", "ln1_g", "ln1_b", "w_mlp_in", "w_mlp_out", "ln2_g", "ln2_b"]
    shaped = {n: tuple(a.reshape((1,) + a.shape) if n in big or n == "w_ada" else a for a in res[n]) for n in order}
    outs = [loss, grad_x.reshape(1, T, D_MODEL)]
    for i in range(4):
        outs += [shaped[n][i] for n in order]
    return tuple(outs)
```

```python
import functools

import jax
import jax.numpy as jnp
import numpy as np
from jax import lax
from jax.experimental import pallas as pl
from jax.experimental.pallas import tpu as pltpu

F32, BF16 = jnp.float32, jnp.bfloat16
N_DEV = 8
D_MODEL = 1024
HEADS = 4
HEAD_DIM = 128
ROPE_DIM = 64
QK_PAD = 256
CHUNK = 64
ROPE_THETA = 10000.0
RMS_EPS = 1e-6
LN_EPS = 1e-5
ALPHA = 2.0 ** 0.25
ATT_SCALE = (HEAD_DIM + ROPE_DIM) ** -0.5
LN2 = float(np.log(2.0))
Q_PRESCALE = ATT_SCALE / LN2
ADAM_LR, ADAM_B1, ADAM_B2, ADAM_EPS, ADAM_WD, ADAM_STEP = 0.001, 0.9, 0.999, 1e-08, 0.01, 10
NEG_BIG = -1e30

ROW_TILE = 512
ROW_TILE_SMALL = 256
ATT_TILE = 512
HGRN_GROUP = 8
MLP_SLABS = 2
VMEM_LIMIT = 56 * 2 ** 20

NN = (((1,), (0,)), ((), ()))
NT = (((1,), (1,)), ((), ()))
TN = (((0,), (0,)), ((), ()))


def _dot(a, b, dims=NN):
    return lax.dot_general(a, b, dims, preferred_element_type=F32)


def _bdot(a, b, dims=NN):
    return lax.dot_general(a.astype(BF16), b.astype(BF16), dims, preferred_element_type=F32)


def _hdot(a, b, dims=NN):
    return lax.dot_general(a, b, dims, precision=lax.Precision.HIGHEST, preferred_element_type=F32)


def _params():
    return pltpu.CompilerParams(vmem_limit_bytes=VMEM_LIMIT)


def _sigmoid(x):
    return 1.0 / (1.0 + jnp.exp(-x))


def _rowsum(x):
    return jnp.sum(x, axis=0, keepdims=True)


def _lanemean(x):
    return jnp.mean(x, axis=-1, keepdims=True)


def _full(shape):
    nd = len(shape)
    return pl.BlockSpec(shape, lambda *_: (0,) * nd)


class _Exchange:
    def __init__(self, arrs, scatter):
        self.arrs, self.scatter, self.n = list(arrs), scatter, len(arrs)
        self.out_shape = [jax.ShapeDtypeStruct((N_DEV,) + (a.shape[1:] if scatter else a.shape), a.dtype)
                          for a in self.arrs]
        n = self.n
        self.scratch = [pltpu.SemaphoreType.DMA((n, N_DEV - 1)), pltpu.SemaphoreType.DMA((n, N_DEV - 1)),
                        pltpu.SemaphoreType.DMA((n,))]

    def _copies(self, ins, outs, sems):
        send_sems, recv_sems, loc_sems = sems
        x, y, c = lax.axis_index("x"), lax.axis_index("y"), lax.axis_index("c")
        me = 4 * x + 2 * y + c
        copies = []
        for k in range(self.n):
            src_of = (lambda i, k=k: ins[k].at[i]) if self.scatter else (lambda i, k=k: ins[k])
            copies.append((pltpu.make_async_copy(src_of(me), outs[k].at[me], loc_sems.at[k]), None))
            for p in range(1, N_DEV):
                px = (1 - x) if p & 4 else x
                py = (1 - y) if p & 2 else y
                pc = (1 - c) if p & 1 else c
                peer = 4 * px + 2 * py + pc
                both = dict(send_sem=send_sems.at[k, p - 1], recv_sem=recv_sems.at[k, p - 1],
                            device_id=(px, py, pc), device_id_type=pl.DeviceIdType.MESH)
                send = pltpu.make_async_remote_copy(src_ref=src_of(peer), dst_ref=outs[k].at[me], **both)
                recv = pltpu.make_async_remote_copy(src_ref=src_of(peer), dst_ref=outs[k].at[peer], **both)
                copies.append((send, recv))
        return copies

    def start(self, ins, outs, sems):
        for first, _ in self._copies(ins, outs, sems):
            first.start()

    def middle(self, ins, outs, sems):
        pass

    def wait(self, ins, outs, sems):
        for first, recv in self._copies(ins, outs, sems):
            if recv is None:
                first.wait()
            else:
                recv.wait_recv()
                first.wait_send()


class _StagedGather:
    def __init__(self, arr):
        self.arrs, self.n = [arr], 1
        self.out_shape = [jax.ShapeDtypeStruct((N_DEV,) + arr.shape, arr.dtype)]
        self.scratch = [pltpu.VMEM((N_DEV,) + arr.shape, arr.dtype), pltpu.SemaphoreType.DMA((7,)),
                        pltpu.SemaphoreType.DMA((7,)), pltpu.SemaphoreType.DMA((2,))]

    def _parts(self, scr):
        stage, send_sems, recv_sems, loc_sems = scr
        x, y, c = lax.axis_index("x"), lax.axis_index("y"), lax.axis_index("c")
        me, sibling = (x, y, c), (x, y, 1 - c)
        chips = [(1 - x, y), (x, 1 - y), (1 - x, 1 - y)]

        def copy(j, block, to):
            px, py, pc = block
            slot = stage.at[4 * px + 2 * py + pc]
            return pltpu.make_async_remote_copy(src_ref=slot, dst_ref=slot, send_sem=send_sems.at[j],
                                                recv_sem=recv_sems.at[j], device_id=to,
                                                device_id_type=pl.DeviceIdType.MESH)

        return stage, loc_sems, me, sibling, chips, c, copy

    def start(self, ins, outs, scr):
        stage, loc_sems, me, sibling, chips, c, copy = self._parts(scr)
        x, y, _ = me
        own = pltpu.make_async_copy(ins[0], stage.at[4 * x + 2 * y + c], loc_sems.at[0])
        own.start()
        own.wait()
        copy(0, me, sibling).start()
        for j, chip in enumerate(chips):
            copy(1 + j, me, (*chip, c)).start()

    def middle(self, ins, outs, scr):
        stage, loc_sems, me, sibling, chips, c, copy = self._parts(scr)
        for j, chip in enumerate(chips):
            copy(1 + j, (*chip, c), me).wait_recv()
            copy(4 + j, (*chip, c), sibling).start()

    def wait(self, ins, outs, scr):
        stage, loc_sems, me, sibling, chips, c, copy = self._parts(scr)
        copy(0, sibling, me).wait_recv()
        for j, chip in enumerate(chips):
            copy(4 + j, (*chip, 1 - c), me).wait_recv()
        copy(0, me, sibling).wait_send()
        for j, chip in enumerate(chips):
            copy(1 + j, me, (*chip, c)).wait_send()
            copy(4 + j, (*chip, c), sibling).wait_send()
        whole = pltpu.make_async_copy(stage, outs[0], loc_sems.at[1])
        whole.start()
        whole.wait()


def _call(body, name, args, out_shape, grid=(), in_specs=(), out_specs=(), scratch_shapes=(), exchange=None,
          middle_at=0.8):
    if exchange is None:
        return pl.pallas_call(body, name=name, grid=grid, in_specs=list(in_specs), out_specs=list(out_specs),
                              out_shape=list(out_shape), scratch_shapes=list(scratch_shapes),
                              compiler_params=_params())(*args), None
    exs = list(exchange) if isinstance(exchange, (list, tuple)) else [exchange]
    ni, no, ns, nx = len(args), len(out_shape), len(scratch_shapes), sum(e.n for e in exs)
    steps = int(np.prod(grid))
    mid_step = min(max(int(steps * middle_at), 1), steps - 1)

    def wrapped(*refs):
        a, xi = refs[:ni], refs[ni:ni + nx]
        o, xo = refs[ni + nx:ni + nx + no], refs[ni + nx + no:ni + 2 * nx + no]
        s, xs = refs[ni + 2 * nx + no:ni + 2 * nx + no + ns], refs[ni + 2 * nx + no + ns:]
        parts, at, sat = [], 0, 0
        for e in exs:
            parts.append((e, xi[at:at + e.n], xo[at:at + e.n], xs[sat:sat + len(e.scratch)]))
            at, sat = at + e.n, sat + len(e.scratch)
        step = 0
        for d, g in enumerate(grid):
            step = step * g + pl.program_id(d)

        @pl.when(step == 0)
        def _():
            for e, ins, outs, sems in parts:
                e.start(ins, outs, sems)

        @pl.when(step == mid_step)
        def _():
            for e, ins, outs, sems in parts:
                e.middle(ins, outs, sems)

        body(*a, *o, *s)

        @pl.when(step == steps - 1)
        def _():
            for e, ins, outs, sems in parts:
                e.wait(ins, outs, sems)

    hbm = pl.BlockSpec(memory_space=pltpu.HBM)
    res = pl.pallas_call(
        wrapped, name=name, grid=grid, in_specs=list(in_specs) + [hbm] * nx, out_specs=list(out_specs) + [hbm] * nx,
        out_shape=list(out_shape) + [o_ for e in exs for o_ in e.out_shape],
        scratch_shapes=list(scratch_shapes) + [s_ for e in exs for s_ in e.scratch],
        compiler_params=_params())(*args, *[a_ for e in exs for a_ in e.arrs])
    return res[:no], res[no:]


def _gather_two_level(arrs, name):
    n = len(arrs)
    out_shape = [jax.ShapeDtypeStruct((N_DEV,) + a.shape, a.dtype) for a in arrs]

    def body(*refs):
        ins, outs = refs[:n], refs[n:2 * n]
        send_sems, recv_sems, loc_sems = refs[2 * n:]
        x, y, c = lax.axis_index("x"), lax.axis_index("y"), lax.axis_index("c")
        me, sibling = (x, y, c), (x, y, 1 - c)
        chips = [(1 - x, y), (x, 1 - y), (1 - x, 1 - y)]

        def copy(k, j, block, to, src=None):
            px, py, pc = block
            dst = outs[k].at[4 * px + 2 * py + pc]
            return pltpu.make_async_remote_copy(src_ref=dst if src is None else src, dst_ref=dst,
                                                send_sem=send_sems.at[k, j], recv_sem=recv_sems.at[k, j],
                                                device_id=to, device_id_type=pl.DeviceIdType.MESH)

        mine = [pltpu.make_async_copy(ins[k], outs[k].at[4 * x + 2 * y + c], loc_sems.at[k]) for k in range(n)]
        first = []
        for k in range(n):
            mine[k].start()
            first.append(copy(k, 0, me, sibling, src=ins[k]))
            first += [copy(k, 1 + j, me, (*chip, c), src=ins[k]) for j, chip in enumerate(chips)]
        for cp in first:
            cp.start()
        passed = []
        for j, chip in enumerate(chips):
            for k in range(n):
                copy(k, 1 + j, (*chip, c), me).wait_recv()
                passed.append(copy(k, 4 + j, (*chip, c), sibling))
                passed[-1].start()
        for k in range(n):
            copy(k, 0, sibling, me).wait_recv()
            for j, chip in enumerate(chips):
                copy(k, 4 + j, (*chip, 1 - c), me).wait_recv()
        for cp in first + passed:
            cp.wait_send()
        for cp in mine:
            cp.wait()

    vmem = pl.BlockSpec(memory_space=pltpu.VMEM)
    return pl.pallas_call(body, name=name, out_shape=out_shape, in_specs=[vmem] * n, out_specs=[vmem] * n,
                          scratch_shapes=[pltpu.SemaphoreType.DMA((n, 7)), pltpu.SemaphoreType.DMA((n, 7)),
                                          pltpu.SemaphoreType.DMA((n,))], compiler_params=_params())(*arrs)


def _exchange(arrs, scatter, name):
    ex = _Exchange(arrs, scatter)

    def body(*refs):
        ins, outs, sems = refs[:ex.n], refs[ex.n:2 * ex.n], refs[2 * ex.n:]
        ex.start(ins, outs, sems)
        ex.wait(ins, outs, sems)

    hbm = pl.BlockSpec(memory_space=pltpu.HBM)
    return pl.pallas_call(body, name=name, out_shape=ex.out_shape, in_specs=[hbm] * ex.n, out_specs=[hbm] * ex.n,
                          scratch_shapes=ex.scratch)(*ex.arrs)


def _matmul(a, b, mode, name, out_dtype=F32, tm=512, tn=1024, tk=1024, a_fn=None, extras=(), out_slabs=None,
            exchange=None):
    if mode == "NN":
        (M, K), N = a.shape, b.shape[1]
    elif mode == "NT":
        (M, K), N = a.shape, b.shape[0]
    else:
        (K, M), N = a.shape, b.shape[1]
    if out_slabs:
        tn = N // out_slabs
    tm, tn, tk = min(tm, M), min(tn, N), min(tk, K)
    assert M % tm == 0 and N % tn == 0 and K % tk == 0, (name, M, N, K)
    nk = K // tk
    dims = {"NN": NN, "NT": NT, "TN": TN}[mode]
    ne = len(extras)

    def body(a_ref, b_ref, *rest):
        e_refs, o_ref, acc_ref = rest[:ne], rest[ne], rest[ne + 1]
        k = pl.program_id(2)

        @pl.when(k == 0)
        def _():
            acc_ref[...] = jnp.zeros_like(acc_ref)

        at = a_ref[...]
        if a_fn is not None:
            at = a_fn(at.astype(F32), *[e[...] for e in e_refs])
        acc_ref[...] += _bdot(at, b_ref[...], dims)

        @pl.when(k == nk - 1)
        def _():
            o_ref[...] = acc_ref[...].astype(out_dtype)

    if mode == "TN":
        a_spec = pl.BlockSpec((tk, tm), lambda i, j, k: (k, i))
        e_spec = pl.BlockSpec((1, tm), lambda i, j, k: (0, i))
    else:
        a_spec = pl.BlockSpec((tm, tk), lambda i, j, k: (i, k))
        e_spec = pl.BlockSpec((1, tk), lambda i, j, k: (0, k))
    if mode == "NT":
        b_spec = pl.BlockSpec((tn, tk), lambda i, j, k: (j, k))
    else:
        b_spec = pl.BlockSpec((tk, tn), lambda i, j, k: (k, j))
    if out_slabs:
        o_shape = jax.ShapeDtypeStruct((out_slabs, M, tn), out_dtype)
        o_spec = pl.BlockSpec((None, tm, tn), lambda i, j, k: (j, i, 0))
    else:
        o_shape = jax.ShapeDtypeStruct((M, N), out_dtype)
        o_spec = pl.BlockSpec((tm, tn), lambda i, j, k: (i, j))
    (out,), got = _call(body, name, (a, b, *extras), [o_shape], grid=(M // tm, N // tn, nk),
                        in_specs=[a_spec, b_spec] + [e_spec] * ne, out_specs=[o_spec],
                        scratch_shapes=[pltpu.VMEM((tm, tn), F32)], exchange=exchange)
    return out if exchange is None else (out, got)


def _modulate(x, sc, sh):
    return x * (1.0 + sc) + sh


def _square(x):
    return x * x


def _mod_part(c_all, w_ada_s, b_s):
    def body(c_ref, w_ref, b_ref, mod_ref, cond_ref):
        cv = c_ref[...]
        cond = cv * _sigmoid(cv)
        cond_ref[...] = cond
        mod_ref[...] = _bdot(cond, w_ref[...]) + b_ref[...]

    return pl.pallas_call(
        body, name="mod_part",
        out_shape=[jax.ShapeDtypeStruct((N_DEV, w_ada_s.shape[1]), F32), jax.ShapeDtypeStruct(c_all.shape, F32)],
        compiler_params=_params(),
    )(c_all, w_ada_s, b_s)


def _rms_fwd(x, w):
    rs = lax.rsqrt(_lanemean(x * x) + RMS_EPS)
    return x * rs * w, rs


def _rms_bwd(x, rs, w, dy):
    xhat = x * rs
    dxh = dy * w
    return rs * (dxh - xhat * _lanemean(dxh * xhat)), dy * xhat


def _mla_pre(z, pos_col, invf, m_one, m_rot, wq_ext, wkv_ext, qnw, kvnw):
    T = z.shape[0]
    tm = min(ROW_TILE, T)

    def body(z_ref, pos_ref, invf_ref, mone_ref, mrot_ref, wq_ref, wkv_ref, qnw_ref, kvnw_ref,
             q_ref, k_ref, v_ref, c1_ref, s1_ref, cqn_ref, ckvn_ref):
        ang = pos_ref[...].astype(F32) * invf_ref[...]
        c1 = mone_ref[...] + mrot_ref[...] * jnp.cos(ang)
        s1 = mrot_ref[...] * jnp.sin(ang)
        c1_ref[...] = c1
        s1_ref[...] = s1
        cqn, _ = _rms_fwd(z_ref[:, 0:256], qnw_ref[...])
        ckvn, _ = _rms_fwd(z_ref[:, 256:512], kvnw_ref[...])
        cqn_ref[...] = cqn.astype(BF16)
        ckvn_ref[...] = ckvn.astype(BF16)
        qe = _bdot(cqn, wq_ref[...])
        kve = _bdot(ckvn, wkv_ref[...])
        k_rope = z_ref[:, 512:768] * c1 + z_ref[:, 768:1024] * s1
        for h in range(HEADS):
            q_ref[h] = ((qe[:, 256 * h:256 * h + 256] * c1 + qe[:, 1024 + 256 * h:1280 + 256 * h] * s1)
                        * Q_PRESCALE).astype(BF16)
            k_ref[h] = (kve[:, 256 * h:256 * h + 256] + k_rope).astype(BF16)
            v_ref[h] = kve[:, 1024 + 128 * h:1152 + 128 * h].astype(BF16)

    row = lambda i: (i, 0)
    head = lambda i: (0, i, 0)
    return pl.pallas_call(
        body, name="mla_pre", grid=(T // tm,),
        in_specs=[pl.BlockSpec((tm, 1024), lambda i: (i, 2)), pl.BlockSpec((tm, 1), row),
                  _full((1, 256)), _full((1, 256)), _full((1, 256)), _full(wq_ext.shape), _full(wkv_ext.shape),
                  _full((1, 256)), _full((1, 256))],
        out_specs=[pl.BlockSpec((HEADS, tm, QK_PAD), head), pl.BlockSpec((HEADS, tm, QK_PAD), head),
                   pl.BlockSpec((HEADS, tm, HEAD_DIM), head), pl.BlockSpec((tm, 256), row), pl.BlockSpec((tm, 256), row),
                   pl.BlockSpec((tm, 256), row), pl.BlockSpec((tm, 256), row)],
        out_shape=[jax.ShapeDtypeStruct((HEADS, T, QK_PAD), BF16), jax.ShapeDtypeStruct((HEADS, T, QK_PAD), BF16),
                   jax.ShapeDtypeStruct((HEADS, T, HEAD_DIM), BF16), jax.ShapeDtypeStruct((T, 256), F32),
                   jax.ShapeDtypeStruct((T, 256), F32), jax.ShapeDtypeStruct((T, 256), BF16),
                   jax.ShapeDtypeStruct((T, 256), BF16)],
        compiler_params=_params(),
    )(z, pos_col, invf, m_one, m_rot, wq_ext, wkv_ext, qnw, kvnw)


def _mla_bwd(dq, dk, dv, z, c1, s1, wq_ext, wkv_ext, qnw, kvnw):
    T = z.shape[0]
    tm = min(ROW_TILE_SMALL, T)

    def body(dq_ref, dk_ref, dv_ref, z_ref, c1_ref, s1_ref, wq_ref, wkv_ref, qnw_ref, kvnw_ref,
             dz_ref, dqe_ref, dkve_ref, dqnw_ref, dkvnw_ref):
        @pl.when(pl.program_id(0) == 0)
        def _():
            dqnw_ref[...] = jnp.zeros_like(dqnw_ref)
            dkvnw_ref[...] = jnp.zeros_like(dkvnw_ref)

        c1, s1 = c1_ref[...], s1_ref[...]
        dkpe = jnp.zeros((tm, QK_PAD), F32)
        for h in range(HEADS):
            dqh, dkh = dq_ref[h] * Q_PRESCALE, dk_ref[h]
            dqe_ref[:, 256 * h:256 * h + 256] = (dqh * c1).astype(BF16)
            dqe_ref[:, 1024 + 256 * h:1280 + 256 * h] = (dqh * s1).astype(BF16)
            dkve_ref[:, 256 * h:256 * h + 256] = dkh.astype(BF16)
            dkve_ref[:, 1024 + 128 * h:1152 + 128 * h] = dv_ref[h].astype(BF16)
            dkpe = dkpe + dkh
        dcqn = _dot(dqe_ref[...], wq_ref[...], NT)
        dckvn = _dot(dkve_ref[...], wkv_ref[...], NT)
        cq, ckv = z_ref[:, 0:256], z_ref[:, 256:512]
        _, rsq = _rms_fwd(cq, qnw_ref[...])
        _, rskv = _rms_fwd(ckv, kvnw_ref[...])
        dcq, wq_rows = _rms_bwd(cq, rsq, qnw_ref[...], dcqn)
        dckv, wkv_rows = _rms_bwd(ckv, rskv, kvnw_ref[...], dckvn)
        dqnw_ref[...] += _rowsum(wq_rows)
        dkvnw_ref[...] += _rowsum(wkv_rows)
        dz_ref[:, 0:256] = dcq
        dz_ref[:, 256:512] = dckv
        dz_ref[:, 512:768] = dkpe * c1
        dz_ref[:, 768:1024] = dkpe * s1

    row = lambda i: (i, 0)
    head = lambda i: (0, i, 0)
    return pl.pallas_call(
        body, name="mla_bwd", grid=(T // tm,),
        in_specs=[pl.BlockSpec((HEADS, tm, QK_PAD), head), pl.BlockSpec((HEADS, tm, QK_PAD), head),
                  pl.BlockSpec((HEADS, tm, HEAD_DIM), head), pl.BlockSpec((tm, 1024), lambda i: (i, 2)),
                  pl.BlockSpec((tm, 256), row), pl.BlockSpec((tm, 256), row), _full(wq_ext.shape), _full(wkv_ext.shape),
                  _full((1, 256)), _full((1, 256))],
        out_specs=[pl.BlockSpec((tm, 1024), row), pl.BlockSpec((tm, 2048), row), pl.BlockSpec((tm, 1536), row),
                   _full((1, 256)), _full((1, 256))],
        out_shape=[jax.ShapeDtypeStruct((T, 1024), F32), jax.ShapeDtypeStruct((T, 2048), BF16),
                   jax.ShapeDtypeStruct((T, 1536), BF16), jax.ShapeDtypeStruct((1, 256), F32),
                   jax.ShapeDtypeStruct((1, 256), F32)],
        compiler_params=_params(),
    )(dq, dk, dv, z, c1, s1, wq_ext, wkv_ext, qnw, kvnw)


_HEAD_LANES = [slice(HEAD_DIM * h, HEAD_DIM * (h + 1)) for h in range(HEADS)]


def _lower_bound(lbraw_ref):
    a0, a1 = lbraw_ref[0:1, :], lbraw_ref[1:2, :]
    mx = jnp.maximum(a0, a1)
    e0, e1 = jnp.exp(a0 - mx), jnp.exp(a1 - mx)
    return e0 / (e0 + e1)


def _tri(lower):
    r = lax.broadcasted_iota(jnp.int32, (CHUNK, CHUNK), 0)
    c = lax.broadcasted_iota(jnp.int32, (CHUNK, CHUNK), 1)
    return (r >= c) if lower else (r <= c)


def _hgrn_gates(q, f, lb, tri_lo):
    sg = _sigmoid(f)
    forget = lb + (1.0 - lb) * sg
    k = 1.0 - forget
    b = _hdot(tri_lo.astype(F32), jnp.log(forget))
    b_ref, b_last = b[CHUNK // 2 - 1:CHUNK // 2, :], b[CHUNK - 1:CHUNK, :]
    e1, e2, e3, e4 = jnp.exp(b - b_ref), jnp.exp(b_ref - b), jnp.exp(b_last - b), jnp.exp(b)
    return dict(sg=sg, forget=forget, k=k, e1=e1, e2=e2, e3=e3, e4=e4, qa=q * e1, ka=k * e2, kl=k * e3, qb=q * e4,
                decay=jnp.exp(b_last))


def _hgrn_fwd(z, lbraw, nw, exchange=None):
    T = z.shape[0]
    G = min(HGRN_GROUP, T // CHUNK)
    rows = G * CHUNK
    n_chunks = T // CHUNK

    def body(q_ref, f_ref, i_ref, g_ref, lbraw_ref, nw_ref, oraw_ref, og_ref, sp_ref, st_ref):
        @pl.when(pl.program_id(0) == 0)
        def _():
            st_ref[...] = jnp.zeros_like(st_ref)

        lb_all = _lower_bound(lbraw_ref)
        tri_lo = _tri(True)

        def chunk(cc, carry):
            rs = pl.ds(pl.multiple_of(cc * CHUNK, CHUNK), CHUNK)
            t = _hgrn_gates(q_ref[rs, :], f_ref[rs, :], lb_all, tri_lo)
            v, gate = i_ref[rs, :], g_ref[rs, :]
            st = [st_ref[h] for h in range(HEADS)]
            a = [jnp.where(tri_lo, _bdot(t["qa"][:, s], t["ka"][:, s], NT), 0.0) for s in _HEAD_LANES]
            kv = [_bdot(v[:, s], t["kl"][:, s], TN) for s in _HEAD_LANES]
            o = [_bdot(a[h], v[:, s]) + _bdot(t["qb"][:, s], st[h], NT) for h, s in enumerate(_HEAD_LANES)]
            for h, s in enumerate(_HEAD_LANES):
                sp_ref[cc, h] = st[h]
                st_ref[h] = st[h] * t["decay"][:, s] + kv[h]
            oraw_ref[rs, :] = jnp.concatenate(o, axis=1)
            on = jnp.concatenate([_rms_fwd(o[h], nw_ref[:, s])[0] for h, s in enumerate(_HEAD_LANES)], axis=1)
            og_ref[rs, :] = (on * (gate * _sigmoid(gate))).astype(BF16)
            return carry

        lax.fori_loop(0, G, chunk, 0, unroll=2)

    col = lambda j: pl.BlockSpec((rows, 512), lambda r, j=j: (r, j))
    return _call(
        body, "hgrn_fwd", (z, z, z, z, lbraw, nw), grid=(T // rows,),
        in_specs=[col(0), col(1), col(2), col(3), _full((2, 512)), _full((1, 512))],
        out_specs=[col(0), col(0), pl.BlockSpec((G, HEADS, HEAD_DIM, HEAD_DIM), lambda r: (r, 0, 0, 0))],
        out_shape=[jax.ShapeDtypeStruct((T, 512), F32), jax.ShapeDtypeStruct((T, 512), BF16),
                   jax.ShapeDtypeStruct((n_chunks, HEADS, HEAD_DIM, HEAD_DIM), F32)],
        scratch_shapes=[pltpu.VMEM((HEADS, HEAD_DIM, HEAD_DIM), F32)], exchange=exchange)


def _hgrn_bwd(dmixcat, z, oraw, sprev, lbraw, nw, exchange=None):
    T = z.shape[0]
    G = min(HGRN_GROUP, T // CHUNK)
    rows = G * CHUNK
    ng = T // rows

    def body(dog_ref, q_ref, f_ref, i_ref, g_ref, oraw_ref, sp_ref, lbraw_ref, nw_ref,
             dz_ref, dlb_ref, dnw_ref, dst_ref):
        @pl.when(pl.program_id(0) == 0)
        def _():
            dst_ref[...] = jnp.zeros_like(dst_ref)
            dlb_ref[...] = jnp.zeros_like(dlb_ref)
            dnw_ref[...] = jnp.zeros_like(dnw_ref)

        lb_all = _lower_bound(lbraw_ref)
        tri_lo, tri_up = _tri(True), _tri(False)
        rowid = lax.broadcasted_iota(jnp.int32, (CHUNK, HEADS * HEAD_DIM), 0)

        def chunk(it, carry):
            cc = G - 1 - it
            rs = pl.ds(pl.multiple_of(cc * CHUNK, CHUNK), CHUNK)
            heads = list(enumerate(_HEAD_LANES))
            cat = lambda parts: jnp.concatenate(parts, axis=1)
            per_head_mean = lambda x: cat([jnp.broadcast_to(_lanemean(x[:, s]), (CHUNK, HEAD_DIM)) for s in _HEAD_LANES])
            t = _hgrn_gates(q_ref[rs, :], f_ref[rs, :], lb_all, tri_lo)
            v, gate, o, dog, nw_all = i_ref[rs, :], g_ref[rs, :], oraw_ref[rs, :], dog_ref[rs, :], nw_ref[...]
            rs_o = lax.rsqrt(per_head_mean(o * o) + RMS_EPS)
            xhat = o * rs_o
            sgg = _sigmoid(gate)
            d_on = dog * (gate * sgg)
            dz_ref[rs, 1536:2048] = dog * (xhat * nw_all) * (sgg * (1.0 + gate * (1.0 - sgg)))
            dxh = d_on * nw_all
            do = rs_o * (dxh - xhat * per_head_mean(dxh * xhat))
            dnw_ref[...] += _rowsum(d_on * xhat)
            st = [sp_ref[cc, h] for h in range(HEADS)]
            dst = [dst_ref[h] for h in range(HEADS)]
            a = [jnp.where(tri_lo, _bdot(t["qa"][:, s], t["ka"][:, s], NT), 0.0) for s in _HEAD_LANES]
            da = [jnp.where(tri_lo, _bdot(do[:, s], v[:, s], NT), 0.0) for s in _HEAD_LANES]
            dqb = cat([_bdot(do[:, s], st[h]) for h, s in heads])
            dkl = cat([_bdot(v[:, s], dst[h]) for h, s in heads])
            dv_ = cat([_bdot(t["kl"][:, s], dst[h], NT) + _bdot(a[h], do[:, s], TN) for h, s in heads])
            dqa = cat([_bdot(da[h], t["ka"][:, s]) for h, s in heads])
            dka = cat([_bdot(da[h], t["qa"][:, s], TN) for h, s in heads])
            ddecay = cat([_rowsum(dst[h] * st[h]) for h in range(HEADS)])
            for h, s in heads:
                dst_ref[h] = dst[h] * t["decay"][:, s] + _bdot(do[:, s], t["qb"][:, s], TN)
            pa, pk, pb, pl_ = dqa * t["qa"], dka * t["ka"], dqb * t["qb"], dkl * t["kl"]
            db = pa - pk + pb - pl_
            db = db + jnp.where(rowid == CHUNK // 2 - 1, _rowsum(pk - pa), 0.0)
            db = db + jnp.where(rowid == CHUNK - 1, _rowsum(pl_) + ddecay * t["decay"], 0.0)
            dlogf = _hdot(tri_up.astype(F32), db)
            dforget = dlogf / t["forget"] - (dka * t["e2"] + dkl * t["e3"])
            sg = t["sg"]
            dz_ref[rs, 0:512] = dqa * t["e1"] + dqb * t["e4"]
            dz_ref[rs, 512:1024] = dforget * (1.0 - lb_all) * sg * (1.0 - sg)
            dz_ref[rs, 1024:1536] = dv_
            dlb_ref[...] += _rowsum(dforget * (1.0 - sg))
            return carry

        lax.fori_loop(0, G, chunk, 0, unroll=2)

    col = lambda j: pl.BlockSpec((rows, 512), lambda r, j=j: (ng - 1 - r, j))
    return _call(
        body, "hgrn_bwd", (dmixcat, z, z, z, z, oraw, sprev, lbraw, nw), grid=(ng,),
        in_specs=[col(0), col(0), col(1), col(2), col(3), col(0),
                  pl.BlockSpec((G, HEADS, HEAD_DIM, HEAD_DIM), lambda r: (ng - 1 - r, 0, 0, 0)),
                  _full((2, 512)), _full((1, 512))],
        out_specs=[pl.BlockSpec((rows, 2048), lambda r: (ng - 1 - r, 0)), _full((1, 512)), _full((1, 512))],
        out_shape=[jax.ShapeDtypeStruct((T, 2048), F32), jax.ShapeDtypeStruct((1, 512), F32),
                   jax.ShapeDtypeStruct((1, 512), F32)],
        scratch_shapes=[pltpu.VMEM((HEADS, HEAD_DIM, HEAD_DIM), F32)], exchange=exchange)


def _diag_mask(t):
    r = lax.broadcasted_iota(jnp.int32, (t, t), 0)
    c = lax.broadcasted_iota(jnp.int32, (t, t), 1)
    return r >= c


def _attn_fwd(q, k, v, exchange=None):
    _, T, _ = q.shape
    t = min(ATT_TILE, T)

    def body(q_ref, k_ref, v_ref, o_ref, lse_ref):
        i = pl.program_id(1)
        qb = q_ref[...]

        def step(j, carry, masked):
            m, l, acc = carry
            ks = pl.ds(pl.multiple_of(j * t, t), t)
            s = _dot(qb, k_ref[ks, :], NT)
            if masked:
                s = jnp.where(_diag_mask(t), s, NEG_BIG)
            mn = jnp.maximum(m, jnp.max(s, axis=-1, keepdims=True))
            p = jnp.exp2(s - mn)
            al = jnp.exp2(m - mn)
            return mn, al * l + jnp.sum(p, axis=-1, keepdims=True), al * acc + _dot(p.astype(BF16), v_ref[ks, :])

        init = (jnp.full((t, 1), NEG_BIG, F32), jnp.zeros((t, 1), F32), jnp.zeros((t, HEAD_DIM), F32))
        carry = lax.fori_loop(0, i, lambda j, c: step(j, c, False), init)
        m, l, acc = step(i, carry, True)
        o_ref[...] = acc / l
        lse_ref[...] = jnp.broadcast_to(m + jnp.log2(l), (t, HEAD_DIM))

    return _call(
        body, "attn_fwd", (q, k, v), grid=(HEADS, T // t),
        in_specs=[pl.BlockSpec((None, t, QK_PAD), lambda h, i: (h, i, 0)),
                  pl.BlockSpec((None, T, QK_PAD), lambda h, i: (h, 0, 0)),
                  pl.BlockSpec((None, T, HEAD_DIM), lambda h, i: (h, 0, 0))],
        out_specs=[pl.BlockSpec((t, HEAD_DIM), lambda h, i: (i, h)),
                   pl.BlockSpec((None, t, HEAD_DIM), lambda h, i: (h, i, 0))],
        out_shape=[jax.ShapeDtypeStruct((T, HEADS * HEAD_DIM), F32), jax.ShapeDtypeStruct((HEADS, T, HEAD_DIM), F32)],
        exchange=exchange)


def _attn_bwd(q, k, v, dmixcat, o, lse, exchange=None):
    _, T, _ = q.shape
    t = min(ATT_TILE, T)
    nq = T // t

    def body(q_ref, k_ref, v_ref, do_ref, o_ref, lse_ref, dq_ref, dk_ref, dv_ref, delta_ref):
        j = pl.program_id(1)

        @pl.when(j == 0)
        def _():
            dq_ref[...] = jnp.zeros_like(dq_ref)

            def fill(i, carry):
                rs = pl.ds(pl.multiple_of(i * t, t), t)
                delta_ref[rs, :] = jnp.broadcast_to(
                    jnp.sum(do_ref[rs, :] * o_ref[rs, :], axis=-1, keepdims=True), (t, HEAD_DIM))
                return carry

            lax.fori_loop(0, nq, fill, 0)

        kb, vb = k_ref[...], v_ref[...]

        def step(i, carry, masked):
            dk, dv = carry
            rs = pl.ds(pl.multiple_of(i * t, t), t)
            qb, dob = q_ref[rs, :], do_ref[rs, :].astype(BF16)
            p = jnp.exp2(_dot(qb, kb, NT) - lse_ref[rs, 0:1])
            if masked:
                p = jnp.where(_diag_mask(t), p, 0.0)
            dp = _dot(dob, vb, NT)
            ds = (p * (dp - delta_ref[rs, 0:1]) * LN2).astype(BF16)
            dq_ref[rs, :] += _dot(ds, kb)
            return dk + _dot(ds, qb, TN), dv + _dot(p.astype(BF16), dob, TN)

        carry = step(j, (jnp.zeros((t, QK_PAD), F32), jnp.zeros((t, HEAD_DIM), F32)), True)
        dk, dv = lax.fori_loop(j + 1, nq, lambda i, c: step(i, c, False), carry)
        dk_ref[...] = dk
        dv_ref[...] = dv

    return _call(
        body, "attn_bwd", (q, k, v, dmixcat, o, lse), grid=(HEADS, nq),
        in_specs=[pl.BlockSpec((None, T, QK_PAD), lambda h, j: (h, 0, 0)),
                  pl.BlockSpec((None, t, QK_PAD), lambda h, j: (h, j, 0)),
                  pl.BlockSpec((None, t, HEAD_DIM), lambda h, j: (h, j, 0)),
                  pl.BlockSpec((T, HEAD_DIM), lambda h, j: (0, HEADS + h)),
                  pl.BlockSpec((T, HEAD_DIM), lambda h, j: (0, h)),
                  pl.BlockSpec((None, T, HEAD_DIM), lambda h, j: (h, 0, 0))],
        out_specs=[pl.BlockSpec((None, T, QK_PAD), lambda h, j: (h, 0, 0)),
                   pl.BlockSpec((None, t, QK_PAD), lambda h, j: (h, j, 0)),
                   pl.BlockSpec((None, t, HEAD_DIM), lambda h, j: (h, j, 0))],
        out_shape=[jax.ShapeDtypeStruct((HEADS, T, QK_PAD), F32), jax.ShapeDtypeStruct((HEADS, T, QK_PAD), F32),
                   jax.ShapeDtypeStruct((HEADS, T, HEAD_DIM), F32)],
        scratch_shapes=[pltpu.VMEM((T, HEAD_DIM), F32)], exchange=exchange)


def _ln_fwd(r):
    mu = _lanemean(r)
    xc = r - mu
    rstd = lax.rsqrt(_lanemean(xc * xc) + LN_EPS)
    return xc * rstd, rstd


def _ln_bwd(dxh, xhat, rstd):
    return rstd * (dxh - _lanemean(dxh) - xhat * _lanemean(dxh * xhat))


def _mix_ln1(o_hg, o_mla, w_out, x, g_a, ln1_g, ln1_b, sc_m, sh_m, exchange=None):
    T = x.shape[0]
    tm = min(ROW_TILE_SMALL, T)
    half = o_hg.shape[1]

    def body(hg_ref, mla_ref, w_ref, x_ref, ga_ref, g_ref, b_ref, sc_ref, sh_ref, mix_ref, xhat_ref, rstd_ref, u2_ref):
        mix = _dot(hg_ref[...], w_ref[0:half, :]) + _bdot(mla_ref[...], w_ref[half:, :])
        mix_ref[...] = mix
        xhat, rstd = _ln_fwd(ALPHA * x_ref[...] + (1.0 + ga_ref[...]) * mix)
        xhat_ref[...] = xhat
        rstd_ref[...] = jnp.broadcast_to(rstd, (tm, 128))
        u2_ref[...] = _modulate(xhat * g_ref[...] + b_ref[...], sc_ref[...], sh_ref[...]).astype(BF16)

    row = pl.BlockSpec((tm, D_MODEL), lambda i: (i, 0))
    vec = _full((1, D_MODEL))
    halfrow = pl.BlockSpec((tm, half), lambda i: (i, 0))
    return _call(
        body, "mix_ln1", (o_hg, o_mla, w_out, x, g_a, ln1_g, ln1_b, sc_m, sh_m), grid=(T // tm,),
        in_specs=[halfrow, halfrow, _full(w_out.shape), row, vec, vec, vec, vec, vec],
        out_specs=[row, row, pl.BlockSpec((tm, 128), lambda i: (i, 0)), row],
        out_shape=[jax.ShapeDtypeStruct((T, D_MODEL), F32), jax.ShapeDtypeStruct((T, D_MODEL), F32),
                   jax.ShapeDtypeStruct((T, 128), F32), jax.ShapeDtypeStruct((T, D_MODEL), BF16)],
        exchange=exchange)


def _mlp_fwd(u2, w1, w2, xhat1, ln1_g, ln1_b, g_m, ln2_g, ln2_b, target):
    T = u2.shape[0]
    tf = w1.shape[-1]
    nf = N_DEV // MLP_SLABS
    tm = min(ROW_TILE, T)

    def body(u2_ref, w1_ref, w2_ref, xhat_ref, g1_ref, b1_ref, gm_ref, g2_ref, b2_ref, tgt_ref,
             r_ref, dr2_ref, dh_ref, dg2_ref, db2_ref, dgm_ref, loss_ref, acc_ref):
        i, f = pl.program_id(0), pl.program_id(1)

        @pl.when((i == 0) & (f == 0))
        def _():
            for ref in (dg2_ref, db2_ref, dgm_ref, loss_ref):
                ref[...] = jnp.zeros_like(ref)

        @pl.when(f == 0)
        def _():
            acc_ref[...] = jnp.zeros_like(acc_ref)

        u2t = u2_ref[...]
        part = None
        for s in range(MLP_SLABS):
            r = jnp.maximum(_dot(u2t, w1_ref[s]), 0.0)
            r_ref[:, s * tf:(s + 1) * tf] = r.astype(BF16)
            d = _bdot(r * r, w2_ref[s])
            part = d if part is None else part + d
        acc_ref[...] += part

        @pl.when(f == nf - 1)
        def _():
            h = acc_ref[...]
            x1 = xhat_ref[...] * g1_ref[...] + b1_ref[...]
            xhat2, rstd2 = _ln_fwd(ALPHA * x1 + (1.0 + gm_ref[...]) * h)
            err = xhat2 * g2_ref[...] + b2_ref[...] - tgt_ref[...]
            loss_ref[...] += jnp.sum(0.5 * _lanemean(err * err), axis=0, keepdims=True)
            dy = err * (1.0 / D_MODEL)
            dg2_ref[...] += _rowsum(dy * xhat2)
            db2_ref[...] += _rowsum(dy)
            dr2 = _ln_bwd(dy * g2_ref[...], xhat2, rstd2)
            dr2_ref[...] = dr2
            dgm_ref[...] += _rowsum(dr2 * h)
            dh_ref[...] = ((1.0 + gm_ref[...]) * dr2).astype(BF16)

    row = pl.BlockSpec((tm, D_MODEL), lambda i, f: (i, 0))
    vec = _full((1, D_MODEL))
    return pl.pallas_call(
        body, name="mlp_fwd", grid=(T // tm, nf),
        in_specs=[row, pl.BlockSpec((MLP_SLABS, D_MODEL, tf), lambda i, f: (f, 0, 0)),
                  pl.BlockSpec((MLP_SLABS, tf, D_MODEL), lambda i, f: (f, 0, 0)),
                  row, vec, vec, vec, vec, vec, row],
        out_specs=[pl.BlockSpec((tm, MLP_SLABS * tf), lambda i, f: (i, f)), row, row, vec, vec, vec, _full((1, 128))],
        out_shape=[jax.ShapeDtypeStruct((T, N_DEV * tf), BF16), jax.ShapeDtypeStruct((T, D_MODEL), F32),
                   jax.ShapeDtypeStruct((T, D_MODEL), BF16), jax.ShapeDtypeStruct((1, D_MODEL), F32),
                   jax.ShapeDtypeStruct((1, D_MODEL), F32), jax.ShapeDtypeStruct((1, D_MODEL), F32),
                   jax.ShapeDtypeStruct((1, 128), F32)],
        scratch_shapes=[pltpu.VMEM((tm, D_MODEL), F32)],
        compiler_params=_params(),
    )(u2, w1, w2, xhat1, ln1_g, ln1_b, g_m, ln2_g, ln2_b, target)


def _mlp_bwd(dh, w1, w2, r, dr2, xhat1, rstd1, mix, ln1_g, ln1_b, sc_m, g_a):
    T = dh.shape[0]
    tf = w1.shape[-1]
    nf = N_DEV // MLP_SLABS
    tm = min(ROW_TILE, T)

    def body(dh_ref, w1_ref, w2_ref, r_ref, dr2_ref, xhat_ref, rstd_ref, mix_ref, g1_ref, b1_ref, sc_ref, ga_ref,
             dhpre_ref, dr1_ref, dmix_ref, dsc_ref, dsh_ref, dg1_ref, db1_ref, dga_ref, acc_ref):
        i, f = pl.program_id(0), pl.program_id(1)

        @pl.when((i == 0) & (f == 0))
        def _():
            for ref in (dsc_ref, dsh_ref, dg1_ref, db1_ref, dga_ref):
                ref[...] = jnp.zeros_like(ref)

        @pl.when(f == 0)
        def _():
            acc_ref[...] = jnp.zeros_like(acc_ref)

        dht = dh_ref[...]
        part = None
        for s in range(MLP_SLABS):
            cols = slice(s * tf, (s + 1) * tf)
            dhpre = (_dot(dht, w2_ref[s], NT) * (2.0 * r_ref[:, cols].astype(F32))).astype(BF16)
            dhpre_ref[:, cols] = dhpre
            d = _dot(dhpre, w1_ref[s], NT)
            part = d if part is None else part + d
        acc_ref[...] += part

        @pl.when(f == nf - 1)
        def _():
            du2 = acc_ref[...]
            xhat = xhat_ref[...]
            x1 = xhat * g1_ref[...] + b1_ref[...]
            dx1 = ALPHA * dr2_ref[...] + du2 * (1.0 + sc_ref[...])
            dsc_ref[...] += _rowsum(du2 * x1)
            dsh_ref[...] += _rowsum(du2)
            dg1_ref[...] += _rowsum(dx1 * xhat)
            db1_ref[...] += _rowsum(dx1)
            dr1 = _ln_bwd(dx1 * g1_ref[...], xhat, rstd_ref[:, 0:1])
            dr1_ref[...] = dr1
            dga_ref[...] += _rowsum(dr1 * mix_ref[...])
            dmix_ref[...] = ((1.0 + ga_ref[...]) * dr1).astype(BF16)

    row = pl.BlockSpec((tm, D_MODEL), lambda i, f: (i, 0))
    vec = _full((1, D_MODEL))
    return pl.pallas_call(
        body, name="mlp_bwd", grid=(T // tm, nf),
        in_specs=[row, pl.BlockSpec((MLP_SLABS, D_MODEL, tf), lambda i, f: (f, 0, 0)),
                  pl.BlockSpec((MLP_SLABS, tf, D_MODEL), lambda i, f: (f, 0, 0)),
                  pl.BlockSpec((tm, MLP_SLABS * tf), lambda i, f: (i, f)), row, row,
                  pl.BlockSpec((tm, 128), lambda i, f: (i, 0)), row, vec, vec, vec, vec],
        out_specs=[pl.BlockSpec((tm, MLP_SLABS * tf), lambda i, f: (i, f)), row, row, vec, vec, vec, vec, vec],
        out_shape=[jax.ShapeDtypeStruct((T, N_DEV * tf), BF16), jax.ShapeDtypeStruct((T, D_MODEL), F32),
                   jax.ShapeDtypeStruct((T, D_MODEL), BF16)] + [jax.ShapeDtypeStruct((1, D_MODEL), F32)] * 5,
        scratch_shapes=[pltpu.VMEM((tm, D_MODEL), F32)],
        compiler_params=_params(),
    )(dh, w1, w2, r, dr2, xhat1, rstd1, mix, ln1_g, ln1_b, sc_m, g_a)


def _input_bwd(dz_h, dz_m, w_in_ext, x, dr1, sc_a, exchange=None):
    T = x.shape[0]
    tm = min(ROW_TILE_SMALL, T)

    def body(dzh_ref, dzm_ref, w_ref, x_ref, dr1_ref, sc_ref, gx_ref, dsc_ref, dsh_ref):
        @pl.when(pl.program_id(0) == 0)
        def _():
            dsc_ref[...] = jnp.zeros_like(dsc_ref)
            dsh_ref[...] = jnp.zeros_like(dsh_ref)

        du = _bdot(dzh_ref[...], w_ref[:, 0:2048], NT) + _bdot(dzm_ref[...], w_ref[:, 2048:3072], NT)
        gx_ref[...] = ALPHA * dr1_ref[...] + du * (1.0 + sc_ref[...])
        dsc_ref[...] += _rowsum(du * x_ref[...])
        dsh_ref[...] += _rowsum(du)

    row = pl.BlockSpec((tm, D_MODEL), lambda i: (i, 0))
    vec = _full((1, D_MODEL))
    return _call(
        body, "input_bwd", (dz_h, dz_m, w_in_ext, x, dr1, sc_a), grid=(T // tm,),
        in_specs=[pl.BlockSpec((tm, 2048), lambda i: (i, 0)), row, _full(w_in_ext.shape), row, row, vec],
        out_specs=[row, vec, vec],
        out_shape=[jax.ShapeDtypeStruct((T, D_MODEL), F32), jax.ShapeDtypeStruct((1, D_MODEL), F32),
                   jax.ShapeDtypeStruct((1, D_MODEL), F32)], exchange=exchange)


def _adam_math(w, g, m, v):
    m = ADAM_B1 * m + (1.0 - ADAM_B1) * g
    v = ADAM_B2 * v + (1.0 - ADAM_B2) * (g * g)
    m_hat = m / (1.0 - ADAM_B1 ** ADAM_STEP)
    v_hat = v / (1.0 - ADAM_B2 ** ADAM_STEP)
    return -ADAM_LR * (m_hat / (jnp.sqrt(v_hat) + ADAM_EPS) + ADAM_WD * w), m, v


def _adam(g_slabs, w, m, v, name, g_fn=None, g_extra=()):
    R, C = w.shape
    tr = R if R <= 256 else 256
    assert R % tr == 0
    ns = 0 if g_slabs is None else g_slabs.shape[0]
    ne = len(g_extra)

    def body(*refs):
        e_refs = refs[:ne]
        refs = refs[ne:]
        if ns:
            gs_ref, refs = refs[0], refs[1:]
        w_ref, m_ref, v_ref, g_ref, d_ref, nm_ref, nv_ref = refs
        if g_fn is not None:
            g = g_fn(*e_refs)
        else:
            g = gs_ref[0].astype(F32)
            for s in range(1, ns):
                g = g + gs_ref[s].astype(F32)
        d, nm, nv = _adam_math(w_ref[...], g, m_ref[...], v_ref[...])
        g_ref[...] = g
        d_ref[...] = d
        nm_ref[...] = nm
        nv_ref[...] = nv

    blk = pl.BlockSpec((tr, C), lambda i: (i, 0))
    in_specs = [pl.BlockSpec((tr, e.shape[1]), lambda i: (i, 0)) if e.shape[0] == R else _full(e.shape) for e in g_extra]
    args = list(g_extra)
    if ns:
        in_specs.append(pl.BlockSpec((ns, tr, C), lambda i: (0, i, 0)))
        args.append(g_slabs)
    return pl.pallas_call(
        body, name=name, grid=(R // tr,), in_specs=in_specs + [blk] * 3, out_specs=[blk] * 4,
        out_shape=[jax.ShapeDtypeStruct((R, C), F32)] * 4, compiler_params=_params(),
    )(*args, w, m, v)


def _adam_small(small_all, params):
    n = len(params)

    def body(*refs):
        s_ref, refs = refs[0], refs[1:]
        wmv, loss_ref, outs = refs[:3 * n], refs[3 * n], refs[3 * n + 1:]
        tot = s_ref[0]
        for i in range(1, N_DEV):
            tot = tot + s_ref[i]
        loss_ref[...] = tot[:, SMALL_W - 128:]
        for j, (w, _, _, off) in enumerate(params):
            w_ref, m_ref, v_ref = wmv[3 * j:3 * j + 3]
            g_ref, d_ref, nm_ref, nv_ref = outs[4 * j:4 * j + 4]
            if w.shape[0] == 2:
                lb = _lower_bound(w_ref)
                g0 = tot[:, off:off + w.shape[1]] * lb * (1.0 - lb)
                rows = [(slice(0, 1), g0), (slice(1, 2), -g0)]
            else:
                rows = [(slice(0, 1), tot[:, off:off + w.shape[1]])]
            for rs, g in rows:
                d, nm, nv = _adam_math(w_ref[rs, :], g, m_ref[rs, :], v_ref[rs, :])
                g_ref[rs, :], d_ref[rs, :], nm_ref[rs, :], nv_ref[rs, :] = g, d, nm, nv

    out_shape = [jax.ShapeDtypeStruct((1, 128), F32)]
    for w, _, _, _ in params:
        out_shape += [jax.ShapeDtypeStruct(w.shape, F32)] * 4
    res = pl.pallas_call(body, name="adam_small", out_shape=out_shape, compiler_params=_params())(
        small_all, *[a for w, m, v, _ in params for a in (w, m, v)])
    return res[0], [tuple(res[1 + 4 * j:5 + 4 * j]) for j in range(n)]


def _rot_half_cols(w):
    return jnp.concatenate([-w[..., 32:], w[..., :32]], axis=-1)


def _unrot_half_cols(dw_rot):
    return jnp.concatenate([dw_rot[..., 32:], -dw_rot[..., :32]], axis=-1)


def _cols_from_slabs(g):
    s, r, c = g.shape
    return jnp.transpose(g, (1, 0, 2)).reshape(r, s * c)


def _slabs_from_cols(w):
    r, c = w.shape
    return jnp.transpose(w.reshape(r, N_DEV, c // N_DEV), (1, 0, 2))


def _ext_in(w_in):
    k_in = w_in.shape[0]
    z64, z128 = jnp.zeros((k_in, 64), BF16), jnp.zeros((k_in, 128), BF16)
    wk = w_in[:, 2560:2624]
    return jnp.concatenate([w_in[:, :2560], z128, wk, z64, z128, _rot_half_cols(wk), z64], axis=1)


def _ext_q(w_q_up):
    r = w_q_up.shape[0]
    z64, z128 = jnp.zeros((r, 64), BF16), jnp.zeros((r, 128), BF16)
    wq = w_q_up.reshape(r, HEADS, HEAD_DIM + ROPE_DIM)
    main = [jnp.concatenate([wq[:, h, :HEAD_DIM], wq[:, h, HEAD_DIM:], z64], axis=1) for h in range(HEADS)]
    rot = [jnp.concatenate([z128, _rot_half_cols(wq[:, h, HEAD_DIM:]), z64], axis=1) for h in range(HEADS)]
    return jnp.concatenate(main + rot, axis=1)


def _ext_kv(w_kv_up):
    r = w_kv_up.shape[0]
    z128 = jnp.zeros((r, 128), BF16)
    wkv = w_kv_up.reshape(r, HEADS, 2 * HEAD_DIM)
    kpad = [jnp.concatenate([wkv[:, h, :HEAD_DIM], z128], axis=1) for h in range(HEADS)]
    vals = [wkv[:, h, HEAD_DIM:] for h in range(HEADS)]
    return jnp.concatenate(kpad + vals, axis=1)


def _grad_in_from_ext(dw_in_h, dw_in_m):
    dwk = dw_in_m[:, 512 + 128:512 + 192] + _unrot_half_cols(dw_in_m[:, 768 + 128:768 + 192])
    return jnp.concatenate([dw_in_h, dw_in_m[:, :512], dwk], axis=1)


def _grads_qkv_from_ext(dwq_ext, dwkv_ext):
    qcols = []
    for h in range(HEADS):
        main, rot = dwq_ext[:, 256 * h:256 * h + 256], dwq_ext[:, 1024 + 256 * h:1280 + 256 * h]
        qcols += [main[:, :128], main[:, 128:192] + _unrot_half_cols(rot[:, 128:192])]
    kvcols = []
    for h in range(HEADS):
        kvcols += [dwkv_ext[:, 256 * h:256 * h + 128], dwkv_ext[:, 1024 + 128 * h:1152 + 128 * h]]
    return jnp.concatenate(qcols, axis=1), jnp.concatenate(kvcols, axis=1)


SMALL_W = 6144 + 512 + 512 + 256 + 256 + 4 * 1024 + 128


def kernel(x, c, positions, w_ada, b_ada, w_in, hg_lower_bounds, hg_norm_w, mla_q_norm_w, w_q_up, mla_kv_norm_w, w_kv_up, w_out, ln1_g, ln1_b, w_mlp_in, w_mlp_out, ln2_g, ln2_b, loss_target, m_w_ada, m_b_ada, m_w_in, m_hg_lower_bounds, m_hg_norm_w, m_mla_q_norm_w, m_w_q_up, m_mla_kv_norm_w, m_w_kv_up, m_w_out, m_ln1_g, m_ln1_b, m_w_mlp_in, m_w_mlp_out, m_ln2_g, m_ln2_b, v_w_ada, v_b_ada, v_w_in, v_hg_lower_bounds, v_hg_norm_w, v_mla_q_norm_w, v_w_q_up, v_mla_kv_norm_w, v_w_kv_up, v_w_out, v_ln1_g, v_ln1_b, v_w_mlp_in, v_w_mlp_out, v_ln2_g, v_ln2_b):
    T = x.shape[1]
    me = 4 * lax.axis_index("x") + 2 * lax.axis_index("y") + lax.axis_index("c")
    xs, tgt = x[0], loss_target[0]
    big = dict(w_in=w_in[0], w_q_up=w_q_up[0], w_kv_up=w_kv_up[0], w_out=w_out[0], w_mlp_in=w_mlp_in[0],
               w_mlp_out=w_mlp_out[0])
    names = list(big)

    bf = {n: big[n].astype(BF16) for n in names}
    g_in, g_c = _gather_two_level([bf["w_in"], c], name="gather_w_in")
    c_all = g_c.reshape(N_DEV, D_MODEL)

    ada_cols = w_ada.shape[2]
    mod_part, cond = _mod_part(c_all, w_ada[0], lax.dynamic_slice(b_ada, (0, me * ada_cols), (1, ada_cols)))
    (mod_all,) = _exchange([mod_part], scatter=False, name="gather_mod")
    mod_row = lax.dynamic_slice(mod_all, (0, me, 0), (N_DEV, 1, ada_cols)).reshape(1, N_DEV * ada_cols)
    sh_a, sc_a, g_a, sh_m, sc_m, g_m = [mod_row[:, D_MODEL * i:D_MODEL * (i + 1)] for i in range(6)]

    w_in_ext = _ext_in(_cols_from_slabs(g_in))
    z, (g_q, g_kv, g_out) = _matmul(xs, w_in_ext, "NN", "in_proj", a_fn=_modulate, extras=(sc_a, sh_a), tn=3072,
                                    exchange=_Exchange([bf["w_q_up"], bf["w_kv_up"], bf["w_out"]], False))
    wq_ext, wkv_ext = _ext_q(_cols_from_slabs(g_q)), _ext_kv(_cols_from_slabs(g_kv))
    w_out_full = g_out.reshape(D_MODEL, D_MODEL)
    inv_freq = 1.0 / (ROPE_THETA ** (jnp.arange(0, ROPE_DIM, 2, dtype=F32) / ROPE_DIM))
    zeros = lambda n: jnp.zeros((n,), F32)
    invf = jnp.concatenate([zeros(128), inv_freq, inv_freq, zeros(64)]).reshape(1, QK_PAD)
    m_one = jnp.concatenate([jnp.ones((128,), F32), zeros(128)]).reshape(1, QK_PAD)
    m_rot = jnp.concatenate([zeros(128), jnp.ones((64,), F32), zeros(64)]).reshape(1, QK_PAD)
    q, k, v, c1, s1, cqn, ckvn = _mla_pre(z, positions.reshape(T, 1), invf, m_one, m_rot, wq_ext, wkv_ext,
                                          mla_q_norm_w, mla_kv_norm_w)
    (o_raw, o_gated, s_prev), (w1,) = _hgrn_fwd(z, hg_lower_bounds, hg_norm_w,
                                                exchange=_StagedGather(bf["w_mlp_in"]))
    (o_mla, lse), (w2,) = _attn_fwd(q, k, v, exchange=_StagedGather(bf["w_mlp_out"]))
    mix, xhat1, rstd1, u2 = _mix_ln1(o_gated, o_mla, w_out_full, xs, g_a, ln1_g, ln1_b, sc_m, sh_m)[0]
    r, dr2, dh, dln2_g, dln2_b, dg_m, loss_part = _mlp_fwd(u2, w1, w2, xhat1, ln1_g, ln1_b, g_m, ln2_g, ln2_b, tgt)

    dhpre, dr1, dmix, dsc_m, dsh_m, dln1_g, dln1_b, dg_a = _mlp_bwd(dh, w1, w2, r, dr2, xhat1, rstd1, mix, ln1_g,
                                                                    ln1_b, sc_m, g_a)
    received = {}
    dw2 = _matmul(r, dh, "TN", "wgrad_mlp_out", out_dtype=BF16, a_fn=_square, tm=1024)
    dw1 = _matmul(u2, dhpre, "TN", "wgrad_mlp_in", out_dtype=BF16, tm=1024, out_slabs=N_DEV)
    dmixcat = _matmul(dmix, w_out_full, "NT", "dgrad_out")
    dw_out = jnp.concatenate([_matmul(o_gated, dmix, "TN", "wgrad_out_hg", out_dtype=BF16),
                              _matmul(o_mla, dmix, "TN", "wgrad_out_mla", out_dtype=BF16)], axis=0)
    (dz_h, dlb, dnw), (received["w_out"],) = _hgrn_bwd(
        dmixcat, z, o_raw, s_prev, hg_lower_bounds, hg_norm_w,
        exchange=_Exchange([dw_out.reshape(N_DEV, D_MODEL // N_DEV, D_MODEL)], True))
    (dq, dk, dv), (received["w_mlp_in"], received["w_mlp_out"]) = _attn_bwd(
        q, k, v, dmixcat, o_mla, lse,
        exchange=_Exchange([dw1, dw2.reshape(N_DEV, dw2.shape[0] // N_DEV, D_MODEL)], True))
    dz_m, dq_ext, dkv_ext, dqnw, dkvnw = _mla_bwd(dq, dk, dv, z, c1, s1, wq_ext, wkv_ext, mla_q_norm_w,
                                                   mla_kv_norm_w)
    dwq_ext = _matmul(cqn, dq_ext, "TN", "wgrad_q_up", tn=2048)
    dwkv_ext = _matmul(ckvn, dkv_ext, "TN", "wgrad_kv_up", tn=1536)
    dwq, dwkv = _grads_qkv_from_ext(dwq_ext, dwkv_ext)
    dw_in_h, (received["w_q_up"], received["w_kv_up"]) = _matmul(
        xs, dz_h, "TN", "wgrad_in_h", a_fn=_modulate, extras=(sc_a, sh_a), tm=1024,
        exchange=_Exchange([_slabs_from_cols(g).astype(BF16) for g in (dwq, dwkv)], True))
    dw_in_m = _matmul(xs, dz_m, "TN", "wgrad_in_m", a_fn=_modulate, extras=(sc_a, sh_a), tm=1024)
    dw_in = _grad_in_from_ext(dw_in_h, dw_in_m)
    (grad_x, dsc_a, dsh_a), (received["w_in"],) = _input_bwd(
        dz_h, dz_m, w_in_ext, xs, dr1, sc_a, exchange=_Exchange([_slabs_from_cols(dw_in).astype(BF16)], True))

    small = jnp.concatenate([dsh_a, dsc_a, dg_a, dsh_m, dsc_m, dg_m, dlb, dnw, dqnw, dkvnw, dln1_g, dln1_b, dln2_g,
                             dln2_b, loss_part], axis=1)
    (small_all,) = _exchange([small], scatter=False, name="gather_small")

    moments = dict(w_in=(m_w_in, v_w_in), w_q_up=(m_w_q_up, v_w_q_up), w_kv_up=(m_w_kv_up, v_w_kv_up),
                   w_out=(m_w_out, v_w_out), w_mlp_in=(m_w_mlp_in, v_w_mlp_in), w_mlp_out=(m_w_mlp_out, v_w_mlp_out))
    res = {}
    for n in names:
        res[n] = _adam(received[n], big[n], moments[n][0][0], moments[n][1][0], name="adam_" + n)
    dmod_cols = lax.dynamic_slice(small_all.reshape(N_DEV, SMALL_W), (0, me * ada_cols), (N_DEV, ada_cols))
    cond_t = cond.T

    def ada_grad(ct_ref, dm_ref):
        g = ct_ref[:, 0:1] * dm_ref[0:1, :]
        for b in range(1, N_DEV):
            g = g + ct_ref[:, b:b + 1] * dm_ref[b:b + 1, :]
        return g

    res["w_ada"] = _adam(None, w_ada[0], m_w_ada[0], v_w_ada[0], name="adam_w_ada", g_fn=ada_grad,
                         g_extra=(cond_t, dmod_cols))

    small_params = [("b_ada", b_ada, m_b_ada, v_b_ada, 0),
                    ("hg_lower_bounds", hg_lower_bounds, m_hg_lower_bounds, v_hg_lower_bounds, 6144),
                    ("hg_norm_w", hg_norm_w, m_hg_norm_w, v_hg_norm_w, 6656),
                    ("mla_q_norm_w", mla_q_norm_w, m_mla_q_norm_w, v_mla_q_norm_w, 7168),
                    ("mla_kv_norm_w", mla_kv_norm_w, m_mla_kv_norm_w, v_mla_kv_norm_w, 7424),
                    ("ln1_g", ln1_g, m_ln1_g, v_ln1_g, 7680), ("ln1_b", ln1_b, m_ln1_b, v_ln1_b, 8704),
                    ("ln2_g", ln2_g, m_ln2_g, v_ln2_g, 9728), ("ln2_b", ln2_b, m_ln2_b, v_ln2_b, 10752)]
    loss_row, small_res = _adam_small(small_all, [p[1:] for p in small_params])
    for p, r4 in zip(small_params, small_res):
        res[p[0]] = r4
    loss = loss_row[0, 0]

    order = ["w_ada", "b_ada", "w_in", "hg_lower_bounds", "hg_norm_w", "mla_q_norm_w", "w_q_up", "mla_kv_norm_w",
             "w_kv_up", "w_out", "ln1_g", "ln1_b", "w_mlp_in", "w_mlp_out", "ln2_g", "ln2_b"]
    shaped = {n: tuple(a.reshape((1,) + a.shape) if n in big or n == "w_ada" else a for a in res[n]) for n in order}
    outs = [loss, grad_x.reshape(1, T, D_MODEL)]
    for i in range(4):
        outs += [shaped[n][i] for n in order]
    return tuple(outs)
```

```python
import functools

import jax
import jax.numpy as jnp
import numpy as np
from jax import lax
from jax.experimental import pallas as pl
from jax.experimental.pallas import tpu as pltpu

F32, BF16 = jnp.float32, jnp.bfloat16
N_DEV = 8
D_MODEL = 1024
HEADS = 4
HEAD_DIM = 128
ROPE_DIM = 64
QK_PAD = 256
CHUNK = 64
ROPE_THETA = 10000.0
RMS_EPS = 1e-6
LN_EPS = 1e-5
ALPHA = 2.0 ** 0.25
ATT_SCALE = (HEAD_DIM + ROPE_DIM) ** -0.5
LN2 = float(np.log(2.0))
Q_PRESCALE = ATT_SCALE / LN2
ADAM_LR, ADAM_B1, ADAM_B2, ADAM_EPS, ADAM_WD, ADAM_STEP = 0.001, 0.9, 0.999, 1e-08, 0.01, 10
NEG_BIG = -1e30

ROW_TILE = 512
ROW_TILE_SMALL = 256
ATT_TILE = 512
HGRN_GROUP = 8
MLP_SLABS = 4
VMEM_LIMIT = 56 * 2 ** 20

NN = (((1,), (0,)), ((), ()))
NT = (((1,), (1,)), ((), ()))
TN = (((0,), (0,)), ((), ()))


def _dot(a, b, dims=NN):
    return lax.dot_general(a, b, dims, preferred_element_type=F32)


def _bdot(a, b, dims=NN):
    return lax.dot_general(a.astype(BF16), b.astype(BF16), dims, preferred_element_type=F32)


def _hdot(a, b, dims=NN):
    return lax.dot_general(a, b, dims, precision=lax.Precision.HIGHEST, preferred_element_type=F32)


def _params():
    return pltpu.CompilerParams(vmem_limit_bytes=VMEM_LIMIT)


def _sigmoid(x):
    return 1.0 / (1.0 + jnp.exp(-x))


def _rowsum(x):
    return jnp.sum(x, axis=0, keepdims=True)


def _lanemean(x):
    return jnp.mean(x, axis=-1, keepdims=True)


def _full(shape):
    nd = len(shape)
    return pl.BlockSpec(shape, lambda *_: (0,) * nd)


class _Exchange:
    def __init__(self, arrs, scatter):
        self.arrs, self.scatter, self.n = list(arrs), scatter, len(arrs)
        self.out_shape = [jax.ShapeDtypeStruct((N_DEV,) + (a.shape[1:] if scatter else a.shape), a.dtype)
                          for a in self.arrs]
        n = self.n
        self.scratch = [pltpu.SemaphoreType.DMA((n, N_DEV - 1)), pltpu.SemaphoreType.DMA((n, N_DEV - 1)),
                        pltpu.SemaphoreType.DMA((n,))]

    def _copies(self, ins, outs, sems):
        send_sems, recv_sems, loc_sems = sems
        x, y, c = lax.axis_index("x"), lax.axis_index("y"), lax.axis_index("c")
        me = 4 * x + 2 * y + c
        copies = []
        for k in range(self.n):
            src_of = (lambda i, k=k: ins[k].at[i]) if self.scatter else (lambda i, k=k: ins[k])
            copies.append((pltpu.make_async_copy(src_of(me), outs[k].at[me], loc_sems.at[k]), None))
            for p in range(1, N_DEV):
                px = (1 - x) if p & 4 else x
                py = (1 - y) if p & 2 else y
                pc = (1 - c) if p & 1 else c
                peer = 4 * px + 2 * py + pc
                both = dict(send_sem=send_sems.at[k, p - 1], recv_sem=recv_sems.at[k, p - 1],
                            device_id=(px, py, pc), device_id_type=pl.DeviceIdType.MESH)
                send = pltpu.make_async_remote_copy(src_ref=src_of(peer), dst_ref=outs[k].at[me], **both)
                recv = pltpu.make_async_remote_copy(src_ref=src_of(peer), dst_ref=outs[k].at[peer], **both)
                copies.append((send, recv))
        return copies

    def start(self, ins, outs, sems):
        for first, _ in self._copies(ins, outs, sems):
            first.start()

    def middle(self, ins, outs, sems):
        pass

    def wait(self, ins, outs, sems):
        for first, recv in self._copies(ins, outs, sems):
            if recv is None:
                first.wait()
            else:
                recv.wait_recv()
                first.wait_send()


class _StagedGather:
    def __init__(self, arr):
        self.arrs, self.n = [arr], 1
        self.out_shape = [jax.ShapeDtypeStruct((N_DEV,) + arr.shape, arr.dtype)]
        self.scratch = [pltpu.VMEM((N_DEV,) + arr.shape, arr.dtype), pltpu.SemaphoreType.DMA((7,)),
                        pltpu.SemaphoreType.DMA((7,)), pltpu.SemaphoreType.DMA((2,))]

    def _parts(self, scr):
        stage, send_sems, recv_sems, loc_sems = scr
        x, y, c = lax.axis_index("x"), lax.axis_index("y"), lax.axis_index("c")
        me, sibling = (x, y, c), (x, y, 1 - c)
        chips = [(1 - x, y), (x, 1 - y), (1 - x, 1 - y)]

        def copy(j, block, to):
            px, py, pc = block
            slot = stage.at[4 * px + 2 * py + pc]
            return pltpu.make_async_remote_copy(src_ref=slot, dst_ref=slot, send_sem=send_sems.at[j],
                                                recv_sem=recv_sems.at[j], device_id=to,
                                                device_id_type=pl.DeviceIdType.MESH)

        return stage, loc_sems, me, sibling, chips, c, copy

    def start(self, ins, outs, scr):
        stage, loc_sems, me, sibling, chips, c, copy = self._parts(scr)
        x, y, _ = me
        own = pltpu.make_async_copy(ins[0], stage.at[4 * x + 2 * y + c], loc_sems.at[0])
        own.start()
        own.wait()
        copy(0, me, sibling).start()
        for j, chip in enumerate(chips):
            copy(1 + j, me, (*chip, c)).start()

    def middle(self, ins, outs, scr):
        stage, loc_sems, me, sibling, chips, c, copy = self._parts(scr)
        for j, chip in enumerate(chips):
            copy(1 + j, (*chip, c), me).wait_recv()
            copy(4 + j, (*chip, c), sibling).start()

    def wait(self, ins, outs, scr):
        stage, loc_sems, me, sibling, chips, c, copy = self._parts(scr)
        copy(0, sibling, me).wait_recv()
        for j, chip in enumerate(chips):
            copy(4 + j, (*chip, 1 - c), me).wait_recv()
        copy(0, me, sibling).wait_send()
        for j, chip in enumerate(chips):
            copy(1 + j, me, (*chip, c)).wait_send()
            copy(4 + j, (*chip, c), sibling).wait_send()
        whole = pltpu.make_async_copy(stage, outs[0], loc_sems.at[1])
        whole.start()
        whole.wait()


def _call(body, name, args, out_shape, grid=(), in_specs=(), out_specs=(), scratch_shapes=(), exchange=None,
          middle_at=0.8):
    if exchange is None:
        return pl.pallas_call(body, name=name, grid=grid, in_specs=list(in_specs), out_specs=list(out_specs),
                              out_shape=list(out_shape), scratch_shapes=list(scratch_shapes),
                              compiler_params=_params())(*args), None
    exs = list(exchange) if isinstance(exchange, (list, tuple)) else [exchange]
    ni, no, ns, nx = len(args), len(out_shape), len(scratch_shapes), sum(e.n for e in exs)
    steps = int(np.prod(grid))
    mid_step = min(max(int(steps * middle_at), 1), steps - 1)

    def wrapped(*refs):
        a, xi = refs[:ni], refs[ni:ni + nx]
        o, xo = refs[ni + nx:ni + nx + no], refs[ni + nx + no:ni + 2 * nx + no]
        s, xs = refs[ni + 2 * nx + no:ni + 2 * nx + no + ns], refs[ni + 2 * nx + no + ns:]
        parts, at, sat = [], 0, 0
        for e in exs:
            parts.append((e, xi[at:at + e.n], xo[at:at + e.n], xs[sat:sat + len(e.scratch)]))
            at, sat = at + e.n, sat + len(e.scratch)
        step = 0
        for d, g in enumerate(grid):
            step = step * g + pl.program_id(d)

        @pl.when(step == 0)
        def _():
            for e, ins, outs, sems in parts:
                e.start(ins, outs, sems)

        @pl.when(step == mid_step)
        def _():
            for e, ins, outs, sems in parts:
                e.middle(ins, outs, sems)

        body(*a, *o, *s)

        @pl.when(step == steps - 1)
        def _():
            for e, ins, outs, sems in parts:
                e.wait(ins, outs, sems)

    hbm = pl.BlockSpec(memory_space=pltpu.HBM)
    res = pl.pallas_call(
        wrapped, name=name, grid=grid, in_specs=list(in_specs) + [hbm] * nx, out_specs=list(out_specs) + [hbm] * nx,
        out_shape=list(out_shape) + [o_ for e in exs for o_ in e.out_shape],
        scratch_shapes=list(scratch_shapes) + [s_ for e in exs for s_ in e.scratch],
        compiler_params=_params())(*args, *[a_ for e in exs for a_ in e.arrs])
    return res[:no], res[no:]


def _gather_two_level(arrs, name):
    n = len(arrs)
    out_shape = [jax.ShapeDtypeStruct((N_DEV,) + a.shape, a.dtype) for a in arrs]

    def body(*refs):
        ins, outs = refs[:n], refs[n:2 * n]
        send_sems, recv_sems, loc_sems = refs[2 * n:]
        x, y, c = lax.axis_index("x"), lax.axis_index("y"), lax.axis_index("c")
        me, sibling = (x, y, c), (x, y, 1 - c)
        chips = [(1 - x, y), (x, 1 - y), (1 - x, 1 - y)]

        def copy(k, j, block, to, src=None):
            px, py, pc = block
            dst = outs[k].at[4 * px + 2 * py + pc]
            return pltpu.make_async_remote_copy(src_ref=dst if src is None else src, dst_ref=dst,
                                                send_sem=send_sems.at[k, j], recv_sem=recv_sems.at[k, j],
                                                device_id=to, device_id_type=pl.DeviceIdType.MESH)

        mine = [pltpu.make_async_copy(ins[k], outs[k].at[4 * x + 2 * y + c], loc_sems.at[k]) for k in range(n)]
        first = []
        for k in range(n):
            mine[k].start()
            first.append(copy(k, 0, me, sibling, src=ins[k]))
            first += [copy(k, 1 + j, me, (*chip, c), src=ins[k]) for j, chip in enumerate(chips)]
        for cp in first:
            cp.start()
        passed = []
        for j, chip in enumerate(chips):
            for k in range(n):
                copy(k, 1 + j, (*chip, c), me).wait_recv()
                passed.append(copy(k, 4 + j, (*chip, c), sibling))
                passed[-1].start()
        for k in range(n):
            copy(k, 0, sibling, me).wait_recv()
            for j, chip in enumerate(chips):
                copy(k, 4 + j, (*chip, 1 - c), me).wait_recv()
        for cp in first + passed:
            cp.wait_send()
        for cp in mine:
            cp.wait()

    vmem = pl.BlockSpec(memory_space=pltpu.VMEM)
    return pl.pallas_call(body, name=name, out_shape=out_shape, in_specs=[vmem] * n, out_specs=[vmem] * n,
                          scratch_shapes=[pltpu.SemaphoreType.DMA((n, 7)), pltpu.SemaphoreType.DMA((n, 7)),
                                          pltpu.SemaphoreType.DMA((n,))], compiler_params=_params())(*arrs)


def _exchange(arrs, scatter, name):
    ex = _Exchange(arrs, scatter)

    def body(*refs):
        ins, outs, sems = refs[:ex.n], refs[ex.n:2 * ex.n], refs[2 * ex.n:]
        ex.start(ins, outs, sems)
        ex.wait(ins, outs, sems)

    hbm = pl.BlockSpec(memory_space=pltpu.HBM)
    return pl.pallas_call(body, name=name, out_shape=ex.out_shape, in_specs=[hbm] * ex.n, out_specs=[hbm] * ex.n,
                          scratch_shapes=ex.scratch)(*ex.arrs)


def _matmul(a, b, mode, name, out_dtype=F32, tm=512, tn=1024, tk=1024, a_fn=None, b_fn=None, extras=(),
            out_slabs=None, exchange=None):
    assert not (a_fn and b_fn) and not (b_fn and mode == "NT")
    if mode == "NN":
        (M, K), N = a.shape, b.shape[1]
    elif mode == "NT":
        (M, K), N = a.shape, b.shape[0]
    else:
        (K, M), N = a.shape, b.shape[1]
    if out_slabs:
        tn = N // out_slabs
    tm, tn, tk = min(tm, M), min(tn, N), min(tk, K)
    assert M % tm == 0 and N % tn == 0 and K % tk == 0, (name, M, N, K)
    nk = K // tk
    dims = {"NN": NN, "NT": NT, "TN": TN}[mode]
    ne = len(extras)

    def body(a_ref, b_ref, *rest):
        e_refs, o_ref, acc_ref = rest[:ne], rest[ne], rest[ne + 1]
        k = pl.program_id(2)

        @pl.when(k == 0)
        def _():
            acc_ref[...] = jnp.zeros_like(acc_ref)

        at, bt = a_ref[...], b_ref[...]
        if a_fn is not None:
            at = a_fn(at.astype(F32), *[e[...] for e in e_refs])
        if b_fn is not None:
            bt = b_fn(bt.astype(F32), *[e[...] for e in e_refs])
        acc_ref[...] += _bdot(at, bt, dims)

        @pl.when(k == nk - 1)
        def _():
            o_ref[...] = acc_ref[...].astype(out_dtype)

    if mode == "TN":
        a_spec = pl.BlockSpec((tk, tm), lambda i, j, k: (k, i))
        e_spec = pl.BlockSpec((1, tm), lambda i, j, k: (0, i))
    else:
        a_spec = pl.BlockSpec((tm, tk), lambda i, j, k: (i, k))
        e_spec = pl.BlockSpec((1, tk), lambda i, j, k: (0, k))
    if mode == "NT":
        b_spec = pl.BlockSpec((tn, tk), lambda i, j, k: (j, k))
    else:
        b_spec = pl.BlockSpec((tk, tn), lambda i, j, k: (k, j))
    if b_fn is not None:
        e_spec = pl.BlockSpec((1, tn), lambda i, j, k: (0, j))
    if out_slabs:
        o_shape = jax.ShapeDtypeStruct((out_slabs, M, tn), out_dtype)
        o_spec = pl.BlockSpec((None, tm, tn), lambda i, j, k: (j, i, 0))
    else:
        o_shape = jax.ShapeDtypeStruct((M, N), out_dtype)
        o_spec = pl.BlockSpec((tm, tn), lambda i, j, k: (i, j))
    (out,), got = _call(body, name, (a, b, *extras), [o_shape], grid=(M // tm, N // tn, nk),
                        in_specs=[a_spec, b_spec] + [e_spec] * ne, out_specs=[o_spec],
                        scratch_shapes=[pltpu.VMEM((tm, tn), F32)], exchange=exchange)
    return out if exchange is None else (out, got)


def _modulate(x, sc, sh):
    return x * (1.0 + sc) + sh


def _square(x):
    return x * x


def _mod_part(c_all, w_ada_s, b_s):
    def body(c_ref, w_ref, b_ref, mod_ref, cond_ref):
        cv = c_ref[...]
        cond = cv * _sigmoid(cv)
        cond_ref[...] = cond
        mod_ref[...] = _bdot(cond, w_ref[...]) + b_ref[...]

    return pl.pallas_call(
        body, name="mod_part",
        out_shape=[jax.ShapeDtypeStruct((N_DEV, w_ada_s.shape[1]), F32), jax.ShapeDtypeStruct(c_all.shape, F32)],
        compiler_params=_params(),
    )(c_all, w_ada_s, b_s)


def _rms_fwd(x, w):
    rs = lax.rsqrt(_lanemean(x * x) + RMS_EPS)
    return x * rs * w, rs


def _rms_bwd(x, rs, w, dy):
    xhat = x * rs
    dxh = dy * w
    return rs * (dxh - xhat * _lanemean(dxh * xhat)), dy * xhat


def _mla_pre(z, pos_col, invf, m_one, m_rot, wq_ext, wkv_ext, qnw, kvnw):
    T = z.shape[0]
    tm = min(ROW_TILE, T)

    def body(z_ref, pos_ref, invf_ref, mone_ref, mrot_ref, wq_ref, wkv_ref, qnw_ref, kvnw_ref,
             q_ref, k_ref, v_ref, c1_ref, s1_ref, cqn_ref, ckvn_ref):
        ang = pos_ref[...].astype(F32) * invf_ref[...]
        c1 = mone_ref[...] + mrot_ref[...] * jnp.cos(ang)
        s1 = mrot_ref[...] * jnp.sin(ang)
        c1_ref[...] = c1
        s1_ref[...] = s1
        cqn, _ = _rms_fwd(z_ref[:, 0:256], qnw_ref[...])
        ckvn, _ = _rms_fwd(z_ref[:, 256:512], kvnw_ref[...])
        cqn_ref[...] = cqn.astype(BF16)
        ckvn_ref[...] = ckvn.astype(BF16)
        qe = _bdot(cqn, wq_ref[...], NT)
        kve = _bdot(ckvn, wkv_ref[...])
        k_rope = z_ref[:, 512:768] * c1 + z_ref[:, 768:1024] * s1
        for h in range(HEADS):
            q_ref[h] = ((qe[:, 256 * h:256 * h + 256] * c1 + qe[:, 1024 + 256 * h:1280 + 256 * h] * s1)
                        * Q_PRESCALE).astype(BF16)
            k_ref[h] = (kve[:, 256 * h:256 * h + 256] + k_rope).astype(BF16)
            v_ref[h] = kve[:, 1024 + 128 * h:1152 + 128 * h].astype(BF16)

    row = lambda i: (i, 0)
    head = lambda i: (0, i, 0)
    return pl.pallas_call(
        body, name="mla_pre", grid=(T // tm,),
        in_specs=[pl.BlockSpec((tm, 1024), lambda i: (i, 2)), pl.BlockSpec((tm, 1), row),
                  _full((1, 256)), _full((1, 256)), _full((1, 256)), _full(wq_ext.shape), _full(wkv_ext.shape),
                  _full((1, 256)), _full((1, 256))],
        out_specs=[pl.BlockSpec((HEADS, tm, QK_PAD), head), pl.BlockSpec((HEADS, tm, QK_PAD), head),
                   pl.BlockSpec((HEADS, tm, HEAD_DIM), head), pl.BlockSpec((tm, 256), row), pl.BlockSpec((tm, 256), row),
                   pl.BlockSpec((tm, 256), row), pl.BlockSpec((tm, 256), row)],
        out_shape=[jax.ShapeDtypeStruct((HEADS, T, QK_PAD), BF16), jax.ShapeDtypeStruct((HEADS, T, QK_PAD), BF16),
                   jax.ShapeDtypeStruct((HEADS, T, HEAD_DIM), BF16), jax.ShapeDtypeStruct((T, 256), F32),
                   jax.ShapeDtypeStruct((T, 256), F32), jax.ShapeDtypeStruct((T, 256), BF16),
                   jax.ShapeDtypeStruct((T, 256), BF16)],
        compiler_params=_params(),
    )(z, pos_col, invf, m_one, m_rot, wq_ext, wkv_ext, qnw, kvnw)


def _mla_bwd(dq, dk, dv, z, c1, s1, wq_ext, wkv_ext, qnw, kvnw):
    T = z.shape[0]
    tm = min(ROW_TILE_SMALL, T)

    def body(dq_ref, dk_ref, dv_ref, z_ref, c1_ref, s1_ref, wq_ref, wkv_ref, qnw_ref, kvnw_ref,
             dz_ref, dqe_ref, dkve_ref, dqnw_ref, dkvnw_ref):
        @pl.when(pl.program_id(0) == 0)
        def _():
            dqnw_ref[...] = jnp.zeros_like(dqnw_ref)
            dkvnw_ref[...] = jnp.zeros_like(dkvnw_ref)

        c1, s1 = c1_ref[...], s1_ref[...]
        dkpe = jnp.zeros((tm, QK_PAD), F32)
        for h in range(HEADS):
            dqh, dkh = dq_ref[h] * Q_PRESCALE, dk_ref[h]
            dqe_ref[:, 256 * h:256 * h + 256] = (dqh * c1).astype(BF16)
            dqe_ref[:, 1024 + 256 * h:1280 + 256 * h] = (dqh * s1).astype(BF16)
            dkve_ref[:, 256 * h:256 * h + 256] = dkh.astype(BF16)
            dkve_ref[:, 1024 + 128 * h:1152 + 128 * h] = dv_ref[h].astype(BF16)
            dkpe = dkpe + dkh
        dcqn = _dot(dqe_ref[...], wq_ref[...])
        dckvn = _dot(dkve_ref[...], wkv_ref[...], NT)
        cq, ckv = z_ref[:, 0:256], z_ref[:, 256:512]
        _, rsq = _rms_fwd(cq, qnw_ref[...])
        _, rskv = _rms_fwd(ckv, kvnw_ref[...])
        dcq, wq_rows = _rms_bwd(cq, rsq, qnw_ref[...], dcqn)
        dckv, wkv_rows = _rms_bwd(ckv, rskv, kvnw_ref[...], dckvn)
        dqnw_ref[...] += _rowsum(wq_rows)
        dkvnw_ref[...] += _rowsum(wkv_rows)
        dz_ref[:, 0:256] = dcq
        dz_ref[:, 256:512] = dckv
        dz_ref[:, 512:768] = dkpe * c1
        dz_ref[:, 768:1024] = dkpe * s1

    row = lambda i: (i, 0)
    head = lambda i: (0, i, 0)
    return pl.pallas_call(
        body, name="mla_bwd", grid=(T // tm,),
        in_specs=[pl.BlockSpec((HEADS, tm, QK_PAD), head), pl.BlockSpec((HEADS, tm, QK_PAD), head),
                  pl.BlockSpec((HEADS, tm, HEAD_DIM), head), pl.BlockSpec((tm, 1024), lambda i: (i, 2)),
                  pl.BlockSpec((tm, 256), row), pl.BlockSpec((tm, 256), row), _full(wq_ext.shape), _full(wkv_ext.shape),
                  _full((1, 256)), _full((1, 256))],
        out_specs=[pl.BlockSpec((tm, 1024), row), pl.BlockSpec((tm, 2048), row), pl.BlockSpec((tm, 1536), row),
                   _full((1, 256)), _full((1, 256))],
        out_shape=[jax.ShapeDtypeStruct((T, 1024), F32), jax.ShapeDtypeStruct((T, 2048), BF16),
                   jax.ShapeDtypeStruct((T, 1536), BF16), jax.ShapeDtypeStruct((1, 256), F32),
                   jax.ShapeDtypeStruct((1, 256), F32)],
        compiler_params=_params(),
    )(dq, dk, dv, z, c1, s1, wq_ext, wkv_ext, qnw, kvnw)


_HEAD_LANES = [slice(HEAD_DIM * h, HEAD_DIM * (h + 1)) for h in range(HEADS)]


def _lower_bound(lbraw_ref):
    a0, a1 = lbraw_ref[0:1, :], lbraw_ref[1:2, :]
    mx = jnp.maximum(a0, a1)
    e0, e1 = jnp.exp(a0 - mx), jnp.exp(a1 - mx)
    return e0 / (e0 + e1)


def _tri(lower):
    r = lax.broadcasted_iota(jnp.int32, (CHUNK, CHUNK), 0)
    c = lax.broadcasted_iota(jnp.int32, (CHUNK, CHUNK), 1)
    return (r >= c) if lower else (r <= c)


def _hgrn_gates(q, f, lb, tri_lo):
    sg = _sigmoid(f)
    forget = lb + (1.0 - lb) * sg
    k = 1.0 - forget
    b = _hdot(tri_lo.astype(F32), jnp.log(forget))
    b_ref, b_last = b[CHUNK // 2 - 1:CHUNK // 2, :], b[CHUNK - 1:CHUNK, :]
    e1, e2, e3, e4 = jnp.exp(b - b_ref), jnp.exp(b_ref - b), jnp.exp(b_last - b), jnp.exp(b)
    return dict(sg=sg, forget=forget, k=k, e1=e1, e2=e2, e3=e3, e4=e4, qa=q * e1, ka=k * e2, kl=k * e3, qb=q * e4,
                decay=jnp.exp(b_last))


def _hgrn_fwd(z, lbraw, nw, exchange=None):
    T = z.shape[0]
    G = min(HGRN_GROUP, T // CHUNK)
    rows = G * CHUNK
    n_chunks = T // CHUNK

    def body(q_ref, f_ref, i_ref, g_ref, lbraw_ref, nw_ref, oraw_ref, og_ref, sp_ref, st_ref):
        @pl.when(pl.program_id(0) == 0)
        def _():
            st_ref[...] = jnp.zeros_like(st_ref)

        lb_all = _lower_bound(lbraw_ref)
        tri_lo = _tri(True)

        def chunk(cc, carry):
            rs = pl.ds(pl.multiple_of(cc * CHUNK, CHUNK), CHUNK)
            t = _hgrn_gates(q_ref[rs, :], f_ref[rs, :], lb_all, tri_lo)
            v, gate = i_ref[rs, :], g_ref[rs, :]
            st = [st_ref[h] for h in range(HEADS)]
            a = [jnp.where(tri_lo, _bdot(t["qa"][:, s], t["ka"][:, s], NT), 0.0) for s in _HEAD_LANES]
            kv = [_bdot(v[:, s], t["kl"][:, s], TN) for s in _HEAD_LANES]
            o = [_bdot(a[h], v[:, s]) + _bdot(t["qb"][:, s], st[h], NT) for h, s in enumerate(_HEAD_LANES)]
            for h, s in enumerate(_HEAD_LANES):
                sp_ref[cc, h] = st[h]
                st_ref[h] = st[h] * t["decay"][:, s] + kv[h]
            oraw_ref[rs, :] = jnp.concatenate(o, axis=1)
            on = jnp.concatenate([_rms_fwd(o[h], nw_ref[:, s])[0] for h, s in enumerate(_HEAD_LANES)], axis=1)
            og_ref[rs, :] = (on * (gate * _sigmoid(gate))).astype(BF16)
            return carry

        lax.fori_loop(0, G, chunk, 0, unroll=2)

    col = lambda j: pl.BlockSpec((rows, 512), lambda r, j=j: (r, j))
    return _call(
        body, "hgrn_fwd", (z, z, z, z, lbraw, nw), grid=(T // rows,),
        in_specs=[col(0), col(1), col(2), col(3), _full((2, 512)), _full((1, 512))],
        out_specs=[col(0), col(0), pl.BlockSpec((G, HEADS, HEAD_DIM, HEAD_DIM), lambda r: (r, 0, 0, 0))],
        out_shape=[jax.ShapeDtypeStruct((T, 512), F32), jax.ShapeDtypeStruct((T, 512), BF16),
                   jax.ShapeDtypeStruct((n_chunks, HEADS, HEAD_DIM, HEAD_DIM), F32)],
        scratch_shapes=[pltpu.VMEM((HEADS, HEAD_DIM, HEAD_DIM), F32)], exchange=exchange)


def _hgrn_bwd(dmixcat, z, oraw, sprev, lbraw, nw, exchange=None):
    T = z.shape[0]
    G = min(HGRN_GROUP, T // CHUNK)
    rows = G * CHUNK
    ng = T // rows

    def body(dog_ref, q_ref, f_ref, i_ref, g_ref, oraw_ref, sp_ref, lbraw_ref, nw_ref,
             dz_ref, dlb_ref, dnw_ref, dst_ref):
        @pl.when(pl.program_id(0) == 0)
        def _():
            dst_ref[...] = jnp.zeros_like(dst_ref)
            dlb_ref[...] = jnp.zeros_like(dlb_ref)
            dnw_ref[...] = jnp.zeros_like(dnw_ref)

        lb_all = _lower_bound(lbraw_ref)
        tri_lo, tri_up = _tri(True), _tri(False)
        rowid = lax.broadcasted_iota(jnp.int32, (CHUNK, HEADS * HEAD_DIM), 0)

        def chunk(it, carry):
            cc = G - 1 - it
            rs = pl.ds(pl.multiple_of(cc * CHUNK, CHUNK), CHUNK)
            heads = list(enumerate(_HEAD_LANES))
            cat = lambda parts: jnp.concatenate(parts, axis=1)
            per_head_mean = lambda x: cat([jnp.broadcast_to(_lanemean(x[:, s]), (CHUNK, HEAD_DIM)) for s in _HEAD_LANES])
            t = _hgrn_gates(q_ref[rs, :], f_ref[rs, :], lb_all, tri_lo)
            v, gate, o, dog, nw_all = i_ref[rs, :], g_ref[rs, :], oraw_ref[rs, :], dog_ref[rs, :], nw_ref[...]
            rs_o = lax.rsqrt(per_head_mean(o * o) + RMS_EPS)
            xhat = o * rs_o
            sgg = _sigmoid(gate)
            d_on = dog * (gate * sgg)
            dz_ref[rs, 1536:2048] = dog * (xhat * nw_all) * (sgg * (1.0 + gate * (1.0 - sgg)))
            dxh = d_on * nw_all
            do = rs_o * (dxh - xhat * per_head_mean(dxh * xhat))
            dnw_ref[...] += _rowsum(d_on * xhat)
            st = [sp_ref[cc, h] for h in range(HEADS)]
            dst = [dst_ref[h] for h in range(HEADS)]
            a = [jnp.where(tri_lo, _bdot(t["qa"][:, s], t["ka"][:, s], NT), 0.0) for s in _HEAD_LANES]
            da = [jnp.where(tri_lo, _bdot(do[:, s], v[:, s], NT), 0.0) for s in _HEAD_LANES]
            dqb = cat([_bdot(do[:, s], st[h]) for h, s in heads])
            dkl = cat([_bdot(v[:, s], dst[h]) for h, s in heads])
            dv_ = cat([_bdot(t["kl"][:, s], dst[h], NT) + _bdot(a[h], do[:, s], TN) for h, s in heads])
            dqa = cat([_bdot(da[h], t["ka"][:, s]) for h, s in heads])
            dka = cat([_bdot(da[h], t["qa"][:, s], TN) for h, s in heads])
            ddecay = cat([_rowsum(dst[h] * st[h]) for h in range(HEADS)])
            for h, s in heads:
                dst_ref[h] = dst[h] * t["decay"][:, s] + _bdot(do[:, s], t["qb"][:, s], TN)
            pa, pk, pb, pl_ = dqa * t["qa"], dka * t["ka"], dqb * t["qb"], dkl * t["kl"]
            db = pa - pk + pb - pl_
            db = db + jnp.where(rowid == CHUNK // 2 - 1, _rowsum(pk - pa), 0.0)
            db = db + jnp.where(rowid == CHUNK - 1, _rowsum(pl_) + ddecay * t["decay"], 0.0)
            dlogf = _hdot(tri_up.astype(F32), db)
            dforget = dlogf / t["forget"] - (dka * t["e2"] + dkl * t["e3"])
            sg = t["sg"]
            dz_ref[rs, 0:512] = dqa * t["e1"] + dqb * t["e4"]
            dz_ref[rs, 512:1024] = dforget * (1.0 - lb_all) * sg * (1.0 - sg)
            dz_ref[rs, 1024:1536] = dv_
            dlb_ref[...] += _rowsum(dforget * (1.0 - sg))
            return carry

        lax.fori_loop(0, G, chunk, 0, unroll=2)

    col = lambda j: pl.BlockSpec((rows, 512), lambda r, j=j: (ng - 1 - r, j))
    return _call(
        body, "hgrn_bwd", (dmixcat, z, z, z, z, oraw, sprev, lbraw, nw), grid=(ng,),
        in_specs=[col(0), col(0), col(1), col(2), col(3), col(0),
                  pl.BlockSpec((G, HEADS, HEAD_DIM, HEAD_DIM), lambda r: (ng - 1 - r, 0, 0, 0)),
                  _full((2, 512)), _full((1, 512))],
        out_specs=[pl.BlockSpec((rows, 2048), lambda r: (ng - 1 - r, 0)), _full((1, 512)), _full((1, 512))],
        out_shape=[jax.ShapeDtypeStruct((T, 2048), F32), jax.ShapeDtypeStruct((1, 512), F32),
                   jax.ShapeDtypeStruct((1, 512), F32)],
        scratch_shapes=[pltpu.VMEM((HEADS, HEAD_DIM, HEAD_DIM), F32)], exchange=exchange)


def _diag_mask(t):
    r = lax.broadcasted_iota(jnp.int32, (t, t), 0)
    c = lax.broadcasted_iota(jnp.int32, (t, t), 1)
    return r >= c


def _attn_fwd(q, k, v, exchange=None):
    _, T, _ = q.shape
    t = min(ATT_TILE, T)

    def body(q_ref, k_ref, v_ref, o_ref, lse_ref):
        i = pl.program_id(1)
        qb = q_ref[...]

        def step(j, carry, masked):
            m, l, acc = carry
            ks = pl.ds(pl.multiple_of(j * t, t), t)
            s = _dot(qb, k_ref[ks, :], NT)
            if masked:
                s = jnp.where(_diag_mask(t), s, NEG_BIG)
            mn = jnp.maximum(m, jnp.max(s, axis=-1, keepdims=True))
            p = jnp.exp2(s - mn)
            al = jnp.exp2(m - mn)
            return mn, al * l + jnp.sum(p, axis=-1, keepdims=True), al * acc + _dot(p.astype(BF16), v_ref[ks, :])

        init = (jnp.full((t, 1), NEG_BIG, F32), jnp.zeros((t, 1), F32), jnp.zeros((t, HEAD_DIM), F32))
        carry = lax.fori_loop(0, i, lambda j, c: step(j, c, False), init)
        m, l, acc = step(i, carry, True)
        o_ref[...] = acc / l
        lse_ref[...] = jnp.broadcast_to(m + jnp.log2(l), (t, HEAD_DIM))

    return _call(
        body, "attn_fwd", (q, k, v), grid=(HEADS, T // t),
        in_specs=[pl.BlockSpec((None, t, QK_PAD), lambda h, i: (h, i, 0)),
                  pl.BlockSpec((None, T, QK_PAD), lambda h, i: (h, 0, 0)),
                  pl.BlockSpec((None, T, HEAD_DIM), lambda h, i: (h, 0, 0))],
        out_specs=[pl.BlockSpec((t, HEAD_DIM), lambda h, i: (i, h)),
                   pl.BlockSpec((None, t, HEAD_DIM), lambda h, i: (h, i, 0))],
        out_shape=[jax.ShapeDtypeStruct((T, HEADS * HEAD_DIM), F32), jax.ShapeDtypeStruct((HEADS, T, HEAD_DIM), F32)],
        exchange=exchange)


def _attn_bwd(q, k, v, dmixcat, o, lse, exchange=None):
    _, T, _ = q.shape
    t = min(ATT_TILE, T)
    nq = T // t

    def body(q_ref, k_ref, v_ref, do_ref, o_ref, lse_ref, dq_ref, dk_ref, dv_ref, delta_ref):
        j = pl.program_id(1)

        @pl.when(j == 0)
        def _():
            dq_ref[...] = jnp.zeros_like(dq_ref)

            def fill(i, carry):
                rs = pl.ds(pl.multiple_of(i * t, t), t)
                delta_ref[rs, :] = jnp.broadcast_to(
                    jnp.sum(do_ref[rs, :] * o_ref[rs, :], axis=-1, keepdims=True), (t, HEAD_DIM))
                return carry

            lax.fori_loop(0, nq, fill, 0)

        kb, vb = k_ref[...], v_ref[...]

        def step(i, carry, masked):
            dk, dv = carry
            rs = pl.ds(pl.multiple_of(i * t, t), t)
            qb, dob = q_ref[rs, :], do_ref[rs, :].astype(BF16)
            p = jnp.exp2(_dot(qb, kb, NT) - lse_ref[rs, 0:1])
            if masked:
                p = jnp.where(_diag_mask(t), p, 0.0)
            dp = _dot(dob, vb, NT)
            ds = (p * (dp - delta_ref[rs, 0:1]) * LN2).astype(BF16)
            dq_ref[rs, :] += _dot(ds, kb)
            return dk + _dot(ds, qb, TN), dv + _dot(p.astype(BF16), dob, TN)

        carry = step(j, (jnp.zeros((t, QK_PAD), F32), jnp.zeros((t, HEAD_DIM), F32)), True)
        dk, dv = lax.fori_loop(j + 1, nq, lambda i, c: step(i, c, False), carry)
        dk_ref[...] = dk
        dv_ref[...] = dv

    return _call(
        body, "attn_bwd", (q, k, v, dmixcat, o, lse), grid=(HEADS, nq),
        in_specs=[pl.BlockSpec((None, T, QK_PAD), lambda h, j: (h, 0, 0)),
                  pl.BlockSpec((None, t, QK_PAD), lambda h, j: (h, j, 0)),
                  pl.BlockSpec((None, t, HEAD_DIM), lambda h, j: (h, j, 0)),
                  pl.BlockSpec((T, HEAD_DIM), lambda h, j: (0, HEADS + h)),
                  pl.BlockSpec((T, HEAD_DIM), lambda h, j: (0, h)),
                  pl.BlockSpec((None, T, HEAD_DIM), lambda h, j: (h, 0, 0))],
        out_specs=[pl.BlockSpec((None, T, QK_PAD), lambda h, j: (h, 0, 0)),
                   pl.BlockSpec((None, t, QK_PAD), lambda h, j: (h, j, 0)),
                   pl.BlockSpec((None, t, HEAD_DIM), lambda h, j: (h, j, 0))],
        out_shape=[jax.ShapeDtypeStruct((HEADS, T, QK_PAD), F32), jax.ShapeDtypeStruct((HEADS, T, QK_PAD), F32),
                   jax.ShapeDtypeStruct((HEADS, T, HEAD_DIM), F32)],
        scratch_shapes=[pltpu.VMEM((T, HEAD_DIM), F32)], exchange=exchange)


def _ln_fwd(r):
    mu = _lanemean(r)
    xc = r - mu
    rstd = lax.rsqrt(_lanemean(xc * xc) + LN_EPS)
    return xc * rstd, rstd


def _ln_bwd(dxh, xhat, rstd):
    return rstd * (dxh - _lanemean(dxh) - xhat * _lanemean(dxh * xhat))


def _mix_ln1(o_hg, o_mla, w_out, x, g_a, ln1_g, ln1_b, sc_m, sh_m, exchange=None):
    T = x.shape[0]
    tm = min(ROW_TILE_SMALL, T)
    half = o_hg.shape[1]

    def body(hg_ref, mla_ref, w_ref, x_ref, ga_ref, g_ref, b_ref, sc_ref, sh_ref, mix_ref, xhat_ref, rstd_ref, u2_ref):
        mix = _dot(hg_ref[...], w_ref[0:half, :]) + _bdot(mla_ref[...], w_ref[half:, :])
        mix_ref[...] = mix
        xhat, rstd = _ln_fwd(ALPHA * x_ref[...] + (1.0 + ga_ref[...]) * mix)
        xhat_ref[...] = xhat
        rstd_ref[...] = jnp.broadcast_to(rstd, (tm, 128))
        u2_ref[...] = _modulate(xhat * g_ref[...] + b_ref[...], sc_ref[...], sh_ref[...]).astype(BF16)

    row = pl.BlockSpec((tm, D_MODEL), lambda i: (i, 0))
    vec = _full((1, D_MODEL))
    halfrow = pl.BlockSpec((tm, half), lambda i: (i, 0))
    return _call(
        body, "mix_ln1", (o_hg, o_mla, w_out, x, g_a, ln1_g, ln1_b, sc_m, sh_m), grid=(T // tm,),
        in_specs=[halfrow, halfrow, _full(w_out.shape), row, vec, vec, vec, vec, vec],
        out_specs=[row, row, pl.BlockSpec((tm, 128), lambda i: (i, 0)), row],
        out_shape=[jax.ShapeDtypeStruct((T, D_MODEL), F32), jax.ShapeDtypeStruct((T, D_MODEL), F32),
                   jax.ShapeDtypeStruct((T, 128), F32), jax.ShapeDtypeStruct((T, D_MODEL), BF16)],
        exchange=exchange)


def _mlp_fwd(u2, w1, w2, xhat1, ln1_g, ln1_b, g_m, ln2_g, ln2_b, target):
    T = u2.shape[0]
    tf = w1.shape[-1]
    nf = N_DEV // MLP_SLABS
    tm = min(ROW_TILE, T)

    def body(u2_ref, w1_ref, w2_ref, xhat_ref, g1_ref, b1_ref, gm_ref, g2_ref, b2_ref, tgt_ref,
             r_ref, dr2_ref, dh_ref, dg2_ref, db2_ref, dgm_ref, loss_ref, acc_ref):
        i, f = pl.program_id(0), pl.program_id(1)

        @pl.when((i == 0) & (f == 0))
        def _():
            for ref in (dg2_ref, db2_ref, dgm_ref, loss_ref):
                ref[...] = jnp.zeros_like(ref)

        @pl.when(f == 0)
        def _():
            acc_ref[...] = jnp.zeros_like(acc_ref)

        u2t = u2_ref[...]
        part = None
        for s in range(MLP_SLABS):
            r = jnp.maximum(_dot(u2t, w1_ref[s]), 0.0)
            r_ref[:, s * tf:(s + 1) * tf] = r.astype(BF16)
            d = _bdot(r * r, w2_ref[s])
            part = d if part is None else part + d
        acc_ref[...] += part

        @pl.when(f == nf - 1)
        def _():
            h = acc_ref[...]
            x1 = xhat_ref[...] * g1_ref[...] + b1_ref[...]
            xhat2, rstd2 = _ln_fwd(ALPHA * x1 + (1.0 + gm_ref[...]) * h)
            err = xhat2 * g2_ref[...] + b2_ref[...] - tgt_ref[...]
            loss_ref[...] += jnp.sum(0.5 * _lanemean(err * err), axis=0, keepdims=True)
            dy = err * (1.0 / D_MODEL)
            dg2_ref[...] += _rowsum(dy * xhat2)
            db2_ref[...] += _rowsum(dy)
            dr2 = _ln_bwd(dy * g2_ref[...], xhat2, rstd2)
            dr2_ref[...] = dr2
            dgm_ref[...] += _rowsum(dr2 * h)
            dh_ref[...] = ((1.0 + gm_ref[...]) * dr2).astype(BF16)

    row = pl.BlockSpec((tm, D_MODEL), lambda i, f: (i, 0))
    vec = _full((1, D_MODEL))
    return pl.pallas_call(
        body, name="mlp_fwd", grid=(T // tm, nf),
        in_specs=[row, pl.BlockSpec((MLP_SLABS, D_MODEL, tf), lambda i, f: (f, 0, 0)),
                  pl.BlockSpec((MLP_SLABS, tf, D_MODEL), lambda i, f: (f, 0, 0)),
                  row, vec, vec, vec, vec, vec, row],
        out_specs=[pl.BlockSpec((tm, MLP_SLABS * tf), lambda i, f: (i, f)), row, row, vec, vec, vec, _full((1, 128))],
        out_shape=[jax.ShapeDtypeStruct((T, N_DEV * tf), BF16), jax.ShapeDtypeStruct((T, D_MODEL), F32),
                   jax.ShapeDtypeStruct((T, D_MODEL), BF16), jax.ShapeDtypeStruct((1, D_MODEL), F32),
                   jax.ShapeDtypeStruct((1, D_MODEL), F32), jax.ShapeDtypeStruct((1, D_MODEL), F32),
                   jax.ShapeDtypeStruct((1, 128), F32)],
        scratch_shapes=[pltpu.VMEM((tm, D_MODEL), F32)],
        compiler_params=_params(),
    )(u2, w1, w2, xhat1, ln1_g, ln1_b, g_m, ln2_g, ln2_b, target)


def _mlp_bwd(dh, w1, w2, r, dr2, xhat1, rstd1, mix, ln1_g, ln1_b, sc_m, g_a):
    T = dh.shape[0]
    tf = w1.shape[-1]
    nf = N_DEV // MLP_SLABS
    tm = min(ROW_TILE, T)

    def body(dh_ref, w1_ref, w2_ref, r_ref, dr2_ref, xhat_ref, rstd_ref, mix_ref, g1_ref, b1_ref, sc_ref, ga_ref,
             dhpre_ref, dr1_ref, dmix_ref, dsc_ref, dsh_ref, dg1_ref, db1_ref, dga_ref, acc_ref):
        i, f = pl.program_id(0), pl.program_id(1)

        @pl.when((i == 0) & (f == 0))
        def _():
            for ref in (dsc_ref, dsh_ref, dg1_ref, db1_ref, dga_ref):
                ref[...] = jnp.zeros_like(ref)

        @pl.when(f == 0)
        def _():
            acc_ref[...] = jnp.zeros_like(acc_ref)

        dht = dh_ref[...]
        part = None
        for s in range(MLP_SLABS):
            cols = slice(s * tf, (s + 1) * tf)
            dhpre = (_dot(dht, w2_ref[s], NT) * (2.0 * r_ref[:, cols].astype(F32))).astype(BF16)
            dhpre_ref[:, cols] = dhpre
            d = _dot(dhpre, w1_ref[s], NT)
            part = d if part is None else part + d
        acc_ref[...] += part

        @pl.when(f == nf - 1)
        def _():
            du2 = acc_ref[...]
            xhat = xhat_ref[...]
            x1 = xhat * g1_ref[...] + b1_ref[...]
            dx1 = ALPHA * dr2_ref[...] + du2 * (1.0 + sc_ref[...])
            dsc_ref[...] += _rowsum(du2 * x1)
            dsh_ref[...] += _rowsum(du2)
            dg1_ref[...] += _rowsum(dx1 * xhat)
            db1_ref[...] += _rowsum(dx1)
            dr1 = _ln_bwd(dx1 * g1_ref[...], xhat, rstd_ref[:, 0:1])
            dr1_ref[...] = dr1
            dga_ref[...] += _rowsum(dr1 * mix_ref[...])
            dmix_ref[...] = ((1.0 + ga_ref[...]) * dr1).astype(BF16)

    row = pl.BlockSpec((tm, D_MODEL), lambda i, f: (i, 0))
    vec = _full((1, D_MODEL))
    return pl.pallas_call(
        body, name="mlp_bwd", grid=(T // tm, nf),
        in_specs=[row, pl.BlockSpec((MLP_SLABS, D_MODEL, tf), lambda i, f: (f, 0, 0)),
                  pl.BlockSpec((MLP_SLABS, tf, D_MODEL), lambda i, f: (f, 0, 0)),
                  pl.BlockSpec((tm, MLP_SLABS * tf), lambda i, f: (i, f)), row, row,
                  pl.BlockSpec((tm, 128), lambda i, f: (i, 0)), row, vec, vec, vec, vec],
        out_specs=[pl.BlockSpec((tm, MLP_SLABS * tf), lambda i, f: (i, f)), row, row, vec, vec, vec, vec, vec],
        out_shape=[jax.ShapeDtypeStruct((T, N_DEV * tf), BF16), jax.ShapeDtypeStruct((T, D_MODEL), F32),
                   jax.ShapeDtypeStruct((T, D_MODEL), BF16)] + [jax.ShapeDtypeStruct((1, D_MODEL), F32)] * 5,
        scratch_shapes=[pltpu.VMEM((tm, D_MODEL), F32)],
        compiler_params=_params(),
    )(dh, w1, w2, r, dr2, xhat1, rstd1, mix, ln1_g, ln1_b, sc_m, g_a)


def _input_bwd(dz_h, dz_m, w_in_ext, x, dr1, sc_a, exchange=None):
    T = x.shape[0]
    tm = min(ROW_TILE_SMALL, T)

    def body(dzh_ref, dzm_ref, w_ref, x_ref, dr1_ref, sc_ref, gx_ref, dsc_ref, dsh_ref):
        @pl.when(pl.program_id(0) == 0)
        def _():
            dsc_ref[...] = jnp.zeros_like(dsc_ref)
            dsh_ref[...] = jnp.zeros_like(dsh_ref)

        du = _bdot(dzh_ref[...], w_ref[0:2048, :]) + _bdot(dzm_ref[...], w_ref[2048:3072, :])
        gx_ref[...] = ALPHA * dr1_ref[...] + du * (1.0 + sc_ref[...])
        dsc_ref[...] += _rowsum(du * x_ref[...])
        dsh_ref[...] += _rowsum(du)

    row = pl.BlockSpec((tm, D_MODEL), lambda i: (i, 0))
    vec = _full((1, D_MODEL))
    return _call(
        body, "input_bwd", (dz_h, dz_m, w_in_ext, x, dr1, sc_a), grid=(T // tm,),
        in_specs=[pl.BlockSpec((tm, 2048), lambda i: (i, 0)), row, _full(w_in_ext.shape), row, row, vec],
        out_specs=[row, vec, vec],
        out_shape=[jax.ShapeDtypeStruct((T, D_MODEL), F32), jax.ShapeDtypeStruct((1, D_MODEL), F32),
                   jax.ShapeDtypeStruct((1, D_MODEL), F32)], exchange=exchange)


def _adam_math(w, g, m, v):
    m = ADAM_B1 * m + (1.0 - ADAM_B1) * g
    v = ADAM_B2 * v + (1.0 - ADAM_B2) * (g * g)
    m_hat = m / (1.0 - ADAM_B1 ** ADAM_STEP)
    v_hat = v / (1.0 - ADAM_B2 ** ADAM_STEP)
    return -ADAM_LR * (m_hat / (jnp.sqrt(v_hat) + ADAM_EPS) + ADAM_WD * w), m, v


def _adam(g_slabs, w, m, v, name, g_fn=None, g_extra=()):
    R, C = w.shape
    tr = 256 if R % 256 == 0 else R
    ns = 0 if g_slabs is None else g_slabs.shape[0]
    ne = len(g_extra)

    def body(*refs):
        e_refs = refs[:ne]
        refs = refs[ne:]
        if ns:
            gs_ref, refs = refs[0], refs[1:]
        w_ref, m_ref, v_ref, g_ref, d_ref, nm_ref, nv_ref = refs
        if g_fn is not None:
            g = g_fn(*e_refs)
        else:
            g = gs_ref[0].astype(F32)
            for s in range(1, ns):
                g = g + gs_ref[s].astype(F32)
        d, nm, nv = _adam_math(w_ref[...], g, m_ref[...], v_ref[...])
        g_ref[...] = g
        d_ref[...] = d
        nm_ref[...] = nm
        nv_ref[...] = nv

    blk = pl.BlockSpec((tr, C), lambda i: (i, 0))
    in_specs = [pl.BlockSpec((tr, e.shape[1]), lambda i: (i, 0)) if e.shape[0] == R else _full(e.shape) for e in g_extra]
    args = list(g_extra)
    if ns:
        in_specs.append(pl.BlockSpec((ns, tr, C), lambda i: (0, i, 0)))
        args.append(g_slabs)
    return pl.pallas_call(
        body, name=name, grid=(R // tr,), in_specs=in_specs + [blk] * 3, out_specs=[blk] * 4,
        out_shape=[jax.ShapeDtypeStruct((R, C), F32)] * 4, compiler_params=_params(),
    )(*args, w, m, v)


def _adam_small(small_all, params):
    n = len(params)

    def body(*refs):
        s_ref, refs = refs[0], refs[1:]
        wmv, loss_ref, outs = refs[:3 * n], refs[3 * n], refs[3 * n + 1:]
        tot = s_ref[0]
        for i in range(1, N_DEV):
            tot = tot + s_ref[i]
        loss_ref[...] = tot[:, SMALL_W - 128:]
        for j, (w, _, _, off) in enumerate(params):
            w_ref, m_ref, v_ref = wmv[3 * j:3 * j + 3]
            g_ref, d_ref, nm_ref, nv_ref = outs[4 * j:4 * j + 4]
            if w.shape[0] == 2:
                lb = _lower_bound(w_ref)
                g0 = tot[:, off:off + w.shape[1]] * lb * (1.0 - lb)
                rows = [(slice(0, 1), g0), (slice(1, 2), -g0)]
            else:
                rows = [(slice(0, 1), tot[:, off:off + w.shape[1]])]
            for rs, g in rows:
                d, nm, nv = _adam_math(w_ref[rs, :], g, m_ref[rs, :], v_ref[rs, :])
                g_ref[rs, :], d_ref[rs, :], nm_ref[rs, :], nv_ref[rs, :] = g, d, nm, nv

    out_shape = [jax.ShapeDtypeStruct((1, 128), F32)]
    for w, _, _, _ in params:
        out_shape += [jax.ShapeDtypeStruct(w.shape, F32)] * 4
    res = pl.pallas_call(body, name="adam_small", out_shape=out_shape, compiler_params=_params())(
        small_all, *[a for w, m, v, _ in params for a in (w, m, v)])
    return res[0], [tuple(res[1 + 4 * j:5 + 4 * j]) for j in range(n)]


def _cols_from_slabs(g):
    s, r, c = g.shape
    return jnp.transpose(g, (1, 0, 2)).reshape(r, s * c)


def _slabs_from_cols(w):
    r, c = w.shape
    return jnp.transpose(w.reshape(r, N_DEV, c // N_DEV), (1, 0, 2))


def _rot_half_rows(wt):
    return jnp.concatenate([-wt[32:], wt[:32]], axis=0)


def _unrot_half_rows(dwt_rot):
    return jnp.concatenate([dwt_rot[32:], -dwt_rot[:32]], axis=0)


def _ext_in_t(g):
    k_in = g.shape[2]
    z64, z128 = jnp.zeros((64, k_in), BF16), jnp.zeros((128, k_in), BF16)
    last = g.shape[1] - ROPE_DIM
    wk = g[N_DEV - 1, last:]
    return jnp.concatenate([g[i] for i in range(N_DEV - 1)] + [g[N_DEV - 1, :last], z128, wk, z64, z128,
                                                               _rot_half_rows(wk), z64], axis=0)


def _ext_q_t(wt):
    r = wt.shape[1]
    z64, z128 = jnp.zeros((64, r), BF16), jnp.zeros((128, r), BF16)
    per = HEAD_DIM + ROPE_DIM
    main = [jnp.concatenate([wt[per * h:per * (h + 1)], z64], axis=0) for h in range(HEADS)]
    rot = [jnp.concatenate([z128, _rot_half_rows(wt[per * h + HEAD_DIM:per * (h + 1)]), z64], axis=0)
           for h in range(HEADS)]
    return jnp.concatenate(main + rot, axis=0)


def _ext_kv(w_kv_up):
    r = w_kv_up.shape[0]
    z128 = jnp.zeros((r, 128), BF16)
    wkv = w_kv_up.reshape(r, HEADS, 2 * HEAD_DIM)
    kpad = [jnp.concatenate([wkv[:, h, :HEAD_DIM], z128], axis=1) for h in range(HEADS)]
    vals = [wkv[:, h, HEAD_DIM:] for h in range(HEADS)]
    return jnp.concatenate(kpad + vals, axis=1)


def _grad_in_from_ext_t(dwt_h, dwt_m):
    dwk = dwt_m[512 + 128:512 + 192] + _unrot_half_rows(dwt_m[768 + 128:768 + 192])
    return jnp.concatenate([dwt_h, dwt_m[:512], dwk], axis=0)


def _grad_q_from_ext_t(dwq_ext_t):
    rows = []
    for h in range(HEADS):
        main, rot = dwq_ext_t[256 * h:256 * h + 256], dwq_ext_t[1024 + 256 * h:1280 + 256 * h]
        rows += [main[:128], main[128:192] + _unrot_half_rows(rot[128:192])]
    return jnp.concatenate(rows, axis=0)


def _grad_kv_from_ext(dwkv_ext):
    kvcols = []
    for h in range(HEADS):
        kvcols += [dwkv_ext[:, 256 * h:256 * h + 128], dwkv_ext[:, 1024 + 128 * h:1152 + 128 * h]]
    return jnp.concatenate(kvcols, axis=1)


SMALL_W = 6144 + 512 + 512 + 256 + 256 + 4 * 1024 + 128


def kernel(x, c, positions, w_ada, b_ada, w_in, hg_lower_bounds, hg_norm_w, mla_q_norm_w, w_q_up, mla_kv_norm_w, w_kv_up, w_out, ln1_g, ln1_b, w_mlp_in, w_mlp_out, ln2_g, ln2_b, loss_target, m_w_ada, m_b_ada, m_w_in, m_hg_lower_bounds, m_hg_norm_w, m_mla_q_norm_w, m_w_q_up, m_mla_kv_norm_w, m_w_kv_up, m_w_out, m_ln1_g, m_ln1_b, m_w_mlp_in, m_w_mlp_out, m_ln2_g, m_ln2_b, v_w_ada, v_b_ada, v_w_in, v_hg_lower_bounds, v_hg_norm_w, v_mla_q_norm_w, v_w_q_up, v_mla_kv_norm_w, v_w_kv_up, v_w_out, v_ln1_g, v_ln1_b, v_w_mlp_in, v_w_mlp_out, v_ln2_g, v_ln2_b):
    T = x.shape[1]
    me = 4 * lax.axis_index("x") + 2 * lax.axis_index("y") + lax.axis_index("c")
    xs, tgt = x[0], loss_target[0]
    transposed = ("w_in", "w_q_up")
    as_used = lambda n, a: a[0].T if n in transposed else a[0]
    big = {n: as_used(n, a) for n, a in dict(w_in=w_in, w_q_up=w_q_up, w_kv_up=w_kv_up, w_out=w_out,
                                              w_mlp_in=w_mlp_in, w_mlp_out=w_mlp_out).items()}
    names = list(big)

    bf = {n: big[n].astype(BF16) for n in names}
    g_in, g_c = _gather_two_level([bf["w_in"], c], name="gather_w_in")
    c_all = g_c.reshape(N_DEV, D_MODEL)

    ada_cols = w_ada.shape[2]
    mod_part, cond = _mod_part(c_all, w_ada[0], lax.dynamic_slice(b_ada, (0, me * ada_cols), (1, ada_cols)))
    (mod_all,) = _exchange([mod_part], scatter=False, name="gather_mod")
    mod_row = lax.dynamic_slice(mod_all, (0, me, 0), (N_DEV, 1, ada_cols)).reshape(1, N_DEV * ada_cols)
    sh_a, sc_a, g_a, sh_m, sc_m, g_m = [mod_row[:, D_MODEL * i:D_MODEL * (i + 1)] for i in range(6)]

    w_in_ext = _ext_in_t(g_in)
    z, (g_q, g_kv, g_out) = _matmul(xs, w_in_ext, "NT", "in_proj", a_fn=_modulate, extras=(sc_a, sh_a), tn=3072,
                                    exchange=_Exchange([bf["w_q_up"], bf["w_kv_up"], bf["w_out"]], False))
    wq_ext = _ext_q_t(g_q.reshape(N_DEV * g_q.shape[1], g_q.shape[2]))
    wkv_ext = _ext_kv(_cols_from_slabs(g_kv))
    w_out_full = g_out.reshape(D_MODEL, D_MODEL)
    inv_freq = 1.0 / (ROPE_THETA ** (jnp.arange(0, ROPE_DIM, 2, dtype=F32) / ROPE_DIM))
    zeros = lambda n: jnp.zeros((n,), F32)
    invf = jnp.concatenate([zeros(128), inv_freq, inv_freq, zeros(64)]).reshape(1, QK_PAD)
    m_one = jnp.concatenate([jnp.ones((128,), F32), zeros(128)]).reshape(1, QK_PAD)
    m_rot = jnp.concatenate([zeros(128), jnp.ones((64,), F32), zeros(64)]).reshape(1, QK_PAD)
    q, k, v, c1, s1, cqn, ckvn = _mla_pre(z, positions.reshape(T, 1), invf, m_one, m_rot, wq_ext, wkv_ext,
                                          mla_q_norm_w, mla_kv_norm_w)
    (o_raw, o_gated, s_prev), (w1,) = _hgrn_fwd(z, hg_lower_bounds, hg_norm_w,
                                                exchange=_StagedGather(bf["w_mlp_in"]))
    (o_mla, lse), (w2,) = _attn_fwd(q, k, v, exchange=_StagedGather(bf["w_mlp_out"]))
    mix, xhat1, rstd1, u2 = _mix_ln1(o_gated, o_mla, w_out_full, xs, g_a, ln1_g, ln1_b, sc_m, sh_m)[0]
    r, dr2, dh, dln2_g, dln2_b, dg_m, loss_part = _mlp_fwd(u2, w1, w2, xhat1, ln1_g, ln1_b, g_m, ln2_g, ln2_b, tgt)

    dhpre, dr1, dmix, dsc_m, dsh_m, dln1_g, dln1_b, dg_a = _mlp_bwd(dh, w1, w2, r, dr2, xhat1, rstd1, mix, ln1_g,
                                                                    ln1_b, sc_m, g_a)
    received = {}
    dw2 = _matmul(r, dh, "TN", "wgrad_mlp_out", out_dtype=BF16, a_fn=_square, tm=1024)
    dw1 = _matmul(u2, dhpre, "TN", "wgrad_mlp_in", out_dtype=BF16, tm=1024, out_slabs=N_DEV)
    dmixcat = _matmul(dmix, w_out_full, "NT", "dgrad_out")
    dw_out = jnp.concatenate([_matmul(o_gated, dmix, "TN", "wgrad_out_hg", out_dtype=BF16),
                              _matmul(o_mla, dmix, "TN", "wgrad_out_mla", out_dtype=BF16)], axis=0)
    (dz_h, dlb, dnw), (received["w_out"],) = _hgrn_bwd(
        dmixcat, z, o_raw, s_prev, hg_lower_bounds, hg_norm_w,
        exchange=_Exchange([dw_out.reshape(N_DEV, D_MODEL // N_DEV, D_MODEL)], True))
    (dq, dk, dv), (received["w_mlp_in"], received["w_mlp_out"]) = _attn_bwd(
        q, k, v, dmixcat, o_mla, lse,
        exchange=_Exchange([dw1, dw2.reshape(N_DEV, dw2.shape[0] // N_DEV, D_MODEL)], True))
    dz_m, dq_ext, dkv_ext, dqnw, dkvnw = _mla_bwd(dq, dk, dv, z, c1, s1, wq_ext, wkv_ext, mla_q_norm_w,
                                                   mla_kv_norm_w)
    dwq_t = _grad_q_from_ext_t(_matmul(dq_ext, cqn, "TN", "wgrad_q_up", tm=1024))
    dwkv = _grad_kv_from_ext(_matmul(ckvn, dkv_ext, "TN", "wgrad_kv_up", tn=1536))
    qkv_slabs = [dwq_t.reshape((N_DEV, dwq_t.shape[0] // N_DEV, dwq_t.shape[1])).astype(BF16),
                 _slabs_from_cols(dwkv).astype(BF16)]
    dwt_h, (received["w_q_up"], received["w_kv_up"]) = _matmul(
        dz_h, xs, "TN", "wgrad_in_h", b_fn=_modulate, extras=(sc_a, sh_a), tm=1024,
        exchange=_Exchange(qkv_slabs, True))
    dwt_m = _matmul(dz_m, xs, "TN", "wgrad_in_m", b_fn=_modulate, extras=(sc_a, sh_a), tm=1024)
    dw_in_t = _grad_in_from_ext_t(dwt_h, dwt_m)
    in_slabs = dw_in_t.reshape((N_DEV, dw_in_t.shape[0] // N_DEV, dw_in_t.shape[1])).astype(BF16)
    (grad_x, dsc_a, dsh_a), (received["w_in"],) = _input_bwd(
        dz_h, dz_m, w_in_ext, xs, dr1, sc_a, exchange=_Exchange([in_slabs], True))

    small = jnp.concatenate([dsh_a, dsc_a, dg_a, dsh_m, dsc_m, dg_m, dlb, dnw, dqnw, dkvnw, dln1_g, dln1_b, dln2_g,
                             dln2_b, loss_part], axis=1)
    (small_all,) = _exchange([small], scatter=False, name="gather_small")

    moments = dict(w_in=(m_w_in, v_w_in), w_q_up=(m_w_q_up, v_w_q_up), w_kv_up=(m_w_kv_up, v_w_kv_up),
                   w_out=(m_w_out, v_w_out), w_mlp_in=(m_w_mlp_in, v_w_mlp_in), w_mlp_out=(m_w_mlp_out, v_w_mlp_out))
    res = {}
    for n in names:
        res[n] = _adam(received[n], big[n], as_used(n, moments[n][0]), as_used(n, moments[n][1]), name="adam_" + n)
    dmod_cols = lax.dynamic_slice(small_all.reshape(N_DEV, SMALL_W), (0, me * ada_cols), (N_DEV, ada_cols))
    cond_t = cond.T

    def ada_grad(ct_ref, dm_ref):
        g = ct_ref[:, 0:1] * dm_ref[0:1, :]
        for b in range(1, N_DEV):
            g = g + ct_ref[:, b:b + 1] * dm_ref[b:b + 1, :]
        return g

    res["w_ada"] = _adam(None, w_ada[0], m_w_ada[0], v_w_ada[0], name="adam_w_ada", g_fn=ada_grad,
                         g_extra=(cond_t, dmod_cols))

    small_params = [("b_ada", b_ada, m_b_ada, v_b_ada, 0),
                    ("hg_lower_bounds", hg_lower_bounds, m_hg_lower_bounds, v_hg_lower_bounds, 6144),
                    ("hg_norm_w", hg_norm_w, m_hg_norm_w, v_hg_norm_w, 6656),
                    ("mla_q_norm_w", mla_q_norm_w, m_mla_q_norm_w, v_mla_q_norm_w, 7168),
                    ("mla_kv_norm_w", mla_kv_norm_w, m_mla_kv_norm_w, v_mla_kv_norm_w, 7424),
                    ("ln1_g", ln1_g, m_ln1_g, v_ln1_g, 7680), ("ln1_b", ln1_b, m_ln1_b, v_ln1_b, 8704),
                    ("ln2_g", ln2_g, m_ln2_g, v_ln2_g, 9728), ("ln2_b", ln2_b, m_ln2_b, v_ln2_b, 10752)]
    loss_row, small_res = _adam_small(small_all, [p[1:] for p in small_params])
    for p, r4 in zip(small_params, small_res):
        res[p[0]] = r4
    loss = loss_row[0, 0]

    order = ["w_ada", "b_ada", "w_in", "hg_lower_bounds", "hg_norm_w", "mla_q_norm_w", "w_q_up", "mla_kv_norm_w",
             "w_kv_up", "w_out", "ln1_g", "ln1_b", "w_mlp_in", "w_mlp_out", "ln2_g", "ln2_b"]
    def as_given(n, a):
        if n in transposed:
            a = a.T
        return a[None] if n in big or n == "w_ada" else a

    shaped = {n: tuple(as_given(n, a) for a in res[n]) for n in order}
    outs = [loss, grad_x.reshape(1, T, D_MODEL)]
    for i in range(4):
        outs += [shaped[n][i] for n in order]
    return tuple(outs)
```

```python
import functools

import jax
import jax.numpy as jnp
import numpy as np
from jax import lax
from jax.experimental import pallas as pl
from jax.experimental.pallas import tpu as pltpu

F32, BF16 = jnp.float32, jnp.bfloat16
N_DEV = 8
D_MODEL = 1024
HEADS = 4
HEAD_DIM = 128
ROPE_DIM = 64
QK_PAD = 256
CHUNK = 64
ROPE_THETA = 10000.0
RMS_EPS = 1e-6
LN_EPS = 1e-5
ALPHA = 2.0 ** 0.25
ATT_SCALE = (HEAD_DIM + ROPE_DIM) ** -0.5
LN2 = float(np.log(2.0))
Q_PRESCALE = ATT_SCALE / LN2
ADAM_LR, ADAM_B1, ADAM_B2, ADAM_EPS, ADAM_WD, ADAM_STEP = 0.001, 0.9, 0.999, 1e-08, 0.01, 10
NEG_BIG = -1e30

ROW_TILE = 512
ROW_TILE_SMALL = 256
ATT_TILE = 512
HGRN_GROUP = 8
MLP_SLABS = 4
VMEM_LIMIT = 56 * 2 ** 20

NN = (((1,), (0,)), ((), ()))
NT = (((1,), (1,)), ((), ()))
TN = (((0,), (0,)), ((), ()))


def _dot(a, b, dims=NN):
    return lax.dot_general(a, b, dims, preferred_element_type=F32)


def _bdot(a, b, dims=NN):
    return lax.dot_general(a.astype(BF16), b.astype(BF16), dims, preferred_element_type=F32)


def _hdot(a, b, dims=NN):
    return lax.dot_general(a, b, dims, precision=lax.Precision.HIGHEST, preferred_element_type=F32)


def _params():
    return pltpu.CompilerParams(vmem_limit_bytes=VMEM_LIMIT)


def _sigmoid(x):
    return 1.0 / (1.0 + jnp.exp(-x))


def _rowsum(x):
    return jnp.sum(x, axis=0, keepdims=True)


def _lanemean(x):
    return jnp.mean(x, axis=-1, keepdims=True)


def _full(shape):
    nd = len(shape)
    return pl.BlockSpec(shape, lambda *_: (0,) * nd)


class _Exchange:
    def __init__(self, arrs, scatter):
        self.arrs, self.scatter, self.n = list(arrs), scatter, len(arrs)
        self.out_shape = [jax.ShapeDtypeStruct((N_DEV,) + (a.shape[1:] if scatter else a.shape), a.dtype)
                          for a in self.arrs]
        n = self.n
        self.scratch = [pltpu.SemaphoreType.DMA((n, N_DEV - 1)), pltpu.SemaphoreType.DMA((n, N_DEV - 1)),
                        pltpu.SemaphoreType.DMA((n,))]

    def _copies(self, ins, outs, sems):
        send_sems, recv_sems, loc_sems = sems
        x, y, c = lax.axis_index("x"), lax.axis_index("y"), lax.axis_index("c")
        me = 4 * x + 2 * y + c
        copies = []
        for k in range(self.n):
            src_of = (lambda i, k=k: ins[k].at[i]) if self.scatter else (lambda i, k=k: ins[k])
            copies.append((pltpu.make_async_copy(src_of(me), outs[k].at[me], loc_sems.at[k]), None))
            for p in range(1, N_DEV):
                px = (1 - x) if p & 4 else x
                py = (1 - y) if p & 2 else y
                pc = (1 - c) if p & 1 else c
                peer = 4 * px + 2 * py + pc
                both = dict(send_sem=send_sems.at[k, p - 1], recv_sem=recv_sems.at[k, p - 1],
                            device_id=(px, py, pc), device_id_type=pl.DeviceIdType.MESH)
                send = pltpu.make_async_remote_copy(src_ref=src_of(peer), dst_ref=outs[k].at[me], **both)
                recv = pltpu.make_async_remote_copy(src_ref=src_of(peer), dst_ref=outs[k].at[peer], **both)
                copies.append((send, recv))
        return copies

    def start(self, ins, outs, sems):
        for first, _ in self._copies(ins, outs, sems):
            first.start()

    def middle(self, ins, outs, sems):
        pass

    def wait(self, ins, outs, sems):
        for first, recv in self._copies(ins, outs, sems):
            if recv is None:
                first.wait()
            else:
                recv.wait_recv()
                first.wait_send()


class _StagedGather:
    def __init__(self, arr):
        self.arrs, self.n = [arr], 1
        self.out_shape = [jax.ShapeDtypeStruct((N_DEV,) + arr.shape, arr.dtype)]
        self.scratch = [pltpu.VMEM((N_DEV,) + arr.shape, arr.dtype), pltpu.SemaphoreType.DMA((7,)),
                        pltpu.SemaphoreType.DMA((7,)), pltpu.SemaphoreType.DMA((2,))]

    def _parts(self, scr):
        stage, send_sems, recv_sems, loc_sems = scr
        x, y, c = lax.axis_index("x"), lax.axis_index("y"), lax.axis_index("c")
        me, sibling = (x, y, c), (x, y, 1 - c)
        chips = [(1 - x, y), (x, 1 - y), (1 - x, 1 - y)]

        def copy(j, block, to):
            px, py, pc = block
            slot = stage.at[4 * px + 2 * py + pc]
            return pltpu.make_async_remote_copy(src_ref=slot, dst_ref=slot, send_sem=send_sems.at[j],
                                                recv_sem=recv_sems.at[j], device_id=to,
                                                device_id_type=pl.DeviceIdType.MESH)

        return stage, loc_sems, me, sibling, chips, c, copy

    def start(self, ins, outs, scr):
        stage, loc_sems, me, sibling, chips, c, copy = self._parts(scr)
        x, y, _ = me
        own = pltpu.make_async_copy(ins[0], stage.at[4 * x + 2 * y + c], loc_sems.at[0])
        own.start()
        own.wait()
        copy(0, me, sibling).start()
        for j, chip in enumerate(chips):
            copy(1 + j, me, (*chip, c)).start()

    def middle(self, ins, outs, scr):
        stage, loc_sems, me, sibling, chips, c, copy = self._parts(scr)
        for j, chip in enumerate(chips):
            copy(1 + j, (*chip, c), me).wait_recv()
            copy(4 + j, (*chip, c), sibling).start()

    def wait(self, ins, outs, scr):
        stage, loc_sems, me, sibling, chips, c, copy = self._parts(scr)
        copy(0, sibling, me).wait_recv()
        for j, chip in enumerate(chips):
            copy(4 + j, (*chip, 1 - c), me).wait_recv()
        copy(0, me, sibling).wait_send()
        for j, chip in enumerate(chips):
            copy(1 + j, me, (*chip, c)).wait_send()
            copy(4 + j, (*chip, c), sibling).wait_send()
        whole = pltpu.make_async_copy(stage, outs[0], loc_sems.at[1])
        whole.start()
        whole.wait()


def _call(body, name, args, out_shape, grid=(), in_specs=(), out_specs=(), scratch_shapes=(), exchange=None,
          middle_at=0.8):
    if exchange is None:
        return pl.pallas_call(body, name=name, grid=grid, in_specs=list(in_specs), out_specs=list(out_specs),
                              out_shape=list(out_shape), scratch_shapes=list(scratch_shapes),
                              compiler_params=_params())(*args), None
    exs = list(exchange) if isinstance(exchange, (list, tuple)) else [exchange]
    ni, no, ns, nx = len(args), len(out_shape), len(scratch_shapes), sum(e.n for e in exs)
    steps = int(np.prod(grid))
    mid_step = min(max(int(steps * middle_at), 1), steps - 1)

    def wrapped(*refs):
        a, xi = refs[:ni], refs[ni:ni + nx]
        o, xo = refs[ni + nx:ni + nx + no], refs[ni + nx + no:ni + 2 * nx + no]
        s, xs = refs[ni + 2 * nx + no:ni + 2 * nx + no + ns], refs[ni + 2 * nx + no + ns:]
        parts, at, sat = [], 0, 0
        for e in exs:
            parts.append((e, xi[at:at + e.n], xo[at:at + e.n], xs[sat:sat + len(e.scratch)]))
            at, sat = at + e.n, sat + len(e.scratch)
        step = 0
        for d, g in enumerate(grid):
            step = step * g + pl.program_id(d)

        @pl.when(step == 0)
        def _():
            for e, ins, outs, sems in parts:
                e.start(ins, outs, sems)

        @pl.when(step == mid_step)
        def _():
            for e, ins, outs, sems in parts:
                e.middle(ins, outs, sems)

        body(*a, *o, *s)

        @pl.when(step == steps - 1)
        def _():
            for e, ins, outs, sems in parts:
                e.wait(ins, outs, sems)

    hbm = pl.BlockSpec(memory_space=pltpu.HBM)
    res = pl.pallas_call(
        wrapped, name=name, grid=grid, in_specs=list(in_specs) + [hbm] * nx, out_specs=list(out_specs) + [hbm] * nx,
        out_shape=list(out_shape) + [o_ for e in exs for o_ in e.out_shape],
        scratch_shapes=list(scratch_shapes) + [s_ for e in exs for s_ in e.scratch],
        compiler_params=_params())(*args, *[a_ for e in exs for a_ in e.arrs])
    return res[:no], res[no:]


def _gather_two_level(arrs, name):
    n = len(arrs)
    out_shape = [jax.ShapeDtypeStruct((N_DEV,) + a.shape, a.dtype) for a in arrs]

    def body(*refs):
        ins, outs = refs[:n], refs[n:2 * n]
        send_sems, recv_sems, loc_sems = refs[2 * n:]
        x, y, c = lax.axis_index("x"), lax.axis_index("y"), lax.axis_index("c")
        me, sibling = (x, y, c), (x, y, 1 - c)
        chips = [(1 - x, y), (x, 1 - y), (1 - x, 1 - y)]

        def copy(k, j, block, to, src=None):
            px, py, pc = block
            dst = outs[k].at[4 * px + 2 * py + pc]
            return pltpu.make_async_remote_copy(src_ref=dst if src is None else src, dst_ref=dst,
                                                send_sem=send_sems.at[k, j], recv_sem=recv_sems.at[k, j],
                                                device_id=to, device_id_type=pl.DeviceIdType.MESH)

        mine = [pltpu.make_async_copy(ins[k], outs[k].at[4 * x + 2 * y + c], loc_sems.at[k]) for k in range(n)]
        first = []
        for k in range(n):
            mine[k].start()
            first.append(copy(k, 0, me, sibling, src=ins[k]))
            first += [copy(k, 1 + j, me, (*chip, c), src=ins[k]) for j, chip in enumerate(chips)]
        for cp in first:
            cp.start()
        passed = []
        for j, chip in enumerate(chips):
            for k in range(n):
                copy(k, 1 + j, (*chip, c), me).wait_recv()
                passed.append(copy(k, 4 + j, (*chip, c), sibling))
                passed[-1].start()
        for k in range(n):
            copy(k, 0, sibling, me).wait_recv()
            for j, chip in enumerate(chips):
                copy(k, 4 + j, (*chip, 1 - c), me).wait_recv()
        for cp in first + passed:
            cp.wait_send()
        for cp in mine:
            cp.wait()

    vmem = pl.BlockSpec(memory_space=pltpu.VMEM)
    return pl.pallas_call(body, name=name, out_shape=out_shape, in_specs=[vmem] * n, out_specs=[vmem] * n,
                          scratch_shapes=[pltpu.SemaphoreType.DMA((n, 7)), pltpu.SemaphoreType.DMA((n, 7)),
                                          pltpu.SemaphoreType.DMA((n,))], compiler_params=_params())(*arrs)


def _exchange(arrs, scatter, name):
    ex = _Exchange(arrs, scatter)

    def body(*refs):
        ins, outs, sems = refs[:ex.n], refs[ex.n:2 * ex.n], refs[2 * ex.n:]
        ex.start(ins, outs, sems)
        ex.wait(ins, outs, sems)

    hbm = pl.BlockSpec(memory_space=pltpu.HBM)
    return pl.pallas_call(body, name=name, out_shape=ex.out_shape, in_specs=[hbm] * ex.n, out_specs=[hbm] * ex.n,
                          scratch_shapes=ex.scratch)(*ex.arrs)


def _matmul(a, b, mode, name, out_dtype=F32, tm=512, tn=1024, tk=1024, a_fn=None, b_fn=None, extras=(),
            out_slabs=None, exchange=None):
    assert not (a_fn and b_fn) and not (b_fn and mode == "NT")
    if mode == "NN":
        (M, K), N = a.shape, b.shape[1]
    elif mode == "NT":
        (M, K), N = a.shape, b.shape[0]
    else:
        (K, M), N = a.shape, b.shape[1]
    if out_slabs:
        tn = N // out_slabs
    tm, tn, tk = min(tm, M), min(tn, N), min(tk, K)
    assert M % tm == 0 and N % tn == 0 and K % tk == 0, (name, M, N, K)
    nk = K // tk
    dims = {"NN": NN, "NT": NT, "TN": TN}[mode]
    ne = len(extras)

    def body(a_ref, b_ref, *rest):
        e_refs, o_ref, acc_ref = rest[:ne], rest[ne], rest[ne + 1]
        k = pl.program_id(2)

        @pl.when(k == 0)
        def _():
            acc_ref[...] = jnp.zeros_like(acc_ref)

        at, bt = a_ref[...], b_ref[...]
        if a_fn is not None:
            at = a_fn(at.astype(F32), *[e[...] for e in e_refs])
        if b_fn is not None:
            bt = b_fn(bt.astype(F32), *[e[...] for e in e_refs])
        acc_ref[...] += _bdot(at, bt, dims)

        @pl.when(k == nk - 1)
        def _():
            o_ref[...] = acc_ref[...].astype(out_dtype)

    if mode == "TN":
        a_spec = pl.BlockSpec((tk, tm), lambda i, j, k: (k, i))
        e_spec = pl.BlockSpec((1, tm), lambda i, j, k: (0, i))
    else:
        a_spec = pl.BlockSpec((tm, tk), lambda i, j, k: (i, k))
        e_spec = pl.BlockSpec((1, tk), lambda i, j, k: (0, k))
    if mode == "NT":
        b_spec = pl.BlockSpec((tn, tk), lambda i, j, k: (j, k))
    else:
        b_spec = pl.BlockSpec((tk, tn), lambda i, j, k: (k, j))
    if b_fn is not None:
        e_spec = pl.BlockSpec((1, tn), lambda i, j, k: (0, j))
    if out_slabs:
        o_shape = jax.ShapeDtypeStruct((out_slabs, M, tn), out_dtype)
        o_spec = pl.BlockSpec((None, tm, tn), lambda i, j, k: (j, i, 0))
    else:
        o_shape = jax.ShapeDtypeStruct((M, N), out_dtype)
        o_spec = pl.BlockSpec((tm, tn), lambda i, j, k: (i, j))
    (out,), got = _call(body, name, (a, b, *extras), [o_shape], grid=(M // tm, N // tn, nk),
                        in_specs=[a_spec, b_spec] + [e_spec] * ne, out_specs=[o_spec],
                        scratch_shapes=[pltpu.VMEM((tm, tn), F32)], exchange=exchange)
    return out if exchange is None else (out, got)


def _modulate(x, sc, sh):
    return x * (1.0 + sc) + sh


def _square(x):
    return x * x


def _mod_part(c_all, w_ada_s, b_s):
    def body(c_ref, w_ref, b_ref, mod_ref, cond_ref):
        cv = c_ref[...]
        cond = cv * _sigmoid(cv)
        cond_ref[...] = cond
        mod_ref[...] = _bdot(cond, w_ref[...]) + b_ref[...]

    return pl.pallas_call(
        body, name="mod_part",
        out_shape=[jax.ShapeDtypeStruct((N_DEV, w_ada_s.shape[1]), F32), jax.ShapeDtypeStruct(c_all.shape, F32)],
        compiler_params=_params(),
    )(c_all, w_ada_s, b_s)


def _rms_fwd(x, w):
    rs = lax.rsqrt(_lanemean(x * x) + RMS_EPS)
    return x * rs * w, rs


def _rms_bwd(x, rs, w, dy):
    xhat = x * rs
    dxh = dy * w
    return rs * (dxh - xhat * _lanemean(dxh * xhat)), dy * xhat


def _mla_pre(z, pos_col, invf, m_one, m_rot, wq_ext, wkv_ext, qnw, kvnw):
    T = z.shape[0]
    tm = min(ROW_TILE, T)

    def body(z_ref, pos_ref, invf_ref, mone_ref, mrot_ref, wq_ref, wkv_ref, qnw_ref, kvnw_ref,
             q_ref, k_ref, v_ref, c1_ref, s1_ref, cqn_ref, ckvn_ref):
        ang = pos_ref[...].astype(F32) * invf_ref[...]
        c1 = mone_ref[...] + mrot_ref[...] * jnp.cos(ang)
        s1 = mrot_ref[...] * jnp.sin(ang)
        c1_ref[...] = c1
        s1_ref[...] = s1
        cqn, _ = _rms_fwd(z_ref[:, 0:256], qnw_ref[...])
        ckvn, _ = _rms_fwd(z_ref[:, 256:512], kvnw_ref[...])
        cqn_ref[...] = cqn.astype(BF16)
        ckvn_ref[...] = ckvn.astype(BF16)
        qe = _bdot(cqn, wq_ref[...], NT)
        kve = _bdot(ckvn, wkv_ref[...])
        k_rope = z_ref[:, 512:768] * c1 + z_ref[:, 768:1024] * s1
        for h in range(HEADS):
            q_ref[h] = ((qe[:, 256 * h:256 * h + 256] * c1 + qe[:, 1024 + 256 * h:1280 + 256 * h] * s1)
                        * Q_PRESCALE).astype(BF16)
            k_ref[h] = (kve[:, 256 * h:256 * h + 256] + k_rope).astype(BF16)
            v_ref[h] = kve[:, 1024 + 128 * h:1152 + 128 * h].astype(BF16)

    row = lambda i: (i, 0)
    head = lambda i: (0, i, 0)
    return pl.pallas_call(
        body, name="mla_pre", grid=(T // tm,),
        in_specs=[pl.BlockSpec((tm, 1024), lambda i: (i, 2)), pl.BlockSpec((tm, 1), row),
                  _full((1, 256)), _full((1, 256)), _full((1, 256)), _full(wq_ext.shape), _full(wkv_ext.shape),
                  _full((1, 256)), _full((1, 256))],
        out_specs=[pl.BlockSpec((HEADS, tm, QK_PAD), head), pl.BlockSpec((HEADS, tm, QK_PAD), head),
                   pl.BlockSpec((HEADS, tm, HEAD_DIM), head), pl.BlockSpec((tm, 256), row), pl.BlockSpec((tm, 256), row),
                   pl.BlockSpec((tm, 256), row), pl.BlockSpec((tm, 256), row)],
        out_shape=[jax.ShapeDtypeStruct((HEADS, T, QK_PAD), BF16), jax.ShapeDtypeStruct((HEADS, T, QK_PAD), BF16),
                   jax.ShapeDtypeStruct((HEADS, T, HEAD_DIM), BF16), jax.ShapeDtypeStruct((T, 256), F32),
                   jax.ShapeDtypeStruct((T, 256), F32), jax.ShapeDtypeStruct((T, 256), BF16),
                   jax.ShapeDtypeStruct((T, 256), BF16)],
        compiler_params=_params(),
    )(z, pos_col, invf, m_one, m_rot, wq_ext, wkv_ext, qnw, kvnw)


def _mla_bwd(dq, dk, dv, z, c1, s1, wq_ext, wkv_ext, qnw, kvnw):
    T = z.shape[0]
    tm = min(ROW_TILE_SMALL, T)

    def body(dq_ref, dk_ref, dv_ref, z_ref, c1_ref, s1_ref, wq_ref, wkv_ref, qnw_ref, kvnw_ref,
             dz_ref, dqe_ref, dkve_ref, dqnw_ref, dkvnw_ref):
        @pl.when(pl.program_id(0) == 0)
        def _():
            dqnw_ref[...] = jnp.zeros_like(dqnw_ref)
            dkvnw_ref[...] = jnp.zeros_like(dkvnw_ref)

        c1, s1 = c1_ref[...], s1_ref[...]
        dkpe = jnp.zeros((tm, QK_PAD), F32)
        for h in range(HEADS):
            dqh, dkh = dq_ref[h] * Q_PRESCALE, dk_ref[h]
            dqe_ref[:, 256 * h:256 * h + 256] = (dqh * c1).astype(BF16)
            dqe_ref[:, 1024 + 256 * h:1280 + 256 * h] = (dqh * s1).astype(BF16)
            dkve_ref[:, 256 * h:256 * h + 256] = dkh.astype(BF16)
            dkve_ref[:, 1024 + 128 * h:1152 + 128 * h] = dv_ref[h].astype(BF16)
            dkpe = dkpe + dkh
        dcqn = _dot(dqe_ref[...], wq_ref[...])
        dckvn = _dot(dkve_ref[...], wkv_ref[...], NT)
        cq, ckv = z_ref[:, 0:256], z_ref[:, 256:512]
        _, rsq = _rms_fwd(cq, qnw_ref[...])
        _, rskv = _rms_fwd(ckv, kvnw_ref[...])
        dcq, wq_rows = _rms_bwd(cq, rsq, qnw_ref[...], dcqn)
        dckv, wkv_rows = _rms_bwd(ckv, rskv, kvnw_ref[...], dckvn)
        dqnw_ref[...] += _rowsum(wq_rows)
        dkvnw_ref[...] += _rowsum(wkv_rows)
        dz_ref[:, 0:256] = dcq
        dz_ref[:, 256:512] = dckv
        dz_ref[:, 512:768] = dkpe * c1
        dz_ref[:, 768:1024] = dkpe * s1

    row = lambda i: (i, 0)
    head = lambda i: (0, i, 0)
    return pl.pallas_call(
        body, name="mla_bwd", grid=(T // tm,),
        in_specs=[pl.BlockSpec((HEADS, tm, QK_PAD), head), pl.BlockSpec((HEADS, tm, QK_PAD), head),
                  pl.BlockSpec((HEADS, tm, HEAD_DIM), head), pl.BlockSpec((tm, 1024), lambda i: (i, 2)),
                  pl.BlockSpec((tm, 256), row), pl.BlockSpec((tm, 256), row), _full(wq_ext.shape), _full(wkv_ext.shape),
                  _full((1, 256)), _full((1, 256))],
        out_specs=[pl.BlockSpec((tm, 1024), row), pl.BlockSpec((tm, 2048), row), pl.BlockSpec((tm, 1536), row),
                   _full((1, 256)), _full((1, 256))],
        out_shape=[jax.ShapeDtypeStruct((T, 1024), F32), jax.ShapeDtypeStruct((T, 2048), BF16),
                   jax.ShapeDtypeStruct((T, 1536), BF16), jax.ShapeDtypeStruct((1, 256), F32),
                   jax.ShapeDtypeStruct((1, 256), F32)],
        compiler_params=_params(),
    )(dq, dk, dv, z, c1, s1, wq_ext, wkv_ext, qnw, kvnw)


_HEAD_LANES = [slice(HEAD_DIM * h, HEAD_DIM * (h + 1)) for h in range(HEADS)]


def _lower_bound(lbraw_ref):
    a0, a1 = lbraw_ref[0:1, :], lbraw_ref[1:2, :]
    mx = jnp.maximum(a0, a1)
    e0, e1 = jnp.exp(a0 - mx), jnp.exp(a1 - mx)
    return e0 / (e0 + e1)


def _tri(lower):
    r = lax.broadcasted_iota(jnp.int32, (CHUNK, CHUNK), 0)
    c = lax.broadcasted_iota(jnp.int32, (CHUNK, CHUNK), 1)
    return (r >= c) if lower else (r <= c)


def _hgrn_gates(q, f, lb, tri_lo):
    sg = _sigmoid(f)
    forget = lb + (1.0 - lb) * sg
    k = 1.0 - forget
    b = _hdot(tri_lo.astype(F32), jnp.log(forget))
    b_ref, b_last = b[CHUNK // 2 - 1:CHUNK // 2, :], b[CHUNK - 1:CHUNK, :]
    e1, e2, e3, e4 = jnp.exp(b - b_ref), jnp.exp(b_ref - b), jnp.exp(b_last - b), jnp.exp(b)
    return dict(sg=sg, forget=forget, k=k, e1=e1, e2=e2, e3=e3, e4=e4, qa=q * e1, ka=k * e2, kl=k * e3, qb=q * e4,
                decay=jnp.exp(b_last))


def _hgrn_fwd(z, lbraw, nw, exchange=None):
    T = z.shape[0]
    G = min(HGRN_GROUP, T // CHUNK)
    rows = G * CHUNK
    n_chunks = T // CHUNK

    def body(q_ref, f_ref, i_ref, g_ref, lbraw_ref, nw_ref, oraw_ref, og_ref, sp_ref, st_ref):
        @pl.when(pl.program_id(0) == 0)
        def _():
            st_ref[...] = jnp.zeros_like(st_ref)

        lb_all = _lower_bound(lbraw_ref)
        tri_lo = _tri(True)

        def chunk(cc, carry):
            rs = pl.ds(pl.multiple_of(cc * CHUNK, CHUNK), CHUNK)
            t = _hgrn_gates(q_ref[rs, :], f_ref[rs, :], lb_all, tri_lo)
            v, gate = i_ref[rs, :], g_ref[rs, :]
            st = [st_ref[h] for h in range(HEADS)]
            a = [jnp.where(tri_lo, _bdot(t["qa"][:, s], t["ka"][:, s], NT), 0.0) for s in _HEAD_LANES]
            kv = [_bdot(v[:, s], t["kl"][:, s], TN) for s in _HEAD_LANES]
            o = [_bdot(a[h], v[:, s]) + _bdot(t["qb"][:, s], st[h], NT) for h, s in enumerate(_HEAD_LANES)]
            for h, s in enumerate(_HEAD_LANES):
                sp_ref[cc, h] = st[h]
                st_ref[h] = st[h] * t["decay"][:, s] + kv[h]
            oraw_ref[rs, :] = jnp.concatenate(o, axis=1)
            on = jnp.concatenate([_rms_fwd(o[h], nw_ref[:, s])[0] for h, s in enumerate(_HEAD_LANES)], axis=1)
            og_ref[rs, :] = (on * (gate * _sigmoid(gate))).astype(BF16)
            return carry

        lax.fori_loop(0, G, chunk, 0, unroll=2)

    col = lambda j: pl.BlockSpec((rows, 512), lambda r, j=j: (r, j))
    return _call(
        body, "hgrn_fwd", (z, z, z, z, lbraw, nw), grid=(T // rows,),
        in_specs=[col(0), col(1), col(2), col(3), _full((2, 512)), _full((1, 512))],
        out_specs=[col(0), col(0), pl.BlockSpec((G, HEADS, HEAD_DIM, HEAD_DIM), lambda r: (r, 0, 0, 0))],
        out_shape=[jax.ShapeDtypeStruct((T, 512), F32), jax.ShapeDtypeStruct((T, 512), BF16),
                   jax.ShapeDtypeStruct((n_chunks, HEADS, HEAD_DIM, HEAD_DIM), F32)],
        scratch_shapes=[pltpu.VMEM((HEADS, HEAD_DIM, HEAD_DIM), F32)], exchange=exchange)


def _hgrn_bwd(dmixcat, z, oraw, sprev, lbraw, nw, exchange=None):
    T = z.shape[0]
    G = min(HGRN_GROUP, T // CHUNK)
    rows = G * CHUNK
    ng = T // rows

    def body(dog_ref, q_ref, f_ref, i_ref, g_ref, oraw_ref, sp_ref, lbraw_ref, nw_ref,
             dz_ref, dlb_ref, dnw_ref, dst_ref):
        @pl.when(pl.program_id(0) == 0)
        def _():
            dst_ref[...] = jnp.zeros_like(dst_ref)
            dlb_ref[...] = jnp.zeros_like(dlb_ref)
            dnw_ref[...] = jnp.zeros_like(dnw_ref)

        lb_all = _lower_bound(lbraw_ref)
        tri_lo, tri_up = _tri(True), _tri(False)
        rowid = lax.broadcasted_iota(jnp.int32, (CHUNK, HEADS * HEAD_DIM), 0)

        def chunk(it, carry):
            cc = G - 1 - it
            rs = pl.ds(pl.multiple_of(cc * CHUNK, CHUNK), CHUNK)
            heads = list(enumerate(_HEAD_LANES))
            cat = lambda parts: jnp.concatenate(parts, axis=1)
            per_head_mean = lambda x: cat([jnp.broadcast_to(_lanemean(x[:, s]), (CHUNK, HEAD_DIM)) for s in _HEAD_LANES])
            t = _hgrn_gates(q_ref[rs, :], f_ref[rs, :], lb_all, tri_lo)
            v, gate, o, dog, nw_all = i_ref[rs, :], g_ref[rs, :], oraw_ref[rs, :], dog_ref[rs, :], nw_ref[...]
            rs_o = lax.rsqrt(per_head_mean(o * o) + RMS_EPS)
            xhat = o * rs_o
            sgg = _sigmoid(gate)
            d_on = dog * (gate * sgg)
            dz_ref[rs, 1536:2048] = dog * (xhat * nw_all) * (sgg * (1.0 + gate * (1.0 - sgg)))
            dxh = d_on * nw_all
            do = rs_o * (dxh - xhat * per_head_mean(dxh * xhat))
            dnw_ref[...] += _rowsum(d_on * xhat)
            st = [sp_ref[cc, h] for h in range(HEADS)]
            dst = [dst_ref[h] for h in range(HEADS)]
            a = [jnp.where(tri_lo, _bdot(t["qa"][:, s], t["ka"][:, s], NT), 0.0) for s in _HEAD_LANES]
            da = [jnp.where(tri_lo, _bdot(do[:, s], v[:, s], NT), 0.0) for s in _HEAD_LANES]
            dqb = cat([_bdot(do[:, s], st[h]) for h, s in heads])
            dkl = cat([_bdot(v[:, s], dst[h]) for h, s in heads])
            dv_ = cat([_bdot(t["kl"][:, s], dst[h], NT) + _bdot(a[h], do[:, s], TN) for h, s in heads])
            dqa = cat([_bdot(da[h], t["ka"][:, s]) for h, s in heads])
            dka = cat([_bdot(da[h], t["qa"][:, s], TN) for h, s in heads])
            ddecay = cat([_rowsum(dst[h] * st[h]) for h in range(HEADS)])
            for h, s in heads:
                dst_ref[h] = dst[h] * t["decay"][:, s] + _bdot(do[:, s], t["qb"][:, s], TN)
            pa, pk, pb, pl_ = dqa * t["qa"], dka * t["ka"], dqb * t["qb"], dkl * t["kl"]
            db = pa - pk + pb - pl_
            db = db + jnp.where(rowid == CHUNK // 2 - 1, _rowsum(pk - pa), 0.0)
            db = db + jnp.where(rowid == CHUNK - 1, _rowsum(pl_) + ddecay * t["decay"], 0.0)
            dlogf = _hdot(tri_up.astype(F32), db)
            dforget = dlogf / t["forget"] - (dka * t["e2"] + dkl * t["e3"])
            sg = t["sg"]
            dz_ref[rs, 0:512] = dqa * t["e1"] + dqb * t["e4"]
            dz_ref[rs, 512:1024] = dforget * (1.0 - lb_all) * sg * (1.0 - sg)
            dz_ref[rs, 1024:1536] = dv_
            dlb_ref[...] += _rowsum(dforget * (1.0 - sg))
            return carry

        lax.fori_loop(0, G, chunk, 0, unroll=2)

    col = lambda j: pl.BlockSpec((rows, 512), lambda r, j=j: (ng - 1 - r, j))
    return _call(
        body, "hgrn_bwd", (dmixcat, z, z, z, z, oraw, sprev, lbraw, nw), grid=(ng,),
        in_specs=[col(0), col(0), col(1), col(2), col(3), col(0),
                  pl.BlockSpec((G, HEADS, HEAD_DIM, HEAD_DIM), lambda r: (ng - 1 - r, 0, 0, 0)),
                  _full((2, 512)), _full((1, 512))],
        out_specs=[pl.BlockSpec((rows, 2048), lambda r: (ng - 1 - r, 0)), _full((1, 512)), _full((1, 512))],
        out_shape=[jax.ShapeDtypeStruct((T, 2048), F32), jax.ShapeDtypeStruct((1, 512), F32),
                   jax.ShapeDtypeStruct((1, 512), F32)],
        scratch_shapes=[pltpu.VMEM((HEADS, HEAD_DIM, HEAD_DIM), F32)], exchange=exchange)


def _diag_mask(t):
    r = lax.broadcasted_iota(jnp.int32, (t, t), 0)
    c = lax.broadcasted_iota(jnp.int32, (t, t), 1)
    return r >= c


def _attn_fwd(q, k, v, exchange=None):
    _, T, _ = q.shape
    t = min(ATT_TILE, T)

    def body(q_ref, k_ref, v_ref, o_ref, lse_ref):
        i = pl.program_id(1)
        qb = q_ref[...]

        rows = lambda j: pl.ds(pl.multiple_of(j * t, t), t)

        def logits(j, masked):
            s = _dot(qb, k_ref[rows(j), :], NT)
            return jnp.where(_diag_mask(t), s, NEG_BIG) if masked else s

        def absorb(s, j, carry):
            m, l, acc = carry
            mn = jnp.maximum(m, jnp.max(s, axis=-1, keepdims=True))
            p = jnp.exp2(s - mn)
            al = jnp.exp2(m - mn)
            return mn, al * l + jnp.sum(p, axis=-1, keepdims=True), al * acc + _dot(p.astype(BF16), v_ref[rows(j), :])

        def pair(j0, carry, last_masked):
            s0, s1 = logits(j0, False), logits(j0 + 1, last_masked)
            return absorb(s1, j0 + 1, absorb(s0, j0, carry))

        init = (jnp.full((t, 1), NEG_BIG, F32), jnp.zeros((t, 1), F32), jnp.zeros((t, HEAD_DIM), F32))
        carry = lax.fori_loop(0, i // 2, lambda jj, c: pair(2 * jj, c, False), init)
        m, l, acc = lax.cond(i % 2 == 1, lambda c: pair(i - 1, c, True),
                             lambda c: absorb(logits(i, True), i, c), carry)
        o_ref[...] = acc / l
        lse_ref[...] = jnp.broadcast_to(m + jnp.log2(l), (t, HEAD_DIM))

    return _call(
        body, "attn_fwd", (q, k, v), grid=(HEADS, T // t),
        in_specs=[pl.BlockSpec((None, t, QK_PAD), lambda h, i: (h, i, 0)),
                  pl.BlockSpec((None, T, QK_PAD), lambda h, i: (h, 0, 0)),
                  pl.BlockSpec((None, T, HEAD_DIM), lambda h, i: (h, 0, 0))],
        out_specs=[pl.BlockSpec((t, HEAD_DIM), lambda h, i: (i, h)),
                   pl.BlockSpec((None, t, HEAD_DIM), lambda h, i: (h, i, 0))],
        out_shape=[jax.ShapeDtypeStruct((T, HEADS * HEAD_DIM), F32), jax.ShapeDtypeStruct((HEADS, T, HEAD_DIM), F32)],
        exchange=exchange)


def _attn_bwd(q, k, v, dmixcat, o, lse, exchange=None):
    _, T, _ = q.shape
    t = min(ATT_TILE, T)
    nq = T // t

    def body(q_ref, k_ref, v_ref, do_ref, o_ref, lse_ref, dq_ref, dk_ref, dv_ref, delta_ref):
        j = pl.program_id(1)

        @pl.when(j == 0)
        def _():
            dq_ref[...] = jnp.zeros_like(dq_ref)

            def fill(i, carry):
                rs = pl.ds(pl.multiple_of(i * t, t), t)
                delta_ref[rs, :] = jnp.broadcast_to(
                    jnp.sum(do_ref[rs, :] * o_ref[rs, :], axis=-1, keepdims=True), (t, HEAD_DIM))
                return carry

            lax.fori_loop(0, nq, fill, 0)

        kb, vb = k_ref[...], v_ref[...]

        def steps(blocks, carry):
            dk, dv = carry
            rs = [pl.ds(pl.multiple_of(i * t, t), t) for i, _ in blocks]
            qb = [q_ref[r, :] for r in rs]
            dob = [do_ref[r, :].astype(BF16) for r in rs]
            s = [_dot(b, kb, NT) for b in qb]
            dp = [_dot(b, vb, NT) for b in dob]
            for n, (_, masked) in enumerate(blocks):
                p = jnp.exp2(s[n] - lse_ref[rs[n], 0:1])
                if masked:
                    p = jnp.where(_diag_mask(t), p, 0.0)
                ds = (p * (dp[n] - delta_ref[rs[n], 0:1]) * LN2).astype(BF16)
                dq_ref[rs[n], :] += _dot(ds, kb)
                dk = dk + _dot(ds, qb[n], TN)
                dv = dv + _dot(p.astype(BF16), dob[n], TN)
            return dk, dv

        zero = (jnp.zeros((t, QK_PAD), F32), jnp.zeros((t, HEAD_DIM), F32))
        rest = nq - 1 - j
        carry = lax.cond(rest % 2 == 1, lambda c: steps([(j, True), (j + 1, False)], c),
                         lambda c: steps([(j, True)], c), zero)
        first = j + 1 + rest % 2
        dk, dv = lax.fori_loop(0, rest // 2, lambda n, c: steps([(first + 2 * n, False), (first + 2 * n + 1, False)], c),
                               carry)
        dk_ref[...] = dk
        dv_ref[...] = dv

    return _call(
        body, "attn_bwd", (q, k, v, dmixcat, o, lse), grid=(HEADS, nq),
        in_specs=[pl.BlockSpec((None, T, QK_PAD), lambda h, j: (h, 0, 0)),
                  pl.BlockSpec((None, t, QK_PAD), lambda h, j: (h, j, 0)),
                  pl.BlockSpec((None, t, HEAD_DIM), lambda h, j: (h, j, 0)),
                  pl.BlockSpec((T, HEAD_DIM), lambda h, j: (0, HEADS + h)),
                  pl.BlockSpec((T, HEAD_DIM), lambda h, j: (0, h)),
                  pl.BlockSpec((None, T, HEAD_DIM), lambda h, j: (h, 0, 0))],
        out_specs=[pl.BlockSpec((None, T, QK_PAD), lambda h, j: (h, 0, 0)),
                   pl.BlockSpec((None, t, QK_PAD), lambda h, j: (h, j, 0)),
                   pl.BlockSpec((None, t, HEAD_DIM), lambda h, j: (h, j, 0))],
        out_shape=[jax.ShapeDtypeStruct((HEADS, T, QK_PAD), F32), jax.ShapeDtypeStruct((HEADS, T, QK_PAD), F32),
                   jax.ShapeDtypeStruct((HEADS, T, HEAD_DIM), F32)],
        scratch_shapes=[pltpu.VMEM((T, HEAD_DIM), F32)], exchange=exchange)


def _ln_fwd(r):
    mu = _lanemean(r)
    xc = r - mu
    rstd = lax.rsqrt(_lanemean(xc * xc) + LN_EPS)
    return xc * rstd, rstd


def _ln_bwd(dxh, xhat, rstd):
    return rstd * (dxh - _lanemean(dxh) - xhat * _lanemean(dxh * xhat))


def _mix_ln1(o_hg, o_mla, w_out, x, g_a, ln1_g, ln1_b, sc_m, sh_m, exchange=None):
    T = x.shape[0]
    tm = min(ROW_TILE_SMALL, T)
    half = o_hg.shape[1]

    def body(hg_ref, mla_ref, w_ref, x_ref, ga_ref, g_ref, b_ref, sc_ref, sh_ref, mix_ref, xhat_ref, rstd_ref, u2_ref):
        mix = _dot(hg_ref[...], w_ref[0:half, :]) + _bdot(mla_ref[...], w_ref[half:, :])
        mix_ref[...] = mix
        xhat, rstd = _ln_fwd(ALPHA * x_ref[...] + (1.0 + ga_ref[...]) * mix)
        xhat_ref[...] = xhat
        rstd_ref[...] = jnp.broadcast_to(rstd, (tm, 128))
        u2_ref[...] = _modulate(xhat * g_ref[...] + b_ref[...], sc_ref[...], sh_ref[...]).astype(BF16)

    row = pl.BlockSpec((tm, D_MODEL), lambda i: (i, 0))
    vec = _full((1, D_MODEL))
    halfrow = pl.BlockSpec((tm, half), lambda i: (i, 0))
    return _call(
        body, "mix_ln1", (o_hg, o_mla, w_out, x, g_a, ln1_g, ln1_b, sc_m, sh_m), grid=(T // tm,),
        in_specs=[halfrow, halfrow, _full(w_out.shape), row, vec, vec, vec, vec, vec],
        out_specs=[row, row, pl.BlockSpec((tm, 128), lambda i: (i, 0)), row],
        out_shape=[jax.ShapeDtypeStruct((T, D_MODEL), F32), jax.ShapeDtypeStruct((T, D_MODEL), F32),
                   jax.ShapeDtypeStruct((T, 128), F32), jax.ShapeDtypeStruct((T, D_MODEL), BF16)],
        exchange=exchange)


def _mlp_fwd(u2, w1, w2, xhat1, ln1_g, ln1_b, g_m, ln2_g, ln2_b, target):
    T = u2.shape[0]
    tf = w1.shape[-1]
    nf = N_DEV // MLP_SLABS
    tm = min(ROW_TILE, T)

    def body(u2_ref, w1_ref, w2_ref, xhat_ref, g1_ref, b1_ref, gm_ref, g2_ref, b2_ref, tgt_ref,
             r_ref, dr2_ref, dh_ref, dg2_ref, db2_ref, dgm_ref, loss_ref, acc_ref):
        i, f = pl.program_id(0), pl.program_id(1)

        @pl.when((i == 0) & (f == 0))
        def _():
            for ref in (dg2_ref, db2_ref, dgm_ref, loss_ref):
                ref[...] = jnp.zeros_like(ref)

        @pl.when(f == 0)
        def _():
            acc_ref[...] = jnp.zeros_like(acc_ref)

        u2t = u2_ref[...]
        part = None
        for s in range(MLP_SLABS):
            r = jnp.maximum(_dot(u2t, w1_ref[s]), 0.0)
            r_ref[:, s * tf:(s + 1) * tf] = r.astype(BF16)
            d = _bdot(r * r, w2_ref[s])
            part = d if part is None else part + d
        acc_ref[...] += part

        @pl.when(f == nf - 1)
        def _():
            h = acc_ref[...]
            x1 = xhat_ref[...] * g1_ref[...] + b1_ref[...]
            xhat2, rstd2 = _ln_fwd(ALPHA * x1 + (1.0 + gm_ref[...]) * h)
            err = xhat2 * g2_ref[...] + b2_ref[...] - tgt_ref[...]
            loss_ref[...] += jnp.sum(0.5 * _lanemean(err * err), axis=0, keepdims=True)
            dy = err * (1.0 / D_MODEL)
            dg2_ref[...] += _rowsum(dy * xhat2)
            db2_ref[...] += _rowsum(dy)
            dr2 = _ln_bwd(dy * g2_ref[...], xhat2, rstd2)
            dr2_ref[...] = dr2
            dgm_ref[...] += _rowsum(dr2 * h)
            dh_ref[...] = ((1.0 + gm_ref[...]) * dr2).astype(BF16)

    row = pl.BlockSpec((tm, D_MODEL), lambda i, f: (i, 0))
    vec = _full((1, D_MODEL))
    return pl.pallas_call(
        body, name="mlp_fwd", grid=(T // tm, nf),
        in_specs=[row, pl.BlockSpec((MLP_SLABS, D_MODEL, tf), lambda i, f: (f, 0, 0)),
                  pl.BlockSpec((MLP_SLABS, tf, D_MODEL), lambda i, f: (f, 0, 0)),
                  row, vec, vec, vec, vec, vec, row],
        out_specs=[pl.BlockSpec((tm, MLP_SLABS * tf), lambda i, f: (i, f)), row, row, vec, vec, vec, _full((1, 128))],
        out_shape=[jax.ShapeDtypeStruct((T, N_DEV * tf), BF16), jax.ShapeDtypeStruct((T, D_MODEL), F32),
                   jax.ShapeDtypeStruct((T, D_MODEL), BF16), jax.ShapeDtypeStruct((1, D_MODEL), F32),
                   jax.ShapeDtypeStruct((1, D_MODEL), F32), jax.ShapeDtypeStruct((1, D_MODEL), F32),
                   jax.ShapeDtypeStruct((1, 128), F32)],
        scratch_shapes=[pltpu.VMEM((tm, D_MODEL), F32)],
        compiler_params=_params(),
    )(u2, w1, w2, xhat1, ln1_g, ln1_b, g_m, ln2_g, ln2_b, target)


def _mlp_bwd(dh, w1, w2, r, dr2, xhat1, rstd1, mix, ln1_g, ln1_b, sc_m, g_a):
    T = dh.shape[0]
    tf = w1.shape[-1]
    nf = N_DEV // MLP_SLABS
    tm = min(ROW_TILE, T)

    def body(dh_ref, w1_ref, w2_ref, r_ref, dr2_ref, xhat_ref, rstd_ref, mix_ref, g1_ref, b1_ref, sc_ref, ga_ref,
             dhpre_ref, dr1_ref, dmix_ref, dsc_ref, dsh_ref, dg1_ref, db1_ref, dga_ref, acc_ref):
        i, f = pl.program_id(0), pl.program_id(1)

        @pl.when((i == 0) & (f == 0))
        def _():
            for ref in (dsc_ref, dsh_ref, dg1_ref, db1_ref, dga_ref):
                ref[...] = jnp.zeros_like(ref)

        @pl.when(f == 0)
        def _():
            acc_ref[...] = jnp.zeros_like(acc_ref)

        dht = dh_ref[...]
        part = None
        for s in range(MLP_SLABS):
            cols = slice(s * tf, (s + 1) * tf)
            dhpre = (_dot(dht, w2_ref[s], NT) * (2.0 * r_ref[:, cols].astype(F32))).astype(BF16)
            dhpre_ref[:, cols] = dhpre
            d = _dot(dhpre, w1_ref[s], NT)
            part = d if part is None else part + d
        acc_ref[...] += part

        @pl.when(f == nf - 1)
        def _():
            du2 = acc_ref[...]
            xhat = xhat_ref[...]
            x1 = xhat * g1_ref[...] + b1_ref[...]
            dx1 = ALPHA * dr2_ref[...] + du2 * (1.0 + sc_ref[...])
            dsc_ref[...] += _rowsum(du2 * x1)
            dsh_ref[...] += _rowsum(du2)
            dg1_ref[...] += _rowsum(dx1 * xhat)
            db1_ref[...] += _rowsum(dx1)
            dr1 = _ln_bwd(dx1 * g1_ref[...], xhat, rstd_ref[:, 0:1])
            dr1_ref[...] = dr1
            dga_ref[...] += _rowsum(dr1 * mix_ref[...])
            dmix_ref[...] = ((1.0 + ga_ref[...]) * dr1).astype(BF16)

    row = pl.BlockSpec((tm, D_MODEL), lambda i, f: (i, 0))
    vec = _full((1, D_MODEL))
    return pl.pallas_call(
        body, name="mlp_bwd", grid=(T // tm, nf),
        in_specs=[row, pl.BlockSpec((MLP_SLABS, D_MODEL, tf), lambda i, f: (f, 0, 0)),
                  pl.BlockSpec((MLP_SLABS, tf, D_MODEL), lambda i, f: (f, 0, 0)),
                  pl.BlockSpec((tm, MLP_SLABS * tf), lambda i, f: (i, f)), row, row,
                  pl.BlockSpec((tm, 128), lambda i, f: (i, 0)), row, vec, vec, vec, vec],
        out_specs=[pl.BlockSpec((tm, MLP_SLABS * tf), lambda i, f: (i, f)), row, row, vec, vec, vec, vec, vec],
        out_shape=[jax.ShapeDtypeStruct((T, N_DEV * tf), BF16), jax.ShapeDtypeStruct((T, D_MODEL), F32),
                   jax.ShapeDtypeStruct((T, D_MODEL), BF16)] + [jax.ShapeDtypeStruct((1, D_MODEL), F32)] * 5,
        scratch_shapes=[pltpu.VMEM((tm, D_MODEL), F32)],
        compiler_params=_params(),
    )(dh, w1, w2, r, dr2, xhat1, rstd1, mix, ln1_g, ln1_b, sc_m, g_a)


def _input_bwd(dz_h, dz_m, w_in_ext, x, dr1, sc_a, exchange=None):
    T = x.shape[0]
    tm = min(ROW_TILE_SMALL, T)

    def body(dzh_ref, dzm_ref, w_ref, x_ref, dr1_ref, sc_ref, gx_ref, dsc_ref, dsh_ref):
        @pl.when(pl.program_id(0) == 0)
        def _():
            dsc_ref[...] = jnp.zeros_like(dsc_ref)
            dsh_ref[...] = jnp.zeros_like(dsh_ref)

        du = _bdot(dzh_ref[...], w_ref[0:2048, :]) + _bdot(dzm_ref[...], w_ref[2048:3072, :])
        gx_ref[...] = ALPHA * dr1_ref[...] + du * (1.0 + sc_ref[...])
        dsc_ref[...] += _rowsum(du * x_ref[...])
        dsh_ref[...] += _rowsum(du)

    row = pl.BlockSpec((tm, D_MODEL), lambda i: (i, 0))
    vec = _full((1, D_MODEL))
    return _call(
        body, "input_bwd", (dz_h, dz_m, w_in_ext, x, dr1, sc_a), grid=(T // tm,),
        in_specs=[pl.BlockSpec((tm, 2048), lambda i: (i, 0)), row, _full(w_in_ext.shape), row, row, vec],
        out_specs=[row, vec, vec],
        out_shape=[jax.ShapeDtypeStruct((T, D_MODEL), F32), jax.ShapeDtypeStruct((1, D_MODEL), F32),
                   jax.ShapeDtypeStruct((1, D_MODEL), F32)], exchange=exchange)


def _adam_math(w, g, m, v):
    m = ADAM_B1 * m + (1.0 - ADAM_B1) * g
    v = ADAM_B2 * v + (1.0 - ADAM_B2) * (g * g)
    m_hat = m / (1.0 - ADAM_B1 ** ADAM_STEP)
    v_hat = v / (1.0 - ADAM_B2 ** ADAM_STEP)
    return -ADAM_LR * (m_hat / (jnp.sqrt(v_hat) + ADAM_EPS) + ADAM_WD * w), m, v


def _adam(g_slabs, w, m, v, name, g_fn=None, g_extra=()):
    R, C = w.shape
    tr = 256 if R % 256 == 0 else R
    ns = 0 if g_slabs is None else g_slabs.shape[0]
    ne = len(g_extra)

    def body(*refs):
        e_refs = refs[:ne]
        refs = refs[ne:]
        if ns:
            gs_ref, refs = refs[0], refs[1:]
        w_ref, m_ref, v_ref, g_ref, d_ref, nm_ref, nv_ref = refs
        if g_fn is not None:
            g = g_fn(*e_refs)
        else:
            g = gs_ref[0].astype(F32)
            for s in range(1, ns):
                g = g + gs_ref[s].astype(F32)
        d, nm, nv = _adam_math(w_ref[...], g, m_ref[...], v_ref[...])
        g_ref[...] = g
        d_ref[...] = d
        nm_ref[...] = nm
        nv_ref[...] = nv

    blk = pl.BlockSpec((tr, C), lambda i: (i, 0))
    in_specs = [pl.BlockSpec((tr, e.shape[1]), lambda i: (i, 0)) if e.shape[0] == R else _full(e.shape) for e in g_extra]
    args = list(g_extra)
    if ns:
        in_specs.append(pl.BlockSpec((ns, tr, C), lambda i: (0, i, 0)))
        args.append(g_slabs)
    return pl.pallas_call(
        body, name=name, grid=(R // tr,), in_specs=in_specs + [blk] * 3, out_specs=[blk] * 4,
        out_shape=[jax.ShapeDtypeStruct((R, C), F32)] * 4, compiler_params=_params(),
    )(*args, w, m, v)


def _adam_small(small_all, params):
    n = len(params)

    def body(*refs):
        s_ref, refs = refs[0], refs[1:]
        wmv, loss_ref, outs = refs[:3 * n], refs[3 * n], refs[3 * n + 1:]
        tot = s_ref[0]
        for i in range(1, N_DEV):
            tot = tot + s_ref[i]
        loss_ref[...] = tot[:, SMALL_W - 128:]
        for j, (w, _, _, off) in enumerate(params):
            w_ref, m_ref, v_ref = wmv[3 * j:3 * j + 3]
            g_ref, d_ref, nm_ref, nv_ref = outs[4 * j:4 * j + 4]
            if w.shape[0] == 2:
                lb = _lower_bound(w_ref)
                g0 = tot[:, off:off + w.shape[1]] * lb * (1.0 - lb)
                rows = [(slice(0, 1), g0), (slice(1, 2), -g0)]
            else:
                rows = [(slice(0, 1), tot[:, off:off + w.shape[1]])]
            for rs, g in rows:
                d, nm, nv = _adam_math(w_ref[rs, :], g, m_ref[rs, :], v_ref[rs, :])
                g_ref[rs, :], d_ref[rs, :], nm_ref[rs, :], nv_ref[rs, :] = g, d, nm, nv

    out_shape = [jax.ShapeDtypeStruct((1, 128), F32)]
    for w, _, _, _ in params:
        out_shape += [jax.ShapeDtypeStruct(w.shape, F32)] * 4
    res = pl.pallas_call(body, name="adam_small", out_shape=out_shape, compiler_params=_params())(
        small_all, *[a for w, m, v, _ in params for a in (w, m, v)])
    return res[0], [tuple(res[1 + 4 * j:5 + 4 * j]) for j in range(n)]


def _cols_from_slabs(g):
    s, r, c = g.shape
    return jnp.transpose(g, (1, 0, 2)).reshape(r, s * c)


def _slabs_from_cols(w):
    r, c = w.shape
    return jnp.transpose(w.reshape(r, N_DEV, c // N_DEV), (1, 0, 2))


def _rot_half_rows(wt):
    return jnp.concatenate([-wt[32:], wt[:32]], axis=0)


def _unrot_half_rows(dwt_rot):
    return jnp.concatenate([dwt_rot[32:], -dwt_rot[:32]], axis=0)


def _ext_in_t(g):
    k_in = g.shape[2]
    z64, z128 = jnp.zeros((64, k_in), BF16), jnp.zeros((128, k_in), BF16)
    last = g.shape[1] - ROPE_DIM
    wk = g[N_DEV - 1, last:]
    return jnp.concatenate([g[i] for i in range(N_DEV - 1)] + [g[N_DEV - 1, :last], z128, wk, z64, z128,
                                                               _rot_half_rows(wk), z64], axis=0)


def _ext_q_t(wt):
    r = wt.shape[1]
    z64, z128 = jnp.zeros((64, r), BF16), jnp.zeros((128, r), BF16)
    per = HEAD_DIM + ROPE_DIM
    main = [jnp.concatenate([wt[per * h:per * (h + 1)], z64], axis=0) for h in range(HEADS)]
    rot = [jnp.concatenate([z128, _rot_half_rows(wt[per * h + HEAD_DIM:per * (h + 1)]), z64], axis=0)
           for h in range(HEADS)]
    return jnp.concatenate(main + rot, axis=0)


def _ext_kv(w_kv_up):
    r = w_kv_up.shape[0]
    z128 = jnp.zeros((r, 128), BF16)
    wkv = w_kv_up.reshape(r, HEADS, 2 * HEAD_DIM)
    kpad = [jnp.concatenate([wkv[:, h, :HEAD_DIM], z128], axis=1) for h in range(HEADS)]
    vals = [wkv[:, h, HEAD_DIM:] for h in range(HEADS)]
    return jnp.concatenate(kpad + vals, axis=1)


def _grad_in_from_ext_t(dwt_h, dwt_m):
    dwk = dwt_m[512 + 128:512 + 192] + _unrot_half_rows(dwt_m[768 + 128:768 + 192])
    return jnp.concatenate([dwt_h, dwt_m[:512], dwk], axis=0)


def _grad_q_from_ext_t(dwq_ext_t):
    rows = []
    for h in range(HEADS):
        main, rot = dwq_ext_t[256 * h:256 * h + 256], dwq_ext_t[1024 + 256 * h:1280 + 256 * h]
        rows += [main[:128], main[128:192] + _unrot_half_rows(rot[128:192])]
    return jnp.concatenate(rows, axis=0)


def _grad_kv_from_ext(dwkv_ext):
    kvcols = []
    for h in range(HEADS):
        kvcols += [dwkv_ext[:, 256 * h:256 * h + 128], dwkv_ext[:, 1024 + 128 * h:1152 + 128 * h]]
    return jnp.concatenate(kvcols, axis=1)


SMALL_W = 6144 + 512 + 512 + 256 + 256 + 4 * 1024 + 128


def kernel(x, c, positions, w_ada, b_ada, w_in, hg_lower_bounds, hg_norm_w, mla_q_norm_w, w_q_up, mla_kv_norm_w, w_kv_up, w_out, ln1_g, ln1_b, w_mlp_in, w_mlp_out, ln2_g, ln2_b, loss_target, m_w_ada, m_b_ada, m_w_in, m_hg_lower_bounds, m_hg_norm_w, m_mla_q_norm_w, m_w_q_up, m_mla_kv_norm_w, m_w_kv_up, m_w_out, m_ln1_g, m_ln1_b, m_w_mlp_in, m_w_mlp_out, m_ln2_g, m_ln2_b, v_w_ada, v_b_ada, v_w_in, v_hg_lower_bounds, v_hg_norm_w, v_mla_q_norm_w, v_w_q_up, v_mla_kv_norm_w, v_w_kv_up, v_w_out, v_ln1_g, v_ln1_b, v_w_mlp_in, v_w_mlp_out, v_ln2_g, v_ln2_b):
    T = x.shape[1]
    me = 4 * lax.axis_index("x") + 2 * lax.axis_index("y") + lax.axis_index("c")
    xs, tgt = x[0], loss_target[0]
    transposed = ("w_in", "w_q_up")
    as_used = lambda n, a: a[0].T if n in transposed else a[0]
    big = {n: as_used(n, a) for n, a in dict(w_in=w_in, w_q_up=w_q_up, w_kv_up=w_kv_up, w_out=w_out,
                                              w_mlp_in=w_mlp_in, w_mlp_out=w_mlp_out).items()}
    names = list(big)

    bf = {n: big[n].astype(BF16) for n in names}
    g_in, g_c = _gather_two_level([bf["w_in"], c], name="gather_w_in")
    c_all = g_c.reshape(N_DEV, D_MODEL)

    ada_cols = w_ada.shape[2]
    mod_part, cond = _mod_part(c_all, w_ada[0], lax.dynamic_slice(b_ada, (0, me * ada_cols), (1, ada_cols)))
    (mod_all,) = _exchange([mod_part], scatter=False, name="gather_mod")
    mod_row = lax.dynamic_slice(mod_all, (0, me, 0), (N_DEV, 1, ada_cols)).reshape(1, N_DEV * ada_cols)
    sh_a, sc_a, g_a, sh_m, sc_m, g_m = [mod_row[:, D_MODEL * i:D_MODEL * (i + 1)] for i in range(6)]

    w_in_ext = _ext_in_t(g_in)
    z, (g_q, g_kv, g_out) = _matmul(xs, w_in_ext, "NT", "in_proj", a_fn=_modulate, extras=(sc_a, sh_a), tn=3072,
                                    exchange=_Exchange([bf["w_q_up"], bf["w_kv_up"], bf["w_out"]], False))
    wq_ext = _ext_q_t(g_q.reshape(N_DEV * g_q.shape[1], g_q.shape[2]))
    wkv_ext = _ext_kv(_cols_from_slabs(g_kv))
    w_out_full = g_out.reshape(D_MODEL, D_MODEL)
    inv_freq = 1.0 / (ROPE_THETA ** (jnp.arange(0, ROPE_DIM, 2, dtype=F32) / ROPE_DIM))
    zeros = lambda n: jnp.zeros((n,), F32)
    invf = jnp.concatenate([zeros(128), inv_freq, inv_freq, zeros(64)]).reshape(1, QK_PAD)
    m_one = jnp.concatenate([jnp.ones((128,), F32), zeros(128)]).reshape(1, QK_PAD)
    m_rot = jnp.concatenate([zeros(128), jnp.ones((64,), F32), zeros(64)]).reshape(1, QK_PAD)
    q, k, v, c1, s1, cqn, ckvn = _mla_pre(z, positions.reshape(T, 1), invf, m_one, m_rot, wq_ext, wkv_ext,
                                          mla_q_norm_w, mla_kv_norm_w)
    (o_raw, o_gated, s_prev), (w1,) = _hgrn_fwd(z, hg_lower_bounds, hg_norm_w,
                                                exchange=_StagedGather(bf["w_mlp_in"]))
    (o_mla, lse), (w2,) = _attn_fwd(q, k, v, exchange=_StagedGather(bf["w_mlp_out"]))
    mix, xhat1, rstd1, u2 = _mix_ln1(o_gated, o_mla, w_out_full, xs, g_a, ln1_g, ln1_b, sc_m, sh_m)[0]
    r, dr2, dh, dln2_g, dln2_b, dg_m, loss_part = _mlp_fwd(u2, w1, w2, xhat1, ln1_g, ln1_b, g_m, ln2_g, ln2_b, tgt)

    dhpre, dr1, dmix, dsc_m, dsh_m, dln1_g, dln1_b, dg_a = _mlp_bwd(dh, w1, w2, r, dr2, xhat1, rstd1, mix, ln1_g,
                                                                    ln1_b, sc_m, g_a)
    received = {}
    dw2 = _matmul(r, dh, "TN", "wgrad_mlp_out", out_dtype=BF16, a_fn=_square, tm=1024)
    dw1 = _matmul(u2, dhpre, "TN", "wgrad_mlp_in", out_dtype=BF16, tm=1024, out_slabs=N_DEV)
    dmixcat = _matmul(dmix, w_out_full, "NT", "dgrad_out")
    dw_out = jnp.concatenate([_matmul(o_gated, dmix, "TN", "wgrad_out_hg", out_dtype=BF16),
                              _matmul(o_mla, dmix, "TN", "wgrad_out_mla", out_dtype=BF16)], axis=0)
    (dz_h, dlb, dnw), (received["w_out"],) = _hgrn_bwd(
        dmixcat, z, o_raw, s_prev, hg_lower_bounds, hg_norm_w,
        exchange=_Exchange([dw_out.reshape(N_DEV, D_MODEL // N_DEV, D_MODEL)], True))
    (dq, dk, dv), (received["w_mlp_in"], received["w_mlp_out"]) = _attn_bwd(
        q, k, v, dmixcat, o_mla, lse,
        exchange=_Exchange([dw1, dw2.reshape(N_DEV, dw2.shape[0] // N_DEV, D_MODEL)], True))
    dz_m, dq_ext, dkv_ext, dqnw, dkvnw = _mla_bwd(dq, dk, dv, z, c1, s1, wq_ext, wkv_ext, mla_q_norm_w,
                                                   mla_kv_norm_w)
    dwq_t = _grad_q_from_ext_t(_matmul(dq_ext, cqn, "TN", "wgrad_q_up", tm=1024))
    dwkv = _grad_kv_from_ext(_matmul(ckvn, dkv_ext, "TN", "wgrad_kv_up", tn=1536))
    qkv_slabs = [dwq_t.reshape((N_DEV, dwq_t.shape[0] // N_DEV, dwq_t.shape[1])).astype(BF16),
                 _slabs_from_cols(dwkv).astype(BF16)]
    dwt_h, (received["w_q_up"], received["w_kv_up"]) = _matmul(
        dz_h, xs, "TN", "wgrad_in_h", b_fn=_modulate, extras=(sc_a, sh_a), tm=1024,
        exchange=_Exchange(qkv_slabs, True))
    dwt_m = _matmul(dz_m, xs, "TN", "wgrad_in_m", b_fn=_modulate, extras=(sc_a, sh_a), tm=1024)
    dw_in_t = _grad_in_from_ext_t(dwt_h, dwt_m)
    in_slabs = dw_in_t.reshape((N_DEV, dw_in_t.shape[0] // N_DEV, dw_in_t.shape[1])).astype(BF16)
    (grad_x, dsc_a, dsh_a), (received["w_in"],) = _input_bwd(
        dz_h, dz_m, w_in_ext, xs, dr1, sc_a, exchange=_Exchange([in_slabs], True))

    small = jnp.concatenate([dsh_a, dsc_a, dg_a, dsh_m, dsc_m, dg_m, dlb, dnw, dqnw, dkvnw, dln1_g, dln1_b, dln2_g,
                             dln2_b, loss_part], axis=1)
    (small_all,) = _exchange([small], scatter=False, name="gather_small")

    moments = dict(w_in=(m_w_in, v_w_in), w_q_up=(m_w_q_up, v_w_q_up), w_kv_up=(m_w_kv_up, v_w_kv_up),
                   w_out=(m_w_out, v_w_out), w_mlp_in=(m_w_mlp_in, v_w_mlp_in), w_mlp_out=(m_w_mlp_out, v_w_mlp_out))
    res = {}
    for n in names:
        res[n] = _adam(received[n], big[n], as_used(n, moments[n][0]), as_used(n, moments[n][1]), name="adam_" + n)
    dmod_cols = lax.dynamic_slice(small_all.reshape(N_DEV, SMALL_W), (0, me * ada_cols), (N_DEV, ada_cols))
    cond_t = cond.T

    def ada_grad(ct_ref, dm_ref):
        g = ct_ref[:, 0:1] * dm_ref[0:1, :]
        for b in range(1, N_DEV):
            g = g + ct_ref[:, b:b + 1] * dm_ref[b:b + 1, :]
        return g

    res["w_ada"] = _adam(None, w_ada[0], m_w_ada[0], v_w_ada[0], name="adam_w_ada", g_fn=ada_grad,
                         g_extra=(cond_t, dmod_cols))

    small_params = [("b_ada", b_ada, m_b_ada, v_b_ada, 0),
                    ("hg_lower_bounds", hg_lower_bounds, m_hg_lower_bounds, v_hg_lower_bounds, 6144),
                    ("hg_norm_w", hg_norm_w, m_hg_norm_w, v_hg_norm_w, 6656),
                    ("mla_q_norm_w", mla_q_norm_w, m_mla_q_norm_w, v_mla_q_norm_w, 7168),
                    ("mla_kv_norm_w", mla_kv_norm_w, m_mla_kv_norm_w, v_mla_kv_norm_w, 7424),
                    ("ln1_g", ln1_g, m_ln1_g, v_ln1_g, 7680), ("ln1_b", ln1_b, m_ln1_b, v_ln1_b, 8704),
                    ("ln2_g", ln2_g, m_ln2_g, v_ln2_g, 9728), ("ln2_b", ln2_b, m_ln2_b, v_ln2_b, 10752)]
    loss_row, small_res = _adam_small(small_all, [p[1:] for p in small_params])
    for p, r4 in zip(small_params, small_res):
        res[p[0]] = r4
    loss = loss_row[0, 0]

    order = ["w_ada", "b_ada", "w_in", "hg_lower_bounds", "hg_norm_w", "mla_q_norm_w", "w_q_up", "mla_kv_norm_w",
             "w_kv_up", "w_out", "ln1_g", "ln1_b", "w_mlp_in", "w_mlp_out", "ln2_g", "ln2_b"]
    def as_given(n, a):
        if n in transposed:
            a = a.T
        return a[None] if n in big or n == "w_ada" else a

    shaped = {n: tuple(as_given(n, a) for a in res[n]) for n in order}
    outs = [loss, grad_x.reshape(1, T, D_MODEL)]
    for i in range(4):
        outs += [shaped[n][i] for n in order]
    return tuple(outs)
```

```python
import functools

import jax
import jax.numpy as jnp
import numpy as np
from jax import lax
from jax.experimental import pallas as pl
from jax.experimental.pallas import tpu as pltpu

F32, BF16 = jnp.float32, jnp.bfloat16
N_DEV = 8
D_MODEL = 1024
HEADS = 4
HEAD_DIM = 128
ROPE_DIM = 64
QK_PAD = 256
CHUNK = 64
ROPE_THETA = 10000.0
RMS_EPS = 1e-6
LN_EPS = 1e-5
ALPHA = 2.0 ** 0.25
ATT_SCALE = (HEAD_DIM + ROPE_DIM) ** -0.5
LN2 = float(np.log(2.0))
Q_PRESCALE = ATT_SCALE / LN2
ADAM_LR, ADAM_B1, ADAM_B2, ADAM_EPS, ADAM_WD, ADAM_STEP = 0.001, 0.9, 0.999, 1e-08, 0.01, 10
NEG_BIG = -1e30

ROW_TILE = 512
ROW_TILE_SMALL = 256
ATT_TILE = 512
HGRN_GROUP = 8
MLP_SLABS = 4
VMEM_LIMIT = 56 * 2 ** 20

NN = (((1,), (0,)), ((), ()))
NT = (((1,), (1,)), ((), ()))
TN = (((0,), (0,)), ((), ()))


def _dot(a, b, dims=NN):
    return lax.dot_general(a, b, dims, preferred_element_type=F32)


def _bdot(a, b, dims=NN):
    return lax.dot_general(a.astype(BF16), b.astype(BF16), dims, preferred_element_type=F32)


def _hdot(a, b, dims=NN):
    return lax.dot_general(a, b, dims, precision=lax.Precision.HIGHEST, preferred_element_type=F32)


def _params():
    return pltpu.CompilerParams(vmem_limit_bytes=VMEM_LIMIT)


def _sigmoid(x):
    return 1.0 / (1.0 + jnp.exp(-x))


def _rowsum(x):
    return jnp.sum(x, axis=0, keepdims=True)


def _lanemean(x):
    return jnp.mean(x, axis=-1, keepdims=True)


def _full(shape):
    nd = len(shape)
    return pl.BlockSpec(shape, lambda *_: (0,) * nd)


class _Exchange:
    def __init__(self, arrs, scatter):
        self.arrs, self.scatter, self.n = list(arrs), scatter, len(arrs)
        self.out_shape = [jax.ShapeDtypeStruct((N_DEV,) + (a.shape[1:] if scatter else a.shape), a.dtype)
                          for a in self.arrs]
        n = self.n
        self.scratch = [pltpu.SemaphoreType.DMA((n, N_DEV - 1)), pltpu.SemaphoreType.DMA((n, N_DEV - 1)),
                        pltpu.SemaphoreType.DMA((n,))]

    def _copies(self, ins, outs, sems):
        send_sems, recv_sems, loc_sems = sems
        x, y, c = lax.axis_index("x"), lax.axis_index("y"), lax.axis_index("c")
        me = 4 * x + 2 * y + c
        copies = []
        for k in range(self.n):
            src_of = (lambda i, k=k: ins[k].at[i]) if self.scatter else (lambda i, k=k: ins[k])
            copies.append((pltpu.make_async_copy(src_of(me), outs[k].at[me], loc_sems.at[k]), None))
            for p in range(1, N_DEV):
                px = (1 - x) if p & 4 else x
                py = (1 - y) if p & 2 else y
                pc = (1 - c) if p & 1 else c
                peer = 4 * px + 2 * py + pc
                both = dict(send_sem=send_sems.at[k, p - 1], recv_sem=recv_sems.at[k, p - 1],
                            device_id=(px, py, pc), device_id_type=pl.DeviceIdType.MESH)
                send = pltpu.make_async_remote_copy(src_ref=src_of(peer), dst_ref=outs[k].at[me], **both)
                recv = pltpu.make_async_remote_copy(src_ref=src_of(peer), dst_ref=outs[k].at[peer], **both)
                copies.append((send, recv))
        return copies

    def start(self, ins, outs, sems):
        for first, _ in self._copies(ins, outs, sems):
            first.start()

    def middle(self, ins, outs, sems):
        pass

    def wait(self, ins, outs, sems):
        for first, recv in self._copies(ins, outs, sems):
            if recv is None:
                first.wait()
            else:
                recv.wait_recv()
                first.wait_send()


class _StagedGather:
    def __init__(self, arr):
        self.arrs, self.n = [arr], 1
        self.out_shape = [jax.ShapeDtypeStruct((N_DEV,) + arr.shape, arr.dtype)]
        self.scratch = [pltpu.VMEM((N_DEV,) + arr.shape, arr.dtype), pltpu.SemaphoreType.DMA((7,)),
                        pltpu.SemaphoreType.DMA((7,)), pltpu.SemaphoreType.DMA((2,))]

    def _parts(self, scr):
        stage, send_sems, recv_sems, loc_sems = scr
        x, y, c = lax.axis_index("x"), lax.axis_index("y"), lax.axis_index("c")
        me, sibling = (x, y, c), (x, y, 1 - c)
        chips = [(1 - x, y), (x, 1 - y), (1 - x, 1 - y)]

        def copy(j, block, to):
            px, py, pc = block
            slot = stage.at[4 * px + 2 * py + pc]
            return pltpu.make_async_remote_copy(src_ref=slot, dst_ref=slot, send_sem=send_sems.at[j],
                                                recv_sem=recv_sems.at[j], device_id=to,
                                                device_id_type=pl.DeviceIdType.MESH)

        return stage, loc_sems, me, sibling, chips, c, copy

    def start(self, ins, outs, scr):
        stage, loc_sems, me, sibling, chips, c, copy = self._parts(scr)
        x, y, _ = me
        own = pltpu.make_async_copy(ins[0], stage.at[4 * x + 2 * y + c], loc_sems.at[0])
        own.start()
        own.wait()
        copy(0, me, sibling).start()
        for j, chip in enumerate(chips):
            copy(1 + j, me, (*chip, c)).start()

    def middle(self, ins, outs, scr):
        stage, loc_sems, me, sibling, chips, c, copy = self._parts(scr)
        for j, chip in enumerate(chips):
            copy(1 + j, (*chip, c), me).wait_recv()
            copy(4 + j, (*chip, c), sibling).start()

    def wait(self, ins, outs, scr):
        stage, loc_sems, me, sibling, chips, c, copy = self._parts(scr)
        copy(0, sibling, me).wait_recv()
        for j, chip in enumerate(chips):
            copy(4 + j, (*chip, 1 - c), me).wait_recv()
        copy(0, me, sibling).wait_send()
        for j, chip in enumerate(chips):
            copy(1 + j, me, (*chip, c)).wait_send()
            copy(4 + j, (*chip, c), sibling).wait_send()
        whole = pltpu.make_async_copy(stage, outs[0], loc_sems.at[1])
        whole.start()
        whole.wait()


def _call(body, name, args, out_shape, grid=(), in_specs=(), out_specs=(), scratch_shapes=(), exchange=None,
          middle_at=0.8):
    if exchange is None:
        return pl.pallas_call(body, name=name, grid=grid, in_specs=list(in_specs), out_specs=list(out_specs),
                              out_shape=list(out_shape), scratch_shapes=list(scratch_shapes),
                              compiler_params=_params())(*args), None
    exs = list(exchange) if isinstance(exchange, (list, tuple)) else [exchange]
    ni, no, ns, nx = len(args), len(out_shape), len(scratch_shapes), sum(e.n for e in exs)
    steps = int(np.prod(grid))
    mid_step = min(max(int(steps * middle_at), 1), steps - 1)

    def wrapped(*refs):
        a, xi = refs[:ni], refs[ni:ni + nx]
        o, xo = refs[ni + nx:ni + nx + no], refs[ni + nx + no:ni + 2 * nx + no]
        s, xs = refs[ni + 2 * nx + no:ni + 2 * nx + no + ns], refs[ni + 2 * nx + no + ns:]
        parts, at, sat = [], 0, 0
        for e in exs:
            parts.append((e, xi[at:at + e.n], xo[at:at + e.n], xs[sat:sat + len(e.scratch)]))
            at, sat = at + e.n, sat + len(e.scratch)
        step = 0
        for d, g in enumerate(grid):
            step = step * g + pl.program_id(d)

        @pl.when(step == 0)
        def _():
            for e, ins, outs, sems in parts:
                e.start(ins, outs, sems)

        @pl.when(step == mid_step)
        def _():
            for e, ins, outs, sems in parts:
                e.middle(ins, outs, sems)

        body(*a, *o, *s)

        @pl.when(step == steps - 1)
        def _():
            for e, ins, outs, sems in parts:
                e.wait(ins, outs, sems)

    hbm = pl.BlockSpec(memory_space=pltpu.HBM)
    res = pl.pallas_call(
        wrapped, name=name, grid=grid, in_specs=list(in_specs) + [hbm] * nx, out_specs=list(out_specs) + [hbm] * nx,
        out_shape=list(out_shape) + [o_ for e in exs for o_ in e.out_shape],
        scratch_shapes=list(scratch_shapes) + [s_ for e in exs for s_ in e.scratch],
        compiler_params=_params())(*args, *[a_ for e in exs for a_ in e.arrs])
    return res[:no], res[no:]


def _gather_two_level(arrs, name):
    n = len(arrs)
    out_shape = [jax.ShapeDtypeStruct((N_DEV,) + a.shape, a.dtype) for a in arrs]

    def body(*refs):
        ins, outs = refs[:n], refs[n:2 * n]
        send_sems, recv_sems, loc_sems = refs[2 * n:]
        x, y, c = lax.axis_index("x"), lax.axis_index("y"), lax.axis_index("c")
        me, sibling = (x, y, c), (x, y, 1 - c)
        chips = [(1 - x, y), (x, 1 - y), (1 - x, 1 - y)]

        def copy(k, j, block, to, src=None):
            px, py, pc = block
            dst = outs[k].at[4 * px + 2 * py + pc]
            return pltpu.make_async_remote_copy(src_ref=dst if src is None else src, dst_ref=dst,
                                                send_sem=send_sems.at[k, j], recv_sem=recv_sems.at[k, j],
                                                device_id=to, device_id_type=pl.DeviceIdType.MESH)

        mine = [pltpu.make_async_copy(ins[k], outs[k].at[4 * x + 2 * y + c], loc_sems.at[k]) for k in range(n)]
        first = []
        for k in range(n):
            mine[k].start()
            first.append(copy(k, 0, me, sibling, src=ins[k]))
            first += [copy(k, 1 + j, me, (*chip, c), src=ins[k]) for j, chip in enumerate(chips)]
        for cp in first:
            cp.start()
        passed = []
        for j, chip in enumerate(chips):
            for k in range(n):
                copy(k, 1 + j, (*chip, c), me).wait_recv()
                passed.append(copy(k, 4 + j, (*chip, c), sibling))
                passed[-1].start()
        for k in range(n):
            copy(k, 0, sibling, me).wait_recv()
            for j, chip in enumerate(chips):
                copy(k, 4 + j, (*chip, 1 - c), me).wait_recv()
        for cp in first + passed:
            cp.wait_send()
        for cp in mine:
            cp.wait()

    vmem = pl.BlockSpec(memory_space=pltpu.VMEM)
    return pl.pallas_call(body, name=name, out_shape=out_shape, in_specs=[vmem] * n, out_specs=[vmem] * n,
                          scratch_shapes=[pltpu.SemaphoreType.DMA((n, 7)), pltpu.SemaphoreType.DMA((n, 7)),
                                          pltpu.SemaphoreType.DMA((n,))], compiler_params=_params())(*arrs)


def _exchange(arrs, scatter, name):
    ex = _Exchange(arrs, scatter)

    def body(*refs):
        ins, outs, sems = refs[:ex.n], refs[ex.n:2 * ex.n], refs[2 * ex.n:]
        ex.start(ins, outs, sems)
        ex.wait(ins, outs, sems)

    hbm = pl.BlockSpec(memory_space=pltpu.HBM)
    return pl.pallas_call(body, name=name, out_shape=ex.out_shape, in_specs=[hbm] * ex.n, out_specs=[hbm] * ex.n,
                          scratch_shapes=ex.scratch)(*ex.arrs)


def _matmul(a, b, mode, name, out_dtype=F32, tm=512, tn=1024, tk=1024, a_fn=None, b_fn=None, extras=(),
            out_slabs=None, exchange=None):
    assert not (a_fn and b_fn) and not (b_fn and mode == "NT")
    if mode == "NN":
        (M, K), N = a.shape, b.shape[1]
    elif mode == "NT":
        (M, K), N = a.shape, b.shape[0]
    else:
        (K, M), N = a.shape, b.shape[1]
    if out_slabs:
        tn = N // out_slabs
    tm, tn, tk = min(tm, M), min(tn, N), min(tk, K)
    assert M % tm == 0 and N % tn == 0 and K % tk == 0, (name, M, N, K)
    nk = K // tk
    dims = {"NN": NN, "NT": NT, "TN": TN}[mode]
    ne = len(extras)

    def body(a_ref, b_ref, *rest):
        e_refs, o_ref, acc_ref = rest[:ne], rest[ne], rest[ne + 1]
        k = pl.program_id(2)

        @pl.when(k == 0)
        def _():
            acc_ref[...] = jnp.zeros_like(acc_ref)

        at, bt = a_ref[...], b_ref[...]
        if a_fn is not None:
            at = a_fn(at.astype(F32), *[e[...] for e in e_refs])
        if b_fn is not None:
            bt = b_fn(bt.astype(F32), *[e[...] for e in e_refs])
        acc_ref[...] += _bdot(at, bt, dims)

        @pl.when(k == nk - 1)
        def _():
            o_ref[...] = acc_ref[...].astype(out_dtype)

    if mode == "TN":
        a_spec = pl.BlockSpec((tk, tm), lambda i, j, k: (k, i))
        e_spec = pl.BlockSpec((1, tm), lambda i, j, k: (0, i))
    else:
        a_spec = pl.BlockSpec((tm, tk), lambda i, j, k: (i, k))
        e_spec = pl.BlockSpec((1, tk), lambda i, j, k: (0, k))
    if mode == "NT":
        b_spec = pl.BlockSpec((tn, tk), lambda i, j, k: (j, k))
    else:
        b_spec = pl.BlockSpec((tk, tn), lambda i, j, k: (k, j))
    if b_fn is not None:
        e_spec = pl.BlockSpec((1, tn), lambda i, j, k: (0, j))
    if out_slabs:
        o_shape = jax.ShapeDtypeStruct((out_slabs, M, tn), out_dtype)
        o_spec = pl.BlockSpec((None, tm, tn), lambda i, j, k: (j, i, 0))
    else:
        o_shape = jax.ShapeDtypeStruct((M, N), out_dtype)
        o_spec = pl.BlockSpec((tm, tn), lambda i, j, k: (i, j))
    (out,), got = _call(body, name, (a, b, *extras), [o_shape], grid=(M // tm, N // tn, nk),
                        in_specs=[a_spec, b_spec] + [e_spec] * ne, out_specs=[o_spec],
                        scratch_shapes=[pltpu.VMEM((tm, tn), F32)], exchange=exchange)
    return out if exchange is None else (out, got)


def _modulate(x, sc, sh):
    return x * (1.0 + sc) + sh


def _square(x):
    return x * x


def _mod_part(c_all, w_ada_s, b_s):
    def body(c_ref, w_ref, b_ref, mod_ref, cond_ref):
        cv = c_ref[...]
        cond = cv * _sigmoid(cv)
        cond_ref[...] = cond
        mod_ref[...] = _bdot(cond, w_ref[...]) + b_ref[...]

    return pl.pallas_call(
        body, name="mod_part",
        out_shape=[jax.ShapeDtypeStruct((N_DEV, w_ada_s.shape[1]), F32), jax.ShapeDtypeStruct(c_all.shape, F32)],
        compiler_params=_params(),
    )(c_all, w_ada_s, b_s)


def _rms_fwd(x, w):
    rs = lax.rsqrt(_lanemean(x * x) + RMS_EPS)
    return x * rs * w, rs


def _rms_bwd(x, rs, w, dy):
    xhat = x * rs
    dxh = dy * w
    return rs * (dxh - xhat * _lanemean(dxh * xhat)), dy * xhat


def _mla_pre(z, pos_col, invf, m_one, m_rot, wq_ext, wkv_ext, qnw, kvnw):
    T = z.shape[0]
    tm = min(ROW_TILE, T)

    def body(z_ref, pos_ref, invf_ref, mone_ref, mrot_ref, wq_ref, wkv_ref, qnw_ref, kvnw_ref,
             q_ref, k_ref, v_ref, c1_ref, s1_ref, cqn_ref, ckvn_ref):
        ang = pos_ref[...].astype(F32) * invf_ref[...]
        c1 = mone_ref[...] + mrot_ref[...] * jnp.cos(ang)
        s1 = mrot_ref[...] * jnp.sin(ang)
        c1_ref[...] = c1
        s1_ref[...] = s1
        cqn, _ = _rms_fwd(z_ref[:, 0:256], qnw_ref[...])
        ckvn, _ = _rms_fwd(z_ref[:, 256:512], kvnw_ref[...])
        cqn_ref[...] = cqn.astype(BF16)
        ckvn_ref[...] = ckvn.astype(BF16)
        qe = _bdot(cqn, wq_ref[...], NT)
        kve = _bdot(ckvn, wkv_ref[...])
        k_rope = z_ref[:, 512:768] * c1 + z_ref[:, 768:1024] * s1
        for h in range(HEADS):
            q_ref[h] = ((qe[:, 256 * h:256 * h + 256] * c1 + qe[:, 1024 + 256 * h:1280 + 256 * h] * s1)
                        * Q_PRESCALE).astype(BF16)
            k_ref[h] = (kve[:, 256 * h:256 * h + 256] + k_rope).astype(BF16)
            v_ref[h] = kve[:, 1024 + 128 * h:1152 + 128 * h].astype(BF16)

    row = lambda i: (i, 0)
    head = lambda i: (0, i, 0)
    return pl.pallas_call(
        body, name="mla_pre", grid=(T // tm,),
        in_specs=[pl.BlockSpec((tm, 1024), lambda i: (i, 2)), pl.BlockSpec((tm, 1), row),
                  _full((1, 256)), _full((1, 256)), _full((1, 256)), _full(wq_ext.shape), _full(wkv_ext.shape),
                  _full((1, 256)), _full((1, 256))],
        out_specs=[pl.BlockSpec((HEADS, tm, QK_PAD), head), pl.BlockSpec((HEADS, tm, QK_PAD), head),
                   pl.BlockSpec((HEADS, tm, HEAD_DIM), head), pl.BlockSpec((tm, 256), row), pl.BlockSpec((tm, 256), row),
                   pl.BlockSpec((tm, 256), row), pl.BlockSpec((tm, 256), row)],
        out_shape=[jax.ShapeDtypeStruct((HEADS, T, QK_PAD), BF16), jax.ShapeDtypeStruct((HEADS, T, QK_PAD), BF16),
                   jax.ShapeDtypeStruct((HEADS, T, HEAD_DIM), BF16), jax.ShapeDtypeStruct((T, 256), F32),
                   jax.ShapeDtypeStruct((T, 256), F32), jax.ShapeDtypeStruct((T, 256), BF16),
                   jax.ShapeDtypeStruct((T, 256), BF16)],
        compiler_params=_params(),
    )(z, pos_col, invf, m_one, m_rot, wq_ext, wkv_ext, qnw, kvnw)


def _mla_bwd(dq, dk, dv, z, c1, s1, wq_ext, wkv_ext, qnw, kvnw):
    T = z.shape[0]
    tm = min(ROW_TILE_SMALL, T)

    def body(dq_ref, dk_ref, dv_ref, z_ref, c1_ref, s1_ref, wq_ref, wkv_ref, qnw_ref, kvnw_ref,
             dz_ref, dqe_ref, dkve_ref, dqnw_ref, dkvnw_ref):
        @pl.when(pl.program_id(0) == 0)
        def _():
            dqnw_ref[...] = jnp.zeros_like(dqnw_ref)
            dkvnw_ref[...] = jnp.zeros_like(dkvnw_ref)

        c1, s1 = c1_ref[...], s1_ref[...]
        dkpe = jnp.zeros((tm, QK_PAD), F32)
        for h in range(HEADS):
            dqh, dkh = dq_ref[h] * Q_PRESCALE, dk_ref[h]
            dqe_ref[:, 256 * h:256 * h + 256] = (dqh * c1).astype(BF16)
            dqe_ref[:, 1024 + 256 * h:1280 + 256 * h] = (dqh * s1).astype(BF16)
            dkve_ref[:, 256 * h:256 * h + 256] = dkh.astype(BF16)
            dkve_ref[:, 1024 + 128 * h:1152 + 128 * h] = dv_ref[h].astype(BF16)
            dkpe = dkpe + dkh
        dcqn = _dot(dqe_ref[...], wq_ref[...])
        dckvn = _dot(dkve_ref[...], wkv_ref[...], NT)
        cq, ckv = z_ref[:, 0:256], z_ref[:, 256:512]
        _, rsq = _rms_fwd(cq, qnw_ref[...])
        _, rskv = _rms_fwd(ckv, kvnw_ref[...])
        dcq, wq_rows = _rms_bwd(cq, rsq, qnw_ref[...], dcqn)
        dckv, wkv_rows = _rms_bwd(ckv, rskv, kvnw_ref[...], dckvn)
        dqnw_ref[...] += _rowsum(wq_rows)
        dkvnw_ref[...] += _rowsum(wkv_rows)
        dz_ref[:, 0:256] = dcq
        dz_ref[:, 256:512] = dckv
        dz_ref[:, 512:768] = dkpe * c1
        dz_ref[:, 768:1024] = dkpe * s1

    row = lambda i: (i, 0)
    head = lambda i: (0, i, 0)
    return pl.pallas_call(
        body, name="mla_bwd", grid=(T // tm,),
        in_specs=[pl.BlockSpec((HEADS, tm, QK_PAD), head), pl.BlockSpec((HEADS, tm, QK_PAD), head),
                  pl.BlockSpec((HEADS, tm, HEAD_DIM), head), pl.BlockSpec((tm, 1024), lambda i: (i, 2)),
                  pl.BlockSpec((tm, 256), row), pl.BlockSpec((tm, 256), row), _full(wq_ext.shape), _full(wkv_ext.shape),
                  _full((1, 256)), _full((1, 256))],
        out_specs=[pl.BlockSpec((tm, 1024), row), pl.BlockSpec((tm, 2048), row), pl.BlockSpec((tm, 1536), row),
                   _full((1, 256)), _full((1, 256))],
        out_shape=[jax.ShapeDtypeStruct((T, 1024), F32), jax.ShapeDtypeStruct((T, 2048), BF16),
                   jax.ShapeDtypeStruct((T, 1536), BF16), jax.ShapeDtypeStruct((1, 256), F32),
                   jax.ShapeDtypeStruct((1, 256), F32)],
        compiler_params=_params(),
    )(dq, dk, dv, z, c1, s1, wq_ext, wkv_ext, qnw, kvnw)


_HEAD_LANES = [slice(HEAD_DIM * h, HEAD_DIM * (h + 1)) for h in range(HEADS)]


def _lower_bound(lbraw_ref):
    a0, a1 = lbraw_ref[0:1, :], lbraw_ref[1:2, :]
    mx = jnp.maximum(a0, a1)
    e0, e1 = jnp.exp(a0 - mx), jnp.exp(a1 - mx)
    return e0 / (e0 + e1)


def _tri(lower):
    r = lax.broadcasted_iota(jnp.int32, (CHUNK, CHUNK), 0)
    c = lax.broadcasted_iota(jnp.int32, (CHUNK, CHUNK), 1)
    return (r >= c) if lower else (r <= c)


def _hgrn_gates(q, f, lb, tri_lo):
    sg = _sigmoid(f)
    forget = lb + (1.0 - lb) * sg
    k = 1.0 - forget
    b = _hdot(tri_lo.astype(F32), jnp.log(forget))
    b_ref, b_last = b[CHUNK // 2 - 1:CHUNK // 2, :], b[CHUNK - 1:CHUNK, :]
    e1, e2, e3, e4 = jnp.exp(b - b_ref), jnp.exp(b_ref - b), jnp.exp(b_last - b), jnp.exp(b)
    return dict(sg=sg, forget=forget, k=k, e1=e1, e2=e2, e3=e3, e4=e4, qa=q * e1, ka=k * e2, kl=k * e3, qb=q * e4,
                decay=jnp.exp(b_last))


def _hgrn_fwd(z, lbraw, nw, exchange=None):
    T = z.shape[0]
    G = min(HGRN_GROUP, T // CHUNK)
    rows = G * CHUNK
    n_chunks = T // CHUNK

    def body(q_ref, f_ref, i_ref, g_ref, lbraw_ref, nw_ref, oraw_ref, og_ref, sp_ref, st_ref):
        @pl.when(pl.program_id(0) == 0)
        def _():
            st_ref[...] = jnp.zeros_like(st_ref)

        lb_all = _lower_bound(lbraw_ref)
        tri_lo = _tri(True)

        def chunk(cc, carry):
            rs = pl.ds(pl.multiple_of(cc * CHUNK, CHUNK), CHUNK)
            t = _hgrn_gates(q_ref[rs, :], f_ref[rs, :], lb_all, tri_lo)
            v, gate = i_ref[rs, :], g_ref[rs, :]
            st = [st_ref[h] for h in range(HEADS)]
            a = [jnp.where(tri_lo, _bdot(t["qa"][:, s], t["ka"][:, s], NT), 0.0) for s in _HEAD_LANES]
            kv = [_bdot(v[:, s], t["kl"][:, s], TN) for s in _HEAD_LANES]
            o = [_bdot(a[h], v[:, s]) + _bdot(t["qb"][:, s], st[h], NT) for h, s in enumerate(_HEAD_LANES)]
            for h, s in enumerate(_HEAD_LANES):
                sp_ref[cc, h] = st[h]
                st_ref[h] = st[h] * t["decay"][:, s] + kv[h]
            oraw_ref[rs, :] = jnp.concatenate(o, axis=1)
            on = jnp.concatenate([_rms_fwd(o[h], nw_ref[:, s])[0] for h, s in enumerate(_HEAD_LANES)], axis=1)
            og_ref[rs, :] = (on * (gate * _sigmoid(gate))).astype(BF16)
            return carry

        lax.fori_loop(0, G, chunk, 0, unroll=2)

    col = lambda j: pl.BlockSpec((rows, 512), lambda r, j=j: (r, j))
    return _call(
        body, "hgrn_fwd", (z, z, z, z, lbraw, nw), grid=(T // rows,),
        in_specs=[col(0), col(1), col(2), col(3), _full((2, 512)), _full((1, 512))],
        out_specs=[col(0), col(0), pl.BlockSpec((G, HEADS, HEAD_DIM, HEAD_DIM), lambda r: (r, 0, 0, 0))],
        out_shape=[jax.ShapeDtypeStruct((T, 512), F32), jax.ShapeDtypeStruct((T, 512), BF16),
                   jax.ShapeDtypeStruct((n_chunks, HEADS, HEAD_DIM, HEAD_DIM), F32)],
        scratch_shapes=[pltpu.VMEM((HEADS, HEAD_DIM, HEAD_DIM), F32)], exchange=exchange)


def _hgrn_bwd(dmixcat, z, oraw, sprev, lbraw, nw, exchange=None):
    T = z.shape[0]
    G = min(HGRN_GROUP, T // CHUNK)
    rows = G * CHUNK
    ng = T // rows

    def body(dog_ref, q_ref, f_ref, i_ref, g_ref, oraw_ref, sp_ref, lbraw_ref, nw_ref,
             dz_ref, dlb_ref, dnw_ref, dst_ref):
        @pl.when(pl.program_id(0) == 0)
        def _():
            dst_ref[...] = jnp.zeros_like(dst_ref)
            dlb_ref[...] = jnp.zeros_like(dlb_ref)
            dnw_ref[...] = jnp.zeros_like(dnw_ref)

        lb_all = _lower_bound(lbraw_ref)
        tri_lo, tri_up = _tri(True), _tri(False)
        rowid = lax.broadcasted_iota(jnp.int32, (CHUNK, HEADS * HEAD_DIM), 0)

        def chunk(it, carry):
            cc = G - 1 - it
            rs = pl.ds(pl.multiple_of(cc * CHUNK, CHUNK), CHUNK)
            heads = list(enumerate(_HEAD_LANES))
            cat = lambda parts: jnp.concatenate(parts, axis=1)
            per_head_mean = lambda x: cat([jnp.broadcast_to(_lanemean(x[:, s]), (CHUNK, HEAD_DIM)) for s in _HEAD_LANES])
            t = _hgrn_gates(q_ref[rs, :], f_ref[rs, :], lb_all, tri_lo)
            v, gate, o, dog, nw_all = i_ref[rs, :], g_ref[rs, :], oraw_ref[rs, :], dog_ref[rs, :], nw_ref[...]
            rs_o = lax.rsqrt(per_head_mean(o * o) + RMS_EPS)
            xhat = o * rs_o
            sgg = _sigmoid(gate)
            d_on = dog * (gate * sgg)
            dz_ref[rs, 1536:2048] = dog * (xhat * nw_all) * (sgg * (1.0 + gate * (1.0 - sgg)))
            dxh = d_on * nw_all
            do = rs_o * (dxh - xhat * per_head_mean(dxh * xhat))
            dnw_ref[...] += _rowsum(d_on * xhat)
            st = [sp_ref[cc, h] for h in range(HEADS)]
            dst = [dst_ref[h] for h in range(HEADS)]
            a = [jnp.where(tri_lo, _bdot(t["qa"][:, s], t["ka"][:, s], NT), 0.0) for s in _HEAD_LANES]
            da = [jnp.where(tri_lo, _bdot(do[:, s], v[:, s], NT), 0.0) for s in _HEAD_LANES]
            dqb = cat([_bdot(do[:, s], st[h]) for h, s in heads])
            dkl = cat([_bdot(v[:, s], dst[h]) for h, s in heads])
            dv_ = cat([_bdot(t["kl"][:, s], dst[h], NT) + _bdot(a[h], do[:, s], TN) for h, s in heads])
            dqa = cat([_bdot(da[h], t["ka"][:, s]) for h, s in heads])
            dka = cat([_bdot(da[h], t["qa"][:, s], TN) for h, s in heads])
            ddecay = cat([_rowsum(dst[h] * st[h]) for h in range(HEADS)])
            for h, s in heads:
                dst_ref[h] = dst[h] * t["decay"][:, s] + _bdot(do[:, s], t["qb"][:, s], TN)
            pa, pk, pb, pl_ = dqa * t["qa"], dka * t["ka"], dqb * t["qb"], dkl * t["kl"]
            db = pa - pk + pb - pl_
            db = db + jnp.where(rowid == CHUNK // 2 - 1, _rowsum(pk - pa), 0.0)
            db = db + jnp.where(rowid == CHUNK - 1, _rowsum(pl_) + ddecay * t["decay"], 0.0)
            dlogf = _hdot(tri_up.astype(F32), db)
            dforget = dlogf / t["forget"] - (dka * t["e2"] + dkl * t["e3"])
            sg = t["sg"]
            dz_ref[rs, 0:512] = dqa * t["e1"] + dqb * t["e4"]
            dz_ref[rs, 512:1024] = dforget * (1.0 - lb_all) * sg * (1.0 - sg)
            dz_ref[rs, 1024:1536] = dv_
            dlb_ref[...] += _rowsum(dforget * (1.0 - sg))
            return carry

        lax.fori_loop(0, G, chunk, 0, unroll=2)

    col = lambda j: pl.BlockSpec((rows, 512), lambda r, j=j: (ng - 1 - r, j))
    return _call(
        body, "hgrn_bwd", (dmixcat, z, z, z, z, oraw, sprev, lbraw, nw), grid=(ng,),
        in_specs=[col(0), col(0), col(1), col(2), col(3), col(0),
                  pl.BlockSpec((G, HEADS, HEAD_DIM, HEAD_DIM), lambda r: (ng - 1 - r, 0, 0, 0)),
                  _full((2, 512)), _full((1, 512))],
        out_specs=[pl.BlockSpec((rows, 2048), lambda r: (ng - 1 - r, 0)), _full((1, 512)), _full((1, 512))],
        out_shape=[jax.ShapeDtypeStruct((T, 2048), F32), jax.ShapeDtypeStruct((1, 512), F32),
                   jax.ShapeDtypeStruct((1, 512), F32)],
        scratch_shapes=[pltpu.VMEM((HEADS, HEAD_DIM, HEAD_DIM), F32)], exchange=exchange)


def _diag_mask(t):
    r = lax.broadcasted_iota(jnp.int32, (t, t), 0)
    c = lax.broadcasted_iota(jnp.int32, (t, t), 1)
    return r >= c


def _attn_fwd(q, k, v, exchange=None):
    _, T, _ = q.shape
    t = min(ATT_TILE, T)

    def body(q_ref, k_ref, v_ref, o_ref, lse_ref):
        i = pl.program_id(1)
        qb = q_ref[...]

        rows = lambda j: pl.ds(pl.multiple_of(j * t, t), t)

        def logits(j, masked):
            s = _dot(qb, k_ref[rows(j), :], NT)
            return jnp.where(_diag_mask(t), s, NEG_BIG) if masked else s

        def absorb(s, j, carry):
            m, l, acc = carry
            mn = jnp.maximum(m, jnp.max(s, axis=-1, keepdims=True))
            p = jnp.exp2(s - mn)
            al = jnp.exp2(m - mn)
            return mn, al * l + jnp.sum(p, axis=-1, keepdims=True), al * acc + _dot(p.astype(BF16), v_ref[rows(j), :])

        def pair(j0, carry, last_masked):
            s0, s1 = logits(j0, False), logits(j0 + 1, last_masked)
            return absorb(s1, j0 + 1, absorb(s0, j0, carry))

        init = (jnp.full((t, 1), NEG_BIG, F32), jnp.zeros((t, 1), F32), jnp.zeros((t, HEAD_DIM), F32))
        carry = lax.fori_loop(0, i // 2, lambda jj, c: pair(2 * jj, c, False), init)
        m, l, acc = lax.cond(i % 2 == 1, lambda c: pair(i - 1, c, True),
                             lambda c: absorb(logits(i, True), i, c), carry)
        o_ref[...] = acc / l
        lse_ref[...] = jnp.broadcast_to(m + jnp.log2(l), (t, HEAD_DIM))

    return _call(
        body, "attn_fwd", (q, k, v), grid=(HEADS, T // t),
        in_specs=[pl.BlockSpec((None, t, QK_PAD), lambda h, i: (h, i, 0)),
                  pl.BlockSpec((None, T, QK_PAD), lambda h, i: (h, 0, 0)),
                  pl.BlockSpec((None, T, HEAD_DIM), lambda h, i: (h, 0, 0))],
        out_specs=[pl.BlockSpec((t, HEAD_DIM), lambda h, i: (i, h)),
                   pl.BlockSpec((None, t, HEAD_DIM), lambda h, i: (h, i, 0))],
        out_shape=[jax.ShapeDtypeStruct((T, HEADS * HEAD_DIM), F32), jax.ShapeDtypeStruct((HEADS, T, HEAD_DIM), F32)],
        exchange=exchange)


def _attn_bwd(q, k, v, dmixcat, o, lse, exchange=None):
    _, T, _ = q.shape
    t = min(ATT_TILE, T)
    nq = T // t

    def body(q_ref, k_ref, v_ref, do_ref, o_ref, lse_ref, dq_ref, dk_ref, dv_ref, delta_ref):
        j = pl.program_id(1)

        @pl.when(j == 0)
        def _():
            dq_ref[...] = jnp.zeros_like(dq_ref)

            def fill(i, carry):
                rs = pl.ds(pl.multiple_of(i * t, t), t)
                delta_ref[rs, :] = jnp.broadcast_to(
                    jnp.sum(do_ref[rs, :] * o_ref[rs, :], axis=-1, keepdims=True), (t, HEAD_DIM))
                return carry

            lax.fori_loop(0, nq, fill, 0)

        kb, vb = k_ref[...], v_ref[...]

        def steps(blocks, carry):
            dk, dv = carry
            rs = [pl.ds(pl.multiple_of(i * t, t), t) for i, _ in blocks]
            qb = [q_ref[r, :] for r in rs]
            dob = [do_ref[r, :].astype(BF16) for r in rs]
            s = [_dot(b, kb, NT) for b in qb]
            dp = [_dot(b, vb, NT) for b in dob]
            for n, (_, masked) in enumerate(blocks):
                p = jnp.exp2(s[n] - lse_ref[rs[n], 0:1])
                if masked:
                    p = jnp.where(_diag_mask(t), p, 0.0)
                ds = (p * (dp[n] - delta_ref[rs[n], 0:1]) * LN2).astype(BF16)
                dq_ref[rs[n], :] += _dot(ds, kb)
                dk = dk + _dot(ds, qb[n], TN)
                dv = dv + _dot(p.astype(BF16), dob[n], TN)
            return dk, dv

        zero = (jnp.zeros((t, QK_PAD), F32), jnp.zeros((t, HEAD_DIM), F32))
        rest = nq - 1 - j
        carry = lax.cond(rest % 2 == 1, lambda c: steps([(j, True), (j + 1, False)], c),
                         lambda c: steps([(j, True)], c), zero)
        first = j + 1 + rest % 2
        dk, dv = lax.fori_loop(0, rest // 2, lambda n, c: steps([(first + 2 * n, False), (first + 2 * n + 1, False)], c),
                               carry)
        dk_ref[...] = dk
        dv_ref[...] = dv

    return _call(
        body, "attn_bwd", (q, k, v, dmixcat, o, lse), grid=(HEADS, nq),
        in_specs=[pl.BlockSpec((None, T, QK_PAD), lambda h, j: (h, 0, 0)),
                  pl.BlockSpec((None, t, QK_PAD), lambda h, j: (h, j, 0)),
                  pl.BlockSpec((None, t, HEAD_DIM), lambda h, j: (h, j, 0)),
                  pl.BlockSpec((T, HEAD_DIM), lambda h, j: (0, HEADS + h)),
                  pl.BlockSpec((T, HEAD_DIM), lambda h, j: (0, h)),
                  pl.BlockSpec((None, T, HEAD_DIM), lambda h, j: (h, 0, 0))],
        out_specs=[pl.BlockSpec((None, T, QK_PAD), lambda h, j: (h, 0, 0)),
                   pl.BlockSpec((None, t, QK_PAD), lambda h, j: (h, j, 0)),
                   pl.BlockSpec((None, t, HEAD_DIM), lambda h, j: (h, j, 0))],
        out_shape=[jax.ShapeDtypeStruct((HEADS, T, QK_PAD), F32), jax.ShapeDtypeStruct((HEADS, T, QK_PAD), F32),
                   jax.ShapeDtypeStruct((HEADS, T, HEAD_DIM), F32)],
        scratch_shapes=[pltpu.VMEM((T, HEAD_DIM), F32)], exchange=exchange)


def _ln_fwd(r):
    mu = _lanemean(r)
    xc = r - mu
    rstd = lax.rsqrt(_lanemean(xc * xc) + LN_EPS)
    return xc * rstd, rstd


def _ln_bwd(dxh, xhat, rstd):
    return rstd * (dxh - _lanemean(dxh) - xhat * _lanemean(dxh * xhat))


def _mix_ln1(o_hg, o_mla, w_out, x, g_a, ln1_g, ln1_b, sc_m, sh_m, exchange=None):
    T = x.shape[0]
    tm = min(ROW_TILE_SMALL, T)
    half = o_hg.shape[1]

    def body(hg_ref, mla_ref, w_ref, x_ref, ga_ref, g_ref, b_ref, sc_ref, sh_ref, mix_ref, xhat_ref, rstd_ref, u2_ref):
        mix = _dot(hg_ref[...], w_ref[0:half, :]) + _bdot(mla_ref[...], w_ref[half:, :])
        mix_ref[...] = mix
        xhat, rstd = _ln_fwd(ALPHA * x_ref[...] + (1.0 + ga_ref[...]) * mix)
        xhat_ref[...] = xhat
        rstd_ref[...] = jnp.broadcast_to(rstd, (tm, 128))
        u2_ref[...] = _modulate(xhat * g_ref[...] + b_ref[...], sc_ref[...], sh_ref[...]).astype(BF16)

    row = pl.BlockSpec((tm, D_MODEL), lambda i: (i, 0))
    vec = _full((1, D_MODEL))
    halfrow = pl.BlockSpec((tm, half), lambda i: (i, 0))
    return _call(
        body, "mix_ln1", (o_hg, o_mla, w_out, x, g_a, ln1_g, ln1_b, sc_m, sh_m), grid=(T // tm,),
        in_specs=[halfrow, halfrow, _full(w_out.shape), row, vec, vec, vec, vec, vec],
        out_specs=[row, row, pl.BlockSpec((tm, 128), lambda i: (i, 0)), row],
        out_shape=[jax.ShapeDtypeStruct((T, D_MODEL), F32), jax.ShapeDtypeStruct((T, D_MODEL), F32),
                   jax.ShapeDtypeStruct((T, 128), F32), jax.ShapeDtypeStruct((T, D_MODEL), BF16)],
        exchange=exchange)


def _mlp_fwd(u2, w1, w2, xhat1, ln1_g, ln1_b, g_m, ln2_g, ln2_b, target):
    T = u2.shape[0]
    tf = w1.shape[-1]
    nf = N_DEV // MLP_SLABS
    tm = min(ROW_TILE, T)

    def body(u2_ref, w1_ref, w2_ref, xhat_ref, g1_ref, b1_ref, gm_ref, g2_ref, b2_ref, tgt_ref,
             r_ref, dr2_ref, dh_ref, dg2_ref, db2_ref, dgm_ref, loss_ref, acc_ref):
        i, f = pl.program_id(0), pl.program_id(1)

        @pl.when((i == 0) & (f == 0))
        def _():
            for ref in (dg2_ref, db2_ref, dgm_ref, loss_ref):
                ref[...] = jnp.zeros_like(ref)

        @pl.when(f == 0)
        def _():
            acc_ref[...] = jnp.zeros_like(acc_ref)

        u2t = u2_ref[...]
        part = None
        for s in range(MLP_SLABS):
            r = jnp.maximum(_dot(u2t, w1_ref[s]), 0.0)
            r_ref[:, s * tf:(s + 1) * tf] = r.astype(BF16)
            d = _bdot(r * r, w2_ref[s])
            part = d if part is None else part + d
        acc_ref[...] += part

        @pl.when(f == nf - 1)
        def _():
            h = acc_ref[...]
            x1 = xhat_ref[...] * g1_ref[...] + b1_ref[...]
            xhat2, rstd2 = _ln_fwd(ALPHA * x1 + (1.0 + gm_ref[...]) * h)
            err = xhat2 * g2_ref[...] + b2_ref[...] - tgt_ref[...]
            loss_ref[...] += jnp.sum(0.5 * _lanemean(err * err), axis=0, keepdims=True)
            dy = err * (1.0 / D_MODEL)
            dg2_ref[...] += _rowsum(dy * xhat2)
            db2_ref[...] += _rowsum(dy)
            dr2 = _ln_bwd(dy * g2_ref[...], xhat2, rstd2)
            dr2_ref[...] = dr2
            dgm_ref[...] += _rowsum(dr2 * h)
            dh_ref[...] = ((1.0 + gm_ref[...]) * dr2).astype(BF16)

    row = pl.BlockSpec((tm, D_MODEL), lambda i, f: (i, 0))
    vec = _full((1, D_MODEL))
    return pl.pallas_call(
        body, name="mlp_fwd", grid=(T // tm, nf),
        in_specs=[row, pl.BlockSpec((MLP_SLABS, D_MODEL, tf), lambda i, f: (f, 0, 0)),
                  pl.BlockSpec((MLP_SLABS, tf, D_MODEL), lambda i, f: (f, 0, 0)),
                  row, vec, vec, vec, vec, vec, row],
        out_specs=[pl.BlockSpec((tm, MLP_SLABS * tf), lambda i, f: (i, f)), row, row, vec, vec, vec, _full((1, 128))],
        out_shape=[jax.ShapeDtypeStruct((T, N_DEV * tf), BF16), jax.ShapeDtypeStruct((T, D_MODEL), F32),
                   jax.ShapeDtypeStruct((T, D_MODEL), BF16), jax.ShapeDtypeStruct((1, D_MODEL), F32),
                   jax.ShapeDtypeStruct((1, D_MODEL), F32), jax.ShapeDtypeStruct((1, D_MODEL), F32),
                   jax.ShapeDtypeStruct((1, 128), F32)],
        scratch_shapes=[pltpu.VMEM((tm, D_MODEL), F32)],
        compiler_params=_params(),
    )(u2, w1, w2, xhat1, ln1_g, ln1_b, g_m, ln2_g, ln2_b, target)


def _mlp_bwd(dh, w1, w2, r, dr2, xhat1, rstd1, mix, ln1_g, ln1_b, sc_m, g_a):
    T = dh.shape[0]
    tf = w1.shape[-1]
    nf = N_DEV // MLP_SLABS
    tm = min(ROW_TILE, T)

    def body(dh_ref, w1_ref, w2_ref, r_ref, dr2_ref, xhat_ref, rstd_ref, mix_ref, g1_ref, b1_ref, sc_ref, ga_ref,
             dhpre_ref, dr1_ref, dmix_ref, dsc_ref, dsh_ref, dg1_ref, db1_ref, dga_ref, acc_ref):
        i, f = pl.program_id(0), pl.program_id(1)

        @pl.when((i == 0) & (f == 0))
        def _():
            for ref in (dsc_ref, dsh_ref, dg1_ref, db1_ref, dga_ref):
                ref[...] = jnp.zeros_like(ref)

        @pl.when(f == 0)
        def _():
            acc_ref[...] = jnp.zeros_like(acc_ref)

        dht = dh_ref[...]
        part = None
        for s in range(MLP_SLABS):
            cols = slice(s * tf, (s + 1) * tf)
            dhpre = (_dot(dht, w2_ref[s], NT) * (2.0 * r_ref[:, cols].astype(F32))).astype(BF16)
            dhpre_ref[:, cols] = dhpre
            d = _dot(dhpre, w1_ref[s], NT)
            part = d if part is None else part + d
        acc_ref[...] += part

        @pl.when(f == nf - 1)
        def _():
            du2 = acc_ref[...]
            xhat = xhat_ref[...]
            x1 = xhat * g1_ref[...] + b1_ref[...]
            dx1 = ALPHA * dr2_ref[...] + du2 * (1.0 + sc_ref[...])
            dsc_ref[...] += _rowsum(du2 * x1)
            dsh_ref[...] += _rowsum(du2)
            dg1_ref[...] += _rowsum(dx1 * xhat)
            db1_ref[...] += _rowsum(dx1)
            dr1 = _ln_bwd(dx1 * g1_ref[...], xhat, rstd_ref[:, 0:1])
            dr1_ref[...] = dr1
            dga_ref[...] += _rowsum(dr1 * mix_ref[...])
            dmix_ref[...] = ((1.0 + ga_ref[...]) * dr1).astype(BF16)

    row = pl.BlockSpec((tm, D_MODEL), lambda i, f: (i, 0))
    vec = _full((1, D_MODEL))
    return pl.pallas_call(
        body, name="mlp_bwd", grid=(T // tm, nf),
        in_specs=[row, pl.BlockSpec((MLP_SLABS, D_MODEL, tf), lambda i, f: (f, 0, 0)),
                  pl.BlockSpec((MLP_SLABS, tf, D_MODEL), lambda i, f: (f, 0, 0)),
                  pl.BlockSpec((tm, MLP_SLABS * tf), lambda i, f: (i, f)), row, row,
                  pl.BlockSpec((tm, 128), lambda i, f: (i, 0)), row, vec, vec, vec, vec],
        out_specs=[pl.BlockSpec((tm, MLP_SLABS * tf), lambda i, f: (i, f)), row, row, vec, vec, vec, vec, vec],
        out_shape=[jax.ShapeDtypeStruct((T, N_DEV * tf), BF16), jax.ShapeDtypeStruct((T, D_MODEL), F32),
                   jax.ShapeDtypeStruct((T, D_MODEL), BF16)] + [jax.ShapeDtypeStruct((1, D_MODEL), F32)] * 5,
        scratch_shapes=[pltpu.VMEM((tm, D_MODEL), F32)],
        compiler_params=_params(),
    )(dh, w1, w2, r, dr2, xhat1, rstd1, mix, ln1_g, ln1_b, sc_m, g_a)


def _input_bwd(dz_h, dz_m, w_in_ext, x, dr1, sc_a, exchange=None):
    T = x.shape[0]
    tm = min(ROW_TILE_SMALL, T)

    def body(dzh_ref, dzm_ref, w_ref, x_ref, dr1_ref, sc_ref, gx_ref, dsc_ref, dsh_ref):
        @pl.when(pl.program_id(0) == 0)
        def _():
            dsc_ref[...] = jnp.zeros_like(dsc_ref)
            dsh_ref[...] = jnp.zeros_like(dsh_ref)

        du = _bdot(dzh_ref[...], w_ref[0:2048, :]) + _bdot(dzm_ref[...], w_ref[2048:3072, :])
        gx_ref[...] = ALPHA * dr1_ref[...] + du * (1.0 + sc_ref[...])
        dsc_ref[...] += _rowsum(du * x_ref[...])
        dsh_ref[...] += _rowsum(du)

    row = pl.BlockSpec((tm, D_MODEL), lambda i: (i, 0))
    vec = _full((1, D_MODEL))
    return _call(
        body, "input_bwd", (dz_h, dz_m, w_in_ext, x, dr1, sc_a), grid=(T // tm,),
        in_specs=[pl.BlockSpec((tm, 2048), lambda i: (i, 0)), row, _full(w_in_ext.shape), row, row, vec],
        out_specs=[row, vec, vec],
        out_shape=[jax.ShapeDtypeStruct((T, D_MODEL), F32), jax.ShapeDtypeStruct((1, D_MODEL), F32),
                   jax.ShapeDtypeStruct((1, D_MODEL), F32)], exchange=exchange)


def _adam_math(w, g, m, v):
    m = ADAM_B1 * m + (1.0 - ADAM_B1) * g
    v = ADAM_B2 * v + (1.0 - ADAM_B2) * (g * g)
    m_hat = m / (1.0 - ADAM_B1 ** ADAM_STEP)
    v_hat = v / (1.0 - ADAM_B2 ** ADAM_STEP)
    return -ADAM_LR * (m_hat / (jnp.sqrt(v_hat) + ADAM_EPS) + ADAM_WD * w), m, v


def _adam(g_slabs, w, m, v, name, g_fn=None, g_extra=()):
    R, C = w.shape
    tr = 256 if R % 256 == 0 else R
    ns = 0 if g_slabs is None else g_slabs.shape[0]
    ne = len(g_extra)

    def body(*refs):
        e_refs = refs[:ne]
        refs = refs[ne:]
        if ns:
            gs_ref, refs = refs[0], refs[1:]
        w_ref, m_ref, v_ref, g_ref, d_ref, nm_ref, nv_ref = refs
        if g_fn is not None:
            g = g_fn(*e_refs)
        else:
            g = gs_ref[0].astype(F32)
            for s in range(1, ns):
                g = g + gs_ref[s].astype(F32)
        d, nm, nv = _adam_math(w_ref[...], g, m_ref[...], v_ref[...])
        g_ref[...] = g
        d_ref[...] = d
        nm_ref[...] = nm
        nv_ref[...] = nv

    blk = pl.BlockSpec((tr, C), lambda i: (i, 0))
    in_specs = [pl.BlockSpec((tr, e.shape[1]), lambda i: (i, 0)) if e.shape[0] == R else _full(e.shape) for e in g_extra]
    args = list(g_extra)
    if ns:
        in_specs.append(pl.BlockSpec((ns, tr, C), lambda i: (0, i, 0)))
        args.append(g_slabs)
    return pl.pallas_call(
        body, name=name, grid=(R // tr,), in_specs=in_specs + [blk] * 3, out_specs=[blk] * 4,
        out_shape=[jax.ShapeDtypeStruct((R, C), F32)] * 4, compiler_params=_params(),
    )(*args, w, m, v)


def _adam_small(small_all, params):
    n = len(params)

    def body(*refs):
        s_ref, refs = refs[0], refs[1:]
        wmv, loss_ref, outs = refs[:3 * n], refs[3 * n], refs[3 * n + 1:]
        tot = s_ref[0]
        for i in range(1, N_DEV):
            tot = tot + s_ref[i]
        loss_ref[...] = tot[:, SMALL_W - 128:]
        for j, (w, _, _, off) in enumerate(params):
            w_ref, m_ref, v_ref = wmv[3 * j:3 * j + 3]
            g_ref, d_ref, nm_ref, nv_ref = outs[4 * j:4 * j + 4]
            if w.shape[0] == 2:
                lb = _lower_bound(w_ref)
                g0 = tot[:, off:off + w.shape[1]] * lb * (1.0 - lb)
                rows = [(slice(0, 1), g0), (slice(1, 2), -g0)]
            else:
                rows = [(slice(0, 1), tot[:, off:off + w.shape[1]])]
            for rs, g in rows:
                d, nm, nv = _adam_math(w_ref[rs, :], g, m_ref[rs, :], v_ref[rs, :])
                g_ref[rs, :], d_ref[rs, :], nm_ref[rs, :], nv_ref[rs, :] = g, d, nm, nv

    out_shape = [jax.ShapeDtypeStruct((1, 128), F32)]
    for w, _, _, _ in params:
        out_shape += [jax.ShapeDtypeStruct(w.shape, F32)] * 4
    res = pl.pallas_call(body, name="adam_small", out_shape=out_shape, compiler_params=_params())(
        small_all, *[a for w, m, v, _ in params for a in (w, m, v)])
    return res[0], [tuple(res[1 + 4 * j:5 + 4 * j]) for j in range(n)]


def _cols_from_slabs(g):
    s, r, c = g.shape
    return jnp.transpose(g, (1, 0, 2)).reshape(r, s * c)


def _slabs_from_cols(w):
    r, c = w.shape
    return jnp.transpose(w.reshape(r, N_DEV, c // N_DEV), (1, 0, 2))


def _rot_half_rows(wt):
    return jnp.concatenate([-wt[32:], wt[:32]], axis=0)


def _unrot_half_rows(dwt_rot):
    return jnp.concatenate([dwt_rot[32:], -dwt_rot[:32]], axis=0)


def _ext_in_t(g):
    k_in = g.shape[2]
    z64, z128 = jnp.zeros((64, k_in), BF16), jnp.zeros((128, k_in), BF16)
    last = g.shape[1] - ROPE_DIM
    wk = g[N_DEV - 1, last:]
    return jnp.concatenate([g[i] for i in range(N_DEV - 1)] + [g[N_DEV - 1, :last], z128, wk, z64, z128,
                                                               _rot_half_rows(wk), z64], axis=0)


def _ext_q_t(wt):
    r = wt.shape[1]
    z64, z128 = jnp.zeros((64, r), BF16), jnp.zeros((128, r), BF16)
    per = HEAD_DIM + ROPE_DIM
    main = [jnp.concatenate([wt[per * h:per * (h + 1)], z64], axis=0) for h in range(HEADS)]
    rot = [jnp.concatenate([z128, _rot_half_rows(wt[per * h + HEAD_DIM:per * (h + 1)]), z64], axis=0)
           for h in range(HEADS)]
    return jnp.concatenate(main + rot, axis=0)


def _ext_kv(w_kv_up):
    r = w_kv_up.shape[0]
    z128 = jnp.zeros((r, 128), BF16)
    wkv = w_kv_up.reshape(r, HEADS, 2 * HEAD_DIM)
    kpad = [jnp.concatenate([wkv[:, h, :HEAD_DIM], z128], axis=1) for h in range(HEADS)]
    vals = [wkv[:, h, HEAD_DIM:] for h in range(HEADS)]
    return jnp.concatenate(kpad + vals, axis=1)


def _grad_in_from_ext_t(dwt_h, dwt_m):
    dwk = dwt_m[512 + 128:512 + 192] + _unrot_half_rows(dwt_m[768 + 128:768 + 192])
    return jnp.concatenate([dwt_h, dwt_m[:512], dwk], axis=0)


def _grad_q_from_ext_t(dwq_ext_t):
    rows = []
    for h in range(HEADS):
        main, rot = dwq_ext_t[256 * h:256 * h + 256], dwq_ext_t[1024 + 256 * h:1280 + 256 * h]
        rows += [main[:128], main[128:192] + _unrot_half_rows(rot[128:192])]
    return jnp.concatenate(rows, axis=0)


def _grad_kv_from_ext(dwkv_ext):
    kvcols = []
    for h in range(HEADS):
        kvcols += [dwkv_ext[:, 256 * h:256 * h + 128], dwkv_ext[:, 1024 + 128 * h:1152 + 128 * h]]
    return jnp.concatenate(kvcols, axis=1)


SMALL_W = 6144 + 512 + 512 + 256 + 256 + 4 * 1024 + 128


def kernel(x, c, positions, w_ada, b_ada, w_in, hg_lower_bounds, hg_norm_w, mla_q_norm_w, w_q_up, mla_kv_norm_w, w_kv_up, w_out, ln1_g, ln1_b, w_mlp_in, w_mlp_out, ln2_g, ln2_b, loss_target, m_w_ada, m_b_ada, m_w_in, m_hg_lower_bounds, m_hg_norm_w, m_mla_q_norm_w, m_w_q_up, m_mla_kv_norm_w, m_w_kv_up, m_w_out, m_ln1_g, m_ln1_b, m_w_mlp_in, m_w_mlp_out, m_ln2_g, m_ln2_b, v_w_ada, v_b_ada, v_w_in, v_hg_lower_bounds, v_hg_norm_w, v_mla_q_norm_w, v_w_q_up, v_mla_kv_norm_w, v_w_kv_up, v_w_out, v_ln1_g, v_ln1_b, v_w_mlp_in, v_w_mlp_out, v_ln2_g, v_ln2_b):
    T = x.shape[1]
    me = 4 * lax.axis_index("x") + 2 * lax.axis_index("y") + lax.axis_index("c")
    xs, tgt = x[0], loss_target[0]
    transposed = ("w_in", "w_q_up")
    as_used = lambda n, a: a[0].T if n in transposed else a[0]
    big = {n: as_used(n, a) for n, a in dict(w_in=w_in, w_q_up=w_q_up, w_kv_up=w_kv_up, w_out=w_out,
                                              w_mlp_in=w_mlp_in, w_mlp_out=w_mlp_out).items()}
    names = list(big)

    bf = {n: big[n].astype(BF16) for n in names}
    g_in, g_c = _gather_two_level([bf["w_in"], c], name="gather_w_in")
    c_all = g_c.reshape(N_DEV, D_MODEL)

    ada_cols = w_ada.shape[2]
    mod_part, cond = _mod_part(c_all, w_ada[0], lax.dynamic_slice(b_ada, (0, me * ada_cols), (1, ada_cols)))
    (mod_all,) = _exchange([mod_part], scatter=False, name="gather_mod")
    mod_row = lax.dynamic_slice(mod_all, (0, me, 0), (N_DEV, 1, ada_cols)).reshape(1, N_DEV * ada_cols)
    sh_a, sc_a, g_a, sh_m, sc_m, g_m = [mod_row[:, D_MODEL * i:D_MODEL * (i + 1)] for i in range(6)]

    w_in_ext = _ext_in_t(g_in)
    z, (g_q, g_kv, g_out) = _matmul(xs, w_in_ext, "NT", "in_proj", a_fn=_modulate, extras=(sc_a, sh_a), tn=3072,
                                    exchange=_Exchange([bf["w_q_up"], bf["w_kv_up"], bf["w_out"]], False))
    wq_ext = _ext_q_t(g_q.reshape(N_DEV * g_q.shape[1], g_q.shape[2]))
    wkv_ext = _ext_kv(_cols_from_slabs(g_kv))
    w_out_full = g_out.reshape(D_MODEL, D_MODEL)
    inv_freq = 1.0 / (ROPE_THETA ** (jnp.arange(0, ROPE_DIM, 2, dtype=F32) / ROPE_DIM))
    zeros = lambda n: jnp.zeros((n,), F32)
    invf = jnp.concatenate([zeros(128), inv_freq, inv_freq, zeros(64)]).reshape(1, QK_PAD)
    m_one = jnp.concatenate([jnp.ones((128,), F32), zeros(128)]).reshape(1, QK_PAD)
    m_rot = jnp.concatenate([zeros(128), jnp.ones((64,), F32), zeros(64)]).reshape(1, QK_PAD)
    q, k, v, c1, s1, cqn, ckvn = _mla_pre(z, positions.reshape(T, 1), invf, m_one, m_rot, wq_ext, wkv_ext,
                                          mla_q_norm_w, mla_kv_norm_w)
    (o_raw, o_gated, s_prev), (w1,) = _hgrn_fwd(z, hg_lower_bounds, hg_norm_w,
                                                exchange=_StagedGather(bf["w_mlp_in"]))
    (o_mla, lse), (w2,) = _attn_fwd(q, k, v, exchange=_StagedGather(bf["w_mlp_out"]))
    mix, xhat1, rstd1, u2 = _mix_ln1(o_gated, o_mla, w_out_full, xs, g_a, ln1_g, ln1_b, sc_m, sh_m)[0]
    r, dr2, dh, dln2_g, dln2_b, dg_m, loss_part = _mlp_fwd(u2, w1, w2, xhat1, ln1_g, ln1_b, g_m, ln2_g, ln2_b, tgt)

    dhpre, dr1, dmix, dsc_m, dsh_m, dln1_g, dln1_b, dg_a = _mlp_bwd(dh, w1, w2, r, dr2, xhat1, rstd1, mix, ln1_g,
                                                                    ln1_b, sc_m, g_a)
    received = {}
    dw2 = _matmul(r, dh, "TN", "wgrad_mlp_out", out_dtype=BF16, a_fn=_square, tm=1024)
    dw1 = _matmul(u2, dhpre, "TN", "wgrad_mlp_in", out_dtype=BF16, tm=1024, out_slabs=N_DEV)
    dmixcat = _matmul(dmix, w_out_full, "NT", "dgrad_out")
    dw_out = jnp.concatenate([_matmul(o_gated, dmix, "TN", "wgrad_out_hg", out_dtype=BF16),
                              _matmul(o_mla, dmix, "TN", "wgrad_out_mla", out_dtype=BF16)], axis=0)
    (dz_h, dlb, dnw), (received["w_mlp_in"],) = _hgrn_bwd(
        dmixcat, z, o_raw, s_prev, hg_lower_bounds, hg_norm_w, exchange=_Exchange([dw1], True))
    (dq, dk, dv), (received["w_mlp_out"], received["w_out"]) = _attn_bwd(
        q, k, v, dmixcat, o_mla, lse,
        exchange=_Exchange([dw2.reshape(N_DEV, dw2.shape[0] // N_DEV, D_MODEL),
                            dw_out.reshape(N_DEV, D_MODEL // N_DEV, D_MODEL)], True))
    dz_m, dq_ext, dkv_ext, dqnw, dkvnw = _mla_bwd(dq, dk, dv, z, c1, s1, wq_ext, wkv_ext, mla_q_norm_w,
                                                   mla_kv_norm_w)
    dwq_t = _grad_q_from_ext_t(_matmul(dq_ext, cqn, "TN", "wgrad_q_up", tm=1024))
    dwkv = _grad_kv_from_ext(_matmul(ckvn, dkv_ext, "TN", "wgrad_kv_up", tn=1536))
    qkv_slabs = [dwq_t.reshape((N_DEV, dwq_t.shape[0] // N_DEV, dwq_t.shape[1])).astype(BF16),
                 _slabs_from_cols(dwkv).astype(BF16)]
    dwt_h, (received["w_q_up"], received["w_kv_up"]) = _matmul(
        dz_h, xs, "TN", "wgrad_in_h", b_fn=_modulate, extras=(sc_a, sh_a), tm=1024,
        exchange=_Exchange(qkv_slabs, True))
    dwt_m = _matmul(dz_m, xs, "TN", "wgrad_in_m", b_fn=_modulate, extras=(sc_a, sh_a), tm=1024)
    dw_in_t = _grad_in_from_ext_t(dwt_h, dwt_m)
    in_slabs = dw_in_t.reshape((N_DEV, dw_in_t.shape[0] // N_DEV, dw_in_t.shape[1])).astype(BF16)
    (grad_x, dsc_a, dsh_a), (received["w_in"],) = _input_bwd(
        dz_h, dz_m, w_in_ext, xs, dr1, sc_a, exchange=_Exchange([in_slabs], True))

    small = jnp.concatenate([dsh_a, dsc_a, dg_a, dsh_m, dsc_m, dg_m, dlb, dnw, dqnw, dkvnw, dln1_g, dln1_b, dln2_g,
                             dln2_b, loss_part], axis=1)
    (small_all,) = _exchange([small], scatter=False, name="gather_small")

    moments = dict(w_in=(m_w_in, v_w_in), w_q_up=(m_w_q_up, v_w_q_up), w_kv_up=(m_w_kv_up, v_w_kv_up),
                   w_out=(m_w_out, v_w_out), w_mlp_in=(m_w_mlp_in, v_w_mlp_in), w_mlp_out=(m_w_mlp_out, v_w_mlp_out))
    res = {}
    for n in names:
        res[n] = _adam(received[n], big[n], as_used(n, moments[n][0]), as_used(n, moments[n][1]), name="adam_" + n)
    dmod_cols = lax.dynamic_slice(small_all.reshape(N_DEV, SMALL_W), (0, me * ada_cols), (N_DEV, ada_cols))
    cond_t = cond.T

    def ada_grad(ct_ref, dm_ref):
        g = ct_ref[:, 0:1] * dm_ref[0:1, :]
        for b in range(1, N_DEV):
            g = g + ct_ref[:, b:b + 1] * dm_ref[b:b + 1, :]
        return g

    res["w_ada"] = _adam(None, w_ada[0], m_w_ada[0], v_w_ada[0], name="adam_w_ada", g_fn=ada_grad,
                         g_extra=(cond_t, dmod_cols))

    small_params = [("b_ada", b_ada, m_b_ada, v_b_ada, 0),
                    ("hg_lower_bounds", hg_lower_bounds, m_hg_lower_bounds, v_hg_lower_bounds, 6144),
                    ("hg_norm_w", hg_norm_w, m_hg_norm_w, v_hg_norm_w, 6656),
                    ("mla_q_norm_w", mla_q_norm_w, m_mla_q_norm_w, v_mla_q_norm_w, 7168),
                    ("mla_kv_norm_w", mla_kv_norm_w, m_mla_kv_norm_w, v_mla_kv_norm_w, 7424),
                    ("ln1_g", ln1_g, m_ln1_g, v_ln1_g, 7680), ("ln1_b", ln1_b, m_ln1_b, v_ln1_b, 8704),
                    ("ln2_g", ln2_g, m_ln2_g, v_ln2_g, 9728), ("ln2_b", ln2_b, m_ln2_b, v_ln2_b, 10752)]
    loss_row, small_res = _adam_small(small_all, [p[1:] for p in small_params])
    for p, r4 in zip(small_params, small_res):
        res[p[0]] = r4
    loss = loss_row[0, 0]

    order = ["w_ada", "b_ada", "w_in", "hg_lower_bounds", "hg_norm_w", "mla_q_norm_w", "w_q_up", "mla_kv_norm_w",
             "w_kv_up", "w_out", "ln1_g", "ln1_b", "w_mlp_in", "w_mlp_out", "ln2_g", "ln2_b"]
    def as_given(n, a):
        if n in transposed:
            a = a.T
        return a[None] if n in big or n == "w_ada" else a

    shaped = {n: tuple(as_given(n, a) for a in res[n]) for n in order}
    outs = [loss, grad_x.reshape(1, T, D_MODEL)]
    for i in range(4):
        outs += [shaped[n][i] for n in order]
    return tuple(outs)
```

```python
import functools

import jax
import jax.numpy as jnp
import numpy as np
from jax import lax
from jax.experimental import pallas as pl
from jax.experimental.pallas import tpu as pltpu

F32, BF16 = jnp.float32, jnp.bfloat16
N_DEV = 8
D_MODEL = 1024
HEADS = 4
HEAD_DIM = 128
ROPE_DIM = 64
QK_PAD = 256
CHUNK = 64
ROPE_THETA = 10000.0
RMS_EPS = 1e-6
LN_EPS = 1e-5
ALPHA = 2.0 ** 0.25
ATT_SCALE = (HEAD_DIM + ROPE_DIM) ** -0.5
LN2 = float(np.log(2.0))
Q_PRESCALE = ATT_SCALE / LN2
ADAM_LR, ADAM_B1, ADAM_B2, ADAM_EPS, ADAM_WD, ADAM_STEP = 0.001, 0.9, 0.999, 1e-08, 0.01, 10
NEG_BIG = -1e30

ROW_TILE = 512
ROW_TILE_SMALL = 256
ATT_TILE = 512
HGRN_GROUP = 8
MLP_SLABS = 4
VMEM_LIMIT = 56 * 2 ** 20

NN = (((1,), (0,)), ((), ()))
NT = (((1,), (1,)), ((), ()))
TN = (((0,), (0,)), ((), ()))


def _dot(a, b, dims=NN):
    return lax.dot_general(a, b, dims, preferred_element_type=F32)


def _bdot(a, b, dims=NN):
    return lax.dot_general(a.astype(BF16), b.astype(BF16), dims, preferred_element_type=F32)


def _hdot(a, b, dims=NN):
    return lax.dot_general(a, b, dims, precision=lax.Precision.HIGHEST, preferred_element_type=F32)


def _params():
    return pltpu.CompilerParams(vmem_limit_bytes=VMEM_LIMIT)


def _sigmoid(x):
    return 1.0 / (1.0 + jnp.exp(-x))


def _rowsum(x):
    return jnp.sum(x, axis=0, keepdims=True)


def _lanemean(x):
    return jnp.mean(x, axis=-1, keepdims=True)


def _full(shape):
    nd = len(shape)
    return pl.BlockSpec(shape, lambda *_: (0,) * nd)


class _Exchange:
    def __init__(self, arrs, scatter):
        self.arrs, self.scatter, self.n = list(arrs), scatter, len(arrs)
        self.out_shape = [jax.ShapeDtypeStruct((N_DEV,) + (a.shape[1:] if scatter else a.shape), a.dtype)
                          for a in self.arrs]
        n = self.n
        self.scratch = [pltpu.SemaphoreType.DMA((n, N_DEV - 1)), pltpu.SemaphoreType.DMA((n, N_DEV - 1)),
                        pltpu.SemaphoreType.DMA((n,))]

    def _copies(self, ins, outs, sems):
        send_sems, recv_sems, loc_sems = sems
        x, y, c = lax.axis_index("x"), lax.axis_index("y"), lax.axis_index("c")
        me = 4 * x + 2 * y + c
        copies = []
        for k in range(self.n):
            src_of = (lambda i, k=k: ins[k].at[i]) if self.scatter else (lambda i, k=k: ins[k])
            copies.append((pltpu.make_async_copy(src_of(me), outs[k].at[me], loc_sems.at[k]), None))
            for p in range(1, N_DEV):
                px = (1 - x) if p & 4 else x
                py = (1 - y) if p & 2 else y
                pc = (1 - c) if p & 1 else c
                peer = 4 * px + 2 * py + pc
                both = dict(send_sem=send_sems.at[k, p - 1], recv_sem=recv_sems.at[k, p - 1],
                            device_id=(px, py, pc), device_id_type=pl.DeviceIdType.MESH)
                send = pltpu.make_async_remote_copy(src_ref=src_of(peer), dst_ref=outs[k].at[me], **both)
                recv = pltpu.make_async_remote_copy(src_ref=src_of(peer), dst_ref=outs[k].at[peer], **both)
                copies.append((send, recv))
        return copies

    def start(self, ins, outs, sems):
        for first, _ in self._copies(ins, outs, sems):
            first.start()

    def middle(self, ins, outs, sems):
        pass

    def wait(self, ins, outs, sems):
        for first, recv in self._copies(ins, outs, sems):
            if recv is None:
                first.wait()
            else:
                recv.wait_recv()
                first.wait_send()


class _StagedGather:
    def __init__(self, arr):
        self.arrs, self.n = [arr], 1
        self.out_shape = [jax.ShapeDtypeStruct((N_DEV,) + arr.shape, arr.dtype)]
        self.scratch = [pltpu.VMEM((N_DEV,) + arr.shape, arr.dtype), pltpu.SemaphoreType.DMA((7,)),
                        pltpu.SemaphoreType.DMA((7,)), pltpu.SemaphoreType.DMA((2,))]

    def _parts(self, scr):
        stage, send_sems, recv_sems, loc_sems = scr
        x, y, c = lax.axis_index("x"), lax.axis_index("y"), lax.axis_index("c")
        me, sibling = (x, y, c), (x, y, 1 - c)
        chips = [(1 - x, y), (x, 1 - y), (1 - x, 1 - y)]

        def copy(j, block, to):
            px, py, pc = block
            slot = stage.at[4 * px + 2 * py + pc]
            return pltpu.make_async_remote_copy(src_ref=slot, dst_ref=slot, send_sem=send_sems.at[j],
                                                recv_sem=recv_sems.at[j], device_id=to,
                                                device_id_type=pl.DeviceIdType.MESH)

        return stage, loc_sems, me, sibling, chips, c, copy

    def start(self, ins, outs, scr):
        stage, loc_sems, me, sibling, chips, c, copy = self._parts(scr)
        x, y, _ = me
        own = pltpu.make_async_copy(ins[0], stage.at[4 * x + 2 * y + c], loc_sems.at[0])
        own.start()
        own.wait()
        copy(0, me, sibling).start()
        for j, chip in enumerate(chips):
            copy(1 + j, me, (*chip, c)).start()

    def middle(self, ins, outs, scr):
        stage, loc_sems, me, sibling, chips, c, copy = self._parts(scr)
        for j, chip in enumerate(chips):
            copy(1 + j, (*chip, c), me).wait_recv()
            copy(4 + j, (*chip, c), sibling).start()

    def wait(self, ins, outs, scr):
        stage, loc_sems, me, sibling, chips, c, copy = self._parts(scr)
        copy(0, sibling, me).wait_recv()
        for j, chip in enumerate(chips):
            copy(4 + j, (*chip, 1 - c), me).wait_recv()
        copy(0, me, sibling).wait_send()
        for j, chip in enumerate(chips):
            copy(1 + j, me, (*chip, c)).wait_send()
            copy(4 + j, (*chip, c), sibling).wait_send()
        whole = pltpu.make_async_copy(stage, outs[0], loc_sems.at[1])
        whole.start()
        whole.wait()


def _call(body, name, args, out_shape, grid=(), in_specs=(), out_specs=(), scratch_shapes=(), exchange=None,
          middle_at=0.9):
    if exchange is None:
        return pl.pallas_call(body, name=name, grid=grid, in_specs=list(in_specs), out_specs=list(out_specs),
                              out_shape=list(out_shape), scratch_shapes=list(scratch_shapes),
                              compiler_params=_params())(*args), None
    exs = list(exchange) if isinstance(exchange, (list, tuple)) else [exchange]
    ni, no, ns, nx = len(args), len(out_shape), len(scratch_shapes), sum(e.n for e in exs)
    steps = int(np.prod(grid))
    mid_step = min(max(int(steps * middle_at), 1), steps - 1)

    def wrapped(*refs):
        a, xi = refs[:ni], refs[ni:ni + nx]
        o, xo = refs[ni + nx:ni + nx + no], refs[ni + nx + no:ni + 2 * nx + no]
        s, xs = refs[ni + 2 * nx + no:ni + 2 * nx + no + ns], refs[ni + 2 * nx + no + ns:]
        parts, at, sat = [], 0, 0
        for e in exs:
            parts.append((e, xi[at:at + e.n], xo[at:at + e.n], xs[sat:sat + len(e.scratch)]))
            at, sat = at + e.n, sat + len(e.scratch)
        step = 0
        for d, g in enumerate(grid):
            step = step * g + pl.program_id(d)

        @pl.when(step == 0)
        def _():
            for e, ins, outs, sems in parts:
                e.start(ins, outs, sems)

        @pl.when(step == mid_step)
        def _():
            for e, ins, outs, sems in parts:
                e.middle(ins, outs, sems)

        body(*a, *o, *s)

        @pl.when(step == steps - 1)
        def _():
            for e, ins, outs, sems in parts:
                e.wait(ins, outs, sems)

    hbm = pl.BlockSpec(memory_space=pltpu.HBM)
    res = pl.pallas_call(
        wrapped, name=name, grid=grid, in_specs=list(in_specs) + [hbm] * nx, out_specs=list(out_specs) + [hbm] * nx,
        out_shape=list(out_shape) + [o_ for e in exs for o_ in e.out_shape],
        scratch_shapes=list(scratch_shapes) + [s_ for e in exs for s_ in e.scratch],
        compiler_params=_params())(*args, *[a_ for e in exs for a_ in e.arrs])
    return res[:no], res[no:]


def _gather_two_level(arrs, name):
    n = len(arrs)
    out_shape = [jax.ShapeDtypeStruct((N_DEV,) + a.shape, a.dtype) for a in arrs]

    def body(*refs):
        ins, outs = refs[:n], refs[n:2 * n]
        send_sems, recv_sems, loc_sems = refs[2 * n:]
        x, y, c = lax.axis_index("x"), lax.axis_index("y"), lax.axis_index("c")
        me, sibling = (x, y, c), (x, y, 1 - c)
        chips = [(1 - x, y), (x, 1 - y), (1 - x, 1 - y)]

        def copy(k, j, block, to, src=None):
            px, py, pc = block
            dst = outs[k].at[4 * px + 2 * py + pc]
            return pltpu.make_async_remote_copy(src_ref=dst if src is None else src, dst_ref=dst,
                                                send_sem=send_sems.at[k, j], recv_sem=recv_sems.at[k, j],
                                                device_id=to, device_id_type=pl.DeviceIdType.MESH)

        mine = [pltpu.make_async_copy(ins[k], outs[k].at[4 * x + 2 * y + c], loc_sems.at[k]) for k in range(n)]
        first = []
        for k in range(n):
            mine[k].start()
            first.append(copy(k, 0, me, sibling, src=ins[k]))
            first += [copy(k, 1 + j, me, (*chip, c), src=ins[k]) for j, chip in enumerate(chips)]
        for cp in first:
            cp.start()
        passed = []
        for j, chip in enumerate(chips):
            for k in range(n):
                copy(k, 1 + j, (*chip, c), me).wait_recv()
                passed.append(copy(k, 4 + j, (*chip, c), sibling))
                passed[-1].start()
        for k in range(n):
            copy(k, 0, sibling, me).wait_recv()
            for j, chip in enumerate(chips):
                copy(k, 4 + j, (*chip, 1 - c), me).wait_recv()
        for cp in first + passed:
            cp.wait_send()
        for cp in mine:
            cp.wait()

    vmem = pl.BlockSpec(memory_space=pltpu.VMEM)
    return pl.pallas_call(body, name=name, out_shape=out_shape, in_specs=[vmem] * n, out_specs=[vmem] * n,
                          scratch_shapes=[pltpu.SemaphoreType.DMA((n, 7)), pltpu.SemaphoreType.DMA((n, 7)),
                                          pltpu.SemaphoreType.DMA((n,))], compiler_params=_params())(*arrs)


def _exchange(arrs, scatter, name):
    ex = _Exchange(arrs, scatter)

    def body(*refs):
        ins, outs, sems = refs[:ex.n], refs[ex.n:2 * ex.n], refs[2 * ex.n:]
        ex.start(ins, outs, sems)
        ex.wait(ins, outs, sems)

    hbm = pl.BlockSpec(memory_space=pltpu.HBM)
    return pl.pallas_call(body, name=name, out_shape=ex.out_shape, in_specs=[hbm] * ex.n, out_specs=[hbm] * ex.n,
                          scratch_shapes=ex.scratch)(*ex.arrs)


def _matmul(a, b, mode, name, out_dtype=F32, tm=512, tn=1024, tk=1024, a_fn=None, b_fn=None, extras=(),
            out_slabs=None, exchange=None):
    assert not (a_fn and b_fn) and not (b_fn and mode == "NT")
    if mode == "NN":
        (M, K), N = a.shape, b.shape[1]
    elif mode == "NT":
        (M, K), N = a.shape, b.shape[0]
    else:
        (K, M), N = a.shape, b.shape[1]
    if out_slabs:
        tn = N // out_slabs
    tm, tn, tk = min(tm, M), min(tn, N), min(tk, K)
    assert M % tm == 0 and N % tn == 0 and K % tk == 0, (name, M, N, K)
    nk = K // tk
    dims = {"NN": NN, "NT": NT, "TN": TN}[mode]
    ne = len(extras)

    def body(a_ref, b_ref, *rest):
        e_refs, o_ref, acc_ref = rest[:ne], rest[ne], rest[ne + 1]
        k = pl.program_id(2)

        @pl.when(k == 0)
        def _():
            acc_ref[...] = jnp.zeros_like(acc_ref)

        at, bt = a_ref[...], b_ref[...]
        if a_fn is not None:
            at = a_fn(at.astype(F32), *[e[...] for e in e_refs])
        if b_fn is not None:
            bt = b_fn(bt.astype(F32), *[e[...] for e in e_refs])
        acc_ref[...] += _bdot(at, bt, dims)

        @pl.when(k == nk - 1)
        def _():
            o_ref[...] = acc_ref[...].astype(out_dtype)

    if mode == "TN":
        a_spec = pl.BlockSpec((tk, tm), lambda i, j, k: (k, i))
        e_spec = pl.BlockSpec((1, tm), lambda i, j, k: (0, i))
    else:
        a_spec = pl.BlockSpec((tm, tk), lambda i, j, k: (i, k))
        e_spec = pl.BlockSpec((1, tk), lambda i, j, k: (0, k))
    if mode == "NT":
        b_spec = pl.BlockSpec((tn, tk), lambda i, j, k: (j, k))
    else:
        b_spec = pl.BlockSpec((tk, tn), lambda i, j, k: (k, j))
    if b_fn is not None:
        e_spec = pl.BlockSpec((1, tn), lambda i, j, k: (0, j))
    if out_slabs:
        o_shape = jax.ShapeDtypeStruct((out_slabs, M, tn), out_dtype)
        o_spec = pl.BlockSpec((None, tm, tn), lambda i, j, k: (j, i, 0))
    else:
        o_shape = jax.ShapeDtypeStruct((M, N), out_dtype)
        o_spec = pl.BlockSpec((tm, tn), lambda i, j, k: (i, j))
    (out,), got = _call(body, name, (a, b, *extras), [o_shape], grid=(M // tm, N // tn, nk),
                        in_specs=[a_spec, b_spec] + [e_spec] * ne, out_specs=[o_spec],
                        scratch_shapes=[pltpu.VMEM((tm, tn), F32)], exchange=exchange)
    return out if exchange is None else (out, got)


def _modulate(x, sc, sh):
    return x * (1.0 + sc) + sh


def _square(x):
    return x * x


def _mod_part(c_all, w_ada_s, b_s):
    def body(c_ref, w_ref, b_ref, mod_ref, cond_ref):
        cv = c_ref[...]
        cond = cv * _sigmoid(cv)
        cond_ref[...] = cond
        mod_ref[...] = _bdot(cond, w_ref[...]) + b_ref[...]

    return pl.pallas_call(
        body, name="mod_part",
        out_shape=[jax.ShapeDtypeStruct((N_DEV, w_ada_s.shape[1]), F32), jax.ShapeDtypeStruct(c_all.shape, F32)],
        compiler_params=_params(),
    )(c_all, w_ada_s, b_s)


def _rms_fwd(x, w):
    rs = lax.rsqrt(_lanemean(x * x) + RMS_EPS)
    return x * rs * w, rs


def _rms_bwd(x, rs, w, dy):
    xhat = x * rs
    dxh = dy * w
    return rs * (dxh - xhat * _lanemean(dxh * xhat)), dy * xhat


def _mla_pre(z, pos_col, invf, m_one, m_rot, wq_ext, wkv_ext, qnw, kvnw):
    T = z.shape[0]
    tm = min(ROW_TILE, T)

    def body(z_ref, pos_ref, invf_ref, mone_ref, mrot_ref, wq_ref, wkv_ref, qnw_ref, kvnw_ref,
             q_ref, k_ref, v_ref, c1_ref, s1_ref, cqn_ref, ckvn_ref):
        ang = pos_ref[...].astype(F32) * invf_ref[...]
        c1 = mone_ref[...] + mrot_ref[...] * jnp.cos(ang)
        s1 = mrot_ref[...] * jnp.sin(ang)
        c1_ref[...] = c1
        s1_ref[...] = s1
        cqn, _ = _rms_fwd(z_ref[:, 0:256], qnw_ref[...])
        ckvn, _ = _rms_fwd(z_ref[:, 256:512], kvnw_ref[...])
        cqn_ref[...] = cqn.astype(BF16)
        ckvn_ref[...] = ckvn.astype(BF16)
        qe = _bdot(cqn, wq_ref[...], NT)
        kve = _bdot(ckvn, wkv_ref[...])
        k_rope = z_ref[:, 512:768] * c1 + z_ref[:, 768:1024] * s1
        for h in range(HEADS):
            q_ref[h] = ((qe[:, 256 * h:256 * h + 256] * c1 + qe[:, 1024 + 256 * h:1280 + 256 * h] * s1)
                        * Q_PRESCALE).astype(BF16)
            k_ref[h] = (kve[:, 256 * h:256 * h + 256] + k_rope).astype(BF16)
            v_ref[h] = kve[:, 1024 + 128 * h:1152 + 128 * h].astype(BF16)

    row = lambda i: (i, 0)
    head = lambda i: (0, i, 0)
    return pl.pallas_call(
        body, name="mla_pre", grid=(T // tm,),
        in_specs=[pl.BlockSpec((tm, 1024), lambda i: (i, 2)), pl.BlockSpec((tm, 1), row),
                  _full((1, 256)), _full((1, 256)), _full((1, 256)), _full(wq_ext.shape), _full(wkv_ext.shape),
                  _full((1, 256)), _full((1, 256))],
        out_specs=[pl.BlockSpec((HEADS, tm, QK_PAD), head), pl.BlockSpec((HEADS, tm, QK_PAD), head),
                   pl.BlockSpec((HEADS, tm, HEAD_DIM), head), pl.BlockSpec((tm, 256), row), pl.BlockSpec((tm, 256), row),
                   pl.BlockSpec((tm, 256), row), pl.BlockSpec((tm, 256), row)],
        out_shape=[jax.ShapeDtypeStruct((HEADS, T, QK_PAD), BF16), jax.ShapeDtypeStruct((HEADS, T, QK_PAD), BF16),
                   jax.ShapeDtypeStruct((HEADS, T, HEAD_DIM), BF16), jax.ShapeDtypeStruct((T, 256), F32),
                   jax.ShapeDtypeStruct((T, 256), F32), jax.ShapeDtypeStruct((T, 256), BF16),
                   jax.ShapeDtypeStruct((T, 256), BF16)],
        compiler_params=_params(),
    )(z, pos_col, invf, m_one, m_rot, wq_ext, wkv_ext, qnw, kvnw)


def _mla_bwd(dq, dk, dv, z, c1, s1, wq_ext, wkv_ext, qnw, kvnw):
    T = z.shape[0]
    tm = min(ROW_TILE_SMALL, T)

    def body(dq_ref, dk_ref, dv_ref, z_ref, c1_ref, s1_ref, wq_ref, wkv_ref, qnw_ref, kvnw_ref,
             dz_ref, dqe_ref, dkve_ref, dqnw_ref, dkvnw_ref):
        @pl.when(pl.program_id(0) == 0)
        def _():
            dqnw_ref[...] = jnp.zeros_like(dqnw_ref)
            dkvnw_ref[...] = jnp.zeros_like(dkvnw_ref)

        c1, s1 = c1_ref[...], s1_ref[...]
        dkpe = jnp.zeros((tm, QK_PAD), F32)
        for h in range(HEADS):
            dqh, dkh = dq_ref[h] * Q_PRESCALE, dk_ref[h]
            dqe_ref[:, 256 * h:256 * h + 256] = (dqh * c1).astype(BF16)
            dqe_ref[:, 1024 + 256 * h:1280 + 256 * h] = (dqh * s1).astype(BF16)
            dkve_ref[:, 256 * h:256 * h + 256] = dkh.astype(BF16)
            dkve_ref[:, 1024 + 128 * h:1152 + 128 * h] = dv_ref[h].astype(BF16)
            dkpe = dkpe + dkh
        dcqn = _dot(dqe_ref[...], wq_ref[...])
        dckvn = _dot(dkve_ref[...], wkv_ref[...], NT)
        cq, ckv = z_ref[:, 0:256], z_ref[:, 256:512]
        _, rsq = _rms_fwd(cq, qnw_ref[...])
        _, rskv = _rms_fwd(ckv, kvnw_ref[...])
        dcq, wq_rows = _rms_bwd(cq, rsq, qnw_ref[...], dcqn)
        dckv, wkv_rows = _rms_bwd(ckv, rskv, kvnw_ref[...], dckvn)
        dqnw_ref[...] += _rowsum(wq_rows)
        dkvnw_ref[...] += _rowsum(wkv_rows)
        dz_ref[:, 0:256] = dcq
        dz_ref[:, 256:512] = dckv
        dz_ref[:, 512:768] = dkpe * c1
        dz_ref[:, 768:1024] = dkpe * s1

    row = lambda i: (i, 0)
    head = lambda i: (0, i, 0)
    return pl.pallas_call(
        body, name="mla_bwd", grid=(T // tm,),
        in_specs=[pl.BlockSpec((HEADS, tm, QK_PAD), head), pl.BlockSpec((HEADS, tm, QK_PAD), head),
                  pl.BlockSpec((HEADS, tm, HEAD_DIM), head), pl.BlockSpec((tm, 1024), lambda i: (i, 2)),
                  pl.BlockSpec((tm, 256), row), pl.BlockSpec((tm, 256), row), _full(wq_ext.shape), _full(wkv_ext.shape),
                  _full((1, 256)), _full((1, 256))],
        out_specs=[pl.BlockSpec((tm, 1024), row), pl.BlockSpec((tm, 2048), row), pl.BlockSpec((tm, 1536), row),
                   _full((1, 256)), _full((1, 256))],
        out_shape=[jax.ShapeDtypeStruct((T, 1024), F32), jax.ShapeDtypeStruct((T, 2048), BF16),
                   jax.ShapeDtypeStruct((T, 1536), BF16), jax.ShapeDtypeStruct((1, 256), F32),
                   jax.ShapeDtypeStruct((1, 256), F32)],
        compiler_params=_params(),
    )(dq, dk, dv, z, c1, s1, wq_ext, wkv_ext, qnw, kvnw)


_HEAD_LANES = [slice(HEAD_DIM * h, HEAD_DIM * (h + 1)) for h in range(HEADS)]


def _lower_bound(lbraw_ref):
    a0, a1 = lbraw_ref[0:1, :], lbraw_ref[1:2, :]
    mx = jnp.maximum(a0, a1)
    e0, e1 = jnp.exp(a0 - mx), jnp.exp(a1 - mx)
    return e0 / (e0 + e1)


def _tri(lower):
    r = lax.broadcasted_iota(jnp.int32, (CHUNK, CHUNK), 0)
    c = lax.broadcasted_iota(jnp.int32, (CHUNK, CHUNK), 1)
    return (r >= c) if lower else (r <= c)


def _hgrn_gates(q, f, lb, tri_lo):
    sg = _sigmoid(f)
    forget = lb + (1.0 - lb) * sg
    k = 1.0 - forget
    b = _hdot(tri_lo.astype(F32), jnp.log(forget))
    b_ref, b_last = b[CHUNK // 2 - 1:CHUNK // 2, :], b[CHUNK - 1:CHUNK, :]
    e1, e2, e3, e4 = jnp.exp(b - b_ref), jnp.exp(b_ref - b), jnp.exp(b_last - b), jnp.exp(b)
    return dict(sg=sg, forget=forget, k=k, e1=e1, e2=e2, e3=e3, e4=e4, qa=q * e1, ka=k * e2, kl=k * e3, qb=q * e4,
                decay=jnp.exp(b_last))


def _hgrn_fwd(z, lbraw, nw, exchange=None):
    T = z.shape[0]
    G = min(HGRN_GROUP, T // CHUNK)
    rows = G * CHUNK
    n_chunks = T // CHUNK

    def body(q_ref, f_ref, i_ref, g_ref, lbraw_ref, nw_ref, oraw_ref, og_ref, sp_ref, st_ref):
        @pl.when(pl.program_id(0) == 0)
        def _():
            st_ref[...] = jnp.zeros_like(st_ref)

        lb_all = _lower_bound(lbraw_ref)
        tri_lo = _tri(True)

        def chunk(cc, carry):
            rs = pl.ds(pl.multiple_of(cc * CHUNK, CHUNK), CHUNK)
            t = _hgrn_gates(q_ref[rs, :], f_ref[rs, :], lb_all, tri_lo)
            v, gate = i_ref[rs, :], g_ref[rs, :]
            st = [st_ref[h] for h in range(HEADS)]
            a = [jnp.where(tri_lo, _bdot(t["qa"][:, s], t["ka"][:, s], NT), 0.0) for s in _HEAD_LANES]
            kv = [_bdot(v[:, s], t["kl"][:, s], TN) for s in _HEAD_LANES]
            o = [_bdot(a[h], v[:, s]) + _bdot(t["qb"][:, s], st[h], NT) for h, s in enumerate(_HEAD_LANES)]
            for h, s in enumerate(_HEAD_LANES):
                sp_ref[cc, h] = st[h]
                st_ref[h] = st[h] * t["decay"][:, s] + kv[h]
            oraw_ref[rs, :] = jnp.concatenate(o, axis=1)
            on = jnp.concatenate([_rms_fwd(o[h], nw_ref[:, s])[0] for h, s in enumerate(_HEAD_LANES)], axis=1)
            og_ref[rs, :] = (on * (gate * _sigmoid(gate))).astype(BF16)
            return carry

        lax.fori_loop(0, G, chunk, 0, unroll=2)

    col = lambda j: pl.BlockSpec((rows, 512), lambda r, j=j: (r, j))
    return _call(
        body, "hgrn_fwd", (z, z, z, z, lbraw, nw), grid=(T // rows,),
        in_specs=[col(0), col(1), col(2), col(3), _full((2, 512)), _full((1, 512))],
        out_specs=[col(0), col(0), pl.BlockSpec((G, HEADS, HEAD_DIM, HEAD_DIM), lambda r: (r, 0, 0, 0))],
        out_shape=[jax.ShapeDtypeStruct((T, 512), F32), jax.ShapeDtypeStruct((T, 512), BF16),
                   jax.ShapeDtypeStruct((n_chunks, HEADS, HEAD_DIM, HEAD_DIM), F32)],
        scratch_shapes=[pltpu.VMEM((HEADS, HEAD_DIM, HEAD_DIM), F32)], exchange=exchange)


def _hgrn_bwd(dmixcat, z, oraw, sprev, lbraw, nw, exchange=None):
    T = z.shape[0]
    G = min(HGRN_GROUP, T // CHUNK)
    rows = G * CHUNK
    ng = T // rows

    def body(dog_ref, q_ref, f_ref, i_ref, g_ref, oraw_ref, sp_ref, lbraw_ref, nw_ref,
             dz_ref, dlb_ref, dnw_ref, dst_ref):
        @pl.when(pl.program_id(0) == 0)
        def _():
            dst_ref[...] = jnp.zeros_like(dst_ref)
            dlb_ref[...] = jnp.zeros_like(dlb_ref)
            dnw_ref[...] = jnp.zeros_like(dnw_ref)

        lb_all = _lower_bound(lbraw_ref)
        tri_lo, tri_up = _tri(True), _tri(False)
        rowid = lax.broadcasted_iota(jnp.int32, (CHUNK, HEADS * HEAD_DIM), 0)

        def chunk(it, carry):
            cc = G - 1 - it
            rs = pl.ds(pl.multiple_of(cc * CHUNK, CHUNK), CHUNK)
            heads = list(enumerate(_HEAD_LANES))
            cat = lambda parts: jnp.concatenate(parts, axis=1)
            per_head_mean = lambda x: cat([jnp.broadcast_to(_lanemean(x[:, s]), (CHUNK, HEAD_DIM)) for s in _HEAD_LANES])
            t = _hgrn_gates(q_ref[rs, :], f_ref[rs, :], lb_all, tri_lo)
            v, gate, o, dog, nw_all = i_ref[rs, :], g_ref[rs, :], oraw_ref[rs, :], dog_ref[rs, :], nw_ref[...]
            rs_o = lax.rsqrt(per_head_mean(o * o) + RMS_EPS)
            xhat = o * rs_o
            sgg = _sigmoid(gate)
            d_on = dog * (gate * sgg)
            dz_ref[rs, 1536:2048] = dog * (xhat * nw_all) * (sgg * (1.0 + gate * (1.0 - sgg)))
            dxh = d_on * nw_all
            do = rs_o * (dxh - xhat * per_head_mean(dxh * xhat))
            dnw_ref[...] += _rowsum(d_on * xhat)
            st = [sp_ref[cc, h] for h in range(HEADS)]
            dst = [dst_ref[h] for h in range(HEADS)]
            a = [jnp.where(tri_lo, _bdot(t["qa"][:, s], t["ka"][:, s], NT), 0.0) for s in _HEAD_LANES]
            da = [jnp.where(tri_lo, _bdot(do[:, s], v[:, s], NT), 0.0) for s in _HEAD_LANES]
            dqb = cat([_bdot(do[:, s], st[h]) for h, s in heads])
            dkl = cat([_bdot(v[:, s], dst[h]) for h, s in heads])
            dv_ = cat([_bdot(t["kl"][:, s], dst[h], NT) + _bdot(a[h], do[:, s], TN) for h, s in heads])
            dqa = cat([_bdot(da[h], t["ka"][:, s]) for h, s in heads])
            dka = cat([_bdot(da[h], t["qa"][:, s], TN) for h, s in heads])
            ddecay = cat([_rowsum(dst[h] * st[h]) for h in range(HEADS)])
            for h, s in heads:
                dst_ref[h] = dst[h] * t["decay"][:, s] + _bdot(do[:, s], t["qb"][:, s], TN)
            pa, pk, pb, pl_ = dqa * t["qa"], dka * t["ka"], dqb * t["qb"], dkl * t["kl"]
            db = pa - pk + pb - pl_
            db = db + jnp.where(rowid == CHUNK // 2 - 1, _rowsum(pk - pa), 0.0)
            db = db + jnp.where(rowid == CHUNK - 1, _rowsum(pl_) + ddecay * t["decay"], 0.0)
            dlogf = _hdot(tri_up.astype(F32), db)
            dforget = dlogf / t["forget"] - (dka * t["e2"] + dkl * t["e3"])
            sg = t["sg"]
            dz_ref[rs, 0:512] = dqa * t["e1"] + dqb * t["e4"]
            dz_ref[rs, 512:1024] = dforget * (1.0 - lb_all) * sg * (1.0 - sg)
            dz_ref[rs, 1024:1536] = dv_
            dlb_ref[...] += _rowsum(dforget * (1.0 - sg))
            return carry

        lax.fori_loop(0, G, chunk, 0, unroll=2)

    col = lambda j: pl.BlockSpec((rows, 512), lambda r, j=j: (ng - 1 - r, j))
    return _call(
        body, "hgrn_bwd", (dmixcat, z, z, z, z, oraw, sprev, lbraw, nw), grid=(ng,),
        in_specs=[col(0), col(0), col(1), col(2), col(3), col(0),
                  pl.BlockSpec((G, HEADS, HEAD_DIM, HEAD_DIM), lambda r: (ng - 1 - r, 0, 0, 0)),
                  _full((2, 512)), _full((1, 512))],
        out_specs=[pl.BlockSpec((rows, 2048), lambda r: (ng - 1 - r, 0)), _full((1, 512)), _full((1, 512))],
        out_shape=[jax.ShapeDtypeStruct((T, 2048), F32), jax.ShapeDtypeStruct((1, 512), F32),
                   jax.ShapeDtypeStruct((1, 512), F32)],
        scratch_shapes=[pltpu.VMEM((HEADS, HEAD_DIM, HEAD_DIM), F32)], exchange=exchange)


def _diag_mask(t):
    r = lax.broadcasted_iota(jnp.int32, (t, t), 0)
    c = lax.broadcasted_iota(jnp.int32, (t, t), 1)
    return r >= c


def _attn_fwd(q, k, v, exchange=None):
    _, T, _ = q.shape
    t = min(ATT_TILE, T)

    def body(q_ref, k_ref, v_ref, o_ref, lse_ref):
        i = pl.program_id(1)
        qb = q_ref[...]

        rows = lambda j: pl.ds(pl.multiple_of(j * t, t), t)

        def logits(j, masked):
            s = _dot(qb, k_ref[rows(j), :], NT)
            return jnp.where(_diag_mask(t), s, NEG_BIG) if masked else s

        def absorb(s, j, carry):
            m, l, acc = carry
            mn = jnp.maximum(m, jnp.max(s, axis=-1, keepdims=True))
            p = jnp.exp2(s - mn)
            al = jnp.exp2(m - mn)
            return mn, al * l + jnp.sum(p, axis=-1, keepdims=True), al * acc + _dot(p.astype(BF16), v_ref[rows(j), :])

        def pair(j0, carry, last_masked):
            s0, s1 = logits(j0, False), logits(j0 + 1, last_masked)
            return absorb(s1, j0 + 1, absorb(s0, j0, carry))

        init = (jnp.full((t, 1), NEG_BIG, F32), jnp.zeros((t, 1), F32), jnp.zeros((t, HEAD_DIM), F32))
        carry = lax.fori_loop(0, i // 2, lambda jj, c: pair(2 * jj, c, False), init)
        m, l, acc = lax.cond(i % 2 == 1, lambda c: pair(i - 1, c, True),
                             lambda c: absorb(logits(i, True), i, c), carry)
        o_ref[...] = acc / l
        lse_ref[...] = jnp.broadcast_to(m + jnp.log2(l), (t, HEAD_DIM))

    return _call(
        body, "attn_fwd", (q, k, v), grid=(HEADS, T // t),
        in_specs=[pl.BlockSpec((None, t, QK_PAD), lambda h, i: (h, i, 0)),
                  pl.BlockSpec((None, T, QK_PAD), lambda h, i: (h, 0, 0)),
                  pl.BlockSpec((None, T, HEAD_DIM), lambda h, i: (h, 0, 0))],
        out_specs=[pl.BlockSpec((t, HEAD_DIM), lambda h, i: (i, h)),
                   pl.BlockSpec((None, t, HEAD_DIM), lambda h, i: (h, i, 0))],
        out_shape=[jax.ShapeDtypeStruct((T, HEADS * HEAD_DIM), F32), jax.ShapeDtypeStruct((HEADS, T, HEAD_DIM), F32)],
        exchange=exchange)


def _attn_bwd(q, k, v, dmixcat, o, lse, exchange=None):
    _, T, _ = q.shape
    t = min(ATT_TILE, T)
    nq = T // t

    def body(q_ref, k_ref, v_ref, do_ref, o_ref, lse_ref, dq_ref, dk_ref, dv_ref, delta_ref):
        j = pl.program_id(1)

        @pl.when(j == 0)
        def _():
            dq_ref[...] = jnp.zeros_like(dq_ref)

            def fill(i, carry):
                rs = pl.ds(pl.multiple_of(i * t, t), t)
                delta_ref[rs, :] = jnp.broadcast_to(
                    jnp.sum(do_ref[rs, :] * o_ref[rs, :], axis=-1, keepdims=True), (t, HEAD_DIM))
                return carry

            lax.fori_loop(0, nq, fill, 0)

        kb, vb = k_ref[...], v_ref[...]

        def steps(blocks, carry):
            dk, dv = carry
            rs = [pl.ds(pl.multiple_of(i * t, t), t) for i, _ in blocks]
            qb = [q_ref[r, :] for r in rs]
            dob = [do_ref[r, :].astype(BF16) for r in rs]
            s = [_dot(b, kb, NT) for b in qb]
            dp = [_dot(b, vb, NT) for b in dob]
            for n, (_, masked) in enumerate(blocks):
                p = jnp.exp2(s[n] - lse_ref[rs[n], 0:1])
                if masked:
                    p = jnp.where(_diag_mask(t), p, 0.0)
                ds = (p * (dp[n] - delta_ref[rs[n], 0:1]) * LN2).astype(BF16)
                dq_ref[rs[n], :] += _dot(ds, kb)
                dk = dk + _dot(ds, qb[n], TN)
                dv = dv + _dot(p.astype(BF16), dob[n], TN)
            return dk, dv

        zero = (jnp.zeros((t, QK_PAD), F32), jnp.zeros((t, HEAD_DIM), F32))
        rest = nq - 1 - j
        carry = lax.cond(rest % 2 == 1, lambda c: steps([(j, True), (j + 1, False)], c),
                         lambda c: steps([(j, True)], c), zero)
        first = j + 1 + rest % 2
        dk, dv = lax.fori_loop(0, rest // 2, lambda n, c: steps([(first + 2 * n, False), (first + 2 * n + 1, False)], c),
                               carry)
        dk_ref[...] = dk
        dv_ref[...] = dv

    return _call(
        body, "attn_bwd", (q, k, v, dmixcat, o, lse), grid=(HEADS, nq),
        in_specs=[pl.BlockSpec((None, T, QK_PAD), lambda h, j: (h, 0, 0)),
                  pl.BlockSpec((None, t, QK_PAD), lambda h, j: (h, j, 0)),
                  pl.BlockSpec((None, t, HEAD_DIM), lambda h, j: (h, j, 0)),
                  pl.BlockSpec((T, HEAD_DIM), lambda h, j: (0, HEADS + h)),
                  pl.BlockSpec((T, HEAD_DIM), lambda h, j: (0, h)),
                  pl.BlockSpec((None, T, HEAD_DIM), lambda h, j: (h, 0, 0))],
        out_specs=[pl.BlockSpec((None, T, QK_PAD), lambda h, j: (h, 0, 0)),
                   pl.BlockSpec((None, t, QK_PAD), lambda h, j: (h, j, 0)),
                   pl.BlockSpec((None, t, HEAD_DIM), lambda h, j: (h, j, 0))],
        out_shape=[jax.ShapeDtypeStruct((HEADS, T, QK_PAD), F32), jax.ShapeDtypeStruct((HEADS, T, QK_PAD), F32),
                   jax.ShapeDtypeStruct((HEADS, T, HEAD_DIM), F32)],
        scratch_shapes=[pltpu.VMEM((T, HEAD_DIM), F32)], exchange=exchange)


def _ln_fwd(r):
    mu = _lanemean(r)
    xc = r - mu
    rstd = lax.rsqrt(_lanemean(xc * xc) + LN_EPS)
    return xc * rstd, rstd


def _ln_bwd(dxh, xhat, rstd):
    return rstd * (dxh - _lanemean(dxh) - xhat * _lanemean(dxh * xhat))


def _mix_ln1(o_hg, o_mla, w_out, x, g_a, ln1_g, ln1_b, sc_m, sh_m, exchange=None):
    T = x.shape[0]
    tm = min(ROW_TILE_SMALL, T)
    half = o_hg.shape[1]

    def body(hg_ref, mla_ref, w_ref, x_ref, ga_ref, g_ref, b_ref, sc_ref, sh_ref, mix_ref, xhat_ref, rstd_ref, u2_ref):
        mix = _dot(hg_ref[...], w_ref[0:half, :]) + _bdot(mla_ref[...], w_ref[half:, :])
        mix_ref[...] = mix
        xhat, rstd = _ln_fwd(ALPHA * x_ref[...] + (1.0 + ga_ref[...]) * mix)
        xhat_ref[...] = xhat
        rstd_ref[...] = jnp.broadcast_to(rstd, (tm, 128))
        u2_ref[...] = _modulate(xhat * g_ref[...] + b_ref[...], sc_ref[...], sh_ref[...]).astype(BF16)

    row = pl.BlockSpec((tm, D_MODEL), lambda i: (i, 0))
    vec = _full((1, D_MODEL))
    halfrow = pl.BlockSpec((tm, half), lambda i: (i, 0))
    return _call(
        body, "mix_ln1", (o_hg, o_mla, w_out, x, g_a, ln1_g, ln1_b, sc_m, sh_m), grid=(T // tm,),
        in_specs=[halfrow, halfrow, _full(w_out.shape), row, vec, vec, vec, vec, vec],
        out_specs=[row, row, pl.BlockSpec((tm, 128), lambda i: (i, 0)), row],
        out_shape=[jax.ShapeDtypeStruct((T, D_MODEL), F32), jax.ShapeDtypeStruct((T, D_MODEL), F32),
                   jax.ShapeDtypeStruct((T, 128), F32), jax.ShapeDtypeStruct((T, D_MODEL), BF16)],
        exchange=exchange)


def _mlp_fwd(u2, w1, w2, xhat1, ln1_g, ln1_b, g_m, ln2_g, ln2_b, target):
    T = u2.shape[0]
    tf = w1.shape[-1]
    nf = N_DEV // MLP_SLABS
    tm = min(ROW_TILE, T)

    def body(u2_ref, w1_ref, w2_ref, xhat_ref, g1_ref, b1_ref, gm_ref, g2_ref, b2_ref, tgt_ref,
             r_ref, dr2_ref, dh_ref, dg2_ref, db2_ref, dgm_ref, loss_ref, acc_ref):
        i, f = pl.program_id(0), pl.program_id(1)

        @pl.when((i == 0) & (f == 0))
        def _():
            for ref in (dg2_ref, db2_ref, dgm_ref, loss_ref):
                ref[...] = jnp.zeros_like(ref)

        @pl.when(f == 0)
        def _():
            acc_ref[...] = jnp.zeros_like(acc_ref)

        u2t = u2_ref[...]
        part = None
        for s in range(MLP_SLABS):
            r = jnp.maximum(_dot(u2t, w1_ref[s]), 0.0)
            r_ref[:, s * tf:(s + 1) * tf] = r.astype(BF16)
            d = _bdot(r * r, w2_ref[s])
            part = d if part is None else part + d
        acc_ref[...] += part

        @pl.when(f == nf - 1)
        def _():
            h = acc_ref[...]
            x1 = xhat_ref[...] * g1_ref[...] + b1_ref[...]
            xhat2, rstd2 = _ln_fwd(ALPHA * x1 + (1.0 + gm_ref[...]) * h)
            err = xhat2 * g2_ref[...] + b2_ref[...] - tgt_ref[...]
            loss_ref[...] += jnp.sum(0.5 * _lanemean(err * err), axis=0, keepdims=True)
            dy = err * (1.0 / D_MODEL)
            dg2_ref[...] += _rowsum(dy * xhat2)
            db2_ref[...] += _rowsum(dy)
            dr2 = _ln_bwd(dy * g2_ref[...], xhat2, rstd2)
            dr2_ref[...] = dr2
            dgm_ref[...] += _rowsum(dr2 * h)
            dh_ref[...] = ((1.0 + gm_ref[...]) * dr2).astype(BF16)

    row = pl.BlockSpec((tm, D_MODEL), lambda i, f: (i, 0))
    vec = _full((1, D_MODEL))
    return pl.pallas_call(
        body, name="mlp_fwd", grid=(T // tm, nf),
        in_specs=[row, pl.BlockSpec((MLP_SLABS, D_MODEL, tf), lambda i, f: (f, 0, 0)),
                  pl.BlockSpec((MLP_SLABS, tf, D_MODEL), lambda i, f: (f, 0, 0)),
                  row, vec, vec, vec, vec, vec, row],
        out_specs=[pl.BlockSpec((tm, MLP_SLABS * tf), lambda i, f: (i, f)), row, row, vec, vec, vec, _full((1, 128))],
        out_shape=[jax.ShapeDtypeStruct((T, N_DEV * tf), BF16), jax.ShapeDtypeStruct((T, D_MODEL), F32),
                   jax.ShapeDtypeStruct((T, D_MODEL), BF16), jax.ShapeDtypeStruct((1, D_MODEL), F32),
                   jax.ShapeDtypeStruct((1, D_MODEL), F32), jax.ShapeDtypeStruct((1, D_MODEL), F32),
                   jax.ShapeDtypeStruct((1, 128), F32)],
        scratch_shapes=[pltpu.VMEM((tm, D_MODEL), F32)],
        compiler_params=_params(),
    )(u2, w1, w2, xhat1, ln1_g, ln1_b, g_m, ln2_g, ln2_b, target)


def _mlp_bwd(dh, w1, w2, r, dr2, xhat1, rstd1, mix, ln1_g, ln1_b, sc_m, g_a):
    T = dh.shape[0]
    tf = w1.shape[-1]
    nf = N_DEV // MLP_SLABS
    tm = min(ROW_TILE, T)

    def body(dh_ref, w1_ref, w2_ref, r_ref, dr2_ref, xhat_ref, rstd_ref, mix_ref, g1_ref, b1_ref, sc_ref, ga_ref,
             dhpre_ref, dr1_ref, dmix_ref, dsc_ref, dsh_ref, dg1_ref, db1_ref, dga_ref, acc_ref):
        i, f = pl.program_id(0), pl.program_id(1)

        @pl.when((i == 0) & (f == 0))
        def _():
            for ref in (dsc_ref, dsh_ref, dg1_ref, db1_ref, dga_ref):
                ref[...] = jnp.zeros_like(ref)

        @pl.when(f == 0)
        def _():
            acc_ref[...] = jnp.zeros_like(acc_ref)

        dht = dh_ref[...]
        part = None
        for s in range(MLP_SLABS):
            cols = slice(s * tf, (s + 1) * tf)
            dhpre = (_dot(dht, w2_ref[s], NT) * (2.0 * r_ref[:, cols].astype(F32))).astype(BF16)
            dhpre_ref[:, cols] = dhpre
            d = _dot(dhpre, w1_ref[s], NT)
            part = d if part is None else part + d
        acc_ref[...] += part

        @pl.when(f == nf - 1)
        def _():
            du2 = acc_ref[...]
            xhat = xhat_ref[...]
            x1 = xhat * g1_ref[...] + b1_ref[...]
            dx1 = ALPHA * dr2_ref[...] + du2 * (1.0 + sc_ref[...])
            dsc_ref[...] += _rowsum(du2 * x1)
            dsh_ref[...] += _rowsum(du2)
            dg1_ref[...] += _rowsum(dx1 * xhat)
            db1_ref[...] += _rowsum(dx1)
            dr1 = _ln_bwd(dx1 * g1_ref[...], xhat, rstd_ref[:, 0:1])
            dr1_ref[...] = dr1
            dga_ref[...] += _rowsum(dr1 * mix_ref[...])
            dmix_ref[...] = ((1.0 + ga_ref[...]) * dr1).astype(BF16)

    row = pl.BlockSpec((tm, D_MODEL), lambda i, f: (i, 0))
    vec = _full((1, D_MODEL))
    return pl.pallas_call(
        body, name="mlp_bwd", grid=(T // tm, nf),
        in_specs=[row, pl.BlockSpec((MLP_SLABS, D_MODEL, tf), lambda i, f: (f, 0, 0)),
                  pl.BlockSpec((MLP_SLABS, tf, D_MODEL), lambda i, f: (f, 0, 0)),
                  pl.BlockSpec((tm, MLP_SLABS * tf), lambda i, f: (i, f)), row, row,
                  pl.BlockSpec((tm, 128), lambda i, f: (i, 0)), row, vec, vec, vec, vec],
        out_specs=[pl.BlockSpec((tm, MLP_SLABS * tf), lambda i, f: (i, f)), row, row, vec, vec, vec, vec, vec],
        out_shape=[jax.ShapeDtypeStruct((T, N_DEV * tf), BF16), jax.ShapeDtypeStruct((T, D_MODEL), F32),
                   jax.ShapeDtypeStruct((T, D_MODEL), BF16)] + [jax.ShapeDtypeStruct((1, D_MODEL), F32)] * 5,
        scratch_shapes=[pltpu.VMEM((tm, D_MODEL), F32)],
        compiler_params=_params(),
    )(dh, w1, w2, r, dr2, xhat1, rstd1, mix, ln1_g, ln1_b, sc_m, g_a)


def _input_bwd(dz_h, dz_m, w_in_ext, x, dr1, sc_a, exchange=None):
    T = x.shape[0]
    tm = min(ROW_TILE_SMALL, T)

    def body(dzh_ref, dzm_ref, w_ref, x_ref, dr1_ref, sc_ref, gx_ref, dsc_ref, dsh_ref):
        @pl.when(pl.program_id(0) == 0)
        def _():
            dsc_ref[...] = jnp.zeros_like(dsc_ref)
            dsh_ref[...] = jnp.zeros_like(dsh_ref)

        du = _bdot(dzh_ref[...], w_ref[0:2048, :]) + _bdot(dzm_ref[...], w_ref[2048:3072, :])
        gx_ref[...] = ALPHA * dr1_ref[...] + du * (1.0 + sc_ref[...])
        dsc_ref[...] += _rowsum(du * x_ref[...])
        dsh_ref[...] += _rowsum(du)

    row = pl.BlockSpec((tm, D_MODEL), lambda i: (i, 0))
    vec = _full((1, D_MODEL))
    return _call(
        body, "input_bwd", (dz_h, dz_m, w_in_ext, x, dr1, sc_a), grid=(T // tm,),
        in_specs=[pl.BlockSpec((tm, 2048), lambda i: (i, 0)), row, _full(w_in_ext.shape), row, row, vec],
        out_specs=[row, vec, vec],
        out_shape=[jax.ShapeDtypeStruct((T, D_MODEL), F32), jax.ShapeDtypeStruct((1, D_MODEL), F32),
                   jax.ShapeDtypeStruct((1, D_MODEL), F32)], exchange=exchange)


def _adam_math(w, g, m, v):
    m = ADAM_B1 * m + (1.0 - ADAM_B1) * g
    v = ADAM_B2 * v + (1.0 - ADAM_B2) * (g * g)
    m_hat = m / (1.0 - ADAM_B1 ** ADAM_STEP)
    v_hat = v / (1.0 - ADAM_B2 ** ADAM_STEP)
    return -ADAM_LR * (m_hat / (jnp.sqrt(v_hat) + ADAM_EPS) + ADAM_WD * w), m, v


def _adam(g_slabs, w, m, v, name, g_fn=None, g_extra=()):
    R, C = w.shape
    tr = 256 if R % 256 == 0 else R
    ns = 0 if g_slabs is None else g_slabs.shape[0]
    ne = len(g_extra)

    def body(*refs):
        e_refs = refs[:ne]
        refs = refs[ne:]
        if ns:
            gs_ref, refs = refs[0], refs[1:]
        w_ref, m_ref, v_ref, g_ref, d_ref, nm_ref, nv_ref = refs
        if g_fn is not None:
            g = g_fn(*e_refs)
        else:
            g = gs_ref[0].astype(F32)
            for s in range(1, ns):
                g = g + gs_ref[s].astype(F32)
        d, nm, nv = _adam_math(w_ref[...], g, m_ref[...], v_ref[...])
        g_ref[...] = g
        d_ref[...] = d
        nm_ref[...] = nm
        nv_ref[...] = nv

    blk = pl.BlockSpec((tr, C), lambda i: (i, 0))
    in_specs = [pl.BlockSpec((tr, e.shape[1]), lambda i: (i, 0)) if e.shape[0] == R else _full(e.shape) for e in g_extra]
    args = list(g_extra)
    if ns:
        in_specs.append(pl.BlockSpec((ns, tr, C), lambda i: (0, i, 0)))
        args.append(g_slabs)
    return pl.pallas_call(
        body, name=name, grid=(R // tr,), in_specs=in_specs + [blk] * 3, out_specs=[blk] * 4,
        out_shape=[jax.ShapeDtypeStruct((R, C), F32)] * 4, compiler_params=_params(),
    )(*args, w, m, v)


def _adam_small(small_all, params):
    n = len(params)

    def body(*refs):
        s_ref, refs = refs[0], refs[1:]
        wmv, loss_ref, outs = refs[:3 * n], refs[3 * n], refs[3 * n + 1:]
        tot = s_ref[0]
        for i in range(1, N_DEV):
            tot = tot + s_ref[i]
        loss_ref[...] = tot[:, SMALL_W - 128:]
        for j, (w, _, _, off) in enumerate(params):
            w_ref, m_ref, v_ref = wmv[3 * j:3 * j + 3]
            g_ref, d_ref, nm_ref, nv_ref = outs[4 * j:4 * j + 4]
            if w.shape[0] == 2:
                lb = _lower_bound(w_ref)
                g0 = tot[:, off:off + w.shape[1]] * lb * (1.0 - lb)
                rows = [(slice(0, 1), g0), (slice(1, 2), -g0)]
            else:
                rows = [(slice(0, 1), tot[:, off:off + w.shape[1]])]
            for rs, g in rows:
                d, nm, nv = _adam_math(w_ref[rs, :], g, m_ref[rs, :], v_ref[rs, :])
                g_ref[rs, :], d_ref[rs, :], nm_ref[rs, :], nv_ref[rs, :] = g, d, nm, nv

    out_shape = [jax.ShapeDtypeStruct((1, 128), F32)]
    for w, _, _, _ in params:
        out_shape += [jax.ShapeDtypeStruct(w.shape, F32)] * 4
    res = pl.pallas_call(body, name="adam_small", out_shape=out_shape, compiler_params=_params())(
        small_all, *[a for w, m, v, _ in params for a in (w, m, v)])
    return res[0], [tuple(res[1 + 4 * j:5 + 4 * j]) for j in range(n)]


def _cols_from_slabs(g):
    s, r, c = g.shape
    return jnp.transpose(g, (1, 0, 2)).reshape(r, s * c)


def _slabs_from_cols(w):
    r, c = w.shape
    return jnp.transpose(w.reshape(r, N_DEV, c // N_DEV), (1, 0, 2))


def _rot_half_rows(wt):
    return jnp.concatenate([-wt[32:], wt[:32]], axis=0)


def _unrot_half_rows(dwt_rot):
    return jnp.concatenate([dwt_rot[32:], -dwt_rot[:32]], axis=0)


def _ext_in_t(g):
    k_in = g.shape[2]
    z64, z128 = jnp.zeros((64, k_in), BF16), jnp.zeros((128, k_in), BF16)
    last = g.shape[1] - ROPE_DIM
    wk = g[N_DEV - 1, last:]
    return jnp.concatenate([g[i] for i in range(N_DEV - 1)] + [g[N_DEV - 1, :last], z128, wk, z64, z128,
                                                               _rot_half_rows(wk), z64], axis=0)


def _ext_q_t(wt):
    r = wt.shape[1]
    z64, z128 = jnp.zeros((64, r), BF16), jnp.zeros((128, r), BF16)
    per = HEAD_DIM + ROPE_DIM
    main = [jnp.concatenate([wt[per * h:per * (h + 1)], z64], axis=0) for h in range(HEADS)]
    rot = [jnp.concatenate([z128, _rot_half_rows(wt[per * h + HEAD_DIM:per * (h + 1)]), z64], axis=0)
           for h in range(HEADS)]
    return jnp.concatenate(main + rot, axis=0)


def _ext_kv(w_kv_up):
    r = w_kv_up.shape[0]
    z128 = jnp.zeros((r, 128), BF16)
    wkv = w_kv_up.reshape(r, HEADS, 2 * HEAD_DIM)
    kpad = [jnp.concatenate([wkv[:, h, :HEAD_DIM], z128], axis=1) for h in range(HEADS)]
    vals = [wkv[:, h, HEAD_DIM:] for h in range(HEADS)]
    return jnp.concatenate(kpad + vals, axis=1)


def _grad_in_from_ext_t(dwt_h, dwt_m):
    dwk = dwt_m[512 + 128:512 + 192] + _unrot_half_rows(dwt_m[768 + 128:768 + 192])
    return jnp.concatenate([dwt_h, dwt_m[:512], dwk], axis=0)


def _grad_q_from_ext_t(dwq_ext_t):
    rows = []
    for h in range(HEADS):
        main, rot = dwq_ext_t[256 * h:256 * h + 256], dwq_ext_t[1024 + 256 * h:1280 + 256 * h]
        rows += [main[:128], main[128:192] + _unrot_half_rows(rot[128:192])]
    return jnp.concatenate(rows, axis=0)


def _grad_kv_from_ext(dwkv_ext):
    kvcols = []
    for h in range(HEADS):
        kvcols += [dwkv_ext[:, 256 * h:256 * h + 128], dwkv_ext[:, 1024 + 128 * h:1152 + 128 * h]]
    return jnp.concatenate(kvcols, axis=1)


SMALL_W = 6144 + 512 + 512 + 256 + 256 + 4 * 1024 + 128


def kernel(x, c, positions, w_ada, b_ada, w_in, hg_lower_bounds, hg_norm_w, mla_q_norm_w, w_q_up, mla_kv_norm_w, w_kv_up, w_out, ln1_g, ln1_b, w_mlp_in, w_mlp_out, ln2_g, ln2_b, loss_target, m_w_ada, m_b_ada, m_w_in, m_hg_lower_bounds, m_hg_norm_w, m_mla_q_norm_w, m_w_q_up, m_mla_kv_norm_w, m_w_kv_up, m_w_out, m_ln1_g, m_ln1_b, m_w_mlp_in, m_w_mlp_out, m_ln2_g, m_ln2_b, v_w_ada, v_b_ada, v_w_in, v_hg_lower_bounds, v_hg_norm_w, v_mla_q_norm_w, v_w_q_up, v_mla_kv_norm_w, v_w_kv_up, v_w_out, v_ln1_g, v_ln1_b, v_w_mlp_in, v_w_mlp_out, v_ln2_g, v_ln2_b):
    T = x.shape[1]
    me = 4 * lax.axis_index("x") + 2 * lax.axis_index("y") + lax.axis_index("c")
    xs, tgt = x[0], loss_target[0]
    transposed = ("w_in", "w_q_up")
    as_used = lambda n, a: a[0].T if n in transposed else a[0]
    big = {n: as_used(n, a) for n, a in dict(w_in=w_in, w_q_up=w_q_up, w_kv_up=w_kv_up, w_out=w_out,
                                              w_mlp_in=w_mlp_in, w_mlp_out=w_mlp_out).items()}
    names = list(big)

    bf = {n: big[n].astype(BF16) for n in names}
    g_in, g_c = _gather_two_level([bf["w_in"], c], name="gather_w_in")
    c_all = g_c.reshape(N_DEV, D_MODEL)

    ada_cols = w_ada.shape[2]
    mod_part, cond = _mod_part(c_all, w_ada[0], lax.dynamic_slice(b_ada, (0, me * ada_cols), (1, ada_cols)))
    (mod_all,) = _exchange([mod_part], scatter=False, name="gather_mod")
    mod_row = lax.dynamic_slice(mod_all, (0, me, 0), (N_DEV, 1, ada_cols)).reshape(1, N_DEV * ada_cols)
    sh_a, sc_a, g_a, sh_m, sc_m, g_m = [mod_row[:, D_MODEL * i:D_MODEL * (i + 1)] for i in range(6)]

    w_in_ext = _ext_in_t(g_in)
    z, (w1,) = _matmul(xs, w_in_ext, "NT", "in_proj", a_fn=_modulate, extras=(sc_a, sh_a), tn=3072,
                       exchange=_StagedGather(bf["w_mlp_in"]))
    (o_raw, o_gated, s_prev), (g_q, g_kv, g_out) = _hgrn_fwd(
        z, hg_lower_bounds, hg_norm_w, exchange=_Exchange([bf["w_q_up"], bf["w_kv_up"], bf["w_out"]], False))
    wq_ext = _ext_q_t(g_q.reshape(N_DEV * g_q.shape[1], g_q.shape[2]))
    wkv_ext = _ext_kv(_cols_from_slabs(g_kv))
    w_out_full = g_out.reshape(D_MODEL, D_MODEL)
    inv_freq = 1.0 / (ROPE_THETA ** (jnp.arange(0, ROPE_DIM, 2, dtype=F32) / ROPE_DIM))
    zeros = lambda n: jnp.zeros((n,), F32)
    invf = jnp.concatenate([zeros(128), inv_freq, inv_freq, zeros(64)]).reshape(1, QK_PAD)
    m_one = jnp.concatenate([jnp.ones((128,), F32), zeros(128)]).reshape(1, QK_PAD)
    m_rot = jnp.concatenate([zeros(128), jnp.ones((64,), F32), zeros(64)]).reshape(1, QK_PAD)
    q, k, v, c1, s1, cqn, ckvn = _mla_pre(z, positions.reshape(T, 1), invf, m_one, m_rot, wq_ext, wkv_ext,
                                          mla_q_norm_w, mla_kv_norm_w)
    (o_mla, lse), (w2,) = _attn_fwd(q, k, v, exchange=_StagedGather(bf["w_mlp_out"]))
    mix, xhat1, rstd1, u2 = _mix_ln1(o_gated, o_mla, w_out_full, xs, g_a, ln1_g, ln1_b, sc_m, sh_m)[0]
    r, dr2, dh, dln2_g, dln2_b, dg_m, loss_part = _mlp_fwd(u2, w1, w2, xhat1, ln1_g, ln1_b, g_m, ln2_g, ln2_b, tgt)

    dhpre, dr1, dmix, dsc_m, dsh_m, dln1_g, dln1_b, dg_a = _mlp_bwd(dh, w1, w2, r, dr2, xhat1, rstd1, mix, ln1_g,
                                                                    ln1_b, sc_m, g_a)
    received = {}
    dw2 = _matmul(r, dh, "TN", "wgrad_mlp_out", out_dtype=BF16, a_fn=_square, tm=1024)
    dw1 = _matmul(u2, dhpre, "TN", "wgrad_mlp_in", out_dtype=BF16, tm=1024, out_slabs=N_DEV)
    dmixcat = _matmul(dmix, w_out_full, "NT", "dgrad_out")
    dw_out = jnp.concatenate([_matmul(o_gated, dmix, "TN", "wgrad_out_hg", out_dtype=BF16),
                              _matmul(o_mla, dmix, "TN", "wgrad_out_mla", out_dtype=BF16)], axis=0)
    (dz_h, dlb, dnw), (received["w_out"],) = _hgrn_bwd(
        dmixcat, z, o_raw, s_prev, hg_lower_bounds, hg_norm_w,
        exchange=_Exchange([dw_out.reshape(N_DEV, D_MODEL // N_DEV, D_MODEL)], True))
    (dq, dk, dv), (received["w_mlp_in"], received["w_mlp_out"]) = _attn_bwd(
        q, k, v, dmixcat, o_mla, lse,
        exchange=_Exchange([dw1, dw2.reshape(N_DEV, dw2.shape[0] // N_DEV, D_MODEL)], True))
    dz_m, dq_ext, dkv_ext, dqnw, dkvnw = _mla_bwd(dq, dk, dv, z, c1, s1, wq_ext, wkv_ext, mla_q_norm_w,
                                                   mla_kv_norm_w)
    dwq_t = _grad_q_from_ext_t(_matmul(dq_ext, cqn, "TN", "wgrad_q_up", tm=1024))
    dwkv = _grad_kv_from_ext(_matmul(ckvn, dkv_ext, "TN", "wgrad_kv_up", tn=1536))
    qkv_slabs = [dwq_t.reshape((N_DEV, dwq_t.shape[0] // N_DEV, dwq_t.shape[1])).astype(BF16),
                 _slabs_from_cols(dwkv).astype(BF16)]
    dwt_h, (received["w_q_up"], received["w_kv_up"]) = _matmul(
        dz_h, xs, "TN", "wgrad_in_h", b_fn=_modulate, extras=(sc_a, sh_a), tm=1024,
        exchange=_Exchange(qkv_slabs, True))
    dwt_m = _matmul(dz_m, xs, "TN", "wgrad_in_m", b_fn=_modulate, extras=(sc_a, sh_a), tm=1024)
    dw_in_t = _grad_in_from_ext_t(dwt_h, dwt_m)
    in_slabs = dw_in_t.reshape((N_DEV, dw_in_t.shape[0] // N_DEV, dw_in_t.shape[1])).astype(BF16)
    (grad_x, dsc_a, dsh_a), (received["w_in"],) = _input_bwd(
        dz_h, dz_m, w_in_ext, xs, dr1, sc_a, exchange=_Exchange([in_slabs], True))

    small = jnp.concatenate([dsh_a, dsc_a, dg_a, dsh_m, dsc_m, dg_m, dlb, dnw, dqnw, dkvnw, dln1_g, dln1_b, dln2_g,
                             dln2_b, loss_part], axis=1)
    (small_all,) = _exchange([small], scatter=False, name="gather_small")

    moments = dict(w_in=(m_w_in, v_w_in), w_q_up=(m_w_q_up, v_w_q_up), w_kv_up=(m_w_kv_up, v_w_kv_up),
                   w_out=(m_w_out, v_w_out), w_mlp_in=(m_w_mlp_in, v_w_mlp_in), w_mlp_out=(m_w_mlp_out, v_w_mlp_out))
    res = {}
    for n in names:
        res[n] = _adam(received[n], big[n], as_used(n, moments[n][0]), as_used(n, moments[n][1]), name="adam_" + n)
    dmod_cols = lax.dynamic_slice(small_all.reshape(N_DEV, SMALL_W), (0, me * ada_cols), (N_DEV, ada_cols))
    cond_t = jnp.pad(cond.T, ((0, 0), (0, 128 - N_DEV)))
    dmod_cols = jnp.pad(dmod_cols, ((0, 128 - N_DEV), (0, 0)))
    res["w_ada"] = _adam(None, w_ada[0], m_w_ada[0], v_w_ada[0], name="adam_w_ada",
                         g_fn=lambda ct_ref, dm_ref: _hdot(ct_ref[...], dm_ref[...]), g_extra=(cond_t, dmod_cols))

    small_params = [("b_ada", b_ada, m_b_ada, v_b_ada, 0),
                    ("hg_lower_bounds", hg_lower_bounds, m_hg_lower_bounds, v_hg_lower_bounds, 6144),
                    ("hg_norm_w", hg_norm_w, m_hg_norm_w, v_hg_norm_w, 6656),
                    ("mla_q_norm_w", mla_q_norm_w, m_mla_q_norm_w, v_mla_q_norm_w, 7168),
                    ("mla_kv_norm_w", mla_kv_norm_w, m_mla_kv_norm_w, v_mla_kv_norm_w, 7424),
                    ("ln1_g", ln1_g, m_ln1_g, v_ln1_g, 7680), ("ln1_b", ln1_b, m_ln1_b, v_ln1_b, 8704),
                    ("ln2_g", ln2_g, m_ln2_g, v_ln2_g, 9728), ("ln2_b", ln2_b, m_ln2_b, v_ln2_b, 10752)]
    loss_row, small_res = _adam_small(small_all, [p[1:] for p in small_params])
    for p, r4 in zip(small_params, small_res):
        res[p[0]] = r4
    loss = loss_row[0, 0]

    order = ["w_ada", "b_ada", "w_in", "hg_lower_bounds", "hg_norm_w", "mla_q_norm_w", "w_q_up", "mla_kv_norm_w",
             "w_kv_up", "w_out", "ln1_g", "ln1_b", "w_mlp_in", "w_mlp_out", "ln2_g", "ln2_b"]
    def as_given(n, a):
        if n in transposed:
            a = a.T
        return a[None] if n in big or n == "w_ada" else a

    shaped = {n: tuple(as_given(n, a) for a in res[n]) for n in order}
    outs = [loss, grad_x.reshape(1, T, D_MODEL)]
    for i in range(4):
        outs += [shaped[n][i] for n in order]
    return tuple(outs)
```

```python
import functools

import jax
import jax.numpy as jnp
import numpy as np
from jax import lax
from jax.experimental import pallas as pl
from jax.experimental.pallas import tpu as pltpu

F32, BF16 = jnp.float32, jnp.bfloat16
N_DEV = 8
D_MODEL = 1024
HEADS = 4
HEAD_DIM = 128
ROPE_DIM = 64
QK_PAD = 256
CHUNK = 64
ROPE_THETA = 10000.0
RMS_EPS = 1e-6
LN_EPS = 1e-5
ALPHA = 2.0 ** 0.25
ATT_SCALE = (HEAD_DIM + ROPE_DIM) ** -0.5
LN2 = float(np.log(2.0))
Q_PRESCALE = ATT_SCALE / LN2
ADAM_LR, ADAM_B1, ADAM_B2, ADAM_EPS, ADAM_WD, ADAM_STEP = 0.001, 0.9, 0.999, 1e-08, 0.01, 10
NEG_BIG = -1e30

ROW_TILE = 512
ROW_TILE_SMALL = 256
ATT_TILE = 512
HGRN_GROUP = 8
MLP_SLABS = 4
VMEM_LIMIT = 56 * 2 ** 20

NN = (((1,), (0,)), ((), ()))
NT = (((1,), (1,)), ((), ()))
TN = (((0,), (0,)), ((), ()))


def _dot(a, b, dims=NN):
    return lax.dot_general(a, b, dims, preferred_element_type=F32)


def _bdot(a, b, dims=NN):
    return lax.dot_general(a.astype(BF16), b.astype(BF16), dims, preferred_element_type=F32)


def _hdot(a, b, dims=NN):
    return lax.dot_general(a, b, dims, precision=lax.Precision.HIGHEST, preferred_element_type=F32)


def _params():
    return pltpu.CompilerParams(vmem_limit_bytes=VMEM_LIMIT)


def _sigmoid(x):
    return 1.0 / (1.0 + jnp.exp(-x))


def _rowsum(x):
    return jnp.sum(x, axis=0, keepdims=True)


def _lanemean(x):
    return jnp.mean(x, axis=-1, keepdims=True)


def _full(shape):
    nd = len(shape)
    return pl.BlockSpec(shape, lambda *_: (0,) * nd)


class _Exchange:
    def __init__(self, arrs, scatter):
        self.arrs, self.scatter, self.n, self.aliases = list(arrs), scatter, len(arrs), []
        self.out_shape = [jax.ShapeDtypeStruct((N_DEV,) + (a.shape[1:] if scatter else a.shape), a.dtype)
                          for a in self.arrs]
        n = self.n
        self.scratch = [pltpu.SemaphoreType.DMA((n, N_DEV - 1)), pltpu.SemaphoreType.DMA((n, N_DEV - 1)),
                        pltpu.SemaphoreType.DMA((n,))]

    def _copies(self, ins, outs, sems):
        send_sems, recv_sems, loc_sems = sems
        x, y, c = lax.axis_index("x"), lax.axis_index("y"), lax.axis_index("c")
        me = 4 * x + 2 * y + c
        copies = []
        for k in range(self.n):
            src_of = (lambda i, k=k: ins[k].at[i]) if self.scatter else (lambda i, k=k: ins[k])
            copies.append((pltpu.make_async_copy(src_of(me), outs[k].at[me], loc_sems.at[k]), None))
            for p in range(1, N_DEV):
                px = (1 - x) if p & 4 else x
                py = (1 - y) if p & 2 else y
                pc = (1 - c) if p & 1 else c
                peer = 4 * px + 2 * py + pc
                both = dict(send_sem=send_sems.at[k, p - 1], recv_sem=recv_sems.at[k, p - 1],
                            device_id=(px, py, pc), device_id_type=pl.DeviceIdType.MESH)
                send = pltpu.make_async_remote_copy(src_ref=src_of(peer), dst_ref=outs[k].at[me], **both)
                recv = pltpu.make_async_remote_copy(src_ref=src_of(peer), dst_ref=outs[k].at[peer], **both)
                copies.append((send, recv))
        return copies

    def start(self, ins, outs, sems):
        for first, _ in self._copies(ins, outs, sems):
            first.start()

    def middle(self, ins, outs, sems):
        pass

    def wait(self, ins, outs, sems):
        for first, recv in self._copies(ins, outs, sems):
            if recv is None:
                first.wait()
            else:
                recv.wait_recv()
                first.wait_send()


class _StagedGather:
    def __init__(self, arr, rows=None, into=None):
        self.r0, total = rows if rows else (0, arr.shape[0])
        self.arrs = [arr] + ([into] if into is not None else [])
        self.out_shape = [jax.ShapeDtypeStruct((N_DEV, total) + arr.shape[1:], arr.dtype)]
        self.aliases = [(1, 0)] if into is not None else []
        self.scratch = [pltpu.VMEM((N_DEV,) + arr.shape, arr.dtype), pltpu.SemaphoreType.DMA((7,)),
                        pltpu.SemaphoreType.DMA((7,)), pltpu.SemaphoreType.DMA((2,))]

    def _parts(self, scr):
        stage, send_sems, recv_sems, loc_sems = scr
        x, y, c = lax.axis_index("x"), lax.axis_index("y"), lax.axis_index("c")
        me, sibling = (x, y, c), (x, y, 1 - c)
        chips = [(1 - x, y), (x, 1 - y), (1 - x, 1 - y)]

        def copy(j, block, to):
            px, py, pc = block
            slot = stage.at[4 * px + 2 * py + pc]
            return pltpu.make_async_remote_copy(src_ref=slot, dst_ref=slot, send_sem=send_sems.at[j],
                                                recv_sem=recv_sems.at[j], device_id=to,
                                                device_id_type=pl.DeviceIdType.MESH)

        return stage, loc_sems, me, sibling, chips, c, copy

    def start(self, ins, outs, scr):
        stage, loc_sems, me, sibling, chips, c, copy = self._parts(scr)
        x, y, _ = me
        own = pltpu.make_async_copy(ins[0], stage.at[4 * x + 2 * y + c], loc_sems.at[0])
        own.start()
        own.wait()
        copy(0, me, sibling).start()
        for j, chip in enumerate(chips):
            copy(1 + j, me, (*chip, c)).start()

    def middle(self, ins, outs, scr):
        stage, loc_sems, me, sibling, chips, c, copy = self._parts(scr)
        for j, chip in enumerate(chips):
            copy(1 + j, (*chip, c), me).wait_recv()
            copy(4 + j, (*chip, c), sibling).start()

    def wait(self, ins, outs, scr):
        stage, loc_sems, me, sibling, chips, c, copy = self._parts(scr)
        copy(0, sibling, me).wait_recv()
        for j, chip in enumerate(chips):
            copy(4 + j, (*chip, 1 - c), me).wait_recv()
        copy(0, me, sibling).wait_send()
        for j, chip in enumerate(chips):
            copy(1 + j, me, (*chip, c)).wait_send()
            copy(4 + j, (*chip, c), sibling).wait_send()
        whole = pltpu.make_async_copy(stage, outs[0].at[:, pl.ds(self.r0, stage.shape[1])], loc_sems.at[1])
        whole.start()
        whole.wait()


def _call(body, name, args, out_shape, grid=(), in_specs=(), out_specs=(), scratch_shapes=(), exchange=None,
          middle_at=0.8):
    if exchange is None:
        return pl.pallas_call(body, name=name, grid=grid, in_specs=list(in_specs), out_specs=list(out_specs),
                              out_shape=list(out_shape), scratch_shapes=list(scratch_shapes),
                              compiler_params=_params())(*args), None
    exs = list(exchange) if isinstance(exchange, (list, tuple)) else [exchange]
    ni, no, ns = len(args), len(out_shape), len(scratch_shapes)
    nxi, nxo = sum(len(e.arrs) for e in exs), sum(len(e.out_shape) for e in exs)
    steps = int(np.prod(grid))
    mid_step = min(max(int(steps * middle_at), 1), steps - 1)
    aliases, iat, oat = {}, ni, no
    for e in exs:
        for src, dst in e.aliases:
            aliases[iat + src] = oat + dst
        iat, oat = iat + len(e.arrs), oat + len(e.out_shape)

    def wrapped(*refs):
        a, xi = refs[:ni], refs[ni:ni + nxi]
        o, xo = refs[ni + nxi:ni + nxi + no], refs[ni + nxi + no:ni + nxi + no + nxo]
        s, xs = refs[ni + nxi + no + nxo:ni + nxi + no + nxo + ns], refs[ni + nxi + no + nxo + ns:]
        parts, iat, oat, sat = [], 0, 0, 0
        for e in exs:
            parts.append((e, xi[iat:iat + len(e.arrs)], xo[oat:oat + len(e.out_shape)], xs[sat:sat + len(e.scratch)]))
            iat, oat, sat = iat + len(e.arrs), oat + len(e.out_shape), sat + len(e.scratch)
        step = 0
        for d, g in enumerate(grid):
            step = step * g + pl.program_id(d)

        @pl.when(step == 0)
        def _():
            for e, ins, outs, sems in parts:
                e.start(ins, outs, sems)

        @pl.when(step == mid_step)
        def _():
            for e, ins, outs, sems in parts:
                e.middle(ins, outs, sems)

        body(*a, *o, *s)

        @pl.when(step == steps - 1)
        def _():
            for e, ins, outs, sems in parts:
                e.wait(ins, outs, sems)

    hbm = pl.BlockSpec(memory_space=pltpu.HBM)
    res = pl.pallas_call(
        wrapped, name=name, grid=grid, in_specs=list(in_specs) + [hbm] * nxi, out_specs=list(out_specs) + [hbm] * nxo,
        out_shape=list(out_shape) + [o_ for e in exs for o_ in e.out_shape],
        scratch_shapes=list(scratch_shapes) + [s_ for e in exs for s_ in e.scratch],
        input_output_aliases=aliases, compiler_params=_params())(*args, *[a_ for e in exs for a_ in e.arrs])
    return res[:no], res[no:]


def _gather_two_level(arrs, name):
    n = len(arrs)
    out_shape = [jax.ShapeDtypeStruct((N_DEV,) + a.shape, a.dtype) for a in arrs]

    def body(*refs):
        ins, outs = refs[:n], refs[n:2 * n]
        send_sems, recv_sems, loc_sems = refs[2 * n:]
        x, y, c = lax.axis_index("x"), lax.axis_index("y"), lax.axis_index("c")
        me, sibling = (x, y, c), (x, y, 1 - c)
        chips = [(1 - x, y), (x, 1 - y), (1 - x, 1 - y)]

        def copy(k, j, block, to, src=None):
            px, py, pc = block
            dst = outs[k].at[4 * px + 2 * py + pc]
            return pltpu.make_async_remote_copy(src_ref=dst if src is None else src, dst_ref=dst,
                                                send_sem=send_sems.at[k, j], recv_sem=recv_sems.at[k, j],
                                                device_id=to, device_id_type=pl.DeviceIdType.MESH)

        mine = [pltpu.make_async_copy(ins[k], outs[k].at[4 * x + 2 * y + c], loc_sems.at[k]) for k in range(n)]
        first = []
        for k in range(n):
            mine[k].start()
            first.append(copy(k, 0, me, sibling, src=ins[k]))
            first += [copy(k, 1 + j, me, (*chip, c), src=ins[k]) for j, chip in enumerate(chips)]
        for cp in first:
            cp.start()
        passed = []
        for j, chip in enumerate(chips):
            for k in range(n):
                copy(k, 1 + j, (*chip, c), me).wait_recv()
                passed.append(copy(k, 4 + j, (*chip, c), sibling))
                passed[-1].start()
        for k in range(n):
            copy(k, 0, sibling, me).wait_recv()
            for j, chip in enumerate(chips):
                copy(k, 4 + j, (*chip, 1 - c), me).wait_recv()
        for cp in first + passed:
            cp.wait_send()
        for cp in mine:
            cp.wait()

    vmem = pl.BlockSpec(memory_space=pltpu.VMEM)
    return pl.pallas_call(body, name=name, out_shape=out_shape, in_specs=[vmem] * n, out_specs=[vmem] * n,
                          scratch_shapes=[pltpu.SemaphoreType.DMA((n, 7)), pltpu.SemaphoreType.DMA((n, 7)),
                                          pltpu.SemaphoreType.DMA((n,))], compiler_params=_params())(*arrs)


def _exchange(arrs, scatter, name):
    ex = _Exchange(arrs, scatter)

    def body(*refs):
        ins, outs, sems = refs[:ex.n], refs[ex.n:2 * ex.n], refs[2 * ex.n:]
        ex.start(ins, outs, sems)
        ex.wait(ins, outs, sems)

    hbm = pl.BlockSpec(memory_space=pltpu.HBM)
    return pl.pallas_call(body, name=name, out_shape=ex.out_shape, in_specs=[hbm] * ex.n, out_specs=[hbm] * ex.n,
                          scratch_shapes=ex.scratch)(*ex.arrs)


def _matmul(a, b, mode, name, out_dtype=F32, tm=512, tn=1024, tk=1024, a_fn=None, b_fn=None, extras=(),
            out_slabs=None, exchange=None):
    assert not (a_fn and b_fn) and not (b_fn and mode == "NT")
    if mode == "NN":
        (M, K), N = a.shape, b.shape[1]
    elif mode == "NT":
        (M, K), N = a.shape, b.shape[0]
    else:
        (K, M), N = a.shape, b.shape[1]
    if out_slabs:
        tn = N // out_slabs
    tm, tn, tk = min(tm, M), min(tn, N), min(tk, K)
    assert M % tm == 0 and N % tn == 0 and K % tk == 0, (name, M, N, K)
    nk = K // tk
    dims = {"NN": NN, "NT": NT, "TN": TN}[mode]
    ne = len(extras)

    def body(a_ref, b_ref, *rest):
        e_refs, o_ref, acc_ref = rest[:ne], rest[ne], rest[ne + 1]
        k = pl.program_id(2)

        @pl.when(k == 0)
        def _():
            acc_ref[...] = jnp.zeros_like(acc_ref)

        at, bt = a_ref[...], b_ref[...]
        if a_fn is not None:
            at = a_fn(at.astype(F32), *[e[...] for e in e_refs])
        if b_fn is not None:
            bt = b_fn(bt.astype(F32), *[e[...] for e in e_refs])
        acc_ref[...] += _bdot(at, bt, dims)

        @pl.when(k == nk - 1)
        def _():
            o_ref[...] = acc_ref[...].astype(out_dtype)

    if mode == "TN":
        a_spec = pl.BlockSpec((tk, tm), lambda i, j, k: (k, i))
        e_spec = pl.BlockSpec((1, tm), lambda i, j, k: (0, i))
    else:
        a_spec = pl.BlockSpec((tm, tk), lambda i, j, k: (i, k))
        e_spec = pl.BlockSpec((1, tk), lambda i, j, k: (0, k))
    if mode == "NT":
        b_spec = pl.BlockSpec((tn, tk), lambda i, j, k: (j, k))
    else:
        b_spec = pl.BlockSpec((tk, tn), lambda i, j, k: (k, j))
    if b_fn is not None:
        e_spec = pl.BlockSpec((1, tn), lambda i, j, k: (0, j))
    if out_slabs:
        o_shape = jax.ShapeDtypeStruct((out_slabs, M, tn), out_dtype)
        o_spec = pl.BlockSpec((None, tm, tn), lambda i, j, k: (j, i, 0))
    else:
        o_shape = jax.ShapeDtypeStruct((M, N), out_dtype)
        o_spec = pl.BlockSpec((tm, tn), lambda i, j, k: (i, j))
    (out,), got = _call(body, name, (a, b, *extras), [o_shape], grid=(M // tm, N // tn, nk),
                        in_specs=[a_spec, b_spec] + [e_spec] * ne, out_specs=[o_spec],
                        scratch_shapes=[pltpu.VMEM((tm, tn), F32)], exchange=exchange)
    return out if exchange is None else (out, got)


def _modulate(x, sc, sh):
    return x * (1.0 + sc) + sh


def _square(x):
    return x * x


def _mod_part(c_all, w_ada_s, b_s):
    def body(c_ref, w_ref, b_ref, mod_ref, cond_ref):
        cv = c_ref[...]
        cond = cv * _sigmoid(cv)
        cond_ref[...] = cond
        mod_ref[...] = _bdot(cond, w_ref[...]) + b_ref[...]

    return pl.pallas_call(
        body, name="mod_part",
        out_shape=[jax.ShapeDtypeStruct((N_DEV, w_ada_s.shape[1]), F32), jax.ShapeDtypeStruct(c_all.shape, F32)],
        compiler_params=_params(),
    )(c_all, w_ada_s, b_s)


def _rms_fwd(x, w):
    rs = lax.rsqrt(_lanemean(x * x) + RMS_EPS)
    return x * rs * w, rs


def _rms_bwd(x, rs, w, dy):
    xhat = x * rs
    dxh = dy * w
    return rs * (dxh - xhat * _lanemean(dxh * xhat)), dy * xhat


def _mla_pre(z, pos_col, invf, m_one, m_rot, wq_ext, wkv_ext, qnw, kvnw, exchange=None):
    T = z.shape[0]
    tm = min(ROW_TILE, T)

    def body(z_ref, pos_ref, invf_ref, mone_ref, mrot_ref, wq_ref, wkv_ref, qnw_ref, kvnw_ref,
             q_ref, k_ref, v_ref, c1_ref, s1_ref, cqn_ref, ckvn_ref):
        ang = pos_ref[...].astype(F32) * invf_ref[...]
        c1 = mone_ref[...] + mrot_ref[...] * jnp.cos(ang)
        s1 = mrot_ref[...] * jnp.sin(ang)
        c1_ref[...] = c1
        s1_ref[...] = s1
        cqn, _ = _rms_fwd(z_ref[:, 0:256], qnw_ref[...])
        ckvn, _ = _rms_fwd(z_ref[:, 256:512], kvnw_ref[...])
        cqn_ref[...] = cqn.astype(BF16)
        ckvn_ref[...] = ckvn.astype(BF16)
        qe = _bdot(cqn, wq_ref[...], NT)
        kve = _bdot(ckvn, wkv_ref[...])
        k_rope = z_ref[:, 512:768] * c1 + z_ref[:, 768:1024] * s1
        for h in range(HEADS):
            q_ref[h] = ((qe[:, 256 * h:256 * h + 256] * c1 + qe[:, 1024 + 256 * h:1280 + 256 * h] * s1)
                        * Q_PRESCALE).astype(BF16)
            k_ref[h] = (kve[:, 256 * h:256 * h + 256] + k_rope).astype(BF16)
            v_ref[h] = kve[:, 1024 + 128 * h:1152 + 128 * h].astype(BF16)

    row = lambda i: (i, 0)
    head = lambda i: (0, i, 0)
    return _call(
        body, "mla_pre", (z, pos_col, invf, m_one, m_rot, wq_ext, wkv_ext, qnw, kvnw), grid=(T // tm,),
        in_specs=[pl.BlockSpec((tm, 1024), lambda i: (i, 2)), pl.BlockSpec((tm, 1), row),
                  _full((1, 256)), _full((1, 256)), _full((1, 256)), _full(wq_ext.shape), _full(wkv_ext.shape),
                  _full((1, 256)), _full((1, 256))],
        out_specs=[pl.BlockSpec((HEADS, tm, QK_PAD), head), pl.BlockSpec((HEADS, tm, QK_PAD), head),
                   pl.BlockSpec((HEADS, tm, HEAD_DIM), head), pl.BlockSpec((tm, 256), row), pl.BlockSpec((tm, 256), row),
                   pl.BlockSpec((tm, 256), row), pl.BlockSpec((tm, 256), row)],
        out_shape=[jax.ShapeDtypeStruct((HEADS, T, QK_PAD), BF16), jax.ShapeDtypeStruct((HEADS, T, QK_PAD), BF16),
                   jax.ShapeDtypeStruct((HEADS, T, HEAD_DIM), BF16), jax.ShapeDtypeStruct((T, 256), F32),
                   jax.ShapeDtypeStruct((T, 256), F32), jax.ShapeDtypeStruct((T, 256), BF16),
                   jax.ShapeDtypeStruct((T, 256), BF16)], exchange=exchange)


def _mla_bwd(dq, dk, dv, z, c1, s1, wq_ext, wkv_ext, qnw, kvnw):
    T = z.shape[0]
    tm = min(ROW_TILE_SMALL, T)

    def body(dq_ref, dk_ref, dv_ref, z_ref, c1_ref, s1_ref, wq_ref, wkv_ref, qnw_ref, kvnw_ref,
             dz_ref, dqe_ref, dkve_ref, dqnw_ref, dkvnw_ref):
        @pl.when(pl.program_id(0) == 0)
        def _():
            dqnw_ref[...] = jnp.zeros_like(dqnw_ref)
            dkvnw_ref[...] = jnp.zeros_like(dkvnw_ref)

        c1, s1 = c1_ref[...], s1_ref[...]
        dkpe = jnp.zeros((tm, QK_PAD), F32)
        for h in range(HEADS):
            dqh, dkh = dq_ref[h] * Q_PRESCALE, dk_ref[h]
            dqe_ref[:, 256 * h:256 * h + 256] = (dqh * c1).astype(BF16)
            dqe_ref[:, 1024 + 256 * h:1280 + 256 * h] = (dqh * s1).astype(BF16)
            dkve_ref[:, 256 * h:256 * h + 256] = dkh.astype(BF16)
            dkve_ref[:, 1024 + 128 * h:1152 + 128 * h] = dv_ref[h].astype(BF16)
            dkpe = dkpe + dkh
        dcqn = _dot(dqe_ref[...], wq_ref[...])
        dckvn = _dot(dkve_ref[...], wkv_ref[...], NT)
        cq, ckv = z_ref[:, 0:256], z_ref[:, 256:512]
        _, rsq = _rms_fwd(cq, qnw_ref[...])
        _, rskv = _rms_fwd(ckv, kvnw_ref[...])
        dcq, wq_rows = _rms_bwd(cq, rsq, qnw_ref[...], dcqn)
        dckv, wkv_rows = _rms_bwd(ckv, rskv, kvnw_ref[...], dckvn)
        dqnw_ref[...] += _rowsum(wq_rows)
        dkvnw_ref[...] += _rowsum(wkv_rows)
        dz_ref[:, 0:256] = dcq
        dz_ref[:, 256:512] = dckv
        dz_ref[:, 512:768] = dkpe * c1
        dz_ref[:, 768:1024] = dkpe * s1

    row = lambda i: (i, 0)
    head = lambda i: (0, i, 0)
    return pl.pallas_call(
        body, name="mla_bwd", grid=(T // tm,),
        in_specs=[pl.BlockSpec((HEADS, tm, QK_PAD), head), pl.BlockSpec((HEADS, tm, QK_PAD), head),
                  pl.BlockSpec((HEADS, tm, HEAD_DIM), head), pl.BlockSpec((tm, 1024), lambda i: (i, 2)),
                  pl.BlockSpec((tm, 256), row), pl.BlockSpec((tm, 256), row), _full(wq_ext.shape), _full(wkv_ext.shape),
                  _full((1, 256)), _full((1, 256))],
        out_specs=[pl.BlockSpec((tm, 1024), row), pl.BlockSpec((tm, 2048), row), pl.BlockSpec((tm, 1536), row),
                   _full((1, 256)), _full((1, 256))],
        out_shape=[jax.ShapeDtypeStruct((T, 1024), F32), jax.ShapeDtypeStruct((T, 2048), BF16),
                   jax.ShapeDtypeStruct((T, 1536), BF16), jax.ShapeDtypeStruct((1, 256), F32),
                   jax.ShapeDtypeStruct((1, 256), F32)],
        compiler_params=_params(),
    )(dq, dk, dv, z, c1, s1, wq_ext, wkv_ext, qnw, kvnw)


_HEAD_LANES = [slice(HEAD_DIM * h, HEAD_DIM * (h + 1)) for h in range(HEADS)]


def _lower_bound(lbraw_ref):
    a0, a1 = lbraw_ref[0:1, :], lbraw_ref[1:2, :]
    mx = jnp.maximum(a0, a1)
    e0, e1 = jnp.exp(a0 - mx), jnp.exp(a1 - mx)
    return e0 / (e0 + e1)


def _tri(lower):
    r = lax.broadcasted_iota(jnp.int32, (CHUNK, CHUNK), 0)
    c = lax.broadcasted_iota(jnp.int32, (CHUNK, CHUNK), 1)
    return (r >= c) if lower else (r <= c)


def _hgrn_gates(q, f, lb, tri_lo):
    sg = _sigmoid(f)
    forget = lb + (1.0 - lb) * sg
    k = 1.0 - forget
    b = _hdot(tri_lo.astype(F32), jnp.log(forget))
    b_ref, b_last = b[CHUNK // 2 - 1:CHUNK // 2, :], b[CHUNK - 1:CHUNK, :]
    e1, e2, e3, e4 = jnp.exp(b - b_ref), jnp.exp(b_ref - b), jnp.exp(b_last - b), jnp.exp(b)
    return dict(sg=sg, forget=forget, k=k, e1=e1, e2=e2, e3=e3, e4=e4, qa=q * e1, ka=k * e2, kl=k * e3, qb=q * e4,
                decay=jnp.exp(b_last))


def _hgrn_fwd(z, lbraw, nw, exchange=None):
    T = z.shape[0]
    G = min(HGRN_GROUP, T // CHUNK)
    rows = G * CHUNK
    n_chunks = T // CHUNK

    def body(q_ref, f_ref, i_ref, g_ref, lbraw_ref, nw_ref, oraw_ref, og_ref, sp_ref, st_ref):
        @pl.when(pl.program_id(0) == 0)
        def _():
            st_ref[...] = jnp.zeros_like(st_ref)

        lb_all = _lower_bound(lbraw_ref)
        tri_lo = _tri(True)

        def chunk(cc, carry):
            rs = pl.ds(pl.multiple_of(cc * CHUNK, CHUNK), CHUNK)
            t = _hgrn_gates(q_ref[rs, :], f_ref[rs, :], lb_all, tri_lo)
            v, gate = i_ref[rs, :], g_ref[rs, :]
            st = [st_ref[h] for h in range(HEADS)]
            a = [jnp.where(tri_lo, _bdot(t["qa"][:, s], t["ka"][:, s], NT), 0.0) for s in _HEAD_LANES]
            kv = [_bdot(v[:, s], t["kl"][:, s], TN) for s in _HEAD_LANES]
            o = [_bdot(a[h], v[:, s]) + _bdot(t["qb"][:, s], st[h], NT) for h, s in enumerate(_HEAD_LANES)]
            for h, s in enumerate(_HEAD_LANES):
                sp_ref[cc, h] = st[h]
                st_ref[h] = st[h] * t["decay"][:, s] + kv[h]
            oraw_ref[rs, :] = jnp.concatenate(o, axis=1)
            on = jnp.concatenate([_rms_fwd(o[h], nw_ref[:, s])[0] for h, s in enumerate(_HEAD_LANES)], axis=1)
            og_ref[rs, :] = (on * (gate * _sigmoid(gate))).astype(BF16)
            return carry

        lax.fori_loop(0, G, chunk, 0, unroll=2)

    col = lambda j: pl.BlockSpec((rows, 512), lambda r, j=j: (r, j))
    return _call(
        body, "hgrn_fwd", (z, z, z, z, lbraw, nw), grid=(T // rows,),
        in_specs=[col(0), col(1), col(2), col(3), _full((2, 512)), _full((1, 512))],
        out_specs=[col(0), col(0), pl.BlockSpec((G, HEADS, HEAD_DIM, HEAD_DIM), lambda r: (r, 0, 0, 0))],
        out_shape=[jax.ShapeDtypeStruct((T, 512), F32), jax.ShapeDtypeStruct((T, 512), BF16),
                   jax.ShapeDtypeStruct((n_chunks, HEADS, HEAD_DIM, HEAD_DIM), F32)],
        scratch_shapes=[pltpu.VMEM((HEADS, HEAD_DIM, HEAD_DIM), F32)], exchange=exchange)


def _hgrn_bwd(dmixcat, z, oraw, sprev, lbraw, nw, exchange=None):
    T = z.shape[0]
    G = min(HGRN_GROUP, T // CHUNK)
    rows = G * CHUNK
    ng = T // rows

    def body(dog_ref, q_ref, f_ref, i_ref, g_ref, oraw_ref, sp_ref, lbraw_ref, nw_ref,
             dz_ref, dlb_ref, dnw_ref, dst_ref):
        @pl.when(pl.program_id(0) == 0)
        def _():
            dst_ref[...] = jnp.zeros_like(dst_ref)
            dlb_ref[...] = jnp.zeros_like(dlb_ref)
            dnw_ref[...] = jnp.zeros_like(dnw_ref)

        lb_all = _lower_bound(lbraw_ref)
        tri_lo, tri_up = _tri(True), _tri(False)
        rowid = lax.broadcasted_iota(jnp.int32, (CHUNK, HEADS * HEAD_DIM), 0)

        def chunk(it, carry):
            cc = G - 1 - it
            rs = pl.ds(pl.multiple_of(cc * CHUNK, CHUNK), CHUNK)
            heads = list(enumerate(_HEAD_LANES))
            cat = lambda parts: jnp.concatenate(parts, axis=1)
            per_head_mean = lambda x: cat([jnp.broadcast_to(_lanemean(x[:, s]), (CHUNK, HEAD_DIM)) for s in _HEAD_LANES])
            t = _hgrn_gates(q_ref[rs, :], f_ref[rs, :], lb_all, tri_lo)
            v, gate, o, dog, nw_all = i_ref[rs, :], g_ref[rs, :], oraw_ref[rs, :], dog_ref[rs, :], nw_ref[...]
            rs_o = lax.rsqrt(per_head_mean(o * o) + RMS_EPS)
            xhat = o * rs_o
            sgg = _sigmoid(gate)
            d_on = dog * (gate * sgg)
            dz_ref[rs, 1536:2048] = dog * (xhat * nw_all) * (sgg * (1.0 + gate * (1.0 - sgg)))
            dxh = d_on * nw_all
            do = rs_o * (dxh - xhat * per_head_mean(dxh * xhat))
            dnw_ref[...] += _rowsum(d_on * xhat)
            st = [sp_ref[cc, h] for h in range(HEADS)]
            dst = [dst_ref[h] for h in range(HEADS)]
            a = [jnp.where(tri_lo, _bdot(t["qa"][:, s], t["ka"][:, s], NT), 0.0) for s in _HEAD_LANES]
            da = [jnp.where(tri_lo, _bdot(do[:, s], v[:, s], NT), 0.0) for s in _HEAD_LANES]
            dqb = cat([_bdot(do[:, s], st[h]) for h, s in heads])
            dkl = cat([_bdot(v[:, s], dst[h]) for h, s in heads])
            dv_ = cat([_bdot(t["kl"][:, s], dst[h], NT) + _bdot(a[h], do[:, s], TN) for h, s in heads])
            dqa = cat([_bdot(da[h], t["ka"][:, s]) for h, s in heads])
            dka = cat([_bdot(da[h], t["qa"][:, s], TN) for h, s in heads])
            ddecay = cat([_rowsum(dst[h] * st[h]) for h in range(HEADS)])
            for h, s in heads:
                dst_ref[h] = dst[h] * t["decay"][:, s] + _bdot(do[:, s], t["qb"][:, s], TN)
            pa, pk, pb, pl_ = dqa * t["qa"], dka * t["ka"], dqb * t["qb"], dkl * t["kl"]
            db = pa - pk + pb - pl_
            db = db + jnp.where(rowid == CHUNK // 2 - 1, _rowsum(pk - pa), 0.0)
            db = db + jnp.where(rowid == CHUNK - 1, _rowsum(pl_) + ddecay * t["decay"], 0.0)
            dlogf = _hdot(tri_up.astype(F32), db)
            dforget = dlogf / t["forget"] - (dka * t["e2"] + dkl * t["e3"])
            sg = t["sg"]
            dz_ref[rs, 0:512] = dqa * t["e1"] + dqb * t["e4"]
            dz_ref[rs, 512:1024] = dforget * (1.0 - lb_all) * sg * (1.0 - sg)
            dz_ref[rs, 1024:1536] = dv_
            dlb_ref[...] += _rowsum(dforget * (1.0 - sg))
            return carry

        lax.fori_loop(0, G, chunk, 0, unroll=2)

    col = lambda j: pl.BlockSpec((rows, 512), lambda r, j=j: (ng - 1 - r, j))
    return _call(
        body, "hgrn_bwd", (dmixcat, z, z, z, z, oraw, sprev, lbraw, nw), grid=(ng,),
        in_specs=[col(0), col(0), col(1), col(2), col(3), col(0),
                  pl.BlockSpec((G, HEADS, HEAD_DIM, HEAD_DIM), lambda r: (ng - 1 - r, 0, 0, 0)),
                  _full((2, 512)), _full((1, 512))],
        out_specs=[pl.BlockSpec((rows, 2048), lambda r: (ng - 1 - r, 0)), _full((1, 512)), _full((1, 512))],
        out_shape=[jax.ShapeDtypeStruct((T, 2048), F32), jax.ShapeDtypeStruct((1, 512), F32),
                   jax.ShapeDtypeStruct((1, 512), F32)],
        scratch_shapes=[pltpu.VMEM((HEADS, HEAD_DIM, HEAD_DIM), F32)], exchange=exchange)


def _diag_mask(t):
    r = lax.broadcasted_iota(jnp.int32, (t, t), 0)
    c = lax.broadcasted_iota(jnp.int32, (t, t), 1)
    return r >= c


def _attn_fwd(q, k, v, exchange=None):
    _, T, _ = q.shape
    t = min(ATT_TILE, T)

    def body(q_ref, k_ref, v_ref, o_ref, lse_ref):
        i = pl.program_id(1)
        qb = q_ref[...]

        rows = lambda j: pl.ds(pl.multiple_of(j * t, t), t)

        def logits(j, masked):
            s = _dot(qb, k_ref[rows(j), :], NT)
            return jnp.where(_diag_mask(t), s, NEG_BIG) if masked else s

        def absorb(s, j, carry):
            m, l, acc = carry
            mn = jnp.maximum(m, jnp.max(s, axis=-1, keepdims=True))
            p = jnp.exp2(s - mn)
            al = jnp.exp2(m - mn)
            return mn, al * l + jnp.sum(p, axis=-1, keepdims=True), al * acc + _dot(p.astype(BF16), v_ref[rows(j), :])

        def pair(j0, carry, last_masked):
            s0, s1 = logits(j0, False), logits(j0 + 1, last_masked)
            return absorb(s1, j0 + 1, absorb(s0, j0, carry))

        init = (jnp.full((t, 1), NEG_BIG, F32), jnp.zeros((t, 1), F32), jnp.zeros((t, HEAD_DIM), F32))
        carry = lax.fori_loop(0, i // 2, lambda jj, c: pair(2 * jj, c, False), init)
        m, l, acc = lax.cond(i % 2 == 1, lambda c: pair(i - 1, c, True),
                             lambda c: absorb(logits(i, True), i, c), carry)
        o_ref[...] = acc / l
        lse_ref[...] = jnp.broadcast_to(m + jnp.log2(l), (t, HEAD_DIM))

    return _call(
        body, "attn_fwd", (q, k, v), grid=(HEADS, T // t),
        in_specs=[pl.BlockSpec((None, t, QK_PAD), lambda h, i: (h, i, 0)),
                  pl.BlockSpec((None, T, QK_PAD), lambda h, i: (h, 0, 0)),
                  pl.BlockSpec((None, T, HEAD_DIM), lambda h, i: (h, 0, 0))],
        out_specs=[pl.BlockSpec((t, HEAD_DIM), lambda h, i: (i, h)),
                   pl.BlockSpec((None, t, HEAD_DIM), lambda h, i: (h, i, 0))],
        out_shape=[jax.ShapeDtypeStruct((T, HEADS * HEAD_DIM), F32), jax.ShapeDtypeStruct((HEADS, T, HEAD_DIM), F32)],
        exchange=exchange)


def _attn_bwd(q, k, v, dmixcat, o, lse, exchange=None):
    _, T, _ = q.shape
    t = min(ATT_TILE, T)
    nq = T // t

    def body(q_ref, k_ref, v_ref, do_ref, o_ref, lse_ref, dq_ref, dk_ref, dv_ref, delta_ref):
        j = pl.program_id(1)

        @pl.when(j == 0)
        def _():
            dq_ref[...] = jnp.zeros_like(dq_ref)

            def fill(i, carry):
                rs = pl.ds(pl.multiple_of(i * t, t), t)
                delta_ref[rs, :] = jnp.broadcast_to(
                    jnp.sum(do_ref[rs, :] * o_ref[rs, :], axis=-1, keepdims=True), (t, HEAD_DIM))
                return carry

            lax.fori_loop(0, nq, fill, 0)

        kb, vb = k_ref[...], v_ref[...]

        def steps(blocks, carry):
            dk, dv = carry
            rs = [pl.ds(pl.multiple_of(i * t, t), t) for i, _ in blocks]
            qb = [q_ref[r, :] for r in rs]
            dob = [do_ref[r, :].astype(BF16) for r in rs]
            s = [_dot(b, kb, NT) for b in qb]
            dp = [_dot(b, vb, NT) for b in dob]
            for n, (_, masked) in enumerate(blocks):
                p = jnp.exp2(s[n] - lse_ref[rs[n], 0:1])
                if masked:
                    p = jnp.where(_diag_mask(t), p, 0.0)
                ds = (p * (dp[n] - delta_ref[rs[n], 0:1]) * LN2).astype(BF16)
                dq_ref[rs[n], :] += _dot(ds, kb)
                dk = dk + _dot(ds, qb[n], TN)
                dv = dv + _dot(p.astype(BF16), dob[n], TN)
            return dk, dv

        zero = (jnp.zeros((t, QK_PAD), F32), jnp.zeros((t, HEAD_DIM), F32))
        rest = nq - 1 - j
        carry = lax.cond(rest % 2 == 1, lambda c: steps([(j, True), (j + 1, False)], c),
                         lambda c: steps([(j, True)], c), zero)
        first = j + 1 + rest % 2
        dk, dv = lax.fori_loop(0, rest // 2, lambda n, c: steps([(first + 2 * n, False), (first + 2 * n + 1, False)], c),
                               carry)
        dk_ref[...] = dk
        dv_ref[...] = dv

    return _call(
        body, "attn_bwd", (q, k, v, dmixcat, o, lse), grid=(HEADS, nq),
        in_specs=[pl.BlockSpec((None, T, QK_PAD), lambda h, j: (h, 0, 0)),
                  pl.BlockSpec((None, t, QK_PAD), lambda h, j: (h, j, 0)),
                  pl.BlockSpec((None, t, HEAD_DIM), lambda h, j: (h, j, 0)),
                  pl.BlockSpec((T, HEAD_DIM), lambda h, j: (0, HEADS + h)),
                  pl.BlockSpec((T, HEAD_DIM), lambda h, j: (0, h)),
                  pl.BlockSpec((None, T, HEAD_DIM), lambda h, j: (h, 0, 0))],
        out_specs=[pl.BlockSpec((None, T, QK_PAD), lambda h, j: (h, 0, 0)),
                   pl.BlockSpec((None, t, QK_PAD), lambda h, j: (h, j, 0)),
                   pl.BlockSpec((None, t, HEAD_DIM), lambda h, j: (h, j, 0))],
        out_shape=[jax.ShapeDtypeStruct((HEADS, T, QK_PAD), F32), jax.ShapeDtypeStruct((HEADS, T, QK_PAD), F32),
                   jax.ShapeDtypeStruct((HEADS, T, HEAD_DIM), F32)],
        scratch_shapes=[pltpu.VMEM((T, HEAD_DIM), F32)], exchange=exchange)


def _ln_fwd(r):
    mu = _lanemean(r)
    xc = r - mu
    rstd = lax.rsqrt(_lanemean(xc * xc) + LN_EPS)
    return xc * rstd, rstd


def _ln_bwd(dxh, xhat, rstd):
    return rstd * (dxh - _lanemean(dxh) - xhat * _lanemean(dxh * xhat))


def _mix_ln1(o_hg, o_mla, w_out, x, g_a, ln1_g, ln1_b, sc_m, sh_m, exchange=None):
    T = x.shape[0]
    tm = min(ROW_TILE_SMALL, T)
    half = o_hg.shape[1]

    def body(hg_ref, mla_ref, w_ref, x_ref, ga_ref, g_ref, b_ref, sc_ref, sh_ref, mix_ref, xhat_ref, rstd_ref, u2_ref):
        mix = _dot(hg_ref[...], w_ref[0:half, :]) + _bdot(mla_ref[...], w_ref[half:, :])
        mix_ref[...] = mix
        xhat, rstd = _ln_fwd(ALPHA * x_ref[...] + (1.0 + ga_ref[...]) * mix)
        xhat_ref[...] = xhat
        rstd_ref[...] = jnp.broadcast_to(rstd, (tm, 128))
        u2_ref[...] = _modulate(xhat * g_ref[...] + b_ref[...], sc_ref[...], sh_ref[...]).astype(BF16)

    row = pl.BlockSpec((tm, D_MODEL), lambda i: (i, 0))
    vec = _full((1, D_MODEL))
    halfrow = pl.BlockSpec((tm, half), lambda i: (i, 0))
    return _call(
        body, "mix_ln1", (o_hg, o_mla, w_out, x, g_a, ln1_g, ln1_b, sc_m, sh_m), grid=(T // tm,),
        in_specs=[halfrow, halfrow, _full(w_out.shape), row, vec, vec, vec, vec, vec],
        out_specs=[row, row, pl.BlockSpec((tm, 128), lambda i: (i, 0)), row],
        out_shape=[jax.ShapeDtypeStruct((T, D_MODEL), F32), jax.ShapeDtypeStruct((T, D_MODEL), F32),
                   jax.ShapeDtypeStruct((T, 128), F32), jax.ShapeDtypeStruct((T, D_MODEL), BF16)],
        exchange=exchange)


def _mlp_fwd(u2, w1, w2, xhat1, ln1_g, ln1_b, g_m, ln2_g, ln2_b, target):
    T = u2.shape[0]
    tf = w1.shape[-1]
    nf = N_DEV // MLP_SLABS
    tm = min(ROW_TILE, T)

    def body(u2_ref, w1_ref, w2_ref, xhat_ref, g1_ref, b1_ref, gm_ref, g2_ref, b2_ref, tgt_ref,
             r_ref, dr2_ref, dh_ref, dg2_ref, db2_ref, dgm_ref, loss_ref, acc_ref):
        i, f = pl.program_id(0), pl.program_id(1)

        @pl.when((i == 0) & (f == 0))
        def _():
            for ref in (dg2_ref, db2_ref, dgm_ref, loss_ref):
                ref[...] = jnp.zeros_like(ref)

        @pl.when(f == 0)
        def _():
            acc_ref[...] = jnp.zeros_like(acc_ref)

        u2t = u2_ref[...]
        part = None
        for s in range(MLP_SLABS):
            r = jnp.maximum(_dot(u2t, w1_ref[s]), 0.0)
            r_ref[:, s * tf:(s + 1) * tf] = r.astype(BF16)
            d = _bdot(r * r, w2_ref[s])
            part = d if part is None else part + d
        acc_ref[...] += part

        @pl.when(f == nf - 1)
        def _():
            h = acc_ref[...]
            x1 = xhat_ref[...] * g1_ref[...] + b1_ref[...]
            xhat2, rstd2 = _ln_fwd(ALPHA * x1 + (1.0 + gm_ref[...]) * h)
            err = xhat2 * g2_ref[...] + b2_ref[...] - tgt_ref[...]
            loss_ref[...] += jnp.sum(0.5 * _lanemean(err * err), axis=0, keepdims=True)
            dy = err * (1.0 / D_MODEL)
            dg2_ref[...] += _rowsum(dy * xhat2)
            db2_ref[...] += _rowsum(dy)
            dr2 = _ln_bwd(dy * g2_ref[...], xhat2, rstd2)
            dr2_ref[...] = dr2
            dgm_ref[...] += _rowsum(dr2 * h)
            dh_ref[...] = ((1.0 + gm_ref[...]) * dr2).astype(BF16)

    row = pl.BlockSpec((tm, D_MODEL), lambda i, f: (i, 0))
    vec = _full((1, D_MODEL))
    return pl.pallas_call(
        body, name="mlp_fwd", grid=(T // tm, nf),
        in_specs=[row, pl.BlockSpec((MLP_SLABS, D_MODEL, tf), lambda i, f: (f, 0, 0)),
                  pl.BlockSpec((MLP_SLABS, tf, D_MODEL), lambda i, f: (f, 0, 0)),
                  row, vec, vec, vec, vec, vec, row],
        out_specs=[pl.BlockSpec((tm, MLP_SLABS * tf), lambda i, f: (i, f)), row, row, vec, vec, vec, _full((1, 128))],
        out_shape=[jax.ShapeDtypeStruct((T, N_DEV * tf), BF16), jax.ShapeDtypeStruct((T, D_MODEL), F32),
                   jax.ShapeDtypeStruct((T, D_MODEL), BF16), jax.ShapeDtypeStruct((1, D_MODEL), F32),
                   jax.ShapeDtypeStruct((1, D_MODEL), F32), jax.ShapeDtypeStruct((1, D_MODEL), F32),
                   jax.ShapeDtypeStruct((1, 128), F32)],
        scratch_shapes=[pltpu.VMEM((tm, D_MODEL), F32)],
        compiler_params=_params(),
    )(u2, w1, w2, xhat1, ln1_g, ln1_b, g_m, ln2_g, ln2_b, target)


def _mlp_bwd(dh, w1, w2, r, dr2, xhat1, rstd1, mix, ln1_g, ln1_b, sc_m, g_a):
    T = dh.shape[0]
    tf = w1.shape[-1]
    nf = N_DEV // MLP_SLABS
    tm = min(ROW_TILE, T)

    def body(dh_ref, w1_ref, w2_ref, r_ref, dr2_ref, xhat_ref, rstd_ref, mix_ref, g1_ref, b1_ref, sc_ref, ga_ref,
             dhpre_ref, dr1_ref, dmix_ref, dsc_ref, dsh_ref, dg1_ref, db1_ref, dga_ref, acc_ref):
        i, f = pl.program_id(0), pl.program_id(1)

        @pl.when((i == 0) & (f == 0))
        def _():
            for ref in (dsc_ref, dsh_ref, dg1_ref, db1_ref, dga_ref):
                ref[...] = jnp.zeros_like(ref)

        @pl.when(f == 0)
        def _():
            acc_ref[...] = jnp.zeros_like(acc_ref)

        dht = dh_ref[...]
        part = None
        for s in range(MLP_SLABS):
            cols = slice(s * tf, (s + 1) * tf)
            dhpre = (_dot(dht, w2_ref[s], NT) * (2.0 * r_ref[:, cols].astype(F32))).astype(BF16)
            dhpre_ref[:, cols] = dhpre
            d = _dot(dhpre, w1_ref[s], NT)
            part = d if part is None else part + d
        acc_ref[...] += part

        @pl.when(f == nf - 1)
        def _():
            du2 = acc_ref[...]
            xhat = xhat_ref[...]
            x1 = xhat * g1_ref[...] + b1_ref[...]
            dx1 = ALPHA * dr2_ref[...] + du2 * (1.0 + sc_ref[...])
            dsc_ref[...] += _rowsum(du2 * x1)
            dsh_ref[...] += _rowsum(du2)
            dg1_ref[...] += _rowsum(dx1 * xhat)
            db1_ref[...] += _rowsum(dx1)
            dr1 = _ln_bwd(dx1 * g1_ref[...], xhat, rstd_ref[:, 0:1])
            dr1_ref[...] = dr1
            dga_ref[...] += _rowsum(dr1 * mix_ref[...])
            dmix_ref[...] = ((1.0 + ga_ref[...]) * dr1).astype(BF16)

    row = pl.BlockSpec((tm, D_MODEL), lambda i, f: (i, 0))
    vec = _full((1, D_MODEL))
    return pl.pallas_call(
        body, name="mlp_bwd", grid=(T // tm, nf),
        in_specs=[row, pl.BlockSpec((MLP_SLABS, D_MODEL, tf), lambda i, f: (f, 0, 0)),
                  pl.BlockSpec((MLP_SLABS, tf, D_MODEL), lambda i, f: (f, 0, 0)),
                  pl.BlockSpec((tm, MLP_SLABS * tf), lambda i, f: (i, f)), row, row,
                  pl.BlockSpec((tm, 128), lambda i, f: (i, 0)), row, vec, vec, vec, vec],
        out_specs=[pl.BlockSpec((tm, MLP_SLABS * tf), lambda i, f: (i, f)), row, row, vec, vec, vec, vec, vec],
        out_shape=[jax.ShapeDtypeStruct((T, N_DEV * tf), BF16), jax.ShapeDtypeStruct((T, D_MODEL), F32),
                   jax.ShapeDtypeStruct((T, D_MODEL), BF16)] + [jax.ShapeDtypeStruct((1, D_MODEL), F32)] * 5,
        scratch_shapes=[pltpu.VMEM((tm, D_MODEL), F32)],
        compiler_params=_params(),
    )(dh, w1, w2, r, dr2, xhat1, rstd1, mix, ln1_g, ln1_b, sc_m, g_a)


def _input_bwd(dz_h, dz_m, w_in_ext, x, dr1, sc_a, exchange=None):
    T = x.shape[0]
    tm = min(ROW_TILE_SMALL, T)

    def body(dzh_ref, dzm_ref, w_ref, x_ref, dr1_ref, sc_ref, gx_ref, dsc_ref, dsh_ref):
        @pl.when(pl.program_id(0) == 0)
        def _():
            dsc_ref[...] = jnp.zeros_like(dsc_ref)
            dsh_ref[...] = jnp.zeros_like(dsh_ref)

        du = _bdot(dzh_ref[...], w_ref[0:2048, :]) + _bdot(dzm_ref[...], w_ref[2048:3072, :])
        gx_ref[...] = ALPHA * dr1_ref[...] + du * (1.0 + sc_ref[...])
        dsc_ref[...] += _rowsum(du * x_ref[...])
        dsh_ref[...] += _rowsum(du)

    row = pl.BlockSpec((tm, D_MODEL), lambda i: (i, 0))
    vec = _full((1, D_MODEL))
    return _call(
        body, "input_bwd", (dz_h, dz_m, w_in_ext, x, dr1, sc_a), grid=(T // tm,),
        in_specs=[pl.BlockSpec((tm, 2048), lambda i: (i, 0)), row, _full(w_in_ext.shape), row, row, vec],
        out_specs=[row, vec, vec],
        out_shape=[jax.ShapeDtypeStruct((T, D_MODEL), F32), jax.ShapeDtypeStruct((1, D_MODEL), F32),
                   jax.ShapeDtypeStruct((1, D_MODEL), F32)], exchange=exchange)


def _adam_math(w, g, m, v):
    m = ADAM_B1 * m + (1.0 - ADAM_B1) * g
    v = ADAM_B2 * v + (1.0 - ADAM_B2) * (g * g)
    m_hat = m / (1.0 - ADAM_B1 ** ADAM_STEP)
    v_hat = v / (1.0 - ADAM_B2 ** ADAM_STEP)
    return -ADAM_LR * (m_hat / (jnp.sqrt(v_hat) + ADAM_EPS) + ADAM_WD * w), m, v


def _adam(g_slabs, w, m, v, name, g_fn=None, g_extra=()):
    R, C = w.shape
    tr = 256 if R % 256 == 0 else R
    ns = 0 if g_slabs is None else g_slabs.shape[0]
    ne = len(g_extra)

    def body(*refs):
        e_refs = refs[:ne]
        refs = refs[ne:]
        if ns:
            gs_ref, refs = refs[0], refs[1:]
        w_ref, m_ref, v_ref, g_ref, d_ref, nm_ref, nv_ref = refs
        if g_fn is not None:
            g = g_fn(*e_refs)
        else:
            g = gs_ref[0].astype(F32)
            for s in range(1, ns):
                g = g + gs_ref[s].astype(F32)
        d, nm, nv = _adam_math(w_ref[...], g, m_ref[...], v_ref[...])
        g_ref[...] = g
        d_ref[...] = d
        nm_ref[...] = nm
        nv_ref[...] = nv

    blk = pl.BlockSpec((tr, C), lambda i: (i, 0))
    in_specs = [pl.BlockSpec((tr, e.shape[1]), lambda i: (i, 0)) if e.shape[0] == R else _full(e.shape) for e in g_extra]
    args = list(g_extra)
    if ns:
        in_specs.append(pl.BlockSpec((ns, tr, C), lambda i: (0, i, 0)))
        args.append(g_slabs)
    return pl.pallas_call(
        body, name=name, grid=(R // tr,), in_specs=in_specs + [blk] * 3, out_specs=[blk] * 4,
        out_shape=[jax.ShapeDtypeStruct((R, C), F32)] * 4, compiler_params=_params(),
    )(*args, w, m, v)


def _adam_small(small_all, params):
    n = len(params)

    def body(*refs):
        s_ref, refs = refs[0], refs[1:]
        wmv, loss_ref, outs = refs[:3 * n], refs[3 * n], refs[3 * n + 1:]
        tot = s_ref[0]
        for i in range(1, N_DEV):
            tot = tot + s_ref[i]
        loss_ref[...] = tot[:, SMALL_W - 128:]
        for j, (w, _, _, off) in enumerate(params):
            w_ref, m_ref, v_ref = wmv[3 * j:3 * j + 3]
            g_ref, d_ref, nm_ref, nv_ref = outs[4 * j:4 * j + 4]
            if w.shape[0] == 2:
                lb = _lower_bound(w_ref)
                g0 = tot[:, off:off + w.shape[1]] * lb * (1.0 - lb)
                rows = [(slice(0, 1), g0), (slice(1, 2), -g0)]
            else:
                rows = [(slice(0, 1), tot[:, off:off + w.shape[1]])]
            for rs, g in rows:
                d, nm, nv = _adam_math(w_ref[rs, :], g, m_ref[rs, :], v_ref[rs, :])
                g_ref[rs, :], d_ref[rs, :], nm_ref[rs, :], nv_ref[rs, :] = g, d, nm, nv

    out_shape = [jax.ShapeDtypeStruct((1, 128), F32)]
    for w, _, _, _ in params:
        out_shape += [jax.ShapeDtypeStruct(w.shape, F32)] * 4
    res = pl.pallas_call(body, name="adam_small", out_shape=out_shape, compiler_params=_params())(
        small_all, *[a for w, m, v, _ in params for a in (w, m, v)])
    return res[0], [tuple(res[1 + 4 * j:5 + 4 * j]) for j in range(n)]


def _cols_from_slabs(g):
    s, r, c = g.shape
    return jnp.transpose(g, (1, 0, 2)).reshape(r, s * c)


def _slabs_from_cols(w):
    r, c = w.shape
    return jnp.transpose(w.reshape(r, N_DEV, c // N_DEV), (1, 0, 2))


def _rot_half_rows(wt):
    return jnp.concatenate([-wt[32:], wt[:32]], axis=0)


def _unrot_half_rows(dwt_rot):
    return jnp.concatenate([dwt_rot[32:], -dwt_rot[:32]], axis=0)


def _ext_in_t(g):
    k_in = g.shape[2]
    z64, z128 = jnp.zeros((64, k_in), BF16), jnp.zeros((128, k_in), BF16)
    last = g.shape[1] - ROPE_DIM
    wk = g[N_DEV - 1, last:]
    return jnp.concatenate([g[i] for i in range(N_DEV - 1)] + [g[N_DEV - 1, :last], z128, wk, z64, z128,
                                                               _rot_half_rows(wk), z64], axis=0)


def _ext_q_t(wt):
    r = wt.shape[1]
    z64, z128 = jnp.zeros((64, r), BF16), jnp.zeros((128, r), BF16)
    per = HEAD_DIM + ROPE_DIM
    main = [jnp.concatenate([wt[per * h:per * (h + 1)], z64], axis=0) for h in range(HEADS)]
    rot = [jnp.concatenate([z128, _rot_half_rows(wt[per * h + HEAD_DIM:per * (h + 1)]), z64], axis=0)
           for h in range(HEADS)]
    return jnp.concatenate(main + rot, axis=0)


def _ext_kv(w_kv_up):
    r = w_kv_up.shape[0]
    z128 = jnp.zeros((r, 128), BF16)
    wkv = w_kv_up.reshape(r, HEADS, 2 * HEAD_DIM)
    kpad = [jnp.concatenate([wkv[:, h, :HEAD_DIM], z128], axis=1) for h in range(HEADS)]
    vals = [wkv[:, h, HEAD_DIM:] for h in range(HEADS)]
    return jnp.concatenate(kpad + vals, axis=1)


def _grad_in_from_ext_t(dwt_h, dwt_m):
    dwk = dwt_m[512 + 128:512 + 192] + _unrot_half_rows(dwt_m[768 + 128:768 + 192])
    return jnp.concatenate([dwt_h, dwt_m[:512], dwk], axis=0)


def _grad_q_from_ext_t(dwq_ext_t):
    rows = []
    for h in range(HEADS):
        main, rot = dwq_ext_t[256 * h:256 * h + 256], dwq_ext_t[1024 + 256 * h:1280 + 256 * h]
        rows += [main[:128], main[128:192] + _unrot_half_rows(rot[128:192])]
    return jnp.concatenate(rows, axis=0)


def _grad_kv_from_ext(dwkv_ext):
    kvcols = []
    for h in range(HEADS):
        kvcols += [dwkv_ext[:, 256 * h:256 * h + 128], dwkv_ext[:, 1024 + 128 * h:1152 + 128 * h]]
    return jnp.concatenate(kvcols, axis=1)


SMALL_W = 6144 + 512 + 512 + 256 + 256 + 4 * 1024 + 128


def kernel(x, c, positions, w_ada, b_ada, w_in, hg_lower_bounds, hg_norm_w, mla_q_norm_w, w_q_up, mla_kv_norm_w, w_kv_up, w_out, ln1_g, ln1_b, w_mlp_in, w_mlp_out, ln2_g, ln2_b, loss_target, m_w_ada, m_b_ada, m_w_in, m_hg_lower_bounds, m_hg_norm_w, m_mla_q_norm_w, m_w_q_up, m_mla_kv_norm_w, m_w_kv_up, m_w_out, m_ln1_g, m_ln1_b, m_w_mlp_in, m_w_mlp_out, m_ln2_g, m_ln2_b, v_w_ada, v_b_ada, v_w_in, v_hg_lower_bounds, v_hg_norm_w, v_mla_q_norm_w, v_w_q_up, v_mla_kv_norm_w, v_w_kv_up, v_w_out, v_ln1_g, v_ln1_b, v_w_mlp_in, v_w_mlp_out, v_ln2_g, v_ln2_b):
    T = x.shape[1]
    me = 4 * lax.axis_index("x") + 2 * lax.axis_index("y") + lax.axis_index("c")
    xs, tgt = x[0], loss_target[0]
    transposed = ("w_in", "w_q_up")
    as_used = lambda n, a: a[0].T if n in transposed else a[0]
    big = {n: as_used(n, a) for n, a in dict(w_in=w_in, w_q_up=w_q_up, w_kv_up=w_kv_up, w_out=w_out,
                                              w_mlp_in=w_mlp_in, w_mlp_out=w_mlp_out).items()}
    names = list(big)

    bf = {n: big[n].astype(BF16) for n in names}
    g_in, g_c = _gather_two_level([bf["w_in"], c], name="gather_w_in")
    c_all = g_c.reshape(N_DEV, D_MODEL)

    ada_cols = w_ada.shape[2]
    mod_part, cond = _mod_part(c_all, w_ada[0], lax.dynamic_slice(b_ada, (0, me * ada_cols), (1, ada_cols)))
    (mod_all,) = _exchange([mod_part], scatter=False, name="gather_mod")
    mod_row = lax.dynamic_slice(mod_all, (0, me, 0), (N_DEV, 1, ada_cols)).reshape(1, N_DEV * ada_cols)
    sh_a, sc_a, g_a, sh_m, sc_m, g_m = [mod_row[:, D_MODEL * i:D_MODEL * (i + 1)] for i in range(6)]

    w_in_ext = _ext_in_t(g_in)
    z, (g_q, g_kv, g_out) = _matmul(xs, w_in_ext, "NT", "in_proj", a_fn=_modulate, extras=(sc_a, sh_a), tn=3072,
                                    exchange=_Exchange([bf["w_q_up"], bf["w_kv_up"], bf["w_out"]], False))
    wq_ext = _ext_q_t(g_q.reshape(N_DEV * g_q.shape[1], g_q.shape[2]))
    wkv_ext = _ext_kv(_cols_from_slabs(g_kv))
    w_out_full = g_out.reshape(D_MODEL, D_MODEL)
    inv_freq = 1.0 / (ROPE_THETA ** (jnp.arange(0, ROPE_DIM, 2, dtype=F32) / ROPE_DIM))
    zeros = lambda n: jnp.zeros((n,), F32)
    invf = jnp.concatenate([zeros(128), inv_freq, inv_freq, zeros(64)]).reshape(1, QK_PAD)
    m_one = jnp.concatenate([jnp.ones((128,), F32), zeros(128)]).reshape(1, QK_PAD)
    m_rot = jnp.concatenate([zeros(128), jnp.ones((64,), F32), zeros(64)]).reshape(1, QK_PAD)
    half = D_MODEL // 2
    (q, k, v, c1, s1, cqn, ckvn), (w1_top,) = _mla_pre(
        z, positions.reshape(T, 1), invf, m_one, m_rot, wq_ext, wkv_ext, mla_q_norm_w, mla_kv_norm_w,
        exchange=_StagedGather(bf["w_mlp_in"][:half], rows=(0, D_MODEL)))
    (o_raw, o_gated, s_prev), (w1,) = _hgrn_fwd(
        z, hg_lower_bounds, hg_norm_w,
        exchange=_StagedGather(bf["w_mlp_in"][half:], rows=(half, D_MODEL), into=w1_top))
    (o_mla, lse), (w2,) = _attn_fwd(q, k, v, exchange=_StagedGather(bf["w_mlp_out"]))
    mix, xhat1, rstd1, u2 = _mix_ln1(o_gated, o_mla, w_out_full, xs, g_a, ln1_g, ln1_b, sc_m, sh_m)[0]
    r, dr2, dh, dln2_g, dln2_b, dg_m, loss_part = _mlp_fwd(u2, w1, w2, xhat1, ln1_g, ln1_b, g_m, ln2_g, ln2_b, tgt)

    dhpre, dr1, dmix, dsc_m, dsh_m, dln1_g, dln1_b, dg_a = _mlp_bwd(dh, w1, w2, r, dr2, xhat1, rstd1, mix, ln1_g,
                                                                    ln1_b, sc_m, g_a)
    received = {}
    dw2 = _matmul(r, dh, "TN", "wgrad_mlp_out", out_dtype=BF16, a_fn=_square, tm=1024)
    dw1 = _matmul(u2, dhpre, "TN", "wgrad_mlp_in", out_dtype=BF16, tm=1024, out_slabs=N_DEV)
    dmixcat = _matmul(dmix, w_out_full, "NT", "dgrad_out")
    dw_out = jnp.concatenate([_matmul(o_gated, dmix, "TN", "wgrad_out_hg", out_dtype=BF16),
                              _matmul(o_mla, dmix, "TN", "wgrad_out_mla", out_dtype=BF16)], axis=0)
    (dz_h, dlb, dnw), (received["w_out"],) = _hgrn_bwd(
        dmixcat, z, o_raw, s_prev, hg_lower_bounds, hg_norm_w,
        exchange=_Exchange([dw_out.reshape(N_DEV, D_MODEL // N_DEV, D_MODEL)], True))
    (dq, dk, dv), (received["w_mlp_in"], received["w_mlp_out"]) = _attn_bwd(
        q, k, v, dmixcat, o_mla, lse,
        exchange=_Exchange([dw1, dw2.reshape(N_DEV, dw2.shape[0] // N_DEV, D_MODEL)], True))
    dz_m, dq_ext, dkv_ext, dqnw, dkvnw = _mla_bwd(dq, dk, dv, z, c1, s1, wq_ext, wkv_ext, mla_q_norm_w,
                                                   mla_kv_norm_w)
    dwq_t = _grad_q_from_ext_t(_matmul(dq_ext, cqn, "TN", "wgrad_q_up", tm=1024))
    dwkv = _grad_kv_from_ext(_matmul(ckvn, dkv_ext, "TN", "wgrad_kv_up", tn=1536))
    qkv_slabs = [dwq_t.reshape((N_DEV, dwq_t.shape[0] // N_DEV, dwq_t.shape[1])).astype(BF16),
                 _slabs_from_cols(dwkv).astype(BF16)]
    dwt_h, (received["w_q_up"], received["w_kv_up"]) = _matmul(
        dz_h, xs, "TN", "wgrad_in_h", b_fn=_modulate, extras=(sc_a, sh_a), tm=1024,
        exchange=_Exchange(qkv_slabs, True))
    dwt_m = _matmul(dz_m, xs, "TN", "wgrad_in_m", b_fn=_modulate, extras=(sc_a, sh_a), tm=1024)
    dw_in_t = _grad_in_from_ext_t(dwt_h, dwt_m)
    in_slabs = dw_in_t.reshape((N_DEV, dw_in_t.shape[0] // N_DEV, dw_in_t.shape[1])).astype(BF16)
    (grad_x, dsc_a, dsh_a), (received["w_in"],) = _input_bwd(
        dz_h, dz_m, w_in_ext, xs, dr1, sc_a, exchange=_Exchange([in_slabs], True))

    small = jnp.concatenate([dsh_a, dsc_a, dg_a, dsh_m, dsc_m, dg_m, dlb, dnw, dqnw, dkvnw, dln1_g, dln1_b, dln2_g,
                             dln2_b, loss_part], axis=1)
    (small_all,) = _exchange([small], scatter=False, name="gather_small")

    moments = dict(w_in=(m_w_in, v_w_in), w_q_up=(m_w_q_up, v_w_q_up), w_kv_up=(m_w_kv_up, v_w_kv_up),
                   w_out=(m_w_out, v_w_out), w_mlp_in=(m_w_mlp_in, v_w_mlp_in), w_mlp_out=(m_w_mlp_out, v_w_mlp_out))
    res = {}
    for n in names:
        res[n] = _adam(received[n], big[n], as_used(n, moments[n][0]), as_used(n, moments[n][1]), name="adam_" + n)
    dmod_cols = lax.dynamic_slice(small_all.reshape(N_DEV, SMALL_W), (0, me * ada_cols), (N_DEV, ada_cols))
    cond_t = cond.T

    def ada_grad(ct_ref, dm_ref):
        g = ct_ref[:, 0:1] * dm_ref[0:1, :]
        for b in range(1, N_DEV):
            g = g + ct_ref[:, b:b + 1] * dm_ref[b:b + 1, :]
        return g

    res["w_ada"] = _adam(None, w_ada[0], m_w_ada[0], v_w_ada[0], name="adam_w_ada", g_fn=ada_grad,
                         g_extra=(cond_t, dmod_cols))

    small_params = [("b_ada", b_ada, m_b_ada, v_b_ada, 0),
                    ("hg_lower_bounds", hg_lower_bounds, m_hg_lower_bounds, v_hg_lower_bounds, 6144),
                    ("hg_norm_w", hg_norm_w, m_hg_norm_w, v_hg_norm_w, 6656),
                    ("mla_q_norm_w", mla_q_norm_w, m_mla_q_norm_w, v_mla_q_norm_w, 7168),
                    ("mla_kv_norm_w", mla_kv_norm_w, m_mla_kv_norm_w, v_mla_kv_norm_w, 7424),
                    ("ln1_g", ln1_g, m_ln1_g, v_ln1_g, 7680), ("ln1_b", ln1_b, m_ln1_b, v_ln1_b, 8704),
                    ("ln2_g", ln2_g, m_ln2_g, v_ln2_g, 9728), ("ln2_b", ln2_b, m_ln2_b, v_ln2_b, 10752)]
    loss_row, small_res = _adam_small(small_all, [p[1:] for p in small_params])
    for p, r4 in zip(small_params, small_res):
        res[p[0]] = r4
    loss = loss_row[0, 0]

    order = ["w_ada", "b_ada", "w_in", "hg_lower_bounds", "hg_norm_w", "mla_q_norm_w", "w_q_up", "mla_kv_norm_w",
             "w_kv_up", "w_out", "ln1_g", "ln1_b", "w_mlp_in", "w_mlp_out", "ln2_g", "ln2_b"]
    def as_given(n, a):
        if n in transposed:
            a = a.T
        return a[None] if n in big or n == "w_ada" else a

    shaped = {n: tuple(as_given(n, a) for a in res[n]) for n in order}
    outs = [loss, grad_x.reshape(1, T, D_MODEL)]
    for i in range(4):
        outs += [shaped[n][i] for n in order]
    return tuple(outs)
```

```python
import functools

import jax
import jax.numpy as jnp
import numpy as np
from jax import lax
from jax.experimental import pallas as pl
from jax.experimental.pallas import tpu as pltpu

F32, BF16 = jnp.float32, jnp.bfloat16
N_DEV = 8
D_MODEL = 1024
HEADS = 4
HEAD_DIM = 128
ROPE_DIM = 64
QK_PAD = 256
CHUNK = 64
ROPE_THETA = 10000.0
RMS_EPS = 1e-6
LN_EPS = 1e-5
ALPHA = 2.0 ** 0.25
ATT_SCALE = (HEAD_DIM + ROPE_DIM) ** -0.5
LN2 = float(np.log(2.0))
Q_PRESCALE = ATT_SCALE / LN2
ADAM_LR, ADAM_B1, ADAM_B2, ADAM_EPS, ADAM_WD, ADAM_STEP = 0.001, 0.9, 0.999, 1e-08, 0.01, 10
NEG_BIG = -1e30

ROW_TILE = 512
ROW_TILE_SMALL = 256
ATT_TILE = 512
HGRN_GROUP = 8
MLP_SLABS = 4
VMEM_LIMIT = 56 * 2 ** 20

NN = (((1,), (0,)), ((), ()))
NT = (((1,), (1,)), ((), ()))
TN = (((0,), (0,)), ((), ()))


def _dot(a, b, dims=NN):
    return lax.dot_general(a, b, dims, preferred_element_type=F32)


def _bdot(a, b, dims=NN):
    return lax.dot_general(a.astype(BF16), b.astype(BF16), dims, preferred_element_type=F32)


def _hdot(a, b, dims=NN):
    return lax.dot_general(a, b, dims, precision=lax.Precision.HIGHEST, preferred_element_type=F32)


def _params():
    return pltpu.CompilerParams(vmem_limit_bytes=VMEM_LIMIT)


def _sigmoid(x):
    return 1.0 / (1.0 + jnp.exp(-x))


def _rowsum(x):
    return jnp.sum(x, axis=0, keepdims=True)


def _lanemean(x):
    return jnp.mean(x, axis=-1, keepdims=True)


def _full(shape):
    nd = len(shape)
    return pl.BlockSpec(shape, lambda *_: (0,) * nd)


class _Exchange:
    def __init__(self, arrs, scatter):
        self.arrs, self.scatter, self.n, self.aliases = list(arrs), scatter, len(arrs), []
        self.out_shape = [jax.ShapeDtypeStruct((N_DEV,) + (a.shape[1:] if scatter else a.shape), a.dtype)
                          for a in self.arrs]
        n = self.n
        self.scratch = [pltpu.SemaphoreType.DMA((n, N_DEV - 1)), pltpu.SemaphoreType.DMA((n, N_DEV - 1)),
                        pltpu.SemaphoreType.DMA((n,))]

    def _copies(self, ins, outs, sems):
        send_sems, recv_sems, loc_sems = sems
        x, y, c = lax.axis_index("x"), lax.axis_index("y"), lax.axis_index("c")
        me = 4 * x + 2 * y + c
        copies = []
        for k in range(self.n):
            src_of = (lambda i, k=k: ins[k].at[i]) if self.scatter else (lambda i, k=k: ins[k])
            copies.append((pltpu.make_async_copy(src_of(me), outs[k].at[me], loc_sems.at[k]), None))
            for p in range(1, N_DEV):
                px = (1 - x) if p & 4 else x
                py = (1 - y) if p & 2 else y
                pc = (1 - c) if p & 1 else c
                peer = 4 * px + 2 * py + pc
                both = dict(send_sem=send_sems.at[k, p - 1], recv_sem=recv_sems.at[k, p - 1],
                            device_id=(px, py, pc), device_id_type=pl.DeviceIdType.MESH)
                send = pltpu.make_async_remote_copy(src_ref=src_of(peer), dst_ref=outs[k].at[me], **both)
                recv = pltpu.make_async_remote_copy(src_ref=src_of(peer), dst_ref=outs[k].at[peer], **both)
                copies.append((send, recv))
        return copies

    def start(self, ins, outs, sems):
        for first, _ in self._copies(ins, outs, sems):
            first.start()

    def middle(self, ins, outs, sems):
        pass

    def wait(self, ins, outs, sems):
        for first, recv in self._copies(ins, outs, sems):
            if recv is None:
                first.wait()
            else:
                recv.wait_recv()
                first.wait_send()


class _StagedGather:
    def __init__(self, arr):
        self.arrs, self.aliases = [arr], []
        self.out_shape = [jax.ShapeDtypeStruct((N_DEV,) + arr.shape, arr.dtype)]
        self.scratch = [pltpu.VMEM((N_DEV,) + arr.shape, arr.dtype), pltpu.SemaphoreType.DMA((7,)),
                        pltpu.SemaphoreType.DMA((7,)), pltpu.SemaphoreType.DMA((2,))]

    def _parts(self, scr):
        stage, send_sems, recv_sems, loc_sems = scr
        x, y, c = lax.axis_index("x"), lax.axis_index("y"), lax.axis_index("c")
        me, sibling = (x, y, c), (x, y, 1 - c)
        chips = [(1 - x, y), (x, 1 - y), (1 - x, 1 - y)]

        def copy(j, block, to):
            px, py, pc = block
            slot = stage.at[4 * px + 2 * py + pc]
            return pltpu.make_async_remote_copy(src_ref=slot, dst_ref=slot, send_sem=send_sems.at[j],
                                                recv_sem=recv_sems.at[j], device_id=to,
                                                device_id_type=pl.DeviceIdType.MESH)

        return stage, loc_sems, me, sibling, chips, c, copy

    def start(self, ins, outs, scr):
        stage, loc_sems, me, sibling, chips, c, copy = self._parts(scr)
        x, y, _ = me
        own = pltpu.make_async_copy(ins[0], stage.at[4 * x + 2 * y + c], loc_sems.at[0])
        own.start()
        own.wait()
        copy(0, me, sibling).start()
        for j, chip in enumerate(chips):
            copy(1 + j, me, (*chip, c)).start()

    def middle(self, ins, outs, scr):
        stage, loc_sems, me, sibling, chips, c, copy = self._parts(scr)
        for j, chip in enumerate(chips):
            copy(1 + j, (*chip, c), me).wait_recv()
            copy(4 + j, (*chip, c), sibling).start()

    def wait(self, ins, outs, scr):
        stage, loc_sems, me, sibling, chips, c, copy = self._parts(scr)
        copy(0, sibling, me).wait_recv()
        for j, chip in enumerate(chips):
            copy(4 + j, (*chip, 1 - c), me).wait_recv()
        copy(0, me, sibling).wait_send()
        for j, chip in enumerate(chips):
            copy(1 + j, me, (*chip, c)).wait_send()
            copy(4 + j, (*chip, c), sibling).wait_send()
        whole = pltpu.make_async_copy(stage, outs[0], loc_sems.at[1])
        whole.start()
        whole.wait()


def _call(body, name, args, out_shape, grid=(), in_specs=(), out_specs=(), scratch_shapes=(), exchange=None,
          middle_at=0.8):
    if exchange is None:
        return pl.pallas_call(body, name=name, grid=grid, in_specs=list(in_specs), out_specs=list(out_specs),
                              out_shape=list(out_shape), scratch_shapes=list(scratch_shapes),
                              compiler_params=_params())(*args), None
    exs = list(exchange) if isinstance(exchange, (list, tuple)) else [exchange]
    ni, no, ns = len(args), len(out_shape), len(scratch_shapes)
    nxi, nxo = sum(len(e.arrs) for e in exs), sum(len(e.out_shape) for e in exs)
    steps = int(np.prod(grid))
    mid_step = min(max(int(steps * middle_at), 1), steps - 1)
    aliases, iat, oat = {}, ni, no
    for e in exs:
        for src, dst in e.aliases:
            aliases[iat + src] = oat + dst
        iat, oat = iat + len(e.arrs), oat + len(e.out_shape)

    def wrapped(*refs):
        a, xi = refs[:ni], refs[ni:ni + nxi]
        o, xo = refs[ni + nxi:ni + nxi + no], refs[ni + nxi + no:ni + nxi + no + nxo]
        s, xs = refs[ni + nxi + no + nxo:ni + nxi + no + nxo + ns], refs[ni + nxi + no + nxo + ns:]
        parts, iat, oat, sat = [], 0, 0, 0
        for e in exs:
            parts.append((e, xi[iat:iat + len(e.arrs)], xo[oat:oat + len(e.out_shape)], xs[sat:sat + len(e.scratch)]))
            iat, oat, sat = iat + len(e.arrs), oat + len(e.out_shape), sat + len(e.scratch)
        step = 0
        for d, g in enumerate(grid):
            step = step * g + pl.program_id(d)

        @pl.when(step == 0)
        def _():
            for e, ins, outs, sems in parts:
                e.start(ins, outs, sems)

        @pl.when(step == mid_step)
        def _():
            for e, ins, outs, sems in parts:
                e.middle(ins, outs, sems)

        body(*a, *o, *s)

        @pl.when(step == steps - 1)
        def _():
            for e, ins, outs, sems in parts:
                e.wait(ins, outs, sems)

    hbm = pl.BlockSpec(memory_space=pltpu.HBM)
    res = pl.pallas_call(
        wrapped, name=name, grid=grid, in_specs=list(in_specs) + [hbm] * nxi, out_specs=list(out_specs) + [hbm] * nxo,
        out_shape=list(out_shape) + [o_ for e in exs for o_ in e.out_shape],
        scratch_shapes=list(scratch_shapes) + [s_ for e in exs for s_ in e.scratch],
        input_output_aliases=aliases, compiler_params=_params())(*args, *[a_ for e in exs for a_ in e.arrs])
    return res[:no], res[no:]


def _gather_two_level(arrs, name):
    n = len(arrs)
    out_shape = [jax.ShapeDtypeStruct((N_DEV,) + a.shape, a.dtype) for a in arrs]

    def body(*refs):
        ins, outs = refs[:n], refs[n:2 * n]
        send_sems, recv_sems, loc_sems = refs[2 * n:]
        x, y, c = lax.axis_index("x"), lax.axis_index("y"), lax.axis_index("c")
        me, sibling = (x, y, c), (x, y, 1 - c)
        chips = [(1 - x, y), (x, 1 - y), (1 - x, 1 - y)]

        def copy(k, j, block, to, src=None):
            px, py, pc = block
            dst = outs[k].at[4 * px + 2 * py + pc]
            return pltpu.make_async_remote_copy(src_ref=dst if src is None else src, dst_ref=dst,
                                                send_sem=send_sems.at[k, j], recv_sem=recv_sems.at[k, j],
                                                device_id=to, device_id_type=pl.DeviceIdType.MESH)

        mine = [pltpu.make_async_copy(ins[k], outs[k].at[4 * x + 2 * y + c], loc_sems.at[k]) for k in range(n)]
        first = []
        for k in range(n):
            mine[k].start()
            first.append(copy(k, 0, me, sibling, src=ins[k]))
            first += [copy(k, 1 + j, me, (*chip, c), src=ins[k]) for j, chip in enumerate(chips)]
        for cp in first:
            cp.start()
        passed = []
        for j, chip in enumerate(chips):
            for k in range(n):
                copy(k, 1 + j, (*chip, c), me).wait_recv()
                passed.append(copy(k, 4 + j, (*chip, c), sibling))
                passed[-1].start()
        for k in range(n):
            copy(k, 0, sibling, me).wait_recv()
            for j, chip in enumerate(chips):
                copy(k, 4 + j, (*chip, 1 - c), me).wait_recv()
        for cp in first + passed:
            cp.wait_send()
        for cp in mine:
            cp.wait()

    vmem = pl.BlockSpec(memory_space=pltpu.VMEM)
    return pl.pallas_call(body, name=name, out_shape=out_shape, in_specs=[vmem] * n, out_specs=[vmem] * n,
                          scratch_shapes=[pltpu.SemaphoreType.DMA((n, 7)), pltpu.SemaphoreType.DMA((n, 7)),
                                          pltpu.SemaphoreType.DMA((n,))], compiler_params=_params())(*arrs)


def _exchange(arrs, scatter, name):
    ex = _Exchange(arrs, scatter)

    def body(*refs):
        ins, outs, sems = refs[:ex.n], refs[ex.n:2 * ex.n], refs[2 * ex.n:]
        ex.start(ins, outs, sems)
        ex.wait(ins, outs, sems)

    hbm = pl.BlockSpec(memory_space=pltpu.HBM)
    return pl.pallas_call(body, name=name, out_shape=ex.out_shape, in_specs=[hbm] * ex.n, out_specs=[hbm] * ex.n,
                          scratch_shapes=ex.scratch)(*ex.arrs)


def _matmul(a, b, mode, name, out_dtype=F32, tm=512, tn=1024, tk=1024, a_fn=None, b_fn=None, extras=(),
            out_slabs=None, exchange=None):
    assert not (a_fn and b_fn) and not (b_fn and mode == "NT")
    if mode == "NN":
        (M, K), N = a.shape, b.shape[1]
    elif mode == "NT":
        (M, K), N = a.shape, b.shape[0]
    else:
        (K, M), N = a.shape, b.shape[1]
    if out_slabs:
        tn = N // out_slabs
    tm, tn, tk = min(tm, M), min(tn, N), min(tk, K)
    assert M % tm == 0 and N % tn == 0 and K % tk == 0, (name, M, N, K)
    nk = K // tk
    dims = {"NN": NN, "NT": NT, "TN": TN}[mode]
    ne = len(extras)

    def body(a_ref, b_ref, *rest):
        e_refs, o_ref, acc_ref = rest[:ne], rest[ne], rest[ne + 1]
        k = pl.program_id(2)

        @pl.when(k == 0)
        def _():
            acc_ref[...] = jnp.zeros_like(acc_ref)

        at, bt = a_ref[...], b_ref[...]
        if a_fn is not None:
            at = a_fn(at.astype(F32), *[e[...] for e in e_refs])
        if b_fn is not None:
            bt = b_fn(bt.astype(F32), *[e[...] for e in e_refs])
        acc_ref[...] += _bdot(at, bt, dims)

        @pl.when(k == nk - 1)
        def _():
            o_ref[...] = acc_ref[...].astype(out_dtype)

    if mode == "TN":
        a_spec = pl.BlockSpec((tk, tm), lambda i, j, k: (k, i))
        e_spec = pl.BlockSpec((1, tm), lambda i, j, k: (0, i))
    else:
        a_spec = pl.BlockSpec((tm, tk), lambda i, j, k: (i, k))
        e_spec = pl.BlockSpec((1, tk), lambda i, j, k: (0, k))
    if mode == "NT":
        b_spec = pl.BlockSpec((tn, tk), lambda i, j, k: (j, k))
    else:
        b_spec = pl.BlockSpec((tk, tn), lambda i, j, k: (k, j))
    if b_fn is not None:
        e_spec = pl.BlockSpec((1, tn), lambda i, j, k: (0, j))
    if out_slabs:
        o_shape = jax.ShapeDtypeStruct((out_slabs, M, tn), out_dtype)
        o_spec = pl.BlockSpec((None, tm, tn), lambda i, j, k: (j, i, 0))
    else:
        o_shape = jax.ShapeDtypeStruct((M, N), out_dtype)
        o_spec = pl.BlockSpec((tm, tn), lambda i, j, k: (i, j))
    (out,), got = _call(body, name, (a, b, *extras), [o_shape], grid=(M // tm, N // tn, nk),
                        in_specs=[a_spec, b_spec] + [e_spec] * ne, out_specs=[o_spec],
                        scratch_shapes=[pltpu.VMEM((tm, tn), F32)], exchange=exchange)
    return out if exchange is None else (out, got)


def _modulate(x, sc, sh):
    return x * (1.0 + sc) + sh


def _square(x):
    return x * x


def _mod_part(c_all, w_ada_s, b_s):
    def body(c_ref, w_ref, b_ref, mod_ref, cond_ref):
        cv = c_ref[...]
        cond = cv * _sigmoid(cv)
        cond_ref[...] = cond
        mod_ref[...] = _bdot(cond, w_ref[...]) + b_ref[...]

    return pl.pallas_call(
        body, name="mod_part",
        out_shape=[jax.ShapeDtypeStruct((N_DEV, w_ada_s.shape[1]), F32), jax.ShapeDtypeStruct(c_all.shape, F32)],
        compiler_params=_params(),
    )(c_all, w_ada_s, b_s)


def _rms_fwd(x, w):
    rs = lax.rsqrt(_lanemean(x * x) + RMS_EPS)
    return x * rs * w, rs


def _rms_bwd(x, rs, w, dy):
    xhat = x * rs
    dxh = dy * w
    return rs * (dxh - xhat * _lanemean(dxh * xhat)), dy * xhat


def _mla_pre(z, pos_col, invf, m_one, m_rot, wq_ext, wkv_ext, qnw, kvnw, exchange=None):
    T = z.shape[0]
    tm = min(ROW_TILE, T)

    def body(z_ref, pos_ref, invf_ref, mone_ref, mrot_ref, wq_ref, wkv_ref, qnw_ref, kvnw_ref,
             q_ref, k_ref, v_ref, c1_ref, s1_ref, cqn_ref, ckvn_ref):
        ang = pos_ref[...].astype(F32) * invf_ref[...]
        c1 = mone_ref[...] + mrot_ref[...] * jnp.cos(ang)
        s1 = mrot_ref[...] * jnp.sin(ang)
        c1_ref[...] = c1
        s1_ref[...] = s1
        cqn, _ = _rms_fwd(z_ref[:, 0:256], qnw_ref[...])
        ckvn, _ = _rms_fwd(z_ref[:, 256:512], kvnw_ref[...])
        cqn_ref[...] = cqn.astype(BF16)
        ckvn_ref[...] = ckvn.astype(BF16)
        qe = _bdot(cqn, wq_ref[...], NT)
        kve = _bdot(ckvn, wkv_ref[...])
        k_rope = z_ref[:, 512:768] * c1 + z_ref[:, 768:1024] * s1
        for h in range(HEADS):
            q_ref[h] = ((qe[:, 256 * h:256 * h + 256] * c1 + qe[:, 1024 + 256 * h:1280 + 256 * h] * s1)
                        * Q_PRESCALE).astype(BF16)
            k_ref[h] = (kve[:, 256 * h:256 * h + 256] + k_rope).astype(BF16)
            v_ref[h] = kve[:, 1024 + 128 * h:1152 + 128 * h].astype(BF16)

    row = lambda i: (i, 0)
    head = lambda i: (0, i, 0)
    return _call(
        body, "mla_pre", (z, pos_col, invf, m_one, m_rot, wq_ext, wkv_ext, qnw, kvnw), grid=(T // tm,),
        in_specs=[pl.BlockSpec((tm, 1024), lambda i: (i, 2)), pl.BlockSpec((tm, 1), row),
                  _full((1, 256)), _full((1, 256)), _full((1, 256)), _full(wq_ext.shape), _full(wkv_ext.shape),
                  _full((1, 256)), _full((1, 256))],
        out_specs=[pl.BlockSpec((HEADS, tm, QK_PAD), head), pl.BlockSpec((HEADS, tm, QK_PAD), head),
                   pl.BlockSpec((HEADS, tm, HEAD_DIM), head), pl.BlockSpec((tm, 256), row), pl.BlockSpec((tm, 256), row),
                   pl.BlockSpec((tm, 256), row), pl.BlockSpec((tm, 256), row)],
        out_shape=[jax.ShapeDtypeStruct((HEADS, T, QK_PAD), BF16), jax.ShapeDtypeStruct((HEADS, T, QK_PAD), BF16),
                   jax.ShapeDtypeStruct((HEADS, T, HEAD_DIM), BF16), jax.ShapeDtypeStruct((T, 256), F32),
                   jax.ShapeDtypeStruct((T, 256), F32), jax.ShapeDtypeStruct((T, 256), BF16),
                   jax.ShapeDtypeStruct((T, 256), BF16)], exchange=exchange)


def _mla_bwd(dq, dk, dv, z, c1, s1, wq_ext, wkv_ext, qnw, kvnw):
    T = z.shape[0]
    tm = min(ROW_TILE_SMALL, T)

    def body(dq_ref, dk_ref, dv_ref, z_ref, c1_ref, s1_ref, wq_ref, wkv_ref, qnw_ref, kvnw_ref,
             dz_ref, dqe_ref, dkve_ref, dnw_ref):
        @pl.when(pl.program_id(0) == 0)
        def _():
            dnw_ref[...] = jnp.zeros_like(dnw_ref)

        c1, s1 = c1_ref[...], s1_ref[...]
        dkpe = jnp.zeros((tm, QK_PAD), F32)
        for h in range(HEADS):
            dqh, dkh = dq_ref[h] * Q_PRESCALE, dk_ref[h]
            dqe_ref[:, 256 * h:256 * h + 256] = (dqh * c1).astype(BF16)
            dqe_ref[:, 1024 + 256 * h:1280 + 256 * h] = (dqh * s1).astype(BF16)
            dkve_ref[:, 256 * h:256 * h + 256] = dkh.astype(BF16)
            dkve_ref[:, 1024 + 128 * h:1152 + 128 * h] = dv_ref[h].astype(BF16)
            dkpe = dkpe + dkh
        dcqn = _dot(dqe_ref[...], wq_ref[...])
        dckvn = _dot(dkve_ref[...], wkv_ref[...], NT)
        cq, ckv = z_ref[:, 0:256], z_ref[:, 256:512]
        _, rsq = _rms_fwd(cq, qnw_ref[...])
        _, rskv = _rms_fwd(ckv, kvnw_ref[...])
        dcq, wq_rows = _rms_bwd(cq, rsq, qnw_ref[...], dcqn)
        dckv, wkv_rows = _rms_bwd(ckv, rskv, kvnw_ref[...], dckvn)
        dnw_ref[:, 0:256] += _rowsum(wq_rows)
        dnw_ref[:, 256:512] += _rowsum(wkv_rows)
        dz_ref[:, 0:256] = dcq
        dz_ref[:, 256:512] = dckv
        dz_ref[:, 512:768] = dkpe * c1
        dz_ref[:, 768:1024] = dkpe * s1

    row = lambda i: (i, 0)
    head = lambda i: (0, i, 0)
    return pl.pallas_call(
        body, name="mla_bwd", grid=(T // tm,),
        in_specs=[pl.BlockSpec((HEADS, tm, QK_PAD), head), pl.BlockSpec((HEADS, tm, QK_PAD), head),
                  pl.BlockSpec((HEADS, tm, HEAD_DIM), head), pl.BlockSpec((tm, 1024), lambda i: (i, 2)),
                  pl.BlockSpec((tm, 256), row), pl.BlockSpec((tm, 256), row), _full(wq_ext.shape), _full(wkv_ext.shape),
                  _full((1, 256)), _full((1, 256))],
        out_specs=[pl.BlockSpec((tm, 1024), row), pl.BlockSpec((tm, 2048), row), pl.BlockSpec((tm, 1536), row),
                   _full((1, 512))],
        out_shape=[jax.ShapeDtypeStruct((T, 1024), F32), jax.ShapeDtypeStruct((T, 2048), BF16),
                   jax.ShapeDtypeStruct((T, 1536), BF16), jax.ShapeDtypeStruct((1, 512), F32)],
        compiler_params=_params(),
    )(dq, dk, dv, z, c1, s1, wq_ext, wkv_ext, qnw, kvnw)


_HEAD_LANES = [slice(HEAD_DIM * h, HEAD_DIM * (h + 1)) for h in range(HEADS)]


def _lower_bound(lbraw_ref):
    a0, a1 = lbraw_ref[0:1, :], lbraw_ref[1:2, :]
    mx = jnp.maximum(a0, a1)
    e0, e1 = jnp.exp(a0 - mx), jnp.exp(a1 - mx)
    return e0 / (e0 + e1)


def _tri(lower):
    r = lax.broadcasted_iota(jnp.int32, (CHUNK, CHUNK), 0)
    c = lax.broadcasted_iota(jnp.int32, (CHUNK, CHUNK), 1)
    return (r >= c) if lower else (r <= c)


def _hgrn_gates(q, f, lb, tri_lo):
    sg = _sigmoid(f)
    forget = lb + (1.0 - lb) * sg
    k = 1.0 - forget
    b = _hdot(tri_lo.astype(F32), jnp.log(forget))
    b_ref, b_last = b[CHUNK // 2 - 1:CHUNK // 2, :], b[CHUNK - 1:CHUNK, :]
    e1, e2, e3, e4 = jnp.exp(b - b_ref), jnp.exp(b_ref - b), jnp.exp(b_last - b), jnp.exp(b)
    return dict(sg=sg, forget=forget, k=k, e1=e1, e2=e2, e3=e3, e4=e4, qa=q * e1, ka=k * e2, kl=k * e3, qb=q * e4,
                decay=jnp.exp(b_last))


def _hgrn_fwd(z, lbraw, nw, exchange=None):
    T = z.shape[0]
    G = min(HGRN_GROUP, T // CHUNK)
    rows = G * CHUNK
    n_chunks = T // CHUNK

    def body(q_ref, f_ref, i_ref, g_ref, lbraw_ref, nw_ref, oraw_ref, og_ref, sp_ref, st_ref):
        @pl.when(pl.program_id(0) == 0)
        def _():
            st_ref[...] = jnp.zeros_like(st_ref)

        lb_all = _lower_bound(lbraw_ref)
        tri_lo = _tri(True)

        def chunk(cc, carry):
            rs = pl.ds(pl.multiple_of(cc * CHUNK, CHUNK), CHUNK)
            t = _hgrn_gates(q_ref[rs, :], f_ref[rs, :], lb_all, tri_lo)
            v, gate = i_ref[rs, :], g_ref[rs, :]
            st = [st_ref[h] for h in range(HEADS)]
            a = [jnp.where(tri_lo, _bdot(t["qa"][:, s], t["ka"][:, s], NT), 0.0) for s in _HEAD_LANES]
            kv = [_bdot(v[:, s], t["kl"][:, s], TN) for s in _HEAD_LANES]
            o = [_bdot(a[h], v[:, s]) + _bdot(t["qb"][:, s], st[h], NT) for h, s in enumerate(_HEAD_LANES)]
            for h, s in enumerate(_HEAD_LANES):
                sp_ref[cc, h] = st[h]
                st_ref[h] = st[h] * t["decay"][:, s] + kv[h]
            oraw_ref[rs, :] = jnp.concatenate(o, axis=1)
            on = jnp.concatenate([_rms_fwd(o[h], nw_ref[:, s])[0] for h, s in enumerate(_HEAD_LANES)], axis=1)
            og_ref[rs, :] = (on * (gate * _sigmoid(gate))).astype(BF16)
            return carry

        lax.fori_loop(0, G, chunk, 0, unroll=2)

    col = lambda j: pl.BlockSpec((rows, 512), lambda r, j=j: (r, j))
    return _call(
        body, "hgrn_fwd", (z, z, z, z, lbraw, nw), grid=(T // rows,),
        in_specs=[col(0), col(1), col(2), col(3), _full((2, 512)), _full((1, 512))],
        out_specs=[col(0), col(0), pl.BlockSpec((G, HEADS, HEAD_DIM, HEAD_DIM), lambda r: (r, 0, 0, 0))],
        out_shape=[jax.ShapeDtypeStruct((T, 512), F32), jax.ShapeDtypeStruct((T, 512), BF16),
                   jax.ShapeDtypeStruct((n_chunks, HEADS, HEAD_DIM, HEAD_DIM), F32)],
        scratch_shapes=[pltpu.VMEM((HEADS, HEAD_DIM, HEAD_DIM), F32)], exchange=exchange)


def _hgrn_bwd(dmixcat, z, oraw, sprev, lbraw, nw, exchange=None):
    T = z.shape[0]
    G = min(HGRN_GROUP, T // CHUNK)
    rows = G * CHUNK
    ng = T // rows

    def body(dog_ref, q_ref, f_ref, i_ref, g_ref, oraw_ref, sp_ref, lbraw_ref, nw_ref,
             dz_ref, dsmall_ref, dst_ref):
        @pl.when(pl.program_id(0) == 0)
        def _():
            dst_ref[...] = jnp.zeros_like(dst_ref)
            dsmall_ref[...] = jnp.zeros_like(dsmall_ref)

        lb_all = _lower_bound(lbraw_ref)
        tri_lo, tri_up = _tri(True), _tri(False)
        rowid = lax.broadcasted_iota(jnp.int32, (CHUNK, HEADS * HEAD_DIM), 0)

        def chunk(it, carry):
            cc = G - 1 - it
            rs = pl.ds(pl.multiple_of(cc * CHUNK, CHUNK), CHUNK)
            heads = list(enumerate(_HEAD_LANES))
            cat = lambda parts: jnp.concatenate(parts, axis=1)
            per_head_mean = lambda x: cat([jnp.broadcast_to(_lanemean(x[:, s]), (CHUNK, HEAD_DIM)) for s in _HEAD_LANES])
            t = _hgrn_gates(q_ref[rs, :], f_ref[rs, :], lb_all, tri_lo)
            v, gate, o, dog, nw_all = i_ref[rs, :], g_ref[rs, :], oraw_ref[rs, :], dog_ref[rs, :], nw_ref[...]
            rs_o = lax.rsqrt(per_head_mean(o * o) + RMS_EPS)
            xhat = o * rs_o
            sgg = _sigmoid(gate)
            d_on = dog * (gate * sgg)
            dz_ref[rs, 1536:2048] = dog * (xhat * nw_all) * (sgg * (1.0 + gate * (1.0 - sgg)))
            dxh = d_on * nw_all
            do = rs_o * (dxh - xhat * per_head_mean(dxh * xhat))
            dsmall_ref[:, 512:1024] += _rowsum(d_on * xhat)
            st = [sp_ref[cc, h] for h in range(HEADS)]
            dst = [dst_ref[h] for h in range(HEADS)]
            a = [jnp.where(tri_lo, _bdot(t["qa"][:, s], t["ka"][:, s], NT), 0.0) for s in _HEAD_LANES]
            da = [jnp.where(tri_lo, _bdot(do[:, s], v[:, s], NT), 0.0) for s in _HEAD_LANES]
            dqb = cat([_bdot(do[:, s], st[h]) for h, s in heads])
            dkl = cat([_bdot(v[:, s], dst[h]) for h, s in heads])
            dv_ = cat([_bdot(t["kl"][:, s], dst[h], NT) + _bdot(a[h], do[:, s], TN) for h, s in heads])
            dqa = cat([_bdot(da[h], t["ka"][:, s]) for h, s in heads])
            dka = cat([_bdot(da[h], t["qa"][:, s], TN) for h, s in heads])
            ddecay = cat([_rowsum(dst[h] * st[h]) for h in range(HEADS)])
            for h, s in heads:
                dst_ref[h] = dst[h] * t["decay"][:, s] + _bdot(do[:, s], t["qb"][:, s], TN)
            pa, pk, pb, pl_ = dqa * t["qa"], dka * t["ka"], dqb * t["qb"], dkl * t["kl"]
            db = pa - pk + pb - pl_
            db = db + jnp.where(rowid == CHUNK // 2 - 1, _rowsum(pk - pa), 0.0)
            db = db + jnp.where(rowid == CHUNK - 1, _rowsum(pl_) + ddecay * t["decay"], 0.0)
            dlogf = _hdot(tri_up.astype(F32), db)
            dforget = dlogf / t["forget"] - (dka * t["e2"] + dkl * t["e3"])
            sg = t["sg"]
            dz_ref[rs, 0:512] = dqa * t["e1"] + dqb * t["e4"]
            dz_ref[rs, 512:1024] = dforget * (1.0 - lb_all) * sg * (1.0 - sg)
            dz_ref[rs, 1024:1536] = dv_
            dsmall_ref[:, 0:512] += _rowsum(dforget * (1.0 - sg))
            return carry

        lax.fori_loop(0, G, chunk, 0, unroll=2)

    col = lambda j: pl.BlockSpec((rows, 512), lambda r, j=j: (ng - 1 - r, j))
    return _call(
        body, "hgrn_bwd", (dmixcat, z, z, z, z, oraw, sprev, lbraw, nw), grid=(ng,),
        in_specs=[col(0), col(0), col(1), col(2), col(3), col(0),
                  pl.BlockSpec((G, HEADS, HEAD_DIM, HEAD_DIM), lambda r: (ng - 1 - r, 0, 0, 0)),
                  _full((2, 512)), _full((1, 512))],
        out_specs=[pl.BlockSpec((rows, 2048), lambda r: (ng - 1 - r, 0)), _full((1, 1024))],
        out_shape=[jax.ShapeDtypeStruct((T, 2048), F32), jax.ShapeDtypeStruct((1, 1024), F32)],
        scratch_shapes=[pltpu.VMEM((HEADS, HEAD_DIM, HEAD_DIM), F32)], exchange=exchange)


def _diag_mask(t):
    r = lax.broadcasted_iota(jnp.int32, (t, t), 0)
    c = lax.broadcasted_iota(jnp.int32, (t, t), 1)
    return r >= c


def _attn_fwd(q, k, v, exchange=None):
    _, T, _ = q.shape
    t = min(ATT_TILE, T)

    def body(q_ref, k_ref, v_ref, o_ref, lse_ref):
        i = pl.program_id(1)
        qb = q_ref[...]

        rows = lambda j: pl.ds(pl.multiple_of(j * t, t), t)

        def logits(j, masked):
            s = _dot(qb, k_ref[rows(j), :], NT)
            return jnp.where(_diag_mask(t), s, NEG_BIG) if masked else s

        def absorb(s, j, carry):
            m, l, acc = carry
            mn = jnp.maximum(m, jnp.max(s, axis=-1, keepdims=True))
            p = jnp.exp2(s - mn)
            al = jnp.exp2(m - mn)
            return mn, al * l + jnp.sum(p, axis=-1, keepdims=True), al * acc + _dot(p.astype(BF16), v_ref[rows(j), :])

        def pair(j0, carry, last_masked):
            s0, s1 = logits(j0, False), logits(j0 + 1, last_masked)
            return absorb(s1, j0 + 1, absorb(s0, j0, carry))

        init = (jnp.full((t, 1), NEG_BIG, F32), jnp.zeros((t, 1), F32), jnp.zeros((t, HEAD_DIM), F32))
        carry = lax.fori_loop(0, i // 2, lambda jj, c: pair(2 * jj, c, False), init)
        m, l, acc = lax.cond(i % 2 == 1, lambda c: pair(i - 1, c, True),
                             lambda c: absorb(logits(i, True), i, c), carry)
        o_ref[...] = acc / l
        lse_ref[...] = jnp.broadcast_to(m + jnp.log2(l), (t, HEAD_DIM))

    return _call(
        body, "attn_fwd", (q, k, v), grid=(HEADS, T // t),
        in_specs=[pl.BlockSpec((None, t, QK_PAD), lambda h, i: (h, i, 0)),
                  pl.BlockSpec((None, T, QK_PAD), lambda h, i: (h, 0, 0)),
                  pl.BlockSpec((None, T, HEAD_DIM), lambda h, i: (h, 0, 0))],
        out_specs=[pl.BlockSpec((t, HEAD_DIM), lambda h, i: (i, h)),
                   pl.BlockSpec((None, t, HEAD_DIM), lambda h, i: (h, i, 0))],
        out_shape=[jax.ShapeDtypeStruct((T, HEADS * HEAD_DIM), F32), jax.ShapeDtypeStruct((HEADS, T, HEAD_DIM), F32)],
        exchange=exchange)


def _attn_bwd(q, k, v, dmixcat, o, lse, exchange=None):
    _, T, _ = q.shape
    t = min(ATT_TILE, T)
    nq = T // t

    def body(q_ref, k_ref, v_ref, do_ref, o_ref, lse_ref, dq_ref, dk_ref, dv_ref, delta_ref):
        j = pl.program_id(1)

        @pl.when(j == 0)
        def _():
            dq_ref[...] = jnp.zeros_like(dq_ref)

            def fill(i, carry):
                rs = pl.ds(pl.multiple_of(i * t, t), t)
                delta_ref[rs, :] = jnp.broadcast_to(
                    jnp.sum(do_ref[rs, :] * o_ref[rs, :], axis=-1, keepdims=True), (t, HEAD_DIM))
                return carry

            lax.fori_loop(0, nq, fill, 0)

        kb, vb = k_ref[...], v_ref[...]

        def steps(blocks, carry):
            dk, dv = carry
            rs = [pl.ds(pl.multiple_of(i * t, t), t) for i, _ in blocks]
            qb = [q_ref[r, :] for r in rs]
            dob = [do_ref[r, :].astype(BF16) for r in rs]
            s = [_dot(b, kb, NT) for b in qb]
            dp = [_dot(b, vb, NT) for b in dob]
            for n, (_, masked) in enumerate(blocks):
                p = jnp.exp2(s[n] - lse_ref[rs[n], 0:1])
                if masked:
                    p = jnp.where(_diag_mask(t), p, 0.0)
                ds = (p * (dp[n] - delta_ref[rs[n], 0:1]) * LN2).astype(BF16)
                dq_ref[rs[n], :] += _dot(ds, kb)
                dk = dk + _dot(ds, qb[n], TN)
                dv = dv + _dot(p.astype(BF16), dob[n], TN)
            return dk, dv

        zero = (jnp.zeros((t, QK_PAD), F32), jnp.zeros((t, HEAD_DIM), F32))
        rest = nq - 1 - j
        carry = lax.cond(rest % 2 == 1, lambda c: steps([(j, True), (j + 1, False)], c),
                         lambda c: steps([(j, True)], c), zero)
        first = j + 1 + rest % 2
        dk, dv = lax.fori_loop(0, rest // 2, lambda n, c: steps([(first + 2 * n, False), (first + 2 * n + 1, False)], c),
                               carry)
        dk_ref[...] = dk
        dv_ref[...] = dv

    return _call(
        body, "attn_bwd", (q, k, v, dmixcat, o, lse), grid=(HEADS, nq),
        in_specs=[pl.BlockSpec((None, T, QK_PAD), lambda h, j: (h, 0, 0)),
                  pl.BlockSpec((None, t, QK_PAD), lambda h, j: (h, j, 0)),
                  pl.BlockSpec((None, t, HEAD_DIM), lambda h, j: (h, j, 0)),
                  pl.BlockSpec((T, HEAD_DIM), lambda h, j: (0, HEADS + h)),
                  pl.BlockSpec((T, HEAD_DIM), lambda h, j: (0, h)),
                  pl.BlockSpec((None, T, HEAD_DIM), lambda h, j: (h, 0, 0))],
        out_specs=[pl.BlockSpec((None, T, QK_PAD), lambda h, j: (h, 0, 0)),
                   pl.BlockSpec((None, t, QK_PAD), lambda h, j: (h, j, 0)),
                   pl.BlockSpec((None, t, HEAD_DIM), lambda h, j: (h, j, 0))],
        out_shape=[jax.ShapeDtypeStruct((HEADS, T, QK_PAD), F32), jax.ShapeDtypeStruct((HEADS, T, QK_PAD), F32),
                   jax.ShapeDtypeStruct((HEADS, T, HEAD_DIM), F32)],
        scratch_shapes=[pltpu.VMEM((T, HEAD_DIM), F32)], exchange=exchange)


def _ln_fwd(r):
    mu = _lanemean(r)
    xc = r - mu
    rstd = lax.rsqrt(_lanemean(xc * xc) + LN_EPS)
    return xc * rstd, rstd


def _ln_bwd(dxh, xhat, rstd):
    return rstd * (dxh - _lanemean(dxh) - xhat * _lanemean(dxh * xhat))


def _mix_ln1(o_hg, o_mla, w_out, x, g_a, ln1_g, ln1_b, sc_m, sh_m, exchange=None):
    T = x.shape[0]
    tm = min(ROW_TILE_SMALL, T)
    half = o_hg.shape[1]

    def body(hg_ref, mla_ref, w_ref, x_ref, ga_ref, g_ref, b_ref, sc_ref, sh_ref, mix_ref, xhat_ref, rstd_ref, u2_ref):
        mix = _dot(hg_ref[...], w_ref[0:half, :]) + _bdot(mla_ref[...], w_ref[half:, :])
        mix_ref[...] = mix
        xhat, rstd = _ln_fwd(ALPHA * x_ref[...] + (1.0 + ga_ref[...]) * mix)
        xhat_ref[...] = xhat
        rstd_ref[...] = jnp.broadcast_to(rstd, (tm, 128))
        u2_ref[...] = _modulate(xhat * g_ref[...] + b_ref[...], sc_ref[...], sh_ref[...]).astype(BF16)

    row = pl.BlockSpec((tm, D_MODEL), lambda i: (i, 0))
    vec = _full((1, D_MODEL))
    halfrow = pl.BlockSpec((tm, half), lambda i: (i, 0))
    return _call(
        body, "mix_ln1", (o_hg, o_mla, w_out, x, g_a, ln1_g, ln1_b, sc_m, sh_m), grid=(T // tm,),
        in_specs=[halfrow, halfrow, _full(w_out.shape), row, vec, vec, vec, vec, vec],
        out_specs=[row, row, pl.BlockSpec((tm, 128), lambda i: (i, 0)), row],
        out_shape=[jax.ShapeDtypeStruct((T, D_MODEL), F32), jax.ShapeDtypeStruct((T, D_MODEL), F32),
                   jax.ShapeDtypeStruct((T, 128), F32), jax.ShapeDtypeStruct((T, D_MODEL), BF16)],
        exchange=exchange)


def _mlp_fwd(u2, w1, w2, xhat1, ln1_g, ln1_b, g_m, ln2_g, ln2_b, target):
    T = u2.shape[0]
    tf = w1.shape[-1]
    nf = N_DEV // MLP_SLABS
    tm = min(ROW_TILE, T)

    def body(u2_ref, w1_ref, w2_ref, xhat_ref, g1_ref, b1_ref, gm_ref, g2_ref, b2_ref, tgt_ref,
             r_ref, dr2_ref, dh_ref, small_ref, acc_ref):
        i, f = pl.program_id(0), pl.program_id(1)
        dm = D_MODEL

        @pl.when((i == 0) & (f == 0))
        def _():
            small_ref[...] = jnp.zeros_like(small_ref)

        @pl.when(f == 0)
        def _():
            acc_ref[...] = jnp.zeros_like(acc_ref)

        u2t = u2_ref[...]
        part = None
        for s in range(MLP_SLABS):
            r = jnp.maximum(_dot(u2t, w1_ref[s]), 0.0)
            r_ref[:, s * tf:(s + 1) * tf] = r.astype(BF16)
            d = _bdot(r * r, w2_ref[s])
            part = d if part is None else part + d
        acc_ref[...] += part

        @pl.when(f == nf - 1)
        def _():
            h = acc_ref[...]
            x1 = xhat_ref[...] * g1_ref[...] + b1_ref[...]
            xhat2, rstd2 = _ln_fwd(ALPHA * x1 + (1.0 + gm_ref[...]) * h)
            err = xhat2 * g2_ref[...] + b2_ref[...] - tgt_ref[...]
            small_ref[:, 3 * dm:] += jnp.sum(0.5 * _lanemean(err * err), axis=0, keepdims=True)
            dy = err * (1.0 / D_MODEL)
            small_ref[:, dm:2 * dm] += _rowsum(dy * xhat2)
            small_ref[:, 2 * dm:3 * dm] += _rowsum(dy)
            dr2 = _ln_bwd(dy * g2_ref[...], xhat2, rstd2)
            dr2_ref[...] = dr2
            small_ref[:, 0:dm] += _rowsum(dr2 * h)
            dh_ref[...] = ((1.0 + gm_ref[...]) * dr2).astype(BF16)

    row = pl.BlockSpec((tm, D_MODEL), lambda i, f: (i, 0))
    vec = _full((1, D_MODEL))
    return pl.pallas_call(
        body, name="mlp_fwd", grid=(T // tm, nf),
        in_specs=[row, pl.BlockSpec((MLP_SLABS, D_MODEL, tf), lambda i, f: (f, 0, 0)),
                  pl.BlockSpec((MLP_SLABS, tf, D_MODEL), lambda i, f: (f, 0, 0)),
                  row, vec, vec, vec, vec, vec, row],
        out_specs=[pl.BlockSpec((tm, MLP_SLABS * tf), lambda i, f: (i, f)), row, row, _full((1, 3 * D_MODEL + 128))],
        out_shape=[jax.ShapeDtypeStruct((T, N_DEV * tf), BF16), jax.ShapeDtypeStruct((T, D_MODEL), F32),
                   jax.ShapeDtypeStruct((T, D_MODEL), BF16), jax.ShapeDtypeStruct((1, 3 * D_MODEL + 128), F32)],
        scratch_shapes=[pltpu.VMEM((tm, D_MODEL), F32)],
        compiler_params=_params(),
    )(u2, w1, w2, xhat1, ln1_g, ln1_b, g_m, ln2_g, ln2_b, target)


def _mlp_bwd(dh, w1, w2, r, dr2, xhat1, rstd1, mix, ln1_g, ln1_b, sc_m, g_a):
    T = dh.shape[0]
    tf = w1.shape[-1]
    nf = N_DEV // MLP_SLABS
    tm = min(ROW_TILE, T)

    def body(dh_ref, w1_ref, w2_ref, r_ref, dr2_ref, xhat_ref, rstd_ref, mix_ref, g1_ref, b1_ref, sc_ref, ga_ref,
             dhpre_ref, dr1_ref, dmix_ref, small_ref, acc_ref):
        i, f = pl.program_id(0), pl.program_id(1)
        dm = D_MODEL

        @pl.when((i == 0) & (f == 0))
        def _():
            small_ref[...] = jnp.zeros_like(small_ref)

        @pl.when(f == 0)
        def _():
            acc_ref[...] = jnp.zeros_like(acc_ref)

        dht = dh_ref[...]
        part = None
        for s in range(MLP_SLABS):
            cols = slice(s * tf, (s + 1) * tf)
            dhpre = (_dot(dht, w2_ref[s], NT) * (2.0 * r_ref[:, cols].astype(F32))).astype(BF16)
            dhpre_ref[:, cols] = dhpre
            d = _dot(dhpre, w1_ref[s], NT)
            part = d if part is None else part + d
        acc_ref[...] += part

        @pl.when(f == nf - 1)
        def _():
            du2 = acc_ref[...]
            xhat = xhat_ref[...]
            x1 = xhat * g1_ref[...] + b1_ref[...]
            dx1 = ALPHA * dr2_ref[...] + du2 * (1.0 + sc_ref[...])
            small_ref[:, 2 * dm:3 * dm] += _rowsum(du2 * x1)
            small_ref[:, dm:2 * dm] += _rowsum(du2)
            small_ref[:, 3 * dm:4 * dm] += _rowsum(dx1 * xhat)
            small_ref[:, 4 * dm:5 * dm] += _rowsum(dx1)
            dr1 = _ln_bwd(dx1 * g1_ref[...], xhat, rstd_ref[:, 0:1])
            dr1_ref[...] = dr1
            small_ref[:, 0:dm] += _rowsum(dr1 * mix_ref[...])
            dmix_ref[...] = ((1.0 + ga_ref[...]) * dr1).astype(BF16)

    row = pl.BlockSpec((tm, D_MODEL), lambda i, f: (i, 0))
    vec = _full((1, D_MODEL))
    return pl.pallas_call(
        body, name="mlp_bwd", grid=(T // tm, nf),
        in_specs=[row, pl.BlockSpec((MLP_SLABS, D_MODEL, tf), lambda i, f: (f, 0, 0)),
                  pl.BlockSpec((MLP_SLABS, tf, D_MODEL), lambda i, f: (f, 0, 0)),
                  pl.BlockSpec((tm, MLP_SLABS * tf), lambda i, f: (i, f)), row, row,
                  pl.BlockSpec((tm, 128), lambda i, f: (i, 0)), row, vec, vec, vec, vec],
        out_specs=[pl.BlockSpec((tm, MLP_SLABS * tf), lambda i, f: (i, f)), row, row, _full((1, 5 * D_MODEL))],
        out_shape=[jax.ShapeDtypeStruct((T, N_DEV * tf), BF16), jax.ShapeDtypeStruct((T, D_MODEL), F32),
                   jax.ShapeDtypeStruct((T, D_MODEL), BF16), jax.ShapeDtypeStruct((1, 5 * D_MODEL), F32)],
        scratch_shapes=[pltpu.VMEM((tm, D_MODEL), F32)],
        compiler_params=_params(),
    )(dh, w1, w2, r, dr2, xhat1, rstd1, mix, ln1_g, ln1_b, sc_m, g_a)


def _input_bwd(dz_h, dz_m, w_in_ext, x, dr1, sc_a, exchange=None):
    T = x.shape[0]
    tm = min(ROW_TILE_SMALL, T)

    def body(dzh_ref, dzm_ref, w_ref, x_ref, dr1_ref, sc_ref, gx_ref, small_ref):
        @pl.when(pl.program_id(0) == 0)
        def _():
            small_ref[...] = jnp.zeros_like(small_ref)

        du = _bdot(dzh_ref[...], w_ref[0:2048, :]) + _bdot(dzm_ref[...], w_ref[2048:3072, :])
        gx_ref[...] = ALPHA * dr1_ref[...] + du * (1.0 + sc_ref[...])
        small_ref[:, D_MODEL:] += _rowsum(du * x_ref[...])
        small_ref[:, 0:D_MODEL] += _rowsum(du)

    row = pl.BlockSpec((tm, D_MODEL), lambda i: (i, 0))
    vec = _full((1, D_MODEL))
    return _call(
        body, "input_bwd", (dz_h, dz_m, w_in_ext, x, dr1, sc_a), grid=(T // tm,),
        in_specs=[pl.BlockSpec((tm, 2048), lambda i: (i, 0)), row, _full(w_in_ext.shape), row, row, vec],
        out_specs=[row, _full((1, 2 * D_MODEL))],
        out_shape=[jax.ShapeDtypeStruct((T, D_MODEL), F32), jax.ShapeDtypeStruct((1, 2 * D_MODEL), F32)],
        exchange=exchange)


def _adam_math(w, g, m, v):
    m = ADAM_B1 * m + (1.0 - ADAM_B1) * g
    v = ADAM_B2 * v + (1.0 - ADAM_B2) * (g * g)
    m_hat = m / (1.0 - ADAM_B1 ** ADAM_STEP)
    v_hat = v / (1.0 - ADAM_B2 ** ADAM_STEP)
    return -ADAM_LR * (m_hat / (jnp.sqrt(v_hat) + ADAM_EPS) + ADAM_WD * w), m, v


def _adam(g_slabs, w, m, v, name, g_fn=None, g_extra=()):
    R, C = w.shape
    tr = 256 if R % 256 == 0 else R
    ns = 0 if g_slabs is None else g_slabs.shape[0]
    ne = len(g_extra)

    def body(*refs):
        e_refs = refs[:ne]
        refs = refs[ne:]
        if ns:
            gs_ref, refs = refs[0], refs[1:]
        w_ref, m_ref, v_ref, g_ref, d_ref, nm_ref, nv_ref = refs
        if g_fn is not None:
            g = g_fn(*e_refs)
        else:
            g = gs_ref[0].astype(F32)
            for s in range(1, ns):
                g = g + gs_ref[s].astype(F32)
        d, nm, nv = _adam_math(w_ref[...], g, m_ref[...], v_ref[...])
        g_ref[...] = g
        d_ref[...] = d
        nm_ref[...] = nm
        nv_ref[...] = nv

    blk = pl.BlockSpec((tr, C), lambda i: (i, 0))
    in_specs = [pl.BlockSpec((tr, e.shape[1]), lambda i: (i, 0)) if e.shape[0] == R else _full(e.shape) for e in g_extra]
    args = list(g_extra)
    if ns:
        in_specs.append(pl.BlockSpec((ns, tr, C), lambda i: (0, i, 0)))
        args.append(g_slabs)
    return pl.pallas_call(
        body, name=name, grid=(R // tr,), in_specs=in_specs + [blk] * 3, out_specs=[blk] * 4,
        out_shape=[jax.ShapeDtypeStruct((R, C), F32)] * 4, compiler_params=_params(),
    )(*args, w, m, v)


def _adam_small(small_all, params):
    n = len(params)

    def body(*refs):
        s_ref, refs = refs[0], refs[1:]
        wmv, loss_ref, outs = refs[:3 * n], refs[3 * n], refs[3 * n + 1:]
        tot = s_ref[0]
        for i in range(1, N_DEV):
            tot = tot + s_ref[i]
        loss_ref[...] = tot[:, SMALL_W - 128:]
        for j, (w, _, _, off) in enumerate(params):
            w_ref, m_ref, v_ref = wmv[3 * j:3 * j + 3]
            g_ref, d_ref, nm_ref, nv_ref = outs[4 * j:4 * j + 4]
            if w.shape[0] == 2:
                lb = _lower_bound(w_ref)
                g0 = tot[:, off:off + w.shape[1]] * lb * (1.0 - lb)
                rows = [(slice(0, 1), g0), (slice(1, 2), -g0)]
            else:
                rows = [(slice(0, 1), tot[:, off:off + w.shape[1]])]
            for rs, g in rows:
                d, nm, nv = _adam_math(w_ref[rs, :], g, m_ref[rs, :], v_ref[rs, :])
                g_ref[rs, :], d_ref[rs, :], nm_ref[rs, :], nv_ref[rs, :] = g, d, nm, nv

    out_shape = [jax.ShapeDtypeStruct((1, 128), F32)]
    for w, _, _, _ in params:
        out_shape += [jax.ShapeDtypeStruct(w.shape, F32)] * 4
    res = pl.pallas_call(body, name="adam_small", out_shape=out_shape, compiler_params=_params())(
        small_all, *[a for w, m, v, _ in params for a in (w, m, v)])
    return res[0], [tuple(res[1 + 4 * j:5 + 4 * j]) for j in range(n)]


def _cols_from_slabs(g):
    s, r, c = g.shape
    return jnp.transpose(g, (1, 0, 2)).reshape(r, s * c)


def _slabs_from_cols(w):
    r, c = w.shape
    return jnp.transpose(w.reshape(r, N_DEV, c // N_DEV), (1, 0, 2))


def _rot_half_rows(wt):
    return jnp.concatenate([-wt[32:], wt[:32]], axis=0)


def _unrot_half_rows(dwt_rot):
    return jnp.concatenate([dwt_rot[32:], -dwt_rot[:32]], axis=0)


def _ext_in_t(g):
    k_in = g.shape[2]
    z64, z128 = jnp.zeros((64, k_in), BF16), jnp.zeros((128, k_in), BF16)
    last = g.shape[1] - ROPE_DIM
    wk = g[N_DEV - 1, last:]
    return jnp.concatenate([g[i] for i in range(N_DEV - 1)] + [g[N_DEV - 1, :last], z128, wk, z64, z128,
                                                               _rot_half_rows(wk), z64], axis=0)


def _ext_q_t(wt):
    r = wt.shape[1]
    z64, z128 = jnp.zeros((64, r), BF16), jnp.zeros((128, r), BF16)
    per = HEAD_DIM + ROPE_DIM
    main = [jnp.concatenate([wt[per * h:per * (h + 1)], z64], axis=0) for h in range(HEADS)]
    rot = [jnp.concatenate([z128, _rot_half_rows(wt[per * h + HEAD_DIM:per * (h + 1)]), z64], axis=0)
           for h in range(HEADS)]
    return jnp.concatenate(main + rot, axis=0)


def _ext_kv(w_kv_up):
    r = w_kv_up.shape[0]
    z128 = jnp.zeros((r, 128), BF16)
    wkv = w_kv_up.reshape(r, HEADS, 2 * HEAD_DIM)
    kpad = [jnp.concatenate([wkv[:, h, :HEAD_DIM], z128], axis=1) for h in range(HEADS)]
    vals = [wkv[:, h, HEAD_DIM:] for h in range(HEADS)]
    return jnp.concatenate(kpad + vals, axis=1)


def _grad_in_from_ext_t(dwt_h, dwt_m):
    dwk = dwt_m[512 + 128:512 + 192] + _unrot_half_rows(dwt_m[768 + 128:768 + 192])
    return jnp.concatenate([dwt_h, dwt_m[:512], dwk], axis=0)


def _grad_q_from_ext_t(dwq_ext_t):
    rows = []
    for h in range(HEADS):
        main, rot = dwq_ext_t[256 * h:256 * h + 256], dwq_ext_t[1024 + 256 * h:1280 + 256 * h]
        rows += [main[:128], main[128:192] + _unrot_half_rows(rot[128:192])]
    return jnp.concatenate(rows, axis=0)


def _grad_kv_from_ext(dwkv_ext):
    kvcols = []
    for h in range(HEADS):
        kvcols += [dwkv_ext[:, 256 * h:256 * h + 128], dwkv_ext[:, 1024 + 128 * h:1152 + 128 * h]]
    return jnp.concatenate(kvcols, axis=1)


SMALL_W = 6144 + 512 + 512 + 256 + 256 + 4 * 1024 + 128


def kernel(x, c, positions, w_ada, b_ada, w_in, hg_lower_bounds, hg_norm_w, mla_q_norm_w, w_q_up, mla_kv_norm_w, w_kv_up, w_out, ln1_g, ln1_b, w_mlp_in, w_mlp_out, ln2_g, ln2_b, loss_target, m_w_ada, m_b_ada, m_w_in, m_hg_lower_bounds, m_hg_norm_w, m_mla_q_norm_w, m_w_q_up, m_mla_kv_norm_w, m_w_kv_up, m_w_out, m_ln1_g, m_ln1_b, m_w_mlp_in, m_w_mlp_out, m_ln2_g, m_ln2_b, v_w_ada, v_b_ada, v_w_in, v_hg_lower_bounds, v_hg_norm_w, v_mla_q_norm_w, v_w_q_up, v_mla_kv_norm_w, v_w_kv_up, v_w_out, v_ln1_g, v_ln1_b, v_w_mlp_in, v_w_mlp_out, v_ln2_g, v_ln2_b):
    T = x.shape[1]
    me = 4 * lax.axis_index("x") + 2 * lax.axis_index("y") + lax.axis_index("c")
    xs, tgt = x[0], loss_target[0]
    transposed = ("w_in", "w_q_up")
    as_used = lambda n, a: a[0].T if n in transposed else a[0]
    big = {n: as_used(n, a) for n, a in dict(w_in=w_in, w_q_up=w_q_up, w_kv_up=w_kv_up, w_out=w_out,
                                              w_mlp_in=w_mlp_in, w_mlp_out=w_mlp_out).items()}
    names = list(big)

    bf = {n: big[n].astype(BF16) for n in names}
    g_in, g_c = _gather_two_level([bf["w_in"], c], name="gather_w_in")
    c_all = g_c.reshape(N_DEV, D_MODEL)

    ada_cols = w_ada.shape[2]
    mod_part, cond = _mod_part(c_all, w_ada[0], lax.dynamic_slice(b_ada, (0, me * ada_cols), (1, ada_cols)))
    (mod_all,) = _exchange([mod_part], scatter=False, name="gather_mod")
    mod_row = lax.dynamic_slice(mod_all, (0, me, 0), (N_DEV, 1, ada_cols)).reshape(1, N_DEV * ada_cols)
    sh_a, sc_a, g_a, sh_m, sc_m, g_m = [mod_row[:, D_MODEL * i:D_MODEL * (i + 1)] for i in range(6)]

    w_in_ext = _ext_in_t(g_in)
    z, (g_q, g_kv, g_out) = _matmul(xs, w_in_ext, "NT", "in_proj", a_fn=_modulate, extras=(sc_a, sh_a), tn=3072,
                                    exchange=_Exchange([bf["w_q_up"], bf["w_kv_up"], bf["w_out"]], False))
    wq_ext = _ext_q_t(g_q.reshape(N_DEV * g_q.shape[1], g_q.shape[2]))
    wkv_ext = _ext_kv(_cols_from_slabs(g_kv))
    w_out_full = g_out.reshape(D_MODEL, D_MODEL)
    inv_freq = 1.0 / (ROPE_THETA ** (jnp.arange(0, ROPE_DIM, 2, dtype=F32) / ROPE_DIM))
    zeros = lambda n: jnp.zeros((n,), F32)
    invf = jnp.concatenate([zeros(128), inv_freq, inv_freq, zeros(64)]).reshape(1, QK_PAD)
    m_one = jnp.concatenate([jnp.ones((128,), F32), zeros(128)]).reshape(1, QK_PAD)
    m_rot = jnp.concatenate([zeros(128), jnp.ones((64,), F32), zeros(64)]).reshape(1, QK_PAD)
    q, k, v, c1, s1, cqn, ckvn = _mla_pre(z, positions.reshape(T, 1), invf, m_one, m_rot, wq_ext, wkv_ext,
                                          mla_q_norm_w, mla_kv_norm_w)[0]
    (o_raw, o_gated, s_prev), (w1,) = _hgrn_fwd(z, hg_lower_bounds, hg_norm_w,
                                                exchange=_StagedGather(bf["w_mlp_in"]))
    (o_mla, lse), (w2,) = _attn_fwd(q, k, v, exchange=_StagedGather(bf["w_mlp_out"]))
    mix, xhat1, rstd1, u2 = _mix_ln1(o_gated, o_mla, w_out_full, xs, g_a, ln1_g, ln1_b, sc_m, sh_m)[0]
    r, dr2, dh, small_mlp_fwd = _mlp_fwd(u2, w1, w2, xhat1, ln1_g, ln1_b, g_m, ln2_g, ln2_b, tgt)

    dhpre, dr1, dmix, small_mlp_bwd = _mlp_bwd(dh, w1, w2, r, dr2, xhat1, rstd1, mix, ln1_g, ln1_b, sc_m, g_a)
    received = {}
    dw2 = _matmul(r, dh, "TN", "wgrad_mlp_out", out_dtype=BF16, a_fn=_square, tm=1024)
    dw1 = _matmul(u2, dhpre, "TN", "wgrad_mlp_in", out_dtype=BF16, tm=1024, out_slabs=N_DEV)
    dmixcat = _matmul(dmix, w_out_full, "NT", "dgrad_out")
    dw_out = jnp.concatenate([_matmul(o_gated, dmix, "TN", "wgrad_out_hg", out_dtype=BF16),
                              _matmul(o_mla, dmix, "TN", "wgrad_out_mla", out_dtype=BF16)], axis=0)
    (dz_h, small_hgrn), (received["w_out"],) = _hgrn_bwd(
        dmixcat, z, o_raw, s_prev, hg_lower_bounds, hg_norm_w,
        exchange=_Exchange([dw_out.reshape(N_DEV, D_MODEL // N_DEV, D_MODEL)], True))
    (dq, dk, dv), (received["w_mlp_in"], received["w_mlp_out"]) = _attn_bwd(
        q, k, v, dmixcat, o_mla, lse,
        exchange=_Exchange([dw1, dw2.reshape(N_DEV, dw2.shape[0] // N_DEV, D_MODEL)], True))
    dz_m, dq_ext, dkv_ext, small_mla = _mla_bwd(dq, dk, dv, z, c1, s1, wq_ext, wkv_ext, mla_q_norm_w, mla_kv_norm_w)
    dwq_t = _grad_q_from_ext_t(_matmul(dq_ext, cqn, "TN", "wgrad_q_up", tm=1024))
    dwkv = _grad_kv_from_ext(_matmul(ckvn, dkv_ext, "TN", "wgrad_kv_up", tn=1536))
    qkv_slabs = [dwq_t.reshape((N_DEV, dwq_t.shape[0] // N_DEV, dwq_t.shape[1])).astype(BF16),
                 _slabs_from_cols(dwkv).astype(BF16)]
    dwt_h, (received["w_q_up"], received["w_kv_up"]) = _matmul(
        dz_h, xs, "TN", "wgrad_in_h", b_fn=_modulate, extras=(sc_a, sh_a), tm=1024,
        exchange=_Exchange(qkv_slabs, True))
    dwt_m = _matmul(dz_m, xs, "TN", "wgrad_in_m", b_fn=_modulate, extras=(sc_a, sh_a), tm=1024)
    dw_in_t = _grad_in_from_ext_t(dwt_h, dwt_m)
    in_slabs = dw_in_t.reshape((N_DEV, dw_in_t.shape[0] // N_DEV, dw_in_t.shape[1])).astype(BF16)
    (grad_x, small_in), (received["w_in"],) = _input_bwd(
        dz_h, dz_m, w_in_ext, xs, dr1, sc_a, exchange=_Exchange([in_slabs], True))

    small = jnp.concatenate([small_in, small_mlp_bwd[:, :3 * D_MODEL], small_mlp_fwd[:, :D_MODEL], small_hgrn,
                             small_mla, small_mlp_bwd[:, 3 * D_MODEL:], small_mlp_fwd[:, D_MODEL:]], axis=1)
    assert small.shape == (1, SMALL_W)
    (small_all,) = _exchange([small], scatter=False, name="gather_small")

    moments = dict(w_in=(m_w_in, v_w_in), w_q_up=(m_w_q_up, v_w_q_up), w_kv_up=(m_w_kv_up, v_w_kv_up),
                   w_out=(m_w_out, v_w_out), w_mlp_in=(m_w_mlp_in, v_w_mlp_in), w_mlp_out=(m_w_mlp_out, v_w_mlp_out))
    res = {}
    for n in names:
        res[n] = _adam(received[n], big[n], as_used(n, moments[n][0]), as_used(n, moments[n][1]), name="adam_" + n)
    dmod_cols = lax.dynamic_slice(small_all.reshape(N_DEV, SMALL_W), (0, me * ada_cols), (N_DEV, ada_cols))
    cond_t = cond.T

    def ada_grad(ct_ref, dm_ref):
        g = ct_ref[:, 0:1] * dm_ref[0:1, :]
        for b in range(1, N_DEV):
            g = g + ct_ref[:, b:b + 1] * dm_ref[b:b + 1, :]
        return g

    res["w_ada"] = _adam(None, w_ada[0], m_w_ada[0], v_w_ada[0], name="adam_w_ada", g_fn=ada_grad,
                         g_extra=(cond_t, dmod_cols))

    small_params = [("b_ada", b_ada, m_b_ada, v_b_ada, 0),
                    ("hg_lower_bounds", hg_lower_bounds, m_hg_lower_bounds, v_hg_lower_bounds, 6144),
                    ("hg_norm_w", hg_norm_w, m_hg_norm_w, v_hg_norm_w, 6656),
                    ("mla_q_norm_w", mla_q_norm_w, m_mla_q_norm_w, v_mla_q_norm_w, 7168),
                    ("mla_kv_norm_w", mla_kv_norm_w, m_mla_kv_norm_w, v_mla_kv_norm_w, 7424),
                    ("ln1_g", ln1_g, m_ln1_g, v_ln1_g, 7680), ("ln1_b", ln1_b, m_ln1_b, v_ln1_b, 8704),
                    ("ln2_g", ln2_g, m_ln2_g, v_ln2_g, 9728), ("ln2_b", ln2_b, m_ln2_b, v_ln2_b, 10752)]
    loss_row, small_res = _adam_small(small_all, [p[1:] for p in small_params])
    for p, r4 in zip(small_params, small_res):
        res[p[0]] = r4
    loss = loss_row[0, 0]

    order = ["w_ada", "b_ada", "w_in", "hg_lower_bounds", "hg_norm_w", "mla_q_norm_w", "w_q_up", "mla_kv_norm_w",
             "w_kv_up", "w_out", "ln1_g", "ln1_b", "w_mlp_in", "w_mlp_out", "ln2_g", "ln2_b"]
    def as_given(n, a):
        if n in transposed:
            a = a.T
        return a[None] if n in big or n == "w_ada" else a

    shaped = {n: tuple(as_given(n, a) for a in res[n]) for n in order}
    outs = [loss, grad_x.reshape(1, T, D_MODEL)]
    for i in range(4):
        outs += [shaped[n][i] for n in order]
    return tuple(outs)
```

```python
import functools

import jax
import jax.numpy as jnp
import numpy as np
from jax import lax
from jax.experimental import pallas as pl
from jax.experimental.pallas import tpu as pltpu

F32, BF16 = jnp.float32, jnp.bfloat16
N_DEV = 8
D_MODEL = 1024
HEADS = 4
HEAD_DIM = 128
ROPE_DIM = 64
QK_PAD = 256
CHUNK = 64
ROPE_THETA = 10000.0
RMS_EPS = 1e-6
LN_EPS = 1e-5
ALPHA = 2.0 ** 0.25
ATT_SCALE = (HEAD_DIM + ROPE_DIM) ** -0.5
LN2 = float(np.log(2.0))
Q_PRESCALE = ATT_SCALE / LN2
ADAM_LR, ADAM_B1, ADAM_B2, ADAM_EPS, ADAM_WD, ADAM_STEP = 0.001, 0.9, 0.999, 1e-08, 0.01, 10
NEG_BIG = -1e30

ROW_TILE = 512
ROW_TILE_SMALL = 256
ATT_TILE = 512
HGRN_GROUP = 8
MLP_SLABS = 4
VMEM_LIMIT = 56 * 2 ** 20

NN = (((1,), (0,)), ((), ()))
NT = (((1,), (1,)), ((), ()))
TN = (((0,), (0,)), ((), ()))


def _dot(a, b, dims=NN):
    return lax.dot_general(a, b, dims, preferred_element_type=F32)


def _bdot(a, b, dims=NN):
    return lax.dot_general(a.astype(BF16), b.astype(BF16), dims, preferred_element_type=F32)


def _hdot(a, b, dims=NN):
    return lax.dot_general(a, b, dims, precision=lax.Precision.HIGHEST, preferred_element_type=F32)


def _params():
    return pltpu.CompilerParams(vmem_limit_bytes=VMEM_LIMIT)


def _sigmoid(x):
    return 1.0 / (1.0 + jnp.exp(-x))


def _rowsum(x):
    return jnp.sum(x, axis=0, keepdims=True)


def _lanemean(x):
    return jnp.mean(x, axis=-1, keepdims=True)


def _full(shape):
    nd = len(shape)
    return pl.BlockSpec(shape, lambda *_: (0,) * nd)


class _Exchange:
    def __init__(self, arrs, scatter):
        self.arrs, self.scatter, self.n, self.aliases = list(arrs), scatter, len(arrs), []
        self.out_shape = [jax.ShapeDtypeStruct((N_DEV,) + (a.shape[1:] if scatter else a.shape), a.dtype)
                          for a in self.arrs]
        n = self.n
        self.scratch = [pltpu.SemaphoreType.DMA((n, N_DEV - 1)), pltpu.SemaphoreType.DMA((n, N_DEV - 1)),
                        pltpu.SemaphoreType.DMA((n,))]

    def _copies(self, ins, outs, sems):
        send_sems, recv_sems, loc_sems = sems
        x, y, c = lax.axis_index("x"), lax.axis_index("y"), lax.axis_index("c")
        me = 4 * x + 2 * y + c
        copies = []
        for k in range(self.n):
            src_of = (lambda i, k=k: ins[k].at[i]) if self.scatter else (lambda i, k=k: ins[k])
            copies.append((pltpu.make_async_copy(src_of(me), outs[k].at[me], loc_sems.at[k]), None))
            for p in range(1, N_DEV):
                px = (1 - x) if p & 4 else x
                py = (1 - y) if p & 2 else y
                pc = (1 - c) if p & 1 else c
                peer = 4 * px + 2 * py + pc
                both = dict(send_sem=send_sems.at[k, p - 1], recv_sem=recv_sems.at[k, p - 1],
                            device_id=(px, py, pc), device_id_type=pl.DeviceIdType.MESH)
                send = pltpu.make_async_remote_copy(src_ref=src_of(peer), dst_ref=outs[k].at[me], **both)
                recv = pltpu.make_async_remote_copy(src_ref=src_of(peer), dst_ref=outs[k].at[peer], **both)
                copies.append((send, recv))
        return copies

    def start(self, ins, outs, sems):
        for first, _ in self._copies(ins, outs, sems):
            first.start()

    def middle(self, ins, outs, sems):
        pass

    def wait(self, ins, outs, sems):
        for first, recv in self._copies(ins, outs, sems):
            if recv is None:
                first.wait()
            else:
                recv.wait_recv()
                first.wait_send()


class _StagedGather:
    def __init__(self, arr):
        self.arrs, self.aliases = [arr], []
        self.out_shape = [jax.ShapeDtypeStruct((N_DEV,) + arr.shape, arr.dtype)]
        self.scratch = [pltpu.VMEM((N_DEV,) + arr.shape, arr.dtype), pltpu.SemaphoreType.DMA((7,)),
                        pltpu.SemaphoreType.DMA((7,)), pltpu.SemaphoreType.DMA((2,))]

    def _parts(self, scr):
        stage, send_sems, recv_sems, loc_sems = scr
        x, y, c = lax.axis_index("x"), lax.axis_index("y"), lax.axis_index("c")
        me, sibling = (x, y, c), (x, y, 1 - c)
        chips = [(1 - x, y), (x, 1 - y), (1 - x, 1 - y)]

        def copy(j, block, to):
            px, py, pc = block
            slot = stage.at[4 * px + 2 * py + pc]
            return pltpu.make_async_remote_copy(src_ref=slot, dst_ref=slot, send_sem=send_sems.at[j],
                                                recv_sem=recv_sems.at[j], device_id=to,
                                                device_id_type=pl.DeviceIdType.MESH)

        return stage, loc_sems, me, sibling, chips, c, copy

    def start(self, ins, outs, scr):
        stage, loc_sems, me, sibling, chips, c, copy = self._parts(scr)
        x, y, _ = me
        own = pltpu.make_async_copy(ins[0], stage.at[4 * x + 2 * y + c], loc_sems.at[0])
        own.start()
        own.wait()
        copy(0, me, sibling).start()
        for j, chip in enumerate(chips):
            copy(1 + j, me, (*chip, c)).start()

    def middle(self, ins, outs, scr):
        stage, loc_sems, me, sibling, chips, c, copy = self._parts(scr)
        for j, chip in enumerate(chips):
            copy(1 + j, (*chip, c), me).wait_recv()
            copy(4 + j, (*chip, c), sibling).start()

    def wait(self, ins, outs, scr):
        stage, loc_sems, me, sibling, chips, c, copy = self._parts(scr)
        copy(0, sibling, me).wait_recv()
        for j, chip in enumerate(chips):
            copy(4 + j, (*chip, 1 - c), me).wait_recv()
        copy(0, me, sibling).wait_send()
        for j, chip in enumerate(chips):
            copy(1 + j, me, (*chip, c)).wait_send()
            copy(4 + j, (*chip, c), sibling).wait_send()
        whole = pltpu.make_async_copy(stage, outs[0], loc_sems.at[1])
        whole.start()
        whole.wait()


def _call(body, name, args, out_shape, grid=(), in_specs=(), out_specs=(), scratch_shapes=(), exchange=None,
          middle_at=0.8):
    if exchange is None:
        return pl.pallas_call(body, name=name, grid=grid, in_specs=list(in_specs), out_specs=list(out_specs),
                              out_shape=list(out_shape), scratch_shapes=list(scratch_shapes),
                              compiler_params=_params())(*args), None
    exs = list(exchange) if isinstance(exchange, (list, tuple)) else [exchange]
    ni, no, ns = len(args), len(out_shape), len(scratch_shapes)
    nxi, nxo = sum(len(e.arrs) for e in exs), sum(len(e.out_shape) for e in exs)
    steps = int(np.prod(grid))
    mid_step = min(max(int(steps * middle_at), 1), steps - 1)
    aliases, iat, oat = {}, ni, no
    for e in exs:
        for src, dst in e.aliases:
            aliases[iat + src] = oat + dst
        iat, oat = iat + len(e.arrs), oat + len(e.out_shape)

    def wrapped(*refs):
        a, xi = refs[:ni], refs[ni:ni + nxi]
        o, xo = refs[ni + nxi:ni + nxi + no], refs[ni + nxi + no:ni + nxi + no + nxo]
        s, xs = refs[ni + nxi + no + nxo:ni + nxi + no + nxo + ns], refs[ni + nxi + no + nxo + ns:]
        parts, iat, oat, sat = [], 0, 0, 0
        for e in exs:
            parts.append((e, xi[iat:iat + len(e.arrs)], xo[oat:oat + len(e.out_shape)], xs[sat:sat + len(e.scratch)]))
            iat, oat, sat = iat + len(e.arrs), oat + len(e.out_shape), sat + len(e.scratch)
        step = 0
        for d, g in enumerate(grid):
            step = step * g + pl.program_id(d)

        @pl.when(step == 0)
        def _():
            for e, ins, outs, sems in parts:
                e.start(ins, outs, sems)

        @pl.when(step == mid_step)
        def _():
            for e, ins, outs, sems in parts:
                e.middle(ins, outs, sems)

        body(*a, *o, *s)

        @pl.when(step == steps - 1)
        def _():
            for e, ins, outs, sems in parts:
                e.wait(ins, outs, sems)

    hbm = pl.BlockSpec(memory_space=pltpu.HBM)
    res = pl.pallas_call(
        wrapped, name=name, grid=grid, in_specs=list(in_specs) + [hbm] * nxi, out_specs=list(out_specs) + [hbm] * nxo,
        out_shape=list(out_shape) + [o_ for e in exs for o_ in e.out_shape],
        scratch_shapes=list(scratch_shapes) + [s_ for e in exs for s_ in e.scratch],
        input_output_aliases=aliases, compiler_params=_params())(*args, *[a_ for e in exs for a_ in e.arrs])
    return res[:no], res[no:]


def _gather_two_level(arrs, name):
    n = len(arrs)
    out_shape = [jax.ShapeDtypeStruct((N_DEV,) + a.shape, a.dtype) for a in arrs]

    def body(*refs):
        ins, outs = refs[:n], refs[n:2 * n]
        send_sems, recv_sems, loc_sems = refs[2 * n:]
        x, y, c = lax.axis_index("x"), lax.axis_index("y"), lax.axis_index("c")
        me, sibling = (x, y, c), (x, y, 1 - c)
        chips = [(1 - x, y), (x, 1 - y), (1 - x, 1 - y)]

        def copy(k, j, block, to, src=None):
            px, py, pc = block
            dst = outs[k].at[4 * px + 2 * py + pc]
            return pltpu.make_async_remote_copy(src_ref=dst if src is None else src, dst_ref=dst,
                                                send_sem=send_sems.at[k, j], recv_sem=recv_sems.at[k, j],
                                                device_id=to, device_id_type=pl.DeviceIdType.MESH)

        mine = [pltpu.make_async_copy(ins[k], outs[k].at[4 * x + 2 * y + c], loc_sems.at[k]) for k in range(n)]
        first = []
        for k in range(n):
            mine[k].start()
            first.append(copy(k, 0, me, sibling, src=ins[k]))
            first += [copy(k, 1 + j, me, (*chip, c), src=ins[k]) for j, chip in enumerate(chips)]
        for cp in first:
            cp.start()
        passed = []
        for j, chip in enumerate(chips):
            for k in range(n):
                copy(k, 1 + j, (*chip, c), me).wait_recv()
                passed.append(copy(k, 4 + j, (*chip, c), sibling))
                passed[-1].start()
        for k in range(n):
            copy(k, 0, sibling, me).wait_recv()
            for j, chip in enumerate(chips):
                copy(k, 4 + j, (*chip, 1 - c), me).wait_recv()
        for cp in first + passed:
            cp.wait_send()
        for cp in mine:
            cp.wait()

    vmem = pl.BlockSpec(memory_space=pltpu.VMEM)
    return pl.pallas_call(body, name=name, out_shape=out_shape, in_specs=[vmem] * n, out_specs=[vmem] * n,
                          scratch_shapes=[pltpu.SemaphoreType.DMA((n, 7)), pltpu.SemaphoreType.DMA((n, 7)),
                                          pltpu.SemaphoreType.DMA((n,))], compiler_params=_params())(*arrs)


def _exchange(arrs, scatter, name):
    ex = _Exchange(arrs, scatter)

    def body(*refs):
        ins, outs, sems = refs[:ex.n], refs[ex.n:2 * ex.n], refs[2 * ex.n:]
        ex.start(ins, outs, sems)
        ex.wait(ins, outs, sems)

    hbm = pl.BlockSpec(memory_space=pltpu.HBM)
    return pl.pallas_call(body, name=name, out_shape=ex.out_shape, in_specs=[hbm] * ex.n, out_specs=[hbm] * ex.n,
                          scratch_shapes=ex.scratch)(*ex.arrs)


def _matmul(a, b, mode, name, out_dtype=F32, tm=512, tn=1024, tk=1024, a_fn=None, b_fn=None, extras=(),
            out_slabs=None, exchange=None):
    assert not (a_fn and b_fn) and not (b_fn and mode == "NT")
    if mode == "NN":
        (M, K), N = a.shape, b.shape[1]
    elif mode == "NT":
        (M, K), N = a.shape, b.shape[0]
    else:
        (K, M), N = a.shape, b.shape[1]
    if out_slabs:
        tn = N // out_slabs
    tm, tn, tk = min(tm, M), min(tn, N), min(tk, K)
    assert M % tm == 0 and N % tn == 0 and K % tk == 0, (name, M, N, K)
    nk = K // tk
    dims = {"NN": NN, "NT": NT, "TN": TN}[mode]
    ne = len(extras)

    def body(a_ref, b_ref, *rest):
        e_refs, o_ref, acc_ref = rest[:ne], rest[ne], rest[ne + 1]
        k = pl.program_id(2)

        @pl.when(k == 0)
        def _():
            acc_ref[...] = jnp.zeros_like(acc_ref)

        at, bt = a_ref[...], b_ref[...]
        if a_fn is not None:
            at = a_fn(at.astype(F32), *[e[...] for e in e_refs])
        if b_fn is not None:
            bt = b_fn(bt.astype(F32), *[e[...] for e in e_refs])
        acc_ref[...] += _bdot(at, bt, dims)

        @pl.when(k == nk - 1)
        def _():
            o_ref[...] = acc_ref[...].astype(out_dtype)

    if mode == "TN":
        a_spec = pl.BlockSpec((tk, tm), lambda i, j, k: (k, i))
        e_spec = pl.BlockSpec((1, tm), lambda i, j, k: (0, i))
    else:
        a_spec = pl.BlockSpec((tm, tk), lambda i, j, k: (i, k))
        e_spec = pl.BlockSpec((1, tk), lambda i, j, k: (0, k))
    if mode == "NT":
        b_spec = pl.BlockSpec((tn, tk), lambda i, j, k: (j, k))
    else:
        b_spec = pl.BlockSpec((tk, tn), lambda i, j, k: (k, j))
    if b_fn is not None:
        e_spec = pl.BlockSpec((1, tn), lambda i, j, k: (0, j))
    if out_slabs:
        o_shape = jax.ShapeDtypeStruct((out_slabs, M, tn), out_dtype)
        o_spec = pl.BlockSpec((None, tm, tn), lambda i, j, k: (j, i, 0))
    else:
        o_shape = jax.ShapeDtypeStruct((M, N), out_dtype)
        o_spec = pl.BlockSpec((tm, tn), lambda i, j, k: (i, j))
    (out,), got = _call(body, name, (a, b, *extras), [o_shape], grid=(M // tm, N // tn, nk),
                        in_specs=[a_spec, b_spec] + [e_spec] * ne, out_specs=[o_spec],
                        scratch_shapes=[pltpu.VMEM((tm, tn), F32)], exchange=exchange)
    return out if exchange is None else (out, got)


def _modulate(x, sc, sh):
    return x * (1.0 + sc) + sh


def _square(x):
    return x * x


def _mod_part(c_all, w_ada_s, b_s):
    def body(c_ref, w_ref, b_ref, mod_ref, cond_ref):
        cv = c_ref[...]
        cond = cv * _sigmoid(cv)
        cond_ref[...] = cond
        mod_ref[...] = _bdot(cond, w_ref[...]) + b_ref[...]

    return pl.pallas_call(
        body, name="mod_part",
        out_shape=[jax.ShapeDtypeStruct((N_DEV, w_ada_s.shape[1]), F32), jax.ShapeDtypeStruct(c_all.shape, F32)],
        compiler_params=_params(),
    )(c_all, w_ada_s, b_s)


def _rms_fwd(x, w):
    rs = lax.rsqrt(_lanemean(x * x) + RMS_EPS)
    return x * rs * w, rs


def _rms_bwd(x, rs, w, dy):
    xhat = x * rs
    dxh = dy * w
    return rs * (dxh - xhat * _lanemean(dxh * xhat)), dy * xhat


def _mla_pre(z, pos_col, invf, m_one, m_rot, wq_ext, wkv_ext, qnw, kvnw, exchange=None):
    T = z.shape[0]
    tm = min(ROW_TILE, T)

    def body(z_ref, pos_ref, invf_ref, mone_ref, mrot_ref, wq_ref, wkv_ref, qnw_ref, kvnw_ref,
             q_ref, k_ref, v_ref, c1_ref, s1_ref, cqn_ref, ckvn_ref):
        ang = pos_ref[...].astype(F32) * invf_ref[...]
        c1 = mone_ref[...] + mrot_ref[...] * jnp.cos(ang)
        s1 = mrot_ref[...] * jnp.sin(ang)
        c1_ref[...] = c1
        s1_ref[...] = s1
        cqn, _ = _rms_fwd(z_ref[:, 0:256], qnw_ref[...])
        ckvn, _ = _rms_fwd(z_ref[:, 256:512], kvnw_ref[...])
        cqn_ref[...] = cqn.astype(BF16)
        ckvn_ref[...] = ckvn.astype(BF16)
        qe = _bdot(cqn, wq_ref[...], NT)
        kve = _bdot(ckvn, wkv_ref[...])
        k_rope = z_ref[:, 512:768] * c1 + z_ref[:, 768:1024] * s1
        for h in range(HEADS):
            q_ref[h] = ((qe[:, 256 * h:256 * h + 256] * c1 + qe[:, 1024 + 256 * h:1280 + 256 * h] * s1)
                        * Q_PRESCALE).astype(BF16)
            k_ref[h] = (kve[:, 256 * h:256 * h + 256] + k_rope).astype(BF16)
            v_ref[h] = kve[:, 1024 + 128 * h:1152 + 128 * h].astype(BF16)

    row = lambda i: (i, 0)
    head = lambda i: (0, i, 0)
    return _call(
        body, "mla_pre", (z, pos_col, invf, m_one, m_rot, wq_ext, wkv_ext, qnw, kvnw), grid=(T // tm,),
        in_specs=[pl.BlockSpec((tm, 1024), lambda i: (i, 2)), pl.BlockSpec((tm, 1), row),
                  _full((1, 256)), _full((1, 256)), _full((1, 256)), _full(wq_ext.shape), _full(wkv_ext.shape),
                  _full((1, 256)), _full((1, 256))],
        out_specs=[pl.BlockSpec((HEADS, tm, QK_PAD), head), pl.BlockSpec((HEADS, tm, QK_PAD), head),
                   pl.BlockSpec((HEADS, tm, HEAD_DIM), head), pl.BlockSpec((tm, 256), row), pl.BlockSpec((tm, 256), row),
                   pl.BlockSpec((tm, 256), row), pl.BlockSpec((tm, 256), row)],
        out_shape=[jax.ShapeDtypeStruct((HEADS, T, QK_PAD), BF16), jax.ShapeDtypeStruct((HEADS, T, QK_PAD), BF16),
                   jax.ShapeDtypeStruct((HEADS, T, HEAD_DIM), BF16), jax.ShapeDtypeStruct((T, 256), F32),
                   jax.ShapeDtypeStruct((T, 256), F32), jax.ShapeDtypeStruct((T, 256), BF16),
                   jax.ShapeDtypeStruct((T, 256), BF16)], exchange=exchange)


def _mla_bwd(dq, dk, dv, z, c1, s1, wq_ext, wkv_ext, qnw, kvnw):
    T = z.shape[0]
    tm = min(ROW_TILE_SMALL, T)

    def body(dq_ref, dk_ref, dv_ref, z_ref, c1_ref, s1_ref, wq_ref, wkv_ref, qnw_ref, kvnw_ref,
             dz_ref, dqe_ref, dkve_ref, dnw_ref):
        @pl.when(pl.program_id(0) == 0)
        def _():
            dnw_ref[...] = jnp.zeros_like(dnw_ref)

        c1, s1 = c1_ref[...], s1_ref[...]
        dkpe = jnp.zeros((tm, QK_PAD), F32)
        for h in range(HEADS):
            dqh, dkh = dq_ref[h] * Q_PRESCALE, dk_ref[h]
            dqe_ref[:, 256 * h:256 * h + 256] = (dqh * c1).astype(BF16)
            dqe_ref[:, 1024 + 256 * h:1280 + 256 * h] = (dqh * s1).astype(BF16)
            dkve_ref[:, 256 * h:256 * h + 256] = dkh.astype(BF16)
            dkve_ref[:, 1024 + 128 * h:1152 + 128 * h] = dv_ref[h].astype(BF16)
            dkpe = dkpe + dkh
        dcqn = _dot(dqe_ref[...], wq_ref[...])
        dckvn = _dot(dkve_ref[...], wkv_ref[...], NT)
        cq, ckv = z_ref[:, 0:256], z_ref[:, 256:512]
        _, rsq = _rms_fwd(cq, qnw_ref[...])
        _, rskv = _rms_fwd(ckv, kvnw_ref[...])
        dcq, wq_rows = _rms_bwd(cq, rsq, qnw_ref[...], dcqn)
        dckv, wkv_rows = _rms_bwd(ckv, rskv, kvnw_ref[...], dckvn)
        dnw_ref[:, 0:256] += _rowsum(wq_rows)
        dnw_ref[:, 256:512] += _rowsum(wkv_rows)
        dz_ref[:, 0:256] = dcq
        dz_ref[:, 256:512] = dckv
        dz_ref[:, 512:768] = dkpe * c1
        dz_ref[:, 768:1024] = dkpe * s1

    row = lambda i: (i, 0)
    head = lambda i: (0, i, 0)
    return pl.pallas_call(
        body, name="mla_bwd", grid=(T // tm,),
        in_specs=[pl.BlockSpec((HEADS, tm, QK_PAD), head), pl.BlockSpec((HEADS, tm, QK_PAD), head),
                  pl.BlockSpec((HEADS, tm, HEAD_DIM), head), pl.BlockSpec((tm, 1024), lambda i: (i, 2)),
                  pl.BlockSpec((tm, 256), row), pl.BlockSpec((tm, 256), row), _full(wq_ext.shape), _full(wkv_ext.shape),
                  _full((1, 256)), _full((1, 256))],
        out_specs=[pl.BlockSpec((tm, 1024), row), pl.BlockSpec((tm, 2048), row), pl.BlockSpec((tm, 1536), row),
                   _full((1, 512))],
        out_shape=[jax.ShapeDtypeStruct((T, 1024), F32), jax.ShapeDtypeStruct((T, 2048), BF16),
                   jax.ShapeDtypeStruct((T, 1536), BF16), jax.ShapeDtypeStruct((1, 512), F32)],
        compiler_params=_params(),
    )(dq, dk, dv, z, c1, s1, wq_ext, wkv_ext, qnw, kvnw)


_HEAD_LANES = [slice(HEAD_DIM * h, HEAD_DIM * (h + 1)) for h in range(HEADS)]


def _lower_bound(lbraw_ref):
    a0, a1 = lbraw_ref[0:1, :], lbraw_ref[1:2, :]
    mx = jnp.maximum(a0, a1)
    e0, e1 = jnp.exp(a0 - mx), jnp.exp(a1 - mx)
    return e0 / (e0 + e1)


def _tri(lower):
    r = lax.broadcasted_iota(jnp.int32, (CHUNK, CHUNK), 0)
    c = lax.broadcasted_iota(jnp.int32, (CHUNK, CHUNK), 1)
    return (r >= c) if lower else (r <= c)


def _hgrn_gates(q, f, lb, tri_lo):
    sg = _sigmoid(f)
    forget = lb + (1.0 - lb) * sg
    k = 1.0 - forget
    b = _hdot(tri_lo.astype(F32), jnp.log(forget))
    b_ref, b_last = b[CHUNK // 2 - 1:CHUNK // 2, :], b[CHUNK - 1:CHUNK, :]
    e1, e2, e3, e4 = jnp.exp(b - b_ref), jnp.exp(b_ref - b), jnp.exp(b_last - b), jnp.exp(b)
    return dict(sg=sg, forget=forget, k=k, e1=e1, e2=e2, e3=e3, e4=e4, qa=q * e1, ka=k * e2, kl=k * e3, qb=q * e4,
                decay=jnp.exp(b_last))


def _hgrn_fwd(z, lbraw, nw, exchange=None):
    T = z.shape[0]
    G = min(HGRN_GROUP, T // CHUNK)
    rows = G * CHUNK
    n_chunks = T // CHUNK

    def body(q_ref, f_ref, i_ref, g_ref, lbraw_ref, nw_ref, oraw_ref, og_ref, sp_ref, st_ref):
        @pl.when(pl.program_id(0) == 0)
        def _():
            st_ref[...] = jnp.zeros_like(st_ref)

        lb_all = _lower_bound(lbraw_ref)
        tri_lo = _tri(True)

        def chunk(cc, carry):
            rs = pl.ds(pl.multiple_of(cc * CHUNK, CHUNK), CHUNK)
            t = _hgrn_gates(q_ref[rs, :], f_ref[rs, :], lb_all, tri_lo)
            v, gate = i_ref[rs, :], g_ref[rs, :]
            st = [st_ref[h] for h in range(HEADS)]
            a = [jnp.where(tri_lo, _bdot(t["qa"][:, s], t["ka"][:, s], NT), 0.0) for s in _HEAD_LANES]
            kv = [_bdot(v[:, s], t["kl"][:, s], TN) for s in _HEAD_LANES]
            o = [_bdot(a[h], v[:, s]) + _bdot(t["qb"][:, s], st[h], NT) for h, s in enumerate(_HEAD_LANES)]
            for h, s in enumerate(_HEAD_LANES):
                sp_ref[cc, h] = st[h]
                st_ref[h] = st[h] * t["decay"][:, s] + kv[h]
            oraw_ref[rs, :] = jnp.concatenate(o, axis=1)
            on = jnp.concatenate([_rms_fwd(o[h], nw_ref[:, s])[0] for h, s in enumerate(_HEAD_LANES)], axis=1)
            og_ref[rs, :] = (on * (gate * _sigmoid(gate))).astype(BF16)
            return carry

        lax.fori_loop(0, G, chunk, 0, unroll=4)

    col = lambda j: pl.BlockSpec((rows, 512), lambda r, j=j: (r, j))
    return _call(
        body, "hgrn_fwd", (z, z, z, z, lbraw, nw), grid=(T // rows,),
        in_specs=[col(0), col(1), col(2), col(3), _full((2, 512)), _full((1, 512))],
        out_specs=[col(0), col(0), pl.BlockSpec((G, HEADS, HEAD_DIM, HEAD_DIM), lambda r: (r, 0, 0, 0))],
        out_shape=[jax.ShapeDtypeStruct((T, 512), F32), jax.ShapeDtypeStruct((T, 512), BF16),
                   jax.ShapeDtypeStruct((n_chunks, HEADS, HEAD_DIM, HEAD_DIM), F32)],
        scratch_shapes=[pltpu.VMEM((HEADS, HEAD_DIM, HEAD_DIM), F32)], exchange=exchange)


def _hgrn_bwd(dmixcat, z, oraw, sprev, lbraw, nw, exchange=None):
    T = z.shape[0]
    G = min(HGRN_GROUP, T // CHUNK)
    rows = G * CHUNK
    ng = T // rows

    def body(dog_ref, q_ref, f_ref, i_ref, g_ref, oraw_ref, sp_ref, lbraw_ref, nw_ref,
             dz_ref, dsmall_ref, dst_ref):
        @pl.when(pl.program_id(0) == 0)
        def _():
            dst_ref[...] = jnp.zeros_like(dst_ref)
            dsmall_ref[...] = jnp.zeros_like(dsmall_ref)

        lb_all = _lower_bound(lbraw_ref)
        tri_lo, tri_up = _tri(True), _tri(False)
        rowid = lax.broadcasted_iota(jnp.int32, (CHUNK, HEADS * HEAD_DIM), 0)

        def chunk(it, carry):
            cc = G - 1 - it
            rs = pl.ds(pl.multiple_of(cc * CHUNK, CHUNK), CHUNK)
            heads = list(enumerate(_HEAD_LANES))
            cat = lambda parts: jnp.concatenate(parts, axis=1)
            per_head_mean = lambda x: cat([jnp.broadcast_to(_lanemean(x[:, s]), (CHUNK, HEAD_DIM)) for s in _HEAD_LANES])
            t = _hgrn_gates(q_ref[rs, :], f_ref[rs, :], lb_all, tri_lo)
            v, gate, o, dog, nw_all = i_ref[rs, :], g_ref[rs, :], oraw_ref[rs, :], dog_ref[rs, :], nw_ref[...]
            rs_o = lax.rsqrt(per_head_mean(o * o) + RMS_EPS)
            xhat = o * rs_o
            sgg = _sigmoid(gate)
            d_on = dog * (gate * sgg)
            dz_ref[rs, 1536:2048] = dog * (xhat * nw_all) * (sgg * (1.0 + gate * (1.0 - sgg)))
            dxh = d_on * nw_all
            do = rs_o * (dxh - xhat * per_head_mean(dxh * xhat))
            dsmall_ref[:, 512:1024] += _rowsum(d_on * xhat)
            st = [sp_ref[cc, h] for h in range(HEADS)]
            dst = [dst_ref[h] for h in range(HEADS)]
            a = [jnp.where(tri_lo, _bdot(t["qa"][:, s], t["ka"][:, s], NT), 0.0) for s in _HEAD_LANES]
            da = [jnp.where(tri_lo, _bdot(do[:, s], v[:, s], NT), 0.0) for s in _HEAD_LANES]
            dqb = cat([_bdot(do[:, s], st[h]) for h, s in heads])
            dkl = cat([_bdot(v[:, s], dst[h]) for h, s in heads])
            dv_ = cat([_bdot(t["kl"][:, s], dst[h], NT) + _bdot(a[h], do[:, s], TN) for h, s in heads])
            dqa = cat([_bdot(da[h], t["ka"][:, s]) for h, s in heads])
            dka = cat([_bdot(da[h], t["qa"][:, s], TN) for h, s in heads])
            ddecay = cat([_rowsum(dst[h] * st[h]) for h in range(HEADS)])
            for h, s in heads:
                dst_ref[h] = dst[h] * t["decay"][:, s] + _bdot(do[:, s], t["qb"][:, s], TN)
            pa, pk, pb, pl_ = dqa * t["qa"], dka * t["ka"], dqb * t["qb"], dkl * t["kl"]
            db = pa - pk + pb - pl_
            db = db + jnp.where(rowid == CHUNK // 2 - 1, _rowsum(pk - pa), 0.0)
            db = db + jnp.where(rowid == CHUNK - 1, _rowsum(pl_) + ddecay * t["decay"], 0.0)
            dlogf = _hdot(tri_up.astype(F32), db)
            dforget = dlogf / t["forget"] - (dka * t["e2"] + dkl * t["e3"])
            sg = t["sg"]
            dz_ref[rs, 0:512] = dqa * t["e1"] + dqb * t["e4"]
            dz_ref[rs, 512:1024] = dforget * (1.0 - lb_all) * sg * (1.0 - sg)
            dz_ref[rs, 1024:1536] = dv_
            dsmall_ref[:, 0:512] += _rowsum(dforget * (1.0 - sg))
            return carry

        lax.fori_loop(0, G, chunk, 0, unroll=4)

    col = lambda j: pl.BlockSpec((rows, 512), lambda r, j=j: (ng - 1 - r, j))
    return _call(
        body, "hgrn_bwd", (dmixcat, z, z, z, z, oraw, sprev, lbraw, nw), grid=(ng,),
        in_specs=[col(0), col(0), col(1), col(2), col(3), col(0),
                  pl.BlockSpec((G, HEADS, HEAD_DIM, HEAD_DIM), lambda r: (ng - 1 - r, 0, 0, 0)),
                  _full((2, 512)), _full((1, 512))],
        out_specs=[pl.BlockSpec((rows, 2048), lambda r: (ng - 1 - r, 0)), _full((1, 1024))],
        out_shape=[jax.ShapeDtypeStruct((T, 2048), F32), jax.ShapeDtypeStruct((1, 1024), F32)],
        scratch_shapes=[pltpu.VMEM((HEADS, HEAD_DIM, HEAD_DIM), F32)], exchange=exchange)


def _diag_mask(t):
    r = lax.broadcasted_iota(jnp.int32, (t, t), 0)
    c = lax.broadcasted_iota(jnp.int32, (t, t), 1)
    return r >= c


def _attn_fwd(q, k, v, exchange=None):
    _, T, _ = q.shape
    t = min(ATT_TILE, T)

    def body(q_ref, k_ref, v_ref, o_ref, lse_ref):
        i = pl.program_id(1)
        qb = q_ref[...]

        rows = lambda j: pl.ds(pl.multiple_of(j * t, t), t)

        def logits(j, masked):
            s = _dot(qb, k_ref[rows(j), :], NT)
            return jnp.where(_diag_mask(t), s, NEG_BIG) if masked else s

        def absorb(s, j, carry):
            m, l, acc = carry
            mn = jnp.maximum(m, jnp.max(s, axis=-1, keepdims=True))
            p = jnp.exp2(s - mn)
            al = jnp.exp2(m - mn)
            return mn, al * l + jnp.sum(p, axis=-1, keepdims=True), al * acc + _dot(p.astype(BF16), v_ref[rows(j), :])

        def pair(j0, carry, last_masked):
            s0, s1 = logits(j0, False), logits(j0 + 1, last_masked)
            return absorb(s1, j0 + 1, absorb(s0, j0, carry))

        init = (jnp.full((t, 1), NEG_BIG, F32), jnp.zeros((t, 1), F32), jnp.zeros((t, HEAD_DIM), F32))
        carry = lax.fori_loop(0, i // 2, lambda jj, c: pair(2 * jj, c, False), init)
        m, l, acc = lax.cond(i % 2 == 1, lambda c: pair(i - 1, c, True),
                             lambda c: absorb(logits(i, True), i, c), carry)
        o_ref[...] = acc / l
        lse_ref[...] = jnp.broadcast_to(m + jnp.log2(l), (t, HEAD_DIM))

    return _call(
        body, "attn_fwd", (q, k, v), grid=(HEADS, T // t),
        in_specs=[pl.BlockSpec((None, t, QK_PAD), lambda h, i: (h, i, 0)),
                  pl.BlockSpec((None, T, QK_PAD), lambda h, i: (h, 0, 0)),
                  pl.BlockSpec((None, T, HEAD_DIM), lambda h, i: (h, 0, 0))],
        out_specs=[pl.BlockSpec((t, HEAD_DIM), lambda h, i: (i, h)),
                   pl.BlockSpec((None, t, HEAD_DIM), lambda h, i: (h, i, 0))],
        out_shape=[jax.ShapeDtypeStruct((T, HEADS * HEAD_DIM), F32), jax.ShapeDtypeStruct((HEADS, T, HEAD_DIM), F32)],
        exchange=exchange)


def _attn_bwd(q, k, v, dmixcat, o, lse, exchange=None):
    _, T, _ = q.shape
    t = min(ATT_TILE, T)
    nq = T // t

    def body(q_ref, k_ref, v_ref, do_ref, o_ref, lse_ref, dq_ref, dk_ref, dv_ref, delta_ref):
        j = pl.program_id(1)

        @pl.when(j == 0)
        def _():
            dq_ref[...] = jnp.zeros_like(dq_ref)

            def fill(i, carry):
                rs = pl.ds(pl.multiple_of(i * t, t), t)
                delta_ref[rs, :] = jnp.broadcast_to(
                    jnp.sum(do_ref[rs, :] * o_ref[rs, :], axis=-1, keepdims=True), (t, HEAD_DIM))
                return carry

            lax.fori_loop(0, nq, fill, 0)

        kb, vb = k_ref[...], v_ref[...]

        def steps(blocks, carry):
            dk, dv = carry
            rs = [pl.ds(pl.multiple_of(i * t, t), t) for i, _ in blocks]
            qb = [q_ref[r, :] for r in rs]
            dob = [do_ref[r, :].astype(BF16) for r in rs]
            s = [_dot(b, kb, NT) for b in qb]
            dp = [_dot(b, vb, NT) for b in dob]
            for n, (_, masked) in enumerate(blocks):
                p = jnp.exp2(s[n] - lse_ref[rs[n], 0:1])
                if masked:
                    p = jnp.where(_diag_mask(t), p, 0.0)
                ds = (p * (dp[n] - delta_ref[rs[n], 0:1]) * LN2).astype(BF16)
                dq_ref[rs[n], :] += _dot(ds, kb)
                dk = dk + _dot(ds, qb[n], TN)
                dv = dv + _dot(p.astype(BF16), dob[n], TN)
            return dk, dv

        zero = (jnp.zeros((t, QK_PAD), F32), jnp.zeros((t, HEAD_DIM), F32))
        rest = nq - 1 - j
        carry = lax.cond(rest % 2 == 1, lambda c: steps([(j, True), (j + 1, False)], c),
                         lambda c: steps([(j, True)], c), zero)
        first = j + 1 + rest % 2
        dk, dv = lax.fori_loop(0, rest // 2, lambda n, c: steps([(first + 2 * n, False), (first + 2 * n + 1, False)], c),
                               carry)
        dk_ref[...] = dk
        dv_ref[...] = dv

    return _call(
        body, "attn_bwd", (q, k, v, dmixcat, o, lse), grid=(HEADS, nq),
        in_specs=[pl.BlockSpec((None, T, QK_PAD), lambda h, j: (h, 0, 0)),
                  pl.BlockSpec((None, t, QK_PAD), lambda h, j: (h, j, 0)),
                  pl.BlockSpec((None, t, HEAD_DIM), lambda h, j: (h, j, 0)),
                  pl.BlockSpec((T, HEAD_DIM), lambda h, j: (0, HEADS + h)),
                  pl.BlockSpec((T, HEAD_DIM), lambda h, j: (0, h)),
                  pl.BlockSpec((None, T, HEAD_DIM), lambda h, j: (h, 0, 0))],
        out_specs=[pl.BlockSpec((None, T, QK_PAD), lambda h, j: (h, 0, 0)),
                   pl.BlockSpec((None, t, QK_PAD), lambda h, j: (h, j, 0)),
                   pl.BlockSpec((None, t, HEAD_DIM), lambda h, j: (h, j, 0))],
        out_shape=[jax.ShapeDtypeStruct((HEADS, T, QK_PAD), F32), jax.ShapeDtypeStruct((HEADS, T, QK_PAD), F32),
                   jax.ShapeDtypeStruct((HEADS, T, HEAD_DIM), F32)],
        scratch_shapes=[pltpu.VMEM((T, HEAD_DIM), F32)], exchange=exchange)


def _ln_fwd(r):
    mu = _lanemean(r)
    xc = r - mu
    rstd = lax.rsqrt(_lanemean(xc * xc) + LN_EPS)
    return xc * rstd, rstd


def _ln_bwd(dxh, xhat, rstd):
    return rstd * (dxh - _lanemean(dxh) - xhat * _lanemean(dxh * xhat))


def _mix_ln1(o_hg, o_mla, w_out, x, g_a, ln1_g, ln1_b, sc_m, sh_m, exchange=None):
    T = x.shape[0]
    tm = min(ROW_TILE_SMALL, T)
    half = o_hg.shape[1]

    def body(hg_ref, mla_ref, w_ref, x_ref, ga_ref, g_ref, b_ref, sc_ref, sh_ref, mix_ref, xhat_ref, rstd_ref, u2_ref):
        mix = _dot(hg_ref[...], w_ref[0:half, :]) + _bdot(mla_ref[...], w_ref[half:, :])
        mix_ref[...] = mix
        xhat, rstd = _ln_fwd(ALPHA * x_ref[...] + (1.0 + ga_ref[...]) * mix)
        xhat_ref[...] = xhat
        rstd_ref[...] = jnp.broadcast_to(rstd, (tm, 128))
        u2_ref[...] = _modulate(xhat * g_ref[...] + b_ref[...], sc_ref[...], sh_ref[...]).astype(BF16)

    row = pl.BlockSpec((tm, D_MODEL), lambda i: (i, 0))
    vec = _full((1, D_MODEL))
    halfrow = pl.BlockSpec((tm, half), lambda i: (i, 0))
    return _call(
        body, "mix_ln1", (o_hg, o_mla, w_out, x, g_a, ln1_g, ln1_b, sc_m, sh_m), grid=(T // tm,),
        in_specs=[halfrow, halfrow, _full(w_out.shape), row, vec, vec, vec, vec, vec],
        out_specs=[row, row, pl.BlockSpec((tm, 128), lambda i: (i, 0)), row],
        out_shape=[jax.ShapeDtypeStruct((T, D_MODEL), F32), jax.ShapeDtypeStruct((T, D_MODEL), F32),
                   jax.ShapeDtypeStruct((T, 128), F32), jax.ShapeDtypeStruct((T, D_MODEL), BF16)],
        exchange=exchange)


def _mlp_fwd(u2, w1, w2, xhat1, ln1_g, ln1_b, g_m, ln2_g, ln2_b, target):
    T = u2.shape[0]
    tf = w1.shape[-1]
    nf = N_DEV // MLP_SLABS
    tm = min(ROW_TILE, T)

    def body(u2_ref, w1_ref, w2_ref, xhat_ref, g1_ref, b1_ref, gm_ref, g2_ref, b2_ref, tgt_ref,
             r_ref, dr2_ref, dh_ref, small_ref, acc_ref):
        i, f = pl.program_id(0), pl.program_id(1)
        dm = D_MODEL

        @pl.when((i == 0) & (f == 0))
        def _():
            small_ref[...] = jnp.zeros_like(small_ref)

        @pl.when(f == 0)
        def _():
            acc_ref[...] = jnp.zeros_like(acc_ref)

        u2t = u2_ref[...]
        part = None
        for s in range(MLP_SLABS):
            r = jnp.maximum(_dot(u2t, w1_ref[s]), 0.0)
            r_ref[:, s * tf:(s + 1) * tf] = r.astype(BF16)
            d = _bdot(r * r, w2_ref[s])
            part = d if part is None else part + d
        acc_ref[...] += part

        @pl.when(f == nf - 1)
        def _():
            h = acc_ref[...]
            x1 = xhat_ref[...] * g1_ref[...] + b1_ref[...]
            xhat2, rstd2 = _ln_fwd(ALPHA * x1 + (1.0 + gm_ref[...]) * h)
            err = xhat2 * g2_ref[...] + b2_ref[...] - tgt_ref[...]
            small_ref[:, 3 * dm:] += jnp.sum(0.5 * _lanemean(err * err), axis=0, keepdims=True)
            dy = err * (1.0 / D_MODEL)
            small_ref[:, dm:2 * dm] += _rowsum(dy * xhat2)
            small_ref[:, 2 * dm:3 * dm] += _rowsum(dy)
            dr2 = _ln_bwd(dy * g2_ref[...], xhat2, rstd2)
            dr2_ref[...] = dr2
            small_ref[:, 0:dm] += _rowsum(dr2 * h)
            dh_ref[...] = ((1.0 + gm_ref[...]) * dr2).astype(BF16)

    row = pl.BlockSpec((tm, D_MODEL), lambda i, f: (i, 0))
    vec = _full((1, D_MODEL))
    return pl.pallas_call(
        body, name="mlp_fwd", grid=(T // tm, nf),
        in_specs=[row, pl.BlockSpec((MLP_SLABS, D_MODEL, tf), lambda i, f: (f, 0, 0)),
                  pl.BlockSpec((MLP_SLABS, tf, D_MODEL), lambda i, f: (f, 0, 0)),
                  row, vec, vec, vec, vec, vec, row],
        out_specs=[pl.BlockSpec((tm, MLP_SLABS * tf), lambda i, f: (i, f)), row, row, _full((1, 3 * D_MODEL + 128))],
        out_shape=[jax.ShapeDtypeStruct((T, N_DEV * tf), BF16), jax.ShapeDtypeStruct((T, D_MODEL), F32),
                   jax.ShapeDtypeStruct((T, D_MODEL), BF16), jax.ShapeDtypeStruct((1, 3 * D_MODEL + 128), F32)],
        scratch_shapes=[pltpu.VMEM((tm, D_MODEL), F32)],
        compiler_params=_params(),
    )(u2, w1, w2, xhat1, ln1_g, ln1_b, g_m, ln2_g, ln2_b, target)


def _mlp_bwd(dh, w1, w2, r, dr2, xhat1, rstd1, mix, ln1_g, ln1_b, sc_m, g_a):
    T = dh.shape[0]
    tf = w1.shape[-1]
    nf = N_DEV // MLP_SLABS
    tm = min(ROW_TILE, T)

    def body(dh_ref, w1_ref, w2_ref, r_ref, dr2_ref, xhat_ref, rstd_ref, mix_ref, g1_ref, b1_ref, sc_ref, ga_ref,
             dhpre_ref, dr1_ref, dmix_ref, small_ref, acc_ref):
        i, f = pl.program_id(0), pl.program_id(1)
        dm = D_MODEL

        @pl.when((i == 0) & (f == 0))
        def _():
            small_ref[...] = jnp.zeros_like(small_ref)

        @pl.when(f == 0)
        def _():
            acc_ref[...] = jnp.zeros_like(acc_ref)

        dht = dh_ref[...]
        part = None
        for s in range(MLP_SLABS):
            cols = slice(s * tf, (s + 1) * tf)
            dhpre = (_dot(dht, w2_ref[s], NT) * (2.0 * r_ref[:, cols].astype(F32))).astype(BF16)
            dhpre_ref[:, cols] = dhpre
            d = _dot(dhpre, w1_ref[s], NT)
            part = d if part is None else part + d
        acc_ref[...] += part

        @pl.when(f == nf - 1)
        def _():
            du2 = acc_ref[...]
            xhat = xhat_ref[...]
            x1 = xhat * g1_ref[...] + b1_ref[...]
            dx1 = ALPHA * dr2_ref[...] + du2 * (1.0 + sc_ref[...])
            small_ref[:, 2 * dm:3 * dm] += _rowsum(du2 * x1)
            small_ref[:, dm:2 * dm] += _rowsum(du2)
            small_ref[:, 3 * dm:4 * dm] += _rowsum(dx1 * xhat)
            small_ref[:, 4 * dm:5 * dm] += _rowsum(dx1)
            dr1 = _ln_bwd(dx1 * g1_ref[...], xhat, rstd_ref[:, 0:1])
            dr1_ref[...] = dr1
            small_ref[:, 0:dm] += _rowsum(dr1 * mix_ref[...])
            dmix_ref[...] = ((1.0 + ga_ref[...]) * dr1).astype(BF16)

    row = pl.BlockSpec((tm, D_MODEL), lambda i, f: (i, 0))
    vec = _full((1, D_MODEL))
    return pl.pallas_call(
        body, name="mlp_bwd", grid=(T // tm, nf),
        in_specs=[row, pl.BlockSpec((MLP_SLABS, D_MODEL, tf), lambda i, f: (f, 0, 0)),
                  pl.BlockSpec((MLP_SLABS, tf, D_MODEL), lambda i, f: (f, 0, 0)),
                  pl.BlockSpec((tm, MLP_SLABS * tf), lambda i, f: (i, f)), row, row,
                  pl.BlockSpec((tm, 128), lambda i, f: (i, 0)), row, vec, vec, vec, vec],
        out_specs=[pl.BlockSpec((tm, MLP_SLABS * tf), lambda i, f: (i, f)), row, row, _full((1, 5 * D_MODEL))],
        out_shape=[jax.ShapeDtypeStruct((T, N_DEV * tf), BF16), jax.ShapeDtypeStruct((T, D_MODEL), F32),
                   jax.ShapeDtypeStruct((T, D_MODEL), BF16), jax.ShapeDtypeStruct((1, 5 * D_MODEL), F32)],
        scratch_shapes=[pltpu.VMEM((tm, D_MODEL), F32)],
        compiler_params=_params(),
    )(dh, w1, w2, r, dr2, xhat1, rstd1, mix, ln1_g, ln1_b, sc_m, g_a)


def _input_bwd(dz_h, dz_m, w_in_ext, x, dr1, sc_a, exchange=None):
    T = x.shape[0]
    tm = min(ROW_TILE_SMALL, T)

    def body(dzh_ref, dzm_ref, w_ref, x_ref, dr1_ref, sc_ref, gx_ref, small_ref):
        @pl.when(pl.program_id(0) == 0)
        def _():
            small_ref[...] = jnp.zeros_like(small_ref)

        du = _bdot(dzh_ref[...], w_ref[0:2048, :]) + _bdot(dzm_ref[...], w_ref[2048:3072, :])
        gx_ref[...] = ALPHA * dr1_ref[...] + du * (1.0 + sc_ref[...])
        small_ref[:, D_MODEL:] += _rowsum(du * x_ref[...])
        small_ref[:, 0:D_MODEL] += _rowsum(du)

    row = pl.BlockSpec((tm, D_MODEL), lambda i: (i, 0))
    vec = _full((1, D_MODEL))
    return _call(
        body, "input_bwd", (dz_h, dz_m, w_in_ext, x, dr1, sc_a), grid=(T // tm,),
        in_specs=[pl.BlockSpec((tm, 2048), lambda i: (i, 0)), row, _full(w_in_ext.shape), row, row, vec],
        out_specs=[row, _full((1, 2 * D_MODEL))],
        out_shape=[jax.ShapeDtypeStruct((T, D_MODEL), F32), jax.ShapeDtypeStruct((1, 2 * D_MODEL), F32)],
        exchange=exchange)


def _adam_math(w, g, m, v):
    m = ADAM_B1 * m + (1.0 - ADAM_B1) * g
    v = ADAM_B2 * v + (1.0 - ADAM_B2) * (g * g)
    m_hat = m / (1.0 - ADAM_B1 ** ADAM_STEP)
    v_hat = v / (1.0 - ADAM_B2 ** ADAM_STEP)
    return -ADAM_LR * (m_hat / (jnp.sqrt(v_hat) + ADAM_EPS) + ADAM_WD * w), m, v


def _adam(g_slabs, w, m, v, name, g_fn=None, g_extra=()):
    R, C = w.shape
    tr = 256 if R % 256 == 0 else R
    ns = 0 if g_slabs is None else g_slabs.shape[0]
    ne = len(g_extra)

    def body(*refs):
        e_refs = refs[:ne]
        refs = refs[ne:]
        if ns:
            gs_ref, refs = refs[0], refs[1:]
        w_ref, m_ref, v_ref, g_ref, d_ref, nm_ref, nv_ref = refs
        if g_fn is not None:
            g = g_fn(*e_refs)
        else:
            g = gs_ref[0].astype(F32)
            for s in range(1, ns):
                g = g + gs_ref[s].astype(F32)
        d, nm, nv = _adam_math(w_ref[...], g, m_ref[...], v_ref[...])
        g_ref[...] = g
        d_ref[...] = d
        nm_ref[...] = nm
        nv_ref[...] = nv

    blk = pl.BlockSpec((tr, C), lambda i: (i, 0))
    in_specs = [pl.BlockSpec((tr, e.shape[1]), lambda i: (i, 0)) if e.shape[0] == R else _full(e.shape) for e in g_extra]
    args = list(g_extra)
    if ns:
        in_specs.append(pl.BlockSpec((ns, tr, C), lambda i: (0, i, 0)))
        args.append(g_slabs)
    return pl.pallas_call(
        body, name=name, grid=(R // tr,), in_specs=in_specs + [blk] * 3, out_specs=[blk] * 4,
        out_shape=[jax.ShapeDtypeStruct((R, C), F32)] * 4, compiler_params=_params(),
    )(*args, w, m, v)


def _adam_small(small_all, params):
    n = len(params)

    def body(*refs):
        s_ref, refs = refs[0], refs[1:]
        wmv, loss_ref, outs = refs[:3 * n], refs[3 * n], refs[3 * n + 1:]
        tot = s_ref[0]
        for i in range(1, N_DEV):
            tot = tot + s_ref[i]
        loss_ref[...] = tot[:, SMALL_W - 128:]
        for j, (w, _, _, off) in enumerate(params):
            w_ref, m_ref, v_ref = wmv[3 * j:3 * j + 3]
            g_ref, d_ref, nm_ref, nv_ref = outs[4 * j:4 * j + 4]
            if w.shape[0] == 2:
                lb = _lower_bound(w_ref)
                g0 = tot[:, off:off + w.shape[1]] * lb * (1.0 - lb)
                rows = [(slice(0, 1), g0), (slice(1, 2), -g0)]
            else:
                rows = [(slice(0, 1), tot[:, off:off + w.shape[1]])]
            for rs, g in rows:
                d, nm, nv = _adam_math(w_ref[rs, :], g, m_ref[rs, :], v_ref[rs, :])
                g_ref[rs, :], d_ref[rs, :], nm_ref[rs, :], nv_ref[rs, :] = g, d, nm, nv

    out_shape = [jax.ShapeDtypeStruct((1, 128), F32)]
    for w, _, _, _ in params:
        out_shape += [jax.ShapeDtypeStruct(w.shape, F32)] * 4
    res = pl.pallas_call(body, name="adam_small", out_shape=out_shape, compiler_params=_params())(
        small_all, *[a for w, m, v, _ in params for a in (w, m, v)])
    return res[0], [tuple(res[1 + 4 * j:5 + 4 * j]) for j in range(n)]


def _cols_from_slabs(g):
    s, r, c = g.shape
    return jnp.transpose(g, (1, 0, 2)).reshape(r, s * c)


def _slabs_from_cols(w):
    r, c = w.shape
    return jnp.transpose(w.reshape(r, N_DEV, c // N_DEV), (1, 0, 2))


def _rot_half_rows(wt):
    return jnp.concatenate([-wt[32:], wt[:32]], axis=0)


def _unrot_half_rows(dwt_rot):
    return jnp.concatenate([dwt_rot[32:], -dwt_rot[:32]], axis=0)


def _ext_in_t(g):
    k_in = g.shape[2]
    z64, z128 = jnp.zeros((64, k_in), BF16), jnp.zeros((128, k_in), BF16)
    wt = g.reshape(N_DEV * g.shape[1], k_in)
    main, wk = wt[:wt.shape[0] - ROPE_DIM], wt[wt.shape[0] - ROPE_DIM:]
    return jnp.concatenate([main, z128, wk, z64, z128, _rot_half_rows(wk), z64], axis=0)


def _ext_q_t(wt):
    r = wt.shape[1]
    z64, z128 = jnp.zeros((64, r), BF16), jnp.zeros((128, r), BF16)
    per = HEAD_DIM + ROPE_DIM
    main = [jnp.concatenate([wt[per * h:per * (h + 1)], z64], axis=0) for h in range(HEADS)]
    rot = [jnp.concatenate([z128, _rot_half_rows(wt[per * h + HEAD_DIM:per * (h + 1)]), z64], axis=0)
           for h in range(HEADS)]
    return jnp.concatenate(main + rot, axis=0)


def _ext_kv(w_kv_up):
    r = w_kv_up.shape[0]
    z128 = jnp.zeros((r, 128), BF16)
    wkv = w_kv_up.reshape(r, HEADS, 2 * HEAD_DIM)
    kpad = [jnp.concatenate([wkv[:, h, :HEAD_DIM], z128], axis=1) for h in range(HEADS)]
    vals = [wkv[:, h, HEAD_DIM:] for h in range(HEADS)]
    return jnp.concatenate(kpad + vals, axis=1)


def _grad_in_from_ext_t(dwt_h, dwt_m):
    dwk = dwt_m[512 + 128:512 + 192] + _unrot_half_rows(dwt_m[768 + 128:768 + 192])
    return jnp.concatenate([dwt_h, dwt_m[:512], dwk], axis=0)


def _grad_q_from_ext_t(dwq_ext_t):
    rows = []
    for h in range(HEADS):
        main, rot = dwq_ext_t[256 * h:256 * h + 256], dwq_ext_t[1024 + 256 * h:1280 + 256 * h]
        rows += [main[:128], main[128:192] + _unrot_half_rows(rot[128:192])]
    return jnp.concatenate(rows, axis=0)


def _grad_kv_from_ext(dwkv_ext):
    kvcols = []
    for h in range(HEADS):
        kvcols += [dwkv_ext[:, 256 * h:256 * h + 128], dwkv_ext[:, 1024 + 128 * h:1152 + 128 * h]]
    return jnp.concatenate(kvcols, axis=1)


SMALL_W = 6144 + 512 + 512 + 256 + 256 + 4 * 1024 + 128


def kernel(x, c, positions, w_ada, b_ada, w_in, hg_lower_bounds, hg_norm_w, mla_q_norm_w, w_q_up, mla_kv_norm_w, w_kv_up, w_out, ln1_g, ln1_b, w_mlp_in, w_mlp_out, ln2_g, ln2_b, loss_target, m_w_ada, m_b_ada, m_w_in, m_hg_lower_bounds, m_hg_norm_w, m_mla_q_norm_w, m_w_q_up, m_mla_kv_norm_w, m_w_kv_up, m_w_out, m_ln1_g, m_ln1_b, m_w_mlp_in, m_w_mlp_out, m_ln2_g, m_ln2_b, v_w_ada, v_b_ada, v_w_in, v_hg_lower_bounds, v_hg_norm_w, v_mla_q_norm_w, v_w_q_up, v_mla_kv_norm_w, v_w_kv_up, v_w_out, v_ln1_g, v_ln1_b, v_w_mlp_in, v_w_mlp_out, v_ln2_g, v_ln2_b):
    T = x.shape[1]
    me = 4 * lax.axis_index("x") + 2 * lax.axis_index("y") + lax.axis_index("c")
    xs, tgt = x[0], loss_target[0]
    transposed = ("w_in", "w_q_up")
    as_used = lambda n, a: a[0].T if n in transposed else a[0]
    big = {n: as_used(n, a) for n, a in dict(w_in=w_in, w_q_up=w_q_up, w_kv_up=w_kv_up, w_out=w_out,
                                              w_mlp_in=w_mlp_in, w_mlp_out=w_mlp_out).items()}
    names = list(big)

    bf = {n: big[n].astype(BF16) for n in names}
    g_in, g_c = _gather_two_level([bf["w_in"], c], name="gather_w_in")
    c_all = g_c.reshape(N_DEV, D_MODEL)

    ada_cols = w_ada.shape[2]
    mod_part, cond = _mod_part(c_all, w_ada[0], lax.dynamic_slice(b_ada, (0, me * ada_cols), (1, ada_cols)))
    (mod_all,) = _exchange([mod_part], scatter=False, name="gather_mod")
    mod_row = lax.dynamic_slice(mod_all, (0, me, 0), (N_DEV, 1, ada_cols)).reshape(1, N_DEV * ada_cols)
    sh_a, sc_a, g_a, sh_m, sc_m, g_m = [mod_row[:, D_MODEL * i:D_MODEL * (i + 1)] for i in range(6)]

    w_in_ext = _ext_in_t(g_in)
    z, (g_q, g_kv, g_out) = _matmul(xs, w_in_ext, "NT", "in_proj", a_fn=_modulate, extras=(sc_a, sh_a), tn=3072,
                                    exchange=_Exchange([bf["w_q_up"], bf["w_kv_up"], bf["w_out"]], False))
    wq_ext = _ext_q_t(g_q.reshape(N_DEV * g_q.shape[1], g_q.shape[2]))
    wkv_ext = _ext_kv(_cols_from_slabs(g_kv))
    w_out_full = g_out.reshape(D_MODEL, D_MODEL)
    inv_freq = 1.0 / (ROPE_THETA ** (jnp.arange(0, ROPE_DIM, 2, dtype=F32) / ROPE_DIM))
    zeros = lambda n: jnp.zeros((n,), F32)
    invf = jnp.concatenate([zeros(128), inv_freq, inv_freq, zeros(64)]).reshape(1, QK_PAD)
    m_one = jnp.concatenate([jnp.ones((128,), F32), zeros(128)]).reshape(1, QK_PAD)
    m_rot = jnp.concatenate([zeros(128), jnp.ones((64,), F32), zeros(64)]).reshape(1, QK_PAD)
    q, k, v, c1, s1, cqn, ckvn = _mla_pre(z, positions.reshape(T, 1), invf, m_one, m_rot, wq_ext, wkv_ext,
                                          mla_q_norm_w, mla_kv_norm_w)[0]
    (o_raw, o_gated, s_prev), (w1,) = _hgrn_fwd(z, hg_lower_bounds, hg_norm_w,
                                                exchange=_StagedGather(bf["w_mlp_in"]))
    (o_mla, lse), (w2,) = _attn_fwd(q, k, v, exchange=_StagedGather(bf["w_mlp_out"]))
    mix, xhat1, rstd1, u2 = _mix_ln1(o_gated, o_mla, w_out_full, xs, g_a, ln1_g, ln1_b, sc_m, sh_m)[0]
    r, dr2, dh, small_mlp_fwd = _mlp_fwd(u2, w1, w2, xhat1, ln1_g, ln1_b, g_m, ln2_g, ln2_b, tgt)

    dhpre, dr1, dmix, small_mlp_bwd = _mlp_bwd(dh, w1, w2, r, dr2, xhat1, rstd1, mix, ln1_g, ln1_b, sc_m, g_a)
    received = {}
    dw2 = _matmul(r, dh, "TN", "wgrad_mlp_out", out_dtype=BF16, a_fn=_square, tm=1024)
    dw1 = _matmul(u2, dhpre, "TN", "wgrad_mlp_in", out_dtype=BF16, tm=1024, out_slabs=N_DEV)
    dmixcat = _matmul(dmix, w_out_full, "NT", "dgrad_out")
    dw_out = jnp.concatenate([_matmul(o_gated, dmix, "TN", "wgrad_out_hg", out_dtype=BF16),
                              _matmul(o_mla, dmix, "TN", "wgrad_out_mla", out_dtype=BF16)], axis=0)
    (dz_h, small_hgrn), (received["w_out"],) = _hgrn_bwd(
        dmixcat, z, o_raw, s_prev, hg_lower_bounds, hg_norm_w,
        exchange=_Exchange([dw_out.reshape(N_DEV, D_MODEL // N_DEV, D_MODEL)], True))
    (dq, dk, dv), (received["w_mlp_in"], received["w_mlp_out"]) = _attn_bwd(
        q, k, v, dmixcat, o_mla, lse,
        exchange=_Exchange([dw1, dw2.reshape(N_DEV, dw2.shape[0] // N_DEV, D_MODEL)], True))
    dz_m, dq_ext, dkv_ext, small_mla = _mla_bwd(dq, dk, dv, z, c1, s1, wq_ext, wkv_ext, mla_q_norm_w, mla_kv_norm_w)
    dwq_t = _grad_q_from_ext_t(_matmul(dq_ext, cqn, "TN", "wgrad_q_up", tm=1024))
    dwkv = _grad_kv_from_ext(_matmul(ckvn, dkv_ext, "TN", "wgrad_kv_up", tn=1536))
    qkv_slabs = [dwq_t.reshape((N_DEV, dwq_t.shape[0] // N_DEV, dwq_t.shape[1])).astype(BF16),
                 _slabs_from_cols(dwkv).astype(BF16)]
    dwt_h, (received["w_q_up"], received["w_kv_up"]) = _matmul(
        dz_h, xs, "TN", "wgrad_in_h", b_fn=_modulate, extras=(sc_a, sh_a), tm=1024,
        exchange=_Exchange(qkv_slabs, True))
    dwt_m = _matmul(dz_m, xs, "TN", "wgrad_in_m", b_fn=_modulate, extras=(sc_a, sh_a), tm=1024)
    dw_in_t = _grad_in_from_ext_t(dwt_h, dwt_m)
    in_slabs = dw_in_t.reshape((N_DEV, dw_in_t.shape[0] // N_DEV, dw_in_t.shape[1])).astype(BF16)
    (grad_x, small_in), (received["w_in"],) = _input_bwd(
        dz_h, dz_m, w_in_ext, xs, dr1, sc_a, exchange=_Exchange([in_slabs], True))

    small = jnp.concatenate([small_in, small_mlp_bwd[:, :3 * D_MODEL], small_mlp_fwd[:, :D_MODEL], small_hgrn,
                             small_mla, small_mlp_bwd[:, 3 * D_MODEL:], small_mlp_fwd[:, D_MODEL:]], axis=1)
    assert small.shape == (1, SMALL_W)
    (small_all,) = _exchange([small], scatter=False, name="gather_small")

    moments = dict(w_in=(m_w_in, v_w_in), w_q_up=(m_w_q_up, v_w_q_up), w_kv_up=(m_w_kv_up, v_w_kv_up),
                   w_out=(m_w_out, v_w_out), w_mlp_in=(m_w_mlp_in, v_w_mlp_in), w_mlp_out=(m_w_mlp_out, v_w_mlp_out))
    res = {}
    for n in names:
        res[n] = _adam(received[n], big[n], as_used(n, moments[n][0]), as_used(n, moments[n][1]), name="adam_" + n)
    dmod_cols = lax.dynamic_slice(small_all.reshape(N_DEV, SMALL_W), (0, me * ada_cols), (N_DEV, ada_cols))
    cond_t = cond.T

    def ada_grad(ct_ref, dm_ref):
        g = ct_ref[:, 0:1] * dm_ref[0:1, :]
        for b in range(1, N_DEV):
            g = g + ct_ref[:, b:b + 1] * dm_ref[b:b + 1, :]
        return g

    res["w_ada"] = _adam(None, w_ada[0], m_w_ada[0], v_w_ada[0], name="adam_w_ada", g_fn=ada_grad,
                         g_extra=(cond_t, dmod_cols))

    small_params = [("b_ada", b_ada, m_b_ada, v_b_ada, 0),
                    ("hg_lower_bounds", hg_lower_bounds, m_hg_lower_bounds, v_hg_lower_bounds, 6144),
                    ("hg_norm_w", hg_norm_w, m_hg_norm_w, v_hg_norm_w, 6656),
                    ("mla_q_norm_w", mla_q_norm_w, m_mla_q_norm_w, v_mla_q_norm_w, 7168),
                    ("mla_kv_norm_w", mla_kv_norm_w, m_mla_kv_norm_w, v_mla_kv_norm_w, 7424),
                    ("ln1_g", ln1_g, m_ln1_g, v_ln1_g, 7680), ("ln1_b", ln1_b, m_ln1_b, v_ln1_b, 8704),
                    ("ln2_g", ln2_g, m_ln2_g, v_ln2_g, 9728), ("ln2_b", ln2_b, m_ln2_b, v_ln2_b, 10752)]
    loss_row, small_res = _adam_small(small_all, [p[1:] for p in small_params])
    for p, r4 in zip(small_params, small_res):
        res[p[0]] = r4
    loss = loss_row[0, 0]

    order = ["w_ada", "b_ada", "w_in", "hg_lower_bounds", "hg_norm_w", "mla_q_norm_w", "w_q_up", "mla_kv_norm_w",
             "w_kv_up", "w_out", "ln1_g", "ln1_b", "w_mlp_in", "w_mlp_out", "ln2_g", "ln2_b"]
    def as_given(n, a):
        if n in transposed:
            a = a.T
        return a[None] if n in big or n == "w_ada" else a

    shaped = {n: tuple(as_given(n, a) for a in res[n]) for n in order}
    outs = [loss, grad_x.reshape(1, T, D_MODEL)]
    for i in range(4):
        outs += [shaped[n][i] for n in order]
    return tuple(outs)
```

```python
import functools

import jax
import jax.numpy as jnp
import numpy as np
from jax import lax
from jax.experimental import pallas as pl
from jax.experimental.pallas import tpu as pltpu

F32, BF16 = jnp.float32, jnp.bfloat16
N_DEV = 8
D_MODEL = 1024
HEADS = 4
HEAD_DIM = 128
ROPE_DIM = 64
QK_PAD = 256
CHUNK = 64
ROPE_THETA = 10000.0
RMS_EPS = 1e-6
LN_EPS = 1e-5
ALPHA = 2.0 ** 0.25
ATT_SCALE = (HEAD_DIM + ROPE_DIM) ** -0.5
LN2 = float(np.log(2.0))
Q_PRESCALE = ATT_SCALE / LN2
ADAM_LR, ADAM_B1, ADAM_B2, ADAM_EPS, ADAM_WD, ADAM_STEP = 0.001, 0.9, 0.999, 1e-08, 0.01, 10
NEG_BIG = -1e30

ROW_TILE = 512
ROW_TILE_SMALL = 256
ATT_TILE = 512
HGRN_GROUP = 8
MLP_SLABS = 4
VMEM_LIMIT = 56 * 2 ** 20

NN = (((1,), (0,)), ((), ()))
NT = (((1,), (1,)), ((), ()))
TN = (((0,), (0,)), ((), ()))


def _dot(a, b, dims=NN):
    return lax.dot_general(a, b, dims, preferred_element_type=F32)


def _bdot(a, b, dims=NN):
    return lax.dot_general(a.astype(BF16), b.astype(BF16), dims, preferred_element_type=F32)


def _hdot(a, b, dims=NN):
    return lax.dot_general(a, b, dims, precision=lax.Precision.HIGHEST, preferred_element_type=F32)


def _params():
    return pltpu.CompilerParams(vmem_limit_bytes=VMEM_LIMIT)


def _sigmoid(x):
    return 1.0 / (1.0 + jnp.exp(-x))


def _rowsum(x):
    return jnp.sum(x, axis=0, keepdims=True)


def _lanemean(x):
    return jnp.mean(x, axis=-1, keepdims=True)


def _full(shape):
    nd = len(shape)
    return pl.BlockSpec(shape, lambda *_: (0,) * nd)


class _Exchange:
    def __init__(self, arrs, scatter):
        self.arrs, self.scatter, self.n, self.aliases = list(arrs), scatter, len(arrs), []
        self.out_shape = [jax.ShapeDtypeStruct((N_DEV,) + (a.shape[1:] if scatter else a.shape), a.dtype)
                          for a in self.arrs]
        n = self.n
        self.scratch = [pltpu.SemaphoreType.DMA((n, N_DEV - 1)), pltpu.SemaphoreType.DMA((n, N_DEV - 1)),
                        pltpu.SemaphoreType.DMA((n,))]

    def _copies(self, ins, outs, sems):
        send_sems, recv_sems, loc_sems = sems
        x, y, c = lax.axis_index("x"), lax.axis_index("y"), lax.axis_index("c")
        me = 4 * x + 2 * y + c
        copies = []
        for k in range(self.n):
            src_of = (lambda i, k=k: ins[k].at[i]) if self.scatter else (lambda i, k=k: ins[k])
            copies.append((pltpu.make_async_copy(src_of(me), outs[k].at[me], loc_sems.at[k]), None))
            for p in range(1, N_DEV):
                px = (1 - x) if p & 4 else x
                py = (1 - y) if p & 2 else y
                pc = (1 - c) if p & 1 else c
                peer = 4 * px + 2 * py + pc
                both = dict(send_sem=send_sems.at[k, p - 1], recv_sem=recv_sems.at[k, p - 1],
                            device_id=(px, py, pc), device_id_type=pl.DeviceIdType.MESH)
                send = pltpu.make_async_remote_copy(src_ref=src_of(peer), dst_ref=outs[k].at[me], **both)
                recv = pltpu.make_async_remote_copy(src_ref=src_of(peer), dst_ref=outs[k].at[peer], **both)
                copies.append((send, recv))
        return copies

    def start(self, ins, outs, sems):
        for first, _ in self._copies(ins, outs, sems):
            first.start()

    def middle(self, ins, outs, sems):
        pass

    def wait(self, ins, outs, sems):
        for first, recv in self._copies(ins, outs, sems):
            if recv is None:
                first.wait()
            else:
                recv.wait_recv()
                first.wait_send()


class _StagedGather:
    def __init__(self, arr):
        self.arrs, self.aliases = [arr], []
        self.out_shape = [jax.ShapeDtypeStruct((N_DEV,) + arr.shape, arr.dtype)]
        self.scratch = [pltpu.VMEM((N_DEV,) + arr.shape, arr.dtype), pltpu.SemaphoreType.DMA((7,)),
                        pltpu.SemaphoreType.DMA((7,)), pltpu.SemaphoreType.DMA((2,))]

    def _parts(self, scr):
        stage, send_sems, recv_sems, loc_sems = scr
        x, y, c = lax.axis_index("x"), lax.axis_index("y"), lax.axis_index("c")
        me, sibling = (x, y, c), (x, y, 1 - c)
        chips = [(1 - x, y), (x, 1 - y), (1 - x, 1 - y)]

        def copy(j, block, to):
            px, py, pc = block
            slot = stage.at[4 * px + 2 * py + pc]
            return pltpu.make_async_remote_copy(src_ref=slot, dst_ref=slot, send_sem=send_sems.at[j],
                                                recv_sem=recv_sems.at[j], device_id=to,
                                                device_id_type=pl.DeviceIdType.MESH)

        return stage, loc_sems, me, sibling, chips, c, copy

    def start(self, ins, outs, scr):
        stage, loc_sems, me, sibling, chips, c, copy = self._parts(scr)
        x, y, _ = me
        own = pltpu.make_async_copy(ins[0], stage.at[4 * x + 2 * y + c], loc_sems.at[0])
        own.start()
        own.wait()
        copy(0, me, sibling).start()
        for j, chip in enumerate(chips):
            copy(1 + j, me, (*chip, c)).start()

    def middle(self, ins, outs, scr):
        stage, loc_sems, me, sibling, chips, c, copy = self._parts(scr)
        for j, chip in enumerate(chips):
            copy(1 + j, (*chip, c), me).wait_recv()
            copy(4 + j, (*chip, c), sibling).start()

    def wait(self, ins, outs, scr):
        stage, loc_sems, me, sibling, chips, c, copy = self._parts(scr)
        copy(0, sibling, me).wait_recv()
        for j, chip in enumerate(chips):
            copy(4 + j, (*chip, 1 - c), me).wait_recv()
        copy(0, me, sibling).wait_send()
        for j, chip in enumerate(chips):
            copy(1 + j, me, (*chip, c)).wait_send()
            copy(4 + j, (*chip, c), sibling).wait_send()
        whole = pltpu.make_async_copy(stage, outs[0], loc_sems.at[1])
        whole.start()
        whole.wait()


def _call(body, name, args, out_shape, grid=(), in_specs=(), out_specs=(), scratch_shapes=(), exchange=None,
          middle_at=0.8):
    if exchange is None:
        return pl.pallas_call(body, name=name, grid=grid, in_specs=list(in_specs), out_specs=list(out_specs),
                              out_shape=list(out_shape), scratch_shapes=list(scratch_shapes),
                              compiler_params=_params())(*args), None
    exs = list(exchange) if isinstance(exchange, (list, tuple)) else [exchange]
    ni, no, ns = len(args), len(out_shape), len(scratch_shapes)
    nxi, nxo = sum(len(e.arrs) for e in exs), sum(len(e.out_shape) for e in exs)
    steps = int(np.prod(grid))
    mid_step = min(max(int(steps * middle_at), 1), steps - 1)
    aliases, iat, oat = {}, ni, no
    for e in exs:
        for src, dst in e.aliases:
            aliases[iat + src] = oat + dst
        iat, oat = iat + len(e.arrs), oat + len(e.out_shape)

    def wrapped(*refs):
        a, xi = refs[:ni], refs[ni:ni + nxi]
        o, xo = refs[ni + nxi:ni + nxi + no], refs[ni + nxi + no:ni + nxi + no + nxo]
        s, xs = refs[ni + nxi + no + nxo:ni + nxi + no + nxo + ns], refs[ni + nxi + no + nxo + ns:]
        parts, iat, oat, sat = [], 0, 0, 0
        for e in exs:
            parts.append((e, xi[iat:iat + len(e.arrs)], xo[oat:oat + len(e.out_shape)], xs[sat:sat + len(e.scratch)]))
            iat, oat, sat = iat + len(e.arrs), oat + len(e.out_shape), sat + len(e.scratch)
        step = 0
        for d, g in enumerate(grid):
            step = step * g + pl.program_id(d)

        @pl.when(step == 0)
        def _():
            for e, ins, outs, sems in parts:
                e.start(ins, outs, sems)

        @pl.when(step == mid_step)
        def _():
            for e, ins, outs, sems in parts:
                e.middle(ins, outs, sems)

        body(*a, *o, *s)

        @pl.when(step == steps - 1)
        def _():
            for e, ins, outs, sems in parts:
                e.wait(ins, outs, sems)

    hbm = pl.BlockSpec(memory_space=pltpu.HBM)
    res = pl.pallas_call(
        wrapped, name=name, grid=grid, in_specs=list(in_specs) + [hbm] * nxi, out_specs=list(out_specs) + [hbm] * nxo,
        out_shape=list(out_shape) + [o_ for e in exs for o_ in e.out_shape],
        scratch_shapes=list(scratch_shapes) + [s_ for e in exs for s_ in e.scratch],
        input_output_aliases=aliases, compiler_params=_params())(*args, *[a_ for e in exs for a_ in e.arrs])
    return res[:no], res[no:]


def _gather_two_level(arrs, name):
    n = len(arrs)
    out_shape = [jax.ShapeDtypeStruct((N_DEV,) + a.shape, a.dtype) for a in arrs]

    def body(*refs):
        ins, outs = refs[:n], refs[n:2 * n]
        send_sems, recv_sems, loc_sems = refs[2 * n:]
        x, y, c = lax.axis_index("x"), lax.axis_index("y"), lax.axis_index("c")
        me, sibling = (x, y, c), (x, y, 1 - c)
        chips = [(1 - x, y), (x, 1 - y), (1 - x, 1 - y)]

        def copy(k, j, block, to, src=None):
            px, py, pc = block
            dst = outs[k].at[4 * px + 2 * py + pc]
            return pltpu.make_async_remote_copy(src_ref=dst if src is None else src, dst_ref=dst,
                                                send_sem=send_sems.at[k, j], recv_sem=recv_sems.at[k, j],
                                                device_id=to, device_id_type=pl.DeviceIdType.MESH)

        mine = [pltpu.make_async_copy(ins[k], outs[k].at[4 * x + 2 * y + c], loc_sems.at[k]) for k in range(n)]
        first = []
        for k in range(n):
            mine[k].start()
            first.append(copy(k, 0, me, sibling, src=ins[k]))
            first += [copy(k, 1 + j, me, (*chip, c), src=ins[k]) for j, chip in enumerate(chips)]
        for cp in first:
            cp.start()
        passed = []
        for j, chip in enumerate(chips):
            for k in range(n):
                copy(k, 1 + j, (*chip, c), me).wait_recv()
                passed.append(copy(k, 4 + j, (*chip, c), sibling))
                passed[-1].start()
        for k in range(n):
            copy(k, 0, sibling, me).wait_recv()
            for j, chip in enumerate(chips):
                copy(k, 4 + j, (*chip, 1 - c), me).wait_recv()
        for cp in first + passed:
            cp.wait_send()
        for cp in mine:
            cp.wait()

    vmem = pl.BlockSpec(memory_space=pltpu.VMEM)
    return pl.pallas_call(body, name=name, out_shape=out_shape, in_specs=[vmem] * n, out_specs=[vmem] * n,
                          scratch_shapes=[pltpu.SemaphoreType.DMA((n, 7)), pltpu.SemaphoreType.DMA((n, 7)),
                                          pltpu.SemaphoreType.DMA((n,))], compiler_params=_params())(*arrs)


def _exchange(arrs, scatter, name):
    ex = _Exchange(arrs, scatter)

    def body(*refs):
        ins, outs, sems = refs[:ex.n], refs[ex.n:2 * ex.n], refs[2 * ex.n:]
        ex.start(ins, outs, sems)
        ex.wait(ins, outs, sems)

    hbm = pl.BlockSpec(memory_space=pltpu.HBM)
    return pl.pallas_call(body, name=name, out_shape=ex.out_shape, in_specs=[hbm] * ex.n, out_specs=[hbm] * ex.n,
                          scratch_shapes=ex.scratch)(*ex.arrs)


def _matmul(a, b, mode, name, out_dtype=F32, tm=512, tn=1024, tk=1024, a_fn=None, b_fn=None, extras=(),
            out_slabs=None, exchange=None):
    assert not (a_fn and b_fn) and not (b_fn and mode == "NT")
    if mode == "NN":
        (M, K), N = a.shape, b.shape[1]
    elif mode == "NT":
        (M, K), N = a.shape, b.shape[0]
    else:
        (K, M), N = a.shape, b.shape[1]
    if out_slabs:
        tn = N // out_slabs
    tm, tn, tk = min(tm, M), min(tn, N), min(tk, K)
    assert M % tm == 0 and N % tn == 0 and K % tk == 0, (name, M, N, K)
    nk = K // tk
    dims = {"NN": NN, "NT": NT, "TN": TN}[mode]
    ne = len(extras)

    def body(a_ref, b_ref, *rest):
        e_refs, o_ref, acc_ref = rest[:ne], rest[ne], rest[ne + 1]
        k = pl.program_id(2)

        @pl.when(k == 0)
        def _():
            acc_ref[...] = jnp.zeros_like(acc_ref)

        at, bt = a_ref[...], b_ref[...]
        if a_fn is not None:
            at = a_fn(at.astype(F32), *[e[...] for e in e_refs])
        if b_fn is not None:
            bt = b_fn(bt.astype(F32), *[e[...] for e in e_refs])
        acc_ref[...] += _bdot(at, bt, dims)

        @pl.when(k == nk - 1)
        def _():
            o_ref[...] = acc_ref[...].astype(out_dtype)

    if mode == "TN":
        a_spec = pl.BlockSpec((tk, tm), lambda i, j, k: (k, i))
        e_spec = pl.BlockSpec((1, tm), lambda i, j, k: (0, i))
    else:
        a_spec = pl.BlockSpec((tm, tk), lambda i, j, k: (i, k))
        e_spec = pl.BlockSpec((1, tk), lambda i, j, k: (0, k))
    if mode == "NT":
        b_spec = pl.BlockSpec((tn, tk), lambda i, j, k: (j, k))
    else:
        b_spec = pl.BlockSpec((tk, tn), lambda i, j, k: (k, j))
    if b_fn is not None:
        e_spec = pl.BlockSpec((1, tn), lambda i, j, k: (0, j))
    if out_slabs:
        o_shape = jax.ShapeDtypeStruct((out_slabs, M, tn), out_dtype)
        o_spec = pl.BlockSpec((None, tm, tn), lambda i, j, k: (j, i, 0))
    else:
        o_shape = jax.ShapeDtypeStruct((M, N), out_dtype)
        o_spec = pl.BlockSpec((tm, tn), lambda i, j, k: (i, j))
    (out,), got = _call(body, name, (a, b, *extras), [o_shape], grid=(M // tm, N // tn, nk),
                        in_specs=[a_spec, b_spec] + [e_spec] * ne, out_specs=[o_spec],
                        scratch_shapes=[pltpu.VMEM((tm, tn), F32)], exchange=exchange)
    return out if exchange is None else (out, got)


def _modulate(x, sc, sh):
    return x * (1.0 + sc) + sh


def _square(x):
    return x * x


def _mod_part(c_all, w_ada_s, b_s):
    def body(c_ref, w_ref, b_ref, mod_ref, cond_ref):
        cv = c_ref[...]
        cond = cv * _sigmoid(cv)
        cond_ref[...] = cond
        mod_ref[...] = _bdot(cond, w_ref[...]) + b_ref[...]

    return pl.pallas_call(
        body, name="mod_part",
        out_shape=[jax.ShapeDtypeStruct((N_DEV, w_ada_s.shape[1]), F32), jax.ShapeDtypeStruct(c_all.shape, F32)],
        compiler_params=_params(),
    )(c_all, w_ada_s, b_s)


def _rms_fwd(x, w):
    rs = lax.rsqrt(_lanemean(x * x) + RMS_EPS)
    return x * rs * w, rs


def _rms_bwd(x, rs, w, dy):
    xhat = x * rs
    dxh = dy * w
    return rs * (dxh - xhat * _lanemean(dxh * xhat)), dy * xhat


def _mla_pre(z, pos_col, invf, m_one, m_rot, wq_ext, wkv_ext, qnw, kvnw, exchange=None):
    T = z.shape[0]
    tm = min(ROW_TILE, T)

    def body(z_ref, pos_ref, invf_ref, mone_ref, mrot_ref, wq_ref, wkv_ref, qnw_ref, kvnw_ref,
             q_ref, k_ref, v_ref, c1_ref, s1_ref, cqn_ref, ckvn_ref):
        ang = pos_ref[...].astype(F32) * invf_ref[...]
        c1 = mone_ref[...] + mrot_ref[...] * jnp.cos(ang)
        s1 = mrot_ref[...] * jnp.sin(ang)
        c1_ref[...] = c1
        s1_ref[...] = s1
        cqn, _ = _rms_fwd(z_ref[:, 0:256], qnw_ref[...])
        ckvn, _ = _rms_fwd(z_ref[:, 256:512], kvnw_ref[...])
        cqn_ref[...] = cqn.astype(BF16)
        ckvn_ref[...] = ckvn.astype(BF16)
        qe = _bdot(cqn, wq_ref[...], NT)
        kve = _bdot(ckvn, wkv_ref[...])
        k_rope = z_ref[:, 512:768] * c1 + z_ref[:, 768:1024] * s1
        for h in range(HEADS):
            q_ref[h] = ((qe[:, 256 * h:256 * h + 256] * c1 + qe[:, 1024 + 256 * h:1280 + 256 * h] * s1)
                        * Q_PRESCALE).astype(BF16)
            k_ref[h] = (kve[:, 256 * h:256 * h + 256] + k_rope).astype(BF16)
            v_ref[h] = kve[:, 1024 + 128 * h:1152 + 128 * h].astype(BF16)

    row = lambda i: (i, 0)
    head = lambda i: (0, i, 0)
    return _call(
        body, "mla_pre", (z, pos_col, invf, m_one, m_rot, wq_ext, wkv_ext, qnw, kvnw), grid=(T // tm,),
        in_specs=[pl.BlockSpec((tm, 1024), lambda i: (i, 2)), pl.BlockSpec((tm, 1), row),
                  _full((1, 256)), _full((1, 256)), _full((1, 256)), _full(wq_ext.shape), _full(wkv_ext.shape),
                  _full((1, 256)), _full((1, 256))],
        out_specs=[pl.BlockSpec((HEADS, tm, QK_PAD), head), pl.BlockSpec((HEADS, tm, QK_PAD), head),
                   pl.BlockSpec((HEADS, tm, HEAD_DIM), head), pl.BlockSpec((tm, 256), row), pl.BlockSpec((tm, 256), row),
                   pl.BlockSpec((tm, 256), row), pl.BlockSpec((tm, 256), row)],
        out_shape=[jax.ShapeDtypeStruct((HEADS, T, QK_PAD), BF16), jax.ShapeDtypeStruct((HEADS, T, QK_PAD), BF16),
                   jax.ShapeDtypeStruct((HEADS, T, HEAD_DIM), BF16), jax.ShapeDtypeStruct((T, 256), F32),
                   jax.ShapeDtypeStruct((T, 256), F32), jax.ShapeDtypeStruct((T, 256), BF16),
                   jax.ShapeDtypeStruct((T, 256), BF16)], exchange=exchange)


def _mla_bwd(dq, dk, dv, z, c1, s1, wq_ext, wkv_ext, qnw, kvnw):
    T = z.shape[0]
    tm = min(ROW_TILE_SMALL, T)

    def body(dq_ref, dk_ref, dv_ref, z_ref, c1_ref, s1_ref, wq_ref, wkv_ref, qnw_ref, kvnw_ref,
             dz_ref, dqe_ref, dkve_ref, dnw_ref):
        @pl.when(pl.program_id(0) == 0)
        def _():
            dnw_ref[...] = jnp.zeros_like(dnw_ref)

        c1, s1 = c1_ref[...], s1_ref[...]
        dkpe = jnp.zeros((tm, QK_PAD), F32)
        for h in range(HEADS):
            dqh, dkh = dq_ref[h] * Q_PRESCALE, dk_ref[h]
            dqe_ref[:, 256 * h:256 * h + 256] = (dqh * c1).astype(BF16)
            dqe_ref[:, 1024 + 256 * h:1280 + 256 * h] = (dqh * s1).astype(BF16)
            dkve_ref[:, 256 * h:256 * h + 256] = dkh.astype(BF16)
            dkve_ref[:, 1024 + 128 * h:1152 + 128 * h] = dv_ref[h].astype(BF16)
            dkpe = dkpe + dkh
        dcqn = _dot(dqe_ref[...], wq_ref[...])
        dckvn = _dot(dkve_ref[...], wkv_ref[...], NT)
        cq, ckv = z_ref[:, 0:256], z_ref[:, 256:512]
        _, rsq = _rms_fwd(cq, qnw_ref[...])
        _, rskv = _rms_fwd(ckv, kvnw_ref[...])
        dcq, wq_rows = _rms_bwd(cq, rsq, qnw_ref[...], dcqn)
        dckv, wkv_rows = _rms_bwd(ckv, rskv, kvnw_ref[...], dckvn)
        dnw_ref[:, 0:256] += _rowsum(wq_rows)
        dnw_ref[:, 256:512] += _rowsum(wkv_rows)
        dz_ref[:, 0:256] = dcq
        dz_ref[:, 256:512] = dckv
        dz_ref[:, 512:768] = dkpe * c1
        dz_ref[:, 768:1024] = dkpe * s1

    row = lambda i: (i, 0)
    head = lambda i: (0, i, 0)
    return pl.pallas_call(
        body, name="mla_bwd", grid=(T // tm,),
        in_specs=[pl.BlockSpec((HEADS, tm, QK_PAD), head), pl.BlockSpec((HEADS, tm, QK_PAD), head),
                  pl.BlockSpec((HEADS, tm, HEAD_DIM), head), pl.BlockSpec((tm, 1024), lambda i: (i, 2)),
                  pl.BlockSpec((tm, 256), row), pl.BlockSpec((tm, 256), row), _full(wq_ext.shape), _full(wkv_ext.shape),
                  _full((1, 256)), _full((1, 256))],
        out_specs=[pl.BlockSpec((tm, 1024), row), pl.BlockSpec((tm, 2048), row), pl.BlockSpec((tm, 1536), row),
                   _full((1, 512))],
        out_shape=[jax.ShapeDtypeStruct((T, 1024), F32), jax.ShapeDtypeStruct((T, 2048), BF16),
                   jax.ShapeDtypeStruct((T, 1536), BF16), jax.ShapeDtypeStruct((1, 512), F32)],
        compiler_params=_params(),
    )(dq, dk, dv, z, c1, s1, wq_ext, wkv_ext, qnw, kvnw)


_HEAD_LANES = [slice(HEAD_DIM * h, HEAD_DIM * (h + 1)) for h in range(HEADS)]


def _lower_bound(lbraw_ref):
    a0, a1 = lbraw_ref[0:1, :], lbraw_ref[1:2, :]
    mx = jnp.maximum(a0, a1)
    e0, e1 = jnp.exp(a0 - mx), jnp.exp(a1 - mx)
    return e0 / (e0 + e1)


def _tri(lower):
    r = lax.broadcasted_iota(jnp.int32, (CHUNK, CHUNK), 0)
    c = lax.broadcasted_iota(jnp.int32, (CHUNK, CHUNK), 1)
    return (r >= c) if lower else (r <= c)


def _hgrn_gates(q, f, lb, tri_lo):
    sg = _sigmoid(f)
    forget = lb + (1.0 - lb) * sg
    k = 1.0 - forget
    b = _hdot(tri_lo.astype(F32), jnp.log(forget))
    b_ref, b_last = b[CHUNK // 2 - 1:CHUNK // 2, :], b[CHUNK - 1:CHUNK, :]
    e1, e2, e3, e4 = jnp.exp(b - b_ref), jnp.exp(b_ref - b), jnp.exp(b_last - b), jnp.exp(b)
    return dict(sg=sg, forget=forget, k=k, e1=e1, e2=e2, e3=e3, e4=e4, qa=q * e1, ka=k * e2, kl=k * e3, qb=q * e4,
                decay=jnp.exp(b_last))


def _hgrn_fwd(z, lbraw, nw, exchange=None):
    T = z.shape[0]
    G = min(HGRN_GROUP, T // CHUNK)
    rows = G * CHUNK
    n_chunks = T // CHUNK

    def body(q_ref, f_ref, i_ref, g_ref, lbraw_ref, nw_ref, oraw_ref, og_ref, sp_ref, st_ref):
        @pl.when(pl.program_id(0) == 0)
        def _():
            st_ref[...] = jnp.zeros_like(st_ref)

        lb_all = _lower_bound(lbraw_ref)
        tri_lo = _tri(True)

        def chunk(cc, carry):
            rs = pl.ds(pl.multiple_of(cc * CHUNK, CHUNK), CHUNK)
            t = _hgrn_gates(q_ref[rs, :], f_ref[rs, :], lb_all, tri_lo)
            v, gate = i_ref[rs, :], g_ref[rs, :]
            st = [st_ref[h] for h in range(HEADS)]
            a = [jnp.where(tri_lo, _bdot(t["qa"][:, s], t["ka"][:, s], NT), 0.0) for s in _HEAD_LANES]
            kv = [_bdot(v[:, s], t["kl"][:, s], TN) for s in _HEAD_LANES]
            o = [_bdot(a[h], v[:, s]) + _bdot(t["qb"][:, s], st[h], NT) for h, s in enumerate(_HEAD_LANES)]
            for h, s in enumerate(_HEAD_LANES):
                sp_ref[cc, h] = st[h]
                st_ref[h] = st[h] * t["decay"][:, s] + kv[h]
            oraw_ref[rs, :] = jnp.concatenate(o, axis=1)
            on = jnp.concatenate([_rms_fwd(o[h], nw_ref[:, s])[0] for h, s in enumerate(_HEAD_LANES)], axis=1)
            og_ref[rs, :] = (on * (gate * _sigmoid(gate))).astype(BF16)
            return carry

        lax.fori_loop(0, G, chunk, 0, unroll=4)

    col = lambda j: pl.BlockSpec((rows, 512), lambda r, j=j: (r, j))
    return _call(
        body, "hgrn_fwd", (z, z, z, z, lbraw, nw), grid=(T // rows,),
        in_specs=[col(0), col(1), col(2), col(3), _full((2, 512)), _full((1, 512))],
        out_specs=[col(0), col(0), pl.BlockSpec((G, HEADS, HEAD_DIM, HEAD_DIM), lambda r: (r, 0, 0, 0))],
        out_shape=[jax.ShapeDtypeStruct((T, 512), F32), jax.ShapeDtypeStruct((T, 512), BF16),
                   jax.ShapeDtypeStruct((n_chunks, HEADS, HEAD_DIM, HEAD_DIM), F32)],
        scratch_shapes=[pltpu.VMEM((HEADS, HEAD_DIM, HEAD_DIM), F32)], exchange=exchange)


def _hgrn_bwd(dmixcat, z, oraw, sprev, lbraw, nw, exchange=None):
    T = z.shape[0]
    G = min(HGRN_GROUP, T // CHUNK)
    rows = G * CHUNK
    ng = T // rows

    def body(dog_ref, q_ref, f_ref, i_ref, g_ref, oraw_ref, sp_ref, lbraw_ref, nw_ref,
             dz_ref, dsmall_ref, dst_ref):
        @pl.when(pl.program_id(0) == 0)
        def _():
            dst_ref[...] = jnp.zeros_like(dst_ref)
            dsmall_ref[...] = jnp.zeros_like(dsmall_ref)

        lb_all = _lower_bound(lbraw_ref)
        tri_lo, tri_up = _tri(True), _tri(False)
        rowid = lax.broadcasted_iota(jnp.int32, (CHUNK, HEADS * HEAD_DIM), 0)

        def chunk(it, carry):
            cc = G - 1 - it
            rs = pl.ds(pl.multiple_of(cc * CHUNK, CHUNK), CHUNK)
            heads = list(enumerate(_HEAD_LANES))
            cat = lambda parts: jnp.concatenate(parts, axis=1)
            per_head_mean = lambda x: cat([jnp.broadcast_to(_lanemean(x[:, s]), (CHUNK, HEAD_DIM)) for s in _HEAD_LANES])
            t = _hgrn_gates(q_ref[rs, :], f_ref[rs, :], lb_all, tri_lo)
            v, gate, o, dog, nw_all = i_ref[rs, :], g_ref[rs, :], oraw_ref[rs, :], dog_ref[rs, :], nw_ref[...]
            rs_o = lax.rsqrt(per_head_mean(o * o) + RMS_EPS)
            xhat = o * rs_o
            sgg = _sigmoid(gate)
            d_on = dog * (gate * sgg)
            dz_ref[rs, 1536:2048] = dog * (xhat * nw_all) * (sgg * (1.0 + gate * (1.0 - sgg)))
            dxh = d_on * nw_all
            do = rs_o * (dxh - xhat * per_head_mean(dxh * xhat))
            dsmall_ref[:, 512:1024] += _rowsum(d_on * xhat)
            st = [sp_ref[cc, h] for h in range(HEADS)]
            dst = [dst_ref[h] for h in range(HEADS)]
            a = [jnp.where(tri_lo, _bdot(t["qa"][:, s], t["ka"][:, s], NT), 0.0) for s in _HEAD_LANES]
            da = [jnp.where(tri_lo, _bdot(do[:, s], v[:, s], NT), 0.0) for s in _HEAD_LANES]
            dqb = cat([_bdot(do[:, s], st[h]) for h, s in heads])
            dkl = cat([_bdot(v[:, s], dst[h]) for h, s in heads])
            dv_ = cat([_bdot(t["kl"][:, s], dst[h], NT) + _bdot(a[h], do[:, s], TN) for h, s in heads])
            dqa = cat([_bdot(da[h], t["ka"][:, s]) for h, s in heads])
            dka = cat([_bdot(da[h], t["qa"][:, s], TN) for h, s in heads])
            ddecay = cat([_rowsum(dst[h] * st[h]) for h in range(HEADS)])
            for h, s in heads:
                dst_ref[h] = dst[h] * t["decay"][:, s] + _bdot(do[:, s], t["qb"][:, s], TN)
            pa, pk, pb, pl_ = dqa * t["qa"], dka * t["ka"], dqb * t["qb"], dkl * t["kl"]
            db = pa - pk + pb - pl_
            db = db + jnp.where(rowid == CHUNK // 2 - 1, _rowsum(pk - pa), 0.0)
            db = db + jnp.where(rowid == CHUNK - 1, _rowsum(pl_) + ddecay * t["decay"], 0.0)
            dlogf = _hdot(tri_up.astype(F32), db)
            dforget = dlogf / t["forget"] - (dka * t["e2"] + dkl * t["e3"])
            sg = t["sg"]
            dz_ref[rs, 0:512] = dqa * t["e1"] + dqb * t["e4"]
            dz_ref[rs, 512:1024] = dforget * (1.0 - lb_all) * sg * (1.0 - sg)
            dz_ref[rs, 1024:1536] = dv_
            dsmall_ref[:, 0:512] += _rowsum(dforget * (1.0 - sg))
            return carry

        lax.fori_loop(0, G, chunk, 0, unroll=4)

    col = lambda j: pl.BlockSpec((rows, 512), lambda r, j=j: (ng - 1 - r, j))
    return _call(
        body, "hgrn_bwd", (dmixcat, z, z, z, z, oraw, sprev, lbraw, nw), grid=(ng,),
        in_specs=[col(0), col(0), col(1), col(2), col(3), col(0),
                  pl.BlockSpec((G, HEADS, HEAD_DIM, HEAD_DIM), lambda r: (ng - 1 - r, 0, 0, 0)),
                  _full((2, 512)), _full((1, 512))],
        out_specs=[pl.BlockSpec((rows, 2048), lambda r: (ng - 1 - r, 0)), _full((1, 1024))],
        out_shape=[jax.ShapeDtypeStruct((T, 2048), F32), jax.ShapeDtypeStruct((1, 1024), F32)],
        scratch_shapes=[pltpu.VMEM((HEADS, HEAD_DIM, HEAD_DIM), F32)], exchange=exchange)


def _diag_mask(t):
    r = lax.broadcasted_iota(jnp.int32, (t, t), 0)
    c = lax.broadcasted_iota(jnp.int32, (t, t), 1)
    return r >= c


def _attn_fwd(q, k, v, exchange=None):
    _, T, _ = q.shape
    t = min(ATT_TILE, T)

    def body(q_ref, k_ref, v_ref, o_ref, lse_ref):
        i = pl.program_id(1)
        qb = q_ref[...]

        rows = lambda j: pl.ds(pl.multiple_of(j * t, t), t)

        def logits(j, masked):
            s = _dot(qb, k_ref[rows(j), :], NT)
            return jnp.where(_diag_mask(t), s, NEG_BIG) if masked else s

        def absorb(s, j, carry):
            m, l, acc = carry
            mn = jnp.maximum(m, jnp.max(s, axis=-1, keepdims=True))
            p = jnp.exp2(s - mn)
            al = jnp.exp2(m - mn)
            return mn, al * l + jnp.sum(p, axis=-1, keepdims=True), al * acc + _dot(p.astype(BF16), v_ref[rows(j), :])

        def pair(j0, carry, last_masked):
            s0, s1 = logits(j0, False), logits(j0 + 1, last_masked)
            return absorb(s1, j0 + 1, absorb(s0, j0, carry))

        init = (jnp.full((t, 1), NEG_BIG, F32), jnp.zeros((t, 1), F32), jnp.zeros((t, HEAD_DIM), F32))
        carry = lax.fori_loop(0, i // 2, lambda jj, c: pair(2 * jj, c, False), init)
        m, l, acc = lax.cond(i % 2 == 1, lambda c: pair(i - 1, c, True),
                             lambda c: absorb(logits(i, True), i, c), carry)
        o_ref[...] = acc / l
        lse_ref[...] = jnp.broadcast_to(m + jnp.log2(l), (t, HEAD_DIM))

    return _call(
        body, "attn_fwd", (q, k, v), grid=(HEADS, T // t),
        in_specs=[pl.BlockSpec((None, t, QK_PAD), lambda h, i: (h, i, 0)),
                  pl.BlockSpec((None, T, QK_PAD), lambda h, i: (h, 0, 0)),
                  pl.BlockSpec((None, T, HEAD_DIM), lambda h, i: (h, 0, 0))],
        out_specs=[pl.BlockSpec((t, HEAD_DIM), lambda h, i: (i, h)),
                   pl.BlockSpec((None, t, HEAD_DIM), lambda h, i: (h, i, 0))],
        out_shape=[jax.ShapeDtypeStruct((T, HEADS * HEAD_DIM), F32), jax.ShapeDtypeStruct((HEADS, T, HEAD_DIM), F32)],
        exchange=exchange)


def _attn_bwd(q, k, v, dmixcat, o, lse, exchange=None):
    _, T, _ = q.shape
    t = min(ATT_TILE, T)
    nq = T // t

    def body(q_ref, k_ref, v_ref, do_ref, o_ref, lse_ref, dq_ref, dk_ref, dv_ref, delta_ref):
        j = pl.program_id(1)

        @pl.when(j == 0)
        def _():
            dq_ref[...] = jnp.zeros_like(dq_ref)

            def fill(i, carry):
                rs = pl.ds(pl.multiple_of(i * t, t), t)
                delta_ref[rs, :] = jnp.broadcast_to(
                    jnp.sum(do_ref[rs, :] * o_ref[rs, :], axis=-1, keepdims=True), (t, HEAD_DIM))
                return carry

            lax.fori_loop(0, nq, fill, 0)

        kb, vb = k_ref[...], v_ref[...]

        def steps(blocks, carry):
            dk, dv = carry
            rs = [pl.ds(pl.multiple_of(i * t, t), t) for i, _ in blocks]
            qb = [q_ref[r, :] for r in rs]
            dob = [do_ref[r, :].astype(BF16) for r in rs]
            s = [_dot(b, kb, NT) for b in qb]
            dp = [_dot(b, vb, NT) for b in dob]
            for n, (_, masked) in enumerate(blocks):
                p = jnp.exp2(s[n] - lse_ref[rs[n], 0:1])
                if masked:
                    p = jnp.where(_diag_mask(t), p, 0.0)
                ds = (p * (dp[n] - delta_ref[rs[n], 0:1]) * LN2).astype(BF16)
                dq_ref[rs[n], :] += _dot(ds, kb)
                dk = dk + _dot(ds, qb[n], TN)
                dv = dv + _dot(p.astype(BF16), dob[n], TN)
            return dk, dv

        zero = (jnp.zeros((t, QK_PAD), F32), jnp.zeros((t, HEAD_DIM), F32))
        rest = nq - 1 - j
        carry = lax.cond(rest % 2 == 1, lambda c: steps([(j, True), (j + 1, False)], c),
                         lambda c: steps([(j, True)], c), zero)
        first = j + 1 + rest % 2
        dk, dv = lax.fori_loop(0, rest // 2, lambda n, c: steps([(first + 2 * n, False), (first + 2 * n + 1, False)], c),
                               carry)
        dk_ref[...] = dk
        dv_ref[...] = dv

    return _call(
        body, "attn_bwd", (q, k, v, dmixcat, o, lse), grid=(HEADS, nq),
        in_specs=[pl.BlockSpec((None, T, QK_PAD), lambda h, j: (h, 0, 0)),
                  pl.BlockSpec((None, t, QK_PAD), lambda h, j: (h, j, 0)),
                  pl.BlockSpec((None, t, HEAD_DIM), lambda h, j: (h, j, 0)),
                  pl.BlockSpec((T, HEAD_DIM), lambda h, j: (0, HEADS + h)),
                  pl.BlockSpec((T, HEAD_DIM), lambda h, j: (0, h)),
                  pl.BlockSpec((None, T, HEAD_DIM), lambda h, j: (h, 0, 0))],
        out_specs=[pl.BlockSpec((None, T, QK_PAD), lambda h, j: (h, 0, 0)),
                   pl.BlockSpec((None, t, QK_PAD), lambda h, j: (h, j, 0)),
                   pl.BlockSpec((None, t, HEAD_DIM), lambda h, j: (h, j, 0))],
        out_shape=[jax.ShapeDtypeStruct((HEADS, T, QK_PAD), F32), jax.ShapeDtypeStruct((HEADS, T, QK_PAD), F32),
                   jax.ShapeDtypeStruct((HEADS, T, HEAD_DIM), F32)],
        scratch_shapes=[pltpu.VMEM((T, HEAD_DIM), F32)], exchange=exchange)


def _ln_fwd(r):
    mu = _lanemean(r)
    xc = r - mu
    rstd = lax.rsqrt(_lanemean(xc * xc) + LN_EPS)
    return xc * rstd, rstd


def _ln_bwd(dxh, xhat, rstd):
    return rstd * (dxh - _lanemean(dxh) - xhat * _lanemean(dxh * xhat))


def _mix_ln1(o_hg, o_mla, w_out, x, g_a, ln1_g, ln1_b, sc_m, sh_m, exchange=None):
    T = x.shape[0]
    tm = min(ROW_TILE_SMALL, T)
    half = o_hg.shape[1]

    def body(hg_ref, mla_ref, w_ref, x_ref, ga_ref, g_ref, b_ref, sc_ref, sh_ref, mix_ref, xhat_ref, rstd_ref, u2_ref):
        mix = _dot(hg_ref[...], w_ref[0:half, :]) + _bdot(mla_ref[...], w_ref[half:, :])
        mix_ref[...] = mix
        xhat, rstd = _ln_fwd(ALPHA * x_ref[...] + (1.0 + ga_ref[...]) * mix)
        xhat_ref[...] = xhat
        rstd_ref[...] = jnp.broadcast_to(rstd, (tm, 128))
        u2_ref[...] = _modulate(xhat * g_ref[...] + b_ref[...], sc_ref[...], sh_ref[...]).astype(BF16)

    row = pl.BlockSpec((tm, D_MODEL), lambda i: (i, 0))
    vec = _full((1, D_MODEL))
    halfrow = pl.BlockSpec((tm, half), lambda i: (i, 0))
    return _call(
        body, "mix_ln1", (o_hg, o_mla, w_out, x, g_a, ln1_g, ln1_b, sc_m, sh_m), grid=(T // tm,),
        in_specs=[halfrow, halfrow, _full(w_out.shape), row, vec, vec, vec, vec, vec],
        out_specs=[row, row, pl.BlockSpec((tm, 128), lambda i: (i, 0)), row],
        out_shape=[jax.ShapeDtypeStruct((T, D_MODEL), F32), jax.ShapeDtypeStruct((T, D_MODEL), F32),
                   jax.ShapeDtypeStruct((T, 128), F32), jax.ShapeDtypeStruct((T, D_MODEL), BF16)],
        exchange=exchange)


def _mlp_fwd(u2, w1, w2, xhat1, ln1_g, ln1_b, g_m, ln2_g, ln2_b, target):
    T = u2.shape[0]
    tf = w1.shape[-1]
    nf = N_DEV // MLP_SLABS
    tm = min(ROW_TILE, T)

    def body(u2_ref, w1_ref, w2_ref, xhat_ref, g1_ref, b1_ref, gm_ref, g2_ref, b2_ref, tgt_ref,
             r_ref, dr2_ref, dh_ref, small_ref, acc_ref):
        i, f = pl.program_id(0), pl.program_id(1)
        dm = D_MODEL

        @pl.when((i == 0) & (f == 0))
        def _():
            small_ref[...] = jnp.zeros_like(small_ref)

        @pl.when(f == 0)
        def _():
            acc_ref[...] = jnp.zeros_like(acc_ref)

        u2t = u2_ref[...]
        part = None
        for s in range(MLP_SLABS):
            r = jnp.maximum(_dot(u2t, w1_ref[s]), 0.0)
            r_ref[:, s * tf:(s + 1) * tf] = r.astype(BF16)
            d = _bdot(r * r, w2_ref[s])
            part = d if part is None else part + d
        acc_ref[...] += part

        @pl.when(f == nf - 1)
        def _():
            h = acc_ref[...]
            x1 = xhat_ref[...] * g1_ref[...] + b1_ref[...]
            xhat2, rstd2 = _ln_fwd(ALPHA * x1 + (1.0 + gm_ref[...]) * h)
            err = xhat2 * g2_ref[...] + b2_ref[...] - tgt_ref[...]
            small_ref[:, 3 * dm:] += jnp.sum(0.5 * _lanemean(err * err), axis=0, keepdims=True)
            dy = err * (1.0 / D_MODEL)
            small_ref[:, dm:2 * dm] += _rowsum(dy * xhat2)
            small_ref[:, 2 * dm:3 * dm] += _rowsum(dy)
            dr2 = _ln_bwd(dy * g2_ref[...], xhat2, rstd2)
            dr2_ref[...] = dr2
            small_ref[:, 0:dm] += _rowsum(dr2 * h)
            dh_ref[...] = ((1.0 + gm_ref[...]) * dr2).astype(BF16)

    row = pl.BlockSpec((tm, D_MODEL), lambda i, f: (i, 0))
    vec = _full((1, D_MODEL))
    return pl.pallas_call(
        body, name="mlp_fwd", grid=(T // tm, nf),
        in_specs=[row, pl.BlockSpec((MLP_SLABS, D_MODEL, tf), lambda i, f: (f, 0, 0)),
                  pl.BlockSpec((MLP_SLABS, tf, D_MODEL), lambda i, f: (f, 0, 0)),
                  row, vec, vec, vec, vec, vec, row],
        out_specs=[pl.BlockSpec((tm, MLP_SLABS * tf), lambda i, f: (i, f)), row, row, _full((1, 3 * D_MODEL + 128))],
        out_shape=[jax.ShapeDtypeStruct((T, N_DEV * tf), BF16), jax.ShapeDtypeStruct((T, D_MODEL), F32),
                   jax.ShapeDtypeStruct((T, D_MODEL), BF16), jax.ShapeDtypeStruct((1, 3 * D_MODEL + 128), F32)],
        scratch_shapes=[pltpu.VMEM((tm, D_MODEL), F32)],
        compiler_params=_params(),
    )(u2, w1, w2, xhat1, ln1_g, ln1_b, g_m, ln2_g, ln2_b, target)


def _mlp_bwd(dh, w1, w2, r, dr2, xhat1, rstd1, mix, ln1_g, ln1_b, sc_m, g_a):
    T = dh.shape[0]
    tf = w1.shape[-1]
    nf = N_DEV // MLP_SLABS
    tm = min(ROW_TILE, T)

    def body(dh_ref, w1_ref, w2_ref, r_ref, dr2_ref, xhat_ref, rstd_ref, mix_ref, g1_ref, b1_ref, sc_ref, ga_ref,
             dhpre_ref, dr1_ref, dmix_ref, small_ref, acc_ref):
        i, f = pl.program_id(0), pl.program_id(1)
        dm = D_MODEL

        @pl.when((i == 0) & (f == 0))
        def _():
            small_ref[...] = jnp.zeros_like(small_ref)

        @pl.when(f == 0)
        def _():
            acc_ref[...] = jnp.zeros_like(acc_ref)

        dht = dh_ref[...]
        part = None
        for s in range(MLP_SLABS):
            cols = slice(s * tf, (s + 1) * tf)
            dhpre = (_dot(dht, w2_ref[s], NT) * (2.0 * r_ref[:, cols].astype(F32))).astype(BF16)
            dhpre_ref[:, cols] = dhpre
            d = _dot(dhpre, w1_ref[s], NT)
            part = d if part is None else part + d
        acc_ref[...] += part

        @pl.when(f == nf - 1)
        def _():
            du2 = acc_ref[...]
            xhat = xhat_ref[...]
            x1 = xhat * g1_ref[...] + b1_ref[...]
            dx1 = ALPHA * dr2_ref[...] + du2 * (1.0 + sc_ref[...])
            small_ref[:, 2 * dm:3 * dm] += _rowsum(du2 * x1)
            small_ref[:, dm:2 * dm] += _rowsum(du2)
            small_ref[:, 3 * dm:4 * dm] += _rowsum(dx1 * xhat)
            small_ref[:, 4 * dm:5 * dm] += _rowsum(dx1)
            dr1 = _ln_bwd(dx1 * g1_ref[...], xhat, rstd_ref[:, 0:1])
            dr1_ref[...] = dr1
            small_ref[:, 0:dm] += _rowsum(dr1 * mix_ref[...])
            dmix_ref[...] = ((1.0 + ga_ref[...]) * dr1).astype(BF16)

    row = pl.BlockSpec((tm, D_MODEL), lambda i, f: (i, 0))
    vec = _full((1, D_MODEL))
    return pl.pallas_call(
        body, name="mlp_bwd", grid=(T // tm, nf),
        in_specs=[row, pl.BlockSpec((MLP_SLABS, D_MODEL, tf), lambda i, f: (f, 0, 0)),
                  pl.BlockSpec((MLP_SLABS, tf, D_MODEL), lambda i, f: (f, 0, 0)),
                  pl.BlockSpec((tm, MLP_SLABS * tf), lambda i, f: (i, f)), row, row,
                  pl.BlockSpec((tm, 128), lambda i, f: (i, 0)), row, vec, vec, vec, vec],
        out_specs=[pl.BlockSpec((tm, MLP_SLABS * tf), lambda i, f: (i, f)), row, row, _full((1, 5 * D_MODEL))],
        out_shape=[jax.ShapeDtypeStruct((T, N_DEV * tf), BF16), jax.ShapeDtypeStruct((T, D_MODEL), F32),
                   jax.ShapeDtypeStruct((T, D_MODEL), BF16), jax.ShapeDtypeStruct((1, 5 * D_MODEL), F32)],
        scratch_shapes=[pltpu.VMEM((tm, D_MODEL), F32)],
        compiler_params=_params(),
    )(dh, w1, w2, r, dr2, xhat1, rstd1, mix, ln1_g, ln1_b, sc_m, g_a)


def _input_bwd(dz_h, dz_m, w_in_ext, x, dr1, sc_a, exchange=None):
    T = x.shape[0]
    tm = min(ROW_TILE, T)

    def body(dzh_ref, dzm_ref, w_ref, x_ref, dr1_ref, sc_ref, gx_ref, small_ref):
        @pl.when(pl.program_id(0) == 0)
        def _():
            small_ref[...] = jnp.zeros_like(small_ref)

        du = _bdot(dzh_ref[...], w_ref[0:2048, :]) + _bdot(dzm_ref[...], w_ref[2048:3072, :])
        gx_ref[...] = ALPHA * dr1_ref[...] + du * (1.0 + sc_ref[...])
        small_ref[:, D_MODEL:] += _rowsum(du * x_ref[...])
        small_ref[:, 0:D_MODEL] += _rowsum(du)

    row = pl.BlockSpec((tm, D_MODEL), lambda i: (i, 0))
    vec = _full((1, D_MODEL))
    return _call(
        body, "input_bwd", (dz_h, dz_m, w_in_ext, x, dr1, sc_a), grid=(T // tm,),
        in_specs=[pl.BlockSpec((tm, 2048), lambda i: (i, 0)), row, _full(w_in_ext.shape), row, row, vec],
        out_specs=[row, _full((1, 2 * D_MODEL))],
        out_shape=[jax.ShapeDtypeStruct((T, D_MODEL), F32), jax.ShapeDtypeStruct((1, 2 * D_MODEL), F32)],
        exchange=exchange)


def _adam_math(w, g, m, v):
    m = ADAM_B1 * m + (1.0 - ADAM_B1) * g
    v = ADAM_B2 * v + (1.0 - ADAM_B2) * (g * g)
    m_hat = m / (1.0 - ADAM_B1 ** ADAM_STEP)
    v_hat = v / (1.0 - ADAM_B2 ** ADAM_STEP)
    return -ADAM_LR * (m_hat / (jnp.sqrt(v_hat) + ADAM_EPS) + ADAM_WD * w), m, v


def _adam(g_slabs, w, m, v, name, g_fn=None, g_extra=()):
    R, C = w.shape
    tr = 256 if R % 256 == 0 else R
    ns = 0 if g_slabs is None else g_slabs.shape[0]
    ne = len(g_extra)

    def body(*refs):
        e_refs = refs[:ne]
        refs = refs[ne:]
        if ns:
            gs_ref, refs = refs[0], refs[1:]
        w_ref, m_ref, v_ref, g_ref, d_ref, nm_ref, nv_ref = refs
        if g_fn is not None:
            g = g_fn(*e_refs)
        else:
            g = gs_ref[0].astype(F32)
            for s in range(1, ns):
                g = g + gs_ref[s].astype(F32)
        d, nm, nv = _adam_math(w_ref[...], g, m_ref[...], v_ref[...])
        g_ref[...] = g
        d_ref[...] = d
        nm_ref[...] = nm
        nv_ref[...] = nv

    blk = pl.BlockSpec((tr, C), lambda i: (i, 0))
    in_specs = [pl.BlockSpec((tr, e.shape[1]), lambda i: (i, 0)) if e.shape[0] == R else _full(e.shape) for e in g_extra]
    args = list(g_extra)
    if ns:
        in_specs.append(pl.BlockSpec((ns, tr, C), lambda i: (0, i, 0)))
        args.append(g_slabs)
    return pl.pallas_call(
        body, name=name, grid=(R // tr,), in_specs=in_specs + [blk] * 3, out_specs=[blk] * 4,
        out_shape=[jax.ShapeDtypeStruct((R, C), F32)] * 4, compiler_params=_params(),
    )(*args, w, m, v)


def _adam_small(small_all, params):
    n = len(params)

    def body(*refs):
        s_ref, refs = refs[0], refs[1:]
        wmv, loss_ref, outs = refs[:3 * n], refs[3 * n], refs[3 * n + 1:]
        tot = s_ref[0]
        for i in range(1, N_DEV):
            tot = tot + s_ref[i]
        loss_ref[...] = tot[:, SMALL_W - 128:]
        for j, (w, _, _, off) in enumerate(params):
            w_ref, m_ref, v_ref = wmv[3 * j:3 * j + 3]
            g_ref, d_ref, nm_ref, nv_ref = outs[4 * j:4 * j + 4]
            if w.shape[0] == 2:
                lb = _lower_bound(w_ref)
                g0 = tot[:, off:off + w.shape[1]] * lb * (1.0 - lb)
                rows = [(slice(0, 1), g0), (slice(1, 2), -g0)]
            else:
                rows = [(slice(0, 1), tot[:, off:off + w.shape[1]])]
            for rs, g in rows:
                d, nm, nv = _adam_math(w_ref[rs, :], g, m_ref[rs, :], v_ref[rs, :])
                g_ref[rs, :], d_ref[rs, :], nm_ref[rs, :], nv_ref[rs, :] = g, d, nm, nv

    out_shape = [jax.ShapeDtypeStruct((1, 128), F32)]
    for w, _, _, _ in params:
        out_shape += [jax.ShapeDtypeStruct(w.shape, F32)] * 4
    res = pl.pallas_call(body, name="adam_small", out_shape=out_shape, compiler_params=_params())(
        small_all, *[a for w, m, v, _ in params for a in (w, m, v)])
    return res[0], [tuple(res[1 + 4 * j:5 + 4 * j]) for j in range(n)]


def _cols_from_slabs(g):
    s, r, c = g.shape
    return jnp.transpose(g, (1, 0, 2)).reshape(r, s * c)


def _slabs_from_cols(w):
    r, c = w.shape
    return jnp.transpose(w.reshape(r, N_DEV, c // N_DEV), (1, 0, 2))


def _rot_half_rows(wt):
    return jnp.concatenate([-wt[32:], wt[:32]], axis=0)


def _unrot_half_rows(dwt_rot):
    return jnp.concatenate([dwt_rot[32:], -dwt_rot[:32]], axis=0)


def _ext_in_t(g):
    k_in = g.shape[2]
    z64, z128 = jnp.zeros((64, k_in), BF16), jnp.zeros((128, k_in), BF16)
    wt = g.reshape(N_DEV * g.shape[1], k_in)
    main, wk = wt[:wt.shape[0] - ROPE_DIM], wt[wt.shape[0] - ROPE_DIM:]
    return jnp.concatenate([main, z128, wk, z64, z128, _rot_half_rows(wk), z64], axis=0)


def _ext_q_t(wt):
    r = wt.shape[1]
    z64, z128 = jnp.zeros((64, r), BF16), jnp.zeros((128, r), BF16)
    per = HEAD_DIM + ROPE_DIM
    main = [jnp.concatenate([wt[per * h:per * (h + 1)], z64], axis=0) for h in range(HEADS)]
    rot = [jnp.concatenate([z128, _rot_half_rows(wt[per * h + HEAD_DIM:per * (h + 1)]), z64], axis=0)
           for h in range(HEADS)]
    return jnp.concatenate(main + rot, axis=0)


def _ext_kv(w_kv_up):
    r = w_kv_up.shape[0]
    z128 = jnp.zeros((r, 128), BF16)
    wkv = w_kv_up.reshape(r, HEADS, 2 * HEAD_DIM)
    kpad = [jnp.concatenate([wkv[:, h, :HEAD_DIM], z128], axis=1) for h in range(HEADS)]
    vals = [wkv[:, h, HEAD_DIM:] for h in range(HEADS)]
    return jnp.concatenate(kpad + vals, axis=1)


def _grad_in_from_ext_t(dwt_h, dwt_m):
    dwk = dwt_m[512 + 128:512 + 192] + _unrot_half_rows(dwt_m[768 + 128:768 + 192])
    return jnp.concatenate([dwt_h, dwt_m[:512], dwk], axis=0)


def _grad_q_from_ext_t(dwq_ext_t):
    rows = []
    for h in range(HEADS):
        main, rot = dwq_ext_t[256 * h:256 * h + 256], dwq_ext_t[1024 + 256 * h:1280 + 256 * h]
        rows += [main[:128], main[128:192] + _unrot_half_rows(rot[128:192])]
    return jnp.concatenate(rows, axis=0)


def _grad_kv_from_ext(dwkv_ext):
    kvcols = []
    for h in range(HEADS):
        kvcols += [dwkv_ext[:, 256 * h:256 * h + 128], dwkv_ext[:, 1024 + 128 * h:1152 + 128 * h]]
    return jnp.concatenate(kvcols, axis=1)


SMALL_W = 6144 + 512 + 512 + 256 + 256 + 4 * 1024 + 128


def kernel(x, c, positions, w_ada, b_ada, w_in, hg_lower_bounds, hg_norm_w, mla_q_norm_w, w_q_up, mla_kv_norm_w, w_kv_up, w_out, ln1_g, ln1_b, w_mlp_in, w_mlp_out, ln2_g, ln2_b, loss_target, m_w_ada, m_b_ada, m_w_in, m_hg_lower_bounds, m_hg_norm_w, m_mla_q_norm_w, m_w_q_up, m_mla_kv_norm_w, m_w_kv_up, m_w_out, m_ln1_g, m_ln1_b, m_w_mlp_in, m_w_mlp_out, m_ln2_g, m_ln2_b, v_w_ada, v_b_ada, v_w_in, v_hg_lower_bounds, v_hg_norm_w, v_mla_q_norm_w, v_w_q_up, v_mla_kv_norm_w, v_w_kv_up, v_w_out, v_ln1_g, v_ln1_b, v_w_mlp_in, v_w_mlp_out, v_ln2_g, v_ln2_b):
    T = x.shape[1]
    me = 4 * lax.axis_index("x") + 2 * lax.axis_index("y") + lax.axis_index("c")
    xs, tgt = x[0], loss_target[0]
    transposed = ("w_in", "w_q_up")
    as_used = lambda n, a: a[0].T if n in transposed else a[0]
    big = {n: as_used(n, a) for n, a in dict(w_in=w_in, w_q_up=w_q_up, w_kv_up=w_kv_up, w_out=w_out,
                                              w_mlp_in=w_mlp_in, w_mlp_out=w_mlp_out).items()}
    names = list(big)

    bf = {n: big[n].astype(BF16) for n in names}
    g_in, g_c = _gather_two_level([bf["w_in"], c], name="gather_w_in")
    c_all = g_c.reshape(N_DEV, D_MODEL)

    ada_cols = w_ada.shape[2]
    mod_part, cond = _mod_part(c_all, w_ada[0], lax.dynamic_slice(b_ada, (0, me * ada_cols), (1, ada_cols)))
    (mod_all,) = _exchange([mod_part], scatter=False, name="gather_mod")
    mod_row = lax.dynamic_slice(mod_all, (0, me, 0), (N_DEV, 1, ada_cols)).reshape(1, N_DEV * ada_cols)
    sh_a, sc_a, g_a, sh_m, sc_m, g_m = [mod_row[:, D_MODEL * i:D_MODEL * (i + 1)] for i in range(6)]

    w_in_ext = _ext_in_t(g_in)
    z, (g_q, g_kv, g_out) = _matmul(xs, w_in_ext, "NT", "in_proj", a_fn=_modulate, extras=(sc_a, sh_a), tn=3072,
                                    exchange=_Exchange([bf["w_q_up"], bf["w_kv_up"], bf["w_out"]], False))
    wq_ext = _ext_q_t(g_q.reshape(N_DEV * g_q.shape[1], g_q.shape[2]))
    wkv_ext = _ext_kv(_cols_from_slabs(g_kv))
    w_out_full = g_out.reshape(D_MODEL, D_MODEL)
    inv_freq = 1.0 / (ROPE_THETA ** (jnp.arange(0, ROPE_DIM, 2, dtype=F32) / ROPE_DIM))
    zeros = lambda n: jnp.zeros((n,), F32)
    invf = jnp.concatenate([zeros(128), inv_freq, inv_freq, zeros(64)]).reshape(1, QK_PAD)
    m_one = jnp.concatenate([jnp.ones((128,), F32), zeros(128)]).reshape(1, QK_PAD)
    m_rot = jnp.concatenate([zeros(128), jnp.ones((64,), F32), zeros(64)]).reshape(1, QK_PAD)
    q, k, v, c1, s1, cqn, ckvn = _mla_pre(z, positions.reshape(T, 1), invf, m_one, m_rot, wq_ext, wkv_ext,
                                          mla_q_norm_w, mla_kv_norm_w)[0]
    (o_raw, o_gated, s_prev), (w1,) = _hgrn_fwd(z, hg_lower_bounds, hg_norm_w,
                                                exchange=_StagedGather(bf["w_mlp_in"]))
    (o_mla, lse), (w2,) = _attn_fwd(q, k, v, exchange=_StagedGather(bf["w_mlp_out"]))
    mix, xhat1, rstd1, u2 = _mix_ln1(o_gated, o_mla, w_out_full, xs, g_a, ln1_g, ln1_b, sc_m, sh_m)[0]
    r, dr2, dh, small_mlp_fwd = _mlp_fwd(u2, w1, w2, xhat1, ln1_g, ln1_b, g_m, ln2_g, ln2_b, tgt)

    dhpre, dr1, dmix, small_mlp_bwd = _mlp_bwd(dh, w1, w2, r, dr2, xhat1, rstd1, mix, ln1_g, ln1_b, sc_m, g_a)
    received = {}
    dw2 = _matmul(r, dh, "TN", "wgrad_mlp_out", out_dtype=BF16, a_fn=_square, tm=1024, tk=2048)
    dw1 = _matmul(u2, dhpre, "TN", "wgrad_mlp_in", out_dtype=BF16, tm=1024, tk=2048, out_slabs=N_DEV)
    dmixcat = _matmul(dmix, w_out_full, "NT", "dgrad_out")
    dw_out = jnp.concatenate([_matmul(o_gated, dmix, "TN", "wgrad_out_hg", out_dtype=BF16, tk=2048),
                              _matmul(o_mla, dmix, "TN", "wgrad_out_mla", out_dtype=BF16, tk=2048)], axis=0)
    (dz_h, small_hgrn), (received["w_out"],) = _hgrn_bwd(
        dmixcat, z, o_raw, s_prev, hg_lower_bounds, hg_norm_w,
        exchange=_Exchange([dw_out.reshape(N_DEV, D_MODEL // N_DEV, D_MODEL)], True))
    (dq, dk, dv), (received["w_mlp_in"], received["w_mlp_out"]) = _attn_bwd(
        q, k, v, dmixcat, o_mla, lse,
        exchange=_Exchange([dw1, dw2.reshape(N_DEV, dw2.shape[0] // N_DEV, D_MODEL)], True))
    dz_m, dq_ext, dkv_ext, small_mla = _mla_bwd(dq, dk, dv, z, c1, s1, wq_ext, wkv_ext, mla_q_norm_w, mla_kv_norm_w)
    dwq_t = _grad_q_from_ext_t(_matmul(dq_ext, cqn, "TN", "wgrad_q_up", tm=1024, tk=2048))
    dwkv = _grad_kv_from_ext(_matmul(ckvn, dkv_ext, "TN", "wgrad_kv_up", tn=1536, tk=2048))
    qkv_slabs = [dwq_t.reshape((N_DEV, dwq_t.shape[0] // N_DEV, dwq_t.shape[1])).astype(BF16),
                 _slabs_from_cols(dwkv).astype(BF16)]
    dwt_h, (received["w_q_up"], received["w_kv_up"]) = _matmul(
        dz_h, xs, "TN", "wgrad_in_h", b_fn=_modulate, extras=(sc_a, sh_a), tm=1024, tk=2048,
        exchange=_Exchange(qkv_slabs, True))
    dwt_m = _matmul(dz_m, xs, "TN", "wgrad_in_m", b_fn=_modulate, extras=(sc_a, sh_a), tm=1024, tk=2048)
    dw_in_t = _grad_in_from_ext_t(dwt_h, dwt_m)
    in_slabs = dw_in_t.reshape((N_DEV, dw_in_t.shape[0] // N_DEV, dw_in_t.shape[1])).astype(BF16)
    (grad_x, small_in), (received["w_in"],) = _input_bwd(
        dz_h, dz_m, w_in_ext, xs, dr1, sc_a, exchange=_Exchange([in_slabs], True))

    small = jnp.concatenate([small_in, small_mlp_bwd[:, :3 * D_MODEL], small_mlp_fwd[:, :D_MODEL], small_hgrn,
                             small_mla, small_mlp_bwd[:, 3 * D_MODEL:], small_mlp_fwd[:, D_MODEL:]], axis=1)
    assert small.shape == (1, SMALL_W)
    (small_all,) = _exchange([small], scatter=False, name="gather_small")

    moments = dict(w_in=(m_w_in, v_w_in), w_q_up=(m_w_q_up, v_w_q_up), w_kv_up=(m_w_kv_up, v_w_kv_up),
                   w_out=(m_w_out, v_w_out), w_mlp_in=(m_w_mlp_in, v_w_mlp_in), w_mlp_out=(m_w_mlp_out, v_w_mlp_out))
    res = {}
    for n in names:
        res[n] = _adam(received[n], big[n], as_used(n, moments[n][0]), as_used(n, moments[n][1]), name="adam_" + n)
    dmod_cols = lax.dynamic_slice(small_all.reshape(N_DEV, SMALL_W), (0, me * ada_cols), (N_DEV, ada_cols))
    cond_t = cond.T

    def ada_grad(ct_ref, dm_ref):
        g = ct_ref[:, 0:1] * dm_ref[0:1, :]
        for b in range(1, N_DEV):
            g = g + ct_ref[:, b:b + 1] * dm_ref[b:b + 1, :]
        return g

    res["w_ada"] = _adam(None, w_ada[0], m_w_ada[0], v_w_ada[0], name="adam_w_ada", g_fn=ada_grad,
                         g_extra=(cond_t, dmod_cols))

    small_params = [("b_ada", b_ada, m_b_ada, v_b_ada, 0),
                    ("hg_lower_bounds", hg_lower_bounds, m_hg_lower_bounds, v_hg_lower_bounds, 6144),
                    ("hg_norm_w", hg_norm_w, m_hg_norm_w, v_hg_norm_w, 6656),
                    ("mla_q_norm_w", mla_q_norm_w, m_mla_q_norm_w, v_mla_q_norm_w, 7168),
                    ("mla_kv_norm_w", mla_kv_norm_w, m_mla_kv_norm_w, v_mla_kv_norm_w, 7424),
                    ("ln1_g", ln1_g, m_ln1_g, v_ln1_g, 7680), ("ln1_b", ln1_b, m_ln1_b, v_ln1_b, 8704),
                    ("ln2_g", ln2_g, m_ln2_g, v_ln2_g, 9728), ("ln2_b", ln2_b, m_ln2_b, v_ln2_b, 10752)]
    loss_row, small_res = _adam_small(small_all, [p[1:] for p in small_params])
    for p, r4 in zip(small_params, small_res):
        res[p[0]] = r4
    loss = loss_row[0, 0]

    order = ["w_ada", "b_ada", "w_in", "hg_lower_bounds", "hg_norm_w", "mla_q_norm_w", "w_q_up", "mla_kv_norm_w",
             "w_kv_up", "w_out", "ln1_g", "ln1_b", "w_mlp_in", "w_mlp_out", "ln2_g", "ln2_b"]
    def as_given(n, a):
        if n in transposed:
            a = a.T
        return a[None] if n in big or n == "w_ada" else a

    shaped = {n: tuple(as_given(n, a) for a in res[n]) for n in order}
    outs = [loss, grad_x.reshape(1, T, D_MODEL)]
    for i in range(4):
        outs += [shaped[n][i] for n in order]
    return tuple(outs)
```

```python
import functools

import jax
import jax.numpy as jnp
import numpy as np
from jax import lax
from jax.experimental import pallas as pl
from jax.experimental.pallas import tpu as pltpu

F32, BF16 = jnp.float32, jnp.bfloat16
N_DEV = 8
D_MODEL = 1024
HEADS = 4
HEAD_DIM = 128
ROPE_DIM = 64
QK_PAD = 256
CHUNK = 64
ROPE_THETA = 10000.0
RMS_EPS = 1e-6
LN_EPS = 1e-5
ALPHA = 2.0 ** 0.25
ATT_SCALE = (HEAD_DIM + ROPE_DIM) ** -0.5
LN2 = float(np.log(2.0))
Q_PRESCALE = ATT_SCALE / LN2
ADAM_LR, ADAM_B1, ADAM_B2, ADAM_EPS, ADAM_WD, ADAM_STEP = 0.001, 0.9, 0.999, 1e-08, 0.01, 10
NEG_BIG = -1e30

ROW_TILE = 512
ROW_TILE_SMALL = 256
ATT_TILE = 512
HGRN_GROUP = 8
MLP_SLABS = 4
VMEM_LIMIT = 56 * 2 ** 20

NN = (((1,), (0,)), ((), ()))
NT = (((1,), (1,)), ((), ()))
TN = (((0,), (0,)), ((), ()))


def _dot(a, b, dims=NN):
    return lax.dot_general(a, b, dims, preferred_element_type=F32)


def _bdot(a, b, dims=NN):
    return lax.dot_general(a.astype(BF16), b.astype(BF16), dims, preferred_element_type=F32)


def _hdot(a, b, dims=NN):
    return lax.dot_general(a, b, dims, precision=lax.Precision.HIGHEST, preferred_element_type=F32)


def _params():
    return pltpu.CompilerParams(vmem_limit_bytes=VMEM_LIMIT)


def _sigmoid(x):
    return 1.0 / (1.0 + jnp.exp(-x))


def _rowsum(x):
    return jnp.sum(x, axis=0, keepdims=True)


def _lanemean(x):
    return jnp.mean(x, axis=-1, keepdims=True)


def _full(shape):
    nd = len(shape)
    return pl.BlockSpec(shape, lambda *_: (0,) * nd)


class _Exchange:
    def __init__(self, arrs, scatter):
        self.arrs, self.scatter, self.n, self.aliases, self.middle_at = list(arrs), scatter, len(arrs), [], 0.5
        self.out_shape = [jax.ShapeDtypeStruct((N_DEV,) + (a.shape[1:] if scatter else a.shape), a.dtype)
                          for a in self.arrs]
        n = self.n
        self.scratch = [pltpu.SemaphoreType.DMA((n, N_DEV - 1)), pltpu.SemaphoreType.DMA((n, N_DEV - 1)),
                        pltpu.SemaphoreType.DMA((n,))]

    def _copies(self, ins, outs, sems):
        send_sems, recv_sems, loc_sems = sems
        x, y, c = lax.axis_index("x"), lax.axis_index("y"), lax.axis_index("c")
        me = 4 * x + 2 * y + c
        copies = []
        for k in range(self.n):
            src_of = (lambda i, k=k: ins[k].at[i]) if self.scatter else (lambda i, k=k: ins[k])
            copies.append((pltpu.make_async_copy(src_of(me), outs[k].at[me], loc_sems.at[k]), None))
            for p in range(1, N_DEV):
                px = (1 - x) if p & 4 else x
                py = (1 - y) if p & 2 else y
                pc = (1 - c) if p & 1 else c
                peer = 4 * px + 2 * py + pc
                both = dict(send_sem=send_sems.at[k, p - 1], recv_sem=recv_sems.at[k, p - 1],
                            device_id=(px, py, pc), device_id_type=pl.DeviceIdType.MESH)
                send = pltpu.make_async_remote_copy(src_ref=src_of(peer), dst_ref=outs[k].at[me], **both)
                recv = pltpu.make_async_remote_copy(src_ref=src_of(peer), dst_ref=outs[k].at[peer], **both)
                copies.append((send, recv))
        return copies

    def start(self, ins, outs, sems):
        for first, _ in self._copies(ins, outs, sems):
            first.start()

    def middle(self, ins, outs, sems):
        pass

    def wait(self, ins, outs, sems):
        for first, recv in self._copies(ins, outs, sems):
            if recv is None:
                first.wait()
            else:
                recv.wait_recv()
                first.wait_send()


class _StagedGather:
    def __init__(self, arr):
        self.arrs, self.aliases, self.middle_at = [arr], [], 0.8
        self.out_shape = [jax.ShapeDtypeStruct((N_DEV,) + arr.shape, arr.dtype)]
        self.scratch = [pltpu.VMEM((N_DEV,) + arr.shape, arr.dtype), pltpu.SemaphoreType.DMA((7,)),
                        pltpu.SemaphoreType.DMA((7,)), pltpu.SemaphoreType.DMA((2,))]

    def _parts(self, scr):
        stage, send_sems, recv_sems, loc_sems = scr
        x, y, c = lax.axis_index("x"), lax.axis_index("y"), lax.axis_index("c")
        me, sibling = (x, y, c), (x, y, 1 - c)
        chips = [(1 - x, y), (x, 1 - y), (1 - x, 1 - y)]

        def copy(j, block, to):
            px, py, pc = block
            slot = stage.at[4 * px + 2 * py + pc]
            return pltpu.make_async_remote_copy(src_ref=slot, dst_ref=slot, send_sem=send_sems.at[j],
                                                recv_sem=recv_sems.at[j], device_id=to,
                                                device_id_type=pl.DeviceIdType.MESH)

        return stage, loc_sems, me, sibling, chips, c, copy

    def start(self, ins, outs, scr):
        stage, loc_sems, me, sibling, chips, c, copy = self._parts(scr)
        x, y, _ = me
        own = pltpu.make_async_copy(ins[0], stage.at[4 * x + 2 * y + c], loc_sems.at[0])
        own.start()
        own.wait()
        copy(0, me, sibling).start()
        for j, chip in enumerate(chips):
            copy(1 + j, me, (*chip, c)).start()

    def middle(self, ins, outs, scr):
        stage, loc_sems, me, sibling, chips, c, copy = self._parts(scr)
        for j, chip in enumerate(chips):
            copy(1 + j, (*chip, c), me).wait_recv()
            copy(4 + j, (*chip, c), sibling).start()

    def wait(self, ins, outs, scr):
        stage, loc_sems, me, sibling, chips, c, copy = self._parts(scr)
        copy(0, sibling, me).wait_recv()
        for j, chip in enumerate(chips):
            copy(4 + j, (*chip, 1 - c), me).wait_recv()
        copy(0, me, sibling).wait_send()
        for j, chip in enumerate(chips):
            copy(1 + j, me, (*chip, c)).wait_send()
            copy(4 + j, (*chip, c), sibling).wait_send()
        whole = pltpu.make_async_copy(stage, outs[0], loc_sems.at[1])
        whole.start()
        whole.wait()


class _StagedScatter:
    def __init__(self, slabs, middle_at=0.2):
        _, r, c = slabs.shape
        self.arrs, self.aliases, self.middle_at = [slabs], [], middle_at
        self.out_shape = [jax.ShapeDtypeStruct((4, r, c), slabs.dtype)]
        self.scratch = [pltpu.VMEM((N_DEV, r, c), slabs.dtype), pltpu.VMEM((4, r, c), slabs.dtype),
                        pltpu.VMEM((3, r, c), slabs.dtype), pltpu.SemaphoreType.DMA((4,)), pltpu.SemaphoreType.DMA((4,)),
                        pltpu.SemaphoreType.DMA((3,)), pltpu.SemaphoreType.DMA((3,)), pltpu.SemaphoreType.DMA((4,))]

    def _parts(self, scr):
        stage, from_sib, from_chips, sib_send, sib_recv, ici_send, ici_recv, loc_sems = scr
        x, y, c = lax.axis_index("x"), lax.axis_index("y"), lax.axis_index("c")
        chips = [(1 - x, y), (x, 1 - y), (1 - x, 1 - y)]

        def to_sibling(j):
            return pltpu.make_async_remote_copy(src_ref=stage.at[2 * j + 1 - c], dst_ref=from_sib.at[j],
                                                send_sem=sib_send.at[j], recv_sem=sib_recv.at[j],
                                                device_id=(x, y, 1 - c), device_id_type=pl.DeviceIdType.MESH)

        def to_chip(k):
            px, py = chips[k]
            return pltpu.make_async_remote_copy(src_ref=stage.at[4 * px + 2 * py + c], dst_ref=from_chips.at[k],
                                                send_sem=ici_send.at[k], recv_sem=ici_recv.at[k],
                                                device_id=(px, py, c), device_id_type=pl.DeviceIdType.MESH)

        return stage, from_sib, from_chips, loc_sems, (x, y, c), chips, to_sibling, to_chip

    def start(self, ins, outs, scr):
        stage, _, _, loc_sems, _, _, to_sibling, _ = self._parts(scr)
        load = pltpu.make_async_copy(ins[0], stage, loc_sems.at[0])
        load.start()
        load.wait()
        for j in range(4):
            to_sibling(j).start()

    def middle(self, ins, outs, scr):
        stage, from_sib, _, _, (x, y, c), _, to_sibling, to_chip = self._parts(scr)
        for j in range(4):
            to_sibling(j).wait_recv()
            mine = stage.at[2 * j + c]
            mine[...] = (mine[...].astype(F32) + from_sib[j].astype(F32)).astype(mine.dtype)
        for k in range(3):
            to_chip(k).start()

    def wait(self, ins, outs, scr):
        stage, _, from_chips, loc_sems, (x, y, c), chips, to_sibling, to_chip = self._parts(scr)
        writes = [pltpu.make_async_copy(stage.at[4 * x + 2 * y + c], outs[0].at[2 * x + y], loc_sems.at[0])]
        for k, (px, py) in enumerate(chips):
            to_chip(k).wait_recv()
            writes.append(pltpu.make_async_copy(from_chips.at[k], outs[0].at[2 * px + py], loc_sems.at[1 + k]))
        for w in writes:
            w.start()
        for j in range(4):
            to_sibling(j).wait_send()
        for k in range(3):
            to_chip(k).wait_send()
        for w in writes:
            w.wait()


def _call(body, name, args, out_shape, grid=(), in_specs=(), out_specs=(), scratch_shapes=(), exchange=None):
    if exchange is None:
        return pl.pallas_call(body, name=name, grid=grid, in_specs=list(in_specs), out_specs=list(out_specs),
                              out_shape=list(out_shape), scratch_shapes=list(scratch_shapes),
                              compiler_params=_params())(*args), None
    exs = list(exchange) if isinstance(exchange, (list, tuple)) else [exchange]
    ni, no, ns = len(args), len(out_shape), len(scratch_shapes)
    nxi, nxo = sum(len(e.arrs) for e in exs), sum(len(e.out_shape) for e in exs)
    steps = int(np.prod(grid))
    mid_step = lambda e: min(max(int(steps * e.middle_at), 1), steps - 1)
    aliases, iat, oat = {}, ni, no
    for e in exs:
        for src, dst in e.aliases:
            aliases[iat + src] = oat + dst
        iat, oat = iat + len(e.arrs), oat + len(e.out_shape)

    def wrapped(*refs):
        a, xi = refs[:ni], refs[ni:ni + nxi]
        o, xo = refs[ni + nxi:ni + nxi + no], refs[ni + nxi + no:ni + nxi + no + nxo]
        s, xs = refs[ni + nxi + no + nxo:ni + nxi + no + nxo + ns], refs[ni + nxi + no + nxo + ns:]
        parts, iat, oat, sat = [], 0, 0, 0
        for e in exs:
            parts.append((e, xi[iat:iat + len(e.arrs)], xo[oat:oat + len(e.out_shape)], xs[sat:sat + len(e.scratch)]))
            iat, oat, sat = iat + len(e.arrs), oat + len(e.out_shape), sat + len(e.scratch)
        step = 0
        for d, g in enumerate(grid):
            step = step * g + pl.program_id(d)

        @pl.when(step == 0)
        def _():
            for e, ins, outs, sems in parts:
                e.start(ins, outs, sems)

        for at_step in sorted({mid_step(e) for e in exs}):
            @pl.when(step == at_step)
            def _():
                for e, ins, outs, sems in parts:
                    if mid_step(e) == at_step:
                        e.middle(ins, outs, sems)

        body(*a, *o, *s)

        @pl.when(step == steps - 1)
        def _():
            for e, ins, outs, sems in parts:
                e.wait(ins, outs, sems)

    hbm = pl.BlockSpec(memory_space=pltpu.HBM)
    res = pl.pallas_call(
        wrapped, name=name, grid=grid, in_specs=list(in_specs) + [hbm] * nxi, out_specs=list(out_specs) + [hbm] * nxo,
        out_shape=list(out_shape) + [o_ for e in exs for o_ in e.out_shape],
        scratch_shapes=list(scratch_shapes) + [s_ for e in exs for s_ in e.scratch],
        input_output_aliases=aliases, compiler_params=_params())(*args, *[a_ for e in exs for a_ in e.arrs])
    return res[:no], res[no:]


def _gather_two_level(arrs, name):
    n = len(arrs)
    out_shape = [jax.ShapeDtypeStruct((N_DEV,) + a.shape, a.dtype) for a in arrs]

    def body(*refs):
        ins, outs = refs[:n], refs[n:2 * n]
        send_sems, recv_sems, loc_sems = refs[2 * n:]
        x, y, c = lax.axis_index("x"), lax.axis_index("y"), lax.axis_index("c")
        me, sibling = (x, y, c), (x, y, 1 - c)
        chips = [(1 - x, y), (x, 1 - y), (1 - x, 1 - y)]

        def copy(k, j, block, to, src=None):
            px, py, pc = block
            dst = outs[k].at[4 * px + 2 * py + pc]
            return pltpu.make_async_remote_copy(src_ref=dst if src is None else src, dst_ref=dst,
                                                send_sem=send_sems.at[k, j], recv_sem=recv_sems.at[k, j],
                                                device_id=to, device_id_type=pl.DeviceIdType.MESH)

        mine = [pltpu.make_async_copy(ins[k], outs[k].at[4 * x + 2 * y + c], loc_sems.at[k]) for k in range(n)]
        first = []
        for k in range(n):
            mine[k].start()
            first.append(copy(k, 0, me, sibling, src=ins[k]))
            first += [copy(k, 1 + j, me, (*chip, c), src=ins[k]) for j, chip in enumerate(chips)]
        for cp in first:
            cp.start()
        passed = []
        for j, chip in enumerate(chips):
            for k in range(n):
                copy(k, 1 + j, (*chip, c), me).wait_recv()
                passed.append(copy(k, 4 + j, (*chip, c), sibling))
                passed[-1].start()
        for k in range(n):
            copy(k, 0, sibling, me).wait_recv()
            for j, chip in enumerate(chips):
                copy(k, 4 + j, (*chip, 1 - c), me).wait_recv()
        for cp in first + passed:
            cp.wait_send()
        for cp in mine:
            cp.wait()

    vmem = pl.BlockSpec(memory_space=pltpu.VMEM)
    return pl.pallas_call(body, name=name, out_shape=out_shape, in_specs=[vmem] * n, out_specs=[vmem] * n,
                          scratch_shapes=[pltpu.SemaphoreType.DMA((n, 7)), pltpu.SemaphoreType.DMA((n, 7)),
                                          pltpu.SemaphoreType.DMA((n,))], compiler_params=_params())(*arrs)


def _exchange(arrs, scatter, name):
    ex = _Exchange(arrs, scatter)

    def body(*refs):
        ins, outs, sems = refs[:ex.n], refs[ex.n:2 * ex.n], refs[2 * ex.n:]
        ex.start(ins, outs, sems)
        ex.wait(ins, outs, sems)

    hbm = pl.BlockSpec(memory_space=pltpu.HBM)
    return pl.pallas_call(body, name=name, out_shape=ex.out_shape, in_specs=[hbm] * ex.n, out_specs=[hbm] * ex.n,
                          scratch_shapes=ex.scratch)(*ex.arrs)


def _matmul(a, b, mode, name, out_dtype=F32, tm=512, tn=1024, tk=1024, a_fn=None, b_fn=None, extras=(),
            out_slabs=None, exchange=None):
    assert not (a_fn and b_fn) and not (b_fn and mode == "NT")
    if mode == "NN":
        (M, K), N = a.shape, b.shape[1]
    elif mode == "NT":
        (M, K), N = a.shape, b.shape[0]
    else:
        (K, M), N = a.shape, b.shape[1]
    if out_slabs:
        tn = N // out_slabs
    tm, tn, tk = min(tm, M), min(tn, N), min(tk, K)
    assert M % tm == 0 and N % tn == 0 and K % tk == 0, (name, M, N, K)
    nk = K // tk
    dims = {"NN": NN, "NT": NT, "TN": TN}[mode]
    ne = len(extras)

    def body(a_ref, b_ref, *rest):
        e_refs, o_ref, acc_ref = rest[:ne], rest[ne], rest[ne + 1]
        k = pl.program_id(2)

        @pl.when(k == 0)
        def _():
            acc_ref[...] = jnp.zeros_like(acc_ref)

        at, bt = a_ref[...], b_ref[...]
        if a_fn is not None:
            at = a_fn(at.astype(F32), *[e[...] for e in e_refs])
        if b_fn is not None:
            bt = b_fn(bt.astype(F32), *[e[...] for e in e_refs])
        acc_ref[...] += _bdot(at, bt, dims)

        @pl.when(k == nk - 1)
        def _():
            o_ref[...] = acc_ref[...].astype(out_dtype)

    if mode == "TN":
        a_spec = pl.BlockSpec((tk, tm), lambda i, j, k: (k, i))
        e_spec = pl.BlockSpec((1, tm), lambda i, j, k: (0, i))
    else:
        a_spec = pl.BlockSpec((tm, tk), lambda i, j, k: (i, k))
        e_spec = pl.BlockSpec((1, tk), lambda i, j, k: (0, k))
    if mode == "NT":
        b_spec = pl.BlockSpec((tn, tk), lambda i, j, k: (j, k))
    else:
        b_spec = pl.BlockSpec((tk, tn), lambda i, j, k: (k, j))
    if b_fn is not None:
        e_spec = pl.BlockSpec((1, tn), lambda i, j, k: (0, j))
    if out_slabs:
        o_shape = jax.ShapeDtypeStruct((out_slabs, M, tn), out_dtype)
        o_spec = pl.BlockSpec((None, tm, tn), lambda i, j, k: (j, i, 0))
    else:
        o_shape = jax.ShapeDtypeStruct((M, N), out_dtype)
        o_spec = pl.BlockSpec((tm, tn), lambda i, j, k: (i, j))
    (out,), got = _call(body, name, (a, b, *extras), [o_shape], grid=(M // tm, N // tn, nk),
                        in_specs=[a_spec, b_spec] + [e_spec] * ne, out_specs=[o_spec],
                        scratch_shapes=[pltpu.VMEM((tm, tn), F32)], exchange=exchange)
    return out if exchange is None else (out, got)


def _modulate(x, sc, sh):
    return x * (1.0 + sc) + sh


def _square(x):
    return x * x


def _mod_part(c_all, w_ada_s, b_s):
    def body(c_ref, w_ref, b_ref, mod_ref, cond_ref):
        cv = c_ref[...]
        cond = cv * _sigmoid(cv)
        cond_ref[...] = cond
        mod_ref[...] = _bdot(cond, w_ref[...]) + b_ref[...]

    return pl.pallas_call(
        body, name="mod_part",
        out_shape=[jax.ShapeDtypeStruct((N_DEV, w_ada_s.shape[1]), F32), jax.ShapeDtypeStruct(c_all.shape, F32)],
        compiler_params=_params(),
    )(c_all, w_ada_s, b_s)


def _rms_fwd(x, w):
    rs = lax.rsqrt(_lanemean(x * x) + RMS_EPS)
    return x * rs * w, rs


def _rms_bwd(x, rs, w, dy):
    xhat = x * rs
    dxh = dy * w
    return rs * (dxh - xhat * _lanemean(dxh * xhat)), dy * xhat


def _mla_pre(z, pos_col, invf, m_one, m_rot, wq_ext, wkv_ext, qnw, kvnw, exchange=None):
    T = z.shape[0]
    tm = min(ROW_TILE, T)

    def body(z_ref, pos_ref, invf_ref, mone_ref, mrot_ref, wq_ref, wkv_ref, qnw_ref, kvnw_ref,
             q_ref, k_ref, v_ref, c1_ref, s1_ref, cqn_ref, ckvn_ref):
        ang = pos_ref[...].astype(F32) * invf_ref[...]
        c1 = mone_ref[...] + mrot_ref[...] * jnp.cos(ang)
        s1 = mrot_ref[...] * jnp.sin(ang)
        c1_ref[...] = c1
        s1_ref[...] = s1
        cqn, _ = _rms_fwd(z_ref[:, 0:256], qnw_ref[...])
        ckvn, _ = _rms_fwd(z_ref[:, 256:512], kvnw_ref[...])
        cqn_ref[...] = cqn.astype(BF16)
        ckvn_ref[...] = ckvn.astype(BF16)
        qe = _bdot(cqn, wq_ref[...], NT)
        kve = _bdot(ckvn, wkv_ref[...])
        k_rope = z_ref[:, 512:768] * c1 + z_ref[:, 768:1024] * s1
        for h in range(HEADS):
            q_ref[h] = ((qe[:, 256 * h:256 * h + 256] * c1 + qe[:, 1024 + 256 * h:1280 + 256 * h] * s1)
                        * Q_PRESCALE).astype(BF16)
            k_ref[h] = (kve[:, 256 * h:256 * h + 256] + k_rope).astype(BF16)
            v_ref[h] = kve[:, 1024 + 128 * h:1152 + 128 * h].astype(BF16)

    row = lambda i: (i, 0)
    head = lambda i: (0, i, 0)
    return _call(
        body, "mla_pre", (z, pos_col, invf, m_one, m_rot, wq_ext, wkv_ext, qnw, kvnw), grid=(T // tm,),
        in_specs=[pl.BlockSpec((tm, 1024), lambda i: (i, 2)), pl.BlockSpec((tm, 1), row),
                  _full((1, 256)), _full((1, 256)), _full((1, 256)), _full(wq_ext.shape), _full(wkv_ext.shape),
                  _full((1, 256)), _full((1, 256))],
        out_specs=[pl.BlockSpec((HEADS, tm, QK_PAD), head), pl.BlockSpec((HEADS, tm, QK_PAD), head),
                   pl.BlockSpec((HEADS, tm, HEAD_DIM), head), pl.BlockSpec((tm, 256), row), pl.BlockSpec((tm, 256), row),
                   pl.BlockSpec((tm, 256), row), pl.BlockSpec((tm, 256), row)],
        out_shape=[jax.ShapeDtypeStruct((HEADS, T, QK_PAD), BF16), jax.ShapeDtypeStruct((HEADS, T, QK_PAD), BF16),
                   jax.ShapeDtypeStruct((HEADS, T, HEAD_DIM), BF16), jax.ShapeDtypeStruct((T, 256), F32),
                   jax.ShapeDtypeStruct((T, 256), F32), jax.ShapeDtypeStruct((T, 256), BF16),
                   jax.ShapeDtypeStruct((T, 256), BF16)], exchange=exchange)


def _mla_bwd(dq, dk, dv, z, c1, s1, wq_ext, wkv_ext, qnw, kvnw):
    T = z.shape[0]
    tm = min(ROW_TILE_SMALL, T)

    def body(dq_ref, dk_ref, dv_ref, z_ref, c1_ref, s1_ref, wq_ref, wkv_ref, qnw_ref, kvnw_ref,
             dz_ref, dqe_ref, dkve_ref, dnw_ref):
        @pl.when(pl.program_id(0) == 0)
        def _():
            dnw_ref[...] = jnp.zeros_like(dnw_ref)

        c1, s1 = c1_ref[...], s1_ref[...]
        dkpe = jnp.zeros((tm, QK_PAD), F32)
        for h in range(HEADS):
            dqh, dkh = dq_ref[h] * Q_PRESCALE, dk_ref[h]
            dqe_ref[:, 256 * h:256 * h + 256] = (dqh * c1).astype(BF16)
            dqe_ref[:, 1024 + 256 * h:1280 + 256 * h] = (dqh * s1).astype(BF16)
            dkve_ref[:, 256 * h:256 * h + 256] = dkh.astype(BF16)
            dkve_ref[:, 1024 + 128 * h:1152 + 128 * h] = dv_ref[h].astype(BF16)
            dkpe = dkpe + dkh
        dcqn = _dot(dqe_ref[...], wq_ref[...])
        dckvn = _dot(dkve_ref[...], wkv_ref[...], NT)
        cq, ckv = z_ref[:, 0:256], z_ref[:, 256:512]
        _, rsq = _rms_fwd(cq, qnw_ref[...])
        _, rskv = _rms_fwd(ckv, kvnw_ref[...])
        dcq, wq_rows = _rms_bwd(cq, rsq, qnw_ref[...], dcqn)
        dckv, wkv_rows = _rms_bwd(ckv, rskv, kvnw_ref[...], dckvn)
        dnw_ref[:, 0:256] += _rowsum(wq_rows)
        dnw_ref[:, 256:512] += _rowsum(wkv_rows)
        dz_ref[:, 0:256] = dcq
        dz_ref[:, 256:512] = dckv
        dz_ref[:, 512:768] = dkpe * c1
        dz_ref[:, 768:1024] = dkpe * s1

    row = lambda i: (i, 0)
    head = lambda i: (0, i, 0)
    return pl.pallas_call(
        body, name="mla_bwd", grid=(T // tm,),
        in_specs=[pl.BlockSpec((HEADS, tm, QK_PAD), head), pl.BlockSpec((HEADS, tm, QK_PAD), head),
                  pl.BlockSpec((HEADS, tm, HEAD_DIM), head), pl.BlockSpec((tm, 1024), lambda i: (i, 2)),
                  pl.BlockSpec((tm, 256), row), pl.BlockSpec((tm, 256), row), _full(wq_ext.shape), _full(wkv_ext.shape),
                  _full((1, 256)), _full((1, 256))],
        out_specs=[pl.BlockSpec((tm, 1024), row), pl.BlockSpec((tm, 2048), row), pl.BlockSpec((tm, 1536), row),
                   _full((1, 512))],
        out_shape=[jax.ShapeDtypeStruct((T, 1024), F32), jax.ShapeDtypeStruct((T, 2048), BF16),
                   jax.ShapeDtypeStruct((T, 1536), BF16), jax.ShapeDtypeStruct((1, 512), F32)],
        compiler_params=_params(),
    )(dq, dk, dv, z, c1, s1, wq_ext, wkv_ext, qnw, kvnw)


_HEAD_LANES = [slice(HEAD_DIM * h, HEAD_DIM * (h + 1)) for h in range(HEADS)]


def _lower_bound(lbraw_ref):
    a0, a1 = lbraw_ref[0:1, :], lbraw_ref[1:2, :]
    mx = jnp.maximum(a0, a1)
    e0, e1 = jnp.exp(a0 - mx), jnp.exp(a1 - mx)
    return e0 / (e0 + e1)


def _tri(lower):
    r = lax.broadcasted_iota(jnp.int32, (CHUNK, CHUNK), 0)
    c = lax.broadcasted_iota(jnp.int32, (CHUNK, CHUNK), 1)
    return (r >= c) if lower else (r <= c)


def _hgrn_gates(q, f, lb, tri_lo):
    sg = _sigmoid(f)
    forget = lb + (1.0 - lb) * sg
    k = 1.0 - forget
    b = _hdot(tri_lo.astype(F32), jnp.log(forget))
    b_ref, b_last = b[CHUNK // 2 - 1:CHUNK // 2, :], b[CHUNK - 1:CHUNK, :]
    e1, e2, e3, e4 = jnp.exp(b - b_ref), jnp.exp(b_ref - b), jnp.exp(b_last - b), jnp.exp(b)
    return dict(sg=sg, forget=forget, k=k, e1=e1, e2=e2, e3=e3, e4=e4, qa=q * e1, ka=k * e2, kl=k * e3, qb=q * e4,
                decay=jnp.exp(b_last))


def _hgrn_fwd(z, lbraw, nw, exchange=None):
    T = z.shape[0]
    G = min(HGRN_GROUP, T // CHUNK)
    rows = G * CHUNK
    n_chunks = T // CHUNK

    def body(q_ref, f_ref, i_ref, g_ref, lbraw_ref, nw_ref, oraw_ref, og_ref, sp_ref, st_ref):
        @pl.when(pl.program_id(0) == 0)
        def _():
            st_ref[...] = jnp.zeros_like(st_ref)

        lb_all = _lower_bound(lbraw_ref)
        tri_lo = _tri(True)

        def chunk(cc, carry):
            rs = pl.ds(pl.multiple_of(cc * CHUNK, CHUNK), CHUNK)
            t = _hgrn_gates(q_ref[rs, :], f_ref[rs, :], lb_all, tri_lo)
            v, gate = i_ref[rs, :], g_ref[rs, :]
            st = [st_ref[h] for h in range(HEADS)]
            a = [jnp.where(tri_lo, _bdot(t["qa"][:, s], t["ka"][:, s], NT), 0.0) for s in _HEAD_LANES]
            kv = [_bdot(v[:, s], t["kl"][:, s], TN) for s in _HEAD_LANES]
            o = [_bdot(a[h], v[:, s]) + _bdot(t["qb"][:, s], st[h], NT) for h, s in enumerate(_HEAD_LANES)]
            for h, s in enumerate(_HEAD_LANES):
                sp_ref[cc, h] = st[h]
                st_ref[h] = st[h] * t["decay"][:, s] + kv[h]
            oraw_ref[rs, :] = jnp.concatenate(o, axis=1)
            on = jnp.concatenate([_rms_fwd(o[h], nw_ref[:, s])[0] for h, s in enumerate(_HEAD_LANES)], axis=1)
            og_ref[rs, :] = (on * (gate * _sigmoid(gate))).astype(BF16)
            return carry

        lax.fori_loop(0, G, chunk, 0, unroll=4)

    col = lambda j: pl.BlockSpec((rows, 512), lambda r, j=j: (r, j))
    return _call(
        body, "hgrn_fwd", (z, z, z, z, lbraw, nw), grid=(T // rows,),
        in_specs=[col(0), col(1), col(2), col(3), _full((2, 512)), _full((1, 512))],
        out_specs=[col(0), col(0), pl.BlockSpec((G, HEADS, HEAD_DIM, HEAD_DIM), lambda r: (r, 0, 0, 0))],
        out_shape=[jax.ShapeDtypeStruct((T, 512), F32), jax.ShapeDtypeStruct((T, 512), BF16),
                   jax.ShapeDtypeStruct((n_chunks, HEADS, HEAD_DIM, HEAD_DIM), F32)],
        scratch_shapes=[pltpu.VMEM((HEADS, HEAD_DIM, HEAD_DIM), F32)], exchange=exchange)


def _hgrn_bwd(dmixcat, z, oraw, sprev, lbraw, nw, exchange=None):
    T = z.shape[0]
    G = min(HGRN_GROUP, T // CHUNK)
    rows = G * CHUNK
    ng = T // rows

    def body(dog_ref, q_ref, f_ref, i_ref, g_ref, oraw_ref, sp_ref, lbraw_ref, nw_ref,
             dz_ref, dsmall_ref, dst_ref):
        @pl.when(pl.program_id(0) == 0)
        def _():
            dst_ref[...] = jnp.zeros_like(dst_ref)
            dsmall_ref[...] = jnp.zeros_like(dsmall_ref)

        lb_all = _lower_bound(lbraw_ref)
        tri_lo, tri_up = _tri(True), _tri(False)
        rowid = lax.broadcasted_iota(jnp.int32, (CHUNK, HEADS * HEAD_DIM), 0)

        def chunk(it, carry):
            cc = G - 1 - it
            rs = pl.ds(pl.multiple_of(cc * CHUNK, CHUNK), CHUNK)
            heads = list(enumerate(_HEAD_LANES))
            cat = lambda parts: jnp.concatenate(parts, axis=1)
            per_head_mean = lambda x: cat([jnp.broadcast_to(_lanemean(x[:, s]), (CHUNK, HEAD_DIM)) for s in _HEAD_LANES])
            t = _hgrn_gates(q_ref[rs, :], f_ref[rs, :], lb_all, tri_lo)
            v, gate, o, dog, nw_all = i_ref[rs, :], g_ref[rs, :], oraw_ref[rs, :], dog_ref[rs, :], nw_ref[...]
            rs_o = lax.rsqrt(per_head_mean(o * o) + RMS_EPS)
            xhat = o * rs_o
            sgg = _sigmoid(gate)
            d_on = dog * (gate * sgg)
            dz_ref[rs, 1536:2048] = dog * (xhat * nw_all) * (sgg * (1.0 + gate * (1.0 - sgg)))
            dxh = d_on * nw_all
            do = rs_o * (dxh - xhat * per_head_mean(dxh * xhat))
            dsmall_ref[:, 512:1024] += _rowsum(d_on * xhat)
            st = [sp_ref[cc, h] for h in range(HEADS)]
            dst = [dst_ref[h] for h in range(HEADS)]
            a = [jnp.where(tri_lo, _bdot(t["qa"][:, s], t["ka"][:, s], NT), 0.0) for s in _HEAD_LANES]
            da = [jnp.where(tri_lo, _bdot(do[:, s], v[:, s], NT), 0.0) for s in _HEAD_LANES]
            dqb = cat([_bdot(do[:, s], st[h]) for h, s in heads])
            dkl = cat([_bdot(v[:, s], dst[h]) for h, s in heads])
            dv_ = cat([_bdot(t["kl"][:, s], dst[h], NT) + _bdot(a[h], do[:, s], TN) for h, s in heads])
            dqa = cat([_bdot(da[h], t["ka"][:, s]) for h, s in heads])
            dka = cat([_bdot(da[h], t["qa"][:, s], TN) for h, s in heads])
            ddecay = cat([_rowsum(dst[h] * st[h]) for h in range(HEADS)])
            for h, s in heads:
                dst_ref[h] = dst[h] * t["decay"][:, s] + _bdot(do[:, s], t["qb"][:, s], TN)
            pa, pk, pb, pl_ = dqa * t["qa"], dka * t["ka"], dqb * t["qb"], dkl * t["kl"]
            db = pa - pk + pb - pl_
            db = db + jnp.where(rowid == CHUNK // 2 - 1, _rowsum(pk - pa), 0.0)
            db = db + jnp.where(rowid == CHUNK - 1, _rowsum(pl_) + ddecay * t["decay"], 0.0)
            dlogf = _hdot(tri_up.astype(F32), db)
            dforget = dlogf / t["forget"] - (dka * t["e2"] + dkl * t["e3"])
            sg = t["sg"]
            dz_ref[rs, 0:512] = dqa * t["e1"] + dqb * t["e4"]
            dz_ref[rs, 512:1024] = dforget * (1.0 - lb_all) * sg * (1.0 - sg)
            dz_ref[rs, 1024:1536] = dv_
            dsmall_ref[:, 0:512] += _rowsum(dforget * (1.0 - sg))
            return carry

        lax.fori_loop(0, G, chunk, 0, unroll=4)

    col = lambda j: pl.BlockSpec((rows, 512), lambda r, j=j: (ng - 1 - r, j))
    return _call(
        body, "hgrn_bwd", (dmixcat, z, z, z, z, oraw, sprev, lbraw, nw), grid=(ng,),
        in_specs=[col(0), col(0), col(1), col(2), col(3), col(0),
                  pl.BlockSpec((G, HEADS, HEAD_DIM, HEAD_DIM), lambda r: (ng - 1 - r, 0, 0, 0)),
                  _full((2, 512)), _full((1, 512))],
        out_specs=[pl.BlockSpec((rows, 2048), lambda r: (ng - 1 - r, 0)), _full((1, 1024))],
        out_shape=[jax.ShapeDtypeStruct((T, 2048), F32), jax.ShapeDtypeStruct((1, 1024), F32)],
        scratch_shapes=[pltpu.VMEM((HEADS, HEAD_DIM, HEAD_DIM), F32)], exchange=exchange)


def _diag_mask(t):
    r = lax.broadcasted_iota(jnp.int32, (t, t), 0)
    c = lax.broadcasted_iota(jnp.int32, (t, t), 1)
    return r >= c


def _attn_fwd(q, k, v, exchange=None):
    _, T, _ = q.shape
    t = min(ATT_TILE, T)

    def body(q_ref, k_ref, v_ref, o_ref, lse_ref):
        i = pl.program_id(1)
        qb = q_ref[...]

        rows = lambda j: pl.ds(pl.multiple_of(j * t, t), t)

        def logits(j, masked):
            s = _dot(qb, k_ref[rows(j), :], NT)
            return jnp.where(_diag_mask(t), s, NEG_BIG) if masked else s

        def absorb(s, j, carry):
            m, l, acc = carry
            mn = jnp.maximum(m, jnp.max(s, axis=-1, keepdims=True))
            p = jnp.exp2(s - mn)
            al = jnp.exp2(m - mn)
            return mn, al * l + jnp.sum(p, axis=-1, keepdims=True), al * acc + _dot(p.astype(BF16), v_ref[rows(j), :])

        def pair(j0, carry, last_masked):
            s0, s1 = logits(j0, False), logits(j0 + 1, last_masked)
            return absorb(s1, j0 + 1, absorb(s0, j0, carry))

        init = (jnp.full((t, 1), NEG_BIG, F32), jnp.zeros((t, 1), F32), jnp.zeros((t, HEAD_DIM), F32))
        carry = lax.fori_loop(0, i // 2, lambda jj, c: pair(2 * jj, c, False), init)
        m, l, acc = lax.cond(i % 2 == 1, lambda c: pair(i - 1, c, True),
                             lambda c: absorb(logits(i, True), i, c), carry)
        o_ref[...] = acc / l
        lse_ref[...] = jnp.broadcast_to(m + jnp.log2(l), (t, HEAD_DIM))

    return _call(
        body, "attn_fwd", (q, k, v), grid=(HEADS, T // t),
        in_specs=[pl.BlockSpec((None, t, QK_PAD), lambda h, i: (h, i, 0)),
                  pl.BlockSpec((None, T, QK_PAD), lambda h, i: (h, 0, 0)),
                  pl.BlockSpec((None, T, HEAD_DIM), lambda h, i: (h, 0, 0))],
        out_specs=[pl.BlockSpec((t, HEAD_DIM), lambda h, i: (i, h)),
                   pl.BlockSpec((None, t, HEAD_DIM), lambda h, i: (h, i, 0))],
        out_shape=[jax.ShapeDtypeStruct((T, HEADS * HEAD_DIM), F32), jax.ShapeDtypeStruct((HEADS, T, HEAD_DIM), F32)],
        exchange=exchange)


def _attn_bwd(q, k, v, dmixcat, o, lse, exchange=None):
    _, T, _ = q.shape
    t = min(ATT_TILE, T)
    nq = T // t

    def body(q_ref, k_ref, v_ref, do_ref, o_ref, lse_ref, dq_ref, dk_ref, dv_ref, delta_ref):
        j = pl.program_id(1)

        @pl.when(j == 0)
        def _():
            dq_ref[...] = jnp.zeros_like(dq_ref)

            def fill(i, carry):
                rs = pl.ds(pl.multiple_of(i * t, t), t)
                delta_ref[rs, :] = jnp.broadcast_to(
                    jnp.sum(do_ref[rs, :] * o_ref[rs, :], axis=-1, keepdims=True), (t, HEAD_DIM))
                return carry

            lax.fori_loop(0, nq, fill, 0)

        kb, vb = k_ref[...], v_ref[...]

        def steps(blocks, carry):
            dk, dv = carry
            rs = [pl.ds(pl.multiple_of(i * t, t), t) for i, _ in blocks]
            qb = [q_ref[r, :] for r in rs]
            dob = [do_ref[r, :].astype(BF16) for r in rs]
            s = [_dot(b, kb, NT) for b in qb]
            dp = [_dot(b, vb, NT) for b in dob]
            for n, (_, masked) in enumerate(blocks):
                p = jnp.exp2(s[n] - lse_ref[rs[n], 0:1])
                if masked:
                    p = jnp.where(_diag_mask(t), p, 0.0)
                ds = (p * (dp[n] - delta_ref[rs[n], 0:1]) * LN2).astype(BF16)
                dq_ref[rs[n], :] += _dot(ds, kb)
                dk = dk + _dot(ds, qb[n], TN)
                dv = dv + _dot(p.astype(BF16), dob[n], TN)
            return dk, dv

        zero = (jnp.zeros((t, QK_PAD), F32), jnp.zeros((t, HEAD_DIM), F32))
        rest = nq - 1 - j
        carry = lax.cond(rest % 2 == 1, lambda c: steps([(j, True), (j + 1, False)], c),
                         lambda c: steps([(j, True)], c), zero)
        first = j + 1 + rest % 2
        dk, dv = lax.fori_loop(0, rest // 2, lambda n, c: steps([(first + 2 * n, False), (first + 2 * n + 1, False)], c),
                               carry)
        dk_ref[...] = dk
        dv_ref[...] = dv

    return _call(
        body, "attn_bwd", (q, k, v, dmixcat, o, lse), grid=(HEADS, nq),
        in_specs=[pl.BlockSpec((None, T, QK_PAD), lambda h, j: (h, 0, 0)),
                  pl.BlockSpec((None, t, QK_PAD), lambda h, j: (h, j, 0)),
                  pl.BlockSpec((None, t, HEAD_DIM), lambda h, j: (h, j, 0)),
                  pl.BlockSpec((T, HEAD_DIM), lambda h, j: (0, HEADS + h)),
                  pl.BlockSpec((T, HEAD_DIM), lambda h, j: (0, h)),
                  pl.BlockSpec((None, T, HEAD_DIM), lambda h, j: (h, 0, 0))],
        out_specs=[pl.BlockSpec((None, T, QK_PAD), lambda h, j: (h, 0, 0)),
                   pl.BlockSpec((None, t, QK_PAD), lambda h, j: (h, j, 0)),
                   pl.BlockSpec((None, t, HEAD_DIM), lambda h, j: (h, j, 0))],
        out_shape=[jax.ShapeDtypeStruct((HEADS, T, QK_PAD), F32), jax.ShapeDtypeStruct((HEADS, T, QK_PAD), F32),
                   jax.ShapeDtypeStruct((HEADS, T, HEAD_DIM), F32)],
        scratch_shapes=[pltpu.VMEM((T, HEAD_DIM), F32)], exchange=exchange)


def _ln_fwd(r):
    mu = _lanemean(r)
    xc = r - mu
    rstd = lax.rsqrt(_lanemean(xc * xc) + LN_EPS)
    return xc * rstd, rstd


def _ln_bwd(dxh, xhat, rstd):
    return rstd * (dxh - _lanemean(dxh) - xhat * _lanemean(dxh * xhat))


def _mix_ln1(o_hg, o_mla, w_out, x, g_a, ln1_g, ln1_b, sc_m, sh_m, exchange=None):
    T = x.shape[0]
    tm = min(ROW_TILE_SMALL, T)
    half = o_hg.shape[1]

    def body(hg_ref, mla_ref, w_ref, x_ref, ga_ref, g_ref, b_ref, sc_ref, sh_ref, mix_ref, xhat_ref, rstd_ref, u2_ref):
        mix = _dot(hg_ref[...], w_ref[0:half, :]) + _bdot(mla_ref[...], w_ref[half:, :])
        mix_ref[...] = mix
        xhat, rstd = _ln_fwd(ALPHA * x_ref[...] + (1.0 + ga_ref[...]) * mix)
        xhat_ref[...] = xhat
        rstd_ref[...] = jnp.broadcast_to(rstd, (tm, 128))
        u2_ref[...] = _modulate(xhat * g_ref[...] + b_ref[...], sc_ref[...], sh_ref[...]).astype(BF16)

    row = pl.BlockSpec((tm, D_MODEL), lambda i: (i, 0))
    vec = _full((1, D_MODEL))
    halfrow = pl.BlockSpec((tm, half), lambda i: (i, 0))
    return _call(
        body, "mix_ln1", (o_hg, o_mla, w_out, x, g_a, ln1_g, ln1_b, sc_m, sh_m), grid=(T // tm,),
        in_specs=[halfrow, halfrow, _full(w_out.shape), row, vec, vec, vec, vec, vec],
        out_specs=[row, row, pl.BlockSpec((tm, 128), lambda i: (i, 0)), row],
        out_shape=[jax.ShapeDtypeStruct((T, D_MODEL), F32), jax.ShapeDtypeStruct((T, D_MODEL), F32),
                   jax.ShapeDtypeStruct((T, 128), F32), jax.ShapeDtypeStruct((T, D_MODEL), BF16)],
        exchange=exchange)


def _mlp_fwd(u2, w1, w2, xhat1, ln1_g, ln1_b, g_m, ln2_g, ln2_b, target):
    T = u2.shape[0]
    tf = w1.shape[-1]
    nf = N_DEV // MLP_SLABS
    tm = min(ROW_TILE, T)

    def body(u2_ref, w1_ref, w2_ref, xhat_ref, g1_ref, b1_ref, gm_ref, g2_ref, b2_ref, tgt_ref,
             r_ref, dr2_ref, dh_ref, small_ref, acc_ref):
        i, f = pl.program_id(0), pl.program_id(1)
        dm = D_MODEL

        @pl.when((i == 0) & (f == 0))
        def _():
            small_ref[...] = jnp.zeros_like(small_ref)

        @pl.when(f == 0)
        def _():
            acc_ref[...] = jnp.zeros_like(acc_ref)

        u2t = u2_ref[...]
        part = None
        for s in range(MLP_SLABS):
            r = jnp.maximum(_dot(u2t, w1_ref[s]), 0.0)
            r_ref[:, s * tf:(s + 1) * tf] = r.astype(BF16)
            d = _bdot(r * r, w2_ref[s])
            part = d if part is None else part + d
        acc_ref[...] += part

        @pl.when(f == nf - 1)
        def _():
            h = acc_ref[...]
            x1 = xhat_ref[...] * g1_ref[...] + b1_ref[...]
            xhat2, rstd2 = _ln_fwd(ALPHA * x1 + (1.0 + gm_ref[...]) * h)
            err = xhat2 * g2_ref[...] + b2_ref[...] - tgt_ref[...]
            small_ref[:, 3 * dm:] += jnp.sum(0.5 * _lanemean(err * err), axis=0, keepdims=True)
            dy = err * (1.0 / D_MODEL)
            small_ref[:, dm:2 * dm] += _rowsum(dy * xhat2)
            small_ref[:, 2 * dm:3 * dm] += _rowsum(dy)
            dr2 = _ln_bwd(dy * g2_ref[...], xhat2, rstd2)
            dr2_ref[...] = dr2
            small_ref[:, 0:dm] += _rowsum(dr2 * h)
            dh_ref[...] = ((1.0 + gm_ref[...]) * dr2).astype(BF16)

    row = pl.BlockSpec((tm, D_MODEL), lambda i, f: (i, 0))
    vec = _full((1, D_MODEL))
    return pl.pallas_call(
        body, name="mlp_fwd", grid=(T // tm, nf),
        in_specs=[row, pl.BlockSpec((MLP_SLABS, D_MODEL, tf), lambda i, f: (f, 0, 0)),
                  pl.BlockSpec((MLP_SLABS, tf, D_MODEL), lambda i, f: (f, 0, 0)),
                  row, vec, vec, vec, vec, vec, row],
        out_specs=[pl.BlockSpec((tm, MLP_SLABS * tf), lambda i, f: (i, f)), row, row, _full((1, 3 * D_MODEL + 128))],
        out_shape=[jax.ShapeDtypeStruct((T, N_DEV * tf), BF16), jax.ShapeDtypeStruct((T, D_MODEL), F32),
                   jax.ShapeDtypeStruct((T, D_MODEL), BF16), jax.ShapeDtypeStruct((1, 3 * D_MODEL + 128), F32)],
        scratch_shapes=[pltpu.VMEM((tm, D_MODEL), F32)],
        compiler_params=_params(),
    )(u2, w1, w2, xhat1, ln1_g, ln1_b, g_m, ln2_g, ln2_b, target)


def _mlp_bwd(dh, w1, w2, r, dr2, xhat1, rstd1, mix, ln1_g, ln1_b, sc_m, g_a):
    T = dh.shape[0]
    tf = w1.shape[-1]
    nf = N_DEV // MLP_SLABS
    tm = min(ROW_TILE, T)

    def body(dh_ref, w1_ref, w2_ref, r_ref, dr2_ref, xhat_ref, rstd_ref, mix_ref, g1_ref, b1_ref, sc_ref, ga_ref,
             dhpre_ref, dr1_ref, dmix_ref, small_ref, acc_ref):
        i, f = pl.program_id(0), pl.program_id(1)
        dm = D_MODEL

        @pl.when((i == 0) & (f == 0))
        def _():
            small_ref[...] = jnp.zeros_like(small_ref)

        @pl.when(f == 0)
        def _():
            acc_ref[...] = jnp.zeros_like(acc_ref)

        dht = dh_ref[...]
        part = None
        for s in range(MLP_SLABS):
            cols = slice(s * tf, (s + 1) * tf)
            dhpre = (_dot(dht, w2_ref[s], NT) * (2.0 * r_ref[:, cols].astype(F32))).astype(BF16)
            dhpre_ref[:, cols] = dhpre
            d = _dot(dhpre, w1_ref[s], NT)
            part = d if part is None else part + d
        acc_ref[...] += part

        @pl.when(f == nf - 1)
        def _():
            du2 = acc_ref[...]
            xhat = xhat_ref[...]
            x1 = xhat * g1_ref[...] + b1_ref[...]
            dx1 = ALPHA * dr2_ref[...] + du2 * (1.0 + sc_ref[...])
            small_ref[:, 2 * dm:3 * dm] += _rowsum(du2 * x1)
            small_ref[:, dm:2 * dm] += _rowsum(du2)
            small_ref[:, 3 * dm:4 * dm] += _rowsum(dx1 * xhat)
            small_ref[:, 4 * dm:5 * dm] += _rowsum(dx1)
            dr1 = _ln_bwd(dx1 * g1_ref[...], xhat, rstd_ref[:, 0:1])
            dr1_ref[...] = dr1
            small_ref[:, 0:dm] += _rowsum(dr1 * mix_ref[...])
            dmix_ref[...] = ((1.0 + ga_ref[...]) * dr1).astype(BF16)

    row = pl.BlockSpec((tm, D_MODEL), lambda i, f: (i, 0))
    vec = _full((1, D_MODEL))
    return pl.pallas_call(
        body, name="mlp_bwd", grid=(T // tm, nf),
        in_specs=[row, pl.BlockSpec((MLP_SLABS, D_MODEL, tf), lambda i, f: (f, 0, 0)),
                  pl.BlockSpec((MLP_SLABS, tf, D_MODEL), lambda i, f: (f, 0, 0)),
                  pl.BlockSpec((tm, MLP_SLABS * tf), lambda i, f: (i, f)), row, row,
                  pl.BlockSpec((tm, 128), lambda i, f: (i, 0)), row, vec, vec, vec, vec],
        out_specs=[pl.BlockSpec((tm, MLP_SLABS * tf), lambda i, f: (i, f)), row, row, _full((1, 5 * D_MODEL))],
        out_shape=[jax.ShapeDtypeStruct((T, N_DEV * tf), BF16), jax.ShapeDtypeStruct((T, D_MODEL), F32),
                   jax.ShapeDtypeStruct((T, D_MODEL), BF16), jax.ShapeDtypeStruct((1, 5 * D_MODEL), F32)],
        scratch_shapes=[pltpu.VMEM((tm, D_MODEL), F32)],
        compiler_params=_params(),
    )(dh, w1, w2, r, dr2, xhat1, rstd1, mix, ln1_g, ln1_b, sc_m, g_a)


def _input_bwd(dz_h, dz_m, w_in_ext, x, dr1, sc_a, exchange=None):
    T = x.shape[0]
    tm = min(ROW_TILE, T)

    def body(dzh_ref, dzm_ref, w_ref, x_ref, dr1_ref, sc_ref, gx_ref, small_ref):
        @pl.when(pl.program_id(0) == 0)
        def _():
            small_ref[...] = jnp.zeros_like(small_ref)

        du = _bdot(dzh_ref[...], w_ref[0:2048, :]) + _bdot(dzm_ref[...], w_ref[2048:3072, :])
        gx_ref[...] = ALPHA * dr1_ref[...] + du * (1.0 + sc_ref[...])
        small_ref[:, D_MODEL:] += _rowsum(du * x_ref[...])
        small_ref[:, 0:D_MODEL] += _rowsum(du)

    row = pl.BlockSpec((tm, D_MODEL), lambda i: (i, 0))
    vec = _full((1, D_MODEL))
    return _call(
        body, "input_bwd", (dz_h, dz_m, w_in_ext, x, dr1, sc_a), grid=(T // tm,),
        in_specs=[pl.BlockSpec((tm, 2048), lambda i: (i, 0)), row, _full(w_in_ext.shape), row, row, vec],
        out_specs=[row, _full((1, 2 * D_MODEL))],
        out_shape=[jax.ShapeDtypeStruct((T, D_MODEL), F32), jax.ShapeDtypeStruct((1, 2 * D_MODEL), F32)],
        exchange=exchange)


def _adam_math(w, g, m, v):
    m = ADAM_B1 * m + (1.0 - ADAM_B1) * g
    v = ADAM_B2 * v + (1.0 - ADAM_B2) * (g * g)
    m_hat = m / (1.0 - ADAM_B1 ** ADAM_STEP)
    v_hat = v / (1.0 - ADAM_B2 ** ADAM_STEP)
    return -ADAM_LR * (m_hat / (jnp.sqrt(v_hat) + ADAM_EPS) + ADAM_WD * w), m, v


def _adam(g_slabs, w, m, v, name, g_fn=None, g_extra=()):
    R, C = w.shape
    tr = 256 if R % 256 == 0 else R
    ns = 0 if g_slabs is None else g_slabs.shape[0]
    slab_rows = tr if g_slabs is None or g_slabs.shape[1] == R else g_slabs.shape[1]
    assert slab_rows == tr or tr == R
    ne = len(g_extra)

    def body(*refs):
        e_refs = refs[:ne]
        refs = refs[ne:]
        if ns:
            gs_ref, refs = refs[0], refs[1:]
        w_ref, m_ref, v_ref, g_ref, d_ref, nm_ref, nv_ref = refs
        if g_fn is not None:
            g = g_fn(*e_refs)
        else:
            g = gs_ref[0].astype(F32)
            for s in range(1, ns):
                g = g + gs_ref[s].astype(F32)
            g = g[:tr]
        d, nm, nv = _adam_math(w_ref[...], g, m_ref[...], v_ref[...])
        g_ref[...] = g
        d_ref[...] = d
        nm_ref[...] = nm
        nv_ref[...] = nv

    blk = pl.BlockSpec((tr, C), lambda i: (i, 0))
    in_specs = [pl.BlockSpec((tr, e.shape[1]), lambda i: (i, 0)) if e.shape[0] == R else _full(e.shape) for e in g_extra]
    args = list(g_extra)
    if ns:
        in_specs.append(pl.BlockSpec((ns, slab_rows, C), lambda i: (0, i, 0)))
        args.append(g_slabs)
    return pl.pallas_call(
        body, name=name, grid=(R // tr,), in_specs=in_specs + [blk] * 3, out_specs=[blk] * 4,
        out_shape=[jax.ShapeDtypeStruct((R, C), F32)] * 4, compiler_params=_params(),
    )(*args, w, m, v)


def _adam_small(small_all, params):
    n = len(params)

    def body(*refs):
        s_ref, refs = refs[0], refs[1:]
        wmv, loss_ref, outs = refs[:3 * n], refs[3 * n], refs[3 * n + 1:]
        tot = s_ref[0]
        for i in range(1, N_DEV):
            tot = tot + s_ref[i]
        loss_ref[...] = tot[:, SMALL_W - 128:]
        for j, (w, _, _, off) in enumerate(params):
            w_ref, m_ref, v_ref = wmv[3 * j:3 * j + 3]
            g_ref, d_ref, nm_ref, nv_ref = outs[4 * j:4 * j + 4]
            if w.shape[0] == 2:
                lb = _lower_bound(w_ref)
                g0 = tot[:, off:off + w.shape[1]] * lb * (1.0 - lb)
                rows = [(slice(0, 1), g0), (slice(1, 2), -g0)]
            else:
                rows = [(slice(0, 1), tot[:, off:off + w.shape[1]])]
            for rs, g in rows:
                d, nm, nv = _adam_math(w_ref[rs, :], g, m_ref[rs, :], v_ref[rs, :])
                g_ref[rs, :], d_ref[rs, :], nm_ref[rs, :], nv_ref[rs, :] = g, d, nm, nv

    out_shape = [jax.ShapeDtypeStruct((1, 128), F32)]
    for w, _, _, _ in params:
        out_shape += [jax.ShapeDtypeStruct(w.shape, F32)] * 4
    res = pl.pallas_call(body, name="adam_small", out_shape=out_shape, compiler_params=_params())(
        small_all, *[a for w, m, v, _ in params for a in (w, m, v)])
    return res[0], [tuple(res[1 + 4 * j:5 + 4 * j]) for j in range(n)]


def _cols_from_slabs(g):
    s, r, c = g.shape
    return jnp.transpose(g, (1, 0, 2)).reshape(r, s * c)


def _slabs_from_cols(w):
    r, c = w.shape
    return jnp.transpose(w.reshape(r, N_DEV, c // N_DEV), (1, 0, 2))


def _rot_half_rows(wt):
    return jnp.concatenate([-wt[32:], wt[:32]], axis=0)


def _unrot_half_rows(dwt_rot):
    return jnp.concatenate([dwt_rot[32:], -dwt_rot[:32]], axis=0)


def _ext_in_t(g):
    k_in = g.shape[2]
    z64, z128 = jnp.zeros((64, k_in), BF16), jnp.zeros((128, k_in), BF16)
    wt = g.reshape(N_DEV * g.shape[1], k_in)
    main, wk = wt[:wt.shape[0] - ROPE_DIM], wt[wt.shape[0] - ROPE_DIM:]
    return jnp.concatenate([main, z128, wk, z64, z128, _rot_half_rows(wk), z64], axis=0)


def _ext_q_t(wt):
    r = wt.shape[1]
    z64, z128 = jnp.zeros((64, r), BF16), jnp.zeros((128, r), BF16)
    per = HEAD_DIM + ROPE_DIM
    main = [jnp.concatenate([wt[per * h:per * (h + 1)], z64], axis=0) for h in range(HEADS)]
    rot = [jnp.concatenate([z128, _rot_half_rows(wt[per * h + HEAD_DIM:per * (h + 1)]), z64], axis=0)
           for h in range(HEADS)]
    return jnp.concatenate(main + rot, axis=0)


def _ext_kv(w_kv_up):
    r = w_kv_up.shape[0]
    z128 = jnp.zeros((r, 128), BF16)
    wkv = w_kv_up.reshape(r, HEADS, 2 * HEAD_DIM)
    kpad = [jnp.concatenate([wkv[:, h, :HEAD_DIM], z128], axis=1) for h in range(HEADS)]
    vals = [wkv[:, h, HEAD_DIM:] for h in range(HEADS)]
    return jnp.concatenate(kpad + vals, axis=1)


def _grad_in_from_ext_t(dwt_h, dwt_m):
    dwk = dwt_m[512 + 128:512 + 192] + _unrot_half_rows(dwt_m[768 + 128:768 + 192])
    return jnp.concatenate([dwt_h, dwt_m[:512], dwk], axis=0)


def _grad_q_from_ext_t(dwq_ext_t):
    rows = []
    for h in range(HEADS):
        main, rot = dwq_ext_t[256 * h:256 * h + 256], dwq_ext_t[1024 + 256 * h:1280 + 256 * h]
        rows += [main[:128], main[128:192] + _unrot_half_rows(rot[128:192])]
    return jnp.concatenate(rows, axis=0)


def _grad_kv_from_ext(dwkv_ext):
    kvcols = []
    for h in range(HEADS):
        kvcols += [dwkv_ext[:, 256 * h:256 * h + 128], dwkv_ext[:, 1024 + 128 * h:1152 + 128 * h]]
    return jnp.concatenate(kvcols, axis=1)


SMALL_W = 6144 + 512 + 512 + 256 + 256 + 4 * 1024 + 128


def kernel(x, c, positions, w_ada, b_ada, w_in, hg_lower_bounds, hg_norm_w, mla_q_norm_w, w_q_up, mla_kv_norm_w, w_kv_up, w_out, ln1_g, ln1_b, w_mlp_in, w_mlp_out, ln2_g, ln2_b, loss_target, m_w_ada, m_b_ada, m_w_in, m_hg_lower_bounds, m_hg_norm_w, m_mla_q_norm_w, m_w_q_up, m_mla_kv_norm_w, m_w_kv_up, m_w_out, m_ln1_g, m_ln1_b, m_w_mlp_in, m_w_mlp_out, m_ln2_g, m_ln2_b, v_w_ada, v_b_ada, v_w_in, v_hg_lower_bounds, v_hg_norm_w, v_mla_q_norm_w, v_w_q_up, v_mla_kv_norm_w, v_w_kv_up, v_w_out, v_ln1_g, v_ln1_b, v_w_mlp_in, v_w_mlp_out, v_ln2_g, v_ln2_b):
    T = x.shape[1]
    me = 4 * lax.axis_index("x") + 2 * lax.axis_index("y") + lax.axis_index("c")
    xs, tgt = x[0], loss_target[0]
    transposed = ("w_in", "w_q_up")
    as_used = lambda n, a: a[0].T if n in transposed else a[0]
    big = {n: as_used(n, a) for n, a in dict(w_in=w_in, w_q_up=w_q_up, w_kv_up=w_kv_up, w_out=w_out,
                                              w_mlp_in=w_mlp_in, w_mlp_out=w_mlp_out).items()}
    names = list(big)

    bf = {n: big[n].astype(BF16) for n in names}
    g_in, g_c = _gather_two_level([bf["w_in"], c], name="gather_w_in")
    c_all = g_c.reshape(N_DEV, D_MODEL)

    ada_cols = w_ada.shape[2]
    mod_part, cond = _mod_part(c_all, w_ada[0], lax.dynamic_slice(b_ada, (0, me * ada_cols), (1, ada_cols)))
    (mod_all,) = _exchange([mod_part], scatter=False, name="gather_mod")
    mod_row = lax.dynamic_slice(mod_all, (0, me, 0), (N_DEV, 1, ada_cols)).reshape(1, N_DEV * ada_cols)
    sh_a, sc_a, g_a, sh_m, sc_m, g_m = [mod_row[:, D_MODEL * i:D_MODEL * (i + 1)] for i in range(6)]

    w_in_ext = _ext_in_t(g_in)
    z, (g_q, g_kv, g_out) = _matmul(xs, w_in_ext, "NT", "in_proj", a_fn=_modulate, extras=(sc_a, sh_a), tn=3072,
                                    exchange=_Exchange([bf["w_q_up"], bf["w_kv_up"], bf["w_out"]], False))
    wq_ext = _ext_q_t(g_q.reshape(N_DEV * g_q.shape[1], g_q.shape[2]))
    wkv_ext = _ext_kv(_cols_from_slabs(g_kv))
    w_out_full = g_out.reshape(D_MODEL, D_MODEL)
    inv_freq = 1.0 / (ROPE_THETA ** (jnp.arange(0, ROPE_DIM, 2, dtype=F32) / ROPE_DIM))
    zeros = lambda n: jnp.zeros((n,), F32)
    invf = jnp.concatenate([zeros(128), inv_freq, inv_freq, zeros(64)]).reshape(1, QK_PAD)
    m_one = jnp.concatenate([jnp.ones((128,), F32), zeros(128)]).reshape(1, QK_PAD)
    m_rot = jnp.concatenate([zeros(128), jnp.ones((64,), F32), zeros(64)]).reshape(1, QK_PAD)
    q, k, v, c1, s1, cqn, ckvn = _mla_pre(z, positions.reshape(T, 1), invf, m_one, m_rot, wq_ext, wkv_ext,
                                          mla_q_norm_w, mla_kv_norm_w)[0]
    (o_raw, o_gated, s_prev), (w1,) = _hgrn_fwd(z, hg_lower_bounds, hg_norm_w,
                                                exchange=_StagedGather(bf["w_mlp_in"]))
    (o_mla, lse), (w2,) = _attn_fwd(q, k, v, exchange=_StagedGather(bf["w_mlp_out"]))
    mix, xhat1, rstd1, u2 = _mix_ln1(o_gated, o_mla, w_out_full, xs, g_a, ln1_g, ln1_b, sc_m, sh_m)[0]
    r, dr2, dh, small_mlp_fwd = _mlp_fwd(u2, w1, w2, xhat1, ln1_g, ln1_b, g_m, ln2_g, ln2_b, tgt)

    dhpre, dr1, dmix, small_mlp_bwd = _mlp_bwd(dh, w1, w2, r, dr2, xhat1, rstd1, mix, ln1_g, ln1_b, sc_m, g_a)
    received = {}
    dw2 = _matmul(r, dh, "TN", "wgrad_mlp_out", out_dtype=BF16, a_fn=_square, tm=1024, tk=2048)
    dw1 = _matmul(u2, dhpre, "TN", "wgrad_mlp_in", out_dtype=BF16, tm=1024, tk=2048, out_slabs=N_DEV)
    dmixcat = _matmul(dmix, w_out_full, "NT", "dgrad_out")
    dw_out = jnp.concatenate([_matmul(o_gated, dmix, "TN", "wgrad_out_hg", out_dtype=BF16, tk=2048),
                              _matmul(o_mla, dmix, "TN", "wgrad_out_mla", out_dtype=BF16, tk=2048)], axis=0)
    (dz_h, small_hgrn), (received["w_out"],) = _hgrn_bwd(
        dmixcat, z, o_raw, s_prev, hg_lower_bounds, hg_norm_w,
        exchange=_Exchange([dw_out.reshape(N_DEV, D_MODEL // N_DEV, D_MODEL)], True))
    (dq, dk, dv), (received["w_mlp_in"], received["w_mlp_out"]) = _attn_bwd(
        q, k, v, dmixcat, o_mla, lse,
        exchange=_Exchange([dw1, dw2.reshape(N_DEV, dw2.shape[0] // N_DEV, D_MODEL)], True))
    dz_m, dq_ext, dkv_ext, small_mla = _mla_bwd(dq, dk, dv, z, c1, s1, wq_ext, wkv_ext, mla_q_norm_w, mla_kv_norm_w)
    dwq_t = _grad_q_from_ext_t(_matmul(dq_ext, cqn, "TN", "wgrad_q_up", tm=1024, tk=2048))
    dwkv = _grad_kv_from_ext(_matmul(ckvn, dkv_ext, "TN", "wgrad_kv_up", tn=1536, tk=2048))
    qkv_slabs = [dwq_t.reshape((N_DEV, dwq_t.shape[0] // N_DEV, dwq_t.shape[1])).astype(BF16),
                 _slabs_from_cols(dwkv).astype(BF16)]
    dwt_h, (received["w_q_up"], received["w_kv_up"]) = _matmul(
        dz_h, xs, "TN", "wgrad_in_h", b_fn=_modulate, extras=(sc_a, sh_a), tm=1024, tk=2048,
        exchange=_Exchange(qkv_slabs, True))
    dwt_m = _matmul(dz_m, xs, "TN", "wgrad_in_m", b_fn=_modulate, extras=(sc_a, sh_a), tm=1024, tk=2048)
    dw_in_t = _grad_in_from_ext_t(dwt_h, dwt_m)
    in_slabs = dw_in_t.reshape((N_DEV, dw_in_t.shape[0] // N_DEV, dw_in_t.shape[1]))
    in_slabs = jnp.pad(in_slabs, ((0, 0), (0, -in_slabs.shape[1] % 16), (0, 0))).astype(BF16)
    (grad_x, small_in), (received["w_in"],) = _input_bwd(
        dz_h, dz_m, w_in_ext, xs, dr1, sc_a, exchange=_StagedScatter(in_slabs))

    small = jnp.concatenate([small_in, small_mlp_bwd[:, :3 * D_MODEL], small_mlp_fwd[:, :D_MODEL], small_hgrn,
                             small_mla, small_mlp_bwd[:, 3 * D_MODEL:], small_mlp_fwd[:, D_MODEL:]], axis=1)
    assert small.shape == (1, SMALL_W)
    (small_all,) = _exchange([small], scatter=False, name="gather_small")

    moments = dict(w_in=(m_w_in, v_w_in), w_q_up=(m_w_q_up, v_w_q_up), w_kv_up=(m_w_kv_up, v_w_kv_up),
                   w_out=(m_w_out, v_w_out), w_mlp_in=(m_w_mlp_in, v_w_mlp_in), w_mlp_out=(m_w_mlp_out, v_w_mlp_out))
    res = {}
    for n in names:
        res[n] = _adam(received[n], big[n], as_used(n, moments[n][0]), as_used(n, moments[n][1]), name="adam_" + n)
    dmod_cols = lax.dynamic_slice(small_all.reshape(N_DEV, SMALL_W), (0, me * ada_cols), (N_DEV, ada_cols))
    cond_t = cond.T

    def ada_grad(ct_ref, dm_ref):
        g = ct_ref[:, 0:1] * dm_ref[0:1, :]
        for b in range(1, N_DEV):
            g = g + ct_ref[:, b:b + 1] * dm_ref[b:b + 1, :]
        return g

    res["w_ada"] = _adam(None, w_ada[0], m_w_ada[0], v_w_ada[0], name="adam_w_ada", g_fn=ada_grad,
                         g_extra=(cond_t, dmod_cols))

    small_params = [("b_ada", b_ada, m_b_ada, v_b_ada, 0),
                    ("hg_lower_bounds", hg_lower_bounds, m_hg_lower_bounds, v_hg_lower_bounds, 6144),
                    ("hg_norm_w", hg_norm_w, m_hg_norm_w, v_hg_norm_w, 6656),
                    ("mla_q_norm_w", mla_q_norm_w, m_mla_q_norm_w, v_mla_q_norm_w, 7168),
                    ("mla_kv_norm_w", mla_kv_norm_w, m_mla_kv_norm_w, v_mla_kv_norm_w, 7424),
                    ("ln1_g", ln1_g, m_ln1_g, v_ln1_g, 7680), ("ln1_b", ln1_b, m_ln1_b, v_ln1_b, 8704),
                    ("ln2_g", ln2_g, m_ln2_g, v_ln2_g, 9728), ("ln2_b", ln2_b, m_ln2_b, v_ln2_b, 10752)]
    loss_row, small_res = _adam_small(small_all, [p[1:] for p in small_params])
    for p, r4 in zip(small_params, small_res):
        res[p[0]] = r4
    loss = loss_row[0, 0]

    order = ["w_ada", "b_ada", "w_in", "hg_lower_bounds", "hg_norm_w", "mla_q_norm_w", "w_q_up", "mla_kv_norm_w",
             "w_kv_up", "w_out", "ln1_g", "ln1_b", "w_mlp_in", "w_mlp_out", "ln2_g", "ln2_b"]
    def as_given(n, a):
        if n in transposed:
            a = a.T
        return a[None] if n in big or n == "w_ada" else a

    shaped = {n: tuple(as_given(n, a) for a in res[n]) for n in order}
    outs = [loss, grad_x.reshape(1, T, D_MODEL)]
    for i in range(4):
        outs += [shaped[n][i] for n in order]
    return tuple(outs)
```

```python
import functools

import jax
import jax.numpy as jnp
import numpy as np
from jax import lax
from jax.experimental import pallas as pl
from jax.experimental.pallas import tpu as pltpu

F32, BF16 = jnp.float32, jnp.bfloat16
N_DEV = 8
D_MODEL = 1024
HEADS = 4
HEAD_DIM = 128
ROPE_DIM = 64
QK_PAD = 256
CHUNK = 64
ROPE_THETA = 10000.0
RMS_EPS = 1e-6
LN_EPS = 1e-5
ALPHA = 2.0 ** 0.25
ATT_SCALE = (HEAD_DIM + ROPE_DIM) ** -0.5
LN2 = float(np.log(2.0))
Q_PRESCALE = ATT_SCALE / LN2
ADAM_LR, ADAM_B1, ADAM_B2, ADAM_EPS, ADAM_WD, ADAM_STEP = 0.001, 0.9, 0.999, 1e-08, 0.01, 10
NEG_BIG = -1e30

ROW_TILE = 512
ROW_TILE_SMALL = 256
ATT_TILE = 512
HGRN_GROUP = 8
MLP_SLABS = 4
VMEM_LIMIT = 56 * 2 ** 20

NN = (((1,), (0,)), ((), ()))
NT = (((1,), (1,)), ((), ()))
TN = (((0,), (0,)), ((), ()))


def _dot(a, b, dims=NN):
    return lax.dot_general(a, b, dims, preferred_element_type=F32)


def _bdot(a, b, dims=NN):
    return lax.dot_general(a.astype(BF16), b.astype(BF16), dims, preferred_element_type=F32)


def _hdot(a, b, dims=NN):
    return lax.dot_general(a, b, dims, precision=lax.Precision.HIGHEST, preferred_element_type=F32)


def _params():
    return pltpu.CompilerParams(vmem_limit_bytes=VMEM_LIMIT)


def _sigmoid(x):
    return 1.0 / (1.0 + jnp.exp(-x))


def _rowsum(x):
    return jnp.sum(x, axis=0, keepdims=True)


def _lanemean(x):
    return jnp.mean(x, axis=-1, keepdims=True)


def _full(shape):
    nd = len(shape)
    return pl.BlockSpec(shape, lambda *_: (0,) * nd)


class _Exchange:
    def __init__(self, arrs, scatter):
        self.arrs, self.scatter, self.n, self.aliases, self.middle_at = list(arrs), scatter, len(arrs), [], 0.5
        self.out_shape = [jax.ShapeDtypeStruct((N_DEV,) + (a.shape[1:] if scatter else a.shape), a.dtype)
                          for a in self.arrs]
        n = self.n
        self.scratch = [pltpu.SemaphoreType.DMA((n, N_DEV - 1)), pltpu.SemaphoreType.DMA((n, N_DEV - 1)),
                        pltpu.SemaphoreType.DMA((n,))]

    def _copies(self, ins, outs, sems):
        send_sems, recv_sems, loc_sems = sems
        x, y, c = lax.axis_index("x"), lax.axis_index("y"), lax.axis_index("c")
        me = 4 * x + 2 * y + c
        copies = []
        for k in range(self.n):
            src_of = (lambda i, k=k: ins[k].at[i]) if self.scatter else (lambda i, k=k: ins[k])
            copies.append((pltpu.make_async_copy(src_of(me), outs[k].at[me], loc_sems.at[k]), None))
            for p in range(1, N_DEV):
                px = (1 - x) if p & 4 else x
                py = (1 - y) if p & 2 else y
                pc = (1 - c) if p & 1 else c
                peer = 4 * px + 2 * py + pc
                both = dict(send_sem=send_sems.at[k, p - 1], recv_sem=recv_sems.at[k, p - 1],
                            device_id=(px, py, pc), device_id_type=pl.DeviceIdType.MESH)
                send = pltpu.make_async_remote_copy(src_ref=src_of(peer), dst_ref=outs[k].at[me], **both)
                recv = pltpu.make_async_remote_copy(src_ref=src_of(peer), dst_ref=outs[k].at[peer], **both)
                copies.append((send, recv))
        return copies

    def start(self, ins, outs, sems):
        for first, _ in self._copies(ins, outs, sems):
            first.start()

    def middle(self, ins, outs, sems):
        pass

    def wait(self, ins, outs, sems):
        for first, recv in self._copies(ins, outs, sems):
            if recv is None:
                first.wait()
            else:
                recv.wait_recv()
                first.wait_send()


class _StagedGather:
    def __init__(self, arr):
        self.arrs, self.aliases, self.middle_at = [arr], [], 0.8
        self.out_shape = [jax.ShapeDtypeStruct((N_DEV,) + arr.shape, arr.dtype)]
        self.scratch = [pltpu.VMEM((N_DEV,) + arr.shape, arr.dtype), pltpu.SemaphoreType.DMA((7,)),
                        pltpu.SemaphoreType.DMA((7,)), pltpu.SemaphoreType.DMA((2,))]

    def _parts(self, scr):
        stage, send_sems, recv_sems, loc_sems = scr
        x, y, c = lax.axis_index("x"), lax.axis_index("y"), lax.axis_index("c")
        me, sibling = (x, y, c), (x, y, 1 - c)
        chips = [(1 - x, y), (x, 1 - y), (1 - x, 1 - y)]

        def copy(j, block, to):
            px, py, pc = block
            slot = stage.at[4 * px + 2 * py + pc]
            return pltpu.make_async_remote_copy(src_ref=slot, dst_ref=slot, send_sem=send_sems.at[j],
                                                recv_sem=recv_sems.at[j], device_id=to,
                                                device_id_type=pl.DeviceIdType.MESH)

        return stage, loc_sems, me, sibling, chips, c, copy

    def start(self, ins, outs, scr):
        stage, loc_sems, me, sibling, chips, c, copy = self._parts(scr)
        x, y, _ = me
        own = pltpu.make_async_copy(ins[0], stage.at[4 * x + 2 * y + c], loc_sems.at[0])
        own.start()
        own.wait()
        copy(0, me, sibling).start()
        for j, chip in enumerate(chips):
            copy(1 + j, me, (*chip, c)).start()

    def middle(self, ins, outs, scr):
        stage, loc_sems, me, sibling, chips, c, copy = self._parts(scr)
        for j, chip in enumerate(chips):
            copy(1 + j, (*chip, c), me).wait_recv()
            copy(4 + j, (*chip, c), sibling).start()

    def wait(self, ins, outs, scr):
        stage, loc_sems, me, sibling, chips, c, copy = self._parts(scr)
        copy(0, sibling, me).wait_recv()
        for j, chip in enumerate(chips):
            copy(4 + j, (*chip, 1 - c), me).wait_recv()
        copy(0, me, sibling).wait_send()
        for j, chip in enumerate(chips):
            copy(1 + j, me, (*chip, c)).wait_send()
            copy(4 + j, (*chip, c), sibling).wait_send()
        whole = pltpu.make_async_copy(stage, outs[0], loc_sems.at[1])
        whole.start()
        whole.wait()


class _StagedScatter:
    def __init__(self, slabs, middle_at=0.2):
        _, r, c = slabs.shape
        self.arrs, self.aliases, self.middle_at = [slabs], [], middle_at
        self.out_shape = [jax.ShapeDtypeStruct((4, r, c), slabs.dtype)]
        self.scratch = [pltpu.VMEM((N_DEV, r, c), slabs.dtype), pltpu.VMEM((4, r, c), slabs.dtype),
                        pltpu.VMEM((3, r, c), slabs.dtype), pltpu.SemaphoreType.DMA((4,)), pltpu.SemaphoreType.DMA((4,)),
                        pltpu.SemaphoreType.DMA((3,)), pltpu.SemaphoreType.DMA((3,)), pltpu.SemaphoreType.DMA((4,))]

    def _parts(self, scr):
        stage, from_sib, from_chips, sib_send, sib_recv, ici_send, ici_recv, loc_sems = scr
        x, y, c = lax.axis_index("x"), lax.axis_index("y"), lax.axis_index("c")
        chips = [(1 - x, y), (x, 1 - y), (1 - x, 1 - y)]

        def to_sibling(j):
            return pltpu.make_async_remote_copy(src_ref=stage.at[2 * j + 1 - c], dst_ref=from_sib.at[j],
                                                send_sem=sib_send.at[j], recv_sem=sib_recv.at[j],
                                                device_id=(x, y, 1 - c), device_id_type=pl.DeviceIdType.MESH)

        def to_chip(k):
            px, py = chips[k]
            return pltpu.make_async_remote_copy(src_ref=stage.at[4 * px + 2 * py + c], dst_ref=from_chips.at[k],
                                                send_sem=ici_send.at[k], recv_sem=ici_recv.at[k],
                                                device_id=(px, py, c), device_id_type=pl.DeviceIdType.MESH)

        return stage, from_sib, from_chips, loc_sems, (x, y, c), chips, to_sibling, to_chip

    def start(self, ins, outs, scr):
        stage, _, _, loc_sems, _, _, to_sibling, _ = self._parts(scr)
        load = pltpu.make_async_copy(ins[0], stage, loc_sems.at[0])
        load.start()
        load.wait()
        for j in range(4):
            to_sibling(j).start()

    def middle(self, ins, outs, scr):
        stage, from_sib, _, _, (x, y, c), _, to_sibling, to_chip = self._parts(scr)
        for j in range(4):
            to_sibling(j).wait_recv()
            mine = stage.at[2 * j + c]
            mine[...] = (mine[...].astype(F32) + from_sib[j].astype(F32)).astype(mine.dtype)
        for k in range(3):
            to_chip(k).start()

    def wait(self, ins, outs, scr):
        stage, _, from_chips, loc_sems, (x, y, c), chips, to_sibling, to_chip = self._parts(scr)
        writes = [pltpu.make_async_copy(stage.at[4 * x + 2 * y + c], outs[0].at[2 * x + y], loc_sems.at[0])]
        for k, (px, py) in enumerate(chips):
            to_chip(k).wait_recv()
            writes.append(pltpu.make_async_copy(from_chips.at[k], outs[0].at[2 * px + py], loc_sems.at[1 + k]))
        for w in writes:
            w.start()
        for j in range(4):
            to_sibling(j).wait_send()
        for k in range(3):
            to_chip(k).wait_send()
        for w in writes:
            w.wait()


def _call(body, name, args, out_shape, grid=(), in_specs=(), out_specs=(), scratch_shapes=(), exchange=None):
    if exchange is None:
        return pl.pallas_call(body, name=name, grid=grid, in_specs=list(in_specs), out_specs=list(out_specs),
                              out_shape=list(out_shape), scratch_shapes=list(scratch_shapes),
                              compiler_params=_params())(*args), None
    exs = list(exchange) if isinstance(exchange, (list, tuple)) else [exchange]
    ni, no, ns = len(args), len(out_shape), len(scratch_shapes)
    nxi, nxo = sum(len(e.arrs) for e in exs), sum(len(e.out_shape) for e in exs)
    steps = int(np.prod(grid))
    mid_step = lambda e: min(max(int(steps * e.middle_at), 1), steps - 1)
    aliases, iat, oat = {}, ni, no
    for e in exs:
        for src, dst in e.aliases:
            aliases[iat + src] = oat + dst
        iat, oat = iat + len(e.arrs), oat + len(e.out_shape)

    def wrapped(*refs):
        a, xi = refs[:ni], refs[ni:ni + nxi]
        o, xo = refs[ni + nxi:ni + nxi + no], refs[ni + nxi + no:ni + nxi + no + nxo]
        s, xs = refs[ni + nxi + no + nxo:ni + nxi + no + nxo + ns], refs[ni + nxi + no + nxo + ns:]
        parts, iat, oat, sat = [], 0, 0, 0
        for e in exs:
            parts.append((e, xi[iat:iat + len(e.arrs)], xo[oat:oat + len(e.out_shape)], xs[sat:sat + len(e.scratch)]))
            iat, oat, sat = iat + len(e.arrs), oat + len(e.out_shape), sat + len(e.scratch)
        step = 0
        for d, g in enumerate(grid):
            step = step * g + pl.program_id(d)

        @pl.when(step == 0)
        def _():
            for e, ins, outs, sems in parts:
                e.start(ins, outs, sems)

        for at_step in sorted({mid_step(e) for e in exs}):
            @pl.when(step == at_step)
            def _():
                for e, ins, outs, sems in parts:
                    if mid_step(e) == at_step:
                        e.middle(ins, outs, sems)

        body(*a, *o, *s)

        @pl.when(step == steps - 1)
        def _():
            for e, ins, outs, sems in parts:
                e.wait(ins, outs, sems)

    hbm = pl.BlockSpec(memory_space=pltpu.HBM)
    res = pl.pallas_call(
        wrapped, name=name, grid=grid, in_specs=list(in_specs) + [hbm] * nxi, out_specs=list(out_specs) + [hbm] * nxo,
        out_shape=list(out_shape) + [o_ for e in exs for o_ in e.out_shape],
        scratch_shapes=list(scratch_shapes) + [s_ for e in exs for s_ in e.scratch],
        input_output_aliases=aliases, compiler_params=_params())(*args, *[a_ for e in exs for a_ in e.arrs])
    return res[:no], res[no:]


def _gather_two_level(arrs, name):
    n = len(arrs)
    out_shape = [jax.ShapeDtypeStruct((N_DEV,) + a.shape, a.dtype) for a in arrs]

    def body(*refs):
        ins, outs = refs[:n], refs[n:2 * n]
        send_sems, recv_sems, loc_sems = refs[2 * n:]
        x, y, c = lax.axis_index("x"), lax.axis_index("y"), lax.axis_index("c")
        me, sibling = (x, y, c), (x, y, 1 - c)
        chips = [(1 - x, y), (x, 1 - y), (1 - x, 1 - y)]

        def copy(k, j, block, to, src=None):
            px, py, pc = block
            dst = outs[k].at[4 * px + 2 * py + pc]
            return pltpu.make_async_remote_copy(src_ref=dst if src is None else src, dst_ref=dst,
                                                send_sem=send_sems.at[k, j], recv_sem=recv_sems.at[k, j],
                                                device_id=to, device_id_type=pl.DeviceIdType.MESH)

        mine = [pltpu.make_async_copy(ins[k], outs[k].at[4 * x + 2 * y + c], loc_sems.at[k]) for k in range(n)]
        first = []
        for k in range(n):
            mine[k].start()
            first.append(copy(k, 0, me, sibling, src=ins[k]))
            first += [copy(k, 1 + j, me, (*chip, c), src=ins[k]) for j, chip in enumerate(chips)]
        for cp in first:
            cp.start()
        passed = []
        for j, chip in enumerate(chips):
            for k in range(n):
                copy(k, 1 + j, (*chip, c), me).wait_recv()
                passed.append(copy(k, 4 + j, (*chip, c), sibling))
                passed[-1].start()
        for k in range(n):
            copy(k, 0, sibling, me).wait_recv()
            for j, chip in enumerate(chips):
                copy(k, 4 + j, (*chip, 1 - c), me).wait_recv()
        for cp in first + passed:
            cp.wait_send()
        for cp in mine:
            cp.wait()

    vmem = pl.BlockSpec(memory_space=pltpu.VMEM)
    return pl.pallas_call(body, name=name, out_shape=out_shape, in_specs=[vmem] * n, out_specs=[vmem] * n,
                          scratch_shapes=[pltpu.SemaphoreType.DMA((n, 7)), pltpu.SemaphoreType.DMA((n, 7)),
                                          pltpu.SemaphoreType.DMA((n,))], compiler_params=_params())(*arrs)


def _exchange(arrs, scatter, name):
    ex = _Exchange(arrs, scatter)

    def body(*refs):
        ins, outs, sems = refs[:ex.n], refs[ex.n:2 * ex.n], refs[2 * ex.n:]
        ex.start(ins, outs, sems)
        ex.wait(ins, outs, sems)

    hbm = pl.BlockSpec(memory_space=pltpu.HBM)
    return pl.pallas_call(body, name=name, out_shape=ex.out_shape, in_specs=[hbm] * ex.n, out_specs=[hbm] * ex.n,
                          scratch_shapes=ex.scratch)(*ex.arrs)


def _matmul(a, b, mode, name, out_dtype=F32, tm=512, tn=1024, tk=1024, a_fn=None, b_fn=None, extras=(),
            out_slabs=None, exchange=None):
    assert not (a_fn and b_fn) and not (b_fn and mode == "NT")
    if mode == "NN":
        (M, K), N = a.shape, b.shape[1]
    elif mode == "NT":
        (M, K), N = a.shape, b.shape[0]
    else:
        (K, M), N = a.shape, b.shape[1]
    if out_slabs:
        tn = N // out_slabs
    tm, tn, tk = min(tm, M), min(tn, N), min(tk, K)
    assert M % tm == 0 and N % tn == 0 and K % tk == 0, (name, M, N, K)
    nk = K // tk
    dims = {"NN": NN, "NT": NT, "TN": TN}[mode]
    ne = len(extras)

    def body(a_ref, b_ref, *rest):
        e_refs, o_ref, acc_ref = rest[:ne], rest[ne], rest[ne + 1]
        k = pl.program_id(2)

        @pl.when(k == 0)
        def _():
            acc_ref[...] = jnp.zeros_like(acc_ref)

        at, bt = a_ref[...], b_ref[...]
        if a_fn is not None:
            at = a_fn(at.astype(F32), *[e[...] for e in e_refs])
        if b_fn is not None:
            bt = b_fn(bt.astype(F32), *[e[...] for e in e_refs])
        acc_ref[...] += _bdot(at, bt, dims)

        @pl.when(k == nk - 1)
        def _():
            o_ref[...] = acc_ref[...].astype(out_dtype)

    if mode == "TN":
        a_spec = pl.BlockSpec((tk, tm), lambda i, j, k: (k, i))
        e_spec = pl.BlockSpec((1, tm), lambda i, j, k: (0, i))
    else:
        a_spec = pl.BlockSpec((tm, tk), lambda i, j, k: (i, k))
        e_spec = pl.BlockSpec((1, tk), lambda i, j, k: (0, k))
    if mode == "NT":
        b_spec = pl.BlockSpec((tn, tk), lambda i, j, k: (j, k))
    else:
        b_spec = pl.BlockSpec((tk, tn), lambda i, j, k: (k, j))
    if b_fn is not None:
        e_spec = pl.BlockSpec((1, tn), lambda i, j, k: (0, j))
    if out_slabs:
        o_shape = jax.ShapeDtypeStruct((out_slabs, M, tn), out_dtype)
        o_spec = pl.BlockSpec((None, tm, tn), lambda i, j, k: (j, i, 0))
    else:
        o_shape = jax.ShapeDtypeStruct((M, N), out_dtype)
        o_spec = pl.BlockSpec((tm, tn), lambda i, j, k: (i, j))
    (out,), got = _call(body, name, (a, b, *extras), [o_shape], grid=(M // tm, N // tn, nk),
                        in_specs=[a_spec, b_spec] + [e_spec] * ne, out_specs=[o_spec],
                        scratch_shapes=[pltpu.VMEM((tm, tn), F32)], exchange=exchange)
    return out if exchange is None else (out, got)


def _modulate(x, sc, sh):
    return x * (1.0 + sc) + sh


def _square(x):
    return x * x


def _mod_part(c_all, w_ada_s, b_s):
    def body(c_ref, w_ref, b_ref, mod_ref, cond_ref):
        cv = c_ref[...]
        cond = cv * _sigmoid(cv)
        cond_ref[...] = cond
        mod_ref[...] = _bdot(cond, w_ref[...]) + b_ref[...]

    return pl.pallas_call(
        body, name="mod_part",
        out_shape=[jax.ShapeDtypeStruct((N_DEV, w_ada_s.shape[1]), F32), jax.ShapeDtypeStruct(c_all.shape, F32)],
        compiler_params=_params(),
    )(c_all, w_ada_s, b_s)


def _rms_fwd(x, w):
    rs = lax.rsqrt(_lanemean(x * x) + RMS_EPS)
    return x * rs * w, rs


def _rms_bwd(x, rs, w, dy):
    xhat = x * rs
    dxh = dy * w
    return rs * (dxh - xhat * _lanemean(dxh * xhat)), dy * xhat


def _mla_pre(z, pos_col, invf, m_one, m_rot, wq_ext, wkv_ext, qnw, kvnw, exchange=None):
    T = z.shape[0]
    tm = min(ROW_TILE, T)

    def body(z_ref, pos_ref, invf_ref, mone_ref, mrot_ref, wq_ref, wkv_ref, qnw_ref, kvnw_ref,
             q_ref, k_ref, v_ref, c1_ref, s1_ref, cqn_ref, ckvn_ref):
        ang = pos_ref[...].astype(F32) * invf_ref[...]
        c1 = mone_ref[...] + mrot_ref[...] * jnp.cos(ang)
        s1 = mrot_ref[...] * jnp.sin(ang)
        c1_ref[...] = c1
        s1_ref[...] = s1
        cqn, _ = _rms_fwd(z_ref[:, 0:256], qnw_ref[...])
        ckvn, _ = _rms_fwd(z_ref[:, 256:512], kvnw_ref[...])
        cqn_ref[...] = cqn.astype(BF16)
        ckvn_ref[...] = ckvn.astype(BF16)
        qe = _bdot(cqn, wq_ref[...], NT)
        kve = _bdot(ckvn, wkv_ref[...])
        k_rope = z_ref[:, 512:768] * c1 + z_ref[:, 768:1024] * s1
        for h in range(HEADS):
            q_ref[h] = ((qe[:, 256 * h:256 * h + 256] * c1 + qe[:, 1024 + 256 * h:1280 + 256 * h] * s1)
                        * Q_PRESCALE).astype(BF16)
            k_ref[h] = (kve[:, 256 * h:256 * h + 256] + k_rope).astype(BF16)
            v_ref[h] = kve[:, 1024 + 128 * h:1152 + 128 * h].astype(BF16)

    row = lambda i: (i, 0)
    head = lambda i: (0, i, 0)
    return _call(
        body, "mla_pre", (z, pos_col, invf, m_one, m_rot, wq_ext, wkv_ext, qnw, kvnw), grid=(T // tm,),
        in_specs=[pl.BlockSpec((tm, 1024), lambda i: (i, 2)), pl.BlockSpec((tm, 1), row),
                  _full((1, 256)), _full((1, 256)), _full((1, 256)), _full(wq_ext.shape), _full(wkv_ext.shape),
                  _full((1, 256)), _full((1, 256))],
        out_specs=[pl.BlockSpec((HEADS, tm, QK_PAD), head), pl.BlockSpec((HEADS, tm, QK_PAD), head),
                   pl.BlockSpec((HEADS, tm, HEAD_DIM), head), pl.BlockSpec((tm, 256), row), pl.BlockSpec((tm, 256), row),
                   pl.BlockSpec((tm, 256), row), pl.BlockSpec((tm, 256), row)],
        out_shape=[jax.ShapeDtypeStruct((HEADS, T, QK_PAD), BF16), jax.ShapeDtypeStruct((HEADS, T, QK_PAD), BF16),
                   jax.ShapeDtypeStruct((HEADS, T, HEAD_DIM), BF16), jax.ShapeDtypeStruct((T, 256), F32),
                   jax.ShapeDtypeStruct((T, 256), F32), jax.ShapeDtypeStruct((T, 256), BF16),
                   jax.ShapeDtypeStruct((T, 256), BF16)], exchange=exchange)


def _mla_bwd(dq, dk, dv, z, c1, s1, wq_ext, wkv_ext, qnw, kvnw):
    T = z.shape[0]
    tm = min(ROW_TILE_SMALL, T)

    def body(dq_ref, dk_ref, dv_ref, z_ref, c1_ref, s1_ref, wq_ref, wkv_ref, qnw_ref, kvnw_ref,
             dz_ref, dqe_ref, dkve_ref, dnw_ref):
        @pl.when(pl.program_id(0) == 0)
        def _():
            dnw_ref[...] = jnp.zeros_like(dnw_ref)

        c1, s1 = c1_ref[...], s1_ref[...]
        dkpe = jnp.zeros((tm, QK_PAD), F32)
        for h in range(HEADS):
            dqh, dkh = dq_ref[h] * Q_PRESCALE, dk_ref[h]
            dqe_ref[:, 256 * h:256 * h + 256] = (dqh * c1).astype(BF16)
            dqe_ref[:, 1024 + 256 * h:1280 + 256 * h] = (dqh * s1).astype(BF16)
            dkve_ref[:, 256 * h:256 * h + 256] = dkh.astype(BF16)
            dkve_ref[:, 1024 + 128 * h:1152 + 128 * h] = dv_ref[h].astype(BF16)
            dkpe = dkpe + dkh
        dcqn = _dot(dqe_ref[...], wq_ref[...])
        dckvn = _dot(dkve_ref[...], wkv_ref[...], NT)
        cq, ckv = z_ref[:, 0:256], z_ref[:, 256:512]
        _, rsq = _rms_fwd(cq, qnw_ref[...])
        _, rskv = _rms_fwd(ckv, kvnw_ref[...])
        dcq, wq_rows = _rms_bwd(cq, rsq, qnw_ref[...], dcqn)
        dckv, wkv_rows = _rms_bwd(ckv, rskv, kvnw_ref[...], dckvn)
        dnw_ref[:, 0:256] += _rowsum(wq_rows)
        dnw_ref[:, 256:512] += _rowsum(wkv_rows)
        dz_ref[:, 0:256] = dcq
        dz_ref[:, 256:512] = dckv
        dz_ref[:, 512:768] = dkpe * c1
        dz_ref[:, 768:1024] = dkpe * s1

    row = lambda i: (i, 0)
    head = lambda i: (0, i, 0)
    return pl.pallas_call(
        body, name="mla_bwd", grid=(T // tm,),
        in_specs=[pl.BlockSpec((HEADS, tm, QK_PAD), head), pl.BlockSpec((HEADS, tm, QK_PAD), head),
                  pl.BlockSpec((HEADS, tm, HEAD_DIM), head), pl.BlockSpec((tm, 1024), lambda i: (i, 2)),
                  pl.BlockSpec((tm, 256), row), pl.BlockSpec((tm, 256), row), _full(wq_ext.shape), _full(wkv_ext.shape),
                  _full((1, 256)), _full((1, 256))],
        out_specs=[pl.BlockSpec((tm, 1024), row), pl.BlockSpec((tm, 2048), row), pl.BlockSpec((tm, 1536), row),
                   _full((1, 512))],
        out_shape=[jax.ShapeDtypeStruct((T, 1024), F32), jax.ShapeDtypeStruct((T, 2048), BF16),
                   jax.ShapeDtypeStruct((T, 1536), BF16), jax.ShapeDtypeStruct((1, 512), F32)],
        compiler_params=_params(),
    )(dq, dk, dv, z, c1, s1, wq_ext, wkv_ext, qnw, kvnw)


_HEAD_LANES = [slice(HEAD_DIM * h, HEAD_DIM * (h + 1)) for h in range(HEADS)]


def _lower_bound(lbraw_ref):
    a0, a1 = lbraw_ref[0:1, :], lbraw_ref[1:2, :]
    mx = jnp.maximum(a0, a1)
    e0, e1 = jnp.exp(a0 - mx), jnp.exp(a1 - mx)
    return e0 / (e0 + e1)


def _tri(lower):
    r = lax.broadcasted_iota(jnp.int32, (CHUNK, CHUNK), 0)
    c = lax.broadcasted_iota(jnp.int32, (CHUNK, CHUNK), 1)
    return (r >= c) if lower else (r <= c)


def _hgrn_gates(q, f, lb, tri_lo):
    sg = _sigmoid(f)
    forget = lb + (1.0 - lb) * sg
    k = 1.0 - forget
    b = _hdot(tri_lo.astype(F32), jnp.log(forget))
    b_ref, b_last = b[CHUNK // 2 - 1:CHUNK // 2, :], b[CHUNK - 1:CHUNK, :]
    e1, e2, e3, e4 = jnp.exp(b - b_ref), jnp.exp(b_ref - b), jnp.exp(b_last - b), jnp.exp(b)
    return dict(sg=sg, forget=forget, k=k, e1=e1, e2=e2, e3=e3, e4=e4, qa=q * e1, ka=k * e2, kl=k * e3, qb=q * e4,
                decay=jnp.exp(b_last))


def _hgrn_fwd(z, lbraw, nw, exchange=None):
    T = z.shape[0]
    G = min(HGRN_GROUP, T // CHUNK)
    rows = G * CHUNK
    n_chunks = T // CHUNK

    def body(q_ref, f_ref, i_ref, g_ref, lbraw_ref, nw_ref, oraw_ref, og_ref, sp_ref, st_ref):
        @pl.when(pl.program_id(0) == 0)
        def _():
            st_ref[...] = jnp.zeros_like(st_ref)

        lb_all = _lower_bound(lbraw_ref)
        tri_lo = _tri(True)

        def chunk(cc, carry):
            rs = pl.ds(pl.multiple_of(cc * CHUNK, CHUNK), CHUNK)
            t = _hgrn_gates(q_ref[rs, :], f_ref[rs, :], lb_all, tri_lo)
            v, gate = i_ref[rs, :], g_ref[rs, :]
            st = [st_ref[h] for h in range(HEADS)]
            a = [jnp.where(tri_lo, _bdot(t["qa"][:, s], t["ka"][:, s], NT), 0.0) for s in _HEAD_LANES]
            kv = [_bdot(v[:, s], t["kl"][:, s], TN) for s in _HEAD_LANES]
            o = [_bdot(a[h], v[:, s]) + _bdot(t["qb"][:, s], st[h], NT) for h, s in enumerate(_HEAD_LANES)]
            for h, s in enumerate(_HEAD_LANES):
                sp_ref[cc, h] = st[h]
                st_ref[h] = st[h] * t["decay"][:, s] + kv[h]
            oraw_ref[rs, :] = jnp.concatenate(o, axis=1)
            on = jnp.concatenate([_rms_fwd(o[h], nw_ref[:, s])[0] for h, s in enumerate(_HEAD_LANES)], axis=1)
            og_ref[rs, :] = (on * (gate * _sigmoid(gate))).astype(BF16)
            return carry

        lax.fori_loop(0, G, chunk, 0, unroll=4)

    col = lambda j: pl.BlockSpec((rows, 512), lambda r, j=j: (r, j))
    return _call(
        body, "hgrn_fwd", (z, z, z, z, lbraw, nw), grid=(T // rows,),
        in_specs=[col(0), col(1), col(2), col(3), _full((2, 512)), _full((1, 512))],
        out_specs=[col(0), col(0), pl.BlockSpec((G, HEADS, HEAD_DIM, HEAD_DIM), lambda r: (r, 0, 0, 0))],
        out_shape=[jax.ShapeDtypeStruct((T, 512), F32), jax.ShapeDtypeStruct((T, 512), BF16),
                   jax.ShapeDtypeStruct((n_chunks, HEADS, HEAD_DIM, HEAD_DIM), F32)],
        scratch_shapes=[pltpu.VMEM((HEADS, HEAD_DIM, HEAD_DIM), F32)], exchange=exchange)


def _hgrn_bwd(dmixcat, z, oraw, sprev, lbraw, nw, exchange=None):
    T = z.shape[0]
    G = min(HGRN_GROUP, T // CHUNK)
    rows = G * CHUNK
    ng = T // rows

    def body(dog_ref, q_ref, f_ref, i_ref, g_ref, oraw_ref, sp_ref, lbraw_ref, nw_ref,
             dz_ref, dsmall_ref, dst_ref):
        @pl.when(pl.program_id(0) == 0)
        def _():
            dst_ref[...] = jnp.zeros_like(dst_ref)
            dsmall_ref[...] = jnp.zeros_like(dsmall_ref)

        lb_all = _lower_bound(lbraw_ref)
        tri_lo, tri_up = _tri(True), _tri(False)
        rowid = lax.broadcasted_iota(jnp.int32, (CHUNK, HEADS * HEAD_DIM), 0)

        def chunk(it, carry):
            cc = G - 1 - it
            rs = pl.ds(pl.multiple_of(cc * CHUNK, CHUNK), CHUNK)
            heads = list(enumerate(_HEAD_LANES))
            cat = lambda parts: jnp.concatenate(parts, axis=1)
            per_head_mean = lambda x: cat([jnp.broadcast_to(_lanemean(x[:, s]), (CHUNK, HEAD_DIM)) for s in _HEAD_LANES])
            t = _hgrn_gates(q_ref[rs, :], f_ref[rs, :], lb_all, tri_lo)
            v, gate, o, dog, nw_all = i_ref[rs, :], g_ref[rs, :], oraw_ref[rs, :], dog_ref[rs, :], nw_ref[...]
            rs_o = lax.rsqrt(per_head_mean(o * o) + RMS_EPS)
            xhat = o * rs_o
            sgg = _sigmoid(gate)
            d_on = dog * (gate * sgg)
            dz_ref[rs, 1536:2048] = dog * (xhat * nw_all) * (sgg * (1.0 + gate * (1.0 - sgg)))
            dxh = d_on * nw_all
            do = rs_o * (dxh - xhat * per_head_mean(dxh * xhat))
            dsmall_ref[:, 512:1024] += _rowsum(d_on * xhat)
            st = [sp_ref[cc, h] for h in range(HEADS)]
            dst = [dst_ref[h] for h in range(HEADS)]
            a = [jnp.where(tri_lo, _bdot(t["qa"][:, s], t["ka"][:, s], NT), 0.0) for s in _HEAD_LANES]
            da = [jnp.where(tri_lo, _bdot(do[:, s], v[:, s], NT), 0.0) for s in _HEAD_LANES]
            dqb = cat([_bdot(do[:, s], st[h]) for h, s in heads])
            dkl = cat([_bdot(v[:, s], dst[h]) for h, s in heads])
            dv_ = cat([_bdot(t["kl"][:, s], dst[h], NT) + _bdot(a[h], do[:, s], TN) for h, s in heads])
            dqa = cat([_bdot(da[h], t["ka"][:, s]) for h, s in heads])
            dka = cat([_bdot(da[h], t["qa"][:, s], TN) for h, s in heads])
            ddecay = cat([_rowsum(dst[h] * st[h]) for h in range(HEADS)])
            for h, s in heads:
                dst_ref[h] = dst[h] * t["decay"][:, s] + _bdot(do[:, s], t["qb"][:, s], TN)
            pa, pk, pb, pl_ = dqa * t["qa"], dka * t["ka"], dqb * t["qb"], dkl * t["kl"]
            db = pa - pk + pb - pl_
            db = db + jnp.where(rowid == CHUNK // 2 - 1, _rowsum(pk - pa), 0.0)
            db = db + jnp.where(rowid == CHUNK - 1, _rowsum(pl_) + ddecay * t["decay"], 0.0)
            dlogf = _hdot(tri_up.astype(F32), db)
            dforget = dlogf / t["forget"] - (dka * t["e2"] + dkl * t["e3"])
            sg = t["sg"]
            dz_ref[rs, 0:512] = dqa * t["e1"] + dqb * t["e4"]
            dz_ref[rs, 512:1024] = dforget * (1.0 - lb_all) * sg * (1.0 - sg)
            dz_ref[rs, 1024:1536] = dv_
            dsmall_ref[:, 0:512] += _rowsum(dforget * (1.0 - sg))
            return carry

        lax.fori_loop(0, G, chunk, 0, unroll=4)

    col = lambda j: pl.BlockSpec((rows, 512), lambda r, j=j: (ng - 1 - r, j))
    return _call(
        body, "hgrn_bwd", (dmixcat, z, z, z, z, oraw, sprev, lbraw, nw), grid=(ng,),
        in_specs=[col(0), col(0), col(1), col(2), col(3), col(0),
                  pl.BlockSpec((G, HEADS, HEAD_DIM, HEAD_DIM), lambda r: (ng - 1 - r, 0, 0, 0)),
                  _full((2, 512)), _full((1, 512))],
        out_specs=[pl.BlockSpec((rows, 2048), lambda r: (ng - 1 - r, 0)), _full((1, 1024))],
        out_shape=[jax.ShapeDtypeStruct((T, 2048), F32), jax.ShapeDtypeStruct((1, 1024), F32)],
        scratch_shapes=[pltpu.VMEM((HEADS, HEAD_DIM, HEAD_DIM), F32)], exchange=exchange)


def _diag_mask(t):
    r = lax.broadcasted_iota(jnp.int32, (t, t), 0)
    c = lax.broadcasted_iota(jnp.int32, (t, t), 1)
    return r >= c


def _attn_fwd(q, k, v, exchange=None):
    _, T, _ = q.shape
    t = min(ATT_TILE, T)

    def body(q_ref, k_ref, v_ref, o_ref, lse_ref):
        i = pl.program_id(1)
        qb = q_ref[...]

        rows = lambda j: pl.ds(pl.multiple_of(j * t, t), t)

        def logits(j, masked):
            s = _dot(qb, k_ref[rows(j), :], NT)
            return jnp.where(_diag_mask(t), s, NEG_BIG) if masked else s

        def absorb(s, j, carry):
            m, l, acc = carry
            mn = jnp.maximum(m, jnp.max(s, axis=-1, keepdims=True))
            p = jnp.exp2(s - mn)
            al = jnp.exp2(m - mn)
            return mn, al * l + jnp.sum(p, axis=-1, keepdims=True), al * acc + _dot(p.astype(BF16), v_ref[rows(j), :])

        def pair(j0, carry, last_masked):
            s0, s1 = logits(j0, False), logits(j0 + 1, last_masked)
            return absorb(s1, j0 + 1, absorb(s0, j0, carry))

        init = (jnp.full((t, 1), NEG_BIG, F32), jnp.zeros((t, 1), F32), jnp.zeros((t, HEAD_DIM), F32))
        carry = lax.fori_loop(0, i // 2, lambda jj, c: pair(2 * jj, c, False), init)
        m, l, acc = lax.cond(i % 2 == 1, lambda c: pair(i - 1, c, True),
                             lambda c: absorb(logits(i, True), i, c), carry)
        o_ref[...] = acc / l
        lse_ref[...] = jnp.broadcast_to(m + jnp.log2(l), (t, HEAD_DIM))

    return _call(
        body, "attn_fwd", (q, k, v), grid=(HEADS, T // t),
        in_specs=[pl.BlockSpec((None, t, QK_PAD), lambda h, i: (h, i, 0)),
                  pl.BlockSpec((None, T, QK_PAD), lambda h, i: (h, 0, 0)),
                  pl.BlockSpec((None, T, HEAD_DIM), lambda h, i: (h, 0, 0))],
        out_specs=[pl.BlockSpec((t, HEAD_DIM), lambda h, i: (i, h)),
                   pl.BlockSpec((None, t, HEAD_DIM), lambda h, i: (h, i, 0))],
        out_shape=[jax.ShapeDtypeStruct((T, HEADS * HEAD_DIM), F32), jax.ShapeDtypeStruct((HEADS, T, HEAD_DIM), F32)],
        exchange=exchange)


def _attn_bwd(q, k, v, dmixcat, o, lse, exchange=None):
    _, T, _ = q.shape
    t = min(ATT_TILE, T)
    nq = T // t

    def body(q_ref, k_ref, v_ref, do_ref, o_ref, lse_ref, dq_ref, dk_ref, dv_ref, delta_ref):
        j = pl.program_id(1)

        @pl.when(j == 0)
        def _():
            dq_ref[...] = jnp.zeros_like(dq_ref)

            def fill(i, carry):
                rs = pl.ds(pl.multiple_of(i * t, t), t)
                delta_ref[rs, :] = jnp.broadcast_to(
                    jnp.sum(do_ref[rs, :] * o_ref[rs, :], axis=-1, keepdims=True), (t, HEAD_DIM))
                return carry

            lax.fori_loop(0, nq, fill, 0)

        kb, vb = k_ref[...], v_ref[...]

        def steps(blocks, carry):
            dk, dv = carry
            rs = [pl.ds(pl.multiple_of(i * t, t), t) for i, _ in blocks]
            qb = [q_ref[r, :] for r in rs]
            dob = [do_ref[r, :].astype(BF16) for r in rs]
            s = [_dot(b, kb, NT) for b in qb]
            dp = [_dot(b, vb, NT) for b in dob]
            for n, (_, masked) in enumerate(blocks):
                p = jnp.exp2(s[n] - lse_ref[rs[n], 0:1])
                if masked:
                    p = jnp.where(_diag_mask(t), p, 0.0)
                ds = (p * (dp[n] - delta_ref[rs[n], 0:1]) * LN2).astype(BF16)
                dq_ref[rs[n], :] += _dot(ds, kb)
                dk = dk + _dot(ds, qb[n], TN)
                dv = dv + _dot(p.astype(BF16), dob[n], TN)
            return dk, dv

        zero = (jnp.zeros((t, QK_PAD), F32), jnp.zeros((t, HEAD_DIM), F32))
        rest = nq - 1 - j
        carry = lax.cond(rest % 2 == 1, lambda c: steps([(j, True), (j + 1, False)], c),
                         lambda c: steps([(j, True)], c), zero)
        first = j + 1 + rest % 2
        dk, dv = lax.fori_loop(0, rest // 2, lambda n, c: steps([(first + 2 * n, False), (first + 2 * n + 1, False)], c),
                               carry)
        dk_ref[...] = dk
        dv_ref[...] = dv

    return _call(
        body, "attn_bwd", (q, k, v, dmixcat, o, lse), grid=(HEADS, nq),
        in_specs=[pl.BlockSpec((None, T, QK_PAD), lambda h, j: (h, 0, 0)),
                  pl.BlockSpec((None, t, QK_PAD), lambda h, j: (h, j, 0)),
                  pl.BlockSpec((None, t, HEAD_DIM), lambda h, j: (h, j, 0)),
                  pl.BlockSpec((T, HEAD_DIM), lambda h, j: (0, HEADS + h)),
                  pl.BlockSpec((T, HEAD_DIM), lambda h, j: (0, h)),
                  pl.BlockSpec((None, T, HEAD_DIM), lambda h, j: (h, 0, 0))],
        out_specs=[pl.BlockSpec((None, T, QK_PAD), lambda h, j: (h, 0, 0)),
                   pl.BlockSpec((None, t, QK_PAD), lambda h, j: (h, j, 0)),
                   pl.BlockSpec((None, t, HEAD_DIM), lambda h, j: (h, j, 0))],
        out_shape=[jax.ShapeDtypeStruct((HEADS, T, QK_PAD), F32), jax.ShapeDtypeStruct((HEADS, T, QK_PAD), F32),
                   jax.ShapeDtypeStruct((HEADS, T, HEAD_DIM), F32)],
        scratch_shapes=[pltpu.VMEM((T, HEAD_DIM), F32)], exchange=exchange)


def _ln_fwd(r):
    mu = _lanemean(r)
    xc = r - mu
    rstd = lax.rsqrt(_lanemean(xc * xc) + LN_EPS)
    return xc * rstd, rstd


def _ln_bwd(dxh, xhat, rstd):
    return rstd * (dxh - _lanemean(dxh) - xhat * _lanemean(dxh * xhat))


def _mix_ln1(o_hg, o_mla, w_out, x, g_a, ln1_g, ln1_b, sc_m, sh_m, exchange=None):
    T = x.shape[0]
    tm = min(ROW_TILE_SMALL, T)
    half = o_hg.shape[1]

    def body(hg_ref, mla_ref, w_ref, x_ref, ga_ref, g_ref, b_ref, sc_ref, sh_ref, mix_ref, xhat_ref, rstd_ref, u2_ref):
        mix = _dot(hg_ref[...], w_ref[0:half, :]) + _bdot(mla_ref[...], w_ref[half:, :])
        mix_ref[...] = mix
        xhat, rstd = _ln_fwd(ALPHA * x_ref[...] + (1.0 + ga_ref[...]) * mix)
        xhat_ref[...] = xhat
        rstd_ref[...] = jnp.broadcast_to(rstd, (tm, 128))
        u2_ref[...] = _modulate(xhat * g_ref[...] + b_ref[...], sc_ref[...], sh_ref[...]).astype(BF16)

    row = pl.BlockSpec((tm, D_MODEL), lambda i: (i, 0))
    vec = _full((1, D_MODEL))
    halfrow = pl.BlockSpec((tm, half), lambda i: (i, 0))
    return _call(
        body, "mix_ln1", (o_hg, o_mla, w_out, x, g_a, ln1_g, ln1_b, sc_m, sh_m), grid=(T // tm,),
        in_specs=[halfrow, halfrow, _full(w_out.shape), row, vec, vec, vec, vec, vec],
        out_specs=[row, row, pl.BlockSpec((tm, 128), lambda i: (i, 0)), row],
        out_shape=[jax.ShapeDtypeStruct((T, D_MODEL), F32), jax.ShapeDtypeStruct((T, D_MODEL), F32),
                   jax.ShapeDtypeStruct((T, 128), F32), jax.ShapeDtypeStruct((T, D_MODEL), BF16)],
        exchange=exchange)


def _mlp_fwd(u2, w1, w2, xhat1, ln1_g, ln1_b, g_m, ln2_g, ln2_b, target):
    T = u2.shape[0]
    tf = w1.shape[-1]
    nf = N_DEV // MLP_SLABS
    tm = min(ROW_TILE, T)

    def body(u2_ref, w1_ref, w2_ref, xhat_ref, g1_ref, b1_ref, gm_ref, g2_ref, b2_ref, tgt_ref,
             r_ref, dr2_ref, dh_ref, small_ref, acc_ref):
        i, f = pl.program_id(0), pl.program_id(1)
        dm = D_MODEL

        @pl.when((i == 0) & (f == 0))
        def _():
            small_ref[...] = jnp.zeros_like(small_ref)

        @pl.when(f == 0)
        def _():
            acc_ref[...] = jnp.zeros_like(acc_ref)

        u2t = u2_ref[...]
        part = None
        for s in range(MLP_SLABS):
            r = jnp.maximum(_dot(u2t, w1_ref[s]), 0.0)
            r_ref[:, s * tf:(s + 1) * tf] = r.astype(BF16)
            d = _bdot(r * r, w2_ref[s])
            part = d if part is None else part + d
        acc_ref[...] += part

        @pl.when(f == nf - 1)
        def _():
            h = acc_ref[...]
            x1 = xhat_ref[...] * g1_ref[...] + b1_ref[...]
            xhat2, rstd2 = _ln_fwd(ALPHA * x1 + (1.0 + gm_ref[...]) * h)
            err = xhat2 * g2_ref[...] + b2_ref[...] - tgt_ref[...]
            small_ref[:, 3 * dm:] += jnp.sum(0.5 * _lanemean(err * err), axis=0, keepdims=True)
            dy = err * (1.0 / D_MODEL)
            small_ref[:, dm:2 * dm] += _rowsum(dy * xhat2)
            small_ref[:, 2 * dm:3 * dm] += _rowsum(dy)
            dr2 = _ln_bwd(dy * g2_ref[...], xhat2, rstd2)
            dr2_ref[...] = dr2
            small_ref[:, 0:dm] += _rowsum(dr2 * h)
            dh_ref[...] = ((1.0 + gm_ref[...]) * dr2).astype(BF16)

    row = pl.BlockSpec((tm, D_MODEL), lambda i, f: (i, 0))
    vec = _full((1, D_MODEL))
    return pl.pallas_call(
        body, name="mlp_fwd", grid=(T // tm, nf),
        in_specs=[row, pl.BlockSpec((MLP_SLABS, D_MODEL, tf), lambda i, f: (f, 0, 0)),
                  pl.BlockSpec((MLP_SLABS, tf, D_MODEL), lambda i, f: (f, 0, 0)),
                  row, vec, vec, vec, vec, vec, row],
        out_specs=[pl.BlockSpec((tm, MLP_SLABS * tf), lambda i, f: (i, f)), row, row, _full((1, 3 * D_MODEL + 128))],
        out_shape=[jax.ShapeDtypeStruct((T, N_DEV * tf), BF16), jax.ShapeDtypeStruct((T, D_MODEL), F32),
                   jax.ShapeDtypeStruct((T, D_MODEL), BF16), jax.ShapeDtypeStruct((1, 3 * D_MODEL + 128), F32)],
        scratch_shapes=[pltpu.VMEM((tm, D_MODEL), F32)],
        compiler_params=_params(),
    )(u2, w1, w2, xhat1, ln1_g, ln1_b, g_m, ln2_g, ln2_b, target)


def _mlp_bwd(dh, w1, w2, r, dr2, xhat1, rstd1, mix, ln1_g, ln1_b, sc_m, g_a):
    T = dh.shape[0]
    tf = w1.shape[-1]
    nf = N_DEV // MLP_SLABS
    tm = min(ROW_TILE, T)

    def body(dh_ref, w1_ref, w2_ref, r_ref, dr2_ref, xhat_ref, rstd_ref, mix_ref, g1_ref, b1_ref, sc_ref, ga_ref,
             dhpre_ref, dr1_ref, dmix_ref, small_ref, acc_ref):
        i, f = pl.program_id(0), pl.program_id(1)
        dm = D_MODEL

        @pl.when((i == 0) & (f == 0))
        def _():
            small_ref[...] = jnp.zeros_like(small_ref)

        @pl.when(f == 0)
        def _():
            acc_ref[...] = jnp.zeros_like(acc_ref)

        dht = dh_ref[...]
        part = None
        for s in range(MLP_SLABS):
            cols = slice(s * tf, (s + 1) * tf)
            dhpre = (_dot(dht, w2_ref[s], NT) * (2.0 * r_ref[:, cols].astype(F32))).astype(BF16)
            dhpre_ref[:, cols] = dhpre
            d = _dot(dhpre, w1_ref[s], NT)
            part = d if part is None else part + d
        acc_ref[...] += part

        @pl.when(f == nf - 1)
        def _():
            du2 = acc_ref[...]
            xhat = xhat_ref[...]
            x1 = xhat * g1_ref[...] + b1_ref[...]
            dx1 = ALPHA * dr2_ref[...] + du2 * (1.0 + sc_ref[...])
            small_ref[:, 2 * dm:3 * dm] += _rowsum(du2 * x1)
            small_ref[:, dm:2 * dm] += _rowsum(du2)
            small_ref[:, 3 * dm:4 * dm] += _rowsum(dx1 * xhat)
            small_ref[:, 4 * dm:5 * dm] += _rowsum(dx1)
            dr1 = _ln_bwd(dx1 * g1_ref[...], xhat, rstd_ref[:, 0:1])
            dr1_ref[...] = dr1
            small_ref[:, 0:dm] += _rowsum(dr1 * mix_ref[...])
            dmix_ref[...] = ((1.0 + ga_ref[...]) * dr1).astype(BF16)

    row = pl.BlockSpec((tm, D_MODEL), lambda i, f: (i, 0))
    vec = _full((1, D_MODEL))
    return pl.pallas_call(
        body, name="mlp_bwd", grid=(T // tm, nf),
        in_specs=[row, pl.BlockSpec((MLP_SLABS, D_MODEL, tf), lambda i, f: (f, 0, 0)),
                  pl.BlockSpec((MLP_SLABS, tf, D_MODEL), lambda i, f: (f, 0, 0)),
                  pl.BlockSpec((tm, MLP_SLABS * tf), lambda i, f: (i, f)), row, row,
                  pl.BlockSpec((tm, 128), lambda i, f: (i, 0)), row, vec, vec, vec, vec],
        out_specs=[pl.BlockSpec((tm, MLP_SLABS * tf), lambda i, f: (i, f)), row, row, _full((1, 5 * D_MODEL))],
        out_shape=[jax.ShapeDtypeStruct((T, N_DEV * tf), BF16), jax.ShapeDtypeStruct((T, D_MODEL), F32),
                   jax.ShapeDtypeStruct((T, D_MODEL), BF16), jax.ShapeDtypeStruct((1, 5 * D_MODEL), F32)],
        scratch_shapes=[pltpu.VMEM((tm, D_MODEL), F32)],
        compiler_params=_params(),
    )(dh, w1, w2, r, dr2, xhat1, rstd1, mix, ln1_g, ln1_b, sc_m, g_a)


def _input_bwd(dz_h, dz_m, w_in_ext, x, dr1, sc_a, exchange=None):
    T = x.shape[0]
    tm = min(ROW_TILE, T)

    def body(dzh_ref, dzm_ref, w_ref, x_ref, dr1_ref, sc_ref, gx_ref, small_ref):
        @pl.when(pl.program_id(0) == 0)
        def _():
            small_ref[...] = jnp.zeros_like(small_ref)

        du = _bdot(dzh_ref[...], w_ref[0:2048, :]) + _bdot(dzm_ref[...], w_ref[2048:3072, :])
        gx_ref[...] = ALPHA * dr1_ref[...] + du * (1.0 + sc_ref[...])
        small_ref[:, D_MODEL:] += _rowsum(du * x_ref[...])
        small_ref[:, 0:D_MODEL] += _rowsum(du)

    row = pl.BlockSpec((tm, D_MODEL), lambda i: (i, 0))
    vec = _full((1, D_MODEL))
    return _call(
        body, "input_bwd", (dz_h, dz_m, w_in_ext, x, dr1, sc_a), grid=(T // tm,),
        in_specs=[pl.BlockSpec((tm, 2048), lambda i: (i, 0)), row, _full(w_in_ext.shape), row, row, vec],
        out_specs=[row, _full((1, 2 * D_MODEL))],
        out_shape=[jax.ShapeDtypeStruct((T, D_MODEL), F32), jax.ShapeDtypeStruct((1, 2 * D_MODEL), F32)],
        exchange=exchange)


def _adam_math(w, g, m, v):
    m = ADAM_B1 * m + (1.0 - ADAM_B1) * g
    v = ADAM_B2 * v + (1.0 - ADAM_B2) * (g * g)
    m_hat = m / (1.0 - ADAM_B1 ** ADAM_STEP)
    v_hat = v / (1.0 - ADAM_B2 ** ADAM_STEP)
    return -ADAM_LR * (m_hat / (jnp.sqrt(v_hat) + ADAM_EPS) + ADAM_WD * w), m, v


def _adam(g_slabs, w, m, v, name, g_fn=None, g_extra=()):
    R, C = w.shape
    tr = 256 if R % 256 == 0 else R
    ns = 0 if g_slabs is None else g_slabs.shape[0]
    slab_rows = tr if g_slabs is None or g_slabs.shape[1] == R else g_slabs.shape[1]
    assert slab_rows == tr or tr == R
    ne = len(g_extra)

    def body(*refs):
        e_refs = refs[:ne]
        refs = refs[ne:]
        if ns:
            gs_ref, refs = refs[0], refs[1:]
        w_ref, m_ref, v_ref, g_ref, d_ref, nm_ref, nv_ref = refs
        if g_fn is not None:
            g = g_fn(*e_refs)
        else:
            g = gs_ref[0].astype(F32)
            for s in range(1, ns):
                g = g + gs_ref[s].astype(F32)
            g = g[:tr]
        d, nm, nv = _adam_math(w_ref[...], g, m_ref[...], v_ref[...])
        g_ref[...] = g
        d_ref[...] = d
        nm_ref[...] = nm
        nv_ref[...] = nv

    blk = pl.BlockSpec((tr, C), lambda i: (i, 0))
    in_specs = [pl.BlockSpec((tr, e.shape[1]), lambda i: (i, 0)) if e.shape[0] == R else _full(e.shape) for e in g_extra]
    args = list(g_extra)
    if ns:
        in_specs.append(pl.BlockSpec((ns, slab_rows, C), lambda i: (0, i, 0)))
        args.append(g_slabs)
    return pl.pallas_call(
        body, name=name, grid=(R // tr,), in_specs=in_specs + [blk] * 3, out_specs=[blk] * 4,
        out_shape=[jax.ShapeDtypeStruct((R, C), F32)] * 4, compiler_params=_params(),
    )(*args, w, m, v)


def _adam_small(small_all, params):
    n = len(params)

    def body(*refs):
        s_ref, refs = refs[0], refs[1:]
        wmv, loss_ref, outs = refs[:3 * n], refs[3 * n], refs[3 * n + 1:]
        tot = s_ref[0]
        for i in range(1, N_DEV):
            tot = tot + s_ref[i]
        loss_ref[...] = tot[:, SMALL_W - 128:]
        for j, (w, _, _, off) in enumerate(params):
            w_ref, m_ref, v_ref = wmv[3 * j:3 * j + 3]
            g_ref, d_ref, nm_ref, nv_ref = outs[4 * j:4 * j + 4]
            if w.shape[0] == 2:
                lb = _lower_bound(w_ref)
                g0 = tot[:, off:off + w.shape[1]] * lb * (1.0 - lb)
                rows = [(slice(0, 1), g0), (slice(1, 2), -g0)]
            else:
                rows = [(slice(0, 1), tot[:, off:off + w.shape[1]])]
            for rs, g in rows:
                d, nm, nv = _adam_math(w_ref[rs, :], g, m_ref[rs, :], v_ref[rs, :])
                g_ref[rs, :], d_ref[rs, :], nm_ref[rs, :], nv_ref[rs, :] = g, d, nm, nv

    out_shape = [jax.ShapeDtypeStruct((1, 128), F32)]
    for w, _, _, _ in params:
        out_shape += [jax.ShapeDtypeStruct(w.shape, F32)] * 4
    res = pl.pallas_call(body, name="adam_small", out_shape=out_shape, compiler_params=_params())(
        small_all, *[a for w, m, v, _ in params for a in (w, m, v)])
    return res[0], [tuple(res[1 + 4 * j:5 + 4 * j]) for j in range(n)]


def _cols_from_slabs(g):
    s, r, c = g.shape
    return jnp.transpose(g, (1, 0, 2)).reshape(r, s * c)


def _slabs_from_cols(w):
    r, c = w.shape
    return jnp.transpose(w.reshape(r, N_DEV, c // N_DEV), (1, 0, 2))


def _rot_half_rows(wt):
    return jnp.concatenate([-wt[32:], wt[:32]], axis=0)


def _unrot_half_rows(dwt_rot):
    return jnp.concatenate([dwt_rot[32:], -dwt_rot[:32]], axis=0)


def _ext_in_t(g):
    k_in = g.shape[2]
    z64, z128 = jnp.zeros((64, k_in), BF16), jnp.zeros((128, k_in), BF16)
    wt = g.reshape(N_DEV * g.shape[1], k_in)
    main, wk = wt[:wt.shape[0] - ROPE_DIM], wt[wt.shape[0] - ROPE_DIM:]
    return jnp.concatenate([main, z128, wk, z64, z128, _rot_half_rows(wk), z64], axis=0)


def _ext_q_t(wt):
    r = wt.shape[1]
    z64, z128 = jnp.zeros((64, r), BF16), jnp.zeros((128, r), BF16)
    per = HEAD_DIM + ROPE_DIM
    main = [jnp.concatenate([wt[per * h:per * (h + 1)], z64], axis=0) for h in range(HEADS)]
    rot = [jnp.concatenate([z128, _rot_half_rows(wt[per * h + HEAD_DIM:per * (h + 1)]), z64], axis=0)
           for h in range(HEADS)]
    return jnp.concatenate(main + rot, axis=0)


def _ext_kv(w_kv_up):
    r = w_kv_up.shape[0]
    z128 = jnp.zeros((r, 128), BF16)
    wkv = w_kv_up.reshape(r, HEADS, 2 * HEAD_DIM)
    kpad = [jnp.concatenate([wkv[:, h, :HEAD_DIM], z128], axis=1) for h in range(HEADS)]
    vals = [wkv[:, h, HEAD_DIM:] for h in range(HEADS)]
    return jnp.concatenate(kpad + vals, axis=1)


def _grad_in_from_ext_t(dwt_h, dwt_m):
    dwk = dwt_m[512 + 128:512 + 192] + _unrot_half_rows(dwt_m[768 + 128:768 + 192])
    return jnp.concatenate([dwt_h, dwt_m[:512], dwk], axis=0)


def _grad_q_from_ext_t(dwq_ext_t):
    rows = []
    for h in range(HEADS):
        main, rot = dwq_ext_t[256 * h:256 * h + 256], dwq_ext_t[1024 + 256 * h:1280 + 256 * h]
        rows += [main[:128], main[128:192] + _unrot_half_rows(rot[128:192])]
    return jnp.concatenate(rows, axis=0)


def _grad_kv_from_ext(dwkv_ext):
    kvcols = []
    for h in range(HEADS):
        kvcols += [dwkv_ext[:, 256 * h:256 * h + 128], dwkv_ext[:, 1024 + 128 * h:1152 + 128 * h]]
    return jnp.concatenate(kvcols, axis=1)


SMALL_W = 6144 + 512 + 512 + 256 + 256 + 4 * 1024 + 128


def kernel(x, c, positions, w_ada, b_ada, w_in, hg_lower_bounds, hg_norm_w, mla_q_norm_w, w_q_up, mla_kv_norm_w, w_kv_up, w_out, ln1_g, ln1_b, w_mlp_in, w_mlp_out, ln2_g, ln2_b, loss_target, m_w_ada, m_b_ada, m_w_in, m_hg_lower_bounds, m_hg_norm_w, m_mla_q_norm_w, m_w_q_up, m_mla_kv_norm_w, m_w_kv_up, m_w_out, m_ln1_g, m_ln1_b, m_w_mlp_in, m_w_mlp_out, m_ln2_g, m_ln2_b, v_w_ada, v_b_ada, v_w_in, v_hg_lower_bounds, v_hg_norm_w, v_mla_q_norm_w, v_w_q_up, v_mla_kv_norm_w, v_w_kv_up, v_w_out, v_ln1_g, v_ln1_b, v_w_mlp_in, v_w_mlp_out, v_ln2_g, v_ln2_b):
    T = x.shape[1]
    me = 4 * lax.axis_index("x") + 2 * lax.axis_index("y") + lax.axis_index("c")
    xs, tgt = x[0], loss_target[0]
    transposed = ("w_in", "w_q_up")
    as_used = lambda n, a: a[0].T if n in transposed else a[0]
    big = {n: as_used(n, a) for n, a in dict(w_in=w_in, w_q_up=w_q_up, w_kv_up=w_kv_up, w_out=w_out,
                                              w_mlp_in=w_mlp_in, w_mlp_out=w_mlp_out).items()}
    names = list(big)

    bf = {n: big[n].astype(BF16) for n in names}
    g_in, g_c = _gather_two_level([bf["w_in"], c], name="gather_w_in")
    c_all = g_c.reshape(N_DEV, D_MODEL)

    ada_cols = w_ada.shape[2]
    mod_part, cond = _mod_part(c_all, w_ada[0], lax.dynamic_slice(b_ada, (0, me * ada_cols), (1, ada_cols)))
    (mod_all,) = _exchange([mod_part], scatter=False, name="gather_mod")
    mod_row = lax.dynamic_slice(mod_all, (0, me, 0), (N_DEV, 1, ada_cols)).reshape(1, N_DEV * ada_cols)
    sh_a, sc_a, g_a, sh_m, sc_m, g_m = [mod_row[:, D_MODEL * i:D_MODEL * (i + 1)] for i in range(6)]

    w_in_ext = _ext_in_t(g_in)
    z, (g_q, g_kv, g_out) = _matmul(xs, w_in_ext, "NT", "in_proj", a_fn=_modulate, extras=(sc_a, sh_a), tn=3072,
                                    exchange=_Exchange([bf["w_q_up"], bf["w_kv_up"], bf["w_out"]], False))
    wq_ext = _ext_q_t(g_q.reshape(N_DEV * g_q.shape[1], g_q.shape[2]))
    wkv_ext = _ext_kv(_cols_from_slabs(g_kv))
    w_out_full = g_out.reshape(D_MODEL, D_MODEL)
    inv_freq = 1.0 / (ROPE_THETA ** (jnp.arange(0, ROPE_DIM, 2, dtype=F32) / ROPE_DIM))
    zeros = lambda n: jnp.zeros((n,), F32)
    invf = jnp.concatenate([zeros(128), inv_freq, inv_freq, zeros(64)]).reshape(1, QK_PAD)
    m_one = jnp.concatenate([jnp.ones((128,), F32), zeros(128)]).reshape(1, QK_PAD)
    m_rot = jnp.concatenate([zeros(128), jnp.ones((64,), F32), zeros(64)]).reshape(1, QK_PAD)
    q, k, v, c1, s1, cqn, ckvn = _mla_pre(z, positions.reshape(T, 1), invf, m_one, m_rot, wq_ext, wkv_ext,
                                          mla_q_norm_w, mla_kv_norm_w)[0]
    (o_raw, o_gated, s_prev), (w1,) = _hgrn_fwd(z, hg_lower_bounds, hg_norm_w,
                                                exchange=_StagedGather(bf["w_mlp_in"]))
    (o_mla, lse), (w2,) = _attn_fwd(q, k, v, exchange=_StagedGather(bf["w_mlp_out"]))
    mix, xhat1, rstd1, u2 = _mix_ln1(o_gated, o_mla, w_out_full, xs, g_a, ln1_g, ln1_b, sc_m, sh_m)[0]
    r, dr2, dh, small_mlp_fwd = _mlp_fwd(u2, w1, w2, xhat1, ln1_g, ln1_b, g_m, ln2_g, ln2_b, tgt)

    dhpre, dr1, dmix, small_mlp_bwd = _mlp_bwd(dh, w1, w2, r, dr2, xhat1, rstd1, mix, ln1_g, ln1_b, sc_m, g_a)
    received = {}
    dw2 = _matmul(r, dh, "TN", "wgrad_mlp_out", out_dtype=BF16, a_fn=_square, tm=1024, tk=2048)
    dw1 = _matmul(u2, dhpre, "TN", "wgrad_mlp_in", out_dtype=BF16, tm=1024, tk=2048, out_slabs=N_DEV)
    dmixcat = _matmul(dmix, w_out_full, "NT", "dgrad_out")
    dw_out = jnp.concatenate([_matmul(o_gated, dmix, "TN", "wgrad_out_hg", out_dtype=BF16, tk=2048),
                              _matmul(o_mla, dmix, "TN", "wgrad_out_mla", out_dtype=BF16, tk=2048)], axis=0)
    (dz_h, small_hgrn), (received["w_out"], received["w_mlp_in"]) = _hgrn_bwd(
        dmixcat, z, o_raw, s_prev, hg_lower_bounds, hg_norm_w,
        exchange=[_Exchange([dw_out.reshape(N_DEV, D_MODEL // N_DEV, D_MODEL)], True), _StagedScatter(dw1, 0.1)])
    (dq, dk, dv), (received["w_mlp_out"],) = _attn_bwd(
        q, k, v, dmixcat, o_mla, lse,
        exchange=_StagedScatter(dw2.reshape(N_DEV, dw2.shape[0] // N_DEV, D_MODEL), 0.1))
    dz_m, dq_ext, dkv_ext, small_mla = _mla_bwd(dq, dk, dv, z, c1, s1, wq_ext, wkv_ext, mla_q_norm_w, mla_kv_norm_w)
    dwq_t = _grad_q_from_ext_t(_matmul(dq_ext, cqn, "TN", "wgrad_q_up", tm=1024, tk=2048))
    dwkv = _grad_kv_from_ext(_matmul(ckvn, dkv_ext, "TN", "wgrad_kv_up", tn=1536, tk=2048))
    qkv_slabs = [dwq_t.reshape((N_DEV, dwq_t.shape[0] // N_DEV, dwq_t.shape[1])).astype(BF16),
                 _slabs_from_cols(dwkv).astype(BF16)]
    dwt_h, (received["w_q_up"], received["w_kv_up"]) = _matmul(
        dz_h, xs, "TN", "wgrad_in_h", b_fn=_modulate, extras=(sc_a, sh_a), tm=1024, tk=2048,
        exchange=_Exchange(qkv_slabs, True))
    dwt_m = _matmul(dz_m, xs, "TN", "wgrad_in_m", b_fn=_modulate, extras=(sc_a, sh_a), tm=1024, tk=2048)
    dw_in_t = _grad_in_from_ext_t(dwt_h, dwt_m)
    in_slabs = dw_in_t.reshape((N_DEV, dw_in_t.shape[0] // N_DEV, dw_in_t.shape[1]))
    in_slabs = jnp.pad(in_slabs, ((0, 0), (0, -in_slabs.shape[1] % 16), (0, 0))).astype(BF16)
    (grad_x, small_in), (received["w_in"],) = _input_bwd(
        dz_h, dz_m, w_in_ext, xs, dr1, sc_a, exchange=_StagedScatter(in_slabs))

    small = jnp.concatenate([small_in, small_mlp_bwd[:, :3 * D_MODEL], small_mlp_fwd[:, :D_MODEL], small_hgrn,
                             small_mla, small_mlp_bwd[:, 3 * D_MODEL:], small_mlp_fwd[:, D_MODEL:]], axis=1)
    assert small.shape == (1, SMALL_W)
    (small_all,) = _exchange([small], scatter=False, name="gather_small")

    moments = dict(w_in=(m_w_in, v_w_in), w_q_up=(m_w_q_up, v_w_q_up), w_kv_up=(m_w_kv_up, v_w_kv_up),
                   w_out=(m_w_out, v_w_out), w_mlp_in=(m_w_mlp_in, v_w_mlp_in), w_mlp_out=(m_w_mlp_out, v_w_mlp_out))
    res = {}
    for n in names:
        res[n] = _adam(received[n], big[n], as_used(n, moments[n][0]), as_used(n, moments[n][1]), name="adam_" + n)
    dmod_cols = lax.dynamic_slice(small_all.reshape(N_DEV, SMALL_W), (0, me * ada_cols), (N_DEV, ada_cols))
    cond_t = cond.T

    def ada_grad(ct_ref, dm_ref):
        g = ct_ref[:, 0:1] * dm_ref[0:1, :]
        for b in range(1, N_DEV):
            g = g + ct_ref[:, b:b + 1] * dm_ref[b:b + 1, :]
        return g

    res["w_ada"] = _adam(None, w_ada[0], m_w_ada[0], v_w_ada[0], name="adam_w_ada", g_fn=ada_grad,
                         g_extra=(cond_t, dmod_cols))

    small_params = [("b_ada", b_ada, m_b_ada, v_b_ada, 0),
                    ("hg_lower_bounds", hg_lower_bounds, m_hg_lower_bounds, v_hg_lower_bounds, 6144),
                    ("hg_norm_w", hg_norm_w, m_hg_norm_w, v_hg_norm_w, 6656),
                    ("mla_q_norm_w", mla_q_norm_w, m_mla_q_norm_w, v_mla_q_norm_w, 7168),
                    ("mla_kv_norm_w", mla_kv_norm_w, m_mla_kv_norm_w, v_mla_kv_norm_w, 7424),
                    ("ln1_g", ln1_g, m_ln1_g, v_ln1_g, 7680), ("ln1_b", ln1_b, m_ln1_b, v_ln1_b, 8704),
                    ("ln2_g", ln2_g, m_ln2_g, v_ln2_g, 9728), ("ln2_b", ln2_b, m_ln2_b, v_ln2_b, 10752)]
    loss_row, small_res = _adam_small(small_all, [p[1:] for p in small_params])
    for p, r4 in zip(small_params, small_res):
        res[p[0]] = r4
    loss = loss_row[0, 0]

    order = ["w_ada", "b_ada", "w_in", "hg_lower_bounds", "hg_norm_w", "mla_q_norm_w", "w_q_up", "mla_kv_norm_w",
             "w_kv_up", "w_out", "ln1_g", "ln1_b", "w_mlp_in", "w_mlp_out", "ln2_g", "ln2_b"]
    def as_given(n, a):
        if n in transposed:
            a = a.T
        return a[None] if n in big or n == "w_ada" else a

    shaped = {n: tuple(as_given(n, a) for a in res[n]) for n in order}
    outs = [loss, grad_x.reshape(1, T, D_MODEL)]
    for i in range(4):
        outs += [shaped[n][i] for n in order]
    return tuple(outs)
```

```python
import functools

import jax
import jax.numpy as jnp
import numpy as np
from jax import lax
from jax.experimental import pallas as pl
from jax.experimental.pallas import tpu as pltpu

F32, BF16 = jnp.float32, jnp.bfloat16
N_DEV = 8
D_MODEL = 1024
HEADS = 4
HEAD_DIM = 128
ROPE_DIM = 64
QK_PAD = 256
CHUNK = 64
ROPE_THETA = 10000.0
RMS_EPS = 1e-6
LN_EPS = 1e-5
ALPHA = 2.0 ** 0.25
ATT_SCALE = (HEAD_DIM + ROPE_DIM) ** -0.5
LN2 = float(np.log(2.0))
Q_PRESCALE = ATT_SCALE / LN2
ADAM_LR, ADAM_B1, ADAM_B2, ADAM_EPS, ADAM_WD, ADAM_STEP = 0.001, 0.9, 0.999, 1e-08, 0.01, 10
NEG_BIG = -1e30

ROW_TILE = 512
ATT_TILE = 512
HGRN_GROUP = 8
MLP_SLABS = 4
VMEM_LIMIT = 56 * 2 ** 20

NN = (((1,), (0,)), ((), ()))
NT = (((1,), (1,)), ((), ()))
TN = (((0,), (0,)), ((), ()))


def _dot(a, b, dims=NN):
    return lax.dot_general(a, b, dims, preferred_element_type=F32)


def _bdot(a, b, dims=NN):
    return lax.dot_general(a.astype(BF16), b.astype(BF16), dims, preferred_element_type=F32)


def _hdot(a, b, dims=NN):
    return lax.dot_general(a, b, dims, precision=lax.Precision.HIGHEST, preferred_element_type=F32)


def _params():
    return pltpu.CompilerParams(vmem_limit_bytes=VMEM_LIMIT)


def _sigmoid(x):
    return 1.0 / (1.0 + jnp.exp(-x))


def _rowsum(x):
    return jnp.sum(x, axis=0, keepdims=True)


def _lanemean(x):
    return jnp.mean(x, axis=-1, keepdims=True)


def _full(shape):
    nd = len(shape)
    return pl.BlockSpec(shape, lambda *_: (0,) * nd)


class _Exchange:
    def __init__(self, arrs, scatter):
        self.arrs, self.scatter, self.n, self.aliases, self.middle_at = list(arrs), scatter, len(arrs), [], 0.5
        self.out_shape = [jax.ShapeDtypeStruct((N_DEV,) + (a.shape[1:] if scatter else a.shape), a.dtype)
                          for a in self.arrs]
        n = self.n
        self.scratch = [pltpu.SemaphoreType.DMA((n, N_DEV - 1)), pltpu.SemaphoreType.DMA((n, N_DEV - 1)),
                        pltpu.SemaphoreType.DMA((n,))]

    def _copies(self, ins, outs, sems):
        send_sems, recv_sems, loc_sems = sems
        x, y, c = lax.axis_index("x"), lax.axis_index("y"), lax.axis_index("c")
        me = 4 * x + 2 * y + c
        copies = []
        for k in range(self.n):
            src_of = (lambda i, k=k: ins[k].at[i]) if self.scatter else (lambda i, k=k: ins[k])
            copies.append((pltpu.make_async_copy(src_of(me), outs[k].at[me], loc_sems.at[k]), None))
            for p in range(1, N_DEV):
                px = (1 - x) if p & 4 else x
                py = (1 - y) if p & 2 else y
                pc = (1 - c) if p & 1 else c
                peer = 4 * px + 2 * py + pc
                both = dict(send_sem=send_sems.at[k, p - 1], recv_sem=recv_sems.at[k, p - 1],
                            device_id=(px, py, pc), device_id_type=pl.DeviceIdType.MESH)
                send = pltpu.make_async_remote_copy(src_ref=src_of(peer), dst_ref=outs[k].at[me], **both)
                recv = pltpu.make_async_remote_copy(src_ref=src_of(peer), dst_ref=outs[k].at[peer], **both)
                copies.append((send, recv))
        return copies

    def start(self, ins, outs, sems):
        for first, _ in self._copies(ins, outs, sems):
            first.start()

    def middle(self, ins, outs, sems):
        pass

    def wait(self, ins, outs, sems):
        for first, recv in self._copies(ins, outs, sems):
            if recv is None:
                first.wait()
            else:
                recv.wait_recv()
                first.wait_send()


class _StagedGather:
    def __init__(self, arr):
        self.arrs, self.aliases, self.middle_at = [arr], [], 0.8
        self.out_shape = [jax.ShapeDtypeStruct((N_DEV,) + arr.shape, arr.dtype)]
        self.scratch = [pltpu.VMEM((N_DEV,) + arr.shape, arr.dtype), pltpu.SemaphoreType.DMA((7,)),
                        pltpu.SemaphoreType.DMA((7,)), pltpu.SemaphoreType.DMA((2,))]

    def _parts(self, scr):
        stage, send_sems, recv_sems, loc_sems = scr
        x, y, c = lax.axis_index("x"), lax.axis_index("y"), lax.axis_index("c")
        me, sibling = (x, y, c), (x, y, 1 - c)
        chips = [(1 - x, y), (x, 1 - y), (1 - x, 1 - y)]

        def copy(j, block, to):
            px, py, pc = block
            slot = stage.at[4 * px + 2 * py + pc]
            return pltpu.make_async_remote_copy(src_ref=slot, dst_ref=slot, send_sem=send_sems.at[j],
                                                recv_sem=recv_sems.at[j], device_id=to,
                                                device_id_type=pl.DeviceIdType.MESH)

        return stage, loc_sems, me, sibling, chips, c, copy

    def start(self, ins, outs, scr):
        stage, loc_sems, me, sibling, chips, c, copy = self._parts(scr)
        x, y, _ = me
        own = pltpu.make_async_copy(ins[0], stage.at[4 * x + 2 * y + c], loc_sems.at[0])
        own.start()
        own.wait()
        copy(0, me, sibling).start()
        for j, chip in enumerate(chips):
            copy(1 + j, me, (*chip, c)).start()

    def middle(self, ins, outs, scr):
        stage, loc_sems, me, sibling, chips, c, copy = self._parts(scr)
        for j, chip in enumerate(chips):
            copy(1 + j, (*chip, c), me).wait_recv()
            copy(4 + j, (*chip, c), sibling).start()

    def wait(self, ins, outs, scr):
        stage, loc_sems, me, sibling, chips, c, copy = self._parts(scr)
        copy(0, sibling, me).wait_recv()
        for j, chip in enumerate(chips):
            copy(4 + j, (*chip, 1 - c), me).wait_recv()
        copy(0, me, sibling).wait_send()
        for j, chip in enumerate(chips):
            copy(1 + j, me, (*chip, c)).wait_send()
            copy(4 + j, (*chip, c), sibling).wait_send()
        whole = pltpu.make_async_copy(stage, outs[0], loc_sems.at[1])
        whole.start()
        whole.wait()


class _StagedScatter:
    def __init__(self, slabs, middle_at=0.2):
        _, r, c = slabs.shape
        self.arrs, self.aliases, self.middle_at = [slabs], [], middle_at
        self.out_shape = [jax.ShapeDtypeStruct((4, r, c), slabs.dtype)]
        self.scratch = [pltpu.VMEM((N_DEV, r, c), slabs.dtype), pltpu.VMEM((4, r, c), slabs.dtype),
                        pltpu.VMEM((3, r, c), slabs.dtype), pltpu.SemaphoreType.DMA((4,)), pltpu.SemaphoreType.DMA((4,)),
                        pltpu.SemaphoreType.DMA((3,)), pltpu.SemaphoreType.DMA((3,)), pltpu.SemaphoreType.DMA((4,))]

    def _parts(self, scr):
        stage, from_sib, from_chips, sib_send, sib_recv, ici_send, ici_recv, loc_sems = scr
        x, y, c = lax.axis_index("x"), lax.axis_index("y"), lax.axis_index("c")
        chips = [(1 - x, y), (x, 1 - y), (1 - x, 1 - y)]

        def to_sibling(j):
            return pltpu.make_async_remote_copy(src_ref=stage.at[2 * j + 1 - c], dst_ref=from_sib.at[j],
                                                send_sem=sib_send.at[j], recv_sem=sib_recv.at[j],
                                                device_id=(x, y, 1 - c), device_id_type=pl.DeviceIdType.MESH)

        def to_chip(k):
            px, py = chips[k]
            return pltpu.make_async_remote_copy(src_ref=stage.at[4 * px + 2 * py + c], dst_ref=from_chips.at[k],
                                                send_sem=ici_send.at[k], recv_sem=ici_recv.at[k],
                                                device_id=(px, py, c), device_id_type=pl.DeviceIdType.MESH)

        return stage, from_sib, from_chips, loc_sems, (x, y, c), chips, to_sibling, to_chip

    def start(self, ins, outs, scr):
        stage, _, _, loc_sems, _, _, to_sibling, _ = self._parts(scr)
        load = pltpu.make_async_copy(ins[0], stage, loc_sems.at[0])
        load.start()
        load.wait()
        for j in range(4):
            to_sibling(j).start()

    def middle(self, ins, outs, scr):
        stage, from_sib, _, _, (x, y, c), _, to_sibling, to_chip = self._parts(scr)
        for j in range(4):
            to_sibling(j).wait_recv()
            mine = stage.at[2 * j + c]
            mine[...] = (mine[...].astype(F32) + from_sib[j].astype(F32)).astype(mine.dtype)
        for k in range(3):
            to_chip(k).start()

    def wait(self, ins, outs, scr):
        stage, _, from_chips, loc_sems, (x, y, c), chips, to_sibling, to_chip = self._parts(scr)
        writes = [pltpu.make_async_copy(stage.at[4 * x + 2 * y + c], outs[0].at[2 * x + y], loc_sems.at[0])]
        for k, (px, py) in enumerate(chips):
            to_chip(k).wait_recv()
            writes.append(pltpu.make_async_copy(from_chips.at[k], outs[0].at[2 * px + py], loc_sems.at[1 + k]))
        for w in writes:
            w.start()
        for j in range(4):
            to_sibling(j).wait_send()
        for k in range(3):
            to_chip(k).wait_send()
        for w in writes:
            w.wait()


def _call(body, name, args, out_shape, grid=(), in_specs=(), out_specs=(), scratch_shapes=(), exchange=None):
    if exchange is None:
        return pl.pallas_call(body, name=name, grid=grid, in_specs=list(in_specs), out_specs=list(out_specs),
                              out_shape=list(out_shape), scratch_shapes=list(scratch_shapes),
                              compiler_params=_params())(*args), None
    exs = list(exchange) if isinstance(exchange, (list, tuple)) else [exchange]
    ni, no, ns = len(args), len(out_shape), len(scratch_shapes)
    nxi, nxo = sum(len(e.arrs) for e in exs), sum(len(e.out_shape) for e in exs)
    steps = int(np.prod(grid))
    mid_step = lambda e: min(max(int(steps * e.middle_at), 1), steps - 1)
    aliases, iat, oat = {}, ni, no
    for e in exs:
        for src, dst in e.aliases:
            aliases[iat + src] = oat + dst
        iat, oat = iat + len(e.arrs), oat + len(e.out_shape)

    def wrapped(*refs):
        a, xi = refs[:ni], refs[ni:ni + nxi]
        o, xo = refs[ni + nxi:ni + nxi + no], refs[ni + nxi + no:ni + nxi + no + nxo]
        s, xs = refs[ni + nxi + no + nxo:ni + nxi + no + nxo + ns], refs[ni + nxi + no + nxo + ns:]
        parts, iat, oat, sat = [], 0, 0, 0
        for e in exs:
            parts.append((e, xi[iat:iat + len(e.arrs)], xo[oat:oat + len(e.out_shape)], xs[sat:sat + len(e.scratch)]))
            iat, oat, sat = iat + len(e.arrs), oat + len(e.out_shape), sat + len(e.scratch)
        step = 0
        for d, g in enumerate(grid):
            step = step * g + pl.program_id(d)

        @pl.when(step == 0)
        def _():
            for e, ins, outs, sems in parts:
                e.start(ins, outs, sems)

        for at_step in sorted({mid_step(e) for e in exs}):
            @pl.when(step == at_step)
            def _():
                for e, ins, outs, sems in parts:
                    if mid_step(e) == at_step:
                        e.middle(ins, outs, sems)

        body(*a, *o, *s)

        @pl.when(step == steps - 1)
        def _():
            for e, ins, outs, sems in parts:
                e.wait(ins, outs, sems)

    hbm = pl.BlockSpec(memory_space=pltpu.HBM)
    res = pl.pallas_call(
        wrapped, name=name, grid=grid, in_specs=list(in_specs) + [hbm] * nxi, out_specs=list(out_specs) + [hbm] * nxo,
        out_shape=list(out_shape) + [o_ for e in exs for o_ in e.out_shape],
        scratch_shapes=list(scratch_shapes) + [s_ for e in exs for s_ in e.scratch],
        input_output_aliases=aliases, compiler_params=_params())(*args, *[a_ for e in exs for a_ in e.arrs])
    return res[:no], res[no:]


def _gather_two_level(arrs, name):
    n = len(arrs)
    out_shape = [jax.ShapeDtypeStruct((N_DEV,) + a.shape, a.dtype) for a in arrs]

    def body(*refs):
        ins, outs = refs[:n], refs[n:2 * n]
        send_sems, recv_sems, loc_sems = refs[2 * n:]
        x, y, c = lax.axis_index("x"), lax.axis_index("y"), lax.axis_index("c")
        me, sibling = (x, y, c), (x, y, 1 - c)
        chips = [(1 - x, y), (x, 1 - y), (1 - x, 1 - y)]

        def copy(k, j, block, to, src=None):
            px, py, pc = block
            dst = outs[k].at[4 * px + 2 * py + pc]
            return pltpu.make_async_remote_copy(src_ref=dst if src is None else src, dst_ref=dst,
                                                send_sem=send_sems.at[k, j], recv_sem=recv_sems.at[k, j],
                                                device_id=to, device_id_type=pl.DeviceIdType.MESH)

        mine = [pltpu.make_async_copy(ins[k], outs[k].at[4 * x + 2 * y + c], loc_sems.at[k]) for k in range(n)]
        first = []
        for k in range(n):
            mine[k].start()
            first.append(copy(k, 0, me, sibling, src=ins[k]))
            first += [copy(k, 1 + j, me, (*chip, c), src=ins[k]) for j, chip in enumerate(chips)]
        for cp in first:
            cp.start()
        passed = []
        for j, chip in enumerate(chips):
            for k in range(n):
                copy(k, 1 + j, (*chip, c), me).wait_recv()
                passed.append(copy(k, 4 + j, (*chip, c), sibling))
                passed[-1].start()
        for k in range(n):
            copy(k, 0, sibling, me).wait_recv()
            for j, chip in enumerate(chips):
                copy(k, 4 + j, (*chip, 1 - c), me).wait_recv()
        for cp in first + passed:
            cp.wait_send()
        for cp in mine:
            cp.wait()

    vmem = pl.BlockSpec(memory_space=pltpu.VMEM)
    return pl.pallas_call(body, name=name, out_shape=out_shape, in_specs=[vmem] * n, out_specs=[vmem] * n,
                          scratch_shapes=[pltpu.SemaphoreType.DMA((n, 7)), pltpu.SemaphoreType.DMA((n, 7)),
                                          pltpu.SemaphoreType.DMA((n,))], compiler_params=_params())(*arrs)


def _exchange(arrs, scatter, name):
    ex = _Exchange(arrs, scatter)

    def body(*refs):
        ins, outs, sems = refs[:ex.n], refs[ex.n:2 * ex.n], refs[2 * ex.n:]
        ex.start(ins, outs, sems)
        ex.wait(ins, outs, sems)

    hbm = pl.BlockSpec(memory_space=pltpu.HBM)
    return pl.pallas_call(body, name=name, out_shape=ex.out_shape, in_specs=[hbm] * ex.n, out_specs=[hbm] * ex.n,
                          scratch_shapes=ex.scratch)(*ex.arrs)


def _matmul(a, b, mode, name, out_dtype=F32, tm=512, tn=1024, tk=1024, a_fn=None, b_fn=None, extras=(),
            out_slabs=None, exchange=None):
    assert not (a_fn and b_fn) and not (b_fn and mode == "NT")
    if mode == "NN":
        (M, K), N = a.shape, b.shape[1]
    elif mode == "NT":
        (M, K), N = a.shape, b.shape[0]
    else:
        (K, M), N = a.shape, b.shape[1]
    if out_slabs:
        tn = N // out_slabs
    tm, tn, tk = min(tm, M), min(tn, N), min(tk, K)
    assert M % tm == 0 and N % tn == 0 and K % tk == 0, (name, M, N, K)
    nk = K // tk
    dims = {"NN": NN, "NT": NT, "TN": TN}[mode]
    ne = len(extras)

    def body(a_ref, b_ref, *rest):
        e_refs, o_ref, acc_ref = rest[:ne], rest[ne], rest[ne + 1]
        k = pl.program_id(2)

        @pl.when(k == 0)
        def _():
            acc_ref[...] = jnp.zeros_like(acc_ref)

        at, bt = a_ref[...], b_ref[...]
        if a_fn is not None:
            at = a_fn(at.astype(F32), *[e[...] for e in e_refs])
        if b_fn is not None:
            bt = b_fn(bt.astype(F32), *[e[...] for e in e_refs])
        acc_ref[...] += _bdot(at, bt, dims)

        @pl.when(k == nk - 1)
        def _():
            o_ref[...] = acc_ref[...].astype(out_dtype)

    if mode == "TN":
        a_spec = pl.BlockSpec((tk, tm), lambda i, j, k: (k, i))
        e_spec = pl.BlockSpec((1, tm), lambda i, j, k: (0, i))
    else:
        a_spec = pl.BlockSpec((tm, tk), lambda i, j, k: (i, k))
        e_spec = pl.BlockSpec((1, tk), lambda i, j, k: (0, k))
    if mode == "NT":
        b_spec = pl.BlockSpec((tn, tk), lambda i, j, k: (j, k))
    else:
        b_spec = pl.BlockSpec((tk, tn), lambda i, j, k: (k, j))
    if b_fn is not None:
        e_spec = pl.BlockSpec((1, tn), lambda i, j, k: (0, j))
    if out_slabs:
        o_shape = jax.ShapeDtypeStruct((out_slabs, M, tn), out_dtype)
        o_spec = pl.BlockSpec((None, tm, tn), lambda i, j, k: (j, i, 0))
    else:
        o_shape = jax.ShapeDtypeStruct((M, N), out_dtype)
        o_spec = pl.BlockSpec((tm, tn), lambda i, j, k: (i, j))
    (out,), got = _call(body, name, (a, b, *extras), [o_shape], grid=(M // tm, N // tn, nk),
                        in_specs=[a_spec, b_spec] + [e_spec] * ne, out_specs=[o_spec],
                        scratch_shapes=[pltpu.VMEM((tm, tn), F32)], exchange=exchange)
    return out if exchange is None else (out, got)


def _modulate(x, sc, sh):
    return x * (1.0 + sc) + sh


def _square(x):
    return x * x


def _mod_part(c_all, w_ada_s, b_s):
    def body(c_ref, w_ref, b_ref, mod_ref, cond_ref):
        cv = c_ref[...]
        cond = cv * _sigmoid(cv)
        cond_ref[...] = cond
        mod_ref[...] = _bdot(cond, w_ref[...]) + b_ref[...]

    return pl.pallas_call(
        body, name="mod_part",
        out_shape=[jax.ShapeDtypeStruct((N_DEV, w_ada_s.shape[1]), F32), jax.ShapeDtypeStruct(c_all.shape, F32)],
        compiler_params=_params(),
    )(c_all, w_ada_s, b_s)


def _rms_fwd(x, w):
    rs = lax.rsqrt(_lanemean(x * x) + RMS_EPS)
    return x * rs * w, rs


def _rms_bwd(x, rs, w, dy):
    xhat = x * rs
    dxh = dy * w
    return rs * (dxh - xhat * _lanemean(dxh * xhat)), dy * xhat


def _mla_pre(z, pos_col, invf, m_one, m_rot, wq_ext, wkv_ext, qnw, kvnw, exchange=None):
    T = z.shape[0]
    tm = min(ROW_TILE, T)

    def body(z_ref, pos_ref, invf_ref, mone_ref, mrot_ref, wq_ref, wkv_ref, qnw_ref, kvnw_ref,
             q_ref, k_ref, v_ref, c1_ref, s1_ref, cqn_ref, ckvn_ref):
        ang = pos_ref[...].astype(F32) * invf_ref[...]
        c1 = mone_ref[...] + mrot_ref[...] * jnp.cos(ang)
        s1 = mrot_ref[...] * jnp.sin(ang)
        c1_ref[...] = c1
        s1_ref[...] = s1
        cqn, _ = _rms_fwd(z_ref[:, 0:256], qnw_ref[...])
        ckvn, _ = _rms_fwd(z_ref[:, 256:512], kvnw_ref[...])
        cqn_ref[...] = cqn.astype(BF16)
        ckvn_ref[...] = ckvn.astype(BF16)
        qe = _bdot(cqn, wq_ref[...], NT)
        kve = _bdot(ckvn, wkv_ref[...])
        k_rope = z_ref[:, 512:768] * c1 + z_ref[:, 768:1024] * s1
        for h in range(HEADS):
            q_ref[h] = ((qe[:, 256 * h:256 * h + 256] * c1 + qe[:, 1024 + 256 * h:1280 + 256 * h] * s1)
                        * Q_PRESCALE).astype(BF16)
            k_ref[h] = (kve[:, 256 * h:256 * h + 256] + k_rope).astype(BF16)
            v_ref[h] = kve[:, 1024 + 128 * h:1152 + 128 * h].astype(BF16)

    row = lambda i: (i, 0)
    head = lambda i: (0, i, 0)
    return _call(
        body, "mla_pre", (z, pos_col, invf, m_one, m_rot, wq_ext, wkv_ext, qnw, kvnw), grid=(T // tm,),
        in_specs=[pl.BlockSpec((tm, 1024), lambda i: (i, 2)), pl.BlockSpec((tm, 1), row),
                  _full((1, 256)), _full((1, 256)), _full((1, 256)), _full(wq_ext.shape), _full(wkv_ext.shape),
                  _full((1, 256)), _full((1, 256))],
        out_specs=[pl.BlockSpec((HEADS, tm, QK_PAD), head), pl.BlockSpec((HEADS, tm, QK_PAD), head),
                   pl.BlockSpec((HEADS, tm, HEAD_DIM), head), pl.BlockSpec((tm, 256), row), pl.BlockSpec((tm, 256), row),
                   pl.BlockSpec((tm, 256), row), pl.BlockSpec((tm, 256), row)],
        out_shape=[jax.ShapeDtypeStruct((HEADS, T, QK_PAD), BF16), jax.ShapeDtypeStruct((HEADS, T, QK_PAD), BF16),
                   jax.ShapeDtypeStruct((HEADS, T, HEAD_DIM), BF16), jax.ShapeDtypeStruct((T, 256), F32),
                   jax.ShapeDtypeStruct((T, 256), F32), jax.ShapeDtypeStruct((T, 256), BF16),
                   jax.ShapeDtypeStruct((T, 256), BF16)], exchange=exchange)


def _mla_bwd(dq, dk, dv, z, c1, s1, wq_ext, wkv_ext, qnw, kvnw):
    T = z.shape[0]
    tm = min(ROW_TILE, T)

    def body(dq_ref, dk_ref, dv_ref, z_ref, c1_ref, s1_ref, wq_ref, wkv_ref, qnw_ref, kvnw_ref,
             dz_ref, dqe_ref, dkve_ref, dnw_ref):
        @pl.when(pl.program_id(0) == 0)
        def _():
            dnw_ref[...] = jnp.zeros_like(dnw_ref)

        c1, s1 = c1_ref[...], s1_ref[...]
        dkpe = jnp.zeros((tm, QK_PAD), F32)
        for h in range(HEADS):
            dqh, dkh = dq_ref[h].astype(F32) * ATT_SCALE, dk_ref[h]
            dqe_ref[:, 256 * h:256 * h + 256] = (dqh * c1).astype(BF16)
            dqe_ref[:, 1024 + 256 * h:1280 + 256 * h] = (dqh * s1).astype(BF16)
            dkve_ref[:, 256 * h:256 * h + 256] = dkh
            dkve_ref[:, 1024 + 128 * h:1152 + 128 * h] = dv_ref[h]
            dkpe = dkpe + dkh.astype(F32)
        dcqn = _dot(dqe_ref[...], wq_ref[...])
        dckvn = _dot(dkve_ref[...], wkv_ref[...], NT)
        cq, ckv = z_ref[:, 0:256], z_ref[:, 256:512]
        _, rsq = _rms_fwd(cq, qnw_ref[...])
        _, rskv = _rms_fwd(ckv, kvnw_ref[...])
        dcq, wq_rows = _rms_bwd(cq, rsq, qnw_ref[...], dcqn)
        dckv, wkv_rows = _rms_bwd(ckv, rskv, kvnw_ref[...], dckvn)
        dnw_ref[:, 0:256] += _rowsum(wq_rows)
        dnw_ref[:, 256:512] += _rowsum(wkv_rows)
        dz_ref[:, 0:256] = dcq.astype(BF16)
        dz_ref[:, 256:512] = dckv.astype(BF16)
        dz_ref[:, 512:768] = (dkpe * c1).astype(BF16)
        dz_ref[:, 768:1024] = (dkpe * s1).astype(BF16)

    row = lambda i: (i, 0)
    head = lambda i: (0, i, 0)
    return pl.pallas_call(
        body, name="mla_bwd", grid=(T // tm,),
        in_specs=[pl.BlockSpec((HEADS, tm, QK_PAD), head), pl.BlockSpec((HEADS, tm, QK_PAD), head),
                  pl.BlockSpec((HEADS, tm, HEAD_DIM), head), pl.BlockSpec((tm, 1024), lambda i: (i, 2)),
                  pl.BlockSpec((tm, 256), row), pl.BlockSpec((tm, 256), row), _full(wq_ext.shape), _full(wkv_ext.shape),
                  _full((1, 256)), _full((1, 256))],
        out_specs=[pl.BlockSpec((tm, 1024), row), pl.BlockSpec((tm, 2048), row), pl.BlockSpec((tm, 1536), row),
                   _full((1, 512))],
        out_shape=[jax.ShapeDtypeStruct((T, 1024), BF16), jax.ShapeDtypeStruct((T, 2048), BF16),
                   jax.ShapeDtypeStruct((T, 1536), BF16), jax.ShapeDtypeStruct((1, 512), F32)],
        compiler_params=_params(),
    )(dq, dk, dv, z, c1, s1, wq_ext, wkv_ext, qnw, kvnw)


_HEAD_LANES = [slice(HEAD_DIM * h, HEAD_DIM * (h + 1)) for h in range(HEADS)]


def _lower_bound(lbraw_ref):
    a0, a1 = lbraw_ref[0:1, :], lbraw_ref[1:2, :]
    mx = jnp.maximum(a0, a1)
    e0, e1 = jnp.exp(a0 - mx), jnp.exp(a1 - mx)
    return e0 / (e0 + e1)


def _tri(lower):
    r = lax.broadcasted_iota(jnp.int32, (CHUNK, CHUNK), 0)
    c = lax.broadcasted_iota(jnp.int32, (CHUNK, CHUNK), 1)
    return (r >= c) if lower else (r <= c)


def _hgrn_gates(q, f, lb, tri_lo):
    sg = _sigmoid(f)
    forget = lb + (1.0 - lb) * sg
    k = 1.0 - forget
    b = _hdot(tri_lo.astype(F32), jnp.log(forget))
    b_ref, b_last = b[CHUNK // 2 - 1:CHUNK // 2, :], b[CHUNK - 1:CHUNK, :]
    e1, e2, e3, e4 = jnp.exp(b - b_ref), jnp.exp(b_ref - b), jnp.exp(b_last - b), jnp.exp(b)
    return dict(sg=sg, forget=forget, k=k, e1=e1, e2=e2, e3=e3, e4=e4, qa=q * e1, ka=k * e2, kl=k * e3, qb=q * e4,
                decay=jnp.exp(b_last))


def _hgrn_fwd(z, lbraw, nw, exchange=None):
    T = z.shape[0]
    G = min(HGRN_GROUP, T // CHUNK)
    rows = G * CHUNK
    n_chunks = T // CHUNK

    def body(q_ref, f_ref, i_ref, g_ref, lbraw_ref, nw_ref, oraw_ref, og_ref, sp_ref, st_ref):
        @pl.when(pl.program_id(0) == 0)
        def _():
            st_ref[...] = jnp.zeros_like(st_ref)

        lb_all = _lower_bound(lbraw_ref)
        tri_lo = _tri(True)

        def chunk(cc, carry):
            rs = pl.ds(pl.multiple_of(cc * CHUNK, CHUNK), CHUNK)
            t = _hgrn_gates(q_ref[rs, :], f_ref[rs, :], lb_all, tri_lo)
            v, gate = i_ref[rs, :], g_ref[rs, :]
            st = [st_ref[h] for h in range(HEADS)]
            a = [jnp.where(tri_lo, _bdot(t["qa"][:, s], t["ka"][:, s], NT), 0.0) for s in _HEAD_LANES]
            kv = [_bdot(v[:, s], t["kl"][:, s], TN) for s in _HEAD_LANES]
            o = [_bdot(a[h], v[:, s]) + _bdot(t["qb"][:, s], st[h], NT) for h, s in enumerate(_HEAD_LANES)]
            for h, s in enumerate(_HEAD_LANES):
                sp_ref[cc, h] = st[h]
                st_ref[h] = st[h] * t["decay"][:, s] + kv[h]
            oraw_ref[rs, :] = jnp.concatenate(o, axis=1)
            on = jnp.concatenate([_rms_fwd(o[h], nw_ref[:, s])[0] for h, s in enumerate(_HEAD_LANES)], axis=1)
            og_ref[rs, :] = (on * (gate * _sigmoid(gate))).astype(BF16)
            return carry

        lax.fori_loop(0, G, chunk, 0, unroll=4)

    col = lambda j: pl.BlockSpec((rows, 512), lambda r, j=j: (r, j))
    return _call(
        body, "hgrn_fwd", (z, z, z, z, lbraw, nw), grid=(T // rows,),
        in_specs=[col(0), col(1), col(2), col(3), _full((2, 512)), _full((1, 512))],
        out_specs=[col(0), col(0), pl.BlockSpec((G, HEADS, HEAD_DIM, HEAD_DIM), lambda r: (r, 0, 0, 0))],
        out_shape=[jax.ShapeDtypeStruct((T, 512), F32), jax.ShapeDtypeStruct((T, 512), BF16),
                   jax.ShapeDtypeStruct((n_chunks, HEADS, HEAD_DIM, HEAD_DIM), F32)],
        scratch_shapes=[pltpu.VMEM((HEADS, HEAD_DIM, HEAD_DIM), F32)], exchange=exchange)


def _hgrn_bwd(dmixcat, z, oraw, sprev, lbraw, nw, exchange=None):
    T = z.shape[0]
    G = min(HGRN_GROUP, T // CHUNK)
    rows = G * CHUNK
    ng = T // rows

    def body(dog_ref, q_ref, f_ref, i_ref, g_ref, oraw_ref, sp_ref, lbraw_ref, nw_ref,
             dz_ref, dsmall_ref, dst_ref):
        @pl.when(pl.program_id(0) == 0)
        def _():
            dst_ref[...] = jnp.zeros_like(dst_ref)
            dsmall_ref[...] = jnp.zeros_like(dsmall_ref)

        lb_all = _lower_bound(lbraw_ref)
        tri_lo, tri_up = _tri(True), _tri(False)
        rowid = lax.broadcasted_iota(jnp.int32, (CHUNK, HEADS * HEAD_DIM), 0)

        def chunk(it, carry):
            cc = G - 1 - it
            rs = pl.ds(pl.multiple_of(cc * CHUNK, CHUNK), CHUNK)
            heads = list(enumerate(_HEAD_LANES))
            cat = lambda parts: jnp.concatenate(parts, axis=1)
            per_head_mean = lambda x: cat([jnp.broadcast_to(_lanemean(x[:, s]), (CHUNK, HEAD_DIM)) for s in _HEAD_LANES])
            t = _hgrn_gates(q_ref[rs, :], f_ref[rs, :], lb_all, tri_lo)
            v, gate, o, dog, nw_all = i_ref[rs, :], g_ref[rs, :], oraw_ref[rs, :], dog_ref[rs, :], nw_ref[...]
            rs_o = lax.rsqrt(per_head_mean(o * o) + RMS_EPS)
            xhat = o * rs_o
            sgg = _sigmoid(gate)
            d_on = dog * (gate * sgg)
            dz_ref[rs, 1536:2048] = (dog * (xhat * nw_all) * (sgg * (1.0 + gate * (1.0 - sgg)))).astype(BF16)
            dxh = d_on * nw_all
            do = rs_o * (dxh - xhat * per_head_mean(dxh * xhat))
            dsmall_ref[:, 512:1024] += _rowsum(d_on * xhat)
            st = [sp_ref[cc, h] for h in range(HEADS)]
            dst = [dst_ref[h] for h in range(HEADS)]
            a = [jnp.where(tri_lo, _bdot(t["qa"][:, s], t["ka"][:, s], NT), 0.0) for s in _HEAD_LANES]
            da = [jnp.where(tri_lo, _bdot(do[:, s], v[:, s], NT), 0.0) for s in _HEAD_LANES]
            dqb = cat([_bdot(do[:, s], st[h]) for h, s in heads])
            dkl = cat([_bdot(v[:, s], dst[h]) for h, s in heads])
            dv_ = cat([_bdot(t["kl"][:, s], dst[h], NT) + _bdot(a[h], do[:, s], TN) for h, s in heads])
            dqa = cat([_bdot(da[h], t["ka"][:, s]) for h, s in heads])
            dka = cat([_bdot(da[h], t["qa"][:, s], TN) for h, s in heads])
            ddecay = cat([_rowsum(dst[h] * st[h]) for h in range(HEADS)])
            for h, s in heads:
                dst_ref[h] = dst[h] * t["decay"][:, s] + _bdot(do[:, s], t["qb"][:, s], TN)
            pa, pk, pb, pl_ = dqa * t["qa"], dka * t["ka"], dqb * t["qb"], dkl * t["kl"]
            db = pa - pk + pb - pl_
            db = db + jnp.where(rowid == CHUNK // 2 - 1, _rowsum(pk - pa), 0.0)
            db = db + jnp.where(rowid == CHUNK - 1, _rowsum(pl_) + ddecay * t["decay"], 0.0)
            dlogf = _hdot(tri_up.astype(F32), db)
            dforget = dlogf / t["forget"] - (dka * t["e2"] + dkl * t["e3"])
            sg = t["sg"]
            dz_ref[rs, 0:512] = (dqa * t["e1"] + dqb * t["e4"]).astype(BF16)
            dz_ref[rs, 512:1024] = (dforget * (1.0 - lb_all) * sg * (1.0 - sg)).astype(BF16)
            dz_ref[rs, 1024:1536] = dv_.astype(BF16)
            dsmall_ref[:, 0:512] += _rowsum(dforget * (1.0 - sg))
            return carry

        lax.fori_loop(0, G, chunk, 0, unroll=4)

    col = lambda j: pl.BlockSpec((rows, 512), lambda r, j=j: (ng - 1 - r, j))
    return _call(
        body, "hgrn_bwd", (dmixcat, z, z, z, z, oraw, sprev, lbraw, nw), grid=(ng,),
        in_specs=[col(0), col(0), col(1), col(2), col(3), col(0),
                  pl.BlockSpec((G, HEADS, HEAD_DIM, HEAD_DIM), lambda r: (ng - 1 - r, 0, 0, 0)),
                  _full((2, 512)), _full((1, 512))],
        out_specs=[pl.BlockSpec((rows, 2048), lambda r: (ng - 1 - r, 0)), _full((1, 1024))],
        out_shape=[jax.ShapeDtypeStruct((T, 2048), BF16), jax.ShapeDtypeStruct((1, 1024), F32)],
        scratch_shapes=[pltpu.VMEM((HEADS, HEAD_DIM, HEAD_DIM), F32)], exchange=exchange)


def _diag_mask(t):
    r = lax.broadcasted_iota(jnp.int32, (t, t), 0)
    c = lax.broadcasted_iota(jnp.int32, (t, t), 1)
    return r >= c


def _attn_fwd(q, k, v, exchange=None):
    _, T, _ = q.shape
    t = min(ATT_TILE, T)

    def body(q_ref, k_ref, v_ref, o_ref, lse_ref):
        i = pl.program_id(1)
        qb = q_ref[...]

        rows = lambda j: pl.ds(pl.multiple_of(j * t, t), t)

        def logits(j, masked):
            s = _dot(qb, k_ref[rows(j), :], NT)
            return jnp.where(_diag_mask(t), s, NEG_BIG) if masked else s

        def absorb(s, j, carry):
            m, l, acc = carry
            mn = jnp.maximum(m, jnp.max(s, axis=-1, keepdims=True))
            p = jnp.exp2(s - mn)
            al = jnp.exp2(m - mn)
            return mn, al * l + jnp.sum(p, axis=-1, keepdims=True), al * acc + _dot(p.astype(BF16), v_ref[rows(j), :])

        def pair(j0, carry, last_masked):
            s0, s1 = logits(j0, False), logits(j0 + 1, last_masked)
            return absorb(s1, j0 + 1, absorb(s0, j0, carry))

        init = (jnp.full((t, 1), NEG_BIG, F32), jnp.zeros((t, 1), F32), jnp.zeros((t, HEAD_DIM), F32))
        carry = lax.fori_loop(0, i // 2, lambda jj, c: pair(2 * jj, c, False), init)
        m, l, acc = lax.cond(i % 2 == 1, lambda c: pair(i - 1, c, True),
                             lambda c: absorb(logits(i, True), i, c), carry)
        o_ref[...] = acc / l
        lse_ref[...] = jnp.broadcast_to(m + jnp.log2(l), (t, HEAD_DIM))

    return _call(
        body, "attn_fwd", (q, k, v), grid=(HEADS, T // t),
        in_specs=[pl.BlockSpec((None, t, QK_PAD), lambda h, i: (h, i, 0)),
                  pl.BlockSpec((None, T, QK_PAD), lambda h, i: (h, 0, 0)),
                  pl.BlockSpec((None, T, HEAD_DIM), lambda h, i: (h, 0, 0))],
        out_specs=[pl.BlockSpec((t, HEAD_DIM), lambda h, i: (i, h)),
                   pl.BlockSpec((None, t, HEAD_DIM), lambda h, i: (h, i, 0))],
        out_shape=[jax.ShapeDtypeStruct((T, HEADS * HEAD_DIM), F32), jax.ShapeDtypeStruct((HEADS, T, HEAD_DIM), F32)],
        exchange=exchange)


def _attn_bwd(q, k, v, dmixcat, o, lse, exchange=None):
    _, T, _ = q.shape
    t = min(ATT_TILE, T)
    nq = T // t

    def body(q_ref, k_ref, v_ref, do_ref, o_ref, lse_ref, dq_ref, dk_ref, dv_ref, delta_ref, dq_acc):
        j = pl.program_id(1)

        @pl.when(j == 0)
        def _():
            dq_acc[...] = jnp.zeros_like(dq_acc)

            def fill(i, carry):
                rs = pl.ds(pl.multiple_of(i * t, t), t)
                delta_ref[rs, :] = jnp.broadcast_to(
                    jnp.sum(do_ref[rs, :] * o_ref[rs, :], axis=-1, keepdims=True), (t, HEAD_DIM))
                return carry

            lax.fori_loop(0, nq, fill, 0)

        kb, vb = k_ref[...], v_ref[...]

        def steps(blocks, carry):
            dk, dv = carry
            rs = [pl.ds(pl.multiple_of(i * t, t), t) for i, _ in blocks]
            qb = [q_ref[r, :] for r in rs]
            dob = [do_ref[r, :].astype(BF16) for r in rs]
            s = [_dot(b, kb, NT) for b in qb]
            dp = [_dot(b, vb, NT) for b in dob]
            for n, (_, masked) in enumerate(blocks):
                p = jnp.exp2(s[n] - lse_ref[rs[n], 0:1])
                if masked:
                    p = jnp.where(_diag_mask(t), p, 0.0)
                ds = (p * (dp[n] - delta_ref[rs[n], 0:1])).astype(BF16)
                dq_acc[rs[n], :] += _dot(ds, kb)
                dk = dk + _dot(ds, qb[n], TN)
                dv = dv + _dot(p.astype(BF16), dob[n], TN)
            return dk, dv

        zero = (jnp.zeros((t, QK_PAD), F32), jnp.zeros((t, HEAD_DIM), F32))
        rest = nq - 1 - j
        carry = lax.cond(rest % 2 == 1, lambda c: steps([(j, True), (j + 1, False)], c),
                         lambda c: steps([(j, True)], c), zero)
        first = j + 1 + rest % 2
        dk, dv = lax.fori_loop(0, rest // 2, lambda n, c: steps([(first + 2 * n, False), (first + 2 * n + 1, False)], c),
                               carry)
        dk_ref[...] = (dk * LN2).astype(BF16)
        dv_ref[...] = dv.astype(BF16)

        @pl.when(j == nq - 1)
        def _():
            dq_ref[...] = dq_acc[...].astype(BF16)

    return _call(
        body, "attn_bwd", (q, k, v, dmixcat, o, lse), grid=(HEADS, nq),
        in_specs=[pl.BlockSpec((None, T, QK_PAD), lambda h, j: (h, 0, 0)),
                  pl.BlockSpec((None, t, QK_PAD), lambda h, j: (h, j, 0)),
                  pl.BlockSpec((None, t, HEAD_DIM), lambda h, j: (h, j, 0)),
                  pl.BlockSpec((T, HEAD_DIM), lambda h, j: (0, HEADS + h)),
                  pl.BlockSpec((T, HEAD_DIM), lambda h, j: (0, h)),
                  pl.BlockSpec((None, T, HEAD_DIM), lambda h, j: (h, 0, 0))],
        out_specs=[pl.BlockSpec((None, T, QK_PAD), lambda h, j: (h, 0, 0)),
                   pl.BlockSpec((None, t, QK_PAD), lambda h, j: (h, j, 0)),
                   pl.BlockSpec((None, t, HEAD_DIM), lambda h, j: (h, j, 0))],
        out_shape=[jax.ShapeDtypeStruct((HEADS, T, QK_PAD), BF16), jax.ShapeDtypeStruct((HEADS, T, QK_PAD), BF16),
                   jax.ShapeDtypeStruct((HEADS, T, HEAD_DIM), BF16)],
        scratch_shapes=[pltpu.VMEM((T, HEAD_DIM), F32), pltpu.VMEM((T, QK_PAD), F32)], exchange=exchange)


def _ln_fwd(r):
    mu = _lanemean(r)
    xc = r - mu
    rstd = lax.rsqrt(_lanemean(xc * xc) + LN_EPS)
    return xc * rstd, rstd


def _ln_bwd(dxh, xhat, rstd):
    return rstd * (dxh - _lanemean(dxh) - xhat * _lanemean(dxh * xhat))


def _mix_ln1(o_hg, o_mla, w_out, x, g_a, ln1_g, ln1_b, sc_m, sh_m, exchange=None):
    T = x.shape[0]
    tm = min(ROW_TILE, T)
    half = o_hg.shape[1]

    def body(hg_ref, mla_ref, w_ref, x_ref, ga_ref, g_ref, b_ref, sc_ref, sh_ref, mix_ref, xhat_ref, rstd_ref, u2_ref):
        mix = _dot(hg_ref[...], w_ref[0:half, :]) + _bdot(mla_ref[...], w_ref[half:, :])
        mix_ref[...] = mix
        xhat, rstd = _ln_fwd(ALPHA * x_ref[...] + (1.0 + ga_ref[...]) * mix)
        xhat_ref[...] = xhat
        rstd_ref[...] = jnp.broadcast_to(rstd, (tm, 128))
        u2_ref[...] = _modulate(xhat * g_ref[...] + b_ref[...], sc_ref[...], sh_ref[...]).astype(BF16)

    row = pl.BlockSpec((tm, D_MODEL), lambda i: (i, 0))
    vec = _full((1, D_MODEL))
    halfrow = pl.BlockSpec((tm, half), lambda i: (i, 0))
    return _call(
        body, "mix_ln1", (o_hg, o_mla, w_out, x, g_a, ln1_g, ln1_b, sc_m, sh_m), grid=(T // tm,),
        in_specs=[halfrow, halfrow, _full(w_out.shape), row, vec, vec, vec, vec, vec],
        out_specs=[row, row, pl.BlockSpec((tm, 128), lambda i: (i, 0)), row],
        out_shape=[jax.ShapeDtypeStruct((T, D_MODEL), F32), jax.ShapeDtypeStruct((T, D_MODEL), F32),
                   jax.ShapeDtypeStruct((T, 128), F32), jax.ShapeDtypeStruct((T, D_MODEL), BF16)],
        exchange=exchange)


def _mlp_fwd(u2, w1, w2, xhat1, ln1_g, ln1_b, g_m, ln2_g, ln2_b, target):
    T = u2.shape[0]
    tf = w1.shape[-1]
    nf = N_DEV // MLP_SLABS
    tm = min(ROW_TILE, T)

    def body(u2_ref, w1_ref, w2_ref, xhat_ref, g1_ref, b1_ref, gm_ref, g2_ref, b2_ref, tgt_ref,
             r_ref, dr2_ref, dh_ref, small_ref, acc_ref):
        i, f = pl.program_id(0), pl.program_id(1)
        dm = D_MODEL

        @pl.when((i == 0) & (f == 0))
        def _():
            small_ref[...] = jnp.zeros_like(small_ref)

        @pl.when(f == 0)
        def _():
            acc_ref[...] = jnp.zeros_like(acc_ref)

        u2t = u2_ref[...]
        part = None
        for s in range(MLP_SLABS):
            r = jnp.maximum(_dot(u2t, w1_ref[s]), 0.0)
            r_ref[:, s * tf:(s + 1) * tf] = r.astype(BF16)
            d = _bdot(r * r, w2_ref[s])
            part = d if part is None else part + d
        acc_ref[...] += part

        @pl.when(f == nf - 1)
        def _():
            h = acc_ref[...]
            x1 = xhat_ref[...] * g1_ref[...] + b1_ref[...]
            xhat2, rstd2 = _ln_fwd(ALPHA * x1 + (1.0 + gm_ref[...]) * h)
            err = xhat2 * g2_ref[...] + b2_ref[...] - tgt_ref[...]
            small_ref[:, 3 * dm:] += jnp.sum(0.5 * _lanemean(err * err), axis=0, keepdims=True)
            dy = err * (1.0 / D_MODEL)
            small_ref[:, dm:2 * dm] += _rowsum(dy * xhat2)
            small_ref[:, 2 * dm:3 * dm] += _rowsum(dy)
            dr2 = _ln_bwd(dy * g2_ref[...], xhat2, rstd2)
            dr2_ref[...] = dr2
            small_ref[:, 0:dm] += _rowsum(dr2 * h)
            dh_ref[...] = ((1.0 + gm_ref[...]) * dr2).astype(BF16)

    row = pl.BlockSpec((tm, D_MODEL), lambda i, f: (i, 0))
    vec = _full((1, D_MODEL))
    return pl.pallas_call(
        body, name="mlp_fwd", grid=(T // tm, nf),
        in_specs=[row, pl.BlockSpec((MLP_SLABS, D_MODEL, tf), lambda i, f: (f, 0, 0)),
                  pl.BlockSpec((MLP_SLABS, tf, D_MODEL), lambda i, f: (f, 0, 0)),
                  row, vec, vec, vec, vec, vec, row],
        out_specs=[pl.BlockSpec((tm, MLP_SLABS * tf), lambda i, f: (i, f)), row, row, _full((1, 3 * D_MODEL + 128))],
        out_shape=[jax.ShapeDtypeStruct((T, N_DEV * tf), BF16), jax.ShapeDtypeStruct((T, D_MODEL), F32),
                   jax.ShapeDtypeStruct((T, D_MODEL), BF16), jax.ShapeDtypeStruct((1, 3 * D_MODEL + 128), F32)],
        scratch_shapes=[pltpu.VMEM((tm, D_MODEL), F32)],
        compiler_params=_params(),
    )(u2, w1, w2, xhat1, ln1_g, ln1_b, g_m, ln2_g, ln2_b, target)


def _mlp_bwd(dh, w1, w2, r, dr2, xhat1, rstd1, mix, ln1_g, ln1_b, sc_m, g_a):
    T = dh.shape[0]
    tf = w1.shape[-1]
    nf = N_DEV // MLP_SLABS
    tm = min(ROW_TILE, T)

    def body(dh_ref, w1_ref, w2_ref, r_ref, dr2_ref, xhat_ref, rstd_ref, mix_ref, g1_ref, b1_ref, sc_ref, ga_ref,
             dhpre_ref, dr1_ref, dmix_ref, small_ref, acc_ref):
        i, f = pl.program_id(0), pl.program_id(1)
        dm = D_MODEL

        @pl.when((i == 0) & (f == 0))
        def _():
            small_ref[...] = jnp.zeros_like(small_ref)

        @pl.when(f == 0)
        def _():
            acc_ref[...] = jnp.zeros_like(acc_ref)

        dht = dh_ref[...]
        part = None
        for s in range(MLP_SLABS):
            cols = slice(s * tf, (s + 1) * tf)
            dhpre = (_dot(dht, w2_ref[s], NT) * (2.0 * r_ref[:, cols].astype(F32))).astype(BF16)
            dhpre_ref[:, cols] = dhpre
            d = _dot(dhpre, w1_ref[s], NT)
            part = d if part is None else part + d
        acc_ref[...] += part

        @pl.when(f == nf - 1)
        def _():
            du2 = acc_ref[...]
            xhat = xhat_ref[...]
            x1 = xhat * g1_ref[...] + b1_ref[...]
            dx1 = ALPHA * dr2_ref[...] + du2 * (1.0 + sc_ref[...])
            small_ref[:, 2 * dm:3 * dm] += _rowsum(du2 * x1)
            small_ref[:, dm:2 * dm] += _rowsum(du2)
            small_ref[:, 3 * dm:4 * dm] += _rowsum(dx1 * xhat)
            small_ref[:, 4 * dm:5 * dm] += _rowsum(dx1)
            dr1 = _ln_bwd(dx1 * g1_ref[...], xhat, rstd_ref[:, 0:1])
            dr1_ref[...] = dr1
            small_ref[:, 0:dm] += _rowsum(dr1 * mix_ref[...])
            dmix_ref[...] = ((1.0 + ga_ref[...]) * dr1).astype(BF16)

    row = pl.BlockSpec((tm, D_MODEL), lambda i, f: (i, 0))
    vec = _full((1, D_MODEL))
    return pl.pallas_call(
        body, name="mlp_bwd", grid=(T // tm, nf),
        in_specs=[row, pl.BlockSpec((MLP_SLABS, D_MODEL, tf), lambda i, f: (f, 0, 0)),
                  pl.BlockSpec((MLP_SLABS, tf, D_MODEL), lambda i, f: (f, 0, 0)),
                  pl.BlockSpec((tm, MLP_SLABS * tf), lambda i, f: (i, f)), row, row,
                  pl.BlockSpec((tm, 128), lambda i, f: (i, 0)), row, vec, vec, vec, vec],
        out_specs=[pl.BlockSpec((tm, MLP_SLABS * tf), lambda i, f: (i, f)), row, row, _full((1, 5 * D_MODEL))],
        out_shape=[jax.ShapeDtypeStruct((T, N_DEV * tf), BF16), jax.ShapeDtypeStruct((T, D_MODEL), F32),
                   jax.ShapeDtypeStruct((T, D_MODEL), BF16), jax.ShapeDtypeStruct((1, 5 * D_MODEL), F32)],
        scratch_shapes=[pltpu.VMEM((tm, D_MODEL), F32)],
        compiler_params=_params(),
    )(dh, w1, w2, r, dr2, xhat1, rstd1, mix, ln1_g, ln1_b, sc_m, g_a)


def _input_bwd(dz_h, dz_m, w_in_ext, x, dr1, sc_a, exchange=None):
    T = x.shape[0]
    tm = min(ROW_TILE, T)

    def body(dzh_ref, dzm_ref, w_ref, x_ref, dr1_ref, sc_ref, gx_ref, small_ref):
        @pl.when(pl.program_id(0) == 0)
        def _():
            small_ref[...] = jnp.zeros_like(small_ref)

        du = _bdot(dzh_ref[...], w_ref[0:2048, :]) + _bdot(dzm_ref[...], w_ref[2048:3072, :])
        gx_ref[...] = ALPHA * dr1_ref[...] + du * (1.0 + sc_ref[...])
        small_ref[:, D_MODEL:] += _rowsum(du * x_ref[...])
        small_ref[:, 0:D_MODEL] += _rowsum(du)

    row = pl.BlockSpec((tm, D_MODEL), lambda i: (i, 0))
    vec = _full((1, D_MODEL))
    return _call(
        body, "input_bwd", (dz_h, dz_m, w_in_ext, x, dr1, sc_a), grid=(T // tm,),
        in_specs=[pl.BlockSpec((tm, 2048), lambda i: (i, 0)), row, _full(w_in_ext.shape), row, row, vec],
        out_specs=[row, _full((1, 2 * D_MODEL))],
        out_shape=[jax.ShapeDtypeStruct((T, D_MODEL), F32), jax.ShapeDtypeStruct((1, 2 * D_MODEL), F32)],
        exchange=exchange)


def _adam_math(w, g, m, v):
    m = ADAM_B1 * m + (1.0 - ADAM_B1) * g
    v = ADAM_B2 * v + (1.0 - ADAM_B2) * (g * g)
    m_hat = m / (1.0 - ADAM_B1 ** ADAM_STEP)
    v_hat = v / (1.0 - ADAM_B2 ** ADAM_STEP)
    return -ADAM_LR * (m_hat / (jnp.sqrt(v_hat) + ADAM_EPS) + ADAM_WD * w), m, v


def _adam(g_slabs, w, m, v, name, g_fn=None, g_extra=()):
    R, C = w.shape
    tr = 256 if R % 256 == 0 else R
    ns = 0 if g_slabs is None else g_slabs.shape[0]
    slab_rows = tr if g_slabs is None or g_slabs.shape[1] == R else g_slabs.shape[1]
    assert slab_rows == tr or tr == R
    ne = len(g_extra)

    def body(*refs):
        e_refs = refs[:ne]
        refs = refs[ne:]
        if ns:
            gs_ref, refs = refs[0], refs[1:]
        w_ref, m_ref, v_ref, g_ref, d_ref, nm_ref, nv_ref = refs
        if g_fn is not None:
            g = g_fn(*e_refs)
        else:
            g = gs_ref[0].astype(F32)
            for s in range(1, ns):
                g = g + gs_ref[s].astype(F32)
            g = g[:tr]
        d, nm, nv = _adam_math(w_ref[...], g, m_ref[...], v_ref[...])
        g_ref[...] = g
        d_ref[...] = d
        nm_ref[...] = nm
        nv_ref[...] = nv

    blk = pl.BlockSpec((tr, C), lambda i: (i, 0))
    in_specs = [pl.BlockSpec((tr, e.shape[1]), lambda i: (i, 0)) if e.shape[0] == R else _full(e.shape) for e in g_extra]
    args = list(g_extra)
    if ns:
        in_specs.append(pl.BlockSpec((ns, slab_rows, C), lambda i: (0, i, 0)))
        args.append(g_slabs)
    return pl.pallas_call(
        body, name=name, grid=(R // tr,), in_specs=in_specs + [blk] * 3, out_specs=[blk] * 4,
        out_shape=[jax.ShapeDtypeStruct((R, C), F32)] * 4, compiler_params=_params(),
    )(*args, w, m, v)


def _adam_small(small_all, params):
    n = len(params)

    def body(*refs):
        s_ref, refs = refs[0], refs[1:]
        wmv, loss_ref, outs = refs[:3 * n], refs[3 * n], refs[3 * n + 1:]
        tot = s_ref[0]
        for i in range(1, N_DEV):
            tot = tot + s_ref[i]
        loss_ref[...] = tot[:, SMALL_W - 128:]
        for j, (w, _, _, off) in enumerate(params):
            w_ref, m_ref, v_ref = wmv[3 * j:3 * j + 3]
            g_ref, d_ref, nm_ref, nv_ref = outs[4 * j:4 * j + 4]
            if w.shape[0] == 2:
                lb = _lower_bound(w_ref)
                g0 = tot[:, off:off + w.shape[1]] * lb * (1.0 - lb)
                rows = [(slice(0, 1), g0), (slice(1, 2), -g0)]
            else:
                rows = [(slice(0, 1), tot[:, off:off + w.shape[1]])]
            for rs, g in rows:
                d, nm, nv = _adam_math(w_ref[rs, :], g, m_ref[rs, :], v_ref[rs, :])
                g_ref[rs, :], d_ref[rs, :], nm_ref[rs, :], nv_ref[rs, :] = g, d, nm, nv

    out_shape = [jax.ShapeDtypeStruct((1, 128), F32)]
    for w, _, _, _ in params:
        out_shape += [jax.ShapeDtypeStruct(w.shape, F32)] * 4
    res = pl.pallas_call(body, name="adam_small", out_shape=out_shape, compiler_params=_params())(
        small_all, *[a for w, m, v, _ in params for a in (w, m, v)])
    return res[0], [tuple(res[1 + 4 * j:5 + 4 * j]) for j in range(n)]


def _cols_from_slabs(g):
    s, r, c = g.shape
    return jnp.transpose(g, (1, 0, 2)).reshape(r, s * c)


def _slabs_from_cols(w):
    r, c = w.shape
    return jnp.transpose(w.reshape(r, N_DEV, c // N_DEV), (1, 0, 2))


def _rot_half_rows(wt):
    return jnp.concatenate([-wt[32:], wt[:32]], axis=0)


def _unrot_half_rows(dwt_rot):
    return jnp.concatenate([dwt_rot[32:], -dwt_rot[:32]], axis=0)


def _ext_in_t(g):
    k_in = g.shape[2]
    z64, z128 = jnp.zeros((64, k_in), BF16), jnp.zeros((128, k_in), BF16)
    wt = g.reshape(N_DEV * g.shape[1], k_in)
    main, wk = wt[:wt.shape[0] - ROPE_DIM], wt[wt.shape[0] - ROPE_DIM:]
    return jnp.concatenate([main, z128, wk, z64, z128, _rot_half_rows(wk), z64], axis=0)


def _ext_q_t(wt):
    r = wt.shape[1]
    z64, z128 = jnp.zeros((64, r), BF16), jnp.zeros((128, r), BF16)
    per = HEAD_DIM + ROPE_DIM
    main = [jnp.concatenate([wt[per * h:per * (h + 1)], z64], axis=0) for h in range(HEADS)]
    rot = [jnp.concatenate([z128, _rot_half_rows(wt[per * h + HEAD_DIM:per * (h + 1)]), z64], axis=0)
           for h in range(HEADS)]
    return jnp.concatenate(main + rot, axis=0)


def _ext_kv(w_kv_up):
    r = w_kv_up.shape[0]
    z128 = jnp.zeros((r, 128), BF16)
    wkv = w_kv_up.reshape(r, HEADS, 2 * HEAD_DIM)
    kpad = [jnp.concatenate([wkv[:, h, :HEAD_DIM], z128], axis=1) for h in range(HEADS)]
    vals = [wkv[:, h, HEAD_DIM:] for h in range(HEADS)]
    return jnp.concatenate(kpad + vals, axis=1)


def _grad_in_from_ext_t(dwt_h, dwt_m):
    dwk = dwt_m[512 + 128:512 + 192] + _unrot_half_rows(dwt_m[768 + 128:768 + 192])
    return jnp.concatenate([dwt_h, dwt_m[:512], dwk], axis=0)


def _grad_q_from_ext_t(dwq_ext_t):
    rows = []
    for h in range(HEADS):
        main, rot = dwq_ext_t[256 * h:256 * h + 256], dwq_ext_t[1024 + 256 * h:1280 + 256 * h]
        rows += [main[:128], main[128:192] + _unrot_half_rows(rot[128:192])]
    return jnp.concatenate(rows, axis=0)


def _grad_kv_from_ext(dwkv_ext):
    kvcols = []
    for h in range(HEADS):
        kvcols += [dwkv_ext[:, 256 * h:256 * h + 128], dwkv_ext[:, 1024 + 128 * h:1152 + 128 * h]]
    return jnp.concatenate(kvcols, axis=1)


SMALL_W = 6144 + 512 + 512 + 256 + 256 + 4 * 1024 + 128


def kernel(x, c, positions, w_ada, b_ada, w_in, hg_lower_bounds, hg_norm_w, mla_q_norm_w, w_q_up, mla_kv_norm_w, w_kv_up, w_out, ln1_g, ln1_b, w_mlp_in, w_mlp_out, ln2_g, ln2_b, loss_target, m_w_ada, m_b_ada, m_w_in, m_hg_lower_bounds, m_hg_norm_w, m_mla_q_norm_w, m_w_q_up, m_mla_kv_norm_w, m_w_kv_up, m_w_out, m_ln1_g, m_ln1_b, m_w_mlp_in, m_w_mlp_out, m_ln2_g, m_ln2_b, v_w_ada, v_b_ada, v_w_in, v_hg_lower_bounds, v_hg_norm_w, v_mla_q_norm_w, v_w_q_up, v_mla_kv_norm_w, v_w_kv_up, v_w_out, v_ln1_g, v_ln1_b, v_w_mlp_in, v_w_mlp_out, v_ln2_g, v_ln2_b):
    T = x.shape[1]
    me = 4 * lax.axis_index("x") + 2 * lax.axis_index("y") + lax.axis_index("c")
    xs, tgt = x[0], loss_target[0]
    transposed = ("w_in", "w_q_up")
    as_used = lambda n, a: a[0].T if n in transposed else a[0]
    big = {n: as_used(n, a) for n, a in dict(w_in=w_in, w_q_up=w_q_up, w_kv_up=w_kv_up, w_out=w_out,
                                              w_mlp_in=w_mlp_in, w_mlp_out=w_mlp_out).items()}
    names = list(big)

    bf = {n: big[n].astype(BF16) for n in names}
    g_in, g_c = _gather_two_level([bf["w_in"], c], name="gather_w_in")
    c_all = g_c.reshape(N_DEV, D_MODEL)

    ada_cols = w_ada.shape[2]
    mod_part, cond = _mod_part(c_all, w_ada[0], lax.dynamic_slice(b_ada, (0, me * ada_cols), (1, ada_cols)))
    (mod_all,) = _exchange([mod_part], scatter=False, name="gather_mod")
    mod_row = lax.dynamic_slice(mod_all, (0, me, 0), (N_DEV, 1, ada_cols)).reshape(1, N_DEV * ada_cols)
    sh_a, sc_a, g_a, sh_m, sc_m, g_m = [mod_row[:, D_MODEL * i:D_MODEL * (i + 1)] for i in range(6)]

    w_in_ext = _ext_in_t(g_in)
    z, (g_q, g_kv, g_out) = _matmul(xs, w_in_ext, "NT", "in_proj", a_fn=_modulate, extras=(sc_a, sh_a), tn=3072,
                                    exchange=_Exchange([bf["w_q_up"], bf["w_kv_up"], bf["w_out"]], False))
    wq_ext = _ext_q_t(g_q.reshape(N_DEV * g_q.shape[1], g_q.shape[2]))
    wkv_ext = _ext_kv(_cols_from_slabs(g_kv))
    w_out_full = g_out.reshape(D_MODEL, D_MODEL)
    inv_freq = 1.0 / (ROPE_THETA ** (jnp.arange(0, ROPE_DIM, 2, dtype=F32) / ROPE_DIM))
    zeros = lambda n: jnp.zeros((n,), F32)
    invf = jnp.concatenate([zeros(128), inv_freq, inv_freq, zeros(64)]).reshape(1, QK_PAD)
    m_one = jnp.concatenate([jnp.ones((128,), F32), zeros(128)]).reshape(1, QK_PAD)
    m_rot = jnp.concatenate([zeros(128), jnp.ones((64,), F32), zeros(64)]).reshape(1, QK_PAD)
    q, k, v, c1, s1, cqn, ckvn = _mla_pre(z, positions.reshape(T, 1), invf, m_one, m_rot, wq_ext, wkv_ext,
                                          mla_q_norm_w, mla_kv_norm_w)[0]
    (o_raw, o_gated, s_prev), (w1,) = _hgrn_fwd(z, hg_lower_bounds, hg_norm_w,
                                                exchange=_StagedGather(bf["w_mlp_in"]))
    (o_mla, lse), (w2,) = _attn_fwd(q, k, v, exchange=_StagedGather(bf["w_mlp_out"]))
    mix, xhat1, rstd1, u2 = _mix_ln1(o_gated, o_mla, w_out_full, xs, g_a, ln1_g, ln1_b, sc_m, sh_m)[0]
    r, dr2, dh, small_mlp_fwd = _mlp_fwd(u2, w1, w2, xhat1, ln1_g, ln1_b, g_m, ln2_g, ln2_b, tgt)

    dhpre, dr1, dmix, small_mlp_bwd = _mlp_bwd(dh, w1, w2, r, dr2, xhat1, rstd1, mix, ln1_g, ln1_b, sc_m, g_a)
    received = {}
    dw2 = _matmul(r, dh, "TN", "wgrad_mlp_out", out_dtype=BF16, a_fn=_square, tm=1024, tk=2048)
    dw1 = _matmul(u2, dhpre, "TN", "wgrad_mlp_in", out_dtype=BF16, tm=1024, tk=2048, out_slabs=N_DEV)
    dmixcat = _matmul(dmix, w_out_full, "NT", "dgrad_out")
    dw_out = jnp.concatenate([_matmul(o_gated, dmix, "TN", "wgrad_out_hg", out_dtype=BF16, tk=2048),
                              _matmul(o_mla, dmix, "TN", "wgrad_out_mla", out_dtype=BF16, tk=2048)], axis=0)
    (dz_h, small_hgrn), (received["w_out"],) = _hgrn_bwd(
        dmixcat, z, o_raw, s_prev, hg_lower_bounds, hg_norm_w,
        exchange=_Exchange([dw_out.reshape(N_DEV, D_MODEL // N_DEV, D_MODEL)], True))
    (dq, dk, dv), (received["w_mlp_in"], received["w_mlp_out"]) = _attn_bwd(
        q, k, v, dmixcat, o_mla, lse,
        exchange=_Exchange([dw1, dw2.reshape(N_DEV, dw2.shape[0] // N_DEV, D_MODEL)], True))
    dz_m, dq_ext, dkv_ext, small_mla = _mla_bwd(dq, dk, dv, z, c1, s1, wq_ext, wkv_ext, mla_q_norm_w, mla_kv_norm_w)
    dwq_t = _grad_q_from_ext_t(_matmul(dq_ext, cqn, "TN", "wgrad_q_up", tm=1024, tk=2048))
    dwkv = _grad_kv_from_ext(_matmul(ckvn, dkv_ext, "TN", "wgrad_kv_up", tn=1536, tk=2048))
    qkv_slabs = [dwq_t.reshape((N_DEV, dwq_t.shape[0] // N_DEV, dwq_t.shape[1])).astype(BF16),
                 _slabs_from_cols(dwkv).astype(BF16)]
    dwt_h, (received["w_q_up"], received["w_kv_up"]) = _matmul(
        dz_h, xs, "TN", "wgrad_in_h", b_fn=_modulate, extras=(sc_a, sh_a), tm=1024, tk=2048,
        exchange=_Exchange(qkv_slabs, True))
    dwt_m = _matmul(dz_m, xs, "TN", "wgrad_in_m", b_fn=_modulate, extras=(sc_a, sh_a), tm=1024, tk=2048)
    dw_in_t = _grad_in_from_ext_t(dwt_h, dwt_m)
    in_slabs = dw_in_t.reshape((N_DEV, dw_in_t.shape[0] // N_DEV, dw_in_t.shape[1]))
    in_slabs = jnp.pad(in_slabs, ((0, 0), (0, -in_slabs.shape[1] % 16), (0, 0))).astype(BF16)
    (grad_x, small_in), (received["w_in"],) = _input_bwd(
        dz_h, dz_m, w_in_ext, xs, dr1, sc_a, exchange=_StagedScatter(in_slabs))

    small = jnp.concatenate([small_in, small_mlp_bwd[:, :3 * D_MODEL], small_mlp_fwd[:, :D_MODEL], small_hgrn,
                             small_mla, small_mlp_bwd[:, 3 * D_MODEL:], small_mlp_fwd[:, D_MODEL:]], axis=1)
    assert small.shape == (1, SMALL_W)
    (small_all,) = _exchange([small], scatter=False, name="gather_small")

    moments = dict(w_in=(m_w_in, v_w_in), w_q_up=(m_w_q_up, v_w_q_up), w_kv_up=(m_w_kv_up, v_w_kv_up),
                   w_out=(m_w_out, v_w_out), w_mlp_in=(m_w_mlp_in, v_w_mlp_in), w_mlp_out=(m_w_mlp_out, v_w_mlp_out))
    res = {}
    for n in names:
        res[n] = _adam(received[n], big[n], as_used(n, moments[n][0]), as_used(n, moments[n][1]), name="adam_" + n)
    dmod_cols = lax.dynamic_slice(small_all.reshape(N_DEV, SMALL_W), (0, me * ada_cols), (N_DEV, ada_cols))
    cond_t = cond.T

    def ada_grad(ct_ref, dm_ref):
        g = ct_ref[:, 0:1] * dm_ref[0:1, :]
        for b in range(1, N_DEV):
            g = g + ct_ref[:, b:b + 1] * dm_ref[b:b + 1, :]
        return g

    res["w_ada"] = _adam(None, w_ada[0], m_w_ada[0], v_w_ada[0], name="adam_w_ada", g_fn=ada_grad,
                         g_extra=(cond_t, dmod_cols))

    small_params = [("b_ada", b_ada, m_b_ada, v_b_ada, 0),
                    ("hg_lower_bounds", hg_lower_bounds, m_hg_lower_bounds, v_hg_lower_bounds, 6144),
                    ("hg_norm_w", hg_norm_w, m_hg_norm_w, v_hg_norm_w, 6656),
                    ("mla_q_norm_w", mla_q_norm_w, m_mla_q_norm_w, v_mla_q_norm_w, 7168),
                    ("mla_kv_norm_w", mla_kv_norm_w, m_mla_kv_norm_w, v_mla_kv_norm_w, 7424),
                    ("ln1_g", ln1_g, m_ln1_g, v_ln1_g, 7680), ("ln1_b", ln1_b, m_ln1_b, v_ln1_b, 8704),
                    ("ln2_g", ln2_g, m_ln2_g, v_ln2_g, 9728), ("ln2_b", ln2_b, m_ln2_b, v_ln2_b, 10752)]
    loss_row, small_res = _adam_small(small_all, [p[1:] for p in small_params])
    for p, r4 in zip(small_params, small_res):
        res[p[0]] = r4
    loss = loss_row[0, 0]

    order = ["w_ada", "b_ada", "w_in", "hg_lower_bounds", "hg_norm_w", "mla_q_norm_w", "w_q_up", "mla_kv_norm_w",
             "w_kv_up", "w_out", "ln1_g", "ln1_b", "w_mlp_in", "w_mlp_out", "ln2_g", "ln2_b"]
    def as_given(n, a):
        if n in transposed:
            a = a.T
        return a[None] if n in big or n == "w_ada" else a

    shaped = {n: tuple(as_given(n, a) for a in res[n]) for n in order}
    outs = [loss, grad_x.reshape(1, T, D_MODEL)]
    for i in range(4):
        outs += [shaped[n][i] for n in order]
    return tuple(outs)
```

```python
import functools

import jax
import jax.numpy as jnp
import numpy as np
from jax import lax
from jax.experimental import pallas as pl
from jax.experimental.pallas import tpu as pltpu

F32, BF16 = jnp.float32, jnp.bfloat16
N_DEV = 8
D_MODEL = 1024
HEADS = 4
HEAD_DIM = 128
ROPE_DIM = 64
QK_PAD = 256
CHUNK = 64
ROPE_THETA = 10000.0
RMS_EPS = 1e-6
LN_EPS = 1e-5
ALPHA = 2.0 ** 0.25
ATT_SCALE = (HEAD_DIM + ROPE_DIM) ** -0.5
LN2 = float(np.log(2.0))
Q_PRESCALE = ATT_SCALE / LN2
ADAM_LR, ADAM_B1, ADAM_B2, ADAM_EPS, ADAM_WD, ADAM_STEP = 0.001, 0.9, 0.999, 1e-08, 0.01, 10
NEG_BIG = -1e30

ROW_TILE = 512
ATT_TILE = 512
HGRN_GROUP = 8
MLP_SLABS = 4
VMEM_LIMIT = 56 * 2 ** 20

NN = (((1,), (0,)), ((), ()))
NT = (((1,), (1,)), ((), ()))
TN = (((0,), (0,)), ((), ()))


def _dot(a, b, dims=NN):
    return lax.dot_general(a, b, dims, preferred_element_type=F32)


def _bdot(a, b, dims=NN):
    return lax.dot_general(a.astype(BF16), b.astype(BF16), dims, preferred_element_type=F32)


def _hdot(a, b, dims=NN):
    return lax.dot_general(a, b, dims, precision=lax.Precision.HIGHEST, preferred_element_type=F32)


def _params():
    return pltpu.CompilerParams(vmem_limit_bytes=VMEM_LIMIT)


def _sigmoid(x):
    return 1.0 / (1.0 + jnp.exp(-x))


def _rowsum(x):
    return jnp.sum(x, axis=0, keepdims=True)


def _lanemean(x):
    return jnp.mean(x, axis=-1, keepdims=True)


def _full(shape):
    nd = len(shape)
    return pl.BlockSpec(shape, lambda *_: (0,) * nd)


class _Exchange:
    def __init__(self, arrs, scatter):
        self.arrs, self.scatter, self.n, self.aliases, self.middle_at = list(arrs), scatter, len(arrs), [], 0.5
        self.out_shape = [jax.ShapeDtypeStruct((N_DEV,) + (a.shape[1:] if scatter else a.shape), a.dtype)
                          for a in self.arrs]
        n = self.n
        self.scratch = [pltpu.SemaphoreType.DMA((n, N_DEV - 1)), pltpu.SemaphoreType.DMA((n, N_DEV - 1)),
                        pltpu.SemaphoreType.DMA((n,))]

    def _copies(self, ins, outs, sems):
        send_sems, recv_sems, loc_sems = sems
        x, y, c = lax.axis_index("x"), lax.axis_index("y"), lax.axis_index("c")
        me = 4 * x + 2 * y + c
        copies = []
        for k in range(self.n):
            src_of = (lambda i, k=k: ins[k].at[i]) if self.scatter else (lambda i, k=k: ins[k])
            copies.append((pltpu.make_async_copy(src_of(me), outs[k].at[me], loc_sems.at[k]), None))
            for p in range(1, N_DEV):
                px = (1 - x) if p & 4 else x
                py = (1 - y) if p & 2 else y
                pc = (1 - c) if p & 1 else c
                peer = 4 * px + 2 * py + pc
                both = dict(send_sem=send_sems.at[k, p - 1], recv_sem=recv_sems.at[k, p - 1],
                            device_id=(px, py, pc), device_id_type=pl.DeviceIdType.MESH)
                send = pltpu.make_async_remote_copy(src_ref=src_of(peer), dst_ref=outs[k].at[me], **both)
                recv = pltpu.make_async_remote_copy(src_ref=src_of(peer), dst_ref=outs[k].at[peer], **both)
                copies.append((send, recv))
        return copies

    def start(self, ins, outs, sems):
        for first, _ in self._copies(ins, outs, sems):
            first.start()

    def middle(self, ins, outs, sems):
        pass

    def wait(self, ins, outs, sems):
        for first, recv in self._copies(ins, outs, sems):
            if recv is None:
                first.wait()
            else:
                recv.wait_recv()
                first.wait_send()


class _StagedGather:
    def __init__(self, arr):
        self.arrs, self.aliases, self.middle_at = [arr], [], 0.8
        self.out_shape = [jax.ShapeDtypeStruct((N_DEV,) + arr.shape, arr.dtype)]
        self.scratch = [pltpu.VMEM((N_DEV,) + arr.shape, arr.dtype), pltpu.SemaphoreType.DMA((7,)),
                        pltpu.SemaphoreType.DMA((7,)), pltpu.SemaphoreType.DMA((2,))]

    def _parts(self, scr):
        stage, send_sems, recv_sems, loc_sems = scr
        x, y, c = lax.axis_index("x"), lax.axis_index("y"), lax.axis_index("c")
        me, sibling = (x, y, c), (x, y, 1 - c)
        chips = [(1 - x, y), (x, 1 - y), (1 - x, 1 - y)]

        def copy(j, block, to):
            px, py, pc = block
            slot = stage.at[4 * px + 2 * py + pc]
            return pltpu.make_async_remote_copy(src_ref=slot, dst_ref=slot, send_sem=send_sems.at[j],
                                                recv_sem=recv_sems.at[j], device_id=to,
                                                device_id_type=pl.DeviceIdType.MESH)

        return stage, loc_sems, me, sibling, chips, c, copy

    def start(self, ins, outs, scr):
        stage, loc_sems, me, sibling, chips, c, copy = self._parts(scr)
        x, y, _ = me
        own = pltpu.make_async_copy(ins[0], stage.at[4 * x + 2 * y + c], loc_sems.at[0])
        own.start()
        own.wait()
        copy(0, me, sibling).start()
        for j, chip in enumerate(chips):
            copy(1 + j, me, (*chip, c)).start()

    def middle(self, ins, outs, scr):
        stage, loc_sems, me, sibling, chips, c, copy = self._parts(scr)
        for j, chip in enumerate(chips):
            copy(1 + j, (*chip, c), me).wait_recv()
            copy(4 + j, (*chip, c), sibling).start()

    def wait(self, ins, outs, scr):
        stage, loc_sems, me, sibling, chips, c, copy = self._parts(scr)
        copy(0, sibling, me).wait_recv()
        for j, chip in enumerate(chips):
            copy(4 + j, (*chip, 1 - c), me).wait_recv()
        copy(0, me, sibling).wait_send()
        for j, chip in enumerate(chips):
            copy(1 + j, me, (*chip, c)).wait_send()
            copy(4 + j, (*chip, c), sibling).wait_send()
        whole = pltpu.make_async_copy(stage, outs[0], loc_sems.at[1])
        whole.start()
        whole.wait()


class _StagedScatter:
    def __init__(self, slabs, middle_at=0.2):
        _, r, c = slabs.shape
        self.arrs, self.aliases, self.middle_at = [slabs], [], middle_at
        self.out_shape = [jax.ShapeDtypeStruct((4, r, c), slabs.dtype)]
        self.scratch = [pltpu.VMEM((N_DEV, r, c), slabs.dtype), pltpu.VMEM((4, r, c), slabs.dtype),
                        pltpu.VMEM((3, r, c), slabs.dtype), pltpu.SemaphoreType.DMA((4,)), pltpu.SemaphoreType.DMA((4,)),
                        pltpu.SemaphoreType.DMA((3,)), pltpu.SemaphoreType.DMA((3,)), pltpu.SemaphoreType.DMA((4,))]

    def _parts(self, scr):
        stage, from_sib, from_chips, sib_send, sib_recv, ici_send, ici_recv, loc_sems = scr
        x, y, c = lax.axis_index("x"), lax.axis_index("y"), lax.axis_index("c")
        chips = [(1 - x, y), (x, 1 - y), (1 - x, 1 - y)]

        def to_sibling(j):
            return pltpu.make_async_remote_copy(src_ref=stage.at[2 * j + 1 - c], dst_ref=from_sib.at[j],
                                                send_sem=sib_send.at[j], recv_sem=sib_recv.at[j],
                                                device_id=(x, y, 1 - c), device_id_type=pl.DeviceIdType.MESH)

        def to_chip(k):
            px, py = chips[k]
            return pltpu.make_async_remote_copy(src_ref=stage.at[4 * px + 2 * py + c], dst_ref=from_chips.at[k],
                                                send_sem=ici_send.at[k], recv_sem=ici_recv.at[k],
                                                device_id=(px, py, c), device_id_type=pl.DeviceIdType.MESH)

        return stage, from_sib, from_chips, loc_sems, (x, y, c), chips, to_sibling, to_chip

    def start(self, ins, outs, scr):
        stage, _, _, loc_sems, _, _, to_sibling, _ = self._parts(scr)
        load = pltpu.make_async_copy(ins[0], stage, loc_sems.at[0])
        load.start()
        load.wait()
        for j in range(4):
            to_sibling(j).start()

    def middle(self, ins, outs, scr):
        stage, from_sib, _, _, (x, y, c), _, to_sibling, to_chip = self._parts(scr)
        for j in range(4):
            to_sibling(j).wait_recv()
            mine = stage.at[2 * j + c]
            mine[...] = (mine[...].astype(F32) + from_sib[j].astype(F32)).astype(mine.dtype)
        for k in range(3):
            to_chip(k).start()

    def wait(self, ins, outs, scr):
        stage, _, from_chips, loc_sems, (x, y, c), chips, to_sibling, to_chip = self._parts(scr)
        writes = [pltpu.make_async_copy(stage.at[4 * x + 2 * y + c], outs[0].at[2 * x + y], loc_sems.at[0])]
        for k, (px, py) in enumerate(chips):
            to_chip(k).wait_recv()
            writes.append(pltpu.make_async_copy(from_chips.at[k], outs[0].at[2 * px + py], loc_sems.at[1 + k]))
        for w in writes:
            w.start()
        for j in range(4):
            to_sibling(j).wait_send()
        for k in range(3):
            to_chip(k).wait_send()
        for w in writes:
            w.wait()


def _call(body, name, args, out_shape, grid=(), in_specs=(), out_specs=(), scratch_shapes=(), exchange=None):
    if exchange is None:
        return pl.pallas_call(body, name=name, grid=grid, in_specs=list(in_specs), out_specs=list(out_specs),
                              out_shape=list(out_shape), scratch_shapes=list(scratch_shapes),
                              compiler_params=_params())(*args), None
    exs = list(exchange) if isinstance(exchange, (list, tuple)) else [exchange]
    ni, no, ns = len(args), len(out_shape), len(scratch_shapes)
    nxi, nxo = sum(len(e.arrs) for e in exs), sum(len(e.out_shape) for e in exs)
    steps = int(np.prod(grid))
    mid_step = lambda e: min(max(int(steps * e.middle_at), 1), steps - 1)
    aliases, iat, oat = {}, ni, no
    for e in exs:
        for src, dst in e.aliases:
            aliases[iat + src] = oat + dst
        iat, oat = iat + len(e.arrs), oat + len(e.out_shape)

    def wrapped(*refs):
        a, xi = refs[:ni], refs[ni:ni + nxi]
        o, xo = refs[ni + nxi:ni + nxi + no], refs[ni + nxi + no:ni + nxi + no + nxo]
        s, xs = refs[ni + nxi + no + nxo:ni + nxi + no + nxo + ns], refs[ni + nxi + no + nxo + ns:]
        parts, iat, oat, sat = [], 0, 0, 0
        for e in exs:
            parts.append((e, xi[iat:iat + len(e.arrs)], xo[oat:oat + len(e.out_shape)], xs[sat:sat + len(e.scratch)]))
            iat, oat, sat = iat + len(e.arrs), oat + len(e.out_shape), sat + len(e.scratch)
        step = 0
        for d, g in enumerate(grid):
            step = step * g + pl.program_id(d)

        @pl.when(step == 0)
        def _():
            for e, ins, outs, sems in parts:
                e.start(ins, outs, sems)

        for at_step in sorted({mid_step(e) for e in exs}):
            @pl.when(step == at_step)
            def _():
                for e, ins, outs, sems in parts:
                    if mid_step(e) == at_step:
                        e.middle(ins, outs, sems)

        body(*a, *o, *s)

        @pl.when(step == steps - 1)
        def _():
            for e, ins, outs, sems in parts:
                e.wait(ins, outs, sems)

    hbm = pl.BlockSpec(memory_space=pltpu.HBM)
    res = pl.pallas_call(
        wrapped, name=name, grid=grid, in_specs=list(in_specs) + [hbm] * nxi, out_specs=list(out_specs) + [hbm] * nxo,
        out_shape=list(out_shape) + [o_ for e in exs for o_ in e.out_shape],
        scratch_shapes=list(scratch_shapes) + [s_ for e in exs for s_ in e.scratch],
        input_output_aliases=aliases, compiler_params=_params())(*args, *[a_ for e in exs for a_ in e.arrs])
    return res[:no], res[no:]


def _gather_two_level(arrs, name):
    n = len(arrs)
    out_shape = [jax.ShapeDtypeStruct((N_DEV,) + a.shape, a.dtype) for a in arrs]

    def body(*refs):
        ins, outs = refs[:n], refs[n:2 * n]
        send_sems, recv_sems, loc_sems = refs[2 * n:]
        x, y, c = lax.axis_index("x"), lax.axis_index("y"), lax.axis_index("c")
        me, sibling = (x, y, c), (x, y, 1 - c)
        chips = [(1 - x, y), (x, 1 - y), (1 - x, 1 - y)]

        def copy(k, j, block, to, src=None):
            px, py, pc = block
            dst = outs[k].at[4 * px + 2 * py + pc]
            return pltpu.make_async_remote_copy(src_ref=dst if src is None else src, dst_ref=dst,
                                                send_sem=send_sems.at[k, j], recv_sem=recv_sems.at[k, j],
                                                device_id=to, device_id_type=pl.DeviceIdType.MESH)

        mine = [pltpu.make_async_copy(ins[k], outs[k].at[4 * x + 2 * y + c], loc_sems.at[k]) for k in range(n)]
        first = []
        for k in range(n):
            mine[k].start()
            first.append(copy(k, 0, me, sibling, src=ins[k]))
            first += [copy(k, 1 + j, me, (*chip, c), src=ins[k]) for j, chip in enumerate(chips)]
        for cp in first:
            cp.start()
        passed = []
        for j, chip in enumerate(chips):
            for k in range(n):
                copy(k, 1 + j, (*chip, c), me).wait_recv()
                passed.append(copy(k, 4 + j, (*chip, c), sibling))
                passed[-1].start()
        for k in range(n):
            copy(k, 0, sibling, me).wait_recv()
            for j, chip in enumerate(chips):
                copy(k, 4 + j, (*chip, 1 - c), me).wait_recv()
        for cp in first + passed:
            cp.wait_send()
        for cp in mine:
            cp.wait()

    vmem = pl.BlockSpec(memory_space=pltpu.VMEM)
    return pl.pallas_call(body, name=name, out_shape=out_shape, in_specs=[vmem] * n, out_specs=[vmem] * n,
                          scratch_shapes=[pltpu.SemaphoreType.DMA((n, 7)), pltpu.SemaphoreType.DMA((n, 7)),
                                          pltpu.SemaphoreType.DMA((n,))], compiler_params=_params())(*arrs)


def _exchange(arrs, scatter, name):
    ex = _Exchange(arrs, scatter)

    def body(*refs):
        ins, outs, sems = refs[:ex.n], refs[ex.n:2 * ex.n], refs[2 * ex.n:]
        ex.start(ins, outs, sems)
        ex.wait(ins, outs, sems)

    hbm = pl.BlockSpec(memory_space=pltpu.HBM)
    return pl.pallas_call(body, name=name, out_shape=ex.out_shape, in_specs=[hbm] * ex.n, out_specs=[hbm] * ex.n,
                          scratch_shapes=ex.scratch)(*ex.arrs)


def _matmul(a, b, mode, name, out_dtype=F32, tm=512, tn=1024, tk=1024, a_fn=None, b_fn=None, extras=(),
            out_slabs=None, exchange=None):
    assert not (a_fn and b_fn) and not (b_fn and mode == "NT")
    if mode == "NN":
        (M, K), N = a.shape, b.shape[1]
    elif mode == "NT":
        (M, K), N = a.shape, b.shape[0]
    else:
        (K, M), N = a.shape, b.shape[1]
    slab_w = N // out_slabs if out_slabs else None
    if out_slabs:
        tn = max(slab_w, min(tn, N) // slab_w * slab_w)
    tm, tn, tk = min(tm, M), min(tn, N), min(tk, K)
    assert M % tm == 0 and N % tn == 0 and K % tk == 0, (name, M, N, K)
    nk = K // tk
    dims = {"NN": NN, "NT": NT, "TN": TN}[mode]
    ne = len(extras)

    def body(a_ref, b_ref, *rest):
        e_refs, o_ref, acc_ref = rest[:ne], rest[ne], rest[ne + 1]
        k = pl.program_id(2)

        @pl.when(k == 0)
        def _():
            acc_ref[...] = jnp.zeros_like(acc_ref)

        at, bt = a_ref[...], b_ref[...]
        if a_fn is not None:
            at = a_fn(at.astype(F32), *[e[...] for e in e_refs])
        if b_fn is not None:
            bt = b_fn(bt.astype(F32), *[e[...] for e in e_refs])
        acc_ref[...] += _bdot(at, bt, dims)

        @pl.when(k == nk - 1)
        def _():
            if out_slabs:
                for s in range(tn // slab_w):
                    o_ref[s] = acc_ref[:, s * slab_w:(s + 1) * slab_w].astype(out_dtype)
            else:
                o_ref[...] = acc_ref[...].astype(out_dtype)

    if mode == "TN":
        a_spec = pl.BlockSpec((tk, tm), lambda i, j, k: (k, i))
        e_spec = pl.BlockSpec((1, tm), lambda i, j, k: (0, i))
    else:
        a_spec = pl.BlockSpec((tm, tk), lambda i, j, k: (i, k))
        e_spec = pl.BlockSpec((1, tk), lambda i, j, k: (0, k))
    if mode == "NT":
        b_spec = pl.BlockSpec((tn, tk), lambda i, j, k: (j, k))
    else:
        b_spec = pl.BlockSpec((tk, tn), lambda i, j, k: (k, j))
    if b_fn is not None:
        e_spec = pl.BlockSpec((1, tn), lambda i, j, k: (0, j))
    if out_slabs:
        o_shape = jax.ShapeDtypeStruct((out_slabs, M, slab_w), out_dtype)
        o_spec = pl.BlockSpec((tn // slab_w, tm, slab_w), lambda i, j, k: (j, i, 0))
    else:
        o_shape = jax.ShapeDtypeStruct((M, N), out_dtype)
        o_spec = pl.BlockSpec((tm, tn), lambda i, j, k: (i, j))
    (out,), got = _call(body, name, (a, b, *extras), [o_shape], grid=(M // tm, N // tn, nk),
                        in_specs=[a_spec, b_spec] + [e_spec] * ne, out_specs=[o_spec],
                        scratch_shapes=[pltpu.VMEM((tm, tn), F32)], exchange=exchange)
    return out if exchange is None else (out, got)


def _modulate(x, sc, sh):
    return x * (1.0 + sc) + sh


def _square(x):
    return x * x


def _mod_part(c_all, w_ada_s, b_s):
    def body(c_ref, w_ref, b_ref, mod_ref, cond_ref):
        cv = c_ref[...]
        cond = cv * _sigmoid(cv)
        cond_ref[...] = cond
        mod_ref[...] = _bdot(cond, w_ref[...]) + b_ref[...]

    return pl.pallas_call(
        body, name="mod_part",
        out_shape=[jax.ShapeDtypeStruct((N_DEV, w_ada_s.shape[1]), F32), jax.ShapeDtypeStruct(c_all.shape, F32)],
        compiler_params=_params(),
    )(c_all, w_ada_s, b_s)


def _rms_fwd(x, w):
    rs = lax.rsqrt(_lanemean(x * x) + RMS_EPS)
    return x * rs * w, rs


def _rms_bwd(x, rs, w, dy):
    xhat = x * rs
    dxh = dy * w
    return rs * (dxh - xhat * _lanemean(dxh * xhat)), dy * xhat


def _mla_pre(z, pos_col, invf, m_rot, wq_ext, wkv_ext, qnw, kvnw, exchange=None):
    T = z.shape[0]
    tm = min(ROW_TILE, T)

    def body(z_ref, pos_ref, invf_ref, mrot_ref, wq_ref, wkv_ref, qnw_ref, kvnw_ref,
             q_ref, k_ref, v_ref, c1_ref, s1_ref, cqn_ref, ckvn_ref):
        hi = slice(HEAD_DIM, QK_PAD)
        ang = pos_ref[...].astype(F32) * invf_ref[:, hi]
        c1 = jnp.concatenate([jnp.ones((tm, HEAD_DIM), F32), mrot_ref[:, hi] * jnp.cos(ang)], axis=1)
        s1 = jnp.concatenate([jnp.zeros((tm, HEAD_DIM), F32), mrot_ref[:, hi] * jnp.sin(ang)], axis=1)
        c1_ref[...] = c1
        s1_ref[...] = s1
        cqn, _ = _rms_fwd(z_ref[:, 0:256], qnw_ref[...])
        ckvn, _ = _rms_fwd(z_ref[:, 256:512], kvnw_ref[...])
        cqn_ref[...] = cqn.astype(BF16)
        ckvn_ref[...] = ckvn.astype(BF16)
        qe = _bdot(cqn, wq_ref[...], NT)
        kve = _bdot(ckvn, wkv_ref[...])
        k_rope = z_ref[:, 512:768] * c1 + z_ref[:, 768:1024] * s1
        for h in range(HEADS):
            q_ref[h] = ((qe[:, 256 * h:256 * h + 256] * c1 + qe[:, 1024 + 256 * h:1280 + 256 * h] * s1)
                        * Q_PRESCALE).astype(BF16)
            k_ref[h] = (kve[:, 256 * h:256 * h + 256] + k_rope).astype(BF16)
            v_ref[h] = kve[:, 1024 + 128 * h:1152 + 128 * h].astype(BF16)

    row = lambda i: (i, 0)
    head = lambda i: (0, i, 0)
    return _call(
        body, "mla_pre", (z, pos_col, invf, m_rot, wq_ext, wkv_ext, qnw, kvnw), grid=(T // tm,),
        in_specs=[pl.BlockSpec((tm, 1024), lambda i: (i, 2)), pl.BlockSpec((tm, 1), row),
                  _full((1, 256)), _full((1, 256)), _full(wq_ext.shape), _full(wkv_ext.shape),
                  _full((1, 256)), _full((1, 256))],
        out_specs=[pl.BlockSpec((HEADS, tm, QK_PAD), head), pl.BlockSpec((HEADS, tm, QK_PAD), head),
                   pl.BlockSpec((HEADS, tm, HEAD_DIM), head), pl.BlockSpec((tm, 256), row), pl.BlockSpec((tm, 256), row),
                   pl.BlockSpec((tm, 256), row), pl.BlockSpec((tm, 256), row)],
        out_shape=[jax.ShapeDtypeStruct((HEADS, T, QK_PAD), BF16), jax.ShapeDtypeStruct((HEADS, T, QK_PAD), BF16),
                   jax.ShapeDtypeStruct((HEADS, T, HEAD_DIM), BF16), jax.ShapeDtypeStruct((T, 256), F32),
                   jax.ShapeDtypeStruct((T, 256), F32), jax.ShapeDtypeStruct((T, 256), BF16),
                   jax.ShapeDtypeStruct((T, 256), BF16)], exchange=exchange)


def _mla_bwd(dq, dk, dv, z, c1, s1, wq_ext, wkv_ext, qnw, kvnw):
    T = z.shape[0]
    tm = min(ROW_TILE, T)

    def body(dq_ref, dk_ref, dv_ref, z_ref, c1_ref, s1_ref, wq_ref, wkv_ref, qnw_ref, kvnw_ref,
             dz_ref, dqe_ref, dkve_ref, dnw_ref):
        @pl.when(pl.program_id(0) == 0)
        def _():
            dnw_ref[...] = jnp.zeros_like(dnw_ref)

        c1, s1 = c1_ref[...], s1_ref[...]
        dkpe = jnp.zeros((tm, QK_PAD), F32)
        for h in range(HEADS):
            dqh, dkh = dq_ref[h].astype(F32) * ATT_SCALE, dk_ref[h]
            dqe_ref[:, 256 * h:256 * h + 256] = (dqh * c1).astype(BF16)
            dqe_ref[:, 1024 + 256 * h:1280 + 256 * h] = (dqh * s1).astype(BF16)
            dkve_ref[:, 256 * h:256 * h + 256] = dkh
            dkve_ref[:, 1024 + 128 * h:1152 + 128 * h] = dv_ref[h]
            dkpe = dkpe + dkh.astype(F32)
        dcqn = _dot(dqe_ref[...], wq_ref[...])
        dckvn = _dot(dkve_ref[...], wkv_ref[...], NT)
        cq, ckv = z_ref[:, 0:256], z_ref[:, 256:512]
        _, rsq = _rms_fwd(cq, qnw_ref[...])
        _, rskv = _rms_fwd(ckv, kvnw_ref[...])
        dcq, wq_rows = _rms_bwd(cq, rsq, qnw_ref[...], dcqn)
        dckv, wkv_rows = _rms_bwd(ckv, rskv, kvnw_ref[...], dckvn)
        dnw_ref[:, 0:256] += _rowsum(wq_rows)
        dnw_ref[:, 256:512] += _rowsum(wkv_rows)
        dz_ref[:, 0:256] = dcq.astype(BF16)
        dz_ref[:, 256:512] = dckv.astype(BF16)
        dz_ref[:, 512:768] = (dkpe * c1).astype(BF16)
        dz_ref[:, 768:1024] = (dkpe * s1).astype(BF16)

    row = lambda i: (i, 0)
    head = lambda i: (0, i, 0)
    return pl.pallas_call(
        body, name="mla_bwd", grid=(T // tm,),
        in_specs=[pl.BlockSpec((HEADS, tm, QK_PAD), head), pl.BlockSpec((HEADS, tm, QK_PAD), head),
                  pl.BlockSpec((HEADS, tm, HEAD_DIM), head), pl.BlockSpec((tm, 1024), lambda i: (i, 2)),
                  pl.BlockSpec((tm, 256), row), pl.BlockSpec((tm, 256), row), _full(wq_ext.shape), _full(wkv_ext.shape),
                  _full((1, 256)), _full((1, 256))],
        out_specs=[pl.BlockSpec((tm, 1024), row), pl.BlockSpec((tm, 2048), row), pl.BlockSpec((tm, 1536), row),
                   _full((1, 512))],
        out_shape=[jax.ShapeDtypeStruct((T, 1024), BF16), jax.ShapeDtypeStruct((T, 2048), BF16),
                   jax.ShapeDtypeStruct((T, 1536), BF16), jax.ShapeDtypeStruct((1, 512), F32)],
        compiler_params=_params(),
    )(dq, dk, dv, z, c1, s1, wq_ext, wkv_ext, qnw, kvnw)


_HEAD_LANES = [slice(HEAD_DIM * h, HEAD_DIM * (h + 1)) for h in range(HEADS)]


def _lower_bound(lbraw_ref):
    a0, a1 = lbraw_ref[0:1, :], lbraw_ref[1:2, :]
    mx = jnp.maximum(a0, a1)
    e0, e1 = jnp.exp(a0 - mx), jnp.exp(a1 - mx)
    return e0 / (e0 + e1)


def _tri(lower):
    r = lax.broadcasted_iota(jnp.int32, (CHUNK, CHUNK), 0)
    c = lax.broadcasted_iota(jnp.int32, (CHUNK, CHUNK), 1)
    return (r >= c) if lower else (r <= c)


def _hgrn_gates(q, f, lb, tri_lo):
    sg = _sigmoid(f)
    forget = lb + (1.0 - lb) * sg
    k = 1.0 - forget
    b = _hdot(tri_lo.astype(F32), jnp.log(forget))
    b_ref, b_last = b[CHUNK // 2 - 1:CHUNK // 2, :], b[CHUNK - 1:CHUNK, :]
    e1, e2, e3, e4 = jnp.exp(b - b_ref), jnp.exp(b_ref - b), jnp.exp(b_last - b), jnp.exp(b)
    return dict(sg=sg, forget=forget, k=k, e1=e1, e2=e2, e3=e3, e4=e4, qa=q * e1, ka=k * e2, kl=k * e3, qb=q * e4,
                decay=jnp.exp(b_last))


def _hgrn_fwd(z, lbraw, nw, exchange=None):
    T = z.shape[0]
    G = min(HGRN_GROUP, T // CHUNK)
    rows = G * CHUNK
    n_chunks = T // CHUNK

    def body(q_ref, f_ref, i_ref, g_ref, lbraw_ref, nw_ref, oraw_ref, og_ref, sp_ref, st_ref):
        @pl.when(pl.program_id(0) == 0)
        def _():
            st_ref[...] = jnp.zeros_like(st_ref)

        lb_all = _lower_bound(lbraw_ref)
        tri_lo = _tri(True)

        def chunk(cc, carry):
            rs = pl.ds(pl.multiple_of(cc * CHUNK, CHUNK), CHUNK)
            t = _hgrn_gates(q_ref[rs, :], f_ref[rs, :], lb_all, tri_lo)
            v, gate = i_ref[rs, :], g_ref[rs, :]
            st = [st_ref[h] for h in range(HEADS)]
            a = [jnp.where(tri_lo, _bdot(t["qa"][:, s], t["ka"][:, s], NT), 0.0) for s in _HEAD_LANES]
            kv = [_bdot(v[:, s], t["kl"][:, s], TN) for s in _HEAD_LANES]
            o = [_bdot(a[h], v[:, s]) + _bdot(t["qb"][:, s], st[h], NT) for h, s in enumerate(_HEAD_LANES)]
            for h, s in enumerate(_HEAD_LANES):
                sp_ref[cc, h] = st[h]
                st_ref[h] = st[h] * t["decay"][:, s] + kv[h]
            oraw_ref[rs, :] = jnp.concatenate(o, axis=1)
            on = jnp.concatenate([_rms_fwd(o[h], nw_ref[:, s])[0] for h, s in enumerate(_HEAD_LANES)], axis=1)
            og_ref[rs, :] = (on * (gate * _sigmoid(gate))).astype(BF16)
            return carry

        lax.fori_loop(0, G, chunk, 0, unroll=4)

    col = lambda j: pl.BlockSpec((rows, 512), lambda r, j=j: (r, j))
    return _call(
        body, "hgrn_fwd", (z, z, z, z, lbraw, nw), grid=(T // rows,),
        in_specs=[col(0), col(1), col(2), col(3), _full((2, 512)), _full((1, 512))],
        out_specs=[col(0), col(0), pl.BlockSpec((G, HEADS, HEAD_DIM, HEAD_DIM), lambda r: (r, 0, 0, 0))],
        out_shape=[jax.ShapeDtypeStruct((T, 512), F32), jax.ShapeDtypeStruct((T, 512), BF16),
                   jax.ShapeDtypeStruct((n_chunks, HEADS, HEAD_DIM, HEAD_DIM), F32)],
        scratch_shapes=[pltpu.VMEM((HEADS, HEAD_DIM, HEAD_DIM), F32)], exchange=exchange)


def _hgrn_bwd(dmixcat, z, oraw, sprev, lbraw, nw, exchange=None):
    T = z.shape[0]
    G = min(HGRN_GROUP, T // CHUNK)
    rows = G * CHUNK
    ng = T // rows

    def body(dog_ref, q_ref, f_ref, i_ref, g_ref, oraw_ref, sp_ref, lbraw_ref, nw_ref,
             dz_ref, dsmall_ref, dst_ref):
        @pl.when(pl.program_id(0) == 0)
        def _():
            dst_ref[...] = jnp.zeros_like(dst_ref)
            dsmall_ref[...] = jnp.zeros_like(dsmall_ref)

        lb_all = _lower_bound(lbraw_ref)
        tri_lo, tri_up = _tri(True), _tri(False)
        rowid = lax.broadcasted_iota(jnp.int32, (CHUNK, HEADS * HEAD_DIM), 0)

        def chunk(it, carry):
            cc = G - 1 - it
            rs = pl.ds(pl.multiple_of(cc * CHUNK, CHUNK), CHUNK)
            heads = list(enumerate(_HEAD_LANES))
            cat = lambda parts: jnp.concatenate(parts, axis=1)
            per_head_mean = lambda x: cat([jnp.broadcast_to(_lanemean(x[:, s]), (CHUNK, HEAD_DIM)) for s in _HEAD_LANES])
            t = _hgrn_gates(q_ref[rs, :], f_ref[rs, :], lb_all, tri_lo)
            v, gate, o, dog, nw_all = i_ref[rs, :], g_ref[rs, :], oraw_ref[rs, :], dog_ref[rs, :], nw_ref[...]
            rs_o = lax.rsqrt(per_head_mean(o * o) + RMS_EPS)
            xhat = o * rs_o
            sgg = _sigmoid(gate)
            d_on = dog * (gate * sgg)
            dz_ref[rs, 1536:2048] = (dog * (xhat * nw_all) * (sgg * (1.0 + gate * (1.0 - sgg)))).astype(BF16)
            dxh = d_on * nw_all
            do = rs_o * (dxh - xhat * per_head_mean(dxh * xhat))
            dsmall_ref[:, 512:1024] += _rowsum(d_on * xhat)
            st = [sp_ref[cc, h] for h in range(HEADS)]
            dst = [dst_ref[h] for h in range(HEADS)]
            a = [jnp.where(tri_lo, _bdot(t["qa"][:, s], t["ka"][:, s], NT), 0.0) for s in _HEAD_LANES]
            da = [jnp.where(tri_lo, _bdot(do[:, s], v[:, s], NT), 0.0) for s in _HEAD_LANES]
            dqb = cat([_bdot(do[:, s], st[h]) for h, s in heads])
            dkl = cat([_bdot(v[:, s], dst[h]) for h, s in heads])
            dv_ = cat([_bdot(t["kl"][:, s], dst[h], NT) + _bdot(a[h], do[:, s], TN) for h, s in heads])
            dqa = cat([_bdot(da[h], t["ka"][:, s]) for h, s in heads])
            dka = cat([_bdot(da[h], t["qa"][:, s], TN) for h, s in heads])
            ddecay = cat([_rowsum(dst[h] * st[h]) for h in range(HEADS)])
            for h, s in heads:
                dst_ref[h] = dst[h] * t["decay"][:, s] + _bdot(do[:, s], t["qb"][:, s], TN)
            pa, pk, pb, pl_ = dqa * t["qa"], dka * t["ka"], dqb * t["qb"], dkl * t["kl"]
            db = pa - pk + pb - pl_
            db = db + jnp.where(rowid == CHUNK // 2 - 1, _rowsum(pk - pa), 0.0)
            db = db + jnp.where(rowid == CHUNK - 1, _rowsum(pl_) + ddecay * t["decay"], 0.0)
            dlogf = _hdot(tri_up.astype(F32), db)
            dforget = dlogf / t["forget"] - (dka * t["e2"] + dkl * t["e3"])
            sg = t["sg"]
            dz_ref[rs, 0:512] = (dqa * t["e1"] + dqb * t["e4"]).astype(BF16)
            dz_ref[rs, 512:1024] = (dforget * (1.0 - lb_all) * sg * (1.0 - sg)).astype(BF16)
            dz_ref[rs, 1024:1536] = dv_.astype(BF16)
            dsmall_ref[:, 0:512] += _rowsum(dforget * (1.0 - sg))
            return carry

        lax.fori_loop(0, G, chunk, 0, unroll=4)

    col = lambda j: pl.BlockSpec((rows, 512), lambda r, j=j: (ng - 1 - r, j))
    return _call(
        body, "hgrn_bwd", (dmixcat, z, z, z, z, oraw, sprev, lbraw, nw), grid=(ng,),
        in_specs=[col(0), col(0), col(1), col(2), col(3), col(0),
                  pl.BlockSpec((G, HEADS, HEAD_DIM, HEAD_DIM), lambda r: (ng - 1 - r, 0, 0, 0)),
                  _full((2, 512)), _full((1, 512))],
        out_specs=[pl.BlockSpec((rows, 2048), lambda r: (ng - 1 - r, 0)), _full((1, 1024))],
        out_shape=[jax.ShapeDtypeStruct((T, 2048), BF16), jax.ShapeDtypeStruct((1, 1024), F32)],
        scratch_shapes=[pltpu.VMEM((HEADS, HEAD_DIM, HEAD_DIM), F32)], exchange=exchange)


def _diag_mask(t):
    r = lax.broadcasted_iota(jnp.int32, (t, t), 0)
    c = lax.broadcasted_iota(jnp.int32, (t, t), 1)
    return r >= c


def _attn_fwd(q, k, v, exchange=None):
    _, T, _ = q.shape
    t = min(ATT_TILE, T)

    def body(q_ref, k_ref, v_ref, o_ref, lse_ref):
        i = pl.program_id(1)
        qb = q_ref[...]

        rows = lambda j: pl.ds(pl.multiple_of(j * t, t), t)

        def logits(j, masked):
            s = _dot(qb, k_ref[rows(j), :], NT)
            return jnp.where(_diag_mask(t), s, NEG_BIG) if masked else s

        def absorb(s, j, carry):
            m, l, acc = carry
            mn = jnp.maximum(m, jnp.max(s, axis=-1, keepdims=True))
            p = jnp.exp2(s - mn)
            al = jnp.exp2(m - mn)
            return mn, al * l + jnp.sum(p, axis=-1, keepdims=True), al * acc + _dot(p.astype(BF16), v_ref[rows(j), :])

        def pair(j0, carry, last_masked):
            s0, s1 = logits(j0, False), logits(j0 + 1, last_masked)
            return absorb(s1, j0 + 1, absorb(s0, j0, carry))

        init = (jnp.full((t, 1), NEG_BIG, F32), jnp.zeros((t, 1), F32), jnp.zeros((t, HEAD_DIM), F32))
        carry = lax.fori_loop(0, i // 2, lambda jj, c: pair(2 * jj, c, False), init)
        m, l, acc = lax.cond(i % 2 == 1, lambda c: pair(i - 1, c, True),
                             lambda c: absorb(logits(i, True), i, c), carry)
        o_ref[...] = acc / l
        lse_ref[...] = jnp.broadcast_to(m + jnp.log2(l), (t, HEAD_DIM))

    return _call(
        body, "attn_fwd", (q, k, v), grid=(HEADS, T // t),
        in_specs=[pl.BlockSpec((None, t, QK_PAD), lambda h, i: (h, i, 0)),
                  pl.BlockSpec((None, T, QK_PAD), lambda h, i: (h, 0, 0)),
                  pl.BlockSpec((None, T, HEAD_DIM), lambda h, i: (h, 0, 0))],
        out_specs=[pl.BlockSpec((t, HEAD_DIM), lambda h, i: (i, h)),
                   pl.BlockSpec((None, t, HEAD_DIM), lambda h, i: (h, i, 0))],
        out_shape=[jax.ShapeDtypeStruct((T, HEADS * HEAD_DIM), F32), jax.ShapeDtypeStruct((HEADS, T, HEAD_DIM), F32)],
        exchange=exchange)


def _attn_bwd(q, k, v, dmixcat, o, lse, exchange=None):
    _, T, _ = q.shape
    t = min(ATT_TILE, T)
    nq = T // t

    def body(q_ref, k_ref, v_ref, do_ref, o_ref, lse_ref, dq_ref, dk_ref, dv_ref, delta_ref, dq_acc):
        j = pl.program_id(1)

        @pl.when(j == 0)
        def _():
            dq_acc[...] = jnp.zeros_like(dq_acc)

            def fill(i, carry):
                rs = pl.ds(pl.multiple_of(i * t, t), t)
                delta_ref[rs, :] = jnp.broadcast_to(
                    jnp.sum(do_ref[rs, :] * o_ref[rs, :], axis=-1, keepdims=True), (t, HEAD_DIM))
                return carry

            lax.fori_loop(0, nq, fill, 0)

        kb, vb = k_ref[...], v_ref[...]

        def steps(blocks, carry):
            dk, dv = carry
            rs = [pl.ds(pl.multiple_of(i * t, t), t) for i, _ in blocks]
            qb = [q_ref[r, :] for r in rs]
            dob = [do_ref[r, :].astype(BF16) for r in rs]
            s = [_dot(b, kb, NT) for b in qb]
            dp = [_dot(b, vb, NT) for b in dob]
            for n, (_, masked) in enumerate(blocks):
                p = jnp.exp2(s[n] - lse_ref[rs[n], 0:1])
                if masked:
                    p = jnp.where(_diag_mask(t), p, 0.0)
                ds = (p * (dp[n] - delta_ref[rs[n], 0:1])).astype(BF16)
                dq_acc[rs[n], :] += _dot(ds, kb)
                dk = dk + _dot(ds, qb[n], TN)
                dv = dv + _dot(p.astype(BF16), dob[n], TN)
            return dk, dv

        zero = (jnp.zeros((t, QK_PAD), F32), jnp.zeros((t, HEAD_DIM), F32))
        rest = nq - 1 - j
        carry = lax.cond(rest % 2 == 1, lambda c: steps([(j, True), (j + 1, False)], c),
                         lambda c: steps([(j, True)], c), zero)
        first = j + 1 + rest % 2
        dk, dv = lax.fori_loop(0, rest // 2, lambda n, c: steps([(first + 2 * n, False), (first + 2 * n + 1, False)], c),
                               carry)
        dk_ref[...] = (dk * LN2).astype(BF16)
        dv_ref[...] = dv.astype(BF16)

        @pl.when(j == nq - 1)
        def _():
            dq_ref[...] = dq_acc[...].astype(BF16)

    return _call(
        body, "attn_bwd", (q, k, v, dmixcat, o, lse), grid=(HEADS, nq),
        in_specs=[pl.BlockSpec((None, T, QK_PAD), lambda h, j: (h, 0, 0)),
                  pl.BlockSpec((None, t, QK_PAD), lambda h, j: (h, j, 0)),
                  pl.BlockSpec((None, t, HEAD_DIM), lambda h, j: (h, j, 0)),
                  pl.BlockSpec((T, HEAD_DIM), lambda h, j: (0, HEADS + h)),
                  pl.BlockSpec((T, HEAD_DIM), lambda h, j: (0, h)),
                  pl.BlockSpec((None, T, HEAD_DIM), lambda h, j: (h, 0, 0))],
        out_specs=[pl.BlockSpec((None, T, QK_PAD), lambda h, j: (h, 0, 0)),
                   pl.BlockSpec((None, t, QK_PAD), lambda h, j: (h, j, 0)),
                   pl.BlockSpec((None, t, HEAD_DIM), lambda h, j: (h, j, 0))],
        out_shape=[jax.ShapeDtypeStruct((HEADS, T, QK_PAD), BF16), jax.ShapeDtypeStruct((HEADS, T, QK_PAD), BF16),
                   jax.ShapeDtypeStruct((HEADS, T, HEAD_DIM), BF16)],
        scratch_shapes=[pltpu.VMEM((T, HEAD_DIM), F32), pltpu.VMEM((T, QK_PAD), F32)], exchange=exchange)


def _ln_fwd(r):
    mu = _lanemean(r)
    xc = r - mu
    rstd = lax.rsqrt(_lanemean(xc * xc) + LN_EPS)
    return xc * rstd, rstd


def _ln_bwd(dxh, xhat, rstd):
    return rstd * (dxh - _lanemean(dxh) - xhat * _lanemean(dxh * xhat))


def _mix_ln1(o_hg, o_mla, w_out, x, g_a, ln1_g, ln1_b, sc_m, sh_m, exchange=None):
    T = x.shape[0]
    tm = min(ROW_TILE, T)
    half = o_hg.shape[1]

    def body(hg_ref, mla_ref, w_ref, x_ref, ga_ref, g_ref, b_ref, sc_ref, sh_ref, mix_ref, xhat_ref, rstd_ref, u2_ref):
        mix = _dot(hg_ref[...], w_ref[0:half, :]) + _bdot(mla_ref[...], w_ref[half:, :])
        mix_ref[...] = mix
        xhat, rstd = _ln_fwd(ALPHA * x_ref[...] + (1.0 + ga_ref[...]) * mix)
        xhat_ref[...] = xhat
        rstd_ref[...] = jnp.broadcast_to(rstd, (tm, 128))
        u2_ref[...] = _modulate(xhat * g_ref[...] + b_ref[...], sc_ref[...], sh_ref[...]).astype(BF16)

    row = pl.BlockSpec((tm, D_MODEL), lambda i: (i, 0))
    vec = _full((1, D_MODEL))
    halfrow = pl.BlockSpec((tm, half), lambda i: (i, 0))
    return _call(
        body, "mix_ln1", (o_hg, o_mla, w_out, x, g_a, ln1_g, ln1_b, sc_m, sh_m), grid=(T // tm,),
        in_specs=[halfrow, halfrow, _full(w_out.shape), row, vec, vec, vec, vec, vec],
        out_specs=[row, row, pl.BlockSpec((tm, 128), lambda i: (i, 0)), row],
        out_shape=[jax.ShapeDtypeStruct((T, D_MODEL), F32), jax.ShapeDtypeStruct((T, D_MODEL), F32),
                   jax.ShapeDtypeStruct((T, 128), F32), jax.ShapeDtypeStruct((T, D_MODEL), BF16)],
        exchange=exchange)


def _mlp_fwd(u2, w1, w2, xhat1, ln1_g, ln1_b, g_m, ln2_g, ln2_b, target):
    T = u2.shape[0]
    tf = w1.shape[-1]
    nf = N_DEV // MLP_SLABS
    tm = min(ROW_TILE, T)

    def body(u2_ref, w1_ref, w2_ref, xhat_ref, g1_ref, b1_ref, gm_ref, g2_ref, b2_ref, tgt_ref,
             r_ref, dr2_ref, dh_ref, small_ref, acc_ref):
        i, f = pl.program_id(0), pl.program_id(1)
        dm = D_MODEL

        @pl.when((i == 0) & (f == 0))
        def _():
            small_ref[...] = jnp.zeros_like(small_ref)

        @pl.when(f == 0)
        def _():
            acc_ref[...] = jnp.zeros_like(acc_ref)

        u2t = u2_ref[...]
        part = None
        for s in range(MLP_SLABS):
            r = jnp.maximum(_dot(u2t, w1_ref[s]), 0.0)
            r_ref[:, s * tf:(s + 1) * tf] = r.astype(BF16)
            d = _bdot(r * r, w2_ref[s])
            part = d if part is None else part + d
        acc_ref[...] += part

        @pl.when(f == nf - 1)
        def _():
            h = acc_ref[...]
            x1 = xhat_ref[...] * g1_ref[...] + b1_ref[...]
            xhat2, rstd2 = _ln_fwd(ALPHA * x1 + (1.0 + gm_ref[...]) * h)
            err = xhat2 * g2_ref[...] + b2_ref[...] - tgt_ref[...]
            small_ref[:, 3 * dm:] += jnp.sum(0.5 * _lanemean(err * err), axis=0, keepdims=True)
            dy = err * (1.0 / D_MODEL)
            small_ref[:, dm:2 * dm] += _rowsum(dy * xhat2)
            small_ref[:, 2 * dm:3 * dm] += _rowsum(dy)
            dr2 = _ln_bwd(dy * g2_ref[...], xhat2, rstd2)
            dr2_ref[...] = dr2
            small_ref[:, 0:dm] += _rowsum(dr2 * h)
            dh_ref[...] = ((1.0 + gm_ref[...]) * dr2).astype(BF16)

    row = pl.BlockSpec((tm, D_MODEL), lambda i, f: (i, 0))
    vec = _full((1, D_MODEL))
    return pl.pallas_call(
        body, name="mlp_fwd", grid=(T // tm, nf),
        in_specs=[row, pl.BlockSpec((MLP_SLABS, D_MODEL, tf), lambda i, f: (f, 0, 0)),
                  pl.BlockSpec((MLP_SLABS, tf, D_MODEL), lambda i, f: (f, 0, 0)),
                  row, vec, vec, vec, vec, vec, row],
        out_specs=[pl.BlockSpec((tm, MLP_SLABS * tf), lambda i, f: (i, f)), row, row, _full((1, 3 * D_MODEL + 128))],
        out_shape=[jax.ShapeDtypeStruct((T, N_DEV * tf), BF16), jax.ShapeDtypeStruct((T, D_MODEL), F32),
                   jax.ShapeDtypeStruct((T, D_MODEL), BF16), jax.ShapeDtypeStruct((1, 3 * D_MODEL + 128), F32)],
        scratch_shapes=[pltpu.VMEM((tm, D_MODEL), F32)],
        compiler_params=_params(),
    )(u2, w1, w2, xhat1, ln1_g, ln1_b, g_m, ln2_g, ln2_b, target)


def _mlp_bwd(dh, w1, w2, r, dr2, xhat1, rstd1, mix, ln1_g, ln1_b, sc_m, g_a):
    T = dh.shape[0]
    tf = w1.shape[-1]
    nf = N_DEV // MLP_SLABS
    tm = min(ROW_TILE, T)

    def body(dh_ref, w1_ref, w2_ref, r_ref, dr2_ref, xhat_ref, rstd_ref, mix_ref, g1_ref, b1_ref, sc_ref, ga_ref,
             dhpre_ref, dr1_ref, dmix_ref, small_ref, acc_ref):
        i, f = pl.program_id(0), pl.program_id(1)
        dm = D_MODEL

        @pl.when((i == 0) & (f == 0))
        def _():
            small_ref[...] = jnp.zeros_like(small_ref)

        @pl.when(f == 0)
        def _():
            acc_ref[...] = jnp.zeros_like(acc_ref)

        dht = dh_ref[...]
        part = None
        for s in range(MLP_SLABS):
            cols = slice(s * tf, (s + 1) * tf)
            dhpre = (_dot(dht, w2_ref[s], NT) * (2.0 * r_ref[:, cols].astype(F32))).astype(BF16)
            dhpre_ref[:, cols] = dhpre
            d = _dot(dhpre, w1_ref[s], NT)
            part = d if part is None else part + d
        acc_ref[...] += part

        @pl.when(f == nf - 1)
        def _():
            du2 = acc_ref[...]
            xhat = xhat_ref[...]
            x1 = xhat * g1_ref[...] + b1_ref[...]
            dx1 = ALPHA * dr2_ref[...] + du2 * (1.0 + sc_ref[...])
            small_ref[:, 2 * dm:3 * dm] += _rowsum(du2 * x1)
            small_ref[:, dm:2 * dm] += _rowsum(du2)
            small_ref[:, 3 * dm:4 * dm] += _rowsum(dx1 * xhat)
            small_ref[:, 4 * dm:5 * dm] += _rowsum(dx1)
            dr1 = _ln_bwd(dx1 * g1_ref[...], xhat, rstd_ref[:, 0:1])
            dr1_ref[...] = dr1
            small_ref[:, 0:dm] += _rowsum(dr1 * mix_ref[...])
            dmix_ref[...] = ((1.0 + ga_ref[...]) * dr1).astype(BF16)

    row = pl.BlockSpec((tm, D_MODEL), lambda i, f: (i, 0))
    vec = _full((1, D_MODEL))
    return pl.pallas_call(
        body, name="mlp_bwd", grid=(T // tm, nf),
        in_specs=[row, pl.BlockSpec((MLP_SLABS, D_MODEL, tf), lambda i, f: (f, 0, 0)),
                  pl.BlockSpec((MLP_SLABS, tf, D_MODEL), lambda i, f: (f, 0, 0)),
                  pl.BlockSpec((tm, MLP_SLABS * tf), lambda i, f: (i, f)), row, row,
                  pl.BlockSpec((tm, 128), lambda i, f: (i, 0)), row, vec, vec, vec, vec],
        out_specs=[pl.BlockSpec((tm, MLP_SLABS * tf), lambda i, f: (i, f)), row, row, _full((1, 5 * D_MODEL))],
        out_shape=[jax.ShapeDtypeStruct((T, N_DEV * tf), BF16), jax.ShapeDtypeStruct((T, D_MODEL), F32),
                   jax.ShapeDtypeStruct((T, D_MODEL), BF16), jax.ShapeDtypeStruct((1, 5 * D_MODEL), F32)],
        scratch_shapes=[pltpu.VMEM((tm, D_MODEL), F32)],
        compiler_params=_params(),
    )(dh, w1, w2, r, dr2, xhat1, rstd1, mix, ln1_g, ln1_b, sc_m, g_a)


def _input_bwd(dz_h, dz_m, w_in_ext, x, dr1, sc_a, exchange=None):
    T = x.shape[0]
    tm = min(ROW_TILE, T)

    def body(dzh_ref, dzm_ref, w_ref, x_ref, dr1_ref, sc_ref, gx_ref, small_ref):
        @pl.when(pl.program_id(0) == 0)
        def _():
            small_ref[...] = jnp.zeros_like(small_ref)

        du = _bdot(dzh_ref[...], w_ref[0:2048, :]) + _bdot(dzm_ref[...], w_ref[2048:3072, :])
        gx_ref[...] = ALPHA * dr1_ref[...] + du * (1.0 + sc_ref[...])
        small_ref[:, D_MODEL:] += _rowsum(du * x_ref[...])
        small_ref[:, 0:D_MODEL] += _rowsum(du)

    row = pl.BlockSpec((tm, D_MODEL), lambda i: (i, 0))
    vec = _full((1, D_MODEL))
    return _call(
        body, "input_bwd", (dz_h, dz_m, w_in_ext, x, dr1, sc_a), grid=(T // tm,),
        in_specs=[pl.BlockSpec((tm, 2048), lambda i: (i, 0)), row, _full(w_in_ext.shape), row, row, vec],
        out_specs=[row, _full((1, 2 * D_MODEL))],
        out_shape=[jax.ShapeDtypeStruct((T, D_MODEL), F32), jax.ShapeDtypeStruct((1, 2 * D_MODEL), F32)],
        exchange=exchange)


def _adam_math(w, g, m, v):
    m = ADAM_B1 * m + (1.0 - ADAM_B1) * g
    v = ADAM_B2 * v + (1.0 - ADAM_B2) * (g * g)
    m_hat = m / (1.0 - ADAM_B1 ** ADAM_STEP)
    v_hat = v / (1.0 - ADAM_B2 ** ADAM_STEP)
    return -ADAM_LR * (m_hat / (jnp.sqrt(v_hat) + ADAM_EPS) + ADAM_WD * w), m, v


def _adam(g_slabs, w, m, v, name, g_fn=None, g_extra=()):
    R, C = w.shape
    tr = 256 if R % 256 == 0 else R
    ns = 0 if g_slabs is None else g_slabs.shape[0]
    slab_rows = tr if g_slabs is None or g_slabs.shape[1] == R else g_slabs.shape[1]
    assert slab_rows == tr or tr == R
    ne = len(g_extra)

    def body(*refs):
        e_refs = refs[:ne]
        refs = refs[ne:]
        if ns:
            gs_ref, refs = refs[0], refs[1:]
        w_ref, m_ref, v_ref, g_ref, d_ref, nm_ref, nv_ref = refs
        if g_fn is not None:
            g = g_fn(*e_refs)
        else:
            g = gs_ref[0].astype(F32)
            for s in range(1, ns):
                g = g + gs_ref[s].astype(F32)
            g = g[:tr]
        d, nm, nv = _adam_math(w_ref[...], g, m_ref[...], v_ref[...])
        g_ref[...] = g
        d_ref[...] = d
        nm_ref[...] = nm
        nv_ref[...] = nv

    blk = pl.BlockSpec((tr, C), lambda i: (i, 0))
    in_specs = [pl.BlockSpec((tr, e.shape[1]), lambda i: (i, 0)) if e.shape[0] == R else _full(e.shape) for e in g_extra]
    args = list(g_extra)
    if ns:
        in_specs.append(pl.BlockSpec((ns, slab_rows, C), lambda i: (0, i, 0)))
        args.append(g_slabs)
    return pl.pallas_call(
        body, name=name, grid=(R // tr,), in_specs=in_specs + [blk] * 3, out_specs=[blk] * 4,
        out_shape=[jax.ShapeDtypeStruct((R, C), F32)] * 4, compiler_params=_params(),
    )(*args, w, m, v)


def _adam_small(small_all, params):
    n = len(params)

    def body(*refs):
        s_ref, refs = refs[0], refs[1:]
        wmv, loss_ref, outs = refs[:3 * n], refs[3 * n], refs[3 * n + 1:]
        tot = s_ref[0]
        for i in range(1, N_DEV):
            tot = tot + s_ref[i]
        loss_ref[...] = tot[:, SMALL_W - 128:]
        for j, (w, _, _, off) in enumerate(params):
            w_ref, m_ref, v_ref = wmv[3 * j:3 * j + 3]
            g_ref, d_ref, nm_ref, nv_ref = outs[4 * j:4 * j + 4]
            if w.shape[0] == 2:
                lb = _lower_bound(w_ref)
                g0 = tot[:, off:off + w.shape[1]] * lb * (1.0 - lb)
                rows = [(slice(0, 1), g0), (slice(1, 2), -g0)]
            else:
                rows = [(slice(0, 1), tot[:, off:off + w.shape[1]])]
            for rs, g in rows:
                d, nm, nv = _adam_math(w_ref[rs, :], g, m_ref[rs, :], v_ref[rs, :])
                g_ref[rs, :], d_ref[rs, :], nm_ref[rs, :], nv_ref[rs, :] = g, d, nm, nv

    out_shape = [jax.ShapeDtypeStruct((1, 128), F32)]
    for w, _, _, _ in params:
        out_shape += [jax.ShapeDtypeStruct(w.shape, F32)] * 4
    res = pl.pallas_call(body, name="adam_small", out_shape=out_shape, compiler_params=_params())(
        small_all, *[a for w, m, v, _ in params for a in (w, m, v)])
    return res[0], [tuple(res[1 + 4 * j:5 + 4 * j]) for j in range(n)]


def _cols_from_slabs(g):
    s, r, c = g.shape
    return jnp.transpose(g, (1, 0, 2)).reshape(r, s * c)


def _slabs_from_cols(w):
    r, c = w.shape
    return jnp.transpose(w.reshape(r, N_DEV, c // N_DEV), (1, 0, 2))


def _rot_half_rows(wt):
    return jnp.concatenate([-wt[32:], wt[:32]], axis=0)


def _unrot_half_rows(dwt_rot):
    return jnp.concatenate([dwt_rot[32:], -dwt_rot[:32]], axis=0)


def _ext_in_t(g):
    k_in = g.shape[2]
    z64, z128 = jnp.zeros((64, k_in), BF16), jnp.zeros((128, k_in), BF16)
    wt = g.reshape(N_DEV * g.shape[1], k_in)
    main, wk = wt[:wt.shape[0] - ROPE_DIM], wt[wt.shape[0] - ROPE_DIM:]
    return jnp.concatenate([main, z128, wk, z64, z128, _rot_half_rows(wk), z64], axis=0)


def _ext_q_t(wt):
    r = wt.shape[1]
    z64, z128 = jnp.zeros((64, r), BF16), jnp.zeros((128, r), BF16)
    per = HEAD_DIM + ROPE_DIM
    main = [jnp.concatenate([wt[per * h:per * (h + 1)], z64], axis=0) for h in range(HEADS)]
    rot = [jnp.concatenate([z128, _rot_half_rows(wt[per * h + HEAD_DIM:per * (h + 1)]), z64], axis=0)
           for h in range(HEADS)]
    return jnp.concatenate(main + rot, axis=0)


def _ext_kv(w_kv_up):
    r = w_kv_up.shape[0]
    z128 = jnp.zeros((r, 128), BF16)
    wkv = w_kv_up.reshape(r, HEADS, 2 * HEAD_DIM)
    kpad = [jnp.concatenate([wkv[:, h, :HEAD_DIM], z128], axis=1) for h in range(HEADS)]
    vals = [wkv[:, h, HEAD_DIM:] for h in range(HEADS)]
    return jnp.concatenate(kpad + vals, axis=1)


def _grad_in_from_ext_t(dwt_h, dwt_m):
    dwk = dwt_m[512 + 128:512 + 192] + _unrot_half_rows(dwt_m[768 + 128:768 + 192])
    return jnp.concatenate([dwt_h, dwt_m[:512], dwk], axis=0)


def _grad_q_from_ext_t(dwq_ext_t):
    rows = []
    for h in range(HEADS):
        main, rot = dwq_ext_t[256 * h:256 * h + 256], dwq_ext_t[1024 + 256 * h:1280 + 256 * h]
        rows += [main[:128], main[128:192] + _unrot_half_rows(rot[128:192])]
    return jnp.concatenate(rows, axis=0)


def _grad_kv_from_ext(dwkv_ext):
    kvcols = []
    for h in range(HEADS):
        kvcols += [dwkv_ext[:, 256 * h:256 * h + 128], dwkv_ext[:, 1024 + 128 * h:1152 + 128 * h]]
    return jnp.concatenate(kvcols, axis=1)


SMALL_W = 6144 + 512 + 512 + 256 + 256 + 4 * 1024 + 128


def kernel(x, c, positions, w_ada, b_ada, w_in, hg_lower_bounds, hg_norm_w, mla_q_norm_w, w_q_up, mla_kv_norm_w, w_kv_up, w_out, ln1_g, ln1_b, w_mlp_in, w_mlp_out, ln2_g, ln2_b, loss_target, m_w_ada, m_b_ada, m_w_in, m_hg_lower_bounds, m_hg_norm_w, m_mla_q_norm_w, m_w_q_up, m_mla_kv_norm_w, m_w_kv_up, m_w_out, m_ln1_g, m_ln1_b, m_w_mlp_in, m_w_mlp_out, m_ln2_g, m_ln2_b, v_w_ada, v_b_ada, v_w_in, v_hg_lower_bounds, v_hg_norm_w, v_mla_q_norm_w, v_w_q_up, v_mla_kv_norm_w, v_w_kv_up, v_w_out, v_ln1_g, v_ln1_b, v_w_mlp_in, v_w_mlp_out, v_ln2_g, v_ln2_b):
    T = x.shape[1]
    me = 4 * lax.axis_index("x") + 2 * lax.axis_index("y") + lax.axis_index("c")
    xs, tgt = x[0], loss_target[0]
    transposed = ("w_in", "w_q_up")
    as_used = lambda n, a: a[0].T if n in transposed else a[0]
    big = {n: as_used(n, a) for n, a in dict(w_in=w_in, w_q_up=w_q_up, w_kv_up=w_kv_up, w_out=w_out,
                                              w_mlp_in=w_mlp_in, w_mlp_out=w_mlp_out).items()}
    names = list(big)

    bf = {n: big[n].astype(BF16) for n in names}
    g_in, g_c = _gather_two_level([bf["w_in"], c], name="gather_w_in")
    c_all = g_c.reshape(N_DEV, D_MODEL)

    ada_cols = w_ada.shape[2]
    mod_part, cond = _mod_part(c_all, w_ada[0], lax.dynamic_slice(b_ada, (0, me * ada_cols), (1, ada_cols)))
    (mod_all,) = _exchange([mod_part], scatter=False, name="gather_mod")
    mod_row = lax.dynamic_slice(mod_all, (0, me, 0), (N_DEV, 1, ada_cols)).reshape(1, N_DEV * ada_cols)
    sh_a, sc_a, g_a, sh_m, sc_m, g_m = [mod_row[:, D_MODEL * i:D_MODEL * (i + 1)] for i in range(6)]

    w_in_ext = _ext_in_t(g_in)
    z, (g_q, g_kv, g_out) = _matmul(xs, w_in_ext, "NT", "in_proj", a_fn=_modulate, extras=(sc_a, sh_a), tn=3072,
                                    exchange=_Exchange([bf["w_q_up"], bf["w_kv_up"], bf["w_out"]], False))
    wq_ext = _ext_q_t(g_q.reshape(N_DEV * g_q.shape[1], g_q.shape[2]))
    wkv_ext = _ext_kv(_cols_from_slabs(g_kv))
    w_out_full = g_out.reshape(D_MODEL, D_MODEL)
    inv_freq = 1.0 / (ROPE_THETA ** (jnp.arange(0, ROPE_DIM, 2, dtype=F32) / ROPE_DIM))
    zeros = lambda n: jnp.zeros((n,), F32)
    invf = jnp.concatenate([zeros(128), inv_freq, inv_freq, zeros(64)]).reshape(1, QK_PAD)
    m_rot = jnp.concatenate([zeros(128), jnp.ones((64,), F32), zeros(64)]).reshape(1, QK_PAD)
    q, k, v, c1, s1, cqn, ckvn = _mla_pre(z, positions.reshape(T, 1), invf, m_rot, wq_ext, wkv_ext,
                                          mla_q_norm_w, mla_kv_norm_w)[0]
    (o_raw, o_gated, s_prev), (w1,) = _hgrn_fwd(z, hg_lower_bounds, hg_norm_w,
                                                exchange=_StagedGather(bf["w_mlp_in"]))
    (o_mla, lse), (w2,) = _attn_fwd(q, k, v, exchange=_StagedGather(bf["w_mlp_out"]))
    mix, xhat1, rstd1, u2 = _mix_ln1(o_gated, o_mla, w_out_full, xs, g_a, ln1_g, ln1_b, sc_m, sh_m)[0]
    r, dr2, dh, small_mlp_fwd = _mlp_fwd(u2, w1, w2, xhat1, ln1_g, ln1_b, g_m, ln2_g, ln2_b, tgt)

    dhpre, dr1, dmix, small_mlp_bwd = _mlp_bwd(dh, w1, w2, r, dr2, xhat1, rstd1, mix, ln1_g, ln1_b, sc_m, g_a)
    received = {}
    dw2 = _matmul(r, dh, "TN", "wgrad_mlp_out", out_dtype=BF16, a_fn=_square, tm=1024, tk=2048)
    dw1 = _matmul(u2, dhpre, "TN", "wgrad_mlp_in", out_dtype=BF16, tm=1024, tk=2048, out_slabs=N_DEV)
    dmixcat = _matmul(dmix, w_out_full, "NT", "dgrad_out")
    dw_out = jnp.concatenate([_matmul(o_gated, dmix, "TN", "wgrad_out_hg", out_dtype=BF16, tk=2048),
                              _matmul(o_mla, dmix, "TN", "wgrad_out_mla", out_dtype=BF16, tk=2048)], axis=0)
    (dz_h, small_hgrn), (received["w_out"],) = _hgrn_bwd(
        dmixcat, z, o_raw, s_prev, hg_lower_bounds, hg_norm_w,
        exchange=_Exchange([dw_out.reshape(N_DEV, D_MODEL // N_DEV, D_MODEL)], True))
    (dq, dk, dv), (received["w_mlp_in"], received["w_mlp_out"]) = _attn_bwd(
        q, k, v, dmixcat, o_mla, lse,
        exchange=_Exchange([dw1, dw2.reshape(N_DEV, dw2.shape[0] // N_DEV, D_MODEL)], True))
    dz_m, dq_ext, dkv_ext, small_mla = _mla_bwd(dq, dk, dv, z, c1, s1, wq_ext, wkv_ext, mla_q_norm_w, mla_kv_norm_w)
    dwq_t = _grad_q_from_ext_t(_matmul(dq_ext, cqn, "TN", "wgrad_q_up", tm=1024, tk=2048))
    dwkv = _grad_kv_from_ext(_matmul(ckvn, dkv_ext, "TN", "wgrad_kv_up", tn=1536, tk=2048))
    qkv_slabs = [dwq_t.reshape((N_DEV, dwq_t.shape[0] // N_DEV, dwq_t.shape[1])).astype(BF16),
                 _slabs_from_cols(dwkv).astype(BF16)]
    dwt_h, (received["w_q_up"], received["w_kv_up"]) = _matmul(
        dz_h, xs, "TN", "wgrad_in_h", b_fn=_modulate, extras=(sc_a, sh_a), tm=1024, tk=2048,
        exchange=_Exchange(qkv_slabs, True))
    dwt_m = _matmul(dz_m, xs, "TN", "wgrad_in_m", b_fn=_modulate, extras=(sc_a, sh_a), tm=1024, tk=2048)
    dw_in_t = _grad_in_from_ext_t(dwt_h, dwt_m)
    in_slabs = dw_in_t.reshape((N_DEV, dw_in_t.shape[0] // N_DEV, dw_in_t.shape[1]))
    in_slabs = jnp.pad(in_slabs, ((0, 0), (0, -in_slabs.shape[1] % 16), (0, 0))).astype(BF16)
    (grad_x, small_in), (received["w_in"],) = _input_bwd(
        dz_h, dz_m, w_in_ext, xs, dr1, sc_a, exchange=_StagedScatter(in_slabs))

    small = jnp.concatenate([small_in, small_mlp_bwd[:, :3 * D_MODEL], small_mlp_fwd[:, :D_MODEL], small_hgrn,
                             small_mla, small_mlp_bwd[:, 3 * D_MODEL:], small_mlp_fwd[:, D_MODEL:]], axis=1)
    assert small.shape == (1, SMALL_W)
    (small_all,) = _exchange([small], scatter=False, name="gather_small")

    moments = dict(w_in=(m_w_in, v_w_in), w_q_up=(m_w_q_up, v_w_q_up), w_kv_up=(m_w_kv_up, v_w_kv_up),
                   w_out=(m_w_out, v_w_out), w_mlp_in=(m_w_mlp_in, v_w_mlp_in), w_mlp_out=(m_w_mlp_out, v_w_mlp_out))
    res = {}
    for n in names:
        res[n] = _adam(received[n], big[n], as_used(n, moments[n][0]), as_used(n, moments[n][1]), name="adam_" + n)
    dmod_cols = lax.dynamic_slice(small_all.reshape(N_DEV, SMALL_W), (0, me * ada_cols), (N_DEV, ada_cols))
    cond_t = cond.T

    def ada_grad(ct_ref, dm_ref):
        g = ct_ref[:, 0:1] * dm_ref[0:1, :]
        for b in range(1, N_DEV):
            g = g + ct_ref[:, b:b + 1] * dm_ref[b:b + 1, :]
        return g

    res["w_ada"] = _adam(None, w_ada[0], m_w_ada[0], v_w_ada[0], name="adam_w_ada", g_fn=ada_grad,
                         g_extra=(cond_t, dmod_cols))

    small_params = [("b_ada", b_ada, m_b_ada, v_b_ada, 0),
                    ("hg_lower_bounds", hg_lower_bounds, m_hg_lower_bounds, v_hg_lower_bounds, 6144),
                    ("hg_norm_w", hg_norm_w, m_hg_norm_w, v_hg_norm_w, 6656),
                    ("mla_q_norm_w", mla_q_norm_w, m_mla_q_norm_w, v_mla_q_norm_w, 7168),
                    ("mla_kv_norm_w", mla_kv_norm_w, m_mla_kv_norm_w, v_mla_kv_norm_w, 7424),
                    ("ln1_g", ln1_g, m_ln1_g, v_ln1_g, 7680), ("ln1_b", ln1_b, m_ln1_b, v_ln1_b, 8704),
                    ("ln2_g", ln2_g, m_ln2_g, v_ln2_g, 9728), ("ln2_b", ln2_b, m_ln2_b, v_ln2_b, 10752)]
    loss_row, small_res = _adam_small(small_all, [p[1:] for p in small_params])
    for p, r4 in zip(small_params, small_res):
        res[p[0]] = r4
    loss = loss_row[0, 0]

    order = ["w_ada", "b_ada", "w_in", "hg_lower_bounds", "hg_norm_w", "mla_q_norm_w", "w_q_up", "mla_kv_norm_w",
             "w_kv_up", "w_out", "ln1_g", "ln1_b", "w_mlp_in", "w_mlp_out", "ln2_g", "ln2_b"]
    def as_given(n, a):
        if n in transposed:
            a = a.T
        return a[None] if n in big or n == "w_ada" else a

    shaped = {n: tuple(as_given(n, a) for a in res[n]) for n in order}
    outs = [loss, grad_x.reshape(1, T, D_MODEL)]
    for i in range(4):
        outs += [shaped[n][i] for n in order]
    return tuple(outs)
```

```python
import functools

import jax
import jax.numpy as jnp
import numpy as np
from jax import lax
from jax.experimental import pallas as pl
from jax.experimental.pallas import tpu as pltpu

F32, BF16 = jnp.float32, jnp.bfloat16
N_DEV = 8
D_MODEL = 1024
HEADS = 4
HEAD_DIM = 128
ROPE_DIM = 64
QK_PAD = 256
CHUNK = 64
ROPE_THETA = 10000.0
RMS_EPS = 1e-6
LN_EPS = 1e-5
ALPHA = 2.0 ** 0.25
ATT_SCALE = (HEAD_DIM + ROPE_DIM) ** -0.5
LN2 = float(np.log(2.0))
Q_PRESCALE = ATT_SCALE / LN2
ADAM_LR, ADAM_B1, ADAM_B2, ADAM_EPS, ADAM_WD, ADAM_STEP = 0.001, 0.9, 0.999, 1e-08, 0.01, 10
NEG_BIG = -1e30

ROW_TILE = 512
ATT_TILE = 512
HGRN_GROUP = 8
MLP_SLABS = 4
VMEM_LIMIT = 56 * 2 ** 20

NN = (((1,), (0,)), ((), ()))
NT = (((1,), (1,)), ((), ()))
TN = (((0,), (0,)), ((), ()))


def _dot(a, b, dims=NN):
    return lax.dot_general(a, b, dims, preferred_element_type=F32)


def _bdot(a, b, dims=NN):
    return lax.dot_general(a.astype(BF16), b.astype(BF16), dims, preferred_element_type=F32)


def _hdot(a, b, dims=NN):
    return lax.dot_general(a, b, dims, precision=lax.Precision.HIGHEST, preferred_element_type=F32)


def _params():
    return pltpu.CompilerParams(vmem_limit_bytes=VMEM_LIMIT)


def _sigmoid(x):
    return 1.0 / (1.0 + jnp.exp(-x))


def _rowsum(x):
    return jnp.sum(x, axis=0, keepdims=True)


def _lanemean(x):
    return jnp.mean(x, axis=-1, keepdims=True)


def _full(shape):
    nd = len(shape)
    return pl.BlockSpec(shape, lambda *_: (0,) * nd)


class _Exchange:
    def __init__(self, arrs, scatter):
        self.arrs, self.scatter, self.n, self.aliases, self.middle_at = list(arrs), scatter, len(arrs), [], 0.5
        self.out_shape = [jax.ShapeDtypeStruct((N_DEV,) + (a.shape[1:] if scatter else a.shape), a.dtype)
                          for a in self.arrs]
        n = self.n
        self.scratch = [pltpu.SemaphoreType.DMA((n, N_DEV - 1)), pltpu.SemaphoreType.DMA((n, N_DEV - 1)),
                        pltpu.SemaphoreType.DMA((n,))]

    def _copies(self, ins, outs, sems):
        send_sems, recv_sems, loc_sems = sems
        x, y, c = lax.axis_index("x"), lax.axis_index("y"), lax.axis_index("c")
        me = 4 * x + 2 * y + c
        copies = []
        for k in range(self.n):
            src_of = (lambda i, k=k: ins[k].at[i]) if self.scatter else (lambda i, k=k: ins[k])
            copies.append((pltpu.make_async_copy(src_of(me), outs[k].at[me], loc_sems.at[k]), None))
            for p in range(1, N_DEV):
                px = (1 - x) if p & 4 else x
                py = (1 - y) if p & 2 else y
                pc = (1 - c) if p & 1 else c
                peer = 4 * px + 2 * py + pc
                both = dict(send_sem=send_sems.at[k, p - 1], recv_sem=recv_sems.at[k, p - 1],
                            device_id=(px, py, pc), device_id_type=pl.DeviceIdType.MESH)
                send = pltpu.make_async_remote_copy(src_ref=src_of(peer), dst_ref=outs[k].at[me], **both)
                recv = pltpu.make_async_remote_copy(src_ref=src_of(peer), dst_ref=outs[k].at[peer], **both)
                copies.append((send, recv))
        return copies

    def start(self, ins, outs, sems):
        for first, _ in self._copies(ins, outs, sems):
            first.start()

    def middle(self, ins, outs, sems):
        pass

    def wait(self, ins, outs, sems):
        for first, recv in self._copies(ins, outs, sems):
            if recv is None:
                first.wait()
            else:
                recv.wait_recv()
                first.wait_send()


class _StagedGather:
    def __init__(self, arr, middle_at=0.8):
        self.arrs, self.aliases, self.middle_at = [arr], [], middle_at
        self.out_shape = [jax.ShapeDtypeStruct((N_DEV,) + arr.shape, arr.dtype)]
        self.scratch = [pltpu.VMEM((N_DEV,) + arr.shape, arr.dtype), pltpu.SemaphoreType.DMA((7,)),
                        pltpu.SemaphoreType.DMA((7,)), pltpu.SemaphoreType.DMA((2,))]

    def _parts(self, scr):
        stage, send_sems, recv_sems, loc_sems = scr
        x, y, c = lax.axis_index("x"), lax.axis_index("y"), lax.axis_index("c")
        me, sibling = (x, y, c), (x, y, 1 - c)
        chips = [(1 - x, y), (x, 1 - y), (1 - x, 1 - y)]

        def copy(j, block, to):
            px, py, pc = block
            slot = stage.at[4 * px + 2 * py + pc]
            return pltpu.make_async_remote_copy(src_ref=slot, dst_ref=slot, send_sem=send_sems.at[j],
                                                recv_sem=recv_sems.at[j], device_id=to,
                                                device_id_type=pl.DeviceIdType.MESH)

        return stage, loc_sems, me, sibling, chips, c, copy

    def start(self, ins, outs, scr):
        stage, loc_sems, me, sibling, chips, c, copy = self._parts(scr)
        x, y, _ = me
        own = pltpu.make_async_copy(ins[0], stage.at[4 * x + 2 * y + c], loc_sems.at[0])
        own.start()
        own.wait()
        copy(0, me, sibling).start()
        for j, chip in enumerate(chips):
            copy(1 + j, me, (*chip, c)).start()

    def middle(self, ins, outs, scr):
        stage, loc_sems, me, sibling, chips, c, copy = self._parts(scr)
        for j, chip in enumerate(chips):
            copy(1 + j, (*chip, c), me).wait_recv()
            copy(4 + j, (*chip, c), sibling).start()

    def wait(self, ins, outs, scr):
        stage, loc_sems, me, sibling, chips, c, copy = self._parts(scr)
        copy(0, sibling, me).wait_recv()
        for j, chip in enumerate(chips):
            copy(4 + j, (*chip, 1 - c), me).wait_recv()
        copy(0, me, sibling).wait_send()
        for j, chip in enumerate(chips):
            copy(1 + j, me, (*chip, c)).wait_send()
            copy(4 + j, (*chip, c), sibling).wait_send()
        whole = pltpu.make_async_copy(stage, outs[0], loc_sems.at[1])
        whole.start()
        whole.wait()


class _StagedScatter:
    def __init__(self, slabs, middle_at=0.2):
        _, r, c = slabs.shape
        self.arrs, self.aliases, self.middle_at = [slabs], [], middle_at
        self.out_shape = [jax.ShapeDtypeStruct((4, r, c), slabs.dtype)]
        self.scratch = [pltpu.VMEM((N_DEV, r, c), slabs.dtype), pltpu.VMEM((4, r, c), slabs.dtype),
                        pltpu.VMEM((3, r, c), slabs.dtype), pltpu.SemaphoreType.DMA((4,)), pltpu.SemaphoreType.DMA((4,)),
                        pltpu.SemaphoreType.DMA((3,)), pltpu.SemaphoreType.DMA((3,)), pltpu.SemaphoreType.DMA((4,))]

    def _parts(self, scr):
        stage, from_sib, from_chips, sib_send, sib_recv, ici_send, ici_recv, loc_sems = scr
        x, y, c = lax.axis_index("x"), lax.axis_index("y"), lax.axis_index("c")
        chips = [(1 - x, y), (x, 1 - y), (1 - x, 1 - y)]

        def to_sibling(j):
            return pltpu.make_async_remote_copy(src_ref=stage.at[2 * j + 1 - c], dst_ref=from_sib.at[j],
                                                send_sem=sib_send.at[j], recv_sem=sib_recv.at[j],
                                                device_id=(x, y, 1 - c), device_id_type=pl.DeviceIdType.MESH)

        def to_chip(k):
            px, py = chips[k]
            return pltpu.make_async_remote_copy(src_ref=stage.at[4 * px + 2 * py + c], dst_ref=from_chips.at[k],
                                                send_sem=ici_send.at[k], recv_sem=ici_recv.at[k],
                                                device_id=(px, py, c), device_id_type=pl.DeviceIdType.MESH)

        return stage, from_sib, from_chips, loc_sems, (x, y, c), chips, to_sibling, to_chip

    def start(self, ins, outs, scr):
        stage, _, _, loc_sems, _, _, to_sibling, _ = self._parts(scr)
        load = pltpu.make_async_copy(ins[0], stage, loc_sems.at[0])
        load.start()
        load.wait()
        for j in range(4):
            to_sibling(j).start()

    def middle(self, ins, outs, scr):
        stage, from_sib, _, _, (x, y, c), _, to_sibling, to_chip = self._parts(scr)
        for j in range(4):
            to_sibling(j).wait_recv()
            mine = stage.at[2 * j + c]
            mine[...] = (mine[...].astype(F32) + from_sib[j].astype(F32)).astype(mine.dtype)
        for k in range(3):
            to_chip(k).start()

    def wait(self, ins, outs, scr):
        stage, _, from_chips, loc_sems, (x, y, c), chips, to_sibling, to_chip = self._parts(scr)
        writes = [pltpu.make_async_copy(stage.at[4 * x + 2 * y + c], outs[0].at[2 * x + y], loc_sems.at[0])]
        for k, (px, py) in enumerate(chips):
            to_chip(k).wait_recv()
            writes.append(pltpu.make_async_copy(from_chips.at[k], outs[0].at[2 * px + py], loc_sems.at[1 + k]))
        for w in writes:
            w.start()
        for j in range(4):
            to_sibling(j).wait_send()
        for k in range(3):
            to_chip(k).wait_send()
        for w in writes:
            w.wait()


def _call(body, name, args, out_shape, grid=(), in_specs=(), out_specs=(), scratch_shapes=(), exchange=None):
    if exchange is None:
        return pl.pallas_call(body, name=name, grid=grid, in_specs=list(in_specs), out_specs=list(out_specs),
                              out_shape=list(out_shape), scratch_shapes=list(scratch_shapes),
                              compiler_params=_params())(*args), None
    exs = list(exchange) if isinstance(exchange, (list, tuple)) else [exchange]
    ni, no, ns = len(args), len(out_shape), len(scratch_shapes)
    nxi, nxo = sum(len(e.arrs) for e in exs), sum(len(e.out_shape) for e in exs)
    steps = int(np.prod(grid))
    mid_step = lambda e: min(max(int(steps * e.middle_at), 1), steps - 1)
    aliases, iat, oat = {}, ni, no
    for e in exs:
        for src, dst in e.aliases:
            aliases[iat + src] = oat + dst
        iat, oat = iat + len(e.arrs), oat + len(e.out_shape)

    def wrapped(*refs):
        a, xi = refs[:ni], refs[ni:ni + nxi]
        o, xo = refs[ni + nxi:ni + nxi + no], refs[ni + nxi + no:ni + nxi + no + nxo]
        s, xs = refs[ni + nxi + no + nxo:ni + nxi + no + nxo + ns], refs[ni + nxi + no + nxo + ns:]
        parts, iat, oat, sat = [], 0, 0, 0
        for e in exs:
            parts.append((e, xi[iat:iat + len(e.arrs)], xo[oat:oat + len(e.out_shape)], xs[sat:sat + len(e.scratch)]))
            iat, oat, sat = iat + len(e.arrs), oat + len(e.out_shape), sat + len(e.scratch)
        step = 0
        for d, g in enumerate(grid):
            step = step * g + pl.program_id(d)

        @pl.when(step == 0)
        def _():
            for e, ins, outs, sems in parts:
                e.start(ins, outs, sems)

        for at_step in sorted({mid_step(e) for e in exs}):
            @pl.when(step == at_step)
            def _():
                for e, ins, outs, sems in parts:
                    if mid_step(e) == at_step:
                        e.middle(ins, outs, sems)

        body(*a, *o, *s)

        @pl.when(step == steps - 1)
        def _():
            for e, ins, outs, sems in parts:
                e.wait(ins, outs, sems)

    hbm = pl.BlockSpec(memory_space=pltpu.HBM)
    res = pl.pallas_call(
        wrapped, name=name, grid=grid, in_specs=list(in_specs) + [hbm] * nxi, out_specs=list(out_specs) + [hbm] * nxo,
        out_shape=list(out_shape) + [o_ for e in exs for o_ in e.out_shape],
        scratch_shapes=list(scratch_shapes) + [s_ for e in exs for s_ in e.scratch],
        input_output_aliases=aliases, compiler_params=_params())(*args, *[a_ for e in exs for a_ in e.arrs])
    return res[:no], res[no:]


def _gather_two_level(arrs, name):
    n = len(arrs)
    out_shape = [jax.ShapeDtypeStruct((N_DEV,) + a.shape, a.dtype) for a in arrs]

    def body(*refs):
        ins, outs = refs[:n], refs[n:2 * n]
        send_sems, recv_sems, loc_sems = refs[2 * n:]
        x, y, c = lax.axis_index("x"), lax.axis_index("y"), lax.axis_index("c")
        me, sibling = (x, y, c), (x, y, 1 - c)
        chips = [(1 - x, y), (x, 1 - y), (1 - x, 1 - y)]

        def copy(k, j, block, to, src=None):
            px, py, pc = block
            dst = outs[k].at[4 * px + 2 * py + pc]
            return pltpu.make_async_remote_copy(src_ref=dst if src is None else src, dst_ref=dst,
                                                send_sem=send_sems.at[k, j], recv_sem=recv_sems.at[k, j],
                                                device_id=to, device_id_type=pl.DeviceIdType.MESH)

        mine = [pltpu.make_async_copy(ins[k], outs[k].at[4 * x + 2 * y + c], loc_sems.at[k]) for k in range(n)]
        first = []
        for k in range(n):
            mine[k].start()
            first.append(copy(k, 0, me, sibling, src=ins[k]))
            first += [copy(k, 1 + j, me, (*chip, c), src=ins[k]) for j, chip in enumerate(chips)]
        for cp in first:
            cp.start()
        passed = []
        for j, chip in enumerate(chips):
            for k in range(n):
                copy(k, 1 + j, (*chip, c), me).wait_recv()
                passed.append(copy(k, 4 + j, (*chip, c), sibling))
                passed[-1].start()
        for k in range(n):
            copy(k, 0, sibling, me).wait_recv()
            for j, chip in enumerate(chips):
                copy(k, 4 + j, (*chip, 1 - c), me).wait_recv()
        for cp in first + passed:
            cp.wait_send()
        for cp in mine:
            cp.wait()

    vmem = pl.BlockSpec(memory_space=pltpu.VMEM)
    return pl.pallas_call(body, name=name, out_shape=out_shape, in_specs=[vmem] * n, out_specs=[vmem] * n,
                          scratch_shapes=[pltpu.SemaphoreType.DMA((n, 7)), pltpu.SemaphoreType.DMA((n, 7)),
                                          pltpu.SemaphoreType.DMA((n,))], compiler_params=_params())(*arrs)


def _exchange(arrs, scatter, name):
    ex = _Exchange(arrs, scatter)

    def body(*refs):
        ins, outs, sems = refs[:ex.n], refs[ex.n:2 * ex.n], refs[2 * ex.n:]
        ex.start(ins, outs, sems)
        ex.wait(ins, outs, sems)

    hbm = pl.BlockSpec(memory_space=pltpu.HBM)
    return pl.pallas_call(body, name=name, out_shape=ex.out_shape, in_specs=[hbm] * ex.n, out_specs=[hbm] * ex.n,
                          scratch_shapes=ex.scratch)(*ex.arrs)


def _matmul(a, b, mode, name, out_dtype=F32, tm=512, tn=1024, tk=1024, a_fn=None, b_fn=None, extras=(),
            out_slabs=None, exchange=None):
    assert not (a_fn and b_fn) and not (b_fn and mode == "NT")
    if mode == "NN":
        (M, K), N = a.shape, b.shape[1]
    elif mode == "NT":
        (M, K), N = a.shape, b.shape[0]
    else:
        (K, M), N = a.shape, b.shape[1]
    slab_w = N // out_slabs if out_slabs else None
    if out_slabs:
        tn = max(slab_w, min(tn, N) // slab_w * slab_w)
    tm, tn, tk = min(tm, M), min(tn, N), min(tk, K)
    assert M % tm == 0 and N % tn == 0 and K % tk == 0, (name, M, N, K)
    nk = K // tk
    dims = {"NN": NN, "NT": NT, "TN": TN}[mode]
    ne = len(extras)

    def body(a_ref, b_ref, *rest):
        e_refs, o_ref, acc_ref = rest[:ne], rest[ne], rest[ne + 1]
        k = pl.program_id(2)

        @pl.when(k == 0)
        def _():
            acc_ref[...] = jnp.zeros_like(acc_ref)

        at, bt = a_ref[...], b_ref[...]
        if a_fn is not None:
            at = a_fn(at.astype(F32), *[e[...] for e in e_refs])
        if b_fn is not None:
            bt = b_fn(bt.astype(F32), *[e[...] for e in e_refs])
        acc_ref[...] += _bdot(at, bt, dims)

        @pl.when(k == nk - 1)
        def _():
            if out_slabs:
                for s in range(tn // slab_w):
                    o_ref[s] = acc_ref[:, s * slab_w:(s + 1) * slab_w].astype(out_dtype)
            else:
                o_ref[...] = acc_ref[...].astype(out_dtype)

    if mode == "TN":
        a_spec = pl.BlockSpec((tk, tm), lambda i, j, k: (k, i))
        e_spec = pl.BlockSpec((1, tm), lambda i, j, k: (0, i))
    else:
        a_spec = pl.BlockSpec((tm, tk), lambda i, j, k: (i, k))
        e_spec = pl.BlockSpec((1, tk), lambda i, j, k: (0, k))
    if mode == "NT":
        b_spec = pl.BlockSpec((tn, tk), lambda i, j, k: (j, k))
    else:
        b_spec = pl.BlockSpec((tk, tn), lambda i, j, k: (k, j))
    if b_fn is not None:
        e_spec = pl.BlockSpec((1, tn), lambda i, j, k: (0, j))
    if out_slabs:
        o_shape = jax.ShapeDtypeStruct((out_slabs, M, slab_w), out_dtype)
        o_spec = pl.BlockSpec((tn // slab_w, tm, slab_w), lambda i, j, k: (j, i, 0))
    else:
        o_shape = jax.ShapeDtypeStruct((M, N), out_dtype)
        o_spec = pl.BlockSpec((tm, tn), lambda i, j, k: (i, j))
    (out,), got = _call(body, name, (a, b, *extras), [o_shape], grid=(M // tm, N // tn, nk),
                        in_specs=[a_spec, b_spec] + [e_spec] * ne, out_specs=[o_spec],
                        scratch_shapes=[pltpu.VMEM((tm, tn), F32)], exchange=exchange)
    return out if exchange is None else (out, got)


def _modulate(x, sc, sh):
    return x * (1.0 + sc) + sh


def _square(x):
    return x * x


def _mod_part(c_all, w_ada_s, b_s):
    def body(c_ref, w_ref, b_ref, mod_ref, cond_ref):
        cv = c_ref[...]
        cond = cv * _sigmoid(cv)
        cond_ref[...] = cond
        mod_ref[...] = _bdot(cond, w_ref[...]) + b_ref[...]

    return pl.pallas_call(
        body, name="mod_part",
        out_shape=[jax.ShapeDtypeStruct((N_DEV, w_ada_s.shape[1]), F32), jax.ShapeDtypeStruct(c_all.shape, F32)],
        compiler_params=_params(),
    )(c_all, w_ada_s, b_s)


def _rms_fwd(x, w):
    rs = lax.rsqrt(_lanemean(x * x) + RMS_EPS)
    return x * rs * w, rs


def _rms_bwd(x, rs, w, dy):
    xhat = x * rs
    dxh = dy * w
    return rs * (dxh - xhat * _lanemean(dxh * xhat)), dy * xhat


def _mla_pre(z, pos_col, invf, m_rot, wq_ext, wkv_ext, qnw, kvnw, exchange=None):
    T = z.shape[0]
    tm = min(ROW_TILE, T)

    def body(z_ref, pos_ref, invf_ref, mrot_ref, wq_ref, wkv_ref, qnw_ref, kvnw_ref,
             q_ref, k_ref, v_ref, c1_ref, s1_ref, cqn_ref, ckvn_ref):
        hi = slice(HEAD_DIM, QK_PAD)
        ang = pos_ref[...].astype(F32) * invf_ref[:, hi]
        c1 = jnp.concatenate([jnp.ones((tm, HEAD_DIM), F32), mrot_ref[:, hi] * jnp.cos(ang)], axis=1)
        s1 = jnp.concatenate([jnp.zeros((tm, HEAD_DIM), F32), mrot_ref[:, hi] * jnp.sin(ang)], axis=1)
        c1_ref[...] = c1
        s1_ref[...] = s1
        cqn, _ = _rms_fwd(z_ref[:, 0:256], qnw_ref[...])
        ckvn, _ = _rms_fwd(z_ref[:, 256:512], kvnw_ref[...])
        cqn_ref[...] = cqn.astype(BF16)
        ckvn_ref[...] = ckvn.astype(BF16)
        qe = _bdot(cqn, wq_ref[...], NT)
        kve = _bdot(ckvn, wkv_ref[...])
        k_rope = z_ref[:, 512:768] * c1 + z_ref[:, 768:1024] * s1
        for h in range(HEADS):
            q_ref[h] = ((qe[:, 256 * h:256 * h + 256] * c1 + qe[:, 1024 + 256 * h:1280 + 256 * h] * s1)
                        * Q_PRESCALE).astype(BF16)
            k_ref[h] = (kve[:, 256 * h:256 * h + 256] + k_rope).astype(BF16)
            v_ref[h] = kve[:, 1024 + 128 * h:1152 + 128 * h].astype(BF16)

    row = lambda i: (i, 0)
    head = lambda i: (0, i, 0)
    return _call(
        body, "mla_pre", (z, pos_col, invf, m_rot, wq_ext, wkv_ext, qnw, kvnw), grid=(T // tm,),
        in_specs=[pl.BlockSpec((tm, 1024), lambda i: (i, 2)), pl.BlockSpec((tm, 1), row),
                  _full((1, 256)), _full((1, 256)), _full(wq_ext.shape), _full(wkv_ext.shape),
                  _full((1, 256)), _full((1, 256))],
        out_specs=[pl.BlockSpec((HEADS, tm, QK_PAD), head), pl.BlockSpec((HEADS, tm, QK_PAD), head),
                   pl.BlockSpec((HEADS, tm, HEAD_DIM), head), pl.BlockSpec((tm, 256), row), pl.BlockSpec((tm, 256), row),
                   pl.BlockSpec((tm, 256), row), pl.BlockSpec((tm, 256), row)],
        out_shape=[jax.ShapeDtypeStruct((HEADS, T, QK_PAD), BF16), jax.ShapeDtypeStruct((HEADS, T, QK_PAD), BF16),
                   jax.ShapeDtypeStruct((HEADS, T, HEAD_DIM), BF16), jax.ShapeDtypeStruct((T, 256), F32),
                   jax.ShapeDtypeStruct((T, 256), F32), jax.ShapeDtypeStruct((T, 256), BF16),
                   jax.ShapeDtypeStruct((T, 256), BF16)], exchange=exchange)


def _mla_bwd(dq, dk, dv, z, c1, s1, wq_ext, wkv_ext, qnw, kvnw):
    T = z.shape[0]
    tm = min(ROW_TILE, T)

    def body(dq_ref, dk_ref, dv_ref, z_ref, c1_ref, s1_ref, wq_ref, wkv_ref, qnw_ref, kvnw_ref,
             dz_ref, dqe_ref, dkve_ref, dnw_ref):
        @pl.when(pl.program_id(0) == 0)
        def _():
            dnw_ref[...] = jnp.zeros_like(dnw_ref)

        c1, s1 = c1_ref[...], s1_ref[...]
        dkpe = jnp.zeros((tm, QK_PAD), F32)
        for h in range(HEADS):
            dqh, dkh = dq_ref[h].astype(F32) * ATT_SCALE, dk_ref[h]
            dqe_ref[:, 256 * h:256 * h + 256] = (dqh * c1).astype(BF16)
            dqe_ref[:, 1024 + 256 * h:1280 + 256 * h] = (dqh * s1).astype(BF16)
            dkve_ref[:, 256 * h:256 * h + 256] = dkh
            dkve_ref[:, 1024 + 128 * h:1152 + 128 * h] = dv_ref[h]
            dkpe = dkpe + dkh.astype(F32)
        dcqn = _dot(dqe_ref[...], wq_ref[...])
        dckvn = _dot(dkve_ref[...], wkv_ref[...], NT)
        cq, ckv = z_ref[:, 0:256], z_ref[:, 256:512]
        _, rsq = _rms_fwd(cq, qnw_ref[...])
        _, rskv = _rms_fwd(ckv, kvnw_ref[...])
        dcq, wq_rows = _rms_bwd(cq, rsq, qnw_ref[...], dcqn)
        dckv, wkv_rows = _rms_bwd(ckv, rskv, kvnw_ref[...], dckvn)
        dnw_ref[:, 0:256] += _rowsum(wq_rows)
        dnw_ref[:, 256:512] += _rowsum(wkv_rows)
        dz_ref[:, 0:256] = dcq.astype(BF16)
        dz_ref[:, 256:512] = dckv.astype(BF16)
        dz_ref[:, 512:768] = (dkpe * c1).astype(BF16)
        dz_ref[:, 768:1024] = (dkpe * s1).astype(BF16)

    row = lambda i: (i, 0)
    head = lambda i: (0, i, 0)
    return pl.pallas_call(
        body, name="mla_bwd", grid=(T // tm,),
        in_specs=[pl.BlockSpec((HEADS, tm, QK_PAD), head), pl.BlockSpec((HEADS, tm, QK_PAD), head),
                  pl.BlockSpec((HEADS, tm, HEAD_DIM), head), pl.BlockSpec((tm, 1024), lambda i: (i, 2)),
                  pl.BlockSpec((tm, 256), row), pl.BlockSpec((tm, 256), row), _full(wq_ext.shape), _full(wkv_ext.shape),
                  _full((1, 256)), _full((1, 256))],
        out_specs=[pl.BlockSpec((tm, 1024), row), pl.BlockSpec((tm, 2048), row), pl.BlockSpec((tm, 1536), row),
                   _full((1, 512))],
        out_shape=[jax.ShapeDtypeStruct((T, 1024), BF16), jax.ShapeDtypeStruct((T, 2048), BF16),
                   jax.ShapeDtypeStruct((T, 1536), BF16), jax.ShapeDtypeStruct((1, 512), F32)],
        compiler_params=_params(),
    )(dq, dk, dv, z, c1, s1, wq_ext, wkv_ext, qnw, kvnw)


_HEAD_LANES = [slice(HEAD_DIM * h, HEAD_DIM * (h + 1)) for h in range(HEADS)]


def _lower_bound(lbraw_ref):
    a0, a1 = lbraw_ref[0:1, :], lbraw_ref[1:2, :]
    mx = jnp.maximum(a0, a1)
    e0, e1 = jnp.exp(a0 - mx), jnp.exp(a1 - mx)
    return e0 / (e0 + e1)


def _tri(lower):
    r = lax.broadcasted_iota(jnp.int32, (CHUNK, CHUNK), 0)
    c = lax.broadcasted_iota(jnp.int32, (CHUNK, CHUNK), 1)
    return (r >= c) if lower else (r <= c)


def _hgrn_gates(q, f, lb, tri_lo):
    sg = _sigmoid(f)
    forget = lb + (1.0 - lb) * sg
    k = 1.0 - forget
    b = _hdot(tri_lo.astype(F32), jnp.log(forget))
    b_ref, b_last = b[CHUNK // 2 - 1:CHUNK // 2, :], b[CHUNK - 1:CHUNK, :]
    e1, e2, e3, e4 = jnp.exp(b - b_ref), jnp.exp(b_ref - b), jnp.exp(b_last - b), jnp.exp(b)
    return dict(sg=sg, forget=forget, k=k, e1=e1, e2=e2, e3=e3, e4=e4, qa=q * e1, ka=k * e2, kl=k * e3, qb=q * e4,
                decay=jnp.exp(b_last))


def _hgrn_fwd(z, lbraw, nw, exchange=None):
    T = z.shape[0]
    G = min(HGRN_GROUP, T // CHUNK)
    rows = G * CHUNK
    n_chunks = T // CHUNK

    def body(q_ref, f_ref, i_ref, g_ref, lbraw_ref, nw_ref, oraw_ref, og_ref, sp_ref, st_ref):
        @pl.when(pl.program_id(0) == 0)
        def _():
            st_ref[...] = jnp.zeros_like(st_ref)

        lb_all = _lower_bound(lbraw_ref)
        tri_lo = _tri(True)

        def chunk(cc, carry):
            rs = pl.ds(pl.multiple_of(cc * CHUNK, CHUNK), CHUNK)
            t = _hgrn_gates(q_ref[rs, :], f_ref[rs, :], lb_all, tri_lo)
            v, gate = i_ref[rs, :], g_ref[rs, :]
            st = [st_ref[h] for h in range(HEADS)]
            a = [jnp.where(tri_lo, _bdot(t["qa"][:, s], t["ka"][:, s], NT), 0.0) for s in _HEAD_LANES]
            kv = [_bdot(v[:, s], t["kl"][:, s], TN) for s in _HEAD_LANES]
            o = [_bdot(a[h], v[:, s]) + _bdot(t["qb"][:, s], st[h], NT) for h, s in enumerate(_HEAD_LANES)]
            for h, s in enumerate(_HEAD_LANES):
                sp_ref[cc, h] = st[h]
                st_ref[h] = st[h] * t["decay"][:, s] + kv[h]
            oraw_ref[rs, :] = jnp.concatenate(o, axis=1)
            on = jnp.concatenate([_rms_fwd(o[h], nw_ref[:, s])[0] for h, s in enumerate(_HEAD_LANES)], axis=1)
            og_ref[rs, :] = (on * (gate * _sigmoid(gate))).astype(BF16)
            return carry

        lax.fori_loop(0, G, chunk, 0, unroll=4)

    col = lambda j: pl.BlockSpec((rows, 512), lambda r, j=j: (r, j))
    return _call(
        body, "hgrn_fwd", (z, z, z, z, lbraw, nw), grid=(T // rows,),
        in_specs=[col(0), col(1), col(2), col(3), _full((2, 512)), _full((1, 512))],
        out_specs=[col(0), col(0), pl.BlockSpec((G, HEADS, HEAD_DIM, HEAD_DIM), lambda r: (r, 0, 0, 0))],
        out_shape=[jax.ShapeDtypeStruct((T, 512), F32), jax.ShapeDtypeStruct((T, 512), BF16),
                   jax.ShapeDtypeStruct((n_chunks, HEADS, HEAD_DIM, HEAD_DIM), F32)],
        scratch_shapes=[pltpu.VMEM((HEADS, HEAD_DIM, HEAD_DIM), F32)], exchange=exchange)


def _hgrn_bwd(dmixcat, z, oraw, sprev, lbraw, nw, exchange=None):
    T = z.shape[0]
    G = min(HGRN_GROUP, T // CHUNK)
    rows = G * CHUNK
    ng = T // rows

    def body(dog_ref, q_ref, f_ref, i_ref, g_ref, oraw_ref, sp_ref, lbraw_ref, nw_ref,
             dz_ref, dsmall_ref, dst_ref):
        @pl.when(pl.program_id(0) == 0)
        def _():
            dst_ref[...] = jnp.zeros_like(dst_ref)
            dsmall_ref[...] = jnp.zeros_like(dsmall_ref)

        lb_all = _lower_bound(lbraw_ref)
        tri_lo, tri_up = _tri(True), _tri(False)
        rowid = lax.broadcasted_iota(jnp.int32, (CHUNK, HEADS * HEAD_DIM), 0)

        def chunk(it, carry):
            cc = G - 1 - it
            rs = pl.ds(pl.multiple_of(cc * CHUNK, CHUNK), CHUNK)
            heads = list(enumerate(_HEAD_LANES))
            cat = lambda parts: jnp.concatenate(parts, axis=1)
            per_head_mean = lambda x: cat([jnp.broadcast_to(_lanemean(x[:, s]), (CHUNK, HEAD_DIM)) for s in _HEAD_LANES])
            t = _hgrn_gates(q_ref[rs, :], f_ref[rs, :], lb_all, tri_lo)
            v, gate, o, dog, nw_all = i_ref[rs, :], g_ref[rs, :], oraw_ref[rs, :], dog_ref[rs, :], nw_ref[...]
            rs_o = lax.rsqrt(per_head_mean(o * o) + RMS_EPS)
            xhat = o * rs_o
            sgg = _sigmoid(gate)
            d_on = dog * (gate * sgg)
            dz_ref[rs, 1536:2048] = (dog * (xhat * nw_all) * (sgg * (1.0 + gate * (1.0 - sgg)))).astype(BF16)
            dxh = d_on * nw_all
            do = rs_o * (dxh - xhat * per_head_mean(dxh * xhat))
            dsmall_ref[:, 512:1024] += _rowsum(d_on * xhat)
            st = [sp_ref[cc, h] for h in range(HEADS)]
            dst = [dst_ref[h] for h in range(HEADS)]
            a = [jnp.where(tri_lo, _bdot(t["qa"][:, s], t["ka"][:, s], NT), 0.0) for s in _HEAD_LANES]
            da = [jnp.where(tri_lo, _bdot(do[:, s], v[:, s], NT), 0.0) for s in _HEAD_LANES]
            dqb = cat([_bdot(do[:, s], st[h]) for h, s in heads])
            dkl = cat([_bdot(v[:, s], dst[h]) for h, s in heads])
            dv_ = cat([_bdot(t["kl"][:, s], dst[h], NT) + _bdot(a[h], do[:, s], TN) for h, s in heads])
            dqa = cat([_bdot(da[h], t["ka"][:, s]) for h, s in heads])
            dka = cat([_bdot(da[h], t["qa"][:, s], TN) for h, s in heads])
            ddecay = cat([_rowsum(dst[h] * st[h]) for h in range(HEADS)])
            for h, s in heads:
                dst_ref[h] = dst[h] * t["decay"][:, s] + _bdot(do[:, s], t["qb"][:, s], TN)
            pa, pk, pb, pl_ = dqa * t["qa"], dka * t["ka"], dqb * t["qb"], dkl * t["kl"]
            db = pa - pk + pb - pl_
            db = db + jnp.where(rowid == CHUNK // 2 - 1, _rowsum(pk - pa), 0.0)
            db = db + jnp.where(rowid == CHUNK - 1, _rowsum(pl_) + ddecay * t["decay"], 0.0)
            dlogf = _hdot(tri_up.astype(F32), db)
            dforget = dlogf / t["forget"] - (dka * t["e2"] + dkl * t["e3"])
            sg = t["sg"]
            dz_ref[rs, 0:512] = (dqa * t["e1"] + dqb * t["e4"]).astype(BF16)
            dz_ref[rs, 512:1024] = (dforget * (1.0 - lb_all) * sg * (1.0 - sg)).astype(BF16)
            dz_ref[rs, 1024:1536] = dv_.astype(BF16)
            dsmall_ref[:, 0:512] += _rowsum(dforget * (1.0 - sg))
            return carry

        lax.fori_loop(0, G, chunk, 0, unroll=4)

    col = lambda j: pl.BlockSpec((rows, 512), lambda r, j=j: (ng - 1 - r, j))
    return _call(
        body, "hgrn_bwd", (dmixcat, z, z, z, z, oraw, sprev, lbraw, nw), grid=(ng,),
        in_specs=[col(0), col(0), col(1), col(2), col(3), col(0),
                  pl.BlockSpec((G, HEADS, HEAD_DIM, HEAD_DIM), lambda r: (ng - 1 - r, 0, 0, 0)),
                  _full((2, 512)), _full((1, 512))],
        out_specs=[pl.BlockSpec((rows, 2048), lambda r: (ng - 1 - r, 0)), _full((1, 1024))],
        out_shape=[jax.ShapeDtypeStruct((T, 2048), BF16), jax.ShapeDtypeStruct((1, 1024), F32)],
        scratch_shapes=[pltpu.VMEM((HEADS, HEAD_DIM, HEAD_DIM), F32)], exchange=exchange)


def _diag_mask(t):
    r = lax.broadcasted_iota(jnp.int32, (t, t), 0)
    c = lax.broadcasted_iota(jnp.int32, (t, t), 1)
    return r >= c


def _attn_fwd(q, k, v, exchange=None):
    _, T, _ = q.shape
    t = min(ATT_TILE, T)

    def body(q_ref, k_ref, v_ref, o_ref, lse_ref):
        i = pl.program_id(1)
        qb = q_ref[...]

        rows = lambda j: pl.ds(pl.multiple_of(j * t, t), t)

        def logits(j, masked):
            s = _dot(qb, k_ref[rows(j), :], NT)
            return jnp.where(_diag_mask(t), s, NEG_BIG) if masked else s

        def absorb(s, j, carry):
            m, l, acc = carry
            mn = jnp.maximum(m, jnp.max(s, axis=-1, keepdims=True))
            p = jnp.exp2(s - mn)
            al = jnp.exp2(m - mn)
            return mn, al * l + jnp.sum(p, axis=-1, keepdims=True), al * acc + _dot(p.astype(BF16), v_ref[rows(j), :])

        def pair(j0, carry, last_masked):
            s0, s1 = logits(j0, False), logits(j0 + 1, last_masked)
            return absorb(s1, j0 + 1, absorb(s0, j0, carry))

        init = (jnp.full((t, 1), NEG_BIG, F32), jnp.zeros((t, 1), F32), jnp.zeros((t, HEAD_DIM), F32))
        carry = lax.fori_loop(0, i // 2, lambda jj, c: pair(2 * jj, c, False), init)
        m, l, acc = lax.cond(i % 2 == 1, lambda c: pair(i - 1, c, True),
                             lambda c: absorb(logits(i, True), i, c), carry)
        o_ref[...] = acc / l
        lse_ref[...] = jnp.broadcast_to(m + jnp.log2(l), (t, HEAD_DIM))

    return _call(
        body, "attn_fwd", (q, k, v), grid=(HEADS, T // t),
        in_specs=[pl.BlockSpec((None, t, QK_PAD), lambda h, i: (h, i, 0)),
                  pl.BlockSpec((None, T, QK_PAD), lambda h, i: (h, 0, 0)),
                  pl.BlockSpec((None, T, HEAD_DIM), lambda h, i: (h, 0, 0))],
        out_specs=[pl.BlockSpec((t, HEAD_DIM), lambda h, i: (i, h)),
                   pl.BlockSpec((None, t, HEAD_DIM), lambda h, i: (h, i, 0))],
        out_shape=[jax.ShapeDtypeStruct((T, HEADS * HEAD_DIM), F32), jax.ShapeDtypeStruct((HEADS, T, HEAD_DIM), F32)],
        exchange=exchange)


def _attn_bwd(q, k, v, dmixcat, o, lse, exchange=None):
    _, T, _ = q.shape
    t = min(ATT_TILE, T)
    nq = T // t

    def body(q_ref, k_ref, v_ref, do_ref, o_ref, lse_ref, dq_ref, dk_ref, dv_ref, delta_ref, dq_acc):
        j = pl.program_id(1)

        @pl.when(j == 0)
        def _():
            dq_acc[...] = jnp.zeros_like(dq_acc)

            def fill(i, carry):
                rs = pl.ds(pl.multiple_of(i * t, t), t)
                delta_ref[rs, :] = jnp.broadcast_to(
                    jnp.sum(do_ref[rs, :] * o_ref[rs, :], axis=-1, keepdims=True), (t, HEAD_DIM))
                return carry

            lax.fori_loop(0, nq, fill, 0)

        kb, vb = k_ref[...], v_ref[...]

        def steps(blocks, carry):
            dk, dv = carry
            rs = [pl.ds(pl.multiple_of(i * t, t), t) for i, _ in blocks]
            qb = [q_ref[r, :] for r in rs]
            dob = [do_ref[r, :].astype(BF16) for r in rs]
            s = [_dot(b, kb, NT) for b in qb]
            dp = [_dot(b, vb, NT) for b in dob]
            for n, (_, masked) in enumerate(blocks):
                p = jnp.exp2(s[n] - lse_ref[rs[n], 0:1])
                if masked:
                    p = jnp.where(_diag_mask(t), p, 0.0)
                ds = (p * (dp[n] - delta_ref[rs[n], 0:1])).astype(BF16)
                dq_acc[rs[n], :] += _dot(ds, kb)
                dk = dk + _dot(ds, qb[n], TN)
                dv = dv + _dot(p.astype(BF16), dob[n], TN)
            return dk, dv

        zero = (jnp.zeros((t, QK_PAD), F32), jnp.zeros((t, HEAD_DIM), F32))
        rest = nq - 1 - j
        carry = lax.cond(rest % 2 == 1, lambda c: steps([(j, True), (j + 1, False)], c),
                         lambda c: steps([(j, True)], c), zero)
        first = j + 1 + rest % 2
        dk, dv = lax.fori_loop(0, rest // 2, lambda n, c: steps([(first + 2 * n, False), (first + 2 * n + 1, False)], c),
                               carry)
        dk_ref[...] = (dk * LN2).astype(BF16)
        dv_ref[...] = dv.astype(BF16)

        @pl.when(j == nq - 1)
        def _():
            dq_ref[...] = dq_acc[...].astype(BF16)

    return _call(
        body, "attn_bwd", (q, k, v, dmixcat, o, lse), grid=(HEADS, nq),
        in_specs=[pl.BlockSpec((None, T, QK_PAD), lambda h, j: (h, 0, 0)),
                  pl.BlockSpec((None, t, QK_PAD), lambda h, j: (h, j, 0)),
                  pl.BlockSpec((None, t, HEAD_DIM), lambda h, j: (h, j, 0)),
                  pl.BlockSpec((T, HEAD_DIM), lambda h, j: (0, HEADS + h)),
                  pl.BlockSpec((T, HEAD_DIM), lambda h, j: (0, h)),
                  pl.BlockSpec((None, T, HEAD_DIM), lambda h, j: (h, 0, 0))],
        out_specs=[pl.BlockSpec((None, T, QK_PAD), lambda h, j: (h, 0, 0)),
                   pl.BlockSpec((None, t, QK_PAD), lambda h, j: (h, j, 0)),
                   pl.BlockSpec((None, t, HEAD_DIM), lambda h, j: (h, j, 0))],
        out_shape=[jax.ShapeDtypeStruct((HEADS, T, QK_PAD), BF16), jax.ShapeDtypeStruct((HEADS, T, QK_PAD), BF16),
                   jax.ShapeDtypeStruct((HEADS, T, HEAD_DIM), BF16)],
        scratch_shapes=[pltpu.VMEM((T, HEAD_DIM), F32), pltpu.VMEM((T, QK_PAD), F32)], exchange=exchange)


def _ln_fwd(r):
    mu = _lanemean(r)
    xc = r - mu
    rstd = lax.rsqrt(_lanemean(xc * xc) + LN_EPS)
    return xc * rstd, rstd


def _ln_bwd(dxh, xhat, rstd):
    return rstd * (dxh - _lanemean(dxh) - xhat * _lanemean(dxh * xhat))


def _mix_ln1(o_hg, o_mla, w_out, x, g_a, ln1_g, ln1_b, sc_m, sh_m, exchange=None):
    T = x.shape[0]
    tm = min(ROW_TILE, T)
    half = o_hg.shape[1]

    def body(hg_ref, mla_ref, w_ref, x_ref, ga_ref, g_ref, b_ref, sc_ref, sh_ref, mix_ref, xhat_ref, rstd_ref, u2_ref):
        mix = _dot(hg_ref[...], w_ref[0:half, :]) + _bdot(mla_ref[...], w_ref[half:, :])
        mix_ref[...] = mix
        xhat, rstd = _ln_fwd(ALPHA * x_ref[...] + (1.0 + ga_ref[...]) * mix)
        xhat_ref[...] = xhat
        rstd_ref[...] = jnp.broadcast_to(rstd, (tm, 128))
        u2_ref[...] = _modulate(xhat * g_ref[...] + b_ref[...], sc_ref[...], sh_ref[...]).astype(BF16)

    row = pl.BlockSpec((tm, D_MODEL), lambda i: (i, 0))
    vec = _full((1, D_MODEL))
    halfrow = pl.BlockSpec((tm, half), lambda i: (i, 0))
    return _call(
        body, "mix_ln1", (o_hg, o_mla, w_out, x, g_a, ln1_g, ln1_b, sc_m, sh_m), grid=(T // tm,),
        in_specs=[halfrow, halfrow, _full(w_out.shape), row, vec, vec, vec, vec, vec],
        out_specs=[row, row, pl.BlockSpec((tm, 128), lambda i: (i, 0)), row],
        out_shape=[jax.ShapeDtypeStruct((T, D_MODEL), F32), jax.ShapeDtypeStruct((T, D_MODEL), F32),
                   jax.ShapeDtypeStruct((T, 128), F32), jax.ShapeDtypeStruct((T, D_MODEL), BF16)],
        exchange=exchange)


def _mlp_fwd(u2, w1, w2, xhat1, ln1_g, ln1_b, g_m, ln2_g, ln2_b, target):
    T = u2.shape[0]
    tf = w1.shape[-1]
    nf = N_DEV // MLP_SLABS
    tm = min(ROW_TILE, T)

    def body(u2_ref, w1_ref, w2_ref, xhat_ref, g1_ref, b1_ref, gm_ref, g2_ref, b2_ref, tgt_ref,
             r_ref, dr2_ref, dh_ref, small_ref, acc_ref):
        i, f = pl.program_id(0), pl.program_id(1)
        dm = D_MODEL

        @pl.when((i == 0) & (f == 0))
        def _():
            small_ref[...] = jnp.zeros_like(small_ref)

        @pl.when(f == 0)
        def _():
            acc_ref[...] = jnp.zeros_like(acc_ref)

        u2t = u2_ref[...]
        part = None
        for s in range(MLP_SLABS):
            r = jnp.maximum(_dot(u2t, w1_ref[s]), 0.0)
            r_ref[:, s * tf:(s + 1) * tf] = r.astype(BF16)
            d = _bdot(r * r, w2_ref[s])
            part = d if part is None else part + d
        acc_ref[...] += part

        @pl.when(f == nf - 1)
        def _():
            h = acc_ref[...]
            x1 = xhat_ref[...] * g1_ref[...] + b1_ref[...]
            xhat2, rstd2 = _ln_fwd(ALPHA * x1 + (1.0 + gm_ref[...]) * h)
            err = xhat2 * g2_ref[...] + b2_ref[...] - tgt_ref[...]
            small_ref[:, 3 * dm:] += jnp.sum(0.5 * _lanemean(err * err), axis=0, keepdims=True)
            dy = err * (1.0 / D_MODEL)
            small_ref[:, dm:2 * dm] += _rowsum(dy * xhat2)
            small_ref[:, 2 * dm:3 * dm] += _rowsum(dy)
            dr2 = _ln_bwd(dy * g2_ref[...], xhat2, rstd2)
            dr2_ref[...] = dr2
            small_ref[:, 0:dm] += _rowsum(dr2 * h)
            dh_ref[...] = ((1.0 + gm_ref[...]) * dr2).astype(BF16)

    row = pl.BlockSpec((tm, D_MODEL), lambda i, f: (i, 0))
    vec = _full((1, D_MODEL))
    return pl.pallas_call(
        body, name="mlp_fwd", grid=(T // tm, nf),
        in_specs=[row, pl.BlockSpec((MLP_SLABS, D_MODEL, tf), lambda i, f: (f, 0, 0)),
                  pl.BlockSpec((MLP_SLABS, tf, D_MODEL), lambda i, f: (f, 0, 0)),
                  row, vec, vec, vec, vec, vec, row],
        out_specs=[pl.BlockSpec((tm, MLP_SLABS * tf), lambda i, f: (i, f)), row, row, _full((1, 3 * D_MODEL + 128))],
        out_shape=[jax.ShapeDtypeStruct((T, N_DEV * tf), BF16), jax.ShapeDtypeStruct((T, D_MODEL), F32),
                   jax.ShapeDtypeStruct((T, D_MODEL), BF16), jax.ShapeDtypeStruct((1, 3 * D_MODEL + 128), F32)],
        scratch_shapes=[pltpu.VMEM((tm, D_MODEL), F32)],
        compiler_params=_params(),
    )(u2, w1, w2, xhat1, ln1_g, ln1_b, g_m, ln2_g, ln2_b, target)


def _mlp_bwd(dh, w1, w2, r, dr2, xhat1, rstd1, mix, ln1_g, ln1_b, sc_m, g_a):
    T = dh.shape[0]
    tf = w1.shape[-1]
    nf = N_DEV // MLP_SLABS
    tm = min(ROW_TILE, T)

    def body(dh_ref, w1_ref, w2_ref, r_ref, dr2_ref, xhat_ref, rstd_ref, mix_ref, g1_ref, b1_ref, sc_ref, ga_ref,
             dhpre_ref, dr1_ref, dmix_ref, small_ref, acc_ref):
        i, f = pl.program_id(0), pl.program_id(1)
        dm = D_MODEL

        @pl.when((i == 0) & (f == 0))
        def _():
            small_ref[...] = jnp.zeros_like(small_ref)

        @pl.when(f == 0)
        def _():
            acc_ref[...] = jnp.zeros_like(acc_ref)

        dht = dh_ref[...]
        part = None
        for s in range(MLP_SLABS):
            cols = slice(s * tf, (s + 1) * tf)
            dhpre = (_dot(dht, w2_ref[s], NT) * (2.0 * r_ref[:, cols].astype(F32))).astype(BF16)
            dhpre_ref[:, cols] = dhpre
            d = _dot(dhpre, w1_ref[s], NT)
            part = d if part is None else part + d
        acc_ref[...] += part

        @pl.when(f == nf - 1)
        def _():
            du2 = acc_ref[...]
            xhat = xhat_ref[...]
            x1 = xhat * g1_ref[...] + b1_ref[...]
            dx1 = ALPHA * dr2_ref[...] + du2 * (1.0 + sc_ref[...])
            small_ref[:, 2 * dm:3 * dm] += _rowsum(du2 * x1)
            small_ref[:, dm:2 * dm] += _rowsum(du2)
            small_ref[:, 3 * dm:4 * dm] += _rowsum(dx1 * xhat)
            small_ref[:, 4 * dm:5 * dm] += _rowsum(dx1)
            dr1 = _ln_bwd(dx1 * g1_ref[...], xhat, rstd_ref[:, 0:1])
            dr1_ref[...] = dr1
            small_ref[:, 0:dm] += _rowsum(dr1 * mix_ref[...])
            dmix_ref[...] = ((1.0 + ga_ref[...]) * dr1).astype(BF16)

    row = pl.BlockSpec((tm, D_MODEL), lambda i, f: (i, 0))
    vec = _full((1, D_MODEL))
    return pl.pallas_call(
        body, name="mlp_bwd", grid=(T // tm, nf),
        in_specs=[row, pl.BlockSpec((MLP_SLABS, D_MODEL, tf), lambda i, f: (f, 0, 0)),
                  pl.BlockSpec((MLP_SLABS, tf, D_MODEL), lambda i, f: (f, 0, 0)),
                  pl.BlockSpec((tm, MLP_SLABS * tf), lambda i, f: (i, f)), row, row,
                  pl.BlockSpec((tm, 128), lambda i, f: (i, 0)), row, vec, vec, vec, vec],
        out_specs=[pl.BlockSpec((tm, MLP_SLABS * tf), lambda i, f: (i, f)), row, row, _full((1, 5 * D_MODEL))],
        out_shape=[jax.ShapeDtypeStruct((T, N_DEV * tf), BF16), jax.ShapeDtypeStruct((T, D_MODEL), F32),
                   jax.ShapeDtypeStruct((T, D_MODEL), BF16), jax.ShapeDtypeStruct((1, 5 * D_MODEL), F32)],
        scratch_shapes=[pltpu.VMEM((tm, D_MODEL), F32)],
        compiler_params=_params(),
    )(dh, w1, w2, r, dr2, xhat1, rstd1, mix, ln1_g, ln1_b, sc_m, g_a)


def _input_bwd(dz_h, dz_m, w_in_ext, x, dr1, sc_a, exchange=None):
    T = x.shape[0]
    tm = min(ROW_TILE, T)

    def body(dzh_ref, dzm_ref, w_ref, x_ref, dr1_ref, sc_ref, gx_ref, small_ref):
        @pl.when(pl.program_id(0) == 0)
        def _():
            small_ref[...] = jnp.zeros_like(small_ref)

        du = _bdot(dzh_ref[...], w_ref[0:2048, :]) + _bdot(dzm_ref[...], w_ref[2048:3072, :])
        gx_ref[...] = ALPHA * dr1_ref[...] + du * (1.0 + sc_ref[...])
        small_ref[:, D_MODEL:] += _rowsum(du * x_ref[...])
        small_ref[:, 0:D_MODEL] += _rowsum(du)

    row = pl.BlockSpec((tm, D_MODEL), lambda i: (i, 0))
    vec = _full((1, D_MODEL))
    return _call(
        body, "input_bwd", (dz_h, dz_m, w_in_ext, x, dr1, sc_a), grid=(T // tm,),
        in_specs=[pl.BlockSpec((tm, 2048), lambda i: (i, 0)), row, _full(w_in_ext.shape), row, row, vec],
        out_specs=[row, _full((1, 2 * D_MODEL))],
        out_shape=[jax.ShapeDtypeStruct((T, D_MODEL), F32), jax.ShapeDtypeStruct((1, 2 * D_MODEL), F32)],
        exchange=exchange)


def _adam_math(w, g, m, v):
    m = ADAM_B1 * m + (1.0 - ADAM_B1) * g
    v = ADAM_B2 * v + (1.0 - ADAM_B2) * (g * g)
    m_hat = m / (1.0 - ADAM_B1 ** ADAM_STEP)
    v_hat = v / (1.0 - ADAM_B2 ** ADAM_STEP)
    return -ADAM_LR * (m_hat / (jnp.sqrt(v_hat) + ADAM_EPS) + ADAM_WD * w), m, v


def _adam(g_slabs, w, m, v, name, g_fn=None, g_extra=()):
    R, C = w.shape
    tr = 256 if R % 256 == 0 else R
    ns = 0 if g_slabs is None else g_slabs.shape[0]
    slab_rows = tr if g_slabs is None or g_slabs.shape[1] == R else g_slabs.shape[1]
    assert slab_rows == tr or tr == R
    ne = len(g_extra)

    def body(*refs):
        e_refs = refs[:ne]
        refs = refs[ne:]
        if ns:
            gs_ref, refs = refs[0], refs[1:]
        w_ref, m_ref, v_ref, g_ref, d_ref, nm_ref, nv_ref = refs
        if g_fn is not None:
            g = g_fn(*e_refs)
        else:
            g = gs_ref[0].astype(F32)
            for s in range(1, ns):
                g = g + gs_ref[s].astype(F32)
            g = g[:tr]
        d, nm, nv = _adam_math(w_ref[...], g, m_ref[...], v_ref[...])
        g_ref[...] = g
        d_ref[...] = d
        nm_ref[...] = nm
        nv_ref[...] = nv

    blk = pl.BlockSpec((tr, C), lambda i: (i, 0))
    in_specs = [pl.BlockSpec((tr, e.shape[1]), lambda i: (i, 0)) if e.shape[0] == R else _full(e.shape) for e in g_extra]
    args = list(g_extra)
    if ns:
        in_specs.append(pl.BlockSpec((ns, slab_rows, C), lambda i: (0, i, 0)))
        args.append(g_slabs)
    return pl.pallas_call(
        body, name=name, grid=(R // tr,), in_specs=in_specs + [blk] * 3, out_specs=[blk] * 4,
        out_shape=[jax.ShapeDtypeStruct((R, C), F32)] * 4, compiler_params=_params(),
    )(*args, w, m, v)


def _adam_small(small_all, params):
    n = len(params)

    def body(*refs):
        s_ref, refs = refs[0], refs[1:]
        wmv, loss_ref, outs = refs[:3 * n], refs[3 * n], refs[3 * n + 1:]
        tot = s_ref[0]
        for i in range(1, N_DEV):
            tot = tot + s_ref[i]
        loss_ref[...] = tot[:, SMALL_W - 128:]
        for j, (w, _, _, off) in enumerate(params):
            w_ref, m_ref, v_ref = wmv[3 * j:3 * j + 3]
            g_ref, d_ref, nm_ref, nv_ref = outs[4 * j:4 * j + 4]
            if w.shape[0] == 2:
                lb = _lower_bound(w_ref)
                g0 = tot[:, off:off + w.shape[1]] * lb * (1.0 - lb)
                rows = [(slice(0, 1), g0), (slice(1, 2), -g0)]
            else:
                rows = [(slice(0, 1), tot[:, off:off + w.shape[1]])]
            for rs, g in rows:
                d, nm, nv = _adam_math(w_ref[rs, :], g, m_ref[rs, :], v_ref[rs, :])
                g_ref[rs, :], d_ref[rs, :], nm_ref[rs, :], nv_ref[rs, :] = g, d, nm, nv

    out_shape = [jax.ShapeDtypeStruct((1, 128), F32)]
    for w, _, _, _ in params:
        out_shape += [jax.ShapeDtypeStruct(w.shape, F32)] * 4
    res = pl.pallas_call(body, name="adam_small", out_shape=out_shape, compiler_params=_params())(
        small_all, *[a for w, m, v, _ in params for a in (w, m, v)])
    return res[0], [tuple(res[1 + 4 * j:5 + 4 * j]) for j in range(n)]


def _cols_from_slabs(g):
    s, r, c = g.shape
    return jnp.transpose(g, (1, 0, 2)).reshape(r, s * c)


def _slabs_from_cols(w):
    r, c = w.shape
    return jnp.transpose(w.reshape(r, N_DEV, c // N_DEV), (1, 0, 2))


def _rot_half_rows(wt):
    return jnp.concatenate([-wt[32:], wt[:32]], axis=0)


def _unrot_half_rows(dwt_rot):
    return jnp.concatenate([dwt_rot[32:], -dwt_rot[:32]], axis=0)


def _ext_in_t(g):
    k_in = g.shape[2]
    z64, z128 = jnp.zeros((64, k_in), BF16), jnp.zeros((128, k_in), BF16)
    wt = g.reshape(N_DEV * g.shape[1], k_in)
    main, wk = wt[:wt.shape[0] - ROPE_DIM], wt[wt.shape[0] - ROPE_DIM:]
    return jnp.concatenate([main, z128, wk, z64, z128, _rot_half_rows(wk), z64], axis=0)


def _ext_q_t(wt):
    r = wt.shape[1]
    z64, z128 = jnp.zeros((64, r), BF16), jnp.zeros((128, r), BF16)
    per = HEAD_DIM + ROPE_DIM
    main = [jnp.concatenate([wt[per * h:per * (h + 1)], z64], axis=0) for h in range(HEADS)]
    rot = [jnp.concatenate([z128, _rot_half_rows(wt[per * h + HEAD_DIM:per * (h + 1)]), z64], axis=0)
           for h in range(HEADS)]
    return jnp.concatenate(main + rot, axis=0)


def _ext_kv(w_kv_up):
    r = w_kv_up.shape[0]
    z128 = jnp.zeros((r, 128), BF16)
    wkv = w_kv_up.reshape(r, HEADS, 2 * HEAD_DIM)
    kpad = [jnp.concatenate([wkv[:, h, :HEAD_DIM], z128], axis=1) for h in range(HEADS)]
    vals = [wkv[:, h, HEAD_DIM:] for h in range(HEADS)]
    return jnp.concatenate(kpad + vals, axis=1)


def _grad_in_from_ext_t(dwt_h, dwt_m):
    dwk = dwt_m[512 + 128:512 + 192] + _unrot_half_rows(dwt_m[768 + 128:768 + 192])
    return jnp.concatenate([dwt_h, dwt_m[:512], dwk], axis=0)


def _grad_q_from_ext_t(dwq_ext_t):
    rows = []
    for h in range(HEADS):
        main, rot = dwq_ext_t[256 * h:256 * h + 256], dwq_ext_t[1024 + 256 * h:1280 + 256 * h]
        rows += [main[:128], main[128:192] + _unrot_half_rows(rot[128:192])]
    return jnp.concatenate(rows, axis=0)


def _grad_kv_from_ext(dwkv_ext):
    kvcols = []
    for h in range(HEADS):
        kvcols += [dwkv_ext[:, 256 * h:256 * h + 128], dwkv_ext[:, 1024 + 128 * h:1152 + 128 * h]]
    return jnp.concatenate(kvcols, axis=1)


SMALL_W = 6144 + 512 + 512 + 256 + 256 + 4 * 1024 + 128


def kernel(x, c, positions, w_ada, b_ada, w_in, hg_lower_bounds, hg_norm_w, mla_q_norm_w, w_q_up, mla_kv_norm_w, w_kv_up, w_out, ln1_g, ln1_b, w_mlp_in, w_mlp_out, ln2_g, ln2_b, loss_target, m_w_ada, m_b_ada, m_w_in, m_hg_lower_bounds, m_hg_norm_w, m_mla_q_norm_w, m_w_q_up, m_mla_kv_norm_w, m_w_kv_up, m_w_out, m_ln1_g, m_ln1_b, m_w_mlp_in, m_w_mlp_out, m_ln2_g, m_ln2_b, v_w_ada, v_b_ada, v_w_in, v_hg_lower_bounds, v_hg_norm_w, v_mla_q_norm_w, v_w_q_up, v_mla_kv_norm_w, v_w_kv_up, v_w_out, v_ln1_g, v_ln1_b, v_w_mlp_in, v_w_mlp_out, v_ln2_g, v_ln2_b):
    T = x.shape[1]
    me = 4 * lax.axis_index("x") + 2 * lax.axis_index("y") + lax.axis_index("c")
    xs, tgt = x[0], loss_target[0]
    transposed = ("w_in", "w_q_up")
    as_used = lambda n, a: a[0].T if n in transposed else a[0]
    big = {n: as_used(n, a) for n, a in dict(w_in=w_in, w_q_up=w_q_up, w_kv_up=w_kv_up, w_out=w_out,
                                              w_mlp_in=w_mlp_in, w_mlp_out=w_mlp_out).items()}
    names = list(big)

    bf = {n: big[n].astype(BF16) for n in names}
    g_in, g_c = _gather_two_level([bf["w_in"], c], name="gather_w_in")
    c_all = g_c.reshape(N_DEV, D_MODEL)

    ada_cols = w_ada.shape[2]
    mod_part, cond = _mod_part(c_all, w_ada[0], lax.dynamic_slice(b_ada, (0, me * ada_cols), (1, ada_cols)))
    (mod_all,) = _exchange([mod_part], scatter=False, name="gather_mod")
    mod_row = lax.dynamic_slice(mod_all, (0, me, 0), (N_DEV, 1, ada_cols)).reshape(1, N_DEV * ada_cols)
    sh_a, sc_a, g_a, sh_m, sc_m, g_m = [mod_row[:, D_MODEL * i:D_MODEL * (i + 1)] for i in range(6)]

    w_in_ext = _ext_in_t(g_in)
    z, (g_q, g_kv, g_out) = _matmul(xs, w_in_ext, "NT", "in_proj", a_fn=_modulate, extras=(sc_a, sh_a), tn=3072,
                                    exchange=_Exchange([bf["w_q_up"], bf["w_kv_up"], bf["w_out"]], False))
    wq_ext = _ext_q_t(g_q.reshape(N_DEV * g_q.shape[1], g_q.shape[2]))
    wkv_ext = _ext_kv(_cols_from_slabs(g_kv))
    w_out_full = g_out.reshape(D_MODEL, D_MODEL)
    inv_freq = 1.0 / (ROPE_THETA ** (jnp.arange(0, ROPE_DIM, 2, dtype=F32) / ROPE_DIM))
    zeros = lambda n: jnp.zeros((n,), F32)
    invf = jnp.concatenate([zeros(128), inv_freq, inv_freq, zeros(64)]).reshape(1, QK_PAD)
    m_rot = jnp.concatenate([zeros(128), jnp.ones((64,), F32), zeros(64)]).reshape(1, QK_PAD)
    q, k, v, c1, s1, cqn, ckvn = _mla_pre(z, positions.reshape(T, 1), invf, m_rot, wq_ext, wkv_ext,
                                          mla_q_norm_w, mla_kv_norm_w)[0]
    o_raw, o_gated, s_prev = _hgrn_fwd(z, hg_lower_bounds, hg_norm_w)[0]
    (o_mla, lse), (w1, w2) = _attn_fwd(q, k, v, exchange=[_StagedGather(bf["w_mlp_in"], 0.9),
                                                          _StagedGather(bf["w_mlp_out"], 0.9)])
    mix, xhat1, rstd1, u2 = _mix_ln1(o_gated, o_mla, w_out_full, xs, g_a, ln1_g, ln1_b, sc_m, sh_m)[0]
    r, dr2, dh, small_mlp_fwd = _mlp_fwd(u2, w1, w2, xhat1, ln1_g, ln1_b, g_m, ln2_g, ln2_b, tgt)

    dhpre, dr1, dmix, small_mlp_bwd = _mlp_bwd(dh, w1, w2, r, dr2, xhat1, rstd1, mix, ln1_g, ln1_b, sc_m, g_a)
    received = {}
    dw2 = _matmul(r, dh, "TN", "wgrad_mlp_out", out_dtype=BF16, a_fn=_square, tm=1024, tk=2048)
    dw1 = _matmul(u2, dhpre, "TN", "wgrad_mlp_in", out_dtype=BF16, tm=1024, tk=2048, out_slabs=N_DEV)
    dmixcat = _matmul(dmix, w_out_full, "NT", "dgrad_out")
    dw_out = jnp.concatenate([_matmul(o_gated, dmix, "TN", "wgrad_out_hg", out_dtype=BF16, tk=2048),
                              _matmul(o_mla, dmix, "TN", "wgrad_out_mla", out_dtype=BF16, tk=2048)], axis=0)
    (dz_h, small_hgrn), (received["w_out"],) = _hgrn_bwd(
        dmixcat, z, o_raw, s_prev, hg_lower_bounds, hg_norm_w,
        exchange=_Exchange([dw_out.reshape(N_DEV, D_MODEL // N_DEV, D_MODEL)], True))
    (dq, dk, dv), (received["w_mlp_in"], received["w_mlp_out"]) = _attn_bwd(
        q, k, v, dmixcat, o_mla, lse,
        exchange=_Exchange([dw1, dw2.reshape(N_DEV, dw2.shape[0] // N_DEV, D_MODEL)], True))
    dz_m, dq_ext, dkv_ext, small_mla = _mla_bwd(dq, dk, dv, z, c1, s1, wq_ext, wkv_ext, mla_q_norm_w, mla_kv_norm_w)
    dwq_t = _grad_q_from_ext_t(_matmul(dq_ext, cqn, "TN", "wgrad_q_up", tm=1024, tk=2048))
    dwkv = _grad_kv_from_ext(_matmul(ckvn, dkv_ext, "TN", "wgrad_kv_up", tn=1536, tk=2048))
    qkv_slabs = [dwq_t.reshape((N_DEV, dwq_t.shape[0] // N_DEV, dwq_t.shape[1])).astype(BF16),
                 _slabs_from_cols(dwkv).astype(BF16)]
    dwt_h, (received["w_q_up"], received["w_kv_up"]) = _matmul(
        dz_h, xs, "TN", "wgrad_in_h", b_fn=_modulate, extras=(sc_a, sh_a), tm=1024, tk=2048,
        exchange=_Exchange(qkv_slabs, True))
    dwt_m = _matmul(dz_m, xs, "TN", "wgrad_in_m", b_fn=_modulate, extras=(sc_a, sh_a), tm=1024, tk=2048)
    dw_in_t = _grad_in_from_ext_t(dwt_h, dwt_m)
    in_slabs = dw_in_t.reshape((N_DEV, dw_in_t.shape[0] // N_DEV, dw_in_t.shape[1]))
    in_slabs = jnp.pad(in_slabs, ((0, 0), (0, -in_slabs.shape[1] % 16), (0, 0))).astype(BF16)
    (grad_x, small_in), (received["w_in"],) = _input_bwd(
        dz_h, dz_m, w_in_ext, xs, dr1, sc_a, exchange=_StagedScatter(in_slabs))

    small = jnp.concatenate([small_in, small_mlp_bwd[:, :3 * D_MODEL], small_mlp_fwd[:, :D_MODEL], small_hgrn,
                             small_mla, small_mlp_bwd[:, 3 * D_MODEL:], small_mlp_fwd[:, D_MODEL:]], axis=1)
    assert small.shape == (1, SMALL_W)
    (small_all,) = _exchange([small], scatter=False, name="gather_small")

    moments = dict(w_in=(m_w_in, v_w_in), w_q_up=(m_w_q_up, v_w_q_up), w_kv_up=(m_w_kv_up, v_w_kv_up),
                   w_out=(m_w_out, v_w_out), w_mlp_in=(m_w_mlp_in, v_w_mlp_in), w_mlp_out=(m_w_mlp_out, v_w_mlp_out))
    res = {}
    for n in names:
        res[n] = _adam(received[n], big[n], as_used(n, moments[n][0]), as_used(n, moments[n][1]), name="adam_" + n)
    dmod_cols = lax.dynamic_slice(small_all.reshape(N_DEV, SMALL_W), (0, me * ada_cols), (N_DEV, ada_cols))
    cond_t = cond.T

    def ada_grad(ct_ref, dm_ref):
        g = ct_ref[:, 0:1] * dm_ref[0:1, :]
        for b in range(1, N_DEV):
            g = g + ct_ref[:, b:b + 1] * dm_ref[b:b + 1, :]
        return g

    res["w_ada"] = _adam(None, w_ada[0], m_w_ada[0], v_w_ada[0], name="adam_w_ada", g_fn=ada_grad,
                         g_extra=(cond_t, dmod_cols))

    small_params = [("b_ada", b_ada, m_b_ada, v_b_ada, 0),
                    ("hg_lower_bounds", hg_lower_bounds, m_hg_lower_bounds, v_hg_lower_bounds, 6144),
                    ("hg_norm_w", hg_norm_w, m_hg_norm_w, v_hg_norm_w, 6656),
                    ("mla_q_norm_w", mla_q_norm_w, m_mla_q_norm_w, v_mla_q_norm_w, 7168),
                    ("mla_kv_norm_w", mla_kv_norm_w, m_mla_kv_norm_w, v_mla_kv_norm_w, 7424),
                    ("ln1_g", ln1_g, m_ln1_g, v_ln1_g, 7680), ("ln1_b", ln1_b, m_ln1_b, v_ln1_b, 8704),
                    ("ln2_g", ln2_g, m_ln2_g, v_ln2_g, 9728), ("ln2_b", ln2_b, m_ln2_b, v_ln2_b, 10752)]
    loss_row, small_res = _adam_small(small_all, [p[1:] for p in small_params])
    for p, r4 in zip(small_params, small_res):
        res[p[0]] = r4
    loss = loss_row[0, 0]

    order = ["w_ada", "b_ada", "w_in", "hg_lower_bounds", "hg_norm_w", "mla_q_norm_w", "w_q_up", "mla_kv_norm_w",
             "w_kv_up", "w_out", "ln1_g", "ln1_b", "w_mlp_in", "w_mlp_out", "ln2_g", "ln2_b"]
    def as_given(n, a):
        if n in transposed:
            a = a.T
        return a[None] if n in big or n == "w_ada" else a

    shaped = {n: tuple(as_given(n, a) for a in res[n]) for n in order}
    outs = [loss, grad_x.reshape(1, T, D_MODEL)]
    for i in range(4):
        outs += [shaped[n][i] for n in order]
    return tuple(outs)
```

```python
import functools

import jax
import jax.numpy as jnp
import numpy as np
from jax import lax
from jax.experimental import pallas as pl
from jax.experimental.pallas import tpu as pltpu

F32, BF16 = jnp.float32, jnp.bfloat16
N_DEV = 8
D_MODEL = 1024
HEADS = 4
HEAD_DIM = 128
ROPE_DIM = 64
QK_PAD = 256
CHUNK = 64
ROPE_THETA = 10000.0
RMS_EPS = 1e-6
LN_EPS = 1e-5
ALPHA = 2.0 ** 0.25
ATT_SCALE = (HEAD_DIM + ROPE_DIM) ** -0.5
LN2 = float(np.log(2.0))
Q_PRESCALE = ATT_SCALE / LN2
ADAM_LR, ADAM_B1, ADAM_B2, ADAM_EPS, ADAM_WD, ADAM_STEP = 0.001, 0.9, 0.999, 1e-08, 0.01, 10
NEG_BIG = -1e30

ROW_TILE = 512
ATT_TILE = 512
HGRN_GROUP = 8
MLP_SLABS = 4
VMEM_LIMIT = 56 * 2 ** 20

NN = (((1,), (0,)), ((), ()))
NT = (((1,), (1,)), ((), ()))
TN = (((0,), (0,)), ((), ()))


def _dot(a, b, dims=NN):
    return lax.dot_general(a, b, dims, preferred_element_type=F32)


def _bdot(a, b, dims=NN):
    return lax.dot_general(a.astype(BF16), b.astype(BF16), dims, preferred_element_type=F32)


def _hdot(a, b, dims=NN):
    return lax.dot_general(a, b, dims, precision=lax.Precision.HIGHEST, preferred_element_type=F32)


def _params():
    return pltpu.CompilerParams(vmem_limit_bytes=VMEM_LIMIT)


def _sigmoid(x):
    return 1.0 / (1.0 + jnp.exp(-x))


def _rowsum(x):
    return jnp.sum(x, axis=0, keepdims=True)


def _lanemean(x):
    return jnp.mean(x, axis=-1, keepdims=True)


def _full(shape):
    nd = len(shape)
    return pl.BlockSpec(shape, lambda *_: (0,) * nd)


class _Exchange:
    def __init__(self, arrs, scatter):
        self.arrs, self.scatter, self.n, self.aliases, self.middle_at = list(arrs), scatter, len(arrs), [], 0.5
        self.out_shape = [jax.ShapeDtypeStruct((N_DEV,) + (a.shape[1:] if scatter else a.shape), a.dtype)
                          for a in self.arrs]
        n = self.n
        self.scratch = [pltpu.SemaphoreType.DMA((n, N_DEV - 1)), pltpu.SemaphoreType.DMA((n, N_DEV - 1)),
                        pltpu.SemaphoreType.DMA((n,))]

    def _copies(self, ins, outs, sems):
        send_sems, recv_sems, loc_sems = sems
        x, y, c = lax.axis_index("x"), lax.axis_index("y"), lax.axis_index("c")
        me = 4 * x + 2 * y + c
        copies = []
        for k in range(self.n):
            src_of = (lambda i, k=k: ins[k].at[i]) if self.scatter else (lambda i, k=k: ins[k])
            copies.append((pltpu.make_async_copy(src_of(me), outs[k].at[me], loc_sems.at[k]), None))
            for p in range(1, N_DEV):
                px = (1 - x) if p & 4 else x
                py = (1 - y) if p & 2 else y
                pc = (1 - c) if p & 1 else c
                peer = 4 * px + 2 * py + pc
                both = dict(send_sem=send_sems.at[k, p - 1], recv_sem=recv_sems.at[k, p - 1],
                            device_id=(px, py, pc), device_id_type=pl.DeviceIdType.MESH)
                send = pltpu.make_async_remote_copy(src_ref=src_of(peer), dst_ref=outs[k].at[me], **both)
                recv = pltpu.make_async_remote_copy(src_ref=src_of(peer), dst_ref=outs[k].at[peer], **both)
                copies.append((send, recv))
        return copies

    def start(self, ins, outs, sems):
        for first, _ in self._copies(ins, outs, sems):
            first.start()

    def middle(self, ins, outs, sems):
        pass

    def wait(self, ins, outs, sems):
        for first, recv in self._copies(ins, outs, sems):
            if recv is None:
                first.wait()
            else:
                recv.wait_recv()
                first.wait_send()


class _StagedGather:
    def __init__(self, arr, middle_at=0.8):
        self.arrs, self.aliases, self.middle_at = [arr], [], middle_at
        self.out_shape = [jax.ShapeDtypeStruct((N_DEV,) + arr.shape, arr.dtype)]
        self.scratch = [pltpu.VMEM((N_DEV,) + arr.shape, arr.dtype), pltpu.SemaphoreType.DMA((7,)),
                        pltpu.SemaphoreType.DMA((7,)), pltpu.SemaphoreType.DMA((2,))]

    def _parts(self, scr):
        stage, send_sems, recv_sems, loc_sems = scr
        x, y, c = lax.axis_index("x"), lax.axis_index("y"), lax.axis_index("c")
        me, sibling = (x, y, c), (x, y, 1 - c)
        chips = [(1 - x, y), (x, 1 - y), (1 - x, 1 - y)]

        def copy(j, block, to):
            px, py, pc = block
            slot = stage.at[4 * px + 2 * py + pc]
            return pltpu.make_async_remote_copy(src_ref=slot, dst_ref=slot, send_sem=send_sems.at[j],
                                                recv_sem=recv_sems.at[j], device_id=to,
                                                device_id_type=pl.DeviceIdType.MESH)

        return stage, loc_sems, me, sibling, chips, c, copy

    def start(self, ins, outs, scr):
        stage, loc_sems, me, sibling, chips, c, copy = self._parts(scr)
        x, y, _ = me
        own = pltpu.make_async_copy(ins[0], stage.at[4 * x + 2 * y + c], loc_sems.at[0])
        own.start()
        own.wait()
        copy(0, me, sibling).start()
        for j, chip in enumerate(chips):
            copy(1 + j, me, (*chip, c)).start()

    def middle(self, ins, outs, scr):
        stage, loc_sems, me, sibling, chips, c, copy = self._parts(scr)
        for j, chip in enumerate(chips):
            copy(1 + j, (*chip, c), me).wait_recv()
            copy(4 + j, (*chip, c), sibling).start()

    def wait(self, ins, outs, scr):
        stage, loc_sems, me, sibling, chips, c, copy = self._parts(scr)
        copy(0, sibling, me).wait_recv()
        for j, chip in enumerate(chips):
            copy(4 + j, (*chip, 1 - c), me).wait_recv()
        copy(0, me, sibling).wait_send()
        for j, chip in enumerate(chips):
            copy(1 + j, me, (*chip, c)).wait_send()
            copy(4 + j, (*chip, c), sibling).wait_send()
        whole = pltpu.make_async_copy(stage, outs[0], loc_sems.at[1])
        whole.start()
        whole.wait()


class _StagedScatter:
    def __init__(self, slabs, middle_at=0.2):
        _, r, c = slabs.shape
        self.arrs, self.aliases, self.middle_at = [slabs], [], middle_at
        self.out_shape = [jax.ShapeDtypeStruct((4, r, c), slabs.dtype)]
        self.scratch = [pltpu.VMEM((N_DEV, r, c), slabs.dtype), pltpu.VMEM((4, r, c), slabs.dtype),
                        pltpu.VMEM((3, r, c), slabs.dtype), pltpu.SemaphoreType.DMA((4,)), pltpu.SemaphoreType.DMA((4,)),
                        pltpu.SemaphoreType.DMA((3,)), pltpu.SemaphoreType.DMA((3,)), pltpu.SemaphoreType.DMA((4,))]

    def _parts(self, scr):
        stage, from_sib, from_chips, sib_send, sib_recv, ici_send, ici_recv, loc_sems = scr
        x, y, c = lax.axis_index("x"), lax.axis_index("y"), lax.axis_index("c")
        chips = [(1 - x, y), (x, 1 - y), (1 - x, 1 - y)]

        def to_sibling(j):
            return pltpu.make_async_remote_copy(src_ref=stage.at[2 * j + 1 - c], dst_ref=from_sib.at[j],
                                                send_sem=sib_send.at[j], recv_sem=sib_recv.at[j],
                                                device_id=(x, y, 1 - c), device_id_type=pl.DeviceIdType.MESH)

        def to_chip(k):
            px, py = chips[k]
            return pltpu.make_async_remote_copy(src_ref=stage.at[4 * px + 2 * py + c], dst_ref=from_chips.at[k],
                                                send_sem=ici_send.at[k], recv_sem=ici_recv.at[k],
                                                device_id=(px, py, c), device_id_type=pl.DeviceIdType.MESH)

        return stage, from_sib, from_chips, loc_sems, (x, y, c), chips, to_sibling, to_chip

    def start(self, ins, outs, scr):
        stage, _, _, loc_sems, _, _, to_sibling, _ = self._parts(scr)
        load = pltpu.make_async_copy(ins[0], stage, loc_sems.at[0])
        load.start()
        load.wait()
        for j in range(4):
            to_sibling(j).start()

    def middle(self, ins, outs, scr):
        stage, from_sib, _, _, (x, y, c), _, to_sibling, to_chip = self._parts(scr)
        for j in range(4):
            to_sibling(j).wait_recv()
            mine = stage.at[2 * j + c]
            mine[...] = (mine[...].astype(F32) + from_sib[j].astype(F32)).astype(mine.dtype)
        for k in range(3):
            to_chip(k).start()

    def wait(self, ins, outs, scr):
        stage, _, from_chips, loc_sems, (x, y, c), chips, to_sibling, to_chip = self._parts(scr)
        writes = [pltpu.make_async_copy(stage.at[4 * x + 2 * y + c], outs[0].at[2 * x + y], loc_sems.at[0])]
        for k, (px, py) in enumerate(chips):
            to_chip(k).wait_recv()
            writes.append(pltpu.make_async_copy(from_chips.at[k], outs[0].at[2 * px + py], loc_sems.at[1 + k]))
        for w in writes:
            w.start()
        for j in range(4):
            to_sibling(j).wait_send()
        for k in range(3):
            to_chip(k).wait_send()
        for w in writes:
            w.wait()


def _call(body, name, args, out_shape, grid=(), in_specs=(), out_specs=(), scratch_shapes=(), exchange=None):
    if exchange is None:
        return pl.pallas_call(body, name=name, grid=grid, in_specs=list(in_specs), out_specs=list(out_specs),
                              out_shape=list(out_shape), scratch_shapes=list(scratch_shapes),
                              compiler_params=_params())(*args), None
    exs = list(exchange) if isinstance(exchange, (list, tuple)) else [exchange]
    ni, no, ns = len(args), len(out_shape), len(scratch_shapes)
    nxi, nxo = sum(len(e.arrs) for e in exs), sum(len(e.out_shape) for e in exs)
    steps = int(np.prod(grid))
    mid_step = lambda e: min(max(int(steps * e.middle_at), 1), steps - 1)
    aliases, iat, oat = {}, ni, no
    for e in exs:
        for src, dst in e.aliases:
            aliases[iat + src] = oat + dst
        iat, oat = iat + len(e.arrs), oat + len(e.out_shape)

    def wrapped(*refs):
        a, xi = refs[:ni], refs[ni:ni + nxi]
        o, xo = refs[ni + nxi:ni + nxi + no], refs[ni + nxi + no:ni + nxi + no + nxo]
        s, xs = refs[ni + nxi + no + nxo:ni + nxi + no + nxo + ns], refs[ni + nxi + no + nxo + ns:]
        parts, iat, oat, sat = [], 0, 0, 0
        for e in exs:
            parts.append((e, xi[iat:iat + len(e.arrs)], xo[oat:oat + len(e.out_shape)], xs[sat:sat + len(e.scratch)]))
            iat, oat, sat = iat + len(e.arrs), oat + len(e.out_shape), sat + len(e.scratch)
        step = 0
        for d, g in enumerate(grid):
            step = step * g + pl.program_id(d)

        @pl.when(step == 0)
        def _():
            for e, ins, outs, sems in parts:
                e.start(ins, outs, sems)

        for at_step in sorted({mid_step(e) for e in exs}):
            @pl.when(step == at_step)
            def _():
                for e, ins, outs, sems in parts:
                    if mid_step(e) == at_step:
                        e.middle(ins, outs, sems)

        body(*a, *o, *s)

        @pl.when(step == steps - 1)
        def _():
            for e, ins, outs, sems in parts:
                e.wait(ins, outs, sems)

    hbm = pl.BlockSpec(memory_space=pltpu.HBM)
    res = pl.pallas_call(
        wrapped, name=name, grid=grid, in_specs=list(in_specs) + [hbm] * nxi, out_specs=list(out_specs) + [hbm] * nxo,
        out_shape=list(out_shape) + [o_ for e in exs for o_ in e.out_shape],
        scratch_shapes=list(scratch_shapes) + [s_ for e in exs for s_ in e.scratch],
        input_output_aliases=aliases, compiler_params=_params())(*args, *[a_ for e in exs for a_ in e.arrs])
    return res[:no], res[no:]


def _gather_two_level(arrs, name):
    n = len(arrs)
    out_shape = [jax.ShapeDtypeStruct((N_DEV,) + a.shape, a.dtype) for a in arrs]

    def body(*refs):
        ins, outs = refs[:n], refs[n:2 * n]
        send_sems, recv_sems, loc_sems = refs[2 * n:]
        x, y, c = lax.axis_index("x"), lax.axis_index("y"), lax.axis_index("c")
        me, sibling = (x, y, c), (x, y, 1 - c)
        chips = [(1 - x, y), (x, 1 - y), (1 - x, 1 - y)]

        def copy(k, j, block, to, src=None):
            px, py, pc = block
            dst = outs[k].at[4 * px + 2 * py + pc]
            return pltpu.make_async_remote_copy(src_ref=dst if src is None else src, dst_ref=dst,
                                                send_sem=send_sems.at[k, j], recv_sem=recv_sems.at[k, j],
                                                device_id=to, device_id_type=pl.DeviceIdType.MESH)

        mine = [pltpu.make_async_copy(ins[k], outs[k].at[4 * x + 2 * y + c], loc_sems.at[k]) for k in range(n)]
        first = []
        for k in range(n):
            mine[k].start()
            first.append(copy(k, 0, me, sibling, src=ins[k]))
            first += [copy(k, 1 + j, me, (*chip, c), src=ins[k]) for j, chip in enumerate(chips)]
        for cp in first:
            cp.start()
        passed = []
        for j, chip in enumerate(chips):
            for k in range(n):
                copy(k, 1 + j, (*chip, c), me).wait_recv()
                passed.append(copy(k, 4 + j, (*chip, c), sibling))
                passed[-1].start()
        for k in range(n):
            copy(k, 0, sibling, me).wait_recv()
            for j, chip in enumerate(chips):
                copy(k, 4 + j, (*chip, 1 - c), me).wait_recv()
        for cp in first + passed:
            cp.wait_send()
        for cp in mine:
            cp.wait()

    vmem = pl.BlockSpec(memory_space=pltpu.VMEM)
    return pl.pallas_call(body, name=name, out_shape=out_shape, in_specs=[vmem] * n, out_specs=[vmem] * n,
                          scratch_shapes=[pltpu.SemaphoreType.DMA((n, 7)), pltpu.SemaphoreType.DMA((n, 7)),
                                          pltpu.SemaphoreType.DMA((n,))], compiler_params=_params())(*arrs)


def _exchange(arrs, scatter, name):
    ex = _Exchange(arrs, scatter)

    def body(*refs):
        ins, outs, sems = refs[:ex.n], refs[ex.n:2 * ex.n], refs[2 * ex.n:]
        ex.start(ins, outs, sems)
        ex.wait(ins, outs, sems)

    hbm = pl.BlockSpec(memory_space=pltpu.HBM)
    return pl.pallas_call(body, name=name, out_shape=ex.out_shape, in_specs=[hbm] * ex.n, out_specs=[hbm] * ex.n,
                          scratch_shapes=ex.scratch)(*ex.arrs)


def _matmul(a, b, mode, name, out_dtype=F32, tm=512, tn=1024, tk=1024, a_fn=None, b_fn=None, extras=(),
            out_slabs=None, exchange=None):
    assert not (a_fn and b_fn) and not (b_fn and mode == "NT")
    if mode == "NN":
        (M, K), N = a.shape, b.shape[1]
    elif mode == "NT":
        (M, K), N = a.shape, b.shape[0]
    else:
        (K, M), N = a.shape, b.shape[1]
    slab_w = N // out_slabs if out_slabs else None
    if out_slabs:
        tn = max(slab_w, min(tn, N) // slab_w * slab_w)
    tm, tn, tk = min(tm, M), min(tn, N), min(tk, K)
    assert M % tm == 0 and N % tn == 0 and K % tk == 0, (name, M, N, K)
    nk = K // tk
    dims = {"NN": NN, "NT": NT, "TN": TN}[mode]
    ne = len(extras)

    def body(a_ref, b_ref, *rest):
        e_refs, o_ref, acc_ref = rest[:ne], rest[ne], rest[ne + 1]
        k = pl.program_id(2)

        @pl.when(k == 0)
        def _():
            acc_ref[...] = jnp.zeros_like(acc_ref)

        at, bt = a_ref[...], b_ref[...]
        if a_fn is not None:
            at = a_fn(at.astype(F32), *[e[...] for e in e_refs])
        if b_fn is not None:
            bt = b_fn(bt.astype(F32), *[e[...] for e in e_refs])
        acc_ref[...] += _bdot(at, bt, dims)

        @pl.when(k == nk - 1)
        def _():
            if out_slabs:
                for s in range(tn // slab_w):
                    o_ref[s] = acc_ref[:, s * slab_w:(s + 1) * slab_w].astype(out_dtype)
            else:
                o_ref[...] = acc_ref[...].astype(out_dtype)

    if mode == "TN":
        a_spec = pl.BlockSpec((tk, tm), lambda i, j, k: (k, i))
        e_spec = pl.BlockSpec((1, tm), lambda i, j, k: (0, i))
    else:
        a_spec = pl.BlockSpec((tm, tk), lambda i, j, k: (i, k))
        e_spec = pl.BlockSpec((1, tk), lambda i, j, k: (0, k))
    if mode == "NT":
        b_spec = pl.BlockSpec((tn, tk), lambda i, j, k: (j, k))
    else:
        b_spec = pl.BlockSpec((tk, tn), lambda i, j, k: (k, j))
    if b_fn is not None:
        e_spec = pl.BlockSpec((1, tn), lambda i, j, k: (0, j))
    if out_slabs:
        o_shape = jax.ShapeDtypeStruct((out_slabs, M, slab_w), out_dtype)
        o_spec = pl.BlockSpec((tn // slab_w, tm, slab_w), lambda i, j, k: (j, i, 0))
    else:
        o_shape = jax.ShapeDtypeStruct((M, N), out_dtype)
        o_spec = pl.BlockSpec((tm, tn), lambda i, j, k: (i, j))
    (out,), got = _call(body, name, (a, b, *extras), [o_shape], grid=(M // tm, N // tn, nk),
                        in_specs=[a_spec, b_spec] + [e_spec] * ne, out_specs=[o_spec],
                        scratch_shapes=[pltpu.VMEM((tm, tn), F32)], exchange=exchange)
    return out if exchange is None else (out, got)


def _modulate(x, sc, sh):
    return x * (1.0 + sc) + sh


def _square(x):
    return x * x


def _mod_part(c_all, w_ada_s, b_s):
    def body(c_ref, w_ref, b_ref, mod_ref, cond_ref):
        cv = c_ref[...]
        cond = cv * _sigmoid(cv)
        cond_ref[...] = cond
        mod_ref[...] = _bdot(cond, w_ref[...]) + b_ref[...]

    return pl.pallas_call(
        body, name="mod_part",
        out_shape=[jax.ShapeDtypeStruct((N_DEV, w_ada_s.shape[1]), F32), jax.ShapeDtypeStruct(c_all.shape, F32)],
        compiler_params=_params(),
    )(c_all, w_ada_s, b_s)


def _rms_fwd(x, w):
    rs = lax.rsqrt(_lanemean(x * x) + RMS_EPS)
    return x * rs * w, rs


def _rms_bwd(x, rs, w, dy):
    xhat = x * rs
    dxh = dy * w
    return rs * (dxh - xhat * _lanemean(dxh * xhat)), dy * xhat


def _mla_pre(z, pos_col, invf, m_rot, wq_ext, wkv_ext, qnw, kvnw, exchange=None):
    T = z.shape[0]
    tm = min(ROW_TILE, T)

    def body(z_ref, pos_ref, invf_ref, mrot_ref, wq_ref, wkv_ref, qnw_ref, kvnw_ref,
             q_ref, k_ref, v_ref, c1_ref, s1_ref, cqn_ref, ckvn_ref):
        hi = slice(HEAD_DIM, QK_PAD)
        ang = pos_ref[...].astype(F32) * invf_ref[:, hi]
        c1 = jnp.concatenate([jnp.ones((tm, HEAD_DIM), F32), mrot_ref[:, hi] * jnp.cos(ang)], axis=1)
        s1 = jnp.concatenate([jnp.zeros((tm, HEAD_DIM), F32), mrot_ref[:, hi] * jnp.sin(ang)], axis=1)
        c1_ref[...] = c1
        s1_ref[...] = s1
        cqn, _ = _rms_fwd(z_ref[:, 0:256], qnw_ref[...])
        ckvn, _ = _rms_fwd(z_ref[:, 256:512], kvnw_ref[...])
        cqn_ref[...] = cqn.astype(BF16)
        ckvn_ref[...] = ckvn.astype(BF16)
        qe = _bdot(cqn, wq_ref[...], NT)
        kve = _bdot(ckvn, wkv_ref[...])
        k_rope = z_ref[:, 512:768] * c1 + z_ref[:, 768:1024] * s1
        for h in range(HEADS):
            q_ref[h] = ((qe[:, 256 * h:256 * h + 256] * c1 + qe[:, 1024 + 256 * h:1280 + 256 * h] * s1)
                        * Q_PRESCALE).astype(BF16)
            k_ref[h] = (kve[:, 256 * h:256 * h + 256] + k_rope).astype(BF16)
            v_ref[h] = kve[:, 1024 + 128 * h:1152 + 128 * h].astype(BF16)

    row = lambda i: (i, 0)
    head = lambda i: (0, i, 0)
    return _call(
        body, "mla_pre", (z, pos_col, invf, m_rot, wq_ext, wkv_ext, qnw, kvnw), grid=(T // tm,),
        in_specs=[pl.BlockSpec((tm, 1024), lambda i: (i, 2)), pl.BlockSpec((tm, 1), row),
                  _full((1, 256)), _full((1, 256)), _full(wq_ext.shape), _full(wkv_ext.shape),
                  _full((1, 256)), _full((1, 256))],
        out_specs=[pl.BlockSpec((HEADS, tm, QK_PAD), head), pl.BlockSpec((HEADS, tm, QK_PAD), head),
                   pl.BlockSpec((HEADS, tm, HEAD_DIM), head), pl.BlockSpec((tm, 256), row), pl.BlockSpec((tm, 256), row),
                   pl.BlockSpec((tm, 256), row), pl.BlockSpec((tm, 256), row)],
        out_shape=[jax.ShapeDtypeStruct((HEADS, T, QK_PAD), BF16), jax.ShapeDtypeStruct((HEADS, T, QK_PAD), BF16),
                   jax.ShapeDtypeStruct((HEADS, T, HEAD_DIM), BF16), jax.ShapeDtypeStruct((T, 256), F32),
                   jax.ShapeDtypeStruct((T, 256), F32), jax.ShapeDtypeStruct((T, 256), BF16),
                   jax.ShapeDtypeStruct((T, 256), BF16)], exchange=exchange)


def _mla_bwd(dq, dk, dv, z, c1, s1, wq_ext, wkv_ext, qnw, kvnw):
    T = z.shape[0]
    tm = min(ROW_TILE, T)

    def body(dq_ref, dk_ref, dv_ref, z_ref, c1_ref, s1_ref, wq_ref, wkv_ref, qnw_ref, kvnw_ref,
             dz_ref, dqe_ref, dkve_ref, dnw_ref):
        @pl.when(pl.program_id(0) == 0)
        def _():
            dnw_ref[...] = jnp.zeros_like(dnw_ref)

        c1, s1 = c1_ref[...], s1_ref[...]
        dkpe = jnp.zeros((tm, QK_PAD), F32)
        for h in range(HEADS):
            dqh, dkh = dq_ref[h].astype(F32) * ATT_SCALE, dk_ref[h]
            dqe_ref[:, 256 * h:256 * h + 256] = (dqh * c1).astype(BF16)
            dqe_ref[:, 1024 + 256 * h:1280 + 256 * h] = (dqh * s1).astype(BF16)
            dkve_ref[:, 256 * h:256 * h + 256] = dkh
            dkve_ref[:, 1024 + 128 * h:1152 + 128 * h] = dv_ref[h]
            dkpe = dkpe + dkh.astype(F32)
        dcqn = _dot(dqe_ref[...], wq_ref[...])
        dckvn = _dot(dkve_ref[...], wkv_ref[...], NT)
        cq, ckv = z_ref[:, 0:256], z_ref[:, 256:512]
        _, rsq = _rms_fwd(cq, qnw_ref[...])
        _, rskv = _rms_fwd(ckv, kvnw_ref[...])
        dcq, wq_rows = _rms_bwd(cq, rsq, qnw_ref[...], dcqn)
        dckv, wkv_rows = _rms_bwd(ckv, rskv, kvnw_ref[...], dckvn)
        dnw_ref[:, 0:256] += _rowsum(wq_rows)
        dnw_ref[:, 256:512] += _rowsum(wkv_rows)
        dz_ref[:, 0:256] = dcq.astype(BF16)
        dz_ref[:, 256:512] = dckv.astype(BF16)
        dz_ref[:, 512:768] = (dkpe * c1).astype(BF16)
        dz_ref[:, 768:1024] = (dkpe * s1).astype(BF16)

    row = lambda i: (i, 0)
    head = lambda i: (0, i, 0)
    return pl.pallas_call(
        body, name="mla_bwd", grid=(T // tm,),
        in_specs=[pl.BlockSpec((HEADS, tm, QK_PAD), head), pl.BlockSpec((HEADS, tm, QK_PAD), head),
                  pl.BlockSpec((HEADS, tm, HEAD_DIM), head), pl.BlockSpec((tm, 1024), lambda i: (i, 2)),
                  pl.BlockSpec((tm, 256), row), pl.BlockSpec((tm, 256), row), _full(wq_ext.shape), _full(wkv_ext.shape),
                  _full((1, 256)), _full((1, 256))],
        out_specs=[pl.BlockSpec((tm, 1024), row), pl.BlockSpec((tm, 2048), row), pl.BlockSpec((tm, 1536), row),
                   _full((1, 512))],
        out_shape=[jax.ShapeDtypeStruct((T, 1024), BF16), jax.ShapeDtypeStruct((T, 2048), BF16),
                   jax.ShapeDtypeStruct((T, 1536), BF16), jax.ShapeDtypeStruct((1, 512), F32)],
        compiler_params=_params(),
    )(dq, dk, dv, z, c1, s1, wq_ext, wkv_ext, qnw, kvnw)


_HEAD_LANES = [slice(HEAD_DIM * h, HEAD_DIM * (h + 1)) for h in range(HEADS)]


def _lower_bound(lbraw_ref):
    a0, a1 = lbraw_ref[0:1, :], lbraw_ref[1:2, :]
    mx = jnp.maximum(a0, a1)
    e0, e1 = jnp.exp(a0 - mx), jnp.exp(a1 - mx)
    return e0 / (e0 + e1)


def _tri(lower):
    r = lax.broadcasted_iota(jnp.int32, (CHUNK, CHUNK), 0)
    c = lax.broadcasted_iota(jnp.int32, (CHUNK, CHUNK), 1)
    return (r >= c) if lower else (r <= c)


def _hgrn_gates(q, f, lb, tri_lo):
    sg = _sigmoid(f)
    forget = lb + (1.0 - lb) * sg
    k = 1.0 - forget
    b = _hdot(tri_lo.astype(F32), jnp.log(forget))
    b_ref, b_last = b[CHUNK // 2 - 1:CHUNK // 2, :], b[CHUNK - 1:CHUNK, :]
    e1, e2, e3, e4 = jnp.exp(b - b_ref), jnp.exp(b_ref - b), jnp.exp(b_last - b), jnp.exp(b)
    return dict(sg=sg, forget=forget, k=k, e1=e1, e2=e2, e3=e3, e4=e4, qa=q * e1, ka=k * e2, kl=k * e3, qb=q * e4,
                decay=jnp.exp(b_last))


def _hgrn_fwd(z, lbraw, nw, exchange=None):
    T = z.shape[0]
    G = min(HGRN_GROUP, T // CHUNK)
    rows = G * CHUNK
    n_chunks = T // CHUNK

    def body(q_ref, f_ref, i_ref, g_ref, lbraw_ref, nw_ref, oraw_ref, og_ref, sp_ref, st_ref):
        @pl.when(pl.program_id(0) == 0)
        def _():
            st_ref[...] = jnp.zeros_like(st_ref)

        lb_all = _lower_bound(lbraw_ref)
        tri_lo = _tri(True)

        def chunk(cc, carry):
            rs = pl.ds(pl.multiple_of(cc * CHUNK, CHUNK), CHUNK)
            t = _hgrn_gates(q_ref[rs, :], f_ref[rs, :], lb_all, tri_lo)
            v, gate = i_ref[rs, :], g_ref[rs, :]
            st = [st_ref[h] for h in range(HEADS)]
            a = [jnp.where(tri_lo, _bdot(t["qa"][:, s], t["ka"][:, s], NT), 0.0) for s in _HEAD_LANES]
            kv = [_bdot(v[:, s], t["kl"][:, s], TN) for s in _HEAD_LANES]
            o = [_bdot(a[h], v[:, s]) + _bdot(t["qb"][:, s], st[h], NT) for h, s in enumerate(_HEAD_LANES)]
            for h, s in enumerate(_HEAD_LANES):
                sp_ref[cc, h] = st[h]
                st_ref[h] = st[h] * t["decay"][:, s] + kv[h]
            oraw_ref[rs, :] = jnp.concatenate(o, axis=1)
            on = jnp.concatenate([_rms_fwd(o[h], nw_ref[:, s])[0] for h, s in enumerate(_HEAD_LANES)], axis=1)
            og_ref[rs, :] = (on * (gate * _sigmoid(gate))).astype(BF16)
            return carry

        lax.fori_loop(0, G, chunk, 0, unroll=4)

    col = lambda j: pl.BlockSpec((rows, 512), lambda r, j=j: (r, j))
    return _call(
        body, "hgrn_fwd", (z, z, z, z, lbraw, nw), grid=(T // rows,),
        in_specs=[col(0), col(1), col(2), col(3), _full((2, 512)), _full((1, 512))],
        out_specs=[col(0), col(0), pl.BlockSpec((G, HEADS, HEAD_DIM, HEAD_DIM), lambda r: (r, 0, 0, 0))],
        out_shape=[jax.ShapeDtypeStruct((T, 512), F32), jax.ShapeDtypeStruct((T, 512), BF16),
                   jax.ShapeDtypeStruct((n_chunks, HEADS, HEAD_DIM, HEAD_DIM), F32)],
        scratch_shapes=[pltpu.VMEM((HEADS, HEAD_DIM, HEAD_DIM), F32)], exchange=exchange)


def _hgrn_bwd(dmixcat, z, oraw, sprev, lbraw, nw, exchange=None):
    T = z.shape[0]
    G = min(HGRN_GROUP, T // CHUNK)
    rows = G * CHUNK
    ng = T // rows

    def body(dog_ref, q_ref, f_ref, i_ref, g_ref, oraw_ref, sp_ref, lbraw_ref, nw_ref,
             dz_ref, dsmall_ref, dst_ref):
        @pl.when(pl.program_id(0) == 0)
        def _():
            dst_ref[...] = jnp.zeros_like(dst_ref)
            dsmall_ref[...] = jnp.zeros_like(dsmall_ref)

        lb_all = _lower_bound(lbraw_ref)
        tri_lo, tri_up = _tri(True), _tri(False)
        rowid = lax.broadcasted_iota(jnp.int32, (CHUNK, HEADS * HEAD_DIM), 0)

        def chunk(it, carry):
            cc = G - 1 - it
            rs = pl.ds(pl.multiple_of(cc * CHUNK, CHUNK), CHUNK)
            heads = list(enumerate(_HEAD_LANES))
            cat = lambda parts: jnp.concatenate(parts, axis=1)
            per_head_mean = lambda x: cat([jnp.broadcast_to(_lanemean(x[:, s]), (CHUNK, HEAD_DIM)) for s in _HEAD_LANES])
            t = _hgrn_gates(q_ref[rs, :], f_ref[rs, :], lb_all, tri_lo)
            v, gate, o, dog, nw_all = i_ref[rs, :], g_ref[rs, :], oraw_ref[rs, :], dog_ref[rs, :], nw_ref[...]
            rs_o = lax.rsqrt(per_head_mean(o * o) + RMS_EPS)
            xhat = o * rs_o
            sgg = _sigmoid(gate)
            d_on = dog * (gate * sgg)
            dz_ref[rs, 1536:2048] = (dog * (xhat * nw_all) * (sgg * (1.0 + gate * (1.0 - sgg)))).astype(BF16)
            dxh = d_on * nw_all
            do = rs_o * (dxh - xhat * per_head_mean(dxh * xhat))
            dsmall_ref[:, 512:1024] += _rowsum(d_on * xhat)
            st = [sp_ref[cc, h] for h in range(HEADS)]
            dst = [dst_ref[h] for h in range(HEADS)]
            a = [jnp.where(tri_lo, _bdot(t["qa"][:, s], t["ka"][:, s], NT), 0.0) for s in _HEAD_LANES]
            da = [jnp.where(tri_lo, _bdot(do[:, s], v[:, s], NT), 0.0) for s in _HEAD_LANES]
            dqb = cat([_bdot(do[:, s], st[h]) for h, s in heads])
            dkl = cat([_bdot(v[:, s], dst[h]) for h, s in heads])
            dv_ = cat([_bdot(t["kl"][:, s], dst[h], NT) + _bdot(a[h], do[:, s], TN) for h, s in heads])
            dqa = cat([_bdot(da[h], t["ka"][:, s]) for h, s in heads])
            dka = cat([_bdot(da[h], t["qa"][:, s], TN) for h, s in heads])
            ddecay = cat([_rowsum(dst[h] * st[h]) for h in range(HEADS)])
            for h, s in heads:
                dst_ref[h] = dst[h] * t["decay"][:, s] + _bdot(do[:, s], t["qb"][:, s], TN)
            pa, pk, pb, pl_ = dqa * t["qa"], dka * t["ka"], dqb * t["qb"], dkl * t["kl"]
            db = pa - pk + pb - pl_
            db = db + jnp.where(rowid == CHUNK // 2 - 1, _rowsum(pk - pa), 0.0)
            db = db + jnp.where(rowid == CHUNK - 1, _rowsum(pl_) + ddecay * t["decay"], 0.0)
            dlogf = _hdot(tri_up.astype(F32), db)
            dforget = dlogf / t["forget"] - (dka * t["e2"] + dkl * t["e3"])
            sg = t["sg"]
            dz_ref[rs, 0:512] = (dqa * t["e1"] + dqb * t["e4"]).astype(BF16)
            dz_ref[rs, 512:1024] = (dforget * (1.0 - lb_all) * sg * (1.0 - sg)).astype(BF16)
            dz_ref[rs, 1024:1536] = dv_.astype(BF16)
            dsmall_ref[:, 0:512] += _rowsum(dforget * (1.0 - sg))
            return carry

        lax.fori_loop(0, G, chunk, 0, unroll=4)

    col = lambda j: pl.BlockSpec((rows, 512), lambda r, j=j: (ng - 1 - r, j))
    return _call(
        body, "hgrn_bwd", (dmixcat, z, z, z, z, oraw, sprev, lbraw, nw), grid=(ng,),
        in_specs=[col(0), col(0), col(1), col(2), col(3), col(0),
                  pl.BlockSpec((G, HEADS, HEAD_DIM, HEAD_DIM), lambda r: (ng - 1 - r, 0, 0, 0)),
                  _full((2, 512)), _full((1, 512))],
        out_specs=[pl.BlockSpec((rows, 2048), lambda r: (ng - 1 - r, 0)), _full((1, 1024))],
        out_shape=[jax.ShapeDtypeStruct((T, 2048), BF16), jax.ShapeDtypeStruct((1, 1024), F32)],
        scratch_shapes=[pltpu.VMEM((HEADS, HEAD_DIM, HEAD_DIM), F32)], exchange=exchange)


def _diag_mask(t):
    r = lax.broadcasted_iota(jnp.int32, (t, t), 0)
    c = lax.broadcasted_iota(jnp.int32, (t, t), 1)
    return r >= c


def _attn_fwd(q, k, v, exchange=None):
    _, T, _ = q.shape
    t = min(ATT_TILE, T)

    def body(q_ref, k_ref, v_ref, o_ref, lse_ref):
        i = pl.program_id(1)
        qb = q_ref[...]

        rows = lambda j: pl.ds(pl.multiple_of(j * t, t), t)

        def logits(j, masked):
            s = _dot(qb, k_ref[rows(j), :], NT)
            return jnp.where(_diag_mask(t), s, NEG_BIG) if masked else s

        def absorb(s, j, carry):
            m, l, acc = carry
            mn = jnp.maximum(m, jnp.max(s, axis=-1, keepdims=True))
            p = jnp.exp2(s - mn)
            al = jnp.exp2(m - mn)
            return mn, al * l + jnp.sum(p, axis=-1, keepdims=True), al * acc + _dot(p.astype(BF16), v_ref[rows(j), :])

        def pair(j0, carry, last_masked):
            s0, s1 = logits(j0, False), logits(j0 + 1, last_masked)
            return absorb(s1, j0 + 1, absorb(s0, j0, carry))

        init = (jnp.full((t, 1), NEG_BIG, F32), jnp.zeros((t, 1), F32), jnp.zeros((t, HEAD_DIM), F32))
        carry = lax.fori_loop(0, i // 2, lambda jj, c: pair(2 * jj, c, False), init)
        m, l, acc = lax.cond(i % 2 == 1, lambda c: pair(i - 1, c, True),
                             lambda c: absorb(logits(i, True), i, c), carry)
        o_ref[...] = acc / l
        lse_ref[...] = jnp.broadcast_to(m + jnp.log2(l), (t, HEAD_DIM))

    return _call(
        body, "attn_fwd", (q, k, v), grid=(HEADS, T // t),
        in_specs=[pl.BlockSpec((None, t, QK_PAD), lambda h, i: (h, i, 0)),
                  pl.BlockSpec((None, T, QK_PAD), lambda h, i: (h, 0, 0)),
                  pl.BlockSpec((None, T, HEAD_DIM), lambda h, i: (h, 0, 0))],
        out_specs=[pl.BlockSpec((t, HEAD_DIM), lambda h, i: (i, h)),
                   pl.BlockSpec((None, t, HEAD_DIM), lambda h, i: (h, i, 0))],
        out_shape=[jax.ShapeDtypeStruct((T, HEADS * HEAD_DIM), F32), jax.ShapeDtypeStruct((HEADS, T, HEAD_DIM), F32)],
        exchange=exchange)


def _attn_bwd(q, k, v, dmixcat, o, lse, exchange=None):
    _, T, _ = q.shape
    t = min(ATT_TILE, T)
    nq = T // t

    def body(q_ref, k_ref, v_ref, do_ref, o_ref, lse_ref, dq_ref, dk_ref, dv_ref, delta_ref, dq_acc):
        j = pl.program_id(1)

        @pl.when(j == 0)
        def _():
            dq_acc[...] = jnp.zeros_like(dq_acc)

            def fill(i, carry):
                rs = pl.ds(pl.multiple_of(i * t, t), t)
                delta_ref[rs, :] = jnp.broadcast_to(
                    jnp.sum(do_ref[rs, :] * o_ref[rs, :], axis=-1, keepdims=True), (t, HEAD_DIM))
                return carry

            lax.fori_loop(0, nq, fill, 0)

        kb, vb = k_ref[...], v_ref[...]

        def steps(blocks, carry):
            dk, dv = carry
            rs = [pl.ds(pl.multiple_of(i * t, t), t) for i, _ in blocks]
            qb = [q_ref[r, :] for r in rs]
            dob = [do_ref[r, :].astype(BF16) for r in rs]
            s = [_dot(b, kb, NT) for b in qb]
            dp = [_dot(b, vb, NT) for b in dob]
            for n, (_, masked) in enumerate(blocks):
                p = jnp.exp2(s[n] - lse_ref[rs[n], 0:1])
                if masked:
                    p = jnp.where(_diag_mask(t), p, 0.0)
                ds = (p * (dp[n] - delta_ref[rs[n], 0:1])).astype(BF16)
                dq_acc[rs[n], :] += _dot(ds, kb)
                dk = dk + _dot(ds, qb[n], TN)
                dv = dv + _dot(p.astype(BF16), dob[n], TN)
            return dk, dv

        zero = (jnp.zeros((t, QK_PAD), F32), jnp.zeros((t, HEAD_DIM), F32))
        rest = nq - 1 - j
        carry = lax.cond(rest % 2 == 1, lambda c: steps([(j, True), (j + 1, False)], c),
                         lambda c: steps([(j, True)], c), zero)
        first = j + 1 + rest % 2
        dk, dv = lax.fori_loop(0, rest // 2, lambda n, c: steps([(first + 2 * n, False), (first + 2 * n + 1, False)], c),
                               carry)
        dk_ref[...] = (dk * LN2).astype(BF16)
        dv_ref[...] = dv.astype(BF16)

        @pl.when(j == nq - 1)
        def _():
            dq_ref[...] = dq_acc[...].astype(BF16)

    return _call(
        body, "attn_bwd", (q, k, v, dmixcat, o, lse), grid=(HEADS, nq),
        in_specs=[pl.BlockSpec((None, T, QK_PAD), lambda h, j: (h, 0, 0)),
                  pl.BlockSpec((None, t, QK_PAD), lambda h, j: (h, j, 0)),
                  pl.BlockSpec((None, t, HEAD_DIM), lambda h, j: (h, j, 0)),
                  pl.BlockSpec((T, HEAD_DIM), lambda h, j: (0, HEADS + h)),
                  pl.BlockSpec((T, HEAD_DIM), lambda h, j: (0, h)),
                  pl.BlockSpec((None, T, HEAD_DIM), lambda h, j: (h, 0, 0))],
        out_specs=[pl.BlockSpec((None, T, QK_PAD), lambda h, j: (h, 0, 0)),
                   pl.BlockSpec((None, t, QK_PAD), lambda h, j: (h, j, 0)),
                   pl.BlockSpec((None, t, HEAD_DIM), lambda h, j: (h, j, 0))],
        out_shape=[jax.ShapeDtypeStruct((HEADS, T, QK_PAD), BF16), jax.ShapeDtypeStruct((HEADS, T, QK_PAD), BF16),
                   jax.ShapeDtypeStruct((HEADS, T, HEAD_DIM), BF16)],
        scratch_shapes=[pltpu.VMEM((T, HEAD_DIM), F32), pltpu.VMEM((T, QK_PAD), F32)], exchange=exchange)


def _ln_fwd(r):
    mu = _lanemean(r)
    xc = r - mu
    rstd = lax.rsqrt(_lanemean(xc * xc) + LN_EPS)
    return xc * rstd, rstd


def _ln_bwd(dxh, xhat, rstd):
    return rstd * (dxh - _lanemean(dxh) - xhat * _lanemean(dxh * xhat))


def _mix_ln1(o_hg, o_mla, w_out, x, g_a, ln1_g, ln1_b, sc_m, sh_m, exchange=None):
    T = x.shape[0]
    tm = min(ROW_TILE, T)
    half = o_hg.shape[1]

    def body(hg_ref, mla_ref, w_ref, x_ref, ga_ref, g_ref, b_ref, sc_ref, sh_ref, mix_ref, xhat_ref, rstd_ref, u2_ref):
        mix = _dot(hg_ref[...], w_ref[0:half, :]) + _bdot(mla_ref[...], w_ref[half:, :])
        mix_ref[...] = mix
        xhat, rstd = _ln_fwd(ALPHA * x_ref[...] + (1.0 + ga_ref[...]) * mix)
        xhat_ref[...] = xhat
        rstd_ref[...] = jnp.broadcast_to(rstd, (tm, 128))
        u2_ref[...] = _modulate(xhat * g_ref[...] + b_ref[...], sc_ref[...], sh_ref[...]).astype(BF16)

    row = pl.BlockSpec((tm, D_MODEL), lambda i: (i, 0))
    vec = _full((1, D_MODEL))
    halfrow = pl.BlockSpec((tm, half), lambda i: (i, 0))
    return _call(
        body, "mix_ln1", (o_hg, o_mla, w_out, x, g_a, ln1_g, ln1_b, sc_m, sh_m), grid=(T // tm,),
        in_specs=[halfrow, halfrow, _full(w_out.shape), row, vec, vec, vec, vec, vec],
        out_specs=[row, row, pl.BlockSpec((tm, 128), lambda i: (i, 0)), row],
        out_shape=[jax.ShapeDtypeStruct((T, D_MODEL), F32), jax.ShapeDtypeStruct((T, D_MODEL), F32),
                   jax.ShapeDtypeStruct((T, 128), F32), jax.ShapeDtypeStruct((T, D_MODEL), BF16)],
        exchange=exchange)


def _mlp_fwd(u2, w1, w2, xhat1, ln1_g, ln1_b, g_m, ln2_g, ln2_b, target):
    T = u2.shape[0]
    tf = w1.shape[-1]
    nf = N_DEV // MLP_SLABS
    tm = min(ROW_TILE, T)

    def body(u2_ref, w1_ref, w2_ref, xhat_ref, g1_ref, b1_ref, gm_ref, g2_ref, b2_ref, tgt_ref,
             r_ref, dr2_ref, dh_ref, small_ref, acc_ref):
        i, f = pl.program_id(0), pl.program_id(1)
        dm = D_MODEL

        @pl.when((i == 0) & (f == 0))
        def _():
            small_ref[...] = jnp.zeros_like(small_ref)

        @pl.when(f == 0)
        def _():
            acc_ref[...] = jnp.zeros_like(acc_ref)

        u2t = u2_ref[...]
        part = None
        for s in range(MLP_SLABS):
            r = jnp.maximum(_dot(u2t, w1_ref[s]), 0.0)
            r_ref[:, s * tf:(s + 1) * tf] = r.astype(BF16)
            d = _bdot(r * r, w2_ref[s])
            part = d if part is None else part + d
        acc_ref[...] += part

        @pl.when(f == nf - 1)
        def _():
            h = acc_ref[...]
            x1 = xhat_ref[...] * g1_ref[...] + b1_ref[...]
            xhat2, rstd2 = _ln_fwd(ALPHA * x1 + (1.0 + gm_ref[...]) * h)
            err = xhat2 * g2_ref[...] + b2_ref[...] - tgt_ref[...]
            small_ref[:, 3 * dm:] += jnp.sum(0.5 * _lanemean(err * err), axis=0, keepdims=True)
            dy = err * (1.0 / D_MODEL)
            small_ref[:, dm:2 * dm] += _rowsum(dy * xhat2)
            small_ref[:, 2 * dm:3 * dm] += _rowsum(dy)
            dr2 = _ln_bwd(dy * g2_ref[...], xhat2, rstd2)
            dr2_ref[...] = dr2
            small_ref[:, 0:dm] += _rowsum(dr2 * h)
            dh_ref[...] = ((1.0 + gm_ref[...]) * dr2).astype(BF16)

    row = pl.BlockSpec((tm, D_MODEL), lambda i, f: (i, 0))
    vec = _full((1, D_MODEL))
    return pl.pallas_call(
        body, name="mlp_fwd", grid=(T // tm, nf),
        in_specs=[row, pl.BlockSpec((MLP_SLABS, D_MODEL, tf), lambda i, f: (f, 0, 0)),
                  pl.BlockSpec((MLP_SLABS, tf, D_MODEL), lambda i, f: (f, 0, 0)),
                  row, vec, vec, vec, vec, vec, row],
        out_specs=[pl.BlockSpec((tm, MLP_SLABS * tf), lambda i, f: (i, f)), row, row, _full((1, 3 * D_MODEL + 128))],
        out_shape=[jax.ShapeDtypeStruct((T, N_DEV * tf), BF16), jax.ShapeDtypeStruct((T, D_MODEL), F32),
                   jax.ShapeDtypeStruct((T, D_MODEL), BF16), jax.ShapeDtypeStruct((1, 3 * D_MODEL + 128), F32)],
        scratch_shapes=[pltpu.VMEM((tm, D_MODEL), F32)],
        compiler_params=_params(),
    )(u2, w1, w2, xhat1, ln1_g, ln1_b, g_m, ln2_g, ln2_b, target)


def _mlp_bwd(dh, w1, w2, r, dr2, xhat1, rstd1, mix, ln1_g, ln1_b, sc_m, g_a):
    T = dh.shape[0]
    tf = w1.shape[-1]
    nf = N_DEV // MLP_SLABS
    tm = min(ROW_TILE, T)

    def body(dh_ref, w1_ref, w2_ref, r_ref, dr2_ref, xhat_ref, rstd_ref, mix_ref, g1_ref, b1_ref, sc_ref, ga_ref,
             dhpre_ref, dr1_ref, dmix_ref, small_ref, acc_ref):
        i, f = pl.program_id(0), pl.program_id(1)
        dm = D_MODEL

        @pl.when((i == 0) & (f == 0))
        def _():
            small_ref[...] = jnp.zeros_like(small_ref)

        @pl.when(f == 0)
        def _():
            acc_ref[...] = jnp.zeros_like(acc_ref)

        dht = dh_ref[...]
        part = None
        for s in range(MLP_SLABS):
            cols = slice(s * tf, (s + 1) * tf)
            dhpre = (_dot(dht, w2_ref[s], NT) * (2.0 * r_ref[:, cols].astype(F32))).astype(BF16)
            dhpre_ref[:, cols] = dhpre
            d = _dot(dhpre, w1_ref[s], NT)
            part = d if part is None else part + d
        acc_ref[...] += part

        @pl.when(f == nf - 1)
        def _():
            du2 = acc_ref[...]
            xhat = xhat_ref[...]
            x1 = xhat * g1_ref[...] + b1_ref[...]
            dx1 = ALPHA * dr2_ref[...] + du2 * (1.0 + sc_ref[...])
            small_ref[:, 2 * dm:3 * dm] += _rowsum(du2 * x1)
            small_ref[:, dm:2 * dm] += _rowsum(du2)
            small_ref[:, 3 * dm:4 * dm] += _rowsum(dx1 * xhat)
            small_ref[:, 4 * dm:5 * dm] += _rowsum(dx1)
            dr1 = _ln_bwd(dx1 * g1_ref[...], xhat, rstd_ref[:, 0:1])
            dr1_ref[...] = dr1
            small_ref[:, 0:dm] += _rowsum(dr1 * mix_ref[...])
            dmix_ref[...] = ((1.0 + ga_ref[...]) * dr1).astype(BF16)

    row = pl.BlockSpec((tm, D_MODEL), lambda i, f: (i, 0))
    vec = _full((1, D_MODEL))
    return pl.pallas_call(
        body, name="mlp_bwd", grid=(T // tm, nf),
        in_specs=[row, pl.BlockSpec((MLP_SLABS, D_MODEL, tf), lambda i, f: (f, 0, 0)),
                  pl.BlockSpec((MLP_SLABS, tf, D_MODEL), lambda i, f: (f, 0, 0)),
                  pl.BlockSpec((tm, MLP_SLABS * tf), lambda i, f: (i, f)), row, row,
                  pl.BlockSpec((tm, 128), lambda i, f: (i, 0)), row, vec, vec, vec, vec],
        out_specs=[pl.BlockSpec((tm, MLP_SLABS * tf), lambda i, f: (i, f)), row, row, _full((1, 5 * D_MODEL))],
        out_shape=[jax.ShapeDtypeStruct((T, N_DEV * tf), BF16), jax.ShapeDtypeStruct((T, D_MODEL), F32),
                   jax.ShapeDtypeStruct((T, D_MODEL), BF16), jax.ShapeDtypeStruct((1, 5 * D_MODEL), F32)],
        scratch_shapes=[pltpu.VMEM((tm, D_MODEL), F32)],
        compiler_params=_params(),
    )(dh, w1, w2, r, dr2, xhat1, rstd1, mix, ln1_g, ln1_b, sc_m, g_a)


def _input_bwd(dz_h, dz_m, w_in_ext, x, dr1, sc_a, exchange=None):
    T = x.shape[0]
    tm = min(ROW_TILE, T)

    def body(dzh_ref, dzm_ref, w_ref, x_ref, dr1_ref, sc_ref, gx_ref, small_ref):
        @pl.when(pl.program_id(0) == 0)
        def _():
            small_ref[...] = jnp.zeros_like(small_ref)

        du = _bdot(dzh_ref[...], w_ref[0:2048, :]) + _bdot(dzm_ref[...], w_ref[2048:3072, :])
        gx_ref[...] = ALPHA * dr1_ref[...] + du * (1.0 + sc_ref[...])
        small_ref[:, D_MODEL:] += _rowsum(du * x_ref[...])
        small_ref[:, 0:D_MODEL] += _rowsum(du)

    row = pl.BlockSpec((tm, D_MODEL), lambda i: (i, 0))
    vec = _full((1, D_MODEL))
    return _call(
        body, "input_bwd", (dz_h, dz_m, w_in_ext, x, dr1, sc_a), grid=(T // tm,),
        in_specs=[pl.BlockSpec((tm, 2048), lambda i: (i, 0)), row, _full(w_in_ext.shape), row, row, vec],
        out_specs=[row, _full((1, 2 * D_MODEL))],
        out_shape=[jax.ShapeDtypeStruct((T, D_MODEL), F32), jax.ShapeDtypeStruct((1, 2 * D_MODEL), F32)],
        exchange=exchange)


def _adam_math(w, g, m, v):
    m = ADAM_B1 * m + (1.0 - ADAM_B1) * g
    v = ADAM_B2 * v + (1.0 - ADAM_B2) * (g * g)
    m_hat = m * (1.0 / (1.0 - ADAM_B1 ** ADAM_STEP))
    v_hat = v * (1.0 / (1.0 - ADAM_B2 ** ADAM_STEP))
    return -ADAM_LR * (m_hat / (jnp.sqrt(v_hat) + ADAM_EPS) + ADAM_WD * w), m, v


def _adam(g_slabs, w, m, v, name, g_fn=None, g_extra=()):
    R, C = w.shape
    tr = 256 if R % 256 == 0 else R
    ns = 0 if g_slabs is None else g_slabs.shape[0]
    slab_rows = tr if g_slabs is None or g_slabs.shape[1] == R else g_slabs.shape[1]
    assert slab_rows == tr or tr == R
    ne = len(g_extra)

    def body(*refs):
        e_refs = refs[:ne]
        refs = refs[ne:]
        if ns:
            gs_ref, refs = refs[0], refs[1:]
        w_ref, m_ref, v_ref, g_ref, d_ref, nm_ref, nv_ref = refs
        if g_fn is not None:
            g = g_fn(*e_refs)
        else:
            g = gs_ref[0].astype(F32)
            for s in range(1, ns):
                g = g + gs_ref[s].astype(F32)
            g = g[:tr]
        d, nm, nv = _adam_math(w_ref[...], g, m_ref[...], v_ref[...])
        g_ref[...] = g
        d_ref[...] = d
        nm_ref[...] = nm
        nv_ref[...] = nv

    blk = pl.BlockSpec((tr, C), lambda i: (i, 0))
    in_specs = [pl.BlockSpec((tr, e.shape[1]), lambda i: (i, 0)) if e.shape[0] == R else _full(e.shape) for e in g_extra]
    args = list(g_extra)
    if ns:
        in_specs.append(pl.BlockSpec((ns, slab_rows, C), lambda i: (0, i, 0)))
        args.append(g_slabs)
    return pl.pallas_call(
        body, name=name, grid=(R // tr,), in_specs=in_specs + [blk] * 3, out_specs=[blk] * 4,
        out_shape=[jax.ShapeDtypeStruct((R, C), F32)] * 4, compiler_params=_params(),
    )(*args, w, m, v)


def _adam_small(small_all, params):
    n = len(params)

    def body(*refs):
        s_ref, refs = refs[0], refs[1:]
        wmv, loss_ref, outs = refs[:3 * n], refs[3 * n], refs[3 * n + 1:]
        tot = s_ref[0]
        for i in range(1, N_DEV):
            tot = tot + s_ref[i]
        loss_ref[...] = tot[:, SMALL_W - 128:]
        for j, (w, _, _, off) in enumerate(params):
            w_ref, m_ref, v_ref = wmv[3 * j:3 * j + 3]
            g_ref, d_ref, nm_ref, nv_ref = outs[4 * j:4 * j + 4]
            if w.shape[0] == 2:
                lb = _lower_bound(w_ref)
                g0 = tot[:, off:off + w.shape[1]] * lb * (1.0 - lb)
                rows = [(slice(0, 1), g0), (slice(1, 2), -g0)]
            else:
                rows = [(slice(0, 1), tot[:, off:off + w.shape[1]])]
            for rs, g in rows:
                d, nm, nv = _adam_math(w_ref[rs, :], g, m_ref[rs, :], v_ref[rs, :])
                g_ref[rs, :], d_ref[rs, :], nm_ref[rs, :], nv_ref[rs, :] = g, d, nm, nv

    out_shape = [jax.ShapeDtypeStruct((1, 128), F32)]
    for w, _, _, _ in params:
        out_shape += [jax.ShapeDtypeStruct(w.shape, F32)] * 4
    res = pl.pallas_call(body, name="adam_small", out_shape=out_shape, compiler_params=_params())(
        small_all, *[a for w, m, v, _ in params for a in (w, m, v)])
    return res[0], [tuple(res[1 + 4 * j:5 + 4 * j]) for j in range(n)]


def _cols_from_slabs(g):
    s, r, c = g.shape
    return jnp.transpose(g, (1, 0, 2)).reshape(r, s * c)


def _slabs_from_cols(w):
    r, c = w.shape
    return jnp.transpose(w.reshape(r, N_DEV, c // N_DEV), (1, 0, 2))


def _rot_half_rows(wt):
    return jnp.concatenate([-wt[32:], wt[:32]], axis=0)


def _unrot_half_rows(dwt_rot):
    return jnp.concatenate([dwt_rot[32:], -dwt_rot[:32]], axis=0)


def _ext_in_t(g):
    k_in = g.shape[2]
    z64, z128 = jnp.zeros((64, k_in), BF16), jnp.zeros((128, k_in), BF16)
    wt = g.reshape(N_DEV * g.shape[1], k_in)
    main, wk = wt[:wt.shape[0] - ROPE_DIM], wt[wt.shape[0] - ROPE_DIM:]
    return jnp.concatenate([main, z128, wk, z64, z128, _rot_half_rows(wk), z64], axis=0)


def _ext_q_t(wt):
    r = wt.shape[1]
    z64, z128 = jnp.zeros((64, r), BF16), jnp.zeros((128, r), BF16)
    per = HEAD_DIM + ROPE_DIM
    main = [jnp.concatenate([wt[per * h:per * (h + 1)], z64], axis=0) for h in range(HEADS)]
    rot = [jnp.concatenate([z128, _rot_half_rows(wt[per * h + HEAD_DIM:per * (h + 1)]), z64], axis=0)
           for h in range(HEADS)]
    return jnp.concatenate(main + rot, axis=0)


def _ext_kv(w_kv_up):
    r = w_kv_up.shape[0]
    z128 = jnp.zeros((r, 128), BF16)
    wkv = w_kv_up.reshape(r, HEADS, 2 * HEAD_DIM)
    kpad = [jnp.concatenate([wkv[:, h, :HEAD_DIM], z128], axis=1) for h in range(HEADS)]
    vals = [wkv[:, h, HEAD_DIM:] for h in range(HEADS)]
    return jnp.concatenate(kpad + vals, axis=1)


def _grad_in_from_ext_t(dwt_h, dwt_m):
    dwk = dwt_m[512 + 128:512 + 192] + _unrot_half_rows(dwt_m[768 + 128:768 + 192])
    return jnp.concatenate([dwt_h, dwt_m[:512], dwk], axis=0)


def _grad_q_from_ext_t(dwq_ext_t):
    rows = []
    for h in range(HEADS):
        main, rot = dwq_ext_t[256 * h:256 * h + 256], dwq_ext_t[1024 + 256 * h:1280 + 256 * h]
        rows += [main[:128], main[128:192] + _unrot_half_rows(rot[128:192])]
    return jnp.concatenate(rows, axis=0)


def _grad_kv_from_ext(dwkv_ext):
    kvcols = []
    for h in range(HEADS):
        kvcols += [dwkv_ext[:, 256 * h:256 * h + 128], dwkv_ext[:, 1024 + 128 * h:1152 + 128 * h]]
    return jnp.concatenate(kvcols, axis=1)


SMALL_W = 6144 + 512 + 512 + 256 + 256 + 4 * 1024 + 128


def kernel(x, c, positions, w_ada, b_ada, w_in, hg_lower_bounds, hg_norm_w, mla_q_norm_w, w_q_up, mla_kv_norm_w, w_kv_up, w_out, ln1_g, ln1_b, w_mlp_in, w_mlp_out, ln2_g, ln2_b, loss_target, m_w_ada, m_b_ada, m_w_in, m_hg_lower_bounds, m_hg_norm_w, m_mla_q_norm_w, m_w_q_up, m_mla_kv_norm_w, m_w_kv_up, m_w_out, m_ln1_g, m_ln1_b, m_w_mlp_in, m_w_mlp_out, m_ln2_g, m_ln2_b, v_w_ada, v_b_ada, v_w_in, v_hg_lower_bounds, v_hg_norm_w, v_mla_q_norm_w, v_w_q_up, v_mla_kv_norm_w, v_w_kv_up, v_w_out, v_ln1_g, v_ln1_b, v_w_mlp_in, v_w_mlp_out, v_ln2_g, v_ln2_b):
    T = x.shape[1]
    me = 4 * lax.axis_index("x") + 2 * lax.axis_index("y") + lax.axis_index("c")
    xs, tgt = x[0], loss_target[0]
    transposed = ("w_in", "w_q_up")
    as_used = lambda n, a: a[0].T if n in transposed else a[0]
    big = {n: as_used(n, a) for n, a in dict(w_in=w_in, w_q_up=w_q_up, w_kv_up=w_kv_up, w_out=w_out,
                                              w_mlp_in=w_mlp_in, w_mlp_out=w_mlp_out).items()}
    names = list(big)

    bf = {n: big[n].astype(BF16) for n in names}
    g_in, g_c = _gather_two_level([bf["w_in"], c], name="gather_w_in")
    c_all = g_c.reshape(N_DEV, D_MODEL)

    ada_cols = w_ada.shape[2]
    mod_part, cond = _mod_part(c_all, w_ada[0], lax.dynamic_slice(b_ada, (0, me * ada_cols), (1, ada_cols)))
    (mod_all,) = _exchange([mod_part], scatter=False, name="gather_mod")
    mod_row = lax.dynamic_slice(mod_all, (0, me, 0), (N_DEV, 1, ada_cols)).reshape(1, N_DEV * ada_cols)
    sh_a, sc_a, g_a, sh_m, sc_m, g_m = [mod_row[:, D_MODEL * i:D_MODEL * (i + 1)] for i in range(6)]

    w_in_ext = _ext_in_t(g_in)
    z, (g_q, g_kv, g_out) = _matmul(xs, w_in_ext, "NT", "in_proj", a_fn=_modulate, extras=(sc_a, sh_a), tn=3072,
                                    exchange=_Exchange([bf["w_q_up"], bf["w_kv_up"], bf["w_out"]], False))
    wq_ext = _ext_q_t(g_q.reshape(N_DEV * g_q.shape[1], g_q.shape[2]))
    wkv_ext = _ext_kv(_cols_from_slabs(g_kv))
    w_out_full = g_out.reshape(D_MODEL, D_MODEL)
    inv_freq = 1.0 / (ROPE_THETA ** (jnp.arange(0, ROPE_DIM, 2, dtype=F32) / ROPE_DIM))
    zeros = lambda n: jnp.zeros((n,), F32)
    invf = jnp.concatenate([zeros(128), inv_freq, inv_freq, zeros(64)]).reshape(1, QK_PAD)
    m_rot = jnp.concatenate([zeros(128), jnp.ones((64,), F32), zeros(64)]).reshape(1, QK_PAD)
    q, k, v, c1, s1, cqn, ckvn = _mla_pre(z, positions.reshape(T, 1), invf, m_rot, wq_ext, wkv_ext,
                                          mla_q_norm_w, mla_kv_norm_w)[0]
    o_raw, o_gated, s_prev = _hgrn_fwd(z, hg_lower_bounds, hg_norm_w)[0]
    (o_mla, lse), (w1, w2) = _attn_fwd(q, k, v, exchange=[_StagedGather(bf["w_mlp_in"], 0.9),
                                                          _StagedGather(bf["w_mlp_out"], 0.9)])
    mix, xhat1, rstd1, u2 = _mix_ln1(o_gated, o_mla, w_out_full, xs, g_a, ln1_g, ln1_b, sc_m, sh_m)[0]
    r, dr2, dh, small_mlp_fwd = _mlp_fwd(u2, w1, w2, xhat1, ln1_g, ln1_b, g_m, ln2_g, ln2_b, tgt)

    dhpre, dr1, dmix, small_mlp_bwd = _mlp_bwd(dh, w1, w2, r, dr2, xhat1, rstd1, mix, ln1_g, ln1_b, sc_m, g_a)
    received = {}
    dw2 = _matmul(r, dh, "TN", "wgrad_mlp_out", out_dtype=BF16, a_fn=_square, tm=1024, tk=2048)
    dw1 = _matmul(u2, dhpre, "TN", "wgrad_mlp_in", out_dtype=BF16, tm=1024, tk=2048, out_slabs=N_DEV)
    dmixcat = _matmul(dmix, w_out_full, "NT", "dgrad_out", tm=1024)
    dw_out = jnp.concatenate([_matmul(o_gated, dmix, "TN", "wgrad_out_hg", out_dtype=BF16, tk=2048),
                              _matmul(o_mla, dmix, "TN", "wgrad_out_mla", out_dtype=BF16, tk=2048)], axis=0)
    (dz_h, small_hgrn), (received["w_out"],) = _hgrn_bwd(
        dmixcat, z, o_raw, s_prev, hg_lower_bounds, hg_norm_w,
        exchange=_Exchange([dw_out.reshape(N_DEV, D_MODEL // N_DEV, D_MODEL)], True))
    (dq, dk, dv), (received["w_mlp_in"], received["w_mlp_out"]) = _attn_bwd(
        q, k, v, dmixcat, o_mla, lse,
        exchange=_Exchange([dw1, dw2.reshape(N_DEV, dw2.shape[0] // N_DEV, D_MODEL)], True))
    dz_m, dq_ext, dkv_ext, small_mla = _mla_bwd(dq, dk, dv, z, c1, s1, wq_ext, wkv_ext, mla_q_norm_w, mla_kv_norm_w)
    dwq_t = _grad_q_from_ext_t(_matmul(dq_ext, cqn, "TN", "wgrad_q_up", tm=1024, tk=2048))
    dwkv = _grad_kv_from_ext(_matmul(ckvn, dkv_ext, "TN", "wgrad_kv_up", tn=1536, tk=2048))
    qkv_slabs = [dwq_t.reshape((N_DEV, dwq_t.shape[0] // N_DEV, dwq_t.shape[1])).astype(BF16),
                 _slabs_from_cols(dwkv).astype(BF16)]
    dwt_h, (received["w_q_up"], received["w_kv_up"]) = _matmul(
        dz_h, xs, "TN", "wgrad_in_h", b_fn=_modulate, extras=(sc_a, sh_a), tm=1024, tk=2048,
        exchange=_Exchange(qkv_slabs, True))
    dwt_m = _matmul(dz_m, xs, "TN", "wgrad_in_m", b_fn=_modulate, extras=(sc_a, sh_a), tm=1024, tk=2048)
    dw_in_t = _grad_in_from_ext_t(dwt_h, dwt_m)
    in_slabs = dw_in_t.reshape((N_DEV, dw_in_t.shape[0] // N_DEV, dw_in_t.shape[1]))
    in_slabs = jnp.pad(in_slabs, ((0, 0), (0, -in_slabs.shape[1] % 16), (0, 0))).astype(BF16)
    (grad_x, small_in), (received["w_in"],) = _input_bwd(
        dz_h, dz_m, w_in_ext, xs, dr1, sc_a, exchange=_StagedScatter(in_slabs))

    small = jnp.concatenate([small_in, small_mlp_bwd[:, :3 * D_MODEL], small_mlp_fwd[:, :D_MODEL], small_hgrn,
                             small_mla, small_mlp_bwd[:, 3 * D_MODEL:], small_mlp_fwd[:, D_MODEL:]], axis=1)
    assert small.shape == (1, SMALL_W)
    (small_all,) = _exchange([small], scatter=False, name="gather_small")

    moments = dict(w_in=(m_w_in, v_w_in), w_q_up=(m_w_q_up, v_w_q_up), w_kv_up=(m_w_kv_up, v_w_kv_up),
                   w_out=(m_w_out, v_w_out), w_mlp_in=(m_w_mlp_in, v_w_mlp_in), w_mlp_out=(m_w_mlp_out, v_w_mlp_out))
    res = {}
    for n in names:
        res[n] = _adam(received[n], big[n], as_used(n, moments[n][0]), as_used(n, moments[n][1]), name="adam_" + n)
    dmod_cols = lax.dynamic_slice(small_all.reshape(N_DEV, SMALL_W), (0, me * ada_cols), (N_DEV, ada_cols))
    cond_t = cond.T

    def ada_grad(ct_ref, dm_ref):
        g = ct_ref[:, 0:1] * dm_ref[0:1, :]
        for b in range(1, N_DEV):
            g = g + ct_ref[:, b:b + 1] * dm_ref[b:b + 1, :]
        return g

    res["w_ada"] = _adam(None, w_ada[0], m_w_ada[0], v_w_ada[0], name="adam_w_ada", g_fn=ada_grad,
                         g_extra=(cond_t, dmod_cols))

    small_params = [("b_ada", b_ada, m_b_ada, v_b_ada, 0),
                    ("hg_lower_bounds", hg_lower_bounds, m_hg_lower_bounds, v_hg_lower_bounds, 6144),
                    ("hg_norm_w", hg_norm_w, m_hg_norm_w, v_hg_norm_w, 6656),
                    ("mla_q_norm_w", mla_q_norm_w, m_mla_q_norm_w, v_mla_q_norm_w, 7168),
                    ("mla_kv_norm_w", mla_kv_norm_w, m_mla_kv_norm_w, v_mla_kv_norm_w, 7424),
                    ("ln1_g", ln1_g, m_ln1_g, v_ln1_g, 7680), ("ln1_b", ln1_b, m_ln1_b, v_ln1_b, 8704),
                    ("ln2_g", ln2_g, m_ln2_g, v_ln2_g, 9728), ("ln2_b", ln2_b, m_ln2_b, v_ln2_b, 10752)]
    loss_row, small_res = _adam_small(small_all, [p[1:] for p in small_params])
    for p, r4 in zip(small_params, small_res):
        res[p[0]] = r4
    loss = loss_row[0, 0]

    order = ["w_ada", "b_ada", "w_in", "hg_lower_bounds", "hg_norm_w", "mla_q_norm_w", "w_q_up", "mla_kv_norm_w",
             "w_kv_up", "w_out", "ln1_g", "ln1_b", "w_mlp_in", "w_mlp_out", "ln2_g", "ln2_b"]
    def as_given(n, a):
        if n in transposed:
            a = a.T
        return a[None] if n in big or n == "w_ada" else a

    shaped = {n: tuple(as_given(n, a) for a in res[n]) for n in order}
    outs = [loss, grad_x.reshape(1, T, D_MODEL)]
    for i in range(4):
        outs += [shaped[n][i] for n in order]
    return tuple(outs)
```

```python
import functools

import jax
import jax.numpy as jnp
import numpy as np
from jax import lax
from jax.experimental import pallas as pl
from jax.experimental.pallas import tpu as pltpu

F32, BF16 = jnp.float32, jnp.bfloat16
N_DEV = 8
D_MODEL = 1024
HEADS = 4
HEAD_DIM = 128
ROPE_DIM = 64
QK_PAD = 256
CHUNK = 64
ROPE_THETA = 10000.0
RMS_EPS = 1e-6
LN_EPS = 1e-5
ALPHA = 2.0 ** 0.25
ATT_SCALE = (HEAD_DIM + ROPE_DIM) ** -0.5
LN2 = float(np.log(2.0))
Q_PRESCALE = ATT_SCALE / LN2
ADAM_LR, ADAM_B1, ADAM_B2, ADAM_EPS, ADAM_WD, ADAM_STEP = 0.001, 0.9, 0.999, 1e-08, 0.01, 10
NEG_BIG = -1e30

ROW_TILE = 512
ATT_TILE = 512
HGRN_GROUP = 8
MLP_SLABS = 4
VMEM_LIMIT = 56 * 2 ** 20

NN = (((1,), (0,)), ((), ()))
NT = (((1,), (1,)), ((), ()))
TN = (((0,), (0,)), ((), ()))


def _dot(a, b, dims=NN):
    return lax.dot_general(a, b, dims, preferred_element_type=F32)


def _bdot(a, b, dims=NN):
    return lax.dot_general(a.astype(BF16), b.astype(BF16), dims, preferred_element_type=F32)


def _hdot(a, b, dims=NN):
    return lax.dot_general(a, b, dims, precision=lax.Precision.HIGHEST, preferred_element_type=F32)


def _params():
    return pltpu.CompilerParams(vmem_limit_bytes=VMEM_LIMIT)


def _sigmoid(x):
    return 1.0 / (1.0 + jnp.exp(-x))


def _rowsum(x):
    return jnp.sum(x, axis=0, keepdims=True)


def _lanemean(x):
    return jnp.mean(x, axis=-1, keepdims=True)


def _full(shape):
    nd = len(shape)
    return pl.BlockSpec(shape, lambda *_: (0,) * nd)


class _Exchange:
    def __init__(self, arrs, scatter):
        self.arrs, self.scatter, self.n, self.aliases, self.middle_at = list(arrs), scatter, len(arrs), [], 0.5
        self.out_shape = [jax.ShapeDtypeStruct((N_DEV,) + (a.shape[1:] if scatter else a.shape), a.dtype)
                          for a in self.arrs]
        n = self.n
        self.scratch = [pltpu.SemaphoreType.DMA((n, N_DEV - 1)), pltpu.SemaphoreType.DMA((n, N_DEV - 1)),
                        pltpu.SemaphoreType.DMA((n,))]

    def _copies(self, ins, outs, sems):
        send_sems, recv_sems, loc_sems = sems
        x, y, c = lax.axis_index("x"), lax.axis_index("y"), lax.axis_index("c")
        me = 4 * x + 2 * y + c
        copies = []
        for k in range(self.n):
            src_of = (lambda i, k=k: ins[k].at[i]) if self.scatter else (lambda i, k=k: ins[k])
            copies.append((pltpu.make_async_copy(src_of(me), outs[k].at[me], loc_sems.at[k]), None))
            for p in range(1, N_DEV):
                px = (1 - x) if p & 4 else x
                py = (1 - y) if p & 2 else y
                pc = (1 - c) if p & 1 else c
                peer = 4 * px + 2 * py + pc
                both = dict(send_sem=send_sems.at[k, p - 1], recv_sem=recv_sems.at[k, p - 1],
                            device_id=(px, py, pc), device_id_type=pl.DeviceIdType.MESH)
                send = pltpu.make_async_remote_copy(src_ref=src_of(peer), dst_ref=outs[k].at[me], **both)
                recv = pltpu.make_async_remote_copy(src_ref=src_of(peer), dst_ref=outs[k].at[peer], **both)
                copies.append((send, recv))
        return copies

    def start(self, ins, outs, sems):
        for first, _ in self._copies(ins, outs, sems):
            first.start()

    def middle(self, ins, outs, sems):
        pass

    def wait(self, ins, outs, sems):
        for first, recv in self._copies(ins, outs, sems):
            if recv is None:
                first.wait()
            else:
                recv.wait_recv()
                first.wait_send()


class _StagedGather:
    def __init__(self, arr, middle_at=0.8):
        self.arrs, self.aliases, self.middle_at = [arr], [], middle_at
        self.out_shape = [jax.ShapeDtypeStruct((N_DEV,) + arr.shape, arr.dtype)]
        self.scratch = [pltpu.VMEM((N_DEV,) + arr.shape, arr.dtype), pltpu.SemaphoreType.DMA((7,)),
                        pltpu.SemaphoreType.DMA((7,)), pltpu.SemaphoreType.DMA((2,))]

    def _parts(self, scr):
        stage, send_sems, recv_sems, loc_sems = scr
        x, y, c = lax.axis_index("x"), lax.axis_index("y"), lax.axis_index("c")
        me, sibling = (x, y, c), (x, y, 1 - c)
        chips = [(1 - x, y), (x, 1 - y), (1 - x, 1 - y)]

        def copy(j, block, to):
            px, py, pc = block
            slot = stage.at[4 * px + 2 * py + pc]
            return pltpu.make_async_remote_copy(src_ref=slot, dst_ref=slot, send_sem=send_sems.at[j],
                                                recv_sem=recv_sems.at[j], device_id=to,
                                                device_id_type=pl.DeviceIdType.MESH)

        return stage, loc_sems, me, sibling, chips, c, copy

    def start(self, ins, outs, scr):
        stage, loc_sems, me, sibling, chips, c, copy = self._parts(scr)
        x, y, _ = me
        own = pltpu.make_async_copy(ins[0], stage.at[4 * x + 2 * y + c], loc_sems.at[0])
        own.start()
        own.wait()
        copy(0, me, sibling).start()
        for j, chip in enumerate(chips):
            copy(1 + j, me, (*chip, c)).start()

    def middle(self, ins, outs, scr):
        stage, loc_sems, me, sibling, chips, c, copy = self._parts(scr)
        for j, chip in enumerate(chips):
            copy(1 + j, (*chip, c), me).wait_recv()
            copy(4 + j, (*chip, c), sibling).start()

    def wait(self, ins, outs, scr):
        stage, loc_sems, me, sibling, chips, c, copy = self._parts(scr)
        copy(0, sibling, me).wait_recv()
        for j, chip in enumerate(chips):
            copy(4 + j, (*chip, 1 - c), me).wait_recv()
        copy(0, me, sibling).wait_send()
        for j, chip in enumerate(chips):
            copy(1 + j, me, (*chip, c)).wait_send()
            copy(4 + j, (*chip, c), sibling).wait_send()
        whole = pltpu.make_async_copy(stage, outs[0], loc_sems.at[1])
        whole.start()
        whole.wait()


class _StagedScatter:
    def __init__(self, slabs, middle_at=0.2):
        _, r, c = slabs.shape
        self.arrs, self.aliases, self.middle_at = [slabs], [], middle_at
        self.out_shape = [jax.ShapeDtypeStruct((4, r, c), slabs.dtype)]
        self.scratch = [pltpu.VMEM((N_DEV, r, c), slabs.dtype), pltpu.VMEM((4, r, c), slabs.dtype),
                        pltpu.VMEM((3, r, c), slabs.dtype), pltpu.SemaphoreType.DMA((4,)), pltpu.SemaphoreType.DMA((4,)),
                        pltpu.SemaphoreType.DMA((3,)), pltpu.SemaphoreType.DMA((3,)), pltpu.SemaphoreType.DMA((4,))]

    def _parts(self, scr):
        stage, from_sib, from_chips, sib_send, sib_recv, ici_send, ici_recv, loc_sems = scr
        x, y, c = lax.axis_index("x"), lax.axis_index("y"), lax.axis_index("c")
        chips = [(1 - x, y), (x, 1 - y), (1 - x, 1 - y)]

        def to_sibling(j):
            return pltpu.make_async_remote_copy(src_ref=stage.at[2 * j + 1 - c], dst_ref=from_sib.at[j],
                                                send_sem=sib_send.at[j], recv_sem=sib_recv.at[j],
                                                device_id=(x, y, 1 - c), device_id_type=pl.DeviceIdType.MESH)

        def to_chip(k):
            px, py = chips[k]
            return pltpu.make_async_remote_copy(src_ref=stage.at[4 * px + 2 * py + c], dst_ref=from_chips.at[k],
                                                send_sem=ici_send.at[k], recv_sem=ici_recv.at[k],
                                                device_id=(px, py, c), device_id_type=pl.DeviceIdType.MESH)

        return stage, from_sib, from_chips, loc_sems, (x, y, c), chips, to_sibling, to_chip

    def start(self, ins, outs, scr):
        stage, _, _, loc_sems, _, _, to_sibling, _ = self._parts(scr)
        load = pltpu.make_async_copy(ins[0], stage, loc_sems.at[0])
        load.start()
        load.wait()
        for j in range(4):
            to_sibling(j).start()

    def middle(self, ins, outs, scr):
        stage, from_sib, _, _, (x, y, c), _, to_sibling, to_chip = self._parts(scr)
        for j in range(4):
            to_sibling(j).wait_recv()
            mine = stage.at[2 * j + c]
            mine[...] = (mine[...].astype(F32) + from_sib[j].astype(F32)).astype(mine.dtype)
        for k in range(3):
            to_chip(k).start()

    def wait(self, ins, outs, scr):
        stage, _, from_chips, loc_sems, (x, y, c), chips, to_sibling, to_chip = self._parts(scr)
        writes = [pltpu.make_async_copy(stage.at[4 * x + 2 * y + c], outs[0].at[2 * x + y], loc_sems.at[0])]
        for k, (px, py) in enumerate(chips):
            to_chip(k).wait_recv()
            writes.append(pltpu.make_async_copy(from_chips.at[k], outs[0].at[2 * px + py], loc_sems.at[1 + k]))
        for w in writes:
            w.start()
        for j in range(4):
            to_sibling(j).wait_send()
        for k in range(3):
            to_chip(k).wait_send()
        for w in writes:
            w.wait()


def _call(body, name, args, out_shape, grid=(), in_specs=(), out_specs=(), scratch_shapes=(), exchange=None):
    if exchange is None:
        return pl.pallas_call(body, name=name, grid=grid, in_specs=list(in_specs), out_specs=list(out_specs),
                              out_shape=list(out_shape), scratch_shapes=list(scratch_shapes),
                              compiler_params=_params())(*args), None
    exs = list(exchange) if isinstance(exchange, (list, tuple)) else [exchange]
    ni, no, ns = len(args), len(out_shape), len(scratch_shapes)
    nxi, nxo = sum(len(e.arrs) for e in exs), sum(len(e.out_shape) for e in exs)
    steps = int(np.prod(grid))
    mid_step = lambda e: min(max(int(steps * e.middle_at), 1), steps - 1)
    aliases, iat, oat = {}, ni, no
    for e in exs:
        for src, dst in e.aliases:
            aliases[iat + src] = oat + dst
        iat, oat = iat + len(e.arrs), oat + len(e.out_shape)

    def wrapped(*refs):
        a, xi = refs[:ni], refs[ni:ni + nxi]
        o, xo = refs[ni + nxi:ni + nxi + no], refs[ni + nxi + no:ni + nxi + no + nxo]
        s, xs = refs[ni + nxi + no + nxo:ni + nxi + no + nxo + ns], refs[ni + nxi + no + nxo + ns:]
        parts, iat, oat, sat = [], 0, 0, 0
        for e in exs:
            parts.append((e, xi[iat:iat + len(e.arrs)], xo[oat:oat + len(e.out_shape)], xs[sat:sat + len(e.scratch)]))
            iat, oat, sat = iat + len(e.arrs), oat + len(e.out_shape), sat + len(e.scratch)
        step = 0
        for d, g in enumerate(grid):
            step = step * g + pl.program_id(d)

        @pl.when(step == 0)
        def _():
            for e, ins, outs, sems in parts:
                e.start(ins, outs, sems)

        for at_step in sorted({mid_step(e) for e in exs}):
            @pl.when(step == at_step)
            def _():
                for e, ins, outs, sems in parts:
                    if mid_step(e) == at_step:
                        e.middle(ins, outs, sems)

        body(*a, *o, *s)

        @pl.when(step == steps - 1)
        def _():
            for e, ins, outs, sems in parts:
                e.wait(ins, outs, sems)

    hbm = pl.BlockSpec(memory_space=pltpu.HBM)
    res = pl.pallas_call(
        wrapped, name=name, grid=grid, in_specs=list(in_specs) + [hbm] * nxi, out_specs=list(out_specs) + [hbm] * nxo,
        out_shape=list(out_shape) + [o_ for e in exs for o_ in e.out_shape],
        scratch_shapes=list(scratch_shapes) + [s_ for e in exs for s_ in e.scratch],
        input_output_aliases=aliases, compiler_params=_params())(*args, *[a_ for e in exs for a_ in e.arrs])
    return res[:no], res[no:]


def _gather_two_level(arrs, name):
    n = len(arrs)
    out_shape = [jax.ShapeDtypeStruct((N_DEV,) + a.shape, a.dtype) for a in arrs]

    def body(*refs):
        ins, outs = refs[:n], refs[n:2 * n]
        send_sems, recv_sems, loc_sems = refs[2 * n:]
        x, y, c = lax.axis_index("x"), lax.axis_index("y"), lax.axis_index("c")
        me, sibling = (x, y, c), (x, y, 1 - c)
        chips = [(1 - x, y), (x, 1 - y), (1 - x, 1 - y)]

        def copy(k, j, block, to, src=None):
            px, py, pc = block
            dst = outs[k].at[4 * px + 2 * py + pc]
            return pltpu.make_async_remote_copy(src_ref=dst if src is None else src, dst_ref=dst,
                                                send_sem=send_sems.at[k, j], recv_sem=recv_sems.at[k, j],
                                                device_id=to, device_id_type=pl.DeviceIdType.MESH)

        mine = [pltpu.make_async_copy(ins[k], outs[k].at[4 * x + 2 * y + c], loc_sems.at[k]) for k in range(n)]
        first = []
        for k in range(n):
            mine[k].start()
            first.append(copy(k, 0, me, sibling, src=ins[k]))
            first += [copy(k, 1 + j, me, (*chip, c), src=ins[k]) for j, chip in enumerate(chips)]
        for cp in first:
            cp.start()
        passed = []
        for j, chip in enumerate(chips):
            for k in range(n):
                copy(k, 1 + j, (*chip, c), me).wait_recv()
                passed.append(copy(k, 4 + j, (*chip, c), sibling))
                passed[-1].start()
        for k in range(n):
            copy(k, 0, sibling, me).wait_recv()
            for j, chip in enumerate(chips):
                copy(k, 4 + j, (*chip, 1 - c), me).wait_recv()
        for cp in first + passed:
            cp.wait_send()
        for cp in mine:
            cp.wait()

    vmem = pl.BlockSpec(memory_space=pltpu.VMEM)
    return pl.pallas_call(body, name=name, out_shape=out_shape, in_specs=[vmem] * n, out_specs=[vmem] * n,
                          scratch_shapes=[pltpu.SemaphoreType.DMA((n, 7)), pltpu.SemaphoreType.DMA((n, 7)),
                                          pltpu.SemaphoreType.DMA((n,))], compiler_params=_params())(*arrs)


def _exchange(arrs, scatter, name):
    ex = _Exchange(arrs, scatter)

    def body(*refs):
        ins, outs, sems = refs[:ex.n], refs[ex.n:2 * ex.n], refs[2 * ex.n:]
        ex.start(ins, outs, sems)
        ex.wait(ins, outs, sems)

    hbm = pl.BlockSpec(memory_space=pltpu.HBM)
    return pl.pallas_call(body, name=name, out_shape=ex.out_shape, in_specs=[hbm] * ex.n, out_specs=[hbm] * ex.n,
                          scratch_shapes=ex.scratch)(*ex.arrs)


def _matmul(a, b, mode, name, out_dtype=F32, tm=512, tn=1024, tk=1024, a_fn=None, b_fn=None, extras=(),
            out_slabs=None, exchange=None):
    assert not (a_fn and b_fn) and not (b_fn and mode == "NT")
    if mode == "NN":
        (M, K), N = a.shape, b.shape[1]
    elif mode == "NT":
        (M, K), N = a.shape, b.shape[0]
    else:
        (K, M), N = a.shape, b.shape[1]
    slab_w = N // out_slabs if out_slabs else None
    if out_slabs:
        tn = max(slab_w, min(tn, N) // slab_w * slab_w)
    tm, tn, tk = min(tm, M), min(tn, N), min(tk, K)
    assert M % tm == 0 and N % tn == 0 and K % tk == 0, (name, M, N, K)
    nk = K // tk
    dims = {"NN": NN, "NT": NT, "TN": TN}[mode]
    ne = len(extras)

    def body(a_ref, b_ref, *rest):
        e_refs, o_ref, acc_ref = rest[:ne], rest[ne], rest[ne + 1]
        k = pl.program_id(2)

        @pl.when(k == 0)
        def _():
            acc_ref[...] = jnp.zeros_like(acc_ref)

        at, bt = a_ref[...], b_ref[...]
        if a_fn is not None:
            at = a_fn(at.astype(F32), *[e[...] for e in e_refs])
        if b_fn is not None:
            bt = b_fn(bt.astype(F32), *[e[...] for e in e_refs])
        acc_ref[...] += _bdot(at, bt, dims)

        @pl.when(k == nk - 1)
        def _():
            if out_slabs:
                for s in range(tn // slab_w):
                    o_ref[s] = acc_ref[:, s * slab_w:(s + 1) * slab_w].astype(out_dtype)
            else:
                o_ref[...] = acc_ref[...].astype(out_dtype)

    if mode == "TN":
        a_spec = pl.BlockSpec((tk, tm), lambda i, j, k: (k, i))
        e_spec = pl.BlockSpec((1, tm), lambda i, j, k: (0, i))
    else:
        a_spec = pl.BlockSpec((tm, tk), lambda i, j, k: (i, k))
        e_spec = pl.BlockSpec((1, tk), lambda i, j, k: (0, k))
    if mode == "NT":
        b_spec = pl.BlockSpec((tn, tk), lambda i, j, k: (j, k))
    else:
        b_spec = pl.BlockSpec((tk, tn), lambda i, j, k: (k, j))
    if b_fn is not None:
        e_spec = pl.BlockSpec((1, tn), lambda i, j, k: (0, j))
    if out_slabs:
        o_shape = jax.ShapeDtypeStruct((out_slabs, M, slab_w), out_dtype)
        o_spec = pl.BlockSpec((tn // slab_w, tm, slab_w), lambda i, j, k: (j, i, 0))
    else:
        o_shape = jax.ShapeDtypeStruct((M, N), out_dtype)
        o_spec = pl.BlockSpec((tm, tn), lambda i, j, k: (i, j))
    (out,), got = _call(body, name, (a, b, *extras), [o_shape], grid=(M // tm, N // tn, nk),
                        in_specs=[a_spec, b_spec] + [e_spec] * ne, out_specs=[o_spec],
                        scratch_shapes=[pltpu.VMEM((tm, tn), F32)], exchange=exchange)
    return out if exchange is None else (out, got)


def _modulate(x, sc, sh):
    return x * (1.0 + sc) + sh


def _square(x):
    return x * x


def _mod_part(c_all, w_ada_s, b_s):
    def body(c_ref, w_ref, b_ref, mod_ref, cond_ref):
        cv = c_ref[...]
        cond = cv * _sigmoid(cv)
        cond_ref[...] = cond
        mod_ref[...] = _bdot(cond, w_ref[...]) + b_ref[...]

    return pl.pallas_call(
        body, name="mod_part",
        out_shape=[jax.ShapeDtypeStruct((N_DEV, w_ada_s.shape[1]), F32), jax.ShapeDtypeStruct(c_all.shape, F32)],
        compiler_params=_params(),
    )(c_all, w_ada_s, b_s)


def _rms_fwd(x, w):
    rs = lax.rsqrt(_lanemean(x * x) + RMS_EPS)
    return x * rs * w, rs


def _rms_bwd(x, rs, w, dy):
    xhat = x * rs
    dxh = dy * w
    return rs * (dxh - xhat * _lanemean(dxh * xhat)), dy * xhat


def _mla_pre(z, pos_col, invf, m_rot, wq_ext, wkv_ext, qnw, kvnw, exchange=None):
    T = z.shape[0]
    tm = min(ROW_TILE, T)

    def body(z_ref, pos_ref, invf_ref, mrot_ref, wq_ref, wkv_ref, qnw_ref, kvnw_ref,
             q_ref, k_ref, v_ref, c1_ref, s1_ref, cqn_ref, ckvn_ref):
        hi = slice(HEAD_DIM, QK_PAD)
        ang = pos_ref[...].astype(F32) * invf_ref[:, hi]
        c1 = jnp.concatenate([jnp.ones((tm, HEAD_DIM), F32), mrot_ref[:, hi] * jnp.cos(ang)], axis=1)
        s1 = jnp.concatenate([jnp.zeros((tm, HEAD_DIM), F32), mrot_ref[:, hi] * jnp.sin(ang)], axis=1)
        c1_ref[...] = c1
        s1_ref[...] = s1
        cqn, _ = _rms_fwd(z_ref[:, 0:256], qnw_ref[...])
        ckvn, _ = _rms_fwd(z_ref[:, 256:512], kvnw_ref[...])
        cqn_ref[...] = cqn.astype(BF16)
        ckvn_ref[...] = ckvn.astype(BF16)
        qe = _bdot(cqn, wq_ref[...], NT)
        kve = _bdot(ckvn, wkv_ref[...])
        k_rope = z_ref[:, 512:768] * c1 + z_ref[:, 768:1024] * s1
        for h in range(HEADS):
            q_ref[h] = ((qe[:, 256 * h:256 * h + 256] * c1 + qe[:, 1024 + 256 * h:1280 + 256 * h] * s1)
                        * Q_PRESCALE).astype(BF16)
            k_ref[h] = (kve[:, 256 * h:256 * h + 256] + k_rope).astype(BF16)
            v_ref[h] = kve[:, 1024 + 128 * h:1152 + 128 * h].astype(BF16)

    row = lambda i: (i, 0)
    head = lambda i: (0, i, 0)
    return _call(
        body, "mla_pre", (z, pos_col, invf, m_rot, wq_ext, wkv_ext, qnw, kvnw), grid=(T // tm,),
        in_specs=[pl.BlockSpec((tm, 1024), lambda i: (i, 2)), pl.BlockSpec((tm, 1), row),
                  _full((1, 256)), _full((1, 256)), _full(wq_ext.shape), _full(wkv_ext.shape),
                  _full((1, 256)), _full((1, 256))],
        out_specs=[pl.BlockSpec((HEADS, tm, QK_PAD), head), pl.BlockSpec((HEADS, tm, QK_PAD), head),
                   pl.BlockSpec((HEADS, tm, HEAD_DIM), head), pl.BlockSpec((tm, 256), row), pl.BlockSpec((tm, 256), row),
                   pl.BlockSpec((tm, 256), row), pl.BlockSpec((tm, 256), row)],
        out_shape=[jax.ShapeDtypeStruct((HEADS, T, QK_PAD), BF16), jax.ShapeDtypeStruct((HEADS, T, QK_PAD), BF16),
                   jax.ShapeDtypeStruct((HEADS, T, HEAD_DIM), BF16), jax.ShapeDtypeStruct((T, 256), F32),
                   jax.ShapeDtypeStruct((T, 256), F32), jax.ShapeDtypeStruct((T, 256), BF16),
                   jax.ShapeDtypeStruct((T, 256), BF16)], exchange=exchange)


def _mla_bwd(dq, dk, dv, z, c1, s1, wq_ext, wkv_ext, qnw, kvnw):
    T = z.shape[0]
    tm = min(ROW_TILE, T)

    def body(dq_ref, dk_ref, dv_ref, z_ref, c1_ref, s1_ref, wq_ref, wkv_ref, qnw_ref, kvnw_ref,
             dz_ref, dqe_ref, dkve_ref, dnw_ref):
        @pl.when(pl.program_id(0) == 0)
        def _():
            dnw_ref[...] = jnp.zeros_like(dnw_ref)

        c1, s1 = c1_ref[...], s1_ref[...]
        dkpe = jnp.zeros((tm, QK_PAD), F32)
        for h in range(HEADS):
            dqh, dkh = dq_ref[h].astype(F32) * ATT_SCALE, dk_ref[h]
            dqe_ref[:, 256 * h:256 * h + 256] = (dqh * c1).astype(BF16)
            dqe_ref[:, 1024 + 256 * h:1280 + 256 * h] = (dqh * s1).astype(BF16)
            dkve_ref[:, 256 * h:256 * h + 256] = dkh
            dkve_ref[:, 1024 + 128 * h:1152 + 128 * h] = dv_ref[h]
            dkpe = dkpe + dkh.astype(F32)
        dcqn = _dot(dqe_ref[...], wq_ref[...])
        dckvn = _dot(dkve_ref[...], wkv_ref[...], NT)
        cq, ckv = z_ref[:, 0:256], z_ref[:, 256:512]
        _, rsq = _rms_fwd(cq, qnw_ref[...])
        _, rskv = _rms_fwd(ckv, kvnw_ref[...])
        dcq, wq_rows = _rms_bwd(cq, rsq, qnw_ref[...], dcqn)
        dckv, wkv_rows = _rms_bwd(ckv, rskv, kvnw_ref[...], dckvn)
        dnw_ref[:, 0:256] += _rowsum(wq_rows)
        dnw_ref[:, 256:512] += _rowsum(wkv_rows)
        dz_ref[:, 0:256] = dcq.astype(BF16)
        dz_ref[:, 256:512] = dckv.astype(BF16)
        dz_ref[:, 512:768] = (dkpe * c1).astype(BF16)
        dz_ref[:, 768:1024] = (dkpe * s1).astype(BF16)

    row = lambda i: (i, 0)
    head = lambda i: (0, i, 0)
    return pl.pallas_call(
        body, name="mla_bwd", grid=(T // tm,),
        in_specs=[pl.BlockSpec((HEADS, tm, QK_PAD), head), pl.BlockSpec((HEADS, tm, QK_PAD), head),
                  pl.BlockSpec((HEADS, tm, HEAD_DIM), head), pl.BlockSpec((tm, 1024), lambda i: (i, 2)),
                  pl.BlockSpec((tm, 256), row), pl.BlockSpec((tm, 256), row), _full(wq_ext.shape), _full(wkv_ext.shape),
                  _full((1, 256)), _full((1, 256))],
        out_specs=[pl.BlockSpec((tm, 1024), row), pl.BlockSpec((tm, 2048), row), pl.BlockSpec((tm, 1536), row),
                   _full((1, 512))],
        out_shape=[jax.ShapeDtypeStruct((T, 1024), BF16), jax.ShapeDtypeStruct((T, 2048), BF16),
                   jax.ShapeDtypeStruct((T, 1536), BF16), jax.ShapeDtypeStruct((1, 512), F32)],
        compiler_params=_params(),
    )(dq, dk, dv, z, c1, s1, wq_ext, wkv_ext, qnw, kvnw)


_HEAD_LANES = [slice(HEAD_DIM * h, HEAD_DIM * (h + 1)) for h in range(HEADS)]


def _lower_bound(lbraw_ref):
    a0, a1 = lbraw_ref[0:1, :], lbraw_ref[1:2, :]
    mx = jnp.maximum(a0, a1)
    e0, e1 = jnp.exp(a0 - mx), jnp.exp(a1 - mx)
    return e0 / (e0 + e1)


def _tri(lower):
    r = lax.broadcasted_iota(jnp.int32, (CHUNK, CHUNK), 0)
    c = lax.broadcasted_iota(jnp.int32, (CHUNK, CHUNK), 1)
    return (r >= c) if lower else (r <= c)


def _hgrn_gates(q, f, lb, tri_lo):
    sg = _sigmoid(f)
    forget = lb + (1.0 - lb) * sg
    k = 1.0 - forget
    b = _hdot(tri_lo.astype(F32), jnp.log(forget))
    b_ref, b_last = b[CHUNK // 2 - 1:CHUNK // 2, :], b[CHUNK - 1:CHUNK, :]
    e1, e2, e3, e4 = jnp.exp(b - b_ref), jnp.exp(b_ref - b), jnp.exp(b_last - b), jnp.exp(b)
    return dict(sg=sg, forget=forget, k=k, e1=e1, e2=e2, e3=e3, e4=e4, qa=q * e1, ka=k * e2, kl=k * e3, qb=q * e4,
                decay=jnp.exp(b_last))


def _hgrn_fwd(z, lbraw, nw, exchange=None):
    T = z.shape[0]
    G = min(HGRN_GROUP, T // CHUNK)
    rows = G * CHUNK
    n_chunks = T // CHUNK

    def body(q_ref, f_ref, i_ref, g_ref, lbraw_ref, nw_ref, oraw_ref, og_ref, sp_ref, st_ref):
        @pl.when(pl.program_id(0) == 0)
        def _():
            st_ref[...] = jnp.zeros_like(st_ref)

        lb_all = _lower_bound(lbraw_ref)
        tri_lo = _tri(True)

        def chunk(cc, carry):
            rs = pl.ds(pl.multiple_of(cc * CHUNK, CHUNK), CHUNK)
            t = _hgrn_gates(q_ref[rs, :], f_ref[rs, :], lb_all, tri_lo)
            v, gate = i_ref[rs, :], g_ref[rs, :]
            st = [st_ref[h] for h in range(HEADS)]
            a = [jnp.where(tri_lo, _bdot(t["qa"][:, s], t["ka"][:, s], NT), 0.0) for s in _HEAD_LANES]
            kv = [_bdot(v[:, s], t["kl"][:, s], TN) for s in _HEAD_LANES]
            o = [_bdot(a[h], v[:, s]) + _bdot(t["qb"][:, s], st[h], NT) for h, s in enumerate(_HEAD_LANES)]
            for h, s in enumerate(_HEAD_LANES):
                sp_ref[cc, h] = st[h]
                st_ref[h] = st[h] * t["decay"][:, s] + kv[h]
            oraw_ref[rs, :] = jnp.concatenate(o, axis=1)
            on = jnp.concatenate([_rms_fwd(o[h], nw_ref[:, s])[0] for h, s in enumerate(_HEAD_LANES)], axis=1)
            og_ref[rs, :] = (on * (gate * _sigmoid(gate))).astype(BF16)
            return carry

        lax.fori_loop(0, G, chunk, 0, unroll=4)

    col = lambda j: pl.BlockSpec((rows, 512), lambda r, j=j: (r, j))
    return _call(
        body, "hgrn_fwd", (z, z, z, z, lbraw, nw), grid=(T // rows,),
        in_specs=[col(0), col(1), col(2), col(3), _full((2, 512)), _full((1, 512))],
        out_specs=[col(0), col(0), pl.BlockSpec((G, HEADS, HEAD_DIM, HEAD_DIM), lambda r: (r, 0, 0, 0))],
        out_shape=[jax.ShapeDtypeStruct((T, 512), F32), jax.ShapeDtypeStruct((T, 512), BF16),
                   jax.ShapeDtypeStruct((n_chunks, HEADS, HEAD_DIM, HEAD_DIM), F32)],
        scratch_shapes=[pltpu.VMEM((HEADS, HEAD_DIM, HEAD_DIM), F32)], exchange=exchange)


def _hgrn_bwd(dmixcat, z, oraw, sprev, lbraw, nw, exchange=None):
    T = z.shape[0]
    G = min(HGRN_GROUP, T // CHUNK)
    rows = G * CHUNK
    ng = T // rows

    def body(dog_ref, q_ref, f_ref, i_ref, g_ref, oraw_ref, sp_ref, lbraw_ref, nw_ref,
             dz_ref, dsmall_ref, dst_ref):
        @pl.when(pl.program_id(0) == 0)
        def _():
            dst_ref[...] = jnp.zeros_like(dst_ref)
            dsmall_ref[...] = jnp.zeros_like(dsmall_ref)

        lb_all = _lower_bound(lbraw_ref)
        tri_lo, tri_up = _tri(True), _tri(False)
        rowid = lax.broadcasted_iota(jnp.int32, (CHUNK, HEADS * HEAD_DIM), 0)

        def chunk(it, carry):
            cc = G - 1 - it
            rs = pl.ds(pl.multiple_of(cc * CHUNK, CHUNK), CHUNK)
            heads = list(enumerate(_HEAD_LANES))
            cat = lambda parts: jnp.concatenate(parts, axis=1)
            per_head_mean = lambda x: cat([jnp.broadcast_to(_lanemean(x[:, s]), (CHUNK, HEAD_DIM)) for s in _HEAD_LANES])
            t = _hgrn_gates(q_ref[rs, :], f_ref[rs, :], lb_all, tri_lo)
            v, gate, o, dog, nw_all = i_ref[rs, :], g_ref[rs, :], oraw_ref[rs, :], dog_ref[rs, :], nw_ref[...]
            rs_o = lax.rsqrt(per_head_mean(o * o) + RMS_EPS)
            xhat = o * rs_o
            sgg = _sigmoid(gate)
            d_on = dog * (gate * sgg)
            dz_ref[rs, 1536:2048] = (dog * (xhat * nw_all) * (sgg * (1.0 + gate * (1.0 - sgg)))).astype(BF16)
            dxh = d_on * nw_all
            do = rs_o * (dxh - xhat * per_head_mean(dxh * xhat))
            dsmall_ref[:, 512:1024] += _rowsum(d_on * xhat)
            st = [sp_ref[cc, h] for h in range(HEADS)]
            dst = [dst_ref[h] for h in range(HEADS)]
            a = [jnp.where(tri_lo, _bdot(t["qa"][:, s], t["ka"][:, s], NT), 0.0) for s in _HEAD_LANES]
            da = [jnp.where(tri_lo, _bdot(do[:, s], v[:, s], NT), 0.0) for s in _HEAD_LANES]
            dqb = cat([_bdot(do[:, s], st[h]) for h, s in heads])
            dkl = cat([_bdot(v[:, s], dst[h]) for h, s in heads])
            dv_ = cat([_bdot(t["kl"][:, s], dst[h], NT) + _bdot(a[h], do[:, s], TN) for h, s in heads])
            dqa = cat([_bdot(da[h], t["ka"][:, s]) for h, s in heads])
            dka = cat([_bdot(da[h], t["qa"][:, s], TN) for h, s in heads])
            ddecay = cat([_rowsum(dst[h] * st[h]) for h in range(HEADS)])
            for h, s in heads:
                dst_ref[h] = dst[h] * t["decay"][:, s] + _bdot(do[:, s], t["qb"][:, s], TN)
            pa, pk, pb, pl_ = dqa * t["qa"], dka * t["ka"], dqb * t["qb"], dkl * t["kl"]
            db = pa - pk + pb - pl_
            db = db + jnp.where(rowid == CHUNK // 2 - 1, _rowsum(pk - pa), 0.0)
            db = db + jnp.where(rowid == CHUNK - 1, _rowsum(pl_) + ddecay * t["decay"], 0.0)
            dlogf = _hdot(tri_up.astype(F32), db)
            dforget = dlogf / t["forget"] - (dka * t["e2"] + dkl * t["e3"])
            sg = t["sg"]
            dz_ref[rs, 0:512] = (dqa * t["e1"] + dqb * t["e4"]).astype(BF16)
            dz_ref[rs, 512:1024] = (dforget * (1.0 - lb_all) * sg * (1.0 - sg)).astype(BF16)
            dz_ref[rs, 1024:1536] = dv_.astype(BF16)
            dsmall_ref[:, 0:512] += _rowsum(dforget * (1.0 - sg))
            return carry

        lax.fori_loop(0, G, chunk, 0, unroll=4)

    col = lambda j: pl.BlockSpec((rows, 512), lambda r, j=j: (ng - 1 - r, j))
    return _call(
        body, "hgrn_bwd", (dmixcat, z, z, z, z, oraw, sprev, lbraw, nw), grid=(ng,),
        in_specs=[col(0), col(0), col(1), col(2), col(3), col(0),
                  pl.BlockSpec((G, HEADS, HEAD_DIM, HEAD_DIM), lambda r: (ng - 1 - r, 0, 0, 0)),
                  _full((2, 512)), _full((1, 512))],
        out_specs=[pl.BlockSpec((rows, 2048), lambda r: (ng - 1 - r, 0)), _full((1, 1024))],
        out_shape=[jax.ShapeDtypeStruct((T, 2048), BF16), jax.ShapeDtypeStruct((1, 1024), F32)],
        scratch_shapes=[pltpu.VMEM((HEADS, HEAD_DIM, HEAD_DIM), F32)], exchange=exchange)


def _diag_mask(t):
    r = lax.broadcasted_iota(jnp.int32, (t, t), 0)
    c = lax.broadcasted_iota(jnp.int32, (t, t), 1)
    return r >= c


def _attn_fwd(q, k, v, exchange=None):
    _, T, _ = q.shape
    t = min(ATT_TILE, T)

    def body(q_ref, k_ref, v_ref, o_ref, lse_ref):
        i = pl.program_id(1)
        qb = q_ref[...]

        rows = lambda j: pl.ds(pl.multiple_of(j * t, t), t)

        def logits(j, masked):
            s = _dot(qb, k_ref[rows(j), :], NT)
            return jnp.where(_diag_mask(t), s, NEG_BIG) if masked else s

        def absorb(s, j, carry):
            m, l, acc = carry
            mn = jnp.maximum(m, jnp.max(s, axis=-1, keepdims=True))
            p = jnp.exp2(s - mn)
            al = jnp.exp2(m - mn)
            return mn, al * l + jnp.sum(p, axis=-1, keepdims=True), al * acc + _dot(p.astype(BF16), v_ref[rows(j), :])

        def pair(j0, carry, last_masked):
            s0, s1 = logits(j0, False), logits(j0 + 1, last_masked)
            return absorb(s1, j0 + 1, absorb(s0, j0, carry))

        init = (jnp.full((t, 1), NEG_BIG, F32), jnp.zeros((t, 1), F32), jnp.zeros((t, HEAD_DIM), F32))
        carry = lax.fori_loop(0, i // 2, lambda jj, c: pair(2 * jj, c, False), init)
        m, l, acc = lax.cond(i % 2 == 1, lambda c: pair(i - 1, c, True),
                             lambda c: absorb(logits(i, True), i, c), carry)
        o_ref[...] = acc / l
        lse_ref[...] = jnp.broadcast_to(m + jnp.log2(l), (t, HEAD_DIM))

    return _call(
        body, "attn_fwd", (q, k, v), grid=(HEADS, T // t),
        in_specs=[pl.BlockSpec((None, t, QK_PAD), lambda h, i: (h, i, 0)),
                  pl.BlockSpec((None, T, QK_PAD), lambda h, i: (h, 0, 0)),
                  pl.BlockSpec((None, T, HEAD_DIM), lambda h, i: (h, 0, 0))],
        out_specs=[pl.BlockSpec((t, HEAD_DIM), lambda h, i: (i, h)),
                   pl.BlockSpec((None, t, HEAD_DIM), lambda h, i: (h, i, 0))],
        out_shape=[jax.ShapeDtypeStruct((T, HEADS * HEAD_DIM), F32), jax.ShapeDtypeStruct((HEADS, T, HEAD_DIM), F32)],
        exchange=exchange)


def _attn_bwd(q, k, v, do, o, lse, exchange=None):
    _, T, _ = q.shape
    t = min(ATT_TILE, T)
    nq = T // t

    def body(q_ref, k_ref, v_ref, do_ref, o_ref, lse_ref, dq_ref, dk_ref, dv_ref, delta_ref, dq_acc):
        j = pl.program_id(1)

        @pl.when(j == 0)
        def _():
            dq_acc[...] = jnp.zeros_like(dq_acc)

            def fill(i, carry):
                rs = pl.ds(pl.multiple_of(i * t, t), t)
                delta_ref[rs, :] = jnp.broadcast_to(
                    jnp.sum(do_ref[rs, :] * o_ref[rs, :], axis=-1, keepdims=True), (t, HEAD_DIM))
                return carry

            lax.fori_loop(0, nq, fill, 0)

        kb, vb = k_ref[...], v_ref[...]

        def steps(blocks, carry):
            dk, dv = carry
            rs = [pl.ds(pl.multiple_of(i * t, t), t) for i, _ in blocks]
            qb = [q_ref[r, :] for r in rs]
            dob = [do_ref[r, :].astype(BF16) for r in rs]
            s = [_dot(b, kb, NT) for b in qb]
            dp = [_dot(b, vb, NT) for b in dob]
            for n, (_, masked) in enumerate(blocks):
                p = jnp.exp2(s[n] - lse_ref[rs[n], 0:1])
                if masked:
                    p = jnp.where(_diag_mask(t), p, 0.0)
                ds = (p * (dp[n] - delta_ref[rs[n], 0:1])).astype(BF16)
                dq_acc[rs[n], :] += _dot(ds, kb)
                dk = dk + _dot(ds, qb[n], TN)
                dv = dv + _dot(p.astype(BF16), dob[n], TN)
            return dk, dv

        zero = (jnp.zeros((t, QK_PAD), F32), jnp.zeros((t, HEAD_DIM), F32))
        rest = nq - 1 - j
        carry = lax.cond(rest % 2 == 1, lambda c: steps([(j, True), (j + 1, False)], c),
                         lambda c: steps([(j, True)], c), zero)
        first = j + 1 + rest % 2
        dk, dv = lax.fori_loop(0, rest // 2, lambda n, c: steps([(first + 2 * n, False), (first + 2 * n + 1, False)], c),
                               carry)
        dk_ref[...] = (dk * LN2).astype(BF16)
        dv_ref[...] = dv.astype(BF16)

        @pl.when(j == nq - 1)
        def _():
            dq_ref[...] = dq_acc[...].astype(BF16)

    return _call(
        body, "attn_bwd", (q, k, v, do, o, lse), grid=(HEADS, nq),
        in_specs=[pl.BlockSpec((None, T, QK_PAD), lambda h, j: (h, 0, 0)),
                  pl.BlockSpec((None, t, QK_PAD), lambda h, j: (h, j, 0)),
                  pl.BlockSpec((None, t, HEAD_DIM), lambda h, j: (h, j, 0)),
                  pl.BlockSpec((None, T, HEAD_DIM), lambda h, j: (h, 0, 0)),
                  pl.BlockSpec((None, T, HEAD_DIM), lambda h, j: (h, 0, 0)),
                  pl.BlockSpec((None, T, HEAD_DIM), lambda h, j: (h, 0, 0))],
        out_specs=[pl.BlockSpec((None, T, QK_PAD), lambda h, j: (h, 0, 0)),
                   pl.BlockSpec((None, t, QK_PAD), lambda h, j: (h, j, 0)),
                   pl.BlockSpec((None, t, HEAD_DIM), lambda h, j: (h, j, 0))],
        out_shape=[jax.ShapeDtypeStruct((HEADS, T, QK_PAD), BF16), jax.ShapeDtypeStruct((HEADS, T, QK_PAD), BF16),
                   jax.ShapeDtypeStruct((HEADS, T, HEAD_DIM), BF16)],
        scratch_shapes=[pltpu.VMEM((T, HEAD_DIM), F32), pltpu.VMEM((T, QK_PAD), F32)], exchange=exchange)


def _ln_fwd(r):
    mu = _lanemean(r)
    xc = r - mu
    rstd = lax.rsqrt(_lanemean(xc * xc) + LN_EPS)
    return xc * rstd, rstd


def _ln_bwd(dxh, xhat, rstd):
    return rstd * (dxh - _lanemean(dxh) - xhat * _lanemean(dxh * xhat))


def _mix_ln1(o_hg, o_mla, w_out, x, g_a, ln1_g, ln1_b, sc_m, sh_m, exchange=None):
    T = x.shape[0]
    tm = min(ROW_TILE, T)
    half = o_hg.shape[1]

    def body(hg_ref, mla_ref, w_ref, x_ref, ga_ref, g_ref, b_ref, sc_ref, sh_ref, mix_ref, xhat_ref, rstd_ref, u2_ref):
        mix = _dot(hg_ref[...], w_ref[0:half, :]) + _bdot(mla_ref[...], w_ref[half:, :])
        mix_ref[...] = mix
        xhat, rstd = _ln_fwd(ALPHA * x_ref[...] + (1.0 + ga_ref[...]) * mix)
        xhat_ref[...] = xhat
        rstd_ref[...] = jnp.broadcast_to(rstd, (tm, 128))
        u2_ref[...] = _modulate(xhat * g_ref[...] + b_ref[...], sc_ref[...], sh_ref[...]).astype(BF16)

    row = pl.BlockSpec((tm, D_MODEL), lambda i: (i, 0))
    vec = _full((1, D_MODEL))
    halfrow = pl.BlockSpec((tm, half), lambda i: (i, 0))
    return _call(
        body, "mix_ln1", (o_hg, o_mla, w_out, x, g_a, ln1_g, ln1_b, sc_m, sh_m), grid=(T // tm,),
        in_specs=[halfrow, halfrow, _full(w_out.shape), row, vec, vec, vec, vec, vec],
        out_specs=[row, row, pl.BlockSpec((tm, 128), lambda i: (i, 0)), row],
        out_shape=[jax.ShapeDtypeStruct((T, D_MODEL), F32), jax.ShapeDtypeStruct((T, D_MODEL), F32),
                   jax.ShapeDtypeStruct((T, 128), F32), jax.ShapeDtypeStruct((T, D_MODEL), BF16)],
        exchange=exchange)


def _mlp_fwd(u2, w1, w2, xhat1, ln1_g, ln1_b, g_m, ln2_g, ln2_b, target):
    T = u2.shape[0]
    tf = w1.shape[-1]
    nf = N_DEV // MLP_SLABS
    tm = min(ROW_TILE, T)

    def body(u2_ref, w1_ref, w2_ref, xhat_ref, g1_ref, b1_ref, gm_ref, g2_ref, b2_ref, tgt_ref,
             r_ref, dr2_ref, dh_ref, small_ref, acc_ref):
        i, f = pl.program_id(0), pl.program_id(1)
        dm = D_MODEL

        @pl.when((i == 0) & (f == 0))
        def _():
            small_ref[...] = jnp.zeros_like(small_ref)

        @pl.when(f == 0)
        def _():
            acc_ref[...] = jnp.zeros_like(acc_ref)

        u2t = u2_ref[...]
        part = None
        for s in range(MLP_SLABS):
            r = jnp.maximum(_dot(u2t, w1_ref[s]), 0.0)
            r_ref[:, s * tf:(s + 1) * tf] = r.astype(BF16)
            d = _bdot(r * r, w2_ref[s])
            part = d if part is None else part + d
        acc_ref[...] += part

        @pl.when(f == nf - 1)
        def _():
            h = acc_ref[...]
            x1 = xhat_ref[...] * g1_ref[...] + b1_ref[...]
            xhat2, rstd2 = _ln_fwd(ALPHA * x1 + (1.0 + gm_ref[...]) * h)
            err = xhat2 * g2_ref[...] + b2_ref[...] - tgt_ref[...]
            small_ref[:, 3 * dm:] += jnp.sum(0.5 * _lanemean(err * err), axis=0, keepdims=True)
            dy = err * (1.0 / D_MODEL)
            small_ref[:, dm:2 * dm] += _rowsum(dy * xhat2)
            small_ref[:, 2 * dm:3 * dm] += _rowsum(dy)
            dr2 = _ln_bwd(dy * g2_ref[...], xhat2, rstd2)
            dr2_ref[...] = dr2
            small_ref[:, 0:dm] += _rowsum(dr2 * h)
            dh_ref[...] = ((1.0 + gm_ref[...]) * dr2).astype(BF16)

    row = pl.BlockSpec((tm, D_MODEL), lambda i, f: (i, 0))
    vec = _full((1, D_MODEL))
    return pl.pallas_call(
        body, name="mlp_fwd", grid=(T // tm, nf),
        in_specs=[row, pl.BlockSpec((MLP_SLABS, D_MODEL, tf), lambda i, f: (f, 0, 0)),
                  pl.BlockSpec((MLP_SLABS, tf, D_MODEL), lambda i, f: (f, 0, 0)),
                  row, vec, vec, vec, vec, vec, row],
        out_specs=[pl.BlockSpec((tm, MLP_SLABS * tf), lambda i, f: (i, f)), row, row, _full((1, 3 * D_MODEL + 128))],
        out_shape=[jax.ShapeDtypeStruct((T, N_DEV * tf), BF16), jax.ShapeDtypeStruct((T, D_MODEL), F32),
                   jax.ShapeDtypeStruct((T, D_MODEL), BF16), jax.ShapeDtypeStruct((1, 3 * D_MODEL + 128), F32)],
        scratch_shapes=[pltpu.VMEM((tm, D_MODEL), F32)],
        compiler_params=_params(),
    )(u2, w1, w2, xhat1, ln1_g, ln1_b, g_m, ln2_g, ln2_b, target)


def _mlp_bwd(dh, w1, w2, r, dr2, xhat1, rstd1, mix, ln1_g, ln1_b, sc_m, g_a):
    T = dh.shape[0]
    tf = w1.shape[-1]
    nf = N_DEV // MLP_SLABS
    tm = min(ROW_TILE, T)

    def body(dh_ref, w1_ref, w2_ref, r_ref, dr2_ref, xhat_ref, rstd_ref, mix_ref, g1_ref, b1_ref, sc_ref, ga_ref,
             dhpre_ref, dr1_ref, dmix_ref, small_ref, acc_ref):
        i, f = pl.program_id(0), pl.program_id(1)
        dm = D_MODEL

        @pl.when((i == 0) & (f == 0))
        def _():
            small_ref[...] = jnp.zeros_like(small_ref)

        @pl.when(f == 0)
        def _():
            acc_ref[...] = jnp.zeros_like(acc_ref)

        dht = dh_ref[...]
        part = None
        for s in range(MLP_SLABS):
            cols = slice(s * tf, (s + 1) * tf)
            dhpre = (_dot(dht, w2_ref[s], NT) * (2.0 * r_ref[:, cols].astype(F32))).astype(BF16)
            dhpre_ref[:, cols] = dhpre
            d = _dot(dhpre, w1_ref[s], NT)
            part = d if part is None else part + d
        acc_ref[...] += part

        @pl.when(f == nf - 1)
        def _():
            du2 = acc_ref[...]
            xhat = xhat_ref[...]
            x1 = xhat * g1_ref[...] + b1_ref[...]
            dx1 = ALPHA * dr2_ref[...] + du2 * (1.0 + sc_ref[...])
            small_ref[:, 2 * dm:3 * dm] += _rowsum(du2 * x1)
            small_ref[:, dm:2 * dm] += _rowsum(du2)
            small_ref[:, 3 * dm:4 * dm] += _rowsum(dx1 * xhat)
            small_ref[:, 4 * dm:5 * dm] += _rowsum(dx1)
            dr1 = _ln_bwd(dx1 * g1_ref[...], xhat, rstd_ref[:, 0:1])
            dr1_ref[...] = dr1
            small_ref[:, 0:dm] += _rowsum(dr1 * mix_ref[...])
            dmix_ref[...] = ((1.0 + ga_ref[...]) * dr1).astype(BF16)

    row = pl.BlockSpec((tm, D_MODEL), lambda i, f: (i, 0))
    vec = _full((1, D_MODEL))
    return pl.pallas_call(
        body, name="mlp_bwd", grid=(T // tm, nf),
        in_specs=[row, pl.BlockSpec((MLP_SLABS, D_MODEL, tf), lambda i, f: (f, 0, 0)),
                  pl.BlockSpec((MLP_SLABS, tf, D_MODEL), lambda i, f: (f, 0, 0)),
                  pl.BlockSpec((tm, MLP_SLABS * tf), lambda i, f: (i, f)), row, row,
                  pl.BlockSpec((tm, 128), lambda i, f: (i, 0)), row, vec, vec, vec, vec],
        out_specs=[pl.BlockSpec((tm, MLP_SLABS * tf), lambda i, f: (i, f)), row, row, _full((1, 5 * D_MODEL))],
        out_shape=[jax.ShapeDtypeStruct((T, N_DEV * tf), BF16), jax.ShapeDtypeStruct((T, D_MODEL), F32),
                   jax.ShapeDtypeStruct((T, D_MODEL), BF16), jax.ShapeDtypeStruct((1, 5 * D_MODEL), F32)],
        scratch_shapes=[pltpu.VMEM((tm, D_MODEL), F32)],
        compiler_params=_params(),
    )(dh, w1, w2, r, dr2, xhat1, rstd1, mix, ln1_g, ln1_b, sc_m, g_a)


def _input_bwd(dz_h, dz_m, w_in_ext, x, dr1, sc_a, exchange=None):
    T = x.shape[0]
    tm = min(ROW_TILE, T)

    def body(dzh_ref, dzm_ref, w_ref, x_ref, dr1_ref, sc_ref, gx_ref, small_ref):
        @pl.when(pl.program_id(0) == 0)
        def _():
            small_ref[...] = jnp.zeros_like(small_ref)

        du = _bdot(dzh_ref[...], w_ref[0:2048, :]) + _bdot(dzm_ref[...], w_ref[2048:3072, :])
        gx_ref[...] = ALPHA * dr1_ref[...] + du * (1.0 + sc_ref[...])
        small_ref[:, D_MODEL:] += _rowsum(du * x_ref[...])
        small_ref[:, 0:D_MODEL] += _rowsum(du)

    row = pl.BlockSpec((tm, D_MODEL), lambda i: (i, 0))
    vec = _full((1, D_MODEL))
    return _call(
        body, "input_bwd", (dz_h, dz_m, w_in_ext, x, dr1, sc_a), grid=(T // tm,),
        in_specs=[pl.BlockSpec((tm, 2048), lambda i: (i, 0)), row, _full(w_in_ext.shape), row, row, vec],
        out_specs=[row, _full((1, 2 * D_MODEL))],
        out_shape=[jax.ShapeDtypeStruct((T, D_MODEL), F32), jax.ShapeDtypeStruct((1, 2 * D_MODEL), F32)],
        exchange=exchange)


def _adam_math(w, g, m, v):
    m = ADAM_B1 * m + (1.0 - ADAM_B1) * g
    v = ADAM_B2 * v + (1.0 - ADAM_B2) * (g * g)
    m_hat = m / (1.0 - ADAM_B1 ** ADAM_STEP)
    v_hat = v / (1.0 - ADAM_B2 ** ADAM_STEP)
    return -ADAM_LR * (m_hat / (jnp.sqrt(v_hat) + ADAM_EPS) + ADAM_WD * w), m, v


def _adam(g_slabs, w, m, v, name, g_fn=None, g_extra=()):
    R, C = w.shape
    tr = 256 if R % 256 == 0 else R
    ns = 0 if g_slabs is None else g_slabs.shape[0]
    slab_rows = tr if g_slabs is None or g_slabs.shape[1] == R else g_slabs.shape[1]
    assert slab_rows == tr or tr == R
    ne = len(g_extra)

    def body(*refs):
        e_refs = refs[:ne]
        refs = refs[ne:]
        if ns:
            gs_ref, refs = refs[0], refs[1:]
        w_ref, m_ref, v_ref, g_ref, d_ref, nm_ref, nv_ref = refs
        if g_fn is not None:
            g = g_fn(*e_refs)
        else:
            g = gs_ref[0].astype(F32)
            for s in range(1, ns):
                g = g + gs_ref[s].astype(F32)
            g = g[:tr]
        d, nm, nv = _adam_math(w_ref[...], g, m_ref[...], v_ref[...])
        g_ref[...] = g
        d_ref[...] = d
        nm_ref[...] = nm
        nv_ref[...] = nv

    blk = pl.BlockSpec((tr, C), lambda i: (i, 0))
    in_specs = [pl.BlockSpec((tr, e.shape[1]), lambda i: (i, 0)) if e.shape[0] == R else _full(e.shape) for e in g_extra]
    args = list(g_extra)
    if ns:
        in_specs.append(pl.BlockSpec((ns, slab_rows, C), lambda i: (0, i, 0)))
        args.append(g_slabs)
    return pl.pallas_call(
        body, name=name, grid=(R // tr,), in_specs=in_specs + [blk] * 3, out_specs=[blk] * 4,
        out_shape=[jax.ShapeDtypeStruct((R, C), F32)] * 4, compiler_params=_params(),
    )(*args, w, m, v)


def _adam_small(small_all, params):
    n = len(params)

    def body(*refs):
        s_ref, refs = refs[0], refs[1:]
        wmv, loss_ref, outs = refs[:3 * n], refs[3 * n], refs[3 * n + 1:]
        tot = s_ref[0]
        for i in range(1, N_DEV):
            tot = tot + s_ref[i]
        loss_ref[...] = tot[:, SMALL_W - 128:]
        for j, (w, _, _, off) in enumerate(params):
            w_ref, m_ref, v_ref = wmv[3 * j:3 * j + 3]
            g_ref, d_ref, nm_ref, nv_ref = outs[4 * j:4 * j + 4]
            if w.shape[0] == 2:
                lb = _lower_bound(w_ref)
                g0 = tot[:, off:off + w.shape[1]] * lb * (1.0 - lb)
                rows = [(slice(0, 1), g0), (slice(1, 2), -g0)]
            else:
                rows = [(slice(0, 1), tot[:, off:off + w.shape[1]])]
            for rs, g in rows:
                d, nm, nv = _adam_math(w_ref[rs, :], g, m_ref[rs, :], v_ref[rs, :])
                g_ref[rs, :], d_ref[rs, :], nm_ref[rs, :], nv_ref[rs, :] = g, d, nm, nv

    out_shape = [jax.ShapeDtypeStruct((1, 128), F32)]
    for w, _, _, _ in params:
        out_shape += [jax.ShapeDtypeStruct(w.shape, F32)] * 4
    res = pl.pallas_call(body, name="adam_small", out_shape=out_shape, compiler_params=_params())(
        small_all, *[a for w, m, v, _ in params for a in (w, m, v)])
    return res[0], [tuple(res[1 + 4 * j:5 + 4 * j]) for j in range(n)]


def _cols_from_slabs(g):
    s, r, c = g.shape
    return jnp.transpose(g, (1, 0, 2)).reshape(r, s * c)


def _slabs_from_cols(w):
    r, c = w.shape
    return jnp.transpose(w.reshape(r, N_DEV, c // N_DEV), (1, 0, 2))


def _rot_half_rows(wt):
    return jnp.concatenate([-wt[32:], wt[:32]], axis=0)


def _unrot_half_rows(dwt_rot):
    return jnp.concatenate([dwt_rot[32:], -dwt_rot[:32]], axis=0)


def _ext_in_t(g):
    k_in = g.shape[2]
    z64, z128 = jnp.zeros((64, k_in), BF16), jnp.zeros((128, k_in), BF16)
    wt = g.reshape(N_DEV * g.shape[1], k_in)
    main, wk = wt[:wt.shape[0] - ROPE_DIM], wt[wt.shape[0] - ROPE_DIM:]
    return jnp.concatenate([main, z128, wk, z64, z128, _rot_half_rows(wk), z64], axis=0)


def _ext_q_t(wt):
    r = wt.shape[1]
    z64, z128 = jnp.zeros((64, r), BF16), jnp.zeros((128, r), BF16)
    per = HEAD_DIM + ROPE_DIM
    main = [jnp.concatenate([wt[per * h:per * (h + 1)], z64], axis=0) for h in range(HEADS)]
    rot = [jnp.concatenate([z128, _rot_half_rows(wt[per * h + HEAD_DIM:per * (h + 1)]), z64], axis=0)
           for h in range(HEADS)]
    return jnp.concatenate(main + rot, axis=0)


def _ext_kv(w_kv_up):
    r = w_kv_up.shape[0]
    z128 = jnp.zeros((r, 128), BF16)
    wkv = w_kv_up.reshape(r, HEADS, 2 * HEAD_DIM)
    kpad = [jnp.concatenate([wkv[:, h, :HEAD_DIM], z128], axis=1) for h in range(HEADS)]
    vals = [wkv[:, h, HEAD_DIM:] for h in range(HEADS)]
    return jnp.concatenate(kpad + vals, axis=1)


def _grad_in_from_ext_t(dwt_h, dwt_m):
    dwk = dwt_m[512 + 128:512 + 192] + _unrot_half_rows(dwt_m[768 + 128:768 + 192])
    return jnp.concatenate([dwt_h, dwt_m[:512], dwk], axis=0)


def _grad_q_from_ext_t(dwq_ext_t):
    rows = []
    for h in range(HEADS):
        main, rot = dwq_ext_t[256 * h:256 * h + 256], dwq_ext_t[1024 + 256 * h:1280 + 256 * h]
        rows += [main[:128], main[128:192] + _unrot_half_rows(rot[128:192])]
    return jnp.concatenate(rows, axis=0)


def _grad_kv_from_ext(dwkv_ext):
    kvcols = []
    for h in range(HEADS):
        kvcols += [dwkv_ext[:, 256 * h:256 * h + 128], dwkv_ext[:, 1024 + 128 * h:1152 + 128 * h]]
    return jnp.concatenate(kvcols, axis=1)


SMALL_W = 6144 + 512 + 512 + 256 + 256 + 4 * 1024 + 128


def kernel(x, c, positions, w_ada, b_ada, w_in, hg_lower_bounds, hg_norm_w, mla_q_norm_w, w_q_up, mla_kv_norm_w, w_kv_up, w_out, ln1_g, ln1_b, w_mlp_in, w_mlp_out, ln2_g, ln2_b, loss_target, m_w_ada, m_b_ada, m_w_in, m_hg_lower_bounds, m_hg_norm_w, m_mla_q_norm_w, m_w_q_up, m_mla_kv_norm_w, m_w_kv_up, m_w_out, m_ln1_g, m_ln1_b, m_w_mlp_in, m_w_mlp_out, m_ln2_g, m_ln2_b, v_w_ada, v_b_ada, v_w_in, v_hg_lower_bounds, v_hg_norm_w, v_mla_q_norm_w, v_w_q_up, v_mla_kv_norm_w, v_w_kv_up, v_w_out, v_ln1_g, v_ln1_b, v_w_mlp_in, v_w_mlp_out, v_ln2_g, v_ln2_b):
    T = x.shape[1]
    me = 4 * lax.axis_index("x") + 2 * lax.axis_index("y") + lax.axis_index("c")
    xs, tgt = x[0], loss_target[0]
    transposed = ("w_in", "w_q_up")
    as_used = lambda n, a: a[0].T if n in transposed else a[0]
    big = {n: as_used(n, a) for n, a in dict(w_in=w_in, w_q_up=w_q_up, w_kv_up=w_kv_up, w_out=w_out,
                                              w_mlp_in=w_mlp_in, w_mlp_out=w_mlp_out).items()}
    names = list(big)

    bf = {n: big[n].astype(BF16) for n in names}
    g_in, g_c = _gather_two_level([bf["w_in"], c], name="gather_w_in")
    c_all = g_c.reshape(N_DEV, D_MODEL)

    ada_cols = w_ada.shape[2]
    mod_part, cond = _mod_part(c_all, w_ada[0], lax.dynamic_slice(b_ada, (0, me * ada_cols), (1, ada_cols)))
    (mod_all,) = _exchange([mod_part], scatter=False, name="gather_mod")
    mod_row = lax.dynamic_slice(mod_all, (0, me, 0), (N_DEV, 1, ada_cols)).reshape(1, N_DEV * ada_cols)
    sh_a, sc_a, g_a, sh_m, sc_m, g_m = [mod_row[:, D_MODEL * i:D_MODEL * (i + 1)] for i in range(6)]

    w_in_ext = _ext_in_t(g_in)
    z, (g_q, g_kv, g_out) = _matmul(xs, w_in_ext, "NT", "in_proj", a_fn=_modulate, extras=(sc_a, sh_a), tn=3072,
                                    exchange=_Exchange([bf["w_q_up"], bf["w_kv_up"], bf["w_out"]], False))
    wq_ext = _ext_q_t(g_q.reshape(N_DEV * g_q.shape[1], g_q.shape[2]))
    wkv_ext = _ext_kv(_cols_from_slabs(g_kv))
    w_out_full = g_out.reshape(D_MODEL, D_MODEL)
    inv_freq = 1.0 / (ROPE_THETA ** (jnp.arange(0, ROPE_DIM, 2, dtype=F32) / ROPE_DIM))
    zeros = lambda n: jnp.zeros((n,), F32)
    invf = jnp.concatenate([zeros(128), inv_freq, inv_freq, zeros(64)]).reshape(1, QK_PAD)
    m_rot = jnp.concatenate([zeros(128), jnp.ones((64,), F32), zeros(64)]).reshape(1, QK_PAD)
    q, k, v, c1, s1, cqn, ckvn = _mla_pre(z, positions.reshape(T, 1), invf, m_rot, wq_ext, wkv_ext,
                                          mla_q_norm_w, mla_kv_norm_w)[0]
    o_raw, o_gated, s_prev = _hgrn_fwd(z, hg_lower_bounds, hg_norm_w)[0]
    (o_mla, lse), (w1, w2) = _attn_fwd(q, k, v, exchange=[_StagedGather(bf["w_mlp_in"], 0.9),
                                                          _StagedGather(bf["w_mlp_out"], 0.9)])
    mix, xhat1, rstd1, u2 = _mix_ln1(o_gated, o_mla, w_out_full, xs, g_a, ln1_g, ln1_b, sc_m, sh_m)[0]
    r, dr2, dh, small_mlp_fwd = _mlp_fwd(u2, w1, w2, xhat1, ln1_g, ln1_b, g_m, ln2_g, ln2_b, tgt)

    dhpre, dr1, dmix, small_mlp_bwd = _mlp_bwd(dh, w1, w2, r, dr2, xhat1, rstd1, mix, ln1_g, ln1_b, sc_m, g_a)
    received = {}
    dw2 = _matmul(r, dh, "TN", "wgrad_mlp_out", out_dtype=BF16, a_fn=_square, tm=1024, tk=2048)
    dw1 = _matmul(u2, dhpre, "TN", "wgrad_mlp_in", out_dtype=BF16, tm=1024, tk=2048, out_slabs=N_DEV)
    dmixcat = _matmul(dmix, w_out_full, "NT", "dgrad_out", tm=1024)
    dw_out = jnp.concatenate([_matmul(o_gated, dmix, "TN", "wgrad_out_hg", out_dtype=BF16, tk=2048),
                              _matmul(o_mla, dmix, "TN", "wgrad_out_mla", out_dtype=BF16, tk=2048)], axis=0)
    (dz_h, small_hgrn), (received["w_out"],) = _hgrn_bwd(
        dmixcat, z, o_raw, s_prev, hg_lower_bounds, hg_norm_w,
        exchange=_Exchange([dw_out.reshape(N_DEV, D_MODEL // N_DEV, D_MODEL)], True))
    by_head = lambda a: jnp.transpose(a.reshape(T, HEADS, HEAD_DIM), (1, 0, 2))
    (dq, dk, dv), (received["w_mlp_in"], received["w_mlp_out"]) = _attn_bwd(
        q, k, v, by_head(dmixcat[:, HEADS * HEAD_DIM:]), by_head(o_mla), lse,
        exchange=_Exchange([dw1, dw2.reshape(N_DEV, dw2.shape[0] // N_DEV, D_MODEL)], True))
    dz_m, dq_ext, dkv_ext, small_mla = _mla_bwd(dq, dk, dv, z, c1, s1, wq_ext, wkv_ext, mla_q_norm_w, mla_kv_norm_w)
    dwq_t = _grad_q_from_ext_t(_matmul(dq_ext, cqn, "TN", "wgrad_q_up", tm=1024, tk=2048))
    dwkv = _grad_kv_from_ext(_matmul(ckvn, dkv_ext, "TN", "wgrad_kv_up", tn=1536, tk=2048))
    qkv_slabs = [dwq_t.reshape((N_DEV, dwq_t.shape[0] // N_DEV, dwq_t.shape[1])).astype(BF16),
                 _slabs_from_cols(dwkv).astype(BF16)]
    dwt_h, (received["w_q_up"], received["w_kv_up"]) = _matmul(
        dz_h, xs, "TN", "wgrad_in_h", b_fn=_modulate, extras=(sc_a, sh_a), tm=1024, tk=2048,
        exchange=_Exchange(qkv_slabs, True))
    dwt_m = _matmul(dz_m, xs, "TN", "wgrad_in_m", b_fn=_modulate, extras=(sc_a, sh_a), tm=1024, tk=2048)
    dw_in_t = _grad_in_from_ext_t(dwt_h, dwt_m)
    in_slabs = dw_in_t.reshape((N_DEV, dw_in_t.shape[0] // N_DEV, dw_in_t.shape[1]))
    in_slabs = jnp.pad(in_slabs, ((0, 0), (0, -in_slabs.shape[1] % 16), (0, 0))).astype(BF16)
    (grad_x, small_in), (received["w_in"],) = _input_bwd(
        dz_h, dz_m, w_in_ext, xs, dr1, sc_a, exchange=_StagedScatter(in_slabs))

    small = jnp.concatenate([small_in, small_mlp_bwd[:, :3 * D_MODEL], small_mlp_fwd[:, :D_MODEL], small_hgrn,
                             small_mla, small_mlp_bwd[:, 3 * D_MODEL:], small_mlp_fwd[:, D_MODEL:]], axis=1)
    assert small.shape == (1, SMALL_W)
    (small_all,) = _exchange([small], scatter=False, name="gather_small")

    moments = dict(w_in=(m_w_in, v_w_in), w_q_up=(m_w_q_up, v_w_q_up), w_kv_up=(m_w_kv_up, v_w_kv_up),
                   w_out=(m_w_out, v_w_out), w_mlp_in=(m_w_mlp_in, v_w_mlp_in), w_mlp_out=(m_w_mlp_out, v_w_mlp_out))
    res = {}
    for n in names:
        res[n] = _adam(received[n], big[n], as_used(n, moments[n][0]), as_used(n, moments[n][1]), name="adam_" + n)
    dmod_cols = lax.dynamic_slice(small_all.reshape(N_DEV, SMALL_W), (0, me * ada_cols), (N_DEV, ada_cols))
    cond_t = cond.T

    def ada_grad(ct_ref, dm_ref):
        g = ct_ref[:, 0:1] * dm_ref[0:1, :]
        for b in range(1, N_DEV):
            g = g + ct_ref[:, b:b + 1] * dm_ref[b:b + 1, :]
        return g

    res["w_ada"] = _adam(None, w_ada[0], m_w_ada[0], v_w_ada[0], name="adam_w_ada", g_fn=ada_grad,
                         g_extra=(cond_t, dmod_cols))

    small_params = [("b_ada", b_ada, m_b_ada, v_b_ada, 0),
                    ("hg_lower_bounds", hg_lower_bounds, m_hg_lower_bounds, v_hg_lower_bounds, 6144),
                    ("hg_norm_w", hg_norm_w, m_hg_norm_w, v_hg_norm_w, 6656),
                    ("mla_q_norm_w", mla_q_norm_w, m_mla_q_norm_w, v_mla_q_norm_w, 7168),
                    ("mla_kv_norm_w", mla_kv_norm_w, m_mla_kv_norm_w, v_mla_kv_norm_w, 7424),
                    ("ln1_g", ln1_g, m_ln1_g, v_ln1_g, 7680), ("ln1_b", ln1_b, m_ln1_b, v_ln1_b, 8704),
                    ("ln2_g", ln2_g, m_ln2_g, v_ln2_g, 9728), ("ln2_b", ln2_b, m_ln2_b, v_ln2_b, 10752)]
    loss_row, small_res = _adam_small(small_all, [p[1:] for p in small_params])
    for p, r4 in zip(small_params, small_res):
        res[p[0]] = r4
    loss = loss_row[0, 0]

    order = ["w_ada", "b_ada", "w_in", "hg_lower_bounds", "hg_norm_w", "mla_q_norm_w", "w_q_up", "mla_kv_norm_w",
             "w_kv_up", "w_out", "ln1_g", "ln1_b", "w_mlp_in", "w_mlp_out", "ln2_g", "ln2_b"]
    def as_given(n, a):
        if n in transposed:
            a = a.T
        return a[None] if n in big or n == "w_ada" else a

    shaped = {n: tuple(as_given(n, a) for a in res[n]) for n in order}
    outs = [loss, grad_x.reshape(1, T, D_MODEL)]
    for i in range(4):
        outs += [shaped[n][i] for n in order]
    return tuple(outs)
```

```python
import functools

import jax
import jax.numpy as jnp
import numpy as np
from jax import lax
from jax.experimental import pallas as pl
from jax.experimental.pallas import tpu as pltpu

F32, BF16 = jnp.float32, jnp.bfloat16
N_DEV = 8
D_MODEL = 1024
HEADS = 4
HEAD_DIM = 128
ROPE_DIM = 64
QK_PAD = 256
CHUNK = 64
ROPE_THETA = 10000.0
RMS_EPS = 1e-6
LN_EPS = 1e-5
ALPHA = 2.0 ** 0.25
ATT_SCALE = (HEAD_DIM + ROPE_DIM) ** -0.5
LN2 = float(np.log(2.0))
Q_PRESCALE = ATT_SCALE / LN2
ADAM_LR, ADAM_B1, ADAM_B2, ADAM_EPS, ADAM_WD, ADAM_STEP = 0.001, 0.9, 0.999, 1e-08, 0.01, 10
NEG_BIG = -1e30

ROW_TILE = 512
ATT_TILE = 512
HGRN_GROUP = 8
MLP_SLABS = 4
VMEM_LIMIT = 56 * 2 ** 20

NN = (((1,), (0,)), ((), ()))
NT = (((1,), (1,)), ((), ()))
TN = (((0,), (0,)), ((), ()))


def _dot(a, b, dims=NN):
    return lax.dot_general(a, b, dims, preferred_element_type=F32)


def _bdot(a, b, dims=NN):
    return lax.dot_general(a.astype(BF16), b.astype(BF16), dims, preferred_element_type=F32)


def _hdot(a, b, dims=NN):
    return lax.dot_general(a, b, dims, precision=lax.Precision.HIGHEST, preferred_element_type=F32)


def _params():
    return pltpu.CompilerParams(vmem_limit_bytes=VMEM_LIMIT)


def _sigmoid(x):
    return 1.0 / (1.0 + jnp.exp(-x))


def _rowsum(x):
    return jnp.sum(x, axis=0, keepdims=True)


def _lanemean(x):
    return jnp.mean(x, axis=-1, keepdims=True)


def _full(shape):
    nd = len(shape)
    return pl.BlockSpec(shape, lambda *_: (0,) * nd)


class _Exchange:
    def __init__(self, arrs, scatter):
        self.arrs, self.scatter, self.n, self.aliases, self.middle_at = list(arrs), scatter, len(arrs), [], 0.5
        self.out_shape = [jax.ShapeDtypeStruct((N_DEV,) + (a.shape[1:] if scatter else a.shape), a.dtype)
                          for a in self.arrs]
        n = self.n
        self.scratch = [pltpu.SemaphoreType.DMA((n, N_DEV - 1)), pltpu.SemaphoreType.DMA((n, N_DEV - 1)),
                        pltpu.SemaphoreType.DMA((n,))]

    def _copies(self, ins, outs, sems):
        send_sems, recv_sems, loc_sems = sems
        x, y, c = lax.axis_index("x"), lax.axis_index("y"), lax.axis_index("c")
        me = 4 * x + 2 * y + c
        copies = []
        for k in range(self.n):
            src_of = (lambda i, k=k: ins[k].at[i]) if self.scatter else (lambda i, k=k: ins[k])
            copies.append((pltpu.make_async_copy(src_of(me), outs[k].at[me], loc_sems.at[k]), None))
            for p in range(1, N_DEV):
                px = (1 - x) if p & 4 else x
                py = (1 - y) if p & 2 else y
                pc = (1 - c) if p & 1 else c
                peer = 4 * px + 2 * py + pc
                both = dict(send_sem=send_sems.at[k, p - 1], recv_sem=recv_sems.at[k, p - 1],
                            device_id=(px, py, pc), device_id_type=pl.DeviceIdType.MESH)
                send = pltpu.make_async_remote_copy(src_ref=src_of(peer), dst_ref=outs[k].at[me], **both)
                recv = pltpu.make_async_remote_copy(src_ref=src_of(peer), dst_ref=outs[k].at[peer], **both)
                copies.append((send, recv))
        return copies

    def start(self, ins, outs, sems):
        for first, _ in self._copies(ins, outs, sems):
            first.start()

    def middle(self, ins, outs, sems):
        pass

    def wait(self, ins, outs, sems):
        for first, recv in self._copies(ins, outs, sems):
            if recv is None:
                first.wait()
            else:
                recv.wait_recv()
                first.wait_send()


class _StagedGather:
    def __init__(self, arr, middle_at=0.8):
        self.arrs, self.aliases, self.middle_at = [arr], [], middle_at
        self.out_shape = [jax.ShapeDtypeStruct((N_DEV,) + arr.shape, arr.dtype)]
        self.scratch = [pltpu.VMEM((N_DEV,) + arr.shape, arr.dtype), pltpu.SemaphoreType.DMA((7,)),
                        pltpu.SemaphoreType.DMA((7,)), pltpu.SemaphoreType.DMA((2,))]

    def _parts(self, scr):
        stage, send_sems, recv_sems, loc_sems = scr
        x, y, c = lax.axis_index("x"), lax.axis_index("y"), lax.axis_index("c")
        me, sibling = (x, y, c), (x, y, 1 - c)
        chips = [(1 - x, y), (x, 1 - y), (1 - x, 1 - y)]

        def copy(j, block, to):
            px, py, pc = block
            slot = stage.at[4 * px + 2 * py + pc]
            return pltpu.make_async_remote_copy(src_ref=slot, dst_ref=slot, send_sem=send_sems.at[j],
                                                recv_sem=recv_sems.at[j], device_id=to,
                                                device_id_type=pl.DeviceIdType.MESH)

        return stage, loc_sems, me, sibling, chips, c, copy

    def start(self, ins, outs, scr):
        stage, loc_sems, me, sibling, chips, c, copy = self._parts(scr)
        x, y, _ = me
        own = pltpu.make_async_copy(ins[0], stage.at[4 * x + 2 * y + c], loc_sems.at[0])
        own.start()
        own.wait()
        copy(0, me, sibling).start()
        for j, chip in enumerate(chips):
            copy(1 + j, me, (*chip, c)).start()

    def middle(self, ins, outs, scr):
        stage, loc_sems, me, sibling, chips, c, copy = self._parts(scr)
        for j, chip in enumerate(chips):
            copy(1 + j, (*chip, c), me).wait_recv()
            copy(4 + j, (*chip, c), sibling).start()

    def wait(self, ins, outs, scr):
        stage, loc_sems, me, sibling, chips, c, copy = self._parts(scr)
        copy(0, sibling, me).wait_recv()
        for j, chip in enumerate(chips):
            copy(4 + j, (*chip, 1 - c), me).wait_recv()
        copy(0, me, sibling).wait_send()
        for j, chip in enumerate(chips):
            copy(1 + j, me, (*chip, c)).wait_send()
            copy(4 + j, (*chip, c), sibling).wait_send()
        whole = pltpu.make_async_copy(stage, outs[0], loc_sems.at[1])
        whole.start()
        whole.wait()


class _StagedScatter:
    def __init__(self, slabs, middle_at=0.2):
        _, r, c = slabs.shape
        self.arrs, self.aliases, self.middle_at = [slabs], [], middle_at
        self.out_shape = [jax.ShapeDtypeStruct((4, r, c), slabs.dtype)]
        self.scratch = [pltpu.VMEM((N_DEV, r, c), slabs.dtype), pltpu.VMEM((4, r, c), slabs.dtype),
                        pltpu.VMEM((3, r, c), slabs.dtype), pltpu.SemaphoreType.DMA((4,)), pltpu.SemaphoreType.DMA((4,)),
                        pltpu.SemaphoreType.DMA((3,)), pltpu.SemaphoreType.DMA((3,)), pltpu.SemaphoreType.DMA((4,))]

    def _parts(self, scr):
        stage, from_sib, from_chips, sib_send, sib_recv, ici_send, ici_recv, loc_sems = scr
        x, y, c = lax.axis_index("x"), lax.axis_index("y"), lax.axis_index("c")
        chips = [(1 - x, y), (x, 1 - y), (1 - x, 1 - y)]

        def to_sibling(j):
            return pltpu.make_async_remote_copy(src_ref=stage.at[2 * j + 1 - c], dst_ref=from_sib.at[j],
                                                send_sem=sib_send.at[j], recv_sem=sib_recv.at[j],
                                                device_id=(x, y, 1 - c), device_id_type=pl.DeviceIdType.MESH)

        def to_chip(k):
            px, py = chips[k]
            return pltpu.make_async_remote_copy(src_ref=stage.at[4 * px + 2 * py + c], dst_ref=from_chips.at[k],
                                                send_sem=ici_send.at[k], recv_sem=ici_recv.at[k],
                                                device_id=(px, py, c), device_id_type=pl.DeviceIdType.MESH)

        return stage, from_sib, from_chips, loc_sems, (x, y, c), chips, to_sibling, to_chip

    def start(self, ins, outs, scr):
        stage, _, _, loc_sems, _, _, to_sibling, _ = self._parts(scr)
        load = pltpu.make_async_copy(ins[0], stage, loc_sems.at[0])
        load.start()
        load.wait()
        for j in range(4):
            to_sibling(j).start()

    def middle(self, ins, outs, scr):
        stage, from_sib, _, _, (x, y, c), _, to_sibling, to_chip = self._parts(scr)
        for j in range(4):
            to_sibling(j).wait_recv()
            mine = stage.at[2 * j + c]
            mine[...] = (mine[...].astype(F32) + from_sib[j].astype(F32)).astype(mine.dtype)
        for k in range(3):
            to_chip(k).start()

    def wait(self, ins, outs, scr):
        stage, _, from_chips, loc_sems, (x, y, c), chips, to_sibling, to_chip = self._parts(scr)
        writes = [pltpu.make_async_copy(stage.at[4 * x + 2 * y + c], outs[0].at[2 * x + y], loc_sems.at[0])]
        for k, (px, py) in enumerate(chips):
            to_chip(k).wait_recv()
            writes.append(pltpu.make_async_copy(from_chips.at[k], outs[0].at[2 * px + py], loc_sems.at[1 + k]))
        for w in writes:
            w.start()
        for j in range(4):
            to_sibling(j).wait_send()
        for k in range(3):
            to_chip(k).wait_send()
        for w in writes:
            w.wait()


def _call(body, name, args, out_shape, grid=(), in_specs=(), out_specs=(), scratch_shapes=(), exchange=None):
    if exchange is None:
        return pl.pallas_call(body, name=name, grid=grid, in_specs=list(in_specs), out_specs=list(out_specs),
                              out_shape=list(out_shape), scratch_shapes=list(scratch_shapes),
                              compiler_params=_params())(*args), None
    exs = list(exchange) if isinstance(exchange, (list, tuple)) else [exchange]
    ni, no, ns = len(args), len(out_shape), len(scratch_shapes)
    nxi, nxo = sum(len(e.arrs) for e in exs), sum(len(e.out_shape) for e in exs)
    steps = int(np.prod(grid))
    mid_step = lambda e: min(max(int(steps * e.middle_at), 1), steps - 1)
    aliases, iat, oat = {}, ni, no
    for e in exs:
        for src, dst in e.aliases:
            aliases[iat + src] = oat + dst
        iat, oat = iat + len(e.arrs), oat + len(e.out_shape)

    def wrapped(*refs):
        a, xi = refs[:ni], refs[ni:ni + nxi]
        o, xo = refs[ni + nxi:ni + nxi + no], refs[ni + nxi + no:ni + nxi + no + nxo]
        s, xs = refs[ni + nxi + no + nxo:ni + nxi + no + nxo + ns], refs[ni + nxi + no + nxo + ns:]
        parts, iat, oat, sat = [], 0, 0, 0
        for e in exs:
            parts.append((e, xi[iat:iat + len(e.arrs)], xo[oat:oat + len(e.out_shape)], xs[sat:sat + len(e.scratch)]))
            iat, oat, sat = iat + len(e.arrs), oat + len(e.out_shape), sat + len(e.scratch)
        step = 0
        for d, g in enumerate(grid):
            step = step * g + pl.program_id(d)

        @pl.when(step == 0)
        def _():
            for e, ins, outs, sems in parts:
                e.start(ins, outs, sems)

        for at_step in sorted({mid_step(e) for e in exs}):
            @pl.when(step == at_step)
            def _():
                for e, ins, outs, sems in parts:
                    if mid_step(e) == at_step:
                        e.middle(ins, outs, sems)

        body(*a, *o, *s)

        @pl.when(step == steps - 1)
        def _():
            for e, ins, outs, sems in parts:
                e.wait(ins, outs, sems)

    hbm = pl.BlockSpec(memory_space=pltpu.HBM)
    res = pl.pallas_call(
        wrapped, name=name, grid=grid, in_specs=list(in_specs) + [hbm] * nxi, out_specs=list(out_specs) + [hbm] * nxo,
        out_shape=list(out_shape) + [o_ for e in exs for o_ in e.out_shape],
        scratch_shapes=list(scratch_shapes) + [s_ for e in exs for s_ in e.scratch],
        input_output_aliases=aliases, compiler_params=_params())(*args, *[a_ for e in exs for a_ in e.arrs])
    return res[:no], res[no:]


def _gather_two_level(arrs, name):
    n = len(arrs)
    out_shape = [jax.ShapeDtypeStruct((N_DEV,) + a.shape, a.dtype) for a in arrs]

    def body(*refs):
        ins, outs = refs[:n], refs[n:2 * n]
        send_sems, recv_sems, loc_sems = refs[2 * n:]
        x, y, c = lax.axis_index("x"), lax.axis_index("y"), lax.axis_index("c")
        me, sibling = (x, y, c), (x, y, 1 - c)
        chips = [(1 - x, y), (x, 1 - y), (1 - x, 1 - y)]

        def copy(k, j, block, to, src=None):
            px, py, pc = block
            dst = outs[k].at[4 * px + 2 * py + pc]
            return pltpu.make_async_remote_copy(src_ref=dst if src is None else src, dst_ref=dst,
                                                send_sem=send_sems.at[k, j], recv_sem=recv_sems.at[k, j],
                                                device_id=to, device_id_type=pl.DeviceIdType.MESH)

        mine = [pltpu.make_async_copy(ins[k], outs[k].at[4 * x + 2 * y + c], loc_sems.at[k]) for k in range(n)]
        first = []
        for k in range(n):
            mine[k].start()
            first.append(copy(k, 0, me, sibling, src=ins[k]))
            first += [copy(k, 1 + j, me, (*chip, c), src=ins[k]) for j, chip in enumerate(chips)]
        for cp in first:
            cp.start()
        passed = []
        for j, chip in enumerate(chips):
            for k in range(n):
                copy(k, 1 + j, (*chip, c), me).wait_recv()
                passed.append(copy(k, 4 + j, (*chip, c), sibling))
                passed[-1].start()
        for k in range(n):
            copy(k, 0, sibling, me).wait_recv()
            for j, chip in enumerate(chips):
                copy(k, 4 + j, (*chip, 1 - c), me).wait_recv()
        for cp in first + passed:
            cp.wait_send()
        for cp in mine:
            cp.wait()

    vmem = pl.BlockSpec(memory_space=pltpu.VMEM)
    return pl.pallas_call(body, name=name, out_shape=out_shape, in_specs=[vmem] * n, out_specs=[vmem] * n,
                          scratch_shapes=[pltpu.SemaphoreType.DMA((n, 7)), pltpu.SemaphoreType.DMA((n, 7)),
                                          pltpu.SemaphoreType.DMA((n,))], compiler_params=_params())(*arrs)


def _exchange(arrs, scatter, name):
    ex = _Exchange(arrs, scatter)

    def body(*refs):
        ins, outs, sems = refs[:ex.n], refs[ex.n:2 * ex.n], refs[2 * ex.n:]
        ex.start(ins, outs, sems)
        ex.wait(ins, outs, sems)

    hbm = pl.BlockSpec(memory_space=pltpu.HBM)
    return pl.pallas_call(body, name=name, out_shape=ex.out_shape, in_specs=[hbm] * ex.n, out_specs=[hbm] * ex.n,
                          scratch_shapes=ex.scratch)(*ex.arrs)


def _matmul(a, b, mode, name, out_dtype=F32, tm=512, tn=1024, tk=1024, a_fn=None, b_fn=None, extras=(),
            out_slabs=None, exchange=None):
    assert not (a_fn and b_fn) and not (b_fn and mode == "NT")
    if mode == "NN":
        (M, K), N = a.shape, b.shape[1]
    elif mode == "NT":
        (M, K), N = a.shape, b.shape[0]
    else:
        (K, M), N = a.shape, b.shape[1]
    slab_w = N // out_slabs if out_slabs else None
    if out_slabs:
        tn = max(slab_w, min(tn, N) // slab_w * slab_w)
    tm, tn, tk = min(tm, M), min(tn, N), min(tk, K)
    assert M % tm == 0 and N % tn == 0 and K % tk == 0, (name, M, N, K)
    nk = K // tk
    dims = {"NN": NN, "NT": NT, "TN": TN}[mode]
    ne = len(extras)

    def body(a_ref, b_ref, *rest):
        e_refs, o_ref, acc_ref = rest[:ne], rest[ne], rest[ne + 1]
        k = pl.program_id(2)

        @pl.when(k == 0)
        def _():
            acc_ref[...] = jnp.zeros_like(acc_ref)

        at, bt = a_ref[...], b_ref[...]
        if a_fn is not None:
            at = a_fn(at.astype(F32), *[e[...] for e in e_refs])
        if b_fn is not None:
            bt = b_fn(bt.astype(F32), *[e[...] for e in e_refs])
        acc_ref[...] += _bdot(at, bt, dims)

        @pl.when(k == nk - 1)
        def _():
            if out_slabs:
                for s in range(tn // slab_w):
                    o_ref[s] = acc_ref[:, s * slab_w:(s + 1) * slab_w].astype(out_dtype)
            else:
                o_ref[...] = acc_ref[...].astype(out_dtype)

    if mode == "TN":
        a_spec = pl.BlockSpec((tk, tm), lambda i, j, k: (k, i))
        e_spec = pl.BlockSpec((1, tm), lambda i, j, k: (0, i))
    else:
        a_spec = pl.BlockSpec((tm, tk), lambda i, j, k: (i, k))
        e_spec = pl.BlockSpec((1, tk), lambda i, j, k: (0, k))
    if mode == "NT":
        b_spec = pl.BlockSpec((tn, tk), lambda i, j, k: (j, k))
    else:
        b_spec = pl.BlockSpec((tk, tn), lambda i, j, k: (k, j))
    if b_fn is not None:
        e_spec = pl.BlockSpec((1, tn), lambda i, j, k: (0, j))
    if out_slabs:
        o_shape = jax.ShapeDtypeStruct((out_slabs, M, slab_w), out_dtype)
        o_spec = pl.BlockSpec((tn // slab_w, tm, slab_w), lambda i, j, k: (j, i, 0))
    else:
        o_shape = jax.ShapeDtypeStruct((M, N), out_dtype)
        o_spec = pl.BlockSpec((tm, tn), lambda i, j, k: (i, j))
    (out,), got = _call(body, name, (a, b, *extras), [o_shape], grid=(M // tm, N // tn, nk),
                        in_specs=[a_spec, b_spec] + [e_spec] * ne, out_specs=[o_spec],
                        scratch_shapes=[pltpu.VMEM((tm, tn), F32)], exchange=exchange)
    return out if exchange is None else (out, got)


def _modulate(x, sc, sh):
    return x * (1.0 + sc) + sh


def _square(x):
    return x * x


def _mod_part(c_all, w_ada_s, b_s):
    def body(c_ref, w_ref, b_ref, mod_ref, cond_ref):
        cv = c_ref[...]
        cond = cv * _sigmoid(cv)
        cond_ref[...] = cond
        mod_ref[...] = _bdot(cond, w_ref[...]) + b_ref[...]

    return pl.pallas_call(
        body, name="mod_part",
        out_shape=[jax.ShapeDtypeStruct((N_DEV, w_ada_s.shape[1]), F32), jax.ShapeDtypeStruct(c_all.shape, F32)],
        compiler_params=_params(),
    )(c_all, w_ada_s, b_s)


def _rms_fwd(x, w):
    rs = lax.rsqrt(_lanemean(x * x) + RMS_EPS)
    return x * rs * w, rs


def _rms_bwd(x, rs, w, dy):
    xhat = x * rs
    dxh = dy * w
    return rs * (dxh - xhat * _lanemean(dxh * xhat)), dy * xhat


def _mla_pre(z, pos_col, invf, m_rot, wq_ext, wkv_ext, qnw, kvnw, exchange=None):
    T = z.shape[0]
    tm = min(ROW_TILE, T)

    def body(z_ref, pos_ref, invf_ref, mrot_ref, wq_ref, wkv_ref, qnw_ref, kvnw_ref,
             q_ref, k_ref, v_ref, c1_ref, s1_ref, cqn_ref, ckvn_ref):
        hi = slice(HEAD_DIM, QK_PAD)
        ang = pos_ref[...].astype(F32) * invf_ref[:, hi]
        c1 = jnp.concatenate([jnp.ones((tm, HEAD_DIM), F32), mrot_ref[:, hi] * jnp.cos(ang)], axis=1)
        s1 = jnp.concatenate([jnp.zeros((tm, HEAD_DIM), F32), mrot_ref[:, hi] * jnp.sin(ang)], axis=1)
        c1_ref[...] = c1
        s1_ref[...] = s1
        cqn, _ = _rms_fwd(z_ref[:, 0:256], qnw_ref[...])
        ckvn, _ = _rms_fwd(z_ref[:, 256:512], kvnw_ref[...])
        cqn_ref[...] = cqn.astype(BF16)
        ckvn_ref[...] = ckvn.astype(BF16)
        qe = _bdot(cqn, wq_ref[...], NT)
        kve = _bdot(ckvn, wkv_ref[...])
        k_rope = z_ref[:, 512:768] * c1 + z_ref[:, 768:1024] * s1
        for h in range(HEADS):
            q_ref[h] = ((qe[:, 256 * h:256 * h + 256] * c1 + qe[:, 1024 + 256 * h:1280 + 256 * h] * s1)
                        * Q_PRESCALE).astype(BF16)
            k_ref[h] = (kve[:, 256 * h:256 * h + 256] + k_rope).astype(BF16)
            v_ref[h] = kve[:, 1024 + 128 * h:1152 + 128 * h].astype(BF16)

    row = lambda i: (i, 0)
    head = lambda i: (0, i, 0)
    return _call(
        body, "mla_pre", (z, pos_col, invf, m_rot, wq_ext, wkv_ext, qnw, kvnw), grid=(T // tm,),
        in_specs=[pl.BlockSpec((tm, 1024), lambda i: (i, 2)), pl.BlockSpec((tm, 1), row),
                  _full((1, 256)), _full((1, 256)), _full(wq_ext.shape), _full(wkv_ext.shape),
                  _full((1, 256)), _full((1, 256))],
        out_specs=[pl.BlockSpec((HEADS, tm, QK_PAD), head), pl.BlockSpec((HEADS, tm, QK_PAD), head),
                   pl.BlockSpec((HEADS, tm, HEAD_DIM), head), pl.BlockSpec((tm, 256), row), pl.BlockSpec((tm, 256), row),
                   pl.BlockSpec((tm, 256), row), pl.BlockSpec((tm, 256), row)],
        out_shape=[jax.ShapeDtypeStruct((HEADS, T, QK_PAD), BF16), jax.ShapeDtypeStruct((HEADS, T, QK_PAD), BF16),
                   jax.ShapeDtypeStruct((HEADS, T, HEAD_DIM), BF16), jax.ShapeDtypeStruct((T, 256), F32),
                   jax.ShapeDtypeStruct((T, 256), F32), jax.ShapeDtypeStruct((T, 256), BF16),
                   jax.ShapeDtypeStruct((T, 256), BF16)], exchange=exchange)


def _mla_bwd(dq, dk, dv, z, c1, s1, wq_ext, wkv_ext, qnw, kvnw):
    T = z.shape[0]
    tm = min(ROW_TILE, T)

    def body(dq_ref, dk_ref, dv_ref, z_ref, c1_ref, s1_ref, wq_ref, wkv_ref, qnw_ref, kvnw_ref,
             dz_ref, dqe_ref, dkve_ref, dnw_ref):
        @pl.when(pl.program_id(0) == 0)
        def _():
            dnw_ref[...] = jnp.zeros_like(dnw_ref)

        c1, s1 = c1_ref[...], s1_ref[...]
        dkpe = jnp.zeros((tm, QK_PAD), F32)
        for h in range(HEADS):
            dqh, dkh = dq_ref[h].astype(F32) * ATT_SCALE, dk_ref[h]
            dqe_ref[:, 256 * h:256 * h + 256] = (dqh * c1).astype(BF16)
            dqe_ref[:, 1024 + 256 * h:1280 + 256 * h] = (dqh * s1).astype(BF16)
            dkve_ref[:, 256 * h:256 * h + 256] = dkh
            dkve_ref[:, 1024 + 128 * h:1152 + 128 * h] = dv_ref[h]
            dkpe = dkpe + dkh.astype(F32)
        dcqn = _dot(dqe_ref[...], wq_ref[...])
        dckvn = _dot(dkve_ref[...], wkv_ref[...], NT)
        cq, ckv = z_ref[:, 0:256], z_ref[:, 256:512]
        _, rsq = _rms_fwd(cq, qnw_ref[...])
        _, rskv = _rms_fwd(ckv, kvnw_ref[...])
        dcq, wq_rows = _rms_bwd(cq, rsq, qnw_ref[...], dcqn)
        dckv, wkv_rows = _rms_bwd(ckv, rskv, kvnw_ref[...], dckvn)
        dnw_ref[:, 0:256] += _rowsum(wq_rows)
        dnw_ref[:, 256:512] += _rowsum(wkv_rows)
        dz_ref[:, 0:256] = dcq.astype(BF16)
        dz_ref[:, 256:512] = dckv.astype(BF16)
        dz_ref[:, 512:768] = (dkpe * c1).astype(BF16)
        dz_ref[:, 768:1024] = (dkpe * s1).astype(BF16)

    row = lambda i: (i, 0)
    head = lambda i: (0, i, 0)
    return pl.pallas_call(
        body, name="mla_bwd", grid=(T // tm,),
        in_specs=[pl.BlockSpec((HEADS, tm, QK_PAD), head), pl.BlockSpec((HEADS, tm, QK_PAD), head),
                  pl.BlockSpec((HEADS, tm, HEAD_DIM), head), pl.BlockSpec((tm, 1024), lambda i: (i, 2)),
                  pl.BlockSpec((tm, 256), row), pl.BlockSpec((tm, 256), row), _full(wq_ext.shape), _full(wkv_ext.shape),
                  _full((1, 256)), _full((1, 256))],
        out_specs=[pl.BlockSpec((tm, 1024), row), pl.BlockSpec((tm, 2048), row), pl.BlockSpec((tm, 1536), row),
                   _full((1, 512))],
        out_shape=[jax.ShapeDtypeStruct((T, 1024), BF16), jax.ShapeDtypeStruct((T, 2048), BF16),
                   jax.ShapeDtypeStruct((T, 1536), BF16), jax.ShapeDtypeStruct((1, 512), F32)],
        compiler_params=_params(),
    )(dq, dk, dv, z, c1, s1, wq_ext, wkv_ext, qnw, kvnw)


_HEAD_LANES = [slice(HEAD_DIM * h, HEAD_DIM * (h + 1)) for h in range(HEADS)]


def _lower_bound(lbraw_ref):
    a0, a1 = lbraw_ref[0:1, :], lbraw_ref[1:2, :]
    mx = jnp.maximum(a0, a1)
    e0, e1 = jnp.exp(a0 - mx), jnp.exp(a1 - mx)
    return e0 / (e0 + e1)


def _tri(lower):
    r = lax.broadcasted_iota(jnp.int32, (CHUNK, CHUNK), 0)
    c = lax.broadcasted_iota(jnp.int32, (CHUNK, CHUNK), 1)
    return (r >= c) if lower else (r <= c)


def _hgrn_gates(q, f, lb, tri_lo):
    sg = _sigmoid(f)
    forget = lb + (1.0 - lb) * sg
    k = 1.0 - forget
    b = _hdot(tri_lo.astype(F32), jnp.log(forget))
    b_ref, b_last = b[CHUNK // 2 - 1:CHUNK // 2, :], b[CHUNK - 1:CHUNK, :]
    e1, e2, e3, e4 = jnp.exp(b - b_ref), jnp.exp(b_ref - b), jnp.exp(b_last - b), jnp.exp(b)
    return dict(sg=sg, forget=forget, k=k, e1=e1, e2=e2, e3=e3, e4=e4, qa=q * e1, ka=k * e2, kl=k * e3, qb=q * e4,
                decay=jnp.exp(b_last))


def _hgrn_fwd(z, lbraw, nw, exchange=None):
    T = z.shape[0]
    G = min(HGRN_GROUP, T // CHUNK)
    rows = G * CHUNK
    n_chunks = T // CHUNK

    def body(q_ref, f_ref, i_ref, g_ref, lbraw_ref, nw_ref, oraw_ref, og_ref, sp_ref, st_ref):
        @pl.when(pl.program_id(0) == 0)
        def _():
            st_ref[...] = jnp.zeros_like(st_ref)

        lb_all = _lower_bound(lbraw_ref)
        tri_lo = _tri(True)

        def chunk(cc, carry):
            rs = pl.ds(pl.multiple_of(cc * CHUNK, CHUNK), CHUNK)
            t = _hgrn_gates(q_ref[rs, :], f_ref[rs, :], lb_all, tri_lo)
            v, gate = i_ref[rs, :], g_ref[rs, :]
            st = [st_ref[h] for h in range(HEADS)]
            a = [jnp.where(tri_lo, _bdot(t["qa"][:, s], t["ka"][:, s], NT), 0.0) for s in _HEAD_LANES]
            kv = [_bdot(v[:, s], t["kl"][:, s], TN) for s in _HEAD_LANES]
            o = [_bdot(a[h], v[:, s]) + _bdot(t["qb"][:, s], st[h], NT) for h, s in enumerate(_HEAD_LANES)]
            for h, s in enumerate(_HEAD_LANES):
                sp_ref[cc, h] = st[h]
                st_ref[h] = st[h] * t["decay"][:, s] + kv[h]
            oraw_ref[rs, :] = jnp.concatenate(o, axis=1)
            on = jnp.concatenate([_rms_fwd(o[h], nw_ref[:, s])[0] for h, s in enumerate(_HEAD_LANES)], axis=1)
            og_ref[rs, :] = (on * (gate * _sigmoid(gate))).astype(BF16)
            return carry

        lax.fori_loop(0, G, chunk, 0, unroll=4)

    col = lambda j: pl.BlockSpec((rows, 512), lambda r, j=j: (r, j))
    return _call(
        body, "hgrn_fwd", (z, z, z, z, lbraw, nw), grid=(T // rows,),
        in_specs=[col(0), col(1), col(2), col(3), _full((2, 512)), _full((1, 512))],
        out_specs=[col(0), col(0), pl.BlockSpec((G, HEADS, HEAD_DIM, HEAD_DIM), lambda r: (r, 0, 0, 0))],
        out_shape=[jax.ShapeDtypeStruct((T, 512), F32), jax.ShapeDtypeStruct((T, 512), BF16),
                   jax.ShapeDtypeStruct((n_chunks, HEADS, HEAD_DIM, HEAD_DIM), F32)],
        scratch_shapes=[pltpu.VMEM((HEADS, HEAD_DIM, HEAD_DIM), F32)], exchange=exchange)


def _hgrn_bwd(dmixcat, z, oraw, sprev, lbraw, nw, exchange=None):
    T = z.shape[0]
    G = min(HGRN_GROUP, T // CHUNK)
    rows = G * CHUNK
    ng = T // rows

    def body(dog_ref, q_ref, f_ref, i_ref, g_ref, oraw_ref, sp_ref, lbraw_ref, nw_ref,
             dz_ref, dsmall_ref, dst_ref):
        @pl.when(pl.program_id(0) == 0)
        def _():
            dst_ref[...] = jnp.zeros_like(dst_ref)
            dsmall_ref[...] = jnp.zeros_like(dsmall_ref)

        lb_all = _lower_bound(lbraw_ref)
        tri_lo, tri_up = _tri(True), _tri(False)
        rowid = lax.broadcasted_iota(jnp.int32, (CHUNK, HEADS * HEAD_DIM), 0)

        def chunk(it, carry):
            cc = G - 1 - it
            rs = pl.ds(pl.multiple_of(cc * CHUNK, CHUNK), CHUNK)
            heads = list(enumerate(_HEAD_LANES))
            cat = lambda parts: jnp.concatenate(parts, axis=1)
            per_head_mean = lambda x: cat([jnp.broadcast_to(_lanemean(x[:, s]), (CHUNK, HEAD_DIM)) for s in _HEAD_LANES])
            t = _hgrn_gates(q_ref[rs, :], f_ref[rs, :], lb_all, tri_lo)
            v, gate, o, dog, nw_all = i_ref[rs, :], g_ref[rs, :], oraw_ref[rs, :], dog_ref[rs, :], nw_ref[...]
            rs_o = lax.rsqrt(per_head_mean(o * o) + RMS_EPS)
            xhat = o * rs_o
            sgg = _sigmoid(gate)
            d_on = dog * (gate * sgg)
            dz_ref[rs, 1536:2048] = (dog * (xhat * nw_all) * (sgg * (1.0 + gate * (1.0 - sgg)))).astype(BF16)
            dxh = d_on * nw_all
            do = rs_o * (dxh - xhat * per_head_mean(dxh * xhat))
            dsmall_ref[:, 512:1024] += _rowsum(d_on * xhat)
            st = [sp_ref[cc, h] for h in range(HEADS)]
            dst = [dst_ref[h] for h in range(HEADS)]
            a = [jnp.where(tri_lo, _bdot(t["qa"][:, s], t["ka"][:, s], NT), 0.0) for s in _HEAD_LANES]
            da = [jnp.where(tri_lo, _bdot(do[:, s], v[:, s], NT), 0.0) for s in _HEAD_LANES]
            dqb = cat([_bdot(do[:, s], st[h]) for h, s in heads])
            dkl = cat([_bdot(v[:, s], dst[h]) for h, s in heads])
            dv_ = cat([_bdot(t["kl"][:, s], dst[h], NT) + _bdot(a[h], do[:, s], TN) for h, s in heads])
            dqa = cat([_bdot(da[h], t["ka"][:, s]) for h, s in heads])
            dka = cat([_bdot(da[h], t["qa"][:, s], TN) for h, s in heads])
            ddecay = cat([_rowsum(dst[h] * st[h]) for h in range(HEADS)])
            for h, s in heads:
                dst_ref[h] = dst[h] * t["decay"][:, s] + _bdot(do[:, s], t["qb"][:, s], TN)
            pa, pk, pb, pl_ = dqa * t["qa"], dka * t["ka"], dqb * t["qb"], dkl * t["kl"]
            db = pa - pk + pb - pl_
            db = db + jnp.where(rowid == CHUNK // 2 - 1, _rowsum(pk - pa), 0.0)
            db = db + jnp.where(rowid == CHUNK - 1, _rowsum(pl_) + ddecay * t["decay"], 0.0)
            dlogf = _hdot(tri_up.astype(F32), db)
            dforget = dlogf / t["forget"] - (dka * t["e2"] + dkl * t["e3"])
            sg = t["sg"]
            dz_ref[rs, 0:512] = (dqa * t["e1"] + dqb * t["e4"]).astype(BF16)
            dz_ref[rs, 512:1024] = (dforget * (1.0 - lb_all) * sg * (1.0 - sg)).astype(BF16)
            dz_ref[rs, 1024:1536] = dv_.astype(BF16)
            dsmall_ref[:, 0:512] += _rowsum(dforget * (1.0 - sg))
            return carry

        lax.fori_loop(0, G, chunk, 0, unroll=4)

    col = lambda j: pl.BlockSpec((rows, 512), lambda r, j=j: (ng - 1 - r, j))
    return _call(
        body, "hgrn_bwd", (dmixcat, z, z, z, z, oraw, sprev, lbraw, nw), grid=(ng,),
        in_specs=[col(0), col(0), col(1), col(2), col(3), col(0),
                  pl.BlockSpec((G, HEADS, HEAD_DIM, HEAD_DIM), lambda r: (ng - 1 - r, 0, 0, 0)),
                  _full((2, 512)), _full((1, 512))],
        out_specs=[pl.BlockSpec((rows, 2048), lambda r: (ng - 1 - r, 0)), _full((1, 1024))],
        out_shape=[jax.ShapeDtypeStruct((T, 2048), BF16), jax.ShapeDtypeStruct((1, 1024), F32)],
        scratch_shapes=[pltpu.VMEM((HEADS, HEAD_DIM, HEAD_DIM), F32)], exchange=exchange)


def _diag_mask(t):
    r = lax.broadcasted_iota(jnp.int32, (t, t), 0)
    c = lax.broadcasted_iota(jnp.int32, (t, t), 1)
    return r >= c


def _attn_fwd(q, k, v, exchange=None):
    _, T, _ = q.shape
    t = min(ATT_TILE, T)

    def body(q_ref, k_ref, v_ref, o_ref, lse_ref):
        i = pl.program_id(1)
        qb = q_ref[...]

        rows = lambda j: pl.ds(pl.multiple_of(j * t, t), t)

        def logits(j, masked):
            s = _dot(qb, k_ref[rows(j), :], NT)
            return jnp.where(_diag_mask(t), s, NEG_BIG) if masked else s

        def absorb(s, j, carry):
            m, l, acc = carry
            mn = jnp.maximum(m, jnp.max(s, axis=-1, keepdims=True))
            p = jnp.exp2(s - mn)
            al = jnp.exp2(m - mn)
            return mn, al * l + jnp.sum(p, axis=-1, keepdims=True), al * acc + _dot(p.astype(BF16), v_ref[rows(j), :])

        def pair(j0, carry, last_masked):
            s0, s1 = logits(j0, False), logits(j0 + 1, last_masked)
            return absorb(s1, j0 + 1, absorb(s0, j0, carry))

        init = (jnp.full((t, 1), NEG_BIG, F32), jnp.zeros((t, 1), F32), jnp.zeros((t, HEAD_DIM), F32))
        carry = lax.fori_loop(0, i // 2, lambda jj, c: pair(2 * jj, c, False), init)
        m, l, acc = lax.cond(i % 2 == 1, lambda c: pair(i - 1, c, True),
                             lambda c: absorb(logits(i, True), i, c), carry)
        o_ref[...] = acc / l
        lse_ref[...] = jnp.broadcast_to(m + jnp.log2(l), (t, HEAD_DIM))

    return _call(
        body, "attn_fwd", (q, k, v), grid=(HEADS, T // t),
        in_specs=[pl.BlockSpec((None, t, QK_PAD), lambda h, i: (h, i, 0)),
                  pl.BlockSpec((None, T, QK_PAD), lambda h, i: (h, 0, 0)),
                  pl.BlockSpec((None, T, HEAD_DIM), lambda h, i: (h, 0, 0))],
        out_specs=[pl.BlockSpec((t, HEAD_DIM), lambda h, i: (i, h)),
                   pl.BlockSpec((None, t, HEAD_DIM), lambda h, i: (h, i, 0))],
        out_shape=[jax.ShapeDtypeStruct((T, HEADS * HEAD_DIM), F32), jax.ShapeDtypeStruct((HEADS, T, HEAD_DIM), F32)],
        exchange=exchange)


def _attn_bwd(q, k, v, dmixcat, o, lse, exchange=None):
    _, T, _ = q.shape
    t = min(ATT_TILE, T)
    nq = T // t

    def body(q_ref, k_ref, v_ref, do_ref, o_ref, lse_ref, dq_ref, dk_ref, dv_ref, delta_ref, dq_acc):
        j = pl.program_id(1)

        @pl.when(j == 0)
        def _():
            dq_acc[...] = jnp.zeros_like(dq_acc)

            def fill(i, carry):
                rs = pl.ds(pl.multiple_of(i * t, t), t)
                delta_ref[rs, :] = jnp.broadcast_to(
                    jnp.sum(do_ref[rs, :] * o_ref[rs, :], axis=-1, keepdims=True), (t, HEAD_DIM))
                return carry

            lax.fori_loop(0, nq, fill, 0)

        kb, vb = k_ref[...], v_ref[...]

        def steps(blocks, carry):
            dk, dv = carry
            rs = [pl.ds(pl.multiple_of(i * t, t), t) for i, _ in blocks]
            qb = [q_ref[r, :] for r in rs]
            dob = [do_ref[r, :].astype(BF16) for r in rs]
            s = [_dot(b, kb, NT) for b in qb]
            dp = [_dot(b, vb, NT) for b in dob]
            for n, (_, masked) in enumerate(blocks):
                p = jnp.exp2(s[n] - lse_ref[rs[n], 0:1])
                if masked:
                    p = jnp.where(_diag_mask(t), p, 0.0)
                ds = (p * (dp[n] - delta_ref[rs[n], 0:1])).astype(BF16)
                dq_acc[rs[n], :] += _dot(ds, kb)
                dk = dk + _dot(ds, qb[n], TN)
                dv = dv + _dot(p.astype(BF16), dob[n], TN)
            return dk, dv

        zero = (jnp.zeros((t, QK_PAD), F32), jnp.zeros((t, HEAD_DIM), F32))
        rest = nq - 1 - j
        carry = lax.cond(rest % 2 == 1, lambda c: steps([(j, True), (j + 1, False)], c),
                         lambda c: steps([(j, True)], c), zero)
        first = j + 1 + rest % 2
        dk, dv = lax.fori_loop(0, rest // 2, lambda n, c: steps([(first + 2 * n, False), (first + 2 * n + 1, False)], c),
                               carry)
        dk_ref[...] = (dk * LN2).astype(BF16)
        dv_ref[...] = dv.astype(BF16)

        @pl.when(j == nq - 1)
        def _():
            dq_ref[...] = dq_acc[...].astype(BF16)

    return _call(
        body, "attn_bwd", (q, k, v, dmixcat, o, lse), grid=(HEADS, nq),
        in_specs=[pl.BlockSpec((None, T, QK_PAD), lambda h, j: (h, 0, 0)),
                  pl.BlockSpec((None, t, QK_PAD), lambda h, j: (h, j, 0)),
                  pl.BlockSpec((None, t, HEAD_DIM), lambda h, j: (h, j, 0)),
                  pl.BlockSpec((T, HEAD_DIM), lambda h, j: (0, HEADS + h)),
                  pl.BlockSpec((T, HEAD_DIM), lambda h, j: (0, h)),
                  pl.BlockSpec((None, T, HEAD_DIM), lambda h, j: (h, 0, 0))],
        out_specs=[pl.BlockSpec((None, T, QK_PAD), lambda h, j: (h, 0, 0)),
                   pl.BlockSpec((None, t, QK_PAD), lambda h, j: (h, j, 0)),
                   pl.BlockSpec((None, t, HEAD_DIM), lambda h, j: (h, j, 0))],
        out_shape=[jax.ShapeDtypeStruct((HEADS, T, QK_PAD), BF16), jax.ShapeDtypeStruct((HEADS, T, QK_PAD), BF16),
                   jax.ShapeDtypeStruct((HEADS, T, HEAD_DIM), BF16)],
        scratch_shapes=[pltpu.VMEM((T, HEAD_DIM), F32), pltpu.VMEM((T, QK_PAD), F32)], exchange=exchange)


def _ln_fwd(r):
    mu = _lanemean(r)
    xc = r - mu
    rstd = lax.rsqrt(_lanemean(xc * xc) + LN_EPS)
    return xc * rstd, rstd


def _ln_bwd(dxh, xhat, rstd):
    return rstd * (dxh - _lanemean(dxh) - xhat * _lanemean(dxh * xhat))


def _mix_ln1(o_hg, o_mla, w_out, x, g_a, ln1_g, ln1_b, sc_m, sh_m, exchange=None):
    T = x.shape[0]
    tm = min(ROW_TILE, T)
    half = o_hg.shape[1]

    def body(hg_ref, mla_ref, w_ref, x_ref, ga_ref, g_ref, b_ref, sc_ref, sh_ref, mix_ref, xhat_ref, rstd_ref, u2_ref):
        mix = _dot(hg_ref[...], w_ref[0:half, :]) + _bdot(mla_ref[...], w_ref[half:, :])
        mix_ref[...] = mix
        xhat, rstd = _ln_fwd(ALPHA * x_ref[...] + (1.0 + ga_ref[...]) * mix)
        xhat_ref[...] = xhat
        rstd_ref[...] = jnp.broadcast_to(rstd, (tm, 128))
        u2_ref[...] = _modulate(xhat * g_ref[...] + b_ref[...], sc_ref[...], sh_ref[...]).astype(BF16)

    row = pl.BlockSpec((tm, D_MODEL), lambda i: (i, 0))
    vec = _full((1, D_MODEL))
    halfrow = pl.BlockSpec((tm, half), lambda i: (i, 0))
    return _call(
        body, "mix_ln1", (o_hg, o_mla, w_out, x, g_a, ln1_g, ln1_b, sc_m, sh_m), grid=(T // tm,),
        in_specs=[halfrow, halfrow, _full(w_out.shape), row, vec, vec, vec, vec, vec],
        out_specs=[row, row, pl.BlockSpec((tm, 128), lambda i: (i, 0)), row],
        out_shape=[jax.ShapeDtypeStruct((T, D_MODEL), F32), jax.ShapeDtypeStruct((T, D_MODEL), F32),
                   jax.ShapeDtypeStruct((T, 128), F32), jax.ShapeDtypeStruct((T, D_MODEL), BF16)],
        exchange=exchange)


def _mlp_fwd(u2, w1, w2, xhat1, ln1_g, ln1_b, g_m, ln2_g, ln2_b, target):
    T = u2.shape[0]
    half, tf = w1[0].shape[1:]
    nf = N_DEV // MLP_SLABS
    tm = min(ROW_TILE, T)

    def body(u2_ref, w1a_ref, w1b_ref, w2_ref, xhat_ref, g1_ref, b1_ref, gm_ref, g2_ref, b2_ref, tgt_ref,
             r_ref, dr2_ref, dh_ref, small_ref, acc_ref):
        i, f = pl.program_id(0), pl.program_id(1)
        dm = D_MODEL

        @pl.when((i == 0) & (f == 0))
        def _():
            small_ref[...] = jnp.zeros_like(small_ref)

        @pl.when(f == 0)
        def _():
            acc_ref[...] = jnp.zeros_like(acc_ref)

        u2t = u2_ref[...]
        part = None
        for s in range(MLP_SLABS):
            r = jnp.maximum(_dot(u2t[:, :half], w1a_ref[s]) + _dot(u2t[:, half:], w1b_ref[s]), 0.0)
            r_ref[:, s * tf:(s + 1) * tf] = r.astype(BF16)
            d = _bdot(r * r, w2_ref[s])
            part = d if part is None else part + d
        acc_ref[...] += part

        @pl.when(f == nf - 1)
        def _():
            h = acc_ref[...]
            x1 = xhat_ref[...] * g1_ref[...] + b1_ref[...]
            xhat2, rstd2 = _ln_fwd(ALPHA * x1 + (1.0 + gm_ref[...]) * h)
            err = xhat2 * g2_ref[...] + b2_ref[...] - tgt_ref[...]
            small_ref[:, 3 * dm:] += jnp.sum(0.5 * _lanemean(err * err), axis=0, keepdims=True)
            dy = err * (1.0 / D_MODEL)
            small_ref[:, dm:2 * dm] += _rowsum(dy * xhat2)
            small_ref[:, 2 * dm:3 * dm] += _rowsum(dy)
            dr2 = _ln_bwd(dy * g2_ref[...], xhat2, rstd2)
            dr2_ref[...] = dr2
            small_ref[:, 0:dm] += _rowsum(dr2 * h)
            dh_ref[...] = ((1.0 + gm_ref[...]) * dr2).astype(BF16)

    row = pl.BlockSpec((tm, D_MODEL), lambda i, f: (i, 0))
    vec = _full((1, D_MODEL))
    return pl.pallas_call(
        body, name="mlp_fwd", grid=(T // tm, nf),
        in_specs=[row, pl.BlockSpec((MLP_SLABS, half, tf), lambda i, f: (f, 0, 0)),
                  pl.BlockSpec((MLP_SLABS, half, tf), lambda i, f: (f, 0, 0)),
                  pl.BlockSpec((MLP_SLABS, tf, D_MODEL), lambda i, f: (f, 0, 0)),
                  row, vec, vec, vec, vec, vec, row],
        out_specs=[pl.BlockSpec((tm, MLP_SLABS * tf), lambda i, f: (i, f)), row, row, _full((1, 3 * D_MODEL + 128))],
        out_shape=[jax.ShapeDtypeStruct((T, N_DEV * tf), BF16), jax.ShapeDtypeStruct((T, D_MODEL), F32),
                   jax.ShapeDtypeStruct((T, D_MODEL), BF16), jax.ShapeDtypeStruct((1, 3 * D_MODEL + 128), F32)],
        scratch_shapes=[pltpu.VMEM((tm, D_MODEL), F32)],
        compiler_params=_params(),
    )(u2, w1[0], w1[1], w2, xhat1, ln1_g, ln1_b, g_m, ln2_g, ln2_b, target)


def _mlp_bwd(dh, w1, w2, r, dr2, xhat1, rstd1, mix, ln1_g, ln1_b, sc_m, g_a):
    T = dh.shape[0]
    half, tf = w1[0].shape[1:]
    nf = N_DEV // MLP_SLABS
    tm = min(ROW_TILE, T)

    def body(dh_ref, w1a_ref, w1b_ref, w2_ref, r_ref, dr2_ref, xhat_ref, rstd_ref, mix_ref, g1_ref, b1_ref, sc_ref, ga_ref,
             dhpre_ref, dr1_ref, dmix_ref, small_ref, acc_ref):
        i, f = pl.program_id(0), pl.program_id(1)
        dm = D_MODEL

        @pl.when((i == 0) & (f == 0))
        def _():
            small_ref[...] = jnp.zeros_like(small_ref)

        @pl.when(f == 0)
        def _():
            acc_ref[...] = jnp.zeros_like(acc_ref)

        dht = dh_ref[...]
        part = None
        for s in range(MLP_SLABS):
            cols = slice(s * tf, (s + 1) * tf)
            dhpre = (_dot(dht, w2_ref[s], NT) * (2.0 * r_ref[:, cols].astype(F32))).astype(BF16)
            dhpre_ref[:, cols] = dhpre
            d = jnp.concatenate([_dot(dhpre, w1a_ref[s], NT), _dot(dhpre, w1b_ref[s], NT)], axis=1)
            part = d if part is None else part + d
        acc_ref[...] += part

        @pl.when(f == nf - 1)
        def _():
            du2 = acc_ref[...]
            xhat = xhat_ref[...]
            x1 = xhat * g1_ref[...] + b1_ref[...]
            dx1 = ALPHA * dr2_ref[...] + du2 * (1.0 + sc_ref[...])
            small_ref[:, 2 * dm:3 * dm] += _rowsum(du2 * x1)
            small_ref[:, dm:2 * dm] += _rowsum(du2)
            small_ref[:, 3 * dm:4 * dm] += _rowsum(dx1 * xhat)
            small_ref[:, 4 * dm:5 * dm] += _rowsum(dx1)
            dr1 = _ln_bwd(dx1 * g1_ref[...], xhat, rstd_ref[:, 0:1])
            dr1_ref[...] = dr1
            small_ref[:, 0:dm] += _rowsum(dr1 * mix_ref[...])
            dmix_ref[...] = ((1.0 + ga_ref[...]) * dr1).astype(BF16)

    row = pl.BlockSpec((tm, D_MODEL), lambda i, f: (i, 0))
    vec = _full((1, D_MODEL))
    return pl.pallas_call(
        body, name="mlp_bwd", grid=(T // tm, nf),
        in_specs=[row, pl.BlockSpec((MLP_SLABS, half, tf), lambda i, f: (f, 0, 0)),
                  pl.BlockSpec((MLP_SLABS, half, tf), lambda i, f: (f, 0, 0)),
                  pl.BlockSpec((MLP_SLABS, tf, D_MODEL), lambda i, f: (f, 0, 0)),
                  pl.BlockSpec((tm, MLP_SLABS * tf), lambda i, f: (i, f)), row, row,
                  pl.BlockSpec((tm, 128), lambda i, f: (i, 0)), row, vec, vec, vec, vec],
        out_specs=[pl.BlockSpec((tm, MLP_SLABS * tf), lambda i, f: (i, f)), row, row, _full((1, 5 * D_MODEL))],
        out_shape=[jax.ShapeDtypeStruct((T, N_DEV * tf), BF16), jax.ShapeDtypeStruct((T, D_MODEL), F32),
                   jax.ShapeDtypeStruct((T, D_MODEL), BF16), jax.ShapeDtypeStruct((1, 5 * D_MODEL), F32)],
        scratch_shapes=[pltpu.VMEM((tm, D_MODEL), F32)],
        compiler_params=_params(),
    )(dh, w1[0], w1[1], w2, r, dr2, xhat1, rstd1, mix, ln1_g, ln1_b, sc_m, g_a)


def _input_bwd(dz_h, dz_m, w_in_ext, x, dr1, sc_a, exchange=None):
    T = x.shape[0]
    tm = min(ROW_TILE, T)

    def body(dzh_ref, dzm_ref, w_ref, x_ref, dr1_ref, sc_ref, gx_ref, small_ref):
        @pl.when(pl.program_id(0) == 0)
        def _():
            small_ref[...] = jnp.zeros_like(small_ref)

        du = _bdot(dzh_ref[...], w_ref[0:2048, :]) + _bdot(dzm_ref[...], w_ref[2048:3072, :])
        gx_ref[...] = ALPHA * dr1_ref[...] + du * (1.0 + sc_ref[...])
        small_ref[:, D_MODEL:] += _rowsum(du * x_ref[...])
        small_ref[:, 0:D_MODEL] += _rowsum(du)

    row = pl.BlockSpec((tm, D_MODEL), lambda i: (i, 0))
    vec = _full((1, D_MODEL))
    return _call(
        body, "input_bwd", (dz_h, dz_m, w_in_ext, x, dr1, sc_a), grid=(T // tm,),
        in_specs=[pl.BlockSpec((tm, 2048), lambda i: (i, 0)), row, _full(w_in_ext.shape), row, row, vec],
        out_specs=[row, _full((1, 2 * D_MODEL))],
        out_shape=[jax.ShapeDtypeStruct((T, D_MODEL), F32), jax.ShapeDtypeStruct((1, 2 * D_MODEL), F32)],
        exchange=exchange)


def _adam_math(w, g, m, v):
    m = ADAM_B1 * m + (1.0 - ADAM_B1) * g
    v = ADAM_B2 * v + (1.0 - ADAM_B2) * (g * g)
    m_hat = m / (1.0 - ADAM_B1 ** ADAM_STEP)
    v_hat = v / (1.0 - ADAM_B2 ** ADAM_STEP)
    return -ADAM_LR * (m_hat / (jnp.sqrt(v_hat) + ADAM_EPS) + ADAM_WD * w), m, v


def _adam(g_slabs, w, m, v, name, g_fn=None, g_extra=()):
    R, C = w.shape
    tr = 256 if R % 256 == 0 else R
    ns = 0 if g_slabs is None else g_slabs.shape[0]
    slab_rows = tr if g_slabs is None or g_slabs.shape[1] == R else g_slabs.shape[1]
    assert slab_rows == tr or tr == R
    ne = len(g_extra)

    def body(*refs):
        e_refs = refs[:ne]
        refs = refs[ne:]
        if ns:
            gs_ref, refs = refs[0], refs[1:]
        w_ref, m_ref, v_ref, g_ref, d_ref, nm_ref, nv_ref = refs
        if g_fn is not None:
            g = g_fn(*e_refs)
        else:
            g = gs_ref[0].astype(F32)
            for s in range(1, ns):
                g = g + gs_ref[s].astype(F32)
            g = g[:tr]
        d, nm, nv = _adam_math(w_ref[...], g, m_ref[...], v_ref[...])
        g_ref[...] = g
        d_ref[...] = d
        nm_ref[...] = nm
        nv_ref[...] = nv

    blk = pl.BlockSpec((tr, C), lambda i: (i, 0))
    in_specs = [pl.BlockSpec((tr, e.shape[1]), lambda i: (i, 0)) if e.shape[0] == R else _full(e.shape) for e in g_extra]
    args = list(g_extra)
    if ns:
        in_specs.append(pl.BlockSpec((ns, slab_rows, C), lambda i: (0, i, 0)))
        args.append(g_slabs)
    return pl.pallas_call(
        body, name=name, grid=(R // tr,), in_specs=in_specs + [blk] * 3, out_specs=[blk] * 4,
        out_shape=[jax.ShapeDtypeStruct((R, C), F32)] * 4, compiler_params=_params(),
    )(*args, w, m, v)


def _adam_small(small_all, params):
    n = len(params)

    def body(*refs):
        s_ref, refs = refs[0], refs[1:]
        wmv, loss_ref, outs = refs[:3 * n], refs[3 * n], refs[3 * n + 1:]
        tot = s_ref[0]
        for i in range(1, N_DEV):
            tot = tot + s_ref[i]
        loss_ref[...] = tot[:, SMALL_W - 128:]
        for j, (w, _, _, off) in enumerate(params):
            w_ref, m_ref, v_ref = wmv[3 * j:3 * j + 3]
            g_ref, d_ref, nm_ref, nv_ref = outs[4 * j:4 * j + 4]
            if w.shape[0] == 2:
                lb = _lower_bound(w_ref)
                g0 = tot[:, off:off + w.shape[1]] * lb * (1.0 - lb)
                rows = [(slice(0, 1), g0), (slice(1, 2), -g0)]
            else:
                rows = [(slice(0, 1), tot[:, off:off + w.shape[1]])]
            for rs, g in rows:
                d, nm, nv = _adam_math(w_ref[rs, :], g, m_ref[rs, :], v_ref[rs, :])
                g_ref[rs, :], d_ref[rs, :], nm_ref[rs, :], nv_ref[rs, :] = g, d, nm, nv

    out_shape = [jax.ShapeDtypeStruct((1, 128), F32)]
    for w, _, _, _ in params:
        out_shape += [jax.ShapeDtypeStruct(w.shape, F32)] * 4
    res = pl.pallas_call(body, name="adam_small", out_shape=out_shape, compiler_params=_params())(
        small_all, *[a for w, m, v, _ in params for a in (w, m, v)])
    return res[0], [tuple(res[1 + 4 * j:5 + 4 * j]) for j in range(n)]


def _cols_from_slabs(g):
    s, r, c = g.shape
    return jnp.transpose(g, (1, 0, 2)).reshape(r, s * c)


def _slabs_from_cols(w):
    r, c = w.shape
    return jnp.transpose(w.reshape(r, N_DEV, c // N_DEV), (1, 0, 2))


def _rot_half_rows(wt):
    return jnp.concatenate([-wt[32:], wt[:32]], axis=0)


def _unrot_half_rows(dwt_rot):
    return jnp.concatenate([dwt_rot[32:], -dwt_rot[:32]], axis=0)


def _ext_in_t(g):
    k_in = g.shape[2]
    z64, z128 = jnp.zeros((64, k_in), BF16), jnp.zeros((128, k_in), BF16)
    wt = g.reshape(N_DEV * g.shape[1], k_in)
    main, wk = wt[:wt.shape[0] - ROPE_DIM], wt[wt.shape[0] - ROPE_DIM:]
    return jnp.concatenate([main, z128, wk, z64, z128, _rot_half_rows(wk), z64], axis=0)


def _ext_q_t(wt):
    r = wt.shape[1]
    z64, z128 = jnp.zeros((64, r), BF16), jnp.zeros((128, r), BF16)
    per = HEAD_DIM + ROPE_DIM
    main = [jnp.concatenate([wt[per * h:per * (h + 1)], z64], axis=0) for h in range(HEADS)]
    rot = [jnp.concatenate([z128, _rot_half_rows(wt[per * h + HEAD_DIM:per * (h + 1)]), z64], axis=0)
           for h in range(HEADS)]
    return jnp.concatenate(main + rot, axis=0)


def _ext_kv(w_kv_up):
    r = w_kv_up.shape[0]
    z128 = jnp.zeros((r, 128), BF16)
    wkv = w_kv_up.reshape(r, HEADS, 2 * HEAD_DIM)
    kpad = [jnp.concatenate([wkv[:, h, :HEAD_DIM], z128], axis=1) for h in range(HEADS)]
    vals = [wkv[:, h, HEAD_DIM:] for h in range(HEADS)]
    return jnp.concatenate(kpad + vals, axis=1)


def _grad_in_from_ext_t(dwt_h, dwt_m):
    dwk = dwt_m[512 + 128:512 + 192] + _unrot_half_rows(dwt_m[768 + 128:768 + 192])
    return jnp.concatenate([dwt_h, dwt_m[:512], dwk], axis=0)


def _grad_q_from_ext_t(dwq_ext_t):
    rows = []
    for h in range(HEADS):
        main, rot = dwq_ext_t[256 * h:256 * h + 256], dwq_ext_t[1024 + 256 * h:1280 + 256 * h]
        rows += [main[:128], main[128:192] + _unrot_half_rows(rot[128:192])]
    return jnp.concatenate(rows, axis=0)


def _grad_kv_from_ext(dwkv_ext):
    kvcols = []
    for h in range(HEADS):
        kvcols += [dwkv_ext[:, 256 * h:256 * h + 128], dwkv_ext[:, 1024 + 128 * h:1152 + 128 * h]]
    return jnp.concatenate(kvcols, axis=1)


SMALL_W = 6144 + 512 + 512 + 256 + 256 + 4 * 1024 + 128


def kernel(x, c, positions, w_ada, b_ada, w_in, hg_lower_bounds, hg_norm_w, mla_q_norm_w, w_q_up, mla_kv_norm_w, w_kv_up, w_out, ln1_g, ln1_b, w_mlp_in, w_mlp_out, ln2_g, ln2_b, loss_target, m_w_ada, m_b_ada, m_w_in, m_hg_lower_bounds, m_hg_norm_w, m_mla_q_norm_w, m_w_q_up, m_mla_kv_norm_w, m_w_kv_up, m_w_out, m_ln1_g, m_ln1_b, m_w_mlp_in, m_w_mlp_out, m_ln2_g, m_ln2_b, v_w_ada, v_b_ada, v_w_in, v_hg_lower_bounds, v_hg_norm_w, v_mla_q_norm_w, v_w_q_up, v_mla_kv_norm_w, v_w_kv_up, v_w_out, v_ln1_g, v_ln1_b, v_w_mlp_in, v_w_mlp_out, v_ln2_g, v_ln2_b):
    T = x.shape[1]
    me = 4 * lax.axis_index("x") + 2 * lax.axis_index("y") + lax.axis_index("c")
    xs, tgt = x[0], loss_target[0]
    transposed = ("w_in", "w_q_up")
    as_used = lambda n, a: a[0].T if n in transposed else a[0]
    big = {n: as_used(n, a) for n, a in dict(w_in=w_in, w_q_up=w_q_up, w_kv_up=w_kv_up, w_out=w_out,
                                              w_mlp_in=w_mlp_in, w_mlp_out=w_mlp_out).items()}
    names = list(big)

    bf = {n: big[n].astype(BF16) for n in names}
    g_in, g_c = _gather_two_level([bf["w_in"], c], name="gather_w_in")
    c_all = g_c.reshape(N_DEV, D_MODEL)

    ada_cols = w_ada.shape[2]
    mod_part, cond = _mod_part(c_all, w_ada[0], lax.dynamic_slice(b_ada, (0, me * ada_cols), (1, ada_cols)))
    (mod_all,) = _exchange([mod_part], scatter=False, name="gather_mod")
    mod_row = lax.dynamic_slice(mod_all, (0, me, 0), (N_DEV, 1, ada_cols)).reshape(1, N_DEV * ada_cols)
    sh_a, sc_a, g_a, sh_m, sc_m, g_m = [mod_row[:, D_MODEL * i:D_MODEL * (i + 1)] for i in range(6)]

    w_in_ext = _ext_in_t(g_in)
    half = D_MODEL // 2
    z, (w1_top,) = _matmul(xs, w_in_ext, "NT", "in_proj", a_fn=_modulate, extras=(sc_a, sh_a), tn=3072,
                           exchange=_StagedGather(bf["w_mlp_in"][:half]))
    (o_raw, o_gated, s_prev), (g_q, g_kv, g_out) = _hgrn_fwd(
        z, hg_lower_bounds, hg_norm_w, exchange=_Exchange([bf["w_q_up"], bf["w_kv_up"], bf["w_out"]], False))
    wq_ext = _ext_q_t(g_q.reshape(N_DEV * g_q.shape[1], g_q.shape[2]))
    wkv_ext = _ext_kv(_cols_from_slabs(g_kv))
    w_out_full = g_out.reshape(D_MODEL, D_MODEL)
    inv_freq = 1.0 / (ROPE_THETA ** (jnp.arange(0, ROPE_DIM, 2, dtype=F32) / ROPE_DIM))
    zeros = lambda n: jnp.zeros((n,), F32)
    invf = jnp.concatenate([zeros(128), inv_freq, inv_freq, zeros(64)]).reshape(1, QK_PAD)
    m_rot = jnp.concatenate([zeros(128), jnp.ones((64,), F32), zeros(64)]).reshape(1, QK_PAD)
    q, k, v, c1, s1, cqn, ckvn = _mla_pre(z, positions.reshape(T, 1), invf, m_rot, wq_ext, wkv_ext,
                                          mla_q_norm_w, mla_kv_norm_w)[0]
    (o_mla, lse), (w1_bot, w2) = _attn_fwd(q, k, v, exchange=[_StagedGather(bf["w_mlp_in"][half:], 0.85),
                                                              _StagedGather(bf["w_mlp_out"], 0.85)])
    w1 = (w1_top, w1_bot)
    mix, xhat1, rstd1, u2 = _mix_ln1(o_gated, o_mla, w_out_full, xs, g_a, ln1_g, ln1_b, sc_m, sh_m)[0]
    r, dr2, dh, small_mlp_fwd = _mlp_fwd(u2, w1, w2, xhat1, ln1_g, ln1_b, g_m, ln2_g, ln2_b, tgt)

    dhpre, dr1, dmix, small_mlp_bwd = _mlp_bwd(dh, w1, w2, r, dr2, xhat1, rstd1, mix, ln1_g, ln1_b, sc_m, g_a)
    received = {}
    dw2 = _matmul(r, dh, "TN", "wgrad_mlp_out", out_dtype=BF16, a_fn=_square, tm=1024, tk=2048)
    dw1 = _matmul(u2, dhpre, "TN", "wgrad_mlp_in", out_dtype=BF16, tm=1024, tk=2048, out_slabs=N_DEV)
    dmixcat = _matmul(dmix, w_out_full, "NT", "dgrad_out", tm=1024)
    dw_out = jnp.concatenate([_matmul(o_gated, dmix, "TN", "wgrad_out_hg", out_dtype=BF16, tk=2048),
                              _matmul(o_mla, dmix, "TN", "wgrad_out_mla", out_dtype=BF16, tk=2048)], axis=0)
    (dz_h, small_hgrn), (received["w_out"],) = _hgrn_bwd(
        dmixcat, z, o_raw, s_prev, hg_lower_bounds, hg_norm_w,
        exchange=_Exchange([dw_out.reshape(N_DEV, D_MODEL // N_DEV, D_MODEL)], True))
    (dq, dk, dv), (received["w_mlp_in"], received["w_mlp_out"]) = _attn_bwd(
        q, k, v, dmixcat, o_mla, lse,
        exchange=_Exchange([dw1, dw2.reshape(N_DEV, dw2.shape[0] // N_DEV, D_MODEL)], True))
    dz_m, dq_ext, dkv_ext, small_mla = _mla_bwd(dq, dk, dv, z, c1, s1, wq_ext, wkv_ext, mla_q_norm_w, mla_kv_norm_w)
    dwq_t = _grad_q_from_ext_t(_matmul(dq_ext, cqn, "TN", "wgrad_q_up", tm=1024, tk=2048))
    dwkv = _grad_kv_from_ext(_matmul(ckvn, dkv_ext, "TN", "wgrad_kv_up", tn=1536, tk=2048))
    qkv_slabs = [dwq_t.reshape((N_DEV, dwq_t.shape[0] // N_DEV, dwq_t.shape[1])).astype(BF16),
                 _slabs_from_cols(dwkv).astype(BF16)]
    dwt_h, (received["w_q_up"], received["w_kv_up"]) = _matmul(
        dz_h, xs, "TN", "wgrad_in_h", b_fn=_modulate, extras=(sc_a, sh_a), tm=1024, tk=2048,
        exchange=_Exchange(qkv_slabs, True))
    dwt_m = _matmul(dz_m, xs, "TN", "wgrad_in_m", b_fn=_modulate, extras=(sc_a, sh_a), tm=1024, tk=2048)
    dw_in_t = _grad_in_from_ext_t(dwt_h, dwt_m)
    in_slabs = dw_in_t.reshape((N_DEV, dw_in_t.shape[0] // N_DEV, dw_in_t.shape[1]))
    in_slabs = jnp.pad(in_slabs, ((0, 0), (0, -in_slabs.shape[1] % 16), (0, 0))).astype(BF16)
    (grad_x, small_in), (received["w_in"],) = _input_bwd(
        dz_h, dz_m, w_in_ext, xs, dr1, sc_a, exchange=_StagedScatter(in_slabs))

    small = jnp.concatenate([small_in, small_mlp_bwd[:, :3 * D_MODEL], small_mlp_fwd[:, :D_MODEL], small_hgrn,
                             small_mla, small_mlp_bwd[:, 3 * D_MODEL:], small_mlp_fwd[:, D_MODEL:]], axis=1)
    assert small.shape == (1, SMALL_W)
    (small_all,) = _exchange([small], scatter=False, name="gather_small")

    moments = dict(w_in=(m_w_in, v_w_in), w_q_up=(m_w_q_up, v_w_q_up), w_kv_up=(m_w_kv_up, v_w_kv_up),
                   w_out=(m_w_out, v_w_out), w_mlp_in=(m_w_mlp_in, v_w_mlp_in), w_mlp_out=(m_w_mlp_out, v_w_mlp_out))
    res = {}
    for n in names:
        res[n] = _adam(received[n], big[n], as_used(n, moments[n][0]), as_used(n, moments[n][1]), name="adam_" + n)
    dmod_cols = lax.dynamic_slice(small_all.reshape(N_DEV, SMALL_W), (0, me * ada_cols), (N_DEV, ada_cols))
    cond_t = cond.T

    def ada_grad(ct_ref, dm_ref):
        g = ct_ref[:, 0:1] * dm_ref[0:1, :]
        for b in range(1, N_DEV):
            g = g + ct_ref[:, b:b + 1] * dm_ref[b:b + 1, :]
        return g

    res["w_ada"] = _adam(None, w_ada[0], m_w_ada[0], v_w_ada[0], name="adam_w_ada", g_fn=ada_grad,
                         g_extra=(cond_t, dmod_cols))

    small_params = [("b_ada", b_ada, m_b_ada, v_b_ada, 0),
                    ("hg_lower_bounds", hg_lower_bounds, m_hg_lower_bounds, v_hg_lower_bounds, 6144),
                    ("hg_norm_w", hg_norm_w, m_hg_norm_w, v_hg_norm_w, 6656),
                    ("mla_q_norm_w", mla_q_norm_w, m_mla_q_norm_w, v_mla_q_norm_w, 7168),
                    ("mla_kv_norm_w", mla_kv_norm_w, m_mla_kv_norm_w, v_mla_kv_norm_w, 7424),
                    ("ln1_g", ln1_g, m_ln1_g, v_ln1_g, 7680), ("ln1_b", ln1_b, m_ln1_b, v_ln1_b, 8704),
                    ("ln2_g", ln2_g, m_ln2_g, v_ln2_g, 9728), ("ln2_b", ln2_b, m_ln2_b, v_ln2_b, 10752)]
    loss_row, small_res = _adam_small(small_all, [p[1:] for p in small_params])
    for p, r4 in zip(small_params, small_res):
        res[p[0]] = r4
    loss = loss_row[0, 0]

    order = ["w_ada", "b_ada", "w_in", "hg_lower_bounds", "hg_norm_w", "mla_q_norm_w", "w_q_up", "mla_kv_norm_w",
             "w_kv_up", "w_out", "ln1_g", "ln1_b", "w_mlp_in", "w_mlp_out", "ln2_g", "ln2_b"]
    def as_given(n, a):
        if n in transposed:
            a = a.T
        return a[None] if n in big or n == "w_ada" else a

    shaped = {n: tuple(as_given(n, a) for a in res[n]) for n in order}
    outs = [loss, grad_x.reshape(1, T, D_MODEL)]
    for i in range(4):
        outs += [shaped[n][i] for n in order]
    return tuple(outs)
```

```python
import functools

import jax
import jax.numpy as jnp
import numpy as np
from jax import lax
from jax.experimental import pallas as pl
from jax.experimental.pallas import tpu as pltpu

F32, BF16 = jnp.float32, jnp.bfloat16
N_DEV = 8
D_MODEL = 1024
HEADS = 4
HEAD_DIM = 128
ROPE_DIM = 64
QK_PAD = 256
CHUNK = 64
ROPE_THETA = 10000.0
RMS_EPS = 1e-6
LN_EPS = 1e-5
ALPHA = 2.0 ** 0.25
ATT_SCALE = (HEAD_DIM + ROPE_DIM) ** -0.5
LN2 = float(np.log(2.0))
Q_PRESCALE = ATT_SCALE / LN2
ADAM_LR, ADAM_B1, ADAM_B2, ADAM_EPS, ADAM_WD, ADAM_STEP = 0.001, 0.9, 0.999, 1e-08, 0.01, 10
NEG_BIG = -1e30

ROW_TILE = 512
ATT_TILE = 512
HGRN_GROUP = 8
MLP_SLABS = 4
VMEM_LIMIT = 56 * 2 ** 20

NN = (((1,), (0,)), ((), ()))
NT = (((1,), (1,)), ((), ()))
TN = (((0,), (0,)), ((), ()))


def _dot(a, b, dims=NN):
    return lax.dot_general(a, b, dims, preferred_element_type=F32)


def _bdot(a, b, dims=NN):
    return lax.dot_general(a.astype(BF16), b.astype(BF16), dims, preferred_element_type=F32)


def _hdot(a, b, dims=NN):
    return lax.dot_general(a, b, dims, precision=lax.Precision.HIGHEST, preferred_element_type=F32)


def _params():
    return pltpu.CompilerParams(vmem_limit_bytes=VMEM_LIMIT)


def _sigmoid(x):
    return 1.0 / (1.0 + jnp.exp(-x))


def _rowsum(x):
    return jnp.sum(x, axis=0, keepdims=True)


def _lanemean(x):
    return jnp.mean(x, axis=-1, keepdims=True)


def _full(shape):
    nd = len(shape)
    return pl.BlockSpec(shape, lambda *_: (0,) * nd)


class _Exchange:
    def __init__(self, arrs, scatter):
        self.arrs, self.scatter, self.n, self.aliases, self.middle_at = list(arrs), scatter, len(arrs), [], 0.5
        self.out_shape = [jax.ShapeDtypeStruct((N_DEV,) + (a.shape[1:] if scatter else a.shape), a.dtype)
                          for a in self.arrs]
        n = self.n
        self.scratch = [pltpu.SemaphoreType.DMA((n, N_DEV - 1)), pltpu.SemaphoreType.DMA((n, N_DEV - 1)),
                        pltpu.SemaphoreType.DMA((n,))]

    def _copies(self, ins, outs, sems):
        send_sems, recv_sems, loc_sems = sems
        x, y, c = lax.axis_index("x"), lax.axis_index("y"), lax.axis_index("c")
        me = 4 * x + 2 * y + c
        copies = []
        for k in range(self.n):
            src_of = (lambda i, k=k: ins[k].at[i]) if self.scatter else (lambda i, k=k: ins[k])
            copies.append((pltpu.make_async_copy(src_of(me), outs[k].at[me], loc_sems.at[k]), None))
            for p in range(1, N_DEV):
                px = (1 - x) if p & 4 else x
                py = (1 - y) if p & 2 else y
                pc = (1 - c) if p & 1 else c
                peer = 4 * px + 2 * py + pc
                both = dict(send_sem=send_sems.at[k, p - 1], recv_sem=recv_sems.at[k, p - 1],
                            device_id=(px, py, pc), device_id_type=pl.DeviceIdType.MESH)
                send = pltpu.make_async_remote_copy(src_ref=src_of(peer), dst_ref=outs[k].at[me], **both)
                recv = pltpu.make_async_remote_copy(src_ref=src_of(peer), dst_ref=outs[k].at[peer], **both)
                copies.append((send, recv))
        return copies

    def start(self, ins, outs, sems):
        for first, _ in self._copies(ins, outs, sems):
            first.start()

    def middle(self, ins, outs, sems):
        pass

    def wait(self, ins, outs, sems):
        for first, recv in self._copies(ins, outs, sems):
            if recv is None:
                first.wait()
            else:
                recv.wait_recv()
                first.wait_send()


class _StagedGather:
    def __init__(self, arr, middle_at=0.8):
        self.arrs, self.aliases, self.middle_at = [arr], [], middle_at
        self.out_shape = [jax.ShapeDtypeStruct((N_DEV,) + arr.shape, arr.dtype)]
        self.scratch = [pltpu.VMEM((N_DEV,) + arr.shape, arr.dtype), pltpu.SemaphoreType.DMA((7,)),
                        pltpu.SemaphoreType.DMA((7,)), pltpu.SemaphoreType.DMA((2,))]

    def _parts(self, scr):
        stage, send_sems, recv_sems, loc_sems = scr
        x, y, c = lax.axis_index("x"), lax.axis_index("y"), lax.axis_index("c")
        me, sibling = (x, y, c), (x, y, 1 - c)
        chips = [(1 - x, y), (x, 1 - y), (1 - x, 1 - y)]

        def copy(j, block, to):
            px, py, pc = block
            slot = stage.at[4 * px + 2 * py + pc]
            return pltpu.make_async_remote_copy(src_ref=slot, dst_ref=slot, send_sem=send_sems.at[j],
                                                recv_sem=recv_sems.at[j], device_id=to,
                                                device_id_type=pl.DeviceIdType.MESH)

        return stage, loc_sems, me, sibling, chips, c, copy

    def start(self, ins, outs, scr):
        stage, loc_sems, me, sibling, chips, c, copy = self._parts(scr)
        x, y, _ = me
        own = pltpu.make_async_copy(ins[0], stage.at[4 * x + 2 * y + c], loc_sems.at[0])
        own.start()
        own.wait()
        copy(0, me, sibling).start()
        for j, chip in enumerate(chips):
            copy(1 + j, me, (*chip, c)).start()

    def middle(self, ins, outs, scr):
        stage, loc_sems, me, sibling, chips, c, copy = self._parts(scr)
        for j, chip in enumerate(chips):
            copy(1 + j, (*chip, c), me).wait_recv()
            copy(4 + j, (*chip, c), sibling).start()

    def wait(self, ins, outs, scr):
        stage, loc_sems, me, sibling, chips, c, copy = self._parts(scr)
        copy(0, sibling, me).wait_recv()
        for j, chip in enumerate(chips):
            copy(4 + j, (*chip, 1 - c), me).wait_recv()
        copy(0, me, sibling).wait_send()
        for j, chip in enumerate(chips):
            copy(1 + j, me, (*chip, c)).wait_send()
            copy(4 + j, (*chip, c), sibling).wait_send()
        whole = pltpu.make_async_copy(stage, outs[0], loc_sems.at[1])
        whole.start()
        whole.wait()


class _StagedScatter:
    def __init__(self, slabs, middle_at=0.2):
        _, r, c = slabs.shape
        self.arrs, self.aliases, self.middle_at = [slabs], [], middle_at
        self.out_shape = [jax.ShapeDtypeStruct((4, r, c), slabs.dtype)]
        self.scratch = [pltpu.VMEM((N_DEV, r, c), slabs.dtype), pltpu.VMEM((4, r, c), slabs.dtype),
                        pltpu.VMEM((3, r, c), slabs.dtype), pltpu.SemaphoreType.DMA((4,)), pltpu.SemaphoreType.DMA((4,)),
                        pltpu.SemaphoreType.DMA((3,)), pltpu.SemaphoreType.DMA((3,)), pltpu.SemaphoreType.DMA((4,))]

    def _parts(self, scr):
        stage, from_sib, from_chips, sib_send, sib_recv, ici_send, ici_recv, loc_sems = scr
        x, y, c = lax.axis_index("x"), lax.axis_index("y"), lax.axis_index("c")
        chips = [(1 - x, y), (x, 1 - y), (1 - x, 1 - y)]

        def to_sibling(j):
            return pltpu.make_async_remote_copy(src_ref=stage.at[2 * j + 1 - c], dst_ref=from_sib.at[j],
                                                send_sem=sib_send.at[j], recv_sem=sib_recv.at[j],
                                                device_id=(x, y, 1 - c), device_id_type=pl.DeviceIdType.MESH)

        def to_chip(k):
            px, py = chips[k]
            return pltpu.make_async_remote_copy(src_ref=stage.at[4 * px + 2 * py + c], dst_ref=from_chips.at[k],
                                                send_sem=ici_send.at[k], recv_sem=ici_recv.at[k],
                                                device_id=(px, py, c), device_id_type=pl.DeviceIdType.MESH)

        return stage, from_sib, from_chips, loc_sems, (x, y, c), chips, to_sibling, to_chip

    def start(self, ins, outs, scr):
        stage, _, _, loc_sems, _, _, to_sibling, _ = self._parts(scr)
        load = pltpu.make_async_copy(ins[0], stage, loc_sems.at[0])
        load.start()
        load.wait()
        for j in range(4):
            to_sibling(j).start()

    def middle(self, ins, outs, scr):
        stage, from_sib, _, _, (x, y, c), _, to_sibling, to_chip = self._parts(scr)
        for j in range(4):
            to_sibling(j).wait_recv()
            mine = stage.at[2 * j + c]
            mine[...] = (mine[...].astype(F32) + from_sib[j].astype(F32)).astype(mine.dtype)
        for k in range(3):
            to_chip(k).start()

    def wait(self, ins, outs, scr):
        stage, _, from_chips, loc_sems, (x, y, c), chips, to_sibling, to_chip = self._parts(scr)
        writes = [pltpu.make_async_copy(stage.at[4 * x + 2 * y + c], outs[0].at[2 * x + y], loc_sems.at[0])]
        for k, (px, py) in enumerate(chips):
            to_chip(k).wait_recv()
            writes.append(pltpu.make_async_copy(from_chips.at[k], outs[0].at[2 * px + py], loc_sems.at[1 + k]))
        for w in writes:
            w.start()
        for j in range(4):
            to_sibling(j).wait_send()
        for k in range(3):
            to_chip(k).wait_send()
        for w in writes:
            w.wait()


def _call(body, name, args, out_shape, grid=(), in_specs=(), out_specs=(), scratch_shapes=(), exchange=None):
    if exchange is None:
        return pl.pallas_call(body, name=name, grid=grid, in_specs=list(in_specs), out_specs=list(out_specs),
                              out_shape=list(out_shape), scratch_shapes=list(scratch_shapes),
                              compiler_params=_params())(*args), None
    exs = list(exchange) if isinstance(exchange, (list, tuple)) else [exchange]
    ni, no, ns = len(args), len(out_shape), len(scratch_shapes)
    nxi, nxo = sum(len(e.arrs) for e in exs), sum(len(e.out_shape) for e in exs)
    steps = int(np.prod(grid))
    mid_step = lambda e: min(max(int(steps * e.middle_at), 1), steps - 1)
    aliases, iat, oat = {}, ni, no
    for e in exs:
        for src, dst in e.aliases:
            aliases[iat + src] = oat + dst
        iat, oat = iat + len(e.arrs), oat + len(e.out_shape)

    def wrapped(*refs):
        a, xi = refs[:ni], refs[ni:ni + nxi]
        o, xo = refs[ni + nxi:ni + nxi + no], refs[ni + nxi + no:ni + nxi + no + nxo]
        s, xs = refs[ni + nxi + no + nxo:ni + nxi + no + nxo + ns], refs[ni + nxi + no + nxo + ns:]
        parts, iat, oat, sat = [], 0, 0, 0
        for e in exs:
            parts.append((e, xi[iat:iat + len(e.arrs)], xo[oat:oat + len(e.out_shape)], xs[sat:sat + len(e.scratch)]))
            iat, oat, sat = iat + len(e.arrs), oat + len(e.out_shape), sat + len(e.scratch)
        step = 0
        for d, g in enumerate(grid):
            step = step * g + pl.program_id(d)

        @pl.when(step == 0)
        def _():
            for e, ins, outs, sems in parts:
                e.start(ins, outs, sems)

        for at_step in sorted({mid_step(e) for e in exs}):
            @pl.when(step == at_step)
            def _():
                for e, ins, outs, sems in parts:
                    if mid_step(e) == at_step:
                        e.middle(ins, outs, sems)

        body(*a, *o, *s)

        @pl.when(step == steps - 1)
        def _():
            for e, ins, outs, sems in parts:
                e.wait(ins, outs, sems)

    hbm = pl.BlockSpec(memory_space=pltpu.HBM)
    res = pl.pallas_call(
        wrapped, name=name, grid=grid, in_specs=list(in_specs) + [hbm] * nxi, out_specs=list(out_specs) + [hbm] * nxo,
        out_shape=list(out_shape) + [o_ for e in exs for o_ in e.out_shape],
        scratch_shapes=list(scratch_shapes) + [s_ for e in exs for s_ in e.scratch],
        input_output_aliases=aliases, compiler_params=_params())(*args, *[a_ for e in exs for a_ in e.arrs])
    return res[:no], res[no:]


def _gather_two_level(arrs, name):
    n = len(arrs)
    out_shape = [jax.ShapeDtypeStruct((N_DEV,) + a.shape, a.dtype) for a in arrs]

    def body(*refs):
        ins, outs = refs[:n], refs[n:2 * n]
        send_sems, recv_sems, loc_sems = refs[2 * n:]
        x, y, c = lax.axis_index("x"), lax.axis_index("y"), lax.axis_index("c")
        me, sibling = (x, y, c), (x, y, 1 - c)
        chips = [(1 - x, y), (x, 1 - y), (1 - x, 1 - y)]

        def copy(k, j, block, to, src=None):
            px, py, pc = block
            dst = outs[k].at[4 * px + 2 * py + pc]
            return pltpu.make_async_remote_copy(src_ref=dst if src is None else src, dst_ref=dst,
                                                send_sem=send_sems.at[k, j], recv_sem=recv_sems.at[k, j],
                                                device_id=to, device_id_type=pl.DeviceIdType.MESH)

        mine = [pltpu.make_async_copy(ins[k], outs[k].at[4 * x + 2 * y + c], loc_sems.at[k]) for k in range(n)]
        first = []
        for k in range(n):
            mine[k].start()
            first.append(copy(k, 0, me, sibling, src=ins[k]))
            first += [copy(k, 1 + j, me, (*chip, c), src=ins[k]) for j, chip in enumerate(chips)]
        for cp in first:
            cp.start()
        passed = []
        for j, chip in enumerate(chips):
            for k in range(n):
                copy(k, 1 + j, (*chip, c), me).wait_recv()
                passed.append(copy(k, 4 + j, (*chip, c), sibling))
                passed[-1].start()
        for k in range(n):
            copy(k, 0, sibling, me).wait_recv()
            for j, chip in enumerate(chips):
                copy(k, 4 + j, (*chip, 1 - c), me).wait_recv()
        for cp in first + passed:
            cp.wait_send()
        for cp in mine:
            cp.wait()

    vmem = pl.BlockSpec(memory_space=pltpu.VMEM)
    return pl.pallas_call(body, name=name, out_shape=out_shape, in_specs=[vmem] * n, out_specs=[vmem] * n,
                          scratch_shapes=[pltpu.SemaphoreType.DMA((n, 7)), pltpu.SemaphoreType.DMA((n, 7)),
                                          pltpu.SemaphoreType.DMA((n,))], compiler_params=_params())(*arrs)


def _exchange(arrs, scatter, name):
    ex = _Exchange(arrs, scatter)

    def body(*refs):
        ins, outs, sems = refs[:ex.n], refs[ex.n:2 * ex.n], refs[2 * ex.n:]
        ex.start(ins, outs, sems)
        ex.wait(ins, outs, sems)

    hbm = pl.BlockSpec(memory_space=pltpu.HBM)
    return pl.pallas_call(body, name=name, out_shape=ex.out_shape, in_specs=[hbm] * ex.n, out_specs=[hbm] * ex.n,
                          scratch_shapes=ex.scratch)(*ex.arrs)


def _matmul(a, b, mode, name, out_dtype=F32, tm=512, tn=1024, tk=1024, a_fn=None, b_fn=None, extras=(),
            out_slabs=None, exchange=None):
    assert not (a_fn and b_fn) and not (b_fn and mode == "NT")
    if mode == "NN":
        (M, K), N = a.shape, b.shape[1]
    elif mode == "NT":
        (M, K), N = a.shape, b.shape[0]
    else:
        (K, M), N = a.shape, b.shape[1]
    slab_w = N // out_slabs if out_slabs else None
    if out_slabs:
        tn = max(slab_w, min(tn, N) // slab_w * slab_w)
    tm, tn, tk = min(tm, M), min(tn, N), min(tk, K)
    assert M % tm == 0 and N % tn == 0 and K % tk == 0, (name, M, N, K)
    nk = K // tk
    dims = {"NN": NN, "NT": NT, "TN": TN}[mode]
    ne = len(extras)

    def body(a_ref, b_ref, *rest):
        e_refs, o_ref, acc_ref = rest[:ne], rest[ne], rest[ne + 1]
        k = pl.program_id(2)

        @pl.when(k == 0)
        def _():
            acc_ref[...] = jnp.zeros_like(acc_ref)

        at, bt = a_ref[...], b_ref[...]
        if a_fn is not None:
            at = a_fn(at.astype(F32), *[e[...] for e in e_refs])
        if b_fn is not None:
            bt = b_fn(bt.astype(F32), *[e[...] for e in e_refs])
        acc_ref[...] += _bdot(at, bt, dims)

        @pl.when(k == nk - 1)
        def _():
            if out_slabs:
                for s in range(tn // slab_w):
                    o_ref[s] = acc_ref[:, s * slab_w:(s + 1) * slab_w].astype(out_dtype)
            else:
                o_ref[...] = acc_ref[...].astype(out_dtype)

    if mode == "TN":
        a_spec = pl.BlockSpec((tk, tm), lambda i, j, k: (k, i))
        e_spec = pl.BlockSpec((1, tm), lambda i, j, k: (0, i))
    else:
        a_spec = pl.BlockSpec((tm, tk), lambda i, j, k: (i, k))
        e_spec = pl.BlockSpec((1, tk), lambda i, j, k: (0, k))
    if mode == "NT":
        b_spec = pl.BlockSpec((tn, tk), lambda i, j, k: (j, k))
    else:
        b_spec = pl.BlockSpec((tk, tn), lambda i, j, k: (k, j))
    if b_fn is not None:
        e_spec = pl.BlockSpec((1, tn), lambda i, j, k: (0, j))
    if out_slabs:
        o_shape = jax.ShapeDtypeStruct((out_slabs, M, slab_w), out_dtype)
        o_spec = pl.BlockSpec((tn // slab_w, tm, slab_w), lambda i, j, k: (j, i, 0))
    else:
        o_shape = jax.ShapeDtypeStruct((M, N), out_dtype)
        o_spec = pl.BlockSpec((tm, tn), lambda i, j, k: (i, j))
    (out,), got = _call(body, name, (a, b, *extras), [o_shape], grid=(M // tm, N // tn, nk),
                        in_specs=[a_spec, b_spec] + [e_spec] * ne, out_specs=[o_spec],
                        scratch_shapes=[pltpu.VMEM((tm, tn), F32)], exchange=exchange)
    return out if exchange is None else (out, got)


def _modulate(x, sc, sh):
    return x * (1.0 + sc) + sh


def _square(x):
    return x * x


def _mod_part(c_all, w_ada_s, b_s):
    def body(c_ref, w_ref, b_ref, mod_ref, cond_ref):
        cv = c_ref[...]
        cond = cv * _sigmoid(cv)
        cond_ref[...] = cond
        mod_ref[...] = _bdot(cond, w_ref[...]) + b_ref[...]

    return pl.pallas_call(
        body, name="mod_part",
        out_shape=[jax.ShapeDtypeStruct((N_DEV, w_ada_s.shape[1]), F32), jax.ShapeDtypeStruct(c_all.shape, F32)],
        compiler_params=_params(),
    )(c_all, w_ada_s, b_s)


def _rms_fwd(x, w):
    rs = lax.rsqrt(_lanemean(x * x) + RMS_EPS)
    return x * rs * w, rs


def _rms_bwd(x, rs, w, dy):
    xhat = x * rs
    dxh = dy * w
    return rs * (dxh - xhat * _lanemean(dxh * xhat)), dy * xhat


def _mla_pre(z, pos_col, invf, m_rot, wq_ext, wkv_ext, qnw, kvnw, exchange=None):
    T = z.shape[0]
    tm = min(ROW_TILE, T)

    def body(z_ref, pos_ref, invf_ref, mrot_ref, wq_ref, wkv_ref, qnw_ref, kvnw_ref,
             q_ref, k_ref, v_ref, c1_ref, s1_ref, cqn_ref, ckvn_ref):
        hi = slice(HEAD_DIM, QK_PAD)
        ang = pos_ref[...].astype(F32) * invf_ref[:, hi]
        c1 = jnp.concatenate([jnp.ones((tm, HEAD_DIM), F32), mrot_ref[:, hi] * jnp.cos(ang)], axis=1)
        s1 = jnp.concatenate([jnp.zeros((tm, HEAD_DIM), F32), mrot_ref[:, hi] * jnp.sin(ang)], axis=1)
        c1_ref[...] = c1
        s1_ref[...] = s1
        cqn, _ = _rms_fwd(z_ref[:, 0:256], qnw_ref[...])
        ckvn, _ = _rms_fwd(z_ref[:, 256:512], kvnw_ref[...])
        cqn_ref[...] = cqn.astype(BF16)
        ckvn_ref[...] = ckvn.astype(BF16)
        qe = _bdot(cqn, wq_ref[...], NT)
        kve = _bdot(ckvn, wkv_ref[...])
        k_rope = z_ref[:, 512:768] * c1 + z_ref[:, 768:1024] * s1
        for h in range(HEADS):
            q_ref[h] = ((qe[:, 256 * h:256 * h + 256] * c1 + qe[:, 1024 + 256 * h:1280 + 256 * h] * s1)
                        * Q_PRESCALE).astype(BF16)
            k_ref[h] = (kve[:, 256 * h:256 * h + 256] + k_rope).astype(BF16)
            v_ref[h] = kve[:, 1024 + 128 * h:1152 + 128 * h].astype(BF16)

    row = lambda i: (i, 0)
    head = lambda i: (0, i, 0)
    return _call(
        body, "mla_pre", (z, pos_col, invf, m_rot, wq_ext, wkv_ext, qnw, kvnw), grid=(T // tm,),
        in_specs=[pl.BlockSpec((tm, 1024), lambda i: (i, 2)), pl.BlockSpec((tm, 1), row),
                  _full((1, 256)), _full((1, 256)), _full(wq_ext.shape), _full(wkv_ext.shape),
                  _full((1, 256)), _full((1, 256))],
        out_specs=[pl.BlockSpec((HEADS, tm, QK_PAD), head), pl.BlockSpec((HEADS, tm, QK_PAD), head),
                   pl.BlockSpec((HEADS, tm, HEAD_DIM), head), pl.BlockSpec((tm, 256), row), pl.BlockSpec((tm, 256), row),
                   pl.BlockSpec((tm, 256), row), pl.BlockSpec((tm, 256), row)],
        out_shape=[jax.ShapeDtypeStruct((HEADS, T, QK_PAD), BF16), jax.ShapeDtypeStruct((HEADS, T, QK_PAD), BF16),
                   jax.ShapeDtypeStruct((HEADS, T, HEAD_DIM), BF16), jax.ShapeDtypeStruct((T, 256), F32),
                   jax.ShapeDtypeStruct((T, 256), F32), jax.ShapeDtypeStruct((T, 256), BF16),
                   jax.ShapeDtypeStruct((T, 256), BF16)], exchange=exchange)


def _mla_bwd(dq, dk, dv, z, c1, s1, wq_ext, wkv_ext, qnw, kvnw):
    T = z.shape[0]
    tm = min(ROW_TILE, T)

    def body(dq_ref, dk_ref, dv_ref, z_ref, c1_ref, s1_ref, wq_ref, wkv_ref, qnw_ref, kvnw_ref,
             dz_ref, dqe_ref, dkve_ref, dnw_ref):
        @pl.when(pl.program_id(0) == 0)
        def _():
            dnw_ref[...] = jnp.zeros_like(dnw_ref)

        c1, s1 = c1_ref[...], s1_ref[...]
        dkpe = jnp.zeros((tm, QK_PAD), F32)
        for h in range(HEADS):
            dqh, dkh = dq_ref[h].astype(F32) * ATT_SCALE, dk_ref[h]
            dqe_ref[:, 256 * h:256 * h + 256] = (dqh * c1).astype(BF16)
            dqe_ref[:, 1024 + 256 * h:1280 + 256 * h] = (dqh * s1).astype(BF16)
            dkve_ref[:, 256 * h:256 * h + 256] = dkh
            dkve_ref[:, 1024 + 128 * h:1152 + 128 * h] = dv_ref[h]
            dkpe = dkpe + dkh.astype(F32)
        dcqn = _dot(dqe_ref[...], wq_ref[...])
        dckvn = _dot(dkve_ref[...], wkv_ref[...], NT)
        cq, ckv = z_ref[:, 0:256], z_ref[:, 256:512]
        _, rsq = _rms_fwd(cq, qnw_ref[...])
        _, rskv = _rms_fwd(ckv, kvnw_ref[...])
        dcq, wq_rows = _rms_bwd(cq, rsq, qnw_ref[...], dcqn)
        dckv, wkv_rows = _rms_bwd(ckv, rskv, kvnw_ref[...], dckvn)
        dnw_ref[:, 0:256] += _rowsum(wq_rows)
        dnw_ref[:, 256:512] += _rowsum(wkv_rows)
        dz_ref[:, 0:256] = dcq.astype(BF16)
        dz_ref[:, 256:512] = dckv.astype(BF16)
        dz_ref[:, 512:768] = (dkpe * c1).astype(BF16)
        dz_ref[:, 768:1024] = (dkpe * s1).astype(BF16)

    row = lambda i: (i, 0)
    head = lambda i: (0, i, 0)
    return pl.pallas_call(
        body, name="mla_bwd", grid=(T // tm,),
        in_specs=[pl.BlockSpec((HEADS, tm, QK_PAD), head), pl.BlockSpec((HEADS, tm, QK_PAD), head),
                  pl.BlockSpec((HEADS, tm, HEAD_DIM), head), pl.BlockSpec((tm, 1024), lambda i: (i, 2)),
                  pl.BlockSpec((tm, 256), row), pl.BlockSpec((tm, 256), row), _full(wq_ext.shape), _full(wkv_ext.shape),
                  _full((1, 256)), _full((1, 256))],
        out_specs=[pl.BlockSpec((tm, 1024), row), pl.BlockSpec((tm, 2048), row), pl.BlockSpec((tm, 1536), row),
                   _full((1, 512))],
        out_shape=[jax.ShapeDtypeStruct((T, 1024), BF16), jax.ShapeDtypeStruct((T, 2048), BF16),
                   jax.ShapeDtypeStruct((T, 1536), BF16), jax.ShapeDtypeStruct((1, 512), F32)],
        compiler_params=_params(),
    )(dq, dk, dv, z, c1, s1, wq_ext, wkv_ext, qnw, kvnw)


_HEAD_LANES = [slice(HEAD_DIM * h, HEAD_DIM * (h + 1)) for h in range(HEADS)]


def _lower_bound(lbraw_ref):
    a0, a1 = lbraw_ref[0:1, :], lbraw_ref[1:2, :]
    mx = jnp.maximum(a0, a1)
    e0, e1 = jnp.exp(a0 - mx), jnp.exp(a1 - mx)
    return e0 / (e0 + e1)


def _tri(lower):
    r = lax.broadcasted_iota(jnp.int32, (CHUNK, CHUNK), 0)
    c = lax.broadcasted_iota(jnp.int32, (CHUNK, CHUNK), 1)
    return (r >= c) if lower else (r <= c)


def _hgrn_gates(q, f, lb, tri_lo):
    sg = _sigmoid(f)
    forget = lb + (1.0 - lb) * sg
    k = 1.0 - forget
    b = _hdot(tri_lo.astype(F32), jnp.log(forget))
    b_ref, b_last = b[CHUNK // 2 - 1:CHUNK // 2, :], b[CHUNK - 1:CHUNK, :]
    e1, e2, e3, e4 = jnp.exp(b - b_ref), jnp.exp(b_ref - b), jnp.exp(b_last - b), jnp.exp(b)
    return dict(sg=sg, forget=forget, k=k, e1=e1, e2=e2, e3=e3, e4=e4, qa=q * e1, ka=k * e2, kl=k * e3, qb=q * e4,
                decay=jnp.exp(b_last))


def _hgrn_fwd(z, lbraw, nw, exchange=None):
    T = z.shape[0]
    G = min(HGRN_GROUP, T // CHUNK)
    rows = G * CHUNK
    n_chunks = T // CHUNK

    def body(q_ref, f_ref, i_ref, g_ref, lbraw_ref, nw_ref, oraw_ref, og_ref, sp_ref, st_ref):
        @pl.when(pl.program_id(0) == 0)
        def _():
            st_ref[...] = jnp.zeros_like(st_ref)

        lb_all = _lower_bound(lbraw_ref)
        tri_lo = _tri(True)

        def chunk(cc, carry):
            rs = pl.ds(pl.multiple_of(cc * CHUNK, CHUNK), CHUNK)
            t = _hgrn_gates(q_ref[rs, :], f_ref[rs, :], lb_all, tri_lo)
            v, gate = i_ref[rs, :], g_ref[rs, :]
            st = [st_ref[h] for h in range(HEADS)]
            a = [jnp.where(tri_lo, _bdot(t["qa"][:, s], t["ka"][:, s], NT), 0.0) for s in _HEAD_LANES]
            kv = [_bdot(v[:, s], t["kl"][:, s], TN) for s in _HEAD_LANES]
            o = [_bdot(a[h], v[:, s]) + _bdot(t["qb"][:, s], st[h], NT) for h, s in enumerate(_HEAD_LANES)]
            for h, s in enumerate(_HEAD_LANES):
                sp_ref[cc, h] = st[h]
                st_ref[h] = st[h] * t["decay"][:, s] + kv[h]
            oraw_ref[rs, :] = jnp.concatenate(o, axis=1)
            on = jnp.concatenate([_rms_fwd(o[h], nw_ref[:, s])[0] for h, s in enumerate(_HEAD_LANES)], axis=1)
            og_ref[rs, :] = (on * (gate * _sigmoid(gate))).astype(BF16)
            return carry

        lax.fori_loop(0, G, chunk, 0, unroll=4)

    col = lambda j: pl.BlockSpec((rows, 512), lambda r, j=j: (r, j))
    return _call(
        body, "hgrn_fwd", (z, z, z, z, lbraw, nw), grid=(T // rows,),
        in_specs=[col(0), col(1), col(2), col(3), _full((2, 512)), _full((1, 512))],
        out_specs=[col(0), col(0), pl.BlockSpec((G, HEADS, HEAD_DIM, HEAD_DIM), lambda r: (r, 0, 0, 0))],
        out_shape=[jax.ShapeDtypeStruct((T, 512), F32), jax.ShapeDtypeStruct((T, 512), BF16),
                   jax.ShapeDtypeStruct((n_chunks, HEADS, HEAD_DIM, HEAD_DIM), F32)],
        scratch_shapes=[pltpu.VMEM((HEADS, HEAD_DIM, HEAD_DIM), F32)], exchange=exchange)


def _hgrn_bwd(dmixcat, z, oraw, sprev, lbraw, nw, exchange=None):
    T = z.shape[0]
    G = min(HGRN_GROUP, T // CHUNK)
    rows = G * CHUNK
    ng = T // rows

    def body(dog_ref, q_ref, f_ref, i_ref, g_ref, oraw_ref, sp_ref, lbraw_ref, nw_ref,
             dz_ref, dsmall_ref, dst_ref):
        @pl.when(pl.program_id(0) == 0)
        def _():
            dst_ref[...] = jnp.zeros_like(dst_ref)
            dsmall_ref[...] = jnp.zeros_like(dsmall_ref)

        lb_all = _lower_bound(lbraw_ref)
        tri_lo, tri_up = _tri(True), _tri(False)
        rowid = lax.broadcasted_iota(jnp.int32, (CHUNK, HEADS * HEAD_DIM), 0)

        def chunk(it, carry):
            cc = G - 1 - it
            rs = pl.ds(pl.multiple_of(cc * CHUNK, CHUNK), CHUNK)
            heads = list(enumerate(_HEAD_LANES))
            cat = lambda parts: jnp.concatenate(parts, axis=1)
            per_head_mean = lambda x: cat([jnp.broadcast_to(_lanemean(x[:, s]), (CHUNK, HEAD_DIM)) for s in _HEAD_LANES])
            t = _hgrn_gates(q_ref[rs, :], f_ref[rs, :], lb_all, tri_lo)
            v, gate, o, dog, nw_all = i_ref[rs, :], g_ref[rs, :], oraw_ref[rs, :], dog_ref[rs, :], nw_ref[...]
            rs_o = lax.rsqrt(per_head_mean(o * o) + RMS_EPS)
            xhat = o * rs_o
            sgg = _sigmoid(gate)
            d_on = dog * (gate * sgg)
            dz_ref[rs, 1536:2048] = (dog * (xhat * nw_all) * (sgg * (1.0 + gate * (1.0 - sgg)))).astype(BF16)
            dxh = d_on * nw_all
            do = rs_o * (dxh - xhat * per_head_mean(dxh * xhat))
            dsmall_ref[:, 512:1024] += _rowsum(d_on * xhat)
            st = [sp_ref[cc, h] for h in range(HEADS)]
            dst = [dst_ref[h] for h in range(HEADS)]
            a = [jnp.where(tri_lo, _bdot(t["qa"][:, s], t["ka"][:, s], NT), 0.0) for s in _HEAD_LANES]
            da = [jnp.where(tri_lo, _bdot(do[:, s], v[:, s], NT), 0.0) for s in _HEAD_LANES]
            dqb = cat([_bdot(do[:, s], st[h]) for h, s in heads])
            dkl = cat([_bdot(v[:, s], dst[h]) for h, s in heads])
            dv_ = cat([_bdot(t["kl"][:, s], dst[h], NT) + _bdot(a[h], do[:, s], TN) for h, s in heads])
            dqa = cat([_bdot(da[h], t["ka"][:, s]) for h, s in heads])
            dka = cat([_bdot(da[h], t["qa"][:, s], TN) for h, s in heads])
            ddecay = cat([_rowsum(dst[h] * st[h]) for h in range(HEADS)])
            for h, s in heads:
                dst_ref[h] = dst[h] * t["decay"][:, s] + _bdot(do[:, s], t["qb"][:, s], TN)
            pa, pk, pb, pl_ = dqa * t["qa"], dka * t["ka"], dqb * t["qb"], dkl * t["kl"]
            db = pa - pk + pb - pl_
            db = db + jnp.where(rowid == CHUNK // 2 - 1, _rowsum(pk - pa), 0.0)
            db = db + jnp.where(rowid == CHUNK - 1, _rowsum(pl_) + ddecay * t["decay"], 0.0)
            dlogf = _hdot(tri_up.astype(F32), db)
            dforget = dlogf / t["forget"] - (dka * t["e2"] + dkl * t["e3"])
            sg = t["sg"]
            dz_ref[rs, 0:512] = (dqa * t["e1"] + dqb * t["e4"]).astype(BF16)
            dz_ref[rs, 512:1024] = (dforget * (1.0 - lb_all) * sg * (1.0 - sg)).astype(BF16)
            dz_ref[rs, 1024:1536] = dv_.astype(BF16)
            dsmall_ref[:, 0:512] += _rowsum(dforget * (1.0 - sg))
            return carry

        lax.fori_loop(0, G, chunk, 0, unroll=4)

    col = lambda j: pl.BlockSpec((rows, 512), lambda r, j=j: (ng - 1 - r, j))
    return _call(
        body, "hgrn_bwd", (dmixcat, z, z, z, z, oraw, sprev, lbraw, nw), grid=(ng,),
        in_specs=[col(0), col(0), col(1), col(2), col(3), col(0),
                  pl.BlockSpec((G, HEADS, HEAD_DIM, HEAD_DIM), lambda r: (ng - 1 - r, 0, 0, 0)),
                  _full((2, 512)), _full((1, 512))],
        out_specs=[pl.BlockSpec((rows, 2048), lambda r: (ng - 1 - r, 0)), _full((1, 1024))],
        out_shape=[jax.ShapeDtypeStruct((T, 2048), BF16), jax.ShapeDtypeStruct((1, 1024), F32)],
        scratch_shapes=[pltpu.VMEM((HEADS, HEAD_DIM, HEAD_DIM), F32)], exchange=exchange)


def _diag_mask(t):
    r = lax.broadcasted_iota(jnp.int32, (t, t), 0)
    c = lax.broadcasted_iota(jnp.int32, (t, t), 1)
    return r >= c


def _attn_fwd(q, k, v, exchange=None):
    _, T, _ = q.shape
    t = min(ATT_TILE, T)

    def body(q_ref, k_ref, v_ref, o_ref, lse_ref):
        i = pl.program_id(1)
        qb = q_ref[...]

        rows = lambda j: pl.ds(pl.multiple_of(j * t, t), t)

        def logits(j, masked):
            s = _dot(qb, k_ref[rows(j), :], NT)
            return jnp.where(_diag_mask(t), s, NEG_BIG) if masked else s

        def absorb(s, j, carry):
            m, l, acc = carry
            mn = jnp.maximum(m, jnp.max(s, axis=-1, keepdims=True))
            p = jnp.exp2(s - mn)
            al = jnp.exp2(m - mn)
            return mn, al * l + jnp.sum(p, axis=-1, keepdims=True), al * acc + _dot(p.astype(BF16), v_ref[rows(j), :])

        def pair(j0, carry, last_masked):
            s0, s1 = logits(j0, False), logits(j0 + 1, last_masked)
            return absorb(s1, j0 + 1, absorb(s0, j0, carry))

        init = (jnp.full((t, 1), NEG_BIG, F32), jnp.zeros((t, 1), F32), jnp.zeros((t, HEAD_DIM), F32))
        carry = lax.fori_loop(0, i // 2, lambda jj, c: pair(2 * jj, c, False), init)
        m, l, acc = lax.cond(i % 2 == 1, lambda c: pair(i - 1, c, True),
                             lambda c: absorb(logits(i, True), i, c), carry)
        o_ref[...] = acc / l
        lse_ref[...] = jnp.broadcast_to(m + jnp.log2(l), (t, HEAD_DIM))

    return _call(
        body, "attn_fwd", (q, k, v), grid=(HEADS, T // t),
        in_specs=[pl.BlockSpec((None, t, QK_PAD), lambda h, i: (h, i, 0)),
                  pl.BlockSpec((None, T, QK_PAD), lambda h, i: (h, 0, 0)),
                  pl.BlockSpec((None, T, HEAD_DIM), lambda h, i: (h, 0, 0))],
        out_specs=[pl.BlockSpec((t, HEAD_DIM), lambda h, i: (i, h)),
                   pl.BlockSpec((None, t, HEAD_DIM), lambda h, i: (h, i, 0))],
        out_shape=[jax.ShapeDtypeStruct((T, HEADS * HEAD_DIM), F32), jax.ShapeDtypeStruct((HEADS, T, HEAD_DIM), F32)],
        exchange=exchange)


def _attn_bwd(q, k, v, dmixcat, o, lse, exchange=None):
    _, T, _ = q.shape
    t = min(ATT_TILE, T)
    nq = T // t

    def body(q_ref, k_ref, v_ref, do_ref, o_ref, lse_ref, dq_ref, dk_ref, dv_ref, delta_ref, dq_acc):
        j = pl.program_id(1)

        @pl.when(j == 0)
        def _():
            dq_acc[...] = jnp.zeros_like(dq_acc)

            def fill(i, carry):
                rs = pl.ds(pl.multiple_of(i * t, t), t)
                delta_ref[rs, :] = jnp.broadcast_to(
                    jnp.sum(do_ref[rs, :] * o_ref[rs, :], axis=-1, keepdims=True), (t, HEAD_DIM))
                return carry

            lax.fori_loop(0, nq, fill, 0)

        kb, vb = k_ref[...], v_ref[...]

        def steps(blocks, carry):
            dk, dv = carry
            rs = [pl.ds(pl.multiple_of(i * t, t), t) for i, _ in blocks]
            qb = [q_ref[r, :] for r in rs]
            dob = [do_ref[r, :].astype(BF16) for r in rs]
            s = [_dot(b, kb, NT) for b in qb]
            dp = [_dot(b, vb, NT) for b in dob]
            for n, (_, masked) in enumerate(blocks):
                p = jnp.exp2(s[n] - lse_ref[rs[n], 0:1])
                if masked:
                    p = jnp.where(_diag_mask(t), p, 0.0)
                ds = (p * (dp[n] - delta_ref[rs[n], 0:1])).astype(BF16)
                dq_acc[rs[n], :] += _dot(ds, kb)
                dk = dk + _dot(ds, qb[n], TN)
                dv = dv + _dot(p.astype(BF16), dob[n], TN)
            return dk, dv

        zero = (jnp.zeros((t, QK_PAD), F32), jnp.zeros((t, HEAD_DIM), F32))
        rest = nq - 1 - j
        carry = lax.cond(rest % 2 == 1, lambda c: steps([(j, True), (j + 1, False)], c),
                         lambda c: steps([(j, True)], c), zero)
        first = j + 1 + rest % 2
        dk, dv = lax.fori_loop(0, rest // 2, lambda n, c: steps([(first + 2 * n, False), (first + 2 * n + 1, False)], c),
                               carry)
        dk_ref[...] = (dk * LN2).astype(BF16)
        dv_ref[...] = dv.astype(BF16)

        @pl.when(j == nq - 1)
        def _():
            dq_ref[...] = dq_acc[...].astype(BF16)

    return _call(
        body, "attn_bwd", (q, k, v, dmixcat, o, lse), grid=(HEADS, nq),
        in_specs=[pl.BlockSpec((None, T, QK_PAD), lambda h, j: (h, 0, 0)),
                  pl.BlockSpec((None, t, QK_PAD), lambda h, j: (h, j, 0)),
                  pl.BlockSpec((None, t, HEAD_DIM), lambda h, j: (h, j, 0)),
                  pl.BlockSpec((T, HEAD_DIM), lambda h, j: (0, HEADS + h)),
                  pl.BlockSpec((T, HEAD_DIM), lambda h, j: (0, h)),
                  pl.BlockSpec((None, T, HEAD_DIM), lambda h, j: (h, 0, 0))],
        out_specs=[pl.BlockSpec((None, T, QK_PAD), lambda h, j: (h, 0, 0)),
                   pl.BlockSpec((None, t, QK_PAD), lambda h, j: (h, j, 0)),
                   pl.BlockSpec((None, t, HEAD_DIM), lambda h, j: (h, j, 0))],
        out_shape=[jax.ShapeDtypeStruct((HEADS, T, QK_PAD), BF16), jax.ShapeDtypeStruct((HEADS, T, QK_PAD), BF16),
                   jax.ShapeDtypeStruct((HEADS, T, HEAD_DIM), BF16)],
        scratch_shapes=[pltpu.VMEM((T, HEAD_DIM), F32), pltpu.VMEM((T, QK_PAD), F32)], exchange=exchange)


def _ln_fwd(r):
    mu = _lanemean(r)
    xc = r - mu
    rstd = lax.rsqrt(_lanemean(xc * xc) + LN_EPS)
    return xc * rstd, rstd


def _ln_bwd(dxh, xhat, rstd):
    return rstd * (dxh - _lanemean(dxh) - xhat * _lanemean(dxh * xhat))


def _mix_ln1(o_hg, o_mla, w_out, x, g_a, ln1_g, ln1_b, sc_m, sh_m, exchange=None):
    T = x.shape[0]
    tm = min(ROW_TILE, T)
    half = o_hg.shape[1]

    def body(hg_ref, mla_ref, w_ref, x_ref, ga_ref, g_ref, b_ref, sc_ref, sh_ref, mix_ref, xhat_ref, rstd_ref, u2_ref):
        mix = _dot(hg_ref[...], w_ref[0:half, :]) + _bdot(mla_ref[...], w_ref[half:, :])
        mix_ref[...] = mix
        xhat, rstd = _ln_fwd(ALPHA * x_ref[...] + (1.0 + ga_ref[...]) * mix)
        xhat_ref[...] = xhat
        rstd_ref[...] = jnp.broadcast_to(rstd, (tm, 128))
        u2_ref[...] = _modulate(xhat * g_ref[...] + b_ref[...], sc_ref[...], sh_ref[...]).astype(BF16)

    row = pl.BlockSpec((tm, D_MODEL), lambda i: (i, 0))
    vec = _full((1, D_MODEL))
    halfrow = pl.BlockSpec((tm, half), lambda i: (i, 0))
    return _call(
        body, "mix_ln1", (o_hg, o_mla, w_out, x, g_a, ln1_g, ln1_b, sc_m, sh_m), grid=(T // tm,),
        in_specs=[halfrow, halfrow, _full(w_out.shape), row, vec, vec, vec, vec, vec],
        out_specs=[row, row, pl.BlockSpec((tm, 128), lambda i: (i, 0)), row],
        out_shape=[jax.ShapeDtypeStruct((T, D_MODEL), F32), jax.ShapeDtypeStruct((T, D_MODEL), F32),
                   jax.ShapeDtypeStruct((T, 128), F32), jax.ShapeDtypeStruct((T, D_MODEL), BF16)],
        exchange=exchange)


def _mlp_fwd(u2, w1, w2, xhat1, ln1_g, ln1_b, g_m, ln2_g, ln2_b, target):
    T = u2.shape[0]
    half, tf = w1[0].shape[1:]
    nf = N_DEV // MLP_SLABS
    tm = min(ROW_TILE, T)

    def body(u2_ref, w1a_ref, w1b_ref, w2_ref, xhat_ref, g1_ref, b1_ref, gm_ref, g2_ref, b2_ref, tgt_ref,
             r_ref, dr2_ref, dh_ref, small_ref, acc_ref):
        i, f = pl.program_id(0), pl.program_id(1)
        dm = D_MODEL

        @pl.when((i == 0) & (f == 0))
        def _():
            small_ref[...] = jnp.zeros_like(small_ref)

        @pl.when(f == 0)
        def _():
            acc_ref[...] = jnp.zeros_like(acc_ref)

        u2t = u2_ref[...]
        part = None
        for s in range(MLP_SLABS):
            r = jnp.maximum(_dot(u2t[:, :half], w1a_ref[s]) + _dot(u2t[:, half:], w1b_ref[s]), 0.0)
            r_ref[:, s * tf:(s + 1) * tf] = r.astype(BF16)
            d = _bdot(r * r, w2_ref[s])
            part = d if part is None else part + d
        acc_ref[...] += part

        @pl.when(f == nf - 1)
        def _():
            h = acc_ref[...]
            x1 = xhat_ref[...] * g1_ref[...] + b1_ref[...]
            xhat2, rstd2 = _ln_fwd(ALPHA * x1 + (1.0 + gm_ref[...]) * h)
            err = xhat2 * g2_ref[...] + b2_ref[...] - tgt_ref[...]
            small_ref[:, 3 * dm:] += jnp.sum(0.5 * _lanemean(err * err), axis=0, keepdims=True)
            dy = err * (1.0 / D_MODEL)
            small_ref[:, dm:2 * dm] += _rowsum(dy * xhat2)
            small_ref[:, 2 * dm:3 * dm] += _rowsum(dy)
            dr2 = _ln_bwd(dy * g2_ref[...], xhat2, rstd2)
            dr2_ref[...] = dr2
            small_ref[:, 0:dm] += _rowsum(dr2 * h)
            dh_ref[...] = ((1.0 + gm_ref[...]) * dr2).astype(BF16)

    row = pl.BlockSpec((tm, D_MODEL), lambda i, f: (i, 0))
    vec = _full((1, D_MODEL))
    return pl.pallas_call(
        body, name="mlp_fwd", grid=(T // tm, nf),
        in_specs=[row, pl.BlockSpec((MLP_SLABS, half, tf), lambda i, f: (f, 0, 0)),
                  pl.BlockSpec((MLP_SLABS, half, tf), lambda i, f: (f, 0, 0)),
                  pl.BlockSpec((MLP_SLABS, tf, D_MODEL), lambda i, f: (f, 0, 0)),
                  row, vec, vec, vec, vec, vec, row],
        out_specs=[pl.BlockSpec((tm, MLP_SLABS * tf), lambda i, f: (i, f)), row, row, _full((1, 3 * D_MODEL + 128))],
        out_shape=[jax.ShapeDtypeStruct((T, N_DEV * tf), BF16), jax.ShapeDtypeStruct((T, D_MODEL), F32),
                   jax.ShapeDtypeStruct((T, D_MODEL), BF16), jax.ShapeDtypeStruct((1, 3 * D_MODEL + 128), F32)],
        scratch_shapes=[pltpu.VMEM((tm, D_MODEL), F32)],
        compiler_params=_params(),
    )(u2, w1[0], w1[1], w2, xhat1, ln1_g, ln1_b, g_m, ln2_g, ln2_b, target)


def _mlp_bwd(dh, w1, w2, r, dr2, xhat1, rstd1, mix, ln1_g, ln1_b, sc_m, g_a):
    T = dh.shape[0]
    half, tf = w1[0].shape[1:]
    nf = N_DEV // MLP_SLABS
    tm = min(ROW_TILE, T)

    def body(dh_ref, w1a_ref, w1b_ref, w2_ref, r_ref, dr2_ref, xhat_ref, rstd_ref, mix_ref, g1_ref, b1_ref, sc_ref, ga_ref,
             dhpre_ref, dr1_ref, dmix_ref, small_ref, acc_ref):
        i, f = pl.program_id(0), pl.program_id(1)
        dm = D_MODEL

        @pl.when((i == 0) & (f == 0))
        def _():
            small_ref[...] = jnp.zeros_like(small_ref)

        @pl.when(f == 0)
        def _():
            acc_ref[...] = jnp.zeros_like(acc_ref)

        dht = dh_ref[...]
        part = None
        for s in range(MLP_SLABS):
            cols = slice(s * tf, (s + 1) * tf)
            dhpre = (_dot(dht, w2_ref[s], NT) * (2.0 * r_ref[:, cols].astype(F32))).astype(BF16)
            dhpre_ref[:, cols] = dhpre
            d = jnp.concatenate([_dot(dhpre, w1a_ref[s], NT), _dot(dhpre, w1b_ref[s], NT)], axis=1)
            part = d if part is None else part + d
        acc_ref[...] += part

        @pl.when(f == nf - 1)
        def _():
            du2 = acc_ref[...]
            xhat = xhat_ref[...]
            x1 = xhat * g1_ref[...] + b1_ref[...]
            dx1 = ALPHA * dr2_ref[...] + du2 * (1.0 + sc_ref[...])
            small_ref[:, 2 * dm:3 * dm] += _rowsum(du2 * x1)
            small_ref[:, dm:2 * dm] += _rowsum(du2)
            small_ref[:, 3 * dm:4 * dm] += _rowsum(dx1 * xhat)
            small_ref[:, 4 * dm:5 * dm] += _rowsum(dx1)
            dr1 = _ln_bwd(dx1 * g1_ref[...], xhat, rstd_ref[:, 0:1])
            dr1_ref[...] = dr1
            small_ref[:, 0:dm] += _rowsum(dr1 * mix_ref[...])
            dmix_ref[...] = ((1.0 + ga_ref[...]) * dr1).astype(BF16)

    row = pl.BlockSpec((tm, D_MODEL), lambda i, f: (i, 0))
    vec = _full((1, D_MODEL))
    return pl.pallas_call(
        body, name="mlp_bwd", grid=(T // tm, nf),
        in_specs=[row, pl.BlockSpec((MLP_SLABS, half, tf), lambda i, f: (f, 0, 0)),
                  pl.BlockSpec((MLP_SLABS, half, tf), lambda i, f: (f, 0, 0)),
                  pl.BlockSpec((MLP_SLABS, tf, D_MODEL), lambda i, f: (f, 0, 0)),
                  pl.BlockSpec((tm, MLP_SLABS * tf), lambda i, f: (i, f)), row, row,
                  pl.BlockSpec((tm, 128), lambda i, f: (i, 0)), row, vec, vec, vec, vec],
        out_specs=[pl.BlockSpec((tm, MLP_SLABS * tf), lambda i, f: (i, f)), row, row, _full((1, 5 * D_MODEL))],
        out_shape=[jax.ShapeDtypeStruct((T, N_DEV * tf), BF16), jax.ShapeDtypeStruct((T, D_MODEL), F32),
                   jax.ShapeDtypeStruct((T, D_MODEL), BF16), jax.ShapeDtypeStruct((1, 5 * D_MODEL), F32)],
        scratch_shapes=[pltpu.VMEM((tm, D_MODEL), F32)],
        compiler_params=_params(),
    )(dh, w1[0], w1[1], w2, r, dr2, xhat1, rstd1, mix, ln1_g, ln1_b, sc_m, g_a)


def _input_bwd(dz_h, dz_m, w_in_ext, x, dr1, sc_a, exchange=None):
    T = x.shape[0]
    tm = min(ROW_TILE, T)

    def body(dzh_ref, dzm_ref, w_ref, x_ref, dr1_ref, sc_ref, gx_ref, small_ref):
        @pl.when(pl.program_id(0) == 0)
        def _():
            small_ref[...] = jnp.zeros_like(small_ref)

        du = _bdot(dzh_ref[...], w_ref[0:2048, :]) + _bdot(dzm_ref[...], w_ref[2048:3072, :])
        gx_ref[...] = ALPHA * dr1_ref[...] + du * (1.0 + sc_ref[...])
        small_ref[:, D_MODEL:] += _rowsum(du * x_ref[...])
        small_ref[:, 0:D_MODEL] += _rowsum(du)

    row = pl.BlockSpec((tm, D_MODEL), lambda i: (i, 0))
    vec = _full((1, D_MODEL))
    return _call(
        body, "input_bwd", (dz_h, dz_m, w_in_ext, x, dr1, sc_a), grid=(T // tm,),
        in_specs=[pl.BlockSpec((tm, 2048), lambda i: (i, 0)), row, _full(w_in_ext.shape), row, row, vec],
        out_specs=[row, _full((1, 2 * D_MODEL))],
        out_shape=[jax.ShapeDtypeStruct((T, D_MODEL), F32), jax.ShapeDtypeStruct((1, 2 * D_MODEL), F32)],
        exchange=exchange)


def _adam_math(w, g, m, v):
    m = ADAM_B1 * m + (1.0 - ADAM_B1) * g
    v = ADAM_B2 * v + (1.0 - ADAM_B2) * (g * g)
    m_hat = m / (1.0 - ADAM_B1 ** ADAM_STEP)
    v_hat = v / (1.0 - ADAM_B2 ** ADAM_STEP)
    return -ADAM_LR * (m_hat / (jnp.sqrt(v_hat) + ADAM_EPS) + ADAM_WD * w), m, v


def _adam(g_slabs, w, m, v, name, g_fn=None, g_extra=()):
    R, C = w.shape
    tr = 256 if R % 256 == 0 else R
    ns = 0 if g_slabs is None else g_slabs.shape[0]
    slab_rows = tr if g_slabs is None or g_slabs.shape[1] == R else g_slabs.shape[1]
    assert slab_rows == tr or tr == R
    ne = len(g_extra)

    def body(*refs):
        e_refs = refs[:ne]
        refs = refs[ne:]
        if ns:
            gs_ref, refs = refs[0], refs[1:]
        w_ref, m_ref, v_ref, g_ref, d_ref, nm_ref, nv_ref = refs
        if g_fn is not None:
            g = g_fn(*e_refs)
        else:
            g = gs_ref[0].astype(F32)
            for s in range(1, ns):
                g = g + gs_ref[s].astype(F32)
            g = g[:tr]
        d, nm, nv = _adam_math(w_ref[...], g, m_ref[...], v_ref[...])
        g_ref[...] = g
        d_ref[...] = d
        nm_ref[...] = nm
        nv_ref[...] = nv

    blk = pl.BlockSpec((tr, C), lambda i: (i, 0))
    in_specs = [pl.BlockSpec((tr, e.shape[1]), lambda i: (i, 0)) if e.shape[0] == R else _full(e.shape) for e in g_extra]
    args = list(g_extra)
    if ns:
        in_specs.append(pl.BlockSpec((ns, slab_rows, C), lambda i: (0, i, 0)))
        args.append(g_slabs)
    return pl.pallas_call(
        body, name=name, grid=(R // tr,), in_specs=in_specs + [blk] * 3, out_specs=[blk] * 4,
        out_shape=[jax.ShapeDtypeStruct((R, C), F32)] * 4, compiler_params=_params(),
    )(*args, w, m, v)


def _adam_small(small_all, params):
    n = len(params)

    def body(*refs):
        s_ref, refs = refs[0], refs[1:]
        wmv, loss_ref, outs = refs[:3 * n], refs[3 * n], refs[3 * n + 1:]
        tot = s_ref[0]
        for i in range(1, N_DEV):
            tot = tot + s_ref[i]
        loss_ref[...] = tot[:, SMALL_W - 128:]
        for j, (w, _, _, off) in enumerate(params):
            w_ref, m_ref, v_ref = wmv[3 * j:3 * j + 3]
            g_ref, d_ref, nm_ref, nv_ref = outs[4 * j:4 * j + 4]
            if w.shape[0] == 2:
                lb = _lower_bound(w_ref)
                g0 = tot[:, off:off + w.shape[1]] * lb * (1.0 - lb)
                rows = [(slice(0, 1), g0), (slice(1, 2), -g0)]
            else:
                rows = [(slice(0, 1), tot[:, off:off + w.shape[1]])]
            for rs, g in rows:
                d, nm, nv = _adam_math(w_ref[rs, :], g, m_ref[rs, :], v_ref[rs, :])
                g_ref[rs, :], d_ref[rs, :], nm_ref[rs, :], nv_ref[rs, :] = g, d, nm, nv

    out_shape = [jax.ShapeDtypeStruct((1, 128), F32)]
    for w, _, _, _ in params:
        out_shape += [jax.ShapeDtypeStruct(w.shape, F32)] * 4
    res = pl.pallas_call(body, name="adam_small", out_shape=out_shape, compiler_params=_params())(
        small_all, *[a for w, m, v, _ in params for a in (w, m, v)])
    return res[0], [tuple(res[1 + 4 * j:5 + 4 * j]) for j in range(n)]


def _cols_from_slabs(g):
    s, r, c = g.shape
    return jnp.transpose(g, (1, 0, 2)).reshape(r, s * c)


def _slabs_from_cols(w):
    r, c = w.shape
    return jnp.transpose(w.reshape(r, N_DEV, c // N_DEV), (1, 0, 2))


def _rot_half_rows(wt):
    return jnp.concatenate([-wt[32:], wt[:32]], axis=0)


def _unrot_half_rows(dwt_rot):
    return jnp.concatenate([dwt_rot[32:], -dwt_rot[:32]], axis=0)


def _ext_in_t(g):
    n, rows, k_in = g.shape
    keep = n * rows - ROPE_DIM

    def body(g_ref, o_ref, stage_ref):
        stage_ref[keep:, :] = jnp.zeros((o_ref.shape[0] - keep, k_in), F32)
        for i in range(n - 1):
            stage_ref[rows * i:rows * (i + 1), :] = g_ref[i].astype(F32)
        last = g_ref[n - 1].astype(F32)
        stage_ref[rows * (n - 1):keep, :] = last[:rows - ROPE_DIM]
        wk = last[rows - ROPE_DIM:]
        stage_ref[keep + 128:keep + 192, :] = wk
        stage_ref[keep + 384:keep + 416, :] = -wk[32:]
        stage_ref[keep + 416:keep + 448, :] = wk[:32]
        o_ref[...] = stage_ref[...].astype(BF16)

    return pl.pallas_call(body, name="ext_w_in", out_shape=jax.ShapeDtypeStruct((keep + 512, k_in), BF16),
                          scratch_shapes=[pltpu.VMEM((keep + 512, k_in), F32)], compiler_params=_params())(g)


def _ext_q_t(wt):
    r = wt.shape[1]
    z64, z128 = jnp.zeros((64, r), BF16), jnp.zeros((128, r), BF16)
    per = HEAD_DIM + ROPE_DIM
    main = [jnp.concatenate([wt[per * h:per * (h + 1)], z64], axis=0) for h in range(HEADS)]
    rot = [jnp.concatenate([z128, _rot_half_rows(wt[per * h + HEAD_DIM:per * (h + 1)]), z64], axis=0)
           for h in range(HEADS)]
    return jnp.concatenate(main + rot, axis=0)


def _ext_kv(w_kv_up):
    r = w_kv_up.shape[0]
    z128 = jnp.zeros((r, 128), BF16)
    wkv = w_kv_up.reshape(r, HEADS, 2 * HEAD_DIM)
    kpad = [jnp.concatenate([wkv[:, h, :HEAD_DIM], z128], axis=1) for h in range(HEADS)]
    vals = [wkv[:, h, HEAD_DIM:] for h in range(HEADS)]
    return jnp.concatenate(kpad + vals, axis=1)


def _grad_in_from_ext_t(dwt_h, dwt_m):
    dwk = dwt_m[512 + 128:512 + 192] + _unrot_half_rows(dwt_m[768 + 128:768 + 192])
    return jnp.concatenate([dwt_h, dwt_m[:512], dwk], axis=0)


def _grad_q_from_ext_t(dwq_ext_t):
    rows = []
    for h in range(HEADS):
        main, rot = dwq_ext_t[256 * h:256 * h + 256], dwq_ext_t[1024 + 256 * h:1280 + 256 * h]
        rows += [main[:128], main[128:192] + _unrot_half_rows(rot[128:192])]
    return jnp.concatenate(rows, axis=0)


def _grad_kv_from_ext(dwkv_ext):
    kvcols = []
    for h in range(HEADS):
        kvcols += [dwkv_ext[:, 256 * h:256 * h + 128], dwkv_ext[:, 1024 + 128 * h:1152 + 128 * h]]
    return jnp.concatenate(kvcols, axis=1)


SMALL_W = 6144 + 512 + 512 + 256 + 256 + 4 * 1024 + 128


def kernel(x, c, positions, w_ada, b_ada, w_in, hg_lower_bounds, hg_norm_w, mla_q_norm_w, w_q_up, mla_kv_norm_w, w_kv_up, w_out, ln1_g, ln1_b, w_mlp_in, w_mlp_out, ln2_g, ln2_b, loss_target, m_w_ada, m_b_ada, m_w_in, m_hg_lower_bounds, m_hg_norm_w, m_mla_q_norm_w, m_w_q_up, m_mla_kv_norm_w, m_w_kv_up, m_w_out, m_ln1_g, m_ln1_b, m_w_mlp_in, m_w_mlp_out, m_ln2_g, m_ln2_b, v_w_ada, v_b_ada, v_w_in, v_hg_lower_bounds, v_hg_norm_w, v_mla_q_norm_w, v_w_q_up, v_mla_kv_norm_w, v_w_kv_up, v_w_out, v_ln1_g, v_ln1_b, v_w_mlp_in, v_w_mlp_out, v_ln2_g, v_ln2_b):
    T = x.shape[1]
    me = 4 * lax.axis_index("x") + 2 * lax.axis_index("y") + lax.axis_index("c")
    xs, tgt = x[0], loss_target[0]
    transposed = ("w_in", "w_q_up")
    as_used = lambda n, a: a[0].T if n in transposed else a[0]
    big = {n: as_used(n, a) for n, a in dict(w_in=w_in, w_q_up=w_q_up, w_kv_up=w_kv_up, w_out=w_out,
                                              w_mlp_in=w_mlp_in, w_mlp_out=w_mlp_out).items()}
    names = list(big)

    bf = {n: big[n].astype(BF16) for n in names}
    g_in, g_c = _gather_two_level([bf["w_in"], c], name="gather_w_in")
    c_all = g_c.reshape(N_DEV, D_MODEL)

    ada_cols = w_ada.shape[2]
    mod_part, cond = _mod_part(c_all, w_ada[0], lax.dynamic_slice(b_ada, (0, me * ada_cols), (1, ada_cols)))
    (mod_all,) = _exchange([mod_part], scatter=False, name="gather_mod")
    mod_row = lax.dynamic_slice(mod_all, (0, me, 0), (N_DEV, 1, ada_cols)).reshape(1, N_DEV * ada_cols)
    sh_a, sc_a, g_a, sh_m, sc_m, g_m = [mod_row[:, D_MODEL * i:D_MODEL * (i + 1)] for i in range(6)]

    w_in_ext = _ext_in_t(g_in)
    half = D_MODEL // 2
    z, (w1_top,) = _matmul(xs, w_in_ext, "NT", "in_proj", a_fn=_modulate, extras=(sc_a, sh_a), tn=3072,
                           exchange=_StagedGather(bf["w_mlp_in"][:half]))
    (o_raw, o_gated, s_prev), (g_q, g_kv, g_out) = _hgrn_fwd(
        z, hg_lower_bounds, hg_norm_w, exchange=_Exchange([bf["w_q_up"], bf["w_kv_up"], bf["w_out"]], False))
    wq_ext = _ext_q_t(g_q.reshape(N_DEV * g_q.shape[1], g_q.shape[2]))
    wkv_ext = _ext_kv(_cols_from_slabs(g_kv))
    w_out_full = g_out.reshape(D_MODEL, D_MODEL)
    inv_freq = 1.0 / (ROPE_THETA ** (jnp.arange(0, ROPE_DIM, 2, dtype=F32) / ROPE_DIM))
    zeros = lambda n: jnp.zeros((n,), F32)
    invf = jnp.concatenate([zeros(128), inv_freq, inv_freq, zeros(64)]).reshape(1, QK_PAD)
    m_rot = jnp.concatenate([zeros(128), jnp.ones((64,), F32), zeros(64)]).reshape(1, QK_PAD)
    q, k, v, c1, s1, cqn, ckvn = _mla_pre(z, positions.reshape(T, 1), invf, m_rot, wq_ext, wkv_ext,
                                          mla_q_norm_w, mla_kv_norm_w)[0]
    (o_mla, lse), (w1_bot, w2) = _attn_fwd(q, k, v, exchange=[_StagedGather(bf["w_mlp_in"][half:], 0.85),
                                                              _StagedGather(bf["w_mlp_out"], 0.85)])
    w1 = (w1_top, w1_bot)
    mix, xhat1, rstd1, u2 = _mix_ln1(o_gated, o_mla, w_out_full, xs, g_a, ln1_g, ln1_b, sc_m, sh_m)[0]
    r, dr2, dh, small_mlp_fwd = _mlp_fwd(u2, w1, w2, xhat1, ln1_g, ln1_b, g_m, ln2_g, ln2_b, tgt)

    dhpre, dr1, dmix, small_mlp_bwd = _mlp_bwd(dh, w1, w2, r, dr2, xhat1, rstd1, mix, ln1_g, ln1_b, sc_m, g_a)
    received = {}
    dw2 = _matmul(r, dh, "TN", "wgrad_mlp_out", out_dtype=BF16, a_fn=_square, tm=1024, tk=2048)
    dw1 = _matmul(u2, dhpre, "TN", "wgrad_mlp_in", out_dtype=BF16, tm=1024, tk=2048, out_slabs=N_DEV)
    dmixcat = _matmul(dmix, w_out_full, "NT", "dgrad_out", tm=1024)
    dw_out = jnp.concatenate([_matmul(o_gated, dmix, "TN", "wgrad_out_hg", out_dtype=BF16, tk=2048),
                              _matmul(o_mla, dmix, "TN", "wgrad_out_mla", out_dtype=BF16, tk=2048)], axis=0)
    (dz_h, small_hgrn), (received["w_out"],) = _hgrn_bwd(
        dmixcat, z, o_raw, s_prev, hg_lower_bounds, hg_norm_w,
        exchange=_Exchange([dw_out.reshape(N_DEV, D_MODEL // N_DEV, D_MODEL)], True))
    (dq, dk, dv), (received["w_mlp_in"], received["w_mlp_out"]) = _attn_bwd(
        q, k, v, dmixcat, o_mla, lse,
        exchange=_Exchange([dw1, dw2.reshape(N_DEV, dw2.shape[0] // N_DEV, D_MODEL)], True))
    dz_m, dq_ext, dkv_ext, small_mla = _mla_bwd(dq, dk, dv, z, c1, s1, wq_ext, wkv_ext, mla_q_norm_w, mla_kv_norm_w)
    dwq_t = _grad_q_from_ext_t(_matmul(dq_ext, cqn, "TN", "wgrad_q_up", tm=1024, tk=2048))
    dwkv = _grad_kv_from_ext(_matmul(ckvn, dkv_ext, "TN", "wgrad_kv_up", tn=1536, tk=2048))
    qkv_slabs = [dwq_t.reshape((N_DEV, dwq_t.shape[0] // N_DEV, dwq_t.shape[1])).astype(BF16),
                 _slabs_from_cols(dwkv).astype(BF16)]
    dwt_h, (received["w_q_up"], received["w_kv_up"]) = _matmul(
        dz_h, xs, "TN", "wgrad_in_h", b_fn=_modulate, extras=(sc_a, sh_a), tm=1024, tk=2048,
        exchange=_Exchange(qkv_slabs, True))
    dwt_m = _matmul(dz_m, xs, "TN", "wgrad_in_m", b_fn=_modulate, extras=(sc_a, sh_a), tm=1024, tk=2048)
    dw_in_t = _grad_in_from_ext_t(dwt_h, dwt_m)
    in_slabs = dw_in_t.reshape((N_DEV, dw_in_t.shape[0] // N_DEV, dw_in_t.shape[1]))
    in_slabs = jnp.pad(in_slabs, ((0, 0), (0, -in_slabs.shape[1] % 16), (0, 0))).astype(BF16)
    (grad_x, small_in), (received["w_in"],) = _input_bwd(
        dz_h, dz_m, w_in_ext, xs, dr1, sc_a, exchange=_StagedScatter(in_slabs))

    small = jnp.concatenate([small_in, small_mlp_bwd[:, :3 * D_MODEL], small_mlp_fwd[:, :D_MODEL], small_hgrn,
                             small_mla, small_mlp_bwd[:, 3 * D_MODEL:], small_mlp_fwd[:, D_MODEL:]], axis=1)
    assert small.shape == (1, SMALL_W)
    (small_all,) = _exchange([small], scatter=False, name="gather_small")

    moments = dict(w_in=(m_w_in, v_w_in), w_q_up=(m_w_q_up, v_w_q_up), w_kv_up=(m_w_kv_up, v_w_kv_up),
                   w_out=(m_w_out, v_w_out), w_mlp_in=(m_w_mlp_in, v_w_mlp_in), w_mlp_out=(m_w_mlp_out, v_w_mlp_out))
    res = {}
    for n in names:
        res[n] = _adam(received[n], big[n], as_used(n, moments[n][0]), as_used(n, moments[n][1]), name="adam_" + n)
    dmod_cols = lax.dynamic_slice(small_all.reshape(N_DEV, SMALL_W), (0, me * ada_cols), (N_DEV, ada_cols))
    cond_t = cond.T

    def ada_grad(ct_ref, dm_ref):
        g = ct_ref[:, 0:1] * dm_ref[0:1, :]
        for b in range(1, N_DEV):
            g = g + ct_ref[:, b:b + 1] * dm_ref[b:b + 1, :]
        return g

    res["w_ada"] = _adam(None, w_ada[0], m_w_ada[0], v_w_ada[0], name="adam_w_ada", g_fn=ada_grad,
                         g_extra=(cond_t, dmod_cols))

    small_params = [("b_ada", b_ada, m_b_ada, v_b_ada, 0),
                    ("hg_lower_bounds", hg_lower_bounds, m_hg_lower_bounds, v_hg_lower_bounds, 6144),
                    ("hg_norm_w", hg_norm_w, m_hg_norm_w, v_hg_norm_w, 6656),
                    ("mla_q_norm_w", mla_q_norm_w, m_mla_q_norm_w, v_mla_q_norm_w, 7168),
                    ("mla_kv_norm_w", mla_kv_norm_w, m_mla_kv_norm_w, v_mla_kv_norm_w, 7424),
                    ("ln1_g", ln1_g, m_ln1_g, v_ln1_g, 7680), ("ln1_b", ln1_b, m_ln1_b, v_ln1_b, 8704),
                    ("ln2_g", ln2_g, m_ln2_g, v_ln2_g, 9728), ("ln2_b", ln2_b, m_ln2_b, v_ln2_b, 10752)]
    loss_row, small_res = _adam_small(small_all, [p[1:] for p in small_params])
    for p, r4 in zip(small_params, small_res):
        res[p[0]] = r4
    loss = loss_row[0, 0]

    order = ["w_ada", "b_ada", "w_in", "hg_lower_bounds", "hg_norm_w", "mla_q_norm_w", "w_q_up", "mla_kv_norm_w",
             "w_kv_up", "w_out", "ln1_g", "ln1_b", "w_mlp_in", "w_mlp_out", "ln2_g", "ln2_b"]
    def as_given(n, a):
        if n in transposed:
            a = a.T
        return a[None] if n in big or n == "w_ada" else a

    shaped = {n: tuple(as_given(n, a) for a in res[n]) for n in order}
    outs = [loss, grad_x.reshape(1, T, D_MODEL)]
    for i in range(4):
        outs += [shaped[n][i] for n in order]
    return tuple(outs)
```

```python
import functools

import jax
import jax.numpy as jnp
import numpy as np
from jax import lax
from jax.experimental import pallas as pl
from jax.experimental.pallas import tpu as pltpu

F32, BF16 = jnp.float32, jnp.bfloat16
N_DEV = 8
D_MODEL = 1024
HEADS = 4
HEAD_DIM = 128
ROPE_DIM = 64
QK_PAD = 256
CHUNK = 64
ROPE_THETA = 10000.0
RMS_EPS = 1e-6
LN_EPS = 1e-5
ALPHA = 2.0 ** 0.25
ATT_SCALE = (HEAD_DIM + ROPE_DIM) ** -0.5
LN2 = float(np.log(2.0))
Q_PRESCALE = ATT_SCALE / LN2
ADAM_LR, ADAM_B1, ADAM_B2, ADAM_EPS, ADAM_WD, ADAM_STEP = 0.001, 0.9, 0.999, 1e-08, 0.01, 10
NEG_BIG = -1e30

ROW_TILE = 512
ATT_TILE = 512
HGRN_GROUP = 8
MLP_SLABS = 4
VMEM_LIMIT = 56 * 2 ** 20

NN = (((1,), (0,)), ((), ()))
NT = (((1,), (1,)), ((), ()))
TN = (((0,), (0,)), ((), ()))


def _dot(a, b, dims=NN):
    return lax.dot_general(a, b, dims, preferred_element_type=F32)


def _bdot(a, b, dims=NN):
    return lax.dot_general(a.astype(BF16), b.astype(BF16), dims, preferred_element_type=F32)


def _hdot(a, b, dims=NN):
    return lax.dot_general(a, b, dims, precision=lax.Precision.HIGHEST, preferred_element_type=F32)


def _params():
    return pltpu.CompilerParams(vmem_limit_bytes=VMEM_LIMIT)


def _sigmoid(x):
    return 1.0 / (1.0 + jnp.exp(-x))


def _rowsum(x):
    return jnp.sum(x, axis=0, keepdims=True)


def _lanemean(x):
    return jnp.mean(x, axis=-1, keepdims=True)


def _full(shape):
    nd = len(shape)
    return pl.BlockSpec(shape, lambda *_: (0,) * nd)


class _Exchange:
    def __init__(self, arrs, scatter):
        self.arrs, self.scatter, self.n, self.aliases, self.middle_at = list(arrs), scatter, len(arrs), [], 0.5
        self.out_shape = [jax.ShapeDtypeStruct((N_DEV,) + (a.shape[1:] if scatter else a.shape), a.dtype)
                          for a in self.arrs]
        n = self.n
        self.scratch = [pltpu.SemaphoreType.DMA((n, N_DEV - 1)), pltpu.SemaphoreType.DMA((n, N_DEV - 1)),
                        pltpu.SemaphoreType.DMA((n,))]

    def _copies(self, ins, outs, sems):
        send_sems, recv_sems, loc_sems = sems
        x, y, c = lax.axis_index("x"), lax.axis_index("y"), lax.axis_index("c")
        me = 4 * x + 2 * y + c
        copies = []
        for k in range(self.n):
            src_of = (lambda i, k=k: ins[k].at[i]) if self.scatter else (lambda i, k=k: ins[k])
            copies.append((pltpu.make_async_copy(src_of(me), outs[k].at[me], loc_sems.at[k]), None))
            for p in range(1, N_DEV):
                px = (1 - x) if p & 4 else x
                py = (1 - y) if p & 2 else y
                pc = (1 - c) if p & 1 else c
                peer = 4 * px + 2 * py + pc
                both = dict(send_sem=send_sems.at[k, p - 1], recv_sem=recv_sems.at[k, p - 1],
                            device_id=(px, py, pc), device_id_type=pl.DeviceIdType.MESH)
                send = pltpu.make_async_remote_copy(src_ref=src_of(peer), dst_ref=outs[k].at[me], **both)
                recv = pltpu.make_async_remote_copy(src_ref=src_of(peer), dst_ref=outs[k].at[peer], **both)
                copies.append((send, recv))
        return copies

    def start(self, ins, outs, sems):
        for first, _ in self._copies(ins, outs, sems):
            first.start()

    def middle(self, ins, outs, sems):
        pass

    def wait(self, ins, outs, sems):
        for first, recv in self._copies(ins, outs, sems):
            if recv is None:
                first.wait()
            else:
                recv.wait_recv()
                first.wait_send()


class _StagedGather:
    def __init__(self, arr, middle_at=0.8, rows=None):
        self.r0, n = rows if rows else (0, arr.shape[0])
        block = (n,) + arr.shape[1:]
        self.arrs, self.aliases, self.middle_at = [arr], [], middle_at
        self.out_shape = [jax.ShapeDtypeStruct((N_DEV,) + block, BF16)]
        self.scratch = [pltpu.VMEM((N_DEV,) + block, BF16), pltpu.SemaphoreType.DMA((7,)),
                        pltpu.SemaphoreType.DMA((7,)), pltpu.SemaphoreType.DMA((2,)), pltpu.VMEM(block, arr.dtype)]

    def _parts(self, scr):
        stage, send_sems, recv_sems, loc_sems = scr[:4]
        x, y, c = lax.axis_index("x"), lax.axis_index("y"), lax.axis_index("c")
        me, sibling = (x, y, c), (x, y, 1 - c)
        chips = [(1 - x, y), (x, 1 - y), (1 - x, 1 - y)]

        def copy(j, block, to):
            px, py, pc = block
            slot = stage.at[4 * px + 2 * py + pc]
            return pltpu.make_async_remote_copy(src_ref=slot, dst_ref=slot, send_sem=send_sems.at[j],
                                                recv_sem=recv_sems.at[j], device_id=to,
                                                device_id_type=pl.DeviceIdType.MESH)

        return stage, loc_sems, me, sibling, chips, c, copy

    def start(self, ins, outs, scr):
        stage, loc_sems, me, sibling, chips, c, copy = self._parts(scr)
        x, y, _ = me
        raw = scr[4]
        own = pltpu.make_async_copy(ins[0].at[pl.ds(self.r0, raw.shape[0])], raw, loc_sems.at[0])
        own.start()
        own.wait()
        stage[4 * x + 2 * y + c] = raw[...].astype(BF16)
        copy(0, me, sibling).start()
        for j, chip in enumerate(chips):
            copy(1 + j, me, (*chip, c)).start()

    def middle(self, ins, outs, scr):
        stage, loc_sems, me, sibling, chips, c, copy = self._parts(scr)
        for j, chip in enumerate(chips):
            copy(1 + j, (*chip, c), me).wait_recv()
            copy(4 + j, (*chip, c), sibling).start()

    def wait(self, ins, outs, scr):
        stage, loc_sems, me, sibling, chips, c, copy = self._parts(scr)
        copy(0, sibling, me).wait_recv()
        for j, chip in enumerate(chips):
            copy(4 + j, (*chip, 1 - c), me).wait_recv()
        copy(0, me, sibling).wait_send()
        for j, chip in enumerate(chips):
            copy(1 + j, me, (*chip, c)).wait_send()
            copy(4 + j, (*chip, c), sibling).wait_send()
        whole = pltpu.make_async_copy(stage, outs[0], loc_sems.at[1])
        whole.start()
        whole.wait()


class _StagedScatter:
    def __init__(self, slabs, middle_at=0.2):
        _, r, c = slabs.shape
        self.arrs, self.aliases, self.middle_at = [slabs], [], middle_at
        self.out_shape = [jax.ShapeDtypeStruct((4, r, c), slabs.dtype)]
        self.scratch = [pltpu.VMEM((N_DEV, r, c), slabs.dtype), pltpu.VMEM((4, r, c), slabs.dtype),
                        pltpu.VMEM((3, r, c), slabs.dtype), pltpu.SemaphoreType.DMA((4,)), pltpu.SemaphoreType.DMA((4,)),
                        pltpu.SemaphoreType.DMA((3,)), pltpu.SemaphoreType.DMA((3,)), pltpu.SemaphoreType.DMA((4,))]

    def _parts(self, scr):
        stage, from_sib, from_chips, sib_send, sib_recv, ici_send, ici_recv, loc_sems = scr
        x, y, c = lax.axis_index("x"), lax.axis_index("y"), lax.axis_index("c")
        chips = [(1 - x, y), (x, 1 - y), (1 - x, 1 - y)]

        def to_sibling(j):
            return pltpu.make_async_remote_copy(src_ref=stage.at[2 * j + 1 - c], dst_ref=from_sib.at[j],
                                                send_sem=sib_send.at[j], recv_sem=sib_recv.at[j],
                                                device_id=(x, y, 1 - c), device_id_type=pl.DeviceIdType.MESH)

        def to_chip(k):
            px, py = chips[k]
            return pltpu.make_async_remote_copy(src_ref=stage.at[4 * px + 2 * py + c], dst_ref=from_chips.at[k],
                                                send_sem=ici_send.at[k], recv_sem=ici_recv.at[k],
                                                device_id=(px, py, c), device_id_type=pl.DeviceIdType.MESH)

        return stage, from_sib, from_chips, loc_sems, (x, y, c), chips, to_sibling, to_chip

    def start(self, ins, outs, scr):
        stage, _, _, loc_sems, _, _, to_sibling, _ = self._parts(scr)
        load = pltpu.make_async_copy(ins[0], stage, loc_sems.at[0])
        load.start()
        load.wait()
        for j in range(4):
            to_sibling(j).start()

    def middle(self, ins, outs, scr):
        stage, from_sib, _, _, (x, y, c), _, to_sibling, to_chip = self._parts(scr)
        for j in range(4):
            to_sibling(j).wait_recv()
            mine = stage.at[2 * j + c]
            mine[...] = (mine[...].astype(F32) + from_sib[j].astype(F32)).astype(mine.dtype)
        for k in range(3):
            to_chip(k).start()

    def wait(self, ins, outs, scr):
        stage, _, from_chips, loc_sems, (x, y, c), chips, to_sibling, to_chip = self._parts(scr)
        writes = [pltpu.make_async_copy(stage.at[4 * x + 2 * y + c], outs[0].at[2 * x + y], loc_sems.at[0])]
        for k, (px, py) in enumerate(chips):
            to_chip(k).wait_recv()
            writes.append(pltpu.make_async_copy(from_chips.at[k], outs[0].at[2 * px + py], loc_sems.at[1 + k]))
        for w in writes:
            w.start()
        for j in range(4):
            to_sibling(j).wait_send()
        for k in range(3):
            to_chip(k).wait_send()
        for w in writes:
            w.wait()


def _call(body, name, args, out_shape, grid=(), in_specs=(), out_specs=(), scratch_shapes=(), exchange=None):
    if exchange is None:
        return pl.pallas_call(body, name=name, grid=grid, in_specs=list(in_specs), out_specs=list(out_specs),
                              out_shape=list(out_shape), scratch_shapes=list(scratch_shapes),
                              compiler_params=_params())(*args), None
    exs = list(exchange) if isinstance(exchange, (list, tuple)) else [exchange]
    ni, no, ns = len(args), len(out_shape), len(scratch_shapes)
    nxi, nxo = sum(len(e.arrs) for e in exs), sum(len(e.out_shape) for e in exs)
    steps = int(np.prod(grid))
    mid_step = lambda e: min(max(int(steps * e.middle_at), 1), steps - 1)
    aliases, iat, oat = {}, ni, no
    for e in exs:
        for src, dst in e.aliases:
            aliases[iat + src] = oat + dst
        iat, oat = iat + len(e.arrs), oat + len(e.out_shape)

    def wrapped(*refs):
        a, xi = refs[:ni], refs[ni:ni + nxi]
        o, xo = refs[ni + nxi:ni + nxi + no], refs[ni + nxi + no:ni + nxi + no + nxo]
        s, xs = refs[ni + nxi + no + nxo:ni + nxi + no + nxo + ns], refs[ni + nxi + no + nxo + ns:]
        parts, iat, oat, sat = [], 0, 0, 0
        for e in exs:
            parts.append((e, xi[iat:iat + len(e.arrs)], xo[oat:oat + len(e.out_shape)], xs[sat:sat + len(e.scratch)]))
            iat, oat, sat = iat + len(e.arrs), oat + len(e.out_shape), sat + len(e.scratch)
        step = 0
        for d, g in enumerate(grid):
            step = step * g + pl.program_id(d)

        @pl.when(step == 0)
        def _():
            for e, ins, outs, sems in parts:
                e.start(ins, outs, sems)

        for at_step in sorted({mid_step(e) for e in exs}):
            @pl.when(step == at_step)
            def _():
                for e, ins, outs, sems in parts:
                    if mid_step(e) == at_step:
                        e.middle(ins, outs, sems)

        body(*a, *o, *s)

        @pl.when(step == steps - 1)
        def _():
            for e, ins, outs, sems in parts:
                e.wait(ins, outs, sems)

    hbm = pl.BlockSpec(memory_space=pltpu.HBM)
    res = pl.pallas_call(
        wrapped, name=name, grid=grid, in_specs=list(in_specs) + [hbm] * nxi, out_specs=list(out_specs) + [hbm] * nxo,
        out_shape=list(out_shape) + [o_ for e in exs for o_ in e.out_shape],
        scratch_shapes=list(scratch_shapes) + [s_ for e in exs for s_ in e.scratch],
        input_output_aliases=aliases, compiler_params=_params())(*args, *[a_ for e in exs for a_ in e.arrs])
    return res[:no], res[no:]


def _gather_two_level(arrs, name):
    n = len(arrs)
    out_shape = [jax.ShapeDtypeStruct((N_DEV,) + a.shape, a.dtype) for a in arrs]

    def body(*refs):
        ins, outs = refs[:n], refs[n:2 * n]
        send_sems, recv_sems, loc_sems = refs[2 * n:]
        x, y, c = lax.axis_index("x"), lax.axis_index("y"), lax.axis_index("c")
        me, sibling = (x, y, c), (x, y, 1 - c)
        chips = [(1 - x, y), (x, 1 - y), (1 - x, 1 - y)]

        def copy(k, j, block, to, src=None):
            px, py, pc = block
            dst = outs[k].at[4 * px + 2 * py + pc]
            return pltpu.make_async_remote_copy(src_ref=dst if src is None else src, dst_ref=dst,
                                                send_sem=send_sems.at[k, j], recv_sem=recv_sems.at[k, j],
                                                device_id=to, device_id_type=pl.DeviceIdType.MESH)

        mine = [pltpu.make_async_copy(ins[k], outs[k].at[4 * x + 2 * y + c], loc_sems.at[k]) for k in range(n)]
        first = []
        for k in range(n):
            mine[k].start()
            first.append(copy(k, 0, me, sibling, src=ins[k]))
            first += [copy(k, 1 + j, me, (*chip, c), src=ins[k]) for j, chip in enumerate(chips)]
        for cp in first:
            cp.start()
        passed = []
        for j, chip in enumerate(chips):
            for k in range(n):
                copy(k, 1 + j, (*chip, c), me).wait_recv()
                passed.append(copy(k, 4 + j, (*chip, c), sibling))
                passed[-1].start()
        for k in range(n):
            copy(k, 0, sibling, me).wait_recv()
            for j, chip in enumerate(chips):
                copy(k, 4 + j, (*chip, 1 - c), me).wait_recv()
        for cp in first + passed:
            cp.wait_send()
        for cp in mine:
            cp.wait()

    vmem = pl.BlockSpec(memory_space=pltpu.VMEM)
    return pl.pallas_call(body, name=name, out_shape=out_shape, in_specs=[vmem] * n, out_specs=[vmem] * n,
                          scratch_shapes=[pltpu.SemaphoreType.DMA((n, 7)), pltpu.SemaphoreType.DMA((n, 7)),
                                          pltpu.SemaphoreType.DMA((n,))], compiler_params=_params())(*arrs)


def _exchange(arrs, scatter, name):
    ex = _Exchange(arrs, scatter)

    def body(*refs):
        ins, outs, sems = refs[:ex.n], refs[ex.n:2 * ex.n], refs[2 * ex.n:]
        ex.start(ins, outs, sems)
        ex.wait(ins, outs, sems)

    hbm = pl.BlockSpec(memory_space=pltpu.HBM)
    return pl.pallas_call(body, name=name, out_shape=ex.out_shape, in_specs=[hbm] * ex.n, out_specs=[hbm] * ex.n,
                          scratch_shapes=ex.scratch)(*ex.arrs)


def _matmul(a, b, mode, name, out_dtype=F32, tm=512, tn=1024, tk=1024, a_fn=None, b_fn=None, extras=(),
            out_slabs=None, exchange=None):
    assert not (a_fn and b_fn) and not (b_fn and mode == "NT")
    if mode == "NN":
        (M, K), N = a.shape, b.shape[1]
    elif mode == "NT":
        (M, K), N = a.shape, b.shape[0]
    else:
        (K, M), N = a.shape, b.shape[1]
    slab_w = N // out_slabs if out_slabs else None
    if out_slabs:
        tn = max(slab_w, min(tn, N) // slab_w * slab_w)
    tm, tn, tk = min(tm, M), min(tn, N), min(tk, K)
    assert M % tm == 0 and N % tn == 0 and K % tk == 0, (name, M, N, K)
    nk = K // tk
    dims = {"NN": NN, "NT": NT, "TN": TN}[mode]
    ne = len(extras)

    def body(a_ref, b_ref, *rest):
        e_refs, o_ref, acc_ref = rest[:ne], rest[ne], rest[ne + 1]
        k = pl.program_id(2)

        @pl.when(k == 0)
        def _():
            acc_ref[...] = jnp.zeros_like(acc_ref)

        at, bt = a_ref[...], b_ref[...]
        if a_fn is not None:
            at = a_fn(at.astype(F32), *[e[...] for e in e_refs])
        if b_fn is not None:
            bt = b_fn(bt.astype(F32), *[e[...] for e in e_refs])
        acc_ref[...] += _bdot(at, bt, dims)

        @pl.when(k == nk - 1)
        def _():
            if out_slabs:
                for s in range(tn // slab_w):
                    o_ref[s] = acc_ref[:, s * slab_w:(s + 1) * slab_w].astype(out_dtype)
            else:
                o_ref[...] = acc_ref[...].astype(out_dtype)

    if mode == "TN":
        a_spec = pl.BlockSpec((tk, tm), lambda i, j, k: (k, i))
        e_spec = pl.BlockSpec((1, tm), lambda i, j, k: (0, i))
    else:
        a_spec = pl.BlockSpec((tm, tk), lambda i, j, k: (i, k))
        e_spec = pl.BlockSpec((1, tk), lambda i, j, k: (0, k))
    if mode == "NT":
        b_spec = pl.BlockSpec((tn, tk), lambda i, j, k: (j, k))
    else:
        b_spec = pl.BlockSpec((tk, tn), lambda i, j, k: (k, j))
    if b_fn is not None:
        e_spec = pl.BlockSpec((1, tn), lambda i, j, k: (0, j))
    if out_slabs:
        o_shape = jax.ShapeDtypeStruct((out_slabs, M, slab_w), out_dtype)
        o_spec = pl.BlockSpec((tn // slab_w, tm, slab_w), lambda i, j, k: (j, i, 0))
    else:
        o_shape = jax.ShapeDtypeStruct((M, N), out_dtype)
        o_spec = pl.BlockSpec((tm, tn), lambda i, j, k: (i, j))
    (out,), got = _call(body, name, (a, b, *extras), [o_shape], grid=(M // tm, N // tn, nk),
                        in_specs=[a_spec, b_spec] + [e_spec] * ne, out_specs=[o_spec],
                        scratch_shapes=[pltpu.VMEM((tm, tn), F32)], exchange=exchange)
    return out if exchange is None else (out, got)


def _modulate(x, sc, sh):
    return x * (1.0 + sc) + sh


def _square(x):
    return x * x


def _mod_part(c_all, w_ada_s, b_s):
    def body(c_ref, w_ref, b_ref, mod_ref, cond_ref):
        cv = c_ref[...]
        cond = cv * _sigmoid(cv)
        cond_ref[...] = cond
        mod_ref[...] = _bdot(cond, w_ref[...]) + b_ref[...]

    return pl.pallas_call(
        body, name="mod_part",
        out_shape=[jax.ShapeDtypeStruct((N_DEV, w_ada_s.shape[1]), F32), jax.ShapeDtypeStruct(c_all.shape, F32)],
        compiler_params=_params(),
    )(c_all, w_ada_s, b_s)


def _rms_fwd(x, w):
    rs = lax.rsqrt(_lanemean(x * x) + RMS_EPS)
    return x * rs * w, rs


def _rms_bwd(x, rs, w, dy):
    xhat = x * rs
    dxh = dy * w
    return rs * (dxh - xhat * _lanemean(dxh * xhat)), dy * xhat


def _mla_pre(z, pos_col, invf, m_rot, wq_ext, wkv_ext, qnw, kvnw, exchange=None):
    T = z.shape[0]
    tm = min(ROW_TILE, T)

    def body(z_ref, pos_ref, invf_ref, mrot_ref, wq_ref, wkv_ref, qnw_ref, kvnw_ref,
             q_ref, k_ref, v_ref, c1_ref, s1_ref, cqn_ref, ckvn_ref):
        hi = slice(HEAD_DIM, QK_PAD)
        ang = pos_ref[...].astype(F32) * invf_ref[:, hi]
        c1 = jnp.concatenate([jnp.ones((tm, HEAD_DIM), F32), mrot_ref[:, hi] * jnp.cos(ang)], axis=1)
        s1 = jnp.concatenate([jnp.zeros((tm, HEAD_DIM), F32), mrot_ref[:, hi] * jnp.sin(ang)], axis=1)
        c1_ref[...] = c1
        s1_ref[...] = s1
        cqn, _ = _rms_fwd(z_ref[:, 0:256], qnw_ref[...])
        ckvn, _ = _rms_fwd(z_ref[:, 256:512], kvnw_ref[...])
        cqn_ref[...] = cqn.astype(BF16)
        ckvn_ref[...] = ckvn.astype(BF16)
        qe = _bdot(cqn, wq_ref[...], NT)
        kve = _bdot(ckvn, wkv_ref[...])
        k_rope = z_ref[:, 512:768] * c1 + z_ref[:, 768:1024] * s1
        for h in range(HEADS):
            q_ref[h] = ((qe[:, 256 * h:256 * h + 256] * c1 + qe[:, 1024 + 256 * h:1280 + 256 * h] * s1)
                        * Q_PRESCALE).astype(BF16)
            k_ref[h] = (kve[:, 256 * h:256 * h + 256] + k_rope).astype(BF16)
            v_ref[h] = kve[:, 1024 + 128 * h:1152 + 128 * h].astype(BF16)

    row = lambda i: (i, 0)
    head = lambda i: (0, i, 0)
    return _call(
        body, "mla_pre", (z, pos_col, invf, m_rot, wq_ext, wkv_ext, qnw, kvnw), grid=(T // tm,),
        in_specs=[pl.BlockSpec((tm, 1024), lambda i: (i, 2)), pl.BlockSpec((tm, 1), row),
                  _full((1, 256)), _full((1, 256)), _full(wq_ext.shape), _full(wkv_ext.shape),
                  _full((1, 256)), _full((1, 256))],
        out_specs=[pl.BlockSpec((HEADS, tm, QK_PAD), head), pl.BlockSpec((HEADS, tm, QK_PAD), head),
                   pl.BlockSpec((HEADS, tm, HEAD_DIM), head), pl.BlockSpec((tm, 256), row), pl.BlockSpec((tm, 256), row),
                   pl.BlockSpec((tm, 256), row), pl.BlockSpec((tm, 256), row)],
        out_shape=[jax.ShapeDtypeStruct((HEADS, T, QK_PAD), BF16), jax.ShapeDtypeStruct((HEADS, T, QK_PAD), BF16),
                   jax.ShapeDtypeStruct((HEADS, T, HEAD_DIM), BF16), jax.ShapeDtypeStruct((T, 256), F32),
                   jax.ShapeDtypeStruct((T, 256), F32), jax.ShapeDtypeStruct((T, 256), BF16),
                   jax.ShapeDtypeStruct((T, 256), BF16)], exchange=exchange)


def _mla_bwd(dq, dk, dv, z, c1, s1, wq_ext, wkv_ext, qnw, kvnw):
    T = z.shape[0]
    tm = min(ROW_TILE, T)

    def body(dq_ref, dk_ref, dv_ref, z_ref, c1_ref, s1_ref, wq_ref, wkv_ref, qnw_ref, kvnw_ref,
             dz_ref, dqe_ref, dkve_ref, dnw_ref):
        @pl.when(pl.program_id(0) == 0)
        def _():
            dnw_ref[...] = jnp.zeros_like(dnw_ref)

        c1, s1 = c1_ref[...], s1_ref[...]
        dkpe = jnp.zeros((tm, QK_PAD), F32)
        for h in range(HEADS):
            dqh, dkh = dq_ref[h].astype(F32) * ATT_SCALE, dk_ref[h]
            dqe_ref[:, 256 * h:256 * h + 256] = (dqh * c1).astype(BF16)
            dqe_ref[:, 1024 + 256 * h:1280 + 256 * h] = (dqh * s1).astype(BF16)
            dkve_ref[:, 256 * h:256 * h + 256] = dkh
            dkve_ref[:, 1024 + 128 * h:1152 + 128 * h] = dv_ref[h]
            dkpe = dkpe + dkh.astype(F32)
        dcqn = _dot(dqe_ref[...], wq_ref[...])
        dckvn = _dot(dkve_ref[...], wkv_ref[...], NT)
        cq, ckv = z_ref[:, 0:256], z_ref[:, 256:512]
        _, rsq = _rms_fwd(cq, qnw_ref[...])
        _, rskv = _rms_fwd(ckv, kvnw_ref[...])
        dcq, wq_rows = _rms_bwd(cq, rsq, qnw_ref[...], dcqn)
        dckv, wkv_rows = _rms_bwd(ckv, rskv, kvnw_ref[...], dckvn)
        dnw_ref[:, 0:256] += _rowsum(wq_rows)
        dnw_ref[:, 256:512] += _rowsum(wkv_rows)
        dz_ref[:, 0:256] = dcq.astype(BF16)
        dz_ref[:, 256:512] = dckv.astype(BF16)
        dz_ref[:, 512:768] = (dkpe * c1).astype(BF16)
        dz_ref[:, 768:1024] = (dkpe * s1).astype(BF16)

    row = lambda i: (i, 0)
    head = lambda i: (0, i, 0)
    return pl.pallas_call(
        body, name="mla_bwd", grid=(T // tm,),
        in_specs=[pl.BlockSpec((HEADS, tm, QK_PAD), head), pl.BlockSpec((HEADS, tm, QK_PAD), head),
                  pl.BlockSpec((HEADS, tm, HEAD_DIM), head), pl.BlockSpec((tm, 1024), lambda i: (i, 2)),
                  pl.BlockSpec((tm, 256), row), pl.BlockSpec((tm, 256), row), _full(wq_ext.shape), _full(wkv_ext.shape),
                  _full((1, 256)), _full((1, 256))],
        out_specs=[pl.BlockSpec((tm, 1024), row), pl.BlockSpec((tm, 2048), row), pl.BlockSpec((tm, 1536), row),
                   _full((1, 512))],
        out_shape=[jax.ShapeDtypeStruct((T, 1024), BF16), jax.ShapeDtypeStruct((T, 2048), BF16),
                   jax.ShapeDtypeStruct((T, 1536), BF16), jax.ShapeDtypeStruct((1, 512), F32)],
        compiler_params=_params(),
    )(dq, dk, dv, z, c1, s1, wq_ext, wkv_ext, qnw, kvnw)


_HEAD_LANES = [slice(HEAD_DIM * h, HEAD_DIM * (h + 1)) for h in range(HEADS)]


def _lower_bound(lbraw_ref):
    a0, a1 = lbraw_ref[0:1, :], lbraw_ref[1:2, :]
    mx = jnp.maximum(a0, a1)
    e0, e1 = jnp.exp(a0 - mx), jnp.exp(a1 - mx)
    return e0 / (e0 + e1)


def _tri(lower):
    r = lax.broadcasted_iota(jnp.int32, (CHUNK, CHUNK), 0)
    c = lax.broadcasted_iota(jnp.int32, (CHUNK, CHUNK), 1)
    return (r >= c) if lower else (r <= c)


def _hgrn_gates(q, f, lb, tri_lo):
    sg = _sigmoid(f)
    forget = lb + (1.0 - lb) * sg
    k = 1.0 - forget
    b = _hdot(tri_lo.astype(F32), jnp.log(forget))
    b_ref, b_last = b[CHUNK // 2 - 1:CHUNK // 2, :], b[CHUNK - 1:CHUNK, :]
    e1, e2, e3, e4 = jnp.exp(b - b_ref), jnp.exp(b_ref - b), jnp.exp(b_last - b), jnp.exp(b)
    return dict(sg=sg, forget=forget, k=k, e1=e1, e2=e2, e3=e3, e4=e4, qa=q * e1, ka=k * e2, kl=k * e3, qb=q * e4,
                decay=jnp.exp(b_last))


def _hgrn_fwd(z, lbraw, nw, exchange=None):
    T = z.shape[0]
    G = min(HGRN_GROUP, T // CHUNK)
    rows = G * CHUNK
    n_chunks = T // CHUNK

    def body(q_ref, f_ref, i_ref, g_ref, lbraw_ref, nw_ref, oraw_ref, og_ref, sp_ref, st_ref):
        @pl.when(pl.program_id(0) == 0)
        def _():
            st_ref[...] = jnp.zeros_like(st_ref)

        lb_all = _lower_bound(lbraw_ref)
        tri_lo = _tri(True)

        def chunk(cc, carry):
            rs = pl.ds(pl.multiple_of(cc * CHUNK, CHUNK), CHUNK)
            t = _hgrn_gates(q_ref[rs, :], f_ref[rs, :], lb_all, tri_lo)
            v, gate = i_ref[rs, :], g_ref[rs, :]
            st = [st_ref[h] for h in range(HEADS)]
            a = [jnp.where(tri_lo, _bdot(t["qa"][:, s], t["ka"][:, s], NT), 0.0) for s in _HEAD_LANES]
            kv = [_bdot(v[:, s], t["kl"][:, s], TN) for s in _HEAD_LANES]
            o = [_bdot(a[h], v[:, s]) + _bdot(t["qb"][:, s], st[h], NT) for h, s in enumerate(_HEAD_LANES)]
            for h, s in enumerate(_HEAD_LANES):
                sp_ref[cc, h] = st[h]
                st_ref[h] = st[h] * t["decay"][:, s] + kv[h]
            oraw_ref[rs, :] = jnp.concatenate(o, axis=1)
            on = jnp.concatenate([_rms_fwd(o[h], nw_ref[:, s])[0] for h, s in enumerate(_HEAD_LANES)], axis=1)
            og_ref[rs, :] = (on * (gate * _sigmoid(gate))).astype(BF16)
            return carry

        lax.fori_loop(0, G, chunk, 0, unroll=4)

    col = lambda j: pl.BlockSpec((rows, 512), lambda r, j=j: (r, j))
    return _call(
        body, "hgrn_fwd", (z, z, z, z, lbraw, nw), grid=(T // rows,),
        in_specs=[col(0), col(1), col(2), col(3), _full((2, 512)), _full((1, 512))],
        out_specs=[col(0), col(0), pl.BlockSpec((G, HEADS, HEAD_DIM, HEAD_DIM), lambda r: (r, 0, 0, 0))],
        out_shape=[jax.ShapeDtypeStruct((T, 512), F32), jax.ShapeDtypeStruct((T, 512), BF16),
                   jax.ShapeDtypeStruct((n_chunks, HEADS, HEAD_DIM, HEAD_DIM), F32)],
        scratch_shapes=[pltpu.VMEM((HEADS, HEAD_DIM, HEAD_DIM), F32)], exchange=exchange)


def _hgrn_bwd(dmixcat, z, oraw, sprev, lbraw, nw, exchange=None):
    T = z.shape[0]
    G = min(HGRN_GROUP, T // CHUNK)
    rows = G * CHUNK
    ng = T // rows

    def body(dog_ref, q_ref, f_ref, i_ref, g_ref, oraw_ref, sp_ref, lbraw_ref, nw_ref,
             dz_ref, dsmall_ref, dst_ref):
        @pl.when(pl.program_id(0) == 0)
        def _():
            dst_ref[...] = jnp.zeros_like(dst_ref)
            dsmall_ref[...] = jnp.zeros_like(dsmall_ref)

        lb_all = _lower_bound(lbraw_ref)
        tri_lo, tri_up = _tri(True), _tri(False)
        rowid = lax.broadcasted_iota(jnp.int32, (CHUNK, HEADS * HEAD_DIM), 0)

        def chunk(it, carry):
            cc = G - 1 - it
            rs = pl.ds(pl.multiple_of(cc * CHUNK, CHUNK), CHUNK)
            heads = list(enumerate(_HEAD_LANES))
            cat = lambda parts: jnp.concatenate(parts, axis=1)
            per_head_mean = lambda x: cat([jnp.broadcast_to(_lanemean(x[:, s]), (CHUNK, HEAD_DIM)) for s in _HEAD_LANES])
            t = _hgrn_gates(q_ref[rs, :], f_ref[rs, :], lb_all, tri_lo)
            v, gate, o, dog, nw_all = i_ref[rs, :], g_ref[rs, :], oraw_ref[rs, :], dog_ref[rs, :], nw_ref[...]
            rs_o = lax.rsqrt(per_head_mean(o * o) + RMS_EPS)
            xhat = o * rs_o
            sgg = _sigmoid(gate)
            d_on = dog * (gate * sgg)
            dz_ref[rs, 1536:2048] = (dog * (xhat * nw_all) * (sgg * (1.0 + gate * (1.0 - sgg)))).astype(BF16)
            dxh = d_on * nw_all
            do = rs_o * (dxh - xhat * per_head_mean(dxh * xhat))
            dsmall_ref[:, 512:1024] += _rowsum(d_on * xhat)
            st = [sp_ref[cc, h] for h in range(HEADS)]
            dst = [dst_ref[h] for h in range(HEADS)]
            a = [jnp.where(tri_lo, _bdot(t["qa"][:, s], t["ka"][:, s], NT), 0.0) for s in _HEAD_LANES]
            da = [jnp.where(tri_lo, _bdot(do[:, s], v[:, s], NT), 0.0) for s in _HEAD_LANES]
            dqb = cat([_bdot(do[:, s], st[h]) for h, s in heads])
            dkl = cat([_bdot(v[:, s], dst[h]) for h, s in heads])
            dv_ = cat([_bdot(t["kl"][:, s], dst[h], NT) + _bdot(a[h], do[:, s], TN) for h, s in heads])
            dqa = cat([_bdot(da[h], t["ka"][:, s]) for h, s in heads])
            dka = cat([_bdot(da[h], t["qa"][:, s], TN) for h, s in heads])
            ddecay = cat([_rowsum(dst[h] * st[h]) for h in range(HEADS)])
            for h, s in heads:
                dst_ref[h] = dst[h] * t["decay"][:, s] + _bdot(do[:, s], t["qb"][:, s], TN)
            pa, pk, pb, pl_ = dqa * t["qa"], dka * t["ka"], dqb * t["qb"], dkl * t["kl"]
            db = pa - pk + pb - pl_
            db = db + jnp.where(rowid == CHUNK // 2 - 1, _rowsum(pk - pa), 0.0)
            db = db + jnp.where(rowid == CHUNK - 1, _rowsum(pl_) + ddecay * t["decay"], 0.0)
            dlogf = _hdot(tri_up.astype(F32), db)
            dforget = dlogf / t["forget"] - (dka * t["e2"] + dkl * t["e3"])
            sg = t["sg"]
            dz_ref[rs, 0:512] = (dqa * t["e1"] + dqb * t["e4"]).astype(BF16)
            dz_ref[rs, 512:1024] = (dforget * (1.0 - lb_all) * sg * (1.0 - sg)).astype(BF16)
            dz_ref[rs, 1024:1536] = dv_.astype(BF16)
            dsmall_ref[:, 0:512] += _rowsum(dforget * (1.0 - sg))
            return carry

        lax.fori_loop(0, G, chunk, 0, unroll=4)

    col = lambda j: pl.BlockSpec((rows, 512), lambda r, j=j: (ng - 1 - r, j))
    return _call(
        body, "hgrn_bwd", (dmixcat, z, z, z, z, oraw, sprev, lbraw, nw), grid=(ng,),
        in_specs=[col(0), col(0), col(1), col(2), col(3), col(0),
                  pl.BlockSpec((G, HEADS, HEAD_DIM, HEAD_DIM), lambda r: (ng - 1 - r, 0, 0, 0)),
                  _full((2, 512)), _full((1, 512))],
        out_specs=[pl.BlockSpec((rows, 2048), lambda r: (ng - 1 - r, 0)), _full((1, 1024))],
        out_shape=[jax.ShapeDtypeStruct((T, 2048), BF16), jax.ShapeDtypeStruct((1, 1024), F32)],
        scratch_shapes=[pltpu.VMEM((HEADS, HEAD_DIM, HEAD_DIM), F32)], exchange=exchange)


def _diag_mask(t):
    r = lax.broadcasted_iota(jnp.int32, (t, t), 0)
    c = lax.broadcasted_iota(jnp.int32, (t, t), 1)
    return r >= c


def _attn_fwd(q, k, v, exchange=None):
    _, T, _ = q.shape
    t = min(ATT_TILE, T)

    def body(q_ref, k_ref, v_ref, o_ref, lse_ref):
        i = pl.program_id(1)
        qb = q_ref[...]

        rows = lambda j: pl.ds(pl.multiple_of(j * t, t), t)

        def logits(j, masked):
            s = _dot(qb, k_ref[rows(j), :], NT)
            return jnp.where(_diag_mask(t), s, NEG_BIG) if masked else s

        def absorb(s, j, carry):
            m, l, acc = carry
            mn = jnp.maximum(m, jnp.max(s, axis=-1, keepdims=True))
            p = jnp.exp2(s - mn)
            al = jnp.exp2(m - mn)
            return mn, al * l + jnp.sum(p, axis=-1, keepdims=True), al * acc + _dot(p.astype(BF16), v_ref[rows(j), :])

        def pair(j0, carry, last_masked):
            s0, s1 = logits(j0, False), logits(j0 + 1, last_masked)
            return absorb(s1, j0 + 1, absorb(s0, j0, carry))

        init = (jnp.full((t, 1), NEG_BIG, F32), jnp.zeros((t, 1), F32), jnp.zeros((t, HEAD_DIM), F32))
        carry = lax.fori_loop(0, i // 2, lambda jj, c: pair(2 * jj, c, False), init)
        m, l, acc = lax.cond(i % 2 == 1, lambda c: pair(i - 1, c, True),
                             lambda c: absorb(logits(i, True), i, c), carry)
        o_ref[...] = acc / l
        lse_ref[...] = jnp.broadcast_to(m + jnp.log2(l), (t, HEAD_DIM))

    return _call(
        body, "attn_fwd", (q, k, v), grid=(HEADS, T // t),
        in_specs=[pl.BlockSpec((None, t, QK_PAD), lambda h, i: (h, i, 0)),
                  pl.BlockSpec((None, T, QK_PAD), lambda h, i: (h, 0, 0)),
                  pl.BlockSpec((None, T, HEAD_DIM), lambda h, i: (h, 0, 0))],
        out_specs=[pl.BlockSpec((t, HEAD_DIM), lambda h, i: (i, h)),
                   pl.BlockSpec((None, t, HEAD_DIM), lambda h, i: (h, i, 0))],
        out_shape=[jax.ShapeDtypeStruct((T, HEADS * HEAD_DIM), F32), jax.ShapeDtypeStruct((HEADS, T, HEAD_DIM), F32)],
        exchange=exchange)


def _attn_bwd(q, k, v, dmixcat, o, lse, exchange=None):
    _, T, _ = q.shape
    t = min(ATT_TILE, T)
    nq = T // t

    def body(q_ref, k_ref, v_ref, do_ref, o_ref, lse_ref, dq_ref, dk_ref, dv_ref, delta_ref, dq_acc):
        j = pl.program_id(1)

        @pl.when(j == 0)
        def _():
            dq_acc[...] = jnp.zeros_like(dq_acc)

            def fill(i, carry):
                rs = pl.ds(pl.multiple_of(i * t, t), t)
                delta_ref[rs, :] = jnp.broadcast_to(
                    jnp.sum(do_ref[rs, :] * o_ref[rs, :], axis=-1, keepdims=True), (t, HEAD_DIM))
                return carry

            lax.fori_loop(0, nq, fill, 0)

        kb, vb = k_ref[...], v_ref[...]

        def steps(blocks, carry):
            dk, dv = carry
            rs = [pl.ds(pl.multiple_of(i * t, t), t) for i, _ in blocks]
            qb = [q_ref[r, :] for r in rs]
            dob = [do_ref[r, :].astype(BF16) for r in rs]
            s = [_dot(b, kb, NT) for b in qb]
            dp = [_dot(b, vb, NT) for b in dob]
            for n, (_, masked) in enumerate(blocks):
                p = jnp.exp2(s[n] - lse_ref[rs[n], 0:1])
                if masked:
                    p = jnp.where(_diag_mask(t), p, 0.0)
                ds = (p * (dp[n] - delta_ref[rs[n], 0:1])).astype(BF16)
                dq_acc[rs[n], :] += _dot(ds, kb)
                dk = dk + _dot(ds, qb[n], TN)
                dv = dv + _dot(p.astype(BF16), dob[n], TN)
            return dk, dv

        zero = (jnp.zeros((t, QK_PAD), F32), jnp.zeros((t, HEAD_DIM), F32))
        rest = nq - 1 - j
        carry = lax.cond(rest % 2 == 1, lambda c: steps([(j, True), (j + 1, False)], c),
                         lambda c: steps([(j, True)], c), zero)
        first = j + 1 + rest % 2
        dk, dv = lax.fori_loop(0, rest // 2, lambda n, c: steps([(first + 2 * n, False), (first + 2 * n + 1, False)], c),
                               carry)
        dk_ref[...] = (dk * LN2).astype(BF16)
        dv_ref[...] = dv.astype(BF16)

        @pl.when(j == nq - 1)
        def _():
            dq_ref[...] = dq_acc[...].astype(BF16)

    return _call(
        body, "attn_bwd", (q, k, v, dmixcat, o, lse), grid=(HEADS, nq),
        in_specs=[pl.BlockSpec((None, T, QK_PAD), lambda h, j: (h, 0, 0)),
                  pl.BlockSpec((None, t, QK_PAD), lambda h, j: (h, j, 0)),
                  pl.BlockSpec((None, t, HEAD_DIM), lambda h, j: (h, j, 0)),
                  pl.BlockSpec((T, HEAD_DIM), lambda h, j: (0, HEADS + h)),
                  pl.BlockSpec((T, HEAD_DIM), lambda h, j: (0, h)),
                  pl.BlockSpec((None, T, HEAD_DIM), lambda h, j: (h, 0, 0))],
        out_specs=[pl.BlockSpec((None, T, QK_PAD), lambda h, j: (h, 0, 0)),
                   pl.BlockSpec((None, t, QK_PAD), lambda h, j: (h, j, 0)),
                   pl.BlockSpec((None, t, HEAD_DIM), lambda h, j: (h, j, 0))],
        out_shape=[jax.ShapeDtypeStruct((HEADS, T, QK_PAD), BF16), jax.ShapeDtypeStruct((HEADS, T, QK_PAD), BF16),
                   jax.ShapeDtypeStruct((HEADS, T, HEAD_DIM), BF16)],
        scratch_shapes=[pltpu.VMEM((T, HEAD_DIM), F32), pltpu.VMEM((T, QK_PAD), F32)], exchange=exchange)


def _ln_fwd(r):
    mu = _lanemean(r)
    xc = r - mu
    rstd = lax.rsqrt(_lanemean(xc * xc) + LN_EPS)
    return xc * rstd, rstd


def _ln_bwd(dxh, xhat, rstd):
    return rstd * (dxh - _lanemean(dxh) - xhat * _lanemean(dxh * xhat))


def _mix_ln1(o_hg, o_mla, w_out, x, g_a, ln1_g, ln1_b, sc_m, sh_m, exchange=None):
    T = x.shape[0]
    tm = min(ROW_TILE, T)
    half = o_hg.shape[1]

    def body(hg_ref, mla_ref, w_ref, x_ref, ga_ref, g_ref, b_ref, sc_ref, sh_ref, mix_ref, xhat_ref, rstd_ref, u2_ref):
        mix = _dot(hg_ref[...], w_ref[0:half, :]) + _bdot(mla_ref[...], w_ref[half:, :])
        mix_ref[...] = mix
        xhat, rstd = _ln_fwd(ALPHA * x_ref[...] + (1.0 + ga_ref[...]) * mix)
        xhat_ref[...] = xhat
        rstd_ref[...] = jnp.broadcast_to(rstd, (tm, 128))
        u2_ref[...] = _modulate(xhat * g_ref[...] + b_ref[...], sc_ref[...], sh_ref[...]).astype(BF16)

    row = pl.BlockSpec((tm, D_MODEL), lambda i: (i, 0))
    vec = _full((1, D_MODEL))
    halfrow = pl.BlockSpec((tm, half), lambda i: (i, 0))
    return _call(
        body, "mix_ln1", (o_hg, o_mla, w_out, x, g_a, ln1_g, ln1_b, sc_m, sh_m), grid=(T // tm,),
        in_specs=[halfrow, halfrow, _full(w_out.shape), row, vec, vec, vec, vec, vec],
        out_specs=[row, row, pl.BlockSpec((tm, 128), lambda i: (i, 0)), row],
        out_shape=[jax.ShapeDtypeStruct((T, D_MODEL), F32), jax.ShapeDtypeStruct((T, D_MODEL), F32),
                   jax.ShapeDtypeStruct((T, 128), F32), jax.ShapeDtypeStruct((T, D_MODEL), BF16)],
        exchange=exchange)


def _mlp_fwd(u2, w1, w2, xhat1, ln1_g, ln1_b, g_m, ln2_g, ln2_b, target):
    T = u2.shape[0]
    half, tf = w1[0].shape[1:]
    nf = N_DEV // MLP_SLABS
    tm = min(ROW_TILE, T)

    def body(u2_ref, w1a_ref, w1b_ref, w2_ref, xhat_ref, g1_ref, b1_ref, gm_ref, g2_ref, b2_ref, tgt_ref,
             r_ref, dr2_ref, dh_ref, small_ref, acc_ref):
        i, f = pl.program_id(0), pl.program_id(1)
        dm = D_MODEL

        @pl.when((i == 0) & (f == 0))
        def _():
            small_ref[...] = jnp.zeros_like(small_ref)

        @pl.when(f == 0)
        def _():
            acc_ref[...] = jnp.zeros_like(acc_ref)

        u2t = u2_ref[...]
        part = None
        for s in range(MLP_SLABS):
            r = jnp.maximum(_dot(u2t[:, :half], w1a_ref[s]) + _dot(u2t[:, half:], w1b_ref[s]), 0.0)
            r_ref[:, s * tf:(s + 1) * tf] = r.astype(BF16)
            d = _bdot(r * r, w2_ref[s])
            part = d if part is None else part + d
        acc_ref[...] += part

        @pl.when(f == nf - 1)
        def _():
            h = acc_ref[...]
            x1 = xhat_ref[...] * g1_ref[...] + b1_ref[...]
            xhat2, rstd2 = _ln_fwd(ALPHA * x1 + (1.0 + gm_ref[...]) * h)
            err = xhat2 * g2_ref[...] + b2_ref[...] - tgt_ref[...]
            small_ref[:, 3 * dm:] += jnp.sum(0.5 * _lanemean(err * err), axis=0, keepdims=True)
            dy = err * (1.0 / D_MODEL)
            small_ref[:, dm:2 * dm] += _rowsum(dy * xhat2)
            small_ref[:, 2 * dm:3 * dm] += _rowsum(dy)
            dr2 = _ln_bwd(dy * g2_ref[...], xhat2, rstd2)
            dr2_ref[...] = dr2
            small_ref[:, 0:dm] += _rowsum(dr2 * h)
            dh_ref[...] = ((1.0 + gm_ref[...]) * dr2).astype(BF16)

    row = pl.BlockSpec((tm, D_MODEL), lambda i, f: (i, 0))
    vec = _full((1, D_MODEL))
    return pl.pallas_call(
        body, name="mlp_fwd", grid=(T // tm, nf),
        in_specs=[row, pl.BlockSpec((MLP_SLABS, half, tf), lambda i, f: (f, 0, 0)),
                  pl.BlockSpec((MLP_SLABS, half, tf), lambda i, f: (f, 0, 0)),
                  pl.BlockSpec((MLP_SLABS, tf, D_MODEL), lambda i, f: (f, 0, 0)),
                  row, vec, vec, vec, vec, vec, row],
        out_specs=[pl.BlockSpec((tm, MLP_SLABS * tf), lambda i, f: (i, f)), row, row, _full((1, 3 * D_MODEL + 128))],
        out_shape=[jax.ShapeDtypeStruct((T, N_DEV * tf), BF16), jax.ShapeDtypeStruct((T, D_MODEL), F32),
                   jax.ShapeDtypeStruct((T, D_MODEL), BF16), jax.ShapeDtypeStruct((1, 3 * D_MODEL + 128), F32)],
        scratch_shapes=[pltpu.VMEM((tm, D_MODEL), F32)],
        compiler_params=_params(),
    )(u2, w1[0], w1[1], w2, xhat1, ln1_g, ln1_b, g_m, ln2_g, ln2_b, target)


def _mlp_bwd(dh, w1, w2, r, dr2, xhat1, rstd1, mix, ln1_g, ln1_b, sc_m, g_a):
    T = dh.shape[0]
    half, tf = w1[0].shape[1:]
    nf = N_DEV // MLP_SLABS
    tm = min(ROW_TILE, T)

    def body(dh_ref, w1a_ref, w1b_ref, w2_ref, r_ref, dr2_ref, xhat_ref, rstd_ref, mix_ref, g1_ref, b1_ref, sc_ref, ga_ref,
             dhpre_ref, dr1_ref, dmix_ref, small_ref, acc_ref):
        i, f = pl.program_id(0), pl.program_id(1)
        dm = D_MODEL

        @pl.when((i == 0) & (f == 0))
        def _():
            small_ref[...] = jnp.zeros_like(small_ref)

        @pl.when(f == 0)
        def _():
            acc_ref[...] = jnp.zeros_like(acc_ref)

        dht = dh_ref[...]
        part = None
        for s in range(MLP_SLABS):
            cols = slice(s * tf, (s + 1) * tf)
            dhpre = (_dot(dht, w2_ref[s], NT) * (2.0 * r_ref[:, cols].astype(F32))).astype(BF16)
            dhpre_ref[:, cols] = dhpre
            d = jnp.concatenate([_dot(dhpre, w1a_ref[s], NT), _dot(dhpre, w1b_ref[s], NT)], axis=1)
            part = d if part is None else part + d
        acc_ref[...] += part

        @pl.when(f == nf - 1)
        def _():
            du2 = acc_ref[...]
            xhat = xhat_ref[...]
            x1 = xhat * g1_ref[...] + b1_ref[...]
            dx1 = ALPHA * dr2_ref[...] + du2 * (1.0 + sc_ref[...])
            small_ref[:, 2 * dm:3 * dm] += _rowsum(du2 * x1)
            small_ref[:, dm:2 * dm] += _rowsum(du2)
            small_ref[:, 3 * dm:4 * dm] += _rowsum(dx1 * xhat)
            small_ref[:, 4 * dm:5 * dm] += _rowsum(dx1)
            dr1 = _ln_bwd(dx1 * g1_ref[...], xhat, rstd_ref[:, 0:1])
            dr1_ref[...] = dr1
            small_ref[:, 0:dm] += _rowsum(dr1 * mix_ref[...])
            dmix_ref[...] = ((1.0 + ga_ref[...]) * dr1).astype(BF16)

    row = pl.BlockSpec((tm, D_MODEL), lambda i, f: (i, 0))
    vec = _full((1, D_MODEL))
    return pl.pallas_call(
        body, name="mlp_bwd", grid=(T // tm, nf),
        in_specs=[row, pl.BlockSpec((MLP_SLABS, half, tf), lambda i, f: (f, 0, 0)),
                  pl.BlockSpec((MLP_SLABS, half, tf), lambda i, f: (f, 0, 0)),
                  pl.BlockSpec((MLP_SLABS, tf, D_MODEL), lambda i, f: (f, 0, 0)),
                  pl.BlockSpec((tm, MLP_SLABS * tf), lambda i, f: (i, f)), row, row,
                  pl.BlockSpec((tm, 128), lambda i, f: (i, 0)), row, vec, vec, vec, vec],
        out_specs=[pl.BlockSpec((tm, MLP_SLABS * tf), lambda i, f: (i, f)), row, row, _full((1, 5 * D_MODEL))],
        out_shape=[jax.ShapeDtypeStruct((T, N_DEV * tf), BF16), jax.ShapeDtypeStruct((T, D_MODEL), F32),
                   jax.ShapeDtypeStruct((T, D_MODEL), BF16), jax.ShapeDtypeStruct((1, 5 * D_MODEL), F32)],
        scratch_shapes=[pltpu.VMEM((tm, D_MODEL), F32)],
        compiler_params=_params(),
    )(dh, w1[0], w1[1], w2, r, dr2, xhat1, rstd1, mix, ln1_g, ln1_b, sc_m, g_a)


def _input_bwd(dz_h, dz_m, w_in_ext, x, dr1, sc_a, exchange=None):
    T = x.shape[0]
    tm = min(ROW_TILE, T)

    def body(dzh_ref, dzm_ref, w_ref, x_ref, dr1_ref, sc_ref, gx_ref, small_ref):
        @pl.when(pl.program_id(0) == 0)
        def _():
            small_ref[...] = jnp.zeros_like(small_ref)

        du = _bdot(dzh_ref[...], w_ref[0:2048, :]) + _bdot(dzm_ref[...], w_ref[2048:3072, :])
        gx_ref[...] = ALPHA * dr1_ref[...] + du * (1.0 + sc_ref[...])
        small_ref[:, D_MODEL:] += _rowsum(du * x_ref[...])
        small_ref[:, 0:D_MODEL] += _rowsum(du)

    row = pl.BlockSpec((tm, D_MODEL), lambda i: (i, 0))
    vec = _full((1, D_MODEL))
    return _call(
        body, "input_bwd", (dz_h, dz_m, w_in_ext, x, dr1, sc_a), grid=(T // tm,),
        in_specs=[pl.BlockSpec((tm, 2048), lambda i: (i, 0)), row, _full(w_in_ext.shape), row, row, vec],
        out_specs=[row, _full((1, 2 * D_MODEL))],
        out_shape=[jax.ShapeDtypeStruct((T, D_MODEL), F32), jax.ShapeDtypeStruct((1, 2 * D_MODEL), F32)],
        exchange=exchange)


def _adam_math(w, g, m, v):
    m = ADAM_B1 * m + (1.0 - ADAM_B1) * g
    v = ADAM_B2 * v + (1.0 - ADAM_B2) * (g * g)
    m_hat = m / (1.0 - ADAM_B1 ** ADAM_STEP)
    v_hat = v / (1.0 - ADAM_B2 ** ADAM_STEP)
    return -ADAM_LR * (m_hat / (jnp.sqrt(v_hat) + ADAM_EPS) + ADAM_WD * w), m, v


def _adam(g_slabs, w, m, v, name, g_fn=None, g_extra=()):
    R, C = w.shape
    tr = 256 if R % 256 == 0 else R
    ns = 0 if g_slabs is None else g_slabs.shape[0]
    slab_rows = tr if g_slabs is None or g_slabs.shape[1] == R else g_slabs.shape[1]
    assert slab_rows == tr or tr == R
    ne = len(g_extra)

    def body(*refs):
        e_refs = refs[:ne]
        refs = refs[ne:]
        if ns:
            gs_ref, refs = refs[0], refs[1:]
        w_ref, m_ref, v_ref, g_ref, d_ref, nm_ref, nv_ref = refs
        if g_fn is not None:
            g = g_fn(*e_refs)
        else:
            g = gs_ref[0].astype(F32)
            for s in range(1, ns):
                g = g + gs_ref[s].astype(F32)
            g = g[:tr]
        d, nm, nv = _adam_math(w_ref[...], g, m_ref[...], v_ref[...])
        g_ref[...] = g
        d_ref[...] = d
        nm_ref[...] = nm
        nv_ref[...] = nv

    blk = pl.BlockSpec((tr, C), lambda i: (i, 0))
    in_specs = [pl.BlockSpec((tr, e.shape[1]), lambda i: (i, 0)) if e.shape[0] == R else _full(e.shape) for e in g_extra]
    args = list(g_extra)
    if ns:
        in_specs.append(pl.BlockSpec((ns, slab_rows, C), lambda i: (0, i, 0)))
        args.append(g_slabs)
    return pl.pallas_call(
        body, name=name, grid=(R // tr,), in_specs=in_specs + [blk] * 3, out_specs=[blk] * 4,
        out_shape=[jax.ShapeDtypeStruct((R, C), F32)] * 4, compiler_params=_params(),
    )(*args, w, m, v)


def _adam_small(small_all, params):
    n = len(params)

    def body(*refs):
        s_ref, refs = refs[0], refs[1:]
        wmv, loss_ref, outs = refs[:3 * n], refs[3 * n], refs[3 * n + 1:]
        tot = s_ref[0]
        for i in range(1, N_DEV):
            tot = tot + s_ref[i]
        loss_ref[...] = tot[:, SMALL_W - 128:]
        for j, (w, _, _, off) in enumerate(params):
            w_ref, m_ref, v_ref = wmv[3 * j:3 * j + 3]
            g_ref, d_ref, nm_ref, nv_ref = outs[4 * j:4 * j + 4]
            if w.shape[0] == 2:
                lb = _lower_bound(w_ref)
                g0 = tot[:, off:off + w.shape[1]] * lb * (1.0 - lb)
                rows = [(slice(0, 1), g0), (slice(1, 2), -g0)]
            else:
                rows = [(slice(0, 1), tot[:, off:off + w.shape[1]])]
            for rs, g in rows:
                d, nm, nv = _adam_math(w_ref[rs, :], g, m_ref[rs, :], v_ref[rs, :])
                g_ref[rs, :], d_ref[rs, :], nm_ref[rs, :], nv_ref[rs, :] = g, d, nm, nv

    out_shape = [jax.ShapeDtypeStruct((1, 128), F32)]
    for w, _, _, _ in params:
        out_shape += [jax.ShapeDtypeStruct(w.shape, F32)] * 4
    res = pl.pallas_call(body, name="adam_small", out_shape=out_shape, compiler_params=_params())(
        small_all, *[a for w, m, v, _ in params for a in (w, m, v)])
    return res[0], [tuple(res[1 + 4 * j:5 + 4 * j]) for j in range(n)]


def _cols_from_slabs(g):
    s, r, c = g.shape
    return jnp.transpose(g, (1, 0, 2)).reshape(r, s * c)


def _slabs_from_cols(w):
    r, c = w.shape
    return jnp.transpose(w.reshape(r, N_DEV, c // N_DEV), (1, 0, 2))


def _rot_half_rows(wt):
    return jnp.concatenate([-wt[32:], wt[:32]], axis=0)


def _unrot_half_rows(dwt_rot):
    return jnp.concatenate([dwt_rot[32:], -dwt_rot[:32]], axis=0)


def _ext_in_t(g):
    n, rows, k_in = g.shape
    keep = n * rows - ROPE_DIM

    def body(g_ref, o_ref, stage_ref):
        stage_ref[keep:, :] = jnp.zeros((o_ref.shape[0] - keep, k_in), F32)
        for i in range(n - 1):
            stage_ref[rows * i:rows * (i + 1), :] = g_ref[i].astype(F32)
        last = g_ref[n - 1].astype(F32)
        stage_ref[rows * (n - 1):keep, :] = last[:rows - ROPE_DIM]
        wk = last[rows - ROPE_DIM:]
        stage_ref[keep + 128:keep + 192, :] = wk
        stage_ref[keep + 384:keep + 416, :] = -wk[32:]
        stage_ref[keep + 416:keep + 448, :] = wk[:32]
        o_ref[...] = stage_ref[...].astype(BF16)

    return pl.pallas_call(body, name="ext_w_in", out_shape=jax.ShapeDtypeStruct((keep + 512, k_in), BF16),
                          scratch_shapes=[pltpu.VMEM((keep + 512, k_in), F32)], compiler_params=_params())(g)


def _ext_q_t(wt):
    r = wt.shape[1]
    z64, z128 = jnp.zeros((64, r), BF16), jnp.zeros((128, r), BF16)
    per = HEAD_DIM + ROPE_DIM
    main = [jnp.concatenate([wt[per * h:per * (h + 1)], z64], axis=0) for h in range(HEADS)]
    rot = [jnp.concatenate([z128, _rot_half_rows(wt[per * h + HEAD_DIM:per * (h + 1)]), z64], axis=0)
           for h in range(HEADS)]
    return jnp.concatenate(main + rot, axis=0)


def _ext_kv(w_kv_up):
    r = w_kv_up.shape[0]
    z128 = jnp.zeros((r, 128), BF16)
    wkv = w_kv_up.reshape(r, HEADS, 2 * HEAD_DIM)
    kpad = [jnp.concatenate([wkv[:, h, :HEAD_DIM], z128], axis=1) for h in range(HEADS)]
    vals = [wkv[:, h, HEAD_DIM:] for h in range(HEADS)]
    return jnp.concatenate(kpad + vals, axis=1)


def _w_in_grad_slabs(dwt_h, dwt_m):
    d = dwt_h.shape[1]
    rows = (dwt_h.shape[0] + 512 + ROPE_DIM) // N_DEV
    padded = rows + (-rows) % 16

    def body(h_ref, m_ref, o_ref, stage_ref):
        stage_ref[0:2048, :] = h_ref[...]
        stage_ref[2048:2560, :] = m_ref[0:512, :]
        rot = m_ref[768 + 128:768 + 192, :]
        stage_ref[2560:2592, :] = m_ref[640:672, :] + rot[32:]
        stage_ref[2592:2624, :] = m_ref[672:704, :] - rot[:32]
        zero = jnp.zeros((padded - rows, d), F32)
        for i in range(N_DEV):
            o_ref[i] = jnp.concatenate([stage_ref[rows * i:rows * (i + 1), :], zero], axis=0).astype(BF16)

    return pl.pallas_call(body, name="w_in_grad_slabs", out_shape=jax.ShapeDtypeStruct((N_DEV, padded, d), BF16),
                          scratch_shapes=[pltpu.VMEM((N_DEV * rows, d), F32)], compiler_params=_params())(dwt_h, dwt_m)


def _grad_q_from_ext_t(dwq_ext_t):
    rows = []
    for h in range(HEADS):
        main, rot = dwq_ext_t[256 * h:256 * h + 256], dwq_ext_t[1024 + 256 * h:1280 + 256 * h]
        rows += [main[:128], main[128:192] + _unrot_half_rows(rot[128:192])]
    return jnp.concatenate(rows, axis=0)


def _grad_kv_from_ext(dwkv_ext):
    kvcols = []
    for h in range(HEADS):
        kvcols += [dwkv_ext[:, 256 * h:256 * h + 128], dwkv_ext[:, 1024 + 128 * h:1152 + 128 * h]]
    return jnp.concatenate(kvcols, axis=1)


SMALL_W = 6144 + 512 + 512 + 256 + 256 + 4 * 1024 + 128


def kernel(x, c, positions, w_ada, b_ada, w_in, hg_lower_bounds, hg_norm_w, mla_q_norm_w, w_q_up, mla_kv_norm_w, w_kv_up, w_out, ln1_g, ln1_b, w_mlp_in, w_mlp_out, ln2_g, ln2_b, loss_target, m_w_ada, m_b_ada, m_w_in, m_hg_lower_bounds, m_hg_norm_w, m_mla_q_norm_w, m_w_q_up, m_mla_kv_norm_w, m_w_kv_up, m_w_out, m_ln1_g, m_ln1_b, m_w_mlp_in, m_w_mlp_out, m_ln2_g, m_ln2_b, v_w_ada, v_b_ada, v_w_in, v_hg_lower_bounds, v_hg_norm_w, v_mla_q_norm_w, v_w_q_up, v_mla_kv_norm_w, v_w_kv_up, v_w_out, v_ln1_g, v_ln1_b, v_w_mlp_in, v_w_mlp_out, v_ln2_g, v_ln2_b):
    T = x.shape[1]
    me = 4 * lax.axis_index("x") + 2 * lax.axis_index("y") + lax.axis_index("c")
    xs, tgt = x[0], loss_target[0]
    transposed = ("w_in", "w_q_up")
    as_used = lambda n, a: a[0].T if n in transposed else a[0]
    big = {n: as_used(n, a) for n, a in dict(w_in=w_in, w_q_up=w_q_up, w_kv_up=w_kv_up, w_out=w_out,
                                              w_mlp_in=w_mlp_in, w_mlp_out=w_mlp_out).items()}
    names = list(big)

    bf = {n: big[n].astype(BF16) for n in ("w_in", "w_q_up", "w_kv_up", "w_out")}
    g_in, g_c = _gather_two_level([bf["w_in"], c], name="gather_w_in")
    c_all = g_c.reshape(N_DEV, D_MODEL)

    ada_cols = w_ada.shape[2]
    mod_part, cond = _mod_part(c_all, w_ada[0], lax.dynamic_slice(b_ada, (0, me * ada_cols), (1, ada_cols)))
    (mod_all,) = _exchange([mod_part], scatter=False, name="gather_mod")
    mod_row = lax.dynamic_slice(mod_all, (0, me, 0), (N_DEV, 1, ada_cols)).reshape(1, N_DEV * ada_cols)
    sh_a, sc_a, g_a, sh_m, sc_m, g_m = [mod_row[:, D_MODEL * i:D_MODEL * (i + 1)] for i in range(6)]

    w_in_ext = _ext_in_t(g_in)
    half = D_MODEL // 2
    z, (w1_top,) = _matmul(xs, w_in_ext, "NT", "in_proj", a_fn=_modulate, extras=(sc_a, sh_a), tn=3072,
                           exchange=_StagedGather(big["w_mlp_in"], rows=(0, half)))
    (o_raw, o_gated, s_prev), (g_q, g_kv, g_out) = _hgrn_fwd(
        z, hg_lower_bounds, hg_norm_w, exchange=_Exchange([bf["w_q_up"], bf["w_kv_up"], bf["w_out"]], False))
    wq_ext = _ext_q_t(g_q.reshape(N_DEV * g_q.shape[1], g_q.shape[2]))
    wkv_ext = _ext_kv(_cols_from_slabs(g_kv))
    w_out_full = g_out.reshape(D_MODEL, D_MODEL)
    inv_freq = 1.0 / (ROPE_THETA ** (jnp.arange(0, ROPE_DIM, 2, dtype=F32) / ROPE_DIM))
    zeros = lambda n: jnp.zeros((n,), F32)
    invf = jnp.concatenate([zeros(128), inv_freq, inv_freq, zeros(64)]).reshape(1, QK_PAD)
    m_rot = jnp.concatenate([zeros(128), jnp.ones((64,), F32), zeros(64)]).reshape(1, QK_PAD)
    q, k, v, c1, s1, cqn, ckvn = _mla_pre(z, positions.reshape(T, 1), invf, m_rot, wq_ext, wkv_ext,
                                          mla_q_norm_w, mla_kv_norm_w)[0]
    (o_mla, lse), (w1_bot, w2) = _attn_fwd(
        q, k, v, exchange=[_StagedGather(big["w_mlp_in"], 0.85, rows=(half, half)),
                           _StagedGather(big["w_mlp_out"], 0.85)])
    w1 = (w1_top, w1_bot)
    mix, xhat1, rstd1, u2 = _mix_ln1(o_gated, o_mla, w_out_full, xs, g_a, ln1_g, ln1_b, sc_m, sh_m)[0]
    r, dr2, dh, small_mlp_fwd = _mlp_fwd(u2, w1, w2, xhat1, ln1_g, ln1_b, g_m, ln2_g, ln2_b, tgt)

    dhpre, dr1, dmix, small_mlp_bwd = _mlp_bwd(dh, w1, w2, r, dr2, xhat1, rstd1, mix, ln1_g, ln1_b, sc_m, g_a)
    received = {}
    dw2 = _matmul(r, dh, "TN", "wgrad_mlp_out", out_dtype=BF16, a_fn=_square, tm=1024, tk=2048)
    dw1 = _matmul(u2, dhpre, "TN", "wgrad_mlp_in", out_dtype=BF16, tm=1024, tk=2048, out_slabs=N_DEV)
    dmixcat = _matmul(dmix, w_out_full, "NT", "dgrad_out", tm=1024)
    dw_out = jnp.concatenate([_matmul(o_gated, dmix, "TN", "wgrad_out_hg", out_dtype=BF16, tk=2048),
                              _matmul(o_mla, dmix, "TN", "wgrad_out_mla", out_dtype=BF16, tk=2048)], axis=0)
    (dz_h, small_hgrn), (received["w_out"],) = _hgrn_bwd(
        dmixcat, z, o_raw, s_prev, hg_lower_bounds, hg_norm_w,
        exchange=_Exchange([dw_out.reshape(N_DEV, D_MODEL // N_DEV, D_MODEL)], True))
    (dq, dk, dv), (received["w_mlp_in"], received["w_mlp_out"]) = _attn_bwd(
        q, k, v, dmixcat, o_mla, lse,
        exchange=_Exchange([dw1, dw2.reshape(N_DEV, dw2.shape[0] // N_DEV, D_MODEL)], True))
    dz_m, dq_ext, dkv_ext, small_mla = _mla_bwd(dq, dk, dv, z, c1, s1, wq_ext, wkv_ext, mla_q_norm_w, mla_kv_norm_w)
    dwq_t = _grad_q_from_ext_t(_matmul(dq_ext, cqn, "TN", "wgrad_q_up", tm=1024, tk=2048))
    dwkv = _grad_kv_from_ext(_matmul(ckvn, dkv_ext, "TN", "wgrad_kv_up", tn=1536, tk=2048))
    qkv_slabs = [dwq_t.reshape((N_DEV, dwq_t.shape[0] // N_DEV, dwq_t.shape[1])).astype(BF16),
                 _slabs_from_cols(dwkv).astype(BF16)]
    dwt_h, (received["w_q_up"], received["w_kv_up"]) = _matmul(
        dz_h, xs, "TN", "wgrad_in_h", b_fn=_modulate, extras=(sc_a, sh_a), tm=1024, tk=2048,
        exchange=_Exchange(qkv_slabs, True))
    dwt_m = _matmul(dz_m, xs, "TN", "wgrad_in_m", b_fn=_modulate, extras=(sc_a, sh_a), tm=1024, tk=2048)
    in_slabs = _w_in_grad_slabs(dwt_h, dwt_m)
    (grad_x, small_in), (received["w_in"],) = _input_bwd(
        dz_h, dz_m, w_in_ext, xs, dr1, sc_a, exchange=_StagedScatter(in_slabs))

    small = jnp.concatenate([small_in, small_mlp_bwd[:, :3 * D_MODEL], small_mlp_fwd[:, :D_MODEL], small_hgrn,
                             small_mla, small_mlp_bwd[:, 3 * D_MODEL:], small_mlp_fwd[:, D_MODEL:]], axis=1)
    assert small.shape == (1, SMALL_W)
    (small_all,) = _exchange([small], scatter=False, name="gather_small")

    moments = dict(w_in=(m_w_in, v_w_in), w_q_up=(m_w_q_up, v_w_q_up), w_kv_up=(m_w_kv_up, v_w_kv_up),
                   w_out=(m_w_out, v_w_out), w_mlp_in=(m_w_mlp_in, v_w_mlp_in), w_mlp_out=(m_w_mlp_out, v_w_mlp_out))
    res = {}
    for n in names:
        res[n] = _adam(received[n], big[n], as_used(n, moments[n][0]), as_used(n, moments[n][1]), name="adam_" + n)
    dmod_cols = lax.dynamic_slice(small_all.reshape(N_DEV, SMALL_W), (0, me * ada_cols), (N_DEV, ada_cols))
    cond_t = cond.T

    def ada_grad(ct_ref, dm_ref):
        g = ct_ref[:, 0:1] * dm_ref[0:1, :]
        for b in range(1, N_DEV):
            g = g + ct_ref[:, b:b + 1] * dm_ref[b:b + 1, :]
        return g

    res["w_ada"] = _adam(None, w_ada[0], m_w_ada[0], v_w_ada[0], name="adam_w_ada", g_fn=ada_grad,
                         g_extra=(cond_t, dmod_cols))

    small_params = [("b_ada", b_ada, m_b_ada, v_b_ada, 0),
                    ("hg_lower_bounds", hg_lower_bounds, m_hg_lower_bounds, v_hg_lower_bounds, 6144),
                    ("hg_norm_w", hg_norm_w, m_hg_norm_w, v_hg_norm_w, 6656),
                    ("mla_q_norm_w", mla_q_norm_w, m_mla_q_norm_w, v_mla_q_norm_w, 7168),
                    ("mla_kv_norm_w", mla_kv_norm_w, m_mla_kv_norm_w, v_mla_kv_norm_w, 7424),
                    ("ln1_g", ln1_g, m_ln1_g, v_ln1_g, 7680), ("ln1_b", ln1_b, m_ln1_b, v_ln1_b, 8704),
                    ("ln2_g", ln2_g, m_ln2_g, v_ln2_g, 9728), ("ln2_b", ln2_b, m_ln2_b, v_ln2_b, 10752)]
    loss_row, small_res = _adam_small(small_all, [p[1:] for p in small_params])
    for p, r4 in zip(small_params, small_res):
        res[p[0]] = r4
    loss = loss_row[0, 0]

    order = ["w_ada", "b_ada", "w_in", "hg_lower_bounds", "hg_norm_w", "mla_q_norm_w", "w_q_up", "mla_kv_norm_w",
             "w_kv_up", "w_out", "ln1_g", "ln1_b", "w_mlp_in", "w_mlp_out", "ln2_g", "ln2_b"]
    def as_given(n, a):
        if n in transposed:
            a = a.T
        return a[None] if n in big or n == "w_ada" else a

    shaped = {n: tuple(as_given(n, a) for a in res[n]) for n in order}
    outs = [loss, grad_x.reshape(1, T, D_MODEL)]
    for i in range(4):
        outs += [shaped[n][i] for n in order]
    return tuple(outs)
```

```python
import functools

import jax
import jax.numpy as jnp
import numpy as np
from jax import lax
from jax.experimental import pallas as pl
from jax.experimental.pallas import tpu as pltpu

F32, BF16 = jnp.float32, jnp.bfloat16
N_DEV = 8
D_MODEL = 1024
HEADS = 4
HEAD_DIM = 128
ROPE_DIM = 64
QK_PAD = 256
CHUNK = 64
ROPE_THETA = 10000.0
RMS_EPS = 1e-6
LN_EPS = 1e-5
ALPHA = 2.0 ** 0.25
ATT_SCALE = (HEAD_DIM + ROPE_DIM) ** -0.5
LN2 = float(np.log(2.0))
Q_PRESCALE = ATT_SCALE / LN2
ADAM_LR, ADAM_B1, ADAM_B2, ADAM_EPS, ADAM_WD, ADAM_STEP = 0.001, 0.9, 0.999, 1e-08, 0.01, 10
NEG_BIG = -1e30

ROW_TILE = 512
ATT_TILE = 512
HGRN_GROUP = 8
ATT_BWD_GROUP = 4
MLP_SLABS = 4
VMEM_LIMIT = 56 * 2 ** 20

NN = (((1,), (0,)), ((), ()))
NT = (((1,), (1,)), ((), ()))
TN = (((0,), (0,)), ((), ()))


def _dot(a, b, dims=NN):
    return lax.dot_general(a, b, dims, preferred_element_type=F32)


def _bdot(a, b, dims=NN):
    return lax.dot_general(a.astype(BF16), b.astype(BF16), dims, preferred_element_type=F32)


def _hdot(a, b, dims=NN):
    return lax.dot_general(a, b, dims, precision=lax.Precision.HIGHEST, preferred_element_type=F32)


def _params():
    return pltpu.CompilerParams(vmem_limit_bytes=VMEM_LIMIT)


def _sigmoid(x):
    return 1.0 / (1.0 + jnp.exp(-x))


def _rowsum(x):
    return jnp.sum(x, axis=0, keepdims=True)


def _lanemean(x):
    return jnp.mean(x, axis=-1, keepdims=True)


def _full(shape):
    nd = len(shape)
    return pl.BlockSpec(shape, lambda *_: (0,) * nd)


class _Exchange:
    def __init__(self, arrs, scatter):
        self.arrs, self.scatter, self.n, self.aliases, self.middle_at = list(arrs), scatter, len(arrs), [], 0.5
        self.out_shape = [jax.ShapeDtypeStruct((N_DEV,) + (a.shape[1:] if scatter else a.shape), a.dtype)
                          for a in self.arrs]
        n = self.n
        self.scratch = [pltpu.SemaphoreType.DMA((n, N_DEV - 1)), pltpu.SemaphoreType.DMA((n, N_DEV - 1)),
                        pltpu.SemaphoreType.DMA((n,))]

    def _copies(self, ins, outs, sems):
        send_sems, recv_sems, loc_sems = sems
        x, y, c = lax.axis_index("x"), lax.axis_index("y"), lax.axis_index("c")
        me = 4 * x + 2 * y + c
        copies = []
        for k in range(self.n):
            src_of = (lambda i, k=k: ins[k].at[i]) if self.scatter else (lambda i, k=k: ins[k])
            copies.append((pltpu.make_async_copy(src_of(me), outs[k].at[me], loc_sems.at[k]), None))
            for p in range(1, N_DEV):
                px = (1 - x) if p & 4 else x
                py = (1 - y) if p & 2 else y
                pc = (1 - c) if p & 1 else c
                peer = 4 * px + 2 * py + pc
                both = dict(send_sem=send_sems.at[k, p - 1], recv_sem=recv_sems.at[k, p - 1],
                            device_id=(px, py, pc), device_id_type=pl.DeviceIdType.MESH)
                send = pltpu.make_async_remote_copy(src_ref=src_of(peer), dst_ref=outs[k].at[me], **both)
                recv = pltpu.make_async_remote_copy(src_ref=src_of(peer), dst_ref=outs[k].at[peer], **both)
                copies.append((send, recv))
        return copies

    def start(self, ins, outs, sems):
        for first, _ in self._copies(ins, outs, sems):
            first.start()

    def middle(self, ins, outs, sems):
        pass

    def wait(self, ins, outs, sems):
        for first, recv in self._copies(ins, outs, sems):
            if recv is None:
                first.wait()
            else:
                recv.wait_recv()
                first.wait_send()


class _StagedGather:
    def __init__(self, arr, middle_at=0.8, rows=None):
        self.r0, n = rows if rows else (0, arr.shape[0])
        block = (n,) + arr.shape[1:]
        self.arrs, self.aliases, self.middle_at = [arr], [], middle_at
        self.out_shape = [jax.ShapeDtypeStruct((N_DEV,) + block, BF16)]
        self.scratch = [pltpu.VMEM((N_DEV,) + block, BF16), pltpu.SemaphoreType.DMA((7,)),
                        pltpu.SemaphoreType.DMA((7,)), pltpu.SemaphoreType.DMA((2,)), pltpu.VMEM(block, arr.dtype)]

    def _parts(self, scr):
        stage, send_sems, recv_sems, loc_sems = scr[:4]
        x, y, c = lax.axis_index("x"), lax.axis_index("y"), lax.axis_index("c")
        me, sibling = (x, y, c), (x, y, 1 - c)
        chips = [(1 - x, y), (x, 1 - y), (1 - x, 1 - y)]

        def copy(j, block, to):
            px, py, pc = block
            slot = stage.at[4 * px + 2 * py + pc]
            return pltpu.make_async_remote_copy(src_ref=slot, dst_ref=slot, send_sem=send_sems.at[j],
                                                recv_sem=recv_sems.at[j], device_id=to,
                                                device_id_type=pl.DeviceIdType.MESH)

        return stage, loc_sems, me, sibling, chips, c, copy

    def start(self, ins, outs, scr):
        stage, loc_sems, me, sibling, chips, c, copy = self._parts(scr)
        x, y, _ = me
        raw = scr[4]
        own = pltpu.make_async_copy(ins[0].at[pl.ds(self.r0, raw.shape[0])], raw, loc_sems.at[0])
        own.start()
        own.wait()
        stage[4 * x + 2 * y + c] = raw[...].astype(BF16)
        copy(0, me, sibling).start()
        for j, chip in enumerate(chips):
            copy(1 + j, me, (*chip, c)).start()

    def middle(self, ins, outs, scr):
        stage, loc_sems, me, sibling, chips, c, copy = self._parts(scr)
        for j, chip in enumerate(chips):
            copy(1 + j, (*chip, c), me).wait_recv()
            copy(4 + j, (*chip, c), sibling).start()

    def wait(self, ins, outs, scr):
        stage, loc_sems, me, sibling, chips, c, copy = self._parts(scr)
        copy(0, sibling, me).wait_recv()
        for j, chip in enumerate(chips):
            copy(4 + j, (*chip, 1 - c), me).wait_recv()
        copy(0, me, sibling).wait_send()
        for j, chip in enumerate(chips):
            copy(1 + j, me, (*chip, c)).wait_send()
            copy(4 + j, (*chip, c), sibling).wait_send()
        whole = pltpu.make_async_copy(stage, outs[0], loc_sems.at[1])
        whole.start()
        whole.wait()


class _StagedScatter:
    def __init__(self, slabs, middle_at=0.2):
        _, r, c = slabs.shape
        self.arrs, self.aliases, self.middle_at = [slabs], [], middle_at
        self.out_shape = [jax.ShapeDtypeStruct((4, r, c), slabs.dtype)]
        self.scratch = [pltpu.VMEM((N_DEV, r, c), slabs.dtype), pltpu.VMEM((4, r, c), slabs.dtype),
                        pltpu.VMEM((3, r, c), slabs.dtype), pltpu.SemaphoreType.DMA((4,)), pltpu.SemaphoreType.DMA((4,)),
                        pltpu.SemaphoreType.DMA((3,)), pltpu.SemaphoreType.DMA((3,)), pltpu.SemaphoreType.DMA((4,))]

    def _parts(self, scr):
        stage, from_sib, from_chips, sib_send, sib_recv, ici_send, ici_recv, loc_sems = scr
        x, y, c = lax.axis_index("x"), lax.axis_index("y"), lax.axis_index("c")
        chips = [(1 - x, y), (x, 1 - y), (1 - x, 1 - y)]

        def to_sibling(j):
            return pltpu.make_async_remote_copy(src_ref=stage.at[2 * j + 1 - c], dst_ref=from_sib.at[j],
                                                send_sem=sib_send.at[j], recv_sem=sib_recv.at[j],
                                                device_id=(x, y, 1 - c), device_id_type=pl.DeviceIdType.MESH)

        def to_chip(k):
            px, py = chips[k]
            return pltpu.make_async_remote_copy(src_ref=stage.at[4 * px + 2 * py + c], dst_ref=from_chips.at[k],
                                                send_sem=ici_send.at[k], recv_sem=ici_recv.at[k],
                                                device_id=(px, py, c), device_id_type=pl.DeviceIdType.MESH)

        return stage, from_sib, from_chips, loc_sems, (x, y, c), chips, to_sibling, to_chip

    def start(self, ins, outs, scr):
        stage, _, _, loc_sems, _, _, to_sibling, _ = self._parts(scr)
        load = pltpu.make_async_copy(ins[0], stage, loc_sems.at[0])
        load.start()
        load.wait()
        for j in range(4):
            to_sibling(j).start()

    def middle(self, ins, outs, scr):
        stage, from_sib, _, _, (x, y, c), _, to_sibling, to_chip = self._parts(scr)
        for j in range(4):
            to_sibling(j).wait_recv()
            mine = stage.at[2 * j + c]
            mine[...] = (mine[...].astype(F32) + from_sib[j].astype(F32)).astype(mine.dtype)
        for k in range(3):
            to_chip(k).start()

    def wait(self, ins, outs, scr):
        stage, _, from_chips, loc_sems, (x, y, c), chips, to_sibling, to_chip = self._parts(scr)
        writes = [pltpu.make_async_copy(stage.at[4 * x + 2 * y + c], outs[0].at[2 * x + y], loc_sems.at[0])]
        for k, (px, py) in enumerate(chips):
            to_chip(k).wait_recv()
            writes.append(pltpu.make_async_copy(from_chips.at[k], outs[0].at[2 * px + py], loc_sems.at[1 + k]))
        for w in writes:
            w.start()
        for j in range(4):
            to_sibling(j).wait_send()
        for k in range(3):
            to_chip(k).wait_send()
        for w in writes:
            w.wait()


def _call(body, name, args, out_shape, grid=(), in_specs=(), out_specs=(), scratch_shapes=(), exchange=None):
    if exchange is None:
        return pl.pallas_call(body, name=name, grid=grid, in_specs=list(in_specs), out_specs=list(out_specs),
                              out_shape=list(out_shape), scratch_shapes=list(scratch_shapes),
                              compiler_params=_params())(*args), None
    exs = list(exchange) if isinstance(exchange, (list, tuple)) else [exchange]
    ni, no, ns = len(args), len(out_shape), len(scratch_shapes)
    nxi, nxo = sum(len(e.arrs) for e in exs), sum(len(e.out_shape) for e in exs)
    steps = int(np.prod(grid))
    mid_step = lambda e: min(max(int(steps * e.middle_at), 1), steps - 1)
    aliases, iat, oat = {}, ni, no
    for e in exs:
        for src, dst in e.aliases:
            aliases[iat + src] = oat + dst
        iat, oat = iat + len(e.arrs), oat + len(e.out_shape)

    def wrapped(*refs):
        a, xi = refs[:ni], refs[ni:ni + nxi]
        o, xo = refs[ni + nxi:ni + nxi + no], refs[ni + nxi + no:ni + nxi + no + nxo]
        s, xs = refs[ni + nxi + no + nxo:ni + nxi + no + nxo + ns], refs[ni + nxi + no + nxo + ns:]
        parts, iat, oat, sat = [], 0, 0, 0
        for e in exs:
            parts.append((e, xi[iat:iat + len(e.arrs)], xo[oat:oat + len(e.out_shape)], xs[sat:sat + len(e.scratch)]))
            iat, oat, sat = iat + len(e.arrs), oat + len(e.out_shape), sat + len(e.scratch)
        step = 0
        for d, g in enumerate(grid):
            step = step * g + pl.program_id(d)

        @pl.when(step == 0)
        def _():
            for e, ins, outs, sems in parts:
                e.start(ins, outs, sems)

        for at_step in sorted({mid_step(e) for e in exs}):
            @pl.when(step == at_step)
            def _():
                for e, ins, outs, sems in parts:
                    if mid_step(e) == at_step:
                        e.middle(ins, outs, sems)

        body(*a, *o, *s)

        @pl.when(step == steps - 1)
        def _():
            for e, ins, outs, sems in parts:
                e.wait(ins, outs, sems)

    hbm = pl.BlockSpec(memory_space=pltpu.HBM)
    res = pl.pallas_call(
        wrapped, name=name, grid=grid, in_specs=list(in_specs) + [hbm] * nxi, out_specs=list(out_specs) + [hbm] * nxo,
        out_shape=list(out_shape) + [o_ for e in exs for o_ in e.out_shape],
        scratch_shapes=list(scratch_shapes) + [s_ for e in exs for s_ in e.scratch],
        input_output_aliases=aliases, compiler_params=_params())(*args, *[a_ for e in exs for a_ in e.arrs])
    return res[:no], res[no:]


def _gather_two_level(arrs, name):
    n = len(arrs)
    out_shape = [jax.ShapeDtypeStruct((N_DEV,) + a.shape, a.dtype) for a in arrs]

    def body(*refs):
        ins, outs = refs[:n], refs[n:2 * n]
        send_sems, recv_sems, loc_sems = refs[2 * n:]
        x, y, c = lax.axis_index("x"), lax.axis_index("y"), lax.axis_index("c")
        me, sibling = (x, y, c), (x, y, 1 - c)
        chips = [(1 - x, y), (x, 1 - y), (1 - x, 1 - y)]

        def copy(k, j, block, to, src=None):
            px, py, pc = block
            dst = outs[k].at[4 * px + 2 * py + pc]
            return pltpu.make_async_remote_copy(src_ref=dst if src is None else src, dst_ref=dst,
                                                send_sem=send_sems.at[k, j], recv_sem=recv_sems.at[k, j],
                                                device_id=to, device_id_type=pl.DeviceIdType.MESH)

        mine = [pltpu.make_async_copy(ins[k], outs[k].at[4 * x + 2 * y + c], loc_sems.at[k]) for k in range(n)]
        first = []
        for k in range(n):
            mine[k].start()
            first.append(copy(k, 0, me, sibling, src=ins[k]))
            first += [copy(k, 1 + j, me, (*chip, c), src=ins[k]) for j, chip in enumerate(chips)]
        for cp in first:
            cp.start()
        passed = []
        for j, chip in enumerate(chips):
            for k in range(n):
                copy(k, 1 + j, (*chip, c), me).wait_recv()
                passed.append(copy(k, 4 + j, (*chip, c), sibling))
                passed[-1].start()
        for k in range(n):
            copy(k, 0, sibling, me).wait_recv()
            for j, chip in enumerate(chips):
                copy(k, 4 + j, (*chip, 1 - c), me).wait_recv()
        for cp in first + passed:
            cp.wait_send()
        for cp in mine:
            cp.wait()

    vmem = pl.BlockSpec(memory_space=pltpu.VMEM)
    return pl.pallas_call(body, name=name, out_shape=out_shape, in_specs=[vmem] * n, out_specs=[vmem] * n,
                          scratch_shapes=[pltpu.SemaphoreType.DMA((n, 7)), pltpu.SemaphoreType.DMA((n, 7)),
                                          pltpu.SemaphoreType.DMA((n,))], compiler_params=_params())(*arrs)


def _exchange(arrs, scatter, name):
    ex = _Exchange(arrs, scatter)

    def body(*refs):
        ins, outs, sems = refs[:ex.n], refs[ex.n:2 * ex.n], refs[2 * ex.n:]
        ex.start(ins, outs, sems)
        ex.wait(ins, outs, sems)

    hbm = pl.BlockSpec(memory_space=pltpu.HBM)
    return pl.pallas_call(body, name=name, out_shape=ex.out_shape, in_specs=[hbm] * ex.n, out_specs=[hbm] * ex.n,
                          scratch_shapes=ex.scratch)(*ex.arrs)


def _matmul(a, b, mode, name, out_dtype=F32, tm=512, tn=1024, tk=1024, a_fn=None, b_fn=None, extras=(),
            out_slabs=None, exchange=None):
    assert not (a_fn and b_fn) and not (b_fn and mode == "NT")
    if mode == "NN":
        (M, K), N = a.shape, b.shape[1]
    elif mode == "NT":
        (M, K), N = a.shape, b.shape[0]
    else:
        (K, M), N = a.shape, b.shape[1]
    slab_w = N // out_slabs if out_slabs else None
    if out_slabs:
        tn = max(slab_w, min(tn, N) // slab_w * slab_w)
    tm, tn, tk = min(tm, M), min(tn, N), min(tk, K)
    assert M % tm == 0 and N % tn == 0 and K % tk == 0, (name, M, N, K)
    nk = K // tk
    dims = {"NN": NN, "NT": NT, "TN": TN}[mode]
    ne = len(extras)

    def body(a_ref, b_ref, *rest):
        e_refs, o_ref, acc_ref = rest[:ne], rest[ne], rest[ne + 1]
        k = pl.program_id(2)

        @pl.when(k == 0)
        def _():
            acc_ref[...] = jnp.zeros_like(acc_ref)

        at, bt = a_ref[...], b_ref[...]
        if a_fn is not None:
            at = a_fn(at.astype(F32), *[e[...] for e in e_refs])
        if b_fn is not None:
            bt = b_fn(bt.astype(F32), *[e[...] for e in e_refs])
        acc_ref[...] += _bdot(at, bt, dims)

        @pl.when(k == nk - 1)
        def _():
            if out_slabs:
                for s in range(tn // slab_w):
                    o_ref[s] = acc_ref[:, s * slab_w:(s + 1) * slab_w].astype(out_dtype)
            else:
                o_ref[...] = acc_ref[...].astype(out_dtype)

    if mode == "TN":
        a_spec = pl.BlockSpec((tk, tm), lambda i, j, k: (k, i))
        e_spec = pl.BlockSpec((1, tm), lambda i, j, k: (0, i))
    else:
        a_spec = pl.BlockSpec((tm, tk), lambda i, j, k: (i, k))
        e_spec = pl.BlockSpec((1, tk), lambda i, j, k: (0, k))
    if mode == "NT":
        b_spec = pl.BlockSpec((tn, tk), lambda i, j, k: (j, k))
    else:
        b_spec = pl.BlockSpec((tk, tn), lambda i, j, k: (k, j))
    if b_fn is not None:
        e_spec = pl.BlockSpec((1, tn), lambda i, j, k: (0, j))
    if out_slabs:
        o_shape = jax.ShapeDtypeStruct((out_slabs, M, slab_w), out_dtype)
        o_spec = pl.BlockSpec((tn // slab_w, tm, slab_w), lambda i, j, k: (j, i, 0))
    else:
        o_shape = jax.ShapeDtypeStruct((M, N), out_dtype)
        o_spec = pl.BlockSpec((tm, tn), lambda i, j, k: (i, j))
    (out,), got = _call(body, name, (a, b, *extras), [o_shape], grid=(M // tm, N // tn, nk),
                        in_specs=[a_spec, b_spec] + [e_spec] * ne, out_specs=[o_spec],
                        scratch_shapes=[pltpu.VMEM((tm, tn), F32)], exchange=exchange)
    return out if exchange is None else (out, got)


def _modulate(x, sc, sh):
    return x * (1.0 + sc) + sh


def _square(x):
    return x * x


def _mod_part(c_all, w_ada_s, b_s):
    def body(c_ref, w_ref, b_ref, mod_ref, cond_ref):
        cv = c_ref[...]
        cond = cv * _sigmoid(cv)
        cond_ref[...] = cond
        mod_ref[...] = _bdot(cond, w_ref[...]) + b_ref[...]

    return pl.pallas_call(
        body, name="mod_part",
        out_shape=[jax.ShapeDtypeStruct((N_DEV, w_ada_s.shape[1]), F32), jax.ShapeDtypeStruct(c_all.shape, F32)],
        compiler_params=_params(),
    )(c_all, w_ada_s, b_s)


def _rms_fwd(x, w):
    rs = lax.rsqrt(_lanemean(x * x) + RMS_EPS)
    return x * rs * w, rs


def _rms_bwd(x, rs, w, dy):
    xhat = x * rs
    dxh = dy * w
    return rs * (dxh - xhat * _lanemean(dxh * xhat)), dy * xhat


def _mla_pre(z, pos_col, invf, m_rot, wq_ext, wkv_ext, qnw, kvnw, exchange=None):
    T = z.shape[0]
    tm = min(ROW_TILE, T)

    def body(z_ref, pos_ref, invf_ref, mrot_ref, wq_ref, wkv_ref, qnw_ref, kvnw_ref,
             q_ref, k_ref, v_ref, c1_ref, s1_ref, cqn_ref, ckvn_ref):
        hi = slice(HEAD_DIM, QK_PAD)
        ang = pos_ref[...].astype(F32) * invf_ref[:, hi]
        c1 = jnp.concatenate([jnp.ones((tm, HEAD_DIM), F32), mrot_ref[:, hi] * jnp.cos(ang)], axis=1)
        s1 = jnp.concatenate([jnp.zeros((tm, HEAD_DIM), F32), mrot_ref[:, hi] * jnp.sin(ang)], axis=1)
        c1_ref[...] = c1
        s1_ref[...] = s1
        cqn, _ = _rms_fwd(z_ref[:, 0:256], qnw_ref[...])
        ckvn, _ = _rms_fwd(z_ref[:, 256:512], kvnw_ref[...])
        cqn_ref[...] = cqn.astype(BF16)
        ckvn_ref[...] = ckvn.astype(BF16)
        qe = _bdot(cqn, wq_ref[...], NT)
        kve = _bdot(ckvn, wkv_ref[...])
        k_rope = z_ref[:, 512:768] * c1 + z_ref[:, 768:1024] * s1
        for h in range(HEADS):
            q_ref[h] = ((qe[:, 256 * h:256 * h + 256] * c1 + qe[:, 1024 + 256 * h:1280 + 256 * h] * s1)
                        * Q_PRESCALE).astype(BF16)
            k_ref[h] = (kve[:, 256 * h:256 * h + 256] + k_rope).astype(BF16)
            v_ref[h] = kve[:, 1024 + 128 * h:1152 + 128 * h].astype(BF16)

    row = lambda i: (i, 0)
    head = lambda i: (0, i, 0)
    return _call(
        body, "mla_pre", (z, pos_col, invf, m_rot, wq_ext, wkv_ext, qnw, kvnw), grid=(T // tm,),
        in_specs=[pl.BlockSpec((tm, 1024), lambda i: (i, 2)), pl.BlockSpec((tm, 1), row),
                  _full((1, 256)), _full((1, 256)), _full(wq_ext.shape), _full(wkv_ext.shape),
                  _full((1, 256)), _full((1, 256))],
        out_specs=[pl.BlockSpec((HEADS, tm, QK_PAD), head), pl.BlockSpec((HEADS, tm, QK_PAD), head),
                   pl.BlockSpec((HEADS, tm, HEAD_DIM), head), pl.BlockSpec((tm, 256), row), pl.BlockSpec((tm, 256), row),
                   pl.BlockSpec((tm, 256), row), pl.BlockSpec((tm, 256), row)],
        out_shape=[jax.ShapeDtypeStruct((HEADS, T, QK_PAD), BF16), jax.ShapeDtypeStruct((HEADS, T, QK_PAD), BF16),
                   jax.ShapeDtypeStruct((HEADS, T, HEAD_DIM), BF16), jax.ShapeDtypeStruct((T, 256), F32),
                   jax.ShapeDtypeStruct((T, 256), F32), jax.ShapeDtypeStruct((T, 256), BF16),
                   jax.ShapeDtypeStruct((T, 256), BF16)], exchange=exchange)


def _mla_bwd(dq, dk, dv, z, c1, s1, wq_ext, wkv_ext, qnw, kvnw):
    T = z.shape[0]
    tm = min(ROW_TILE, T)

    def body(dq_ref, dk_ref, dv_ref, z_ref, c1_ref, s1_ref, wq_ref, wkv_ref, qnw_ref, kvnw_ref,
             dz_ref, dqe_ref, dkve_ref, dnw_ref):
        @pl.when(pl.program_id(0) == 0)
        def _():
            dnw_ref[...] = jnp.zeros_like(dnw_ref)

        c1, s1 = c1_ref[...], s1_ref[...]
        dkpe = jnp.zeros((tm, QK_PAD), F32)
        for h in range(HEADS):
            dqh, dkh = dq_ref[h].astype(F32) * ATT_SCALE, dk_ref[h]
            dqe_ref[:, 256 * h:256 * h + 256] = (dqh * c1).astype(BF16)
            dqe_ref[:, 1024 + 256 * h:1280 + 256 * h] = (dqh * s1).astype(BF16)
            dkve_ref[:, 256 * h:256 * h + 256] = dkh
            dkve_ref[:, 1024 + 128 * h:1152 + 128 * h] = dv_ref[h]
            dkpe = dkpe + dkh.astype(F32)
        dcqn = _dot(dqe_ref[...], wq_ref[...])
        dckvn = _dot(dkve_ref[...], wkv_ref[...], NT)
        cq, ckv = z_ref[:, 0:256], z_ref[:, 256:512]
        _, rsq = _rms_fwd(cq, qnw_ref[...])
        _, rskv = _rms_fwd(ckv, kvnw_ref[...])
        dcq, wq_rows = _rms_bwd(cq, rsq, qnw_ref[...], dcqn)
        dckv, wkv_rows = _rms_bwd(ckv, rskv, kvnw_ref[...], dckvn)
        dnw_ref[:, 0:256] += _rowsum(wq_rows)
        dnw_ref[:, 256:512] += _rowsum(wkv_rows)
        dz_ref[:, 0:256] = dcq.astype(BF16)
        dz_ref[:, 256:512] = dckv.astype(BF16)
        dz_ref[:, 512:768] = (dkpe * c1).astype(BF16)
        dz_ref[:, 768:1024] = (dkpe * s1).astype(BF16)

    row = lambda i: (i, 0)
    head = lambda i: (0, i, 0)
    return pl.pallas_call(
        body, name="mla_bwd", grid=(T // tm,),
        in_specs=[pl.BlockSpec((HEADS, tm, QK_PAD), head), pl.BlockSpec((HEADS, tm, QK_PAD), head),
                  pl.BlockSpec((HEADS, tm, HEAD_DIM), head), pl.BlockSpec((tm, 1024), lambda i: (i, 2)),
                  pl.BlockSpec((tm, 256), row), pl.BlockSpec((tm, 256), row), _full(wq_ext.shape), _full(wkv_ext.shape),
                  _full((1, 256)), _full((1, 256))],
        out_specs=[pl.BlockSpec((tm, 1024), row), pl.BlockSpec((tm, 2048), row), pl.BlockSpec((tm, 1536), row),
                   _full((1, 512))],
        out_shape=[jax.ShapeDtypeStruct((T, 1024), BF16), jax.ShapeDtypeStruct((T, 2048), BF16),
                   jax.ShapeDtypeStruct((T, 1536), BF16), jax.ShapeDtypeStruct((1, 512), F32)],
        compiler_params=_params(),
    )(dq, dk, dv, z, c1, s1, wq_ext, wkv_ext, qnw, kvnw)


_HEAD_LANES = [slice(HEAD_DIM * h, HEAD_DIM * (h + 1)) for h in range(HEADS)]


def _lower_bound(lbraw_ref):
    a0, a1 = lbraw_ref[0:1, :], lbraw_ref[1:2, :]
    mx = jnp.maximum(a0, a1)
    e0, e1 = jnp.exp(a0 - mx), jnp.exp(a1 - mx)
    return e0 / (e0 + e1)


def _tri(lower):
    r = lax.broadcasted_iota(jnp.int32, (CHUNK, CHUNK), 0)
    c = lax.broadcasted_iota(jnp.int32, (CHUNK, CHUNK), 1)
    return (r >= c) if lower else (r <= c)


def _hgrn_gates(q, f, lb, tri_lo):
    sg = _sigmoid(f)
    forget = lb + (1.0 - lb) * sg
    k = 1.0 - forget
    b = _hdot(tri_lo.astype(F32), jnp.log(forget))
    b_ref, b_last = b[CHUNK // 2 - 1:CHUNK // 2, :], b[CHUNK - 1:CHUNK, :]
    e1, e2, e3, e4 = jnp.exp(b - b_ref), jnp.exp(b_ref - b), jnp.exp(b_last - b), jnp.exp(b)
    return dict(sg=sg, forget=forget, k=k, e1=e1, e2=e2, e3=e3, e4=e4, qa=q * e1, ka=k * e2, kl=k * e3, qb=q * e4,
                decay=jnp.exp(b_last))


def _hgrn_fwd(z, lbraw, nw, exchange=None):
    T = z.shape[0]
    G = min(HGRN_GROUP, T // CHUNK)
    rows = G * CHUNK
    n_chunks = T // CHUNK

    def body(q_ref, f_ref, i_ref, g_ref, lbraw_ref, nw_ref, oraw_ref, og_ref, sp_ref, st_ref):
        @pl.when(pl.program_id(0) == 0)
        def _():
            st_ref[...] = jnp.zeros_like(st_ref)

        lb_all = _lower_bound(lbraw_ref)
        tri_lo = _tri(True)

        def chunk(cc, carry):
            rs = pl.ds(pl.multiple_of(cc * CHUNK, CHUNK), CHUNK)
            t = _hgrn_gates(q_ref[rs, :], f_ref[rs, :], lb_all, tri_lo)
            v, gate = i_ref[rs, :], g_ref[rs, :]
            st = [st_ref[h] for h in range(HEADS)]
            a = [jnp.where(tri_lo, _bdot(t["qa"][:, s], t["ka"][:, s], NT), 0.0) for s in _HEAD_LANES]
            kv = [_bdot(v[:, s], t["kl"][:, s], TN) for s in _HEAD_LANES]
            o = [_bdot(a[h], v[:, s]) + _bdot(t["qb"][:, s], st[h], NT) for h, s in enumerate(_HEAD_LANES)]
            for h, s in enumerate(_HEAD_LANES):
                sp_ref[cc, h] = st[h]
                st_ref[h] = st[h] * t["decay"][:, s] + kv[h]
            oraw_ref[rs, :] = jnp.concatenate(o, axis=1)
            on = jnp.concatenate([_rms_fwd(o[h], nw_ref[:, s])[0] for h, s in enumerate(_HEAD_LANES)], axis=1)
            og_ref[rs, :] = (on * (gate * _sigmoid(gate))).astype(BF16)
            return carry

        lax.fori_loop(0, G, chunk, 0, unroll=4)

    col = lambda j: pl.BlockSpec((rows, 512), lambda r, j=j: (r, j))
    return _call(
        body, "hgrn_fwd", (z, z, z, z, lbraw, nw), grid=(T // rows,),
        in_specs=[col(0), col(1), col(2), col(3), _full((2, 512)), _full((1, 512))],
        out_specs=[col(0), col(0), pl.BlockSpec((G, HEADS, HEAD_DIM, HEAD_DIM), lambda r: (r, 0, 0, 0))],
        out_shape=[jax.ShapeDtypeStruct((T, 512), F32), jax.ShapeDtypeStruct((T, 512), BF16),
                   jax.ShapeDtypeStruct((n_chunks, HEADS, HEAD_DIM, HEAD_DIM), F32)],
        scratch_shapes=[pltpu.VMEM((HEADS, HEAD_DIM, HEAD_DIM), F32)], exchange=exchange)


def _hgrn_bwd(dmixcat, z, oraw, sprev, lbraw, nw, exchange=None):
    T = z.shape[0]
    G = min(HGRN_GROUP, T // CHUNK)
    rows = G * CHUNK
    ng = T // rows

    def body(dog_ref, q_ref, f_ref, i_ref, g_ref, oraw_ref, sp_ref, lbraw_ref, nw_ref,
             dz_ref, dsmall_ref, dst_ref):
        @pl.when(pl.program_id(0) == 0)
        def _():
            dst_ref[...] = jnp.zeros_like(dst_ref)
            dsmall_ref[...] = jnp.zeros_like(dsmall_ref)

        lb_all = _lower_bound(lbraw_ref)
        tri_lo, tri_up = _tri(True), _tri(False)
        rowid = lax.broadcasted_iota(jnp.int32, (CHUNK, HEADS * HEAD_DIM), 0)

        def chunk(it, carry):
            cc = G - 1 - it
            rs = pl.ds(pl.multiple_of(cc * CHUNK, CHUNK), CHUNK)
            heads = list(enumerate(_HEAD_LANES))
            cat = lambda parts: jnp.concatenate(parts, axis=1)
            per_head_mean = lambda x: cat([jnp.broadcast_to(_lanemean(x[:, s]), (CHUNK, HEAD_DIM)) for s in _HEAD_LANES])
            t = _hgrn_gates(q_ref[rs, :], f_ref[rs, :], lb_all, tri_lo)
            v, gate, o, dog, nw_all = i_ref[rs, :], g_ref[rs, :], oraw_ref[rs, :], dog_ref[rs, :], nw_ref[...]
            rs_o = lax.rsqrt(per_head_mean(o * o) + RMS_EPS)
            xhat = o * rs_o
            sgg = _sigmoid(gate)
            d_on = dog * (gate * sgg)
            dz_ref[rs, 1536:2048] = (dog * (xhat * nw_all) * (sgg * (1.0 + gate * (1.0 - sgg)))).astype(BF16)
            dxh = d_on * nw_all
            do = rs_o * (dxh - xhat * per_head_mean(dxh * xhat))
            dsmall_ref[:, 512:1024] += _rowsum(d_on * xhat)
            st = [sp_ref[cc, h] for h in range(HEADS)]
            dst = [dst_ref[h] for h in range(HEADS)]
            a = [jnp.where(tri_lo, _bdot(t["qa"][:, s], t["ka"][:, s], NT), 0.0) for s in _HEAD_LANES]
            da = [jnp.where(tri_lo, _bdot(do[:, s], v[:, s], NT), 0.0) for s in _HEAD_LANES]
            dqb = cat([_bdot(do[:, s], st[h]) for h, s in heads])
            dkl = cat([_bdot(v[:, s], dst[h]) for h, s in heads])
            dv_ = cat([_bdot(t["kl"][:, s], dst[h], NT) + _bdot(a[h], do[:, s], TN) for h, s in heads])
            dqa = cat([_bdot(da[h], t["ka"][:, s]) for h, s in heads])
            dka = cat([_bdot(da[h], t["qa"][:, s], TN) for h, s in heads])
            ddecay = cat([_rowsum(dst[h] * st[h]) for h in range(HEADS)])
            for h, s in heads:
                dst_ref[h] = dst[h] * t["decay"][:, s] + _bdot(do[:, s], t["qb"][:, s], TN)
            pa, pk, pb, pl_ = dqa * t["qa"], dka * t["ka"], dqb * t["qb"], dkl * t["kl"]
            db = pa - pk + pb - pl_
            db = db + jnp.where(rowid == CHUNK // 2 - 1, _rowsum(pk - pa), 0.0)
            db = db + jnp.where(rowid == CHUNK - 1, _rowsum(pl_) + ddecay * t["decay"], 0.0)
            dlogf = _hdot(tri_up.astype(F32), db)
            dforget = dlogf / t["forget"] - (dka * t["e2"] + dkl * t["e3"])
            sg = t["sg"]
            dz_ref[rs, 0:512] = (dqa * t["e1"] + dqb * t["e4"]).astype(BF16)
            dz_ref[rs, 512:1024] = (dforget * (1.0 - lb_all) * sg * (1.0 - sg)).astype(BF16)
            dz_ref[rs, 1024:1536] = dv_.astype(BF16)
            dsmall_ref[:, 0:512] += _rowsum(dforget * (1.0 - sg))
            return carry

        lax.fori_loop(0, G, chunk, 0, unroll=4)

    col = lambda j: pl.BlockSpec((rows, 512), lambda r, j=j: (ng - 1 - r, j))
    return _call(
        body, "hgrn_bwd", (dmixcat, z, z, z, z, oraw, sprev, lbraw, nw), grid=(ng,),
        in_specs=[col(0), col(0), col(1), col(2), col(3), col(0),
                  pl.BlockSpec((G, HEADS, HEAD_DIM, HEAD_DIM), lambda r: (ng - 1 - r, 0, 0, 0)),
                  _full((2, 512)), _full((1, 512))],
        out_specs=[pl.BlockSpec((rows, 2048), lambda r: (ng - 1 - r, 0)), _full((1, 1024))],
        out_shape=[jax.ShapeDtypeStruct((T, 2048), BF16), jax.ShapeDtypeStruct((1, 1024), F32)],
        scratch_shapes=[pltpu.VMEM((HEADS, HEAD_DIM, HEAD_DIM), F32)], exchange=exchange)


def _diag_mask(t):
    r = lax.broadcasted_iota(jnp.int32, (t, t), 0)
    c = lax.broadcasted_iota(jnp.int32, (t, t), 1)
    return r >= c


def _attn_fwd(q, k, v, exchange=None):
    _, T, _ = q.shape
    t = min(ATT_TILE, T)

    def body(q_ref, k_ref, v_ref, o_ref, lse_ref):
        i = pl.program_id(1)
        qb = q_ref[...]

        rows = lambda j: pl.ds(pl.multiple_of(j * t, t), t)

        def logits(j, masked):
            s = _dot(qb, k_ref[rows(j), :], NT)
            return jnp.where(_diag_mask(t), s, NEG_BIG) if masked else s

        def absorb(s, j, carry):
            m, l, acc = carry
            mn = jnp.maximum(m, jnp.max(s, axis=-1, keepdims=True))
            p = jnp.exp2(s - mn)
            al = jnp.exp2(m - mn)
            return mn, al * l + jnp.sum(p, axis=-1, keepdims=True), al * acc + _dot(p.astype(BF16), v_ref[rows(j), :])

        def pair(j0, carry, last_masked):
            s0, s1 = logits(j0, False), logits(j0 + 1, last_masked)
            return absorb(s1, j0 + 1, absorb(s0, j0, carry))

        init = (jnp.full((t, 1), NEG_BIG, F32), jnp.zeros((t, 1), F32), jnp.zeros((t, HEAD_DIM), F32))
        carry = lax.fori_loop(0, i // 2, lambda jj, c: pair(2 * jj, c, False), init)
        m, l, acc = lax.cond(i % 2 == 1, lambda c: pair(i - 1, c, True),
                             lambda c: absorb(logits(i, True), i, c), carry)
        o_ref[...] = acc / l
        lse_ref[...] = jnp.broadcast_to(m + jnp.log2(l), (t, HEAD_DIM))

    return _call(
        body, "attn_fwd", (q, k, v), grid=(HEADS, T // t),
        in_specs=[pl.BlockSpec((None, t, QK_PAD), lambda h, i: (h, i, 0)),
                  pl.BlockSpec((None, T, QK_PAD), lambda h, i: (h, 0, 0)),
                  pl.BlockSpec((None, T, HEAD_DIM), lambda h, i: (h, 0, 0))],
        out_specs=[pl.BlockSpec((t, HEAD_DIM), lambda h, i: (i, h)),
                   pl.BlockSpec((None, t, HEAD_DIM), lambda h, i: (h, i, 0))],
        out_shape=[jax.ShapeDtypeStruct((T, HEADS * HEAD_DIM), F32), jax.ShapeDtypeStruct((HEADS, T, HEAD_DIM), F32)],
        exchange=exchange)


def _attn_bwd(q, k, v, dmixcat, o, lse, exchange=None):
    _, T, _ = q.shape
    t = min(ATT_TILE, T)
    nq = T // t

    def body(q_ref, k_ref, v_ref, do_ref, o_ref, lse_ref, dq_ref, dk_ref, dv_ref, delta_ref, dq_acc):
        j = pl.program_id(1)

        @pl.when(j == 0)
        def _():
            dq_acc[...] = jnp.zeros_like(dq_acc)

            def fill(i, carry):
                rs = pl.ds(pl.multiple_of(i * t, t), t)
                delta_ref[rs, :] = jnp.broadcast_to(
                    jnp.sum(do_ref[rs, :] * o_ref[rs, :], axis=-1, keepdims=True), (t, HEAD_DIM))
                return carry

            lax.fori_loop(0, nq, fill, 0)

        kb, vb = k_ref[...], v_ref[...]

        def steps(blocks, carry):
            dk, dv = carry
            rs = [pl.ds(pl.multiple_of(i * t, t), t) for i, _ in blocks]
            qb = [q_ref[r, :] for r in rs]
            dob = [do_ref[r, :].astype(BF16) for r in rs]
            s = [_dot(b, kb, NT) for b in qb]
            dp = [_dot(b, vb, NT) for b in dob]
            for n, (_, masked) in enumerate(blocks):
                p = jnp.exp2(s[n] - lse_ref[rs[n], 0:1])
                if masked:
                    p = jnp.where(_diag_mask(t), p, 0.0)
                ds = (p * (dp[n] - delta_ref[rs[n], 0:1])).astype(BF16)
                dq_acc[rs[n], :] += _dot(ds, kb)
                dk = dk + _dot(ds, qb[n], TN)
                dv = dv + _dot(p.astype(BF16), dob[n], TN)
            return dk, dv

        zero = (jnp.zeros((t, QK_PAD), F32), jnp.zeros((t, HEAD_DIM), F32))
        rest = nq - 1 - j
        group = ATT_BWD_GROUP
        lead = rest % group
        carry = lax.switch(lead, [lambda c, n=n: steps([(j, True)] + [(j + 1 + m, False) for m in range(n)], c)
                                  for n in range(group)], zero)
        first = j + 1 + lead
        dk, dv = lax.fori_loop(0, rest // group,
                               lambda n, c: steps([(first + group * n + m, False) for m in range(group)], c), carry)
        dk_ref[...] = (dk * LN2).astype(BF16)
        dv_ref[...] = dv.astype(BF16)

        @pl.when(j == nq - 1)
        def _():
            dq_ref[...] = dq_acc[...].astype(BF16)

    return _call(
        body, "attn_bwd", (q, k, v, dmixcat, o, lse), grid=(HEADS, nq),
        in_specs=[pl.BlockSpec((None, T, QK_PAD), lambda h, j: (h, 0, 0)),
                  pl.BlockSpec((None, t, QK_PAD), lambda h, j: (h, j, 0)),
                  pl.BlockSpec((None, t, HEAD_DIM), lambda h, j: (h, j, 0)),
                  pl.BlockSpec((T, HEAD_DIM), lambda h, j: (0, HEADS + h)),
                  pl.BlockSpec((T, HEAD_DIM), lambda h, j: (0, h)),
                  pl.BlockSpec((None, T, HEAD_DIM), lambda h, j: (h, 0, 0))],
        out_specs=[pl.BlockSpec((None, T, QK_PAD), lambda h, j: (h, 0, 0)),
                   pl.BlockSpec((None, t, QK_PAD), lambda h, j: (h, j, 0)),
                   pl.BlockSpec((None, t, HEAD_DIM), lambda h, j: (h, j, 0))],
        out_shape=[jax.ShapeDtypeStruct((HEADS, T, QK_PAD), BF16), jax.ShapeDtypeStruct((HEADS, T, QK_PAD), BF16),
                   jax.ShapeDtypeStruct((HEADS, T, HEAD_DIM), BF16)],
        scratch_shapes=[pltpu.VMEM((T, HEAD_DIM), F32), pltpu.VMEM((T, QK_PAD), F32)], exchange=exchange)


def _ln_fwd(r):
    mu = _lanemean(r)
    xc = r - mu
    rstd = lax.rsqrt(_lanemean(xc * xc) + LN_EPS)
    return xc * rstd, rstd


def _ln_bwd(dxh, xhat, rstd):
    return rstd * (dxh - _lanemean(dxh) - xhat * _lanemean(dxh * xhat))


def _mix_ln1(o_hg, o_mla, w_out, x, g_a, ln1_g, ln1_b, sc_m, sh_m, exchange=None):
    T = x.shape[0]
    tm = min(ROW_TILE, T)
    half = o_hg.shape[1]

    def body(hg_ref, mla_ref, w_ref, x_ref, ga_ref, g_ref, b_ref, sc_ref, sh_ref, mix_ref, xhat_ref, rstd_ref, u2_ref):
        mix = _dot(hg_ref[...], w_ref[0:half, :]) + _bdot(mla_ref[...], w_ref[half:, :])
        mix_ref[...] = mix
        xhat, rstd = _ln_fwd(ALPHA * x_ref[...] + (1.0 + ga_ref[...]) * mix)
        xhat_ref[...] = xhat
        rstd_ref[...] = jnp.broadcast_to(rstd, (tm, 128))
        u2_ref[...] = _modulate(xhat * g_ref[...] + b_ref[...], sc_ref[...], sh_ref[...]).astype(BF16)

    row = pl.BlockSpec((tm, D_MODEL), lambda i: (i, 0))
    vec = _full((1, D_MODEL))
    halfrow = pl.BlockSpec((tm, half), lambda i: (i, 0))
    return _call(
        body, "mix_ln1", (o_hg, o_mla, w_out, x, g_a, ln1_g, ln1_b, sc_m, sh_m), grid=(T // tm,),
        in_specs=[halfrow, halfrow, _full(w_out.shape), row, vec, vec, vec, vec, vec],
        out_specs=[row, row, pl.BlockSpec((tm, 128), lambda i: (i, 0)), row],
        out_shape=[jax.ShapeDtypeStruct((T, D_MODEL), F32), jax.ShapeDtypeStruct((T, D_MODEL), F32),
                   jax.ShapeDtypeStruct((T, 128), F32), jax.ShapeDtypeStruct((T, D_MODEL), BF16)],
        exchange=exchange)


def _mlp_fwd(u2, w1, w2, xhat1, ln1_g, ln1_b, g_m, ln2_g, ln2_b, target):
    T = u2.shape[0]
    half, tf = w1[0].shape[1:]
    nf = N_DEV // MLP_SLABS
    tm = min(ROW_TILE, T)

    def body(u2_ref, w1a_ref, w1b_ref, w2_ref, xhat_ref, g1_ref, b1_ref, gm_ref, g2_ref, b2_ref, tgt_ref,
             r_ref, dr2_ref, dh_ref, small_ref, acc_ref):
        i, f = pl.program_id(0), pl.program_id(1)
        dm = D_MODEL

        @pl.when((i == 0) & (f == 0))
        def _():
            small_ref[...] = jnp.zeros_like(small_ref)

        @pl.when(f == 0)
        def _():
            acc_ref[...] = jnp.zeros_like(acc_ref)

        u2t = u2_ref[...]
        part = None
        for s in range(MLP_SLABS):
            r = jnp.maximum(_dot(u2t[:, :half], w1a_ref[s]) + _dot(u2t[:, half:], w1b_ref[s]), 0.0)
            r_ref[:, s * tf:(s + 1) * tf] = r.astype(BF16)
            d = _bdot(r * r, w2_ref[s])
            part = d if part is None else part + d
        acc_ref[...] += part

        @pl.when(f == nf - 1)
        def _():
            h = acc_ref[...]
            x1 = xhat_ref[...] * g1_ref[...] + b1_ref[...]
            xhat2, rstd2 = _ln_fwd(ALPHA * x1 + (1.0 + gm_ref[...]) * h)
            err = xhat2 * g2_ref[...] + b2_ref[...] - tgt_ref[...]
            small_ref[:, 3 * dm:] += jnp.sum(0.5 * _lanemean(err * err), axis=0, keepdims=True)
            dy = err * (1.0 / D_MODEL)
            small_ref[:, dm:2 * dm] += _rowsum(dy * xhat2)
            small_ref[:, 2 * dm:3 * dm] += _rowsum(dy)
            dr2 = _ln_bwd(dy * g2_ref[...], xhat2, rstd2)
            dr2_ref[...] = dr2
            small_ref[:, 0:dm] += _rowsum(dr2 * h)
            dh_ref[...] = ((1.0 + gm_ref[...]) * dr2).astype(BF16)

    row = pl.BlockSpec((tm, D_MODEL), lambda i, f: (i, 0))
    vec = _full((1, D_MODEL))
    return pl.pallas_call(
        body, name="mlp_fwd", grid=(T // tm, nf),
        in_specs=[row, pl.BlockSpec((MLP_SLABS, half, tf), lambda i, f: (f, 0, 0)),
                  pl.BlockSpec((MLP_SLABS, half, tf), lambda i, f: (f, 0, 0)),
                  pl.BlockSpec((MLP_SLABS, tf, D_MODEL), lambda i, f: (f, 0, 0)),
                  row, vec, vec, vec, vec, vec, row],
        out_specs=[pl.BlockSpec((tm, MLP_SLABS * tf), lambda i, f: (i, f)), row, row, _full((1, 3 * D_MODEL + 128))],
        out_shape=[jax.ShapeDtypeStruct((T, N_DEV * tf), BF16), jax.ShapeDtypeStruct((T, D_MODEL), F32),
                   jax.ShapeDtypeStruct((T, D_MODEL), BF16), jax.ShapeDtypeStruct((1, 3 * D_MODEL + 128), F32)],
        scratch_shapes=[pltpu.VMEM((tm, D_MODEL), F32)],
        compiler_params=_params(),
    )(u2, w1[0], w1[1], w2, xhat1, ln1_g, ln1_b, g_m, ln2_g, ln2_b, target)


def _mlp_bwd(dh, w1, w2, r, dr2, xhat1, rstd1, mix, ln1_g, ln1_b, sc_m, g_a):
    T = dh.shape[0]
    half, tf = w1[0].shape[1:]
    nf = N_DEV // MLP_SLABS
    tm = min(ROW_TILE, T)

    def body(dh_ref, w1a_ref, w1b_ref, w2_ref, r_ref, dr2_ref, xhat_ref, rstd_ref, mix_ref, g1_ref, b1_ref, sc_ref, ga_ref,
             dhpre_ref, dr1_ref, dmix_ref, small_ref, acc_ref):
        i, f = pl.program_id(0), pl.program_id(1)
        dm = D_MODEL

        @pl.when((i == 0) & (f == 0))
        def _():
            small_ref[...] = jnp.zeros_like(small_ref)

        @pl.when(f == 0)
        def _():
            acc_ref[...] = jnp.zeros_like(acc_ref)

        dht = dh_ref[...]
        part = None
        for s in range(MLP_SLABS):
            cols = slice(s * tf, (s + 1) * tf)
            dhpre = (_dot(dht, w2_ref[s], NT) * (2.0 * r_ref[:, cols].astype(F32))).astype(BF16)
            dhpre_ref[:, cols] = dhpre
            d = jnp.concatenate([_dot(dhpre, w1a_ref[s], NT), _dot(dhpre, w1b_ref[s], NT)], axis=1)
            part = d if part is None else part + d
        acc_ref[...] += part

        @pl.when(f == nf - 1)
        def _():
            du2 = acc_ref[...]
            xhat = xhat_ref[...]
            x1 = xhat * g1_ref[...] + b1_ref[...]
            dx1 = ALPHA * dr2_ref[...] + du2 * (1.0 + sc_ref[...])
            small_ref[:, 2 * dm:3 * dm] += _rowsum(du2 * x1)
            small_ref[:, dm:2 * dm] += _rowsum(du2)
            small_ref[:, 3 * dm:4 * dm] += _rowsum(dx1 * xhat)
            small_ref[:, 4 * dm:5 * dm] += _rowsum(dx1)
            dr1 = _ln_bwd(dx1 * g1_ref[...], xhat, rstd_ref[:, 0:1])
            dr1_ref[...] = dr1
            small_ref[:, 0:dm] += _rowsum(dr1 * mix_ref[...])
            dmix_ref[...] = ((1.0 + ga_ref[...]) * dr1).astype(BF16)

    row = pl.BlockSpec((tm, D_MODEL), lambda i, f: (i, 0))
    vec = _full((1, D_MODEL))
    return pl.pallas_call(
        body, name="mlp_bwd", grid=(T // tm, nf),
        in_specs=[row, pl.BlockSpec((MLP_SLABS, half, tf), lambda i, f: (f, 0, 0)),
                  pl.BlockSpec((MLP_SLABS, half, tf), lambda i, f: (f, 0, 0)),
                  pl.BlockSpec((MLP_SLABS, tf, D_MODEL), lambda i, f: (f, 0, 0)),
                  pl.BlockSpec((tm, MLP_SLABS * tf), lambda i, f: (i, f)), row, row,
                  pl.BlockSpec((tm, 128), lambda i, f: (i, 0)), row, vec, vec, vec, vec],
        out_specs=[pl.BlockSpec((tm, MLP_SLABS * tf), lambda i, f: (i, f)), row, row, _full((1, 5 * D_MODEL))],
        out_shape=[jax.ShapeDtypeStruct((T, N_DEV * tf), BF16), jax.ShapeDtypeStruct((T, D_MODEL), F32),
                   jax.ShapeDtypeStruct((T, D_MODEL), BF16), jax.ShapeDtypeStruct((1, 5 * D_MODEL), F32)],
        scratch_shapes=[pltpu.VMEM((tm, D_MODEL), F32)],
        compiler_params=_params(),
    )(dh, w1[0], w1[1], w2, r, dr2, xhat1, rstd1, mix, ln1_g, ln1_b, sc_m, g_a)


def _input_bwd(dz_h, dz_m, w_in_ext, x, dr1, sc_a, exchange=None):
    T = x.shape[0]
    tm = min(ROW_TILE, T)

    def body(dzh_ref, dzm_ref, w_ref, x_ref, dr1_ref, sc_ref, gx_ref, small_ref):
        @pl.when(pl.program_id(0) == 0)
        def _():
            small_ref[...] = jnp.zeros_like(small_ref)

        du = _bdot(dzh_ref[...], w_ref[0:2048, :]) + _bdot(dzm_ref[...], w_ref[2048:3072, :])
        gx_ref[...] = ALPHA * dr1_ref[...] + du * (1.0 + sc_ref[...])
        small_ref[:, D_MODEL:] += _rowsum(du * x_ref[...])
        small_ref[:, 0:D_MODEL] += _rowsum(du)

    row = pl.BlockSpec((tm, D_MODEL), lambda i: (i, 0))
    vec = _full((1, D_MODEL))
    return _call(
        body, "input_bwd", (dz_h, dz_m, w_in_ext, x, dr1, sc_a), grid=(T // tm,),
        in_specs=[pl.BlockSpec((tm, 2048), lambda i: (i, 0)), row, _full(w_in_ext.shape), row, row, vec],
        out_specs=[row, _full((1, 2 * D_MODEL))],
        out_shape=[jax.ShapeDtypeStruct((T, D_MODEL), F32), jax.ShapeDtypeStruct((1, 2 * D_MODEL), F32)],
        exchange=exchange)


def _adam_math(w, g, m, v):
    m = ADAM_B1 * m + (1.0 - ADAM_B1) * g
    v = ADAM_B2 * v + (1.0 - ADAM_B2) * (g * g)
    m_hat = m / (1.0 - ADAM_B1 ** ADAM_STEP)
    v_hat = v / (1.0 - ADAM_B2 ** ADAM_STEP)
    return -ADAM_LR * (m_hat / (jnp.sqrt(v_hat) + ADAM_EPS) + ADAM_WD * w), m, v


def _adam(g_slabs, w, m, v, name, g_fn=None, g_extra=()):
    R, C = w.shape
    tr = 256 if R % 256 == 0 else R
    ns = 0 if g_slabs is None else g_slabs.shape[0]
    slab_rows = tr if g_slabs is None or g_slabs.shape[1] == R else g_slabs.shape[1]
    assert slab_rows == tr or tr == R
    ne = len(g_extra)

    def body(*refs):
        e_refs = refs[:ne]
        refs = refs[ne:]
        if ns:
            gs_ref, refs = refs[0], refs[1:]
        w_ref, m_ref, v_ref, g_ref, d_ref, nm_ref, nv_ref = refs
        if g_fn is not None:
            g = g_fn(*e_refs)
        else:
            g = gs_ref[0].astype(F32)
            for s in range(1, ns):
                g = g + gs_ref[s].astype(F32)
            g = g[:tr]
        d, nm, nv = _adam_math(w_ref[...], g, m_ref[...], v_ref[...])
        g_ref[...] = g
        d_ref[...] = d
        nm_ref[...] = nm
        nv_ref[...] = nv

    blk = pl.BlockSpec((tr, C), lambda i: (i, 0))
    in_specs = [pl.BlockSpec((tr, e.shape[1]), lambda i: (i, 0)) if e.shape[0] == R else _full(e.shape) for e in g_extra]
    args = list(g_extra)
    if ns:
        in_specs.append(pl.BlockSpec((ns, slab_rows, C), lambda i: (0, i, 0)))
        args.append(g_slabs)
    return pl.pallas_call(
        body, name=name, grid=(R // tr,), in_specs=in_specs + [blk] * 3, out_specs=[blk] * 4,
        out_shape=[jax.ShapeDtypeStruct((R, C), F32)] * 4, compiler_params=_params(),
    )(*args, w, m, v)


def _adam_small(small_all, params):
    n = len(params)

    def body(*refs):
        s_ref, refs = refs[0], refs[1:]
        wmv, loss_ref, outs = refs[:3 * n], refs[3 * n], refs[3 * n + 1:]
        tot = s_ref[0]
        for i in range(1, N_DEV):
            tot = tot + s_ref[i]
        loss_ref[...] = tot[:, SMALL_W - 128:]
        for j, (w, _, _, off) in enumerate(params):
            w_ref, m_ref, v_ref = wmv[3 * j:3 * j + 3]
            g_ref, d_ref, nm_ref, nv_ref = outs[4 * j:4 * j + 4]
            if w.shape[0] == 2:
                lb = _lower_bound(w_ref)
                g0 = tot[:, off:off + w.shape[1]] * lb * (1.0 - lb)
                rows = [(slice(0, 1), g0), (slice(1, 2), -g0)]
            else:
                rows = [(slice(0, 1), tot[:, off:off + w.shape[1]])]
            for rs, g in rows:
                d, nm, nv = _adam_math(w_ref[rs, :], g, m_ref[rs, :], v_ref[rs, :])
                g_ref[rs, :], d_ref[rs, :], nm_ref[rs, :], nv_ref[rs, :] = g, d, nm, nv

    out_shape = [jax.ShapeDtypeStruct((1, 128), F32)]
    for w, _, _, _ in params:
        out_shape += [jax.ShapeDtypeStruct(w.shape, F32)] * 4
    res = pl.pallas_call(body, name="adam_small", out_shape=out_shape, compiler_params=_params())(
        small_all, *[a for w, m, v, _ in params for a in (w, m, v)])
    return res[0], [tuple(res[1 + 4 * j:5 + 4 * j]) for j in range(n)]


def _cols_from_slabs(g):
    s, r, c = g.shape
    return jnp.transpose(g, (1, 0, 2)).reshape(r, s * c)


def _slabs_from_cols(w):
    r, c = w.shape
    return jnp.transpose(w.reshape(r, N_DEV, c // N_DEV), (1, 0, 2))


def _rot_half_rows(wt):
    return jnp.concatenate([-wt[32:], wt[:32]], axis=0)


def _unrot_half_rows(dwt_rot):
    return jnp.concatenate([dwt_rot[32:], -dwt_rot[:32]], axis=0)


def _ext_in_t(g):
    n, rows, k_in = g.shape
    keep = n * rows - ROPE_DIM

    def body(g_ref, o_ref, stage_ref):
        stage_ref[keep:, :] = jnp.zeros((o_ref.shape[0] - keep, k_in), F32)
        for i in range(n - 1):
            stage_ref[rows * i:rows * (i + 1), :] = g_ref[i].astype(F32)
        last = g_ref[n - 1].astype(F32)
        stage_ref[rows * (n - 1):keep, :] = last[:rows - ROPE_DIM]
        wk = last[rows - ROPE_DIM:]
        stage_ref[keep + 128:keep + 192, :] = wk
        stage_ref[keep + 384:keep + 416, :] = -wk[32:]
        stage_ref[keep + 416:keep + 448, :] = wk[:32]
        o_ref[...] = stage_ref[...].astype(BF16)

    return pl.pallas_call(body, name="ext_w_in", out_shape=jax.ShapeDtypeStruct((keep + 512, k_in), BF16),
                          scratch_shapes=[pltpu.VMEM((keep + 512, k_in), F32)], compiler_params=_params())(g)


def _ext_q_t(wt):
    r = wt.shape[1]
    z64, z128 = jnp.zeros((64, r), BF16), jnp.zeros((128, r), BF16)
    per = HEAD_DIM + ROPE_DIM
    main = [jnp.concatenate([wt[per * h:per * (h + 1)], z64], axis=0) for h in range(HEADS)]
    rot = [jnp.concatenate([z128, _rot_half_rows(wt[per * h + HEAD_DIM:per * (h + 1)]), z64], axis=0)
           for h in range(HEADS)]
    return jnp.concatenate(main + rot, axis=0)


def _ext_kv(w_kv_up):
    r = w_kv_up.shape[0]
    z128 = jnp.zeros((r, 128), BF16)
    wkv = w_kv_up.reshape(r, HEADS, 2 * HEAD_DIM)
    kpad = [jnp.concatenate([wkv[:, h, :HEAD_DIM], z128], axis=1) for h in range(HEADS)]
    vals = [wkv[:, h, HEAD_DIM:] for h in range(HEADS)]
    return jnp.concatenate(kpad + vals, axis=1)


def _w_in_grad_slabs(dwt_h, dwt_m):
    d = dwt_h.shape[1]
    rows = (dwt_h.shape[0] + 512 + ROPE_DIM) // N_DEV
    padded = rows + (-rows) % 16

    def body(h_ref, m_ref, o_ref, stage_ref):
        stage_ref[0:2048, :] = h_ref[...]
        stage_ref[2048:2560, :] = m_ref[0:512, :]
        rot = m_ref[768 + 128:768 + 192, :]
        stage_ref[2560:2592, :] = m_ref[640:672, :] + rot[32:]
        stage_ref[2592:2624, :] = m_ref[672:704, :] - rot[:32]
        zero = jnp.zeros((padded - rows, d), F32)
        for i in range(N_DEV):
            o_ref[i] = jnp.concatenate([stage_ref[rows * i:rows * (i + 1), :], zero], axis=0).astype(BF16)

    return pl.pallas_call(body, name="w_in_grad_slabs", out_shape=jax.ShapeDtypeStruct((N_DEV, padded, d), BF16),
                          scratch_shapes=[pltpu.VMEM((N_DEV * rows, d), F32)], compiler_params=_params())(dwt_h, dwt_m)


def _grad_q_from_ext_t(dwq_ext_t):
    rows = []
    for h in range(HEADS):
        main, rot = dwq_ext_t[256 * h:256 * h + 256], dwq_ext_t[1024 + 256 * h:1280 + 256 * h]
        rows += [main[:128], main[128:192] + _unrot_half_rows(rot[128:192])]
    return jnp.concatenate(rows, axis=0)


def _grad_kv_from_ext(dwkv_ext):
    kvcols = []
    for h in range(HEADS):
        kvcols += [dwkv_ext[:, 256 * h:256 * h + 128], dwkv_ext[:, 1024 + 128 * h:1152 + 128 * h]]
    return jnp.concatenate(kvcols, axis=1)


SMALL_W = 6144 + 512 + 512 + 256 + 256 + 4 * 1024 + 128


def kernel(x, c, positions, w_ada, b_ada, w_in, hg_lower_bounds, hg_norm_w, mla_q_norm_w, w_q_up, mla_kv_norm_w, w_kv_up, w_out, ln1_g, ln1_b, w_mlp_in, w_mlp_out, ln2_g, ln2_b, loss_target, m_w_ada, m_b_ada, m_w_in, m_hg_lower_bounds, m_hg_norm_w, m_mla_q_norm_w, m_w_q_up, m_mla_kv_norm_w, m_w_kv_up, m_w_out, m_ln1_g, m_ln1_b, m_w_mlp_in, m_w_mlp_out, m_ln2_g, m_ln2_b, v_w_ada, v_b_ada, v_w_in, v_hg_lower_bounds, v_hg_norm_w, v_mla_q_norm_w, v_w_q_up, v_mla_kv_norm_w, v_w_kv_up, v_w_out, v_ln1_g, v_ln1_b, v_w_mlp_in, v_w_mlp_out, v_ln2_g, v_ln2_b):
    T = x.shape[1]
    me = 4 * lax.axis_index("x") + 2 * lax.axis_index("y") + lax.axis_index("c")
    xs, tgt = x[0], loss_target[0]
    transposed = ("w_in", "w_q_up")
    as_used = lambda n, a: a[0].T if n in transposed else a[0]
    big = {n: as_used(n, a) for n, a in dict(w_in=w_in, w_q_up=w_q_up, w_kv_up=w_kv_up, w_out=w_out,
                                              w_mlp_in=w_mlp_in, w_mlp_out=w_mlp_out).items()}
    names = list(big)

    bf = {n: big[n].astype(BF16) for n in ("w_in", "w_q_up", "w_kv_up", "w_out")}
    g_in, g_c = _gather_two_level([bf["w_in"], c], name="gather_w_in")
    c_all = g_c.reshape(N_DEV, D_MODEL)

    ada_cols = w_ada.shape[2]
    mod_part, cond = _mod_part(c_all, w_ada[0], lax.dynamic_slice(b_ada, (0, me * ada_cols), (1, ada_cols)))
    (mod_all,) = _exchange([mod_part], scatter=False, name="gather_mod")
    mod_row = lax.dynamic_slice(mod_all, (0, me, 0), (N_DEV, 1, ada_cols)).reshape(1, N_DEV * ada_cols)
    sh_a, sc_a, g_a, sh_m, sc_m, g_m = [mod_row[:, D_MODEL * i:D_MODEL * (i + 1)] for i in range(6)]

    w_in_ext = _ext_in_t(g_in)
    half = D_MODEL // 2
    z, (w1_top,) = _matmul(xs, w_in_ext, "NT", "in_proj", a_fn=_modulate, extras=(sc_a, sh_a), tn=3072,
                           exchange=_StagedGather(big["w_mlp_in"], rows=(0, half)))
    (o_raw, o_gated, s_prev), (g_q, g_kv, g_out) = _hgrn_fwd(
        z, hg_lower_bounds, hg_norm_w, exchange=_Exchange([bf["w_q_up"], bf["w_kv_up"], bf["w_out"]], False))
    wq_ext = _ext_q_t(g_q.reshape(N_DEV * g_q.shape[1], g_q.shape[2]))
    wkv_ext = _ext_kv(_cols_from_slabs(g_kv))
    w_out_full = g_out.reshape(D_MODEL, D_MODEL)
    inv_freq = 1.0 / (ROPE_THETA ** (jnp.arange(0, ROPE_DIM, 2, dtype=F32) / ROPE_DIM))
    zeros = lambda n: jnp.zeros((n,), F32)
    invf = jnp.concatenate([zeros(128), inv_freq, inv_freq, zeros(64)]).reshape(1, QK_PAD)
    m_rot = jnp.concatenate([zeros(128), jnp.ones((64,), F32), zeros(64)]).reshape(1, QK_PAD)
    q, k, v, c1, s1, cqn, ckvn = _mla_pre(z, positions.reshape(T, 1), invf, m_rot, wq_ext, wkv_ext,
                                          mla_q_norm_w, mla_kv_norm_w)[0]
    (o_mla, lse), (w1_bot, w2) = _attn_fwd(
        q, k, v, exchange=[_StagedGather(big["w_mlp_in"], 0.85, rows=(half, half)),
                           _StagedGather(big["w_mlp_out"], 0.85)])
    w1 = (w1_top, w1_bot)
    mix, xhat1, rstd1, u2 = _mix_ln1(o_gated, o_mla, w_out_full, xs, g_a, ln1_g, ln1_b, sc_m, sh_m)[0]
    r, dr2, dh, small_mlp_fwd = _mlp_fwd(u2, w1, w2, xhat1, ln1_g, ln1_b, g_m, ln2_g, ln2_b, tgt)

    dhpre, dr1, dmix, small_mlp_bwd = _mlp_bwd(dh, w1, w2, r, dr2, xhat1, rstd1, mix, ln1_g, ln1_b, sc_m, g_a)
    received = {}
    dw2 = _matmul(r, dh, "TN", "wgrad_mlp_out", out_dtype=BF16, a_fn=_square, tm=1024, tk=2048)
    dw1 = _matmul(u2, dhpre, "TN", "wgrad_mlp_in", out_dtype=BF16, tm=1024, tk=2048, out_slabs=N_DEV)
    dmixcat = _matmul(dmix, w_out_full, "NT", "dgrad_out", tm=1024)
    dw_out = jnp.concatenate([_matmul(o_gated, dmix, "TN", "wgrad_out_hg", out_dtype=BF16, tk=2048),
                              _matmul(o_mla, dmix, "TN", "wgrad_out_mla", out_dtype=BF16, tk=2048)], axis=0)
    (dz_h, small_hgrn), (received["w_out"],) = _hgrn_bwd(
        dmixcat, z, o_raw, s_prev, hg_lower_bounds, hg_norm_w,
        exchange=_Exchange([dw_out.reshape(N_DEV, D_MODEL // N_DEV, D_MODEL)], True))
    (dq, dk, dv), (received["w_mlp_in"], received["w_mlp_out"]) = _attn_bwd(
        q, k, v, dmixcat, o_mla, lse,
        exchange=_Exchange([dw1, dw2.reshape(N_DEV, dw2.shape[0] // N_DEV, D_MODEL)], True))
    dz_m, dq_ext, dkv_ext, small_mla = _mla_bwd(dq, dk, dv, z, c1, s1, wq_ext, wkv_ext, mla_q_norm_w, mla_kv_norm_w)
    dwq_t = _grad_q_from_ext_t(_matmul(dq_ext, cqn, "TN", "wgrad_q_up", tm=1024, tk=2048))
    dwkv = _grad_kv_from_ext(_matmul(ckvn, dkv_ext, "TN", "wgrad_kv_up", tn=1536, tk=2048))
    qkv_slabs = [dwq_t.reshape((N_DEV, dwq_t.shape[0] // N_DEV, dwq_t.shape[1])).astype(BF16),
                 _slabs_from_cols(dwkv).astype(BF16)]
    dwt_h, (received["w_q_up"], received["w_kv_up"]) = _matmul(
        dz_h, xs, "TN", "wgrad_in_h", b_fn=_modulate, extras=(sc_a, sh_a), tm=1024, tk=2048,
        exchange=_Exchange(qkv_slabs, True))
    dwt_m = _matmul(dz_m, xs, "TN", "wgrad_in_m", b_fn=_modulate, extras=(sc_a, sh_a), tm=1024, tk=2048)
    in_slabs = _w_in_grad_slabs(dwt_h, dwt_m)
    (grad_x, small_in), (received["w_in"],) = _input_bwd(
        dz_h, dz_m, w_in_ext, xs, dr1, sc_a, exchange=_StagedScatter(in_slabs))

    small = jnp.concatenate([small_in, small_mlp_bwd[:, :3 * D_MODEL], small_mlp_fwd[:, :D_MODEL], small_hgrn,
                             small_mla, small_mlp_bwd[:, 3 * D_MODEL:], small_mlp_fwd[:, D_MODEL:]], axis=1)
    assert small.shape == (1, SMALL_W)
    (small_all,) = _exchange([small], scatter=False, name="gather_small")

    moments = dict(w_in=(m_w_in, v_w_in), w_q_up=(m_w_q_up, v_w_q_up), w_kv_up=(m_w_kv_up, v_w_kv_up),
                   w_out=(m_w_out, v_w_out), w_mlp_in=(m_w_mlp_in, v_w_mlp_in), w_mlp_out=(m_w_mlp_out, v_w_mlp_out))
    res = {}
    for n in names:
        res[n] = _adam(received[n], big[n], as_used(n, moments[n][0]), as_used(n, moments[n][1]), name="adam_" + n)
    dmod_cols = lax.dynamic_slice(small_all.reshape(N_DEV, SMALL_W), (0, me * ada_cols), (N_DEV, ada_cols))
    cond_t = cond.T

    def ada_grad(ct_ref, dm_ref):
        g = ct_ref[:, 0:1] * dm_ref[0:1, :]
        for b in range(1, N_DEV):
            g = g + ct_ref[:, b:b + 1] * dm_ref[b:b + 1, :]
        return g

    res["w_ada"] = _adam(None, w_ada[0], m_w_ada[0], v_w_ada[0], name="adam_w_ada", g_fn=ada_grad,
                         g_extra=(cond_t, dmod_cols))

    small_params = [("b_ada", b_ada, m_b_ada, v_b_ada, 0),
                    ("hg_lower_bounds", hg_lower_bounds, m_hg_lower_bounds, v_hg_lower_bounds, 6144),
                    ("hg_norm_w", hg_norm_w, m_hg_norm_w, v_hg_norm_w, 6656),
                    ("mla_q_norm_w", mla_q_norm_w, m_mla_q_norm_w, v_mla_q_norm_w, 7168),
                    ("mla_kv_norm_w", mla_kv_norm_w, m_mla_kv_norm_w, v_mla_kv_norm_w, 7424),
                    ("ln1_g", ln1_g, m_ln1_g, v_ln1_g, 7680), ("ln1_b", ln1_b, m_ln1_b, v_ln1_b, 8704),
                    ("ln2_g", ln2_g, m_ln2_g, v_ln2_g, 9728), ("ln2_b", ln2_b, m_ln2_b, v_ln2_b, 10752)]
    loss_row, small_res = _adam_small(small_all, [p[1:] for p in small_params])
    for p, r4 in zip(small_params, small_res):
        res[p[0]] = r4
    loss = loss_row[0, 0]

    order = ["w_ada", "b_ada", "w_in", "hg_lower_bounds", "hg_norm_w", "mla_q_norm_w", "w_q_up", "mla_kv_norm_w",
             "w_kv_up", "w_out", "ln1_g", "ln1_b", "w_mlp_in", "w_mlp_out", "ln2_g", "ln2_b"]
    def as_given(n, a):
        if n in transposed:
            a = a.T
        return a[None] if n in big or n == "w_ada" else a

    shaped = {n: tuple(as_given(n, a) for a in res[n]) for n in order}
    outs = [loss, grad_x.reshape(1, T, D_MODEL)]
    for i in range(4):
        outs += [shaped[n][i] for n in order]
    return tuple(outs)
```

```python
import functools

import jax
import jax.numpy as jnp
import numpy as np
from jax import lax
from jax.experimental import pallas as pl
from jax.experimental.pallas import tpu as pltpu

F32, BF16 = jnp.float32, jnp.bfloat16
N_DEV = 8
D_MODEL = 1024
HEADS = 4
HEAD_DIM = 128
ROPE_DIM = 64
QK_PAD = 256
CHUNK = 64
ROPE_THETA = 10000.0
RMS_EPS = 1e-6
LN_EPS = 1e-5
ALPHA = 2.0 ** 0.25
ATT_SCALE = (HEAD_DIM + ROPE_DIM) ** -0.5
LN2 = float(np.log(2.0))
Q_PRESCALE = ATT_SCALE / LN2
ADAM_LR, ADAM_B1, ADAM_B2, ADAM_EPS, ADAM_WD, ADAM_STEP = 0.001, 0.9, 0.999, 1e-08, 0.01, 10
NEG_BIG = -1e30

ROW_TILE = 512
ATT_TILE = 512
HGRN_GROUP = 8
MLP_SLABS = 4
VMEM_LIMIT = 56 * 2 ** 20

NN = (((1,), (0,)), ((), ()))
NT = (((1,), (1,)), ((), ()))
TN = (((0,), (0,)), ((), ()))


def _dot(a, b, dims=NN):
    return lax.dot_general(a, b, dims, preferred_element_type=F32)


def _bdot(a, b, dims=NN):
    return lax.dot_general(a.astype(BF16), b.astype(BF16), dims, preferred_element_type=F32)


def _hdot(a, b, dims=NN):
    return lax.dot_general(a, b, dims, precision=lax.Precision.HIGHEST, preferred_element_type=F32)


def _params():
    return pltpu.CompilerParams(vmem_limit_bytes=VMEM_LIMIT)


def _sigmoid(x):
    return 1.0 / (1.0 + jnp.exp(-x))


def _rowsum(x):
    return jnp.sum(x, axis=0, keepdims=True)


def _lanemean(x):
    return jnp.mean(x, axis=-1, keepdims=True)


def _full(shape):
    nd = len(shape)
    return pl.BlockSpec(shape, lambda *_: (0,) * nd)


class _Exchange:
    def __init__(self, arrs, scatter):
        self.arrs, self.scatter, self.n, self.aliases, self.middle_at = list(arrs), scatter, len(arrs), [], 0.5
        self.out_shape = [jax.ShapeDtypeStruct((N_DEV,) + (a.shape[1:] if scatter else a.shape), a.dtype)
                          for a in self.arrs]
        n = self.n
        self.scratch = [pltpu.SemaphoreType.DMA((n, N_DEV - 1)), pltpu.SemaphoreType.DMA((n, N_DEV - 1)),
                        pltpu.SemaphoreType.DMA((n,))]

    def _copies(self, ins, outs, sems):
        send_sems, recv_sems, loc_sems = sems
        x, y, c = lax.axis_index("x"), lax.axis_index("y"), lax.axis_index("c")
        me = 4 * x + 2 * y + c
        copies = []
        for k in range(self.n):
            src_of = (lambda i, k=k: ins[k].at[i]) if self.scatter else (lambda i, k=k: ins[k])
            copies.append((pltpu.make_async_copy(src_of(me), outs[k].at[me], loc_sems.at[k]), None))
            for p in range(1, N_DEV):
                px = (1 - x) if p & 4 else x
                py = (1 - y) if p & 2 else y
                pc = (1 - c) if p & 1 else c
                peer = 4 * px + 2 * py + pc
                both = dict(send_sem=send_sems.at[k, p - 1], recv_sem=recv_sems.at[k, p - 1],
                            device_id=(px, py, pc), device_id_type=pl.DeviceIdType.MESH)
                send = pltpu.make_async_remote_copy(src_ref=src_of(peer), dst_ref=outs[k].at[me], **both)
                recv = pltpu.make_async_remote_copy(src_ref=src_of(peer), dst_ref=outs[k].at[peer], **both)
                copies.append((send, recv))
        return copies

    def start(self, ins, outs, sems):
        for first, _ in self._copies(ins, outs, sems):
            first.start()

    def middle(self, ins, outs, sems):
        pass

    def wait(self, ins, outs, sems):
        for first, recv in self._copies(ins, outs, sems):
            if recv is None:
                first.wait()
            else:
                recv.wait_recv()
                first.wait_send()


class _StagedGather:
    def __init__(self, arr, middle_at=0.8, rows=None):
        self.r0, n = rows if rows else (0, arr.shape[0])
        block = (n,) + arr.shape[1:]
        self.arrs, self.aliases, self.middle_at = [arr], [], middle_at
        self.out_shape = [jax.ShapeDtypeStruct((N_DEV,) + block, BF16)]
        self.scratch = [pltpu.VMEM((N_DEV,) + block, BF16), pltpu.SemaphoreType.DMA((7,)),
                        pltpu.SemaphoreType.DMA((7,)), pltpu.SemaphoreType.DMA((2,)), pltpu.VMEM(block, arr.dtype)]

    def _parts(self, scr):
        stage, send_sems, recv_sems, loc_sems = scr[:4]
        x, y, c = lax.axis_index("x"), lax.axis_index("y"), lax.axis_index("c")
        me, sibling = (x, y, c), (x, y, 1 - c)
        chips = [(1 - x, y), (x, 1 - y), (1 - x, 1 - y)]

        def copy(j, block, to):
            px, py, pc = block
            slot = stage.at[4 * px + 2 * py + pc]
            return pltpu.make_async_remote_copy(src_ref=slot, dst_ref=slot, send_sem=send_sems.at[j],
                                                recv_sem=recv_sems.at[j], device_id=to,
                                                device_id_type=pl.DeviceIdType.MESH)

        return stage, loc_sems, me, sibling, chips, c, copy

    def start(self, ins, outs, scr):
        stage, loc_sems, me, sibling, chips, c, copy = self._parts(scr)
        x, y, _ = me
        raw = scr[4]
        own = pltpu.make_async_copy(ins[0].at[pl.ds(self.r0, raw.shape[0])], raw, loc_sems.at[0])
        own.start()
        own.wait()
        stage[4 * x + 2 * y + c] = raw[...].astype(BF16)
        copy(0, me, sibling).start()
        for j, chip in enumerate(chips):
            copy(1 + j, me, (*chip, c)).start()

    def middle(self, ins, outs, scr):
        stage, loc_sems, me, sibling, chips, c, copy = self._parts(scr)
        for j, chip in enumerate(chips):
            copy(1 + j, (*chip, c), me).wait_recv()
            copy(4 + j, (*chip, c), sibling).start()

    def wait(self, ins, outs, scr):
        stage, loc_sems, me, sibling, chips, c, copy = self._parts(scr)
        copy(0, sibling, me).wait_recv()
        for j, chip in enumerate(chips):
            copy(4 + j, (*chip, 1 - c), me).wait_recv()
        copy(0, me, sibling).wait_send()
        for j, chip in enumerate(chips):
            copy(1 + j, me, (*chip, c)).wait_send()
            copy(4 + j, (*chip, c), sibling).wait_send()
        whole = pltpu.make_async_copy(stage, outs[0], loc_sems.at[1])
        whole.start()
        whole.wait()


class _StagedScatter:
    def __init__(self, slabs, middle_at=0.2):
        _, r, c = slabs.shape
        self.arrs, self.aliases, self.middle_at = [slabs], [], middle_at
        self.out_shape = [jax.ShapeDtypeStruct((4, r, c), slabs.dtype)]
        self.scratch = [pltpu.VMEM((N_DEV, r, c), slabs.dtype), pltpu.VMEM((4, r, c), slabs.dtype),
                        pltpu.VMEM((3, r, c), slabs.dtype), pltpu.SemaphoreType.DMA((4,)), pltpu.SemaphoreType.DMA((4,)),
                        pltpu.SemaphoreType.DMA((3,)), pltpu.SemaphoreType.DMA((3,)), pltpu.SemaphoreType.DMA((4,))]

    def _parts(self, scr):
        stage, from_sib, from_chips, sib_send, sib_recv, ici_send, ici_recv, loc_sems = scr
        x, y, c = lax.axis_index("x"), lax.axis_index("y"), lax.axis_index("c")
        chips = [(1 - x, y), (x, 1 - y), (1 - x, 1 - y)]

        def to_sibling(j):
            return pltpu.make_async_remote_copy(src_ref=stage.at[2 * j + 1 - c], dst_ref=from_sib.at[j],
                                                send_sem=sib_send.at[j], recv_sem=sib_recv.at[j],
                                                device_id=(x, y, 1 - c), device_id_type=pl.DeviceIdType.MESH)

        def to_chip(k):
            px, py = chips[k]
            return pltpu.make_async_remote_copy(src_ref=stage.at[4 * px + 2 * py + c], dst_ref=from_chips.at[k],
                                                send_sem=ici_send.at[k], recv_sem=ici_recv.at[k],
                                                device_id=(px, py, c), device_id_type=pl.DeviceIdType.MESH)

        return stage, from_sib, from_chips, loc_sems, (x, y, c), chips, to_sibling, to_chip

    def start(self, ins, outs, scr):
        stage, _, _, loc_sems, _, _, to_sibling, _ = self._parts(scr)
        load = pltpu.make_async_copy(ins[0], stage, loc_sems.at[0])
        load.start()
        load.wait()
        for j in range(4):
            to_sibling(j).start()

    def middle(self, ins, outs, scr):
        stage, from_sib, _, _, (x, y, c), _, to_sibling, to_chip = self._parts(scr)
        for j in range(4):
            to_sibling(j).wait_recv()
            mine = stage.at[2 * j + c]
            mine[...] = (mine[...].astype(F32) + from_sib[j].astype(F32)).astype(mine.dtype)
        for k in range(3):
            to_chip(k).start()

    def wait(self, ins, outs, scr):
        stage, _, from_chips, loc_sems, (x, y, c), chips, to_sibling, to_chip = self._parts(scr)
        writes = [pltpu.make_async_copy(stage.at[4 * x + 2 * y + c], outs[0].at[2 * x + y], loc_sems.at[0])]
        for k, (px, py) in enumerate(chips):
            to_chip(k).wait_recv()
            writes.append(pltpu.make_async_copy(from_chips.at[k], outs[0].at[2 * px + py], loc_sems.at[1 + k]))
        for w in writes:
            w.start()
        for j in range(4):
            to_sibling(j).wait_send()
        for k in range(3):
            to_chip(k).wait_send()
        for w in writes:
            w.wait()


def _call(body, name, args, out_shape, grid=(), in_specs=(), out_specs=(), scratch_shapes=(), exchange=None):
    if exchange is None:
        return pl.pallas_call(body, name=name, grid=grid, in_specs=list(in_specs), out_specs=list(out_specs),
                              out_shape=list(out_shape), scratch_shapes=list(scratch_shapes),
                              compiler_params=_params())(*args), None
    exs = list(exchange) if isinstance(exchange, (list, tuple)) else [exchange]
    ni, no, ns = len(args), len(out_shape), len(scratch_shapes)
    nxi, nxo = sum(len(e.arrs) for e in exs), sum(len(e.out_shape) for e in exs)
    steps = int(np.prod(grid))
    mid_step = lambda e: min(max(int(steps * e.middle_at), 1), steps - 1)
    aliases, iat, oat = {}, ni, no
    for e in exs:
        for src, dst in e.aliases:
            aliases[iat + src] = oat + dst
        iat, oat = iat + len(e.arrs), oat + len(e.out_shape)

    def wrapped(*refs):
        a, xi = refs[:ni], refs[ni:ni + nxi]
        o, xo = refs[ni + nxi:ni + nxi + no], refs[ni + nxi + no:ni + nxi + no + nxo]
        s, xs = refs[ni + nxi + no + nxo:ni + nxi + no + nxo + ns], refs[ni + nxi + no + nxo + ns:]
        parts, iat, oat, sat = [], 0, 0, 0
        for e in exs:
            parts.append((e, xi[iat:iat + len(e.arrs)], xo[oat:oat + len(e.out_shape)], xs[sat:sat + len(e.scratch)]))
            iat, oat, sat = iat + len(e.arrs), oat + len(e.out_shape), sat + len(e.scratch)
        step = 0
        for d, g in enumerate(grid):
            step = step * g + pl.program_id(d)

        @pl.when(step == 0)
        def _():
            for e, ins, outs, sems in parts:
                e.start(ins, outs, sems)

        for at_step in sorted({mid_step(e) for e in exs}):
            @pl.when(step == at_step)
            def _():
                for e, ins, outs, sems in parts:
                    if mid_step(e) == at_step:
                        e.middle(ins, outs, sems)

        body(*a, *o, *s)

        @pl.when(step == steps - 1)
        def _():
            for e, ins, outs, sems in parts:
                e.wait(ins, outs, sems)

    hbm = pl.BlockSpec(memory_space=pltpu.HBM)
    res = pl.pallas_call(
        wrapped, name=name, grid=grid, in_specs=list(in_specs) + [hbm] * nxi, out_specs=list(out_specs) + [hbm] * nxo,
        out_shape=list(out_shape) + [o_ for e in exs for o_ in e.out_shape],
        scratch_shapes=list(scratch_shapes) + [s_ for e in exs for s_ in e.scratch],
        input_output_aliases=aliases, compiler_params=_params())(*args, *[a_ for e in exs for a_ in e.arrs])
    return res[:no], res[no:]


def _gather_two_level(arrs, name):
    n = len(arrs)
    out_shape = [jax.ShapeDtypeStruct((N_DEV,) + a.shape, a.dtype) for a in arrs]

    def body(*refs):
        ins, outs = refs[:n], refs[n:2 * n]
        send_sems, recv_sems, loc_sems = refs[2 * n:]
        x, y, c = lax.axis_index("x"), lax.axis_index("y"), lax.axis_index("c")
        me, sibling = (x, y, c), (x, y, 1 - c)
        chips = [(1 - x, y), (x, 1 - y), (1 - x, 1 - y)]

        def copy(k, j, block, to, src=None):
            px, py, pc = block
            dst = outs[k].at[4 * px + 2 * py + pc]
            return pltpu.make_async_remote_copy(src_ref=dst if src is None else src, dst_ref=dst,
                                                send_sem=send_sems.at[k, j], recv_sem=recv_sems.at[k, j],
                                                device_id=to, device_id_type=pl.DeviceIdType.MESH)

        mine = [pltpu.make_async_copy(ins[k], outs[k].at[4 * x + 2 * y + c], loc_sems.at[k]) for k in range(n)]
        first = []
        for k in range(n):
            mine[k].start()
            first.append(copy(k, 0, me, sibling, src=ins[k]))
            first += [copy(k, 1 + j, me, (*chip, c), src=ins[k]) for j, chip in enumerate(chips)]
        for cp in first:
            cp.start()
        passed = []
        for j, chip in enumerate(chips):
            for k in range(n):
                copy(k, 1 + j, (*chip, c), me).wait_recv()
                passed.append(copy(k, 4 + j, (*chip, c), sibling))
                passed[-1].start()
        for k in range(n):
            copy(k, 0, sibling, me).wait_recv()
            for j, chip in enumerate(chips):
                copy(k, 4 + j, (*chip, 1 - c), me).wait_recv()
        for cp in first + passed:
            cp.wait_send()
        for cp in mine:
            cp.wait()

    vmem = pl.BlockSpec(memory_space=pltpu.VMEM)
    return pl.pallas_call(body, name=name, out_shape=out_shape, in_specs=[vmem] * n, out_specs=[vmem] * n,
                          scratch_shapes=[pltpu.SemaphoreType.DMA((n, 7)), pltpu.SemaphoreType.DMA((n, 7)),
                                          pltpu.SemaphoreType.DMA((n,))], compiler_params=_params())(*arrs)


def _exchange(arrs, scatter, name):
    ex = _Exchange(arrs, scatter)

    def body(*refs):
        ins, outs, sems = refs[:ex.n], refs[ex.n:2 * ex.n], refs[2 * ex.n:]
        ex.start(ins, outs, sems)
        ex.wait(ins, outs, sems)

    hbm = pl.BlockSpec(memory_space=pltpu.HBM)
    return pl.pallas_call(body, name=name, out_shape=ex.out_shape, in_specs=[hbm] * ex.n, out_specs=[hbm] * ex.n,
                          scratch_shapes=ex.scratch)(*ex.arrs)


def _matmul(a, b, mode, name, out_dtype=F32, tm=512, tn=1024, tk=1024, a_fn=None, b_fn=None, extras=(),
            out_slabs=None, exchange=None):
    assert not (a_fn and b_fn) and not (b_fn and mode == "NT")
    if mode == "NN":
        (M, K), N = a.shape, b.shape[1]
    elif mode == "NT":
        (M, K), N = a.shape, b.shape[0]
    else:
        (K, M), N = a.shape, b.shape[1]
    slab_w = N // out_slabs if out_slabs else None
    if out_slabs:
        tn = max(slab_w, min(tn, N) // slab_w * slab_w)
    tm, tn, tk = min(tm, M), min(tn, N), min(tk, K)
    assert M % tm == 0 and N % tn == 0 and K % tk == 0, (name, M, N, K)
    nk = K // tk
    dims = {"NN": NN, "NT": NT, "TN": TN}[mode]
    ne = len(extras)

    def body(a_ref, b_ref, *rest):
        e_refs, o_ref, acc_ref = rest[:ne], rest[ne], rest[ne + 1]
        k = pl.program_id(2)

        @pl.when(k == 0)
        def _():
            acc_ref[...] = jnp.zeros_like(acc_ref)

        at, bt = a_ref[...], b_ref[...]
        if a_fn is not None:
            at = a_fn(at.astype(F32), *[e[...] for e in e_refs])
        if b_fn is not None:
            bt = b_fn(bt.astype(F32), *[e[...] for e in e_refs])
        acc_ref[...] += _bdot(at, bt, dims)

        @pl.when(k == nk - 1)
        def _():
            if out_slabs:
                for s in range(tn // slab_w):
                    o_ref[s] = acc_ref[:, s * slab_w:(s + 1) * slab_w].astype(out_dtype)
            else:
                o_ref[...] = acc_ref[...].astype(out_dtype)

    if mode == "TN":
        a_spec = pl.BlockSpec((tk, tm), lambda i, j, k: (k, i))
        e_spec = pl.BlockSpec((1, tm), lambda i, j, k: (0, i))
    else:
        a_spec = pl.BlockSpec((tm, tk), lambda i, j, k: (i, k))
        e_spec = pl.BlockSpec((1, tk), lambda i, j, k: (0, k))
    if mode == "NT":
        b_spec = pl.BlockSpec((tn, tk), lambda i, j, k: (j, k))
    else:
        b_spec = pl.BlockSpec((tk, tn), lambda i, j, k: (k, j))
    if b_fn is not None:
        e_spec = pl.BlockSpec((1, tn), lambda i, j, k: (0, j))
    if out_slabs:
        o_shape = jax.ShapeDtypeStruct((out_slabs, M, slab_w), out_dtype)
        o_spec = pl.BlockSpec((tn // slab_w, tm, slab_w), lambda i, j, k: (j, i, 0))
    else:
        o_shape = jax.ShapeDtypeStruct((M, N), out_dtype)
        o_spec = pl.BlockSpec((tm, tn), lambda i, j, k: (i, j))
    (out,), got = _call(body, name, (a, b, *extras), [o_shape], grid=(M // tm, N // tn, nk),
                        in_specs=[a_spec, b_spec] + [e_spec] * ne, out_specs=[o_spec],
                        scratch_shapes=[pltpu.VMEM((tm, tn), F32)], exchange=exchange)
    return out if exchange is None else (out, got)


def _modulate(x, sc, sh):
    return x * (1.0 + sc) + sh


def _square(x):
    return x * x


def _mod_part(c_all, w_ada_s, b_s):
    def body(c_ref, w_ref, b_ref, mod_ref, cond_ref):
        cv = c_ref[...]
        cond = cv * _sigmoid(cv)
        cond_ref[...] = cond
        mod_ref[...] = _bdot(cond, w_ref[...]) + b_ref[...]

    return pl.pallas_call(
        body, name="mod_part",
        out_shape=[jax.ShapeDtypeStruct((N_DEV, w_ada_s.shape[1]), F32), jax.ShapeDtypeStruct(c_all.shape, F32)],
        compiler_params=_params(),
    )(c_all, w_ada_s, b_s)


def _rms_fwd(x, w):
    rs = lax.rsqrt(_lanemean(x * x) + RMS_EPS)
    return x * rs * w, rs


def _rms_bwd(x, rs, w, dy):
    xhat = x * rs
    dxh = dy * w
    return rs * (dxh - xhat * _lanemean(dxh * xhat)), dy * xhat


def _mla_pre(z, pos_col, invf, m_rot, wq_ext, wkv_ext, qnw, kvnw, exchange=None):
    T = z.shape[0]
    tm = min(ROW_TILE, T)

    def body(z_ref, pos_ref, invf_ref, mrot_ref, wq_ref, wkv_ref, qnw_ref, kvnw_ref,
             q_ref, k_ref, v_ref, c1_ref, s1_ref, cqn_ref, ckvn_ref):
        hi = slice(HEAD_DIM, QK_PAD)
        ang = pos_ref[...].astype(F32) * invf_ref[:, hi]
        c1 = jnp.concatenate([jnp.ones((tm, HEAD_DIM), F32), mrot_ref[:, hi] * jnp.cos(ang)], axis=1)
        s1 = jnp.concatenate([jnp.zeros((tm, HEAD_DIM), F32), mrot_ref[:, hi] * jnp.sin(ang)], axis=1)
        c1_ref[...] = c1
        s1_ref[...] = s1
        cqn, _ = _rms_fwd(z_ref[:, 0:256], qnw_ref[...])
        ckvn, _ = _rms_fwd(z_ref[:, 256:512], kvnw_ref[...])
        cqn_ref[...] = cqn.astype(BF16)
        ckvn_ref[...] = ckvn.astype(BF16)
        qe = _bdot(cqn, wq_ref[...], NT)
        kve = _bdot(ckvn, wkv_ref[...])
        k_rope = z_ref[:, 512:768] * c1 + z_ref[:, 768:1024] * s1
        for h in range(HEADS):
            q_ref[h] = ((qe[:, 256 * h:256 * h + 256] * c1 + qe[:, 1024 + 256 * h:1280 + 256 * h] * s1)
                        * Q_PRESCALE).astype(BF16)
            k_ref[h] = (kve[:, 256 * h:256 * h + 256] + k_rope).astype(BF16)
            v_ref[h] = kve[:, 1024 + 128 * h:1152 + 128 * h].astype(BF16)

    row = lambda i: (i, 0)
    head = lambda i: (0, i, 0)
    return _call(
        body, "mla_pre", (z, pos_col, invf, m_rot, wq_ext, wkv_ext, qnw, kvnw), grid=(T // tm,),
        in_specs=[pl.BlockSpec((tm, 1024), lambda i: (i, 2)), pl.BlockSpec((tm, 1), row),
                  _full((1, 256)), _full((1, 256)), _full(wq_ext.shape), _full(wkv_ext.shape),
                  _full((1, 256)), _full((1, 256))],
        out_specs=[pl.BlockSpec((HEADS, tm, QK_PAD), head), pl.BlockSpec((HEADS, tm, QK_PAD), head),
                   pl.BlockSpec((HEADS, tm, HEAD_DIM), head), pl.BlockSpec((tm, 256), row), pl.BlockSpec((tm, 256), row),
                   pl.BlockSpec((tm, 256), row), pl.BlockSpec((tm, 256), row)],
        out_shape=[jax.ShapeDtypeStruct((HEADS, T, QK_PAD), BF16), jax.ShapeDtypeStruct((HEADS, T, QK_PAD), BF16),
                   jax.ShapeDtypeStruct((HEADS, T, HEAD_DIM), BF16), jax.ShapeDtypeStruct((T, 256), F32),
                   jax.ShapeDtypeStruct((T, 256), F32), jax.ShapeDtypeStruct((T, 256), BF16),
                   jax.ShapeDtypeStruct((T, 256), BF16)], exchange=exchange)


def _mla_bwd(dq, dk, dv, z, c1, s1, wq_ext, wkv_ext, qnw, kvnw):
    T = z.shape[0]
    tm = min(ROW_TILE, T)

    def body(dq_ref, dk_ref, dv_ref, z_ref, c1_ref, s1_ref, wq_ref, wkv_ref, qnw_ref, kvnw_ref,
             dz_ref, dqe_ref, dkve_ref, dnw_ref):
        @pl.when(pl.program_id(0) == 0)
        def _():
            dnw_ref[...] = jnp.zeros_like(dnw_ref)

        c1, s1 = c1_ref[...], s1_ref[...]
        dkpe = jnp.zeros((tm, QK_PAD), F32)
        for h in range(HEADS):
            dqh, dkh = dq_ref[h].astype(F32) * ATT_SCALE, dk_ref[h]
            dqe_ref[:, 256 * h:256 * h + 256] = (dqh * c1).astype(BF16)
            dqe_ref[:, 1024 + 256 * h:1280 + 256 * h] = (dqh * s1).astype(BF16)
            dkve_ref[:, 256 * h:256 * h + 256] = dkh
            dkve_ref[:, 1024 + 128 * h:1152 + 128 * h] = dv_ref[h]
            dkpe = dkpe + dkh.astype(F32)
        dcqn = _dot(dqe_ref[...], wq_ref[...])
        dckvn = _dot(dkve_ref[...], wkv_ref[...], NT)
        cq, ckv = z_ref[:, 0:256], z_ref[:, 256:512]
        _, rsq = _rms_fwd(cq, qnw_ref[...])
        _, rskv = _rms_fwd(ckv, kvnw_ref[...])
        dcq, wq_rows = _rms_bwd(cq, rsq, qnw_ref[...], dcqn)
        dckv, wkv_rows = _rms_bwd(ckv, rskv, kvnw_ref[...], dckvn)
        dnw_ref[:, 0:256] += _rowsum(wq_rows)
        dnw_ref[:, 256:512] += _rowsum(wkv_rows)
        dz_ref[:, 0:256] = dcq.astype(BF16)
        dz_ref[:, 256:512] = dckv.astype(BF16)
        dz_ref[:, 512:768] = (dkpe * c1).astype(BF16)
        dz_ref[:, 768:1024] = (dkpe * s1).astype(BF16)

    row = lambda i: (i, 0)
    head = lambda i: (0, i, 0)
    return pl.pallas_call(
        body, name="mla_bwd", grid=(T // tm,),
        in_specs=[pl.BlockSpec((HEADS, tm, QK_PAD), head), pl.BlockSpec((HEADS, tm, QK_PAD), head),
                  pl.BlockSpec((HEADS, tm, HEAD_DIM), head), pl.BlockSpec((tm, 1024), lambda i: (i, 2)),
                  pl.BlockSpec((tm, 256), row), pl.BlockSpec((tm, 256), row), _full(wq_ext.shape), _full(wkv_ext.shape),
                  _full((1, 256)), _full((1, 256))],
        out_specs=[pl.BlockSpec((tm, 1024), row), pl.BlockSpec((tm, 2048), row), pl.BlockSpec((tm, 1536), row),
                   _full((1, 512))],
        out_shape=[jax.ShapeDtypeStruct((T, 1024), BF16), jax.ShapeDtypeStruct((T, 2048), BF16),
                   jax.ShapeDtypeStruct((T, 1536), BF16), jax.ShapeDtypeStruct((1, 512), F32)],
        compiler_params=_params(),
    )(dq, dk, dv, z, c1, s1, wq_ext, wkv_ext, qnw, kvnw)


_HEAD_LANES = [slice(HEAD_DIM * h, HEAD_DIM * (h + 1)) for h in range(HEADS)]


def _lower_bound(lbraw_ref):
    a0, a1 = lbraw_ref[0:1, :], lbraw_ref[1:2, :]
    mx = jnp.maximum(a0, a1)
    e0, e1 = jnp.exp(a0 - mx), jnp.exp(a1 - mx)
    return e0 / (e0 + e1)


def _tri(lower):
    r = lax.broadcasted_iota(jnp.int32, (CHUNK, CHUNK), 0)
    c = lax.broadcasted_iota(jnp.int32, (CHUNK, CHUNK), 1)
    return (r >= c) if lower else (r <= c)


def _hgrn_gates(q, f, lb, tri_lo):
    sg = _sigmoid(f)
    forget = lb + (1.0 - lb) * sg
    k = 1.0 - forget
    b = _hdot(tri_lo.astype(F32), jnp.log(forget))
    b_ref, b_last = b[CHUNK // 2 - 1:CHUNK // 2, :], b[CHUNK - 1:CHUNK, :]
    e1, e2, e3, e4 = jnp.exp(b - b_ref), jnp.exp(b_ref - b), jnp.exp(b_last - b), jnp.exp(b)
    return dict(sg=sg, forget=forget, k=k, e1=e1, e2=e2, e3=e3, e4=e4, qa=q * e1, ka=k * e2, kl=k * e3, qb=q * e4,
                decay=jnp.exp(b_last))


def _hgrn_fwd(z, lbraw, nw, exchange=None):
    T = z.shape[0]
    G = min(HGRN_GROUP, T // CHUNK)
    rows = G * CHUNK
    n_chunks = T // CHUNK

    def body(q_ref, f_ref, i_ref, g_ref, lbraw_ref, nw_ref, oraw_ref, og_ref, sp_ref, st_ref):
        @pl.when(pl.program_id(0) == 0)
        def _():
            st_ref[...] = jnp.zeros_like(st_ref)

        lb_all = _lower_bound(lbraw_ref)
        tri_lo = _tri(True)

        def chunk(cc, carry):
            rs = pl.ds(pl.multiple_of(cc * CHUNK, CHUNK), CHUNK)
            t = _hgrn_gates(q_ref[rs, :], f_ref[rs, :], lb_all, tri_lo)
            v, gate = i_ref[rs, :], g_ref[rs, :]
            st = [st_ref[h] for h in range(HEADS)]
            a = [jnp.where(tri_lo, _bdot(t["qa"][:, s], t["ka"][:, s], NT), 0.0) for s in _HEAD_LANES]
            kv = [_bdot(v[:, s], t["kl"][:, s], TN) for s in _HEAD_LANES]
            o = [_bdot(a[h], v[:, s]) + _bdot(t["qb"][:, s], st[h], NT) for h, s in enumerate(_HEAD_LANES)]
            for h, s in enumerate(_HEAD_LANES):
                sp_ref[cc, h] = st[h]
                st_ref[h] = st[h] * t["decay"][:, s] + kv[h]
            oraw_ref[rs, :] = jnp.concatenate(o, axis=1)
            on = jnp.concatenate([_rms_fwd(o[h], nw_ref[:, s])[0] for h, s in enumerate(_HEAD_LANES)], axis=1)
            og_ref[rs, :] = (on * (gate * _sigmoid(gate))).astype(BF16)
            return carry

        lax.fori_loop(0, G, chunk, 0, unroll=4)

    col = lambda j: pl.BlockSpec((rows, 512), lambda r, j=j: (r, j))
    return _call(
        body, "hgrn_fwd", (z, z, z, z, lbraw, nw), grid=(T // rows,),
        in_specs=[col(0), col(1), col(2), col(3), _full((2, 512)), _full((1, 512))],
        out_specs=[col(0), col(0), pl.BlockSpec((G, HEADS, HEAD_DIM, HEAD_DIM), lambda r: (r, 0, 0, 0))],
        out_shape=[jax.ShapeDtypeStruct((T, 512), F32), jax.ShapeDtypeStruct((T, 512), BF16),
                   jax.ShapeDtypeStruct((n_chunks, HEADS, HEAD_DIM, HEAD_DIM), F32)],
        scratch_shapes=[pltpu.VMEM((HEADS, HEAD_DIM, HEAD_DIM), F32)], exchange=exchange)


def _hgrn_bwd(dmixcat, z, oraw, sprev, lbraw, nw, exchange=None):
    T = z.shape[0]
    G = min(HGRN_GROUP, T // CHUNK)
    rows = G * CHUNK
    ng = T // rows

    def body(dog_ref, q_ref, f_ref, i_ref, g_ref, oraw_ref, sp_ref, lbraw_ref, nw_ref,
             dz_ref, dsmall_ref, dst_ref):
        @pl.when(pl.program_id(0) == 0)
        def _():
            dst_ref[...] = jnp.zeros_like(dst_ref)
            dsmall_ref[...] = jnp.zeros_like(dsmall_ref)

        lb_all = _lower_bound(lbraw_ref)
        tri_lo, tri_up = _tri(True), _tri(False)
        rowid = lax.broadcasted_iota(jnp.int32, (CHUNK, HEADS * HEAD_DIM), 0)

        def chunk(it, carry):
            cc = G - 1 - it
            rs = pl.ds(pl.multiple_of(cc * CHUNK, CHUNK), CHUNK)
            heads = list(enumerate(_HEAD_LANES))
            cat = lambda parts: jnp.concatenate(parts, axis=1)
            per_head_mean = lambda x: cat([jnp.broadcast_to(_lanemean(x[:, s]), (CHUNK, HEAD_DIM)) for s in _HEAD_LANES])
            t = _hgrn_gates(q_ref[rs, :], f_ref[rs, :], lb_all, tri_lo)
            v, gate, o, dog, nw_all = i_ref[rs, :], g_ref[rs, :], oraw_ref[rs, :], dog_ref[rs, :], nw_ref[...]
            rs_o = lax.rsqrt(per_head_mean(o * o) + RMS_EPS)
            xhat = o * rs_o
            sgg = _sigmoid(gate)
            d_on = dog * (gate * sgg)
            dz_ref[rs, 1536:2048] = (dog * (xhat * nw_all) * (sgg * (1.0 + gate * (1.0 - sgg)))).astype(BF16)
            dxh = d_on * nw_all
            do = rs_o * (dxh - xhat * per_head_mean(dxh * xhat))
            dsmall_ref[:, 512:1024] += _rowsum(d_on * xhat)
            st = [sp_ref[cc, h] for h in range(HEADS)]
            dst = [dst_ref[h] for h in range(HEADS)]
            a = [jnp.where(tri_lo, _bdot(t["qa"][:, s], t["ka"][:, s], NT), 0.0) for s in _HEAD_LANES]
            da = [jnp.where(tri_lo, _bdot(do[:, s], v[:, s], NT), 0.0) for s in _HEAD_LANES]
            dqb = cat([_bdot(do[:, s], st[h]) for h, s in heads])
            dkl = cat([_bdot(v[:, s], dst[h]) for h, s in heads])
            dv_ = cat([_bdot(t["kl"][:, s], dst[h], NT) + _bdot(a[h], do[:, s], TN) for h, s in heads])
            dqa = cat([_bdot(da[h], t["ka"][:, s]) for h, s in heads])
            dka = cat([_bdot(da[h], t["qa"][:, s], TN) for h, s in heads])
            ddecay = cat([_rowsum(dst[h] * st[h]) for h in range(HEADS)])
            for h, s in heads:
                dst_ref[h] = dst[h] * t["decay"][:, s] + _bdot(do[:, s], t["qb"][:, s], TN)
            pa, pk, pb, pl_ = dqa * t["qa"], dka * t["ka"], dqb * t["qb"], dkl * t["kl"]
            db = pa - pk + pb - pl_
            db = db + jnp.where(rowid == CHUNK // 2 - 1, _rowsum(pk - pa), 0.0)
            db = db + jnp.where(rowid == CHUNK - 1, _rowsum(pl_) + ddecay * t["decay"], 0.0)
            dlogf = _hdot(tri_up.astype(F32), db)
            dforget = dlogf / t["forget"] - (dka * t["e2"] + dkl * t["e3"])
            sg = t["sg"]
            dz_ref[rs, 0:512] = (dqa * t["e1"] + dqb * t["e4"]).astype(BF16)
            dz_ref[rs, 512:1024] = (dforget * (1.0 - lb_all) * sg * (1.0 - sg)).astype(BF16)
            dz_ref[rs, 1024:1536] = dv_.astype(BF16)
            dsmall_ref[:, 0:512] += _rowsum(dforget * (1.0 - sg))
            return carry

        lax.fori_loop(0, G, chunk, 0, unroll=4)

    col = lambda j: pl.BlockSpec((rows, 512), lambda r, j=j: (ng - 1 - r, j))
    return _call(
        body, "hgrn_bwd", (dmixcat, z, z, z, z, oraw, sprev, lbraw, nw), grid=(ng,),
        in_specs=[col(0), col(0), col(1), col(2), col(3), col(0),
                  pl.BlockSpec((G, HEADS, HEAD_DIM, HEAD_DIM), lambda r: (ng - 1 - r, 0, 0, 0)),
                  _full((2, 512)), _full((1, 512))],
        out_specs=[pl.BlockSpec((rows, 2048), lambda r: (ng - 1 - r, 0)), _full((1, 1024))],
        out_shape=[jax.ShapeDtypeStruct((T, 2048), BF16), jax.ShapeDtypeStruct((1, 1024), F32)],
        scratch_shapes=[pltpu.VMEM((HEADS, HEAD_DIM, HEAD_DIM), F32)], exchange=exchange)


def _diag_mask(t):
    r = lax.broadcasted_iota(jnp.int32, (t, t), 0)
    c = lax.broadcasted_iota(jnp.int32, (t, t), 1)
    return r >= c


def _attn_fwd(q, k, v, exchange=None):
    _, T, _ = q.shape
    t = min(ATT_TILE, T)

    def body(q_ref, k_ref, v_ref, o_ref, lse_ref):
        i = pl.program_id(1)
        qb = q_ref[...]

        rows = lambda j: pl.ds(pl.multiple_of(j * t, t), t)

        def logits(j, masked):
            s = _dot(qb, k_ref[rows(j), :], NT)
            return jnp.where(_diag_mask(t), s, NEG_BIG) if masked else s

        def absorb(s, j, carry):
            m, l, acc = carry
            mn = jnp.maximum(m, jnp.max(s, axis=-1, keepdims=True))
            p = jnp.exp2(s - mn)
            al = jnp.exp2(m - mn)
            return mn, al * l + jnp.sum(p, axis=-1, keepdims=True), al * acc + _dot(p.astype(BF16), v_ref[rows(j), :])

        def pair(j0, carry, last_masked):
            s0, s1 = logits(j0, False), logits(j0 + 1, last_masked)
            return absorb(s1, j0 + 1, absorb(s0, j0, carry))

        init = (jnp.full((t, 1), NEG_BIG, F32), jnp.zeros((t, 1), F32), jnp.zeros((t, HEAD_DIM), F32))
        carry = lax.fori_loop(0, i // 2, lambda jj, c: pair(2 * jj, c, False), init)
        m, l, acc = lax.cond(i % 2 == 1, lambda c: pair(i - 1, c, True),
                             lambda c: absorb(logits(i, True), i, c), carry)
        o_ref[...] = acc / l
        lse_ref[...] = jnp.broadcast_to(m + jnp.log2(l), (t, HEAD_DIM))

    return _call(
        body, "attn_fwd", (q, k, v), grid=(HEADS, T // t),
        in_specs=[pl.BlockSpec((None, t, QK_PAD), lambda h, i: (h, i, 0)),
                  pl.BlockSpec((None, T, QK_PAD), lambda h, i: (h, 0, 0)),
                  pl.BlockSpec((None, T, HEAD_DIM), lambda h, i: (h, 0, 0))],
        out_specs=[pl.BlockSpec((t, HEAD_DIM), lambda h, i: (i, h)),
                   pl.BlockSpec((None, t, HEAD_DIM), lambda h, i: (h, i, 0))],
        out_shape=[jax.ShapeDtypeStruct((T, HEADS * HEAD_DIM), F32), jax.ShapeDtypeStruct((HEADS, T, HEAD_DIM), F32)],
        exchange=exchange)


def _attn_bwd(q, k, v, dmixcat, o, lse, exchange=None):
    _, T, _ = q.shape
    t = min(ATT_TILE, T)
    nq = T // t

    def body(q_ref, k_ref, v_ref, do_ref, o_ref, lse_ref, dq_ref, dk_ref, dv_ref, delta_ref, dq_acc):
        j = pl.program_id(1)

        @pl.when(j == 0)
        def _():
            dq_acc[...] = jnp.zeros_like(dq_acc)

            def fill(i, carry):
                rs = pl.ds(pl.multiple_of(i * t, t), t)
                delta_ref[rs, :] = jnp.broadcast_to(
                    jnp.sum(do_ref[rs, :] * o_ref[rs, :], axis=-1, keepdims=True), (t, HEAD_DIM))
                return carry

            lax.fori_loop(0, nq, fill, 0)

        kb, vb = k_ref[...], v_ref[...]

        def steps(blocks, carry):
            dk, dv = carry
            rs = [pl.ds(pl.multiple_of(i * t, t), t) for i, _ in blocks]
            qb = [q_ref[r, :] for r in rs]
            dob = [do_ref[r, :].astype(BF16) for r in rs]
            s = [_dot(b, kb, NT) for b in qb]
            dp = [_dot(b, vb, NT) for b in dob]
            for n, (_, shift) in enumerate(blocks):
                p = jnp.exp2(s[n] - lse_ref[rs[n], 0:1])
                if shift is not None:
                    row = lax.broadcasted_iota(jnp.int32, (t, 2 * t), 0)
                    col = lax.broadcasted_iota(jnp.int32, (t, 2 * t), 1)
                    p = jnp.where(col <= row + shift, p, 0.0)
                ds = (p * (dp[n] - delta_ref[rs[n], 0:1])).astype(BF16)
                dq_acc[rs[n], :] += _dot(ds, kb)
                dk = dk + _dot(ds, qb[n], TN)
                dv = dv + _dot(p.astype(BF16), dob[n], TN)
            return dk, dv

        zero = (jnp.zeros((2 * t, QK_PAD), F32), jnp.zeros((2 * t, HEAD_DIM), F32))
        carry = steps([(2 * j, 0), (2 * j + 1, t)], zero)
        first = 2 * j + 2
        dk, dv = lax.fori_loop(0, (nq - first) // 2, lambda n, c: steps([(first + 2 * n, None), (first + 2 * n + 1, None)], c),
                               carry)
        dk_ref[...] = (dk * LN2).astype(BF16)
        dv_ref[...] = dv.astype(BF16)

        @pl.when(j == nk - 1)
        def _():
            dq_ref[...] = dq_acc[...].astype(BF16)

    nk = nq // 2
    return _call(
        body, "attn_bwd", (q, k, v, dmixcat, o, lse), grid=(HEADS, nk),
        in_specs=[pl.BlockSpec((None, T, QK_PAD), lambda h, j: (h, 0, 0)),
                  pl.BlockSpec((None, 2 * t, QK_PAD), lambda h, j: (h, j, 0)),
                  pl.BlockSpec((None, 2 * t, HEAD_DIM), lambda h, j: (h, j, 0)),
                  pl.BlockSpec((T, HEAD_DIM), lambda h, j: (0, HEADS + h)),
                  pl.BlockSpec((T, HEAD_DIM), lambda h, j: (0, h)),
                  pl.BlockSpec((None, T, HEAD_DIM), lambda h, j: (h, 0, 0))],
        out_specs=[pl.BlockSpec((None, T, QK_PAD), lambda h, j: (h, 0, 0)),
                   pl.BlockSpec((None, 2 * t, QK_PAD), lambda h, j: (h, j, 0)),
                   pl.BlockSpec((None, 2 * t, HEAD_DIM), lambda h, j: (h, j, 0))],
        out_shape=[jax.ShapeDtypeStruct((HEADS, T, QK_PAD), BF16), jax.ShapeDtypeStruct((HEADS, T, QK_PAD), BF16),
                   jax.ShapeDtypeStruct((HEADS, T, HEAD_DIM), BF16)],
        scratch_shapes=[pltpu.VMEM((T, HEAD_DIM), F32), pltpu.VMEM((T, QK_PAD), F32)], exchange=exchange)


def _ln_fwd(r):
    mu = _lanemean(r)
    xc = r - mu
    rstd = lax.rsqrt(_lanemean(xc * xc) + LN_EPS)
    return xc * rstd, rstd


def _ln_bwd(dxh, xhat, rstd):
    return rstd * (dxh - _lanemean(dxh) - xhat * _lanemean(dxh * xhat))


def _mix_ln1(o_hg, o_mla, w_out, x, g_a, ln1_g, ln1_b, sc_m, sh_m, exchange=None):
    T = x.shape[0]
    tm = min(ROW_TILE, T)
    half = o_hg.shape[1]

    def body(hg_ref, mla_ref, w_ref, x_ref, ga_ref, g_ref, b_ref, sc_ref, sh_ref, mix_ref, xhat_ref, rstd_ref, u2_ref):
        mix = _dot(hg_ref[...], w_ref[0:half, :]) + _bdot(mla_ref[...], w_ref[half:, :])
        mix_ref[...] = mix
        xhat, rstd = _ln_fwd(ALPHA * x_ref[...] + (1.0 + ga_ref[...]) * mix)
        xhat_ref[...] = xhat
        rstd_ref[...] = jnp.broadcast_to(rstd, (tm, 128))
        u2_ref[...] = _modulate(xhat * g_ref[...] + b_ref[...], sc_ref[...], sh_ref[...]).astype(BF16)

    row = pl.BlockSpec((tm, D_MODEL), lambda i: (i, 0))
    vec = _full((1, D_MODEL))
    halfrow = pl.BlockSpec((tm, half), lambda i: (i, 0))
    return _call(
        body, "mix_ln1", (o_hg, o_mla, w_out, x, g_a, ln1_g, ln1_b, sc_m, sh_m), grid=(T // tm,),
        in_specs=[halfrow, halfrow, _full(w_out.shape), row, vec, vec, vec, vec, vec],
        out_specs=[row, row, pl.BlockSpec((tm, 128), lambda i: (i, 0)), row],
        out_shape=[jax.ShapeDtypeStruct((T, D_MODEL), F32), jax.ShapeDtypeStruct((T, D_MODEL), F32),
                   jax.ShapeDtypeStruct((T, 128), F32), jax.ShapeDtypeStruct((T, D_MODEL), BF16)],
        exchange=exchange)


def _mlp_fwd(u2, w1, w2, xhat1, ln1_g, ln1_b, g_m, ln2_g, ln2_b, target):
    T = u2.shape[0]
    half, tf = w1[0].shape[1:]
    nf = N_DEV // MLP_SLABS
    tm = min(ROW_TILE, T)

    def body(u2_ref, w1a_ref, w1b_ref, w2_ref, xhat_ref, g1_ref, b1_ref, gm_ref, g2_ref, b2_ref, tgt_ref,
             r_ref, dr2_ref, dh_ref, small_ref, acc_ref):
        i, f = pl.program_id(0), pl.program_id(1)
        dm = D_MODEL

        @pl.when((i == 0) & (f == 0))
        def _():
            small_ref[...] = jnp.zeros_like(small_ref)

        @pl.when(f == 0)
        def _():
            acc_ref[...] = jnp.zeros_like(acc_ref)

        u2t = u2_ref[...]
        part = None
        for s in range(MLP_SLABS):
            r = jnp.maximum(_dot(u2t[:, :half], w1a_ref[s]) + _dot(u2t[:, half:], w1b_ref[s]), 0.0)
            r_ref[:, s * tf:(s + 1) * tf] = r.astype(BF16)
            d = _bdot(r * r, w2_ref[s])
            part = d if part is None else part + d
        acc_ref[...] += part

        @pl.when(f == nf - 1)
        def _():
            h = acc_ref[...]
            x1 = xhat_ref[...] * g1_ref[...] + b1_ref[...]
            xhat2, rstd2 = _ln_fwd(ALPHA * x1 + (1.0 + gm_ref[...]) * h)
            err = xhat2 * g2_ref[...] + b2_ref[...] - tgt_ref[...]
            small_ref[:, 3 * dm:] += jnp.sum(0.5 * _lanemean(err * err), axis=0, keepdims=True)
            dy = err * (1.0 / D_MODEL)
            small_ref[:, dm:2 * dm] += _rowsum(dy * xhat2)
            small_ref[:, 2 * dm:3 * dm] += _rowsum(dy)
            dr2 = _ln_bwd(dy * g2_ref[...], xhat2, rstd2)
            dr2_ref[...] = dr2
            small_ref[:, 0:dm] += _rowsum(dr2 * h)
            dh_ref[...] = ((1.0 + gm_ref[...]) * dr2).astype(BF16)

    row = pl.BlockSpec((tm, D_MODEL), lambda i, f: (i, 0))
    vec = _full((1, D_MODEL))
    return pl.pallas_call(
        body, name="mlp_fwd", grid=(T // tm, nf),
        in_specs=[row, pl.BlockSpec((MLP_SLABS, half, tf), lambda i, f: (f, 0, 0)),
                  pl.BlockSpec((MLP_SLABS, half, tf), lambda i, f: (f, 0, 0)),
                  pl.BlockSpec((MLP_SLABS, tf, D_MODEL), lambda i, f: (f, 0, 0)),
                  row, vec, vec, vec, vec, vec, row],
        out_specs=[pl.BlockSpec((tm, MLP_SLABS * tf), lambda i, f: (i, f)), row, row, _full((1, 3 * D_MODEL + 128))],
        out_shape=[jax.ShapeDtypeStruct((T, N_DEV * tf), BF16), jax.ShapeDtypeStruct((T, D_MODEL), F32),
                   jax.ShapeDtypeStruct((T, D_MODEL), BF16), jax.ShapeDtypeStruct((1, 3 * D_MODEL + 128), F32)],
        scratch_shapes=[pltpu.VMEM((tm, D_MODEL), F32)],
        compiler_params=_params(),
    )(u2, w1[0], w1[1], w2, xhat1, ln1_g, ln1_b, g_m, ln2_g, ln2_b, target)


def _mlp_bwd(dh, w1, w2, r, dr2, xhat1, rstd1, mix, ln1_g, ln1_b, sc_m, g_a):
    T = dh.shape[0]
    half, tf = w1[0].shape[1:]
    nf = N_DEV // MLP_SLABS
    tm = min(ROW_TILE, T)

    def body(dh_ref, w1a_ref, w1b_ref, w2_ref, r_ref, dr2_ref, xhat_ref, rstd_ref, mix_ref, g1_ref, b1_ref, sc_ref, ga_ref,
             dhpre_ref, dr1_ref, dmix_ref, small_ref, acc_ref):
        i, f = pl.program_id(0), pl.program_id(1)
        dm = D_MODEL

        @pl.when((i == 0) & (f == 0))
        def _():
            small_ref[...] = jnp.zeros_like(small_ref)

        @pl.when(f == 0)
        def _():
            acc_ref[...] = jnp.zeros_like(acc_ref)

        dht = dh_ref[...]
        part = None
        for s in range(MLP_SLABS):
            cols = slice(s * tf, (s + 1) * tf)
            dhpre = (_dot(dht, w2_ref[s], NT) * (2.0 * r_ref[:, cols].astype(F32))).astype(BF16)
            dhpre_ref[:, cols] = dhpre
            d = jnp.concatenate([_dot(dhpre, w1a_ref[s], NT), _dot(dhpre, w1b_ref[s], NT)], axis=1)
            part = d if part is None else part + d
        acc_ref[...] += part

        @pl.when(f == nf - 1)
        def _():
            du2 = acc_ref[...]
            xhat = xhat_ref[...]
            x1 = xhat * g1_ref[...] + b1_ref[...]
            dx1 = ALPHA * dr2_ref[...] + du2 * (1.0 + sc_ref[...])
            small_ref[:, 2 * dm:3 * dm] += _rowsum(du2 * x1)
            small_ref[:, dm:2 * dm] += _rowsum(du2)
            small_ref[:, 3 * dm:4 * dm] += _rowsum(dx1 * xhat)
            small_ref[:, 4 * dm:5 * dm] += _rowsum(dx1)
            dr1 = _ln_bwd(dx1 * g1_ref[...], xhat, rstd_ref[:, 0:1])
            dr1_ref[...] = dr1
            small_ref[:, 0:dm] += _rowsum(dr1 * mix_ref[...])
            dmix_ref[...] = ((1.0 + ga_ref[...]) * dr1).astype(BF16)

    row = pl.BlockSpec((tm, D_MODEL), lambda i, f: (i, 0))
    vec = _full((1, D_MODEL))
    return pl.pallas_call(
        body, name="mlp_bwd", grid=(T // tm, nf),
        in_specs=[row, pl.BlockSpec((MLP_SLABS, half, tf), lambda i, f: (f, 0, 0)),
                  pl.BlockSpec((MLP_SLABS, half, tf), lambda i, f: (f, 0, 0)),
                  pl.BlockSpec((MLP_SLABS, tf, D_MODEL), lambda i, f: (f, 0, 0)),
                  pl.BlockSpec((tm, MLP_SLABS * tf), lambda i, f: (i, f)), row, row,
                  pl.BlockSpec((tm, 128), lambda i, f: (i, 0)), row, vec, vec, vec, vec],
        out_specs=[pl.BlockSpec((tm, MLP_SLABS * tf), lambda i, f: (i, f)), row, row, _full((1, 5 * D_MODEL))],
        out_shape=[jax.ShapeDtypeStruct((T, N_DEV * tf), BF16), jax.ShapeDtypeStruct((T, D_MODEL), F32),
                   jax.ShapeDtypeStruct((T, D_MODEL), BF16), jax.ShapeDtypeStruct((1, 5 * D_MODEL), F32)],
        scratch_shapes=[pltpu.VMEM((tm, D_MODEL), F32)],
        compiler_params=_params(),
    )(dh, w1[0], w1[1], w2, r, dr2, xhat1, rstd1, mix, ln1_g, ln1_b, sc_m, g_a)


def _input_bwd(dz_h, dz_m, w_in_ext, x, dr1, sc_a, exchange=None):
    T = x.shape[0]
    tm = min(ROW_TILE, T)

    def body(dzh_ref, dzm_ref, w_ref, x_ref, dr1_ref, sc_ref, gx_ref, small_ref):
        @pl.when(pl.program_id(0) == 0)
        def _():
            small_ref[...] = jnp.zeros_like(small_ref)

        du = _bdot(dzh_ref[...], w_ref[0:2048, :]) + _bdot(dzm_ref[...], w_ref[2048:3072, :])
        gx_ref[...] = ALPHA * dr1_ref[...] + du * (1.0 + sc_ref[...])
        small_ref[:, D_MODEL:] += _rowsum(du * x_ref[...])
        small_ref[:, 0:D_MODEL] += _rowsum(du)

    row = pl.BlockSpec((tm, D_MODEL), lambda i: (i, 0))
    vec = _full((1, D_MODEL))
    return _call(
        body, "input_bwd", (dz_h, dz_m, w_in_ext, x, dr1, sc_a), grid=(T // tm,),
        in_specs=[pl.BlockSpec((tm, 2048), lambda i: (i, 0)), row, _full(w_in_ext.shape), row, row, vec],
        out_specs=[row, _full((1, 2 * D_MODEL))],
        out_shape=[jax.ShapeDtypeStruct((T, D_MODEL), F32), jax.ShapeDtypeStruct((1, 2 * D_MODEL), F32)],
        exchange=exchange)


def _adam_math(w, g, m, v):
    m = ADAM_B1 * m + (1.0 - ADAM_B1) * g
    v = ADAM_B2 * v + (1.0 - ADAM_B2) * (g * g)
    m_hat = m / (1.0 - ADAM_B1 ** ADAM_STEP)
    v_hat = v / (1.0 - ADAM_B2 ** ADAM_STEP)
    return -ADAM_LR * (m_hat / (jnp.sqrt(v_hat) + ADAM_EPS) + ADAM_WD * w), m, v


def _adam(g_slabs, w, m, v, name, g_fn=None, g_extra=()):
    R, C = w.shape
    tr = 256 if R % 256 == 0 else R
    ns = 0 if g_slabs is None else g_slabs.shape[0]
    slab_rows = tr if g_slabs is None or g_slabs.shape[1] == R else g_slabs.shape[1]
    assert slab_rows == tr or tr == R
    ne = len(g_extra)

    def body(*refs):
        e_refs = refs[:ne]
        refs = refs[ne:]
        if ns:
            gs_ref, refs = refs[0], refs[1:]
        w_ref, m_ref, v_ref, g_ref, d_ref, nm_ref, nv_ref = refs
        if g_fn is not None:
            g = g_fn(*e_refs)
        else:
            g = gs_ref[0].astype(F32)
            for s in range(1, ns):
                g = g + gs_ref[s].astype(F32)
            g = g[:tr]
        d, nm, nv = _adam_math(w_ref[...], g, m_ref[...], v_ref[...])
        g_ref[...] = g
        d_ref[...] = d
        nm_ref[...] = nm
        nv_ref[...] = nv

    blk = pl.BlockSpec((tr, C), lambda i: (i, 0))
    in_specs = [pl.BlockSpec((tr, e.shape[1]), lambda i: (i, 0)) if e.shape[0] == R else _full(e.shape) for e in g_extra]
    args = list(g_extra)
    if ns:
        in_specs.append(pl.BlockSpec((ns, slab_rows, C), lambda i: (0, i, 0)))
        args.append(g_slabs)
    return pl.pallas_call(
        body, name=name, grid=(R // tr,), in_specs=in_specs + [blk] * 3, out_specs=[blk] * 4,
        out_shape=[jax.ShapeDtypeStruct((R, C), F32)] * 4, compiler_params=_params(),
    )(*args, w, m, v)


def _adam_small(small_all, params):
    n = len(params)

    def body(*refs):
        s_ref, refs = refs[0], refs[1:]
        wmv, loss_ref, outs = refs[:3 * n], refs[3 * n], refs[3 * n + 1:]
        tot = s_ref[0]
        for i in range(1, N_DEV):
            tot = tot + s_ref[i]
        loss_ref[...] = tot[:, SMALL_W - 128:]
        for j, (w, _, _, off) in enumerate(params):
            w_ref, m_ref, v_ref = wmv[3 * j:3 * j + 3]
            g_ref, d_ref, nm_ref, nv_ref = outs[4 * j:4 * j + 4]
            if w.shape[0] == 2:
                lb = _lower_bound(w_ref)
                g0 = tot[:, off:off + w.shape[1]] * lb * (1.0 - lb)
                rows = [(slice(0, 1), g0), (slice(1, 2), -g0)]
            else:
                rows = [(slice(0, 1), tot[:, off:off + w.shape[1]])]
            for rs, g in rows:
                d, nm, nv = _adam_math(w_ref[rs, :], g, m_ref[rs, :], v_ref[rs, :])
                g_ref[rs, :], d_ref[rs, :], nm_ref[rs, :], nv_ref[rs, :] = g, d, nm, nv

    out_shape = [jax.ShapeDtypeStruct((1, 128), F32)]
    for w, _, _, _ in params:
        out_shape += [jax.ShapeDtypeStruct(w.shape, F32)] * 4
    res = pl.pallas_call(body, name="adam_small", out_shape=out_shape, compiler_params=_params())(
        small_all, *[a for w, m, v, _ in params for a in (w, m, v)])
    return res[0], [tuple(res[1 + 4 * j:5 + 4 * j]) for j in range(n)]


def _cols_from_slabs(g):
    s, r, c = g.shape
    return jnp.transpose(g, (1, 0, 2)).reshape(r, s * c)


def _slabs_from_cols(w):
    r, c = w.shape
    return jnp.transpose(w.reshape(r, N_DEV, c // N_DEV), (1, 0, 2))


def _rot_half_rows(wt):
    return jnp.concatenate([-wt[32:], wt[:32]], axis=0)


def _unrot_half_rows(dwt_rot):
    return jnp.concatenate([dwt_rot[32:], -dwt_rot[:32]], axis=0)


def _ext_in_t(g):
    n, rows, k_in = g.shape
    keep = n * rows - ROPE_DIM

    def body(g_ref, o_ref, stage_ref):
        stage_ref[keep:, :] = jnp.zeros((o_ref.shape[0] - keep, k_in), F32)
        for i in range(n - 1):
            stage_ref[rows * i:rows * (i + 1), :] = g_ref[i].astype(F32)
        last = g_ref[n - 1].astype(F32)
        stage_ref[rows * (n - 1):keep, :] = last[:rows - ROPE_DIM]
        wk = last[rows - ROPE_DIM:]
        stage_ref[keep + 128:keep + 192, :] = wk
        stage_ref[keep + 384:keep + 416, :] = -wk[32:]
        stage_ref[keep + 416:keep + 448, :] = wk[:32]
        o_ref[...] = stage_ref[...].astype(BF16)

    return pl.pallas_call(body, name="ext_w_in", out_shape=jax.ShapeDtypeStruct((keep + 512, k_in), BF16),
                          scratch_shapes=[pltpu.VMEM((keep + 512, k_in), F32)], compiler_params=_params())(g)


def _ext_q_t(wt):
    r = wt.shape[1]
    z64, z128 = jnp.zeros((64, r), BF16), jnp.zeros((128, r), BF16)
    per = HEAD_DIM + ROPE_DIM
    main = [jnp.concatenate([wt[per * h:per * (h + 1)], z64], axis=0) for h in range(HEADS)]
    rot = [jnp.concatenate([z128, _rot_half_rows(wt[per * h + HEAD_DIM:per * (h + 1)]), z64], axis=0)
           for h in range(HEADS)]
    return jnp.concatenate(main + rot, axis=0)


def _ext_kv(w_kv_up):
    r = w_kv_up.shape[0]
    z128 = jnp.zeros((r, 128), BF16)
    wkv = w_kv_up.reshape(r, HEADS, 2 * HEAD_DIM)
    kpad = [jnp.concatenate([wkv[:, h, :HEAD_DIM], z128], axis=1) for h in range(HEADS)]
    vals = [wkv[:, h, HEAD_DIM:] for h in range(HEADS)]
    return jnp.concatenate(kpad + vals, axis=1)


def _w_in_grad_slabs(dwt_h, dwt_m):
    d = dwt_h.shape[1]
    rows = (dwt_h.shape[0] + 512 + ROPE_DIM) // N_DEV
    padded = rows + (-rows) % 16

    def body(h_ref, m_ref, o_ref, stage_ref):
        stage_ref[0:2048, :] = h_ref[...]
        stage_ref[2048:2560, :] = m_ref[0:512, :]
        rot = m_ref[768 + 128:768 + 192, :]
        stage_ref[2560:2592, :] = m_ref[640:672, :] + rot[32:]
        stage_ref[2592:2624, :] = m_ref[672:704, :] - rot[:32]
        zero = jnp.zeros((padded - rows, d), F32)
        for i in range(N_DEV):
            o_ref[i] = jnp.concatenate([stage_ref[rows * i:rows * (i + 1), :], zero], axis=0).astype(BF16)

    return pl.pallas_call(body, name="w_in_grad_slabs", out_shape=jax.ShapeDtypeStruct((N_DEV, padded, d), BF16),
                          scratch_shapes=[pltpu.VMEM((N_DEV * rows, d), F32)], compiler_params=_params())(dwt_h, dwt_m)


def _grad_q_from_ext_t(dwq_ext_t):
    rows = []
    for h in range(HEADS):
        main, rot = dwq_ext_t[256 * h:256 * h + 256], dwq_ext_t[1024 + 256 * h:1280 + 256 * h]
        rows += [main[:128], main[128:192] + _unrot_half_rows(rot[128:192])]
    return jnp.concatenate(rows, axis=0)


def _grad_kv_from_ext(dwkv_ext):
    kvcols = []
    for h in range(HEADS):
        kvcols += [dwkv_ext[:, 256 * h:256 * h + 128], dwkv_ext[:, 1024 + 128 * h:1152 + 128 * h]]
    return jnp.concatenate(kvcols, axis=1)


SMALL_W = 6144 + 512 + 512 + 256 + 256 + 4 * 1024 + 128


def kernel(x, c, positions, w_ada, b_ada, w_in, hg_lower_bounds, hg_norm_w, mla_q_norm_w, w_q_up, mla_kv_norm_w, w_kv_up, w_out, ln1_g, ln1_b, w_mlp_in, w_mlp_out, ln2_g, ln2_b, loss_target, m_w_ada, m_b_ada, m_w_in, m_hg_lower_bounds, m_hg_norm_w, m_mla_q_norm_w, m_w_q_up, m_mla_kv_norm_w, m_w_kv_up, m_w_out, m_ln1_g, m_ln1_b, m_w_mlp_in, m_w_mlp_out, m_ln2_g, m_ln2_b, v_w_ada, v_b_ada, v_w_in, v_hg_lower_bounds, v_hg_norm_w, v_mla_q_norm_w, v_w_q_up, v_mla_kv_norm_w, v_w_kv_up, v_w_out, v_ln1_g, v_ln1_b, v_w_mlp_in, v_w_mlp_out, v_ln2_g, v_ln2_b):
    T = x.shape[1]
    me = 4 * lax.axis_index("x") + 2 * lax.axis_index("y") + lax.axis_index("c")
    xs, tgt = x[0], loss_target[0]
    transposed = ("w_in", "w_q_up")
    as_used = lambda n, a: a[0].T if n in transposed else a[0]
    big = {n: as_used(n, a) for n, a in dict(w_in=w_in, w_q_up=w_q_up, w_kv_up=w_kv_up, w_out=w_out,
                                              w_mlp_in=w_mlp_in, w_mlp_out=w_mlp_out).items()}
    names = list(big)

    bf = {n: big[n].astype(BF16) for n in ("w_in", "w_q_up", "w_kv_up", "w_out")}
    g_in, g_c = _gather_two_level([bf["w_in"], c], name="gather_w_in")
    c_all = g_c.reshape(N_DEV, D_MODEL)

    ada_cols = w_ada.shape[2]
    mod_part, cond = _mod_part(c_all, w_ada[0], lax.dynamic_slice(b_ada, (0, me * ada_cols), (1, ada_cols)))
    (mod_all,) = _exchange([mod_part], scatter=False, name="gather_mod")
    mod_row = lax.dynamic_slice(mod_all, (0, me, 0), (N_DEV, 1, ada_cols)).reshape(1, N_DEV * ada_cols)
    sh_a, sc_a, g_a, sh_m, sc_m, g_m = [mod_row[:, D_MODEL * i:D_MODEL * (i + 1)] for i in range(6)]

    w_in_ext = _ext_in_t(g_in)
    half = D_MODEL // 2
    z, (w1_top,) = _matmul(xs, w_in_ext, "NT", "in_proj", a_fn=_modulate, extras=(sc_a, sh_a), tn=3072,
                           exchange=_StagedGather(big["w_mlp_in"], rows=(0, half)))
    (o_raw, o_gated, s_prev), (g_q, g_kv, g_out) = _hgrn_fwd(
        z, hg_lower_bounds, hg_norm_w, exchange=_Exchange([bf["w_q_up"], bf["w_kv_up"], bf["w_out"]], False))
    wq_ext = _ext_q_t(g_q.reshape(N_DEV * g_q.shape[1], g_q.shape[2]))
    wkv_ext = _ext_kv(_cols_from_slabs(g_kv))
    w_out_full = g_out.reshape(D_MODEL, D_MODEL)
    inv_freq = 1.0 / (ROPE_THETA ** (jnp.arange(0, ROPE_DIM, 2, dtype=F32) / ROPE_DIM))
    zeros = lambda n: jnp.zeros((n,), F32)
    invf = jnp.concatenate([zeros(128), inv_freq, inv_freq, zeros(64)]).reshape(1, QK_PAD)
    m_rot = jnp.concatenate([zeros(128), jnp.ones((64,), F32), zeros(64)]).reshape(1, QK_PAD)
    q, k, v, c1, s1, cqn, ckvn = _mla_pre(z, positions.reshape(T, 1), invf, m_rot, wq_ext, wkv_ext,
                                          mla_q_norm_w, mla_kv_norm_w)[0]
    (o_mla, lse), (w1_bot, w2) = _attn_fwd(
        q, k, v, exchange=[_StagedGather(big["w_mlp_in"], 0.85, rows=(half, half)),
                           _StagedGather(big["w_mlp_out"], 0.85)])
    w1 = (w1_top, w1_bot)
    mix, xhat1, rstd1, u2 = _mix_ln1(o_gated, o_mla, w_out_full, xs, g_a, ln1_g, ln1_b, sc_m, sh_m)[0]
    r, dr2, dh, small_mlp_fwd = _mlp_fwd(u2, w1, w2, xhat1, ln1_g, ln1_b, g_m, ln2_g, ln2_b, tgt)

    dhpre, dr1, dmix, small_mlp_bwd = _mlp_bwd(dh, w1, w2, r, dr2, xhat1, rstd1, mix, ln1_g, ln1_b, sc_m, g_a)
    received = {}
    dw2 = _matmul(r, dh, "TN", "wgrad_mlp_out", out_dtype=BF16, a_fn=_square, tm=1024, tk=2048)
    dw1 = _matmul(u2, dhpre, "TN", "wgrad_mlp_in", out_dtype=BF16, tm=1024, tk=2048, out_slabs=N_DEV)
    dmixcat = _matmul(dmix, w_out_full, "NT", "dgrad_out", tm=1024)
    dw_out = jnp.concatenate([_matmul(o_gated, dmix, "TN", "wgrad_out_hg", out_dtype=BF16, tk=2048),
                              _matmul(o_mla, dmix, "TN", "wgrad_out_mla", out_dtype=BF16, tk=2048)], axis=0)
    (dz_h, small_hgrn), (received["w_out"],) = _hgrn_bwd(
        dmixcat, z, o_raw, s_prev, hg_lower_bounds, hg_norm_w,
        exchange=_Exchange([dw_out.reshape(N_DEV, D_MODEL // N_DEV, D_MODEL)], True))
    (dq, dk, dv), (received["w_mlp_in"], received["w_mlp_out"]) = _attn_bwd(
        q, k, v, dmixcat, o_mla, lse,
        exchange=_Exchange([dw1, dw2.reshape(N_DEV, dw2.shape[0] // N_DEV, D_MODEL)], True))
    dz_m, dq_ext, dkv_ext, small_mla = _mla_bwd(dq, dk, dv, z, c1, s1, wq_ext, wkv_ext, mla_q_norm_w, mla_kv_norm_w)
    dwq_t = _grad_q_from_ext_t(_matmul(dq_ext, cqn, "TN", "wgrad_q_up", tm=1024, tk=2048))
    dwkv = _grad_kv_from_ext(_matmul(ckvn, dkv_ext, "TN", "wgrad_kv_up", tn=1536, tk=2048))
    qkv_slabs = [dwq_t.reshape((N_DEV, dwq_t.shape[0] // N_DEV, dwq_t.shape[1])).astype(BF16),
                 _slabs_from_cols(dwkv).astype(BF16)]
    dwt_h, (received["w_q_up"], received["w_kv_up"]) = _matmul(
        dz_h, xs, "TN", "wgrad_in_h", b_fn=_modulate, extras=(sc_a, sh_a), tm=1024, tk=2048,
        exchange=_Exchange(qkv_slabs, True))
    dwt_m = _matmul(dz_m, xs, "TN", "wgrad_in_m", b_fn=_modulate, extras=(sc_a, sh_a), tm=1024, tk=2048)
    in_slabs = _w_in_grad_slabs(dwt_h, dwt_m)
    (grad_x, small_in), (received["w_in"],) = _input_bwd(
        dz_h, dz_m, w_in_ext, xs, dr1, sc_a, exchange=_StagedScatter(in_slabs))

    small = jnp.concatenate([small_in, small_mlp_bwd[:, :3 * D_MODEL], small_mlp_fwd[:, :D_MODEL], small_hgrn,
                             small_mla, small_mlp_bwd[:, 3 * D_MODEL:], small_mlp_fwd[:, D_MODEL:]], axis=1)
    assert small.shape == (1, SMALL_W)
    (small_all,) = _exchange([small], scatter=False, name="gather_small")

    moments = dict(w_in=(m_w_in, v_w_in), w_q_up=(m_w_q_up, v_w_q_up), w_kv_up=(m_w_kv_up, v_w_kv_up),
                   w_out=(m_w_out, v_w_out), w_mlp_in=(m_w_mlp_in, v_w_mlp_in), w_mlp_out=(m_w_mlp_out, v_w_mlp_out))
    res = {}
    for n in names:
        res[n] = _adam(received[n], big[n], as_used(n, moments[n][0]), as_used(n, moments[n][1]), name="adam_" + n)
    dmod_cols = lax.dynamic_slice(small_all.reshape(N_DEV, SMALL_W), (0, me * ada_cols), (N_DEV, ada_cols))
    cond_t = cond.T

    def ada_grad(ct_ref, dm_ref):
        g = ct_ref[:, 0:1] * dm_ref[0:1, :]
        for b in range(1, N_DEV):
            g = g + ct_ref[:, b:b + 1] * dm_ref[b:b + 1, :]
        return g

    res["w_ada"] = _adam(None, w_ada[0], m_w_ada[0], v_w_ada[0], name="adam_w_ada", g_fn=ada_grad,
                         g_extra=(cond_t, dmod_cols))

    small_params = [("b_ada", b_ada, m_b_ada, v_b_ada, 0),
                    ("hg_lower_bounds", hg_lower_bounds, m_hg_lower_bounds, v_hg_lower_bounds, 6144),
                    ("hg_norm_w", hg_norm_w, m_hg_norm_w, v_hg_norm_w, 6656),
                    ("mla_q_norm_w", mla_q_norm_w, m_mla_q_norm_w, v_mla_q_norm_w, 7168),
                    ("mla_kv_norm_w", mla_kv_norm_w, m_mla_kv_norm_w, v_mla_kv_norm_w, 7424),
                    ("ln1_g", ln1_g, m_ln1_g, v_ln1_g, 7680), ("ln1_b", ln1_b, m_ln1_b, v_ln1_b, 8704),
                    ("ln2_g", ln2_g, m_ln2_g, v_ln2_g, 9728), ("ln2_b", ln2_b, m_ln2_b, v_ln2_b, 10752)]
    loss_row, small_res = _adam_small(small_all, [p[1:] for p in small_params])
    for p, r4 in zip(small_params, small_res):
        res[p[0]] = r4
    loss = loss_row[0, 0]

    order = ["w_ada", "b_ada", "w_in", "hg_lower_bounds", "hg_norm_w", "mla_q_norm_w", "w_q_up", "mla_kv_norm_w",
             "w_kv_up", "w_out", "ln1_g", "ln1_b", "w_mlp_in", "w_mlp_out", "ln2_g", "ln2_b"]
    def as_given(n, a):
        if n in transposed:
            a = a.T
        return a[None] if n in big or n == "w_ada" else a

    shaped = {n: tuple(as_given(n, a) for a in res[n]) for n in order}
    outs = [loss, grad_x.reshape(1, T, D_MODEL)]
    for i in range(4):
        outs += [shaped[n][i] for n in order]
    return tuple(outs)
```

```python
import functools

import jax
import jax.numpy as jnp
import numpy as np
from jax import lax
from jax.experimental import pallas as pl
from jax.experimental.pallas import tpu as pltpu

F32, BF16 = jnp.float32, jnp.bfloat16
N_DEV = 8
D_MODEL = 1024
HEADS = 4
HEAD_DIM = 128
ROPE_DIM = 64
QK_PAD = 256
CHUNK = 64
ROPE_THETA = 10000.0
RMS_EPS = 1e-6
LN_EPS = 1e-5
ALPHA = 2.0 ** 0.25
ATT_SCALE = (HEAD_DIM + ROPE_DIM) ** -0.5
LN2 = float(np.log(2.0))
Q_PRESCALE = ATT_SCALE / LN2
ADAM_LR, ADAM_B1, ADAM_B2, ADAM_EPS, ADAM_WD, ADAM_STEP = 0.001, 0.9, 0.999, 1e-08, 0.01, 10
NEG_BIG = -1e30

ROW_TILE = 512
ATT_TILE = 512
HGRN_GROUP = 8
MLP_SLABS = 4
VMEM_LIMIT = 56 * 2 ** 20

NN = (((1,), (0,)), ((), ()))
NT = (((1,), (1,)), ((), ()))
TN = (((0,), (0,)), ((), ()))


def _dot(a, b, dims=NN):
    return lax.dot_general(a, b, dims, preferred_element_type=F32)


def _bdot(a, b, dims=NN):
    return lax.dot_general(a.astype(BF16), b.astype(BF16), dims, preferred_element_type=F32)


def _hdot(a, b, dims=NN):
    return lax.dot_general(a, b, dims, precision=lax.Precision.HIGHEST, preferred_element_type=F32)


def _params():
    return pltpu.CompilerParams(vmem_limit_bytes=VMEM_LIMIT)


def _sigmoid(x):
    return 1.0 / (1.0 + jnp.exp(-x))


def _rowsum(x):
    return jnp.sum(x, axis=0, keepdims=True)


def _lanemean(x):
    return jnp.mean(x, axis=-1, keepdims=True)


def _full(shape):
    nd = len(shape)
    return pl.BlockSpec(shape, lambda *_: (0,) * nd)


class _Exchange:
    def __init__(self, arrs, scatter):
        self.arrs, self.scatter, self.n, self.aliases, self.middle_at = list(arrs), scatter, len(arrs), [], 0.5
        self.out_shape = [jax.ShapeDtypeStruct((N_DEV,) + (a.shape[1:] if scatter else a.shape), a.dtype)
                          for a in self.arrs]
        n = self.n
        self.scratch = [pltpu.SemaphoreType.DMA((n, N_DEV - 1)), pltpu.SemaphoreType.DMA((n, N_DEV - 1)),
                        pltpu.SemaphoreType.DMA((n,))]

    def _copies(self, ins, outs, sems):
        send_sems, recv_sems, loc_sems = sems
        x, y, c = lax.axis_index("x"), lax.axis_index("y"), lax.axis_index("c")
        me = 4 * x + 2 * y + c
        copies = []
        for k in range(self.n):
            src_of = (lambda i, k=k: ins[k].at[i]) if self.scatter else (lambda i, k=k: ins[k])
            copies.append((pltpu.make_async_copy(src_of(me), outs[k].at[me], loc_sems.at[k]), None))
            for p in range(1, N_DEV):
                px = (1 - x) if p & 4 else x
                py = (1 - y) if p & 2 else y
                pc = (1 - c) if p & 1 else c
                peer = 4 * px + 2 * py + pc
                both = dict(send_sem=send_sems.at[k, p - 1], recv_sem=recv_sems.at[k, p - 1],
                            device_id=(px, py, pc), device_id_type=pl.DeviceIdType.MESH)
                send = pltpu.make_async_remote_copy(src_ref=src_of(peer), dst_ref=outs[k].at[me], **both)
                recv = pltpu.make_async_remote_copy(src_ref=src_of(peer), dst_ref=outs[k].at[peer], **both)
                copies.append((send, recv))
        return copies

    def start(self, ins, outs, sems):
        for first, _ in self._copies(ins, outs, sems):
            first.start()

    def middle(self, ins, outs, sems):
        pass

    def wait(self, ins, outs, sems):
        for first, recv in self._copies(ins, outs, sems):
            if recv is None:
                first.wait()
            else:
                recv.wait_recv()
                first.wait_send()


class _StagedGather:
    def __init__(self, arr, middle_at=0.8, rows=None):
        self.r0, n = rows if rows else (0, arr.shape[0])
        block = (n,) + arr.shape[1:]
        self.arrs, self.aliases, self.middle_at = [arr], [], middle_at
        self.out_shape = [jax.ShapeDtypeStruct((N_DEV,) + block, BF16)]
        self.scratch = [pltpu.VMEM((N_DEV,) + block, BF16), pltpu.SemaphoreType.DMA((7,)),
                        pltpu.SemaphoreType.DMA((7,)), pltpu.SemaphoreType.DMA((2,)), pltpu.VMEM(block, arr.dtype)]

    def _parts(self, scr):
        stage, send_sems, recv_sems, loc_sems = scr[:4]
        x, y, c = lax.axis_index("x"), lax.axis_index("y"), lax.axis_index("c")
        me, sibling = (x, y, c), (x, y, 1 - c)
        chips = [(1 - x, y), (x, 1 - y), (1 - x, 1 - y)]

        def copy(j, block, to):
            px, py, pc = block
            slot = stage.at[4 * px + 2 * py + pc]
            return pltpu.make_async_remote_copy(src_ref=slot, dst_ref=slot, send_sem=send_sems.at[j],
                                                recv_sem=recv_sems.at[j], device_id=to,
                                                device_id_type=pl.DeviceIdType.MESH)

        return stage, loc_sems, me, sibling, chips, c, copy

    def start(self, ins, outs, scr):
        stage, loc_sems, me, sibling, chips, c, copy = self._parts(scr)
        x, y, _ = me
        raw = scr[4]
        own = pltpu.make_async_copy(ins[0].at[pl.ds(self.r0, raw.shape[0])], raw, loc_sems.at[0])
        own.start()
        own.wait()
        stage[4 * x + 2 * y + c] = raw[...].astype(BF16)
        copy(0, me, sibling).start()
        for j, chip in enumerate(chips):
            copy(1 + j, me, (*chip, c)).start()

    def middle(self, ins, outs, scr):
        stage, loc_sems, me, sibling, chips, c, copy = self._parts(scr)
        for j, chip in enumerate(chips):
            copy(1 + j, (*chip, c), me).wait_recv()
            copy(4 + j, (*chip, c), sibling).start()

    def wait(self, ins, outs, scr):
        stage, loc_sems, me, sibling, chips, c, copy = self._parts(scr)
        copy(0, sibling, me).wait_recv()
        for j, chip in enumerate(chips):
            copy(4 + j, (*chip, 1 - c), me).wait_recv()
        copy(0, me, sibling).wait_send()
        for j, chip in enumerate(chips):
            copy(1 + j, me, (*chip, c)).wait_send()
            copy(4 + j, (*chip, c), sibling).wait_send()
        whole = pltpu.make_async_copy(stage, outs[0], loc_sems.at[1])
        whole.start()
        whole.wait()


class _StagedScatter:
    def __init__(self, slabs, middle_at=0.2):
        _, r, c = slabs.shape
        self.arrs, self.aliases, self.middle_at = [slabs], [], middle_at
        self.out_shape = [jax.ShapeDtypeStruct((4, r, c), slabs.dtype)]
        self.scratch = [pltpu.VMEM((N_DEV, r, c), slabs.dtype), pltpu.VMEM((4, r, c), slabs.dtype),
                        pltpu.VMEM((3, r, c), slabs.dtype), pltpu.SemaphoreType.DMA((4,)), pltpu.SemaphoreType.DMA((4,)),
                        pltpu.SemaphoreType.DMA((3,)), pltpu.SemaphoreType.DMA((3,)), pltpu.SemaphoreType.DMA((4,))]

    def _parts(self, scr):
        stage, from_sib, from_chips, sib_send, sib_recv, ici_send, ici_recv, loc_sems = scr
        x, y, c = lax.axis_index("x"), lax.axis_index("y"), lax.axis_index("c")
        chips = [(1 - x, y), (x, 1 - y), (1 - x, 1 - y)]

        def to_sibling(j):
            return pltpu.make_async_remote_copy(src_ref=stage.at[2 * j + 1 - c], dst_ref=from_sib.at[j],
                                                send_sem=sib_send.at[j], recv_sem=sib_recv.at[j],
                                                device_id=(x, y, 1 - c), device_id_type=pl.DeviceIdType.MESH)

        def to_chip(k):
            px, py = chips[k]
            return pltpu.make_async_remote_copy(src_ref=stage.at[4 * px + 2 * py + c], dst_ref=from_chips.at[k],
                                                send_sem=ici_send.at[k], recv_sem=ici_recv.at[k],
                                                device_id=(px, py, c), device_id_type=pl.DeviceIdType.MESH)

        return stage, from_sib, from_chips, loc_sems, (x, y, c), chips, to_sibling, to_chip

    def start(self, ins, outs, scr):
        stage, _, _, loc_sems, _, _, to_sibling, _ = self._parts(scr)
        load = pltpu.make_async_copy(ins[0], stage, loc_sems.at[0])
        load.start()
        load.wait()
        for j in range(4):
            to_sibling(j).start()

    def middle(self, ins, outs, scr):
        stage, from_sib, _, _, (x, y, c), _, to_sibling, to_chip = self._parts(scr)
        for j in range(4):
            to_sibling(j).wait_recv()
            mine = stage.at[2 * j + c]
            mine[...] = (mine[...].astype(F32) + from_sib[j].astype(F32)).astype(mine.dtype)
        for k in range(3):
            to_chip(k).start()

    def wait(self, ins, outs, scr):
        stage, _, from_chips, loc_sems, (x, y, c), chips, to_sibling, to_chip = self._parts(scr)
        writes = [pltpu.make_async_copy(stage.at[4 * x + 2 * y + c], outs[0].at[2 * x + y], loc_sems.at[0])]
        for k, (px, py) in enumerate(chips):
            to_chip(k).wait_recv()
            writes.append(pltpu.make_async_copy(from_chips.at[k], outs[0].at[2 * px + py], loc_sems.at[1 + k]))
        for w in writes:
            w.start()
        for j in range(4):
            to_sibling(j).wait_send()
        for k in range(3):
            to_chip(k).wait_send()
        for w in writes:
            w.wait()


def _call(body, name, args, out_shape, grid=(), in_specs=(), out_specs=(), scratch_shapes=(), exchange=None):
    if exchange is None:
        return pl.pallas_call(body, name=name, grid=grid, in_specs=list(in_specs), out_specs=list(out_specs),
                              out_shape=list(out_shape), scratch_shapes=list(scratch_shapes),
                              compiler_params=_params())(*args), None
    exs = list(exchange) if isinstance(exchange, (list, tuple)) else [exchange]
    ni, no, ns = len(args), len(out_shape), len(scratch_shapes)
    nxi, nxo = sum(len(e.arrs) for e in exs), sum(len(e.out_shape) for e in exs)
    steps = int(np.prod(grid))
    mid_step = lambda e: min(max(int(steps * e.middle_at), 1), steps - 1)
    aliases, iat, oat = {}, ni, no
    for e in exs:
        for src, dst in e.aliases:
            aliases[iat + src] = oat + dst
        iat, oat = iat + len(e.arrs), oat + len(e.out_shape)

    def wrapped(*refs):
        a, xi = refs[:ni], refs[ni:ni + nxi]
        o, xo = refs[ni + nxi:ni + nxi + no], refs[ni + nxi + no:ni + nxi + no + nxo]
        s, xs = refs[ni + nxi + no + nxo:ni + nxi + no + nxo + ns], refs[ni + nxi + no + nxo + ns:]
        parts, iat, oat, sat = [], 0, 0, 0
        for e in exs:
            parts.append((e, xi[iat:iat + len(e.arrs)], xo[oat:oat + len(e.out_shape)], xs[sat:sat + len(e.scratch)]))
            iat, oat, sat = iat + len(e.arrs), oat + len(e.out_shape), sat + len(e.scratch)
        step = 0
        for d, g in enumerate(grid):
            step = step * g + pl.program_id(d)

        @pl.when(step == 0)
        def _():
            for e, ins, outs, sems in parts:
                e.start(ins, outs, sems)

        for at_step in sorted({mid_step(e) for e in exs}):
            @pl.when(step == at_step)
            def _():
                for e, ins, outs, sems in parts:
                    if mid_step(e) == at_step:
                        e.middle(ins, outs, sems)

        body(*a, *o, *s)

        @pl.when(step == steps - 1)
        def _():
            for e, ins, outs, sems in parts:
                e.wait(ins, outs, sems)

    hbm = pl.BlockSpec(memory_space=pltpu.HBM)
    res = pl.pallas_call(
        wrapped, name=name, grid=grid, in_specs=list(in_specs) + [hbm] * nxi, out_specs=list(out_specs) + [hbm] * nxo,
        out_shape=list(out_shape) + [o_ for e in exs for o_ in e.out_shape],
        scratch_shapes=list(scratch_shapes) + [s_ for e in exs for s_ in e.scratch],
        input_output_aliases=aliases, compiler_params=_params())(*args, *[a_ for e in exs for a_ in e.arrs])
    return res[:no], res[no:]


def _gather_two_level(arrs, name):
    n = len(arrs)
    out_shape = [jax.ShapeDtypeStruct((N_DEV,) + a.shape, a.dtype) for a in arrs]

    def body(*refs):
        ins, outs = refs[:n], refs[n:2 * n]
        send_sems, recv_sems, loc_sems = refs[2 * n:]
        x, y, c = lax.axis_index("x"), lax.axis_index("y"), lax.axis_index("c")
        me, sibling = (x, y, c), (x, y, 1 - c)
        chips = [(1 - x, y), (x, 1 - y), (1 - x, 1 - y)]

        def copy(k, j, block, to, src=None):
            px, py, pc = block
            dst = outs[k].at[4 * px + 2 * py + pc]
            return pltpu.make_async_remote_copy(src_ref=dst if src is None else src, dst_ref=dst,
                                                send_sem=send_sems.at[k, j], recv_sem=recv_sems.at[k, j],
                                                device_id=to, device_id_type=pl.DeviceIdType.MESH)

        mine = [pltpu.make_async_copy(ins[k], outs[k].at[4 * x + 2 * y + c], loc_sems.at[k]) for k in range(n)]
        first = []
        for k in range(n):
            mine[k].start()
            first.append(copy(k, 0, me, sibling, src=ins[k]))
            first += [copy(k, 1 + j, me, (*chip, c), src=ins[k]) for j, chip in enumerate(chips)]
        for cp in first:
            cp.start()
        passed = []
        for j, chip in enumerate(chips):
            for k in range(n):
                copy(k, 1 + j, (*chip, c), me).wait_recv()
                passed.append(copy(k, 4 + j, (*chip, c), sibling))
                passed[-1].start()
        for k in range(n):
            copy(k, 0, sibling, me).wait_recv()
            for j, chip in enumerate(chips):
                copy(k, 4 + j, (*chip, 1 - c), me).wait_recv()
        for cp in first + passed:
            cp.wait_send()
        for cp in mine:
            cp.wait()

    vmem = pl.BlockSpec(memory_space=pltpu.VMEM)
    return pl.pallas_call(body, name=name, out_shape=out_shape, in_specs=[vmem] * n, out_specs=[vmem] * n,
                          scratch_shapes=[pltpu.SemaphoreType.DMA((n, 7)), pltpu.SemaphoreType.DMA((n, 7)),
                                          pltpu.SemaphoreType.DMA((n,))], compiler_params=_params())(*arrs)


def _exchange(arrs, scatter, name):
    ex = _Exchange(arrs, scatter)

    def body(*refs):
        ins, outs, sems = refs[:ex.n], refs[ex.n:2 * ex.n], refs[2 * ex.n:]
        ex.start(ins, outs, sems)
        ex.wait(ins, outs, sems)

    hbm = pl.BlockSpec(memory_space=pltpu.HBM)
    return pl.pallas_call(body, name=name, out_shape=ex.out_shape, in_specs=[hbm] * ex.n, out_specs=[hbm] * ex.n,
                          scratch_shapes=ex.scratch)(*ex.arrs)


def _matmul(a, b, mode, name, out_dtype=F32, tm=512, tn=1024, tk=1024, a_fn=None, b_fn=None, extras=(),
            out_slabs=None, exchange=None):
    assert not (a_fn and b_fn) and not (b_fn and mode == "NT")
    if mode == "NN":
        (M, K), N = a.shape, b.shape[1]
    elif mode == "NT":
        (M, K), N = a.shape, b.shape[0]
    else:
        (K, M), N = a.shape, b.shape[1]
    slab_w = N // out_slabs if out_slabs else None
    if out_slabs:
        tn = max(slab_w, min(tn, N) // slab_w * slab_w)
    tm, tn, tk = min(tm, M), min(tn, N), min(tk, K)
    assert M % tm == 0 and N % tn == 0 and K % tk == 0, (name, M, N, K)
    nk = K // tk
    dims = {"NN": NN, "NT": NT, "TN": TN}[mode]
    ne = len(extras)

    def body(a_ref, b_ref, *rest):
        e_refs, o_ref, acc_ref = rest[:ne], rest[ne], rest[ne + 1]
        k = pl.program_id(2)

        @pl.when(k == 0)
        def _():
            acc_ref[...] = jnp.zeros_like(acc_ref)

        at, bt = a_ref[...], b_ref[...]
        if a_fn is not None:
            at = a_fn(at.astype(F32), *[e[...] for e in e_refs])
        if b_fn is not None:
            bt = b_fn(bt.astype(F32), *[e[...] for e in e_refs])
        acc_ref[...] += _bdot(at, bt, dims)

        @pl.when(k == nk - 1)
        def _():
            if out_slabs:
                for s in range(tn // slab_w):
                    o_ref[s] = acc_ref[:, s * slab_w:(s + 1) * slab_w].astype(out_dtype)
            else:
                o_ref[...] = acc_ref[...].astype(out_dtype)

    if mode == "TN":
        a_spec = pl.BlockSpec((tk, tm), lambda i, j, k: (k, i))
        e_spec = pl.BlockSpec((1, tm), lambda i, j, k: (0, i))
    else:
        a_spec = pl.BlockSpec((tm, tk), lambda i, j, k: (i, k))
        e_spec = pl.BlockSpec((1, tk), lambda i, j, k: (0, k))
    if mode == "NT":
        b_spec = pl.BlockSpec((tn, tk), lambda i, j, k: (j, k))
    else:
        b_spec = pl.BlockSpec((tk, tn), lambda i, j, k: (k, j))
    if b_fn is not None:
        e_spec = pl.BlockSpec((1, tn), lambda i, j, k: (0, j))
    if out_slabs:
        o_shape = jax.ShapeDtypeStruct((out_slabs, M, slab_w), out_dtype)
        o_spec = pl.BlockSpec((tn // slab_w, tm, slab_w), lambda i, j, k: (j, i, 0))
    else:
        o_shape = jax.ShapeDtypeStruct((M, N), out_dtype)
        o_spec = pl.BlockSpec((tm, tn), lambda i, j, k: (i, j))
    (out,), got = _call(body, name, (a, b, *extras), [o_shape], grid=(M // tm, N // tn, nk),
                        in_specs=[a_spec, b_spec] + [e_spec] * ne, out_specs=[o_spec],
                        scratch_shapes=[pltpu.VMEM((tm, tn), F32)], exchange=exchange)
    return out if exchange is None else (out, got)


def _modulate(x, sc, sh):
    return x * (1.0 + sc) + sh


def _square(x):
    return x * x


def _mod_part(c_all, w_ada_s, b_s):
    def body(c_ref, w_ref, b_ref, mod_ref, cond_ref):
        cv = c_ref[...]
        cond = cv * _sigmoid(cv)
        cond_ref[...] = cond
        mod_ref[...] = _bdot(cond, w_ref[...]) + b_ref[...]

    return pl.pallas_call(
        body, name="mod_part",
        out_shape=[jax.ShapeDtypeStruct((N_DEV, w_ada_s.shape[1]), F32), jax.ShapeDtypeStruct(c_all.shape, F32)],
        compiler_params=_params(),
    )(c_all, w_ada_s, b_s)


def _rms_fwd(x, w):
    rs = lax.rsqrt(_lanemean(x * x) + RMS_EPS)
    return x * rs * w, rs


def _rms_bwd(x, rs, w, dy):
    xhat = x * rs
    dxh = dy * w
    return rs * (dxh - xhat * _lanemean(dxh * xhat)), dy * xhat


def _mla_pre(z, pos_col, invf, m_rot, wq_ext, wkv_ext, qnw, kvnw, exchange=None):
    T = z.shape[0]
    tm = min(ROW_TILE, T)

    def body(z_ref, pos_ref, invf_ref, mrot_ref, wq_ref, wkv_ref, qnw_ref, kvnw_ref,
             q_ref, k_ref, v_ref, c1_ref, s1_ref, cqn_ref, ckvn_ref):
        hi = slice(HEAD_DIM, QK_PAD)
        ang = pos_ref[...].astype(F32) * invf_ref[:, hi]
        c1 = jnp.concatenate([jnp.ones((tm, HEAD_DIM), F32), mrot_ref[:, hi] * jnp.cos(ang)], axis=1)
        s1 = jnp.concatenate([jnp.zeros((tm, HEAD_DIM), F32), mrot_ref[:, hi] * jnp.sin(ang)], axis=1)
        c1_ref[...] = c1
        s1_ref[...] = s1
        cqn, _ = _rms_fwd(z_ref[:, 0:256], qnw_ref[...])
        ckvn, _ = _rms_fwd(z_ref[:, 256:512], kvnw_ref[...])
        cqn_ref[...] = cqn.astype(BF16)
        ckvn_ref[...] = ckvn.astype(BF16)
        qe = _bdot(cqn, wq_ref[...], NT)
        kve = _bdot(ckvn, wkv_ref[...])
        k_rope = z_ref[:, 512:768] * c1 + z_ref[:, 768:1024] * s1
        for h in range(HEADS):
            q_ref[h] = ((qe[:, 256 * h:256 * h + 256] * c1 + qe[:, 1024 + 256 * h:1280 + 256 * h] * s1)
                        * Q_PRESCALE).astype(BF16)
            k_ref[h] = (kve[:, 256 * h:256 * h + 256] + k_rope).astype(BF16)
            v_ref[h] = kve[:, 1024 + 128 * h:1152 + 128 * h].astype(BF16)

    row = lambda i: (i, 0)
    head = lambda i: (0, i, 0)
    return _call(
        body, "mla_pre", (z, pos_col, invf, m_rot, wq_ext, wkv_ext, qnw, kvnw), grid=(T // tm,),
        in_specs=[pl.BlockSpec((tm, 1024), lambda i: (i, 2)), pl.BlockSpec((tm, 1), row),
                  _full((1, 256)), _full((1, 256)), _full(wq_ext.shape), _full(wkv_ext.shape),
                  _full((1, 256)), _full((1, 256))],
        out_specs=[pl.BlockSpec((HEADS, tm, QK_PAD), head), pl.BlockSpec((HEADS, tm, QK_PAD), head),
                   pl.BlockSpec((HEADS, tm, HEAD_DIM), head), pl.BlockSpec((tm, 256), row), pl.BlockSpec((tm, 256), row),
                   pl.BlockSpec((tm, 256), row), pl.BlockSpec((tm, 256), row)],
        out_shape=[jax.ShapeDtypeStruct((HEADS, T, QK_PAD), BF16), jax.ShapeDtypeStruct((HEADS, T, QK_PAD), BF16),
                   jax.ShapeDtypeStruct((HEADS, T, HEAD_DIM), BF16), jax.ShapeDtypeStruct((T, 256), F32),
                   jax.ShapeDtypeStruct((T, 256), F32), jax.ShapeDtypeStruct((T, 256), BF16),
                   jax.ShapeDtypeStruct((T, 256), BF16)], exchange=exchange)


def _mla_bwd(dq, dk, dv, z, c1, s1, wq_ext, wkv_ext, qnw, kvnw):
    T = z.shape[0]
    tm = min(ROW_TILE, T)

    def body(dq_ref, dk_ref, dv_ref, z_ref, c1_ref, s1_ref, wq_ref, wkv_ref, qnw_ref, kvnw_ref,
             dz_ref, dqe_ref, dkve_ref, dnw_ref):
        @pl.when(pl.program_id(0) == 0)
        def _():
            dnw_ref[...] = jnp.zeros_like(dnw_ref)

        c1, s1 = c1_ref[...], s1_ref[...]
        dkpe = jnp.zeros((tm, QK_PAD), F32)
        for h in range(HEADS):
            dqh, dkh = dq_ref[h].astype(F32) * ATT_SCALE, dk_ref[h]
            dqe_ref[:, 256 * h:256 * h + 256] = (dqh * c1).astype(BF16)
            dqe_ref[:, 1024 + 256 * h:1280 + 256 * h] = (dqh * s1).astype(BF16)
            dkve_ref[:, 256 * h:256 * h + 256] = dkh
            dkve_ref[:, 1024 + 128 * h:1152 + 128 * h] = dv_ref[h]
            dkpe = dkpe + dkh.astype(F32)
        dcqn = _dot(dqe_ref[...], wq_ref[...])
        dckvn = _dot(dkve_ref[...], wkv_ref[...], NT)
        cq, ckv = z_ref[:, 0:256], z_ref[:, 256:512]
        _, rsq = _rms_fwd(cq, qnw_ref[...])
        _, rskv = _rms_fwd(ckv, kvnw_ref[...])
        dcq, wq_rows = _rms_bwd(cq, rsq, qnw_ref[...], dcqn)
        dckv, wkv_rows = _rms_bwd(ckv, rskv, kvnw_ref[...], dckvn)
        dnw_ref[:, 0:256] += _rowsum(wq_rows)
        dnw_ref[:, 256:512] += _rowsum(wkv_rows)
        dz_ref[:, 0:256] = dcq.astype(BF16)
        dz_ref[:, 256:512] = dckv.astype(BF16)
        dz_ref[:, 512:768] = (dkpe * c1).astype(BF16)
        dz_ref[:, 768:1024] = (dkpe * s1).astype(BF16)

    row = lambda i: (i, 0)
    head = lambda i: (0, i, 0)
    return pl.pallas_call(
        body, name="mla_bwd", grid=(T // tm,),
        in_specs=[pl.BlockSpec((HEADS, tm, QK_PAD), head), pl.BlockSpec((HEADS, tm, QK_PAD), head),
                  pl.BlockSpec((HEADS, tm, HEAD_DIM), head), pl.BlockSpec((tm, 1024), lambda i: (i, 2)),
                  pl.BlockSpec((tm, 256), row), pl.BlockSpec((tm, 256), row), _full(wq_ext.shape), _full(wkv_ext.shape),
                  _full((1, 256)), _full((1, 256))],
        out_specs=[pl.BlockSpec((tm, 1024), row), pl.BlockSpec((tm, 2048), row), pl.BlockSpec((tm, 1536), row),
                   _full((1, 512))],
        out_shape=[jax.ShapeDtypeStruct((T, 1024), BF16), jax.ShapeDtypeStruct((T, 2048), BF16),
                   jax.ShapeDtypeStruct((T, 1536), BF16), jax.ShapeDtypeStruct((1, 512), F32)],
        compiler_params=_params(),
    )(dq, dk, dv, z, c1, s1, wq_ext, wkv_ext, qnw, kvnw)


_HEAD_LANES = [slice(HEAD_DIM * h, HEAD_DIM * (h + 1)) for h in range(HEADS)]


def _lower_bound(lbraw_ref):
    a0, a1 = lbraw_ref[0:1, :], lbraw_ref[1:2, :]
    mx = jnp.maximum(a0, a1)
    e0, e1 = jnp.exp(a0 - mx), jnp.exp(a1 - mx)
    return e0 / (e0 + e1)


def _tri(lower):
    r = lax.broadcasted_iota(jnp.int32, (CHUNK, CHUNK), 0)
    c = lax.broadcasted_iota(jnp.int32, (CHUNK, CHUNK), 1)
    return (r >= c) if lower else (r <= c)


def _hgrn_gates(q, f, lb, tri_lo):
    sg = _sigmoid(f)
    forget = lb + (1.0 - lb) * sg
    k = 1.0 - forget
    b = _hdot(tri_lo.astype(F32), jnp.log(forget))
    b_ref, b_last = b[CHUNK // 2 - 1:CHUNK // 2, :], b[CHUNK - 1:CHUNK, :]
    e1, e2, e3, e4 = jnp.exp(b - b_ref), jnp.exp(b_ref - b), jnp.exp(b_last - b), jnp.exp(b)
    return dict(sg=sg, forget=forget, k=k, e1=e1, e2=e2, e3=e3, e4=e4, qa=q * e1, ka=k * e2, kl=k * e3, qb=q * e4,
                decay=jnp.exp(b_last))


def _hgrn_fwd(z, lbraw, nw, exchange=None):
    T = z.shape[0]
    G = min(HGRN_GROUP, T // CHUNK)
    rows = G * CHUNK
    n_chunks = T // CHUNK

    def body(q_ref, f_ref, i_ref, g_ref, lbraw_ref, nw_ref, oraw_ref, og_ref, sp_ref, st_ref):
        @pl.when(pl.program_id(0) == 0)
        def _():
            st_ref[...] = jnp.zeros_like(st_ref)

        lb_all = _lower_bound(lbraw_ref)
        tri_lo = _tri(True)

        def chunk(cc, carry):
            rs = pl.ds(pl.multiple_of(cc * CHUNK, CHUNK), CHUNK)
            t = _hgrn_gates(q_ref[rs, :], f_ref[rs, :], lb_all, tri_lo)
            v, gate = i_ref[rs, :], g_ref[rs, :]
            st = [st_ref[h] for h in range(HEADS)]
            a = [jnp.where(tri_lo, _bdot(t["qa"][:, s], t["ka"][:, s], NT), 0.0) for s in _HEAD_LANES]
            kv = [_bdot(v[:, s], t["kl"][:, s], TN) for s in _HEAD_LANES]
            o = [_bdot(a[h], v[:, s]) + _bdot(t["qb"][:, s], st[h], NT) for h, s in enumerate(_HEAD_LANES)]
            for h, s in enumerate(_HEAD_LANES):
                sp_ref[cc, h] = st[h]
                st_ref[h] = st[h] * t["decay"][:, s] + kv[h]
            oraw_ref[rs, :] = jnp.concatenate(o, axis=1)
            on = jnp.concatenate([_rms_fwd(o[h], nw_ref[:, s])[0] for h, s in enumerate(_HEAD_LANES)], axis=1)
            og_ref[rs, :] = (on * (gate * _sigmoid(gate))).astype(BF16)
            return carry

        lax.fori_loop(0, G, chunk, 0, unroll=4)

    col = lambda j: pl.BlockSpec((rows, 512), lambda r, j=j: (r, j))
    return _call(
        body, "hgrn_fwd", (z, z, z, z, lbraw, nw), grid=(T // rows,),
        in_specs=[col(0), col(1), col(2), col(3), _full((2, 512)), _full((1, 512))],
        out_specs=[col(0), col(0), pl.BlockSpec((G, HEADS, HEAD_DIM, HEAD_DIM), lambda r: (r, 0, 0, 0))],
        out_shape=[jax.ShapeDtypeStruct((T, 512), F32), jax.ShapeDtypeStruct((T, 512), BF16),
                   jax.ShapeDtypeStruct((n_chunks, HEADS, HEAD_DIM, HEAD_DIM), F32)],
        scratch_shapes=[pltpu.VMEM((HEADS, HEAD_DIM, HEAD_DIM), F32)], exchange=exchange)


def _hgrn_bwd(dmixcat, z, oraw, sprev, lbraw, nw, exchange=None):
    T = z.shape[0]
    G = min(HGRN_GROUP, T // CHUNK)
    rows = G * CHUNK
    ng = T // rows

    def body(dog_ref, q_ref, f_ref, i_ref, g_ref, oraw_ref, sp_ref, lbraw_ref, nw_ref,
             dz_ref, dsmall_ref, dst_ref):
        @pl.when(pl.program_id(0) == 0)
        def _():
            dst_ref[...] = jnp.zeros_like(dst_ref)
            dsmall_ref[...] = jnp.zeros_like(dsmall_ref)

        lb_all = _lower_bound(lbraw_ref)
        tri_lo, tri_up = _tri(True), _tri(False)
        rowid = lax.broadcasted_iota(jnp.int32, (CHUNK, HEADS * HEAD_DIM), 0)

        def chunk(it, carry):
            cc = G - 1 - it
            rs = pl.ds(pl.multiple_of(cc * CHUNK, CHUNK), CHUNK)
            heads = list(enumerate(_HEAD_LANES))
            cat = lambda parts: jnp.concatenate(parts, axis=1)
            per_head_mean = lambda x: cat([jnp.broadcast_to(_lanemean(x[:, s]), (CHUNK, HEAD_DIM)) for s in _HEAD_LANES])
            t = _hgrn_gates(q_ref[rs, :], f_ref[rs, :], lb_all, tri_lo)
            v, gate, o, dog, nw_all = i_ref[rs, :], g_ref[rs, :], oraw_ref[rs, :], dog_ref[rs, :], nw_ref[...]
            rs_o = lax.rsqrt(per_head_mean(o * o) + RMS_EPS)
            xhat = o * rs_o
            sgg = _sigmoid(gate)
            d_on = dog * (gate * sgg)
            dz_ref[rs, 1536:2048] = (dog * (xhat * nw_all) * (sgg * (1.0 + gate * (1.0 - sgg)))).astype(BF16)
            dxh = d_on * nw_all
            do = rs_o * (dxh - xhat * per_head_mean(dxh * xhat))
            dsmall_ref[:, 512:1024] += _rowsum(d_on * xhat)
            st = [sp_ref[cc, h] for h in range(HEADS)]
            dst = [dst_ref[h] for h in range(HEADS)]
            a = [jnp.where(tri_lo, _bdot(t["qa"][:, s], t["ka"][:, s], NT), 0.0) for s in _HEAD_LANES]
            da = [jnp.where(tri_lo, _bdot(do[:, s], v[:, s], NT), 0.0) for s in _HEAD_LANES]
            dqb = cat([_bdot(do[:, s], st[h]) for h, s in heads])
            dkl = cat([_bdot(v[:, s], dst[h]) for h, s in heads])
            dv_ = cat([_bdot(t["kl"][:, s], dst[h], NT) + _bdot(a[h], do[:, s], TN) for h, s in heads])
            dqa = cat([_bdot(da[h], t["ka"][:, s]) for h, s in heads])
            dka = cat([_bdot(da[h], t["qa"][:, s], TN) for h, s in heads])
            ddecay = cat([_rowsum(dst[h] * st[h]) for h in range(HEADS)])
            for h, s in heads:
                dst_ref[h] = dst[h] * t["decay"][:, s] + _bdot(do[:, s], t["qb"][:, s], TN)
            pa, pk, pb, pl_ = dqa * t["qa"], dka * t["ka"], dqb * t["qb"], dkl * t["kl"]
            db = pa - pk + pb - pl_
            db = db + jnp.where(rowid == CHUNK // 2 - 1, _rowsum(pk - pa), 0.0)
            db = db + jnp.where(rowid == CHUNK - 1, _rowsum(pl_) + ddecay * t["decay"], 0.0)
            dlogf = _hdot(tri_up.astype(F32), db)
            dforget = dlogf / t["forget"] - (dka * t["e2"] + dkl * t["e3"])
            sg = t["sg"]
            dz_ref[rs, 0:512] = (dqa * t["e1"] + dqb * t["e4"]).astype(BF16)
            dz_ref[rs, 512:1024] = (dforget * (1.0 - lb_all) * sg * (1.0 - sg)).astype(BF16)
            dz_ref[rs, 1024:1536] = dv_.astype(BF16)
            dsmall_ref[:, 0:512] += _rowsum(dforget * (1.0 - sg))
            return carry

        lax.fori_loop(0, G, chunk, 0, unroll=4)

    col = lambda j: pl.BlockSpec((rows, 512), lambda r, j=j: (ng - 1 - r, j))
    return _call(
        body, "hgrn_bwd", (dmixcat, z, z, z, z, oraw, sprev, lbraw, nw), grid=(ng,),
        in_specs=[col(0), col(0), col(1), col(2), col(3), col(0),
                  pl.BlockSpec((G, HEADS, HEAD_DIM, HEAD_DIM), lambda r: (ng - 1 - r, 0, 0, 0)),
                  _full((2, 512)), _full((1, 512))],
        out_specs=[pl.BlockSpec((rows, 2048), lambda r: (ng - 1 - r, 0)), _full((1, 1024))],
        out_shape=[jax.ShapeDtypeStruct((T, 2048), BF16), jax.ShapeDtypeStruct((1, 1024), F32)],
        scratch_shapes=[pltpu.VMEM((HEADS, HEAD_DIM, HEAD_DIM), F32)], exchange=exchange)


def _diag_mask(t):
    r = lax.broadcasted_iota(jnp.int32, (t, t), 0)
    c = lax.broadcasted_iota(jnp.int32, (t, t), 1)
    return r >= c


def _attn_fwd(q, k, v, exchange=None):
    _, T, _ = q.shape
    t = min(ATT_TILE, T)

    def body(q_ref, k_ref, v_ref, o_ref, lse_ref):
        i = pl.program_id(1)
        qb = q_ref[...]

        rows = lambda j: pl.ds(pl.multiple_of(j * t, t), t)

        def logits(j, masked):
            s = _dot(qb, k_ref[rows(j), :], NT)
            return jnp.where(_diag_mask(t), s, NEG_BIG) if masked else s

        def absorb(s, j, carry):
            m, l, acc = carry
            mn = jnp.maximum(m, jnp.max(s, axis=-1, keepdims=True))
            p = jnp.exp2(s - mn)
            al = jnp.exp2(m - mn)
            return mn, al * l + jnp.sum(p, axis=-1, keepdims=True), al * acc + _dot(p.astype(BF16), v_ref[rows(j), :])

        def pair(j0, carry, last_masked):
            s0, s1 = logits(j0, False), logits(j0 + 1, last_masked)
            return absorb(s1, j0 + 1, absorb(s0, j0, carry))

        init = (jnp.full((t, 1), NEG_BIG, F32), jnp.zeros((t, 1), F32), jnp.zeros((t, HEAD_DIM), F32))
        carry = lax.fori_loop(0, i // 2, lambda jj, c: pair(2 * jj, c, False), init)
        m, l, acc = lax.cond(i % 2 == 1, lambda c: pair(i - 1, c, True),
                             lambda c: absorb(logits(i, True), i, c), carry)
        o_ref[...] = acc / l
        lse_ref[...] = jnp.broadcast_to(m + jnp.log2(l), (t, HEAD_DIM))

    return _call(
        body, "attn_fwd", (q, k, v), grid=(HEADS, T // t),
        in_specs=[pl.BlockSpec((None, t, QK_PAD), lambda h, i: (h, i, 0)),
                  pl.BlockSpec((None, T, QK_PAD), lambda h, i: (h, 0, 0)),
                  pl.BlockSpec((None, T, HEAD_DIM), lambda h, i: (h, 0, 0))],
        out_specs=[pl.BlockSpec((t, HEAD_DIM), lambda h, i: (i, h)),
                   pl.BlockSpec((None, t, HEAD_DIM), lambda h, i: (h, i, 0))],
        out_shape=[jax.ShapeDtypeStruct((T, HEADS * HEAD_DIM), F32), jax.ShapeDtypeStruct((HEADS, T, HEAD_DIM), F32)],
        exchange=exchange)


def _attn_bwd(q, k, v, dmixcat, o, lse, exchange=None):
    _, T, _ = q.shape
    t = min(ATT_TILE, T)
    nq = T // t

    def body(q_ref, k_ref, v_ref, do_ref, o_ref, lse_ref, dq_ref, dk_ref, dv_ref, delta_ref, dq_acc):
        j = pl.program_id(1)

        @pl.when(j == 0)
        def _():
            dq_acc[...] = jnp.zeros_like(dq_acc)

            def fill(i, carry):
                rs = pl.ds(pl.multiple_of(i * t, t), t)
                delta_ref[rs, :] = jnp.broadcast_to(
                    jnp.sum(do_ref[rs, :] * o_ref[rs, :], axis=-1, keepdims=True), (t, HEAD_DIM))
                return carry

            lax.fori_loop(0, nq, fill, 0)

        kb, vb = k_ref[...], v_ref[...]

        def steps(blocks, carry):
            dk, dv = carry
            rs = [pl.ds(pl.multiple_of(i * t, t), t) for i, _ in blocks]
            qb = [q_ref[r, :] for r in rs]
            dob = [do_ref[r, :].astype(BF16) for r in rs]
            s = [_dot(b, kb, NT) for b in qb]
            dp = [_dot(b, vb, NT) for b in dob]
            for n, (_, shift) in enumerate(blocks):
                p = jnp.exp2(s[n] - lse_ref[rs[n], 0:1])
                if shift is not None:
                    row = lax.broadcasted_iota(jnp.int32, (t, 2 * t), 0)
                    col = lax.broadcasted_iota(jnp.int32, (t, 2 * t), 1)
                    p = jnp.where(col <= row + shift, p, 0.0)
                ds = (p * (dp[n] - delta_ref[rs[n], 0:1])).astype(BF16)
                dq_acc[rs[n], :] += _dot(ds, kb)
                dk = dk + _dot(ds, qb[n], TN)
                dv = dv + _dot(p.astype(BF16), dob[n], TN)
            return dk, dv

        zero = (jnp.zeros((2 * t, QK_PAD), F32), jnp.zeros((2 * t, HEAD_DIM), F32))
        carry = steps([(2 * j, 0), (2 * j + 1, t)], zero)
        first = 2 * j + 2
        dk, dv = lax.fori_loop(0, (nq - first) // 2, lambda n, c: steps([(first + 2 * n, None), (first + 2 * n + 1, None)], c),
                               carry)
        dk_ref[...] = (dk * LN2).astype(BF16)
        dv_ref[...] = dv.astype(BF16)

        @pl.when(j == nk - 1)
        def _():
            dq_ref[...] = dq_acc[...].astype(BF16)

    nk = nq // 2
    return _call(
        body, "attn_bwd", (q, k, v, dmixcat, o, lse), grid=(HEADS, nk),
        in_specs=[pl.BlockSpec((None, T, QK_PAD), lambda h, j: (h, 0, 0)),
                  pl.BlockSpec((None, 2 * t, QK_PAD), lambda h, j: (h, j, 0)),
                  pl.BlockSpec((None, 2 * t, HEAD_DIM), lambda h, j: (h, j, 0)),
                  pl.BlockSpec((T, HEAD_DIM), lambda h, j: (0, HEADS + h)),
                  pl.BlockSpec((T, HEAD_DIM), lambda h, j: (0, h)),
                  pl.BlockSpec((None, T, HEAD_DIM), lambda h, j: (h, 0, 0))],
        out_specs=[pl.BlockSpec((None, T, QK_PAD), lambda h, j: (h, 0, 0)),
                   pl.BlockSpec((None, 2 * t, QK_PAD), lambda h, j: (h, j, 0)),
                   pl.BlockSpec((None, 2 * t, HEAD_DIM), lambda h, j: (h, j, 0))],
        out_shape=[jax.ShapeDtypeStruct((HEADS, T, QK_PAD), BF16), jax.ShapeDtypeStruct((HEADS, T, QK_PAD), BF16),
                   jax.ShapeDtypeStruct((HEADS, T, HEAD_DIM), BF16)],
        scratch_shapes=[pltpu.VMEM((T, HEAD_DIM), F32), pltpu.VMEM((T, QK_PAD), F32)], exchange=exchange)


def _ln_fwd(r):
    mu = _lanemean(r)
    xc = r - mu
    rstd = lax.rsqrt(_lanemean(xc * xc) + LN_EPS)
    return xc * rstd, rstd


def _ln_bwd(dxh, xhat, rstd):
    return rstd * (dxh - _lanemean(dxh) - xhat * _lanemean(dxh * xhat))


def _mix_ln1(o_hg, o_mla, w_out, x, g_a, ln1_g, ln1_b, sc_m, sh_m, exchange=None):
    T = x.shape[0]
    tm = min(ROW_TILE, T)
    half = o_hg.shape[1]

    def body(hg_ref, mla_ref, w_ref, x_ref, ga_ref, g_ref, b_ref, sc_ref, sh_ref, mix_ref, xhat_ref, rstd_ref, u2_ref):
        mix = _dot(hg_ref[...], w_ref[0:half, :]) + _bdot(mla_ref[...], w_ref[half:, :])
        mix_ref[...] = mix
        xhat, rstd = _ln_fwd(ALPHA * x_ref[...] + (1.0 + ga_ref[...]) * mix)
        xhat_ref[...] = xhat
        rstd_ref[...] = jnp.broadcast_to(rstd, (tm, 128))
        u2_ref[...] = _modulate(xhat * g_ref[...] + b_ref[...], sc_ref[...], sh_ref[...]).astype(BF16)

    row = pl.BlockSpec((tm, D_MODEL), lambda i: (i, 0))
    vec = _full((1, D_MODEL))
    halfrow = pl.BlockSpec((tm, half), lambda i: (i, 0))
    return _call(
        body, "mix_ln1", (o_hg, o_mla, w_out, x, g_a, ln1_g, ln1_b, sc_m, sh_m), grid=(T // tm,),
        in_specs=[halfrow, halfrow, _full(w_out.shape), row, vec, vec, vec, vec, vec],
        out_specs=[row, row, pl.BlockSpec((tm, 128), lambda i: (i, 0)), row],
        out_shape=[jax.ShapeDtypeStruct((T, D_MODEL), F32), jax.ShapeDtypeStruct((T, D_MODEL), F32),
                   jax.ShapeDtypeStruct((T, 128), F32), jax.ShapeDtypeStruct((T, D_MODEL), BF16)],
        exchange=exchange)


def _mlp_fwd(u2, w1, w2, xhat1, ln1_g, ln1_b, g_m, ln2_g, ln2_b, target):
    T = u2.shape[0]
    half, tf = w1[0].shape[1:]
    nf = N_DEV // MLP_SLABS
    tm = min(ROW_TILE, T)

    def body(u2_ref, w1a_ref, w1b_ref, w2_ref, xhat_ref, g1_ref, b1_ref, gm_ref, g2_ref, b2_ref, tgt_ref,
             r_ref, dr2_ref, dh_ref, small_ref, acc_ref):
        i, f = pl.program_id(0), pl.program_id(1)
        dm = D_MODEL

        @pl.when((i == 0) & (f == 0))
        def _():
            small_ref[...] = jnp.zeros_like(small_ref)

        @pl.when(f == 0)
        def _():
            acc_ref[...] = jnp.zeros_like(acc_ref)

        u2t = u2_ref[...]
        part = None
        for s in range(MLP_SLABS):
            r = jnp.maximum(_dot(u2t[:, :half], w1a_ref[s]) + _dot(u2t[:, half:], w1b_ref[s]), 0.0)
            r_ref[:, s * tf:(s + 1) * tf] = r.astype(BF16)
            d = _bdot(r * r, w2_ref[s])
            part = d if part is None else part + d
        acc_ref[...] += part

        @pl.when(f == nf - 1)
        def _():
            h = acc_ref[...]
            x1 = xhat_ref[...] * g1_ref[...] + b1_ref[...]
            xhat2, rstd2 = _ln_fwd(ALPHA * x1 + (1.0 + gm_ref[...]) * h)
            err = xhat2 * g2_ref[...] + b2_ref[...] - tgt_ref[...]
            small_ref[:, 3 * dm:] += jnp.sum(0.5 * _lanemean(err * err), axis=0, keepdims=True)
            dy = err * (1.0 / D_MODEL)
            small_ref[:, dm:2 * dm] += _rowsum(dy * xhat2)
            small_ref[:, 2 * dm:3 * dm] += _rowsum(dy)
            dr2 = _ln_bwd(dy * g2_ref[...], xhat2, rstd2)
            dr2_ref[...] = dr2
            small_ref[:, 0:dm] += _rowsum(dr2 * h)
            dh_ref[...] = ((1.0 + gm_ref[...]) * dr2).astype(BF16)

    row = pl.BlockSpec((tm, D_MODEL), lambda i, f: (i, 0))
    vec = _full((1, D_MODEL))
    return pl.pallas_call(
        body, name="mlp_fwd", grid=(T // tm, nf),
        in_specs=[row, pl.BlockSpec((MLP_SLABS, half, tf), lambda i, f: (f, 0, 0)),
                  pl.BlockSpec((MLP_SLABS, half, tf), lambda i, f: (f, 0, 0)),
                  pl.BlockSpec((MLP_SLABS, tf, D_MODEL), lambda i, f: (f, 0, 0)),
                  row, vec, vec, vec, vec, vec, row],
        out_specs=[pl.BlockSpec((tm, MLP_SLABS * tf), lambda i, f: (i, f)), row, row, _full((1, 3 * D_MODEL + 128))],
        out_shape=[jax.ShapeDtypeStruct((T, N_DEV * tf), BF16), jax.ShapeDtypeStruct((T, D_MODEL), F32),
                   jax.ShapeDtypeStruct((T, D_MODEL), BF16), jax.ShapeDtypeStruct((1, 3 * D_MODEL + 128), F32)],
        scratch_shapes=[pltpu.VMEM((tm, D_MODEL), F32)],
        compiler_params=_params(),
    )(u2, w1[0], w1[1], w2, xhat1, ln1_g, ln1_b, g_m, ln2_g, ln2_b, target)


def _mlp_bwd(dh, w1, w2, r, dr2, xhat1, rstd1, mix, ln1_g, ln1_b, sc_m, g_a):
    T = dh.shape[0]
    half, tf = w1[0].shape[1:]
    nf = N_DEV // MLP_SLABS
    tm = min(ROW_TILE, T)

    def body(dh_ref, w1a_ref, w1b_ref, w2_ref, r_ref, dr2_ref, xhat_ref, rstd_ref, mix_ref, g1_ref, b1_ref, sc_ref, ga_ref,
             dhpre_ref, dr1_ref, dmix_ref, small_ref, acc_ref):
        i, f = pl.program_id(0), pl.program_id(1)
        dm = D_MODEL

        @pl.when((i == 0) & (f == 0))
        def _():
            small_ref[...] = jnp.zeros_like(small_ref)

        @pl.when(f == 0)
        def _():
            acc_ref[...] = jnp.zeros_like(acc_ref)

        dht = dh_ref[...]
        part = None
        for s in range(MLP_SLABS):
            cols = slice(s * tf, (s + 1) * tf)
            dhpre = (_dot(dht, w2_ref[s], NT) * (2.0 * r_ref[:, cols].astype(F32))).astype(BF16)
            dhpre_ref[:, cols] = dhpre
            d = jnp.concatenate([_dot(dhpre, w1a_ref[s], NT), _dot(dhpre, w1b_ref[s], NT)], axis=1)
            part = d if part is None else part + d
        acc_ref[...] += part

        @pl.when(f == nf - 1)
        def _():
            du2 = acc_ref[...]
            xhat = xhat_ref[...]
            x1 = xhat * g1_ref[...] + b1_ref[...]
            dx1 = ALPHA * dr2_ref[...] + du2 * (1.0 + sc_ref[...])
            small_ref[:, 2 * dm:3 * dm] += _rowsum(du2 * x1)
            small_ref[:, dm:2 * dm] += _rowsum(du2)
            small_ref[:, 3 * dm:4 * dm] += _rowsum(dx1 * xhat)
            small_ref[:, 4 * dm:5 * dm] += _rowsum(dx1)
            dr1 = _ln_bwd(dx1 * g1_ref[...], xhat, rstd_ref[:, 0:1])
            dr1_ref[...] = dr1
            small_ref[:, 0:dm] += _rowsum(dr1 * mix_ref[...])
            dmix_ref[...] = ((1.0 + ga_ref[...]) * dr1).astype(BF16)

    row = pl.BlockSpec((tm, D_MODEL), lambda i, f: (i, 0))
    vec = _full((1, D_MODEL))
    return pl.pallas_call(
        body, name="mlp_bwd", grid=(T // tm, nf),
        in_specs=[row, pl.BlockSpec((MLP_SLABS, half, tf), lambda i, f: (f, 0, 0)),
                  pl.BlockSpec((MLP_SLABS, half, tf), lambda i, f: (f, 0, 0)),
                  pl.BlockSpec((MLP_SLABS, tf, D_MODEL), lambda i, f: (f, 0, 0)),
                  pl.BlockSpec((tm, MLP_SLABS * tf), lambda i, f: (i, f)), row, row,
                  pl.BlockSpec((tm, 128), lambda i, f: (i, 0)), row, vec, vec, vec, vec],
        out_specs=[pl.BlockSpec((tm, MLP_SLABS * tf), lambda i, f: (i, f)), row, row, _full((1, 5 * D_MODEL))],
        out_shape=[jax.ShapeDtypeStruct((T, N_DEV * tf), BF16), jax.ShapeDtypeStruct((T, D_MODEL), F32),
                   jax.ShapeDtypeStruct((T, D_MODEL), BF16), jax.ShapeDtypeStruct((1, 5 * D_MODEL), F32)],
        scratch_shapes=[pltpu.VMEM((tm, D_MODEL), F32)],
        compiler_params=_params(),
    )(dh, w1[0], w1[1], w2, r, dr2, xhat1, rstd1, mix, ln1_g, ln1_b, sc_m, g_a)


def _input_bwd(dz_h, dz_m, w_in_ext, x, dr1, sc_a, exchange=None):
    T = x.shape[0]
    tm = min(ROW_TILE, T)

    def body(dzh_ref, dzm_ref, w_ref, x_ref, dr1_ref, sc_ref, gx_ref, small_ref):
        @pl.when(pl.program_id(0) == 0)
        def _():
            small_ref[...] = jnp.zeros_like(small_ref)

        du = _bdot(dzh_ref[...], w_ref[0:2048, :]) + _bdot(dzm_ref[...], w_ref[2048:3072, :])
        gx_ref[...] = ALPHA * dr1_ref[...] + du * (1.0 + sc_ref[...])
        small_ref[:, D_MODEL:] += _rowsum(du * x_ref[...])
        small_ref[:, 0:D_MODEL] += _rowsum(du)

    row = pl.BlockSpec((tm, D_MODEL), lambda i: (i, 0))
    vec = _full((1, D_MODEL))
    return _call(
        body, "input_bwd", (dz_h, dz_m, w_in_ext, x, dr1, sc_a), grid=(T // tm,),
        in_specs=[pl.BlockSpec((tm, 2048), lambda i: (i, 0)), row, _full(w_in_ext.shape), row, row, vec],
        out_specs=[row, _full((1, 2 * D_MODEL))],
        out_shape=[jax.ShapeDtypeStruct((T, D_MODEL), F32), jax.ShapeDtypeStruct((1, 2 * D_MODEL), F32)],
        exchange=exchange)


def _adam_math(w, g, m, v):
    m = ADAM_B1 * m + (1.0 - ADAM_B1) * g
    v = ADAM_B2 * v + (1.0 - ADAM_B2) * (g * g)
    m_hat = m / (1.0 - ADAM_B1 ** ADAM_STEP)
    v_hat = v / (1.0 - ADAM_B2 ** ADAM_STEP)
    return -ADAM_LR * (m_hat / (jnp.sqrt(v_hat) + ADAM_EPS) + ADAM_WD * w), m, v


def _adam(g_slabs, w, m, v, name, g_fn=None, g_extra=()):
    R, C = w.shape
    tr = 256 if R % 256 == 0 else R
    ns = 0 if g_slabs is None else g_slabs.shape[0]
    slab_rows = tr if g_slabs is None or g_slabs.shape[1] == R else g_slabs.shape[1]
    assert slab_rows == tr or tr == R
    ne = len(g_extra)

    def body(*refs):
        e_refs = refs[:ne]
        refs = refs[ne:]
        if ns:
            gs_ref, refs = refs[0], refs[1:]
        w_ref, m_ref, v_ref, g_ref, d_ref, nm_ref, nv_ref = refs
        if g_fn is not None:
            g = g_fn(*e_refs)
        else:
            g = gs_ref[0].astype(F32)
            for s in range(1, ns):
                g = g + gs_ref[s].astype(F32)
            g = g[:tr]
        d, nm, nv = _adam_math(w_ref[...], g, m_ref[...], v_ref[...])
        g_ref[...] = g
        d_ref[...] = d
        nm_ref[...] = nm
        nv_ref[...] = nv

    blk = pl.BlockSpec((tr, C), lambda i: (i, 0))
    in_specs = [pl.BlockSpec((tr, e.shape[1]), lambda i: (i, 0)) if e.shape[0] == R else _full(e.shape) for e in g_extra]
    args = list(g_extra)
    if ns:
        in_specs.append(pl.BlockSpec((ns, slab_rows, C), lambda i: (0, i, 0)))
        args.append(g_slabs)
    return pl.pallas_call(
        body, name=name, grid=(R // tr,), in_specs=in_specs + [blk] * 3, out_specs=[blk] * 4,
        out_shape=[jax.ShapeDtypeStruct((R, C), F32)] * 4, compiler_params=_params(),
    )(*args, w, m, v)


def _adam_small(small_all, params):
    n = len(params)

    def body(*refs):
        s_ref, refs = refs[0], refs[1:]
        wmv, loss_ref, outs = refs[:3 * n], refs[3 * n], refs[3 * n + 1:]
        tot = s_ref[0]
        for i in range(1, N_DEV):
            tot = tot + s_ref[i]
        loss_ref[...] = tot[:, SMALL_W - 128:]
        for j, (w, _, _, off) in enumerate(params):
            w_ref, m_ref, v_ref = wmv[3 * j:3 * j + 3]
            g_ref, d_ref, nm_ref, nv_ref = outs[4 * j:4 * j + 4]
            if w.shape[0] == 2:
                lb = _lower_bound(w_ref)
                g0 = tot[:, off:off + w.shape[1]] * lb * (1.0 - lb)
                rows = [(slice(0, 1), g0), (slice(1, 2), -g0)]
            else:
                rows = [(slice(0, 1), tot[:, off:off + w.shape[1]])]
            for rs, g in rows:
                d, nm, nv = _adam_math(w_ref[rs, :], g, m_ref[rs, :], v_ref[rs, :])
                g_ref[rs, :], d_ref[rs, :], nm_ref[rs, :], nv_ref[rs, :] = g, d, nm, nv

    out_shape = [jax.ShapeDtypeStruct((1, 128), F32)]
    for w, _, _, _ in params:
        out_shape += [jax.ShapeDtypeStruct(w.shape, F32)] * 4
    res = pl.pallas_call(body, name="adam_small", out_shape=out_shape, compiler_params=_params())(
        small_all, *[a for w, m, v, _ in params for a in (w, m, v)])
    return res[0], [tuple(res[1 + 4 * j:5 + 4 * j]) for j in range(n)]


def _cols_from_slabs(g):
    s, r, c = g.shape
    return jnp.transpose(g, (1, 0, 2)).reshape(r, s * c)


def _slabs_from_cols(w):
    r, c = w.shape
    return jnp.transpose(w.reshape(r, N_DEV, c // N_DEV), (1, 0, 2))


def _rot_half_rows(wt):
    return jnp.concatenate([-wt[32:], wt[:32]], axis=0)


def _unrot_half_rows(dwt_rot):
    return jnp.concatenate([dwt_rot[32:], -dwt_rot[:32]], axis=0)


def _ext_in_t(g):
    n, rows, k_in = g.shape
    keep = n * rows - ROPE_DIM

    def body(g_ref, o_ref, stage_ref):
        stage_ref[keep:, :] = jnp.zeros((o_ref.shape[0] - keep, k_in), F32)
        for i in range(n - 1):
            stage_ref[rows * i:rows * (i + 1), :] = g_ref[i].astype(F32)
        last = g_ref[n - 1].astype(F32)
        stage_ref[rows * (n - 1):keep, :] = last[:rows - ROPE_DIM]
        wk = last[rows - ROPE_DIM:]
        stage_ref[keep + 128:keep + 192, :] = wk
        stage_ref[keep + 384:keep + 416, :] = -wk[32:]
        stage_ref[keep + 416:keep + 448, :] = wk[:32]
        o_ref[...] = stage_ref[...].astype(BF16)

    return pl.pallas_call(body, name="ext_w_in", out_shape=jax.ShapeDtypeStruct((keep + 512, k_in), BF16),
                          scratch_shapes=[pltpu.VMEM((keep + 512, k_in), F32)], compiler_params=_params())(g)


def _ext_q_t(wt):
    r = wt.shape[1]
    z64, z128 = jnp.zeros((64, r), BF16), jnp.zeros((128, r), BF16)
    per = HEAD_DIM + ROPE_DIM
    main = [jnp.concatenate([wt[per * h:per * (h + 1)], z64], axis=0) for h in range(HEADS)]
    rot = [jnp.concatenate([z128, _rot_half_rows(wt[per * h + HEAD_DIM:per * (h + 1)]), z64], axis=0)
           for h in range(HEADS)]
    return jnp.concatenate(main + rot, axis=0)


def _ext_kv(w_kv_up):
    r = w_kv_up.shape[0]
    z128 = jnp.zeros((r, 128), BF16)
    wkv = w_kv_up.reshape(r, HEADS, 2 * HEAD_DIM)
    kpad = [jnp.concatenate([wkv[:, h, :HEAD_DIM], z128], axis=1) for h in range(HEADS)]
    vals = [wkv[:, h, HEAD_DIM:] for h in range(HEADS)]
    return jnp.concatenate(kpad + vals, axis=1)


def _w_in_grad_slabs(dwt_h, dwt_m):
    d = dwt_h.shape[1]
    rows = (dwt_h.shape[0] + 512 + ROPE_DIM) // N_DEV
    padded = rows + (-rows) % 16

    def body(h_ref, m_ref, o_ref, stage_ref):
        stage_ref[0:2048, :] = h_ref[...]
        stage_ref[2048:2560, :] = m_ref[0:512, :]
        rot = m_ref[768 + 128:768 + 192, :]
        stage_ref[2560:2592, :] = m_ref[640:672, :] + rot[32:]
        stage_ref[2592:2624, :] = m_ref[672:704, :] - rot[:32]
        zero = jnp.zeros((padded - rows, d), F32)
        for i in range(N_DEV):
            o_ref[i] = jnp.concatenate([stage_ref[rows * i:rows * (i + 1), :], zero], axis=0).astype(BF16)

    return pl.pallas_call(body, name="w_in_grad_slabs", out_shape=jax.ShapeDtypeStruct((N_DEV, padded, d), BF16),
                          scratch_shapes=[pltpu.VMEM((N_DEV * rows, d), F32)], compiler_params=_params())(dwt_h, dwt_m)


def _grad_q_from_ext_t(dwq_ext_t):
    rows = []
    for h in range(HEADS):
        main, rot = dwq_ext_t[256 * h:256 * h + 256], dwq_ext_t[1024 + 256 * h:1280 + 256 * h]
        rows += [main[:128], main[128:192] + _unrot_half_rows(rot[128:192])]
    return jnp.concatenate(rows, axis=0)


def _grad_kv_from_ext(dwkv_ext):
    kvcols = []
    for h in range(HEADS):
        kvcols += [dwkv_ext[:, 256 * h:256 * h + 128], dwkv_ext[:, 1024 + 128 * h:1152 + 128 * h]]
    return jnp.concatenate(kvcols, axis=1)


SMALL_W = 6144 + 512 + 512 + 256 + 256 + 4 * 1024 + 128


def kernel(x, c, positions, w_ada, b_ada, w_in, hg_lower_bounds, hg_norm_w, mla_q_norm_w, w_q_up, mla_kv_norm_w, w_kv_up, w_out, ln1_g, ln1_b, w_mlp_in, w_mlp_out, ln2_g, ln2_b, loss_target, m_w_ada, m_b_ada, m_w_in, m_hg_lower_bounds, m_hg_norm_w, m_mla_q_norm_w, m_w_q_up, m_mla_kv_norm_w, m_w_kv_up, m_w_out, m_ln1_g, m_ln1_b, m_w_mlp_in, m_w_mlp_out, m_ln2_g, m_ln2_b, v_w_ada, v_b_ada, v_w_in, v_hg_lower_bounds, v_hg_norm_w, v_mla_q_norm_w, v_w_q_up, v_mla_kv_norm_w, v_w_kv_up, v_w_out, v_ln1_g, v_ln1_b, v_w_mlp_in, v_w_mlp_out, v_ln2_g, v_ln2_b):
    T = x.shape[1]
    me = 4 * lax.axis_index("x") + 2 * lax.axis_index("y") + lax.axis_index("c")
    xs, tgt = x[0], loss_target[0]
    transposed = ("w_in", "w_q_up")
    as_used = lambda n, a: a[0].T if n in transposed else a[0]
    big = {n: as_used(n, a) for n, a in dict(w_in=w_in, w_q_up=w_q_up, w_kv_up=w_kv_up, w_out=w_out,
                                              w_mlp_in=w_mlp_in, w_mlp_out=w_mlp_out).items()}
    names = list(big)

    bf = {n: big[n].astype(BF16) for n in ("w_in", "w_q_up", "w_kv_up", "w_out")}
    g_in, g_c = _gather_two_level([bf["w_in"], c], name="gather_w_in")
    c_all = g_c.reshape(N_DEV, D_MODEL)

    ada_cols = w_ada.shape[2]
    mod_part, cond = _mod_part(c_all, w_ada[0], lax.dynamic_slice(b_ada, (0, me * ada_cols), (1, ada_cols)))
    (mod_all,) = _exchange([mod_part], scatter=False, name="gather_mod")
    mod_row = lax.dynamic_slice(mod_all, (0, me, 0), (N_DEV, 1, ada_cols)).reshape(1, N_DEV * ada_cols)
    sh_a, sc_a, g_a, sh_m, sc_m, g_m = [mod_row[:, D_MODEL * i:D_MODEL * (i + 1)] for i in range(6)]

    w_in_ext = _ext_in_t(g_in)
    half = D_MODEL // 2
    z, (w1_top,) = _matmul(xs, w_in_ext, "NT", "in_proj", a_fn=_modulate, extras=(sc_a, sh_a), tn=3072,
                           exchange=_StagedGather(big["w_mlp_in"], rows=(0, half)))
    (o_raw, o_gated, s_prev), (g_q, g_kv, g_out) = _hgrn_fwd(
        z, hg_lower_bounds, hg_norm_w, exchange=_Exchange([bf["w_q_up"], bf["w_kv_up"], bf["w_out"]], False))
    wq_ext = _ext_q_t(g_q.reshape(N_DEV * g_q.shape[1], g_q.shape[2]))
    wkv_ext = _ext_kv(_cols_from_slabs(g_kv))
    w_out_full = g_out.reshape(D_MODEL, D_MODEL)
    inv_freq = 1.0 / (ROPE_THETA ** (jnp.arange(0, ROPE_DIM, 2, dtype=F32) / ROPE_DIM))
    zeros = lambda n: jnp.zeros((n,), F32)
    invf = jnp.concatenate([zeros(128), inv_freq, inv_freq, zeros(64)]).reshape(1, QK_PAD)
    m_rot = jnp.concatenate([zeros(128), jnp.ones((64,), F32), zeros(64)]).reshape(1, QK_PAD)
    q, k, v, c1, s1, cqn, ckvn = _mla_pre(z, positions.reshape(T, 1), invf, m_rot, wq_ext, wkv_ext,
                                          mla_q_norm_w, mla_kv_norm_w)[0]
    (o_mla, lse), (w1_bot, w2) = _attn_fwd(
        q, k, v, exchange=[_StagedGather(big["w_mlp_in"], 0.85, rows=(half, half)),
                           _StagedGather(big["w_mlp_out"], 0.85)])
    w1 = (w1_top, w1_bot)
    mix, xhat1, rstd1, u2 = _mix_ln1(o_gated, o_mla, w_out_full, xs, g_a, ln1_g, ln1_b, sc_m, sh_m)[0]
    r, dr2, dh, small_mlp_fwd = _mlp_fwd(u2, w1, w2, xhat1, ln1_g, ln1_b, g_m, ln2_g, ln2_b, tgt)

    dhpre, dr1, dmix, small_mlp_bwd = _mlp_bwd(dh, w1, w2, r, dr2, xhat1, rstd1, mix, ln1_g, ln1_b, sc_m, g_a)
    received = {}
    dw2 = _matmul(r, dh, "TN", "wgrad_mlp_out", out_dtype=BF16, a_fn=_square, tm=1024, tk=2048)
    dw1 = _matmul(u2, dhpre, "TN", "wgrad_mlp_in", out_dtype=BF16, tm=1024, tk=2048, out_slabs=N_DEV)
    dmixcat = _matmul(dmix, w_out_full, "NT", "dgrad_out", tm=1024)
    dw_out = jnp.concatenate([_matmul(o_gated, dmix, "TN", "wgrad_out_hg", out_dtype=BF16, tk=2048),
                              _matmul(o_mla, dmix, "TN", "wgrad_out_mla", out_dtype=BF16, tk=2048)], axis=0)
    (dz_h, small_hgrn), (received["w_out"], received["w_mlp_in"]) = _hgrn_bwd(
        dmixcat, z, o_raw, s_prev, hg_lower_bounds, hg_norm_w,
        exchange=[_Exchange([dw_out.reshape(N_DEV, D_MODEL // N_DEV, D_MODEL)], True), _StagedScatter(dw1, 0.1)])
    (dq, dk, dv), (received["w_mlp_out"],) = _attn_bwd(
        q, k, v, dmixcat, o_mla, lse,
        exchange=_Exchange([dw2.reshape(N_DEV, dw2.shape[0] // N_DEV, D_MODEL)], True))
    dz_m, dq_ext, dkv_ext, small_mla = _mla_bwd(dq, dk, dv, z, c1, s1, wq_ext, wkv_ext, mla_q_norm_w, mla_kv_norm_w)
    dwq_t = _grad_q_from_ext_t(_matmul(dq_ext, cqn, "TN", "wgrad_q_up", tm=1024, tk=2048))
    dwkv = _grad_kv_from_ext(_matmul(ckvn, dkv_ext, "TN", "wgrad_kv_up", tn=1536, tk=2048))
    qkv_slabs = [dwq_t.reshape((N_DEV, dwq_t.shape[0] // N_DEV, dwq_t.shape[1])).astype(BF16),
                 _slabs_from_cols(dwkv).astype(BF16)]
    dwt_h, (received["w_q_up"], received["w_kv_up"]) = _matmul(
        dz_h, xs, "TN", "wgrad_in_h", b_fn=_modulate, extras=(sc_a, sh_a), tm=1024, tk=2048,
        exchange=_Exchange(qkv_slabs, True))
    dwt_m = _matmul(dz_m, xs, "TN", "wgrad_in_m", b_fn=_modulate, extras=(sc_a, sh_a), tm=1024, tk=2048)
    in_slabs = _w_in_grad_slabs(dwt_h, dwt_m)
    (grad_x, small_in), (received["w_in"],) = _input_bwd(
        dz_h, dz_m, w_in_ext, xs, dr1, sc_a, exchange=_StagedScatter(in_slabs))

    small = jnp.concatenate([small_in, small_mlp_bwd[:, :3 * D_MODEL], small_mlp_fwd[:, :D_MODEL], small_hgrn,
                             small_mla, small_mlp_bwd[:, 3 * D_MODEL:], small_mlp_fwd[:, D_MODEL:]], axis=1)
    assert small.shape == (1, SMALL_W)
    (small_all,) = _exchange([small], scatter=False, name="gather_small")

    moments = dict(w_in=(m_w_in, v_w_in), w_q_up=(m_w_q_up, v_w_q_up), w_kv_up=(m_w_kv_up, v_w_kv_up),
                   w_out=(m_w_out, v_w_out), w_mlp_in=(m_w_mlp_in, v_w_mlp_in), w_mlp_out=(m_w_mlp_out, v_w_mlp_out))
    res = {}
    for n in names:
        res[n] = _adam(received[n], big[n], as_used(n, moments[n][0]), as_used(n, moments[n][1]), name="adam_" + n)
    dmod_cols = lax.dynamic_slice(small_all.reshape(N_DEV, SMALL_W), (0, me * ada_cols), (N_DEV, ada_cols))
    cond_t = cond.T

    def ada_grad(ct_ref, dm_ref):
        g = ct_ref[:, 0:1] * dm_ref[0:1, :]
        for b in range(1, N_DEV):
            g = g + ct_ref[:, b:b + 1] * dm_ref[b:b + 1, :]
        return g

    res["w_ada"] = _adam(None, w_ada[0], m_w_ada[0], v_w_ada[0], name="adam_w_ada", g_fn=ada_grad,
                         g_extra=(cond_t, dmod_cols))

    small_params = [("b_ada", b_ada, m_b_ada, v_b_ada, 0),
                    ("hg_lower_bounds", hg_lower_bounds, m_hg_lower_bounds, v_hg_lower_bounds, 6144),
                    ("hg_norm_w", hg_norm_w, m_hg_norm_w, v_hg_norm_w, 6656),
                    ("mla_q_norm_w", mla_q_norm_w, m_mla_q_norm_w, v_mla_q_norm_w, 7168),
                    ("mla_kv_norm_w", mla_kv_norm_w, m_mla_kv_norm_w, v_mla_kv_norm_w, 7424),
                    ("ln1_g", ln1_g, m_ln1_g, v_ln1_g, 7680), ("ln1_b", ln1_b, m_ln1_b, v_ln1_b, 8704),
                    ("ln2_g", ln2_g, m_ln2_g, v_ln2_g, 9728), ("ln2_b", ln2_b, m_ln2_b, v_ln2_b, 10752)]
    loss_row, small_res = _adam_small(small_all, [p[1:] for p in small_params])
    for p, r4 in zip(small_params, small_res):
        res[p[0]] = r4
    loss = loss_row[0, 0]

    order = ["w_ada", "b_ada", "w_in", "hg_lower_bounds", "hg_norm_w", "mla_q_norm_w", "w_q_up", "mla_kv_norm_w",
             "w_kv_up", "w_out", "ln1_g", "ln1_b", "w_mlp_in", "w_mlp_out", "ln2_g", "ln2_b"]
    def as_given(n, a):
        if n in transposed:
            a = a.T
        return a[None] if n in big or n == "w_ada" else a

    shaped = {n: tuple(as_given(n, a) for a in res[n]) for n in order}
    outs = [loss, grad_x.reshape(1, T, D_MODEL)]
    for i in range(4):
        outs += [shaped[n][i] for n in order]
    return tuple(outs)
```

```python
import functools

import jax
import jax.numpy as jnp
import numpy as np
from jax import lax
from jax.experimental import pallas as pl
from jax.experimental.pallas import tpu as pltpu

F32, BF16 = jnp.float32, jnp.bfloat16
N_DEV = 8
D_MODEL = 1024
HEADS = 4
HEAD_DIM = 128
ROPE_DIM = 64
QK_PAD = 256
CHUNK = 64
ROPE_THETA = 10000.0
RMS_EPS = 1e-6
LN_EPS = 1e-5
ALPHA = 2.0 ** 0.25
ATT_SCALE = (HEAD_DIM + ROPE_DIM) ** -0.5
LN2 = float(np.log(2.0))
Q_PRESCALE = ATT_SCALE / LN2
ADAM_LR, ADAM_B1, ADAM_B2, ADAM_EPS, ADAM_WD, ADAM_STEP = 0.001, 0.9, 0.999, 1e-08, 0.01, 10
NEG_BIG = -1e30

ROW_TILE = 512
ATT_TILE = 512
HGRN_GROUP = 8
MLP_SLABS = 4
VMEM_LIMIT = 56 * 2 ** 20

NN = (((1,), (0,)), ((), ()))
NT = (((1,), (1,)), ((), ()))
TN = (((0,), (0,)), ((), ()))


def _dot(a, b, dims=NN):
    return lax.dot_general(a, b, dims, preferred_element_type=F32)


def _bdot(a, b, dims=NN):
    return lax.dot_general(a.astype(BF16), b.astype(BF16), dims, preferred_element_type=F32)


def _hdot(a, b, dims=NN):
    return lax.dot_general(a, b, dims, precision=lax.Precision.HIGHEST, preferred_element_type=F32)


def _params():
    return pltpu.CompilerParams(vmem_limit_bytes=VMEM_LIMIT)


def _sigmoid(x):
    return 1.0 / (1.0 + jnp.exp(-x))


def _rowsum(x):
    return jnp.sum(x, axis=0, keepdims=True)


def _lanemean(x):
    return jnp.mean(x, axis=-1, keepdims=True)


def _full(shape):
    nd = len(shape)
    return pl.BlockSpec(shape, lambda *_: (0,) * nd)


class _Exchange:
    def __init__(self, arrs, scatter):
        self.arrs, self.scatter, self.n, self.aliases, self.middle_at = list(arrs), scatter, len(arrs), [], 0.5
        self.out_shape = [jax.ShapeDtypeStruct((N_DEV,) + (a.shape[1:] if scatter else a.shape), a.dtype)
                          for a in self.arrs]
        n = self.n
        self.scratch = [pltpu.SemaphoreType.DMA((n, N_DEV - 1)), pltpu.SemaphoreType.DMA((n, N_DEV - 1)),
                        pltpu.SemaphoreType.DMA((n,))]

    def _copies(self, ins, outs, sems):
        send_sems, recv_sems, loc_sems = sems
        x, y, c = lax.axis_index("x"), lax.axis_index("y"), lax.axis_index("c")
        me = 4 * x + 2 * y + c
        copies = []
        for k in range(self.n):
            src_of = (lambda i, k=k: ins[k].at[i]) if self.scatter else (lambda i, k=k: ins[k])
            copies.append((pltpu.make_async_copy(src_of(me), outs[k].at[me], loc_sems.at[k]), None))
            for p in range(1, N_DEV):
                px = (1 - x) if p & 4 else x
                py = (1 - y) if p & 2 else y
                pc = (1 - c) if p & 1 else c
                peer = 4 * px + 2 * py + pc
                both = dict(send_sem=send_sems.at[k, p - 1], recv_sem=recv_sems.at[k, p - 1],
                            device_id=(px, py, pc), device_id_type=pl.DeviceIdType.MESH)
                send = pltpu.make_async_remote_copy(src_ref=src_of(peer), dst_ref=outs[k].at[me], **both)
                recv = pltpu.make_async_remote_copy(src_ref=src_of(peer), dst_ref=outs[k].at[peer], **both)
                copies.append((send, recv))
        return copies

    def start(self, ins, outs, sems):
        for first, _ in self._copies(ins, outs, sems):
            first.start()

    def middle(self, ins, outs, sems):
        pass

    def wait(self, ins, outs, sems):
        for first, recv in self._copies(ins, outs, sems):
            if recv is None:
                first.wait()
            else:
                recv.wait_recv()
                first.wait_send()


class _StagedGather:
    def __init__(self, arr, middle_at=0.8, rows=None):
        self.r0, n = rows if rows else (0, arr.shape[0])
        block = (n,) + arr.shape[1:]
        self.arrs, self.aliases, self.middle_at = [arr], [], middle_at
        self.out_shape = [jax.ShapeDtypeStruct((N_DEV,) + block, BF16)]
        self.scratch = [pltpu.VMEM((N_DEV,) + block, BF16), pltpu.SemaphoreType.DMA((7,)),
                        pltpu.SemaphoreType.DMA((7,)), pltpu.SemaphoreType.DMA((2,)), pltpu.VMEM(block, arr.dtype)]

    def _parts(self, scr):
        stage, send_sems, recv_sems, loc_sems = scr[:4]
        x, y, c = lax.axis_index("x"), lax.axis_index("y"), lax.axis_index("c")
        me, sibling = (x, y, c), (x, y, 1 - c)
        chips = [(1 - x, y), (x, 1 - y), (1 - x, 1 - y)]

        def copy(j, block, to):
            px, py, pc = block
            slot = stage.at[4 * px + 2 * py + pc]
            return pltpu.make_async_remote_copy(src_ref=slot, dst_ref=slot, send_sem=send_sems.at[j],
                                                recv_sem=recv_sems.at[j], device_id=to,
                                                device_id_type=pl.DeviceIdType.MESH)

        return stage, loc_sems, me, sibling, chips, c, copy

    def start(self, ins, outs, scr):
        stage, loc_sems, me, sibling, chips, c, copy = self._parts(scr)
        x, y, _ = me
        raw = scr[4]
        own = pltpu.make_async_copy(ins[0].at[pl.ds(self.r0, raw.shape[0])], raw, loc_sems.at[0])
        own.start()
        own.wait()
        stage[4 * x + 2 * y + c] = raw[...].astype(BF16)
        copy(0, me, sibling).start()
        for j, chip in enumerate(chips):
            copy(1 + j, me, (*chip, c)).start()

    def middle(self, ins, outs, scr):
        stage, loc_sems, me, sibling, chips, c, copy = self._parts(scr)
        for j, chip in enumerate(chips):
            copy(1 + j, (*chip, c), me).wait_recv()
            copy(4 + j, (*chip, c), sibling).start()

    def wait(self, ins, outs, scr):
        stage, loc_sems, me, sibling, chips, c, copy = self._parts(scr)
        copy(0, sibling, me).wait_recv()
        for j, chip in enumerate(chips):
            copy(4 + j, (*chip, 1 - c), me).wait_recv()
        copy(0, me, sibling).wait_send()
        for j, chip in enumerate(chips):
            copy(1 + j, me, (*chip, c)).wait_send()
            copy(4 + j, (*chip, c), sibling).wait_send()
        whole = pltpu.make_async_copy(stage, outs[0], loc_sems.at[1])
        whole.start()
        whole.wait()


class _StagedScatter:
    def __init__(self, slabs, middle_at=0.2):
        _, r, c = slabs.shape
        self.arrs, self.aliases, self.middle_at = [slabs], [], middle_at
        self.out_shape = [jax.ShapeDtypeStruct((4, r, c), slabs.dtype)]
        self.scratch = [pltpu.VMEM((N_DEV, r, c), slabs.dtype), pltpu.VMEM((4, r, c), slabs.dtype),
                        pltpu.VMEM((3, r, c), slabs.dtype), pltpu.SemaphoreType.DMA((4,)), pltpu.SemaphoreType.DMA((4,)),
                        pltpu.SemaphoreType.DMA((3,)), pltpu.SemaphoreType.DMA((3,)), pltpu.SemaphoreType.DMA((4,))]

    def _parts(self, scr):
        stage, from_sib, from_chips, sib_send, sib_recv, ici_send, ici_recv, loc_sems = scr
        x, y, c = lax.axis_index("x"), lax.axis_index("y"), lax.axis_index("c")
        chips = [(1 - x, y), (x, 1 - y), (1 - x, 1 - y)]

        def to_sibling(j):
            return pltpu.make_async_remote_copy(src_ref=stage.at[2 * j + 1 - c], dst_ref=from_sib.at[j],
                                                send_sem=sib_send.at[j], recv_sem=sib_recv.at[j],
                                                device_id=(x, y, 1 - c), device_id_type=pl.DeviceIdType.MESH)

        def to_chip(k):
            px, py = chips[k]
            return pltpu.make_async_remote_copy(src_ref=stage.at[4 * px + 2 * py + c], dst_ref=from_chips.at[k],
                                                send_sem=ici_send.at[k], recv_sem=ici_recv.at[k],
                                                device_id=(px, py, c), device_id_type=pl.DeviceIdType.MESH)

        return stage, from_sib, from_chips, loc_sems, (x, y, c), chips, to_sibling, to_chip

    def start(self, ins, outs, scr):
        stage, _, _, loc_sems, _, _, to_sibling, _ = self._parts(scr)
        load = pltpu.make_async_copy(ins[0], stage, loc_sems.at[0])
        load.start()
        load.wait()
        for j in range(4):
            to_sibling(j).start()

    def middle(self, ins, outs, scr):
        stage, from_sib, _, _, (x, y, c), _, to_sibling, to_chip = self._parts(scr)
        for j in range(4):
            to_sibling(j).wait_recv()
            mine = stage.at[2 * j + c]
            mine[...] = (mine[...].astype(F32) + from_sib[j].astype(F32)).astype(mine.dtype)
        for k in range(3):
            to_chip(k).start()

    def wait(self, ins, outs, scr):
        stage, _, from_chips, loc_sems, (x, y, c), chips, to_sibling, to_chip = self._parts(scr)
        writes = [pltpu.make_async_copy(stage.at[4 * x + 2 * y + c], outs[0].at[2 * x + y], loc_sems.at[0])]
        for k, (px, py) in enumerate(chips):
            to_chip(k).wait_recv()
            writes.append(pltpu.make_async_copy(from_chips.at[k], outs[0].at[2 * px + py], loc_sems.at[1 + k]))
        for w in writes:
            w.start()
        for j in range(4):
            to_sibling(j).wait_send()
        for k in range(3):
            to_chip(k).wait_send()
        for w in writes:
            w.wait()


def _call(body, name, args, out_shape, grid=(), in_specs=(), out_specs=(), scratch_shapes=(), exchange=None):
    if exchange is None:
        return pl.pallas_call(body, name=name, grid=grid, in_specs=list(in_specs), out_specs=list(out_specs),
                              out_shape=list(out_shape), scratch_shapes=list(scratch_shapes),
                              compiler_params=_params())(*args), None
    exs = list(exchange) if isinstance(exchange, (list, tuple)) else [exchange]
    ni, no, ns = len(args), len(out_shape), len(scratch_shapes)
    nxi, nxo = sum(len(e.arrs) for e in exs), sum(len(e.out_shape) for e in exs)
    steps = int(np.prod(grid))
    mid_step = lambda e: min(max(int(steps * e.middle_at), 1), steps - 1)
    aliases, iat, oat = {}, ni, no
    for e in exs:
        for src, dst in e.aliases:
            aliases[iat + src] = oat + dst
        iat, oat = iat + len(e.arrs), oat + len(e.out_shape)

    def wrapped(*refs):
        a, xi = refs[:ni], refs[ni:ni + nxi]
        o, xo = refs[ni + nxi:ni + nxi + no], refs[ni + nxi + no:ni + nxi + no + nxo]
        s, xs = refs[ni + nxi + no + nxo:ni + nxi + no + nxo + ns], refs[ni + nxi + no + nxo + ns:]
        parts, iat, oat, sat = [], 0, 0, 0
        for e in exs:
            parts.append((e, xi[iat:iat + len(e.arrs)], xo[oat:oat + len(e.out_shape)], xs[sat:sat + len(e.scratch)]))
            iat, oat, sat = iat + len(e.arrs), oat + len(e.out_shape), sat + len(e.scratch)
        step = 0
        for d, g in enumerate(grid):
            step = step * g + pl.program_id(d)

        @pl.when(step == 0)
        def _():
            for e, ins, outs, sems in parts:
                e.start(ins, outs, sems)

        for at_step in sorted({mid_step(e) for e in exs}):
            @pl.when(step == at_step)
            def _():
                for e, ins, outs, sems in parts:
                    if mid_step(e) == at_step:
                        e.middle(ins, outs, sems)

        body(*a, *o, *s)

        @pl.when(step == steps - 1)
        def _():
            for e, ins, outs, sems in parts:
                e.wait(ins, outs, sems)

    hbm = pl.BlockSpec(memory_space=pltpu.HBM)
    res = pl.pallas_call(
        wrapped, name=name, grid=grid, in_specs=list(in_specs) + [hbm] * nxi, out_specs=list(out_specs) + [hbm] * nxo,
        out_shape=list(out_shape) + [o_ for e in exs for o_ in e.out_shape],
        scratch_shapes=list(scratch_shapes) + [s_ for e in exs for s_ in e.scratch],
        input_output_aliases=aliases, compiler_params=_params())(*args, *[a_ for e in exs for a_ in e.arrs])
    return res[:no], res[no:]


def _gather_two_level(arrs, name):
    n = len(arrs)
    out_shape = [jax.ShapeDtypeStruct((N_DEV,) + a.shape, a.dtype) for a in arrs]

    def body(*refs):
        ins, outs = refs[:n], refs[n:2 * n]
        send_sems, recv_sems, loc_sems = refs[2 * n:]
        x, y, c = lax.axis_index("x"), lax.axis_index("y"), lax.axis_index("c")
        me, sibling = (x, y, c), (x, y, 1 - c)
        chips = [(1 - x, y), (x, 1 - y), (1 - x, 1 - y)]

        def copy(k, j, block, to, src=None):
            px, py, pc = block
            dst = outs[k].at[4 * px + 2 * py + pc]
            return pltpu.make_async_remote_copy(src_ref=dst if src is None else src, dst_ref=dst,
                                                send_sem=send_sems.at[k, j], recv_sem=recv_sems.at[k, j],
                                                device_id=to, device_id_type=pl.DeviceIdType.MESH)

        mine = [pltpu.make_async_copy(ins[k], outs[k].at[4 * x + 2 * y + c], loc_sems.at[k]) for k in range(n)]
        first = []
        for k in range(n):
            mine[k].start()
            first.append(copy(k, 0, me, sibling, src=ins[k]))
            first += [copy(k, 1 + j, me, (*chip, c), src=ins[k]) for j, chip in enumerate(chips)]
        for cp in first:
            cp.start()
        passed = []
        for j, chip in enumerate(chips):
            for k in range(n):
                copy(k, 1 + j, (*chip, c), me).wait_recv()
                passed.append(copy(k, 4 + j, (*chip, c), sibling))
                passed[-1].start()
        for k in range(n):
            copy(k, 0, sibling, me).wait_recv()
            for j, chip in enumerate(chips):
                copy(k, 4 + j, (*chip, 1 - c), me).wait_recv()
        for cp in first + passed:
            cp.wait_send()
        for cp in mine:
            cp.wait()

    vmem = pl.BlockSpec(memory_space=pltpu.VMEM)
    return pl.pallas_call(body, name=name, out_shape=out_shape, in_specs=[vmem] * n, out_specs=[vmem] * n,
                          scratch_shapes=[pltpu.SemaphoreType.DMA((n, 7)), pltpu.SemaphoreType.DMA((n, 7)),
                                          pltpu.SemaphoreType.DMA((n,))], compiler_params=_params())(*arrs)


def _exchange(arrs, scatter, name):
    ex = _Exchange(arrs, scatter)

    def body(*refs):
        ins, outs, sems = refs[:ex.n], refs[ex.n:2 * ex.n], refs[2 * ex.n:]
        ex.start(ins, outs, sems)
        ex.wait(ins, outs, sems)

    hbm = pl.BlockSpec(memory_space=pltpu.HBM)
    return pl.pallas_call(body, name=name, out_shape=ex.out_shape, in_specs=[hbm] * ex.n, out_specs=[hbm] * ex.n,
                          scratch_shapes=ex.scratch)(*ex.arrs)


def _matmul(a, b, mode, name, out_dtype=F32, tm=512, tn=1024, tk=1024, a_fn=None, b_fn=None, extras=(),
            out_slabs=None, exchange=None):
    assert not (a_fn and b_fn) and not (b_fn and mode == "NT")
    if mode == "NN":
        (M, K), N = a.shape, b.shape[1]
    elif mode == "NT":
        (M, K), N = a.shape, b.shape[0]
    else:
        (K, M), N = a.shape, b.shape[1]
    slab_w = N // out_slabs if out_slabs else None
    if out_slabs:
        tn = max(slab_w, min(tn, N) // slab_w * slab_w)
    tm, tn, tk = min(tm, M), min(tn, N), min(tk, K)
    assert M % tm == 0 and N % tn == 0 and K % tk == 0, (name, M, N, K)
    nk = K // tk
    dims = {"NN": NN, "NT": NT, "TN": TN}[mode]
    ne = len(extras)

    def body(a_ref, b_ref, *rest):
        e_refs, o_ref, acc_ref = rest[:ne], rest[ne], rest[ne + 1]
        k = pl.program_id(2)

        @pl.when(k == 0)
        def _():
            acc_ref[...] = jnp.zeros_like(acc_ref)

        at, bt = a_ref[...], b_ref[...]
        if a_fn is not None:
            at = a_fn(at.astype(F32), *[e[...] for e in e_refs])
        if b_fn is not None:
            bt = b_fn(bt.astype(F32), *[e[...] for e in e_refs])
        acc_ref[...] += _bdot(at, bt, dims)

        @pl.when(k == nk - 1)
        def _():
            if out_slabs:
                for s in range(tn // slab_w):
                    o_ref[s] = acc_ref[:, s * slab_w:(s + 1) * slab_w].astype(out_dtype)
            else:
                o_ref[...] = acc_ref[...].astype(out_dtype)

    if mode == "TN":
        a_spec = pl.BlockSpec((tk, tm), lambda i, j, k: (k, i))
        e_spec = pl.BlockSpec((1, tm), lambda i, j, k: (0, i))
    else:
        a_spec = pl.BlockSpec((tm, tk), lambda i, j, k: (i, k))
        e_spec = pl.BlockSpec((1, tk), lambda i, j, k: (0, k))
    if mode == "NT":
        b_spec = pl.BlockSpec((tn, tk), lambda i, j, k: (j, k))
    else:
        b_spec = pl.BlockSpec((tk, tn), lambda i, j, k: (k, j))
    if b_fn is not None:
        e_spec = pl.BlockSpec((1, tn), lambda i, j, k: (0, j))
    if out_slabs:
        o_shape = jax.ShapeDtypeStruct((out_slabs, M, slab_w), out_dtype)
        o_spec = pl.BlockSpec((tn // slab_w, tm, slab_w), lambda i, j, k: (j, i, 0))
    else:
        o_shape = jax.ShapeDtypeStruct((M, N), out_dtype)
        o_spec = pl.BlockSpec((tm, tn), lambda i, j, k: (i, j))
    (out,), got = _call(body, name, (a, b, *extras), [o_shape], grid=(M // tm, N // tn, nk),
                        in_specs=[a_spec, b_spec] + [e_spec] * ne, out_specs=[o_spec],
                        scratch_shapes=[pltpu.VMEM((tm, tn), F32)], exchange=exchange)
    return out if exchange is None else (out, got)


def _modulate(x, sc, sh):
    return x * (1.0 + sc) + sh


def _square(x):
    return x * x


def _mod_part(c_all, w_ada_s, b_s):
    def body(c_ref, w_ref, b_ref, mod_ref, cond_ref):
        cv = c_ref[...]
        cond = cv * _sigmoid(cv)
        cond_ref[...] = cond
        mod_ref[...] = _bdot(cond, w_ref[...]) + b_ref[...]

    return pl.pallas_call(
        body, name="mod_part",
        out_shape=[jax.ShapeDtypeStruct((N_DEV, w_ada_s.shape[1]), F32), jax.ShapeDtypeStruct(c_all.shape, F32)],
        compiler_params=_params(),
    )(c_all, w_ada_s, b_s)


def _rms_fwd(x, w):
    rs = lax.rsqrt(_lanemean(x * x) + RMS_EPS)
    return x * rs * w, rs


def _rms_bwd(x, rs, w, dy):
    xhat = x * rs
    dxh = dy * w
    return rs * (dxh - xhat * _lanemean(dxh * xhat)), dy * xhat


def _mla_pre(z, pos_col, invf, m_rot, wq_ext, wkv_ext, qnw, kvnw, exchange=None):
    T = z.shape[0]
    tm = min(ROW_TILE, T)

    def body(z_ref, pos_ref, invf_ref, mrot_ref, wq_ref, wkv_ref, qnw_ref, kvnw_ref,
             q_ref, k_ref, v_ref, c1_ref, s1_ref, cqn_ref, ckvn_ref):
        hi = slice(HEAD_DIM, QK_PAD)
        ang = pos_ref[...].astype(F32) * invf_ref[:, hi]
        c1 = jnp.concatenate([jnp.ones((tm, HEAD_DIM), F32), mrot_ref[:, hi] * jnp.cos(ang)], axis=1)
        s1 = jnp.concatenate([jnp.zeros((tm, HEAD_DIM), F32), mrot_ref[:, hi] * jnp.sin(ang)], axis=1)
        c1_ref[...] = c1
        s1_ref[...] = s1
        cqn, _ = _rms_fwd(z_ref[:, 0:256], qnw_ref[...])
        ckvn, _ = _rms_fwd(z_ref[:, 256:512], kvnw_ref[...])
        cqn_ref[...] = cqn.astype(BF16)
        ckvn_ref[...] = ckvn.astype(BF16)
        qe = _bdot(cqn, wq_ref[...], NT)
        kve = _bdot(ckvn, wkv_ref[...])
        k_rope = z_ref[:, 512:768] * c1 + z_ref[:, 768:1024] * s1
        for h in range(HEADS):
            q_ref[h] = ((qe[:, 256 * h:256 * h + 256] * c1 + qe[:, 1024 + 256 * h:1280 + 256 * h] * s1)
                        * Q_PRESCALE).astype(BF16)
            k_ref[h] = (kve[:, 256 * h:256 * h + 256] + k_rope).astype(BF16)
            v_ref[h] = kve[:, 1024 + 128 * h:1152 + 128 * h].astype(BF16)

    row = lambda i: (i, 0)
    head = lambda i: (0, i, 0)
    return _call(
        body, "mla_pre", (z, pos_col, invf, m_rot, wq_ext, wkv_ext, qnw, kvnw), grid=(T // tm,),
        in_specs=[pl.BlockSpec((tm, 1024), lambda i: (i, 2)), pl.BlockSpec((tm, 1), row),
                  _full((1, 256)), _full((1, 256)), _full(wq_ext.shape), _full(wkv_ext.shape),
                  _full((1, 256)), _full((1, 256))],
        out_specs=[pl.BlockSpec((HEADS, tm, QK_PAD), head), pl.BlockSpec((HEADS, tm, QK_PAD), head),
                   pl.BlockSpec((HEADS, tm, HEAD_DIM), head), pl.BlockSpec((tm, 256), row), pl.BlockSpec((tm, 256), row),
                   pl.BlockSpec((tm, 256), row), pl.BlockSpec((tm, 256), row)],
        out_shape=[jax.ShapeDtypeStruct((HEADS, T, QK_PAD), BF16), jax.ShapeDtypeStruct((HEADS, T, QK_PAD), BF16),
                   jax.ShapeDtypeStruct((HEADS, T, HEAD_DIM), BF16), jax.ShapeDtypeStruct((T, 256), F32),
                   jax.ShapeDtypeStruct((T, 256), F32), jax.ShapeDtypeStruct((T, 256), BF16),
                   jax.ShapeDtypeStruct((T, 256), BF16)], exchange=exchange)


def _mla_bwd(dq, dk, dv, z, c1, s1, wq_ext, wkv_ext, qnw, kvnw):
    T = z.shape[0]
    tm = min(ROW_TILE, T)

    def body(dq_ref, dk_ref, dv_ref, z_ref, c1_ref, s1_ref, wq_ref, wkv_ref, qnw_ref, kvnw_ref,
             dz_ref, dqe_ref, dkve_ref, dnw_ref):
        @pl.when(pl.program_id(0) == 0)
        def _():
            dnw_ref[...] = jnp.zeros_like(dnw_ref)

        c1, s1 = c1_ref[...], s1_ref[...]
        dkpe = jnp.zeros((tm, QK_PAD), F32)
        for h in range(HEADS):
            dqh, dkh = dq_ref[h].astype(F32) * ATT_SCALE, dk_ref[h]
            dqe_ref[:, 256 * h:256 * h + 256] = (dqh * c1).astype(BF16)
            dqe_ref[:, 1024 + 256 * h:1280 + 256 * h] = (dqh * s1).astype(BF16)
            dkve_ref[:, 256 * h:256 * h + 256] = dkh
            dkve_ref[:, 1024 + 128 * h:1152 + 128 * h] = dv_ref[h]
            dkpe = dkpe + dkh.astype(F32)
        dcqn = _dot(dqe_ref[...], wq_ref[...])
        dckvn = _dot(dkve_ref[...], wkv_ref[...], NT)
        cq, ckv = z_ref[:, 0:256], z_ref[:, 256:512]
        _, rsq = _rms_fwd(cq, qnw_ref[...])
        _, rskv = _rms_fwd(ckv, kvnw_ref[...])
        dcq, wq_rows = _rms_bwd(cq, rsq, qnw_ref[...], dcqn)
        dckv, wkv_rows = _rms_bwd(ckv, rskv, kvnw_ref[...], dckvn)
        dnw_ref[:, 0:256] += _rowsum(wq_rows)
        dnw_ref[:, 256:512] += _rowsum(wkv_rows)
        dz_ref[:, 0:256] = dcq.astype(BF16)
        dz_ref[:, 256:512] = dckv.astype(BF16)
        dz_ref[:, 512:768] = (dkpe * c1).astype(BF16)
        dz_ref[:, 768:1024] = (dkpe * s1).astype(BF16)

    row = lambda i: (i, 0)
    head = lambda i: (0, i, 0)
    return pl.pallas_call(
        body, name="mla_bwd", grid=(T // tm,),
        in_specs=[pl.BlockSpec((HEADS, tm, QK_PAD), head), pl.BlockSpec((HEADS, tm, QK_PAD), head),
                  pl.BlockSpec((HEADS, tm, HEAD_DIM), head), pl.BlockSpec((tm, 1024), lambda i: (i, 2)),
                  pl.BlockSpec((tm, 256), row), pl.BlockSpec((tm, 256), row), _full(wq_ext.shape), _full(wkv_ext.shape),
                  _full((1, 256)), _full((1, 256))],
        out_specs=[pl.BlockSpec((tm, 1024), row), pl.BlockSpec((tm, 2048), row), pl.BlockSpec((tm, 1536), row),
                   _full((1, 512))],
        out_shape=[jax.ShapeDtypeStruct((T, 1024), BF16), jax.ShapeDtypeStruct((T, 2048), BF16),
                   jax.ShapeDtypeStruct((T, 1536), BF16), jax.ShapeDtypeStruct((1, 512), F32)],
        compiler_params=_params(),
    )(dq, dk, dv, z, c1, s1, wq_ext, wkv_ext, qnw, kvnw)


_HEAD_LANES = [slice(HEAD_DIM * h, HEAD_DIM * (h + 1)) for h in range(HEADS)]


def _lower_bound(lbraw_ref):
    a0, a1 = lbraw_ref[0:1, :], lbraw_ref[1:2, :]
    mx = jnp.maximum(a0, a1)
    e0, e1 = jnp.exp(a0 - mx), jnp.exp(a1 - mx)
    return e0 / (e0 + e1)


def _tri(lower):
    r = lax.broadcasted_iota(jnp.int32, (CHUNK, CHUNK), 0)
    c = lax.broadcasted_iota(jnp.int32, (CHUNK, CHUNK), 1)
    return (r >= c) if lower else (r <= c)


def _hgrn_gates(q, f, lb, tri_lo):
    sg = _sigmoid(f)
    forget = lb + (1.0 - lb) * sg
    k = 1.0 - forget
    b = _hdot(tri_lo.astype(F32), jnp.log(forget))
    b_ref, b_last = b[CHUNK // 2 - 1:CHUNK // 2, :], b[CHUNK - 1:CHUNK, :]
    e1, e2, e3, e4 = jnp.exp(b - b_ref), jnp.exp(b_ref - b), jnp.exp(b_last - b), jnp.exp(b)
    return dict(sg=sg, forget=forget, k=k, e1=e1, e2=e2, e3=e3, e4=e4, qa=q * e1, ka=k * e2, kl=k * e3, qb=q * e4,
                decay=jnp.exp(b_last))


def _hgrn_fwd(z, lbraw, nw, exchange=None):
    T = z.shape[0]
    G = min(HGRN_GROUP, T // CHUNK)
    rows = G * CHUNK
    n_chunks = T // CHUNK

    def body(q_ref, f_ref, i_ref, g_ref, lbraw_ref, nw_ref, oraw_ref, og_ref, sp_ref, st_ref):
        @pl.when(pl.program_id(0) == 0)
        def _():
            st_ref[...] = jnp.zeros_like(st_ref)

        lb_all = _lower_bound(lbraw_ref)
        tri_lo = _tri(True)

        def chunk(cc, carry):
            rs = pl.ds(pl.multiple_of(cc * CHUNK, CHUNK), CHUNK)
            t = _hgrn_gates(q_ref[rs, :], f_ref[rs, :], lb_all, tri_lo)
            v, gate = i_ref[rs, :], g_ref[rs, :]
            st = [st_ref[h] for h in range(HEADS)]
            a = [jnp.where(tri_lo, _bdot(t["qa"][:, s], t["ka"][:, s], NT), 0.0) for s in _HEAD_LANES]
            kv = [_bdot(v[:, s], t["kl"][:, s], TN) for s in _HEAD_LANES]
            o = [_bdot(a[h], v[:, s]) + _bdot(t["qb"][:, s], st[h], NT) for h, s in enumerate(_HEAD_LANES)]
            for h, s in enumerate(_HEAD_LANES):
                sp_ref[cc, h] = st[h]
                st_ref[h] = st[h] * t["decay"][:, s] + kv[h]
            oraw_ref[rs, :] = jnp.concatenate(o, axis=1)
            on = jnp.concatenate([_rms_fwd(o[h], nw_ref[:, s])[0] for h, s in enumerate(_HEAD_LANES)], axis=1)
            og_ref[rs, :] = (on * (gate * _sigmoid(gate))).astype(BF16)
            return carry

        lax.fori_loop(0, G, chunk, 0, unroll=4)

    col = lambda j: pl.BlockSpec((rows, 512), lambda r, j=j: (r, j))
    return _call(
        body, "hgrn_fwd", (z, z, z, z, lbraw, nw), grid=(T // rows,),
        in_specs=[col(0), col(1), col(2), col(3), _full((2, 512)), _full((1, 512))],
        out_specs=[col(0), col(0), pl.BlockSpec((G, HEADS, HEAD_DIM, HEAD_DIM), lambda r: (r, 0, 0, 0))],
        out_shape=[jax.ShapeDtypeStruct((T, 512), F32), jax.ShapeDtypeStruct((T, 512), BF16),
                   jax.ShapeDtypeStruct((n_chunks, HEADS, HEAD_DIM, HEAD_DIM), F32)],
        scratch_shapes=[pltpu.VMEM((HEADS, HEAD_DIM, HEAD_DIM), F32)], exchange=exchange)


def _hgrn_bwd(dmixcat, z, oraw, sprev, lbraw, nw, exchange=None):
    T = z.shape[0]
    G = min(HGRN_GROUP, T // CHUNK)
    rows = G * CHUNK
    ng = T // rows

    def body(dog_ref, q_ref, f_ref, i_ref, g_ref, oraw_ref, sp_ref, lbraw_ref, nw_ref,
             dz_ref, dsmall_ref, dst_ref):
        @pl.when(pl.program_id(0) == 0)
        def _():
            dst_ref[...] = jnp.zeros_like(dst_ref)
            dsmall_ref[...] = jnp.zeros_like(dsmall_ref)

        lb_all = _lower_bound(lbraw_ref)
        tri_lo, tri_up = _tri(True), _tri(False)
        rowid = lax.broadcasted_iota(jnp.int32, (CHUNK, HEADS * HEAD_DIM), 0)

        def chunk(it, carry):
            cc = G - 1 - it
            rs = pl.ds(pl.multiple_of(cc * CHUNK, CHUNK), CHUNK)
            heads = list(enumerate(_HEAD_LANES))
            cat = lambda parts: jnp.concatenate(parts, axis=1)
            per_head_mean = lambda x: cat([jnp.broadcast_to(_lanemean(x[:, s]), (CHUNK, HEAD_DIM)) for s in _HEAD_LANES])
            t = _hgrn_gates(q_ref[rs, :], f_ref[rs, :], lb_all, tri_lo)
            v, gate, o, dog, nw_all = i_ref[rs, :], g_ref[rs, :], oraw_ref[rs, :], dog_ref[rs, :], nw_ref[...]
            rs_o = lax.rsqrt(per_head_mean(o * o) + RMS_EPS)
            xhat = o * rs_o
            sgg = _sigmoid(gate)
            d_on = dog * (gate * sgg)
            dz_ref[rs, 1536:2048] = (dog * (xhat * nw_all) * (sgg * (1.0 + gate * (1.0 - sgg)))).astype(BF16)
            dxh = d_on * nw_all
            do = rs_o * (dxh - xhat * per_head_mean(dxh * xhat))
            dsmall_ref[:, 512:1024] += _rowsum(d_on * xhat)
            st = [sp_ref[cc, h] for h in range(HEADS)]
            dst = [dst_ref[h] for h in range(HEADS)]
            a = [jnp.where(tri_lo, _bdot(t["qa"][:, s], t["ka"][:, s], NT), 0.0) for s in _HEAD_LANES]
            da = [jnp.where(tri_lo, _bdot(do[:, s], v[:, s], NT), 0.0) for s in _HEAD_LANES]
            dqb = cat([_bdot(do[:, s], st[h]) for h, s in heads])
            dkl = cat([_bdot(v[:, s], dst[h]) for h, s in heads])
            dv_ = cat([_bdot(t["kl"][:, s], dst[h], NT) + _bdot(a[h], do[:, s], TN) for h, s in heads])
            dqa = cat([_bdot(da[h], t["ka"][:, s]) for h, s in heads])
            dka = cat([_bdot(da[h], t["qa"][:, s], TN) for h, s in heads])
            ddecay = cat([_rowsum(dst[h] * st[h]) for h in range(HEADS)])
            for h, s in heads:
                dst_ref[h] = dst[h] * t["decay"][:, s] + _bdot(do[:, s], t["qb"][:, s], TN)
            pa, pk, pb, pl_ = dqa * t["qa"], dka * t["ka"], dqb * t["qb"], dkl * t["kl"]
            db = pa - pk + pb - pl_
            db = db + jnp.where(rowid == CHUNK // 2 - 1, _rowsum(pk - pa), 0.0)
            db = db + jnp.where(rowid == CHUNK - 1, _rowsum(pl_) + ddecay * t["decay"], 0.0)
            dlogf = _hdot(tri_up.astype(F32), db)
            dforget = dlogf / t["forget"] - (dka * t["e2"] + dkl * t["e3"])
            sg = t["sg"]
            dz_ref[rs, 0:512] = (dqa * t["e1"] + dqb * t["e4"]).astype(BF16)
            dz_ref[rs, 512:1024] = (dforget * (1.0 - lb_all) * sg * (1.0 - sg)).astype(BF16)
            dz_ref[rs, 1024:1536] = dv_.astype(BF16)
            dsmall_ref[:, 0:512] += _rowsum(dforget * (1.0 - sg))
            return carry

        lax.fori_loop(0, G, chunk, 0, unroll=4)

    col = lambda j: pl.BlockSpec((rows, 512), lambda r, j=j: (ng - 1 - r, j))
    return _call(
        body, "hgrn_bwd", (dmixcat, z, z, z, z, oraw, sprev, lbraw, nw), grid=(ng,),
        in_specs=[col(0), col(0), col(1), col(2), col(3), col(0),
                  pl.BlockSpec((G, HEADS, HEAD_DIM, HEAD_DIM), lambda r: (ng - 1 - r, 0, 0, 0)),
                  _full((2, 512)), _full((1, 512))],
        out_specs=[pl.BlockSpec((rows, 2048), lambda r: (ng - 1 - r, 0)), _full((1, 1024))],
        out_shape=[jax.ShapeDtypeStruct((T, 2048), BF16), jax.ShapeDtypeStruct((1, 1024), F32)],
        scratch_shapes=[pltpu.VMEM((HEADS, HEAD_DIM, HEAD_DIM), F32)], exchange=exchange)


def _diag_mask(t):
    r = lax.broadcasted_iota(jnp.int32, (t, t), 0)
    c = lax.broadcasted_iota(jnp.int32, (t, t), 1)
    return r >= c


def _attn_fwd(q, k, v, exchange=None):
    _, T, _ = q.shape
    t = min(ATT_TILE, T)

    def body(q_ref, k_ref, v_ref, o_ref, lse_ref):
        i = pl.program_id(1)
        qb = q_ref[...]

        rows = lambda j: pl.ds(pl.multiple_of(j * t, t), t)

        def logits(j, masked):
            s = _dot(qb, k_ref[rows(j), :], NT)
            return jnp.where(_diag_mask(t), s, NEG_BIG) if masked else s

        def absorb(s, j, carry):
            m, l, acc = carry
            mn = jnp.maximum(m, jnp.max(s, axis=-1, keepdims=True))
            p = jnp.exp2(s - mn)
            al = jnp.exp2(m - mn)
            return mn, al * l + jnp.sum(p, axis=-1, keepdims=True), al * acc + _dot(p.astype(BF16), v_ref[rows(j), :])

        def pair(j0, carry, last_masked):
            s0, s1 = logits(j0, False), logits(j0 + 1, last_masked)
            return absorb(s1, j0 + 1, absorb(s0, j0, carry))

        init = (jnp.full((t, 1), NEG_BIG, F32), jnp.zeros((t, 1), F32), jnp.zeros((t, HEAD_DIM), F32))
        carry = lax.fori_loop(0, i // 2, lambda jj, c: pair(2 * jj, c, False), init)
        m, l, acc = lax.cond(i % 2 == 1, lambda c: pair(i - 1, c, True),
                             lambda c: absorb(logits(i, True), i, c), carry)
        o_ref[...] = acc / l
        lse_ref[...] = jnp.broadcast_to(m + jnp.log2(l), (t, HEAD_DIM))

    return _call(
        body, "attn_fwd", (q, k, v), grid=(HEADS, T // t),
        in_specs=[pl.BlockSpec((None, t, QK_PAD), lambda h, i: (h, i, 0)),
                  pl.BlockSpec((None, T, QK_PAD), lambda h, i: (h, 0, 0)),
                  pl.BlockSpec((None, T, HEAD_DIM), lambda h, i: (h, 0, 0))],
        out_specs=[pl.BlockSpec((t, HEAD_DIM), lambda h, i: (i, h)),
                   pl.BlockSpec((None, t, HEAD_DIM), lambda h, i: (h, i, 0))],
        out_shape=[jax.ShapeDtypeStruct((T, HEADS * HEAD_DIM), F32), jax.ShapeDtypeStruct((HEADS, T, HEAD_DIM), F32)],
        exchange=exchange)


def _attn_bwd(q, k, v, dmixcat, o, lse, exchange=None):
    _, T, _ = q.shape
    t = min(ATT_TILE, T)
    nq = T // t

    def body(q_ref, k_ref, v_ref, do_ref, o_ref, lse_ref, dq_ref, dk_ref, dv_ref, delta_ref, dq_acc):
        j = pl.program_id(1)

        @pl.when(j == 0)
        def _():
            dq_acc[...] = jnp.zeros_like(dq_acc)

            def fill(i, carry):
                rs = pl.ds(pl.multiple_of(i * t, t), t)
                delta_ref[rs, :] = jnp.broadcast_to(
                    jnp.sum(do_ref[rs, :] * o_ref[rs, :], axis=-1, keepdims=True), (t, HEAD_DIM))
                return carry

            lax.fori_loop(0, nq, fill, 0)

        kb, vb = k_ref[...], v_ref[...]

        def steps(blocks, carry):
            dk, dv = carry
            rs = [pl.ds(pl.multiple_of(i * t, t), t) for i, _ in blocks]
            qb = [q_ref[r, :] for r in rs]
            dob = [do_ref[r, :].astype(BF16) for r in rs]
            s = [_dot(b, kb, NT) for b in qb]
            dp = [_dot(b, vb, NT) for b in dob]
            for n, (_, shift) in enumerate(blocks):
                p = jnp.exp2(s[n] - lse_ref[rs[n], 0:1])
                if shift is not None:
                    row = lax.broadcasted_iota(jnp.int32, (t, 2 * t), 0)
                    col = lax.broadcasted_iota(jnp.int32, (t, 2 * t), 1)
                    p = jnp.where(col <= row + shift, p, 0.0)
                ds = (p * (dp[n] - delta_ref[rs[n], 0:1])).astype(BF16)
                dq_acc[rs[n], :] += _dot(ds, kb)
                dk = dk + _dot(ds, qb[n], TN)
                dv = dv + _dot(p.astype(BF16), dob[n], TN)
            return dk, dv

        zero = (jnp.zeros((2 * t, QK_PAD), F32), jnp.zeros((2 * t, HEAD_DIM), F32))
        carry = steps([(2 * j, 0), (2 * j + 1, t)], zero)
        first = 2 * j + 2
        dk, dv = lax.fori_loop(0, (nq - first) // 2, lambda n, c: steps([(first + 2 * n, None), (first + 2 * n + 1, None)], c),
                               carry)
        dk_ref[...] = (dk * LN2).astype(BF16)
        dv_ref[...] = dv.astype(BF16)

        @pl.when(j == nk - 1)
        def _():
            dq_ref[...] = dq_acc[...].astype(BF16)

    nk = nq // 2
    return _call(
        body, "attn_bwd", (q, k, v, dmixcat, o, lse), grid=(HEADS, nk),
        in_specs=[pl.BlockSpec((None, T, QK_PAD), lambda h, j: (h, 0, 0)),
                  pl.BlockSpec((None, 2 * t, QK_PAD), lambda h, j: (h, j, 0)),
                  pl.BlockSpec((None, 2 * t, HEAD_DIM), lambda h, j: (h, j, 0)),
                  pl.BlockSpec((T, HEAD_DIM), lambda h, j: (0, HEADS + h)),
                  pl.BlockSpec((T, HEAD_DIM), lambda h, j: (0, h)),
                  pl.BlockSpec((None, T, HEAD_DIM), lambda h, j: (h, 0, 0))],
        out_specs=[pl.BlockSpec((None, T, QK_PAD), lambda h, j: (h, 0, 0)),
                   pl.BlockSpec((None, 2 * t, QK_PAD), lambda h, j: (h, j, 0)),
                   pl.BlockSpec((None, 2 * t, HEAD_DIM), lambda h, j: (h, j, 0))],
        out_shape=[jax.ShapeDtypeStruct((HEADS, T, QK_PAD), BF16), jax.ShapeDtypeStruct((HEADS, T, QK_PAD), BF16),
                   jax.ShapeDtypeStruct((HEADS, T, HEAD_DIM), BF16)],
        scratch_shapes=[pltpu.VMEM((T, HEAD_DIM), F32), pltpu.VMEM((T, QK_PAD), F32)], exchange=exchange)


def _ln_fwd(r):
    mu = _lanemean(r)
    xc = r - mu
    rstd = lax.rsqrt(_lanemean(xc * xc) + LN_EPS)
    return xc * rstd, rstd


def _ln_bwd(dxh, xhat, rstd):
    return rstd * (dxh - _lanemean(dxh) - xhat * _lanemean(dxh * xhat))


def _mix_ln1(o_hg, o_mla, w_out, x, g_a, ln1_g, ln1_b, sc_m, sh_m, exchange=None):
    T = x.shape[0]
    tm = min(ROW_TILE, T)
    half = o_hg.shape[1]

    def body(hg_ref, mla_ref, w_ref, x_ref, ga_ref, g_ref, b_ref, sc_ref, sh_ref, mix_ref, xhat_ref, rstd_ref, u2_ref):
        mix = _dot(hg_ref[...], w_ref[0:half, :]) + _bdot(mla_ref[...], w_ref[half:, :])
        mix_ref[...] = mix
        xhat, rstd = _ln_fwd(ALPHA * x_ref[...] + (1.0 + ga_ref[...]) * mix)
        xhat_ref[...] = xhat
        rstd_ref[...] = jnp.broadcast_to(rstd, (tm, 128))
        u2_ref[...] = _modulate(xhat * g_ref[...] + b_ref[...], sc_ref[...], sh_ref[...]).astype(BF16)

    row = pl.BlockSpec((tm, D_MODEL), lambda i: (i, 0))
    vec = _full((1, D_MODEL))
    halfrow = pl.BlockSpec((tm, half), lambda i: (i, 0))
    return _call(
        body, "mix_ln1", (o_hg, o_mla, w_out, x, g_a, ln1_g, ln1_b, sc_m, sh_m), grid=(T // tm,),
        in_specs=[halfrow, halfrow, _full(w_out.shape), row, vec, vec, vec, vec, vec],
        out_specs=[row, row, pl.BlockSpec((tm, 128), lambda i: (i, 0)), row],
        out_shape=[jax.ShapeDtypeStruct((T, D_MODEL), F32), jax.ShapeDtypeStruct((T, D_MODEL), F32),
                   jax.ShapeDtypeStruct((T, 128), F32), jax.ShapeDtypeStruct((T, D_MODEL), BF16)],
        exchange=exchange)


def _mlp_fwd(u2, w1, w2, xhat1, ln1_g, ln1_b, g_m, ln2_g, ln2_b, target):
    T = u2.shape[0]
    half, tf = w1[0].shape[1:]
    nf = N_DEV // MLP_SLABS
    tm = min(ROW_TILE, T)

    def body(u2_ref, w1a_ref, w1b_ref, w2_ref, xhat_ref, g1_ref, b1_ref, gm_ref, g2_ref, b2_ref, tgt_ref,
             r_ref, dr2_ref, dh_ref, small_ref, acc_ref):
        i, f = pl.program_id(0), pl.program_id(1)
        dm = D_MODEL

        @pl.when((i == 0) & (f == 0))
        def _():
            small_ref[...] = jnp.zeros_like(small_ref)

        @pl.when(f == 0)
        def _():
            acc_ref[...] = jnp.zeros_like(acc_ref)

        u2t = u2_ref[...]
        part = None
        for s in range(MLP_SLABS):
            r = jnp.maximum(_dot(u2t[:, :half], w1a_ref[s]) + _dot(u2t[:, half:], w1b_ref[s]), 0.0)
            r_ref[:, s * tf:(s + 1) * tf] = r.astype(BF16)
            d = _bdot(r * r, w2_ref[s])
            part = d if part is None else part + d
        acc_ref[...] += part

        @pl.when(f == nf - 1)
        def _():
            h = acc_ref[...]
            x1 = xhat_ref[...] * g1_ref[...] + b1_ref[...]
            xhat2, rstd2 = _ln_fwd(ALPHA * x1 + (1.0 + gm_ref[...]) * h)
            err = xhat2 * g2_ref[...] + b2_ref[...] - tgt_ref[...]
            small_ref[:, 3 * dm:] += jnp.sum(0.5 * _lanemean(err * err), axis=0, keepdims=True)
            dy = err * (1.0 / D_MODEL)
            small_ref[:, dm:2 * dm] += _rowsum(dy * xhat2)
            small_ref[:, 2 * dm:3 * dm] += _rowsum(dy)
            dr2 = _ln_bwd(dy * g2_ref[...], xhat2, rstd2)
            dr2_ref[...] = dr2
            small_ref[:, 0:dm] += _rowsum(dr2 * h)
            dh_ref[...] = ((1.0 + gm_ref[...]) * dr2).astype(BF16)

    row = pl.BlockSpec((tm, D_MODEL), lambda i, f: (i, 0))
    vec = _full((1, D_MODEL))
    return pl.pallas_call(
        body, name="mlp_fwd", grid=(T // tm, nf),
        in_specs=[row, pl.BlockSpec((MLP_SLABS, half, tf), lambda i, f: (f, 0, 0)),
                  pl.BlockSpec((MLP_SLABS, half, tf), lambda i, f: (f, 0, 0)),
                  pl.BlockSpec((MLP_SLABS, tf, D_MODEL), lambda i, f: (f, 0, 0)),
                  row, vec, vec, vec, vec, vec, row],
        out_specs=[pl.BlockSpec((tm, MLP_SLABS * tf), lambda i, f: (i, f)), row, row, _full((1, 3 * D_MODEL + 128))],
        out_shape=[jax.ShapeDtypeStruct((T, N_DEV * tf), BF16), jax.ShapeDtypeStruct((T, D_MODEL), F32),
                   jax.ShapeDtypeStruct((T, D_MODEL), BF16), jax.ShapeDtypeStruct((1, 3 * D_MODEL + 128), F32)],
        scratch_shapes=[pltpu.VMEM((tm, D_MODEL), F32)],
        compiler_params=_params(),
    )(u2, w1[0], w1[1], w2, xhat1, ln1_g, ln1_b, g_m, ln2_g, ln2_b, target)


def _mlp_bwd(dh, w1, w2, r, dr2, xhat1, rstd1, mix, ln1_g, ln1_b, sc_m, g_a):
    T = dh.shape[0]
    half, tf = w1[0].shape[1:]
    nf = N_DEV // MLP_SLABS
    tm = min(ROW_TILE, T)

    def body(dh_ref, w1a_ref, w1b_ref, w2_ref, r_ref, dr2_ref, xhat_ref, rstd_ref, mix_ref, g1_ref, b1_ref, sc_ref, ga_ref,
             dhpre_ref, dr1_ref, dmix_ref, small_ref, acc_ref):
        i, f = pl.program_id(0), pl.program_id(1)
        dm = D_MODEL

        @pl.when((i == 0) & (f == 0))
        def _():
            small_ref[...] = jnp.zeros_like(small_ref)

        @pl.when(f == 0)
        def _():
            acc_ref[...] = jnp.zeros_like(acc_ref)

        dht = dh_ref[...]
        part = None
        for s in range(MLP_SLABS):
            cols = slice(s * tf, (s + 1) * tf)
            dhpre = (_dot(dht, w2_ref[s], NT) * (2.0 * r_ref[:, cols].astype(F32))).astype(BF16)
            dhpre_ref[:, cols] = dhpre
            d = jnp.concatenate([_dot(dhpre, w1a_ref[s], NT), _dot(dhpre, w1b_ref[s], NT)], axis=1)
            part = d if part is None else part + d
        acc_ref[...] += part

        @pl.when(f == nf - 1)
        def _():
            du2 = acc_ref[...]
            xhat = xhat_ref[...]
            x1 = xhat * g1_ref[...] + b1_ref[...]
            dx1 = ALPHA * dr2_ref[...] + du2 * (1.0 + sc_ref[...])
            small_ref[:, 2 * dm:3 * dm] += _rowsum(du2 * x1)
            small_ref[:, dm:2 * dm] += _rowsum(du2)
            small_ref[:, 3 * dm:4 * dm] += _rowsum(dx1 * xhat)
            small_ref[:, 4 * dm:5 * dm] += _rowsum(dx1)
            dr1 = _ln_bwd(dx1 * g1_ref[...], xhat, rstd_ref[:, 0:1])
            dr1_ref[...] = dr1
            small_ref[:, 0:dm] += _rowsum(dr1 * mix_ref[...])
            dmix_ref[...] = ((1.0 + ga_ref[...]) * dr1).astype(BF16)

    row = pl.BlockSpec((tm, D_MODEL), lambda i, f: (i, 0))
    vec = _full((1, D_MODEL))
    return pl.pallas_call(
        body, name="mlp_bwd", grid=(T // tm, nf),
        in_specs=[row, pl.BlockSpec((MLP_SLABS, half, tf), lambda i, f: (f, 0, 0)),
                  pl.BlockSpec((MLP_SLABS, half, tf), lambda i, f: (f, 0, 0)),
                  pl.BlockSpec((MLP_SLABS, tf, D_MODEL), lambda i, f: (f, 0, 0)),
                  pl.BlockSpec((tm, MLP_SLABS * tf), lambda i, f: (i, f)), row, row,
                  pl.BlockSpec((tm, 128), lambda i, f: (i, 0)), row, vec, vec, vec, vec],
        out_specs=[pl.BlockSpec((tm, MLP_SLABS * tf), lambda i, f: (i, f)), row, row, _full((1, 5 * D_MODEL))],
        out_shape=[jax.ShapeDtypeStruct((T, N_DEV * tf), BF16), jax.ShapeDtypeStruct((T, D_MODEL), F32),
                   jax.ShapeDtypeStruct((T, D_MODEL), BF16), jax.ShapeDtypeStruct((1, 5 * D_MODEL), F32)],
        scratch_shapes=[pltpu.VMEM((tm, D_MODEL), F32)],
        compiler_params=_params(),
    )(dh, w1[0], w1[1], w2, r, dr2, xhat1, rstd1, mix, ln1_g, ln1_b, sc_m, g_a)


def _input_bwd(dz_h, dz_m, w_in_ext, x, dr1, sc_a, exchange=None):
    T = x.shape[0]
    tm = min(ROW_TILE, T)

    def body(dzh_ref, dzm_ref, w_ref, x_ref, dr1_ref, sc_ref, gx_ref, small_ref):
        @pl.when(pl.program_id(0) == 0)
        def _():
            small_ref[...] = jnp.zeros_like(small_ref)

        du = _bdot(dzh_ref[...], w_ref[0:2048, :]) + _bdot(dzm_ref[...], w_ref[2048:3072, :])
        gx_ref[...] = ALPHA * dr1_ref[...] + du * (1.0 + sc_ref[...])
        small_ref[:, D_MODEL:] += _rowsum(du * x_ref[...])
        small_ref[:, 0:D_MODEL] += _rowsum(du)

    row = pl.BlockSpec((tm, D_MODEL), lambda i: (i, 0))
    vec = _full((1, D_MODEL))
    return _call(
        body, "input_bwd", (dz_h, dz_m, w_in_ext, x, dr1, sc_a), grid=(T // tm,),
        in_specs=[pl.BlockSpec((tm, 2048), lambda i: (i, 0)), row, _full(w_in_ext.shape), row, row, vec],
        out_specs=[row, _full((1, 2 * D_MODEL))],
        out_shape=[jax.ShapeDtypeStruct((T, D_MODEL), F32), jax.ShapeDtypeStruct((1, 2 * D_MODEL), F32)],
        exchange=exchange)


def _adam_math(w, g, m, v):
    m = ADAM_B1 * m + (1.0 - ADAM_B1) * g
    v = ADAM_B2 * v + (1.0 - ADAM_B2) * (g * g)
    m_hat = m / (1.0 - ADAM_B1 ** ADAM_STEP)
    v_hat = v / (1.0 - ADAM_B2 ** ADAM_STEP)
    return -ADAM_LR * (m_hat / (jnp.sqrt(v_hat) + ADAM_EPS) + ADAM_WD * w), m, v


def _adam(g_slabs, w, m, v, name, g_fn=None, g_extra=()):
    R, C = w.shape
    tr = 256 if R % 256 == 0 else R
    ns = 0 if g_slabs is None else g_slabs.shape[0]
    slab_rows = tr if g_slabs is None or g_slabs.shape[1] == R else g_slabs.shape[1]
    assert slab_rows == tr or tr == R
    ne = len(g_extra)

    def body(*refs):
        e_refs = refs[:ne]
        refs = refs[ne:]
        if ns:
            gs_ref, refs = refs[0], refs[1:]
        w_ref, m_ref, v_ref, g_ref, d_ref, nm_ref, nv_ref = refs
        if g_fn is not None:
            g = g_fn(*e_refs)
        else:
            g = gs_ref[0].astype(F32)
            for s in range(1, ns):
                g = g + gs_ref[s].astype(F32)
            g = g[:tr]
        d, nm, nv = _adam_math(w_ref[...], g, m_ref[...], v_ref[...])
        g_ref[...] = g
        d_ref[...] = d
        nm_ref[...] = nm
        nv_ref[...] = nv

    blk = pl.BlockSpec((tr, C), lambda i: (i, 0))
    in_specs = [pl.BlockSpec((tr, e.shape[1]), lambda i: (i, 0)) if e.shape[0] == R else _full(e.shape) for e in g_extra]
    args = list(g_extra)
    if ns:
        in_specs.append(pl.BlockSpec((ns, slab_rows, C), lambda i: (0, i, 0)))
        args.append(g_slabs)
    return pl.pallas_call(
        body, name=name, grid=(R // tr,), in_specs=in_specs + [blk] * 3, out_specs=[blk] * 4,
        out_shape=[jax.ShapeDtypeStruct((R, C), F32)] * 4, compiler_params=_params(),
    )(*args, w, m, v)


def _adam_small(small_all, params):
    n = len(params)

    def body(*refs):
        s_ref, refs = refs[0], refs[1:]
        wmv, loss_ref, outs = refs[:3 * n], refs[3 * n], refs[3 * n + 1:]
        tot = s_ref[0]
        for i in range(1, N_DEV):
            tot = tot + s_ref[i]
        loss_ref[...] = tot[:, SMALL_W - 128:]
        for j, (w, _, _, off) in enumerate(params):
            w_ref, m_ref, v_ref = wmv[3 * j:3 * j + 3]
            g_ref, d_ref, nm_ref, nv_ref = outs[4 * j:4 * j + 4]
            if w.shape[0] == 2:
                lb = _lower_bound(w_ref)
                g0 = tot[:, off:off + w.shape[1]] * lb * (1.0 - lb)
                rows = [(slice(0, 1), g0), (slice(1, 2), -g0)]
            else:
                rows = [(slice(0, 1), tot[:, off:off + w.shape[1]])]
            for rs, g in rows:
                d, nm, nv = _adam_math(w_ref[rs, :], g, m_ref[rs, :], v_ref[rs, :])
                g_ref[rs, :], d_ref[rs, :], nm_ref[rs, :], nv_ref[rs, :] = g, d, nm, nv

    out_shape = [jax.ShapeDtypeStruct((1, 128), F32)]
    for w, _, _, _ in params:
        out_shape += [jax.ShapeDtypeStruct(w.shape, F32)] * 4
    res = pl.pallas_call(body, name="adam_small", out_shape=out_shape, compiler_params=_params())(
        small_all, *[a for w, m, v, _ in params for a in (w, m, v)])
    return res[0], [tuple(res[1 + 4 * j:5 + 4 * j]) for j in range(n)]


def _cols_from_slabs(g):
    s, r, c = g.shape
    return jnp.transpose(g, (1, 0, 2)).reshape(r, s * c)


def _slabs_from_cols(w):
    r, c = w.shape
    return jnp.transpose(w.reshape(r, N_DEV, c // N_DEV), (1, 0, 2))


def _rot_half_rows(wt):
    return jnp.concatenate([-wt[32:], wt[:32]], axis=0)


def _unrot_half_rows(dwt_rot):
    return jnp.concatenate([dwt_rot[32:], -dwt_rot[:32]], axis=0)


def _ext_in_t(g):
    n, rows, k_in = g.shape
    keep = n * rows - ROPE_DIM

    def body(g_ref, o_ref, stage_ref):
        stage_ref[keep:, :] = jnp.zeros((o_ref.shape[0] - keep, k_in), F32)
        for i in range(n - 1):
            stage_ref[rows * i:rows * (i + 1), :] = g_ref[i].astype(F32)
        last = g_ref[n - 1].astype(F32)
        stage_ref[rows * (n - 1):keep, :] = last[:rows - ROPE_DIM]
        wk = last[rows - ROPE_DIM:]
        stage_ref[keep + 128:keep + 192, :] = wk
        stage_ref[keep + 384:keep + 416, :] = -wk[32:]
        stage_ref[keep + 416:keep + 448, :] = wk[:32]
        o_ref[...] = stage_ref[...].astype(BF16)

    return pl.pallas_call(body, name="ext_w_in", out_shape=jax.ShapeDtypeStruct((keep + 512, k_in), BF16),
                          scratch_shapes=[pltpu.VMEM((keep + 512, k_in), F32)], compiler_params=_params())(g)


def _ext_q_t(wt):
    r = wt.shape[1]
    z64, z128 = jnp.zeros((64, r), BF16), jnp.zeros((128, r), BF16)
    per = HEAD_DIM + ROPE_DIM
    main = [jnp.concatenate([wt[per * h:per * (h + 1)], z64], axis=0) for h in range(HEADS)]
    rot = [jnp.concatenate([z128, _rot_half_rows(wt[per * h + HEAD_DIM:per * (h + 1)]), z64], axis=0)
           for h in range(HEADS)]
    return jnp.concatenate(main + rot, axis=0)


def _ext_kv(w_kv_up):
    r = w_kv_up.shape[0]
    z128 = jnp.zeros((r, 128), BF16)
    wkv = w_kv_up.reshape(r, HEADS, 2 * HEAD_DIM)
    kpad = [jnp.concatenate([wkv[:, h, :HEAD_DIM], z128], axis=1) for h in range(HEADS)]
    vals = [wkv[:, h, HEAD_DIM:] for h in range(HEADS)]
    return jnp.concatenate(kpad + vals, axis=1)


def _w_in_grad_slabs(dwt_h, dwt_m):
    d = dwt_h.shape[1]
    rows = (dwt_h.shape[0] + 512 + ROPE_DIM) // N_DEV
    padded = rows + (-rows) % 16

    def body(h_ref, m_ref, o_ref, stage_ref):
        stage_ref[0:2048, :] = h_ref[...]
        stage_ref[2048:2560, :] = m_ref[0:512, :]
        rot = m_ref[768 + 128:768 + 192, :]
        stage_ref[2560:2592, :] = m_ref[640:672, :] + rot[32:]
        stage_ref[2592:2624, :] = m_ref[672:704, :] - rot[:32]
        zero = jnp.zeros((padded - rows, d), F32)
        for i in range(N_DEV):
            o_ref[i] = jnp.concatenate([stage_ref[rows * i:rows * (i + 1), :], zero], axis=0).astype(BF16)

    return pl.pallas_call(body, name="w_in_grad_slabs", out_shape=jax.ShapeDtypeStruct((N_DEV, padded, d), BF16),
                          scratch_shapes=[pltpu.VMEM((N_DEV * rows, d), F32)], compiler_params=_params())(dwt_h, dwt_m)


def _grad_q_from_ext_t(dwq_ext_t):
    rows = []
    for h in range(HEADS):
        main, rot = dwq_ext_t[256 * h:256 * h + 256], dwq_ext_t[1024 + 256 * h:1280 + 256 * h]
        rows += [main[:128], main[128:192] + _unrot_half_rows(rot[128:192])]
    return jnp.concatenate(rows, axis=0)


def _grad_kv_from_ext(dwkv_ext):
    kvcols = []
    for h in range(HEADS):
        kvcols += [dwkv_ext[:, 256 * h:256 * h + 128], dwkv_ext[:, 1024 + 128 * h:1152 + 128 * h]]
    return jnp.concatenate(kvcols, axis=1)


SMALL_W = 6144 + 512 + 512 + 256 + 256 + 4 * 1024 + 128


def kernel(x, c, positions, w_ada, b_ada, w_in, hg_lower_bounds, hg_norm_w, mla_q_norm_w, w_q_up, mla_kv_norm_w, w_kv_up, w_out, ln1_g, ln1_b, w_mlp_in, w_mlp_out, ln2_g, ln2_b, loss_target, m_w_ada, m_b_ada, m_w_in, m_hg_lower_bounds, m_hg_norm_w, m_mla_q_norm_w, m_w_q_up, m_mla_kv_norm_w, m_w_kv_up, m_w_out, m_ln1_g, m_ln1_b, m_w_mlp_in, m_w_mlp_out, m_ln2_g, m_ln2_b, v_w_ada, v_b_ada, v_w_in, v_hg_lower_bounds, v_hg_norm_w, v_mla_q_norm_w, v_w_q_up, v_mla_kv_norm_w, v_w_kv_up, v_w_out, v_ln1_g, v_ln1_b, v_w_mlp_in, v_w_mlp_out, v_ln2_g, v_ln2_b):
    T = x.shape[1]
    me = 4 * lax.axis_index("x") + 2 * lax.axis_index("y") + lax.axis_index("c")
    xs, tgt = x[0], loss_target[0]
    transposed = ("w_in", "w_q_up")
    as_used = lambda n, a: a[0].T if n in transposed else a[0]
    big = {n: as_used(n, a) for n, a in dict(w_in=w_in, w_q_up=w_q_up, w_kv_up=w_kv_up, w_out=w_out,
                                              w_mlp_in=w_mlp_in, w_mlp_out=w_mlp_out).items()}
    names = list(big)

    bf = {n: big[n].astype(BF16) for n in ("w_in", "w_q_up", "w_kv_up", "w_out")}
    g_in, g_c = _gather_two_level([bf["w_in"], c], name="gather_w_in")
    c_all = g_c.reshape(N_DEV, D_MODEL)

    ada_cols = w_ada.shape[2]
    mod_part, cond = _mod_part(c_all, w_ada[0], lax.dynamic_slice(b_ada, (0, me * ada_cols), (1, ada_cols)))
    (mod_all,) = _exchange([mod_part], scatter=False, name="gather_mod")
    mod_row = lax.dynamic_slice(mod_all, (0, me, 0), (N_DEV, 1, ada_cols)).reshape(1, N_DEV * ada_cols)
    sh_a, sc_a, g_a, sh_m, sc_m, g_m = [mod_row[:, D_MODEL * i:D_MODEL * (i + 1)] for i in range(6)]

    w_in_ext = _ext_in_t(g_in)
    half = D_MODEL // 2
    z, (w1_top,) = _matmul(xs, w_in_ext, "NT", "in_proj", a_fn=_modulate, extras=(sc_a, sh_a), tn=3072,
                           exchange=_StagedGather(big["w_mlp_in"], rows=(0, half)))
    (o_raw, o_gated, s_prev), (g_q, g_kv, g_out) = _hgrn_fwd(
        z, hg_lower_bounds, hg_norm_w, exchange=_Exchange([bf["w_q_up"], bf["w_kv_up"], bf["w_out"]], False))
    wq_ext = _ext_q_t(g_q.reshape(N_DEV * g_q.shape[1], g_q.shape[2]))
    wkv_ext = _ext_kv(_cols_from_slabs(g_kv))
    w_out_full = g_out.reshape(D_MODEL, D_MODEL)
    inv_freq = 1.0 / (ROPE_THETA ** (jnp.arange(0, ROPE_DIM, 2, dtype=F32) / ROPE_DIM))
    zeros = lambda n: jnp.zeros((n,), F32)
    invf = jnp.concatenate([zeros(128), inv_freq, inv_freq, zeros(64)]).reshape(1, QK_PAD)
    m_rot = jnp.concatenate([zeros(128), jnp.ones((64,), F32), zeros(64)]).reshape(1, QK_PAD)
    q, k, v, c1, s1, cqn, ckvn = _mla_pre(z, positions.reshape(T, 1), invf, m_rot, wq_ext, wkv_ext,
                                          mla_q_norm_w, mla_kv_norm_w)[0]
    (o_mla, lse), (w1_bot, w2) = _attn_fwd(
        q, k, v, exchange=[_StagedGather(big["w_mlp_in"], 0.85, rows=(half, half)),
                           _StagedGather(big["w_mlp_out"], 0.85)])
    w1 = (w1_top, w1_bot)
    mix, xhat1, rstd1, u2 = _mix_ln1(o_gated, o_mla, w_out_full, xs, g_a, ln1_g, ln1_b, sc_m, sh_m)[0]
    r, dr2, dh, small_mlp_fwd = _mlp_fwd(u2, w1, w2, xhat1, ln1_g, ln1_b, g_m, ln2_g, ln2_b, tgt)

    dhpre, dr1, dmix, small_mlp_bwd = _mlp_bwd(dh, w1, w2, r, dr2, xhat1, rstd1, mix, ln1_g, ln1_b, sc_m, g_a)
    received = {}
    dw2 = _matmul(r, dh, "TN", "wgrad_mlp_out", out_dtype=BF16, a_fn=_square, tm=1024, tk=2048)
    dw1 = _matmul(u2, dhpre, "TN", "wgrad_mlp_in", out_dtype=BF16, tm=1024, tk=2048, out_slabs=N_DEV)
    dmixcat = _matmul(dmix, w_out_full, "NT", "dgrad_out", tm=1024)
    dw_out = jnp.concatenate([_matmul(o_gated, dmix, "TN", "wgrad_out_hg", out_dtype=BF16, tk=2048),
                              _matmul(o_mla, dmix, "TN", "wgrad_out_mla", out_dtype=BF16, tk=2048)], axis=0)
    (dz_h, small_hgrn), (received["w_mlp_in"],) = _hgrn_bwd(
        dmixcat, z, o_raw, s_prev, hg_lower_bounds, hg_norm_w, exchange=_StagedScatter(dw1, 0.1))
    (dq, dk, dv), (received["w_mlp_out"], received["w_out"]) = _attn_bwd(
        q, k, v, dmixcat, o_mla, lse,
        exchange=_Exchange([dw2.reshape(N_DEV, dw2.shape[0] // N_DEV, D_MODEL),
                            dw_out.reshape(N_DEV, D_MODEL // N_DEV, D_MODEL)], True))
    dz_m, dq_ext, dkv_ext, small_mla = _mla_bwd(dq, dk, dv, z, c1, s1, wq_ext, wkv_ext, mla_q_norm_w, mla_kv_norm_w)
    dwq_t = _grad_q_from_ext_t(_matmul(dq_ext, cqn, "TN", "wgrad_q_up", tm=1024, tk=2048))
    dwkv = _grad_kv_from_ext(_matmul(ckvn, dkv_ext, "TN", "wgrad_kv_up", tn=1536, tk=2048))
    qkv_slabs = [dwq_t.reshape((N_DEV, dwq_t.shape[0] // N_DEV, dwq_t.shape[1])).astype(BF16),
                 _slabs_from_cols(dwkv).astype(BF16)]
    dwt_h, (received["w_q_up"], received["w_kv_up"]) = _matmul(
        dz_h, xs, "TN", "wgrad_in_h", b_fn=_modulate, extras=(sc_a, sh_a), tm=1024, tk=2048,
        exchange=_Exchange(qkv_slabs, True))
    dwt_m = _matmul(dz_m, xs, "TN", "wgrad_in_m", b_fn=_modulate, extras=(sc_a, sh_a), tm=1024, tk=2048)
    in_slabs = _w_in_grad_slabs(dwt_h, dwt_m)
    (grad_x, small_in), (received["w_in"],) = _input_bwd(
        dz_h, dz_m, w_in_ext, xs, dr1, sc_a, exchange=_StagedScatter(in_slabs))

    small = jnp.concatenate([small_in, small_mlp_bwd[:, :3 * D_MODEL], small_mlp_fwd[:, :D_MODEL], small_hgrn,
                             small_mla, small_mlp_bwd[:, 3 * D_MODEL:], small_mlp_fwd[:, D_MODEL:]], axis=1)
    assert small.shape == (1, SMALL_W)
    (small_all,) = _exchange([small], scatter=False, name="gather_small")

    moments = dict(w_in=(m_w_in, v_w_in), w_q_up=(m_w_q_up, v_w_q_up), w_kv_up=(m_w_kv_up, v_w_kv_up),
                   w_out=(m_w_out, v_w_out), w_mlp_in=(m_w_mlp_in, v_w_mlp_in), w_mlp_out=(m_w_mlp_out, v_w_mlp_out))
    res = {}
    for n in names:
        res[n] = _adam(received[n], big[n], as_used(n, moments[n][0]), as_used(n, moments[n][1]), name="adam_" + n)
    dmod_cols = lax.dynamic_slice(small_all.reshape(N_DEV, SMALL_W), (0, me * ada_cols), (N_DEV, ada_cols))
    cond_t = cond.T

    def ada_grad(ct_ref, dm_ref):
        g = ct_ref[:, 0:1] * dm_ref[0:1, :]
        for b in range(1, N_DEV):
            g = g + ct_ref[:, b:b + 1] * dm_ref[b:b + 1, :]
        return g

    res["w_ada"] = _adam(None, w_ada[0], m_w_ada[0], v_w_ada[0], name="adam_w_ada", g_fn=ada_grad,
                         g_extra=(cond_t, dmod_cols))

    small_params = [("b_ada", b_ada, m_b_ada, v_b_ada, 0),
                    ("hg_lower_bounds", hg_lower_bounds, m_hg_lower_bounds, v_hg_lower_bounds, 6144),
                    ("hg_norm_w", hg_norm_w, m_hg_norm_w, v_hg_norm_w, 6656),
                    ("mla_q_norm_w", mla_q_norm_w, m_mla_q_norm_w, v_mla_q_norm_w, 7168),
                    ("mla_kv_norm_w", mla_kv_norm_w, m_mla_kv_norm_w, v_mla_kv_norm_w, 7424),
                    ("ln1_g", ln1_g, m_ln1_g, v_ln1_g, 7680), ("ln1_b", ln1_b, m_ln1_b, v_ln1_b, 8704),
                    ("ln2_g", ln2_g, m_ln2_g, v_ln2_g, 9728), ("ln2_b", ln2_b, m_ln2_b, v_ln2_b, 10752)]
    loss_row, small_res = _adam_small(small_all, [p[1:] for p in small_params])
    for p, r4 in zip(small_params, small_res):
        res[p[0]] = r4
    loss = loss_row[0, 0]

    order = ["w_ada", "b_ada", "w_in", "hg_lower_bounds", "hg_norm_w", "mla_q_norm_w", "w_q_up", "mla_kv_norm_w",
             "w_kv_up", "w_out", "ln1_g", "ln1_b", "w_mlp_in", "w_mlp_out", "ln2_g", "ln2_b"]
    def as_given(n, a):
        if n in transposed:
            a = a.T
        return a[None] if n in big or n == "w_ada" else a

    shaped = {n: tuple(as_given(n, a) for a in res[n]) for n in order}
    outs = [loss, grad_x.reshape(1, T, D_MODEL)]
    for i in range(4):
        outs += [shaped[n][i] for n in order]
    return tuple(outs)
```

```python
import functools

import jax
import jax.numpy as jnp
import numpy as np
from jax import lax
from jax.experimental import pallas as pl
from jax.experimental.pallas import tpu as pltpu

F32, BF16 = jnp.float32, jnp.bfloat16
N_DEV = 8
D_MODEL = 1024
HEADS = 4
HEAD_DIM = 128
ROPE_DIM = 64
QK_PAD = 256
CHUNK = 64
ROPE_THETA = 10000.0
RMS_EPS = 1e-6
LN_EPS = 1e-5
ALPHA = 2.0 ** 0.25
ATT_SCALE = (HEAD_DIM + ROPE_DIM) ** -0.5
LN2 = float(np.log(2.0))
Q_PRESCALE = ATT_SCALE / LN2
ADAM_LR, ADAM_B1, ADAM_B2, ADAM_EPS, ADAM_WD, ADAM_STEP = 0.001, 0.9, 0.999, 1e-08, 0.01, 10
NEG_BIG = -1e30

ROW_TILE = 512
ATT_TILE = 512
HGRN_GROUP = 8
MLP_SLABS = 4
VMEM_LIMIT = 56 * 2 ** 20

NN = (((1,), (0,)), ((), ()))
NT = (((1,), (1,)), ((), ()))
TN = (((0,), (0,)), ((), ()))


def _dot(a, b, dims=NN):
    return lax.dot_general(a, b, dims, preferred_element_type=F32)


def _bdot(a, b, dims=NN):
    return lax.dot_general(a.astype(BF16), b.astype(BF16), dims, preferred_element_type=F32)


def _hdot(a, b, dims=NN):
    return lax.dot_general(a, b, dims, precision=lax.Precision.HIGHEST, preferred_element_type=F32)


def _params():
    return pltpu.CompilerParams(vmem_limit_bytes=VMEM_LIMIT)


def _sigmoid(x):
    return 1.0 / (1.0 + jnp.exp(-x))


def _rowsum(x):
    return jnp.sum(x, axis=0, keepdims=True)


def _lanemean(x):
    return jnp.mean(x, axis=-1, keepdims=True)


def _full(shape):
    nd = len(shape)
    return pl.BlockSpec(shape, lambda *_: (0,) * nd)


class _Exchange:
    def __init__(self, arrs, scatter):
        self.arrs, self.scatter, self.n, self.aliases, self.middle_at = list(arrs), scatter, len(arrs), [], 0.5
        self.out_shape = [jax.ShapeDtypeStruct((N_DEV,) + (a.shape[1:] if scatter else a.shape), a.dtype)
                          for a in self.arrs]
        n = self.n
        self.scratch = [pltpu.SemaphoreType.DMA((n, N_DEV - 1)), pltpu.SemaphoreType.DMA((n, N_DEV - 1)),
                        pltpu.SemaphoreType.DMA((n,))]

    def _copies(self, ins, outs, sems):
        send_sems, recv_sems, loc_sems = sems
        x, y, c = lax.axis_index("x"), lax.axis_index("y"), lax.axis_index("c")
        me = 4 * x + 2 * y + c
        copies = []
        for k in range(self.n):
            src_of = (lambda i, k=k: ins[k].at[i]) if self.scatter else (lambda i, k=k: ins[k])
            copies.append((pltpu.make_async_copy(src_of(me), outs[k].at[me], loc_sems.at[k]), None))
            for p in range(1, N_DEV):
                px = (1 - x) if p & 4 else x
                py = (1 - y) if p & 2 else y
                pc = (1 - c) if p & 1 else c
                peer = 4 * px + 2 * py + pc
                both = dict(send_sem=send_sems.at[k, p - 1], recv_sem=recv_sems.at[k, p - 1],
                            device_id=(px, py, pc), device_id_type=pl.DeviceIdType.MESH)
                send = pltpu.make_async_remote_copy(src_ref=src_of(peer), dst_ref=outs[k].at[me], **both)
                recv = pltpu.make_async_remote_copy(src_ref=src_of(peer), dst_ref=outs[k].at[peer], **both)
                copies.append((send, recv))
        return copies

    def start(self, ins, outs, sems):
        for first, _ in self._copies(ins, outs, sems):
            first.start()

    def middle(self, ins, outs, sems):
        pass

    def wait(self, ins, outs, sems):
        for first, recv in self._copies(ins, outs, sems):
            if recv is None:
                first.wait()
            else:
                recv.wait_recv()
                first.wait_send()


class _StagedGather:
    def __init__(self, arr, middle_at=0.8, rows=None):
        self.r0, n = rows if rows else (0, arr.shape[0])
        block = (n,) + arr.shape[1:]
        self.arrs, self.aliases, self.middle_at = [arr], [], middle_at
        self.out_shape = [jax.ShapeDtypeStruct((N_DEV,) + block, BF16)]
        self.scratch = [pltpu.VMEM((N_DEV,) + block, BF16), pltpu.SemaphoreType.DMA((7,)),
                        pltpu.SemaphoreType.DMA((7,)), pltpu.SemaphoreType.DMA((2,)), pltpu.VMEM(block, arr.dtype)]

    def _parts(self, scr):
        stage, send_sems, recv_sems, loc_sems = scr[:4]
        x, y, c = lax.axis_index("x"), lax.axis_index("y"), lax.axis_index("c")
        me, sibling = (x, y, c), (x, y, 1 - c)
        chips = [(1 - x, y), (x, 1 - y), (1 - x, 1 - y)]

        def copy(j, block, to):
            px, py, pc = block
            slot = stage.at[4 * px + 2 * py + pc]
            return pltpu.make_async_remote_copy(src_ref=slot, dst_ref=slot, send_sem=send_sems.at[j],
                                                recv_sem=recv_sems.at[j], device_id=to,
                                                device_id_type=pl.DeviceIdType.MESH)

        return stage, loc_sems, me, sibling, chips, c, copy

    def start(self, ins, outs, scr):
        stage, loc_sems, me, sibling, chips, c, copy = self._parts(scr)
        x, y, _ = me
        raw = scr[4]
        own = pltpu.make_async_copy(ins[0].at[pl.ds(self.r0, raw.shape[0])], raw, loc_sems.at[0])
        own.start()
        own.wait()
        stage[4 * x + 2 * y + c] = raw[...].astype(BF16)
        copy(0, me, sibling).start()
        for j, chip in enumerate(chips):
            copy(1 + j, me, (*chip, c)).start()

    def middle(self, ins, outs, scr):
        stage, loc_sems, me, sibling, chips, c, copy = self._parts(scr)
        for j, chip in enumerate(chips):
            copy(1 + j, (*chip, c), me).wait_recv()
            copy(4 + j, (*chip, c), sibling).start()

    def wait(self, ins, outs, scr):
        stage, loc_sems, me, sibling, chips, c, copy = self._parts(scr)
        copy(0, sibling, me).wait_recv()
        for j, chip in enumerate(chips):
            copy(4 + j, (*chip, 1 - c), me).wait_recv()
        copy(0, me, sibling).wait_send()
        for j, chip in enumerate(chips):
            copy(1 + j, me, (*chip, c)).wait_send()
            copy(4 + j, (*chip, c), sibling).wait_send()
        whole = pltpu.make_async_copy(stage, outs[0], loc_sems.at[1])
        whole.start()
        whole.wait()


class _StagedScatter:
    def __init__(self, slabs, middle_at=0.2):
        _, r, c = slabs.shape
        self.arrs, self.aliases, self.middle_at = [slabs], [], middle_at
        self.out_shape = [jax.ShapeDtypeStruct((4, r, c), slabs.dtype)]
        self.scratch = [pltpu.VMEM((N_DEV, r, c), slabs.dtype), pltpu.VMEM((4, r, c), slabs.dtype),
                        pltpu.VMEM((3, r, c), slabs.dtype), pltpu.SemaphoreType.DMA((4,)), pltpu.SemaphoreType.DMA((4,)),
                        pltpu.SemaphoreType.DMA((3,)), pltpu.SemaphoreType.DMA((3,)), pltpu.SemaphoreType.DMA((4,))]

    def _parts(self, scr):
        stage, from_sib, from_chips, sib_send, sib_recv, ici_send, ici_recv, loc_sems = scr
        x, y, c = lax.axis_index("x"), lax.axis_index("y"), lax.axis_index("c")
        chips = [(1 - x, y), (x, 1 - y), (1 - x, 1 - y)]

        def to_sibling(j):
            return pltpu.make_async_remote_copy(src_ref=stage.at[2 * j + 1 - c], dst_ref=from_sib.at[j],
                                                send_sem=sib_send.at[j], recv_sem=sib_recv.at[j],
                                                device_id=(x, y, 1 - c), device_id_type=pl.DeviceIdType.MESH)

        def to_chip(k):
            px, py = chips[k]
            return pltpu.make_async_remote_copy(src_ref=stage.at[4 * px + 2 * py + c], dst_ref=from_chips.at[k],
                                                send_sem=ici_send.at[k], recv_sem=ici_recv.at[k],
                                                device_id=(px, py, c), device_id_type=pl.DeviceIdType.MESH)

        return stage, from_sib, from_chips, loc_sems, (x, y, c), chips, to_sibling, to_chip

    def start(self, ins, outs, scr):
        stage, _, _, loc_sems, _, _, to_sibling, _ = self._parts(scr)
        load = pltpu.make_async_copy(ins[0], stage, loc_sems.at[0])
        load.start()
        load.wait()
        for j in range(4):
            to_sibling(j).start()

    def middle(self, ins, outs, scr):
        stage, from_sib, _, _, (x, y, c), _, to_sibling, to_chip = self._parts(scr)
        for j in range(4):
            to_sibling(j).wait_recv()
            mine = stage.at[2 * j + c]
            mine[...] = (mine[...].astype(F32) + from_sib[j].astype(F32)).astype(mine.dtype)
        for k in range(3):
            to_chip(k).start()

    def wait(self, ins, outs, scr):
        stage, _, from_chips, loc_sems, (x, y, c), chips, to_sibling, to_chip = self._parts(scr)
        writes = [pltpu.make_async_copy(stage.at[4 * x + 2 * y + c], outs[0].at[2 * x + y], loc_sems.at[0])]
        for k, (px, py) in enumerate(chips):
            to_chip(k).wait_recv()
            writes.append(pltpu.make_async_copy(from_chips.at[k], outs[0].at[2 * px + py], loc_sems.at[1 + k]))
        for w in writes:
            w.start()
        for j in range(4):
            to_sibling(j).wait_send()
        for k in range(3):
            to_chip(k).wait_send()
        for w in writes:
            w.wait()


def _call(body, name, args, out_shape, grid=(), in_specs=(), out_specs=(), scratch_shapes=(), exchange=None):
    if exchange is None:
        return pl.pallas_call(body, name=name, grid=grid, in_specs=list(in_specs), out_specs=list(out_specs),
                              out_shape=list(out_shape), scratch_shapes=list(scratch_shapes),
                              compiler_params=_params())(*args), None
    exs = list(exchange) if isinstance(exchange, (list, tuple)) else [exchange]
    ni, no, ns = len(args), len(out_shape), len(scratch_shapes)
    nxi, nxo = sum(len(e.arrs) for e in exs), sum(len(e.out_shape) for e in exs)
    steps = int(np.prod(grid))
    mid_step = lambda e: min(max(int(steps * e.middle_at), 1), steps - 1)
    aliases, iat, oat = {}, ni, no
    for e in exs:
        for src, dst in e.aliases:
            aliases[iat + src] = oat + dst
        iat, oat = iat + len(e.arrs), oat + len(e.out_shape)

    def wrapped(*refs):
        a, xi = refs[:ni], refs[ni:ni + nxi]
        o, xo = refs[ni + nxi:ni + nxi + no], refs[ni + nxi + no:ni + nxi + no + nxo]
        s, xs = refs[ni + nxi + no + nxo:ni + nxi + no + nxo + ns], refs[ni + nxi + no + nxo + ns:]
        parts, iat, oat, sat = [], 0, 0, 0
        for e in exs:
            parts.append((e, xi[iat:iat + len(e.arrs)], xo[oat:oat + len(e.out_shape)], xs[sat:sat + len(e.scratch)]))
            iat, oat, sat = iat + len(e.arrs), oat + len(e.out_shape), sat + len(e.scratch)
        step = 0
        for d, g in enumerate(grid):
            step = step * g + pl.program_id(d)

        @pl.when(step == 0)
        def _():
            for e, ins, outs, sems in parts:
                e.start(ins, outs, sems)

        for at_step in sorted({mid_step(e) for e in exs}):
            @pl.when(step == at_step)
            def _():
                for e, ins, outs, sems in parts:
                    if mid_step(e) == at_step:
                        e.middle(ins, outs, sems)

        body(*a, *o, *s)

        @pl.when(step == steps - 1)
        def _():
            for e, ins, outs, sems in parts:
                e.wait(ins, outs, sems)

    hbm = pl.BlockSpec(memory_space=pltpu.HBM)
    res = pl.pallas_call(
        wrapped, name=name, grid=grid, in_specs=list(in_specs) + [hbm] * nxi, out_specs=list(out_specs) + [hbm] * nxo,
        out_shape=list(out_shape) + [o_ for e in exs for o_ in e.out_shape],
        scratch_shapes=list(scratch_shapes) + [s_ for e in exs for s_ in e.scratch],
        input_output_aliases=aliases, compiler_params=_params())(*args, *[a_ for e in exs for a_ in e.arrs])
    return res[:no], res[no:]


def _gather_two_level(arrs, name):
    n = len(arrs)
    out_shape = [jax.ShapeDtypeStruct((N_DEV,) + a.shape, a.dtype) for a in arrs]

    def body(*refs):
        ins, outs = refs[:n], refs[n:2 * n]
        send_sems, recv_sems, loc_sems = refs[2 * n:]
        x, y, c = lax.axis_index("x"), lax.axis_index("y"), lax.axis_index("c")
        me, sibling = (x, y, c), (x, y, 1 - c)
        chips = [(1 - x, y), (x, 1 - y), (1 - x, 1 - y)]

        def copy(k, j, block, to, src=None):
            px, py, pc = block
            dst = outs[k].at[4 * px + 2 * py + pc]
            return pltpu.make_async_remote_copy(src_ref=dst if src is None else src, dst_ref=dst,
                                                send_sem=send_sems.at[k, j], recv_sem=recv_sems.at[k, j],
                                                device_id=to, device_id_type=pl.DeviceIdType.MESH)

        mine = [pltpu.make_async_copy(ins[k], outs[k].at[4 * x + 2 * y + c], loc_sems.at[k]) for k in range(n)]
        first = []
        for k in range(n):
            mine[k].start()
            first.append(copy(k, 0, me, sibling, src=ins[k]))
            first += [copy(k, 1 + j, me, (*chip, c), src=ins[k]) for j, chip in enumerate(chips)]
        for cp in first:
            cp.start()
        passed = []
        for j, chip in enumerate(chips):
            for k in range(n):
                copy(k, 1 + j, (*chip, c), me).wait_recv()
                passed.append(copy(k, 4 + j, (*chip, c), sibling))
                passed[-1].start()
        for k in range(n):
            copy(k, 0, sibling, me).wait_recv()
            for j, chip in enumerate(chips):
                copy(k, 4 + j, (*chip, 1 - c), me).wait_recv()
        for cp in first + passed:
            cp.wait_send()
        for cp in mine:
            cp.wait()

    vmem = pl.BlockSpec(memory_space=pltpu.VMEM)
    return pl.pallas_call(body, name=name, out_shape=out_shape, in_specs=[vmem] * n, out_specs=[vmem] * n,
                          scratch_shapes=[pltpu.SemaphoreType.DMA((n, 7)), pltpu.SemaphoreType.DMA((n, 7)),
                                          pltpu.SemaphoreType.DMA((n,))], compiler_params=_params())(*arrs)


def _exchange(arrs, scatter, name):
    ex = _Exchange(arrs, scatter)

    def body(*refs):
        ins, outs, sems = refs[:ex.n], refs[ex.n:2 * ex.n], refs[2 * ex.n:]
        ex.start(ins, outs, sems)
        ex.wait(ins, outs, sems)

    hbm = pl.BlockSpec(memory_space=pltpu.HBM)
    return pl.pallas_call(body, name=name, out_shape=ex.out_shape, in_specs=[hbm] * ex.n, out_specs=[hbm] * ex.n,
                          scratch_shapes=ex.scratch)(*ex.arrs)


def _matmul(a, b, mode, name, out_dtype=F32, tm=512, tn=1024, tk=1024, a_fn=None, b_fn=None, extras=(),
            out_slabs=None, exchange=None):
    assert not (a_fn and b_fn) and not (b_fn and mode == "NT")
    if mode == "NN":
        (M, K), N = a.shape, b.shape[1]
    elif mode == "NT":
        (M, K), N = a.shape, b.shape[0]
    else:
        (K, M), N = a.shape, b.shape[1]
    slab_w = N // out_slabs if out_slabs else None
    if out_slabs:
        tn = max(slab_w, min(tn, N) // slab_w * slab_w)
    tm, tn, tk = min(tm, M), min(tn, N), min(tk, K)
    assert M % tm == 0 and N % tn == 0 and K % tk == 0, (name, M, N, K)
    nk = K // tk
    dims = {"NN": NN, "NT": NT, "TN": TN}[mode]
    ne = len(extras)

    def body(a_ref, b_ref, *rest):
        e_refs, o_ref, acc_ref = rest[:ne], rest[ne], rest[ne + 1]
        k = pl.program_id(2)

        @pl.when(k == 0)
        def _():
            acc_ref[...] = jnp.zeros_like(acc_ref)

        at, bt = a_ref[...], b_ref[...]
        if a_fn is not None:
            at = a_fn(at.astype(F32), *[e[...] for e in e_refs])
        if b_fn is not None:
            bt = b_fn(bt.astype(F32), *[e[...] for e in e_refs])
        acc_ref[...] += _bdot(at, bt, dims)

        @pl.when(k == nk - 1)
        def _():
            if out_slabs:
                for s in range(tn // slab_w):
                    o_ref[s] = acc_ref[:, s * slab_w:(s + 1) * slab_w].astype(out_dtype)
            else:
                o_ref[...] = acc_ref[...].astype(out_dtype)

    if mode == "TN":
        a_spec = pl.BlockSpec((tk, tm), lambda i, j, k: (k, i))
        e_spec = pl.BlockSpec((1, tm), lambda i, j, k: (0, i))
    else:
        a_spec = pl.BlockSpec((tm, tk), lambda i, j, k: (i, k))
        e_spec = pl.BlockSpec((1, tk), lambda i, j, k: (0, k))
    if mode == "NT":
        b_spec = pl.BlockSpec((tn, tk), lambda i, j, k: (j, k))
    else:
        b_spec = pl.BlockSpec((tk, tn), lambda i, j, k: (k, j))
    if b_fn is not None:
        e_spec = pl.BlockSpec((1, tn), lambda i, j, k: (0, j))
    if out_slabs:
        o_shape = jax.ShapeDtypeStruct((out_slabs, M, slab_w), out_dtype)
        o_spec = pl.BlockSpec((tn // slab_w, tm, slab_w), lambda i, j, k: (j, i, 0))
    else:
        o_shape = jax.ShapeDtypeStruct((M, N), out_dtype)
        o_spec = pl.BlockSpec((tm, tn), lambda i, j, k: (i, j))
    (out,), got = _call(body, name, (a, b, *extras), [o_shape], grid=(M // tm, N // tn, nk),
                        in_specs=[a_spec, b_spec] + [e_spec] * ne, out_specs=[o_spec],
                        scratch_shapes=[pltpu.VMEM((tm, tn), F32)], exchange=exchange)
    return out if exchange is None else (out, got)


def _modulate(x, sc, sh):
    return x * (1.0 + sc) + sh


def _square(x):
    return x * x


def _mod_part(c_all, w_ada_s, b_s):
    def body(c_ref, w_ref, b_ref, mod_ref, cond_ref):
        cv = c_ref[...]
        cond = cv * _sigmoid(cv)
        cond_ref[...] = cond
        mod_ref[...] = _bdot(cond, w_ref[...]) + b_ref[...]

    return pl.pallas_call(
        body, name="mod_part",
        out_shape=[jax.ShapeDtypeStruct((N_DEV, w_ada_s.shape[1]), F32), jax.ShapeDtypeStruct(c_all.shape, F32)],
        compiler_params=_params(),
    )(c_all, w_ada_s, b_s)


def _rms_fwd(x, w):
    rs = lax.rsqrt(_lanemean(x * x) + RMS_EPS)
    return x * rs * w, rs


def _rms_bwd(x, rs, w, dy):
    xhat = x * rs
    dxh = dy * w
    return rs * (dxh - xhat * _lanemean(dxh * xhat)), dy * xhat


def _mla_pre(z, pos_col, invf, m_rot, wq_ext, wkv_ext, qnw, kvnw, exchange=None):
    T = z.shape[0]
    tm = min(ROW_TILE, T)

    def body(z_ref, pos_ref, invf_ref, mrot_ref, wq_ref, wkv_ref, qnw_ref, kvnw_ref,
             q_ref, k_ref, v_ref, c1_ref, s1_ref, cqn_ref, ckvn_ref):
        hi = slice(HEAD_DIM, QK_PAD)
        ang = pos_ref[...].astype(F32) * invf_ref[:, hi]
        c1 = jnp.concatenate([jnp.ones((tm, HEAD_DIM), F32), mrot_ref[:, hi] * jnp.cos(ang)], axis=1)
        s1 = jnp.concatenate([jnp.zeros((tm, HEAD_DIM), F32), mrot_ref[:, hi] * jnp.sin(ang)], axis=1)
        c1_ref[...] = c1
        s1_ref[...] = s1
        cqn, _ = _rms_fwd(z_ref[:, 0:256], qnw_ref[...])
        ckvn, _ = _rms_fwd(z_ref[:, 256:512], kvnw_ref[...])
        cqn_ref[...] = cqn.astype(BF16)
        ckvn_ref[...] = ckvn.astype(BF16)
        qe = _bdot(cqn, wq_ref[...], NT)
        kve = _bdot(ckvn, wkv_ref[...])
        k_rope = z_ref[:, 512:768] * c1 + z_ref[:, 768:1024] * s1
        for h in range(HEADS):
            q_ref[h] = ((qe[:, 256 * h:256 * h + 256] * c1 + qe[:, 1024 + 256 * h:1280 + 256 * h] * s1)
                        * Q_PRESCALE).astype(BF16)
            k_ref[h] = (kve[:, 256 * h:256 * h + 256] + k_rope).astype(BF16)
            v_ref[h] = kve[:, 1024 + 128 * h:1152 + 128 * h].astype(BF16)

    row = lambda i: (i, 0)
    head = lambda i: (0, i, 0)
    return _call(
        body, "mla_pre", (z, pos_col, invf, m_rot, wq_ext, wkv_ext, qnw, kvnw), grid=(T // tm,),
        in_specs=[pl.BlockSpec((tm, 1024), lambda i: (i, 2)), pl.BlockSpec((tm, 1), row),
                  _full((1, 256)), _full((1, 256)), _full(wq_ext.shape), _full(wkv_ext.shape),
                  _full((1, 256)), _full((1, 256))],
        out_specs=[pl.BlockSpec((HEADS, tm, QK_PAD), head), pl.BlockSpec((HEADS, tm, QK_PAD), head),
                   pl.BlockSpec((HEADS, tm, HEAD_DIM), head), pl.BlockSpec((tm, 256), row), pl.BlockSpec((tm, 256), row),
                   pl.BlockSpec((tm, 256), row), pl.BlockSpec((tm, 256), row)],
        out_shape=[jax.ShapeDtypeStruct((HEADS, T, QK_PAD), BF16), jax.ShapeDtypeStruct((HEADS, T, QK_PAD), BF16),
                   jax.ShapeDtypeStruct((HEADS, T, HEAD_DIM), BF16), jax.ShapeDtypeStruct((T, 256), F32),
                   jax.ShapeDtypeStruct((T, 256), F32), jax.ShapeDtypeStruct((T, 256), BF16),
                   jax.ShapeDtypeStruct((T, 256), BF16)], exchange=exchange)


def _mla_bwd(dq, dk, dv, z, c1, s1, wq_ext, wkv_ext, qnw, kvnw):
    T = z.shape[0]
    tm = min(ROW_TILE, T)

    def body(dq_ref, dk_ref, dv_ref, z_ref, c1_ref, s1_ref, wq_ref, wkv_ref, qnw_ref, kvnw_ref,
             dz_ref, dqe_ref, dkve_ref, dnw_ref):
        @pl.when(pl.program_id(0) == 0)
        def _():
            dnw_ref[...] = jnp.zeros_like(dnw_ref)

        c1, s1 = c1_ref[...], s1_ref[...]
        dkpe = jnp.zeros((tm, QK_PAD), F32)
        for h in range(HEADS):
            dqh, dkh = dq_ref[h].astype(F32) * ATT_SCALE, dk_ref[h]
            dqe_ref[:, 256 * h:256 * h + 256] = (dqh * c1).astype(BF16)
            dqe_ref[:, 1024 + 256 * h:1280 + 256 * h] = (dqh * s1).astype(BF16)
            dkve_ref[:, 256 * h:256 * h + 256] = dkh
            dkve_ref[:, 1024 + 128 * h:1152 + 128 * h] = dv_ref[h]
            dkpe = dkpe + dkh.astype(F32)
        dcqn = _dot(dqe_ref[...], wq_ref[...])
        dckvn = _dot(dkve_ref[...], wkv_ref[...], NT)
        cq, ckv = z_ref[:, 0:256], z_ref[:, 256:512]
        _, rsq = _rms_fwd(cq, qnw_ref[...])
        _, rskv = _rms_fwd(ckv, kvnw_ref[...])
        dcq, wq_rows = _rms_bwd(cq, rsq, qnw_ref[...], dcqn)
        dckv, wkv_rows = _rms_bwd(ckv, rskv, kvnw_ref[...], dckvn)
        dnw_ref[:, 0:256] += _rowsum(wq_rows)
        dnw_ref[:, 256:512] += _rowsum(wkv_rows)
        dz_ref[:, 0:256] = dcq.astype(BF16)
        dz_ref[:, 256:512] = dckv.astype(BF16)
        dz_ref[:, 512:768] = (dkpe * c1).astype(BF16)
        dz_ref[:, 768:1024] = (dkpe * s1).astype(BF16)

    row = lambda i: (i, 0)
    head = lambda i: (0, i, 0)
    return pl.pallas_call(
        body, name="mla_bwd", grid=(T // tm,),
        in_specs=[pl.BlockSpec((HEADS, tm, QK_PAD), head), pl.BlockSpec((HEADS, tm, QK_PAD), head),
                  pl.BlockSpec((HEADS, tm, HEAD_DIM), head), pl.BlockSpec((tm, 1024), lambda i: (i, 2)),
                  pl.BlockSpec((tm, 256), row), pl.BlockSpec((tm, 256), row), _full(wq_ext.shape), _full(wkv_ext.shape),
                  _full((1, 256)), _full((1, 256))],
        out_specs=[pl.BlockSpec((tm, 1024), row), pl.BlockSpec((tm, 2048), row), pl.BlockSpec((tm, 1536), row),
                   _full((1, 512))],
        out_shape=[jax.ShapeDtypeStruct((T, 1024), BF16), jax.ShapeDtypeStruct((T, 2048), BF16),
                   jax.ShapeDtypeStruct((T, 1536), BF16), jax.ShapeDtypeStruct((1, 512), F32)],
        compiler_params=_params(),
    )(dq, dk, dv, z, c1, s1, wq_ext, wkv_ext, qnw, kvnw)


_HEAD_LANES = [slice(HEAD_DIM * h, HEAD_DIM * (h + 1)) for h in range(HEADS)]


def _lower_bound(lbraw_ref):
    a0, a1 = lbraw_ref[0:1, :], lbraw_ref[1:2, :]
    mx = jnp.maximum(a0, a1)
    e0, e1 = jnp.exp(a0 - mx), jnp.exp(a1 - mx)
    return e0 / (e0 + e1)


def _tri(lower):
    r = lax.broadcasted_iota(jnp.int32, (CHUNK, CHUNK), 0)
    c = lax.broadcasted_iota(jnp.int32, (CHUNK, CHUNK), 1)
    return (r >= c) if lower else (r <= c)


def _hgrn_gates(q, f, lb, tri_lo):
    sg = _sigmoid(f)
    forget = lb + (1.0 - lb) * sg
    k = 1.0 - forget
    b = _hdot(tri_lo.astype(F32), jnp.log(forget))
    b_ref, b_last = b[CHUNK // 2 - 1:CHUNK // 2, :], b[CHUNK - 1:CHUNK, :]
    e1, e2, e3, e4 = jnp.exp(b - b_ref), jnp.exp(b_ref - b), jnp.exp(b_last - b), jnp.exp(b)
    return dict(sg=sg, forget=forget, k=k, e1=e1, e2=e2, e3=e3, e4=e4, qa=q * e1, ka=k * e2, kl=k * e3, qb=q * e4,
                decay=jnp.exp(b_last))


def _hgrn_fwd(z, lbraw, nw, exchange=None):
    T = z.shape[0]
    G = min(HGRN_GROUP, T // CHUNK)
    rows = G * CHUNK
    n_chunks = T // CHUNK

    def body(q_ref, f_ref, i_ref, g_ref, lbraw_ref, nw_ref, oraw_ref, og_ref, sp_ref, st_ref):
        @pl.when(pl.program_id(0) == 0)
        def _():
            st_ref[...] = jnp.zeros_like(st_ref)

        lb_all = _lower_bound(lbraw_ref)
        tri_lo = _tri(True)

        def chunk(cc, carry):
            rs = pl.ds(pl.multiple_of(cc * CHUNK, CHUNK), CHUNK)
            t = _hgrn_gates(q_ref[rs, :], f_ref[rs, :], lb_all, tri_lo)
            v, gate = i_ref[rs, :], g_ref[rs, :]
            st = [st_ref[h] for h in range(HEADS)]
            a = [jnp.where(tri_lo, _bdot(t["qa"][:, s], t["ka"][:, s], NT), 0.0) for s in _HEAD_LANES]
            kv = [_bdot(v[:, s], t["kl"][:, s], TN) for s in _HEAD_LANES]
            o = [_bdot(a[h], v[:, s]) + _bdot(t["qb"][:, s], st[h], NT) for h, s in enumerate(_HEAD_LANES)]
            for h, s in enumerate(_HEAD_LANES):
                sp_ref[cc, h] = st[h]
                st_ref[h] = st[h] * t["decay"][:, s] + kv[h]
            oraw_ref[rs, :] = jnp.concatenate(o, axis=1)
            on = jnp.concatenate([_rms_fwd(o[h], nw_ref[:, s])[0] for h, s in enumerate(_HEAD_LANES)], axis=1)
            og_ref[rs, :] = (on * (gate * _sigmoid(gate))).astype(BF16)
            return carry

        lax.fori_loop(0, G, chunk, 0, unroll=4)

    col = lambda j: pl.BlockSpec((rows, 512), lambda r, j=j: (r, j))
    return _call(
        body, "hgrn_fwd", (z, z, z, z, lbraw, nw), grid=(T // rows,),
        in_specs=[col(0), col(1), col(2), col(3), _full((2, 512)), _full((1, 512))],
        out_specs=[col(0), col(0), pl.BlockSpec((G, HEADS, HEAD_DIM, HEAD_DIM), lambda r: (r, 0, 0, 0))],
        out_shape=[jax.ShapeDtypeStruct((T, 512), F32), jax.ShapeDtypeStruct((T, 512), BF16),
                   jax.ShapeDtypeStruct((n_chunks, HEADS, HEAD_DIM, HEAD_DIM), F32)],
        scratch_shapes=[pltpu.VMEM((HEADS, HEAD_DIM, HEAD_DIM), F32)], exchange=exchange)


def _hgrn_bwd(dmixcat, z, oraw, sprev, lbraw, nw, exchange=None):
    T = z.shape[0]
    G = min(HGRN_GROUP, T // CHUNK)
    rows = G * CHUNK
    ng = T // rows

    def body(dog_ref, q_ref, f_ref, i_ref, g_ref, oraw_ref, sp_ref, lbraw_ref, nw_ref,
             dz_ref, dsmall_ref, dst_ref):
        @pl.when(pl.program_id(0) == 0)
        def _():
            dst_ref[...] = jnp.zeros_like(dst_ref)
            dsmall_ref[...] = jnp.zeros_like(dsmall_ref)

        lb_all = _lower_bound(lbraw_ref)
        tri_lo, tri_up = _tri(True), _tri(False)
        rowid = lax.broadcasted_iota(jnp.int32, (CHUNK, HEADS * HEAD_DIM), 0)

        def chunk(it, carry):
            cc = G - 1 - it
            rs = pl.ds(pl.multiple_of(cc * CHUNK, CHUNK), CHUNK)
            heads = list(enumerate(_HEAD_LANES))
            cat = lambda parts: jnp.concatenate(parts, axis=1)
            per_head_mean = lambda x: cat([jnp.broadcast_to(_lanemean(x[:, s]), (CHUNK, HEAD_DIM)) for s in _HEAD_LANES])
            t = _hgrn_gates(q_ref[rs, :], f_ref[rs, :], lb_all, tri_lo)
            v, gate, o, dog, nw_all = i_ref[rs, :], g_ref[rs, :], oraw_ref[rs, :], dog_ref[rs, :], nw_ref[...]
            rs_o = lax.rsqrt(per_head_mean(o * o) + RMS_EPS)
            xhat = o * rs_o
            sgg = _sigmoid(gate)
            d_on = dog * (gate * sgg)
            dz_ref[rs, 1536:2048] = (dog * (xhat * nw_all) * (sgg * (1.0 + gate * (1.0 - sgg)))).astype(BF16)
            dxh = d_on * nw_all
            do = rs_o * (dxh - xhat * per_head_mean(dxh * xhat))
            dsmall_ref[:, 512:1024] += _rowsum(d_on * xhat)
            st = [sp_ref[cc, h] for h in range(HEADS)]
            dst = [dst_ref[h] for h in range(HEADS)]
            a = [jnp.where(tri_lo, _bdot(t["qa"][:, s], t["ka"][:, s], NT), 0.0) for s in _HEAD_LANES]
            da = [jnp.where(tri_lo, _bdot(do[:, s], v[:, s], NT), 0.0) for s in _HEAD_LANES]
            dqb = cat([_bdot(do[:, s], st[h]) for h, s in heads])
            dkl = cat([_bdot(v[:, s], dst[h]) for h, s in heads])
            dv_ = cat([_bdot(t["kl"][:, s], dst[h], NT) + _bdot(a[h], do[:, s], TN) for h, s in heads])
            dqa = cat([_bdot(da[h], t["ka"][:, s]) for h, s in heads])
            dka = cat([_bdot(da[h], t["qa"][:, s], TN) for h, s in heads])
            ddecay = cat([_rowsum(dst[h] * st[h]) for h in range(HEADS)])
            for h, s in heads:
                dst_ref[h] = dst[h] * t["decay"][:, s] + _bdot(do[:, s], t["qb"][:, s], TN)
            pa, pk, pb, pl_ = dqa * t["qa"], dka * t["ka"], dqb * t["qb"], dkl * t["kl"]
            db = pa - pk + pb - pl_
            db = db + jnp.where(rowid == CHUNK // 2 - 1, _rowsum(pk - pa), 0.0)
            db = db + jnp.where(rowid == CHUNK - 1, _rowsum(pl_) + ddecay * t["decay"], 0.0)
            dlogf = _hdot(tri_up.astype(F32), db)
            dforget = dlogf / t["forget"] - (dka * t["e2"] + dkl * t["e3"])
            sg = t["sg"]
            dz_ref[rs, 0:512] = (dqa * t["e1"] + dqb * t["e4"]).astype(BF16)
            dz_ref[rs, 512:1024] = (dforget * (1.0 - lb_all) * sg * (1.0 - sg)).astype(BF16)
            dz_ref[rs, 1024:1536] = dv_.astype(BF16)
            dsmall_ref[:, 0:512] += _rowsum(dforget * (1.0 - sg))
            return carry

        lax.fori_loop(0, G, chunk, 0, unroll=4)

    col = lambda j: pl.BlockSpec((rows, 512), lambda r, j=j: (ng - 1 - r, j))
    return _call(
        body, "hgrn_bwd", (dmixcat, z, z, z, z, oraw, sprev, lbraw, nw), grid=(ng,),
        in_specs=[col(0), col(0), col(1), col(2), col(3), col(0),
                  pl.BlockSpec((G, HEADS, HEAD_DIM, HEAD_DIM), lambda r: (ng - 1 - r, 0, 0, 0)),
                  _full((2, 512)), _full((1, 512))],
        out_specs=[pl.BlockSpec((rows, 2048), lambda r: (ng - 1 - r, 0)), _full((1, 1024))],
        out_shape=[jax.ShapeDtypeStruct((T, 2048), BF16), jax.ShapeDtypeStruct((1, 1024), F32)],
        scratch_shapes=[pltpu.VMEM((HEADS, HEAD_DIM, HEAD_DIM), F32)], exchange=exchange)


def _diag_mask(t):
    r = lax.broadcasted_iota(jnp.int32, (t, t), 0)
    c = lax.broadcasted_iota(jnp.int32, (t, t), 1)
    return r >= c


def _attn_fwd(q, k, v, exchange=None):
    _, T, _ = q.shape
    t = min(ATT_TILE, T)

    def body(q_ref, k_ref, v_ref, o_ref, lse_ref):
        i = pl.program_id(1)
        qb = q_ref[...]

        rows = lambda j: pl.ds(pl.multiple_of(j * t, t), t)

        def logits(j, masked):
            s = _dot(qb, k_ref[rows(j), :], NT)
            return jnp.where(_diag_mask(t), s, NEG_BIG) if masked else s

        def absorb(s, j, carry):
            m, l, acc = carry
            mn = jnp.maximum(m, jnp.max(s, axis=-1, keepdims=True))
            p = jnp.exp2(s - mn)
            al = jnp.exp2(m - mn)
            return mn, al * l + jnp.sum(p, axis=-1, keepdims=True), al * acc + _dot(p.astype(BF16), v_ref[rows(j), :])

        def pair(j0, carry, last_masked):
            s0, s1 = logits(j0, False), logits(j0 + 1, last_masked)
            return absorb(s1, j0 + 1, absorb(s0, j0, carry))

        init = (jnp.full((t, 1), NEG_BIG, F32), jnp.zeros((t, 1), F32), jnp.zeros((t, HEAD_DIM), F32))
        carry = lax.fori_loop(0, i // 2, lambda jj, c: pair(2 * jj, c, False), init)
        m, l, acc = lax.cond(i % 2 == 1, lambda c: pair(i - 1, c, True),
                             lambda c: absorb(logits(i, True), i, c), carry)
        o_ref[...] = acc / l
        lse_ref[...] = jnp.broadcast_to(m + jnp.log2(l), (t, HEAD_DIM))

    return _call(
        body, "attn_fwd", (q, k, v), grid=(HEADS, T // t),
        in_specs=[pl.BlockSpec((None, t, QK_PAD), lambda h, i: (h, i, 0)),
                  pl.BlockSpec((None, T, QK_PAD), lambda h, i: (h, 0, 0)),
                  pl.BlockSpec((None, T, HEAD_DIM), lambda h, i: (h, 0, 0))],
        out_specs=[pl.BlockSpec((t, HEAD_DIM), lambda h, i: (i, h)),
                   pl.BlockSpec((None, t, HEAD_DIM), lambda h, i: (h, i, 0))],
        out_shape=[jax.ShapeDtypeStruct((T, HEADS * HEAD_DIM), F32), jax.ShapeDtypeStruct((HEADS, T, HEAD_DIM), F32)],
        exchange=exchange)


def _attn_bwd(q, k, v, dmixcat, o, lse, exchange=None):
    _, T, _ = q.shape
    t = min(ATT_TILE, T)
    nq = T // t

    def body(q_ref, k_ref, v_ref, do_ref, o_ref, lse_ref, dq_ref, dk_ref, dv_ref, delta_ref, dq_acc):
        j = pl.program_id(1)

        @pl.when(j == 0)
        def _():
            dq_acc[...] = jnp.zeros_like(dq_acc)

            def fill(i, carry):
                rs = pl.ds(pl.multiple_of(i * t, t), t)
                delta_ref[rs, :] = jnp.broadcast_to(
                    jnp.sum(do_ref[rs, :] * o_ref[rs, :], axis=-1, keepdims=True), (t, HEAD_DIM))
                return carry

            lax.fori_loop(0, nq, fill, 0)

        kb, vb = k_ref[...], v_ref[...]

        def steps(blocks, carry):
            dk, dv = carry
            rs = [pl.ds(pl.multiple_of(i * t, t), t) for i, _ in blocks]
            qb = [q_ref[r, :] for r in rs]
            dob = [do_ref[r, :].astype(BF16) for r in rs]
            s = [_dot(b, kb, NT) for b in qb]
            dp = [_dot(b, vb, NT) for b in dob]
            for n, (_, shift) in enumerate(blocks):
                p = jnp.exp2(s[n] - lse_ref[rs[n], 0:1])
                if shift is not None:
                    row = lax.broadcasted_iota(jnp.int32, (t, 2 * t), 0)
                    col = lax.broadcasted_iota(jnp.int32, (t, 2 * t), 1)
                    p = jnp.where(col <= row + shift, p, 0.0)
                ds = (p * (dp[n] - delta_ref[rs[n], 0:1])).astype(BF16)
                dq_acc[rs[n], :] += _dot(ds, kb)
                dk = dk + _dot(ds, qb[n], TN)
                dv = dv + _dot(p.astype(BF16), dob[n], TN)
            return dk, dv

        zero = (jnp.zeros((2 * t, QK_PAD), F32), jnp.zeros((2 * t, HEAD_DIM), F32))
        carry = steps([(2 * j, 0), (2 * j + 1, t)], zero)
        first = 2 * j + 2
        dk, dv = lax.fori_loop(0, (nq - first) // 2, lambda n, c: steps([(first + 2 * n, None), (first + 2 * n + 1, None)], c),
                               carry)
        dk_ref[...] = (dk * LN2).astype(BF16)
        dv_ref[...] = dv.astype(BF16)

        @pl.when(j == nk - 1)
        def _():
            dq_ref[...] = dq_acc[...].astype(BF16)

    nk = nq // 2
    return _call(
        body, "attn_bwd", (q, k, v, dmixcat, o, lse), grid=(HEADS, nk),
        in_specs=[pl.BlockSpec((None, T, QK_PAD), lambda h, j: (h, 0, 0)),
                  pl.BlockSpec((None, 2 * t, QK_PAD), lambda h, j: (h, j, 0)),
                  pl.BlockSpec((None, 2 * t, HEAD_DIM), lambda h, j: (h, j, 0)),
                  pl.BlockSpec((T, HEAD_DIM), lambda h, j: (0, HEADS + h)),
                  pl.BlockSpec((T, HEAD_DIM), lambda h, j: (0, h)),
                  pl.BlockSpec((None, T, HEAD_DIM), lambda h, j: (h, 0, 0))],
        out_specs=[pl.BlockSpec((None, T, QK_PAD), lambda h, j: (h, 0, 0)),
                   pl.BlockSpec((None, 2 * t, QK_PAD), lambda h, j: (h, j, 0)),
                   pl.BlockSpec((None, 2 * t, HEAD_DIM), lambda h, j: (h, j, 0))],
        out_shape=[jax.ShapeDtypeStruct((HEADS, T, QK_PAD), BF16), jax.ShapeDtypeStruct((HEADS, T, QK_PAD), BF16),
                   jax.ShapeDtypeStruct((HEADS, T, HEAD_DIM), BF16)],
        scratch_shapes=[pltpu.VMEM((T, HEAD_DIM), F32), pltpu.VMEM((T, QK_PAD), F32)], exchange=exchange)


def _ln_fwd(r):
    mu = _lanemean(r)
    xc = r - mu
    rstd = lax.rsqrt(_lanemean(xc * xc) + LN_EPS)
    return xc * rstd, rstd


def _ln_bwd(dxh, xhat, rstd):
    return rstd * (dxh - _lanemean(dxh) - xhat * _lanemean(dxh * xhat))


def _mix_ln1(o_hg, o_mla, w_out, x, g_a, ln1_g, ln1_b, sc_m, sh_m, exchange=None):
    T = x.shape[0]
    tm = min(ROW_TILE, T)
    half = o_hg.shape[1]

    def body(hg_ref, mla_ref, w_ref, x_ref, ga_ref, g_ref, b_ref, sc_ref, sh_ref, mix_ref, xhat_ref, rstd_ref, u2_ref):
        mix = _dot(hg_ref[...], w_ref[0:half, :]) + _bdot(mla_ref[...], w_ref[half:, :])
        mix_ref[...] = mix
        xhat, rstd = _ln_fwd(ALPHA * x_ref[...] + (1.0 + ga_ref[...]) * mix)
        xhat_ref[...] = xhat
        rstd_ref[...] = jnp.broadcast_to(rstd, (tm, 128))
        u2_ref[...] = _modulate(xhat * g_ref[...] + b_ref[...], sc_ref[...], sh_ref[...]).astype(BF16)

    row = pl.BlockSpec((tm, D_MODEL), lambda i: (i, 0))
    vec = _full((1, D_MODEL))
    halfrow = pl.BlockSpec((tm, half), lambda i: (i, 0))
    return _call(
        body, "mix_ln1", (o_hg, o_mla, w_out, x, g_a, ln1_g, ln1_b, sc_m, sh_m), grid=(T // tm,),
        in_specs=[halfrow, halfrow, _full(w_out.shape), row, vec, vec, vec, vec, vec],
        out_specs=[row, row, pl.BlockSpec((tm, 128), lambda i: (i, 0)), row],
        out_shape=[jax.ShapeDtypeStruct((T, D_MODEL), F32), jax.ShapeDtypeStruct((T, D_MODEL), F32),
                   jax.ShapeDtypeStruct((T, 128), F32), jax.ShapeDtypeStruct((T, D_MODEL), BF16)],
        exchange=exchange)


def _mlp_fwd(u2, w1, w2, xhat1, ln1_g, ln1_b, g_m, ln2_g, ln2_b, target):
    T = u2.shape[0]
    half, tf = w1[0].shape[1:]
    nf = N_DEV // MLP_SLABS
    tm = min(ROW_TILE, T)

    def body(u2_ref, w1a_ref, w1b_ref, w2_ref, xhat_ref, g1_ref, b1_ref, gm_ref, g2_ref, b2_ref, tgt_ref,
             r_ref, dr2_ref, dh_ref, small_ref, acc_ref):
        i, f = pl.program_id(0), pl.program_id(1)
        dm = D_MODEL

        @pl.when((i == 0) & (f == 0))
        def _():
            small_ref[...] = jnp.zeros_like(small_ref)

        @pl.when(f == 0)
        def _():
            acc_ref[...] = jnp.zeros_like(acc_ref)

        u2t = u2_ref[...]
        part = None
        for s in range(MLP_SLABS):
            r = jnp.maximum(_dot(u2t[:, :half], w1a_ref[s]) + _dot(u2t[:, half:], w1b_ref[s]), 0.0)
            r_ref[:, s * tf:(s + 1) * tf] = r.astype(BF16)
            d = _bdot(r * r, w2_ref[s])
            part = d if part is None else part + d
        acc_ref[...] += part

        @pl.when(f == nf - 1)
        def _():
            h = acc_ref[...]
            x1 = xhat_ref[...] * g1_ref[...] + b1_ref[...]
            xhat2, rstd2 = _ln_fwd(ALPHA * x1 + (1.0 + gm_ref[...]) * h)
            err = xhat2 * g2_ref[...] + b2_ref[...] - tgt_ref[...]
            small_ref[:, 3 * dm:] += jnp.sum(0.5 * _lanemean(err * err), axis=0, keepdims=True)
            dy = err * (1.0 / D_MODEL)
            small_ref[:, dm:2 * dm] += _rowsum(dy * xhat2)
            small_ref[:, 2 * dm:3 * dm] += _rowsum(dy)
            dr2 = _ln_bwd(dy * g2_ref[...], xhat2, rstd2)
            dr2_ref[...] = dr2
            small_ref[:, 0:dm] += _rowsum(dr2 * h)
            dh_ref[...] = ((1.0 + gm_ref[...]) * dr2).astype(BF16)

    row = pl.BlockSpec((tm, D_MODEL), lambda i, f: (i, 0))
    vec = _full((1, D_MODEL))
    return pl.pallas_call(
        body, name="mlp_fwd", grid=(T // tm, nf),
        in_specs=[row, pl.BlockSpec((MLP_SLABS, half, tf), lambda i, f: (f, 0, 0)),
                  pl.BlockSpec((MLP_SLABS, half, tf), lambda i, f: (f, 0, 0)),
                  pl.BlockSpec((MLP_SLABS, tf, D_MODEL), lambda i, f: (f, 0, 0)),
                  row, vec, vec, vec, vec, vec, row],
        out_specs=[pl.BlockSpec((tm, MLP_SLABS * tf), lambda i, f: (i, f)), row, row, _full((1, 3 * D_MODEL + 128))],
        out_shape=[jax.ShapeDtypeStruct((T, N_DEV * tf), BF16), jax.ShapeDtypeStruct((T, D_MODEL), F32),
                   jax.ShapeDtypeStruct((T, D_MODEL), BF16), jax.ShapeDtypeStruct((1, 3 * D_MODEL + 128), F32)],
        scratch_shapes=[pltpu.VMEM((tm, D_MODEL), F32)],
        compiler_params=_params(),
    )(u2, w1[0], w1[1], w2, xhat1, ln1_g, ln1_b, g_m, ln2_g, ln2_b, target)


def _mlp_bwd(dh, w1, w2, r, dr2, xhat1, rstd1, mix, ln1_g, ln1_b, sc_m, g_a):
    T = dh.shape[0]
    half, tf = w1[0].shape[1:]
    nf = N_DEV // MLP_SLABS
    tm = min(ROW_TILE, T)

    def body(dh_ref, w1a_ref, w1b_ref, w2_ref, r_ref, dr2_ref, xhat_ref, rstd_ref, mix_ref, g1_ref, b1_ref, sc_ref, ga_ref,
             dhpre_ref, dr1_ref, dmix_ref, small_ref, acc_ref):
        i, f = pl.program_id(0), pl.program_id(1)
        dm = D_MODEL

        @pl.when((i == 0) & (f == 0))
        def _():
            small_ref[...] = jnp.zeros_like(small_ref)

        @pl.when(f == 0)
        def _():
            acc_ref[...] = jnp.zeros_like(acc_ref)

        dht = dh_ref[...]
        part = None
        for s in range(MLP_SLABS):
            cols = slice(s * tf, (s + 1) * tf)
            dhpre = (_dot(dht, w2_ref[s], NT) * (2.0 * r_ref[:, cols].astype(F32))).astype(BF16)
            dhpre_ref[:, cols] = dhpre
            d = jnp.concatenate([_dot(dhpre, w1a_ref[s], NT), _dot(dhpre, w1b_ref[s], NT)], axis=1)
            part = d if part is None else part + d
        acc_ref[...] += part

        @pl.when(f == nf - 1)
        def _():
            du2 = acc_ref[...]
            xhat = xhat_ref[...]
            x1 = xhat * g1_ref[...] + b1_ref[...]
            dx1 = ALPHA * dr2_ref[...] + du2 * (1.0 + sc_ref[...])
            small_ref[:, 2 * dm:3 * dm] += _rowsum(du2 * x1)
            small_ref[:, dm:2 * dm] += _rowsum(du2)
            small_ref[:, 3 * dm:4 * dm] += _rowsum(dx1 * xhat)
            small_ref[:, 4 * dm:5 * dm] += _rowsum(dx1)
            dr1 = _ln_bwd(dx1 * g1_ref[...], xhat, rstd_ref[:, 0:1])
            dr1_ref[...] = dr1
            small_ref[:, 0:dm] += _rowsum(dr1 * mix_ref[...])
            dmix_ref[...] = ((1.0 + ga_ref[...]) * dr1).astype(BF16)

    row = pl.BlockSpec((tm, D_MODEL), lambda i, f: (i, 0))
    vec = _full((1, D_MODEL))
    return pl.pallas_call(
        body, name="mlp_bwd", grid=(T // tm, nf),
        in_specs=[row, pl.BlockSpec((MLP_SLABS, half, tf), lambda i, f: (f, 0, 0)),
                  pl.BlockSpec((MLP_SLABS, half, tf), lambda i, f: (f, 0, 0)),
                  pl.BlockSpec((MLP_SLABS, tf, D_MODEL), lambda i, f: (f, 0, 0)),
                  pl.BlockSpec((tm, MLP_SLABS * tf), lambda i, f: (i, f)), row, row,
                  pl.BlockSpec((tm, 128), lambda i, f: (i, 0)), row, vec, vec, vec, vec],
        out_specs=[pl.BlockSpec((tm, MLP_SLABS * tf), lambda i, f: (i, f)), row, row, _full((1, 5 * D_MODEL))],
        out_shape=[jax.ShapeDtypeStruct((T, N_DEV * tf), BF16), jax.ShapeDtypeStruct((T, D_MODEL), F32),
                   jax.ShapeDtypeStruct((T, D_MODEL), BF16), jax.ShapeDtypeStruct((1, 5 * D_MODEL), F32)],
        scratch_shapes=[pltpu.VMEM((tm, D_MODEL), F32)],
        compiler_params=_params(),
    )(dh, w1[0], w1[1], w2, r, dr2, xhat1, rstd1, mix, ln1_g, ln1_b, sc_m, g_a)


def _input_bwd(dz_h, dz_m, w_in_ext, x, dr1, sc_a, exchange=None):
    T = x.shape[0]
    tm = min(ROW_TILE, T)

    def body(dzh_ref, dzm_ref, w_ref, x_ref, dr1_ref, sc_ref, gx_ref, small_ref):
        @pl.when(pl.program_id(0) == 0)
        def _():
            small_ref[...] = jnp.zeros_like(small_ref)

        du = _bdot(dzh_ref[...], w_ref[0:2048, :]) + _bdot(dzm_ref[...], w_ref[2048:3072, :])
        gx_ref[...] = ALPHA * dr1_ref[...] + du * (1.0 + sc_ref[...])
        small_ref[:, D_MODEL:] += _rowsum(du * x_ref[...])
        small_ref[:, 0:D_MODEL] += _rowsum(du)

    row = pl.BlockSpec((tm, D_MODEL), lambda i: (i, 0))
    vec = _full((1, D_MODEL))
    return _call(
        body, "input_bwd", (dz_h, dz_m, w_in_ext, x, dr1, sc_a), grid=(T // tm,),
        in_specs=[pl.BlockSpec((tm, 2048), lambda i: (i, 0)), row, _full(w_in_ext.shape), row, row, vec],
        out_specs=[row, _full((1, 2 * D_MODEL))],
        out_shape=[jax.ShapeDtypeStruct((T, D_MODEL), F32), jax.ShapeDtypeStruct((1, 2 * D_MODEL), F32)],
        exchange=exchange)


def _adam_math(w, g, m, v):
    m = ADAM_B1 * m + (1.0 - ADAM_B1) * g
    v = ADAM_B2 * v + (1.0 - ADAM_B2) * (g * g)
    m_hat = m / (1.0 - ADAM_B1 ** ADAM_STEP)
    v_hat = v / (1.0 - ADAM_B2 ** ADAM_STEP)
    return -ADAM_LR * (m_hat / (jnp.sqrt(v_hat) + ADAM_EPS) + ADAM_WD * w), m, v


def _adam(g_slabs, w, m, v, name, g_fn=None, g_extra=()):
    R, C = w.shape
    tr = 256 if R % 256 == 0 else R
    ns = 0 if g_slabs is None else g_slabs.shape[0]
    slab_rows = tr if g_slabs is None or g_slabs.shape[1] == R else g_slabs.shape[1]
    assert slab_rows == tr or tr == R
    ne = len(g_extra)

    def body(*refs):
        e_refs = refs[:ne]
        refs = refs[ne:]
        if ns:
            gs_ref, refs = refs[0], refs[1:]
        w_ref, m_ref, v_ref, g_ref, d_ref, nm_ref, nv_ref = refs
        if g_fn is not None:
            g = g_fn(*e_refs)
        else:
            g = gs_ref[0].astype(F32)
            for s in range(1, ns):
                g = g + gs_ref[s].astype(F32)
            g = g[:tr]
        d, nm, nv = _adam_math(w_ref[...], g, m_ref[...], v_ref[...])
        g_ref[...] = g
        d_ref[...] = d
        nm_ref[...] = nm
        nv_ref[...] = nv

    blk = pl.BlockSpec((tr, C), lambda i: (i, 0))
    in_specs = [pl.BlockSpec((tr, e.shape[1]), lambda i: (i, 0)) if e.shape[0] == R else _full(e.shape) for e in g_extra]
    args = list(g_extra)
    if ns:
        in_specs.append(pl.BlockSpec((ns, slab_rows, C), lambda i: (0, i, 0)))
        args.append(g_slabs)
    return pl.pallas_call(
        body, name=name, grid=(R // tr,), in_specs=in_specs + [blk] * 3, out_specs=[blk] * 4,
        out_shape=[jax.ShapeDtypeStruct((R, C), F32)] * 4, compiler_params=_params(),
    )(*args, w, m, v)


def _adam_small(small_all, params):
    n = len(params)

    def body(*refs):
        s_ref, refs = refs[0], refs[1:]
        wmv, loss_ref, outs = refs[:3 * n], refs[3 * n], refs[3 * n + 1:]
        tot = s_ref[0]
        for i in range(1, N_DEV):
            tot = tot + s_ref[i]
        loss_ref[...] = tot[:, SMALL_W - 128:]
        for j, (w, _, _, off) in enumerate(params):
            w_ref, m_ref, v_ref = wmv[3 * j:3 * j + 3]
            g_ref, d_ref, nm_ref, nv_ref = outs[4 * j:4 * j + 4]
            if w.shape[0] == 2:
                lb = _lower_bound(w_ref)
                g0 = tot[:, off:off + w.shape[1]] * lb * (1.0 - lb)
                rows = [(slice(0, 1), g0), (slice(1, 2), -g0)]
            else:
                rows = [(slice(0, 1), tot[:, off:off + w.shape[1]])]
            for rs, g in rows:
                d, nm, nv = _adam_math(w_ref[rs, :], g, m_ref[rs, :], v_ref[rs, :])
                g_ref[rs, :], d_ref[rs, :], nm_ref[rs, :], nv_ref[rs, :] = g, d, nm, nv

    out_shape = [jax.ShapeDtypeStruct((1, 128), F32)]
    for w, _, _, _ in params:
        out_shape += [jax.ShapeDtypeStruct(w.shape, F32)] * 4
    res = pl.pallas_call(body, name="adam_small", out_shape=out_shape, compiler_params=_params())(
        small_all, *[a for w, m, v, _ in params for a in (w, m, v)])
    return res[0], [tuple(res[1 + 4 * j:5 + 4 * j]) for j in range(n)]


def _cols_from_slabs(g):
    s, r, c = g.shape
    return jnp.transpose(g, (1, 0, 2)).reshape(r, s * c)


def _slabs_from_cols(w):
    r, c = w.shape
    return jnp.transpose(w.reshape(r, N_DEV, c // N_DEV), (1, 0, 2))


def _rot_half_rows(wt):
    return jnp.concatenate([-wt[32:], wt[:32]], axis=0)


def _unrot_half_rows(dwt_rot):
    return jnp.concatenate([dwt_rot[32:], -dwt_rot[:32]], axis=0)


def _ext_in_t(g):
    n, rows, k_in = g.shape
    keep = n * rows - ROPE_DIM

    def body(g_ref, o_ref, stage_ref):
        stage_ref[keep:, :] = jnp.zeros((o_ref.shape[0] - keep, k_in), F32)
        for i in range(n - 1):
            stage_ref[rows * i:rows * (i + 1), :] = g_ref[i].astype(F32)
        last = g_ref[n - 1].astype(F32)
        stage_ref[rows * (n - 1):keep, :] = last[:rows - ROPE_DIM]
        wk = last[rows - ROPE_DIM:]
        stage_ref[keep + 128:keep + 192, :] = wk
        stage_ref[keep + 384:keep + 416, :] = -wk[32:]
        stage_ref[keep + 416:keep + 448, :] = wk[:32]
        o_ref[...] = stage_ref[...].astype(BF16)

    return pl.pallas_call(body, name="ext_w_in", out_shape=jax.ShapeDtypeStruct((keep + 512, k_in), BF16),
                          scratch_shapes=[pltpu.VMEM((keep + 512, k_in), F32)], compiler_params=_params())(g)


def _ext_q_t(wt):
    r = wt.shape[1]
    z64, z128 = jnp.zeros((64, r), BF16), jnp.zeros((128, r), BF16)
    per = HEAD_DIM + ROPE_DIM
    main = [jnp.concatenate([wt[per * h:per * (h + 1)], z64], axis=0) for h in range(HEADS)]
    rot = [jnp.concatenate([z128, _rot_half_rows(wt[per * h + HEAD_DIM:per * (h + 1)]), z64], axis=0)
           for h in range(HEADS)]
    return jnp.concatenate(main + rot, axis=0)


def _ext_kv(w_kv_up):
    r = w_kv_up.shape[0]
    z128 = jnp.zeros((r, 128), BF16)
    wkv = w_kv_up.reshape(r, HEADS, 2 * HEAD_DIM)
    kpad = [jnp.concatenate([wkv[:, h, :HEAD_DIM], z128], axis=1) for h in range(HEADS)]
    vals = [wkv[:, h, HEAD_DIM:] for h in range(HEADS)]
    return jnp.concatenate(kpad + vals, axis=1)


def _w_in_grad_slabs(dwt_h, dwt_m):
    d = dwt_h.shape[1]
    rows = (dwt_h.shape[0] + 512 + ROPE_DIM) // N_DEV
    padded = rows + (-rows) % 16

    def body(h_ref, m_ref, o_ref, stage_ref):
        stage_ref[0:2048, :] = h_ref[...]
        stage_ref[2048:2560, :] = m_ref[0:512, :]
        rot = m_ref[768 + 128:768 + 192, :]
        stage_ref[2560:2592, :] = m_ref[640:672, :] + rot[32:]
        stage_ref[2592:2624, :] = m_ref[672:704, :] - rot[:32]
        zero = jnp.zeros((padded - rows, d), F32)
        for i in range(N_DEV):
            o_ref[i] = jnp.concatenate([stage_ref[rows * i:rows * (i + 1), :], zero], axis=0).astype(BF16)

    return pl.pallas_call(body, name="w_in_grad_slabs", out_shape=jax.ShapeDtypeStruct((N_DEV, padded, d), BF16),
                          scratch_shapes=[pltpu.VMEM((N_DEV * rows, d), F32)], compiler_params=_params())(dwt_h, dwt_m)


def _grad_q_from_ext_t(dwq_ext_t):
    rows = []
    for h in range(HEADS):
        main, rot = dwq_ext_t[256 * h:256 * h + 256], dwq_ext_t[1024 + 256 * h:1280 + 256 * h]
        rows += [main[:128], main[128:192] + _unrot_half_rows(rot[128:192])]
    return jnp.concatenate(rows, axis=0)


def _grad_kv_from_ext(dwkv_ext):
    kvcols = []
    for h in range(HEADS):
        kvcols += [dwkv_ext[:, 256 * h:256 * h + 128], dwkv_ext[:, 1024 + 128 * h:1152 + 128 * h]]
    return jnp.concatenate(kvcols, axis=1)


SMALL_W = 6144 + 512 + 512 + 256 + 256 + 4 * 1024 + 128


def kernel(x, c, positions, w_ada, b_ada, w_in, hg_lower_bounds, hg_norm_w, mla_q_norm_w, w_q_up, mla_kv_norm_w, w_kv_up, w_out, ln1_g, ln1_b, w_mlp_in, w_mlp_out, ln2_g, ln2_b, loss_target, m_w_ada, m_b_ada, m_w_in, m_hg_lower_bounds, m_hg_norm_w, m_mla_q_norm_w, m_w_q_up, m_mla_kv_norm_w, m_w_kv_up, m_w_out, m_ln1_g, m_ln1_b, m_w_mlp_in, m_w_mlp_out, m_ln2_g, m_ln2_b, v_w_ada, v_b_ada, v_w_in, v_hg_lower_bounds, v_hg_norm_w, v_mla_q_norm_w, v_w_q_up, v_mla_kv_norm_w, v_w_kv_up, v_w_out, v_ln1_g, v_ln1_b, v_w_mlp_in, v_w_mlp_out, v_ln2_g, v_ln2_b):
    T = x.shape[1]
    me = 4 * lax.axis_index("x") + 2 * lax.axis_index("y") + lax.axis_index("c")
    xs, tgt = x[0], loss_target[0]
    transposed = ("w_in", "w_q_up")
    as_used = lambda n, a: a[0].T if n in transposed else a[0]
    big = {n: as_used(n, a) for n, a in dict(w_in=w_in, w_q_up=w_q_up, w_kv_up=w_kv_up, w_out=w_out,
                                              w_mlp_in=w_mlp_in, w_mlp_out=w_mlp_out).items()}
    names = list(big)

    bf = {n: big[n].astype(BF16) for n in ("w_in", "w_q_up", "w_kv_up", "w_out")}
    g_in, g_c = _gather_two_level([bf["w_in"], c], name="gather_w_in")
    c_all = g_c.reshape(N_DEV, D_MODEL)

    ada_cols = w_ada.shape[2]
    mod_part, cond = _mod_part(c_all, w_ada[0], lax.dynamic_slice(b_ada, (0, me * ada_cols), (1, ada_cols)))
    (mod_all,) = _exchange([mod_part], scatter=False, name="gather_mod")
    mod_row = lax.dynamic_slice(mod_all, (0, me, 0), (N_DEV, 1, ada_cols)).reshape(1, N_DEV * ada_cols)
    sh_a, sc_a, g_a, sh_m, sc_m, g_m = [mod_row[:, D_MODEL * i:D_MODEL * (i + 1)] for i in range(6)]

    w_in_ext = _ext_in_t(g_in)
    half = D_MODEL // 2
    z, (g_q, g_kv) = _matmul(xs, w_in_ext, "NT", "in_proj", a_fn=_modulate, extras=(sc_a, sh_a), tn=3072,
                             exchange=_Exchange([bf["w_q_up"], bf["w_kv_up"]], False))
    (o_raw, o_gated, s_prev), (g_out, w1_top) = _hgrn_fwd(
        z, hg_lower_bounds, hg_norm_w,
        exchange=[_Exchange([bf["w_out"]], False), _StagedGather(big["w_mlp_in"], rows=(0, half))])
    wq_ext = _ext_q_t(g_q.reshape(N_DEV * g_q.shape[1], g_q.shape[2]))
    wkv_ext = _ext_kv(_cols_from_slabs(g_kv))
    w_out_full = g_out.reshape(D_MODEL, D_MODEL)
    inv_freq = 1.0 / (ROPE_THETA ** (jnp.arange(0, ROPE_DIM, 2, dtype=F32) / ROPE_DIM))
    zeros = lambda n: jnp.zeros((n,), F32)
    invf = jnp.concatenate([zeros(128), inv_freq, inv_freq, zeros(64)]).reshape(1, QK_PAD)
    m_rot = jnp.concatenate([zeros(128), jnp.ones((64,), F32), zeros(64)]).reshape(1, QK_PAD)
    q, k, v, c1, s1, cqn, ckvn = _mla_pre(z, positions.reshape(T, 1), invf, m_rot, wq_ext, wkv_ext,
                                          mla_q_norm_w, mla_kv_norm_w)[0]
    (o_mla, lse), (w1_bot, w2) = _attn_fwd(
        q, k, v, exchange=[_StagedGather(big["w_mlp_in"], 0.85, rows=(half, half)),
                           _StagedGather(big["w_mlp_out"], 0.85)])
    w1 = (w1_top, w1_bot)
    mix, xhat1, rstd1, u2 = _mix_ln1(o_gated, o_mla, w_out_full, xs, g_a, ln1_g, ln1_b, sc_m, sh_m)[0]
    r, dr2, dh, small_mlp_fwd = _mlp_fwd(u2, w1, w2, xhat1, ln1_g, ln1_b, g_m, ln2_g, ln2_b, tgt)

    dhpre, dr1, dmix, small_mlp_bwd = _mlp_bwd(dh, w1, w2, r, dr2, xhat1, rstd1, mix, ln1_g, ln1_b, sc_m, g_a)
    received = {}
    dw2 = _matmul(r, dh, "TN", "wgrad_mlp_out", out_dtype=BF16, a_fn=_square, tm=1024, tk=2048)
    dw1 = _matmul(u2, dhpre, "TN", "wgrad_mlp_in", out_dtype=BF16, tm=1024, tk=2048, out_slabs=N_DEV)
    dmixcat = _matmul(dmix, w_out_full, "NT", "dgrad_out", tm=1024)
    dw_out = jnp.concatenate([_matmul(o_gated, dmix, "TN", "wgrad_out_hg", out_dtype=BF16, tk=2048),
                              _matmul(o_mla, dmix, "TN", "wgrad_out_mla", out_dtype=BF16, tk=2048)], axis=0)
    (dz_h, small_hgrn), (received["w_mlp_in"],) = _hgrn_bwd(
        dmixcat, z, o_raw, s_prev, hg_lower_bounds, hg_norm_w, exchange=_StagedScatter(dw1, 0.1))
    (dq, dk, dv), (received["w_mlp_out"], received["w_out"]) = _attn_bwd(
        q, k, v, dmixcat, o_mla, lse,
        exchange=_Exchange([dw2.reshape(N_DEV, dw2.shape[0] // N_DEV, D_MODEL),
                            dw_out.reshape(N_DEV, D_MODEL // N_DEV, D_MODEL)], True))
    dz_m, dq_ext, dkv_ext, small_mla = _mla_bwd(dq, dk, dv, z, c1, s1, wq_ext, wkv_ext, mla_q_norm_w, mla_kv_norm_w)
    dwq_t = _grad_q_from_ext_t(_matmul(dq_ext, cqn, "TN", "wgrad_q_up", tm=1024, tk=2048))
    dwkv = _grad_kv_from_ext(_matmul(ckvn, dkv_ext, "TN", "wgrad_kv_up", tn=1536, tk=2048))
    qkv_slabs = [dwq_t.reshape((N_DEV, dwq_t.shape[0] // N_DEV, dwq_t.shape[1])).astype(BF16),
                 _slabs_from_cols(dwkv).astype(BF16)]
    dwt_h, (received["w_q_up"], received["w_kv_up"]) = _matmul(
        dz_h, xs, "TN", "wgrad_in_h", b_fn=_modulate, extras=(sc_a, sh_a), tm=1024, tk=2048,
        exchange=_Exchange(qkv_slabs, True))
    dwt_m = _matmul(dz_m, xs, "TN", "wgrad_in_m", b_fn=_modulate, extras=(sc_a, sh_a), tm=1024, tk=2048)
    in_slabs = _w_in_grad_slabs(dwt_h, dwt_m)
    (grad_x, small_in), (received["w_in"],) = _input_bwd(
        dz_h, dz_m, w_in_ext, xs, dr1, sc_a, exchange=_StagedScatter(in_slabs))

    small = jnp.concatenate([small_in, small_mlp_bwd[:, :3 * D_MODEL], small_mlp_fwd[:, :D_MODEL], small_hgrn,
                             small_mla, small_mlp_bwd[:, 3 * D_MODEL:], small_mlp_fwd[:, D_MODEL:]], axis=1)
    assert small.shape == (1, SMALL_W)
    (small_all,) = _exchange([small], scatter=False, name="gather_small")

    moments = dict(w_in=(m_w_in, v_w_in), w_q_up=(m_w_q_up, v_w_q_up), w_kv_up=(m_w_kv_up, v_w_kv_up),
                   w_out=(m_w_out, v_w_out), w_mlp_in=(m_w_mlp_in, v_w_mlp_in), w_mlp_out=(m_w_mlp_out, v_w_mlp_out))
    res = {}
    for n in names:
        res[n] = _adam(received[n], big[n], as_used(n, moments[n][0]), as_used(n, moments[n][1]), name="adam_" + n)
    dmod_cols = lax.dynamic_slice(small_all.reshape(N_DEV, SMALL_W), (0, me * ada_cols), (N_DEV, ada_cols))
    cond_t = cond.T

    def ada_grad(ct_ref, dm_ref):
        g = ct_ref[:, 0:1] * dm_ref[0:1, :]
        for b in range(1, N_DEV):
            g = g + ct_ref[:, b:b + 1] * dm_ref[b:b + 1, :]
        return g

    res["w_ada"] = _adam(None, w_ada[0], m_w_ada[0], v_w_ada[0], name="adam_w_ada", g_fn=ada_grad,
                         g_extra=(cond_t, dmod_cols))

    small_params = [("b_ada", b_ada, m_b_ada, v_b_ada, 0),
                    ("hg_lower_bounds", hg_lower_bounds, m_hg_lower_bounds, v_hg_lower_bounds, 6144),
                    ("hg_norm_w", hg_norm_w, m_hg_norm_w, v_hg_norm_w, 6656),
                    ("mla_q_norm_w", mla_q_norm_w, m_mla_q_norm_w, v_mla_q_norm_w, 7168),
                    ("mla_kv_norm_w", mla_kv_norm_w, m_mla_kv_norm_w, v_mla_kv_norm_w, 7424),
                    ("ln1_g", ln1_g, m_ln1_g, v_ln1_g, 7680), ("ln1_b", ln1_b, m_ln1_b, v_ln1_b, 8704),
                    ("ln2_g", ln2_g, m_ln2_g, v_ln2_g, 9728), ("ln2_b", ln2_b, m_ln2_b, v_ln2_b, 10752)]
    loss_row, small_res = _adam_small(small_all, [p[1:] for p in small_params])
    for p, r4 in zip(small_params, small_res):
        res[p[0]] = r4
    loss = loss_row[0, 0]

    order = ["w_ada", "b_ada", "w_in", "hg_lower_bounds", "hg_norm_w", "mla_q_norm_w", "w_q_up", "mla_kv_norm_w",
             "w_kv_up", "w_out", "ln1_g", "ln1_b", "w_mlp_in", "w_mlp_out", "ln2_g", "ln2_b"]
    def as_given(n, a):
        if n in transposed:
            a = a.T
        return a[None] if n in big or n == "w_ada" else a

    shaped = {n: tuple(as_given(n, a) for a in res[n]) for n in order}
    outs = [loss, grad_x.reshape(1, T, D_MODEL)]
    for i in range(4):
        outs += [shaped[n][i] for n in order]
    return tuple(outs)
```

```python
import jax
import jax.numpy as jnp
import numpy as np
from jax import lax
from jax.experimental import pallas as pl
from jax.experimental.pallas import tpu as pltpu

F32, BF16 = jnp.float32, jnp.bfloat16
N_DEV = 8
D_MODEL = 1024
HEADS = 4
HEAD_DIM = 128
ROPE_DIM = 64
QK_PAD = 256
CHUNK = 64
ROPE_THETA = 10000.0
RMS_EPS = 1e-6
LN_EPS = 1e-5
ALPHA = 2.0 ** 0.25
ATT_SCALE = (HEAD_DIM + ROPE_DIM) ** -0.5
LN2 = float(np.log(2.0))
Q_PRESCALE = ATT_SCALE / LN2
ADAM_LR, ADAM_B1, ADAM_B2, ADAM_EPS, ADAM_WD, ADAM_STEP = 0.001, 0.9, 0.999, 1e-08, 0.01, 10
NEG_BIG = -1e30

ROW_TILE = 512
ATT_TILE = 512
HGRN_GROUP = 8
MLP_SLABS = 4
VMEM_LIMIT = 56 * 2 ** 20

NN = (((1,), (0,)), ((), ()))
NT = (((1,), (1,)), ((), ()))
TN = (((0,), (0,)), ((), ()))


def _dot(a, b, dims=NN):
    return lax.dot_general(a, b, dims, preferred_element_type=F32)


def _bdot(a, b, dims=NN):
    return lax.dot_general(a.astype(BF16), b.astype(BF16), dims, preferred_element_type=F32)


def _hdot(a, b, dims=NN):
    return lax.dot_general(a, b, dims, precision=lax.Precision.HIGHEST, preferred_element_type=F32)


def _params():
    return pltpu.CompilerParams(vmem_limit_bytes=VMEM_LIMIT)


def _sigmoid(x):
    return 1.0 / (1.0 + jnp.exp(-x))


def _rowsum(x):
    return jnp.sum(x, axis=0, keepdims=True)


def _lanemean(x):
    return jnp.mean(x, axis=-1, keepdims=True)


def _full(shape):
    nd = len(shape)
    return pl.BlockSpec(shape, lambda *_: (0,) * nd)


class _Exchange:
    def __init__(self, arrs, scatter):
        self.arrs, self.scatter, self.n, self.aliases, self.middle_at = list(arrs), scatter, len(arrs), [], 0.5
        self.out_shape = [jax.ShapeDtypeStruct((N_DEV,) + (a.shape[1:] if scatter else a.shape), a.dtype)
                          for a in self.arrs]
        n = self.n
        self.scratch = [pltpu.SemaphoreType.DMA((n, N_DEV - 1)), pltpu.SemaphoreType.DMA((n, N_DEV - 1)),
                        pltpu.SemaphoreType.DMA((n,))]

    def _copies(self, ins, outs, sems):
        send_sems, recv_sems, loc_sems = sems
        x, y, c = lax.axis_index("x"), lax.axis_index("y"), lax.axis_index("c")
        me = 4 * x + 2 * y + c
        copies = []
        for k in range(self.n):
            src_of = (lambda i, k=k: ins[k].at[i]) if self.scatter else (lambda i, k=k: ins[k])
            copies.append((pltpu.make_async_copy(src_of(me), outs[k].at[me], loc_sems.at[k]), None))
            for p in range(1, N_DEV):
                px = (1 - x) if p & 4 else x
                py = (1 - y) if p & 2 else y
                pc = (1 - c) if p & 1 else c
                peer = 4 * px + 2 * py + pc
                both = dict(send_sem=send_sems.at[k, p - 1], recv_sem=recv_sems.at[k, p - 1],
                            device_id=(px, py, pc), device_id_type=pl.DeviceIdType.MESH)
                send = pltpu.make_async_remote_copy(src_ref=src_of(peer), dst_ref=outs[k].at[me], **both)
                recv = pltpu.make_async_remote_copy(src_ref=src_of(peer), dst_ref=outs[k].at[peer], **both)
                copies.append((send, recv))
        return copies

    def start(self, ins, outs, sems):
        for first, _ in self._copies(ins, outs, sems):
            first.start()

    def middle(self, ins, outs, sems):
        pass

    def wait(self, ins, outs, sems):
        for first, recv in self._copies(ins, outs, sems):
            if recv is None:
                first.wait()
            else:
                recv.wait_recv()
                first.wait_send()


class _StagedGather:
    def __init__(self, arr, middle_at=0.8, rows=None):
        self.r0, n = rows if rows else (0, arr.shape[0])
        block = (n,) + arr.shape[1:]
        self.arrs, self.aliases, self.middle_at = [arr], [], middle_at
        self.out_shape = [jax.ShapeDtypeStruct((N_DEV,) + block, BF16)]
        self.scratch = [pltpu.VMEM((N_DEV,) + block, BF16), pltpu.SemaphoreType.DMA((7,)),
                        pltpu.SemaphoreType.DMA((7,)), pltpu.SemaphoreType.DMA((2,)), pltpu.VMEM(block, arr.dtype)]

    def _parts(self, scr):
        stage, send_sems, recv_sems, loc_sems = scr[:4]
        x, y, c = lax.axis_index("x"), lax.axis_index("y"), lax.axis_index("c")
        me, sibling = (x, y, c), (x, y, 1 - c)
        chips = [(1 - x, y), (x, 1 - y), (1 - x, 1 - y)]

        def copy(j, block, to):
            px, py, pc = block
            slot = stage.at[4 * px + 2 * py + pc]
            return pltpu.make_async_remote_copy(src_ref=slot, dst_ref=slot, send_sem=send_sems.at[j],
                                                recv_sem=recv_sems.at[j], device_id=to,
                                                device_id_type=pl.DeviceIdType.MESH)

        return stage, loc_sems, me, sibling, chips, c, copy

    def start(self, ins, outs, scr):
        stage, loc_sems, me, sibling, chips, c, copy = self._parts(scr)
        x, y, _ = me
        raw = scr[4]
        own = pltpu.make_async_copy(ins[0].at[pl.ds(self.r0, raw.shape[0])], raw, loc_sems.at[0])
        own.start()
        own.wait()
        stage[4 * x + 2 * y + c] = raw[...].astype(BF16)
        copy(0, me, sibling).start()
        for j, chip in enumerate(chips):
            copy(1 + j, me, (*chip, c)).start()

    def middle(self, ins, outs, scr):
        stage, loc_sems, me, sibling, chips, c, copy = self._parts(scr)
        for j, chip in enumerate(chips):
            copy(1 + j, (*chip, c), me).wait_recv()
            copy(4 + j, (*chip, c), sibling).start()

    def wait(self, ins, outs, scr):
        stage, loc_sems, me, sibling, chips, c, copy = self._parts(scr)
        copy(0, sibling, me).wait_recv()
        for j, chip in enumerate(chips):
            copy(4 + j, (*chip, 1 - c), me).wait_recv()
        copy(0, me, sibling).wait_send()
        for j, chip in enumerate(chips):
            copy(1 + j, me, (*chip, c)).wait_send()
            copy(4 + j, (*chip, c), sibling).wait_send()
        whole = pltpu.make_async_copy(stage, outs[0], loc_sems.at[1])
        whole.start()
        whole.wait()


class _StagedScatter:
    def __init__(self, slabs, middle_at=0.2):
        _, r, c = slabs.shape
        self.arrs, self.aliases, self.middle_at = [slabs], [], middle_at
        self.out_shape = [jax.ShapeDtypeStruct((4, r, c), slabs.dtype)]
        self.scratch = [pltpu.VMEM((N_DEV, r, c), slabs.dtype), pltpu.VMEM((4, r, c), slabs.dtype),
                        pltpu.VMEM((3, r, c), slabs.dtype), pltpu.SemaphoreType.DMA((4,)), pltpu.SemaphoreType.DMA((4,)),
                        pltpu.SemaphoreType.DMA((3,)), pltpu.SemaphoreType.DMA((3,)), pltpu.SemaphoreType.DMA((4,))]

    def _parts(self, scr):
        stage, from_sib, from_chips, sib_send, sib_recv, ici_send, ici_recv, loc_sems = scr
        x, y, c = lax.axis_index("x"), lax.axis_index("y"), lax.axis_index("c")
        chips = [(1 - x, y), (x, 1 - y), (1 - x, 1 - y)]

        def to_sibling(j):
            return pltpu.make_async_remote_copy(src_ref=stage.at[2 * j + 1 - c], dst_ref=from_sib.at[j],
                                                send_sem=sib_send.at[j], recv_sem=sib_recv.at[j],
                                                device_id=(x, y, 1 - c), device_id_type=pl.DeviceIdType.MESH)

        def to_chip(k):
            px, py = chips[k]
            return pltpu.make_async_remote_copy(src_ref=stage.at[4 * px + 2 * py + c], dst_ref=from_chips.at[k],
                                                send_sem=ici_send.at[k], recv_sem=ici_recv.at[k],
                                                device_id=(px, py, c), device_id_type=pl.DeviceIdType.MESH)

        return stage, from_sib, from_chips, loc_sems, (x, y, c), chips, to_sibling, to_chip

    def start(self, ins, outs, scr):
        stage, _, _, loc_sems, _, _, to_sibling, _ = self._parts(scr)
        load = pltpu.make_async_copy(ins[0], stage, loc_sems.at[0])
        load.start()
        load.wait()
        for j in range(4):
            to_sibling(j).start()

    def middle(self, ins, outs, scr):
        stage, from_sib, _, _, (x, y, c), _, to_sibling, to_chip = self._parts(scr)
        for j in range(4):
            to_sibling(j).wait_recv()
            mine = stage.at[2 * j + c]
            mine[...] = (mine[...].astype(F32) + from_sib[j].astype(F32)).astype(mine.dtype)
        for k in range(3):
            to_chip(k).start()

    def wait(self, ins, outs, scr):
        stage, _, from_chips, loc_sems, (x, y, c), chips, to_sibling, to_chip = self._parts(scr)
        writes = [pltpu.make_async_copy(stage.at[4 * x + 2 * y + c], outs[0].at[2 * x + y], loc_sems.at[0])]
        for k, (px, py) in enumerate(chips):
            to_chip(k).wait_recv()
            writes.append(pltpu.make_async_copy(from_chips.at[k], outs[0].at[2 * px + py], loc_sems.at[1 + k]))
        for w in writes:
            w.start()
        for j in range(4):
            to_sibling(j).wait_send()
        for k in range(3):
            to_chip(k).wait_send()
        for w in writes:
            w.wait()


def _call(body, name, args, out_shape, grid=(), in_specs=(), out_specs=(), scratch_shapes=(), exchange=None):
    if exchange is None:
        return pl.pallas_call(body, name=name, grid=grid, in_specs=list(in_specs), out_specs=list(out_specs),
                              out_shape=list(out_shape), scratch_shapes=list(scratch_shapes),
                              compiler_params=_params())(*args), None
    exs = list(exchange) if isinstance(exchange, (list, tuple)) else [exchange]
    ni, no, ns = len(args), len(out_shape), len(scratch_shapes)
    nxi, nxo = sum(len(e.arrs) for e in exs), sum(len(e.out_shape) for e in exs)
    steps = int(np.prod(grid))
    mid_step = lambda e: min(max(int(steps * e.middle_at), 1), steps - 1)
    aliases, iat, oat = {}, ni, no
    for e in exs:
        for src, dst in e.aliases:
            aliases[iat + src] = oat + dst
        iat, oat = iat + len(e.arrs), oat + len(e.out_shape)

    def wrapped(*refs):
        a, xi = refs[:ni], refs[ni:ni + nxi]
        o, xo = refs[ni + nxi:ni + nxi + no], refs[ni + nxi + no:ni + nxi + no + nxo]
        s, xs = refs[ni + nxi + no + nxo:ni + nxi + no + nxo + ns], refs[ni + nxi + no + nxo + ns:]
        parts, iat, oat, sat = [], 0, 0, 0
        for e in exs:
            parts.append((e, xi[iat:iat + len(e.arrs)], xo[oat:oat + len(e.out_shape)], xs[sat:sat + len(e.scratch)]))
            iat, oat, sat = iat + len(e.arrs), oat + len(e.out_shape), sat + len(e.scratch)
        step = 0
        for d, g in enumerate(grid):
            step = step * g + pl.program_id(d)

        @pl.when(step == 0)
        def _():
            for e, ins, outs, sems in parts:
                e.start(ins, outs, sems)

        for at_step in sorted({mid_step(e) for e in exs}):
            @pl.when(step == at_step)
            def _():
                for e, ins, outs, sems in parts:
                    if mid_step(e) == at_step:
                        e.middle(ins, outs, sems)

        body(*a, *o, *s)

        @pl.when(step == steps - 1)
        def _():
            for e, ins, outs, sems in parts:
                e.wait(ins, outs, sems)

    hbm = pl.BlockSpec(memory_space=pltpu.HBM)
    res = pl.pallas_call(
        wrapped, name=name, grid=grid, in_specs=list(in_specs) + [hbm] * nxi, out_specs=list(out_specs) + [hbm] * nxo,
        out_shape=list(out_shape) + [o_ for e in exs for o_ in e.out_shape],
        scratch_shapes=list(scratch_shapes) + [s_ for e in exs for s_ in e.scratch],
        input_output_aliases=aliases, compiler_params=_params())(*args, *[a_ for e in exs for a_ in e.arrs])
    return res[:no], res[no:]


def _gather_two_level(arrs, name):
    n = len(arrs)
    out_shape = [jax.ShapeDtypeStruct((N_DEV,) + a.shape, a.dtype) for a in arrs]

    def body(*refs):
        ins, outs = refs[:n], refs[n:2 * n]
        send_sems, recv_sems, loc_sems = refs[2 * n:]
        x, y, c = lax.axis_index("x"), lax.axis_index("y"), lax.axis_index("c")
        me, sibling = (x, y, c), (x, y, 1 - c)
        chips = [(1 - x, y), (x, 1 - y), (1 - x, 1 - y)]

        def copy(k, j, block, to, src=None):
            px, py, pc = block
            dst = outs[k].at[4 * px + 2 * py + pc]
            return pltpu.make_async_remote_copy(src_ref=dst if src is None else src, dst_ref=dst,
                                                send_sem=send_sems.at[k, j], recv_sem=recv_sems.at[k, j],
                                                device_id=to, device_id_type=pl.DeviceIdType.MESH)

        mine = [pltpu.make_async_copy(ins[k], outs[k].at[4 * x + 2 * y + c], loc_sems.at[k]) for k in range(n)]
        first = []
        for k in range(n):
            mine[k].start()
            first.append(copy(k, 0, me, sibling, src=ins[k]))
            first += [copy(k, 1 + j, me, (*chip, c), src=ins[k]) for j, chip in enumerate(chips)]
        for cp in first:
            cp.start()
        passed = []
        for j, chip in enumerate(chips):
            for k in range(n):
                copy(k, 1 + j, (*chip, c), me).wait_recv()
                passed.append(copy(k, 4 + j, (*chip, c), sibling))
                passed[-1].start()
        for k in range(n):
            copy(k, 0, sibling, me).wait_recv()
            for j, chip in enumerate(chips):
                copy(k, 4 + j, (*chip, 1 - c), me).wait_recv()
        for cp in first + passed:
            cp.wait_send()
        for cp in mine:
            cp.wait()

    vmem = pl.BlockSpec(memory_space=pltpu.VMEM)
    return pl.pallas_call(body, name=name, out_shape=out_shape, in_specs=[vmem] * n, out_specs=[vmem] * n,
                          scratch_shapes=[pltpu.SemaphoreType.DMA((n, 7)), pltpu.SemaphoreType.DMA((n, 7)),
                                          pltpu.SemaphoreType.DMA((n,))], compiler_params=_params())(*arrs)


def _exchange(arrs, scatter, name):
    ex = _Exchange(arrs, scatter)

    def body(*refs):
        ins, outs, sems = refs[:ex.n], refs[ex.n:2 * ex.n], refs[2 * ex.n:]
        ex.start(ins, outs, sems)
        ex.wait(ins, outs, sems)

    hbm = pl.BlockSpec(memory_space=pltpu.HBM)
    return pl.pallas_call(body, name=name, out_shape=ex.out_shape, in_specs=[hbm] * ex.n, out_specs=[hbm] * ex.n,
                          scratch_shapes=ex.scratch)(*ex.arrs)


def _matmul(a, b, mode, name, out_dtype=F32, tm=512, tn=1024, tk=1024, a_fn=None, b_fn=None, extras=(),
            out_slabs=None, exchange=None):
    assert not (a_fn and b_fn) and not (b_fn and mode == "NT")
    if mode == "NN":
        (M, K), N = a.shape, b.shape[1]
    elif mode == "NT":
        (M, K), N = a.shape, b.shape[0]
    else:
        (K, M), N = a.shape, b.shape[1]
    slab_w = N // out_slabs if out_slabs else None
    if out_slabs:
        tn = max(slab_w, min(tn, N) // slab_w * slab_w)
    tm, tn, tk = min(tm, M), min(tn, N), min(tk, K)
    assert M % tm == 0 and N % tn == 0 and K % tk == 0, (name, M, N, K)
    nk = K // tk
    dims = {"NN": NN, "NT": NT, "TN": TN}[mode]
    ne = len(extras)

    def body(a_ref, b_ref, *rest):
        e_refs, o_ref, acc_ref = rest[:ne], rest[ne], rest[ne + 1]
        k = pl.program_id(2)

        @pl.when(k == 0)
        def _():
            acc_ref[...] = jnp.zeros_like(acc_ref)

        at, bt = a_ref[...], b_ref[...]
        if a_fn is not None:
            at = a_fn(at.astype(F32), *[e[...] for e in e_refs])
        if b_fn is not None:
            bt = b_fn(bt.astype(F32), *[e[...] for e in e_refs])
        acc_ref[...] += _bdot(at, bt, dims)

        @pl.when(k == nk - 1)
        def _():
            if out_slabs:
                for s in range(tn // slab_w):
                    o_ref[s] = acc_ref[:, s * slab_w:(s + 1) * slab_w].astype(out_dtype)
            else:
                o_ref[...] = acc_ref[...].astype(out_dtype)

    if mode == "TN":
        a_spec = pl.BlockSpec((tk, tm), lambda i, j, k: (k, i))
        e_spec = pl.BlockSpec((1, tm), lambda i, j, k: (0, i))
    else:
        a_spec = pl.BlockSpec((tm, tk), lambda i, j, k: (i, k))
        e_spec = pl.BlockSpec((1, tk), lambda i, j, k: (0, k))
    if mode == "NT":
        b_spec = pl.BlockSpec((tn, tk), lambda i, j, k: (j, k))
    else:
        b_spec = pl.BlockSpec((tk, tn), lambda i, j, k: (k, j))
    if b_fn is not None:
        e_spec = pl.BlockSpec((1, tn), lambda i, j, k: (0, j))
    if out_slabs:
        o_shape = jax.ShapeDtypeStruct((out_slabs, M, slab_w), out_dtype)
        o_spec = pl.BlockSpec((tn // slab_w, tm, slab_w), lambda i, j, k: (j, i, 0))
    else:
        o_shape = jax.ShapeDtypeStruct((M, N), out_dtype)
        o_spec = pl.BlockSpec((tm, tn), lambda i, j, k: (i, j))
    (out,), got = _call(body, name, (a, b, *extras), [o_shape], grid=(M // tm, N // tn, nk),
                        in_specs=[a_spec, b_spec] + [e_spec] * ne, out_specs=[o_spec],
                        scratch_shapes=[pltpu.VMEM((tm, tn), F32)], exchange=exchange)
    return out if exchange is None else (out, got)


def _modulate(x, sc, sh):
    return x * (1.0 + sc) + sh


def _square(x):
    return x * x


def _mod_part(c_all, w_ada_s, b_s):
    def body(c_ref, w_ref, b_ref, mod_ref, cond_ref):
        cv = c_ref[...]
        cond = cv * _sigmoid(cv)
        cond_ref[...] = cond
        mod_ref[...] = _bdot(cond, w_ref[...]) + b_ref[...]

    return pl.pallas_call(
        body, name="mod_part",
        out_shape=[jax.ShapeDtypeStruct((N_DEV, w_ada_s.shape[1]), F32), jax.ShapeDtypeStruct(c_all.shape, F32)],
        compiler_params=_params(),
    )(c_all, w_ada_s, b_s)


def _rms_fwd(x, w):
    rs = lax.rsqrt(_lanemean(x * x) + RMS_EPS)
    return x * rs * w, rs


def _rms_bwd(x, rs, w, dy):
    xhat = x * rs
    dxh = dy * w
    return rs * (dxh - xhat * _lanemean(dxh * xhat)), dy * xhat


def _mla_pre(z, pos_col, invf, m_rot, wq_ext, wkv_ext, qnw, kvnw, exchange=None):
    T = z.shape[0]
    tm = min(ROW_TILE, T)

    def body(z_ref, pos_ref, invf_ref, mrot_ref, wq_ref, wkv_ref, qnw_ref, kvnw_ref,
             q_ref, k_ref, v_ref, c1_ref, s1_ref, cqn_ref, ckvn_ref):
        hi = slice(HEAD_DIM, QK_PAD)
        ang = pos_ref[...].astype(F32) * invf_ref[:, hi]
        c1 = jnp.concatenate([jnp.ones((tm, HEAD_DIM), F32), mrot_ref[:, hi] * jnp.cos(ang)], axis=1)
        s1 = jnp.concatenate([jnp.zeros((tm, HEAD_DIM), F32), mrot_ref[:, hi] * jnp.sin(ang)], axis=1)
        c1_ref[...] = c1
        s1_ref[...] = s1
        cqn, _ = _rms_fwd(z_ref[:, 0:256], qnw_ref[...])
        ckvn, _ = _rms_fwd(z_ref[:, 256:512], kvnw_ref[...])
        cqn_ref[...] = cqn.astype(BF16)
        ckvn_ref[...] = ckvn.astype(BF16)
        qe = _bdot(cqn, wq_ref[...], NT)
        kve = _bdot(ckvn, wkv_ref[...])
        k_rope = z_ref[:, 512:768] * c1 + z_ref[:, 768:1024] * s1
        for h in range(HEADS):
            q_ref[h] = ((qe[:, 256 * h:256 * h + 256] * c1 + qe[:, 1024 + 256 * h:1280 + 256 * h] * s1)
                        * Q_PRESCALE).astype(BF16)
            k_ref[h] = (kve[:, 256 * h:256 * h + 256] + k_rope).astype(BF16)
            v_ref[h] = kve[:, 1024 + 128 * h:1152 + 128 * h].astype(BF16)

    row = lambda i: (i, 0)
    head = lambda i: (0, i, 0)
    return _call(
        body, "mla_pre", (z, pos_col, invf, m_rot, wq_ext, wkv_ext, qnw, kvnw), grid=(T // tm,),
        in_specs=[pl.BlockSpec((tm, 1024), lambda i: (i, 2)), pl.BlockSpec((tm, 1), row),
                  _full((1, 256)), _full((1, 256)), _full(wq_ext.shape), _full(wkv_ext.shape),
                  _full((1, 256)), _full((1, 256))],
        out_specs=[pl.BlockSpec((HEADS, tm, QK_PAD), head), pl.BlockSpec((HEADS, tm, QK_PAD), head),
                   pl.BlockSpec((HEADS, tm, HEAD_DIM), head), pl.BlockSpec((tm, 256), row), pl.BlockSpec((tm, 256), row),
                   pl.BlockSpec((tm, 256), row), pl.BlockSpec((tm, 256), row)],
        out_shape=[jax.ShapeDtypeStruct((HEADS, T, QK_PAD), BF16), jax.ShapeDtypeStruct((HEADS, T, QK_PAD), BF16),
                   jax.ShapeDtypeStruct((HEADS, T, HEAD_DIM), BF16), jax.ShapeDtypeStruct((T, 256), F32),
                   jax.ShapeDtypeStruct((T, 256), F32), jax.ShapeDtypeStruct((T, 256), BF16),
                   jax.ShapeDtypeStruct((T, 256), BF16)], exchange=exchange)


def _mla_bwd(dq, dk, dv, z, c1, s1, wq_ext, wkv_ext, qnw, kvnw):
    T = z.shape[0]
    tm = min(ROW_TILE, T)

    def body(dq_ref, dk_ref, dv_ref, z_ref, c1_ref, s1_ref, wq_ref, wkv_ref, qnw_ref, kvnw_ref,
             dz_ref, dqe_ref, dkve_ref, dnw_ref):
        @pl.when(pl.program_id(0) == 0)
        def _():
            dnw_ref[...] = jnp.zeros_like(dnw_ref)

        c1, s1 = c1_ref[...], s1_ref[...]
        dkpe = jnp.zeros((tm, QK_PAD), F32)
        for h in range(HEADS):
            dqh, dkh = dq_ref[h].astype(F32) * ATT_SCALE, dk_ref[h]
            dqe_ref[:, 256 * h:256 * h + 256] = (dqh * c1).astype(BF16)
            dqe_ref[:, 1024 + 256 * h:1280 + 256 * h] = (dqh * s1).astype(BF16)
            dkve_ref[:, 256 * h:256 * h + 256] = dkh
            dkve_ref[:, 1024 + 128 * h:1152 + 128 * h] = dv_ref[h]
            dkpe = dkpe + dkh.astype(F32)
        dcqn = _dot(dqe_ref[...], wq_ref[...])
        dckvn = _dot(dkve_ref[...], wkv_ref[...], NT)
        cq, ckv = z_ref[:, 0:256], z_ref[:, 256:512]
        _, rsq = _rms_fwd(cq, qnw_ref[...])
        _, rskv = _rms_fwd(ckv, kvnw_ref[...])
        dcq, wq_rows = _rms_bwd(cq, rsq, qnw_ref[...], dcqn)
        dckv, wkv_rows = _rms_bwd(ckv, rskv, kvnw_ref[...], dckvn)
        dnw_ref[:, 0:256] += _rowsum(wq_rows)
        dnw_ref[:, 256:512] += _rowsum(wkv_rows)
        dz_ref[:, 0:256] = dcq.astype(BF16)
        dz_ref[:, 256:512] = dckv.astype(BF16)
        dz_ref[:, 512:768] = (dkpe * c1).astype(BF16)
        dz_ref[:, 768:1024] = (dkpe * s1).astype(BF16)

    row = lambda i: (i, 0)
    head = lambda i: (0, i, 0)
    return pl.pallas_call(
        body, name="mla_bwd", grid=(T // tm,),
        in_specs=[pl.BlockSpec((HEADS, tm, QK_PAD), head), pl.BlockSpec((HEADS, tm, QK_PAD), head),
                  pl.BlockSpec((HEADS, tm, HEAD_DIM), head), pl.BlockSpec((tm, 1024), lambda i: (i, 2)),
                  pl.BlockSpec((tm, 256), row), pl.BlockSpec((tm, 256), row), _full(wq_ext.shape), _full(wkv_ext.shape),
                  _full((1, 256)), _full((1, 256))],
        out_specs=[pl.BlockSpec((tm, 1024), row), pl.BlockSpec((tm, 2048), row), pl.BlockSpec((tm, 1536), row),
                   _full((1, 512))],
        out_shape=[jax.ShapeDtypeStruct((T, 1024), BF16), jax.ShapeDtypeStruct((T, 2048), BF16),
                   jax.ShapeDtypeStruct((T, 1536), BF16), jax.ShapeDtypeStruct((1, 512), F32)],
        compiler_params=_params(),
    )(dq, dk, dv, z, c1, s1, wq_ext, wkv_ext, qnw, kvnw)


_HEAD_LANES = [slice(HEAD_DIM * h, HEAD_DIM * (h + 1)) for h in range(HEADS)]


def _lower_bound(lbraw_ref):
    a0, a1 = lbraw_ref[0:1, :], lbraw_ref[1:2, :]
    mx = jnp.maximum(a0, a1)
    e0, e1 = jnp.exp(a0 - mx), jnp.exp(a1 - mx)
    return e0 / (e0 + e1)


def _tri(lower):
    r = lax.broadcasted_iota(jnp.int32, (CHUNK, CHUNK), 0)
    c = lax.broadcasted_iota(jnp.int32, (CHUNK, CHUNK), 1)
    return (r >= c) if lower else (r <= c)


def _hgrn_gates(q, f, lb, tri_lo):
    sg = _sigmoid(f)
    forget = lb + (1.0 - lb) * sg
    k = 1.0 - forget
    b = _hdot(tri_lo.astype(F32), jnp.log(forget))
    b_ref, b_last = b[CHUNK // 2 - 1:CHUNK // 2, :], b[CHUNK - 1:CHUNK, :]
    e1, e2, e3, e4 = jnp.exp(b - b_ref), jnp.exp(b_ref - b), jnp.exp(b_last - b), jnp.exp(b)
    return dict(sg=sg, forget=forget, k=k, e1=e1, e2=e2, e3=e3, e4=e4, qa=q * e1, ka=k * e2, kl=k * e3, qb=q * e4,
                decay=jnp.exp(b_last))


def _hgrn_fwd(z, lbraw, nw, exchange=None):
    T = z.shape[0]
    G = min(HGRN_GROUP, T // CHUNK)
    rows = G * CHUNK
    n_chunks = T // CHUNK

    def body(q_ref, f_ref, i_ref, g_ref, lbraw_ref, nw_ref, oraw_ref, og_ref, sp_ref, st_ref):
        @pl.when(pl.program_id(0) == 0)
        def _():
            st_ref[...] = jnp.zeros_like(st_ref)

        lb_all = _lower_bound(lbraw_ref)
        tri_lo = _tri(True)

        def chunk(cc, carry):
            rs = pl.ds(pl.multiple_of(cc * CHUNK, CHUNK), CHUNK)
            t = _hgrn_gates(q_ref[rs, :], f_ref[rs, :], lb_all, tri_lo)
            v, gate = i_ref[rs, :], g_ref[rs, :]
            st = [st_ref[h] for h in range(HEADS)]
            a = [jnp.where(tri_lo, _bdot(t["qa"][:, s], t["ka"][:, s], NT), 0.0) for s in _HEAD_LANES]
            kv = [_bdot(v[:, s], t["kl"][:, s], TN) for s in _HEAD_LANES]
            o = [_bdot(a[h], v[:, s]) + _bdot(t["qb"][:, s], st[h], NT) for h, s in enumerate(_HEAD_LANES)]
            for h, s in enumerate(_HEAD_LANES):
                sp_ref[cc, h] = st[h]
                st_ref[h] = st[h] * t["decay"][:, s] + kv[h]
            oraw_ref[rs, :] = jnp.concatenate(o, axis=1)
            on = jnp.concatenate([_rms_fwd(o[h], nw_ref[:, s])[0] for h, s in enumerate(_HEAD_LANES)], axis=1)
            og_ref[rs, :] = (on * (gate * _sigmoid(gate))).astype(BF16)
            return carry

        lax.fori_loop(0, G, chunk, 0, unroll=4)

    col = lambda j: pl.BlockSpec((rows, 512), lambda r, j=j: (r, j))
    return _call(
        body, "hgrn_fwd", (z, z, z, z, lbraw, nw), grid=(T // rows,),
        in_specs=[col(0), col(1), col(2), col(3), _full((2, 512)), _full((1, 512))],
        out_specs=[col(0), col(0), pl.BlockSpec((G, HEADS, HEAD_DIM, HEAD_DIM), lambda r: (r, 0, 0, 0))],
        out_shape=[jax.ShapeDtypeStruct((T, 512), F32), jax.ShapeDtypeStruct((T, 512), BF16),
                   jax.ShapeDtypeStruct((n_chunks, HEADS, HEAD_DIM, HEAD_DIM), F32)],
        scratch_shapes=[pltpu.VMEM((HEADS, HEAD_DIM, HEAD_DIM), F32)], exchange=exchange)


def _hgrn_bwd(dmixcat, z, oraw, sprev, lbraw, nw, exchange=None):
    T = z.shape[0]
    G = min(HGRN_GROUP, T // CHUNK)
    rows = G * CHUNK
    ng = T // rows

    def body(dog_ref, q_ref, f_ref, i_ref, g_ref, oraw_ref, sp_ref, lbraw_ref, nw_ref,
             dz_ref, dsmall_ref, dst_ref):
        @pl.when(pl.program_id(0) == 0)
        def _():
            dst_ref[...] = jnp.zeros_like(dst_ref)
            dsmall_ref[...] = jnp.zeros_like(dsmall_ref)

        lb_all = _lower_bound(lbraw_ref)
        tri_lo, tri_up = _tri(True), _tri(False)
        rowid = lax.broadcasted_iota(jnp.int32, (CHUNK, HEADS * HEAD_DIM), 0)

        def chunk(it, carry):
            cc = G - 1 - it
            rs = pl.ds(pl.multiple_of(cc * CHUNK, CHUNK), CHUNK)
            heads = list(enumerate(_HEAD_LANES))
            cat = lambda parts: jnp.concatenate(parts, axis=1)
            per_head_mean = lambda x: cat([jnp.broadcast_to(_lanemean(x[:, s]), (CHUNK, HEAD_DIM)) for s in _HEAD_LANES])
            t = _hgrn_gates(q_ref[rs, :], f_ref[rs, :], lb_all, tri_lo)
            v, gate, o, dog, nw_all = i_ref[rs, :], g_ref[rs, :], oraw_ref[rs, :], dog_ref[rs, :], nw_ref[...]
            rs_o = lax.rsqrt(per_head_mean(o * o) + RMS_EPS)
            xhat = o * rs_o
            sgg = _sigmoid(gate)
            d_on = dog * (gate * sgg)
            dz_ref[rs, 1536:2048] = (dog * (xhat * nw_all) * (sgg * (1.0 + gate * (1.0 - sgg)))).astype(BF16)
            dxh = d_on * nw_all
            do = rs_o * (dxh - xhat * per_head_mean(dxh * xhat))
            dsmall_ref[:, 512:1024] += _rowsum(d_on * xhat)
            st = [sp_ref[cc, h] for h in range(HEADS)]
            dst = [dst_ref[h] for h in range(HEADS)]
            a = [jnp.where(tri_lo, _bdot(t["qa"][:, s], t["ka"][:, s], NT), 0.0) for s in _HEAD_LANES]
            da = [jnp.where(tri_lo, _bdot(do[:, s], v[:, s], NT), 0.0) for s in _HEAD_LANES]
            dqb = cat([_bdot(do[:, s], st[h]) for h, s in heads])
            dkl = cat([_bdot(v[:, s], dst[h]) for h, s in heads])
            dv_ = cat([_bdot(t["kl"][:, s], dst[h], NT) + _bdot(a[h], do[:, s], TN) for h, s in heads])
            dqa = cat([_bdot(da[h], t["ka"][:, s]) for h, s in heads])
            dka = cat([_bdot(da[h], t["qa"][:, s], TN) for h, s in heads])
            ddecay = cat([_rowsum(dst[h] * st[h]) for h in range(HEADS)])
            for h, s in heads:
                dst_ref[h] = dst[h] * t["decay"][:, s] + _bdot(do[:, s], t["qb"][:, s], TN)
            pa, pk, pb, pl_ = dqa * t["qa"], dka * t["ka"], dqb * t["qb"], dkl * t["kl"]
            db = pa - pk + pb - pl_
            db = db + jnp.where(rowid == CHUNK // 2 - 1, _rowsum(pk - pa), 0.0)
            db = db + jnp.where(rowid == CHUNK - 1, _rowsum(pl_) + ddecay * t["decay"], 0.0)
            dlogf = _hdot(tri_up.astype(F32), db)
            dforget = dlogf / t["forget"] - (dka * t["e2"] + dkl * t["e3"])
            sg = t["sg"]
            dz_ref[rs, 0:512] = (dqa * t["e1"] + dqb * t["e4"]).astype(BF16)
            dz_ref[rs, 512:1024] = (dforget * (1.0 - lb_all) * sg * (1.0 - sg)).astype(BF16)
            dz_ref[rs, 1024:1536] = dv_.astype(BF16)
            dsmall_ref[:, 0:512] += _rowsum(dforget * (1.0 - sg))
            return carry

        lax.fori_loop(0, G, chunk, 0, unroll=4)

    col = lambda j: pl.BlockSpec((rows, 512), lambda r, j=j: (ng - 1 - r, j))
    return _call(
        body, "hgrn_bwd", (dmixcat, z, z, z, z, oraw, sprev, lbraw, nw), grid=(ng,),
        in_specs=[col(0), col(0), col(1), col(2), col(3), col(0),
                  pl.BlockSpec((G, HEADS, HEAD_DIM, HEAD_DIM), lambda r: (ng - 1 - r, 0, 0, 0)),
                  _full((2, 512)), _full((1, 512))],
        out_specs=[pl.BlockSpec((rows, 2048), lambda r: (ng - 1 - r, 0)), _full((1, 1024))],
        out_shape=[jax.ShapeDtypeStruct((T, 2048), BF16), jax.ShapeDtypeStruct((1, 1024), F32)],
        scratch_shapes=[pltpu.VMEM((HEADS, HEAD_DIM, HEAD_DIM), F32)], exchange=exchange)


def _diag_mask(t):
    r = lax.broadcasted_iota(jnp.int32, (t, t), 0)
    c = lax.broadcasted_iota(jnp.int32, (t, t), 1)
    return r >= c


def _attn_fwd(q, k, v, exchange=None):
    _, T, _ = q.shape
    t = min(ATT_TILE, T)

    def body(q_ref, k_ref, v_ref, o_ref, lse_ref):
        i = pl.program_id(1)
        qb = q_ref[...]

        rows = lambda j: pl.ds(pl.multiple_of(j * t, t), t)

        def logits(j, masked):
            s = _dot(qb, k_ref[rows(j), :], NT)
            return jnp.where(_diag_mask(t), s, NEG_BIG) if masked else s

        def absorb(s, j, carry):
            m, l, acc = carry
            mn = jnp.maximum(m, jnp.max(s, axis=-1, keepdims=True))
            p = jnp.exp2(s - mn)
            al = jnp.exp2(m - mn)
            return mn, al * l + jnp.sum(p, axis=-1, keepdims=True), al * acc + _dot(p.astype(BF16), v_ref[rows(j), :])

        def pair(j0, carry, last_masked):
            s0, s1 = logits(j0, False), logits(j0 + 1, last_masked)
            return absorb(s1, j0 + 1, absorb(s0, j0, carry))

        init = (jnp.full((t, 1), NEG_BIG, F32), jnp.zeros((t, 1), F32), jnp.zeros((t, HEAD_DIM), F32))
        carry = lax.fori_loop(0, i // 2, lambda jj, c: pair(2 * jj, c, False), init)
        m, l, acc = lax.cond(i % 2 == 1, lambda c: pair(i - 1, c, True),
                             lambda c: absorb(logits(i, True), i, c), carry)
        o_ref[...] = acc / l
        lse_ref[...] = jnp.broadcast_to(m + jnp.log2(l), (t, HEAD_DIM))

    return _call(
        body, "attn_fwd", (q, k, v), grid=(HEADS, T // t),
        in_specs=[pl.BlockSpec((None, t, QK_PAD), lambda h, i: (h, i, 0)),
                  pl.BlockSpec((None, T, QK_PAD), lambda h, i: (h, 0, 0)),
                  pl.BlockSpec((None, T, HEAD_DIM), lambda h, i: (h, 0, 0))],
        out_specs=[pl.BlockSpec((t, HEAD_DIM), lambda h, i: (i, h)),
                   pl.BlockSpec((None, t, HEAD_DIM), lambda h, i: (h, i, 0))],
        out_shape=[jax.ShapeDtypeStruct((T, HEADS * HEAD_DIM), F32), jax.ShapeDtypeStruct((HEADS, T, HEAD_DIM), F32)],
        exchange=exchange)


def _attn_bwd(q, k, v, dmixcat, o, lse, exchange=None):
    _, T, _ = q.shape
    t = min(ATT_TILE, T)
    nq = T // t

    def body(q_ref, k_ref, v_ref, do_ref, o_ref, lse_ref, dq_ref, dk_ref, dv_ref, delta_ref, dq_acc):
        j = pl.program_id(1)

        @pl.when(j == 0)
        def _():
            dq_acc[...] = jnp.zeros_like(dq_acc)

            def fill(i, carry):
                rs = pl.ds(pl.multiple_of(i * t, t), t)
                delta_ref[rs, :] = jnp.broadcast_to(
                    jnp.sum(do_ref[rs, :] * o_ref[rs, :], axis=-1, keepdims=True), (t, HEAD_DIM))
                return carry

            lax.fori_loop(0, nq, fill, 0)

        kb, vb = k_ref[...], v_ref[...]

        def steps(blocks, carry):
            dk, dv = carry
            rs = [pl.ds(pl.multiple_of(i * t, t), t) for i, _ in blocks]
            qb = [q_ref[r, :] for r in rs]
            dob = [do_ref[r, :].astype(BF16) for r in rs]
            s = [_dot(b, kb, NT) for b in qb]
            dp = [_dot(b, vb, NT) for b in dob]
            for n, (_, shift) in enumerate(blocks):
                p = jnp.exp2(s[n] - lse_ref[rs[n], 0:1])
                if shift is not None:
                    row = lax.broadcasted_iota(jnp.int32, (t, 2 * t), 0)
                    col = lax.broadcasted_iota(jnp.int32, (t, 2 * t), 1)
                    p = jnp.where(col <= row + shift, p, 0.0)
                ds = (p * (dp[n] - delta_ref[rs[n], 0:1])).astype(BF16)
                dq_acc[rs[n], :] += _dot(ds, kb)
                dk = dk + _dot(ds, qb[n], TN)
                dv = dv + _dot(p.astype(BF16), dob[n], TN)
            return dk, dv

        zero = (jnp.zeros((2 * t, QK_PAD), F32), jnp.zeros((2 * t, HEAD_DIM), F32))
        carry = steps([(2 * j, 0), (2 * j + 1, t)], zero)
        first = 2 * j + 2
        dk, dv = lax.fori_loop(0, (nq - first) // 2, lambda n, c: steps([(first + 2 * n, None), (first + 2 * n + 1, None)], c),
                               carry)
        dk_ref[...] = (dk * LN2).astype(BF16)
        dv_ref[...] = dv.astype(BF16)

        @pl.when(j == nk - 1)
        def _():
            dq_ref[...] = dq_acc[...].astype(BF16)

    nk = nq // 2
    return _call(
        body, "attn_bwd", (q, k, v, dmixcat, o, lse), grid=(HEADS, nk),
        in_specs=[pl.BlockSpec((None, T, QK_PAD), lambda h, j: (h, 0, 0)),
                  pl.BlockSpec((None, 2 * t, QK_PAD), lambda h, j: (h, j, 0)),
                  pl.BlockSpec((None, 2 * t, HEAD_DIM), lambda h, j: (h, j, 0)),
                  pl.BlockSpec((T, HEAD_DIM), lambda h, j: (0, HEADS + h)),
                  pl.BlockSpec((T, HEAD_DIM), lambda h, j: (0, h)),
                  pl.BlockSpec((None, T, HEAD_DIM), lambda h, j: (h, 0, 0))],
        out_specs=[pl.BlockSpec((None, T, QK_PAD), lambda h, j: (h, 0, 0)),
                   pl.BlockSpec((None, 2 * t, QK_PAD), lambda h, j: (h, j, 0)),
                   pl.BlockSpec((None, 2 * t, HEAD_DIM), lambda h, j: (h, j, 0))],
        out_shape=[jax.ShapeDtypeStruct((HEADS, T, QK_PAD), BF16), jax.ShapeDtypeStruct((HEADS, T, QK_PAD), BF16),
                   jax.ShapeDtypeStruct((HEADS, T, HEAD_DIM), BF16)],
        scratch_shapes=[pltpu.VMEM((T, HEAD_DIM), F32), pltpu.VMEM((T, QK_PAD), F32)], exchange=exchange)


def _ln_fwd(r):
    mu = _lanemean(r)
    xc = r - mu
    rstd = lax.rsqrt(_lanemean(xc * xc) + LN_EPS)
    return xc * rstd, rstd


def _ln_bwd(dxh, xhat, rstd):
    return rstd * (dxh - _lanemean(dxh) - xhat * _lanemean(dxh * xhat))


def _mix_ln1(o_hg, o_mla, w_out, x, g_a, ln1_g, ln1_b, sc_m, sh_m, exchange=None):
    T = x.shape[0]
    tm = min(ROW_TILE, T)
    half = o_hg.shape[1]

    def body(hg_ref, mla_ref, w_ref, x_ref, ga_ref, g_ref, b_ref, sc_ref, sh_ref, mix_ref, xhat_ref, rstd_ref, u2_ref):
        mix = _dot(hg_ref[...], w_ref[0:half, :]) + _bdot(mla_ref[...], w_ref[half:, :])
        mix_ref[...] = mix
        xhat, rstd = _ln_fwd(ALPHA * x_ref[...] + (1.0 + ga_ref[...]) * mix)
        xhat_ref[...] = xhat
        rstd_ref[...] = jnp.broadcast_to(rstd, (tm, 128))
        u2_ref[...] = _modulate(xhat * g_ref[...] + b_ref[...], sc_ref[...], sh_ref[...]).astype(BF16)

    row = pl.BlockSpec((tm, D_MODEL), lambda i: (i, 0))
    vec = _full((1, D_MODEL))
    halfrow = pl.BlockSpec((tm, half), lambda i: (i, 0))
    return _call(
        body, "mix_ln1", (o_hg, o_mla, w_out, x, g_a, ln1_g, ln1_b, sc_m, sh_m), grid=(T // tm,),
        in_specs=[halfrow, halfrow, _full(w_out.shape), row, vec, vec, vec, vec, vec],
        out_specs=[row, row, pl.BlockSpec((tm, 128), lambda i: (i, 0)), row],
        out_shape=[jax.ShapeDtypeStruct((T, D_MODEL), F32), jax.ShapeDtypeStruct((T, D_MODEL), F32),
                   jax.ShapeDtypeStruct((T, 128), F32), jax.ShapeDtypeStruct((T, D_MODEL), BF16)],
        exchange=exchange)


def _mlp_fwd(u2, w1, w2, xhat1, ln1_g, ln1_b, g_m, ln2_g, ln2_b, target):
    T = u2.shape[0]
    half, tf = w1[0].shape[1:]
    nf = N_DEV // MLP_SLABS
    tm = min(ROW_TILE, T)

    def body(u2_ref, w1a_ref, w1b_ref, w2_ref, xhat_ref, g1_ref, b1_ref, gm_ref, g2_ref, b2_ref, tgt_ref,
             r_ref, dr2_ref, dh_ref, small_ref, acc_ref):
        i, f = pl.program_id(0), pl.program_id(1)
        dm = D_MODEL

        @pl.when((i == 0) & (f == 0))
        def _():
            small_ref[...] = jnp.zeros_like(small_ref)

        @pl.when(f == 0)
        def _():
            acc_ref[...] = jnp.zeros_like(acc_ref)

        u2t = u2_ref[...]
        part = None
        for s in range(MLP_SLABS):
            r = jnp.maximum(_dot(u2t[:, :half], w1a_ref[s]) + _dot(u2t[:, half:], w1b_ref[s]), 0.0)
            r_ref[:, s * tf:(s + 1) * tf] = r.astype(BF16)
            d = _bdot(r * r, w2_ref[s])
            part = d if part is None else part + d
        acc_ref[...] += part

        @pl.when(f == nf - 1)
        def _():
            h = acc_ref[...]
            x1 = xhat_ref[...] * g1_ref[...] + b1_ref[...]
            xhat2, rstd2 = _ln_fwd(ALPHA * x1 + (1.0 + gm_ref[...]) * h)
            err = xhat2 * g2_ref[...] + b2_ref[...] - tgt_ref[...]
            small_ref[:, 3 * dm:] += jnp.sum(0.5 * _lanemean(err * err), axis=0, keepdims=True)
            dy = err * (1.0 / D_MODEL)
            small_ref[:, dm:2 * dm] += _rowsum(dy * xhat2)
            small_ref[:, 2 * dm:3 * dm] += _rowsum(dy)
            dr2 = _ln_bwd(dy * g2_ref[...], xhat2, rstd2)
            dr2_ref[...] = dr2
            small_ref[:, 0:dm] += _rowsum(dr2 * h)
            dh_ref[...] = ((1.0 + gm_ref[...]) * dr2).astype(BF16)

    row = pl.BlockSpec((tm, D_MODEL), lambda i, f: (i, 0))
    vec = _full((1, D_MODEL))
    return pl.pallas_call(
        body, name="mlp_fwd", grid=(T // tm, nf),
        in_specs=[row, pl.BlockSpec((MLP_SLABS, half, tf), lambda i, f: (f, 0, 0)),
                  pl.BlockSpec((MLP_SLABS, half, tf), lambda i, f: (f, 0, 0)),
                  pl.BlockSpec((MLP_SLABS, tf, D_MODEL), lambda i, f: (f, 0, 0)),
                  row, vec, vec, vec, vec, vec, row],
        out_specs=[pl.BlockSpec((tm, MLP_SLABS * tf), lambda i, f: (i, f)), row, row, _full((1, 3 * D_MODEL + 128))],
        out_shape=[jax.ShapeDtypeStruct((T, N_DEV * tf), BF16), jax.ShapeDtypeStruct((T, D_MODEL), F32),
                   jax.ShapeDtypeStruct((T, D_MODEL), BF16), jax.ShapeDtypeStruct((1, 3 * D_MODEL + 128), F32)],
        scratch_shapes=[pltpu.VMEM((tm, D_MODEL), F32)],
        compiler_params=_params(),
    )(u2, w1[0], w1[1], w2, xhat1, ln1_g, ln1_b, g_m, ln2_g, ln2_b, target)


def _mlp_bwd(dh, w1, w2, r, dr2, xhat1, rstd1, mix, ln1_g, ln1_b, sc_m, g_a):
    T = dh.shape[0]
    half, tf = w1[0].shape[1:]
    nf = N_DEV // MLP_SLABS
    tm = min(ROW_TILE, T)

    def body(dh_ref, w1a_ref, w1b_ref, w2_ref, r_ref, dr2_ref, xhat_ref, rstd_ref, mix_ref, g1_ref, b1_ref, sc_ref, ga_ref,
             dhpre_ref, dr1_ref, dmix_ref, small_ref, acc_ref):
        i, f = pl.program_id(0), pl.program_id(1)
        dm = D_MODEL

        @pl.when((i == 0) & (f == 0))
        def _():
            small_ref[...] = jnp.zeros_like(small_ref)

        @pl.when(f == 0)
        def _():
            acc_ref[...] = jnp.zeros_like(acc_ref)

        dht = dh_ref[...]
        part = None
        for s in range(MLP_SLABS):
            cols = slice(s * tf, (s + 1) * tf)
            dhpre = (_dot(dht, w2_ref[s], NT) * (2.0 * r_ref[:, cols].astype(F32))).astype(BF16)
            dhpre_ref[:, cols] = dhpre
            d = jnp.concatenate([_dot(dhpre, w1a_ref[s], NT), _dot(dhpre, w1b_ref[s], NT)], axis=1)
            part = d if part is None else part + d
        acc_ref[...] += part

        @pl.when(f == nf - 1)
        def _():
            du2 = acc_ref[...]
            xhat = xhat_ref[...]
            x1 = xhat * g1_ref[...] + b1_ref[...]
            dx1 = ALPHA * dr2_ref[...] + du2 * (1.0 + sc_ref[...])
            small_ref[:, 2 * dm:3 * dm] += _rowsum(du2 * x1)
            small_ref[:, dm:2 * dm] += _rowsum(du2)
            small_ref[:, 3 * dm:4 * dm] += _rowsum(dx1 * xhat)
            small_ref[:, 4 * dm:5 * dm] += _rowsum(dx1)
            dr1 = _ln_bwd(dx1 * g1_ref[...], xhat, rstd_ref[:, 0:1])
            dr1_ref[...] = dr1
            small_ref[:, 0:dm] += _rowsum(dr1 * mix_ref[...])
            dmix_ref[...] = ((1.0 + ga_ref[...]) * dr1).astype(BF16)

    row = pl.BlockSpec((tm, D_MODEL), lambda i, f: (i, 0))
    vec = _full((1, D_MODEL))
    return pl.pallas_call(
        body, name="mlp_bwd", grid=(T // tm, nf),
        in_specs=[row, pl.BlockSpec((MLP_SLABS, half, tf), lambda i, f: (f, 0, 0)),
                  pl.BlockSpec((MLP_SLABS, half, tf), lambda i, f: (f, 0, 0)),
                  pl.BlockSpec((MLP_SLABS, tf, D_MODEL), lambda i, f: (f, 0, 0)),
                  pl.BlockSpec((tm, MLP_SLABS * tf), lambda i, f: (i, f)), row, row,
                  pl.BlockSpec((tm, 128), lambda i, f: (i, 0)), row, vec, vec, vec, vec],
        out_specs=[pl.BlockSpec((tm, MLP_SLABS * tf), lambda i, f: (i, f)), row, row, _full((1, 5 * D_MODEL))],
        out_shape=[jax.ShapeDtypeStruct((T, N_DEV * tf), BF16), jax.ShapeDtypeStruct((T, D_MODEL), F32),
                   jax.ShapeDtypeStruct((T, D_MODEL), BF16), jax.ShapeDtypeStruct((1, 5 * D_MODEL), F32)],
        scratch_shapes=[pltpu.VMEM((tm, D_MODEL), F32)],
        compiler_params=_params(),
    )(dh, w1[0], w1[1], w2, r, dr2, xhat1, rstd1, mix, ln1_g, ln1_b, sc_m, g_a)


def _input_bwd(dz_h, dz_m, w_in_ext, x, dr1, sc_a, exchange=None):
    T = x.shape[0]
    tm = min(ROW_TILE, T)

    def body(dzh_ref, dzm_ref, w_ref, x_ref, dr1_ref, sc_ref, gx_ref, small_ref):
        @pl.when(pl.program_id(0) == 0)
        def _():
            small_ref[...] = jnp.zeros_like(small_ref)

        du = _bdot(dzh_ref[...], w_ref[0:2048, :]) + _bdot(dzm_ref[...], w_ref[2048:3072, :])
        gx_ref[...] = ALPHA * dr1_ref[...] + du * (1.0 + sc_ref[...])
        small_ref[:, D_MODEL:] += _rowsum(du * x_ref[...])
        small_ref[:, 0:D_MODEL] += _rowsum(du)

    row = pl.BlockSpec((tm, D_MODEL), lambda i: (i, 0))
    vec = _full((1, D_MODEL))
    return _call(
        body, "input_bwd", (dz_h, dz_m, w_in_ext, x, dr1, sc_a), grid=(T // tm,),
        in_specs=[pl.BlockSpec((tm, 2048), lambda i: (i, 0)), row, _full(w_in_ext.shape), row, row, vec],
        out_specs=[row, _full((1, 2 * D_MODEL))],
        out_shape=[jax.ShapeDtypeStruct((T, D_MODEL), F32), jax.ShapeDtypeStruct((1, 2 * D_MODEL), F32)],
        exchange=exchange)


def _adam_math(w, g, m, v):
    m = ADAM_B1 * m + (1.0 - ADAM_B1) * g
    v = ADAM_B2 * v + (1.0 - ADAM_B2) * (g * g)
    m_hat = m / (1.0 - ADAM_B1 ** ADAM_STEP)
    v_hat = v / (1.0 - ADAM_B2 ** ADAM_STEP)
    return -ADAM_LR * (m_hat / (jnp.sqrt(v_hat) + ADAM_EPS) + ADAM_WD * w), m, v


def _adam(g_slabs, w, m, v, name, g_fn=None, g_extra=()):
    R, C = w.shape
    tr = 256 if R % 256 == 0 else R
    ns = 0 if g_slabs is None else g_slabs.shape[0]
    slab_rows = tr if g_slabs is None or g_slabs.shape[1] == R else g_slabs.shape[1]
    assert slab_rows == tr or tr == R
    ne = len(g_extra)

    def body(*refs):
        e_refs = refs[:ne]
        refs = refs[ne:]
        if ns:
            gs_ref, refs = refs[0], refs[1:]
        w_ref, m_ref, v_ref, g_ref, d_ref, nm_ref, nv_ref = refs
        if g_fn is not None:
            g = g_fn(*e_refs)
        else:
            g = gs_ref[0].astype(F32)
            for s in range(1, ns):
                g = g + gs_ref[s].astype(F32)
            g = g[:tr]
        d, nm, nv = _adam_math(w_ref[...], g, m_ref[...], v_ref[...])
        g_ref[...] = g
        d_ref[...] = d
        nm_ref[...] = nm
        nv_ref[...] = nv

    blk = pl.BlockSpec((tr, C), lambda i: (i, 0))
    in_specs = [pl.BlockSpec((tr, e.shape[1]), lambda i: (i, 0)) if e.shape[0] == R else _full(e.shape) for e in g_extra]
    args = list(g_extra)
    if ns:
        in_specs.append(pl.BlockSpec((ns, slab_rows, C), lambda i: (0, i, 0)))
        args.append(g_slabs)
    return pl.pallas_call(
        body, name=name, grid=(R // tr,), in_specs=in_specs + [blk] * 3, out_specs=[blk] * 4,
        out_shape=[jax.ShapeDtypeStruct((R, C), F32)] * 4, compiler_params=_params(),
    )(*args, w, m, v)


def _adam_small(small_all, params):
    n = len(params)

    def body(*refs):
        s_ref, refs = refs[0], refs[1:]
        wmv, loss_ref, outs = refs[:3 * n], refs[3 * n], refs[3 * n + 1:]
        tot = s_ref[0]
        for i in range(1, N_DEV):
            tot = tot + s_ref[i]
        loss_ref[...] = tot[:, SMALL_W - 128:]
        for j, (w, _, _, off) in enumerate(params):
            w_ref, m_ref, v_ref = wmv[3 * j:3 * j + 3]
            g_ref, d_ref, nm_ref, nv_ref = outs[4 * j:4 * j + 4]
            if w.shape[0] == 2:
                lb = _lower_bound(w_ref)
                g0 = tot[:, off:off + w.shape[1]] * lb * (1.0 - lb)
                rows = [(slice(0, 1), g0), (slice(1, 2), -g0)]
            else:
                rows = [(slice(0, 1), tot[:, off:off + w.shape[1]])]
            for rs, g in rows:
                d, nm, nv = _adam_math(w_ref[rs, :], g, m_ref[rs, :], v_ref[rs, :])
                g_ref[rs, :], d_ref[rs, :], nm_ref[rs, :], nv_ref[rs, :] = g, d, nm, nv

    out_shape = [jax.ShapeDtypeStruct((1, 128), F32)]
    for w, _, _, _ in params:
        out_shape += [jax.ShapeDtypeStruct(w.shape, F32)] * 4
    res = pl.pallas_call(body, name="adam_small", out_shape=out_shape, compiler_params=_params())(
        small_all, *[a for w, m, v, _ in params for a in (w, m, v)])
    return res[0], [tuple(res[1 + 4 * j:5 + 4 * j]) for j in range(n)]


def _cols_from_slabs(g):
    s, r, c = g.shape
    return jnp.transpose(g, (1, 0, 2)).reshape(r, s * c)


def _slabs_from_cols(w):
    r, c = w.shape
    return jnp.transpose(w.reshape(r, N_DEV, c // N_DEV), (1, 0, 2))


def _rot_half_rows(wt):
    return jnp.concatenate([-wt[32:], wt[:32]], axis=0)


def _unrot_half_rows(dwt_rot):
    return jnp.concatenate([dwt_rot[32:], -dwt_rot[:32]], axis=0)


def _ext_in_t(g):
    n, rows, k_in = g.shape
    keep = n * rows - ROPE_DIM

    def body(g_ref, o_ref, stage_ref):
        stage_ref[keep:, :] = jnp.zeros((o_ref.shape[0] - keep, k_in), F32)
        for i in range(n - 1):
            stage_ref[rows * i:rows * (i + 1), :] = g_ref[i].astype(F32)
        last = g_ref[n - 1].astype(F32)
        stage_ref[rows * (n - 1):keep, :] = last[:rows - ROPE_DIM]
        wk = last[rows - ROPE_DIM:]
        stage_ref[keep + 128:keep + 192, :] = wk
        stage_ref[keep + 384:keep + 416, :] = -wk[32:]
        stage_ref[keep + 416:keep + 448, :] = wk[:32]
        o_ref[...] = stage_ref[...].astype(BF16)

    return pl.pallas_call(body, name="ext_w_in", out_shape=jax.ShapeDtypeStruct((keep + 512, k_in), BF16),
                          scratch_shapes=[pltpu.VMEM((keep + 512, k_in), F32)], compiler_params=_params())(g)


def _ext_q_t(wt):
    r = wt.shape[1]
    z64, z128 = jnp.zeros((64, r), BF16), jnp.zeros((128, r), BF16)
    per = HEAD_DIM + ROPE_DIM
    main = [jnp.concatenate([wt[per * h:per * (h + 1)], z64], axis=0) for h in range(HEADS)]
    rot = [jnp.concatenate([z128, _rot_half_rows(wt[per * h + HEAD_DIM:per * (h + 1)]), z64], axis=0)
           for h in range(HEADS)]
    return jnp.concatenate(main + rot, axis=0)


def _ext_kv(w_kv_up):
    r = w_kv_up.shape[0]
    z128 = jnp.zeros((r, 128), BF16)
    wkv = w_kv_up.reshape(r, HEADS, 2 * HEAD_DIM)
    kpad = [jnp.concatenate([wkv[:, h, :HEAD_DIM], z128], axis=1) for h in range(HEADS)]
    vals = [wkv[:, h, HEAD_DIM:] for h in range(HEADS)]
    return jnp.concatenate(kpad + vals, axis=1)


def _w_in_grad_slabs(dwt_h, dwt_m):
    d = dwt_h.shape[1]
    rows = (dwt_h.shape[0] + 512 + ROPE_DIM) // N_DEV
    padded = rows + (-rows) % 16

    def body(h_ref, m_ref, o_ref, stage_ref):
        stage_ref[0:2048, :] = h_ref[...]
        stage_ref[2048:2560, :] = m_ref[0:512, :]
        rot = m_ref[768 + 128:768 + 192, :]
        stage_ref[2560:2592, :] = m_ref[640:672, :] + rot[32:]
        stage_ref[2592:2624, :] = m_ref[672:704, :] - rot[:32]
        zero = jnp.zeros((padded - rows, d), F32)
        for i in range(N_DEV):
            o_ref[i] = jnp.concatenate([stage_ref[rows * i:rows * (i + 1), :], zero], axis=0).astype(BF16)

    return pl.pallas_call(body, name="w_in_grad_slabs", out_shape=jax.ShapeDtypeStruct((N_DEV, padded, d), BF16),
                          scratch_shapes=[pltpu.VMEM((N_DEV * rows, d), F32)], compiler_params=_params())(dwt_h, dwt_m)


def _grad_q_from_ext_t(dwq_ext_t):
    rows = []
    for h in range(HEADS):
        main, rot = dwq_ext_t[256 * h:256 * h + 256], dwq_ext_t[1024 + 256 * h:1280 + 256 * h]
        rows += [main[:128], main[128:192] + _unrot_half_rows(rot[128:192])]
    return jnp.concatenate(rows, axis=0)


def _grad_kv_from_ext(dwkv_ext):
    kvcols = []
    for h in range(HEADS):
        kvcols += [dwkv_ext[:, 256 * h:256 * h + 128], dwkv_ext[:, 1024 + 128 * h:1152 + 128 * h]]
    return jnp.concatenate(kvcols, axis=1)


SMALL_W = 6144 + 512 + 512 + 256 + 256 + 4 * 1024 + 128


def kernel(x, c, positions, w_ada, b_ada, w_in, hg_lower_bounds, hg_norm_w, mla_q_norm_w, w_q_up, mla_kv_norm_w, w_kv_up, w_out, ln1_g, ln1_b, w_mlp_in, w_mlp_out, ln2_g, ln2_b, loss_target, m_w_ada, m_b_ada, m_w_in, m_hg_lower_bounds, m_hg_norm_w, m_mla_q_norm_w, m_w_q_up, m_mla_kv_norm_w, m_w_kv_up, m_w_out, m_ln1_g, m_ln1_b, m_w_mlp_in, m_w_mlp_out, m_ln2_g, m_ln2_b, v_w_ada, v_b_ada, v_w_in, v_hg_lower_bounds, v_hg_norm_w, v_mla_q_norm_w, v_w_q_up, v_mla_kv_norm_w, v_w_kv_up, v_w_out, v_ln1_g, v_ln1_b, v_w_mlp_in, v_w_mlp_out, v_ln2_g, v_ln2_b):
    T = x.shape[1]
    me = 4 * lax.axis_index("x") + 2 * lax.axis_index("y") + lax.axis_index("c")
    xs, tgt = x[0], loss_target[0]
    transposed = ("w_in", "w_q_up")
    as_used = lambda n, a: a[0].T if n in transposed else a[0]
    big = {n: as_used(n, a) for n, a in dict(w_in=w_in, w_q_up=w_q_up, w_kv_up=w_kv_up, w_out=w_out,
                                              w_mlp_in=w_mlp_in, w_mlp_out=w_mlp_out).items()}
    names = list(big)

    bf = {n: big[n].astype(BF16) for n in ("w_in", "w_q_up", "w_kv_up", "w_out")}
    g_in, g_c = _gather_two_level([bf["w_in"], c], name="gather_w_in")
    c_all = g_c.reshape(N_DEV, D_MODEL)

    ada_cols = w_ada.shape[2]
    mod_part, cond = _mod_part(c_all, w_ada[0], lax.dynamic_slice(b_ada, (0, me * ada_cols), (1, ada_cols)))
    (mod_all,) = _exchange([mod_part], scatter=False, name="gather_mod")
    mod_row = lax.dynamic_slice(mod_all, (0, me, 0), (N_DEV, 1, ada_cols)).reshape(1, N_DEV * ada_cols)
    sh_a, sc_a, g_a, sh_m, sc_m, g_m = [mod_row[:, D_MODEL * i:D_MODEL * (i + 1)] for i in range(6)]

    w_in_ext = _ext_in_t(g_in)
    half = D_MODEL // 2
    z, (w1_top,) = _matmul(xs, w_in_ext, "NT", "in_proj", a_fn=_modulate, extras=(sc_a, sh_a), tn=3072,
                           exchange=_StagedGather(big["w_mlp_in"], rows=(0, half)))
    (o_raw, o_gated, s_prev), (g_q, g_kv, g_out) = _hgrn_fwd(
        z, hg_lower_bounds, hg_norm_w, exchange=_Exchange([bf["w_q_up"], bf["w_kv_up"], bf["w_out"]], False))
    wq_ext = _ext_q_t(g_q.reshape(N_DEV * g_q.shape[1], g_q.shape[2]))
    wkv_ext = _ext_kv(_cols_from_slabs(g_kv))
    w_out_full = g_out.reshape(D_MODEL, D_MODEL)
    inv_freq = 1.0 / (ROPE_THETA ** (jnp.arange(0, ROPE_DIM, 2, dtype=F32) / ROPE_DIM))
    zeros = lambda n: jnp.zeros((n,), F32)
    invf = jnp.concatenate([zeros(128), inv_freq, inv_freq, zeros(64)]).reshape(1, QK_PAD)
    m_rot = jnp.concatenate([zeros(128), jnp.ones((64,), F32), zeros(64)]).reshape(1, QK_PAD)
    q, k, v, c1, s1, cqn, ckvn = _mla_pre(z, positions.reshape(T, 1), invf, m_rot, wq_ext, wkv_ext,
                                          mla_q_norm_w, mla_kv_norm_w)[0]
    (o_mla, lse), (w1_bot, w2) = _attn_fwd(
        q, k, v, exchange=[_StagedGather(big["w_mlp_in"], 0.85, rows=(half, half)),
                           _StagedGather(big["w_mlp_out"], 0.85)])
    w1 = (w1_top, w1_bot)
    mix, xhat1, rstd1, u2 = _mix_ln1(o_gated, o_mla, w_out_full, xs, g_a, ln1_g, ln1_b, sc_m, sh_m)[0]
    r, dr2, dh, small_mlp_fwd = _mlp_fwd(u2, w1, w2, xhat1, ln1_g, ln1_b, g_m, ln2_g, ln2_b, tgt)

    dhpre, dr1, dmix, small_mlp_bwd = _mlp_bwd(dh, w1, w2, r, dr2, xhat1, rstd1, mix, ln1_g, ln1_b, sc_m, g_a)
    received = {}
    dw2 = _matmul(r, dh, "TN", "wgrad_mlp_out", out_dtype=BF16, a_fn=_square, tm=1024, tk=2048)
    dw1 = _matmul(u2, dhpre, "TN", "wgrad_mlp_in", out_dtype=BF16, tm=1024, tk=2048, out_slabs=N_DEV)
    dmixcat = _matmul(dmix, w_out_full, "NT", "dgrad_out", tm=1024)
    dw_out = jnp.concatenate([_matmul(o_gated, dmix, "TN", "wgrad_out_hg", out_dtype=BF16, tk=2048),
                              _matmul(o_mla, dmix, "TN", "wgrad_out_mla", out_dtype=BF16, tk=2048)], axis=0)
    (dz_h, small_hgrn), (received["w_mlp_in"],) = _hgrn_bwd(
        dmixcat, z, o_raw, s_prev, hg_lower_bounds, hg_norm_w, exchange=_StagedScatter(dw1, 0.1))
    (dq, dk, dv), (received["w_mlp_out"], received["w_out"]) = _attn_bwd(
        q, k, v, dmixcat, o_mla, lse,
        exchange=_Exchange([dw2.reshape(N_DEV, dw2.shape[0] // N_DEV, D_MODEL),
                            dw_out.reshape(N_DEV, D_MODEL // N_DEV, D_MODEL)], True))
    dz_m, dq_ext, dkv_ext, small_mla = _mla_bwd(dq, dk, dv, z, c1, s1, wq_ext, wkv_ext, mla_q_norm_w, mla_kv_norm_w)
    dwq_t = _grad_q_from_ext_t(_matmul(dq_ext, cqn, "TN", "wgrad_q_up", tm=1024, tk=2048))
    dwkv = _grad_kv_from_ext(_matmul(ckvn, dkv_ext, "TN", "wgrad_kv_up", tn=1536, tk=2048))
    qkv_slabs = [dwq_t.reshape((N_DEV, dwq_t.shape[0] // N_DEV, dwq_t.shape[1])).astype(BF16),
                 _slabs_from_cols(dwkv).astype(BF16)]
    dwt_h = _matmul(dz_h, xs, "TN", "wgrad_in_h", b_fn=_modulate, extras=(sc_a, sh_a), tm=1024, tk=2048)
    dwt_m = _matmul(dz_m, xs, "TN", "wgrad_in_m", b_fn=_modulate, extras=(sc_a, sh_a), tm=1024, tk=2048)
    in_slabs = _w_in_grad_slabs(dwt_h, dwt_m)
    (grad_x, small_in), (received["w_q_up"], received["w_kv_up"], received["w_in"]) = _input_bwd(
        dz_h, dz_m, w_in_ext, xs, dr1, sc_a,
        exchange=[_Exchange(qkv_slabs, True), _StagedScatter(in_slabs)])

    small = jnp.concatenate([small_in, small_mlp_bwd[:, :3 * D_MODEL], small_mlp_fwd[:, :D_MODEL], small_hgrn,
                             small_mla, small_mlp_bwd[:, 3 * D_MODEL:], small_mlp_fwd[:, D_MODEL:]], axis=1)
    assert small.shape == (1, SMALL_W)
    (small_all,) = _exchange([small], scatter=False, name="gather_small")

    moments = dict(w_in=(m_w_in, v_w_in), w_q_up=(m_w_q_up, v_w_q_up), w_kv_up=(m_w_kv_up, v_w_kv_up),
                   w_out=(m_w_out, v_w_out), w_mlp_in=(m_w_mlp_in, v_w_mlp_in), w_mlp_out=(m_w_mlp_out, v_w_mlp_out))
    res = {}
    for n in names:
        res[n] = _adam(received[n], big[n], as_used(n, moments[n][0]), as_used(n, moments[n][1]), name="adam_" + n)
    dmod_cols = lax.dynamic_slice(small_all.reshape(N_DEV, SMALL_W), (0, me * ada_cols), (N_DEV, ada_cols))
    cond_t = cond.T

    def ada_grad(ct_ref, dm_ref):
        g = ct_ref[:, 0:1] * dm_ref[0:1, :]
        for b in range(1, N_DEV):
            g = g + ct_ref[:, b:b + 1] * dm_ref[b:b + 1, :]
        return g

    res["w_ada"] = _adam(None, w_ada[0], m_w_ada[0], v_w_ada[0], name="adam_w_ada", g_fn=ada_grad,
                         g_extra=(cond_t, dmod_cols))

    small_params = [("b_ada", b_ada, m_b_ada, v_b_ada, 0),
                    ("hg_lower_bounds", hg_lower_bounds, m_hg_lower_bounds, v_hg_lower_bounds, 6144),
                    ("hg_norm_w", hg_norm_w, m_hg_norm_w, v_hg_norm_w, 6656),
                    ("mla_q_norm_w", mla_q_norm_w, m_mla_q_norm_w, v_mla_q_norm_w, 7168),
                    ("mla_kv_norm_w", mla_kv_norm_w, m_mla_kv_norm_w, v_mla_kv_norm_w, 7424),
                    ("ln1_g", ln1_g, m_ln1_g, v_ln1_g, 7680), ("ln1_b", ln1_b, m_ln1_b, v_ln1_b, 8704),
                    ("ln2_g", ln2_g, m_ln2_g, v_ln2_g, 9728), ("ln2_b", ln2_b, m_ln2_b, v_ln2_b, 10752)]
    loss_row, small_res = _adam_small(small_all, [p[1:] for p in small_params])
    for p, r4 in zip(small_params, small_res):
        res[p[0]] = r4
    loss = loss_row[0, 0]

    order = ["w_ada", "b_ada", "w_in", "hg_lower_bounds", "hg_norm_w", "mla_q_norm_w", "w_q_up", "mla_kv_norm_w",
             "w_kv_up", "w_out", "ln1_g", "ln1_b", "w_mlp_in", "w_mlp_out", "ln2_g", "ln2_b"]
    def as_given(n, a):
        if n in transposed:
            a = a.T
        return a[None] if n in big or n == "w_ada" else a

    shaped = {n: tuple(as_given(n, a) for a in res[n]) for n in order}
    outs = [loss, grad_x.reshape(1, T, D_MODEL)]
    for i in range(4):
        outs += [shaped[n][i] for n in order]
    return tuple(outs)
```

```python
import jax
import jax.numpy as jnp
import numpy as np
from jax import lax
from jax.experimental import pallas as pl
from jax.experimental.pallas import tpu as pltpu

F32, BF16 = jnp.float32, jnp.bfloat16
N_DEV = 8
D_MODEL = 1024
HEADS = 4
HEAD_DIM = 128
ROPE_DIM = 64
QK_PAD = 256
CHUNK = 64
ROPE_THETA = 10000.0
RMS_EPS = 1e-6
LN_EPS = 1e-5
ALPHA = 2.0 ** 0.25
ATT_SCALE = (HEAD_DIM + ROPE_DIM) ** -0.5
LN2 = float(np.log(2.0))
Q_PRESCALE = ATT_SCALE / LN2
ADAM_LR, ADAM_B1, ADAM_B2, ADAM_EPS, ADAM_WD, ADAM_STEP = 0.001, 0.9, 0.999, 1e-08, 0.01, 10
NEG_BIG = -1e30

ROW_TILE = 512
ATT_TILE = 512
HGRN_GROUP = 8
MLP_SLABS = 4
VMEM_LIMIT = 56 * 2 ** 20

NN = (((1,), (0,)), ((), ()))
NT = (((1,), (1,)), ((), ()))
TN = (((0,), (0,)), ((), ()))


def _dot(a, b, dims=NN):
    return lax.dot_general(a, b, dims, preferred_element_type=F32)


def _bdot(a, b, dims=NN):
    return lax.dot_general(a.astype(BF16), b.astype(BF16), dims, preferred_element_type=F32)


def _hdot(a, b, dims=NN):
    return lax.dot_general(a, b, dims, precision=lax.Precision.HIGHEST, preferred_element_type=F32)


def _params():
    return pltpu.CompilerParams(vmem_limit_bytes=VMEM_LIMIT)


def _sigmoid(x):
    return 1.0 / (1.0 + jnp.exp(-x))


def _rowsum(x):
    return jnp.sum(x, axis=0, keepdims=True)


def _lanemean(x):
    return jnp.mean(x, axis=-1, keepdims=True)


def _full(shape):
    nd = len(shape)
    return pl.BlockSpec(shape, lambda *_: (0,) * nd)


class _Exchange:
    def __init__(self, arrs, scatter):
        self.arrs, self.scatter, self.n, self.aliases, self.middle_at = list(arrs), scatter, len(arrs), [], 0.5
        self.out_shape = [jax.ShapeDtypeStruct((N_DEV,) + (a.shape[1:] if scatter else a.shape), a.dtype)
                          for a in self.arrs]
        n = self.n
        self.scratch = [pltpu.SemaphoreType.DMA((n, N_DEV - 1)), pltpu.SemaphoreType.DMA((n, N_DEV - 1)),
                        pltpu.SemaphoreType.DMA((n,))]

    def _copies(self, ins, outs, sems):
        send_sems, recv_sems, loc_sems = sems
        x, y, c = lax.axis_index("x"), lax.axis_index("y"), lax.axis_index("c")
        me = 4 * x + 2 * y + c
        copies = []
        for k in range(self.n):
            src_of = (lambda i, k=k: ins[k].at[i]) if self.scatter else (lambda i, k=k: ins[k])
            copies.append((pltpu.make_async_copy(src_of(me), outs[k].at[me], loc_sems.at[k]), None))
            for p in range(1, N_DEV):
                px = (1 - x) if p & 4 else x
                py = (1 - y) if p & 2 else y
                pc = (1 - c) if p & 1 else c
                peer = 4 * px + 2 * py + pc
                both = dict(send_sem=send_sems.at[k, p - 1], recv_sem=recv_sems.at[k, p - 1],
                            device_id=(px, py, pc), device_id_type=pl.DeviceIdType.MESH)
                send = pltpu.make_async_remote_copy(src_ref=src_of(peer), dst_ref=outs[k].at[me], **both)
                recv = pltpu.make_async_remote_copy(src_ref=src_of(peer), dst_ref=outs[k].at[peer], **both)
                copies.append((send, recv))
        return copies

    def start(self, ins, outs, sems):
        for first, _ in self._copies(ins, outs, sems):
            first.start()

    def middle(self, ins, outs, sems):
        pass

    def wait(self, ins, outs, sems):
        for first, recv in self._copies(ins, outs, sems):
            if recv is None:
                first.wait()
            else:
                recv.wait_recv()
                first.wait_send()


class _StagedGather:
    def __init__(self, arr, middle_at=0.8, rows=None):
        self.r0, n = rows if rows else (0, arr.shape[0])
        block = (n,) + arr.shape[1:]
        self.arrs, self.aliases, self.middle_at = [arr], [], middle_at
        self.out_shape = [jax.ShapeDtypeStruct((N_DEV,) + block, BF16)]
        self.scratch = [pltpu.VMEM((N_DEV,) + block, BF16), pltpu.SemaphoreType.DMA((7,)),
                        pltpu.SemaphoreType.DMA((7,)), pltpu.SemaphoreType.DMA((2,)), pltpu.VMEM(block, arr.dtype)]

    def _parts(self, scr):
        stage, send_sems, recv_sems, loc_sems = scr[:4]
        x, y, c = lax.axis_index("x"), lax.axis_index("y"), lax.axis_index("c")
        me, sibling = (x, y, c), (x, y, 1 - c)
        chips = [(1 - x, y), (x, 1 - y), (1 - x, 1 - y)]

        def copy(j, block, to):
            px, py, pc = block
            slot = stage.at[4 * px + 2 * py + pc]
            return pltpu.make_async_remote_copy(src_ref=slot, dst_ref=slot, send_sem=send_sems.at[j],
                                                recv_sem=recv_sems.at[j], device_id=to,
                                                device_id_type=pl.DeviceIdType.MESH)

        return stage, loc_sems, me, sibling, chips, c, copy

    def start(self, ins, outs, scr):
        stage, loc_sems, me, sibling, chips, c, copy = self._parts(scr)
        x, y, _ = me
        raw = scr[4]
        own = pltpu.make_async_copy(ins[0].at[pl.ds(self.r0, raw.shape[0])], raw, loc_sems.at[0])
        own.start()
        own.wait()
        stage[4 * x + 2 * y + c] = raw[...].astype(BF16)
        copy(0, me, sibling).start()
        for j, chip in enumerate(chips):
            copy(1 + j, me, (*chip, c)).start()

    def middle(self, ins, outs, scr):
        stage, loc_sems, me, sibling, chips, c, copy = self._parts(scr)
        for j, chip in enumerate(chips):
            copy(1 + j, (*chip, c), me).wait_recv()
            copy(4 + j, (*chip, c), sibling).start()

    def wait(self, ins, outs, scr):
        stage, loc_sems, me, sibling, chips, c, copy = self._parts(scr)
        copy(0, sibling, me).wait_recv()
        for j, chip in enumerate(chips):
            copy(4 + j, (*chip, 1 - c), me).wait_recv()
        copy(0, me, sibling).wait_send()
        for j, chip in enumerate(chips):
            copy(1 + j, me, (*chip, c)).wait_send()
            copy(4 + j, (*chip, c), sibling).wait_send()
        whole = pltpu.make_async_copy(stage, outs[0], loc_sems.at[1])
        whole.start()
        whole.wait()


class _StagedScatter:
    def __init__(self, slabs, middle_at=0.2):
        _, r, c = slabs.shape
        self.arrs, self.aliases, self.middle_at = [slabs], [], middle_at
        self.out_shape = [jax.ShapeDtypeStruct((4, r, c), slabs.dtype)]
        self.scratch = [pltpu.VMEM((N_DEV, r, c), slabs.dtype), pltpu.VMEM((4, r, c), slabs.dtype),
                        pltpu.VMEM((3, r, c), slabs.dtype), pltpu.SemaphoreType.DMA((4,)), pltpu.SemaphoreType.DMA((4,)),
                        pltpu.SemaphoreType.DMA((3,)), pltpu.SemaphoreType.DMA((3,)), pltpu.SemaphoreType.DMA((4,))]

    def _parts(self, scr):
        stage, from_sib, from_chips, sib_send, sib_recv, ici_send, ici_recv, loc_sems = scr
        x, y, c = lax.axis_index("x"), lax.axis_index("y"), lax.axis_index("c")
        chips = [(1 - x, y), (x, 1 - y), (1 - x, 1 - y)]

        def to_sibling(j):
            return pltpu.make_async_remote_copy(src_ref=stage.at[2 * j + 1 - c], dst_ref=from_sib.at[j],
                                                send_sem=sib_send.at[j], recv_sem=sib_recv.at[j],
                                                device_id=(x, y, 1 - c), device_id_type=pl.DeviceIdType.MESH)

        def to_chip(k):
            px, py = chips[k]
            return pltpu.make_async_remote_copy(src_ref=stage.at[4 * px + 2 * py + c], dst_ref=from_chips.at[k],
                                                send_sem=ici_send.at[k], recv_sem=ici_recv.at[k],
                                                device_id=(px, py, c), device_id_type=pl.DeviceIdType.MESH)

        return stage, from_sib, from_chips, loc_sems, (x, y, c), chips, to_sibling, to_chip

    def start(self, ins, outs, scr):
        stage, _, _, loc_sems, _, _, to_sibling, _ = self._parts(scr)
        load = pltpu.make_async_copy(ins[0], stage, loc_sems.at[0])
        load.start()
        load.wait()
        for j in range(4):
            to_sibling(j).start()

    def middle(self, ins, outs, scr):
        stage, from_sib, _, _, (x, y, c), _, to_sibling, to_chip = self._parts(scr)
        for j in range(4):
            to_sibling(j).wait_recv()
            mine = stage.at[2 * j + c]
            mine[...] = (mine[...].astype(F32) + from_sib[j].astype(F32)).astype(mine.dtype)
        for k in range(3):
            to_chip(k).start()

    def wait(self, ins, outs, scr):
        stage, _, from_chips, loc_sems, (x, y, c), chips, to_sibling, to_chip = self._parts(scr)
        writes = [pltpu.make_async_copy(stage.at[4 * x + 2 * y + c], outs[0].at[2 * x + y], loc_sems.at[0])]
        for k, (px, py) in enumerate(chips):
            to_chip(k).wait_recv()
            writes.append(pltpu.make_async_copy(from_chips.at[k], outs[0].at[2 * px + py], loc_sems.at[1 + k]))
        for w in writes:
            w.start()
        for j in range(4):
            to_sibling(j).wait_send()
        for k in range(3):
            to_chip(k).wait_send()
        for w in writes:
            w.wait()


def _call(body, name, args, out_shape, grid=(), in_specs=(), out_specs=(), scratch_shapes=(), exchange=None):
    if exchange is None:
        return pl.pallas_call(body, name=name, grid=grid, in_specs=list(in_specs), out_specs=list(out_specs),
                              out_shape=list(out_shape), scratch_shapes=list(scratch_shapes),
                              compiler_params=_params())(*args), None
    exs = list(exchange) if isinstance(exchange, (list, tuple)) else [exchange]
    ni, no, ns = len(args), len(out_shape), len(scratch_shapes)
    nxi, nxo = sum(len(e.arrs) for e in exs), sum(len(e.out_shape) for e in exs)
    steps = int(np.prod(grid))
    mid_step = lambda e: min(max(int(steps * e.middle_at), 1), steps - 1)
    aliases, iat, oat = {}, ni, no
    for e in exs:
        for src, dst in e.aliases:
            aliases[iat + src] = oat + dst
        iat, oat = iat + len(e.arrs), oat + len(e.out_shape)

    def wrapped(*refs):
        a, xi = refs[:ni], refs[ni:ni + nxi]
        o, xo = refs[ni + nxi:ni + nxi + no], refs[ni + nxi + no:ni + nxi + no + nxo]
        s, xs = refs[ni + nxi + no + nxo:ni + nxi + no + nxo + ns], refs[ni + nxi + no + nxo + ns:]
        parts, iat, oat, sat = [], 0, 0, 0
        for e in exs:
            parts.append((e, xi[iat:iat + len(e.arrs)], xo[oat:oat + len(e.out_shape)], xs[sat:sat + len(e.scratch)]))
            iat, oat, sat = iat + len(e.arrs), oat + len(e.out_shape), sat + len(e.scratch)
        step = 0
        for d, g in enumerate(grid):
            step = step * g + pl.program_id(d)

        @pl.when(step == 0)
        def _():
            for e, ins, outs, sems in parts:
                e.start(ins, outs, sems)

        for at_step in sorted({mid_step(e) for e in exs}):
            @pl.when(step == at_step)
            def _():
                for e, ins, outs, sems in parts:
                    if mid_step(e) == at_step:
                        e.middle(ins, outs, sems)

        body(*a, *o, *s)

        @pl.when(step == steps - 1)
        def _():
            for e, ins, outs, sems in parts:
                e.wait(ins, outs, sems)

    hbm = pl.BlockSpec(memory_space=pltpu.HBM)
    res = pl.pallas_call(
        wrapped, name=name, grid=grid, in_specs=list(in_specs) + [hbm] * nxi, out_specs=list(out_specs) + [hbm] * nxo,
        out_shape=list(out_shape) + [o_ for e in exs for o_ in e.out_shape],
        scratch_shapes=list(scratch_shapes) + [s_ for e in exs for s_ in e.scratch],
        input_output_aliases=aliases, compiler_params=_params())(*args, *[a_ for e in exs for a_ in e.arrs])
    return res[:no], res[no:]


def _gather_two_level(arrs, name):
    n = len(arrs)
    out_shape = [jax.ShapeDtypeStruct((N_DEV,) + a.shape, a.dtype) for a in arrs]

    def body(*refs):
        ins, outs = refs[:n], refs[n:2 * n]
        send_sems, recv_sems, loc_sems = refs[2 * n:]
        x, y, c = lax.axis_index("x"), lax.axis_index("y"), lax.axis_index("c")
        me, sibling = (x, y, c), (x, y, 1 - c)
        chips = [(1 - x, y), (x, 1 - y), (1 - x, 1 - y)]

        def copy(k, j, block, to, src=None):
            px, py, pc = block
            dst = outs[k].at[4 * px + 2 * py + pc]
            return pltpu.make_async_remote_copy(src_ref=dst if src is None else src, dst_ref=dst,
                                                send_sem=send_sems.at[k, j], recv_sem=recv_sems.at[k, j],
                                                device_id=to, device_id_type=pl.DeviceIdType.MESH)

        mine = [pltpu.make_async_copy(ins[k], outs[k].at[4 * x + 2 * y + c], loc_sems.at[k]) for k in range(n)]
        first = []
        for k in range(n):
            mine[k].start()
            first.append(copy(k, 0, me, sibling, src=ins[k]))
            first += [copy(k, 1 + j, me, (*chip, c), src=ins[k]) for j, chip in enumerate(chips)]
        for cp in first:
            cp.start()
        passed = []
        for j, chip in enumerate(chips):
            for k in range(n):
                copy(k, 1 + j, (*chip, c), me).wait_recv()
                passed.append(copy(k, 4 + j, (*chip, c), sibling))
                passed[-1].start()
        for k in range(n):
            copy(k, 0, sibling, me).wait_recv()
            for j, chip in enumerate(chips):
                copy(k, 4 + j, (*chip, 1 - c), me).wait_recv()
        for cp in first + passed:
            cp.wait_send()
        for cp in mine:
            cp.wait()

    vmem = pl.BlockSpec(memory_space=pltpu.VMEM)
    return pl.pallas_call(body, name=name, out_shape=out_shape, in_specs=[vmem] * n, out_specs=[vmem] * n,
                          scratch_shapes=[pltpu.SemaphoreType.DMA((n, 7)), pltpu.SemaphoreType.DMA((n, 7)),
                                          pltpu.SemaphoreType.DMA((n,))], compiler_params=_params())(*arrs)


def _exchange(arrs, scatter, name):
    ex = _Exchange(arrs, scatter)

    def body(*refs):
        ins, outs, sems = refs[:ex.n], refs[ex.n:2 * ex.n], refs[2 * ex.n:]
        ex.start(ins, outs, sems)
        ex.wait(ins, outs, sems)

    hbm = pl.BlockSpec(memory_space=pltpu.HBM)
    return pl.pallas_call(body, name=name, out_shape=ex.out_shape, in_specs=[hbm] * ex.n, out_specs=[hbm] * ex.n,
                          scratch_shapes=ex.scratch)(*ex.arrs)


def _matmul(a, b, mode, name, out_dtype=F32, tm=512, tn=1024, tk=1024, a_fn=None, b_fn=None, extras=(),
            out_slabs=None, exchange=None):
    assert not (a_fn and b_fn) and not (b_fn and mode == "NT")
    if mode == "NN":
        (M, K), N = a.shape, b.shape[1]
    elif mode == "NT":
        (M, K), N = a.shape, b.shape[0]
    else:
        (K, M), N = a.shape, b.shape[1]
    slab_w = N // out_slabs if out_slabs else None
    if out_slabs:
        tn = max(slab_w, min(tn, N) // slab_w * slab_w)
    tm, tn, tk = min(tm, M), min(tn, N), min(tk, K)
    assert M % tm == 0 and N % tn == 0 and K % tk == 0, (name, M, N, K)
    nk = K // tk
    dims = {"NN": NN, "NT": NT, "TN": TN}[mode]
    ne = len(extras)

    def body(a_ref, b_ref, *rest):
        e_refs, o_ref, acc_ref = rest[:ne], rest[ne], rest[ne + 1]
        k = pl.program_id(2)

        @pl.when(k == 0)
        def _():
            acc_ref[...] = jnp.zeros_like(acc_ref)

        at, bt = a_ref[...], b_ref[...]
        if a_fn is not None:
            at = a_fn(at.astype(F32), *[e[...] for e in e_refs])
        if b_fn is not None:
            bt = b_fn(bt.astype(F32), *[e[...] for e in e_refs])
        acc_ref[...] += _bdot(at, bt, dims)

        @pl.when(k == nk - 1)
        def _():
            if out_slabs:
                for s in range(tn // slab_w):
                    o_ref[s] = acc_ref[:, s * slab_w:(s + 1) * slab_w].astype(out_dtype)
            else:
                o_ref[...] = acc_ref[...].astype(out_dtype)

    if mode == "TN":
        a_spec = pl.BlockSpec((tk, tm), lambda i, j, k: (k, i))
        e_spec = pl.BlockSpec((1, tm), lambda i, j, k: (0, i))
    else:
        a_spec = pl.BlockSpec((tm, tk), lambda i, j, k: (i, k))
        e_spec = pl.BlockSpec((1, tk), lambda i, j, k: (0, k))
    if mode == "NT":
        b_spec = pl.BlockSpec((tn, tk), lambda i, j, k: (j, k))
    else:
        b_spec = pl.BlockSpec((tk, tn), lambda i, j, k: (k, j))
    if b_fn is not None:
        e_spec = pl.BlockSpec((1, tn), lambda i, j, k: (0, j))
    if out_slabs:
        o_shape = jax.ShapeDtypeStruct((out_slabs, M, slab_w), out_dtype)
        o_spec = pl.BlockSpec((tn // slab_w, tm, slab_w), lambda i, j, k: (j, i, 0))
    else:
        o_shape = jax.ShapeDtypeStruct((M, N), out_dtype)
        o_spec = pl.BlockSpec((tm, tn), lambda i, j, k: (i, j))
    (out,), got = _call(body, name, (a, b, *extras), [o_shape], grid=(M // tm, N // tn, nk),
                        in_specs=[a_spec, b_spec] + [e_spec] * ne, out_specs=[o_spec],
                        scratch_shapes=[pltpu.VMEM((tm, tn), F32)], exchange=exchange)
    return out if exchange is None else (out, got)


def _modulate(x, sc, sh):
    return x * (1.0 + sc) + sh


def _square(x):
    return x * x


def _mod_part(c_all, w_ada_s, b_s):
    def body(c_ref, w_ref, b_ref, mod_ref, cond_ref):
        cv = c_ref[...]
        cond = cv * _sigmoid(cv)
        cond_ref[...] = cond
        mod_ref[...] = _bdot(cond, w_ref[...]) + b_ref[...]

    return pl.pallas_call(
        body, name="mod_part",
        out_shape=[jax.ShapeDtypeStruct((N_DEV, w_ada_s.shape[1]), F32), jax.ShapeDtypeStruct(c_all.shape, F32)],
        compiler_params=_params(),
    )(c_all, w_ada_s, b_s)


def _rms_fwd(x, w):
    rs = lax.rsqrt(_lanemean(x * x) + RMS_EPS)
    return x * rs * w, rs


def _rms_bwd(x, rs, w, dy):
    xhat = x * rs
    dxh = dy * w
    return rs * (dxh - xhat * _lanemean(dxh * xhat)), dy * xhat


def _mla_pre(z, pos_col, invf, m_rot, wq_ext, wkv_ext, qnw, kvnw, exchange=None):
    T = z.shape[0]
    tm = min(ROW_TILE, T)

    def body(z_ref, pos_ref, invf_ref, mrot_ref, wq_ref, wkv_ref, qnw_ref, kvnw_ref,
             q_ref, k_ref, v_ref, c1_ref, s1_ref, cqn_ref, ckvn_ref):
        hi = slice(HEAD_DIM, QK_PAD)
        ang = pos_ref[...].astype(F32) * invf_ref[:, hi]
        c1 = jnp.concatenate([jnp.ones((tm, HEAD_DIM), F32), mrot_ref[:, hi] * jnp.cos(ang)], axis=1)
        s1 = jnp.concatenate([jnp.zeros((tm, HEAD_DIM), F32), mrot_ref[:, hi] * jnp.sin(ang)], axis=1)
        c1_ref[...] = c1
        s1_ref[...] = s1
        cqn, _ = _rms_fwd(z_ref[:, 0:256], qnw_ref[...])
        ckvn, _ = _rms_fwd(z_ref[:, 256:512], kvnw_ref[...])
        cqn_ref[...] = cqn.astype(BF16)
        ckvn_ref[...] = ckvn.astype(BF16)
        qe = _bdot(cqn, wq_ref[...], NT)
        kve = _bdot(ckvn, wkv_ref[...])
        k_rope = z_ref[:, 512:768] * c1 + z_ref[:, 768:1024] * s1
        for h in range(HEADS):
            q_ref[h] = ((qe[:, 256 * h:256 * h + 256] * c1 + qe[:, 1024 + 256 * h:1280 + 256 * h] * s1)
                        * Q_PRESCALE).astype(BF16)
            k_ref[h] = (kve[:, 256 * h:256 * h + 256] + k_rope).astype(BF16)
            v_ref[h] = kve[:, 1024 + 128 * h:1152 + 128 * h].astype(BF16)

    row = lambda i: (i, 0)
    head = lambda i: (0, i, 0)
    return _call(
        body, "mla_pre", (z, pos_col, invf, m_rot, wq_ext, wkv_ext, qnw, kvnw), grid=(T // tm,),
        in_specs=[pl.BlockSpec((tm, 1024), lambda i: (i, 2)), pl.BlockSpec((tm, 1), row),
                  _full((1, 256)), _full((1, 256)), _full(wq_ext.shape), _full(wkv_ext.shape),
                  _full((1, 256)), _full((1, 256))],
        out_specs=[pl.BlockSpec((HEADS, tm, QK_PAD), head), pl.BlockSpec((HEADS, tm, QK_PAD), head),
                   pl.BlockSpec((HEADS, tm, HEAD_DIM), head), pl.BlockSpec((tm, 256), row), pl.BlockSpec((tm, 256), row),
                   pl.BlockSpec((tm, 256), row), pl.BlockSpec((tm, 256), row)],
        out_shape=[jax.ShapeDtypeStruct((HEADS, T, QK_PAD), BF16), jax.ShapeDtypeStruct((HEADS, T, QK_PAD), BF16),
                   jax.ShapeDtypeStruct((HEADS, T, HEAD_DIM), BF16), jax.ShapeDtypeStruct((T, 256), F32),
                   jax.ShapeDtypeStruct((T, 256), F32), jax.ShapeDtypeStruct((T, 256), BF16),
                   jax.ShapeDtypeStruct((T, 256), BF16)], exchange=exchange)


def _mla_bwd(dq, dk, dv, z, c1, s1, wq_ext, wkv_ext, qnw, kvnw):
    T = z.shape[0]
    tm = min(ROW_TILE, T)

    def body(dq_ref, dk_ref, dv_ref, z_ref, c1_ref, s1_ref, wq_ref, wkv_ref, qnw_ref, kvnw_ref,
             dz_ref, dqe_ref, dkve_ref, dnw_ref):
        @pl.when(pl.program_id(0) == 0)
        def _():
            dnw_ref[...] = jnp.zeros_like(dnw_ref)

        c1, s1 = c1_ref[...], s1_ref[...]
        dkpe = jnp.zeros((tm, QK_PAD), F32)
        for h in range(HEADS):
            dqh, dkh = dq_ref[h].astype(F32) * ATT_SCALE, dk_ref[h]
            dqe_ref[:, 256 * h:256 * h + 256] = (dqh * c1).astype(BF16)
            dqe_ref[:, 1024 + 256 * h:1280 + 256 * h] = (dqh * s1).astype(BF16)
            dkve_ref[:, 256 * h:256 * h + 256] = dkh
            dkve_ref[:, 1024 + 128 * h:1152 + 128 * h] = dv_ref[h]
            dkpe = dkpe + dkh.astype(F32)
        dcqn = _dot(dqe_ref[...], wq_ref[...])
        dckvn = _dot(dkve_ref[...], wkv_ref[...], NT)
        cq, ckv = z_ref[:, 0:256], z_ref[:, 256:512]
        _, rsq = _rms_fwd(cq, qnw_ref[...])
        _, rskv = _rms_fwd(ckv, kvnw_ref[...])
        dcq, wq_rows = _rms_bwd(cq, rsq, qnw_ref[...], dcqn)
        dckv, wkv_rows = _rms_bwd(ckv, rskv, kvnw_ref[...], dckvn)
        dnw_ref[:, 0:256] += _rowsum(wq_rows)
        dnw_ref[:, 256:512] += _rowsum(wkv_rows)
        dz_ref[:, 0:256] = dcq.astype(BF16)
        dz_ref[:, 256:512] = dckv.astype(BF16)
        dz_ref[:, 512:768] = (dkpe * c1).astype(BF16)
        dz_ref[:, 768:1024] = (dkpe * s1).astype(BF16)

    row = lambda i: (i, 0)
    head = lambda i: (0, i, 0)
    return pl.pallas_call(
        body, name="mla_bwd", grid=(T // tm,),
        in_specs=[pl.BlockSpec((HEADS, tm, QK_PAD), head), pl.BlockSpec((HEADS, tm, QK_PAD), head),
                  pl.BlockSpec((HEADS, tm, HEAD_DIM), head), pl.BlockSpec((tm, 1024), lambda i: (i, 2)),
                  pl.BlockSpec((tm, 256), row), pl.BlockSpec((tm, 256), row), _full(wq_ext.shape), _full(wkv_ext.shape),
                  _full((1, 256)), _full((1, 256))],
        out_specs=[pl.BlockSpec((tm, 1024), row), pl.BlockSpec((tm, 2048), row), pl.BlockSpec((tm, 1536), row),
                   _full((1, 512))],
        out_shape=[jax.ShapeDtypeStruct((T, 1024), BF16), jax.ShapeDtypeStruct((T, 2048), BF16),
                   jax.ShapeDtypeStruct((T, 1536), BF16), jax.ShapeDtypeStruct((1, 512), F32)],
        compiler_params=_params(),
    )(dq, dk, dv, z, c1, s1, wq_ext, wkv_ext, qnw, kvnw)


_HEAD_LANES = [slice(HEAD_DIM * h, HEAD_DIM * (h + 1)) for h in range(HEADS)]


def _lower_bound(lbraw_ref):
    a0, a1 = lbraw_ref[0:1, :], lbraw_ref[1:2, :]
    mx = jnp.maximum(a0, a1)
    e0, e1 = jnp.exp(a0 - mx), jnp.exp(a1 - mx)
    return e0 / (e0 + e1)


def _tri(lower):
    r = lax.broadcasted_iota(jnp.int32, (CHUNK, CHUNK), 0)
    c = lax.broadcasted_iota(jnp.int32, (CHUNK, CHUNK), 1)
    return (r >= c) if lower else (r <= c)


def _hgrn_gates(q, f, lb, tri_lo):
    sg = _sigmoid(f)
    forget = lb + (1.0 - lb) * sg
    k = 1.0 - forget
    b = _hdot(tri_lo.astype(F32), jnp.log(forget))
    b_ref, b_last = b[CHUNK // 2 - 1:CHUNK // 2, :], b[CHUNK - 1:CHUNK, :]
    e1, e2, e3, e4 = jnp.exp(b - b_ref), jnp.exp(b_ref - b), jnp.exp(b_last - b), jnp.exp(b)
    return dict(sg=sg, forget=forget, k=k, e1=e1, e2=e2, e3=e3, e4=e4, qa=q * e1, ka=k * e2, kl=k * e3, qb=q * e4,
                decay=jnp.exp(b_last))


def _hgrn_fwd(z, lbraw, nw, exchange=None):
    T = z.shape[0]
    G = min(HGRN_GROUP, T // CHUNK)
    rows = G * CHUNK
    n_chunks = T // CHUNK

    def body(q_ref, f_ref, i_ref, g_ref, lbraw_ref, nw_ref, oraw_ref, og_ref, sp_ref, st_ref):
        @pl.when(pl.program_id(0) == 0)
        def _():
            st_ref[...] = jnp.zeros_like(st_ref)

        lb_all = _lower_bound(lbraw_ref)
        tri_lo = _tri(True)

        def chunk(cc, carry):
            rs = pl.ds(pl.multiple_of(cc * CHUNK, CHUNK), CHUNK)
            t = _hgrn_gates(q_ref[rs, :], f_ref[rs, :], lb_all, tri_lo)
            v, gate = i_ref[rs, :], g_ref[rs, :]
            st = [st_ref[h] for h in range(HEADS)]
            a = [jnp.where(tri_lo, _bdot(t["qa"][:, s], t["ka"][:, s], NT), 0.0) for s in _HEAD_LANES]
            kv = [_bdot(v[:, s], t["kl"][:, s], TN) for s in _HEAD_LANES]
            o = [_bdot(a[h], v[:, s]) + _bdot(t["qb"][:, s], st[h], NT) for h, s in enumerate(_HEAD_LANES)]
            for h, s in enumerate(_HEAD_LANES):
                sp_ref[cc, h] = st[h]
                st_ref[h] = st[h] * t["decay"][:, s] + kv[h]
            oraw_ref[rs, :] = jnp.concatenate(o, axis=1)
            on = jnp.concatenate([_rms_fwd(o[h], nw_ref[:, s])[0] for h, s in enumerate(_HEAD_LANES)], axis=1)
            og_ref[rs, :] = (on * (gate * _sigmoid(gate))).astype(BF16)
            return carry

        lax.fori_loop(0, G, chunk, 0, unroll=4)

    col = lambda j: pl.BlockSpec((rows, 512), lambda r, j=j: (r, j))
    return _call(
        body, "hgrn_fwd", (z, z, z, z, lbraw, nw), grid=(T // rows,),
        in_specs=[col(0), col(1), col(2), col(3), _full((2, 512)), _full((1, 512))],
        out_specs=[col(0), col(0), pl.BlockSpec((G, HEADS, HEAD_DIM, HEAD_DIM), lambda r: (r, 0, 0, 0))],
        out_shape=[jax.ShapeDtypeStruct((T, 512), F32), jax.ShapeDtypeStruct((T, 512), BF16),
                   jax.ShapeDtypeStruct((n_chunks, HEADS, HEAD_DIM, HEAD_DIM), F32)],
        scratch_shapes=[pltpu.VMEM((HEADS, HEAD_DIM, HEAD_DIM), F32)], exchange=exchange)


def _hgrn_bwd(dmixcat, z, oraw, sprev, lbraw, nw, exchange=None):
    T = z.shape[0]
    G = min(HGRN_GROUP, T // CHUNK)
    rows = G * CHUNK
    ng = T // rows

    def body(dog_ref, q_ref, f_ref, i_ref, g_ref, oraw_ref, sp_ref, lbraw_ref, nw_ref,
             dz_ref, dsmall_ref, dst_ref):
        @pl.when(pl.program_id(0) == 0)
        def _():
            dst_ref[...] = jnp.zeros_like(dst_ref)
            dsmall_ref[...] = jnp.zeros_like(dsmall_ref)

        lb_all = _lower_bound(lbraw_ref)
        tri_lo, tri_up = _tri(True), _tri(False)
        rowid = lax.broadcasted_iota(jnp.int32, (CHUNK, HEADS * HEAD_DIM), 0)

        def chunk(it, carry):
            cc = G - 1 - it
            rs = pl.ds(pl.multiple_of(cc * CHUNK, CHUNK), CHUNK)
            heads = list(enumerate(_HEAD_LANES))
            cat = lambda parts: jnp.concatenate(parts, axis=1)
            per_head_mean = lambda x: cat([jnp.broadcast_to(_lanemean(x[:, s]), (CHUNK, HEAD_DIM)) for s in _HEAD_LANES])
            t = _hgrn_gates(q_ref[rs, :], f_ref[rs, :], lb_all, tri_lo)
            v, gate, o, dog, nw_all = i_ref[rs, :], g_ref[rs, :], oraw_ref[rs, :], dog_ref[rs, :], nw_ref[...]
            rs_o = lax.rsqrt(per_head_mean(o * o) + RMS_EPS)
            xhat = o * rs_o
            sgg = _sigmoid(gate)
            d_on = dog * (gate * sgg)
            dz_ref[rs, 1536:2048] = (dog * (xhat * nw_all) * (sgg * (1.0 + gate * (1.0 - sgg)))).astype(BF16)
            dxh = d_on * nw_all
            do = rs_o * (dxh - xhat * per_head_mean(dxh * xhat))
            dsmall_ref[:, 512:1024] += _rowsum(d_on * xhat)
            st = [sp_ref[cc, h] for h in range(HEADS)]
            dst = [dst_ref[h] for h in range(HEADS)]
            a = [jnp.where(tri_lo, _bdot(t["qa"][:, s], t["ka"][:, s], NT), 0.0) for s in _HEAD_LANES]
            da = [jnp.where(tri_lo, _bdot(do[:, s], v[:, s], NT), 0.0) for s in _HEAD_LANES]
            dqb = cat([_bdot(do[:, s], st[h]) for h, s in heads])
            dkl = cat([_bdot(v[:, s], dst[h]) for h, s in heads])
            dv_ = cat([_bdot(t["kl"][:, s], dst[h], NT) + _bdot(a[h], do[:, s], TN) for h, s in heads])
            dqa = cat([_bdot(da[h], t["ka"][:, s]) for h, s in heads])
            dka = cat([_bdot(da[h], t["qa"][:, s], TN) for h, s in heads])
            ddecay = cat([_rowsum(dst[h] * st[h]) for h in range(HEADS)])
            for h, s in heads:
                dst_ref[h] = dst[h] * t["decay"][:, s] + _bdot(do[:, s], t["qb"][:, s], TN)
            pa, pk, pb, pl_ = dqa * t["qa"], dka * t["ka"], dqb * t["qb"], dkl * t["kl"]
            db = pa - pk + pb - pl_
            db = db + jnp.where(rowid == CHUNK // 2 - 1, _rowsum(pk - pa), 0.0)
            db = db + jnp.where(rowid == CHUNK - 1, _rowsum(pl_) + ddecay * t["decay"], 0.0)
            dlogf = _hdot(tri_up.astype(F32), db)
            dforget = dlogf / t["forget"] - (dka * t["e2"] + dkl * t["e3"])
            sg = t["sg"]
            dz_ref[rs, 0:512] = (dqa * t["e1"] + dqb * t["e4"]).astype(BF16)
            dz_ref[rs, 512:1024] = (dforget * (1.0 - lb_all) * sg * (1.0 - sg)).astype(BF16)
            dz_ref[rs, 1024:1536] = dv_.astype(BF16)
            dsmall_ref[:, 0:512] += _rowsum(dforget * (1.0 - sg))
            return carry

        lax.fori_loop(0, G, chunk, 0, unroll=4)

    col = lambda j: pl.BlockSpec((rows, 512), lambda r, j=j: (ng - 1 - r, j))
    return _call(
        body, "hgrn_bwd", (dmixcat, z, z, z, z, oraw, sprev, lbraw, nw), grid=(ng,),
        in_specs=[col(0), col(0), col(1), col(2), col(3), col(0),
                  pl.BlockSpec((G, HEADS, HEAD_DIM, HEAD_DIM), lambda r: (ng - 1 - r, 0, 0, 0)),
                  _full((2, 512)), _full((1, 512))],
        out_specs=[pl.BlockSpec((rows, 2048), lambda r: (ng - 1 - r, 0)), _full((1, 1024))],
        out_shape=[jax.ShapeDtypeStruct((T, 2048), BF16), jax.ShapeDtypeStruct((1, 1024), F32)],
        scratch_shapes=[pltpu.VMEM((HEADS, HEAD_DIM, HEAD_DIM), F32)], exchange=exchange)


def _diag_mask(t):
    r = lax.broadcasted_iota(jnp.int32, (t, t), 0)
    c = lax.broadcasted_iota(jnp.int32, (t, t), 1)
    return r >= c


def _attn_fwd(q, k, v, exchange=None):
    _, T, _ = q.shape
    t = min(ATT_TILE, T)

    def body(q_ref, k_ref, v_ref, o_ref, lse_ref):
        i = pl.program_id(1)
        qb = q_ref[...]

        rows = lambda j: pl.ds(pl.multiple_of(j * t, t), t)

        def logits(j, masked):
            s = _dot(qb, k_ref[rows(j), :], NT)
            return jnp.where(_diag_mask(t), s, NEG_BIG) if masked else s

        def absorb(s, j, carry):
            m, l, acc = carry
            mn = jnp.maximum(m, jnp.max(s, axis=-1, keepdims=True))
            p = jnp.exp2(s - mn)
            al = jnp.exp2(m - mn)
            return mn, al * l + jnp.sum(p, axis=-1, keepdims=True), al * acc + _dot(p.astype(BF16), v_ref[rows(j), :])

        def pair(j0, carry, last_masked):
            s0, s1 = logits(j0, False), logits(j0 + 1, last_masked)
            return absorb(s1, j0 + 1, absorb(s0, j0, carry))

        init = (jnp.full((t, 1), NEG_BIG, F32), jnp.zeros((t, 1), F32), jnp.zeros((t, HEAD_DIM), F32))
        carry = lax.fori_loop(0, i // 2, lambda jj, c: pair(2 * jj, c, False), init)
        m, l, acc = lax.cond(i % 2 == 1, lambda c: pair(i - 1, c, True),
                             lambda c: absorb(logits(i, True), i, c), carry)
        o_ref[...] = acc / l
        lse_ref[...] = jnp.broadcast_to(m + jnp.log2(l), (t, HEAD_DIM))

    return _call(
        body, "attn_fwd", (q, k, v), grid=(HEADS, T // t),
        in_specs=[pl.BlockSpec((None, t, QK_PAD), lambda h, i: (h, i, 0)),
                  pl.BlockSpec((None, T, QK_PAD), lambda h, i: (h, 0, 0)),
                  pl.BlockSpec((None, T, HEAD_DIM), lambda h, i: (h, 0, 0))],
        out_specs=[pl.BlockSpec((t, HEAD_DIM), lambda h, i: (i, h)),
                   pl.BlockSpec((None, t, HEAD_DIM), lambda h, i: (h, i, 0))],
        out_shape=[jax.ShapeDtypeStruct((T, HEADS * HEAD_DIM), F32), jax.ShapeDtypeStruct((HEADS, T, HEAD_DIM), F32)],
        exchange=exchange)


def _attn_bwd(q, k, v, dmixcat, o, lse, exchange=None):
    _, T, _ = q.shape
    t = min(ATT_TILE, T)
    nq = T // t

    def body(q_ref, k_ref, v_ref, do_ref, o_ref, lse_ref, dq_ref, dk_ref, dv_ref, delta_ref, dq_acc):
        j = pl.program_id(1)

        @pl.when(j == 0)
        def _():
            dq_acc[...] = jnp.zeros_like(dq_acc)

            def fill(i, carry):
                rs = pl.ds(pl.multiple_of(i * t, t), t)
                delta_ref[rs, :] = jnp.broadcast_to(
                    jnp.sum(do_ref[rs, :] * o_ref[rs, :], axis=-1, keepdims=True), (t, HEAD_DIM))
                return carry

            lax.fori_loop(0, nq, fill, 0)

        kb, vb = k_ref[...], v_ref[...]

        def steps(blocks, carry):
            dk, dv = carry
            rs = [pl.ds(pl.multiple_of(i * t, t), t) for i, _ in blocks]
            qb = [q_ref[r, :] for r in rs]
            dob = [do_ref[r, :].astype(BF16) for r in rs]
            s = [_dot(b, kb, NT) for b in qb]
            dp = [_dot(b, vb, NT) for b in dob]
            for n, (_, shift) in enumerate(blocks):
                p = jnp.exp2(s[n] - lse_ref[rs[n], 0:1])
                if shift is not None:
                    row = lax.broadcasted_iota(jnp.int32, (t, 2 * t), 0)
                    col = lax.broadcasted_iota(jnp.int32, (t, 2 * t), 1)
                    p = jnp.where(col <= row + shift, p, 0.0)
                ds = (p * (dp[n] - delta_ref[rs[n], 0:1])).astype(BF16)
                dq_acc[rs[n], :] += _dot(ds, kb)
                dk = dk + _dot(ds, qb[n], TN)
                dv = dv + _dot(p.astype(BF16), dob[n], TN)
            return dk, dv

        zero = (jnp.zeros((2 * t, QK_PAD), F32), jnp.zeros((2 * t, HEAD_DIM), F32))
        carry = steps([(2 * j, 0), (2 * j + 1, t)], zero)
        first = 2 * j + 2
        dk, dv = lax.fori_loop(0, (nq - first) // 2, lambda n, c: steps([(first + 2 * n, None), (first + 2 * n + 1, None)], c),
                               carry)
        dk_ref[...] = (dk * LN2).astype(BF16)
        dv_ref[...] = dv.astype(BF16)

        @pl.when(j == nk - 1)
        def _():
            dq_ref[...] = dq_acc[...].astype(BF16)

    nk = nq // 2
    return _call(
        body, "attn_bwd", (q, k, v, dmixcat, o, lse), grid=(HEADS, nk),
        in_specs=[pl.BlockSpec((None, T, QK_PAD), lambda h, j: (h, 0, 0)),
                  pl.BlockSpec((None, 2 * t, QK_PAD), lambda h, j: (h, j, 0)),
                  pl.BlockSpec((None, 2 * t, HEAD_DIM), lambda h, j: (h, j, 0)),
                  pl.BlockSpec((T, HEAD_DIM), lambda h, j: (0, HEADS + h)),
                  pl.BlockSpec((T, HEAD_DIM), lambda h, j: (0, h)),
                  pl.BlockSpec((None, T, HEAD_DIM), lambda h, j: (h, 0, 0))],
        out_specs=[pl.BlockSpec((None, T, QK_PAD), lambda h, j: (h, 0, 0)),
                   pl.BlockSpec((None, 2 * t, QK_PAD), lambda h, j: (h, j, 0)),
                   pl.BlockSpec((None, 2 * t, HEAD_DIM), lambda h, j: (h, j, 0))],
        out_shape=[jax.ShapeDtypeStruct((HEADS, T, QK_PAD), BF16), jax.ShapeDtypeStruct((HEADS, T, QK_PAD), BF16),
                   jax.ShapeDtypeStruct((HEADS, T, HEAD_DIM), BF16)],
        scratch_shapes=[pltpu.VMEM((T, HEAD_DIM), F32), pltpu.VMEM((T, QK_PAD), F32)], exchange=exchange)


def _ln_fwd(r):
    mu = _lanemean(r)
    xc = r - mu
    rstd = lax.rsqrt(_lanemean(xc * xc) + LN_EPS)
    return xc * rstd, rstd


def _ln_bwd(dxh, xhat, rstd):
    return rstd * (dxh - _lanemean(dxh) - xhat * _lanemean(dxh * xhat))


def _mix_ln1(o_hg, o_mla, w_out, x, g_a, ln1_g, ln1_b, sc_m, sh_m, exchange=None):
    T = x.shape[0]
    tm = min(ROW_TILE, T)
    half = o_hg.shape[1]

    def body(hg_ref, mla_ref, w_ref, x_ref, ga_ref, g_ref, b_ref, sc_ref, sh_ref, mix_ref, xhat_ref, rstd_ref, u2_ref):
        mix = _dot(hg_ref[...], w_ref[0:half, :]) + _bdot(mla_ref[...], w_ref[half:, :])
        mix_ref[...] = mix
        xhat, rstd = _ln_fwd(ALPHA * x_ref[...] + (1.0 + ga_ref[...]) * mix)
        xhat_ref[...] = xhat
        rstd_ref[...] = jnp.broadcast_to(rstd, (tm, 128))
        u2_ref[...] = _modulate(xhat * g_ref[...] + b_ref[...], sc_ref[...], sh_ref[...]).astype(BF16)

    row = pl.BlockSpec((tm, D_MODEL), lambda i: (i, 0))
    vec = _full((1, D_MODEL))
    halfrow = pl.BlockSpec((tm, half), lambda i: (i, 0))
    return _call(
        body, "mix_ln1", (o_hg, o_mla, w_out, x, g_a, ln1_g, ln1_b, sc_m, sh_m), grid=(T // tm,),
        in_specs=[halfrow, halfrow, _full(w_out.shape), row, vec, vec, vec, vec, vec],
        out_specs=[row, row, pl.BlockSpec((tm, 128), lambda i: (i, 0)), row],
        out_shape=[jax.ShapeDtypeStruct((T, D_MODEL), F32), jax.ShapeDtypeStruct((T, D_MODEL), F32),
                   jax.ShapeDtypeStruct((T, 128), F32), jax.ShapeDtypeStruct((T, D_MODEL), BF16)],
        exchange=exchange)


def _mlp_fwd(u2, w1, w2, xhat1, ln1_g, ln1_b, g_m, ln2_g, ln2_b, target):
    T = u2.shape[0]
    half, tf = w1[0].shape[1:]
    nf = N_DEV // MLP_SLABS
    tm = min(ROW_TILE, T)

    def body(u2_ref, w1a_ref, w1b_ref, w2_ref, xhat_ref, g1_ref, b1_ref, gm_ref, g2_ref, b2_ref, tgt_ref,
             r_ref, dr2_ref, dh_ref, small_ref, acc_ref):
        i, f = pl.program_id(0), pl.program_id(1)
        dm = D_MODEL

        @pl.when((i == 0) & (f == 0))
        def _():
            small_ref[...] = jnp.zeros_like(small_ref)

        @pl.when(f == 0)
        def _():
            acc_ref[...] = jnp.zeros_like(acc_ref)

        u2t = u2_ref[...]
        part = None
        for s in range(MLP_SLABS):
            r = jnp.maximum(_dot(u2t[:, :half], w1a_ref[s]) + _dot(u2t[:, half:], w1b_ref[s]), 0.0)
            r_ref[:, s * tf:(s + 1) * tf] = r.astype(BF16)
            d = _bdot(r * r, w2_ref[s])
            part = d if part is None else part + d
        acc_ref[...] += part

        @pl.when(f == nf - 1)
        def _():
            h = acc_ref[...]
            x1 = xhat_ref[...] * g1_ref[...] + b1_ref[...]
            xhat2, rstd2 = _ln_fwd(ALPHA * x1 + (1.0 + gm_ref[...]) * h)
            err = xhat2 * g2_ref[...] + b2_ref[...] - tgt_ref[...]
            small_ref[:, 3 * dm:] += jnp.sum(0.5 * _lanemean(err * err), axis=0, keepdims=True)
            dy = err * (1.0 / D_MODEL)
            small_ref[:, dm:2 * dm] += _rowsum(dy * xhat2)
            small_ref[:, 2 * dm:3 * dm] += _rowsum(dy)
            dr2 = _ln_bwd(dy * g2_ref[...], xhat2, rstd2)
            dr2_ref[...] = dr2
            small_ref[:, 0:dm] += _rowsum(dr2 * h)
            dh_ref[...] = ((1.0 + gm_ref[...]) * dr2).astype(BF16)

    row = pl.BlockSpec((tm, D_MODEL), lambda i, f: (i, 0))
    vec = _full((1, D_MODEL))
    return pl.pallas_call(
        body, name="mlp_fwd", grid=(T // tm, nf),
        in_specs=[row, pl.BlockSpec((MLP_SLABS, half, tf), lambda i, f: (f, 0, 0)),
                  pl.BlockSpec((MLP_SLABS, half, tf), lambda i, f: (f, 0, 0)),
                  pl.BlockSpec((MLP_SLABS, tf, D_MODEL), lambda i, f: (f, 0, 0)),
                  row, vec, vec, vec, vec, vec, row],
        out_specs=[pl.BlockSpec((tm, MLP_SLABS * tf), lambda i, f: (i, f)), row, row, _full((1, 3 * D_MODEL + 128))],
        out_shape=[jax.ShapeDtypeStruct((T, N_DEV * tf), BF16), jax.ShapeDtypeStruct((T, D_MODEL), F32),
                   jax.ShapeDtypeStruct((T, D_MODEL), BF16), jax.ShapeDtypeStruct((1, 3 * D_MODEL + 128), F32)],
        scratch_shapes=[pltpu.VMEM((tm, D_MODEL), F32)],
        compiler_params=_params(),
    )(u2, w1[0], w1[1], w2, xhat1, ln1_g, ln1_b, g_m, ln2_g, ln2_b, target)


def _mlp_bwd(dh, w1, w2, r, dr2, xhat1, rstd1, mix, ln1_g, ln1_b, sc_m, g_a):
    T = dh.shape[0]
    half, tf = w1[0].shape[1:]
    nf = N_DEV // MLP_SLABS
    tm = min(ROW_TILE, T)

    def body(dh_ref, w1a_ref, w1b_ref, w2_ref, r_ref, dr2_ref, xhat_ref, rstd_ref, mix_ref, g1_ref, b1_ref, sc_ref, ga_ref,
             dhpre_ref, dr1_ref, dmix_ref, small_ref, acc_ref):
        i, f = pl.program_id(0), pl.program_id(1)
        dm = D_MODEL

        @pl.when((i == 0) & (f == 0))
        def _():
            small_ref[...] = jnp.zeros_like(small_ref)

        @pl.when(f == 0)
        def _():
            acc_ref[...] = jnp.zeros_like(acc_ref)

        dht = dh_ref[...]
        part = None
        for s in range(MLP_SLABS):
            cols = slice(s * tf, (s + 1) * tf)
            dhpre = (_dot(dht, w2_ref[s], NT) * (2.0 * r_ref[:, cols].astype(F32))).astype(BF16)
            dhpre_ref[:, cols] = dhpre
            d = jnp.concatenate([_dot(dhpre, w1a_ref[s], NT), _dot(dhpre, w1b_ref[s], NT)], axis=1)
            part = d if part is None else part + d
        acc_ref[...] += part

        @pl.when(f == nf - 1)
        def _():
            du2 = acc_ref[...]
            xhat = xhat_ref[...]
            x1 = xhat * g1_ref[...] + b1_ref[...]
            dx1 = ALPHA * dr2_ref[...] + du2 * (1.0 + sc_ref[...])
            small_ref[:, 2 * dm:3 * dm] += _rowsum(du2 * x1)
            small_ref[:, dm:2 * dm] += _rowsum(du2)
            small_ref[:, 3 * dm:4 * dm] += _rowsum(dx1 * xhat)
            small_ref[:, 4 * dm:5 * dm] += _rowsum(dx1)
            dr1 = _ln_bwd(dx1 * g1_ref[...], xhat, rstd_ref[:, 0:1])
            dr1_ref[...] = dr1
            small_ref[:, 0:dm] += _rowsum(dr1 * mix_ref[...])
            dmix_ref[...] = ((1.0 + ga_ref[...]) * dr1).astype(BF16)

    row = pl.BlockSpec((tm, D_MODEL), lambda i, f: (i, 0))
    vec = _full((1, D_MODEL))
    return pl.pallas_call(
        body, name="mlp_bwd", grid=(T // tm, nf),
        in_specs=[row, pl.BlockSpec((MLP_SLABS, half, tf), lambda i, f: (f, 0, 0)),
                  pl.BlockSpec((MLP_SLABS, half, tf), lambda i, f: (f, 0, 0)),
                  pl.BlockSpec((MLP_SLABS, tf, D_MODEL), lambda i, f: (f, 0, 0)),
                  pl.BlockSpec((tm, MLP_SLABS * tf), lambda i, f: (i, f)), row, row,
                  pl.BlockSpec((tm, 128), lambda i, f: (i, 0)), row, vec, vec, vec, vec],
        out_specs=[pl.BlockSpec((tm, MLP_SLABS * tf), lambda i, f: (i, f)), row, row, _full((1, 5 * D_MODEL))],
        out_shape=[jax.ShapeDtypeStruct((T, N_DEV * tf), BF16), jax.ShapeDtypeStruct((T, D_MODEL), F32),
                   jax.ShapeDtypeStruct((T, D_MODEL), BF16), jax.ShapeDtypeStruct((1, 5 * D_MODEL), F32)],
        scratch_shapes=[pltpu.VMEM((tm, D_MODEL), F32)],
        compiler_params=_params(),
    )(dh, w1[0], w1[1], w2, r, dr2, xhat1, rstd1, mix, ln1_g, ln1_b, sc_m, g_a)


def _input_bwd(dz_h, dz_m, w_in_ext, x, dr1, sc_a, exchange=None):
    T = x.shape[0]
    tm = min(ROW_TILE, T)

    def body(dzh_ref, dzm_ref, w_ref, x_ref, dr1_ref, sc_ref, gx_ref, small_ref):
        @pl.when(pl.program_id(0) == 0)
        def _():
            small_ref[...] = jnp.zeros_like(small_ref)

        du = _bdot(dzh_ref[...], w_ref[0:2048, :]) + _bdot(dzm_ref[...], w_ref[2048:3072, :])
        gx_ref[...] = ALPHA * dr1_ref[...] + du * (1.0 + sc_ref[...])
        small_ref[:, D_MODEL:] += _rowsum(du * x_ref[...])
        small_ref[:, 0:D_MODEL] += _rowsum(du)

    row = pl.BlockSpec((tm, D_MODEL), lambda i: (i, 0))
    vec = _full((1, D_MODEL))
    return _call(
        body, "input_bwd", (dz_h, dz_m, w_in_ext, x, dr1, sc_a), grid=(T // tm,),
        in_specs=[pl.BlockSpec((tm, 2048), lambda i: (i, 0)), row, _full(w_in_ext.shape), row, row, vec],
        out_specs=[row, _full((1, 2 * D_MODEL))],
        out_shape=[jax.ShapeDtypeStruct((T, D_MODEL), F32), jax.ShapeDtypeStruct((1, 2 * D_MODEL), F32)],
        exchange=exchange)


def _adam_math(w, g, m, v):
    m = ADAM_B1 * m + (1.0 - ADAM_B1) * g
    v = ADAM_B2 * v + (1.0 - ADAM_B2) * (g * g)
    m_hat = m / (1.0 - ADAM_B1 ** ADAM_STEP)
    v_hat = v / (1.0 - ADAM_B2 ** ADAM_STEP)
    return -ADAM_LR * (m_hat / (jnp.sqrt(v_hat) + ADAM_EPS) + ADAM_WD * w), m, v


def _adam(g_slabs, w, m, v, name, g_fn=None, g_extra=()):
    R, C = w.shape
    tr = 256 if R % 256 == 0 else R
    ns = 0 if g_slabs is None else g_slabs.shape[0]
    slab_rows = tr if g_slabs is None or g_slabs.shape[1] == R else g_slabs.shape[1]
    assert slab_rows == tr or tr == R
    ne = len(g_extra)

    def body(*refs):
        e_refs = refs[:ne]
        refs = refs[ne:]
        if ns:
            gs_ref, refs = refs[0], refs[1:]
        w_ref, m_ref, v_ref, g_ref, d_ref, nm_ref, nv_ref = refs
        if g_fn is not None:
            g = g_fn(*e_refs)
        else:
            g = gs_ref[0].astype(F32)
            for s in range(1, ns):
                g = g + gs_ref[s].astype(F32)
            g = g[:tr]
        d, nm, nv = _adam_math(w_ref[...], g, m_ref[...], v_ref[...])
        g_ref[...] = g
        d_ref[...] = d
        nm_ref[...] = nm
        nv_ref[...] = nv

    blk = pl.BlockSpec((tr, C), lambda i: (i, 0))
    in_specs = [pl.BlockSpec((tr, e.shape[1]), lambda i: (i, 0)) if e.shape[0] == R else _full(e.shape) for e in g_extra]
    args = list(g_extra)
    if ns:
        in_specs.append(pl.BlockSpec((ns, slab_rows, C), lambda i: (0, i, 0)))
        args.append(g_slabs)
    return pl.pallas_call(
        body, name=name, grid=(R // tr,), in_specs=in_specs + [blk] * 3, out_specs=[blk] * 4,
        out_shape=[jax.ShapeDtypeStruct((R, C), F32)] * 4, compiler_params=_params(),
    )(*args, w, m, v)


def _adam_small(small_all, params):
    n = len(params)

    def body(*refs):
        s_ref, refs = refs[0], refs[1:]
        wmv, loss_ref, outs = refs[:3 * n], refs[3 * n], refs[3 * n + 1:]
        tot = s_ref[0]
        for i in range(1, N_DEV):
            tot = tot + s_ref[i]
        loss_ref[...] = tot[:, SMALL_W - 128:]
        for j, (w, _, _, off) in enumerate(params):
            w_ref, m_ref, v_ref = wmv[3 * j:3 * j + 3]
            g_ref, d_ref, nm_ref, nv_ref = outs[4 * j:4 * j + 4]
            if w.shape[0] == 2:
                lb = _lower_bound(w_ref)
                g0 = tot[:, off:off + w.shape[1]] * lb * (1.0 - lb)
                rows = [(slice(0, 1), g0), (slice(1, 2), -g0)]
            else:
                rows = [(slice(0, 1), tot[:, off:off + w.shape[1]])]
            for rs, g in rows:
                d, nm, nv = _adam_math(w_ref[rs, :], g, m_ref[rs, :], v_ref[rs, :])
                g_ref[rs, :], d_ref[rs, :], nm_ref[rs, :], nv_ref[rs, :] = g, d, nm, nv

    out_shape = [jax.ShapeDtypeStruct((1, 128), F32)]
    for w, _, _, _ in params:
        out_shape += [jax.ShapeDtypeStruct(w.shape, F32)] * 4
    res = pl.pallas_call(body, name="adam_small", out_shape=out_shape, compiler_params=_params())(
        small_all, *[a for w, m, v, _ in params for a in (w, m, v)])
    return res[0], [tuple(res[1 + 4 * j:5 + 4 * j]) for j in range(n)]


def _cols_from_slabs(g):
    s, r, c = g.shape
    return jnp.transpose(g, (1, 0, 2)).reshape(r, s * c)


def _slabs_from_cols(w):
    r, c = w.shape
    return jnp.transpose(w.reshape(r, N_DEV, c // N_DEV), (1, 0, 2))


def _rot_half_rows(wt):
    return jnp.concatenate([-wt[32:], wt[:32]], axis=0)


def _unrot_half_rows(dwt_rot):
    return jnp.concatenate([dwt_rot[32:], -dwt_rot[:32]], axis=0)


def _ext_in_t(g):
    n, rows, k_in = g.shape
    keep = n * rows - ROPE_DIM

    def body(g_ref, o_ref, stage_ref):
        stage_ref[keep:, :] = jnp.zeros((o_ref.shape[0] - keep, k_in), F32)
        for i in range(n - 1):
            stage_ref[rows * i:rows * (i + 1), :] = g_ref[i].astype(F32)
        last = g_ref[n - 1].astype(F32)
        stage_ref[rows * (n - 1):keep, :] = last[:rows - ROPE_DIM]
        wk = last[rows - ROPE_DIM:]
        stage_ref[keep + 128:keep + 192, :] = wk
        stage_ref[keep + 384:keep + 416, :] = -wk[32:]
        stage_ref[keep + 416:keep + 448, :] = wk[:32]
        o_ref[...] = stage_ref[...].astype(BF16)

    return pl.pallas_call(body, name="ext_w_in", out_shape=jax.ShapeDtypeStruct((keep + 512, k_in), BF16),
                          scratch_shapes=[pltpu.VMEM((keep + 512, k_in), F32)], compiler_params=_params())(g)


def _ext_q_t(wt):
    r = wt.shape[1]
    z64, z128 = jnp.zeros((64, r), BF16), jnp.zeros((128, r), BF16)
    per = HEAD_DIM + ROPE_DIM
    main = [jnp.concatenate([wt[per * h:per * (h + 1)], z64], axis=0) for h in range(HEADS)]
    rot = [jnp.concatenate([z128, _rot_half_rows(wt[per * h + HEAD_DIM:per * (h + 1)]), z64], axis=0)
           for h in range(HEADS)]
    return jnp.concatenate(main + rot, axis=0)


def _ext_kv(w_kv_up):
    r = w_kv_up.shape[0]
    z128 = jnp.zeros((r, 128), BF16)
    wkv = w_kv_up.reshape(r, HEADS, 2 * HEAD_DIM)
    kpad = [jnp.concatenate([wkv[:, h, :HEAD_DIM], z128], axis=1) for h in range(HEADS)]
    vals = [wkv[:, h, HEAD_DIM:] for h in range(HEADS)]
    return jnp.concatenate(kpad + vals, axis=1)


def _w_in_grad_slabs(dwt_h, dwt_m):
    d = dwt_h.shape[1]
    rows = (dwt_h.shape[0] + 512 + ROPE_DIM) // N_DEV
    padded = rows + (-rows) % 16

    def body(h_ref, m_ref, o_ref, stage_ref):
        stage_ref[0:2048, :] = h_ref[...]
        stage_ref[2048:2560, :] = m_ref[0:512, :]
        rot = m_ref[768 + 128:768 + 192, :]
        stage_ref[2560:2592, :] = m_ref[640:672, :] + rot[32:]
        stage_ref[2592:2624, :] = m_ref[672:704, :] - rot[:32]
        zero = jnp.zeros((padded - rows, d), F32)
        for i in range(N_DEV):
            o_ref[i] = jnp.concatenate([stage_ref[rows * i:rows * (i + 1), :], zero], axis=0).astype(BF16)

    return pl.pallas_call(body, name="w_in_grad_slabs", out_shape=jax.ShapeDtypeStruct((N_DEV, padded, d), BF16),
                          scratch_shapes=[pltpu.VMEM((N_DEV * rows, d), F32)], compiler_params=_params())(dwt_h, dwt_m)


def _grad_q_from_ext_t(dwq_ext_t):
    rows = []
    for h in range(HEADS):
        main, rot = dwq_ext_t[256 * h:256 * h + 256], dwq_ext_t[1024 + 256 * h:1280 + 256 * h]
        rows += [main[:128], main[128:192] + _unrot_half_rows(rot[128:192])]
    return jnp.concatenate(rows, axis=0)


def _grad_kv_from_ext(dwkv_ext):
    kvcols = []
    for h in range(HEADS):
        kvcols += [dwkv_ext[:, 256 * h:256 * h + 128], dwkv_ext[:, 1024 + 128 * h:1152 + 128 * h]]
    return jnp.concatenate(kvcols, axis=1)


SMALL_W = 6144 + 512 + 512 + 256 + 256 + 4 * 1024 + 128


def kernel(x, c, positions, w_ada, b_ada, w_in, hg_lower_bounds, hg_norm_w, mla_q_norm_w, w_q_up, mla_kv_norm_w, w_kv_up, w_out, ln1_g, ln1_b, w_mlp_in, w_mlp_out, ln2_g, ln2_b, loss_target, m_w_ada, m_b_ada, m_w_in, m_hg_lower_bounds, m_hg_norm_w, m_mla_q_norm_w, m_w_q_up, m_mla_kv_norm_w, m_w_kv_up, m_w_out, m_ln1_g, m_ln1_b, m_w_mlp_in, m_w_mlp_out, m_ln2_g, m_ln2_b, v_w_ada, v_b_ada, v_w_in, v_hg_lower_bounds, v_hg_norm_w, v_mla_q_norm_w, v_w_q_up, v_mla_kv_norm_w, v_w_kv_up, v_w_out, v_ln1_g, v_ln1_b, v_w_mlp_in, v_w_mlp_out, v_ln2_g, v_ln2_b):
    T = x.shape[1]
    me = 4 * lax.axis_index("x") + 2 * lax.axis_index("y") + lax.axis_index("c")
    xs, tgt = x[0], loss_target[0]
    transposed = ("w_in", "w_q_up")
    as_used = lambda n, a: a[0].T if n in transposed else a[0]
    big = {n: as_used(n, a) for n, a in dict(w_in=w_in, w_q_up=w_q_up, w_kv_up=w_kv_up, w_out=w_out,
                                              w_mlp_in=w_mlp_in, w_mlp_out=w_mlp_out).items()}
    names = list(big)

    bf = {n: big[n].astype(BF16) for n in ("w_in", "w_q_up", "w_kv_up", "w_out")}
    g_in, g_c = _gather_two_level([bf["w_in"], c], name="gather_w_in")
    c_all = g_c.reshape(N_DEV, D_MODEL)

    ada_cols = w_ada.shape[2]
    mod_part, cond = _mod_part(c_all, w_ada[0], lax.dynamic_slice(b_ada, (0, me * ada_cols), (1, ada_cols)))
    (mod_all,) = _exchange([mod_part], scatter=False, name="gather_mod")
    mod_row = lax.dynamic_slice(mod_all, (0, me, 0), (N_DEV, 1, ada_cols)).reshape(1, N_DEV * ada_cols)
    sh_a, sc_a, g_a, sh_m, sc_m, g_m = [mod_row[:, D_MODEL * i:D_MODEL * (i + 1)] for i in range(6)]

    w_in_ext = _ext_in_t(g_in)
    half = D_MODEL // 2
    z, (w1_top,) = _matmul(xs, w_in_ext, "NT", "in_proj", a_fn=_modulate, extras=(sc_a, sh_a), tn=3072,
                           exchange=_StagedGather(big["w_mlp_in"], rows=(0, half)))
    (o_raw, o_gated, s_prev), (g_q, g_kv, g_out) = _hgrn_fwd(
        z, hg_lower_bounds, hg_norm_w, exchange=_Exchange([bf["w_q_up"], bf["w_kv_up"], bf["w_out"]], False))
    wq_ext = _ext_q_t(g_q.reshape(N_DEV * g_q.shape[1], g_q.shape[2]))
    wkv_ext = _ext_kv(_cols_from_slabs(g_kv))
    w_out_full = g_out.reshape(D_MODEL, D_MODEL)
    inv_freq = 1.0 / (ROPE_THETA ** (jnp.arange(0, ROPE_DIM, 2, dtype=F32) / ROPE_DIM))
    zeros = lambda n: jnp.zeros((n,), F32)
    invf = jnp.concatenate([zeros(128), inv_freq, inv_freq, zeros(64)]).reshape(1, QK_PAD)
    m_rot = jnp.concatenate([zeros(128), jnp.ones((64,), F32), zeros(64)]).reshape(1, QK_PAD)
    q, k, v, c1, s1, cqn, ckvn = _mla_pre(z, positions.reshape(T, 1), invf, m_rot, wq_ext, wkv_ext,
                                          mla_q_norm_w, mla_kv_norm_w)[0]
    (o_mla, lse), (w1_bot, w2) = _attn_fwd(
        q, k, v, exchange=[_StagedGather(big["w_mlp_in"], 0.75, rows=(half, half)),
                           _StagedGather(big["w_mlp_out"], 0.75)])
    w1 = (w1_top, w1_bot)
    mix, xhat1, rstd1, u2 = _mix_ln1(o_gated, o_mla, w_out_full, xs, g_a, ln1_g, ln1_b, sc_m, sh_m)[0]
    r, dr2, dh, small_mlp_fwd = _mlp_fwd(u2, w1, w2, xhat1, ln1_g, ln1_b, g_m, ln2_g, ln2_b, tgt)

    dhpre, dr1, dmix, small_mlp_bwd = _mlp_bwd(dh, w1, w2, r, dr2, xhat1, rstd1, mix, ln1_g, ln1_b, sc_m, g_a)
    received = {}
    dw2 = _matmul(r, dh, "TN", "wgrad_mlp_out", out_dtype=BF16, a_fn=_square, tm=1024, tk=2048)
    dw1 = _matmul(u2, dhpre, "TN", "wgrad_mlp_in", out_dtype=BF16, tm=1024, tk=2048, out_slabs=N_DEV)
    dmixcat = _matmul(dmix, w_out_full, "NT", "dgrad_out", tm=1024)
    dw_out = jnp.concatenate([_matmul(o_gated, dmix, "TN", "wgrad_out_hg", out_dtype=BF16, tk=2048),
                              _matmul(o_mla, dmix, "TN", "wgrad_out_mla", out_dtype=BF16, tk=2048)], axis=0)
    (dz_h, small_hgrn), (received["w_mlp_in"],) = _hgrn_bwd(
        dmixcat, z, o_raw, s_prev, hg_lower_bounds, hg_norm_w, exchange=_StagedScatter(dw1, 0.1))
    (dq, dk, dv), (received["w_mlp_out"], received["w_out"]) = _attn_bwd(
        q, k, v, dmixcat, o_mla, lse,
        exchange=_Exchange([dw2.reshape(N_DEV, dw2.shape[0] // N_DEV, D_MODEL),
                            dw_out.reshape(N_DEV, D_MODEL // N_DEV, D_MODEL)], True))
    dz_m, dq_ext, dkv_ext, small_mla = _mla_bwd(dq, dk, dv, z, c1, s1, wq_ext, wkv_ext, mla_q_norm_w, mla_kv_norm_w)
    dwq_t = _grad_q_from_ext_t(_matmul(dq_ext, cqn, "TN", "wgrad_q_up", tm=1024, tk=2048))
    dwkv = _grad_kv_from_ext(_matmul(ckvn, dkv_ext, "TN", "wgrad_kv_up", tn=1536, tk=2048))
    qkv_slabs = [dwq_t.reshape((N_DEV, dwq_t.shape[0] // N_DEV, dwq_t.shape[1])).astype(BF16),
                 _slabs_from_cols(dwkv).astype(BF16)]
    dwt_h = _matmul(dz_h, xs, "TN", "wgrad_in_h", b_fn=_modulate, extras=(sc_a, sh_a), tm=1024, tk=2048)
    dwt_m = _matmul(dz_m, xs, "TN", "wgrad_in_m", b_fn=_modulate, extras=(sc_a, sh_a), tm=1024, tk=2048)
    in_slabs = _w_in_grad_slabs(dwt_h, dwt_m)
    (grad_x, small_in), (received["w_q_up"], received["w_kv_up"], received["w_in"]) = _input_bwd(
        dz_h, dz_m, w_in_ext, xs, dr1, sc_a,
        exchange=[_Exchange(qkv_slabs, True), _StagedScatter(in_slabs)])

    small = jnp.concatenate([small_in, small_mlp_bwd[:, :3 * D_MODEL], small_mlp_fwd[:, :D_MODEL], small_hgrn,
                             small_mla, small_mlp_bwd[:, 3 * D_MODEL:], small_mlp_fwd[:, D_MODEL:]], axis=1)
    assert small.shape == (1, SMALL_W)
    (small_all,) = _exchange([small], scatter=False, name="gather_small")

    moments = dict(w_in=(m_w_in, v_w_in), w_q_up=(m_w_q_up, v_w_q_up), w_kv_up=(m_w_kv_up, v_w_kv_up),
                   w_out=(m_w_out, v_w_out), w_mlp_in=(m_w_mlp_in, v_w_mlp_in), w_mlp_out=(m_w_mlp_out, v_w_mlp_out))
    res = {}
    for n in names:
        res[n] = _adam(received[n], big[n], as_used(n, moments[n][0]), as_used(n, moments[n][1]), name="adam_" + n)
    dmod_cols = lax.dynamic_slice(small_all.reshape(N_DEV, SMALL_W), (0, me * ada_cols), (N_DEV, ada_cols))
    cond_t = cond.T

    def ada_grad(ct_ref, dm_ref):
        g = ct_ref[:, 0:1] * dm_ref[0:1, :]
        for b in range(1, N_DEV):
            g = g + ct_ref[:, b:b + 1] * dm_ref[b:b + 1, :]
        return g

    res["w_ada"] = _adam(None, w_ada[0], m_w_ada[0], v_w_ada[0], name="adam_w_ada", g_fn=ada_grad,
                         g_extra=(cond_t, dmod_cols))

    small_params = [("b_ada", b_ada, m_b_ada, v_b_ada, 0),
                    ("hg_lower_bounds", hg_lower_bounds, m_hg_lower_bounds, v_hg_lower_bounds, 6144),
                    ("hg_norm_w", hg_norm_w, m_hg_norm_w, v_hg_norm_w, 6656),
                    ("mla_q_norm_w", mla_q_norm_w, m_mla_q_norm_w, v_mla_q_norm_w, 7168),
                    ("mla_kv_norm_w", mla_kv_norm_w, m_mla_kv_norm_w, v_mla_kv_norm_w, 7424),
                    ("ln1_g", ln1_g, m_ln1_g, v_ln1_g, 7680), ("ln1_b", ln1_b, m_ln1_b, v_ln1_b, 8704),
                    ("ln2_g", ln2_g, m_ln2_g, v_ln2_g, 9728), ("ln2_b", ln2_b, m_ln2_b, v_ln2_b, 10752)]
    loss_row, small_res = _adam_small(small_all, [p[1:] for p in small_params])
    for p, r4 in zip(small_params, small_res):
        res[p[0]] = r4
    loss = loss_row[0, 0]

    order = ["w_ada", "b_ada", "w_in", "hg_lower_bounds", "hg_norm_w", "mla_q_norm_w", "w_q_up", "mla_kv_norm_w",
             "w_kv_up", "w_out", "ln1_g", "ln1_b", "w_mlp_in", "w_mlp_out", "ln2_g", "ln2_b"]
    def as_given(n, a):
        if n in transposed:
            a = a.T
        return a[None] if n in big or n == "w_ada" else a

    shaped = {n: tuple(as_given(n, a) for a in res[n]) for n in order}
    outs = [loss, grad_x.reshape(1, T, D_MODEL)]
    for i in range(4):
        outs += [shaped[n][i] for n in order]
    return tuple(outs)
```

```python
import jax
import jax.numpy as jnp
import numpy as np
from jax import lax
from jax.experimental import pallas as pl
from jax.experimental.pallas import tpu as pltpu

F32, BF16 = jnp.float32, jnp.bfloat16
N_DEV = 8
D_MODEL = 1024
HEADS = 4
HEAD_DIM = 128
ROPE_DIM = 64
QK_PAD = 256
CHUNK = 64
ROPE_THETA = 10000.0
RMS_EPS = 1e-6
LN_EPS = 1e-5
ALPHA = 2.0 ** 0.25
ATT_SCALE = (HEAD_DIM + ROPE_DIM) ** -0.5
LN2 = float(np.log(2.0))
Q_PRESCALE = ATT_SCALE / LN2
ADAM_LR, ADAM_B1, ADAM_B2, ADAM_EPS, ADAM_WD, ADAM_STEP = 0.001, 0.9, 0.999, 1e-08, 0.01, 10
NEG_BIG = -1e30

ROW_TILE = 512
ATT_TILE = 512
HGRN_GROUP = 8
MLP_SLABS = 4
VMEM_LIMIT = 56 * 2 ** 20

NN = (((1,), (0,)), ((), ()))
NT = (((1,), (1,)), ((), ()))
TN = (((0,), (0,)), ((), ()))


def _dot(a, b, dims=NN):
    return lax.dot_general(a, b, dims, preferred_element_type=F32)


def _bdot(a, b, dims=NN):
    return lax.dot_general(a.astype(BF16), b.astype(BF16), dims, preferred_element_type=F32)


def _hdot(a, b, dims=NN):
    return lax.dot_general(a, b, dims, precision=lax.Precision.HIGHEST, preferred_element_type=F32)


def _params():
    return pltpu.CompilerParams(vmem_limit_bytes=VMEM_LIMIT)


def _sigmoid(x):
    return 1.0 / (1.0 + jnp.exp(-x))


def _rowsum(x):
    return jnp.sum(x, axis=0, keepdims=True)


def _lanemean(x):
    return jnp.mean(x, axis=-1, keepdims=True)


def _full(shape):
    nd = len(shape)
    return pl.BlockSpec(shape, lambda *_: (0,) * nd)


class _Exchange:
    def __init__(self, arrs, scatter):
        self.arrs, self.scatter, self.n, self.aliases, self.middle_at = list(arrs), scatter, len(arrs), [], 0.5
        self.out_shape = [jax.ShapeDtypeStruct((N_DEV,) + (a.shape[1:] if scatter else a.shape), a.dtype)
                          for a in self.arrs]
        n = self.n
        self.scratch = [pltpu.SemaphoreType.DMA((n, N_DEV - 1)), pltpu.SemaphoreType.DMA((n, N_DEV - 1)),
                        pltpu.SemaphoreType.DMA((n,))]

    def _copies(self, ins, outs, sems):
        send_sems, recv_sems, loc_sems = sems
        x, y, c = lax.axis_index("x"), lax.axis_index("y"), lax.axis_index("c")
        me = 4 * x + 2 * y + c
        copies = []
        for k in range(self.n):
            src_of = (lambda i, k=k: ins[k].at[i]) if self.scatter else (lambda i, k=k: ins[k])
            copies.append((pltpu.make_async_copy(src_of(me), outs[k].at[me], loc_sems.at[k]), None))
            for p in range(1, N_DEV):
                px = (1 - x) if p & 4 else x
                py = (1 - y) if p & 2 else y
                pc = (1 - c) if p & 1 else c
                peer = 4 * px + 2 * py + pc
                both = dict(send_sem=send_sems.at[k, p - 1], recv_sem=recv_sems.at[k, p - 1],
                            device_id=(px, py, pc), device_id_type=pl.DeviceIdType.MESH)
                send = pltpu.make_async_remote_copy(src_ref=src_of(peer), dst_ref=outs[k].at[me], **both)
                recv = pltpu.make_async_remote_copy(src_ref=src_of(peer), dst_ref=outs[k].at[peer], **both)
                copies.append((send, recv))
        return copies

    def start(self, ins, outs, sems):
        for first, _ in self._copies(ins, outs, sems):
            first.start()

    def middle(self, ins, outs, sems):
        pass

    def finish(self, ins, outs, sems):
        pass

    def wait(self, ins, outs, sems):
        for first, recv in self._copies(ins, outs, sems):
            if recv is None:
                first.wait()
            else:
                recv.wait_recv()
                first.wait_send()


class _StagedGather:
    def __init__(self, arr, middle_at=0.8, rows=None):
        self.r0, n = rows if rows else (0, arr.shape[0])
        block = (n,) + arr.shape[1:]
        self.arrs, self.aliases, self.middle_at = [arr], [], middle_at
        self.out_shape = [jax.ShapeDtypeStruct((N_DEV,) + block, BF16)]
        self.scratch = [pltpu.VMEM((N_DEV,) + block, BF16), pltpu.SemaphoreType.DMA((7,)),
                        pltpu.SemaphoreType.DMA((7,)), pltpu.SemaphoreType.DMA((2,)), pltpu.VMEM(block, arr.dtype)]

    def _parts(self, scr):
        stage, send_sems, recv_sems, loc_sems = scr[:4]
        x, y, c = lax.axis_index("x"), lax.axis_index("y"), lax.axis_index("c")
        me, sibling = (x, y, c), (x, y, 1 - c)
        chips = [(1 - x, y), (x, 1 - y), (1 - x, 1 - y)]

        def copy(j, block, to):
            px, py, pc = block
            slot = stage.at[4 * px + 2 * py + pc]
            return pltpu.make_async_remote_copy(src_ref=slot, dst_ref=slot, send_sem=send_sems.at[j],
                                                recv_sem=recv_sems.at[j], device_id=to,
                                                device_id_type=pl.DeviceIdType.MESH)

        return stage, loc_sems, me, sibling, chips, c, copy

    def start(self, ins, outs, scr):
        stage, loc_sems, me, sibling, chips, c, copy = self._parts(scr)
        x, y, _ = me
        raw = scr[4]
        own = pltpu.make_async_copy(ins[0].at[pl.ds(self.r0, raw.shape[0])], raw, loc_sems.at[0])
        own.start()
        own.wait()
        stage[4 * x + 2 * y + c] = raw[...].astype(BF16)
        copy(0, me, sibling).start()
        for j, chip in enumerate(chips):
            copy(1 + j, me, (*chip, c)).start()

    def middle(self, ins, outs, scr):
        stage, loc_sems, me, sibling, chips, c, copy = self._parts(scr)
        for j, chip in enumerate(chips):
            copy(1 + j, (*chip, c), me).wait_recv()
            copy(4 + j, (*chip, c), sibling).start()

    def wait(self, ins, outs, scr):
        stage, loc_sems, me, sibling, chips, c, copy = self._parts(scr)
        copy(0, sibling, me).wait_recv()
        for j, chip in enumerate(chips):
            copy(4 + j, (*chip, 1 - c), me).wait_recv()
        copy(0, me, sibling).wait_send()
        for j, chip in enumerate(chips):
            copy(1 + j, me, (*chip, c)).wait_send()
            copy(4 + j, (*chip, c), sibling).wait_send()
        pltpu.make_async_copy(stage, outs[0], loc_sems.at[1]).start()

    def finish(self, ins, outs, scr):
        pltpu.make_async_copy(scr[0], outs[0], scr[3].at[1]).wait()


class _StagedScatter:
    def __init__(self, slabs, middle_at=0.2):
        _, r, c = slabs.shape
        self.arrs, self.aliases, self.middle_at = [slabs], [], middle_at
        self.out_shape = [jax.ShapeDtypeStruct((4, r, c), slabs.dtype)]
        self.scratch = [pltpu.VMEM((N_DEV, r, c), slabs.dtype), pltpu.VMEM((4, r, c), slabs.dtype),
                        pltpu.VMEM((3, r, c), slabs.dtype), pltpu.SemaphoreType.DMA((4,)), pltpu.SemaphoreType.DMA((4,)),
                        pltpu.SemaphoreType.DMA((3,)), pltpu.SemaphoreType.DMA((3,)), pltpu.SemaphoreType.DMA((4,))]

    def _parts(self, scr):
        stage, from_sib, from_chips, sib_send, sib_recv, ici_send, ici_recv, loc_sems = scr
        x, y, c = lax.axis_index("x"), lax.axis_index("y"), lax.axis_index("c")
        chips = [(1 - x, y), (x, 1 - y), (1 - x, 1 - y)]

        def to_sibling(j):
            return pltpu.make_async_remote_copy(src_ref=stage.at[2 * j + 1 - c], dst_ref=from_sib.at[j],
                                                send_sem=sib_send.at[j], recv_sem=sib_recv.at[j],
                                                device_id=(x, y, 1 - c), device_id_type=pl.DeviceIdType.MESH)

        def to_chip(k):
            px, py = chips[k]
            return pltpu.make_async_remote_copy(src_ref=stage.at[4 * px + 2 * py + c], dst_ref=from_chips.at[k],
                                                send_sem=ici_send.at[k], recv_sem=ici_recv.at[k],
                                                device_id=(px, py, c), device_id_type=pl.DeviceIdType.MESH)

        return stage, from_sib, from_chips, loc_sems, (x, y, c), chips, to_sibling, to_chip

    def start(self, ins, outs, scr):
        stage, _, _, loc_sems, _, _, to_sibling, _ = self._parts(scr)
        load = pltpu.make_async_copy(ins[0], stage, loc_sems.at[0])
        load.start()
        load.wait()
        for j in range(4):
            to_sibling(j).start()

    def middle(self, ins, outs, scr):
        stage, from_sib, _, _, (x, y, c), _, to_sibling, to_chip = self._parts(scr)
        for j in range(4):
            to_sibling(j).wait_recv()
            mine = stage.at[2 * j + c]
            mine[...] = (mine[...].astype(F32) + from_sib[j].astype(F32)).astype(mine.dtype)
        for k in range(3):
            to_chip(k).start()

    def wait(self, ins, outs, scr):
        stage, _, from_chips, loc_sems, (x, y, c), chips, to_sibling, to_chip = self._parts(scr)
        writes = [pltpu.make_async_copy(stage.at[4 * x + 2 * y + c], outs[0].at[2 * x + y], loc_sems.at[0])]
        for k, (px, py) in enumerate(chips):
            to_chip(k).wait_recv()
            writes.append(pltpu.make_async_copy(from_chips.at[k], outs[0].at[2 * px + py], loc_sems.at[1 + k]))
        for w in writes:
            w.start()
        for j in range(4):
            to_sibling(j).wait_send()
        for k in range(3):
            to_chip(k).wait_send()
        for w in writes:
            w.wait()

    def finish(self, ins, outs, scr):
        pass


def _call(body, name, args, out_shape, grid=(), in_specs=(), out_specs=(), scratch_shapes=(), exchange=None):
    if exchange is None:
        return pl.pallas_call(body, name=name, grid=grid, in_specs=list(in_specs), out_specs=list(out_specs),
                              out_shape=list(out_shape), scratch_shapes=list(scratch_shapes),
                              compiler_params=_params())(*args), None
    exs = list(exchange) if isinstance(exchange, (list, tuple)) else [exchange]
    ni, no, ns = len(args), len(out_shape), len(scratch_shapes)
    nxi, nxo = sum(len(e.arrs) for e in exs), sum(len(e.out_shape) for e in exs)
    steps = int(np.prod(grid))
    mid_step = lambda e: min(max(int(steps * e.middle_at), 1), steps - 1)
    aliases, iat, oat = {}, ni, no
    for e in exs:
        for src, dst in e.aliases:
            aliases[iat + src] = oat + dst
        iat, oat = iat + len(e.arrs), oat + len(e.out_shape)

    def wrapped(*refs):
        a, xi = refs[:ni], refs[ni:ni + nxi]
        o, xo = refs[ni + nxi:ni + nxi + no], refs[ni + nxi + no:ni + nxi + no + nxo]
        s, xs = refs[ni + nxi + no + nxo:ni + nxi + no + nxo + ns], refs[ni + nxi + no + nxo + ns:]
        parts, iat, oat, sat = [], 0, 0, 0
        for e in exs:
            parts.append((e, xi[iat:iat + len(e.arrs)], xo[oat:oat + len(e.out_shape)], xs[sat:sat + len(e.scratch)]))
            iat, oat, sat = iat + len(e.arrs), oat + len(e.out_shape), sat + len(e.scratch)
        step = 0
        for d, g in enumerate(grid):
            step = step * g + pl.program_id(d)

        @pl.when(step == 0)
        def _():
            for e, ins, outs, sems in parts:
                e.start(ins, outs, sems)

        for at_step in sorted({mid_step(e) for e in exs}):
            @pl.when(step == at_step)
            def _():
                for e, ins, outs, sems in parts:
                    if mid_step(e) == at_step:
                        e.middle(ins, outs, sems)

        body(*a, *o, *s)

        @pl.when(step == steps - 1)
        def _():
            for e, ins, outs, sems in parts:
                e.wait(ins, outs, sems)
            for e, ins, outs, sems in parts:
                e.finish(ins, outs, sems)

    hbm = pl.BlockSpec(memory_space=pltpu.HBM)
    res = pl.pallas_call(
        wrapped, name=name, grid=grid, in_specs=list(in_specs) + [hbm] * nxi, out_specs=list(out_specs) + [hbm] * nxo,
        out_shape=list(out_shape) + [o_ for e in exs for o_ in e.out_shape],
        scratch_shapes=list(scratch_shapes) + [s_ for e in exs for s_ in e.scratch],
        input_output_aliases=aliases, compiler_params=_params())(*args, *[a_ for e in exs for a_ in e.arrs])
    return res[:no], res[no:]


def _gather_two_level(arrs, name):
    n = len(arrs)
    out_shape = [jax.ShapeDtypeStruct((N_DEV,) + a.shape, a.dtype) for a in arrs]

    def body(*refs):
        ins, outs = refs[:n], refs[n:2 * n]
        send_sems, recv_sems, loc_sems = refs[2 * n:]
        x, y, c = lax.axis_index("x"), lax.axis_index("y"), lax.axis_index("c")
        me, sibling = (x, y, c), (x, y, 1 - c)
        chips = [(1 - x, y), (x, 1 - y), (1 - x, 1 - y)]

        def copy(k, j, block, to, src=None):
            px, py, pc = block
            dst = outs[k].at[4 * px + 2 * py + pc]
            return pltpu.make_async_remote_copy(src_ref=dst if src is None else src, dst_ref=dst,
                                                send_sem=send_sems.at[k, j], recv_sem=recv_sems.at[k, j],
                                                device_id=to, device_id_type=pl.DeviceIdType.MESH)

        mine = [pltpu.make_async_copy(ins[k], outs[k].at[4 * x + 2 * y + c], loc_sems.at[k]) for k in range(n)]
        first = []
        for k in range(n):
            mine[k].start()
            first.append(copy(k, 0, me, sibling, src=ins[k]))
            first += [copy(k, 1 + j, me, (*chip, c), src=ins[k]) for j, chip in enumerate(chips)]
        for cp in first:
            cp.start()
        passed = []
        for j, chip in enumerate(chips):
            for k in range(n):
                copy(k, 1 + j, (*chip, c), me).wait_recv()
                passed.append(copy(k, 4 + j, (*chip, c), sibling))
                passed[-1].start()
        for k in range(n):
            copy(k, 0, sibling, me).wait_recv()
            for j, chip in enumerate(chips):
                copy(k, 4 + j, (*chip, 1 - c), me).wait_recv()
        for cp in first + passed:
            cp.wait_send()
        for cp in mine:
            cp.wait()

    vmem = pl.BlockSpec(memory_space=pltpu.VMEM)
    return pl.pallas_call(body, name=name, out_shape=out_shape, in_specs=[vmem] * n, out_specs=[vmem] * n,
                          scratch_shapes=[pltpu.SemaphoreType.DMA((n, 7)), pltpu.SemaphoreType.DMA((n, 7)),
                                          pltpu.SemaphoreType.DMA((n,))], compiler_params=_params())(*arrs)


def _exchange(arrs, scatter, name):
    ex = _Exchange(arrs, scatter)

    def body(*refs):
        ins, outs, sems = refs[:ex.n], refs[ex.n:2 * ex.n], refs[2 * ex.n:]
        ex.start(ins, outs, sems)
        ex.wait(ins, outs, sems)

    hbm = pl.BlockSpec(memory_space=pltpu.HBM)
    return pl.pallas_call(body, name=name, out_shape=ex.out_shape, in_specs=[hbm] * ex.n, out_specs=[hbm] * ex.n,
                          scratch_shapes=ex.scratch)(*ex.arrs)


def _matmul(a, b, mode, name, out_dtype=F32, tm=512, tn=1024, tk=1024, a_fn=None, b_fn=None, extras=(),
            out_slabs=None, exchange=None):
    assert not (a_fn and b_fn) and not (b_fn and mode == "NT")
    if mode == "NN":
        (M, K), N = a.shape, b.shape[1]
    elif mode == "NT":
        (M, K), N = a.shape, b.shape[0]
    else:
        (K, M), N = a.shape, b.shape[1]
    slab_w = N // out_slabs if out_slabs else None
    if out_slabs:
        tn = max(slab_w, min(tn, N) // slab_w * slab_w)
    tm, tn, tk = min(tm, M), min(tn, N), min(tk, K)
    assert M % tm == 0 and N % tn == 0 and K % tk == 0, (name, M, N, K)
    nk = K // tk
    dims = {"NN": NN, "NT": NT, "TN": TN}[mode]
    ne = len(extras)

    def body(a_ref, b_ref, *rest):
        e_refs, o_ref, acc_ref = rest[:ne], rest[ne], rest[ne + 1]
        k = pl.program_id(2)

        @pl.when(k == 0)
        def _():
            acc_ref[...] = jnp.zeros_like(acc_ref)

        at, bt = a_ref[...], b_ref[...]
        if a_fn is not None:
            at = a_fn(at.astype(F32), *[e[...] for e in e_refs])
        if b_fn is not None:
            bt = b_fn(bt.astype(F32), *[e[...] for e in e_refs])
        acc_ref[...] += _bdot(at, bt, dims)

        @pl.when(k == nk - 1)
        def _():
            if out_slabs:
                for s in range(tn // slab_w):
                    o_ref[s] = acc_ref[:, s * slab_w:(s + 1) * slab_w].astype(out_dtype)
            else:
                o_ref[...] = acc_ref[...].astype(out_dtype)

    if mode == "TN":
        a_spec = pl.BlockSpec((tk, tm), lambda i, j, k: (k, i))
        e_spec = pl.BlockSpec((1, tm), lambda i, j, k: (0, i))
    else:
        a_spec = pl.BlockSpec((tm, tk), lambda i, j, k: (i, k))
        e_spec = pl.BlockSpec((1, tk), lambda i, j, k: (0, k))
    if mode == "NT":
        b_spec = pl.BlockSpec((tn, tk), lambda i, j, k: (j, k))
    else:
        b_spec = pl.BlockSpec((tk, tn), lambda i, j, k: (k, j))
    if b_fn is not None:
        e_spec = pl.BlockSpec((1, tn), lambda i, j, k: (0, j))
    if out_slabs:
        o_shape = jax.ShapeDtypeStruct((out_slabs, M, slab_w), out_dtype)
        o_spec = pl.BlockSpec((tn // slab_w, tm, slab_w), lambda i, j, k: (j, i, 0))
    else:
        o_shape = jax.ShapeDtypeStruct((M, N), out_dtype)
        o_spec = pl.BlockSpec((tm, tn), lambda i, j, k: (i, j))
    (out,), got = _call(body, name, (a, b, *extras), [o_shape], grid=(M // tm, N // tn, nk),
                        in_specs=[a_spec, b_spec] + [e_spec] * ne, out_specs=[o_spec],
                        scratch_shapes=[pltpu.VMEM((tm, tn), F32)], exchange=exchange)
    return out if exchange is None else (out, got)


def _modulate(x, sc, sh):
    return x * (1.0 + sc) + sh


def _square(x):
    return x * x


def _mod_part(c_all, w_ada_s, b_s):
    def body(c_ref, w_ref, b_ref, mod_ref, cond_ref):
        cv = c_ref[...]
        cond = cv * _sigmoid(cv)
        cond_ref[...] = cond
        mod_ref[...] = _bdot(cond, w_ref[...]) + b_ref[...]

    return pl.pallas_call(
        body, name="mod_part",
        out_shape=[jax.ShapeDtypeStruct((N_DEV, w_ada_s.shape[1]), F32), jax.ShapeDtypeStruct(c_all.shape, F32)],
        compiler_params=_params(),
    )(c_all, w_ada_s, b_s)


def _rms_fwd(x, w):
    rs = lax.rsqrt(_lanemean(x * x) + RMS_EPS)
    return x * rs * w, rs


def _rms_bwd(x, rs, w, dy):
    xhat = x * rs
    dxh = dy * w
    return rs * (dxh - xhat * _lanemean(dxh * xhat)), dy * xhat


def _mla_pre(z, pos_col, invf, m_rot, wq_ext, wkv_ext, qnw, kvnw, exchange=None):
    T = z.shape[0]
    tm = min(ROW_TILE, T)

    def body(z_ref, pos_ref, invf_ref, mrot_ref, wq_ref, wkv_ref, qnw_ref, kvnw_ref,
             q_ref, k_ref, v_ref, c1_ref, s1_ref, cqn_ref, ckvn_ref):
        hi = slice(HEAD_DIM, QK_PAD)
        ang = pos_ref[...].astype(F32) * invf_ref[:, hi]
        c1 = jnp.concatenate([jnp.ones((tm, HEAD_DIM), F32), mrot_ref[:, hi] * jnp.cos(ang)], axis=1)
        s1 = jnp.concatenate([jnp.zeros((tm, HEAD_DIM), F32), mrot_ref[:, hi] * jnp.sin(ang)], axis=1)
        c1_ref[...] = c1
        s1_ref[...] = s1
        cqn, _ = _rms_fwd(z_ref[:, 0:256], qnw_ref[...])
        ckvn, _ = _rms_fwd(z_ref[:, 256:512], kvnw_ref[...])
        cqn_ref[...] = cqn.astype(BF16)
        ckvn_ref[...] = ckvn.astype(BF16)
        qe = _bdot(cqn, wq_ref[...], NT)
        kve = _bdot(ckvn, wkv_ref[...])
        k_rope = z_ref[:, 512:768] * c1 + z_ref[:, 768:1024] * s1
        for h in range(HEADS):
            q_ref[h] = ((qe[:, 256 * h:256 * h + 256] * c1 + qe[:, 1024 + 256 * h:1280 + 256 * h] * s1)
                        * Q_PRESCALE).astype(BF16)
            k_ref[h] = (kve[:, 256 * h:256 * h + 256] + k_rope).astype(BF16)
            v_ref[h] = kve[:, 1024 + 128 * h:1152 + 128 * h].astype(BF16)

    row = lambda i: (i, 0)
    head = lambda i: (0, i, 0)
    return _call(
        body, "mla_pre", (z, pos_col, invf, m_rot, wq_ext, wkv_ext, qnw, kvnw), grid=(T // tm,),
        in_specs=[pl.BlockSpec((tm, 1024), lambda i: (i, 2)), pl.BlockSpec((tm, 1), row),
                  _full((1, 256)), _full((1, 256)), _full(wq_ext.shape), _full(wkv_ext.shape),
                  _full((1, 256)), _full((1, 256))],
        out_specs=[pl.BlockSpec((HEADS, tm, QK_PAD), head), pl.BlockSpec((HEADS, tm, QK_PAD), head),
                   pl.BlockSpec((HEADS, tm, HEAD_DIM), head), pl.BlockSpec((tm, 256), row), pl.BlockSpec((tm, 256), row),
                   pl.BlockSpec((tm, 256), row), pl.BlockSpec((tm, 256), row)],
        out_shape=[jax.ShapeDtypeStruct((HEADS, T, QK_PAD), BF16), jax.ShapeDtypeStruct((HEADS, T, QK_PAD), BF16),
                   jax.ShapeDtypeStruct((HEADS, T, HEAD_DIM), BF16), jax.ShapeDtypeStruct((T, 256), F32),
                   jax.ShapeDtypeStruct((T, 256), F32), jax.ShapeDtypeStruct((T, 256), BF16),
                   jax.ShapeDtypeStruct((T, 256), BF16)], exchange=exchange)


def _mla_bwd(dq, dk, dv, z, c1, s1, wq_ext, wkv_ext, qnw, kvnw):
    T = z.shape[0]
    tm = min(ROW_TILE, T)

    def body(dq_ref, dk_ref, dv_ref, z_ref, c1_ref, s1_ref, wq_ref, wkv_ref, qnw_ref, kvnw_ref,
             dz_ref, dqe_ref, dkve_ref, dnw_ref):
        @pl.when(pl.program_id(0) == 0)
        def _():
            dnw_ref[...] = jnp.zeros_like(dnw_ref)

        c1, s1 = c1_ref[...], s1_ref[...]
        dkpe = jnp.zeros((tm, QK_PAD), F32)
        for h in range(HEADS):
            dqh, dkh = dq_ref[h].astype(F32) * ATT_SCALE, dk_ref[h]
            dqe_ref[:, 256 * h:256 * h + 256] = (dqh * c1).astype(BF16)
            dqe_ref[:, 1024 + 256 * h:1280 + 256 * h] = (dqh * s1).astype(BF16)
            dkve_ref[:, 256 * h:256 * h + 256] = dkh
            dkve_ref[:, 1024 + 128 * h:1152 + 128 * h] = dv_ref[h]
            dkpe = dkpe + dkh.astype(F32)
        dcqn = _dot(dqe_ref[...], wq_ref[...])
        dckvn = _dot(dkve_ref[...], wkv_ref[...], NT)
        cq, ckv = z_ref[:, 0:256], z_ref[:, 256:512]
        _, rsq = _rms_fwd(cq, qnw_ref[...])
        _, rskv = _rms_fwd(ckv, kvnw_ref[...])
        dcq, wq_rows = _rms_bwd(cq, rsq, qnw_ref[...], dcqn)
        dckv, wkv_rows = _rms_bwd(ckv, rskv, kvnw_ref[...], dckvn)
        dnw_ref[:, 0:256] += _rowsum(wq_rows)
        dnw_ref[:, 256:512] += _rowsum(wkv_rows)
        dz_ref[:, 0:256] = dcq.astype(BF16)
        dz_ref[:, 256:512] = dckv.astype(BF16)
        dz_ref[:, 512:768] = (dkpe * c1).astype(BF16)
        dz_ref[:, 768:1024] = (dkpe * s1).astype(BF16)

    row = lambda i: (i, 0)
    head = lambda i: (0, i, 0)
    return pl.pallas_call(
        body, name="mla_bwd", grid=(T // tm,),
        in_specs=[pl.BlockSpec((HEADS, tm, QK_PAD), head), pl.BlockSpec((HEADS, tm, QK_PAD), head),
                  pl.BlockSpec((HEADS, tm, HEAD_DIM), head), pl.BlockSpec((tm, 1024), lambda i: (i, 2)),
                  pl.BlockSpec((tm, 256), row), pl.BlockSpec((tm, 256), row), _full(wq_ext.shape), _full(wkv_ext.shape),
                  _full((1, 256)), _full((1, 256))],
        out_specs=[pl.BlockSpec((tm, 1024), row), pl.BlockSpec((tm, 2048), row), pl.BlockSpec((tm, 1536), row),
                   _full((1, 512))],
        out_shape=[jax.ShapeDtypeStruct((T, 1024), BF16), jax.ShapeDtypeStruct((T, 2048), BF16),
                   jax.ShapeDtypeStruct((T, 1536), BF16), jax.ShapeDtypeStruct((1, 512), F32)],
        compiler_params=_params(),
    )(dq, dk, dv, z, c1, s1, wq_ext, wkv_ext, qnw, kvnw)


_HEAD_LANES = [slice(HEAD_DIM * h, HEAD_DIM * (h + 1)) for h in range(HEADS)]


def _lower_bound(lbraw_ref):
    a0, a1 = lbraw_ref[0:1, :], lbraw_ref[1:2, :]
    mx = jnp.maximum(a0, a1)
    e0, e1 = jnp.exp(a0 - mx), jnp.exp(a1 - mx)
    return e0 / (e0 + e1)


def _tri(lower):
    r = lax.broadcasted_iota(jnp.int32, (CHUNK, CHUNK), 0)
    c = lax.broadcasted_iota(jnp.int32, (CHUNK, CHUNK), 1)
    return (r >= c) if lower else (r <= c)


def _hgrn_gates(q, f, lb, tri_lo):
    sg = _sigmoid(f)
    forget = lb + (1.0 - lb) * sg
    k = 1.0 - forget
    b = _hdot(tri_lo.astype(F32), jnp.log(forget))
    b_ref, b_last = b[CHUNK // 2 - 1:CHUNK // 2, :], b[CHUNK - 1:CHUNK, :]
    e1, e2, e3, e4 = jnp.exp(b - b_ref), jnp.exp(b_ref - b), jnp.exp(b_last - b), jnp.exp(b)
    return dict(sg=sg, forget=forget, k=k, e1=e1, e2=e2, e3=e3, e4=e4, qa=q * e1, ka=k * e2, kl=k * e3, qb=q * e4,
                decay=jnp.exp(b_last))


def _hgrn_fwd(z, lbraw, nw, exchange=None):
    T = z.shape[0]
    G = min(HGRN_GROUP, T // CHUNK)
    rows = G * CHUNK
    n_chunks = T // CHUNK

    def body(q_ref, f_ref, i_ref, g_ref, lbraw_ref, nw_ref, oraw_ref, og_ref, sp_ref, st_ref):
        @pl.when(pl.program_id(0) == 0)
        def _():
            st_ref[...] = jnp.zeros_like(st_ref)

        lb_all = _lower_bound(lbraw_ref)
        tri_lo = _tri(True)

        def chunk(cc, carry):
            rs = pl.ds(pl.multiple_of(cc * CHUNK, CHUNK), CHUNK)
            t = _hgrn_gates(q_ref[rs, :], f_ref[rs, :], lb_all, tri_lo)
            v, gate = i_ref[rs, :], g_ref[rs, :]
            st = [st_ref[h] for h in range(HEADS)]
            a = [jnp.where(tri_lo, _bdot(t["qa"][:, s], t["ka"][:, s], NT), 0.0) for s in _HEAD_LANES]
            kv = [_bdot(v[:, s], t["kl"][:, s], TN) for s in _HEAD_LANES]
            o = [_bdot(a[h], v[:, s]) + _bdot(t["qb"][:, s], st[h], NT) for h, s in enumerate(_HEAD_LANES)]
            for h, s in enumerate(_HEAD_LANES):
                sp_ref[cc, h] = st[h]
                st_ref[h] = st[h] * t["decay"][:, s] + kv[h]
            oraw_ref[rs, :] = jnp.concatenate(o, axis=1)
            on = jnp.concatenate([_rms_fwd(o[h], nw_ref[:, s])[0] for h, s in enumerate(_HEAD_LANES)], axis=1)
            og_ref[rs, :] = (on * (gate * _sigmoid(gate))).astype(BF16)
            return carry

        lax.fori_loop(0, G, chunk, 0, unroll=4)

    col = lambda j: pl.BlockSpec((rows, 512), lambda r, j=j: (r, j))
    return _call(
        body, "hgrn_fwd", (z, z, z, z, lbraw, nw), grid=(T // rows,),
        in_specs=[col(0), col(1), col(2), col(3), _full((2, 512)), _full((1, 512))],
        out_specs=[col(0), col(0), pl.BlockSpec((G, HEADS, HEAD_DIM, HEAD_DIM), lambda r: (r, 0, 0, 0))],
        out_shape=[jax.ShapeDtypeStruct((T, 512), F32), jax.ShapeDtypeStruct((T, 512), BF16),
                   jax.ShapeDtypeStruct((n_chunks, HEADS, HEAD_DIM, HEAD_DIM), F32)],
        scratch_shapes=[pltpu.VMEM((HEADS, HEAD_DIM, HEAD_DIM), F32)], exchange=exchange)


def _hgrn_bwd(dmixcat, z, oraw, sprev, lbraw, nw, exchange=None):
    T = z.shape[0]
    G = min(HGRN_GROUP, T // CHUNK)
    rows = G * CHUNK
    ng = T // rows

    def body(dog_ref, q_ref, f_ref, i_ref, g_ref, oraw_ref, sp_ref, lbraw_ref, nw_ref,
             dz_ref, dsmall_ref, dst_ref):
        @pl.when(pl.program_id(0) == 0)
        def _():
            dst_ref[...] = jnp.zeros_like(dst_ref)
            dsmall_ref[...] = jnp.zeros_like(dsmall_ref)

        lb_all = _lower_bound(lbraw_ref)
        tri_lo, tri_up = _tri(True), _tri(False)
        rowid = lax.broadcasted_iota(jnp.int32, (CHUNK, HEADS * HEAD_DIM), 0)

        def chunk(it, carry):
            cc = G - 1 - it
            rs = pl.ds(pl.multiple_of(cc * CHUNK, CHUNK), CHUNK)
            heads = list(enumerate(_HEAD_LANES))
            cat = lambda parts: jnp.concatenate(parts, axis=1)
            per_head_mean = lambda x: cat([jnp.broadcast_to(_lanemean(x[:, s]), (CHUNK, HEAD_DIM)) for s in _HEAD_LANES])
            t = _hgrn_gates(q_ref[rs, :], f_ref[rs, :], lb_all, tri_lo)
            v, gate, o, dog, nw_all = i_ref[rs, :], g_ref[rs, :], oraw_ref[rs, :], dog_ref[rs, :], nw_ref[...]
            rs_o = lax.rsqrt(per_head_mean(o * o) + RMS_EPS)
            xhat = o * rs_o
            sgg = _sigmoid(gate)
            d_on = dog * (gate * sgg)
            dz_ref[rs, 1536:2048] = (dog * (xhat * nw_all) * (sgg * (1.0 + gate * (1.0 - sgg)))).astype(BF16)
            dxh = d_on * nw_all
            do = rs_o * (dxh - xhat * per_head_mean(dxh * xhat))
            dsmall_ref[:, 512:1024] += _rowsum(d_on * xhat)
            st = [sp_ref[cc, h] for h in range(HEADS)]
            dst = [dst_ref[h] for h in range(HEADS)]
            a = [jnp.where(tri_lo, _bdot(t["qa"][:, s], t["ka"][:, s], NT), 0.0) for s in _HEAD_LANES]
            da = [jnp.where(tri_lo, _bdot(do[:, s], v[:, s], NT), 0.0) for s in _HEAD_LANES]
            dqb = cat([_bdot(do[:, s], st[h]) for h, s in heads])
            dkl = cat([_bdot(v[:, s], dst[h]) for h, s in heads])
            dv_ = cat([_bdot(t["kl"][:, s], dst[h], NT) + _bdot(a[h], do[:, s], TN) for h, s in heads])
            dqa = cat([_bdot(da[h], t["ka"][:, s]) for h, s in heads])
            dka = cat([_bdot(da[h], t["qa"][:, s], TN) for h, s in heads])
            ddecay = cat([_rowsum(dst[h] * st[h]) for h in range(HEADS)])
            for h, s in heads:
                dst_ref[h] = dst[h] * t["decay"][:, s] + _bdot(do[:, s], t["qb"][:, s], TN)
            pa, pk, pb, pl_ = dqa * t["qa"], dka * t["ka"], dqb * t["qb"], dkl * t["kl"]
            db = pa - pk + pb - pl_
            db = db + jnp.where(rowid == CHUNK // 2 - 1, _rowsum(pk - pa), 0.0)
            db = db + jnp.where(rowid == CHUNK - 1, _rowsum(pl_) + ddecay * t["decay"], 0.0)
            dlogf = _hdot(tri_up.astype(F32), db)
            dforget = dlogf / t["forget"] - (dka * t["e2"] + dkl * t["e3"])
            sg = t["sg"]
            dz_ref[rs, 0:512] = (dqa * t["e1"] + dqb * t["e4"]).astype(BF16)
            dz_ref[rs, 512:1024] = (dforget * (1.0 - lb_all) * sg * (1.0 - sg)).astype(BF16)
            dz_ref[rs, 1024:1536] = dv_.astype(BF16)
            dsmall_ref[:, 0:512] += _rowsum(dforget * (1.0 - sg))
            return carry

        lax.fori_loop(0, G, chunk, 0, unroll=4)

    col = lambda j: pl.BlockSpec((rows, 512), lambda r, j=j: (ng - 1 - r, j))
    return _call(
        body, "hgrn_bwd", (dmixcat, z, z, z, z, oraw, sprev, lbraw, nw), grid=(ng,),
        in_specs=[col(0), col(0), col(1), col(2), col(3), col(0),
                  pl.BlockSpec((G, HEADS, HEAD_DIM, HEAD_DIM), lambda r: (ng - 1 - r, 0, 0, 0)),
                  _full((2, 512)), _full((1, 512))],
        out_specs=[pl.BlockSpec((rows, 2048), lambda r: (ng - 1 - r, 0)), _full((1, 1024))],
        out_shape=[jax.ShapeDtypeStruct((T, 2048), BF16), jax.ShapeDtypeStruct((1, 1024), F32)],
        scratch_shapes=[pltpu.VMEM((HEADS, HEAD_DIM, HEAD_DIM), F32)], exchange=exchange)


def _diag_mask(t):
    r = lax.broadcasted_iota(jnp.int32, (t, t), 0)
    c = lax.broadcasted_iota(jnp.int32, (t, t), 1)
    return r >= c


def _attn_fwd(q, k, v, exchange=None):
    _, T, _ = q.shape
    t = min(ATT_TILE, T)

    def body(q_ref, k_ref, v_ref, o_ref, lse_ref):
        i = pl.program_id(1)
        qb = q_ref[...]

        rows = lambda j: pl.ds(pl.multiple_of(j * t, t), t)

        def logits(j, masked):
            s = _dot(qb, k_ref[rows(j), :], NT)
            return jnp.where(_diag_mask(t), s, NEG_BIG) if masked else s

        def absorb(s, j, carry):
            m, l, acc = carry
            mn = jnp.maximum(m, jnp.max(s, axis=-1, keepdims=True))
            p = jnp.exp2(s - mn)
            al = jnp.exp2(m - mn)
            return mn, al * l + jnp.sum(p, axis=-1, keepdims=True), al * acc + _dot(p.astype(BF16), v_ref[rows(j), :])

        def pair(j0, carry, last_masked):
            s0, s1 = logits(j0, False), logits(j0 + 1, last_masked)
            return absorb(s1, j0 + 1, absorb(s0, j0, carry))

        init = (jnp.full((t, 1), NEG_BIG, F32), jnp.zeros((t, 1), F32), jnp.zeros((t, HEAD_DIM), F32))
        carry = lax.fori_loop(0, i // 2, lambda jj, c: pair(2 * jj, c, False), init)
        m, l, acc = lax.cond(i % 2 == 1, lambda c: pair(i - 1, c, True),
                             lambda c: absorb(logits(i, True), i, c), carry)
        o_ref[...] = acc / l
        lse_ref[...] = jnp.broadcast_to(m + jnp.log2(l), (t, HEAD_DIM))

    return _call(
        body, "attn_fwd", (q, k, v), grid=(HEADS, T // t),
        in_specs=[pl.BlockSpec((None, t, QK_PAD), lambda h, i: (h, i, 0)),
                  pl.BlockSpec((None, T, QK_PAD), lambda h, i: (h, 0, 0)),
                  pl.BlockSpec((None, T, HEAD_DIM), lambda h, i: (h, 0, 0))],
        out_specs=[pl.BlockSpec((t, HEAD_DIM), lambda h, i: (i, h)),
                   pl.BlockSpec((None, t, HEAD_DIM), lambda h, i: (h, i, 0))],
        out_shape=[jax.ShapeDtypeStruct((T, HEADS * HEAD_DIM), F32), jax.ShapeDtypeStruct((HEADS, T, HEAD_DIM), F32)],
        exchange=exchange)


def _attn_bwd(q, k, v, dmixcat, o, lse, exchange=None):
    _, T, _ = q.shape
    t = min(ATT_TILE, T)
    nq = T // t

    def body(q_ref, k_ref, v_ref, do_ref, o_ref, lse_ref, dq_ref, dk_ref, dv_ref, delta_ref, dq_acc):
        j = pl.program_id(1)

        @pl.when(j == 0)
        def _():
            dq_acc[...] = jnp.zeros_like(dq_acc)

            def fill(i, carry):
                rs = pl.ds(pl.multiple_of(i * t, t), t)
                delta_ref[rs, :] = jnp.broadcast_to(
                    jnp.sum(do_ref[rs, :] * o_ref[rs, :], axis=-1, keepdims=True), (t, HEAD_DIM))
                return carry

            lax.fori_loop(0, nq, fill, 0)

        kb, vb = k_ref[...], v_ref[...]

        def steps(blocks, carry):
            dk, dv = carry
            rs = [pl.ds(pl.multiple_of(i * t, t), t) for i, _ in blocks]
            qb = [q_ref[r, :] for r in rs]
            dob = [do_ref[r, :].astype(BF16) for r in rs]
            s = [_dot(b, kb, NT) for b in qb]
            dp = [_dot(b, vb, NT) for b in dob]
            for n, (_, shift) in enumerate(blocks):
                p = jnp.exp2(s[n] - lse_ref[rs[n], 0:1])
                if shift is not None:
                    row = lax.broadcasted_iota(jnp.int32, (t, 2 * t), 0)
                    col = lax.broadcasted_iota(jnp.int32, (t, 2 * t), 1)
                    p = jnp.where(col <= row + shift, p, 0.0)
                ds = (p * (dp[n] - delta_ref[rs[n], 0:1])).astype(BF16)
                dq_acc[rs[n], :] += _dot(ds, kb)
                dk = dk + _dot(ds, qb[n], TN)
                dv = dv + _dot(p.astype(BF16), dob[n], TN)
            return dk, dv

        zero = (jnp.zeros((2 * t, QK_PAD), F32), jnp.zeros((2 * t, HEAD_DIM), F32))
        carry = steps([(2 * j, 0), (2 * j + 1, t)], zero)
        first = 2 * j + 2
        dk, dv = lax.fori_loop(0, (nq - first) // 2, lambda n, c: steps([(first + 2 * n, None), (first + 2 * n + 1, None)], c),
                               carry)
        dk_ref[...] = (dk * LN2).astype(BF16)
        dv_ref[...] = dv.astype(BF16)

        @pl.when(j == nk - 1)
        def _():
            dq_ref[...] = dq_acc[...].astype(BF16)

    nk = nq // 2
    return _call(
        body, "attn_bwd", (q, k, v, dmixcat, o, lse), grid=(HEADS, nk),
        in_specs=[pl.BlockSpec((None, T, QK_PAD), lambda h, j: (h, 0, 0)),
                  pl.BlockSpec((None, 2 * t, QK_PAD), lambda h, j: (h, j, 0)),
                  pl.BlockSpec((None, 2 * t, HEAD_DIM), lambda h, j: (h, j, 0)),
                  pl.BlockSpec((T, HEAD_DIM), lambda h, j: (0, HEADS + h)),
                  pl.BlockSpec((T, HEAD_DIM), lambda h, j: (0, h)),
                  pl.BlockSpec((None, T, HEAD_DIM), lambda h, j: (h, 0, 0))],
        out_specs=[pl.BlockSpec((None, T, QK_PAD), lambda h, j: (h, 0, 0)),
                   pl.BlockSpec((None, 2 * t, QK_PAD), lambda h, j: (h, j, 0)),
                   pl.BlockSpec((None, 2 * t, HEAD_DIM), lambda h, j: (h, j, 0))],
        out_shape=[jax.ShapeDtypeStruct((HEADS, T, QK_PAD), BF16), jax.ShapeDtypeStruct((HEADS, T, QK_PAD), BF16),
                   jax.ShapeDtypeStruct((HEADS, T, HEAD_DIM), BF16)],
        scratch_shapes=[pltpu.VMEM((T, HEAD_DIM), F32), pltpu.VMEM((T, QK_PAD), F32)], exchange=exchange)


def _ln_fwd(r):
    mu = _lanemean(r)
    xc = r - mu
    rstd = lax.rsqrt(_lanemean(xc * xc) + LN_EPS)
    return xc * rstd, rstd


def _ln_bwd(dxh, xhat, rstd):
    return rstd * (dxh - _lanemean(dxh) - xhat * _lanemean(dxh * xhat))


def _mix_ln1(o_hg, o_mla, w_out, x, g_a, ln1_g, ln1_b, sc_m, sh_m, exchange=None):
    T = x.shape[0]
    tm = min(ROW_TILE, T)
    half = o_hg.shape[1]

    def body(hg_ref, mla_ref, w_ref, x_ref, ga_ref, g_ref, b_ref, sc_ref, sh_ref, mix_ref, xhat_ref, rstd_ref, u2_ref):
        mix = _dot(hg_ref[...], w_ref[0:half, :]) + _bdot(mla_ref[...], w_ref[half:, :])
        mix_ref[...] = mix
        xhat, rstd = _ln_fwd(ALPHA * x_ref[...] + (1.0 + ga_ref[...]) * mix)
        xhat_ref[...] = xhat
        rstd_ref[...] = jnp.broadcast_to(rstd, (tm, 128))
        u2_ref[...] = _modulate(xhat * g_ref[...] + b_ref[...], sc_ref[...], sh_ref[...]).astype(BF16)

    row = pl.BlockSpec((tm, D_MODEL), lambda i: (i, 0))
    vec = _full((1, D_MODEL))
    halfrow = pl.BlockSpec((tm, half), lambda i: (i, 0))
    return _call(
        body, "mix_ln1", (o_hg, o_mla, w_out, x, g_a, ln1_g, ln1_b, sc_m, sh_m), grid=(T // tm,),
        in_specs=[halfrow, halfrow, _full(w_out.shape), row, vec, vec, vec, vec, vec],
        out_specs=[row, row, pl.BlockSpec((tm, 128), lambda i: (i, 0)), row],
        out_shape=[jax.ShapeDtypeStruct((T, D_MODEL), F32), jax.ShapeDtypeStruct((T, D_MODEL), F32),
                   jax.ShapeDtypeStruct((T, 128), F32), jax.ShapeDtypeStruct((T, D_MODEL), BF16)],
        exchange=exchange)


def _mlp_fwd(u2, w1, w2, xhat1, ln1_g, ln1_b, g_m, ln2_g, ln2_b, target):
    T = u2.shape[0]
    half, tf = w1[0].shape[1:]
    nf = N_DEV // MLP_SLABS
    tm = min(ROW_TILE, T)

    def body(u2_ref, w1a_ref, w1b_ref, w2_ref, xhat_ref, g1_ref, b1_ref, gm_ref, g2_ref, b2_ref, tgt_ref,
             r_ref, dr2_ref, dh_ref, small_ref, acc_ref):
        i, f = pl.program_id(0), pl.program_id(1)
        dm = D_MODEL

        @pl.when((i == 0) & (f == 0))
        def _():
            small_ref[...] = jnp.zeros_like(small_ref)

        @pl.when(f == 0)
        def _():
            acc_ref[...] = jnp.zeros_like(acc_ref)

        u2t = u2_ref[...]
        part = None
        for s in range(MLP_SLABS):
            r = jnp.maximum(_dot(u2t[:, :half], w1a_ref[s]) + _dot(u2t[:, half:], w1b_ref[s]), 0.0)
            r_ref[:, s * tf:(s + 1) * tf] = r.astype(BF16)
            d = _bdot(r * r, w2_ref[s])
            part = d if part is None else part + d
        acc_ref[...] += part

        @pl.when(f == nf - 1)
        def _():
            h = acc_ref[...]
            x1 = xhat_ref[...] * g1_ref[...] + b1_ref[...]
            xhat2, rstd2 = _ln_fwd(ALPHA * x1 + (1.0 + gm_ref[...]) * h)
            err = xhat2 * g2_ref[...] + b2_ref[...] - tgt_ref[...]
            small_ref[:, 3 * dm:] += jnp.sum(0.5 * _lanemean(err * err), axis=0, keepdims=True)
            dy = err * (1.0 / D_MODEL)
            small_ref[:, dm:2 * dm] += _rowsum(dy * xhat2)
            small_ref[:, 2 * dm:3 * dm] += _rowsum(dy)
            dr2 = _ln_bwd(dy * g2_ref[...], xhat2, rstd2)
            dr2_ref[...] = dr2
            small_ref[:, 0:dm] += _rowsum(dr2 * h)
            dh_ref[...] = ((1.0 + gm_ref[...]) * dr2).astype(BF16)

    row = pl.BlockSpec((tm, D_MODEL), lambda i, f: (i, 0))
    vec = _full((1, D_MODEL))
    return pl.pallas_call(
        body, name="mlp_fwd", grid=(T // tm, nf),
        in_specs=[row, pl.BlockSpec((MLP_SLABS, half, tf), lambda i, f: (f, 0, 0)),
                  pl.BlockSpec((MLP_SLABS, half, tf), lambda i, f: (f, 0, 0)),
                  pl.BlockSpec((MLP_SLABS, tf, D_MODEL), lambda i, f: (f, 0, 0)),
                  row, vec, vec, vec, vec, vec, row],
        out_specs=[pl.BlockSpec((tm, MLP_SLABS * tf), lambda i, f: (i, f)), row, row, _full((1, 3 * D_MODEL + 128))],
        out_shape=[jax.ShapeDtypeStruct((T, N_DEV * tf), BF16), jax.ShapeDtypeStruct((T, D_MODEL), F32),
                   jax.ShapeDtypeStruct((T, D_MODEL), BF16), jax.ShapeDtypeStruct((1, 3 * D_MODEL + 128), F32)],
        scratch_shapes=[pltpu.VMEM((tm, D_MODEL), F32)],
        compiler_params=_params(),
    )(u2, w1[0], w1[1], w2, xhat1, ln1_g, ln1_b, g_m, ln2_g, ln2_b, target)


def _mlp_bwd(dh, w1, w2, r, dr2, xhat1, rstd1, mix, ln1_g, ln1_b, sc_m, g_a):
    T = dh.shape[0]
    half, tf = w1[0].shape[1:]
    nf = N_DEV // MLP_SLABS
    tm = min(ROW_TILE, T)

    def body(dh_ref, w1a_ref, w1b_ref, w2_ref, r_ref, dr2_ref, xhat_ref, rstd_ref, mix_ref, g1_ref, b1_ref, sc_ref, ga_ref,
             dhpre_ref, dr1_ref, dmix_ref, small_ref, acc_ref):
        i, f = pl.program_id(0), pl.program_id(1)
        dm = D_MODEL

        @pl.when((i == 0) & (f == 0))
        def _():
            small_ref[...] = jnp.zeros_like(small_ref)

        @pl.when(f == 0)
        def _():
            acc_ref[...] = jnp.zeros_like(acc_ref)

        dht = dh_ref[...]
        part = None
        for s in range(MLP_SLABS):
            cols = slice(s * tf, (s + 1) * tf)
            dhpre = (_dot(dht, w2_ref[s], NT) * (2.0 * r_ref[:, cols].astype(F32))).astype(BF16)
            dhpre_ref[:, cols] = dhpre
            d = jnp.concatenate([_dot(dhpre, w1a_ref[s], NT), _dot(dhpre, w1b_ref[s], NT)], axis=1)
            part = d if part is None else part + d
        acc_ref[...] += part

        @pl.when(f == nf - 1)
        def _():
            du2 = acc_ref[...]
            xhat = xhat_ref[...]
            x1 = xhat * g1_ref[...] + b1_ref[...]
            dx1 = ALPHA * dr2_ref[...] + du2 * (1.0 + sc_ref[...])
            small_ref[:, 2 * dm:3 * dm] += _rowsum(du2 * x1)
            small_ref[:, dm:2 * dm] += _rowsum(du2)
            small_ref[:, 3 * dm:4 * dm] += _rowsum(dx1 * xhat)
            small_ref[:, 4 * dm:5 * dm] += _rowsum(dx1)
            dr1 = _ln_bwd(dx1 * g1_ref[...], xhat, rstd_ref[:, 0:1])
            dr1_ref[...] = dr1
            small_ref[:, 0:dm] += _rowsum(dr1 * mix_ref[...])
            dmix_ref[...] = ((1.0 + ga_ref[...]) * dr1).astype(BF16)

    row = pl.BlockSpec((tm, D_MODEL), lambda i, f: (i, 0))
    vec = _full((1, D_MODEL))
    return pl.pallas_call(
        body, name="mlp_bwd", grid=(T // tm, nf),
        in_specs=[row, pl.BlockSpec((MLP_SLABS, half, tf), lambda i, f: (f, 0, 0)),
                  pl.BlockSpec((MLP_SLABS, half, tf), lambda i, f: (f, 0, 0)),
                  pl.BlockSpec((MLP_SLABS, tf, D_MODEL), lambda i, f: (f, 0, 0)),
                  pl.BlockSpec((tm, MLP_SLABS * tf), lambda i, f: (i, f)), row, row,
                  pl.BlockSpec((tm, 128), lambda i, f: (i, 0)), row, vec, vec, vec, vec],
        out_specs=[pl.BlockSpec((tm, MLP_SLABS * tf), lambda i, f: (i, f)), row, row, _full((1, 5 * D_MODEL))],
        out_shape=[jax.ShapeDtypeStruct((T, N_DEV * tf), BF16), jax.ShapeDtypeStruct((T, D_MODEL), F32),
                   jax.ShapeDtypeStruct((T, D_MODEL), BF16), jax.ShapeDtypeStruct((1, 5 * D_MODEL), F32)],
        scratch_shapes=[pltpu.VMEM((tm, D_MODEL), F32)],
        compiler_params=_params(),
    )(dh, w1[0], w1[1], w2, r, dr2, xhat1, rstd1, mix, ln1_g, ln1_b, sc_m, g_a)


def _input_bwd(dz_h, dz_m, w_in_ext, x, dr1, sc_a, exchange=None):
    T = x.shape[0]
    tm = min(ROW_TILE, T)

    def body(dzh_ref, dzm_ref, w_ref, x_ref, dr1_ref, sc_ref, gx_ref, small_ref):
        @pl.when(pl.program_id(0) == 0)
        def _():
            small_ref[...] = jnp.zeros_like(small_ref)

        du = _bdot(dzh_ref[...], w_ref[0:2048, :]) + _bdot(dzm_ref[...], w_ref[2048:3072, :])
        gx_ref[...] = ALPHA * dr1_ref[...] + du * (1.0 + sc_ref[...])
        small_ref[:, D_MODEL:] += _rowsum(du * x_ref[...])
        small_ref[:, 0:D_MODEL] += _rowsum(du)

    row = pl.BlockSpec((tm, D_MODEL), lambda i: (i, 0))
    vec = _full((1, D_MODEL))
    return _call(
        body, "input_bwd", (dz_h, dz_m, w_in_ext, x, dr1, sc_a), grid=(T // tm,),
        in_specs=[pl.BlockSpec((tm, 2048), lambda i: (i, 0)), row, _full(w_in_ext.shape), row, row, vec],
        out_specs=[row, _full((1, 2 * D_MODEL))],
        out_shape=[jax.ShapeDtypeStruct((T, D_MODEL), F32), jax.ShapeDtypeStruct((1, 2 * D_MODEL), F32)],
        exchange=exchange)


def _adam_math(w, g, m, v):
    m = ADAM_B1 * m + (1.0 - ADAM_B1) * g
    v = ADAM_B2 * v + (1.0 - ADAM_B2) * (g * g)
    m_hat = m / (1.0 - ADAM_B1 ** ADAM_STEP)
    v_hat = v / (1.0 - ADAM_B2 ** ADAM_STEP)
    return -ADAM_LR * (m_hat / (jnp.sqrt(v_hat) + ADAM_EPS) + ADAM_WD * w), m, v


def _adam(g_slabs, w, m, v, name, g_fn=None, g_extra=()):
    R, C = w.shape
    tr = 256 if R % 256 == 0 else R
    ns = 0 if g_slabs is None else g_slabs.shape[0]
    slab_rows = tr if g_slabs is None or g_slabs.shape[1] == R else g_slabs.shape[1]
    assert slab_rows == tr or tr == R
    ne = len(g_extra)

    def body(*refs):
        e_refs = refs[:ne]
        refs = refs[ne:]
        if ns:
            gs_ref, refs = refs[0], refs[1:]
        w_ref, m_ref, v_ref, g_ref, d_ref, nm_ref, nv_ref = refs
        if g_fn is not None:
            g = g_fn(*e_refs)
        else:
            g = gs_ref[0].astype(F32)
            for s in range(1, ns):
                g = g + gs_ref[s].astype(F32)
            g = g[:tr]
        d, nm, nv = _adam_math(w_ref[...], g, m_ref[...], v_ref[...])
        g_ref[...] = g
        d_ref[...] = d
        nm_ref[...] = nm
        nv_ref[...] = nv

    blk = pl.BlockSpec((tr, C), lambda i: (i, 0))
    in_specs = [pl.BlockSpec((tr, e.shape[1]), lambda i: (i, 0)) if e.shape[0] == R else _full(e.shape) for e in g_extra]
    args = list(g_extra)
    if ns:
        in_specs.append(pl.BlockSpec((ns, slab_rows, C), lambda i: (0, i, 0)))
        args.append(g_slabs)
    return pl.pallas_call(
        body, name=name, grid=(R // tr,), in_specs=in_specs + [blk] * 3, out_specs=[blk] * 4,
        out_shape=[jax.ShapeDtypeStruct((R, C), F32)] * 4, compiler_params=_params(),
    )(*args, w, m, v)


def _adam_small(small_all, params):
    n = len(params)

    def body(*refs):
        s_ref, refs = refs[0], refs[1:]
        wmv, loss_ref, outs = refs[:3 * n], refs[3 * n], refs[3 * n + 1:]
        tot = s_ref[0]
        for i in range(1, N_DEV):
            tot = tot + s_ref[i]
        loss_ref[...] = tot[:, SMALL_W - 128:]
        for j, (w, _, _, off) in enumerate(params):
            w_ref, m_ref, v_ref = wmv[3 * j:3 * j + 3]
            g_ref, d_ref, nm_ref, nv_ref = outs[4 * j:4 * j + 4]
            if w.shape[0] == 2:
                lb = _lower_bound(w_ref)
                g0 = tot[:, off:off + w.shape[1]] * lb * (1.0 - lb)
                rows = [(slice(0, 1), g0), (slice(1, 2), -g0)]
            else:
                rows = [(slice(0, 1), tot[:, off:off + w.shape[1]])]
            for rs, g in rows:
                d, nm, nv = _adam_math(w_ref[rs, :], g, m_ref[rs, :], v_ref[rs, :])
                g_ref[rs, :], d_ref[rs, :], nm_ref[rs, :], nv_ref[rs, :] = g, d, nm, nv

    out_shape = [jax.ShapeDtypeStruct((1, 128), F32)]
    for w, _, _, _ in params:
        out_shape += [jax.ShapeDtypeStruct(w.shape, F32)] * 4
    res = pl.pallas_call(body, name="adam_small", out_shape=out_shape, compiler_params=_params())(
        small_all, *[a for w, m, v, _ in params for a in (w, m, v)])
    return res[0], [tuple(res[1 + 4 * j:5 + 4 * j]) for j in range(n)]


def _cols_from_slabs(g):
    s, r, c = g.shape
    return jnp.transpose(g, (1, 0, 2)).reshape(r, s * c)


def _slabs_from_cols(w):
    r, c = w.shape
    return jnp.transpose(w.reshape(r, N_DEV, c // N_DEV), (1, 0, 2))


def _rot_half_rows(wt):
    return jnp.concatenate([-wt[32:], wt[:32]], axis=0)


def _unrot_half_rows(dwt_rot):
    return jnp.concatenate([dwt_rot[32:], -dwt_rot[:32]], axis=0)


def _ext_in_t(g):
    n, rows, k_in = g.shape
    keep = n * rows - ROPE_DIM

    def body(g_ref, o_ref, stage_ref):
        stage_ref[keep:, :] = jnp.zeros((o_ref.shape[0] - keep, k_in), F32)
        for i in range(n - 1):
            stage_ref[rows * i:rows * (i + 1), :] = g_ref[i].astype(F32)
        last = g_ref[n - 1].astype(F32)
        stage_ref[rows * (n - 1):keep, :] = last[:rows - ROPE_DIM]
        wk = last[rows - ROPE_DIM:]
        stage_ref[keep + 128:keep + 192, :] = wk
        stage_ref[keep + 384:keep + 416, :] = -wk[32:]
        stage_ref[keep + 416:keep + 448, :] = wk[:32]
        o_ref[...] = stage_ref[...].astype(BF16)

    return pl.pallas_call(body, name="ext_w_in", out_shape=jax.ShapeDtypeStruct((keep + 512, k_in), BF16),
                          scratch_shapes=[pltpu.VMEM((keep + 512, k_in), F32)], compiler_params=_params())(g)


def _ext_q_t(wt):
    r = wt.shape[1]
    z64, z128 = jnp.zeros((64, r), BF16), jnp.zeros((128, r), BF16)
    per = HEAD_DIM + ROPE_DIM
    main = [jnp.concatenate([wt[per * h:per * (h + 1)], z64], axis=0) for h in range(HEADS)]
    rot = [jnp.concatenate([z128, _rot_half_rows(wt[per * h + HEAD_DIM:per * (h + 1)]), z64], axis=0)
           for h in range(HEADS)]
    return jnp.concatenate(main + rot, axis=0)


def _ext_kv(w_kv_up):
    r = w_kv_up.shape[0]
    z128 = jnp.zeros((r, 128), BF16)
    wkv = w_kv_up.reshape(r, HEADS, 2 * HEAD_DIM)
    kpad = [jnp.concatenate([wkv[:, h, :HEAD_DIM], z128], axis=1) for h in range(HEADS)]
    vals = [wkv[:, h, HEAD_DIM:] for h in range(HEADS)]
    return jnp.concatenate(kpad + vals, axis=1)


def _w_in_grad_slabs(dwt_h, dwt_m):
    d = dwt_h.shape[1]
    rows = (dwt_h.shape[0] + 512 + ROPE_DIM) // N_DEV
    padded = rows + (-rows) % 16

    def body(h_ref, m_ref, o_ref, stage_ref):
        stage_ref[0:2048, :] = h_ref[...]
        stage_ref[2048:2560, :] = m_ref[0:512, :]
        rot = m_ref[768 + 128:768 + 192, :]
        stage_ref[2560:2592, :] = m_ref[640:672, :] + rot[32:]
        stage_ref[2592:2624, :] = m_ref[672:704, :] - rot[:32]
        zero = jnp.zeros((padded - rows, d), F32)
        for i in range(N_DEV):
            o_ref[i] = jnp.concatenate([stage_ref[rows * i:rows * (i + 1), :], zero], axis=0).astype(BF16)

    return pl.pallas_call(body, name="w_in_grad_slabs", out_shape=jax.ShapeDtypeStruct((N_DEV, padded, d), BF16),
                          scratch_shapes=[pltpu.VMEM((N_DEV * rows, d), F32)], compiler_params=_params())(dwt_h, dwt_m)


def _grad_q_from_ext_t(dwq_ext_t):
    rows = []
    for h in range(HEADS):
        main, rot = dwq_ext_t[256 * h:256 * h + 256], dwq_ext_t[1024 + 256 * h:1280 + 256 * h]
        rows += [main[:128], main[128:192] + _unrot_half_rows(rot[128:192])]
    return jnp.concatenate(rows, axis=0)


def _grad_kv_from_ext(dwkv_ext):
    kvcols = []
    for h in range(HEADS):
        kvcols += [dwkv_ext[:, 256 * h:256 * h + 128], dwkv_ext[:, 1024 + 128 * h:1152 + 128 * h]]
    return jnp.concatenate(kvcols, axis=1)


SMALL_W = 6144 + 512 + 512 + 256 + 256 + 4 * 1024 + 128


def kernel(x, c, positions, w_ada, b_ada, w_in, hg_lower_bounds, hg_norm_w, mla_q_norm_w, w_q_up, mla_kv_norm_w, w_kv_up, w_out, ln1_g, ln1_b, w_mlp_in, w_mlp_out, ln2_g, ln2_b, loss_target, m_w_ada, m_b_ada, m_w_in, m_hg_lower_bounds, m_hg_norm_w, m_mla_q_norm_w, m_w_q_up, m_mla_kv_norm_w, m_w_kv_up, m_w_out, m_ln1_g, m_ln1_b, m_w_mlp_in, m_w_mlp_out, m_ln2_g, m_ln2_b, v_w_ada, v_b_ada, v_w_in, v_hg_lower_bounds, v_hg_norm_w, v_mla_q_norm_w, v_w_q_up, v_mla_kv_norm_w, v_w_kv_up, v_w_out, v_ln1_g, v_ln1_b, v_w_mlp_in, v_w_mlp_out, v_ln2_g, v_ln2_b):
    T = x.shape[1]
    me = 4 * lax.axis_index("x") + 2 * lax.axis_index("y") + lax.axis_index("c")
    xs, tgt = x[0], loss_target[0]
    transposed = ("w_in", "w_q_up")
    as_used = lambda n, a: a[0].T if n in transposed else a[0]
    big = {n: as_used(n, a) for n, a in dict(w_in=w_in, w_q_up=w_q_up, w_kv_up=w_kv_up, w_out=w_out,
                                              w_mlp_in=w_mlp_in, w_mlp_out=w_mlp_out).items()}
    names = list(big)

    bf = {n: big[n].astype(BF16) for n in ("w_in", "w_q_up", "w_kv_up", "w_out")}
    g_in, g_c = _gather_two_level([bf["w_in"], c], name="gather_w_in")
    c_all = g_c.reshape(N_DEV, D_MODEL)

    ada_cols = w_ada.shape[2]
    mod_part, cond = _mod_part(c_all, w_ada[0], lax.dynamic_slice(b_ada, (0, me * ada_cols), (1, ada_cols)))
    (mod_all,) = _exchange([mod_part], scatter=False, name="gather_mod")
    mod_row = lax.dynamic_slice(mod_all, (0, me, 0), (N_DEV, 1, ada_cols)).reshape(1, N_DEV * ada_cols)
    sh_a, sc_a, g_a, sh_m, sc_m, g_m = [mod_row[:, D_MODEL * i:D_MODEL * (i + 1)] for i in range(6)]

    w_in_ext = _ext_in_t(g_in)
    half = D_MODEL // 2
    z, (w1_top,) = _matmul(xs, w_in_ext, "NT", "in_proj", a_fn=_modulate, extras=(sc_a, sh_a), tn=3072,
                           exchange=_StagedGather(big["w_mlp_in"], rows=(0, half)))
    (o_raw, o_gated, s_prev), (g_q, g_kv, g_out) = _hgrn_fwd(
        z, hg_lower_bounds, hg_norm_w, exchange=_Exchange([bf["w_q_up"], bf["w_kv_up"], bf["w_out"]], False))
    wq_ext = _ext_q_t(g_q.reshape(N_DEV * g_q.shape[1], g_q.shape[2]))
    wkv_ext = _ext_kv(_cols_from_slabs(g_kv))
    w_out_full = g_out.reshape(D_MODEL, D_MODEL)
    inv_freq = 1.0 / (ROPE_THETA ** (jnp.arange(0, ROPE_DIM, 2, dtype=F32) / ROPE_DIM))
    zeros = lambda n: jnp.zeros((n,), F32)
    invf = jnp.concatenate([zeros(128), inv_freq, inv_freq, zeros(64)]).reshape(1, QK_PAD)
    m_rot = jnp.concatenate([zeros(128), jnp.ones((64,), F32), zeros(64)]).reshape(1, QK_PAD)
    q, k, v, c1, s1, cqn, ckvn = _mla_pre(z, positions.reshape(T, 1), invf, m_rot, wq_ext, wkv_ext,
                                          mla_q_norm_w, mla_kv_norm_w)[0]
    (o_mla, lse), (w1_bot, w2) = _attn_fwd(
        q, k, v, exchange=[_StagedGather(big["w_mlp_in"], 0.75, rows=(half, half)),
                           _StagedGather(big["w_mlp_out"], 0.75)])
    w1 = (w1_top, w1_bot)
    mix, xhat1, rstd1, u2 = _mix_ln1(o_gated, o_mla, w_out_full, xs, g_a, ln1_g, ln1_b, sc_m, sh_m)[0]
    r, dr2, dh, small_mlp_fwd = _mlp_fwd(u2, w1, w2, xhat1, ln1_g, ln1_b, g_m, ln2_g, ln2_b, tgt)

    dhpre, dr1, dmix, small_mlp_bwd = _mlp_bwd(dh, w1, w2, r, dr2, xhat1, rstd1, mix, ln1_g, ln1_b, sc_m, g_a)
    received = {}
    dw2 = _matmul(r, dh, "TN", "wgrad_mlp_out", out_dtype=BF16, a_fn=_square, tm=1024, tk=2048)
    dw1 = _matmul(u2, dhpre, "TN", "wgrad_mlp_in", out_dtype=BF16, tm=1024, tk=2048, out_slabs=N_DEV)
    dmixcat = _matmul(dmix, w_out_full, "NT", "dgrad_out", tm=1024)
    dw_out = jnp.concatenate([_matmul(o_gated, dmix, "TN", "wgrad_out_hg", out_dtype=BF16, tk=2048),
                              _matmul(o_mla, dmix, "TN", "wgrad_out_mla", out_dtype=BF16, tk=2048)], axis=0)
    (dz_h, small_hgrn), (received["w_mlp_in"],) = _hgrn_bwd(
        dmixcat, z, o_raw, s_prev, hg_lower_bounds, hg_norm_w, exchange=_StagedScatter(dw1, 0.1))
    (dq, dk, dv), (received["w_mlp_out"], received["w_out"]) = _attn_bwd(
        q, k, v, dmixcat, o_mla, lse,
        exchange=_Exchange([dw2.reshape(N_DEV, dw2.shape[0] // N_DEV, D_MODEL),
                            dw_out.reshape(N_DEV, D_MODEL // N_DEV, D_MODEL)], True))
    dz_m, dq_ext, dkv_ext, small_mla = _mla_bwd(dq, dk, dv, z, c1, s1, wq_ext, wkv_ext, mla_q_norm_w, mla_kv_norm_w)
    dwq_t = _grad_q_from_ext_t(_matmul(dq_ext, cqn, "TN", "wgrad_q_up", tm=1024, tk=2048))
    dwkv = _grad_kv_from_ext(_matmul(ckvn, dkv_ext, "TN", "wgrad_kv_up", tn=1536, tk=2048))
    qkv_slabs = [dwq_t.reshape((N_DEV, dwq_t.shape[0] // N_DEV, dwq_t.shape[1])).astype(BF16),
                 _slabs_from_cols(dwkv).astype(BF16)]
    dwt_h = _matmul(dz_h, xs, "TN", "wgrad_in_h", b_fn=_modulate, extras=(sc_a, sh_a), tm=1024, tk=2048)
    dwt_m = _matmul(dz_m, xs, "TN", "wgrad_in_m", b_fn=_modulate, extras=(sc_a, sh_a), tm=1024, tk=2048)
    in_slabs = _w_in_grad_slabs(dwt_h, dwt_m)
    (grad_x, small_in), (received["w_q_up"], received["w_kv_up"], received["w_in"]) = _input_bwd(
        dz_h, dz_m, w_in_ext, xs, dr1, sc_a,
        exchange=[_Exchange(qkv_slabs, True), _StagedScatter(in_slabs)])

    small = jnp.concatenate([small_in, small_mlp_bwd[:, :3 * D_MODEL], small_mlp_fwd[:, :D_MODEL], small_hgrn,
                             small_mla, small_mlp_bwd[:, 3 * D_MODEL:], small_mlp_fwd[:, D_MODEL:]], axis=1)
    assert small.shape == (1, SMALL_W)
    (small_all,) = _exchange([small], scatter=False, name="gather_small")

    moments = dict(w_in=(m_w_in, v_w_in), w_q_up=(m_w_q_up, v_w_q_up), w_kv_up=(m_w_kv_up, v_w_kv_up),
                   w_out=(m_w_out, v_w_out), w_mlp_in=(m_w_mlp_in, v_w_mlp_in), w_mlp_out=(m_w_mlp_out, v_w_mlp_out))
    res = {}
    for n in names:
        res[n] = _adam(received[n], big[n], as_used(n, moments[n][0]), as_used(n, moments[n][1]), name="adam_" + n)
    dmod_cols = lax.dynamic_slice(small_all.reshape(N_DEV, SMALL_W), (0, me * ada_cols), (N_DEV, ada_cols))
    cond_t = cond.T

    def ada_grad(ct_ref, dm_ref):
        g = ct_ref[:, 0:1] * dm_ref[0:1, :]
        for b in range(1, N_DEV):
            g = g + ct_ref[:, b:b + 1] * dm_ref[b:b + 1, :]
        return g

    res["w_ada"] = _adam(None, w_ada[0], m_w_ada[0], v_w_ada[0], name="adam_w_ada", g_fn=ada_grad,
                         g_extra=(cond_t, dmod_cols))

    small_params = [("b_ada", b_ada, m_b_ada, v_b_ada, 0),
                    ("hg_lower_bounds", hg_lower_bounds, m_hg_lower_bounds, v_hg_lower_bounds, 6144),
                    ("hg_norm_w", hg_norm_w, m_hg_norm_w, v_hg_norm_w, 6656),
                    ("mla_q_norm_w", mla_q_norm_w, m_mla_q_norm_w, v_mla_q_norm_w, 7168),
                    ("mla_kv_norm_w", mla_kv_norm_w, m_mla_kv_norm_w, v_mla_kv_norm_w, 7424),
                    ("ln1_g", ln1_g, m_ln1_g, v_ln1_g, 7680), ("ln1_b", ln1_b, m_ln1_b, v_ln1_b, 8704),
                    ("ln2_g", ln2_g, m_ln2_g, v_ln2_g, 9728), ("ln2_b", ln2_b, m_ln2_b, v_ln2_b, 10752)]
    loss_row, small_res = _adam_small(small_all, [p[1:] for p in small_params])
    for p, r4 in zip(small_params, small_res):
        res[p[0]] = r4
    loss = loss_row[0, 0]

    order = ["w_ada", "b_ada", "w_in", "hg_lower_bounds", "hg_norm_w", "mla_q_norm_w", "w_q_up", "mla_kv_norm_w",
             "w_kv_up", "w_out", "ln1_g", "ln1_b", "w_mlp_in", "w_mlp_out", "ln2_g", "ln2_b"]
    def as_given(n, a):
        if n in transposed:
            a = a.T
        return a[None] if n in big or n == "w_ada" else a

    shaped = {n: tuple(as_given(n, a) for a in res[n]) for n in order}
    outs = [loss, grad_x.reshape(1, T, D_MODEL)]
    for i in range(4):
        outs += [shaped[n][i] for n in order]
    return tuple(outs)
```

```python
import jax
import jax.numpy as jnp
import numpy as np
from jax import lax
from jax.experimental import pallas as pl
from jax.experimental.pallas import tpu as pltpu

F32, BF16 = jnp.float32, jnp.bfloat16
N_DEV = 8
D_MODEL = 1024
HEADS = 4
HEAD_DIM = 128
ROPE_DIM = 64
QK_PAD = 256
CHUNK = 64
ROPE_THETA = 10000.0
RMS_EPS = 1e-6
LN_EPS = 1e-5
ALPHA = 2.0 ** 0.25
ATT_SCALE = (HEAD_DIM + ROPE_DIM) ** -0.5
LN2 = float(np.log(2.0))
Q_PRESCALE = ATT_SCALE / LN2
ADAM_LR, ADAM_B1, ADAM_B2, ADAM_EPS, ADAM_WD, ADAM_STEP = 0.001, 0.9, 0.999, 1e-08, 0.01, 10
NEG_BIG = -1e30

ROW_TILE = 512
ATT_TILE = 512
HGRN_GROUP = 8
MLP_SLABS = 4
VMEM_LIMIT = 56 * 2 ** 20

NN = (((1,), (0,)), ((), ()))
NT = (((1,), (1,)), ((), ()))
TN = (((0,), (0,)), ((), ()))


def _dot(a, b, dims=NN):
    return lax.dot_general(a, b, dims, preferred_element_type=F32)


def _bdot(a, b, dims=NN):
    return lax.dot_general(a.astype(BF16), b.astype(BF16), dims, preferred_element_type=F32)


def _hdot(a, b, dims=NN):
    return lax.dot_general(a, b, dims, precision=lax.Precision.HIGHEST, preferred_element_type=F32)


def _params():
    return pltpu.CompilerParams(vmem_limit_bytes=VMEM_LIMIT)


def _sigmoid(x):
    return 1.0 / (1.0 + jnp.exp(-x))


def _rowsum(x):
    return jnp.sum(x, axis=0, keepdims=True)


def _lanemean(x):
    return jnp.mean(x, axis=-1, keepdims=True)


def _full(shape):
    nd = len(shape)
    return pl.BlockSpec(shape, lambda *_: (0,) * nd)


class _Exchange:
    def __init__(self, arrs, scatter):
        self.arrs, self.scatter, self.n, self.aliases, self.middle_at = list(arrs), scatter, len(arrs), [], 0.5
        self.out_shape = [jax.ShapeDtypeStruct((N_DEV,) + (a.shape[1:] if scatter else a.shape), a.dtype)
                          for a in self.arrs]
        n = self.n
        self.scratch = [pltpu.SemaphoreType.DMA((n, N_DEV - 1)), pltpu.SemaphoreType.DMA((n, N_DEV - 1)),
                        pltpu.SemaphoreType.DMA((n,))]

    def _copies(self, ins, outs, sems):
        send_sems, recv_sems, loc_sems = sems
        x, y, c = lax.axis_index("x"), lax.axis_index("y"), lax.axis_index("c")
        me = 4 * x + 2 * y + c
        copies = []
        for k in range(self.n):
            src_of = (lambda i, k=k: ins[k].at[i]) if self.scatter else (lambda i, k=k: ins[k])
            copies.append((pltpu.make_async_copy(src_of(me), outs[k].at[me], loc_sems.at[k]), None))
            for p in range(1, N_DEV):
                px = (1 - x) if p & 4 else x
                py = (1 - y) if p & 2 else y
                pc = (1 - c) if p & 1 else c
                peer = 4 * px + 2 * py + pc
                both = dict(send_sem=send_sems.at[k, p - 1], recv_sem=recv_sems.at[k, p - 1],
                            device_id=(px, py, pc), device_id_type=pl.DeviceIdType.MESH)
                send = pltpu.make_async_remote_copy(src_ref=src_of(peer), dst_ref=outs[k].at[me], **both)
                recv = pltpu.make_async_remote_copy(src_ref=src_of(peer), dst_ref=outs[k].at[peer], **both)
                copies.append((send, recv))
        return copies

    def start(self, ins, outs, sems):
        for first, _ in self._copies(ins, outs, sems):
            first.start()

    def middle(self, ins, outs, sems):
        pass

    def finish(self, ins, outs, sems):
        pass

    def wait(self, ins, outs, sems):
        for first, recv in self._copies(ins, outs, sems):
            if recv is None:
                first.wait()
            else:
                recv.wait_recv()
                first.wait_send()


class _StagedGather:
    def __init__(self, arr, middle_at=0.8, rows=None):
        self.r0, n = rows if rows else (0, arr.shape[0])
        block = (n,) + arr.shape[1:]
        self.arrs, self.aliases, self.middle_at = [arr], [], middle_at
        self.out_shape = [jax.ShapeDtypeStruct((N_DEV,) + block, BF16)]
        self.scratch = [pltpu.VMEM((N_DEV,) + block, BF16), pltpu.SemaphoreType.DMA((7,)),
                        pltpu.SemaphoreType.DMA((7,)), pltpu.SemaphoreType.DMA((9,)), pltpu.VMEM(block, arr.dtype)]

    def _parts(self, scr):
        stage, send_sems, recv_sems, loc_sems = scr[:4]
        x, y, c = lax.axis_index("x"), lax.axis_index("y"), lax.axis_index("c")
        me, sibling = (x, y, c), (x, y, 1 - c)
        chips = [(1 - x, y), (x, 1 - y), (1 - x, 1 - y)]

        def copy(j, block, to):
            px, py, pc = block
            slot = stage.at[4 * px + 2 * py + pc]
            return pltpu.make_async_remote_copy(src_ref=slot, dst_ref=slot, send_sem=send_sems.at[j],
                                                recv_sem=recv_sems.at[j], device_id=to,
                                                device_id_type=pl.DeviceIdType.MESH)

        return stage, loc_sems, me, sibling, chips, c, copy

    def start(self, ins, outs, scr):
        stage, loc_sems, me, sibling, chips, c, copy = self._parts(scr)
        x, y, _ = me
        raw = scr[4]
        own = pltpu.make_async_copy(ins[0].at[pl.ds(self.r0, raw.shape[0])], raw, loc_sems.at[0])
        own.start()
        own.wait()
        stage[4 * x + 2 * y + c] = raw[...].astype(BF16)
        copy(0, me, sibling).start()
        for j, chip in enumerate(chips):
            copy(1 + j, me, (*chip, c)).start()
        self._write(scr, outs, 0, me).start()

    def _write(self, scr, outs, k, block):
        px, py, pc = block
        slot = 4 * px + 2 * py + pc
        return pltpu.make_async_copy(scr[0].at[slot], outs[0].at[slot], scr[3].at[1 + k])

    def _blocks(self, scr):
        _, _, me, sibling, chips, c, _ = self._parts(scr)
        return [me] + [(*chip, c) for chip in chips] + [sibling] + [(*chip, 1 - c) for chip in chips]

    def middle(self, ins, outs, scr):
        stage, loc_sems, me, sibling, chips, c, copy = self._parts(scr)
        for j, chip in enumerate(chips):
            copy(1 + j, (*chip, c), me).wait_recv()
            copy(4 + j, (*chip, c), sibling).start()
            self._write(scr, outs, 1 + j, (*chip, c)).start()

    def wait(self, ins, outs, scr):
        stage, loc_sems, me, sibling, chips, c, copy = self._parts(scr)
        copy(0, sibling, me).wait_recv()
        self._write(scr, outs, 4, sibling).start()
        for j, chip in enumerate(chips):
            copy(4 + j, (*chip, 1 - c), me).wait_recv()
            self._write(scr, outs, 5 + j, (*chip, 1 - c)).start()
        copy(0, me, sibling).wait_send()
        for j, chip in enumerate(chips):
            copy(1 + j, me, (*chip, c)).wait_send()
            copy(4 + j, (*chip, c), sibling).wait_send()

    def finish(self, ins, outs, scr):
        for k, block in enumerate(self._blocks(scr)):
            self._write(scr, outs, k, block).wait()


class _StagedScatter:
    def __init__(self, slabs, middle_at=0.2):
        _, r, c = slabs.shape
        self.arrs, self.aliases, self.middle_at = [slabs], [], middle_at
        self.out_shape = [jax.ShapeDtypeStruct((4, r, c), slabs.dtype)]
        self.scratch = [pltpu.VMEM((N_DEV, r, c), slabs.dtype), pltpu.VMEM((4, r, c), slabs.dtype),
                        pltpu.VMEM((3, r, c), slabs.dtype), pltpu.SemaphoreType.DMA((4,)), pltpu.SemaphoreType.DMA((4,)),
                        pltpu.SemaphoreType.DMA((3,)), pltpu.SemaphoreType.DMA((3,)), pltpu.SemaphoreType.DMA((4,))]

    def _parts(self, scr):
        stage, from_sib, from_chips, sib_send, sib_recv, ici_send, ici_recv, loc_sems = scr
        x, y, c = lax.axis_index("x"), lax.axis_index("y"), lax.axis_index("c")
        chips = [(1 - x, y), (x, 1 - y), (1 - x, 1 - y)]

        def to_sibling(j):
            return pltpu.make_async_remote_copy(src_ref=stage.at[2 * j + 1 - c], dst_ref=from_sib.at[j],
                                                send_sem=sib_send.at[j], recv_sem=sib_recv.at[j],
                                                device_id=(x, y, 1 - c), device_id_type=pl.DeviceIdType.MESH)

        def to_chip(k):
            px, py = chips[k]
            return pltpu.make_async_remote_copy(src_ref=stage.at[4 * px + 2 * py + c], dst_ref=from_chips.at[k],
                                                send_sem=ici_send.at[k], recv_sem=ici_recv.at[k],
                                                device_id=(px, py, c), device_id_type=pl.DeviceIdType.MESH)

        return stage, from_sib, from_chips, loc_sems, (x, y, c), chips, to_sibling, to_chip

    def start(self, ins, outs, scr):
        stage, _, _, loc_sems, _, _, to_sibling, _ = self._parts(scr)
        load = pltpu.make_async_copy(ins[0], stage, loc_sems.at[0])
        load.start()
        load.wait()
        for j in range(4):
            to_sibling(j).start()

    def middle(self, ins, outs, scr):
        stage, from_sib, _, _, (x, y, c), _, to_sibling, to_chip = self._parts(scr)
        for j in range(4):
            to_sibling(j).wait_recv()
            mine = stage.at[2 * j + c]
            mine[...] = (mine[...].astype(F32) + from_sib[j].astype(F32)).astype(mine.dtype)
        for k in range(3):
            to_chip(k).start()

    def wait(self, ins, outs, scr):
        stage, _, from_chips, loc_sems, (x, y, c), chips, to_sibling, to_chip = self._parts(scr)
        writes = [pltpu.make_async_copy(stage.at[4 * x + 2 * y + c], outs[0].at[2 * x + y], loc_sems.at[0])]
        for k, (px, py) in enumerate(chips):
            to_chip(k).wait_recv()
            writes.append(pltpu.make_async_copy(from_chips.at[k], outs[0].at[2 * px + py], loc_sems.at[1 + k]))
        for w in writes:
            w.start()
        for j in range(4):
            to_sibling(j).wait_send()
        for k in range(3):
            to_chip(k).wait_send()
        for w in writes:
            w.wait()

    def finish(self, ins, outs, scr):
        pass


def _call(body, name, args, out_shape, grid=(), in_specs=(), out_specs=(), scratch_shapes=(), exchange=None):
    if exchange is None:
        return pl.pallas_call(body, name=name, grid=grid, in_specs=list(in_specs), out_specs=list(out_specs),
                              out_shape=list(out_shape), scratch_shapes=list(scratch_shapes),
                              compiler_params=_params())(*args), None
    exs = list(exchange) if isinstance(exchange, (list, tuple)) else [exchange]
    ni, no, ns = len(args), len(out_shape), len(scratch_shapes)
    nxi, nxo = sum(len(e.arrs) for e in exs), sum(len(e.out_shape) for e in exs)
    steps = int(np.prod(grid))
    mid_step = lambda e: min(max(int(steps * e.middle_at), 1), steps - 1)
    aliases, iat, oat = {}, ni, no
    for e in exs:
        for src, dst in e.aliases:
            aliases[iat + src] = oat + dst
        iat, oat = iat + len(e.arrs), oat + len(e.out_shape)

    def wrapped(*refs):
        a, xi = refs[:ni], refs[ni:ni + nxi]
        o, xo = refs[ni + nxi:ni + nxi + no], refs[ni + nxi + no:ni + nxi + no + nxo]
        s, xs = refs[ni + nxi + no + nxo:ni + nxi + no + nxo + ns], refs[ni + nxi + no + nxo + ns:]
        parts, iat, oat, sat = [], 0, 0, 0
        for e in exs:
            parts.append((e, xi[iat:iat + len(e.arrs)], xo[oat:oat + len(e.out_shape)], xs[sat:sat + len(e.scratch)]))
            iat, oat, sat = iat + len(e.arrs), oat + len(e.out_shape), sat + len(e.scratch)
        step = 0
        for d, g in enumerate(grid):
            step = step * g + pl.program_id(d)

        @pl.when(step == 0)
        def _():
            for e, ins, outs, sems in parts:
                e.start(ins, outs, sems)

        for at_step in sorted({mid_step(e) for e in exs}):
            @pl.when(step == at_step)
            def _():
                for e, ins, outs, sems in parts:
                    if mid_step(e) == at_step:
                        e.middle(ins, outs, sems)

        body(*a, *o, *s)

        @pl.when(step == steps - 1)
        def _():
            for e, ins, outs, sems in parts:
                e.wait(ins, outs, sems)
            for e, ins, outs, sems in parts:
                e.finish(ins, outs, sems)

    hbm = pl.BlockSpec(memory_space=pltpu.HBM)
    res = pl.pallas_call(
        wrapped, name=name, grid=grid, in_specs=list(in_specs) + [hbm] * nxi, out_specs=list(out_specs) + [hbm] * nxo,
        out_shape=list(out_shape) + [o_ for e in exs for o_ in e.out_shape],
        scratch_shapes=list(scratch_shapes) + [s_ for e in exs for s_ in e.scratch],
        input_output_aliases=aliases, compiler_params=_params())(*args, *[a_ for e in exs for a_ in e.arrs])
    return res[:no], res[no:]


def _gather_two_level(arrs, name):
    n = len(arrs)
    out_shape = [jax.ShapeDtypeStruct((N_DEV,) + a.shape, a.dtype) for a in arrs]

    def body(*refs):
        ins, outs = refs[:n], refs[n:2 * n]
        send_sems, recv_sems, loc_sems = refs[2 * n:]
        x, y, c = lax.axis_index("x"), lax.axis_index("y"), lax.axis_index("c")
        me, sibling = (x, y, c), (x, y, 1 - c)
        chips = [(1 - x, y), (x, 1 - y), (1 - x, 1 - y)]

        def copy(k, j, block, to, src=None):
            px, py, pc = block
            dst = outs[k].at[4 * px + 2 * py + pc]
            return pltpu.make_async_remote_copy(src_ref=dst if src is None else src, dst_ref=dst,
                                                send_sem=send_sems.at[k, j], recv_sem=recv_sems.at[k, j],
                                                device_id=to, device_id_type=pl.DeviceIdType.MESH)

        mine = [pltpu.make_async_copy(ins[k], outs[k].at[4 * x + 2 * y + c], loc_sems.at[k]) for k in range(n)]
        first = []
        for k in range(n):
            mine[k].start()
            first.append(copy(k, 0, me, sibling, src=ins[k]))
            first += [copy(k, 1 + j, me, (*chip, c), src=ins[k]) for j, chip in enumerate(chips)]
        for cp in first:
            cp.start()
        passed = []
        for j, chip in enumerate(chips):
            for k in range(n):
                copy(k, 1 + j, (*chip, c), me).wait_recv()
                passed.append(copy(k, 4 + j, (*chip, c), sibling))
                passed[-1].start()
        for k in range(n):
            copy(k, 0, sibling, me).wait_recv()
            for j, chip in enumerate(chips):
                copy(k, 4 + j, (*chip, 1 - c), me).wait_recv()
        for cp in first + passed:
            cp.wait_send()
        for cp in mine:
            cp.wait()

    vmem = pl.BlockSpec(memory_space=pltpu.VMEM)
    return pl.pallas_call(body, name=name, out_shape=out_shape, in_specs=[vmem] * n, out_specs=[vmem] * n,
                          scratch_shapes=[pltpu.SemaphoreType.DMA((n, 7)), pltpu.SemaphoreType.DMA((n, 7)),
                                          pltpu.SemaphoreType.DMA((n,))], compiler_params=_params())(*arrs)


def _exchange(arrs, scatter, name):
    ex = _Exchange(arrs, scatter)

    def body(*refs):
        ins, outs, sems = refs[:ex.n], refs[ex.n:2 * ex.n], refs[2 * ex.n:]
        ex.start(ins, outs, sems)
        ex.wait(ins, outs, sems)

    hbm = pl.BlockSpec(memory_space=pltpu.HBM)
    return pl.pallas_call(body, name=name, out_shape=ex.out_shape, in_specs=[hbm] * ex.n, out_specs=[hbm] * ex.n,
                          scratch_shapes=ex.scratch)(*ex.arrs)


def _matmul(a, b, mode, name, out_dtype=F32, tm=512, tn=1024, tk=1024, a_fn=None, b_fn=None, extras=(),
            out_slabs=None, exchange=None):
    assert not (a_fn and b_fn) and not (b_fn and mode == "NT")
    if mode == "NN":
        (M, K), N = a.shape, b.shape[1]
    elif mode == "NT":
        (M, K), N = a.shape, b.shape[0]
    else:
        (K, M), N = a.shape, b.shape[1]
    slab_w = N // out_slabs if out_slabs else None
    if out_slabs:
        tn = max(slab_w, min(tn, N) // slab_w * slab_w)
    tm, tn, tk = min(tm, M), min(tn, N), min(tk, K)
    assert M % tm == 0 and N % tn == 0 and K % tk == 0, (name, M, N, K)
    nk = K // tk
    dims = {"NN": NN, "NT": NT, "TN": TN}[mode]
    ne = len(extras)

    def body(a_ref, b_ref, *rest):
        e_refs, o_ref, acc_ref = rest[:ne], rest[ne], rest[ne + 1]
        k = pl.program_id(2)

        @pl.when(k == 0)
        def _():
            acc_ref[...] = jnp.zeros_like(acc_ref)

        at, bt = a_ref[...], b_ref[...]
        if a_fn is not None:
            at = a_fn(at.astype(F32), *[e[...] for e in e_refs])
        if b_fn is not None:
            bt = b_fn(bt.astype(F32), *[e[...] for e in e_refs])
        acc_ref[...] += _bdot(at, bt, dims)

        @pl.when(k == nk - 1)
        def _():
            if out_slabs:
                for s in range(tn // slab_w):
                    o_ref[s] = acc_ref[:, s * slab_w:(s + 1) * slab_w].astype(out_dtype)
            else:
                o_ref[...] = acc_ref[...].astype(out_dtype)

    if mode == "TN":
        a_spec = pl.BlockSpec((tk, tm), lambda i, j, k: (k, i))
        e_spec = pl.BlockSpec((1, tm), lambda i, j, k: (0, i))
    else:
        a_spec = pl.BlockSpec((tm, tk), lambda i, j, k: (i, k))
        e_spec = pl.BlockSpec((1, tk), lambda i, j, k: (0, k))
    if mode == "NT":
        b_spec = pl.BlockSpec((tn, tk), lambda i, j, k: (j, k))
    else:
        b_spec = pl.BlockSpec((tk, tn), lambda i, j, k: (k, j))
    if b_fn is not None:
        e_spec = pl.BlockSpec((1, tn), lambda i, j, k: (0, j))
    if out_slabs:
        o_shape = jax.ShapeDtypeStruct((out_slabs, M, slab_w), out_dtype)
        o_spec = pl.BlockSpec((tn // slab_w, tm, slab_w), lambda i, j, k: (j, i, 0))
    else:
        o_shape = jax.ShapeDtypeStruct((M, N), out_dtype)
        o_spec = pl.BlockSpec((tm, tn), lambda i, j, k: (i, j))
    (out,), got = _call(body, name, (a, b, *extras), [o_shape], grid=(M // tm, N // tn, nk),
                        in_specs=[a_spec, b_spec] + [e_spec] * ne, out_specs=[o_spec],
                        scratch_shapes=[pltpu.VMEM((tm, tn), F32)], exchange=exchange)
    return out if exchange is None else (out, got)


def _modulate(x, sc, sh):
    return x * (1.0 + sc) + sh


def _square(x):
    return x * x


def _mod_part(c_all, w_ada_s, b_s):
    def body(c_ref, w_ref, b_ref, mod_ref, cond_ref):
        cv = c_ref[...]
        cond = cv * _sigmoid(cv)
        cond_ref[...] = cond
        mod_ref[...] = _bdot(cond, w_ref[...]) + b_ref[...]

    return pl.pallas_call(
        body, name="mod_part",
        out_shape=[jax.ShapeDtypeStruct((N_DEV, w_ada_s.shape[1]), F32), jax.ShapeDtypeStruct(c_all.shape, F32)],
        compiler_params=_params(),
    )(c_all, w_ada_s, b_s)


def _rms_fwd(x, w):
    rs = lax.rsqrt(_lanemean(x * x) + RMS_EPS)
    return x * rs * w, rs


def _rms_bwd(x, rs, w, dy):
    xhat = x * rs
    dxh = dy * w
    return rs * (dxh - xhat * _lanemean(dxh * xhat)), dy * xhat


def _mla_pre(z, pos_col, invf, m_rot, wq_ext, wkv_ext, qnw, kvnw, exchange=None):
    T = z.shape[0]
    tm = min(ROW_TILE, T)

    def body(z_ref, pos_ref, invf_ref, mrot_ref, wq_ref, wkv_ref, qnw_ref, kvnw_ref,
             q_ref, k_ref, v_ref, c1_ref, s1_ref, cqn_ref, ckvn_ref):
        hi = slice(HEAD_DIM, QK_PAD)
        ang = pos_ref[...].astype(F32) * invf_ref[:, hi]
        c1 = jnp.concatenate([jnp.ones((tm, HEAD_DIM), F32), mrot_ref[:, hi] * jnp.cos(ang)], axis=1)
        s1 = jnp.concatenate([jnp.zeros((tm, HEAD_DIM), F32), mrot_ref[:, hi] * jnp.sin(ang)], axis=1)
        c1_ref[...] = c1
        s1_ref[...] = s1
        cqn, _ = _rms_fwd(z_ref[:, 0:256], qnw_ref[...])
        ckvn, _ = _rms_fwd(z_ref[:, 256:512], kvnw_ref[...])
        cqn_ref[...] = cqn.astype(BF16)
        ckvn_ref[...] = ckvn.astype(BF16)
        qe = _bdot(cqn, wq_ref[...], NT)
        kve = _bdot(ckvn, wkv_ref[...])
        k_rope = z_ref[:, 512:768] * c1 + z_ref[:, 768:1024] * s1
        for h in range(HEADS):
            q_ref[h] = ((qe[:, 256 * h:256 * h + 256] * c1 + qe[:, 1024 + 256 * h:1280 + 256 * h] * s1)
                        * Q_PRESCALE).astype(BF16)
            k_ref[h] = (kve[:, 256 * h:256 * h + 256] + k_rope).astype(BF16)
            v_ref[h] = kve[:, 1024 + 128 * h:1152 + 128 * h].astype(BF16)

    row = lambda i: (i, 0)
    head = lambda i: (0, i, 0)
    return _call(
        body, "mla_pre", (z, pos_col, invf, m_rot, wq_ext, wkv_ext, qnw, kvnw), grid=(T // tm,),
        in_specs=[pl.BlockSpec((tm, 1024), lambda i: (i, 2)), pl.BlockSpec((tm, 1), row),
                  _full((1, 256)), _full((1, 256)), _full(wq_ext.shape), _full(wkv_ext.shape),
                  _full((1, 256)), _full((1, 256))],
        out_specs=[pl.BlockSpec((HEADS, tm, QK_PAD), head), pl.BlockSpec((HEADS, tm, QK_PAD), head),
                   pl.BlockSpec((HEADS, tm, HEAD_DIM), head), pl.BlockSpec((tm, 256), row), pl.BlockSpec((tm, 256), row),
                   pl.BlockSpec((tm, 256), row), pl.BlockSpec((tm, 256), row)],
        out_shape=[jax.ShapeDtypeStruct((HEADS, T, QK_PAD), BF16), jax.ShapeDtypeStruct((HEADS, T, QK_PAD), BF16),
                   jax.ShapeDtypeStruct((HEADS, T, HEAD_DIM), BF16), jax.ShapeDtypeStruct((T, 256), F32),
                   jax.ShapeDtypeStruct((T, 256), F32), jax.ShapeDtypeStruct((T, 256), BF16),
                   jax.ShapeDtypeStruct((T, 256), BF16)], exchange=exchange)


def _mla_bwd(dq, dk, dv, z, c1, s1, wq_ext, wkv_ext, qnw, kvnw):
    T = z.shape[0]
    tm = min(ROW_TILE, T)

    def body(dq_ref, dk_ref, dv_ref, z_ref, c1_ref, s1_ref, wq_ref, wkv_ref, qnw_ref, kvnw_ref,
             dz_ref, dqe_ref, dkve_ref, dnw_ref):
        @pl.when(pl.program_id(0) == 0)
        def _():
            dnw_ref[...] = jnp.zeros_like(dnw_ref)

        c1, s1 = c1_ref[...], s1_ref[...]
        dkpe = jnp.zeros((tm, QK_PAD), F32)
        for h in range(HEADS):
            dqh, dkh = dq_ref[h].astype(F32) * ATT_SCALE, dk_ref[h]
            dqe_ref[:, 256 * h:256 * h + 256] = (dqh * c1).astype(BF16)
            dqe_ref[:, 1024 + 256 * h:1280 + 256 * h] = (dqh * s1).astype(BF16)
            dkve_ref[:, 256 * h:256 * h + 256] = dkh
            dkve_ref[:, 1024 + 128 * h:1152 + 128 * h] = dv_ref[h]
            dkpe = dkpe + dkh.astype(F32)
        dcqn = _dot(dqe_ref[...], wq_ref[...])
        dckvn = _dot(dkve_ref[...], wkv_ref[...], NT)
        cq, ckv = z_ref[:, 0:256], z_ref[:, 256:512]
        _, rsq = _rms_fwd(cq, qnw_ref[...])
        _, rskv = _rms_fwd(ckv, kvnw_ref[...])
        dcq, wq_rows = _rms_bwd(cq, rsq, qnw_ref[...], dcqn)
        dckv, wkv_rows = _rms_bwd(ckv, rskv, kvnw_ref[...], dckvn)
        dnw_ref[:, 0:256] += _rowsum(wq_rows)
        dnw_ref[:, 256:512] += _rowsum(wkv_rows)
        dz_ref[:, 0:256] = dcq.astype(BF16)
        dz_ref[:, 256:512] = dckv.astype(BF16)
        dz_ref[:, 512:768] = (dkpe * c1).astype(BF16)
        dz_ref[:, 768:1024] = (dkpe * s1).astype(BF16)

    row = lambda i: (i, 0)
    head = lambda i: (0, i, 0)
    return pl.pallas_call(
        body, name="mla_bwd", grid=(T // tm,),
        in_specs=[pl.BlockSpec((HEADS, tm, QK_PAD), head), pl.BlockSpec((HEADS, tm, QK_PAD), head),
                  pl.BlockSpec((HEADS, tm, HEAD_DIM), head), pl.BlockSpec((tm, 1024), lambda i: (i, 2)),
                  pl.BlockSpec((tm, 256), row), pl.BlockSpec((tm, 256), row), _full(wq_ext.shape), _full(wkv_ext.shape),
                  _full((1, 256)), _full((1, 256))],
        out_specs=[pl.BlockSpec((tm, 1024), row), pl.BlockSpec((tm, 2048), row), pl.BlockSpec((tm, 1536), row),
                   _full((1, 512))],
        out_shape=[jax.ShapeDtypeStruct((T, 1024), BF16), jax.ShapeDtypeStruct((T, 2048), BF16),
                   jax.ShapeDtypeStruct((T, 1536), BF16), jax.ShapeDtypeStruct((1, 512), F32)],
        compiler_params=_params(),
    )(dq, dk, dv, z, c1, s1, wq_ext, wkv_ext, qnw, kvnw)


_HEAD_LANES = [slice(HEAD_DIM * h, HEAD_DIM * (h + 1)) for h in range(HEADS)]


def _lower_bound(lbraw_ref):
    a0, a1 = lbraw_ref[0:1, :], lbraw_ref[1:2, :]
    mx = jnp.maximum(a0, a1)
    e0, e1 = jnp.exp(a0 - mx), jnp.exp(a1 - mx)
    return e0 / (e0 + e1)


def _tri(lower):
    r = lax.broadcasted_iota(jnp.int32, (CHUNK, CHUNK), 0)
    c = lax.broadcasted_iota(jnp.int32, (CHUNK, CHUNK), 1)
    return (r >= c) if lower else (r <= c)


def _hgrn_gates(q, f, lb, tri_lo):
    sg = _sigmoid(f)
    forget = lb + (1.0 - lb) * sg
    k = 1.0 - forget
    b = _hdot(tri_lo.astype(F32), jnp.log(forget))
    b_ref, b_last = b[CHUNK // 2 - 1:CHUNK // 2, :], b[CHUNK - 1:CHUNK, :]
    e1, e2, e3, e4 = jnp.exp(b - b_ref), jnp.exp(b_ref - b), jnp.exp(b_last - b), jnp.exp(b)
    return dict(sg=sg, forget=forget, k=k, e1=e1, e2=e2, e3=e3, e4=e4, qa=q * e1, ka=k * e2, kl=k * e3, qb=q * e4,
                decay=jnp.exp(b_last))


def _hgrn_fwd(z, lbraw, nw, exchange=None):
    T = z.shape[0]
    G = min(HGRN_GROUP, T // CHUNK)
    rows = G * CHUNK
    n_chunks = T // CHUNK

    def body(q_ref, f_ref, i_ref, g_ref, lbraw_ref, nw_ref, oraw_ref, og_ref, sp_ref, st_ref):
        @pl.when(pl.program_id(0) == 0)
        def _():
            st_ref[...] = jnp.zeros_like(st_ref)

        lb_all = _lower_bound(lbraw_ref)
        tri_lo = _tri(True)

        def chunk(cc, carry):
            rs = pl.ds(pl.multiple_of(cc * CHUNK, CHUNK), CHUNK)
            t = _hgrn_gates(q_ref[rs, :], f_ref[rs, :], lb_all, tri_lo)
            v, gate = i_ref[rs, :], g_ref[rs, :]
            st = [st_ref[h] for h in range(HEADS)]
            a = [jnp.where(tri_lo, _bdot(t["qa"][:, s], t["ka"][:, s], NT), 0.0) for s in _HEAD_LANES]
            kv = [_bdot(v[:, s], t["kl"][:, s], TN) for s in _HEAD_LANES]
            o = [_bdot(a[h], v[:, s]) + _bdot(t["qb"][:, s], st[h], NT) for h, s in enumerate(_HEAD_LANES)]
            for h, s in enumerate(_HEAD_LANES):
                sp_ref[cc, h] = st[h]
                st_ref[h] = st[h] * t["decay"][:, s] + kv[h]
            oraw_ref[rs, :] = jnp.concatenate(o, axis=1)
            on = jnp.concatenate([_rms_fwd(o[h], nw_ref[:, s])[0] for h, s in enumerate(_HEAD_LANES)], axis=1)
            og_ref[rs, :] = (on * (gate * _sigmoid(gate))).astype(BF16)
            return carry

        lax.fori_loop(0, G, chunk, 0, unroll=4)

    col = lambda j: pl.BlockSpec((rows, 512), lambda r, j=j: (r, j))
    return _call(
        body, "hgrn_fwd", (z, z, z, z, lbraw, nw), grid=(T // rows,),
        in_specs=[col(0), col(1), col(2), col(3), _full((2, 512)), _full((1, 512))],
        out_specs=[col(0), col(0), pl.BlockSpec((G, HEADS, HEAD_DIM, HEAD_DIM), lambda r: (r, 0, 0, 0))],
        out_shape=[jax.ShapeDtypeStruct((T, 512), F32), jax.ShapeDtypeStruct((T, 512), BF16),
                   jax.ShapeDtypeStruct((n_chunks, HEADS, HEAD_DIM, HEAD_DIM), F32)],
        scratch_shapes=[pltpu.VMEM((HEADS, HEAD_DIM, HEAD_DIM), F32)], exchange=exchange)


def _hgrn_bwd(dmixcat, z, oraw, sprev, lbraw, nw, exchange=None):
    T = z.shape[0]
    G = min(HGRN_GROUP, T // CHUNK)
    rows = G * CHUNK
    ng = T // rows

    def body(dog_ref, q_ref, f_ref, i_ref, g_ref, oraw_ref, sp_ref, lbraw_ref, nw_ref,
             dz_ref, dsmall_ref, dst_ref):
        @pl.when(pl.program_id(0) == 0)
        def _():
            dst_ref[...] = jnp.zeros_like(dst_ref)
            dsmall_ref[...] = jnp.zeros_like(dsmall_ref)

        lb_all = _lower_bound(lbraw_ref)
        tri_lo, tri_up = _tri(True), _tri(False)
        rowid = lax.broadcasted_iota(jnp.int32, (CHUNK, HEADS * HEAD_DIM), 0)

        def chunk(it, carry):
            cc = G - 1 - it
            rs = pl.ds(pl.multiple_of(cc * CHUNK, CHUNK), CHUNK)
            heads = list(enumerate(_HEAD_LANES))
            cat = lambda parts: jnp.concatenate(parts, axis=1)
            per_head_mean = lambda x: cat([jnp.broadcast_to(_lanemean(x[:, s]), (CHUNK, HEAD_DIM)) for s in _HEAD_LANES])
            t = _hgrn_gates(q_ref[rs, :], f_ref[rs, :], lb_all, tri_lo)
            v, gate, o, dog, nw_all = i_ref[rs, :], g_ref[rs, :], oraw_ref[rs, :], dog_ref[rs, :], nw_ref[...]
            rs_o = lax.rsqrt(per_head_mean(o * o) + RMS_EPS)
            xhat = o * rs_o
            sgg = _sigmoid(gate)
            d_on = dog * (gate * sgg)
            dz_ref[rs, 1536:2048] = (dog * (xhat * nw_all) * (sgg * (1.0 + gate * (1.0 - sgg)))).astype(BF16)
            dxh = d_on * nw_all
            do = rs_o * (dxh - xhat * per_head_mean(dxh * xhat))
            dsmall_ref[:, 512:1024] += _rowsum(d_on * xhat)
            st = [sp_ref[cc, h] for h in range(HEADS)]
            dst = [dst_ref[h] for h in range(HEADS)]
            a = [jnp.where(tri_lo, _bdot(t["qa"][:, s], t["ka"][:, s], NT), 0.0) for s in _HEAD_LANES]
            da = [jnp.where(tri_lo, _bdot(do[:, s], v[:, s], NT), 0.0) for s in _HEAD_LANES]
            dqb = cat([_bdot(do[:, s], st[h]) for h, s in heads])
            dkl = cat([_bdot(v[:, s], dst[h]) for h, s in heads])
            dv_ = cat([_bdot(t["kl"][:, s], dst[h], NT) + _bdot(a[h], do[:, s], TN) for h, s in heads])
            dqa = cat([_bdot(da[h], t["ka"][:, s]) for h, s in heads])
            dka = cat([_bdot(da[h], t["qa"][:, s], TN) for h, s in heads])
            ddecay = cat([_rowsum(dst[h] * st[h]) for h in range(HEADS)])
            for h, s in heads:
                dst_ref[h] = dst[h] * t["decay"][:, s] + _bdot(do[:, s], t["qb"][:, s], TN)
            pa, pk, pb, pl_ = dqa * t["qa"], dka * t["ka"], dqb * t["qb"], dkl * t["kl"]
            db = pa - pk + pb - pl_
            db = db + jnp.where(rowid == CHUNK // 2 - 1, _rowsum(pk - pa), 0.0)
            db = db + jnp.where(rowid == CHUNK - 1, _rowsum(pl_) + ddecay * t["decay"], 0.0)
            dlogf = _hdot(tri_up.astype(F32), db)
            dforget = dlogf / t["forget"] - (dka * t["e2"] + dkl * t["e3"])
            sg = t["sg"]
            dz_ref[rs, 0:512] = (dqa * t["e1"] + dqb * t["e4"]).astype(BF16)
            dz_ref[rs, 512:1024] = (dforget * (1.0 - lb_all) * sg * (1.0 - sg)).astype(BF16)
            dz_ref[rs, 1024:1536] = dv_.astype(BF16)
            dsmall_ref[:, 0:512] += _rowsum(dforget * (1.0 - sg))
            return carry

        lax.fori_loop(0, G, chunk, 0, unroll=4)

    col = lambda j: pl.BlockSpec((rows, 512), lambda r, j=j: (ng - 1 - r, j))
    return _call(
        body, "hgrn_bwd", (dmixcat, z, z, z, z, oraw, sprev, lbraw, nw), grid=(ng,),
        in_specs=[col(0), col(0), col(1), col(2), col(3), col(0),
                  pl.BlockSpec((G, HEADS, HEAD_DIM, HEAD_DIM), lambda r: (ng - 1 - r, 0, 0, 0)),
                  _full((2, 512)), _full((1, 512))],
        out_specs=[pl.BlockSpec((rows, 2048), lambda r: (ng - 1 - r, 0)), _full((1, 1024))],
        out_shape=[jax.ShapeDtypeStruct((T, 2048), BF16), jax.ShapeDtypeStruct((1, 1024), F32)],
        scratch_shapes=[pltpu.VMEM((HEADS, HEAD_DIM, HEAD_DIM), F32)], exchange=exchange)


def _diag_mask(t):
    r = lax.broadcasted_iota(jnp.int32, (t, t), 0)
    c = lax.broadcasted_iota(jnp.int32, (t, t), 1)
    return r >= c


def _attn_fwd(q, k, v, exchange=None):
    _, T, _ = q.shape
    t = min(ATT_TILE, T)

    def body(q_ref, k_ref, v_ref, o_ref, lse_ref):
        i = pl.program_id(1)
        qb = q_ref[...]

        rows = lambda j: pl.ds(pl.multiple_of(j * t, t), t)

        def logits(j, masked):
            s = _dot(qb, k_ref[rows(j), :], NT)
            return jnp.where(_diag_mask(t), s, NEG_BIG) if masked else s

        def absorb(s, j, carry):
            m, l, acc = carry
            mn = jnp.maximum(m, jnp.max(s, axis=-1, keepdims=True))
            p = jnp.exp2(s - mn)
            al = jnp.exp2(m - mn)
            return mn, al * l + jnp.sum(p, axis=-1, keepdims=True), al * acc + _dot(p.astype(BF16), v_ref[rows(j), :])

        def pair(j0, carry, last_masked):
            s0, s1 = logits(j0, False), logits(j0 + 1, last_masked)
            return absorb(s1, j0 + 1, absorb(s0, j0, carry))

        init = (jnp.full((t, 1), NEG_BIG, F32), jnp.zeros((t, 1), F32), jnp.zeros((t, HEAD_DIM), F32))
        carry = lax.fori_loop(0, i // 2, lambda jj, c: pair(2 * jj, c, False), init)
        m, l, acc = lax.cond(i % 2 == 1, lambda c: pair(i - 1, c, True),
                             lambda c: absorb(logits(i, True), i, c), carry)
        o_ref[...] = acc / l
        lse_ref[...] = jnp.broadcast_to(m + jnp.log2(l), (t, HEAD_DIM))

    return _call(
        body, "attn_fwd", (q, k, v), grid=(HEADS, T // t),
        in_specs=[pl.BlockSpec((None, t, QK_PAD), lambda h, i: (h, i, 0)),
                  pl.BlockSpec((None, T, QK_PAD), lambda h, i: (h, 0, 0)),
                  pl.BlockSpec((None, T, HEAD_DIM), lambda h, i: (h, 0, 0))],
        out_specs=[pl.BlockSpec((t, HEAD_DIM), lambda h, i: (i, h)),
                   pl.BlockSpec((None, t, HEAD_DIM), lambda h, i: (h, i, 0))],
        out_shape=[jax.ShapeDtypeStruct((T, HEADS * HEAD_DIM), F32), jax.ShapeDtypeStruct((HEADS, T, HEAD_DIM), F32)],
        exchange=exchange)


def _attn_bwd(q, k, v, dmixcat, o, lse, exchange=None):
    _, T, _ = q.shape
    t = min(ATT_TILE, T)
    nq = T // t

    def body(q_ref, k_ref, v_ref, do_ref, o_ref, lse_ref, dq_ref, dk_ref, dv_ref, delta_ref, dq_acc):
        j = pl.program_id(1)

        @pl.when(j == 0)
        def _():
            dq_acc[...] = jnp.zeros_like(dq_acc)

            def fill(i, carry):
                rs = pl.ds(pl.multiple_of(i * t, t), t)
                delta_ref[rs, :] = jnp.broadcast_to(
                    jnp.sum(do_ref[rs, :] * o_ref[rs, :], axis=-1, keepdims=True), (t, HEAD_DIM))
                return carry

            lax.fori_loop(0, nq, fill, 0)

        kb, vb = k_ref[...], v_ref[...]

        def steps(blocks, carry):
            dk, dv = carry
            rs = [pl.ds(pl.multiple_of(i * t, t), t) for i, _ in blocks]
            qb = [q_ref[r, :] for r in rs]
            dob = [do_ref[r, :].astype(BF16) for r in rs]
            s = [_dot(b, kb, NT) for b in qb]
            dp = [_dot(b, vb, NT) for b in dob]
            for n, (_, shift) in enumerate(blocks):
                p = jnp.exp2(s[n] - lse_ref[rs[n], 0:1])
                if shift is not None:
                    row = lax.broadcasted_iota(jnp.int32, (t, 2 * t), 0)
                    col = lax.broadcasted_iota(jnp.int32, (t, 2 * t), 1)
                    p = jnp.where(col <= row + shift, p, 0.0)
                ds = (p * (dp[n] - delta_ref[rs[n], 0:1])).astype(BF16)
                dq_acc[rs[n], :] += _dot(ds, kb)
                dk = dk + _dot(ds, qb[n], TN)
                dv = dv + _dot(p.astype(BF16), dob[n], TN)
            return dk, dv

        zero = (jnp.zeros((2 * t, QK_PAD), F32), jnp.zeros((2 * t, HEAD_DIM), F32))
        carry = steps([(2 * j, 0), (2 * j + 1, t)], zero)
        first = 2 * j + 2
        dk, dv = lax.fori_loop(0, (nq - first) // 2, lambda n, c: steps([(first + 2 * n, None), (first + 2 * n + 1, None)], c),
                               carry)
        dk_ref[...] = (dk * LN2).astype(BF16)
        dv_ref[...] = dv.astype(BF16)

        @pl.when(j == nk - 1)
        def _():
            dq_ref[...] = dq_acc[...].astype(BF16)

    nk = nq // 2
    return _call(
        body, "attn_bwd", (q, k, v, dmixcat, o, lse), grid=(HEADS, nk),
        in_specs=[pl.BlockSpec((None, T, QK_PAD), lambda h, j: (h, 0, 0)),
                  pl.BlockSpec((None, 2 * t, QK_PAD), lambda h, j: (h, j, 0)),
                  pl.BlockSpec((None, 2 * t, HEAD_DIM), lambda h, j: (h, j, 0)),
                  pl.BlockSpec((T, HEAD_DIM), lambda h, j: (0, HEADS + h)),
                  pl.BlockSpec((T, HEAD_DIM), lambda h, j: (0, h)),
                  pl.BlockSpec((None, T, HEAD_DIM), lambda h, j: (h, 0, 0))],
        out_specs=[pl.BlockSpec((None, T, QK_PAD), lambda h, j: (h, 0, 0)),
                   pl.BlockSpec((None, 2 * t, QK_PAD), lambda h, j: (h, j, 0)),
                   pl.BlockSpec((None, 2 * t, HEAD_DIM), lambda h, j: (h, j, 0))],
        out_shape=[jax.ShapeDtypeStruct((HEADS, T, QK_PAD), BF16), jax.ShapeDtypeStruct((HEADS, T, QK_PAD), BF16),
                   jax.ShapeDtypeStruct((HEADS, T, HEAD_DIM), BF16)],
        scratch_shapes=[pltpu.VMEM((T, HEAD_DIM), F32), pltpu.VMEM((T, QK_PAD), F32)], exchange=exchange)


def _ln_fwd(r):
    mu = _lanemean(r)
    xc = r - mu
    rstd = lax.rsqrt(_lanemean(xc * xc) + LN_EPS)
    return xc * rstd, rstd


def _ln_bwd(dxh, xhat, rstd):
    return rstd * (dxh - _lanemean(dxh) - xhat * _lanemean(dxh * xhat))


def _mix_ln1(o_hg, o_mla, w_out, x, g_a, ln1_g, ln1_b, sc_m, sh_m, exchange=None):
    T = x.shape[0]
    tm = min(ROW_TILE, T)
    half = o_hg.shape[1]

    def body(hg_ref, mla_ref, w_ref, x_ref, ga_ref, g_ref, b_ref, sc_ref, sh_ref, mix_ref, xhat_ref, rstd_ref, u2_ref):
        mix = _dot(hg_ref[...], w_ref[0:half, :]) + _bdot(mla_ref[...], w_ref[half:, :])
        mix_ref[...] = mix
        xhat, rstd = _ln_fwd(ALPHA * x_ref[...] + (1.0 + ga_ref[...]) * mix)
        xhat_ref[...] = xhat
        rstd_ref[...] = jnp.broadcast_to(rstd, (tm, 128))
        u2_ref[...] = _modulate(xhat * g_ref[...] + b_ref[...], sc_ref[...], sh_ref[...]).astype(BF16)

    row = pl.BlockSpec((tm, D_MODEL), lambda i: (i, 0))
    vec = _full((1, D_MODEL))
    halfrow = pl.BlockSpec((tm, half), lambda i: (i, 0))
    return _call(
        body, "mix_ln1", (o_hg, o_mla, w_out, x, g_a, ln1_g, ln1_b, sc_m, sh_m), grid=(T // tm,),
        in_specs=[halfrow, halfrow, _full(w_out.shape), row, vec, vec, vec, vec, vec],
        out_specs=[row, row, pl.BlockSpec((tm, 128), lambda i: (i, 0)), row],
        out_shape=[jax.ShapeDtypeStruct((T, D_MODEL), F32), jax.ShapeDtypeStruct((T, D_MODEL), F32),
                   jax.ShapeDtypeStruct((T, 128), F32), jax.ShapeDtypeStruct((T, D_MODEL), BF16)],
        exchange=exchange)


def _mlp_fwd(u2, w1, w2, xhat1, ln1_g, ln1_b, g_m, ln2_g, ln2_b, target):
    T = u2.shape[0]
    half, tf = w1[0].shape[1:]
    nf = N_DEV // MLP_SLABS
    tm = min(ROW_TILE, T)

    def body(u2_ref, w1a_ref, w1b_ref, w2_ref, xhat_ref, g1_ref, b1_ref, gm_ref, g2_ref, b2_ref, tgt_ref,
             r_ref, dr2_ref, dh_ref, small_ref, acc_ref):
        i, f = pl.program_id(0), pl.program_id(1)
        dm = D_MODEL

        @pl.when((i == 0) & (f == 0))
        def _():
            small_ref[...] = jnp.zeros_like(small_ref)

        @pl.when(f == 0)
        def _():
            acc_ref[...] = jnp.zeros_like(acc_ref)

        u2t = u2_ref[...]
        part = None
        for s in range(MLP_SLABS):
            r = jnp.maximum(_dot(u2t[:, :half], w1a_ref[s]) + _dot(u2t[:, half:], w1b_ref[s]), 0.0)
            r_ref[:, s * tf:(s + 1) * tf] = r.astype(BF16)
            d = _bdot(r * r, w2_ref[s])
            part = d if part is None else part + d
        acc_ref[...] += part

        @pl.when(f == nf - 1)
        def _():
            h = acc_ref[...]
            x1 = xhat_ref[...] * g1_ref[...] + b1_ref[...]
            xhat2, rstd2 = _ln_fwd(ALPHA * x1 + (1.0 + gm_ref[...]) * h)
            err = xhat2 * g2_ref[...] + b2_ref[...] - tgt_ref[...]
            small_ref[:, 3 * dm:] += jnp.sum(0.5 * _lanemean(err * err), axis=0, keepdims=True)
            dy = err * (1.0 / D_MODEL)
            small_ref[:, dm:2 * dm] += _rowsum(dy * xhat2)
            small_ref[:, 2 * dm:3 * dm] += _rowsum(dy)
            dr2 = _ln_bwd(dy * g2_ref[...], xhat2, rstd2)
            dr2_ref[...] = dr2
            small_ref[:, 0:dm] += _rowsum(dr2 * h)
            dh_ref[...] = ((1.0 + gm_ref[...]) * dr2).astype(BF16)

    row = pl.BlockSpec((tm, D_MODEL), lambda i, f: (i, 0))
    vec = _full((1, D_MODEL))
    return pl.pallas_call(
        body, name="mlp_fwd", grid=(T // tm, nf),
        in_specs=[row, pl.BlockSpec((MLP_SLABS, half, tf), lambda i, f: (f, 0, 0)),
                  pl.BlockSpec((MLP_SLABS, half, tf), lambda i, f: (f, 0, 0)),
                  pl.BlockSpec((MLP_SLABS, tf, D_MODEL), lambda i, f: (f, 0, 0)),
                  row, vec, vec, vec, vec, vec, row],
        out_specs=[pl.BlockSpec((tm, MLP_SLABS * tf), lambda i, f: (i, f)), row, row, _full((1, 3 * D_MODEL + 128))],
        out_shape=[jax.ShapeDtypeStruct((T, N_DEV * tf), BF16), jax.ShapeDtypeStruct((T, D_MODEL), F32),
                   jax.ShapeDtypeStruct((T, D_MODEL), BF16), jax.ShapeDtypeStruct((1, 3 * D_MODEL + 128), F32)],
        scratch_shapes=[pltpu.VMEM((tm, D_MODEL), F32)],
        compiler_params=_params(),
    )(u2, w1[0], w1[1], w2, xhat1, ln1_g, ln1_b, g_m, ln2_g, ln2_b, target)


def _mlp_bwd(dh, w1, w2, r, dr2, xhat1, rstd1, mix, ln1_g, ln1_b, sc_m, g_a):
    T = dh.shape[0]
    half, tf = w1[0].shape[1:]
    nf = N_DEV // MLP_SLABS
    tm = min(ROW_TILE, T)

    def body(dh_ref, w1a_ref, w1b_ref, w2_ref, r_ref, dr2_ref, xhat_ref, rstd_ref, mix_ref, g1_ref, b1_ref, sc_ref, ga_ref,
             dhpre_ref, dr1_ref, dmix_ref, small_ref, acc_ref):
        i, f = pl.program_id(0), pl.program_id(1)
        dm = D_MODEL

        @pl.when((i == 0) & (f == 0))
        def _():
            small_ref[...] = jnp.zeros_like(small_ref)

        @pl.when(f == 0)
        def _():
            acc_ref[...] = jnp.zeros_like(acc_ref)

        dht = dh_ref[...]
        part = None
        for s in range(MLP_SLABS):
            cols = slice(s * tf, (s + 1) * tf)
            dhpre = (_dot(dht, w2_ref[s], NT) * (2.0 * r_ref[:, cols].astype(F32))).astype(BF16)
            dhpre_ref[:, cols] = dhpre
            d = jnp.concatenate([_dot(dhpre, w1a_ref[s], NT), _dot(dhpre, w1b_ref[s], NT)], axis=1)
            part = d if part is None else part + d
        acc_ref[...] += part

        @pl.when(f == nf - 1)
        def _():
            du2 = acc_ref[...]
            xhat = xhat_ref[...]
            x1 = xhat * g1_ref[...] + b1_ref[...]
            dx1 = ALPHA * dr2_ref[...] + du2 * (1.0 + sc_ref[...])
            small_ref[:, 2 * dm:3 * dm] += _rowsum(du2 * x1)
            small_ref[:, dm:2 * dm] += _rowsum(du2)
            small_ref[:, 3 * dm:4 * dm] += _rowsum(dx1 * xhat)
            small_ref[:, 4 * dm:5 * dm] += _rowsum(dx1)
            dr1 = _ln_bwd(dx1 * g1_ref[...], xhat, rstd_ref[:, 0:1])
            dr1_ref[...] = dr1
            small_ref[:, 0:dm] += _rowsum(dr1 * mix_ref[...])
            dmix_ref[...] = ((1.0 + ga_ref[...]) * dr1).astype(BF16)

    row = pl.BlockSpec((tm, D_MODEL), lambda i, f: (i, 0))
    vec = _full((1, D_MODEL))
    return pl.pallas_call(
        body, name="mlp_bwd", grid=(T // tm, nf),
        in_specs=[row, pl.BlockSpec((MLP_SLABS, half, tf), lambda i, f: (f, 0, 0)),
                  pl.BlockSpec((MLP_SLABS, half, tf), lambda i, f: (f, 0, 0)),
                  pl.BlockSpec((MLP_SLABS, tf, D_MODEL), lambda i, f: (f, 0, 0)),
                  pl.BlockSpec((tm, MLP_SLABS * tf), lambda i, f: (i, f)), row, row,
                  pl.BlockSpec((tm, 128), lambda i, f: (i, 0)), row, vec, vec, vec, vec],
        out_specs=[pl.BlockSpec((tm, MLP_SLABS * tf), lambda i, f: (i, f)), row, row, _full((1, 5 * D_MODEL))],
        out_shape=[jax.ShapeDtypeStruct((T, N_DEV * tf), BF16), jax.ShapeDtypeStruct((T, D_MODEL), F32),
                   jax.ShapeDtypeStruct((T, D_MODEL), BF16), jax.ShapeDtypeStruct((1, 5 * D_MODEL), F32)],
        scratch_shapes=[pltpu.VMEM((tm, D_MODEL), F32)],
        compiler_params=_params(),
    )(dh, w1[0], w1[1], w2, r, dr2, xhat1, rstd1, mix, ln1_g, ln1_b, sc_m, g_a)


def _input_bwd(dz_h, dz_m, w_in_ext, x, dr1, sc_a, exchange=None):
    T = x.shape[0]
    tm = min(ROW_TILE, T)

    def body(dzh_ref, dzm_ref, w_ref, x_ref, dr1_ref, sc_ref, gx_ref, small_ref):
        @pl.when(pl.program_id(0) == 0)
        def _():
            small_ref[...] = jnp.zeros_like(small_ref)

        du = _bdot(dzh_ref[...], w_ref[0:2048, :]) + _bdot(dzm_ref[...], w_ref[2048:3072, :])
        gx_ref[...] = ALPHA * dr1_ref[...] + du * (1.0 + sc_ref[...])
        small_ref[:, D_MODEL:] += _rowsum(du * x_ref[...])
        small_ref[:, 0:D_MODEL] += _rowsum(du)

    row = pl.BlockSpec((tm, D_MODEL), lambda i: (i, 0))
    vec = _full((1, D_MODEL))
    return _call(
        body, "input_bwd", (dz_h, dz_m, w_in_ext, x, dr1, sc_a), grid=(T // tm,),
        in_specs=[pl.BlockSpec((tm, 2048), lambda i: (i, 0)), row, _full(w_in_ext.shape), row, row, vec],
        out_specs=[row, _full((1, 2 * D_MODEL))],
        out_shape=[jax.ShapeDtypeStruct((T, D_MODEL), F32), jax.ShapeDtypeStruct((1, 2 * D_MODEL), F32)],
        exchange=exchange)


def _adam_math(w, g, m, v):
    m = ADAM_B1 * m + (1.0 - ADAM_B1) * g
    v = ADAM_B2 * v + (1.0 - ADAM_B2) * (g * g)
    m_hat = m / (1.0 - ADAM_B1 ** ADAM_STEP)
    v_hat = v / (1.0 - ADAM_B2 ** ADAM_STEP)
    return -ADAM_LR * (m_hat / (jnp.sqrt(v_hat) + ADAM_EPS) + ADAM_WD * w), m, v


def _adam(g_slabs, w, m, v, name, g_fn=None, g_extra=()):
    R, C = w.shape
    tr = 256 if R % 256 == 0 else R
    ns = 0 if g_slabs is None else g_slabs.shape[0]
    slab_rows = tr if g_slabs is None or g_slabs.shape[1] == R else g_slabs.shape[1]
    assert slab_rows == tr or tr == R
    ne = len(g_extra)

    def body(*refs):
        e_refs = refs[:ne]
        refs = refs[ne:]
        if ns:
            gs_ref, refs = refs[0], refs[1:]
        w_ref, m_ref, v_ref, g_ref, d_ref, nm_ref, nv_ref = refs
        if g_fn is not None:
            g = g_fn(*e_refs)
        else:
            g = gs_ref[0].astype(F32)
            for s in range(1, ns):
                g = g + gs_ref[s].astype(F32)
            g = g[:tr]
        d, nm, nv = _adam_math(w_ref[...], g, m_ref[...], v_ref[...])
        g_ref[...] = g
        d_ref[...] = d
        nm_ref[...] = nm
        nv_ref[...] = nv

    blk = pl.BlockSpec((tr, C), lambda i: (i, 0))
    in_specs = [pl.BlockSpec((tr, e.shape[1]), lambda i: (i, 0)) if e.shape[0] == R else _full(e.shape) for e in g_extra]
    args = list(g_extra)
    if ns:
        in_specs.append(pl.BlockSpec((ns, slab_rows, C), lambda i: (0, i, 0)))
        args.append(g_slabs)
    return pl.pallas_call(
        body, name=name, grid=(R // tr,), in_specs=in_specs + [blk] * 3, out_specs=[blk] * 4,
        out_shape=[jax.ShapeDtypeStruct((R, C), F32)] * 4, compiler_params=_params(),
    )(*args, w, m, v)


def _adam_small(small_all, params):
    n = len(params)

    def body(*refs):
        s_ref, refs = refs[0], refs[1:]
        wmv, loss_ref, outs = refs[:3 * n], refs[3 * n], refs[3 * n + 1:]
        tot = s_ref[0]
        for i in range(1, N_DEV):
            tot = tot + s_ref[i]
        loss_ref[...] = tot[:, SMALL_W - 128:]
        for j, (w, _, _, off) in enumerate(params):
            w_ref, m_ref, v_ref = wmv[3 * j:3 * j + 3]
            g_ref, d_ref, nm_ref, nv_ref = outs[4 * j:4 * j + 4]
            if w.shape[0] == 2:
                lb = _lower_bound(w_ref)
                g0 = tot[:, off:off + w.shape[1]] * lb * (1.0 - lb)
                rows = [(slice(0, 1), g0), (slice(1, 2), -g0)]
            else:
                rows = [(slice(0, 1), tot[:, off:off + w.shape[1]])]
            for rs, g in rows:
                d, nm, nv = _adam_math(w_ref[rs, :], g, m_ref[rs, :], v_ref[rs, :])
                g_ref[rs, :], d_ref[rs, :], nm_ref[rs, :], nv_ref[rs, :] = g, d, nm, nv

    out_shape = [jax.ShapeDtypeStruct((1, 128), F32)]
    for w, _, _, _ in params:
        out_shape += [jax.ShapeDtypeStruct(w.shape, F32)] * 4
    res = pl.pallas_call(body, name="adam_small", out_shape=out_shape, compiler_params=_params())(
        small_all, *[a for w, m, v, _ in params for a in (w, m, v)])
    return res[0], [tuple(res[1 + 4 * j:5 + 4 * j]) for j in range(n)]


def _cols_from_slabs(g):
    s, r, c = g.shape
    return jnp.transpose(g, (1, 0, 2)).reshape(r, s * c)


def _slabs_from_cols(w):
    r, c = w.shape
    return jnp.transpose(w.reshape(r, N_DEV, c // N_DEV), (1, 0, 2))


def _rot_half_rows(wt):
    return jnp.concatenate([-wt[32:], wt[:32]], axis=0)


def _unrot_half_rows(dwt_rot):
    return jnp.concatenate([dwt_rot[32:], -dwt_rot[:32]], axis=0)


def _ext_in_t(g):
    n, rows, k_in = g.shape
    keep = n * rows - ROPE_DIM

    def body(g_ref, o_ref, stage_ref):
        stage_ref[keep:, :] = jnp.zeros((o_ref.shape[0] - keep, k_in), F32)
        for i in range(n - 1):
            stage_ref[rows * i:rows * (i + 1), :] = g_ref[i].astype(F32)
        last = g_ref[n - 1].astype(F32)
        stage_ref[rows * (n - 1):keep, :] = last[:rows - ROPE_DIM]
        wk = last[rows - ROPE_DIM:]
        stage_ref[keep + 128:keep + 192, :] = wk
        stage_ref[keep + 384:keep + 416, :] = -wk[32:]
        stage_ref[keep + 416:keep + 448, :] = wk[:32]
        o_ref[...] = stage_ref[...].astype(BF16)

    return pl.pallas_call(body, name="ext_w_in", out_shape=jax.ShapeDtypeStruct((keep + 512, k_in), BF16),
                          scratch_shapes=[pltpu.VMEM((keep + 512, k_in), F32)], compiler_params=_params())(g)


def _ext_q_t(wt):
    r = wt.shape[1]
    z64, z128 = jnp.zeros((64, r), BF16), jnp.zeros((128, r), BF16)
    per = HEAD_DIM + ROPE_DIM
    main = [jnp.concatenate([wt[per * h:per * (h + 1)], z64], axis=0) for h in range(HEADS)]
    rot = [jnp.concatenate([z128, _rot_half_rows(wt[per * h + HEAD_DIM:per * (h + 1)]), z64], axis=0)
           for h in range(HEADS)]
    return jnp.concatenate(main + rot, axis=0)


def _ext_kv(w_kv_up):
    r = w_kv_up.shape[0]
    z128 = jnp.zeros((r, 128), BF16)
    wkv = w_kv_up.reshape(r, HEADS, 2 * HEAD_DIM)
    kpad = [jnp.concatenate([wkv[:, h, :HEAD_DIM], z128], axis=1) for h in range(HEADS)]
    vals = [wkv[:, h, HEAD_DIM:] for h in range(HEADS)]
    return jnp.concatenate(kpad + vals, axis=1)


def _w_in_grad_slabs(dwt_h, dwt_m):
    d = dwt_h.shape[1]
    rows = (dwt_h.shape[0] + 512 + ROPE_DIM) // N_DEV
    padded = rows + (-rows) % 16

    def body(h_ref, m_ref, o_ref, stage_ref):
        stage_ref[0:2048, :] = h_ref[...]
        stage_ref[2048:2560, :] = m_ref[0:512, :]
        rot = m_ref[768 + 128:768 + 192, :]
        stage_ref[2560:2592, :] = m_ref[640:672, :] + rot[32:]
        stage_ref[2592:2624, :] = m_ref[672:704, :] - rot[:32]
        zero = jnp.zeros((padded - rows, d), F32)
        for i in range(N_DEV):
            o_ref[i] = jnp.concatenate([stage_ref[rows * i:rows * (i + 1), :], zero], axis=0).astype(BF16)

    return pl.pallas_call(body, name="w_in_grad_slabs", out_shape=jax.ShapeDtypeStruct((N_DEV, padded, d), BF16),
                          scratch_shapes=[pltpu.VMEM((N_DEV * rows, d), F32)], compiler_params=_params())(dwt_h, dwt_m)


def _grad_q_from_ext_t(dwq_ext_t):
    rows = []
    for h in range(HEADS):
        main, rot = dwq_ext_t[256 * h:256 * h + 256], dwq_ext_t[1024 + 256 * h:1280 + 256 * h]
        rows += [main[:128], main[128:192] + _unrot_half_rows(rot[128:192])]
    return jnp.concatenate(rows, axis=0)


def _grad_kv_from_ext(dwkv_ext):
    kvcols = []
    for h in range(HEADS):
        kvcols += [dwkv_ext[:, 256 * h:256 * h + 128], dwkv_ext[:, 1024 + 128 * h:1152 + 128 * h]]
    return jnp.concatenate(kvcols, axis=1)


SMALL_W = 6144 + 512 + 512 + 256 + 256 + 4 * 1024 + 128


def kernel(x, c, positions, w_ada, b_ada, w_in, hg_lower_bounds, hg_norm_w, mla_q_norm_w, w_q_up, mla_kv_norm_w, w_kv_up, w_out, ln1_g, ln1_b, w_mlp_in, w_mlp_out, ln2_g, ln2_b, loss_target, m_w_ada, m_b_ada, m_w_in, m_hg_lower_bounds, m_hg_norm_w, m_mla_q_norm_w, m_w_q_up, m_mla_kv_norm_w, m_w_kv_up, m_w_out, m_ln1_g, m_ln1_b, m_w_mlp_in, m_w_mlp_out, m_ln2_g, m_ln2_b, v_w_ada, v_b_ada, v_w_in, v_hg_lower_bounds, v_hg_norm_w, v_mla_q_norm_w, v_w_q_up, v_mla_kv_norm_w, v_w_kv_up, v_w_out, v_ln1_g, v_ln1_b, v_w_mlp_in, v_w_mlp_out, v_ln2_g, v_ln2_b):
    T = x.shape[1]
    me = 4 * lax.axis_index("x") + 2 * lax.axis_index("y") + lax.axis_index("c")
    xs, tgt = x[0], loss_target[0]
    transposed = ("w_in", "w_q_up")
    as_used = lambda n, a: a[0].T if n in transposed else a[0]
    big = {n: as_used(n, a) for n, a in dict(w_in=w_in, w_q_up=w_q_up, w_kv_up=w_kv_up, w_out=w_out,
                                              w_mlp_in=w_mlp_in, w_mlp_out=w_mlp_out).items()}
    names = list(big)

    bf = {n: big[n].astype(BF16) for n in ("w_in", "w_q_up", "w_kv_up", "w_out")}
    g_in, g_c = _gather_two_level([bf["w_in"], c], name="gather_w_in")
    c_all = g_c.reshape(N_DEV, D_MODEL)

    ada_cols = w_ada.shape[2]
    mod_part, cond = _mod_part(c_all, w_ada[0], lax.dynamic_slice(b_ada, (0, me * ada_cols), (1, ada_cols)))
    (mod_all,) = _exchange([mod_part], scatter=False, name="gather_mod")
    mod_row = lax.dynamic_slice(mod_all, (0, me, 0), (N_DEV, 1, ada_cols)).reshape(1, N_DEV * ada_cols)
    sh_a, sc_a, g_a, sh_m, sc_m, g_m = [mod_row[:, D_MODEL * i:D_MODEL * (i + 1)] for i in range(6)]

    w_in_ext = _ext_in_t(g_in)
    half = D_MODEL // 2
    z, (w1_top,) = _matmul(xs, w_in_ext, "NT", "in_proj", a_fn=_modulate, extras=(sc_a, sh_a), tn=3072,
                           exchange=_StagedGather(big["w_mlp_in"], rows=(0, half)))
    (o_raw, o_gated, s_prev), (g_q, g_kv, g_out) = _hgrn_fwd(
        z, hg_lower_bounds, hg_norm_w, exchange=_Exchange([bf["w_q_up"], bf["w_kv_up"], bf["w_out"]], False))
    wq_ext = _ext_q_t(g_q.reshape(N_DEV * g_q.shape[1], g_q.shape[2]))
    wkv_ext = _ext_kv(_cols_from_slabs(g_kv))
    w_out_full = g_out.reshape(D_MODEL, D_MODEL)
    inv_freq = 1.0 / (ROPE_THETA ** (jnp.arange(0, ROPE_DIM, 2, dtype=F32) / ROPE_DIM))
    zeros = lambda n: jnp.zeros((n,), F32)
    invf = jnp.concatenate([zeros(128), inv_freq, inv_freq, zeros(64)]).reshape(1, QK_PAD)
    m_rot = jnp.concatenate([zeros(128), jnp.ones((64,), F32), zeros(64)]).reshape(1, QK_PAD)
    q, k, v, c1, s1, cqn, ckvn = _mla_pre(z, positions.reshape(T, 1), invf, m_rot, wq_ext, wkv_ext,
                                          mla_q_norm_w, mla_kv_norm_w)[0]
    (o_mla, lse), (w1_bot, w2) = _attn_fwd(
        q, k, v, exchange=[_StagedGather(big["w_mlp_in"], 0.75, rows=(half, half)),
                           _StagedGather(big["w_mlp_out"], 0.75)])
    w1 = (w1_top, w1_bot)
    mix, xhat1, rstd1, u2 = _mix_ln1(o_gated, o_mla, w_out_full, xs, g_a, ln1_g, ln1_b, sc_m, sh_m)[0]
    r, dr2, dh, small_mlp_fwd = _mlp_fwd(u2, w1, w2, xhat1, ln1_g, ln1_b, g_m, ln2_g, ln2_b, tgt)

    dhpre, dr1, dmix, small_mlp_bwd = _mlp_bwd(dh, w1, w2, r, dr2, xhat1, rstd1, mix, ln1_g, ln1_b, sc_m, g_a)
    received = {}
    dw2 = _matmul(r, dh, "TN", "wgrad_mlp_out", out_dtype=BF16, a_fn=_square, tm=1024, tk=2048)
    dw1 = _matmul(u2, dhpre, "TN", "wgrad_mlp_in", out_dtype=BF16, tm=1024, tk=2048, out_slabs=N_DEV)
    dmixcat = _matmul(dmix, w_out_full, "NT", "dgrad_out", tm=1024)
    dw_out = jnp.concatenate([_matmul(o_gated, dmix, "TN", "wgrad_out_hg", out_dtype=BF16, tk=2048),
                              _matmul(o_mla, dmix, "TN", "wgrad_out_mla", out_dtype=BF16, tk=2048)], axis=0)
    (dz_h, small_hgrn), (received["w_mlp_in"],) = _hgrn_bwd(
        dmixcat, z, o_raw, s_prev, hg_lower_bounds, hg_norm_w, exchange=_StagedScatter(dw1, 0.1))
    (dq, dk, dv), (received["w_mlp_out"], received["w_out"]) = _attn_bwd(
        q, k, v, dmixcat, o_mla, lse,
        exchange=_Exchange([dw2.reshape(N_DEV, dw2.shape[0] // N_DEV, D_MODEL),
                            dw_out.reshape(N_DEV, D_MODEL // N_DEV, D_MODEL)], True))
    dz_m, dq_ext, dkv_ext, small_mla = _mla_bwd(dq, dk, dv, z, c1, s1, wq_ext, wkv_ext, mla_q_norm_w, mla_kv_norm_w)
    dwq_t = _grad_q_from_ext_t(_matmul(dq_ext, cqn, "TN", "wgrad_q_up", tm=1024, tk=2048))
    dwkv = _grad_kv_from_ext(_matmul(ckvn, dkv_ext, "TN", "wgrad_kv_up", tn=1536, tk=2048))
    qkv_slabs = [dwq_t.reshape((N_DEV, dwq_t.shape[0] // N_DEV, dwq_t.shape[1])).astype(BF16),
                 _slabs_from_cols(dwkv).astype(BF16)]
    dwt_h = _matmul(dz_h, xs, "TN", "wgrad_in_h", b_fn=_modulate, extras=(sc_a, sh_a), tm=1024, tk=2048)
    dwt_m = _matmul(dz_m, xs, "TN", "wgrad_in_m", b_fn=_modulate, extras=(sc_a, sh_a), tm=1024, tk=2048)
    in_slabs = _w_in_grad_slabs(dwt_h, dwt_m)
    (grad_x, small_in), (received["w_q_up"], received["w_kv_up"], received["w_in"]) = _input_bwd(
        dz_h, dz_m, w_in_ext, xs, dr1, sc_a,
        exchange=[_Exchange(qkv_slabs, True), _StagedScatter(in_slabs)])

    small = jnp.concatenate([small_in, small_mlp_bwd[:, :3 * D_MODEL], small_mlp_fwd[:, :D_MODEL], small_hgrn,
                             small_mla, small_mlp_bwd[:, 3 * D_MODEL:], small_mlp_fwd[:, D_MODEL:]], axis=1)
    assert small.shape == (1, SMALL_W)
    (small_all,) = _exchange([small], scatter=False, name="gather_small")

    moments = dict(w_in=(m_w_in, v_w_in), w_q_up=(m_w_q_up, v_w_q_up), w_kv_up=(m_w_kv_up, v_w_kv_up),
                   w_out=(m_w_out, v_w_out), w_mlp_in=(m_w_mlp_in, v_w_mlp_in), w_mlp_out=(m_w_mlp_out, v_w_mlp_out))
    res = {}
    for n in names:
        res[n] = _adam(received[n], big[n], as_used(n, moments[n][0]), as_used(n, moments[n][1]), name="adam_" + n)
    dmod_cols = lax.dynamic_slice(small_all.reshape(N_DEV, SMALL_W), (0, me * ada_cols), (N_DEV, ada_cols))
    cond_t = cond.T

    def ada_grad(ct_ref, dm_ref):
        g = ct_ref[:, 0:1] * dm_ref[0:1, :]
        for b in range(1, N_DEV):
            g = g + ct_ref[:, b:b + 1] * dm_ref[b:b + 1, :]
        return g

    res["w_ada"] = _adam(None, w_ada[0], m_w_ada[0], v_w_ada[0], name="adam_w_ada", g_fn=ada_grad,
                         g_extra=(cond_t, dmod_cols))

    small_params = [("b_ada", b_ada, m_b_ada, v_b_ada, 0),
                    ("hg_lower_bounds", hg_lower_bounds, m_hg_lower_bounds, v_hg_lower_bounds, 6144),
                    ("hg_norm_w", hg_norm_w, m_hg_norm_w, v_hg_norm_w, 6656),
                    ("mla_q_norm_w", mla_q_norm_w, m_mla_q_norm_w, v_mla_q_norm_w, 7168),
                    ("mla_kv_norm_w", mla_kv_norm_w, m_mla_kv_norm_w, v_mla_kv_norm_w, 7424),
                    ("ln1_g", ln1_g, m_ln1_g, v_ln1_g, 7680), ("ln1_b", ln1_b, m_ln1_b, v_ln1_b, 8704),
                    ("ln2_g", ln2_g, m_ln2_g, v_ln2_g, 9728), ("ln2_b", ln2_b, m_ln2_b, v_ln2_b, 10752)]
    loss_row, small_res = _adam_small(small_all, [p[1:] for p in small_params])
    for p, r4 in zip(small_params, small_res):
        res[p[0]] = r4
    loss = loss_row[0, 0]

    order = ["w_ada", "b_ada", "w_in", "hg_lower_bounds", "hg_norm_w", "mla_q_norm_w", "w_q_up", "mla_kv_norm_w",
             "w_kv_up", "w_out", "ln1_g", "ln1_b", "w_mlp_in", "w_mlp_out", "ln2_g", "ln2_b"]
    def as_given(n, a):
        if n in transposed:
            a = a.T
        return a[None] if n in big or n == "w_ada" else a

    shaped = {n: tuple(as_given(n, a) for a in res[n]) for n in order}
    outs = [loss, grad_x.reshape(1, T, D_MODEL)]
    for i in range(4):
        outs += [shaped[n][i] for n in order]
    return tuple(outs)
```

```python
import jax
import jax.numpy as jnp
import numpy as np
from jax import lax
from jax.experimental import pallas as pl
from jax.experimental.pallas import tpu as pltpu

F32, BF16 = jnp.float32, jnp.bfloat16
N_DEV = 8
D_MODEL = 1024
HEADS = 4
HEAD_DIM = 128
ROPE_DIM = 64
QK_PAD = 256
CHUNK = 64
ROPE_THETA = 10000.0
RMS_EPS = 1e-6
LN_EPS = 1e-5
ALPHA = 2.0 ** 0.25
ATT_SCALE = (HEAD_DIM + ROPE_DIM) ** -0.5
LN2 = float(np.log(2.0))
Q_PRESCALE = ATT_SCALE / LN2
ADAM_LR, ADAM_B1, ADAM_B2, ADAM_EPS, ADAM_WD, ADAM_STEP = 0.001, 0.9, 0.999, 1e-08, 0.01, 10
NEG_BIG = -1e30

ROW_TILE = 512
ATT_TILE = 512
HGRN_GROUP = 8
MLP_SLABS = 4
VMEM_LIMIT = 56 * 2 ** 20

NN = (((1,), (0,)), ((), ()))
NT = (((1,), (1,)), ((), ()))
TN = (((0,), (0,)), ((), ()))


def _dot(a, b, dims=NN):
    return lax.dot_general(a, b, dims, preferred_element_type=F32)


def _bdot(a, b, dims=NN):
    return lax.dot_general(a.astype(BF16), b.astype(BF16), dims, preferred_element_type=F32)


def _hdot(a, b, dims=NN):
    return lax.dot_general(a, b, dims, precision=lax.Precision.HIGHEST, preferred_element_type=F32)


def _params():
    return pltpu.CompilerParams(vmem_limit_bytes=VMEM_LIMIT)


def _sigmoid(x):
    return 1.0 / (1.0 + jnp.exp(-x))


def _rowsum(x):
    return jnp.sum(x, axis=0, keepdims=True)


def _lanemean(x):
    return jnp.mean(x, axis=-1, keepdims=True)


def _full(shape):
    nd = len(shape)
    return pl.BlockSpec(shape, lambda *_: (0,) * nd)


class _Exchange:
    def __init__(self, arrs, scatter):
        self.arrs, self.scatter, self.n, self.aliases, self.middle_at = list(arrs), scatter, len(arrs), [], 0.5
        self.out_shape = [jax.ShapeDtypeStruct((N_DEV,) + (a.shape[1:] if scatter else a.shape), a.dtype)
                          for a in self.arrs]
        n = self.n
        self.scratch = [pltpu.SemaphoreType.DMA((n, N_DEV - 1)), pltpu.SemaphoreType.DMA((n, N_DEV - 1)),
                        pltpu.SemaphoreType.DMA((n,))]

    def _copies(self, ins, outs, sems):
        send_sems, recv_sems, loc_sems = sems
        x, y, c = lax.axis_index("x"), lax.axis_index("y"), lax.axis_index("c")
        me = 4 * x + 2 * y + c
        copies = []
        for k in range(self.n):
            src_of = (lambda i, k=k: ins[k].at[i]) if self.scatter else (lambda i, k=k: ins[k])
            copies.append((pltpu.make_async_copy(src_of(me), outs[k].at[me], loc_sems.at[k]), None))
            for p in range(1, N_DEV):
                px = (1 - x) if p & 4 else x
                py = (1 - y) if p & 2 else y
                pc = (1 - c) if p & 1 else c
                peer = 4 * px + 2 * py + pc
                both = dict(send_sem=send_sems.at[k, p - 1], recv_sem=recv_sems.at[k, p - 1],
                            device_id=(px, py, pc), device_id_type=pl.DeviceIdType.MESH)
                send = pltpu.make_async_remote_copy(src_ref=src_of(peer), dst_ref=outs[k].at[me], **both)
                recv = pltpu.make_async_remote_copy(src_ref=src_of(peer), dst_ref=outs[k].at[peer], **both)
                copies.append((send, recv))
        return copies

    def start(self, ins, outs, sems):
        for first, _ in self._copies(ins, outs, sems):
            first.start()

    def middle(self, ins, outs, sems):
        pass

    def finish(self, ins, outs, sems):
        pass

    def wait(self, ins, outs, sems):
        for first, recv in self._copies(ins, outs, sems):
            if recv is None:
                first.wait()
            else:
                recv.wait_recv()
                first.wait_send()


class _StagedGather:
    def __init__(self, arr, middle_at=0.8, rows=None):
        self.r0, n = rows if rows else (0, arr.shape[0])
        block = (n,) + arr.shape[1:]
        self.arrs, self.aliases, self.middle_at = [arr], [], middle_at
        self.out_shape = [jax.ShapeDtypeStruct((N_DEV,) + block, BF16)]
        self.scratch = [pltpu.VMEM((N_DEV,) + block, BF16), pltpu.SemaphoreType.DMA((7,)),
                        pltpu.SemaphoreType.DMA((7,)), pltpu.SemaphoreType.DMA((2,)), pltpu.VMEM(block, arr.dtype)]

    def _parts(self, scr):
        stage, send_sems, recv_sems, loc_sems = scr[:4]
        x, y, c = lax.axis_index("x"), lax.axis_index("y"), lax.axis_index("c")
        me, sibling = (x, y, c), (x, y, 1 - c)
        chips = [(1 - x, y), (x, 1 - y), (1 - x, 1 - y)]

        def copy(j, block, to):
            px, py, pc = block
            slot = stage.at[4 * px + 2 * py + pc]
            return pltpu.make_async_remote_copy(src_ref=slot, dst_ref=slot, send_sem=send_sems.at[j],
                                                recv_sem=recv_sems.at[j], device_id=to,
                                                device_id_type=pl.DeviceIdType.MESH)

        return stage, loc_sems, me, sibling, chips, c, copy

    def start(self, ins, outs, scr):
        stage, loc_sems, me, sibling, chips, c, copy = self._parts(scr)
        x, y, _ = me
        raw = scr[4]
        own = pltpu.make_async_copy(ins[0].at[pl.ds(self.r0, raw.shape[0])], raw, loc_sems.at[0])
        own.start()
        own.wait()
        stage[4 * x + 2 * y + c] = raw[...].astype(BF16)
        copy(0, me, sibling).start()
        for j, chip in enumerate(chips):
            copy(1 + j, me, (*chip, c)).start()

    def middle(self, ins, outs, scr):
        stage, loc_sems, me, sibling, chips, c, copy = self._parts(scr)
        for j, chip in enumerate(chips):
            copy(1 + j, (*chip, c), me).wait_recv()
            copy(4 + j, (*chip, c), sibling).start()

    def wait(self, ins, outs, scr):
        stage, loc_sems, me, sibling, chips, c, copy = self._parts(scr)
        copy(0, sibling, me).wait_recv()
        for j, chip in enumerate(chips):
            copy(4 + j, (*chip, 1 - c), me).wait_recv()
        copy(0, me, sibling).wait_send()
        for j, chip in enumerate(chips):
            copy(1 + j, me, (*chip, c)).wait_send()
            copy(4 + j, (*chip, c), sibling).wait_send()
        pltpu.make_async_copy(stage, outs[0], loc_sems.at[1]).start()

    def finish(self, ins, outs, scr):
        pltpu.make_async_copy(scr[0], outs[0], scr[3].at[1]).wait()


class _StagedScatter:
    def __init__(self, slabs, middle_at=0.2):
        _, r, c = slabs.shape
        self.arrs, self.aliases, self.middle_at = [slabs], [], middle_at
        self.out_shape = [jax.ShapeDtypeStruct((4, r, c), slabs.dtype)]
        self.scratch = [pltpu.VMEM((N_DEV, r, c), slabs.dtype), pltpu.VMEM((4, r, c), slabs.dtype),
                        pltpu.VMEM((3, r, c), slabs.dtype), pltpu.SemaphoreType.DMA((4,)), pltpu.SemaphoreType.DMA((4,)),
                        pltpu.SemaphoreType.DMA((3,)), pltpu.SemaphoreType.DMA((3,)), pltpu.SemaphoreType.DMA((4,))]

    def _parts(self, scr):
        stage, from_sib, from_chips, sib_send, sib_recv, ici_send, ici_recv, loc_sems = scr
        x, y, c = lax.axis_index("x"), lax.axis_index("y"), lax.axis_index("c")
        chips = [(1 - x, y), (x, 1 - y), (1 - x, 1 - y)]

        def to_sibling(j):
            return pltpu.make_async_remote_copy(src_ref=stage.at[2 * j + 1 - c], dst_ref=from_sib.at[j],
                                                send_sem=sib_send.at[j], recv_sem=sib_recv.at[j],
                                                device_id=(x, y, 1 - c), device_id_type=pl.DeviceIdType.MESH)

        def to_chip(k):
            px, py = chips[k]
            return pltpu.make_async_remote_copy(src_ref=stage.at[4 * px + 2 * py + c], dst_ref=from_chips.at[k],
                                                send_sem=ici_send.at[k], recv_sem=ici_recv.at[k],
                                                device_id=(px, py, c), device_id_type=pl.DeviceIdType.MESH)

        return stage, from_sib, from_chips, loc_sems, (x, y, c), chips, to_sibling, to_chip

    def start(self, ins, outs, scr):
        stage, _, _, loc_sems, _, _, to_sibling, _ = self._parts(scr)
        load = pltpu.make_async_copy(ins[0], stage, loc_sems.at[0])
        load.start()
        load.wait()
        for j in range(4):
            to_sibling(j).start()

    def middle(self, ins, outs, scr):
        stage, from_sib, _, _, (x, y, c), _, to_sibling, to_chip = self._parts(scr)
        for j in range(4):
            to_sibling(j).wait_recv()
            mine = stage.at[2 * j + c]
            mine[...] = (mine[...].astype(F32) + from_sib[j].astype(F32)).astype(mine.dtype)
        for k in range(3):
            to_chip(k).start()

    def wait(self, ins, outs, scr):
        stage, _, from_chips, loc_sems, (x, y, c), chips, to_sibling, to_chip = self._parts(scr)
        writes = [pltpu.make_async_copy(stage.at[4 * x + 2 * y + c], outs[0].at[2 * x + y], loc_sems.at[0])]
        for k, (px, py) in enumerate(chips):
            to_chip(k).wait_recv()
            writes.append(pltpu.make_async_copy(from_chips.at[k], outs[0].at[2 * px + py], loc_sems.at[1 + k]))
        for w in writes:
            w.start()
        for j in range(4):
            to_sibling(j).wait_send()
        for k in range(3):
            to_chip(k).wait_send()
        for w in writes:
            w.wait()

    def finish(self, ins, outs, scr):
        pass


def _call(body, name, args, out_shape, grid=(), in_specs=(), out_specs=(), scratch_shapes=(), exchange=None):
    if exchange is None:
        return pl.pallas_call(body, name=name, grid=grid, in_specs=list(in_specs), out_specs=list(out_specs),
                              out_shape=list(out_shape), scratch_shapes=list(scratch_shapes),
                              compiler_params=_params())(*args), None
    exs = list(exchange) if isinstance(exchange, (list, tuple)) else [exchange]
    ni, no, ns = len(args), len(out_shape), len(scratch_shapes)
    nxi, nxo = sum(len(e.arrs) for e in exs), sum(len(e.out_shape) for e in exs)
    steps = int(np.prod(grid))
    mid_step = lambda e: min(max(int(steps * e.middle_at), 1), steps - 1)
    aliases, iat, oat = {}, ni, no
    for e in exs:
        for src, dst in e.aliases:
            aliases[iat + src] = oat + dst
        iat, oat = iat + len(e.arrs), oat + len(e.out_shape)

    def wrapped(*refs):
        a, xi = refs[:ni], refs[ni:ni + nxi]
        o, xo = refs[ni + nxi:ni + nxi + no], refs[ni + nxi + no:ni + nxi + no + nxo]
        s, xs = refs[ni + nxi + no + nxo:ni + nxi + no + nxo + ns], refs[ni + nxi + no + nxo + ns:]
        parts, iat, oat, sat = [], 0, 0, 0
        for e in exs:
            parts.append((e, xi[iat:iat + len(e.arrs)], xo[oat:oat + len(e.out_shape)], xs[sat:sat + len(e.scratch)]))
            iat, oat, sat = iat + len(e.arrs), oat + len(e.out_shape), sat + len(e.scratch)
        step = 0
        for d, g in enumerate(grid):
            step = step * g + pl.program_id(d)

        @pl.when(step == 0)
        def _():
            for e, ins, outs, sems in parts:
                e.start(ins, outs, sems)

        for at_step in sorted({mid_step(e) for e in exs}):
            @pl.when(step == at_step)
            def _():
                for e, ins, outs, sems in parts:
                    if mid_step(e) == at_step:
                        e.middle(ins, outs, sems)

        body(*a, *o, *s)

        @pl.when(step == steps - 1)
        def _():
            for e, ins, outs, sems in parts:
                e.wait(ins, outs, sems)
            for e, ins, outs, sems in parts:
                e.finish(ins, outs, sems)

    hbm = pl.BlockSpec(memory_space=pltpu.HBM)
    res = pl.pallas_call(
        wrapped, name=name, grid=grid, in_specs=list(in_specs) + [hbm] * nxi, out_specs=list(out_specs) + [hbm] * nxo,
        out_shape=list(out_shape) + [o_ for e in exs for o_ in e.out_shape],
        scratch_shapes=list(scratch_shapes) + [s_ for e in exs for s_ in e.scratch],
        input_output_aliases=aliases, compiler_params=_params())(*args, *[a_ for e in exs for a_ in e.arrs])
    return res[:no], res[no:]


def _gather_two_level(arrs, name):
    n = len(arrs)
    out_shape = [jax.ShapeDtypeStruct((N_DEV,) + a.shape, a.dtype) for a in arrs]

    def body(*refs):
        ins, outs = refs[:n], refs[n:2 * n]
        send_sems, recv_sems, loc_sems = refs[2 * n:]
        x, y, c = lax.axis_index("x"), lax.axis_index("y"), lax.axis_index("c")
        me, sibling = (x, y, c), (x, y, 1 - c)
        chips = [(1 - x, y), (x, 1 - y), (1 - x, 1 - y)]

        def copy(k, j, block, to, src=None):
            px, py, pc = block
            dst = outs[k].at[4 * px + 2 * py + pc]
            return pltpu.make_async_remote_copy(src_ref=dst if src is None else src, dst_ref=dst,
                                                send_sem=send_sems.at[k, j], recv_sem=recv_sems.at[k, j],
                                                device_id=to, device_id_type=pl.DeviceIdType.MESH)

        mine = [pltpu.make_async_copy(ins[k], outs[k].at[4 * x + 2 * y + c], loc_sems.at[k]) for k in range(n)]
        first = []
        for k in range(n):
            mine[k].start()
            first.append(copy(k, 0, me, sibling, src=ins[k]))
            first += [copy(k, 1 + j, me, (*chip, c), src=ins[k]) for j, chip in enumerate(chips)]
        for cp in first:
            cp.start()
        passed = []
        for j, chip in enumerate(chips):
            for k in range(n):
                copy(k, 1 + j, (*chip, c), me).wait_recv()
                passed.append(copy(k, 4 + j, (*chip, c), sibling))
                passed[-1].start()
        for k in range(n):
            copy(k, 0, sibling, me).wait_recv()
            for j, chip in enumerate(chips):
                copy(k, 4 + j, (*chip, 1 - c), me).wait_recv()
        for cp in first + passed:
            cp.wait_send()
        for cp in mine:
            cp.wait()

    vmem = pl.BlockSpec(memory_space=pltpu.VMEM)
    return pl.pallas_call(body, name=name, out_shape=out_shape, in_specs=[vmem] * n, out_specs=[vmem] * n,
                          scratch_shapes=[pltpu.SemaphoreType.DMA((n, 7)), pltpu.SemaphoreType.DMA((n, 7)),
                                          pltpu.SemaphoreType.DMA((n,))], compiler_params=_params())(*arrs)


def _exchange(arrs, scatter, name):
    ex = _Exchange(arrs, scatter)

    def body(*refs):
        ins, outs, sems = refs[:ex.n], refs[ex.n:2 * ex.n], refs[2 * ex.n:]
        ex.start(ins, outs, sems)
        ex.wait(ins, outs, sems)

    hbm = pl.BlockSpec(memory_space=pltpu.HBM)
    return pl.pallas_call(body, name=name, out_shape=ex.out_shape, in_specs=[hbm] * ex.n, out_specs=[hbm] * ex.n,
                          scratch_shapes=ex.scratch)(*ex.arrs)


def _matmul(a, b, mode, name, out_dtype=F32, tm=512, tn=1024, tk=1024, a_fn=None, b_fn=None, extras=(),
            out_slabs=None, exchange=None):
    assert not (a_fn and b_fn) and not (b_fn and mode == "NT")
    if mode == "NN":
        (M, K), N = a.shape, b.shape[1]
    elif mode == "NT":
        (M, K), N = a.shape, b.shape[0]
    else:
        (K, M), N = a.shape, b.shape[1]
    slab_w = N // out_slabs if out_slabs else None
    if out_slabs:
        tn = max(slab_w, min(tn, N) // slab_w * slab_w)
    tm, tn, tk = min(tm, M), min(tn, N), min(tk, K)
    assert M % tm == 0 and N % tn == 0 and K % tk == 0, (name, M, N, K)
    nk = K // tk
    dims = {"NN": NN, "NT": NT, "TN": TN}[mode]
    ne = len(extras)

    def body(a_ref, b_ref, *rest):
        e_refs, o_ref, acc_ref = rest[:ne], rest[ne], rest[ne + 1]
        k = pl.program_id(2)

        @pl.when(k == 0)
        def _():
            acc_ref[...] = jnp.zeros_like(acc_ref)

        at, bt = a_ref[...], b_ref[...]
        if a_fn is not None:
            at = a_fn(at.astype(F32), *[e[...] for e in e_refs])
        if b_fn is not None:
            bt = b_fn(bt.astype(F32), *[e[...] for e in e_refs])
        acc_ref[...] += _bdot(at, bt, dims)

        @pl.when(k == nk - 1)
        def _():
            if out_slabs:
                for s in range(tn // slab_w):
                    o_ref[s] = acc_ref[:, s * slab_w:(s + 1) * slab_w].astype(out_dtype)
            else:
                o_ref[...] = acc_ref[...].astype(out_dtype)

    if mode == "TN":
        a_spec = pl.BlockSpec((tk, tm), lambda i, j, k: (k, i))
        e_spec = pl.BlockSpec((1, tm), lambda i, j, k: (0, i))
    else:
        a_spec = pl.BlockSpec((tm, tk), lambda i, j, k: (i, k))
        e_spec = pl.BlockSpec((1, tk), lambda i, j, k: (0, k))
    if mode == "NT":
        b_spec = pl.BlockSpec((tn, tk), lambda i, j, k: (j, k))
    else:
        b_spec = pl.BlockSpec((tk, tn), lambda i, j, k: (k, j))
    if b_fn is not None:
        e_spec = pl.BlockSpec((1, tn), lambda i, j, k: (0, j))
    if out_slabs:
        o_shape = jax.ShapeDtypeStruct((out_slabs, M, slab_w), out_dtype)
        o_spec = pl.BlockSpec((tn // slab_w, tm, slab_w), lambda i, j, k: (j, i, 0))
    else:
        o_shape = jax.ShapeDtypeStruct((M, N), out_dtype)
        o_spec = pl.BlockSpec((tm, tn), lambda i, j, k: (i, j))
    (out,), got = _call(body, name, (a, b, *extras), [o_shape], grid=(M // tm, N // tn, nk),
                        in_specs=[a_spec, b_spec] + [e_spec] * ne, out_specs=[o_spec],
                        scratch_shapes=[pltpu.VMEM((tm, tn), F32)], exchange=exchange)
    return out if exchange is None else (out, got)


def _modulate(x, sc, sh):
    return x * (1.0 + sc) + sh


def _square(x):
    return x * x


def _mod_gather(c_all, w_ada_s, b_s):
    def body(c_ref, w_ref, b_ref, mod_ref, cond_ref, send_sems, recv_sems):
        x, y, c = lax.axis_index("x"), lax.axis_index("y"), lax.axis_index("c")
        me = 4 * x + 2 * y + c
        cv = c_ref[...]
        cond = cv * _sigmoid(cv)
        cond_ref[...] = cond
        mod_ref[me] = _bdot(cond, w_ref[...]) + b_ref[...]
        pairs = []
        for p in range(1, N_DEV):
            px = (1 - x) if p & 4 else x
            py = (1 - y) if p & 2 else y
            pc = (1 - c) if p & 1 else c
            both = dict(send_sem=send_sems.at[p - 1], recv_sem=recv_sems.at[p - 1], device_id=(px, py, pc),
                        device_id_type=pl.DeviceIdType.MESH)
            send = pltpu.make_async_remote_copy(src_ref=mod_ref.at[me], dst_ref=mod_ref.at[me], **both)
            send.start()
            theirs = mod_ref.at[4 * px + 2 * py + pc]
            pairs.append((send, pltpu.make_async_remote_copy(src_ref=theirs, dst_ref=theirs, **both)))
        for send, recv in pairs:
            recv.wait_recv()
            send.wait_send()

    vmem = pl.BlockSpec(memory_space=pltpu.VMEM)
    return pl.pallas_call(
        body, name="mod_gather", in_specs=[vmem] * 3, out_specs=[vmem] * 2,
        out_shape=[jax.ShapeDtypeStruct((N_DEV, N_DEV, w_ada_s.shape[1]), F32), jax.ShapeDtypeStruct(c_all.shape, F32)],
        scratch_shapes=[pltpu.SemaphoreType.DMA((N_DEV - 1,)), pltpu.SemaphoreType.DMA((N_DEV - 1,))],
        compiler_params=_params(),
    )(c_all, w_ada_s, b_s)


def _rms_fwd(x, w):
    rs = lax.rsqrt(_lanemean(x * x) + RMS_EPS)
    return x * rs * w, rs


def _rms_bwd(x, rs, w, dy):
    xhat = x * rs
    dxh = dy * w
    return rs * (dxh - xhat * _lanemean(dxh * xhat)), dy * xhat


def _mla_pre(z, pos_col, invf, m_rot, wq_ext, wkv_ext, qnw, kvnw, exchange=None):
    T = z.shape[0]
    tm = min(ROW_TILE, T)

    def body(z_ref, pos_ref, invf_ref, mrot_ref, wq_ref, wkv_ref, qnw_ref, kvnw_ref,
             q_ref, k_ref, v_ref, c1_ref, s1_ref, cqn_ref, ckvn_ref):
        hi = slice(HEAD_DIM, QK_PAD)
        ang = pos_ref[...].astype(F32) * invf_ref[:, hi]
        c1 = jnp.concatenate([jnp.ones((tm, HEAD_DIM), F32), mrot_ref[:, hi] * jnp.cos(ang)], axis=1)
        s1 = jnp.concatenate([jnp.zeros((tm, HEAD_DIM), F32), mrot_ref[:, hi] * jnp.sin(ang)], axis=1)
        c1_ref[...] = c1
        s1_ref[...] = s1
        cqn, _ = _rms_fwd(z_ref[:, 0:256], qnw_ref[...])
        ckvn, _ = _rms_fwd(z_ref[:, 256:512], kvnw_ref[...])
        cqn_ref[...] = cqn.astype(BF16)
        ckvn_ref[...] = ckvn.astype(BF16)
        qe = _bdot(cqn, wq_ref[...], NT)
        kve = _bdot(ckvn, wkv_ref[...])
        k_rope = z_ref[:, 512:768] * c1 + z_ref[:, 768:1024] * s1
        for h in range(HEADS):
            q_ref[h] = ((qe[:, 256 * h:256 * h + 256] * c1 + qe[:, 1024 + 256 * h:1280 + 256 * h] * s1)
                        * Q_PRESCALE).astype(BF16)
            k_ref[h] = (kve[:, 256 * h:256 * h + 256] + k_rope).astype(BF16)
            v_ref[h] = kve[:, 1024 + 128 * h:1152 + 128 * h].astype(BF16)

    row = lambda i: (i, 0)
    head = lambda i: (0, i, 0)
    return _call(
        body, "mla_pre", (z, pos_col, invf, m_rot, wq_ext, wkv_ext, qnw, kvnw), grid=(T // tm,),
        in_specs=[pl.BlockSpec((tm, 1024), lambda i: (i, 2)), pl.BlockSpec((tm, 1), row),
                  _full((1, 256)), _full((1, 256)), _full(wq_ext.shape), _full(wkv_ext.shape),
                  _full((1, 256)), _full((1, 256))],
        out_specs=[pl.BlockSpec((HEADS, tm, QK_PAD), head), pl.BlockSpec((HEADS, tm, QK_PAD), head),
                   pl.BlockSpec((HEADS, tm, HEAD_DIM), head), pl.BlockSpec((tm, 256), row), pl.BlockSpec((tm, 256), row),
                   pl.BlockSpec((tm, 256), row), pl.BlockSpec((tm, 256), row)],
        out_shape=[jax.ShapeDtypeStruct((HEADS, T, QK_PAD), BF16), jax.ShapeDtypeStruct((HEADS, T, QK_PAD), BF16),
                   jax.ShapeDtypeStruct((HEADS, T, HEAD_DIM), BF16), jax.ShapeDtypeStruct((T, 256), F32),
                   jax.ShapeDtypeStruct((T, 256), F32), jax.ShapeDtypeStruct((T, 256), BF16),
                   jax.ShapeDtypeStruct((T, 256), BF16)], exchange=exchange)


def _mla_bwd(dq, dk, dv, z, c1, s1, wq_ext, wkv_ext, qnw, kvnw):
    T = z.shape[0]
    tm = min(ROW_TILE, T)

    def body(dq_ref, dk_ref, dv_ref, z_ref, c1_ref, s1_ref, wq_ref, wkv_ref, qnw_ref, kvnw_ref,
             dz_ref, dqe_ref, dkve_ref, dnw_ref):
        @pl.when(pl.program_id(0) == 0)
        def _():
            dnw_ref[...] = jnp.zeros_like(dnw_ref)

        c1, s1 = c1_ref[...], s1_ref[...]
        dkpe = jnp.zeros((tm, QK_PAD), F32)
        for h in range(HEADS):
            dqh, dkh = dq_ref[h].astype(F32) * ATT_SCALE, dk_ref[h]
            dqe_ref[:, 256 * h:256 * h + 256] = (dqh * c1).astype(BF16)
            dqe_ref[:, 1024 + 256 * h:1280 + 256 * h] = (dqh * s1).astype(BF16)
            dkve_ref[:, 256 * h:256 * h + 256] = dkh
            dkve_ref[:, 1024 + 128 * h:1152 + 128 * h] = dv_ref[h]
            dkpe = dkpe + dkh.astype(F32)
        dcqn = _dot(dqe_ref[...], wq_ref[...])
        dckvn = _dot(dkve_ref[...], wkv_ref[...], NT)
        cq, ckv = z_ref[:, 0:256], z_ref[:, 256:512]
        _, rsq = _rms_fwd(cq, qnw_ref[...])
        _, rskv = _rms_fwd(ckv, kvnw_ref[...])
        dcq, wq_rows = _rms_bwd(cq, rsq, qnw_ref[...], dcqn)
        dckv, wkv_rows = _rms_bwd(ckv, rskv, kvnw_ref[...], dckvn)
        dnw_ref[:, 0:256] += _rowsum(wq_rows)
        dnw_ref[:, 256:512] += _rowsum(wkv_rows)
        dz_ref[:, 0:256] = dcq.astype(BF16)
        dz_ref[:, 256:512] = dckv.astype(BF16)
        dz_ref[:, 512:768] = (dkpe * c1).astype(BF16)
        dz_ref[:, 768:1024] = (dkpe * s1).astype(BF16)

    row = lambda i: (i, 0)
    head = lambda i: (0, i, 0)
    return pl.pallas_call(
        body, name="mla_bwd", grid=(T // tm,),
        in_specs=[pl.BlockSpec((HEADS, tm, QK_PAD), head), pl.BlockSpec((HEADS, tm, QK_PAD), head),
                  pl.BlockSpec((HEADS, tm, HEAD_DIM), head), pl.BlockSpec((tm, 1024), lambda i: (i, 2)),
                  pl.BlockSpec((tm, 256), row), pl.BlockSpec((tm, 256), row), _full(wq_ext.shape), _full(wkv_ext.shape),
                  _full((1, 256)), _full((1, 256))],
        out_specs=[pl.BlockSpec((tm, 1024), row), pl.BlockSpec((tm, 2048), row), pl.BlockSpec((tm, 1536), row),
                   _full((1, 512))],
        out_shape=[jax.ShapeDtypeStruct((T, 1024), BF16), jax.ShapeDtypeStruct((T, 2048), BF16),
                   jax.ShapeDtypeStruct((T, 1536), BF16), jax.ShapeDtypeStruct((1, 512), F32)],
        compiler_params=_params(),
    )(dq, dk, dv, z, c1, s1, wq_ext, wkv_ext, qnw, kvnw)


_HEAD_LANES = [slice(HEAD_DIM * h, HEAD_DIM * (h + 1)) for h in range(HEADS)]


def _lower_bound(lbraw_ref):
    a0, a1 = lbraw_ref[0:1, :], lbraw_ref[1:2, :]
    mx = jnp.maximum(a0, a1)
    e0, e1 = jnp.exp(a0 - mx), jnp.exp(a1 - mx)
    return e0 / (e0 + e1)


def _tri(lower):
    r = lax.broadcasted_iota(jnp.int32, (CHUNK, CHUNK), 0)
    c = lax.broadcasted_iota(jnp.int32, (CHUNK, CHUNK), 1)
    return (r >= c) if lower else (r <= c)


def _hgrn_gates(q, f, lb, tri_lo):
    sg = _sigmoid(f)
    forget = lb + (1.0 - lb) * sg
    k = 1.0 - forget
    b = _hdot(tri_lo.astype(F32), jnp.log(forget))
    b_ref, b_last = b[CHUNK // 2 - 1:CHUNK // 2, :], b[CHUNK - 1:CHUNK, :]
    e1, e2, e3, e4 = jnp.exp(b - b_ref), jnp.exp(b_ref - b), jnp.exp(b_last - b), jnp.exp(b)
    return dict(sg=sg, forget=forget, k=k, e1=e1, e2=e2, e3=e3, e4=e4, qa=q * e1, ka=k * e2, kl=k * e3, qb=q * e4,
                decay=jnp.exp(b_last))


def _hgrn_fwd(z, lbraw, nw, exchange=None):
    T = z.shape[0]
    G = min(HGRN_GROUP, T // CHUNK)
    rows = G * CHUNK
    n_chunks = T // CHUNK

    def body(q_ref, f_ref, i_ref, g_ref, lbraw_ref, nw_ref, oraw_ref, og_ref, sp_ref, st_ref):
        @pl.when(pl.program_id(0) == 0)
        def _():
            st_ref[...] = jnp.zeros_like(st_ref)

        lb_all = _lower_bound(lbraw_ref)
        tri_lo = _tri(True)

        def chunk(cc, carry):
            rs = pl.ds(pl.multiple_of(cc * CHUNK, CHUNK), CHUNK)
            t = _hgrn_gates(q_ref[rs, :], f_ref[rs, :], lb_all, tri_lo)
            v, gate = i_ref[rs, :], g_ref[rs, :]
            st = [st_ref[h] for h in range(HEADS)]
            a = [jnp.where(tri_lo, _bdot(t["qa"][:, s], t["ka"][:, s], NT), 0.0) for s in _HEAD_LANES]
            kv = [_bdot(v[:, s], t["kl"][:, s], TN) for s in _HEAD_LANES]
            o = [_bdot(a[h], v[:, s]) + _bdot(t["qb"][:, s], st[h], NT) for h, s in enumerate(_HEAD_LANES)]
            for h, s in enumerate(_HEAD_LANES):
                sp_ref[cc, h] = st[h]
                st_ref[h] = st[h] * t["decay"][:, s] + kv[h]
            oraw_ref[rs, :] = jnp.concatenate(o, axis=1)
            on = jnp.concatenate([_rms_fwd(o[h], nw_ref[:, s])[0] for h, s in enumerate(_HEAD_LANES)], axis=1)
            og_ref[rs, :] = (on * (gate * _sigmoid(gate))).astype(BF16)
            return carry

        lax.fori_loop(0, G, chunk, 0, unroll=4)

    col = lambda j: pl.BlockSpec((rows, 512), lambda r, j=j: (r, j))
    return _call(
        body, "hgrn_fwd", (z, z, z, z, lbraw, nw), grid=(T // rows,),
        in_specs=[col(0), col(1), col(2), col(3), _full((2, 512)), _full((1, 512))],
        out_specs=[col(0), col(0), pl.BlockSpec((G, HEADS, HEAD_DIM, HEAD_DIM), lambda r: (r, 0, 0, 0))],
        out_shape=[jax.ShapeDtypeStruct((T, 512), F32), jax.ShapeDtypeStruct((T, 512), BF16),
                   jax.ShapeDtypeStruct((n_chunks, HEADS, HEAD_DIM, HEAD_DIM), F32)],
        scratch_shapes=[pltpu.VMEM((HEADS, HEAD_DIM, HEAD_DIM), F32)], exchange=exchange)


def _hgrn_bwd(dmixcat, z, oraw, sprev, lbraw, nw, exchange=None):
    T = z.shape[0]
    G = min(HGRN_GROUP, T // CHUNK)
    rows = G * CHUNK
    ng = T // rows

    def body(dog_ref, q_ref, f_ref, i_ref, g_ref, oraw_ref, sp_ref, lbraw_ref, nw_ref,
             dz_ref, dsmall_ref, dst_ref):
        @pl.when(pl.program_id(0) == 0)
        def _():
            dst_ref[...] = jnp.zeros_like(dst_ref)
            dsmall_ref[...] = jnp.zeros_like(dsmall_ref)

        lb_all = _lower_bound(lbraw_ref)
        tri_lo, tri_up = _tri(True), _tri(False)
        rowid = lax.broadcasted_iota(jnp.int32, (CHUNK, HEADS * HEAD_DIM), 0)

        def chunk(it, carry):
            cc = G - 1 - it
            rs = pl.ds(pl.multiple_of(cc * CHUNK, CHUNK), CHUNK)
            heads = list(enumerate(_HEAD_LANES))
            cat = lambda parts: jnp.concatenate(parts, axis=1)
            per_head_mean = lambda x: cat([jnp.broadcast_to(_lanemean(x[:, s]), (CHUNK, HEAD_DIM)) for s in _HEAD_LANES])
            t = _hgrn_gates(q_ref[rs, :], f_ref[rs, :], lb_all, tri_lo)
            v, gate, o, dog, nw_all = i_ref[rs, :], g_ref[rs, :], oraw_ref[rs, :], dog_ref[rs, :], nw_ref[...]
            rs_o = lax.rsqrt(per_head_mean(o * o) + RMS_EPS)
            xhat = o * rs_o
            sgg = _sigmoid(gate)
            d_on = dog * (gate * sgg)
            dz_ref[rs, 1536:2048] = (dog * (xhat * nw_all) * (sgg * (1.0 + gate * (1.0 - sgg)))).astype(BF16)
            dxh = d_on * nw_all
            do = rs_o * (dxh - xhat * per_head_mean(dxh * xhat))
            dsmall_ref[:, 512:1024] += _rowsum(d_on * xhat)
            st = [sp_ref[cc, h] for h in range(HEADS)]
            dst = [dst_ref[h] for h in range(HEADS)]
            a = [jnp.where(tri_lo, _bdot(t["qa"][:, s], t["ka"][:, s], NT), 0.0) for s in _HEAD_LANES]
            da = [jnp.where(tri_lo, _bdot(do[:, s], v[:, s], NT), 0.0) for s in _HEAD_LANES]
            dqb = cat([_bdot(do[:, s], st[h]) for h, s in heads])
            dkl = cat([_bdot(v[:, s], dst[h]) for h, s in heads])
            dv_ = cat([_bdot(t["kl"][:, s], dst[h], NT) + _bdot(a[h], do[:, s], TN) for h, s in heads])
            dqa = cat([_bdot(da[h], t["ka"][:, s]) for h, s in heads])
            dka = cat([_bdot(da[h], t["qa"][:, s], TN) for h, s in heads])
            ddecay = cat([_rowsum(dst[h] * st[h]) for h in range(HEADS)])
            for h, s in heads:
                dst_ref[h] = dst[h] * t["decay"][:, s] + _bdot(do[:, s], t["qb"][:, s], TN)
            pa, pk, pb, pl_ = dqa * t["qa"], dka * t["ka"], dqb * t["qb"], dkl * t["kl"]
            db = pa - pk + pb - pl_
            db = db + jnp.where(rowid == CHUNK // 2 - 1, _rowsum(pk - pa), 0.0)
            db = db + jnp.where(rowid == CHUNK - 1, _rowsum(pl_) + ddecay * t["decay"], 0.0)
            dlogf = _hdot(tri_up.astype(F32), db)
            dforget = dlogf / t["forget"] - (dka * t["e2"] + dkl * t["e3"])
            sg = t["sg"]
            dz_ref[rs, 0:512] = (dqa * t["e1"] + dqb * t["e4"]).astype(BF16)
            dz_ref[rs, 512:1024] = (dforget * (1.0 - lb_all) * sg * (1.0 - sg)).astype(BF16)
            dz_ref[rs, 1024:1536] = dv_.astype(BF16)
            dsmall_ref[:, 0:512] += _rowsum(dforget * (1.0 - sg))
            return carry

        lax.fori_loop(0, G, chunk, 0, unroll=4)

    col = lambda j: pl.BlockSpec((rows, 512), lambda r, j=j: (ng - 1 - r, j))
    return _call(
        body, "hgrn_bwd", (dmixcat, z, z, z, z, oraw, sprev, lbraw, nw), grid=(ng,),
        in_specs=[col(0), col(0), col(1), col(2), col(3), col(0),
                  pl.BlockSpec((G, HEADS, HEAD_DIM, HEAD_DIM), lambda r: (ng - 1 - r, 0, 0, 0)),
                  _full((2, 512)), _full((1, 512))],
        out_specs=[pl.BlockSpec((rows, 2048), lambda r: (ng - 1 - r, 0)), _full((1, 1024))],
        out_shape=[jax.ShapeDtypeStruct((T, 2048), BF16), jax.ShapeDtypeStruct((1, 1024), F32)],
        scratch_shapes=[pltpu.VMEM((HEADS, HEAD_DIM, HEAD_DIM), F32)], exchange=exchange)


def _diag_mask(t):
    r = lax.broadcasted_iota(jnp.int32, (t, t), 0)
    c = lax.broadcasted_iota(jnp.int32, (t, t), 1)
    return r >= c


def _attn_fwd(q, k, v, exchange=None):
    _, T, _ = q.shape
    t = min(ATT_TILE, T)

    def body(q_ref, k_ref, v_ref, o_ref, lse_ref):
        i = pl.program_id(1)
        qb = q_ref[...]

        rows = lambda j: pl.ds(pl.multiple_of(j * t, t), t)

        def logits(j, masked):
            s = _dot(qb, k_ref[rows(j), :], NT)
            return jnp.where(_diag_mask(t), s, NEG_BIG) if masked else s

        def absorb(s, j, carry):
            m, l, acc = carry
            mn = jnp.maximum(m, jnp.max(s, axis=-1, keepdims=True))
            p = jnp.exp2(s - mn)
            al = jnp.exp2(m - mn)
            return mn, al * l + jnp.sum(p, axis=-1, keepdims=True), al * acc + _dot(p.astype(BF16), v_ref[rows(j), :])

        def pair(j0, carry, last_masked):
            s0, s1 = logits(j0, False), logits(j0 + 1, last_masked)
            return absorb(s1, j0 + 1, absorb(s0, j0, carry))

        init = (jnp.full((t, 1), NEG_BIG, F32), jnp.zeros((t, 1), F32), jnp.zeros((t, HEAD_DIM), F32))
        carry = lax.fori_loop(0, i // 2, lambda jj, c: pair(2 * jj, c, False), init)
        m, l, acc = lax.cond(i % 2 == 1, lambda c: pair(i - 1, c, True),
                             lambda c: absorb(logits(i, True), i, c), carry)
        o_ref[...] = acc / l
        lse_ref[...] = jnp.broadcast_to(m + jnp.log2(l), (t, HEAD_DIM))

    return _call(
        body, "attn_fwd", (q, k, v), grid=(HEADS, T // t),
        in_specs=[pl.BlockSpec((None, t, QK_PAD), lambda h, i: (h, i, 0)),
                  pl.BlockSpec((None, T, QK_PAD), lambda h, i: (h, 0, 0)),
                  pl.BlockSpec((None, T, HEAD_DIM), lambda h, i: (h, 0, 0))],
        out_specs=[pl.BlockSpec((t, HEAD_DIM), lambda h, i: (i, h)),
                   pl.BlockSpec((None, t, HEAD_DIM), lambda h, i: (h, i, 0))],
        out_shape=[jax.ShapeDtypeStruct((T, HEADS * HEAD_DIM), F32), jax.ShapeDtypeStruct((HEADS, T, HEAD_DIM), F32)],
        exchange=exchange)


def _attn_bwd(q, k, v, dmixcat, o, lse, exchange=None):
    _, T, _ = q.shape
    t = min(ATT_TILE, T)
    nq = T // t

    def body(q_ref, k_ref, v_ref, do_ref, o_ref, lse_ref, dq_ref, dk_ref, dv_ref, delta_ref, dq_acc):
        j = pl.program_id(1)

        @pl.when(j == 0)
        def _():
            dq_acc[...] = jnp.zeros_like(dq_acc)

            def fill(i, carry):
                rs = pl.ds(pl.multiple_of(i * t, t), t)
                delta_ref[rs, :] = jnp.broadcast_to(
                    jnp.sum(do_ref[rs, :] * o_ref[rs, :], axis=-1, keepdims=True), (t, HEAD_DIM))
                return carry

            lax.fori_loop(0, nq, fill, 0)

        kb, vb = k_ref[...], v_ref[...]

        def steps(blocks, carry):
            dk, dv = carry
            rs = [pl.ds(pl.multiple_of(i * t, t), t) for i, _ in blocks]
            qb = [q_ref[r, :] for r in rs]
            dob = [do_ref[r, :].astype(BF16) for r in rs]
            s = [_dot(b, kb, NT) for b in qb]
            dp = [_dot(b, vb, NT) for b in dob]
            for n, (_, shift) in enumerate(blocks):
                p = jnp.exp2(s[n] - lse_ref[rs[n], 0:1])
                if shift is not None:
                    row = lax.broadcasted_iota(jnp.int32, (t, 2 * t), 0)
                    col = lax.broadcasted_iota(jnp.int32, (t, 2 * t), 1)
                    p = jnp.where(col <= row + shift, p, 0.0)
                ds = (p * (dp[n] - delta_ref[rs[n], 0:1])).astype(BF16)
                dq_acc[rs[n], :] += _dot(ds, kb)
                dk = dk + _dot(ds, qb[n], TN)
                dv = dv + _dot(p.astype(BF16), dob[n], TN)
            return dk, dv

        zero = (jnp.zeros((2 * t, QK_PAD), F32), jnp.zeros((2 * t, HEAD_DIM), F32))
        carry = steps([(2 * j, 0), (2 * j + 1, t)], zero)
        first = 2 * j + 2
        dk, dv = lax.fori_loop(0, (nq - first) // 2, lambda n, c: steps([(first + 2 * n, None), (first + 2 * n + 1, None)], c),
                               carry)
        dk_ref[...] = (dk * LN2).astype(BF16)
        dv_ref[...] = dv.astype(BF16)

        @pl.when(j == nk - 1)
        def _():
            dq_ref[...] = dq_acc[...].astype(BF16)

    nk = nq // 2
    return _call(
        body, "attn_bwd", (q, k, v, dmixcat, o, lse), grid=(HEADS, nk),
        in_specs=[pl.BlockSpec((None, T, QK_PAD), lambda h, j: (h, 0, 0)),
                  pl.BlockSpec((None, 2 * t, QK_PAD), lambda h, j: (h, j, 0)),
                  pl.BlockSpec((None, 2 * t, HEAD_DIM), lambda h, j: (h, j, 0)),
                  pl.BlockSpec((T, HEAD_DIM), lambda h, j: (0, HEADS + h)),
                  pl.BlockSpec((T, HEAD_DIM), lambda h, j: (0, h)),
                  pl.BlockSpec((None, T, HEAD_DIM), lambda h, j: (h, 0, 0))],
        out_specs=[pl.BlockSpec((None, T, QK_PAD), lambda h, j: (h, 0, 0)),
                   pl.BlockSpec((None, 2 * t, QK_PAD), lambda h, j: (h, j, 0)),
                   pl.BlockSpec((None, 2 * t, HEAD_DIM), lambda h, j: (h, j, 0))],
        out_shape=[jax.ShapeDtypeStruct((HEADS, T, QK_PAD), BF16), jax.ShapeDtypeStruct((HEADS, T, QK_PAD), BF16),
                   jax.ShapeDtypeStruct((HEADS, T, HEAD_DIM), BF16)],
        scratch_shapes=[pltpu.VMEM((T, HEAD_DIM), F32), pltpu.VMEM((T, QK_PAD), F32)], exchange=exchange)


def _ln_fwd(r):
    mu = _lanemean(r)
    xc = r - mu
    rstd = lax.rsqrt(_lanemean(xc * xc) + LN_EPS)
    return xc * rstd, rstd


def _ln_bwd(dxh, xhat, rstd):
    return rstd * (dxh - _lanemean(dxh) - xhat * _lanemean(dxh * xhat))


def _mix_ln1(o_hg, o_mla, w_out, x, g_a, ln1_g, ln1_b, sc_m, sh_m, exchange=None):
    T = x.shape[0]
    tm = min(ROW_TILE, T)
    half = o_hg.shape[1]

    def body(hg_ref, mla_ref, w_ref, x_ref, ga_ref, g_ref, b_ref, sc_ref, sh_ref, mix_ref, xhat_ref, rstd_ref, u2_ref):
        mix = _dot(hg_ref[...], w_ref[0:half, :]) + _bdot(mla_ref[...], w_ref[half:, :])
        mix_ref[...] = mix
        xhat, rstd = _ln_fwd(ALPHA * x_ref[...] + (1.0 + ga_ref[...]) * mix)
        xhat_ref[...] = xhat
        rstd_ref[...] = jnp.broadcast_to(rstd, (tm, 128))
        u2_ref[...] = _modulate(xhat * g_ref[...] + b_ref[...], sc_ref[...], sh_ref[...]).astype(BF16)

    row = pl.BlockSpec((tm, D_MODEL), lambda i: (i, 0))
    vec = _full((1, D_MODEL))
    halfrow = pl.BlockSpec((tm, half), lambda i: (i, 0))
    return _call(
        body, "mix_ln1", (o_hg, o_mla, w_out, x, g_a, ln1_g, ln1_b, sc_m, sh_m), grid=(T // tm,),
        in_specs=[halfrow, halfrow, _full(w_out.shape), row, vec, vec, vec, vec, vec],
        out_specs=[row, row, pl.BlockSpec((tm, 128), lambda i: (i, 0)), row],
        out_shape=[jax.ShapeDtypeStruct((T, D_MODEL), F32), jax.ShapeDtypeStruct((T, D_MODEL), F32),
                   jax.ShapeDtypeStruct((T, 128), F32), jax.ShapeDtypeStruct((T, D_MODEL), BF16)],
        exchange=exchange)


def _mlp_fwd(u2, w1, w2, xhat1, ln1_g, ln1_b, g_m, ln2_g, ln2_b, target):
    T = u2.shape[0]
    half, tf = w1[0].shape[1:]
    nf = N_DEV // MLP_SLABS
    tm = min(ROW_TILE, T)

    def body(u2_ref, w1a_ref, w1b_ref, w2_ref, xhat_ref, g1_ref, b1_ref, gm_ref, g2_ref, b2_ref, tgt_ref,
             r_ref, dr2_ref, dh_ref, small_ref, acc_ref):
        i, f = pl.program_id(0), pl.program_id(1)
        dm = D_MODEL

        @pl.when((i == 0) & (f == 0))
        def _():
            small_ref[...] = jnp.zeros_like(small_ref)

        @pl.when(f == 0)
        def _():
            acc_ref[...] = jnp.zeros_like(acc_ref)

        u2t = u2_ref[...]
        part = None
        for s in range(MLP_SLABS):
            r = jnp.maximum(_dot(u2t[:, :half], w1a_ref[s]) + _dot(u2t[:, half:], w1b_ref[s]), 0.0)
            r_ref[:, s * tf:(s + 1) * tf] = r.astype(BF16)
            d = _bdot(r * r, w2_ref[s])
            part = d if part is None else part + d
        acc_ref[...] += part

        @pl.when(f == nf - 1)
        def _():
            h = acc_ref[...]
            x1 = xhat_ref[...] * g1_ref[...] + b1_ref[...]
            xhat2, rstd2 = _ln_fwd(ALPHA * x1 + (1.0 + gm_ref[...]) * h)
            err = xhat2 * g2_ref[...] + b2_ref[...] - tgt_ref[...]
            small_ref[:, 3 * dm:] += jnp.sum(0.5 * _lanemean(err * err), axis=0, keepdims=True)
            dy = err * (1.0 / D_MODEL)
            small_ref[:, dm:2 * dm] += _rowsum(dy * xhat2)
            small_ref[:, 2 * dm:3 * dm] += _rowsum(dy)
            dr2 = _ln_bwd(dy * g2_ref[...], xhat2, rstd2)
            dr2_ref[...] = dr2
            small_ref[:, 0:dm] += _rowsum(dr2 * h)
            dh_ref[...] = ((1.0 + gm_ref[...]) * dr2).astype(BF16)

    row = pl.BlockSpec((tm, D_MODEL), lambda i, f: (i, 0))
    vec = _full((1, D_MODEL))
    return pl.pallas_call(
        body, name="mlp_fwd", grid=(T // tm, nf),
        in_specs=[row, pl.BlockSpec((MLP_SLABS, half, tf), lambda i, f: (f, 0, 0)),
                  pl.BlockSpec((MLP_SLABS, half, tf), lambda i, f: (f, 0, 0)),
                  pl.BlockSpec((MLP_SLABS, tf, D_MODEL), lambda i, f: (f, 0, 0)),
                  row, vec, vec, vec, vec, vec, row],
        out_specs=[pl.BlockSpec((tm, MLP_SLABS * tf), lambda i, f: (i, f)), row, row, _full((1, 3 * D_MODEL + 128))],
        out_shape=[jax.ShapeDtypeStruct((T, N_DEV * tf), BF16), jax.ShapeDtypeStruct((T, D_MODEL), F32),
                   jax.ShapeDtypeStruct((T, D_MODEL), BF16), jax.ShapeDtypeStruct((1, 3 * D_MODEL + 128), F32)],
        scratch_shapes=[pltpu.VMEM((tm, D_MODEL), F32)],
        compiler_params=_params(),
    )(u2, w1[0], w1[1], w2, xhat1, ln1_g, ln1_b, g_m, ln2_g, ln2_b, target)


def _mlp_bwd(dh, w1, w2, r, dr2, xhat1, rstd1, mix, ln1_g, ln1_b, sc_m, g_a):
    T = dh.shape[0]
    half, tf = w1[0].shape[1:]
    nf = N_DEV // MLP_SLABS
    tm = min(ROW_TILE, T)

    def body(dh_ref, w1a_ref, w1b_ref, w2_ref, r_ref, dr2_ref, xhat_ref, rstd_ref, mix_ref, g1_ref, b1_ref, sc_ref, ga_ref,
             dhpre_ref, dr1_ref, dmix_ref, small_ref, acc_ref):
        i, f = pl.program_id(0), pl.program_id(1)
        dm = D_MODEL

        @pl.when((i == 0) & (f == 0))
        def _():
            small_ref[...] = jnp.zeros_like(small_ref)

        @pl.when(f == 0)
        def _():
            acc_ref[...] = jnp.zeros_like(acc_ref)

        dht = dh_ref[...]
        part = None
        for s in range(MLP_SLABS):
            cols = slice(s * tf, (s + 1) * tf)
            dhpre = (_dot(dht, w2_ref[s], NT) * (2.0 * r_ref[:, cols].astype(F32))).astype(BF16)
            dhpre_ref[:, cols] = dhpre
            d = jnp.concatenate([_dot(dhpre, w1a_ref[s], NT), _dot(dhpre, w1b_ref[s], NT)], axis=1)
            part = d if part is None else part + d
        acc_ref[...] += part

        @pl.when(f == nf - 1)
        def _():
            du2 = acc_ref[...]
            xhat = xhat_ref[...]
            x1 = xhat * g1_ref[...] + b1_ref[...]
            dx1 = ALPHA * dr2_ref[...] + du2 * (1.0 + sc_ref[...])
            small_ref[:, 2 * dm:3 * dm] += _rowsum(du2 * x1)
            small_ref[:, dm:2 * dm] += _rowsum(du2)
            small_ref[:, 3 * dm:4 * dm] += _rowsum(dx1 * xhat)
            small_ref[:, 4 * dm:5 * dm] += _rowsum(dx1)
            dr1 = _ln_bwd(dx1 * g1_ref[...], xhat, rstd_ref[:, 0:1])
            dr1_ref[...] = dr1
            small_ref[:, 0:dm] += _rowsum(dr1 * mix_ref[...])
            dmix_ref[...] = ((1.0 + ga_ref[...]) * dr1).astype(BF16)

    row = pl.BlockSpec((tm, D_MODEL), lambda i, f: (i, 0))
    vec = _full((1, D_MODEL))
    return pl.pallas_call(
        body, name="mlp_bwd", grid=(T // tm, nf),
        in_specs=[row, pl.BlockSpec((MLP_SLABS, half, tf), lambda i, f: (f, 0, 0)),
                  pl.BlockSpec((MLP_SLABS, half, tf), lambda i, f: (f, 0, 0)),
                  pl.BlockSpec((MLP_SLABS, tf, D_MODEL), lambda i, f: (f, 0, 0)),
                  pl.BlockSpec((tm, MLP_SLABS * tf), lambda i, f: (i, f)), row, row,
                  pl.BlockSpec((tm, 128), lambda i, f: (i, 0)), row, vec, vec, vec, vec],
        out_specs=[pl.BlockSpec((tm, MLP_SLABS * tf), lambda i, f: (i, f)), row, row, _full((1, 5 * D_MODEL))],
        out_shape=[jax.ShapeDtypeStruct((T, N_DEV * tf), BF16), jax.ShapeDtypeStruct((T, D_MODEL), F32),
                   jax.ShapeDtypeStruct((T, D_MODEL), BF16), jax.ShapeDtypeStruct((1, 5 * D_MODEL), F32)],
        scratch_shapes=[pltpu.VMEM((tm, D_MODEL), F32)],
        compiler_params=_params(),
    )(dh, w1[0], w1[1], w2, r, dr2, xhat1, rstd1, mix, ln1_g, ln1_b, sc_m, g_a)


def _input_bwd(dz_h, dz_m, w_in_ext, x, dr1, sc_a, exchange=None):
    T = x.shape[0]
    tm = min(ROW_TILE, T)

    def body(dzh_ref, dzm_ref, w_ref, x_ref, dr1_ref, sc_ref, gx_ref, small_ref):
        @pl.when(pl.program_id(0) == 0)
        def _():
            small_ref[...] = jnp.zeros_like(small_ref)

        du = _bdot(dzh_ref[...], w_ref[0:2048, :]) + _bdot(dzm_ref[...], w_ref[2048:3072, :])
        gx_ref[...] = ALPHA * dr1_ref[...] + du * (1.0 + sc_ref[...])
        small_ref[:, D_MODEL:] += _rowsum(du * x_ref[...])
        small_ref[:, 0:D_MODEL] += _rowsum(du)

    row = pl.BlockSpec((tm, D_MODEL), lambda i: (i, 0))
    vec = _full((1, D_MODEL))
    return _call(
        body, "input_bwd", (dz_h, dz_m, w_in_ext, x, dr1, sc_a), grid=(T // tm,),
        in_specs=[pl.BlockSpec((tm, 2048), lambda i: (i, 0)), row, _full(w_in_ext.shape), row, row, vec],
        out_specs=[row, _full((1, 2 * D_MODEL))],
        out_shape=[jax.ShapeDtypeStruct((T, D_MODEL), F32), jax.ShapeDtypeStruct((1, 2 * D_MODEL), F32)],
        exchange=exchange)


def _adam_math(w, g, m, v):
    m = ADAM_B1 * m + (1.0 - ADAM_B1) * g
    v = ADAM_B2 * v + (1.0 - ADAM_B2) * (g * g)
    m_hat = m / (1.0 - ADAM_B1 ** ADAM_STEP)
    v_hat = v / (1.0 - ADAM_B2 ** ADAM_STEP)
    return -ADAM_LR * (m_hat / (jnp.sqrt(v_hat) + ADAM_EPS) + ADAM_WD * w), m, v


def _adam(g_slabs, w, m, v, name, g_fn=None, g_extra=()):
    R, C = w.shape
    tr = 256 if R % 256 == 0 else R
    ns = 0 if g_slabs is None else g_slabs.shape[0]
    slab_rows = tr if g_slabs is None or g_slabs.shape[1] == R else g_slabs.shape[1]
    assert slab_rows == tr or tr == R
    ne = len(g_extra)

    def body(*refs):
        e_refs = refs[:ne]
        refs = refs[ne:]
        if ns:
            gs_ref, refs = refs[0], refs[1:]
        w_ref, m_ref, v_ref, g_ref, d_ref, nm_ref, nv_ref = refs
        if g_fn is not None:
            g = g_fn(*e_refs)
        else:
            g = gs_ref[0].astype(F32)
            for s in range(1, ns):
                g = g + gs_ref[s].astype(F32)
            g = g[:tr]
        d, nm, nv = _adam_math(w_ref[...], g, m_ref[...], v_ref[...])
        g_ref[...] = g
        d_ref[...] = d
        nm_ref[...] = nm
        nv_ref[...] = nv

    blk = pl.BlockSpec((tr, C), lambda i: (i, 0))
    in_specs = [pl.BlockSpec((tr, e.shape[1]), lambda i: (i, 0)) if e.shape[0] == R else _full(e.shape) for e in g_extra]
    args = list(g_extra)
    if ns:
        in_specs.append(pl.BlockSpec((ns, slab_rows, C), lambda i: (0, i, 0)))
        args.append(g_slabs)
    return pl.pallas_call(
        body, name=name, grid=(R // tr,), in_specs=in_specs + [blk] * 3, out_specs=[blk] * 4,
        out_shape=[jax.ShapeDtypeStruct((R, C), F32)] * 4, compiler_params=_params(),
    )(*args, w, m, v)


def _adam_small(small_all, params):
    n = len(params)

    def body(*refs):
        s_ref, refs = refs[0], refs[1:]
        wmv, loss_ref, outs = refs[:3 * n], refs[3 * n], refs[3 * n + 1:]
        tot = s_ref[0]
        for i in range(1, N_DEV):
            tot = tot + s_ref[i]
        loss_ref[...] = tot[:, SMALL_W - 128:]
        for j, (w, _, _, off) in enumerate(params):
            w_ref, m_ref, v_ref = wmv[3 * j:3 * j + 3]
            g_ref, d_ref, nm_ref, nv_ref = outs[4 * j:4 * j + 4]
            if w.shape[0] == 2:
                lb = _lower_bound(w_ref)
                g0 = tot[:, off:off + w.shape[1]] * lb * (1.0 - lb)
                rows = [(slice(0, 1), g0), (slice(1, 2), -g0)]
            else:
                rows = [(slice(0, 1), tot[:, off:off + w.shape[1]])]
            for rs, g in rows:
                d, nm, nv = _adam_math(w_ref[rs, :], g, m_ref[rs, :], v_ref[rs, :])
                g_ref[rs, :], d_ref[rs, :], nm_ref[rs, :], nv_ref[rs, :] = g, d, nm, nv

    out_shape = [jax.ShapeDtypeStruct((1, 128), F32)]
    for w, _, _, _ in params:
        out_shape += [jax.ShapeDtypeStruct(w.shape, F32)] * 4
    res = pl.pallas_call(body, name="adam_small", out_shape=out_shape, compiler_params=_params())(
        small_all, *[a for w, m, v, _ in params for a in (w, m, v)])
    return res[0], [tuple(res[1 + 4 * j:5 + 4 * j]) for j in range(n)]


def _cols_from_slabs(g):
    s, r, c = g.shape
    return jnp.transpose(g, (1, 0, 2)).reshape(r, s * c)


def _slabs_from_cols(w):
    r, c = w.shape
    return jnp.transpose(w.reshape(r, N_DEV, c // N_DEV), (1, 0, 2))


def _rot_half_rows(wt):
    return jnp.concatenate([-wt[32:], wt[:32]], axis=0)


def _unrot_half_rows(dwt_rot):
    return jnp.concatenate([dwt_rot[32:], -dwt_rot[:32]], axis=0)


def _ext_in_t(g):
    n, rows, k_in = g.shape
    keep = n * rows - ROPE_DIM

    def body(g_ref, o_ref, stage_ref):
        stage_ref[keep:, :] = jnp.zeros((o_ref.shape[0] - keep, k_in), F32)
        for i in range(n - 1):
            stage_ref[rows * i:rows * (i + 1), :] = g_ref[i].astype(F32)
        last = g_ref[n - 1].astype(F32)
        stage_ref[rows * (n - 1):keep, :] = last[:rows - ROPE_DIM]
        wk = last[rows - ROPE_DIM:]
        stage_ref[keep + 128:keep + 192, :] = wk
        stage_ref[keep + 384:keep + 416, :] = -wk[32:]
        stage_ref[keep + 416:keep + 448, :] = wk[:32]
        o_ref[...] = stage_ref[...].astype(BF16)

    return pl.pallas_call(body, name="ext_w_in", out_shape=jax.ShapeDtypeStruct((keep + 512, k_in), BF16),
                          scratch_shapes=[pltpu.VMEM((keep + 512, k_in), F32)], compiler_params=_params())(g)


def _ext_q_t(wt):
    r = wt.shape[1]
    z64, z128 = jnp.zeros((64, r), BF16), jnp.zeros((128, r), BF16)
    per = HEAD_DIM + ROPE_DIM
    main = [jnp.concatenate([wt[per * h:per * (h + 1)], z64], axis=0) for h in range(HEADS)]
    rot = [jnp.concatenate([z128, _rot_half_rows(wt[per * h + HEAD_DIM:per * (h + 1)]), z64], axis=0)
           for h in range(HEADS)]
    return jnp.concatenate(main + rot, axis=0)


def _ext_kv(w_kv_up):
    r = w_kv_up.shape[0]
    z128 = jnp.zeros((r, 128), BF16)
    wkv = w_kv_up.reshape(r, HEADS, 2 * HEAD_DIM)
    kpad = [jnp.concatenate([wkv[:, h, :HEAD_DIM], z128], axis=1) for h in range(HEADS)]
    vals = [wkv[:, h, HEAD_DIM:] for h in range(HEADS)]
    return jnp.concatenate(kpad + vals, axis=1)


def _w_in_grad_slabs(dwt_h, dwt_m):
    d = dwt_h.shape[1]
    rows = (dwt_h.shape[0] + 512 + ROPE_DIM) // N_DEV
    padded = rows + (-rows) % 16

    def body(h_ref, m_ref, o_ref, stage_ref):
        stage_ref[0:2048, :] = h_ref[...]
        stage_ref[2048:2560, :] = m_ref[0:512, :]
        rot = m_ref[768 + 128:768 + 192, :]
        stage_ref[2560:2592, :] = m_ref[640:672, :] + rot[32:]
        stage_ref[2592:2624, :] = m_ref[672:704, :] - rot[:32]
        zero = jnp.zeros((padded - rows, d), F32)
        for i in range(N_DEV):
            o_ref[i] = jnp.concatenate([stage_ref[rows * i:rows * (i + 1), :], zero], axis=0).astype(BF16)

    return pl.pallas_call(body, name="w_in_grad_slabs", out_shape=jax.ShapeDtypeStruct((N_DEV, padded, d), BF16),
                          scratch_shapes=[pltpu.VMEM((N_DEV * rows, d), F32)], compiler_params=_params())(dwt_h, dwt_m)


def _grad_q_from_ext_t(dwq_ext_t):
    rows = []
    for h in range(HEADS):
        main, rot = dwq_ext_t[256 * h:256 * h + 256], dwq_ext_t[1024 + 256 * h:1280 + 256 * h]
        rows += [main[:128], main[128:192] + _unrot_half_rows(rot[128:192])]
    return jnp.concatenate(rows, axis=0)


def _grad_kv_from_ext(dwkv_ext):
    kvcols = []
    for h in range(HEADS):
        kvcols += [dwkv_ext[:, 256 * h:256 * h + 128], dwkv_ext[:, 1024 + 128 * h:1152 + 128 * h]]
    return jnp.concatenate(kvcols, axis=1)


SMALL_W = 6144 + 512 + 512 + 256 + 256 + 4 * 1024 + 128


def kernel(x, c, positions, w_ada, b_ada, w_in, hg_lower_bounds, hg_norm_w, mla_q_norm_w, w_q_up, mla_kv_norm_w, w_kv_up, w_out, ln1_g, ln1_b, w_mlp_in, w_mlp_out, ln2_g, ln2_b, loss_target, m_w_ada, m_b_ada, m_w_in, m_hg_lower_bounds, m_hg_norm_w, m_mla_q_norm_w, m_w_q_up, m_mla_kv_norm_w, m_w_kv_up, m_w_out, m_ln1_g, m_ln1_b, m_w_mlp_in, m_w_mlp_out, m_ln2_g, m_ln2_b, v_w_ada, v_b_ada, v_w_in, v_hg_lower_bounds, v_hg_norm_w, v_mla_q_norm_w, v_w_q_up, v_mla_kv_norm_w, v_w_kv_up, v_w_out, v_ln1_g, v_ln1_b, v_w_mlp_in, v_w_mlp_out, v_ln2_g, v_ln2_b):
    T = x.shape[1]
    me = 4 * lax.axis_index("x") + 2 * lax.axis_index("y") + lax.axis_index("c")
    xs, tgt = x[0], loss_target[0]
    transposed = ("w_in", "w_q_up")
    as_used = lambda n, a: a[0].T if n in transposed else a[0]
    big = {n: as_used(n, a) for n, a in dict(w_in=w_in, w_q_up=w_q_up, w_kv_up=w_kv_up, w_out=w_out,
                                              w_mlp_in=w_mlp_in, w_mlp_out=w_mlp_out).items()}
    names = list(big)

    bf = {n: big[n].astype(BF16) for n in ("w_in", "w_q_up", "w_kv_up", "w_out")}
    g_in, g_c = _gather_two_level([bf["w_in"], c], name="gather_w_in")
    c_all = g_c.reshape(N_DEV, D_MODEL)

    ada_cols = w_ada.shape[2]
    mod_all, cond = _mod_gather(c_all, w_ada[0],
                                lax.dynamic_slice(b_ada, (0, me * ada_cols), (1, ada_cols)))
    mod_row = lax.dynamic_slice(mod_all, (0, me, 0), (N_DEV, 1, ada_cols)).reshape(1, N_DEV * ada_cols)
    sh_a, sc_a, g_a, sh_m, sc_m, g_m = [mod_row[:, D_MODEL * i:D_MODEL * (i + 1)] for i in range(6)]

    w_in_ext = _ext_in_t(g_in)
    half = D_MODEL // 2
    z, (w1_top,) = _matmul(xs, w_in_ext, "NT", "in_proj", a_fn=_modulate, extras=(sc_a, sh_a), tn=3072,
                           exchange=_StagedGather(big["w_mlp_in"], rows=(0, half)))
    (o_raw, o_gated, s_prev), (g_q, g_kv, g_out) = _hgrn_fwd(
        z, hg_lower_bounds, hg_norm_w, exchange=_Exchange([bf["w_q_up"], bf["w_kv_up"], bf["w_out"]], False))
    wq_ext = _ext_q_t(g_q.reshape(N_DEV * g_q.shape[1], g_q.shape[2]))
    wkv_ext = _ext_kv(_cols_from_slabs(g_kv))
    w_out_full = g_out.reshape(D_MODEL, D_MODEL)
    inv_freq = 1.0 / (ROPE_THETA ** (jnp.arange(0, ROPE_DIM, 2, dtype=F32) / ROPE_DIM))
    zeros = lambda n: jnp.zeros((n,), F32)
    invf = jnp.concatenate([zeros(128), inv_freq, inv_freq, zeros(64)]).reshape(1, QK_PAD)
    m_rot = jnp.concatenate([zeros(128), jnp.ones((64,), F32), zeros(64)]).reshape(1, QK_PAD)
    q, k, v, c1, s1, cqn, ckvn = _mla_pre(z, positions.reshape(T, 1), invf, m_rot, wq_ext, wkv_ext,
                                          mla_q_norm_w, mla_kv_norm_w)[0]
    (o_mla, lse), (w1_bot, w2) = _attn_fwd(
        q, k, v, exchange=[_StagedGather(big["w_mlp_in"], 0.75, rows=(half, half)),
                           _StagedGather(big["w_mlp_out"], 0.75)])
    w1 = (w1_top, w1_bot)
    mix, xhat1, rstd1, u2 = _mix_ln1(o_gated, o_mla, w_out_full, xs, g_a, ln1_g, ln1_b, sc_m, sh_m)[0]
    r, dr2, dh, small_mlp_fwd = _mlp_fwd(u2, w1, w2, xhat1, ln1_g, ln1_b, g_m, ln2_g, ln2_b, tgt)

    dhpre, dr1, dmix, small_mlp_bwd = _mlp_bwd(dh, w1, w2, r, dr2, xhat1, rstd1, mix, ln1_g, ln1_b, sc_m, g_a)
    received = {}
    dw2 = _matmul(r, dh, "TN", "wgrad_mlp_out", out_dtype=BF16, a_fn=_square, tm=1024, tk=2048)
    dw1 = _matmul(u2, dhpre, "TN", "wgrad_mlp_in", out_dtype=BF16, tm=1024, tk=2048, out_slabs=N_DEV)
    dmixcat = _matmul(dmix, w_out_full, "NT", "dgrad_out", tm=1024)
    dw_out = jnp.concatenate([_matmul(o_gated, dmix, "TN", "wgrad_out_hg", out_dtype=BF16, tk=2048),
                              _matmul(o_mla, dmix, "TN", "wgrad_out_mla", out_dtype=BF16, tk=2048)], axis=0)
    (dz_h, small_hgrn), (received["w_mlp_in"],) = _hgrn_bwd(
        dmixcat, z, o_raw, s_prev, hg_lower_bounds, hg_norm_w, exchange=_StagedScatter(dw1, 0.1))
    (dq, dk, dv), (received["w_mlp_out"], received["w_out"]) = _attn_bwd(
        q, k, v, dmixcat, o_mla, lse,
        exchange=_Exchange([dw2.reshape(N_DEV, dw2.shape[0] // N_DEV, D_MODEL),
                            dw_out.reshape(N_DEV, D_MODEL // N_DEV, D_MODEL)], True))
    dz_m, dq_ext, dkv_ext, small_mla = _mla_bwd(dq, dk, dv, z, c1, s1, wq_ext, wkv_ext, mla_q_norm_w, mla_kv_norm_w)
    dwq_t = _grad_q_from_ext_t(_matmul(dq_ext, cqn, "TN", "wgrad_q_up", tm=1024, tk=2048))
    dwkv = _grad_kv_from_ext(_matmul(ckvn, dkv_ext, "TN", "wgrad_kv_up", tn=1536, tk=2048))
    qkv_slabs = [dwq_t.reshape((N_DEV, dwq_t.shape[0] // N_DEV, dwq_t.shape[1])).astype(BF16),
                 _slabs_from_cols(dwkv).astype(BF16)]
    dwt_h = _matmul(dz_h, xs, "TN", "wgrad_in_h", b_fn=_modulate, extras=(sc_a, sh_a), tm=1024, tk=2048)
    dwt_m = _matmul(dz_m, xs, "TN", "wgrad_in_m", b_fn=_modulate, extras=(sc_a, sh_a), tm=1024, tk=2048)
    in_slabs = _w_in_grad_slabs(dwt_h, dwt_m)
    (grad_x, small_in), (received["w_q_up"], received["w_kv_up"], received["w_in"]) = _input_bwd(
        dz_h, dz_m, w_in_ext, xs, dr1, sc_a,
        exchange=[_Exchange(qkv_slabs, True), _StagedScatter(in_slabs)])

    small = jnp.concatenate([small_in, small_mlp_bwd[:, :3 * D_MODEL], small_mlp_fwd[:, :D_MODEL], small_hgrn,
                             small_mla, small_mlp_bwd[:, 3 * D_MODEL:], small_mlp_fwd[:, D_MODEL:]], axis=1)
    assert small.shape == (1, SMALL_W)
    (small_all,) = _exchange([small], scatter=False, name="gather_small")

    moments = dict(w_in=(m_w_in, v_w_in), w_q_up=(m_w_q_up, v_w_q_up), w_kv_up=(m_w_kv_up, v_w_kv_up),
                   w_out=(m_w_out, v_w_out), w_mlp_in=(m_w_mlp_in, v_w_mlp_in), w_mlp_out=(m_w_mlp_out, v_w_mlp_out))
    res = {}
    for n in names:
        res[n] = _adam(received[n], big[n], as_used(n, moments[n][0]), as_used(n, moments[n][1]), name="adam_" + n)
    dmod_cols = lax.dynamic_slice(small_all.reshape(N_DEV, SMALL_W), (0, me * ada_cols), (N_DEV, ada_cols))
    cond_t = cond.T

    def ada_grad(ct_ref, dm_ref):
        g = ct_ref[:, 0:1] * dm_ref[0:1, :]
        for b in range(1, N_DEV):
            g = g + ct_ref[:, b:b + 1] * dm_ref[b:b + 1, :]
        return g

    res["w_ada"] = _adam(None, w_ada[0], m_w_ada[0], v_w_ada[0], name="adam_w_ada", g_fn=ada_grad,
                         g_extra=(cond_t, dmod_cols))

    small_params = [("b_ada", b_ada, m_b_ada, v_b_ada, 0),
                    ("hg_lower_bounds", hg_lower_bounds, m_hg_lower_bounds, v_hg_lower_bounds, 6144),
                    ("hg_norm_w", hg_norm_w, m_hg_norm_w, v_hg_norm_w, 6656),
                    ("mla_q_norm_w", mla_q_norm_w, m_mla_q_norm_w, v_mla_q_norm_w, 7168),
                    ("mla_kv_norm_w", mla_kv_norm_w, m_mla_kv_norm_w, v_mla_kv_norm_w, 7424),
                    ("ln1_g", ln1_g, m_ln1_g, v_ln1_g, 7680), ("ln1_b", ln1_b, m_ln1_b, v_ln1_b, 8704),
                    ("ln2_g", ln2_g, m_ln2_g, v_ln2_g, 9728), ("ln2_b", ln2_b, m_ln2_b, v_ln2_b, 10752)]
    loss_row, small_res = _adam_small(small_all, [p[1:] for p in small_params])
    for p, r4 in zip(small_params, small_res):
        res[p[0]] = r4
    loss = loss_row[0, 0]

    order = ["w_ada", "b_ada", "w_in", "hg_lower_bounds", "hg_norm_w", "mla_q_norm_w", "w_q_up", "mla_kv_norm_w",
             "w_kv_up", "w_out", "ln1_g", "ln1_b", "w_mlp_in", "w_mlp_out", "ln2_g", "ln2_b"]
    def as_given(n, a):
        if n in transposed:
            a = a.T
        return a[None] if n in big or n == "w_ada" else a

    shaped = {n: tuple(as_given(n, a) for a in res[n]) for n in order}
    outs = [loss, grad_x.reshape(1, T, D_MODEL)]
    for i in range(4):
        outs += [shaped[n][i] for n in order]
    return tuple(outs)
```
